```python
import math
import jax, jax.numpy as jnp
from jax import lax
import numpy as np

D_MODEL = 1024
BATCH = 8
SEQ = 4096
DEPTH = 1

GDN_HEADS = 8
GDN_HEAD_DIM = 64
GDN_WIDTH = GDN_HEADS * GDN_HEAD_DIM
GDN_CONV = 4
GDN_CHUNK = 64
RWKV_HEADS = 8
RWKV_HEAD_DIM = 64
RWKV_WIDTH = RWKV_HEADS * RWKV_HEAD_DIM
DECAY_LORA = 64
ICLR_LORA = 64
GATE_LORA = 160
RWKV_COLS = 3 * RWKV_WIDTH + DECAY_LORA + ICLR_LORA + GATE_LORA
IN_SIZES = (3 * GDN_WIDTH, GDN_WIDTH, GDN_HEADS, GDN_HEADS, RWKV_COLS, D_MODEL, D_MODEL)
D_IN = 3 * GDN_WIDTH + GDN_WIDTH + 2 * GDN_HEADS + RWKV_COLS + 2 * D_MODEL
D_FF = 2816
FFN_CONV = 3
NORM_EPS = 1e-6
LNX_EPS = 64e-5

kernel_name = "hybrid_gdn_rwkv7_convglu_adaln"


def rms_norm(x, w, eps=NORM_EPS):
    xf = x.astype(jnp.float32)
    y = xf * lax.rsqrt(jnp.mean(xf * xf, axis=-1, keepdims=True) + eps)
    return (y * w.astype(jnp.float32)).astype(x.dtype)


def l2_normalize(x, eps=1e-6):
    xf = x.astype(jnp.float32)
    return xf * lax.rsqrt(jnp.sum(xf * xf, axis=-1, keepdims=True) + eps)


def split_cols(p, sizes):
    out, start = [], 0
    for s in sizes:
        out.append(p[..., start:start + s])
        start += s
    return out


def token_shift(p):
    return jnp.pad(p, ((0, 0), (1, 0), (0, 0)))[:, :-1]


def causal_depthwise_conv(x, w):
    k = w.shape[0]
    return lax.conv_general_dilated(
        x, w[:, None, :].astype(x.dtype), window_strides=(1,), padding=((k - 1, 0),),
        dimension_numbers=('NWC', 'WIO', 'NWC'), feature_group_count=x.shape[-1])


def chunk_gated_delta_rule(q, k, v, g, beta):
    B, T, H, dk = q.shape
    dv = v.shape[-1]
    C = GDN_CHUNK
    n = T // C
    to_chunks = lambda t: t.reshape(B, n, C, H, -1).transpose(0, 3, 1, 2, 4)
    q, k, v = to_chunks(q), to_chunks(k), to_chunks(v)
    g = g.reshape(B, n, C, H).transpose(0, 3, 1, 2)
    beta = beta.reshape(B, n, C, H).transpose(0, 3, 1, 2)
    g_cum = jnp.cumsum(g, axis=-1)
    causal = jnp.tril(jnp.ones((C, C), dtype=bool))
    strict = jnp.tril(jnp.ones((C, C), dtype=bool), -1)
    diff = g_cum[..., :, None] - g_cum[..., None, :]
    decay = jnp.where(causal, jnp.exp(jnp.where(causal, diff, 0.0)), 0.0)
    k_beta = k * beta[..., None]
    v_beta = v * beta[..., None]
    lower = jnp.where(strict, jnp.einsum('bhncd,bhnsd->bhncs', k_beta, k) * decay, 0.0)
    eye = jnp.eye(C, dtype=q.dtype)
    t_mat = lax.linalg.triangular_solve(lower + eye, jnp.broadcast_to(eye, lower.shape),
                                        left_side=True, lower=True, unit_diagonal=True)
    u = jnp.matmul(t_mat, v_beta)
    w = jnp.matmul(t_mat, k_beta * jnp.exp(g_cum)[..., None])
    attn = jnp.where(causal, jnp.einsum('bhncd,bhnsd->bhncs', q, k) * decay, 0.0)

    def step(S, inp):
        q_i, k_i, u_i, w_i, a_i, gc_i = inp
        v_new = u_i - jnp.matmul(w_i, S)
        o_i = jnp.matmul(q_i * jnp.exp(gc_i)[..., None], S) + jnp.matmul(a_i, v_new)
        g_last = gc_i[..., -1]
        k_dec = k_i * jnp.exp(g_last[..., None] - gc_i)[..., None]
        S = S * jnp.exp(g_last)[..., None, None] + jnp.einsum('bhcd,bhce->bhde', k_dec, v_new)
        return S, o_i

    mv = lambda t: jnp.moveaxis(t, 2, 0)
    S0 = jnp.zeros((B, H, dk, dv), dtype=q.dtype)
    _, o = lax.scan(step, S0, (mv(q), mv(k), mv(u), mv(w), mv(attn), mv(g_cum)))
    return o.transpose(1, 0, 3, 2, 4).reshape(B, T, H, dv)


def gated_deltanet(qkv, z, b_logit, a_logit, conv_w, a_log, dt_bias, onorm_w):
    B, T, _ = qkv.shape
    qkv = jax.nn.silu(causal_depthwise_conv(qkv, conv_w)).astype(jnp.float32)
    q, k, v = jnp.split(qkv, 3, axis=-1)
    heads = lambda t: t.reshape(B, T, GDN_HEADS, GDN_HEAD_DIM)
    q = l2_normalize(heads(q)) * (GDN_HEAD_DIM ** -0.5)
    k = l2_normalize(heads(k))
    v = heads(v)
    beta = jax.nn.sigmoid(b_logit.astype(jnp.float32))
    g = -jnp.exp(a_log.astype(jnp.float32)) * jax.nn.softplus(
        a_logit.astype(jnp.float32) + dt_bias.astype(jnp.float32))
    o = chunk_gated_delta_rule(q, k, v, g, beta)
    o = rms_norm(o, onorm_w) * jax.nn.silu(heads(z).astype(jnp.float32))
    return o.reshape(B, T, GDN_WIDTH)


def rwkv7_scan(r, decay, k, v, a, b):
    B, T, H, N = r.shape

    def step(S, inp):
        r_t, w_t, k_t, v_t, a_t, b_t = inp
        sa = jnp.einsum('bhvk,bhk->bhv', S, a_t)
        S = S * w_t[:, :, None, :] + sa[..., None] * b_t[:, :, None, :] + v_t[..., None] * k_t[:, :, None, :]
        return S, jnp.einsum('bhvk,bhk->bhv', S, r_t)

    tm = lambda t: jnp.moveaxis(t, 1, 0)
    S0 = jnp.zeros((B, H, N, N), dtype=jnp.float32)
    _, y = lax.scan(step, S0, (tm(r), tm(decay), tm(k), tm(v), tm(a), tm(b)))
    return jnp.moveaxis(y, 0, 1)


def rwkv7_time_mix(pb, mu, w0, w2, a0, a2, g2, k_k, k_a, r_k, lnx_w, lnx_b):
    f32 = lambda t: t.astype(jnp.float32)
    B, T, _ = pb.shape
    pb = f32(pb)
    pb = pb + (token_shift(pb) - pb) * f32(mu)
    r, k, v, w_lo, a_lo, g_lo = split_cols(
        pb, (RWKV_WIDTH, RWKV_WIDTH, RWKV_WIDTH, DECAY_LORA, ICLR_LORA, GATE_LORA))
    w = -jax.nn.softplus(-(f32(w0) + jnp.tanh(w_lo) @ f32(w2))) - 0.5
    decay = jnp.exp(-jnp.exp(w))
    a = jax.nn.sigmoid(f32(a0) + a_lo @ f32(a2))
    g = jax.nn.sigmoid(g_lo) @ f32(g2)
    heads = lambda t: t.reshape(B, T, RWKV_HEADS, RWKV_HEAD_DIM)
    kk = l2_normalize(heads(k * f32(k_k)))
    k = k * (1.0 + (a - 1.0) * f32(k_a))
    r, k, v, a, decay = heads(r), heads(k), heads(v), heads(a), heads(decay)
    y = rwkv7_scan(r, decay, k, v, -kk, kk * a)
    mean = jnp.mean(y, axis=-1, keepdims=True)
    var = jnp.mean(jnp.square(y - mean), axis=-1, keepdims=True)
    y = ((y - mean) * lax.rsqrt(var + LNX_EPS)).reshape(B, T, RWKV_WIDTH) * f32(lnx_w) + f32(lnx_b)
    bonus = jnp.sum(r * k * f32(r_k), axis=-1, keepdims=True) * v
    return (y + bonus.reshape(B, T, RWKV_WIDTH)) * g


def conv_glu(h, w_in, conv_w, w_out):
    gate, up = jnp.split(h @ w_in, 2, axis=-1)
    gate = causal_depthwise_conv(gate, conv_w)
    return (jax.nn.silu(gate) * up) @ w_out


def _fwd_setup_inputs(seed: int = 0) -> dict:
    key = jax.random.key(seed)
    ks = iter(jax.random.split(key, 40))
    nrm = lambda shape, scale: jax.random.normal(next(ks), shape, jnp.float32) * scale
    uni = lambda shape, lo, hi: jax.random.uniform(next(ks), shape, jnp.float32, minval=lo, maxval=hi)
    L = DEPTH
    x = nrm((BATCH, SEQ, D_MODEL), 1.0)
    c = nrm((BATCH, D_MODEL), 1.0)
    w_ada = nrm((L, D_MODEL, 6 * D_MODEL), D_MODEL ** -0.5)
    b_ada = nrm((L, 6 * D_MODEL), 0.02)
    norm1_w = 1.0 + nrm((L, D_MODEL), 0.02)
    w_in = nrm((L, D_MODEL, D_IN), D_MODEL ** -0.5)
    conv_gdn = nrm((L, GDN_CONV, 3 * GDN_WIDTH), GDN_CONV ** -0.5)
    a_log = jnp.log(uni((L, GDN_HEADS), 1.0, 16.0))
    dt = jnp.exp(uni((L, GDN_HEADS), math.log(1e-3), math.log(1e-1)))
    dt_bias = dt + jnp.log(-jnp.expm1(-dt))
    onorm_gdn = 1.0 + nrm((L, GDN_HEAD_DIM), 0.02)
    w_branch_gdn = nrm((L, GDN_WIDTH, D_MODEL), GDN_WIDTH ** -0.5)
    mu_rwkv = uni((L, RWKV_COLS), 0.0, 1.0)
    w0 = uni((L, RWKV_WIDTH), -6.5, -1.5)
    w2 = nrm((L, DECAY_LORA, RWKV_WIDTH), 0.5 * DECAY_LORA ** -0.5)
    a0 = nrm((L, RWKV_WIDTH), 0.1)
    a2 = nrm((L, ICLR_LORA, RWKV_WIDTH), 0.5 * ICLR_LORA ** -0.5)
    g2 = nrm((L, GATE_LORA, RWKV_WIDTH), GATE_LORA ** -0.5)
    k_k = 0.85 + nrm((L, RWKV_WIDTH), 0.02)
    k_a = 1.0 + nrm((L, RWKV_WIDTH), 0.02)
    r_k = nrm((L, RWKV_HEADS, RWKV_HEAD_DIM), 0.1)
    lnx_w = 1.0 + nrm((L, RWKV_WIDTH), 0.02)
    lnx_b = nrm((L, RWKV_WIDTH), 0.02)
    w_branch_rwkv = nrm((L, RWKV_WIDTH, D_MODEL), RWKV_WIDTH ** -0.5)
    w_out = nrm((L, D_MODEL, D_MODEL), D_MODEL ** -0.5)
    norm2_w = 1.0 + nrm((L, D_MODEL), 0.02)
    w_ffn_in = nrm((L, D_MODEL, 2 * D_FF), D_MODEL ** -0.5)
    conv_ffn = nrm((L, FFN_CONV, D_FF), FFN_CONV ** -0.5)
    w_ffn_out = nrm((L, D_FF, D_MODEL), D_FF ** -0.5)
    norm_f_w = 1.0 + nrm((D_MODEL,), 0.02)
    return {"x": x, "c": c, "w_ada": w_ada, "b_ada": b_ada, "norm1_w": norm1_w, "w_in": w_in,
            "conv_gdn": conv_gdn, "a_log": a_log, "dt_bias": dt_bias, "onorm_gdn": onorm_gdn,
            "w_branch_gdn": w_branch_gdn, "mu_rwkv": mu_rwkv, "w0": w0, "w2": w2, "a0": a0, "a2": a2,
            "g2": g2, "k_k": k_k, "k_a": k_a, "r_k": r_k, "lnx_w": lnx_w, "lnx_b": lnx_b,
            "w_branch_rwkv": w_branch_rwkv, "w_out": w_out, "norm2_w": norm2_w, "w_ffn_in": w_ffn_in,
            "conv_ffn": conv_ffn, "w_ffn_out": w_ffn_out, "norm_f_w": norm_f_w}


def _fwd_reference(x, c, w_ada, b_ada, norm1_w, w_in, conv_gdn, a_log, dt_bias, onorm_gdn, w_branch_gdn,
              mu_rwkv, w0, w2, a0, a2, g2, k_k, k_a, r_k, lnx_w, lnx_b, w_branch_rwkv, w_out,
              norm2_w, w_ffn_in, conv_ffn, w_ffn_out, norm_f_w):
    cond = jax.nn.silu(c)
    for i in range(DEPTH):
        mod = cond @ w_ada[i] + b_ada[i]
        shift1, scale1, gate1, shift2, scale2, gate2 = jnp.split(mod[:, None, :], 6, axis=-1)
        h = rms_norm(x, norm1_w[i]) * (1.0 + scale1) + shift1
        p = h @ w_in[i]
        qkv_a, z_a, b_a, a_a, p_b, gl_a, gl_b = split_cols(p, IN_SIZES)
        y_a = gated_deltanet(qkv_a, z_a, b_a, a_a, conv_gdn[i], a_log[i], dt_bias[i], onorm_gdn[i])
        y_b = rwkv7_time_mix(p_b, mu_rwkv[i], w0[i], w2[i], a0[i], a2[i], g2[i], k_k[i], k_a[i],
                             r_k[i], lnx_w[i], lnx_b[i])
        y_a = y_a.astype(x.dtype) @ w_branch_gdn[i]
        y_b = y_b.astype(x.dtype) @ w_branch_rwkv[i]
        merged = jax.nn.sigmoid(gl_a) * y_a + jax.nn.sigmoid(gl_b) * y_b
        x = x + gate1 * (merged @ w_out[i])
        h = rms_norm(x, norm2_w[i]) * (1.0 + scale2) + shift2
        x = x + gate2 * conv_glu(h, w_ffn_in[i], conv_ffn[i], w_ffn_out[i])
    return rms_norm(x, norm_f_w)


import jax as _jax
import jax.numpy as _jnp

TWIN_FORMAT = 'train_step'
FWD_PARAMS = ['x', 'c', 'w_ada', 'b_ada', 'norm1_w', 'w_in', 'conv_gdn', 'a_log', 'dt_bias', 'onorm_gdn', 'w_branch_gdn', 'mu_rwkv', 'w0', 'w2', 'a0', 'a2', 'g2', 'k_k', 'k_a', 'r_k', 'lnx_w', 'lnx_b', 'w_branch_rwkv', 'w_out', 'norm2_w', 'w_ffn_in', 'conv_ffn', 'w_ffn_out', 'norm_f_w']
TWIN_WEIGHTS = ['w_ada', 'b_ada', 'norm1_w', 'w_in', 'conv_gdn', 'a_log', 'dt_bias', 'onorm_gdn', 'w_branch_gdn', 'mu_rwkv', 'w0', 'w2', 'a0', 'a2', 'g2', 'k_k', 'k_a', 'r_k', 'lnx_w', 'lnx_b', 'w_branch_rwkv', 'w_out', 'norm2_w', 'w_ffn_in', 'conv_ffn', 'w_ffn_out', 'norm_f_w']
TWIN_DIFF_INPUT = 'x'
TWIN_INPUTS = ['x', 'c', 'w_ada', 'b_ada', 'norm1_w', 'w_in', 'conv_gdn', 'a_log', 'dt_bias', 'onorm_gdn', 'w_branch_gdn', 'mu_rwkv', 'w0', 'w2', 'a0', 'a2', 'g2', 'k_k', 'k_a', 'r_k', 'lnx_w', 'lnx_b', 'w_branch_rwkv', 'w_out', 'norm2_w', 'w_ffn_in', 'conv_ffn', 'w_ffn_out', 'norm_f_w', 'loss_target', 'm_w_ada', 'm_b_ada', 'm_norm1_w', 'm_w_in', 'm_conv_gdn', 'm_a_log', 'm_dt_bias', 'm_onorm_gdn', 'm_w_branch_gdn', 'm_mu_rwkv', 'm_w0', 'm_w2', 'm_a0', 'm_a2', 'm_g2', 'm_k_k', 'm_k_a', 'm_r_k', 'm_lnx_w', 'm_lnx_b', 'm_w_branch_rwkv', 'm_w_out', 'm_norm2_w', 'm_w_ffn_in', 'm_conv_ffn', 'm_w_ffn_out', 'm_norm_f_w', 'v_w_ada', 'v_b_ada', 'v_norm1_w', 'v_w_in', 'v_conv_gdn', 'v_a_log', 'v_dt_bias', 'v_onorm_gdn', 'v_w_branch_gdn', 'v_mu_rwkv', 'v_w0', 'v_w2', 'v_a0', 'v_a2', 'v_g2', 'v_k_k', 'v_k_a', 'v_r_k', 'v_lnx_w', 'v_lnx_b', 'v_w_branch_rwkv', 'v_w_out', 'v_norm2_w', 'v_w_ffn_in', 'v_conv_ffn', 'v_w_ffn_out', 'v_norm_f_w']
TWIN_OUTPUTS = ['loss', 'grad_x', 'grad_w_ada', 'grad_b_ada', 'grad_norm1_w', 'grad_w_in', 'grad_conv_gdn', 'grad_a_log', 'grad_dt_bias', 'grad_onorm_gdn', 'grad_w_branch_gdn', 'grad_mu_rwkv', 'grad_w0', 'grad_w2', 'grad_a0', 'grad_a2', 'grad_g2', 'grad_k_k', 'grad_k_a', 'grad_r_k', 'grad_lnx_w', 'grad_lnx_b', 'grad_w_branch_rwkv', 'grad_w_out', 'grad_norm2_w', 'grad_w_ffn_in', 'grad_conv_ffn', 'grad_w_ffn_out', 'grad_norm_f_w', 'delta_w_ada', 'delta_b_ada', 'delta_norm1_w', 'delta_w_in', 'delta_conv_gdn', 'delta_a_log', 'delta_dt_bias', 'delta_onorm_gdn', 'delta_w_branch_gdn', 'delta_mu_rwkv', 'delta_w0', 'delta_w2', 'delta_a0', 'delta_a2', 'delta_g2', 'delta_k_k', 'delta_k_a', 'delta_r_k', 'delta_lnx_w', 'delta_lnx_b', 'delta_w_branch_rwkv', 'delta_w_out', 'delta_norm2_w', 'delta_w_ffn_in', 'delta_conv_ffn', 'delta_w_ffn_out', 'delta_norm_f_w', 'new_m_w_ada', 'new_m_b_ada', 'new_m_norm1_w', 'new_m_w_in', 'new_m_conv_gdn', 'new_m_a_log', 'new_m_dt_bias', 'new_m_onorm_gdn', 'new_m_w_branch_gdn', 'new_m_mu_rwkv', 'new_m_w0', 'new_m_w2', 'new_m_a0', 'new_m_a2', 'new_m_g2', 'new_m_k_k', 'new_m_k_a', 'new_m_r_k', 'new_m_lnx_w', 'new_m_lnx_b', 'new_m_w_branch_rwkv', 'new_m_w_out', 'new_m_norm2_w', 'new_m_w_ffn_in', 'new_m_conv_ffn', 'new_m_w_ffn_out', 'new_m_norm_f_w', 'new_v_w_ada', 'new_v_b_ada', 'new_v_norm1_w', 'new_v_w_in', 'new_v_conv_gdn', 'new_v_a_log', 'new_v_dt_bias', 'new_v_onorm_gdn', 'new_v_w_branch_gdn', 'new_v_mu_rwkv', 'new_v_w0', 'new_v_w2', 'new_v_a0', 'new_v_a2', 'new_v_g2', 'new_v_k_k', 'new_v_k_a', 'new_v_r_k', 'new_v_lnx_w', 'new_v_lnx_b', 'new_v_w_branch_rwkv', 'new_v_w_out', 'new_v_norm2_w', 'new_v_w_ffn_in', 'new_v_conv_ffn', 'new_v_w_ffn_out', 'new_v_norm_f_w']
TWIN_LEAF_KINDS = {'loss': 'loss', 'grad_x': 'grad_x', 'grad_w_ada': 'grad_w', 'grad_b_ada': 'grad_w', 'grad_norm1_w': 'grad_w', 'grad_w_in': 'grad_w', 'grad_conv_gdn': 'grad_w', 'grad_a_log': 'grad_w', 'grad_dt_bias': 'grad_w', 'grad_onorm_gdn': 'grad_w', 'grad_w_branch_gdn': 'grad_w', 'grad_mu_rwkv': 'grad_w', 'grad_w0': 'grad_w', 'grad_w2': 'grad_w', 'grad_a0': 'grad_w', 'grad_a2': 'grad_w', 'grad_g2': 'grad_w', 'grad_k_k': 'grad_w', 'grad_k_a': 'grad_w', 'grad_r_k': 'grad_w', 'grad_lnx_w': 'grad_w', 'grad_lnx_b': 'grad_w', 'grad_w_branch_rwkv': 'grad_w', 'grad_w_out': 'grad_w', 'grad_norm2_w': 'grad_w', 'grad_w_ffn_in': 'grad_w', 'grad_conv_ffn': 'grad_w', 'grad_w_ffn_out': 'grad_w', 'grad_norm_f_w': 'grad_w', 'delta_w_ada': 'delta_w', 'delta_b_ada': 'delta_w', 'delta_norm1_w': 'delta_w', 'delta_w_in': 'delta_w', 'delta_conv_gdn': 'delta_w', 'delta_a_log': 'delta_w', 'delta_dt_bias': 'delta_w', 'delta_onorm_gdn': 'delta_w', 'delta_w_branch_gdn': 'delta_w', 'delta_mu_rwkv': 'delta_w', 'delta_w0': 'delta_w', 'delta_w2': 'delta_w', 'delta_a0': 'delta_w', 'delta_a2': 'delta_w', 'delta_g2': 'delta_w', 'delta_k_k': 'delta_w', 'delta_k_a': 'delta_w', 'delta_r_k': 'delta_w', 'delta_lnx_w': 'delta_w', 'delta_lnx_b': 'delta_w', 'delta_w_branch_rwkv': 'delta_w', 'delta_w_out': 'delta_w', 'delta_norm2_w': 'delta_w', 'delta_w_ffn_in': 'delta_w', 'delta_conv_ffn': 'delta_w', 'delta_w_ffn_out': 'delta_w', 'delta_norm_f_w': 'delta_w', 'new_m_w_ada': 'new_m', 'new_m_b_ada': 'new_m', 'new_m_norm1_w': 'new_m', 'new_m_w_in': 'new_m', 'new_m_conv_gdn': 'new_m', 'new_m_a_log': 'new_m', 'new_m_dt_bias': 'new_m', 'new_m_onorm_gdn': 'new_m', 'new_m_w_branch_gdn': 'new_m', 'new_m_mu_rwkv': 'new_m', 'new_m_w0': 'new_m', 'new_m_w2': 'new_m', 'new_m_a0': 'new_m', 'new_m_a2': 'new_m', 'new_m_g2': 'new_m', 'new_m_k_k': 'new_m', 'new_m_k_a': 'new_m', 'new_m_r_k': 'new_m', 'new_m_lnx_w': 'new_m', 'new_m_lnx_b': 'new_m', 'new_m_w_branch_rwkv': 'new_m', 'new_m_w_out': 'new_m', 'new_m_norm2_w': 'new_m', 'new_m_w_ffn_in': 'new_m', 'new_m_conv_ffn': 'new_m', 'new_m_w_ffn_out': 'new_m', 'new_m_norm_f_w': 'new_m', 'new_v_w_ada': 'new_v', 'new_v_b_ada': 'new_v', 'new_v_norm1_w': 'new_v', 'new_v_w_in': 'new_v', 'new_v_conv_gdn': 'new_v', 'new_v_a_log': 'new_v', 'new_v_dt_bias': 'new_v', 'new_v_onorm_gdn': 'new_v', 'new_v_w_branch_gdn': 'new_v', 'new_v_mu_rwkv': 'new_v', 'new_v_w0': 'new_v', 'new_v_w2': 'new_v', 'new_v_a0': 'new_v', 'new_v_a2': 'new_v', 'new_v_g2': 'new_v', 'new_v_k_k': 'new_v', 'new_v_k_a': 'new_v', 'new_v_r_k': 'new_v', 'new_v_lnx_w': 'new_v', 'new_v_lnx_b': 'new_v', 'new_v_w_branch_rwkv': 'new_v', 'new_v_w_out': 'new_v', 'new_v_norm2_w': 'new_v', 'new_v_w_ffn_in': 'new_v', 'new_v_conv_ffn': 'new_v', 'new_v_w_ffn_out': 'new_v', 'new_v_norm_f_w': 'new_v'}


def _forward(args):
    return _fwd_reference(*[args[k] for k in FWD_PARAMS])


def _output_shape():
    out = _jax.eval_shape(lambda: _forward(_fwd_setup_inputs(0)))
    return out.shape, out.dtype

N_MICROBATCH = 1
ADAM_LR = 0.001
ADAM_B1 = 0.9
ADAM_B2 = 0.999
ADAM_EPS = 1e-08
ADAM_WD = 0.01
ADAM_STEP = 10
PER_EXAMPLE_BATCH_AXIS = {'x': 0, 'c': 0, 'loss_target': 0}
SHARED_INPUTS = []
_WEIGHT_DTYPES = {'w_ada': _jnp.float32, 'b_ada': _jnp.float32, 'norm1_w': _jnp.float32, 'w_in': _jnp.float32, 'conv_gdn': _jnp.float32, 'a_log': _jnp.float32, 'dt_bias': _jnp.float32, 'onorm_gdn': _jnp.float32, 'w_branch_gdn': _jnp.float32, 'mu_rwkv': _jnp.float32, 'w0': _jnp.float32, 'w2': _jnp.float32, 'a0': _jnp.float32, 'a2': _jnp.float32, 'g2': _jnp.float32, 'k_k': _jnp.float32, 'k_a': _jnp.float32, 'r_k': _jnp.float32, 'lnx_w': _jnp.float32, 'lnx_b': _jnp.float32, 'w_branch_rwkv': _jnp.float32, 'w_out': _jnp.float32, 'norm2_w': _jnp.float32, 'w_ffn_in': _jnp.float32, 'conv_ffn': _jnp.float32, 'w_ffn_out': _jnp.float32, 'norm_f_w': _jnp.float32}
MOMENT_SCALE = {'w_ada': 8.230249e-02, 'b_ada': 1.358491e-01, 'norm1_w': 1.117302e-01, 'w_in': 5.491916e-02, 'conv_gdn': 6.655196e-02, 'a_log': 2.033883e-01, 'dt_bias': 1.995395e-01, 'onorm_gdn': 2.610128e-01, 'w_branch_gdn': 5.476416e-02, 'mu_rwkv': 8.348380e-02, 'w0': 2.610549e-02, 'w2': 7.577501e-03, 'a0': 2.286834e-02, 'a2': 2.761827e-02, 'g2': 5.926708e-02, 'k_k': 4.198165e-01, 'k_a': 3.792735e-01, 'r_k': 2.875204e-01, 'lnx_w': 4.373288e-02, 'lnx_b': 4.714351e-02, 'w_branch_rwkv': 3.702563e-02, 'w_out': 6.555814e-02, 'norm2_w': 1.208810e-01, 'w_ffn_in': 5.905768e-02, 'conv_ffn': 6.033110e-02, 'w_ffn_out': 9.744380e-02, 'norm_f_w': 3.278269e+01}


def _to_microbatches(a, axis):
    t = _jnp.moveaxis(a, axis, 0)
    t = t.reshape((N_MICROBATCH, t.shape[0] // N_MICROBATCH) + t.shape[1:])
    return _jnp.moveaxis(t, 1, axis + 1)


def setup_inputs(seed: int = 0) -> dict:
    inp = _fwd_setup_inputs(seed)
    key = _jax.random.fold_in(_jax.random.key(seed), 7919)
    shape, _ = _output_shape()
    out = dict(inp)
    out["loss_target"] = _jax.random.normal(_jax.random.fold_in(key, 0), shape, _jnp.float32)
    for i, name in enumerate(TWIN_WEIGHTS):
        w = inp[name].astype(_jnp.float32)
        if MOMENT_SCALE is None:
            s = _jnp.sqrt(_jnp.mean(_jnp.square(w)) + 1e-30)
        else:
            s = MOMENT_SCALE[name]
        km, kv = _jax.random.split(_jax.random.fold_in(key, i + 1))
        out[name] = w
        out["m_" + name] = s * _jax.random.normal(km, w.shape, _jnp.float32)
        out["v_" + name] = (s * s) * _jax.random.uniform(kv, w.shape, _jnp.float32, 0.5, 1.5)
    if N_MICROBATCH > 1:
        for name, axis in PER_EXAMPLE_BATCH_AXIS.items():
            out[name] = _to_microbatches(out[name], axis)
    return {'x': out['x'], 'c': out['c'], 'w_ada': out['w_ada'], 'b_ada': out['b_ada'], 'norm1_w': out['norm1_w'], 'w_in': out['w_in'], 'conv_gdn': out['conv_gdn'], 'a_log': out['a_log'], 'dt_bias': out['dt_bias'], 'onorm_gdn': out['onorm_gdn'], 'w_branch_gdn': out['w_branch_gdn'], 'mu_rwkv': out['mu_rwkv'], 'w0': out['w0'], 'w2': out['w2'], 'a0': out['a0'], 'a2': out['a2'], 'g2': out['g2'], 'k_k': out['k_k'], 'k_a': out['k_a'], 'r_k': out['r_k'], 'lnx_w': out['lnx_w'], 'lnx_b': out['lnx_b'], 'w_branch_rwkv': out['w_branch_rwkv'], 'w_out': out['w_out'], 'norm2_w': out['norm2_w'], 'w_ffn_in': out['w_ffn_in'], 'conv_ffn': out['conv_ffn'], 'w_ffn_out': out['w_ffn_out'], 'norm_f_w': out['norm_f_w'], 'loss_target': out['loss_target'], 'm_w_ada': out['m_w_ada'], 'm_b_ada': out['m_b_ada'], 'm_norm1_w': out['m_norm1_w'], 'm_w_in': out['m_w_in'], 'm_conv_gdn': out['m_conv_gdn'], 'm_a_log': out['m_a_log'], 'm_dt_bias': out['m_dt_bias'], 'm_onorm_gdn': out['m_onorm_gdn'], 'm_w_branch_gdn': out['m_w_branch_gdn'], 'm_mu_rwkv': out['m_mu_rwkv'], 'm_w0': out['m_w0'], 'm_w2': out['m_w2'], 'm_a0': out['m_a0'], 'm_a2': out['m_a2'], 'm_g2': out['m_g2'], 'm_k_k': out['m_k_k'], 'm_k_a': out['m_k_a'], 'm_r_k': out['m_r_k'], 'm_lnx_w': out['m_lnx_w'], 'm_lnx_b': out['m_lnx_b'], 'm_w_branch_rwkv': out['m_w_branch_rwkv'], 'm_w_out': out['m_w_out'], 'm_norm2_w': out['m_norm2_w'], 'm_w_ffn_in': out['m_w_ffn_in'], 'm_conv_ffn': out['m_conv_ffn'], 'm_w_ffn_out': out['m_w_ffn_out'], 'm_norm_f_w': out['m_norm_f_w'], 'v_w_ada': out['v_w_ada'], 'v_b_ada': out['v_b_ada'], 'v_norm1_w': out['v_norm1_w'], 'v_w_in': out['v_w_in'], 'v_conv_gdn': out['v_conv_gdn'], 'v_a_log': out['v_a_log'], 'v_dt_bias': out['v_dt_bias'], 'v_onorm_gdn': out['v_onorm_gdn'], 'v_w_branch_gdn': out['v_w_branch_gdn'], 'v_mu_rwkv': out['v_mu_rwkv'], 'v_w0': out['v_w0'], 'v_w2': out['v_w2'], 'v_a0': out['v_a0'], 'v_a2': out['v_a2'], 'v_g2': out['v_g2'], 'v_k_k': out['v_k_k'], 'v_k_a': out['v_k_a'], 'v_r_k': out['v_r_k'], 'v_lnx_w': out['v_lnx_w'], 'v_lnx_b': out['v_lnx_b'], 'v_w_branch_rwkv': out['v_w_branch_rwkv'], 'v_w_out': out['v_w_out'], 'v_norm2_w': out['v_norm2_w'], 'v_w_ffn_in': out['v_w_ffn_in'], 'v_conv_ffn': out['v_conv_ffn'], 'v_w_ffn_out': out['v_w_ffn_out'], 'v_norm_f_w': out['v_norm_f_w']}


def _loss(weights, diff, rest, loss_target):
    with _jax.named_scope("forward"):
        args = {**rest, TWIN_DIFF_INPUT: diff, **{k: w.astype(_WEIGHT_DTYPES[k]) for k, w in weights.items()}}
        y = _forward(args)
    with _jax.named_scope("loss_head"):
        err = _jnp.square(y.astype(_jnp.float32) - loss_target)
        return 0.5 * _jnp.sum(_jnp.mean(err, axis=-1)) if err.ndim else 0.5 * err


def _adamw(w, g, m, v):
    m = ADAM_B1 * m + (1.0 - ADAM_B1) * g
    v = ADAM_B2 * v + (1.0 - ADAM_B2) * _jnp.square(g)
    m_hat = m / (1.0 - ADAM_B1 ** ADAM_STEP)
    v_hat = v / (1.0 - ADAM_B2 ** ADAM_STEP)
    delta = -ADAM_LR * (m_hat / (_jnp.sqrt(v_hat) + ADAM_EPS) + ADAM_WD * w)
    return delta, m, v


def reference(x, c, w_ada, b_ada, norm1_w, w_in, conv_gdn, a_log, dt_bias, onorm_gdn, w_branch_gdn, mu_rwkv, w0, w2, a0, a2, g2, k_k, k_a, r_k, lnx_w, lnx_b, w_branch_rwkv, w_out, norm2_w, w_ffn_in, conv_ffn, w_ffn_out, norm_f_w, loss_target, m_w_ada, m_b_ada, m_norm1_w, m_w_in, m_conv_gdn, m_a_log, m_dt_bias, m_onorm_gdn, m_w_branch_gdn, m_mu_rwkv, m_w0, m_w2, m_a0, m_a2, m_g2, m_k_k, m_k_a, m_r_k, m_lnx_w, m_lnx_b, m_w_branch_rwkv, m_w_out, m_norm2_w, m_w_ffn_in, m_conv_ffn, m_w_ffn_out, m_norm_f_w, v_w_ada, v_b_ada, v_norm1_w, v_w_in, v_conv_gdn, v_a_log, v_dt_bias, v_onorm_gdn, v_w_branch_gdn, v_mu_rwkv, v_w0, v_w2, v_a0, v_a2, v_g2, v_k_k, v_k_a, v_r_k, v_lnx_w, v_lnx_b, v_w_branch_rwkv, v_w_out, v_norm2_w, v_w_ffn_in, v_conv_ffn, v_w_ffn_out, v_norm_f_w):
    given = dict(x=x, c=c, w_ada=w_ada, b_ada=b_ada, norm1_w=norm1_w, w_in=w_in, conv_gdn=conv_gdn, a_log=a_log, dt_bias=dt_bias, onorm_gdn=onorm_gdn, w_branch_gdn=w_branch_gdn, mu_rwkv=mu_rwkv, w0=w0, w2=w2, a0=a0, a2=a2, g2=g2, k_k=k_k, k_a=k_a, r_k=r_k, lnx_w=lnx_w, lnx_b=lnx_b, w_branch_rwkv=w_branch_rwkv, w_out=w_out, norm2_w=norm2_w, w_ffn_in=w_ffn_in, conv_ffn=conv_ffn, w_ffn_out=w_ffn_out, norm_f_w=norm_f_w, loss_target=loss_target, m_w_ada=m_w_ada, m_b_ada=m_b_ada, m_norm1_w=m_norm1_w, m_w_in=m_w_in, m_conv_gdn=m_conv_gdn, m_a_log=m_a_log, m_dt_bias=m_dt_bias, m_onorm_gdn=m_onorm_gdn, m_w_branch_gdn=m_w_branch_gdn, m_mu_rwkv=m_mu_rwkv, m_w0=m_w0, m_w2=m_w2, m_a0=m_a0, m_a2=m_a2, m_g2=m_g2, m_k_k=m_k_k, m_k_a=m_k_a, m_r_k=m_r_k, m_lnx_w=m_lnx_w, m_lnx_b=m_lnx_b, m_w_branch_rwkv=m_w_branch_rwkv, m_w_out=m_w_out, m_norm2_w=m_norm2_w, m_w_ffn_in=m_w_ffn_in, m_conv_ffn=m_conv_ffn, m_w_ffn_out=m_w_ffn_out, m_norm_f_w=m_norm_f_w, v_w_ada=v_w_ada, v_b_ada=v_b_ada, v_norm1_w=v_norm1_w, v_w_in=v_w_in, v_conv_gdn=v_conv_gdn, v_a_log=v_a_log, v_dt_bias=v_dt_bias, v_onorm_gdn=v_onorm_gdn, v_w_branch_gdn=v_w_branch_gdn, v_mu_rwkv=v_mu_rwkv, v_w0=v_w0, v_w2=v_w2, v_a0=v_a0, v_a2=v_a2, v_g2=v_g2, v_k_k=v_k_k, v_k_a=v_k_a, v_r_k=v_r_k, v_lnx_w=v_lnx_w, v_lnx_b=v_lnx_b, v_w_branch_rwkv=v_w_branch_rwkv, v_w_out=v_w_out, v_norm2_w=v_norm2_w, v_w_ffn_in=v_w_ffn_in, v_conv_ffn=v_conv_ffn, v_w_ffn_out=v_w_ffn_out, v_norm_f_w=v_norm_f_w)
    weights = {n: given[n] for n in TWIN_WEIGHTS}
    shared = {n: given[n] for n in SHARED_INPUTS}
    per_example = {n: given[n] for n in ['x', 'c']}
    grad_fn = _jax.value_and_grad(_loss, argnums=(0, 1))

    def one_microbatch(ex, loss_target):
        ex = dict(ex)
        diff = ex.pop(TWIN_DIFF_INPUT)
        return grad_fn(weights, diff, {**shared, **ex}, loss_target)

    if N_MICROBATCH == 1:
        loss, (grad_w, grad_x) = one_microbatch(per_example, given["loss_target"])
    else:
        def body(carry, xs):
            loss_sum, grad_sum = carry
            l_k, (gw_k, gx_k) = one_microbatch(xs[0], xs[1])
            with _jax.named_scope("update"):
                return (loss_sum + l_k, _jax.tree.map(_jnp.add, grad_sum, gw_k)), gx_k

        init = (_jnp.zeros((), _jnp.float32), _jax.tree.map(_jnp.zeros_like, weights))
        (loss, grad_w), grad_x = _jax.lax.scan(body, init, (per_example, given["loss_target"]))
    with _jax.named_scope("update"):
        delta_w, new_m, new_v = {}, {}, {}
        for n in TWIN_WEIGHTS:
            delta_w[n], new_m[n], new_v[n] = _adamw(weights[n], grad_w[n], given["m_" + n], given["v_" + n])
    return (loss, grad_x, *[grad_w[n] for n in TWIN_WEIGHTS], *[delta_w[n] for n in TWIN_WEIGHTS],
            *[new_m[n] for n in TWIN_WEIGHTS], *[new_v[n] for n in TWIN_WEIGHTS])
```

```python
import functools
import math

import numpy as np
import jax
import jax.numpy as jnp
from jax import lax
from jax.experimental import pallas as pl
from jax.experimental.pallas import tpu as pltpu

f32 = jnp.float32
bf16 = jnp.bfloat16

LANES = 128
HEADS = 8
HEAD_DIM = 64
WIDTH = HEADS * HEAD_DIM
CHUNK = 64
D_FF = 2816
NORM_EPS = 1e-6
LNX_EPS = 64e-5
PACK_W = 1024
MESH_ID = pl.DeviceIdType.MESH

ADAM_LR, ADAM_B1, ADAM_B2, ADAM_EPS, ADAM_WD, ADAM_STEP = 0.001, 0.9, 0.999, 1e-08, 0.01, 10


def _pick(n, target, mult):
    if n <= target:
        return n
    best = None
    for t in range(mult, target + 1, mult):
        if n % t == 0:
            best = t
    assert best is not None, (n, target, mult)
    return best


def _split_bf16(x, n):
    parts, r = [], x
    for i in range(n):
        p = r.astype(bf16)
        parts.append(p)
        if i + 1 < n:
            r = r - p.astype(f32)
    return parts


def _xdot_r_impl(x, m, n, dims):
    acc = None
    for p in _split_bf16(x, n):
        t = lax.dot_general(p, m, dims, preferred_element_type=f32)
        acc = t if acc is None else acc + t
    return acc


def _make_xdot_r(n):
    nn = (((1,), (0,)), ((), ()))
    nt = (((1,), (1,)), ((), ()))

    @jax.custom_vjp
    def xdot(x, m):
        return _xdot_r_impl(x, m, n, nn)

    def fwd(x, m):
        return _xdot_r_impl(x, m, n, nn), m

    def bwd(m, ct):
        return _xdot_r_impl(ct, m, n, nt), jnp.zeros_like(m)

    xdot.defvjp(fwd, bwd)
    return xdot


_segsum = _make_xdot_r(2)


def _xdot_l_impl(m, x, n, dims):
    acc = None
    for p in _split_bf16(x, n):
        t = lax.dot_general(m, p, dims, preferred_element_type=f32)
        acc = t if acc is None else acc + t
    return acc


@jax.custom_vjp
def _xdot_l(m, x):
    return _xdot_l_impl(m, x, 3, (((1,), (0,)), ((), ())))


def _xdot_l_fwd(m, x):
    return _xdot_l(m, x), m


def _xdot_l_bwd(m, ct):
    return jnp.zeros_like(m), _xdot_l_impl(m, ct, 3, (((0,), (0,)), ((), ())))


_xdot_l.defvjp(_xdot_l_fwd, _xdot_l_bwd)


@jax.custom_vjp
def _bdot(x, w):
    return jnp.dot(x.astype(bf16), w.astype(bf16), preferred_element_type=f32)


def _bdot_fwd(x, w):
    return _bdot(x, w), (x, w)


def _bdot_bwd(res, ct):
    x, w = res
    c = ct.astype(bf16)
    dx = lax.dot_general(c, w.astype(bf16), (((1,), (1,)), ((), ())), preferred_element_type=f32)
    dw = lax.dot_general(x.astype(bf16), c, (((0,), (0,)), ((), ())), preferred_element_type=f32)
    return dx, dw


_bdot.defvjp(_bdot_fwd, _bdot_bwd)


def _silu(x):
    return x * jax.nn.sigmoid(x)


def _softplus(x):
    return jnp.maximum(x, 0.0) + jnp.log(1.0 + jnp.exp(-jnp.abs(x)))


def _rms(x, w, eps):
    return x * lax.rsqrt(jnp.mean(x * x, axis=-1, keepdims=True) + eps) * w


def _seg_matrix(width, seg):
    i = np.arange(width)
    return jnp.asarray((i[:, None] // seg) == (i[None, :] // seg), dtype=bf16)


def _chunk_tri(rows, chunk):
    i = np.arange(rows)
    return jnp.asarray(((i[:, None] // chunk) == (i[None, :] // chunk)) & (i[:, None] >= i[None, :]), dtype=bf16)


def _shift_down(a, s):
    return jnp.pad(a, ((s, 0), (0, 0)))[: a.shape[0]]


def _shift_up(a, s):
    return jnp.pad(a, ((0, s), (0, 0)))[s:]


def _tile_spec(tt, w, cidx):
    return pl.BlockSpec((tt, w), lambda i: (i, cidx))


def _full_spec(shape):
    nd = len(shape)
    return pl.BlockSpec(shape, lambda i: (0,) * nd)


def _stage_fwd(name, fn, tiles, params, outs, tt):
    rows = tiles[0][0].shape[0]
    nt, npar = len(tiles), len(params)

    def body(*refs):
        ts = [r[...].astype(f32) for r in refs[:nt]]
        ps = [r[...] for r in refs[nt:nt + npar]]
        res = fn(ps, ts)
        for r, v in zip(refs[nt + npar:], res):
            r[...] = v.astype(r.dtype)

    return pl.pallas_call(
        body, grid=(rows // tt,),
        in_specs=[_tile_spec(tt, w, ci) for (_, w, ci) in tiles] + [_full_spec(p.shape) for p in params],
        out_specs=[_tile_spec(tt, w, 0) for (w, _) in outs],
        out_shape=[jax.ShapeDtypeStruct((rows, w), dt) for (w, dt) in outs],
        compiler_params=pltpu.CompilerParams(dimension_semantics=("parallel",)),
        name=name,
    )(*[t[0] for t in tiles], *params)


def _stage_bwd(name, fn, tiles, params, cots, tt, tile_grad, param_grad):
    rows = tiles[0][0].shape[0]
    nt, npar = len(tiles), len(params)
    flat_cots = [c for group in cots for c in group]
    groups = [len(g) for g in cots]
    ncot = len(flat_cots)
    dt_w = [w for (_, w, _), g in zip(tiles, tile_grad) if g]
    dp_shapes = [p.shape for p, g in zip(params, param_grad) if g]
    ndt = len(dt_w)

    def body(*refs):
        i = pl.program_id(0)
        t_refs = refs[:nt]
        p_refs = refs[nt:nt + npar]
        c_refs = refs[nt + npar:nt + npar + ncot]
        dt_refs = refs[nt + npar + ncot:nt + npar + ncot + ndt]
        dp_refs = refs[nt + npar + ncot + ndt:]
        ts = [r[...].astype(f32) for r in t_refs]
        ps = [r[...] for r in p_refs]

        def f(dp, dt):
            dp, dt = iter(dp), iter(dt)
            pp = [next(dp) if g else p for p, g in zip(ps, param_grad)]
            tl = [next(dt) if g else t for t, g in zip(ts, tile_grad)]
            return fn(pp, tl)

        _, vjp = jax.vjp(f, [p for p, g in zip(ps, param_grad) if g], [t for t, g in zip(ts, tile_grad) if g])
        cs, j = [], 0
        for n in groups:
            acc = c_refs[j][...].astype(f32)
            for q in range(1, n):
                acc = acc + c_refs[j + q][...].astype(f32)
            cs.append(acc)
            j += n
        gp, gt = vjp(cs)
        for r, v in zip(dt_refs, gt):
            r[...] = v

        @pl.when(i == 0)
        def _():
            for r in dp_refs:
                r[...] = jnp.zeros_like(r)

        for r, v in zip(dp_refs, gp):
            r[...] += v

    res = pl.pallas_call(
        body, grid=(rows // tt,),
        in_specs=[_tile_spec(tt, w, ci) for (_, w, ci) in tiles] + [_full_spec(p.shape) for p in params]
        + [_tile_spec(tt, w, ci) for (_, w, ci) in flat_cots],
        out_specs=[_tile_spec(tt, w, 0) for w in dt_w] + [_full_spec(s) for s in dp_shapes],
        out_shape=[jax.ShapeDtypeStruct((rows, w), f32) for w in dt_w] + [jax.ShapeDtypeStruct(s, f32) for s in dp_shapes],
        compiler_params=pltpu.CompilerParams(dimension_semantics=("arbitrary",)),
        name=name,
    )(*[t[0] for t in tiles], *params, *[c[0] for c in flat_cots])
    return list(res[:ndt]), list(res[ndt:])


def _whole(a):
    return (a, a.shape[1], 0)


def _matmul(name, a, b, mode, out_dtype=f32, tm=512, tn=512, tk=512):
    if mode == "nn":
        (M, K), (_, N) = a.shape, b.shape
    elif mode == "nt":
        (M, K), (N, _) = a.shape, b.shape
    else:
        (K, M), (_, N) = a.shape, b.shape
    tm, tn = _pick(M, tm, LANES), _pick(N, tn, LANES)
    tk = _pick(K, tk, LANES if mode != "tn" else 16)
    nk = K // tk
    if mode == "nn":
        a_spec = pl.BlockSpec((tm, tk), lambda i, j, k: (i, k))
        b_spec = pl.BlockSpec((tk, tn), lambda i, j, k: (k, j))
        dims = (((1,), (0,)), ((), ()))
    elif mode == "nt":
        a_spec = pl.BlockSpec((tm, tk), lambda i, j, k: (i, k))
        b_spec = pl.BlockSpec((tn, tk), lambda i, j, k: (j, k))
        dims = (((1,), (1,)), ((), ()))
    else:
        a_spec = pl.BlockSpec((tk, tm), lambda i, j, k: (k, i))
        b_spec = pl.BlockSpec((tk, tn), lambda i, j, k: (k, j))
        dims = (((0,), (0,)), ((), ()))

    def body(a_ref, b_ref, o_ref, acc_ref):
        k = pl.program_id(2)

        @pl.when(k == 0)
        def _():
            acc_ref[...] = jnp.zeros_like(acc_ref)

        acc_ref[...] += lax.dot_general(a_ref[...].astype(bf16), b_ref[...].astype(bf16), dims,
                                        preferred_element_type=f32)

        @pl.when(k == nk - 1)
        def _():
            o_ref[...] = acc_ref[...].astype(o_ref.dtype)

    return pl.pallas_call(
        body, grid=(M // tm, N // tn, nk),
        in_specs=[a_spec, b_spec],
        out_specs=pl.BlockSpec((tm, tn), lambda i, j, k: (i, j)),
        out_shape=jax.ShapeDtypeStruct((M, N), out_dtype),
        scratch_shapes=[pltpu.VMEM((tm, tn), f32)],
        compiler_params=pltpu.CompilerParams(dimension_semantics=("parallel", "parallel", "arbitrary")),
        name=name,
    )(a, b)


_SCAN_PRECISION = lax.Precision.HIGHEST


def _bmm(a, b):
    return jnp.einsum("hij,hjk->hik", a, b, preferred_element_type=f32, precision=_SCAN_PRECISION)


def _bmm_nt(a, b):
    return jnp.einsum("hik,hjk->hij", a, b, preferred_element_type=f32, precision=_SCAN_PRECISION)


def _bmm_tn(a, b):
    return jnp.einsum("hki,hkj->hij", a, b, preferred_element_type=f32, precision=_SCAN_PRECISION)


def _masks(n):
    r = lax.broadcasted_iota(jnp.int32, (n, n), 0)
    c = lax.broadcasted_iota(jnp.int32, (n, n), 1)
    return (r >= c)[None], (r > c)[None], (r == c)[None]


def _neumann_inverse(m, eye, n):
    p = eye + m
    mk, k = m, 1
    while 2 * k < n:
        mk = _bmm(mk, mk)
        p = p + _bmm(p, mk)
        k *= 2
    return p


def _gdn_chunk(s, q, k, v, beta, gc, gr, gl):
    n = q.shape[1]
    causal, strict, diag = _masks(n)
    eye = diag.astype(f32)
    decay = jnp.where(causal, jnp.exp(jnp.where(causal, gc - gr, 0.0)), 0.0)
    kb = k * beta
    vb = v * beta
    lower = jnp.where(strict, _bmm_nt(kb, k) * decay, 0.0)
    t_mat = _neumann_inverse(-lower, eye, n)
    egc = jnp.exp(gc)
    u = _bmm(t_mat, vb)
    w = _bmm(t_mat, kb * egc)
    attn = jnp.where(causal, _bmm_nt(q, k) * decay, 0.0)
    v_new = u - _bmm(w, s)
    o = _bmm(q * egc, s) + _bmm(attn, v_new)
    k_dec = k * jnp.exp(gl - gc)
    s_new = s * jnp.exp(gl) + _bmm_tn(k_dec, v_new)
    return s_new, o


def _rwkv_chunk(s, r, lw, k, v, a, b):
    n = r.shape[1]
    causal, strict, diag = _masks(n)
    eye = diag.astype(f32)
    tri = jnp.broadcast_to(causal.astype(f32), (r.shape[0], n, n))
    lc = _bmm(tri, lw)
    ein = jnp.exp(lc)
    eout = jnp.exp(-lc)
    a_t = a * jnp.exp(lc - lw)
    b_t = b * eout
    k_t = k * eout
    r_t = r * ein
    a_ab = jnp.where(strict, _bmm_nt(a_t, b_t), 0.0)
    a_ak = jnp.where(strict, _bmm_nt(a_t, k_t), 0.0)
    u = _bmm(_neumann_inverse(a_ab, eye, n), _bmm_nt(a_t, s) + _bmm(a_ak, v))
    y = (_bmm_nt(r_t, s) + _bmm(jnp.where(causal, _bmm_nt(r_t, b_t), 0.0), u)
         + _bmm(jnp.where(causal, _bmm_nt(r_t, k_t), 0.0), v))
    e_last = jnp.exp(jnp.sum(lw, axis=1, keepdims=True))
    s_new = s * e_last + _bmm_tn(u, b_t * e_last) + _bmm_tn(v, k_t * e_last)
    return s_new, y


def _chunk_spec(shape, n, reverse):
    blk = (1,) + tuple(shape[1:])
    if reverse:
        return pl.BlockSpec(blk, lambda i: (n - 1 - i, 0, 0, 0))
    return pl.BlockSpec(blk, lambda i: (i, 0, 0, 0))


def _scan_fwd(name, fn, ins, out_dim):
    n, h = ins[0].shape[0], ins[0].shape[1]
    c = ins[0].shape[2]
    nin = len(ins)

    def body(*refs):
        in_refs, o_ref, sh_ref, s_scr = refs[:nin], refs[nin], refs[nin + 1], refs[nin + 2]

        @pl.when(pl.program_id(0) == 0)
        def _():
            s_scr[...] = jnp.zeros_like(s_scr)

        s = s_scr[...]
        sh_ref[0] = s
        s_new, o = fn(s, *[r[0] for r in in_refs])
        o_ref[0] = o
        s_scr[...] = s_new

    oshape = (n, h, c, out_dim)
    sshape = (n, h, HEAD_DIM, HEAD_DIM)
    return pl.pallas_call(
        body, grid=(n,),
        in_specs=[_chunk_spec(a.shape, n, False) for a in ins],
        out_specs=[_chunk_spec(oshape, n, False), _chunk_spec(sshape, n, False)],
        out_shape=[jax.ShapeDtypeStruct(oshape, f32), jax.ShapeDtypeStruct(sshape, f32)],
        scratch_shapes=[pltpu.VMEM((h, HEAD_DIM, HEAD_DIM), f32)],
        compiler_params=pltpu.CompilerParams(dimension_semantics=("arbitrary",)),
        name=name,
    )(*ins)


def _scan_bwd(name, fn, ins, s_hist, d_out):
    n = ins[0].shape[0]
    nin = len(ins)

    def body(*refs):
        in_refs = refs[:nin]
        sh_ref, do_ref = refs[nin], refs[nin + 1]
        g_refs = refs[nin + 2:nin + 2 + nin]
        ds_scr = refs[nin + 2 + nin]

        @pl.when(pl.program_id(0) == 0)
        def _():
            ds_scr[...] = jnp.zeros_like(ds_scr)

        _, vjp = jax.vjp(fn, sh_ref[0], *[r[0] for r in in_refs])
        g = vjp((ds_scr[...], do_ref[0]))
        ds_scr[...] = g[0]
        for r, v in zip(g_refs, g[1:]):
            r[0] = v

    return pl.pallas_call(
        body, grid=(n,),
        in_specs=[_chunk_spec(a.shape, n, True) for a in ins]
        + [_chunk_spec(s_hist.shape, n, True), _chunk_spec(d_out.shape, n, True)],
        out_specs=[_chunk_spec(a.shape, n, True) for a in ins],
        out_shape=[jax.ShapeDtypeStruct(a.shape, f32) for a in ins],
        scratch_shapes=[pltpu.VMEM(s_hist.shape[1:], f32)],
        compiler_params=pltpu.CompilerParams(dimension_semantics=("arbitrary",)),
        name=name,
    )(*ins, s_hist, d_out)


def _to_chunks(a, n):
    return a.reshape(n, CHUNK, HEADS, a.shape[1] // HEADS).transpose(0, 2, 1, 3)


def _from_chunks(a):
    n, h, c, d = a.shape
    return a.transpose(0, 2, 1, 3).reshape(n * c, h * d)


def _fn_norm_mod(ps, ts):
    nw, shift, scale = ps
    (x,) = ts
    return [_rms(x, nw, NORM_EPS) * (1.0 + scale) + shift]


def _fn_resid_norm_mod(ps, ts):
    gate, nw, shift, scale = ps
    x, mo = ts
    x1 = x + gate * mo
    return [x1, _rms(x1, nw, NORM_EPS) * (1.0 + scale) + shift]


def _fn_gdn_pre(ps, ts):
    cw = ps[:12]
    alog, dtb, seg, tri = ps[12:]
    ba = ts[12]
    outs = []
    for part in range(3):
        x = ts[4 * part:4 * part + 4]
        w = cw[4 * part:4 * part + 4]
        conv = w[3] * x[0] + w[2] * x[1] + w[1] * x[2] + w[0] * x[3]
        u = _silu(conv)
        if part < 2:
            u = u * lax.rsqrt(_segsum(u * u, seg) + 1e-6)
            if part == 0:
                u = u * (HEAD_DIM ** -0.5)
        outs.append(u)
    beta = jax.nn.sigmoid(ba)
    g = -jnp.exp(alog) * _softplus(ba + dtb)
    gc = _xdot_l(tri, g)
    return outs + [beta, gc]


def _fn_gdn_post(ps, ts):
    ow, seg = ps
    o, z = ts
    ms = _segsum(o * o, seg) * (1.0 / HEAD_DIM)
    return [o * lax.rsqrt(ms + NORM_EPS) * ow * _silu(z)]


def _fn_rwkv_pre(ps, ts):
    mu_r, mu_k, mu_v, mu_l, mu_g, w0, w2p, a0, a2p, g2p, k_k, k_a, seg = ps
    r0, k0, v0, l0, g0, r1, k1, v1, l1, g1 = ts
    xr = r0 + (r1 - r0) * mu_r
    xk = k0 + (k1 - k0) * mu_k
    xv = v0 + (v1 - v0) * mu_v
    xl = l0 + (l1 - l0) * mu_l
    xg = g0 + (g1 - g0) * mu_g
    w = -_softplus(-(w0 + _bdot(jnp.tanh(xl), w2p))) - 0.5
    lw = -jnp.exp(w)
    a = jax.nn.sigmoid(a0 + _bdot(xl, a2p))
    g = _bdot(jax.nn.sigmoid(xg), g2p)
    kk = xk * k_k
    kk = kk * lax.rsqrt(_segsum(kk * kk, seg) + 1e-6)
    k2 = xk * (1.0 + (a - 1.0) * k_a)
    return [xr, lw, k2, xv, -kk, kk * a, g]


def _fn_rwkv_post(ps, ts):
    lw_, lb_, rk, seg = ps
    y, r, k2, v, g = ts
    inv = 1.0 / HEAD_DIM
    yc = y - _segsum(y, seg) * inv
    var = _segsum(yc * yc, seg) * inv
    yn = yc * lax.rsqrt(var + LNX_EPS) * lw_ + lb_
    bonus = _segsum(r * k2 * rk, seg) * v
    return [(yn + bonus) * g]


def _fn_merge(ps, ts):
    gla, glb, ya, yb = ts
    return [jax.nn.sigmoid(gla) * ya + jax.nn.sigmoid(glb) * yb]


def _fn_convglu(ps, ts):
    c0, c1, c2 = ps
    g0, g1, g2, up = ts
    return [_silu(c2 * g0 + c1 * g1 + c0 * g2) * up]


def _fn_add(ps, ts):
    acc = ts[0]
    for t in ts[1:]:
        acc = acc + t
    return [acc]


def _add_n(name, arrs, tt):
    return _stage_fwd(name, _fn_add, [_whole(a) for a in arrs], [], [(arrs[0].shape[1], f32)], tt)[0]


def _final_stage(x1, fo, tgt, gate2, nfw, tt):
    rows, d = x1.shape

    def loss_fn(gate, nw, xa, fa, tg):
        y = _rms(xa + gate * fa, nw, NORM_EPS)
        err = (y - tg) ** 2
        return 0.5 * jnp.sum(jnp.mean(err, axis=-1, keepdims=True), axis=0, keepdims=True)

    def body(x_ref, f_ref, t_ref, g_ref, w_ref, dx_ref, df_ref, dg_ref, dw_ref, l_ref):
        i = pl.program_id(0)
        args = (g_ref[...], w_ref[...], x_ref[...], f_ref[...])
        tg = t_ref[...]
        lv, vjp = jax.vjp(lambda g, w, xa, fa: loss_fn(g, w, xa, fa, tg), *args)
        dg, dw, dx, df = vjp(jnp.ones((1, 1), f32))
        dx_ref[...] = dx
        df_ref[...] = df

        @pl.when(i == 0)
        def _():
            dg_ref[...] = jnp.zeros_like(dg_ref)
            dw_ref[...] = jnp.zeros_like(dw_ref)
            l_ref[...] = jnp.zeros_like(l_ref)

        dg_ref[...] += dg
        dw_ref[...] += dw
        l_ref[...] += jnp.broadcast_to(lv, l_ref.shape)

    row = pl.BlockSpec((tt, d), lambda i: (i, 0))
    vec = pl.BlockSpec((1, d), lambda i: (0, 0))
    return pl.pallas_call(
        body, grid=(rows // tt,),
        in_specs=[row, row, row, vec, vec],
        out_specs=[row, row, vec, vec, pl.BlockSpec((1, LANES), lambda i: (0, 0))],
        out_shape=[jax.ShapeDtypeStruct((rows, d), f32)] * 2 + [jax.ShapeDtypeStruct((1, d), f32)] * 2
        + [jax.ShapeDtypeStruct((1, LANES), f32)],
        compiler_params=pltpu.CompilerParams(dimension_semantics=("arbitrary",)),
        name="loss_head",
    )(x1, fo, tgt, gate2, nfw)


def _ada_fwd(c_all, w_shard, b_cols):
    def body(c_ref, w_ref, b_ref, cond_ref, mod_ref):
        cond = _silu(c_ref[...])
        cond_ref[...] = cond
        mod_ref[...] = jnp.dot(cond.astype(bf16), w_ref[...].astype(bf16), preferred_element_type=f32) + b_ref[...]

    n = w_shard.shape[1]
    return pl.pallas_call(
        body, out_shape=[jax.ShapeDtypeStruct(c_all.shape, f32), jax.ShapeDtypeStruct((c_all.shape[0], n), f32)],
        name="ada_fwd",
    )(c_all, w_shard, b_cols)


def _adamw(name, w, g, m, v):
    rows, width = w.shape
    tt = _pick(rows, 512, 8)
    c1 = 1.0 - ADAM_B1 ** ADAM_STEP
    c2 = 1.0 - ADAM_B2 ** ADAM_STEP

    def body(w_ref, g_ref, m_ref, v_ref, d_ref, mo_ref, vo_ref):
        gg = g_ref[...]
        mn = ADAM_B1 * m_ref[...] + (1.0 - ADAM_B1) * gg
        vn = ADAM_B2 * v_ref[...] + (1.0 - ADAM_B2) * (gg * gg)
        m_hat = mn / c1
        v_hat = vn / c2
        d_ref[...] = -ADAM_LR * (m_hat / (jnp.sqrt(v_hat) + ADAM_EPS) + ADAM_WD * w_ref[...])
        mo_ref[...] = mn
        vo_ref[...] = vn

    spec = pl.BlockSpec((tt, width), lambda i: (i, 0))
    return pl.pallas_call(
        body, grid=(rows // tt,), in_specs=[spec] * 4, out_specs=[spec] * 3,
        out_shape=[jax.ShapeDtypeStruct((rows, width), f32)] * 3,
        compiler_params=pltpu.CompilerParams(dimension_semantics=("parallel",)),
        name=name,
    )(w, g, m, v)


def _place():
    return lax.axis_index("x"), lax.axis_index("y"), lax.axis_index("c")


def _ag8(name, blk):
    m, w = blk.shape
    flips = [(dx, dy, dc) for dx in (0, 1) for dy in (0, 1) for dc in (0, 1)][1:]

    def body(x_ref, out_ref, send_sems, recv_sems, local_sem):
        x, y, c = _place()
        me = 4 * x + 2 * y + c
        mine = pltpu.make_async_copy(x_ref, out_ref.at[me], local_sem)
        mine.start()
        peers = [((1 - x) if dx else x, (1 - y) if dy else y, (1 - c) if dc else c) for dx, dy, dc in flips]
        sends = []
        for k, peer in enumerate(peers):
            cp = pltpu.make_async_remote_copy(src_ref=x_ref, dst_ref=out_ref.at[me], send_sem=send_sems.at[k],
                                              recv_sem=recv_sems.at[k], device_id=peer, device_id_type=MESH_ID)
            cp.start()
            sends.append(cp)
        for k, (px, py, pc) in enumerate(peers):
            pltpu.make_async_remote_copy(src_ref=x_ref, dst_ref=out_ref.at[4 * px + 2 * py + pc],
                                         send_sem=send_sems.at[k], recv_sem=recv_sems.at[k],
                                         device_id=(px, py, pc), device_id_type=MESH_ID).wait_recv()
        for cp in sends:
            cp.wait_send()
        mine.wait()

    return pl.pallas_call(
        body, out_shape=jax.ShapeDtypeStruct((8, m, w), blk.dtype),
        in_specs=[pl.BlockSpec(memory_space=pltpu.VMEM)], out_specs=pl.BlockSpec(memory_space=pltpu.VMEM),
        scratch_shapes=[pltpu.SemaphoreType.DMA((7,)), pltpu.SemaphoreType.DMA((7,)), pltpu.SemaphoreType.DMA],
        name=name,
    )(blk)


def _other_chips(x, y):
    return [(1 - x, y), (x, 1 - y), (1 - x, 1 - y)]


_ANY = pl.BlockSpec(memory_space=pl.ANY)


def _ag4_big(w):
    r, width = w.shape
    rh = r // 2

    def body(w_ref, out_ref, send_sems, recv_sems, local_sem):
        x, y, c = _place()
        chip = 2 * x + y
        my_half = pl.ds(c * rh, rh)
        other_half = pl.ds((1 - c) * rh, rh)
        sibling = (x, y, 1 - c)
        mine = pltpu.make_async_copy(w_ref, out_ref.at[chip], local_sem)
        mine.start()

        def rcopy(src, dst, k, dev):
            return pltpu.make_async_remote_copy(src_ref=src, dst_ref=dst, send_sem=send_sems.at[k],
                                                recv_sem=recv_sems.at[k], device_id=dev, device_id_type=MESH_ID)

        chips = _other_chips(x, y)
        copies = []
        for k, (px, py) in enumerate(chips):
            cp = rcopy(w_ref.at[my_half], out_ref.at[chip, my_half], k, (px, py, c))
            cp.start()
            copies.append(cp)
        for k, (px, py) in enumerate(chips):
            blk = out_ref.at[2 * px + py, my_half]
            rcopy(blk, blk, k, (px, py, c)).wait_recv()
            fw = rcopy(blk, blk, 3 + k, sibling)
            fw.start()
            copies.append(fw)
        for k, (px, py) in enumerate(chips):
            blk = out_ref.at[2 * px + py, other_half]
            rcopy(blk, blk, 3 + k, sibling).wait_recv()
        for cp in copies:
            cp.wait_send()
        mine.wait()

    return pl.pallas_call(
        body, out_shape=jax.ShapeDtypeStruct((4, r, width), w.dtype), in_specs=[_ANY], out_specs=_ANY,
        scratch_shapes=[pltpu.SemaphoreType.DMA((6,)), pltpu.SemaphoreType.DMA((6,)), pltpu.SemaphoreType.DMA],
        name="weights_all_gather",
    )(w)


def _rs_sibling_swap(g):
    _, r, width = g.shape
    rh = r // 2

    def body(g_ref, mine_ref, recv_ref, send_sem, recv_sem, local_sem):
        x, y, c = _place()
        loc = pltpu.make_async_copy(g_ref.at[pl.ds(0, 4), pl.ds(c * rh, rh)], mine_ref, local_sem)
        loc.start()
        cp = pltpu.make_async_remote_copy(src_ref=g_ref.at[pl.ds(0, 4), pl.ds((1 - c) * rh, rh)], dst_ref=recv_ref,
                                          send_sem=send_sem, recv_sem=recv_sem, device_id=(x, y, 1 - c),
                                          device_id_type=MESH_ID)
        cp.start()
        cp.wait()
        loc.wait()

    shp = jax.ShapeDtypeStruct((4, rh, width), g.dtype)
    return pl.pallas_call(
        body, out_shape=[shp, shp], in_specs=[_ANY], out_specs=[_ANY, _ANY],
        scratch_shapes=[pltpu.SemaphoreType.DMA, pltpu.SemaphoreType.DMA, pltpu.SemaphoreType.DMA],
        name="grads_sibling_swap",
    )(g)


def _rs_chip_exchange(p):
    _, rh, width = p.shape

    def body(p_ref, own_ref, recv_ref, send_sems, recv_sems, local_sem):
        x, y, c = _place()
        loc = pltpu.make_async_copy(p_ref.at[2 * x + y], own_ref, local_sem)
        loc.start()
        chips = _other_chips(x, y)
        copies = []
        for k, (px, py) in enumerate(chips):
            cp = pltpu.make_async_remote_copy(src_ref=p_ref.at[2 * px + py], dst_ref=recv_ref.at[k],
                                              send_sem=send_sems.at[k], recv_sem=recv_sems.at[k],
                                              device_id=(px, py, c), device_id_type=MESH_ID)
            cp.start()
            copies.append(cp)
        for cp in copies:
            cp.wait_recv()
        for cp in copies:
            cp.wait_send()
        loc.wait()

    return pl.pallas_call(
        body, out_shape=[jax.ShapeDtypeStruct((rh, width), p.dtype), jax.ShapeDtypeStruct((3, rh, width), p.dtype)],
        in_specs=[_ANY], out_specs=[_ANY, _ANY],
        scratch_shapes=[pltpu.SemaphoreType.DMA((3,)), pltpu.SemaphoreType.DMA((3,)), pltpu.SemaphoreType.DMA],
        name="grads_chip_exchange",
    )(p)


def _rs_sibling_join(q):
    rh, width = q.shape

    def body(q_ref, out_ref, send_sem, recv_sem, local_sem):
        x, y, c = _place()
        my_half = pl.ds(c * rh, rh)
        loc = pltpu.make_async_copy(q_ref, out_ref.at[my_half], local_sem)
        loc.start()
        pltpu.make_async_remote_copy(src_ref=q_ref, dst_ref=out_ref.at[my_half], send_sem=send_sem, recv_sem=recv_sem,
                                     device_id=(x, y, 1 - c), device_id_type=MESH_ID).start()
        wait = pltpu.make_async_remote_copy(src_ref=q_ref, dst_ref=out_ref.at[pl.ds((1 - c) * rh, rh)],
                                            send_sem=send_sem, recv_sem=recv_sem, device_id=(x, y, 1 - c),
                                            device_id_type=MESH_ID)
        wait.wait_recv()
        wait.wait_send()
        loc.wait()

    return pl.pallas_call(
        body, out_shape=jax.ShapeDtypeStruct((2 * rh, width), q.dtype), in_specs=[_ANY], out_specs=_ANY,
        scratch_shapes=[pltpu.SemaphoreType.DMA, pltpu.SemaphoreType.DMA, pltpu.SemaphoreType.DMA],
        name="grads_sibling_join",
    )(q)


def _sum_devices(gathered, head_row):
    _, rows, width = gathered.shape

    def body(g_ref, out_ref, head_ref):
        acc = g_ref[0]
        for d in range(1, 8):
            acc = acc + g_ref[d]
        out_ref[...] = acc
        row = acc[head_row:head_row + 1, :]
        hs = row[:, 0:HEAD_DIM]
        for h in range(1, HEADS):
            hs = hs + row[:, h * HEAD_DIM:(h + 1) * HEAD_DIM]
        head_ref[...] = jnp.zeros_like(head_ref)
        head_ref[0:1, 0:HEAD_DIM] = hs

    return pl.pallas_call(
        body, out_shape=[jax.ShapeDtypeStruct((rows, width), f32), jax.ShapeDtypeStruct((8, LANES), f32)],
        name="small_grads_sum",
    )(gathered)


def _pack(arrs, rows_mult, dtype):
    flat = jnp.concatenate([a.reshape(-1).astype(dtype) for a in arrs])
    per = PACK_W * rows_mult
    total = -(-flat.shape[0] // per) * per
    return jnp.pad(flat, (0, total - flat.shape[0])).reshape(total // PACK_W, PACK_W)


def _unpack(buf, shapes):
    flat = buf.reshape(-1)
    out, off = [], 0
    for s in shapes:
        n = int(np.prod(s))
        out.append(flat[off:off + n].reshape(s))
        off += n
    return out


_BIG = ["w_in", "conv_gdn", "w_branch_gdn", "w2", "a2", "g2", "w_branch_rwkv", "w_out", "w_ffn_in", "conv_ffn",
        "w_ffn_out"]
_SMALL = ["b_ada", "norm1_w", "a_log", "dt_bias", "onorm_gdn", "mu_rwkv", "w0", "a0", "k_k", "k_a", "r_k", "lnx_w",
          "lnx_b", "norm2_w", "norm_f_w"]
_ORDER = ["w_ada", "b_ada", "norm1_w", "w_in", "conv_gdn", "a_log", "dt_bias", "onorm_gdn", "w_branch_gdn", "mu_rwkv",
          "w0", "w2", "a0", "a2", "g2", "k_k", "k_a", "r_k", "lnx_w", "lnx_b", "w_branch_rwkv", "w_out", "norm2_w",
          "w_ffn_in", "conv_ffn", "w_ffn_out", "norm_f_w"]


def _win_pad(w):
    z = lambda n: jnp.zeros((w.shape[0], n), w.dtype)
    return jnp.concatenate([w[:, 0:2048], w[:, 2064:3728], w[:, 2048:2064], z(112), w[:, 3728:3888], z(96),
                            w[:, 3888:5936]], axis=1)


def _win_unpad(g):
    return jnp.concatenate([g[:, 0:2048], g[:, 3712:3728], g[:, 2048:3712], g[:, 3840:4000], g[:, 4096:6144]], axis=1)


def kernel(x, c, w_ada, b_ada, norm1_w, w_in, conv_gdn, a_log, dt_bias, onorm_gdn, w_branch_gdn, mu_rwkv, w0, w2, a0, a2, g2, k_k, k_a, r_k, lnx_w, lnx_b, w_branch_rwkv, w_out, norm2_w, w_ffn_in, conv_ffn, w_ffn_out, norm_f_w, loss_target, m_w_ada, m_b_ada, m_norm1_w, m_w_in, m_conv_gdn, m_a_log, m_dt_bias, m_onorm_gdn, m_w_branch_gdn, m_mu_rwkv, m_w0, m_w2, m_a0, m_a2, m_g2, m_k_k, m_k_a, m_r_k, m_lnx_w, m_lnx_b, m_w_branch_rwkv, m_w_out, m_norm2_w, m_w_ffn_in, m_conv_ffn, m_w_ffn_out, m_norm_f_w, v_w_ada, v_b_ada, v_norm1_w, v_w_in, v_conv_gdn, v_a_log, v_dt_bias, v_onorm_gdn, v_w_branch_gdn, v_mu_rwkv, v_w0, v_w2, v_a0, v_a2, v_g2, v_k_k, v_k_a, v_r_k, v_lnx_w, v_lnx_b, v_w_branch_rwkv, v_w_out, v_norm2_w, v_w_ffn_in, v_conv_ffn, v_w_ffn_out, v_norm_f_w):
    args = dict(locals())
    W = {n: args[n] for n in _ORDER}
    Mo = {n: args["m_" + n] for n in _ORDER}
    Vo = {n: args["v_" + n] for n in _ORDER}
    shapes = {n: W[n].shape for n in _ORDER}
    sq = lambda a: a.reshape(a.shape[-2:]) if a.ndim == 3 else a.reshape(1, -1)
    row = lambda a: a.reshape(1, -1)

    xi, yi, ci = lax.axis_index("x"), lax.axis_index("y"), lax.axis_index("c")
    dev = 4 * xi + 2 * yi + ci
    chip = 2 * xi + yi

    x2 = x[0]
    tgt = loss_target[0]
    T, D = x2.shape
    N = T // CHUNK
    tt_l = _pick(T, 256, CHUNK)
    tt_h = _pick(T, 128, CHUNK)

    small_blk = _pack([c, W["conv_gdn"], W["conv_ffn"]], 8, f32)
    small_all = _ag8("gather_c_conv", small_blk)
    c_all = small_all[:, 0, :]
    per_chip = small_all[0::2].reshape(4, -1)
    ncg = W["conv_gdn"].size
    conv_gdn_f = jnp.concatenate([per_chip[j, D:D + ncg].reshape(4, 384) for j in range(4)], axis=1)
    ncf = W["conv_ffn"].size
    conv_ffn_f = jnp.concatenate([per_chip[j, D + ncg:D + ncg + ncf].reshape(3, 704) for j in range(4)], axis=1)

    big_shapes = [shapes[n][1:] for n in _BIG]
    wq = _pack([W[n] for n in _BIG], 32, bf16)
    R = wq.shape[0]
    wq_all = _ag4_big(wq)
    parts = [dict(zip(_BIG, _unpack(wq_all[j], big_shapes))) for j in range(4)]
    cat = lambda n, ax: jnp.concatenate([parts[j][n] for j in range(4)], axis=ax)
    win_p = _win_pad(cat("w_in", 1))
    w_bg = cat("w_branch_gdn", 1)
    w_br = cat("w_branch_rwkv", 1)
    w2f, a2f, g2f = cat("w2", 1), cat("a2", 1), cat("g2", 1)
    wout_f = cat("w_out", 0)
    wfi = cat("w_ffn_in", 1)
    wfo = cat("w_ffn_out", 0)
    zpad = lambda a, top, bot: jnp.pad(a, ((top, bot), (0, 0)))
    w2p, a2p, g2p = zpad(w2f, 0, 64), zpad(a2f, 64, 0), zpad(g2f, 0, 96)

    ncol = shapes["w_ada"][2]
    b_cols = lax.dynamic_slice(sq(W["b_ada"]), (0, chip * ncol), (1, ncol))
    cond16, mod_cols = _ada_fwd(jnp.pad(c_all, ((0, 8), (0, 0))), sq(W["w_ada"]), b_cols)
    mod_all = _ag8("gather_mod", mod_cols[:8])
    mod_mine = lax.dynamic_slice(mod_all[0::2], (0, dev, 0), (4, 1, ncol)).reshape(1, 4 * ncol)
    shift1, scale1, gate1, shift2, scale2, gate2 = [mod_mine[:, i * D:(i + 1) * D] for i in range(6)]

    seg = _seg_matrix(WIDTH, HEAD_DIM)
    norm1 = [sq(W["norm1_w"]), shift1, scale1]
    h1 = _stage_fwd("norm_mod1", _fn_norm_mod, [_whole(x2)], norm1, [(D, bf16)], tt_l)[0]
    p = _matmul("in_proj", h1, win_p, "nn")
    qkv_s = [p[:, 0:1536]] + [_shift_down(p[:, 0:1536], s) for s in (1, 2, 3)]
    p_prev = _shift_down(p[:, 2048:4096], 1)

    cgq = [row(conv_gdn_f[j, part * WIDTH:(part + 1) * WIDTH]) for part in range(3) for j in range(4)]
    lane_pad = lambda a: jnp.pad(row(a), ((0, 0), (8, LANES - 16)))
    gdn_pre_ps = cgq + [lane_pad(W["a_log"]), lane_pad(W["dt_bias"]), seg, _chunk_tri(tt_h, CHUNK)]
    gdn_pre_ts = [(qkv_s[s], WIDTH, part) for part in range(3) for s in range(4)] + [(p, LANES, 29)]
    q_, k_, v_, beta_t, gc_t = _stage_fwd("gdn_pre", _fn_gdn_pre, gdn_pre_ts, gdn_pre_ps,
                                          [(WIDTH, f32)] * 3 + [(LANES, f32)] * 2, tt_h)
    heads_col = lambda a, lo: a[:, lo:lo + HEADS].reshape(N, CHUNK, HEADS).transpose(0, 2, 1)
    beta_c = heads_col(beta_t, 0)[..., None]
    gch = heads_col(gc_t, 8)
    gdn_ins = [_to_chunks(q_, N), _to_chunks(k_, N), _to_chunks(v_, N), beta_c, gch[..., None], gch[:, :, None, :],
               gch[:, :, CHUNK - 1:CHUNK, None]]
    o_ch, gdn_hist = _scan_fwd("gdn_scan", _gdn_chunk, gdn_ins, HEAD_DIM)
    o_ = _from_chunks(o_ch)
    ow512 = jnp.tile(row(W["onorm_gdn"]), (1, HEADS))
    gdn_post_ts = [_whole(o_), (p, WIDTH, 3)]
    ya = _stage_fwd("gdn_post", _fn_gdn_post, gdn_post_ts, [ow512, seg], [(WIDTH, bf16)], tt_l)[0]

    mu = sq(W["mu_rwkv"])
    rw_ps = [mu[:, 0:512], mu[:, 512:1024], mu[:, 1024:1536], mu[:, 1536:1664], jnp.pad(mu[:, 1664:1824], ((0, 0), (0, 96))),
             sq(W["w0"]), w2p, sq(W["a0"]), a2p, g2p, sq(W["k_k"]), sq(W["k_a"]), seg]
    rw_ts = [(p, WIDTH, 4), (p, WIDTH, 5), (p, WIDTH, 6), (p, LANES, 28), (p, 256, 15),
             (p_prev, WIDTH, 0), (p_prev, WIDTH, 1), (p_prev, WIDTH, 2), (p_prev, LANES, 12), (p_prev, 256, 7)]
    rw_out = _stage_fwd("rwkv_pre", _fn_rwkv_pre, rw_ts, rw_ps, [(WIDTH, f32)] * 7, tt_h)
    r_, lw_, k2_, vv_, na_, b_, g_ = rw_out
    rw_ins = [_to_chunks(a, N) for a in (r_, lw_, k2_, vv_, na_, b_)]
    y_ch, rw_hist = _scan_fwd("rwkv_scan", _rwkv_chunk, rw_ins, HEAD_DIM)
    y_ = _from_chunks(y_ch)
    rwp_ps = [sq(W["lnx_w"]), sq(W["lnx_b"]), row(W["r_k"]), seg]
    rwp_ts = [_whole(y_), _whole(r_), _whole(k2_), _whole(vv_), _whole(g_)]
    yb = _stage_fwd("rwkv_post", _fn_rwkv_post, rwp_ts, rwp_ps, [(WIDTH, bf16)], tt_l)[0]

    big_a = _matmul("branch_gdn", ya, w_bg, "nn")
    big_b = _matmul("branch_rwkv", yb, w_br, "nn")
    merge_ts = [(p, D, 4), (p, D, 5), _whole(big_a), _whole(big_b)]
    merged = _stage_fwd("merge", _fn_merge, merge_ts, [], [(D, bf16)], tt_l)[0]
    mo = _matmul("out_proj", merged, wout_f, "nn")
    norm2 = [gate1, sq(W["norm2_w"]), shift2, scale2]
    x1, h2 = _stage_fwd("resid_norm_mod2", _fn_resid_norm_mod, [_whole(x2), _whole(mo)], norm2, [(D, f32), (D, bf16)], tt_l)
    f = _matmul("ffn_in", h2, wfi, "nn")
    gate_s = [f[:, 0:D_FF], _shift_down(f[:, 0:D_FF], 1), _shift_down(f[:, 0:D_FF], 2)]
    up = f[:, D_FF:]
    cg_ps = [row(conv_ffn_f[j]) for j in range(3)]
    cg_ts = [_whole(a) for a in gate_s] + [_whole(up)]
    act = _stage_fwd("convglu", _fn_convglu, cg_ts, cg_ps, [(D_FF, bf16)], tt_h)[0]
    fo = _matmul("ffn_out", act, wfo, "nn")

    dx1_a, dfo, dgate2, dnormf, loss_part = _final_stage(x1, fo, tgt, gate2, row(W["norm_f_w"]), tt_l)

    dact = _matmul("d_act", dfo, wfo, "nt")
    g_wfo = _matmul("g_ffn_out", act, dfo, "tn")
    (dg0, dg1, dg2, dup), dcf = _stage_bwd("convglu_bwd", _fn_convglu, cg_ts, cg_ps, [[_whole(dact)]], tt_h,
                                           [True] * 4, [True] * 3)
    dgate = _add_n("d_gate", [dg0, _shift_up(dg1, 1), _shift_up(dg2, 2)], tt_l)
    df = jnp.concatenate([dgate, dup], axis=1)
    dh2 = _matmul("d_h2", df, wfi, "nt")
    g_wfi = _matmul("g_ffn_in", h2, df, "tn")
    (dx_a, dmo), (dgate1, dnorm2, dshift2, dscale2) = _stage_bwd(
        "resid_norm_mod2_bwd", _fn_resid_norm_mod, [_whole(x2), _whole(mo)], norm2,
        [[_whole(dx1_a)], [_whole(dh2)]], tt_l, [True, True], [True] * 4)
    dmerged = _matmul("d_merged", dmo, wout_f, "nt")
    g_wout = _matmul("g_out_proj", merged, dmo, "tn")
    (dgla, dglb, dbig_a, dbig_b), _ = _stage_bwd("merge_bwd", _fn_merge, merge_ts, [], [[_whole(dmerged)]], tt_l,
                                                 [True] * 4, [])
    dya = _matmul("d_ya", dbig_a, w_bg, "nt")
    g_wbg = _matmul("g_branch_gdn", ya, dbig_a, "tn")
    dyb = _matmul("d_yb", dbig_b, w_br, "nt")
    g_wbr = _matmul("g_branch_rwkv", yb, dbig_b, "tn")

    (dy_, dr_p, dk2_p, dv_p, dg_p), (dlnxw, dlnxb, drk) = _stage_bwd(
        "rwkv_post_bwd", _fn_rwkv_post, rwp_ts, rwp_ps, [[_whole(dyb)]], tt_l, [True] * 5, [True, True, True, False])
    d_rw = _scan_bwd("rwkv_scan_bwd", _rwkv_chunk, rw_ins, rw_hist, _to_chunks(dy_, N))
    dr_c, dlw_c, dk2_c, dv_c, dna_c, db_c = [_from_chunks(a) for a in d_rw]
    rw_cots = [[_whole(dr_p), _whole(dr_c)], [_whole(dlw_c)], [_whole(dk2_p), _whole(dk2_c)],
               [_whole(dv_p), _whole(dv_c)], [_whole(dna_c)], [_whole(db_c)], [_whole(dg_p)]]
    rw_dt, rw_dp = _stage_bwd("rwkv_pre_bwd", _fn_rwkv_pre, rw_ts, rw_ps, rw_cots, tt_h, [True] * 10,
                              [True] * 12 + [False])
    dmu_r, dmu_k, dmu_v, dmu_l, dmu_g, dw0, dw2p, da0, da2p, dg2p, dkk, dka = rw_dp
    d_rkv = [_add_n("d_p_rwkv%d" % i, [rw_dt[i], _shift_up(rw_dt[5 + i], 1)], tt_l) for i in range(5)]

    (do_, dz), (dow512,) = _stage_bwd("gdn_post_bwd", _fn_gdn_post, gdn_post_ts, [ow512, seg], [[_whole(dya)]], tt_l,
                                      [True, True], [True, False])
    d_gdn = _scan_bwd("gdn_scan_bwd", _gdn_chunk, gdn_ins, gdn_hist, _to_chunks(do_, N))
    dq_c, dk_c, dv_c2, dbeta_c, dgc_col, dgc_row, dgl = d_gdn
    back = lambda a, lo: jnp.pad(a.transpose(0, 2, 1).reshape(T, HEADS), ((0, 0), (lo, LANES - HEADS - lo)))
    dbeta_t = back(dbeta_c[..., 0], 0)
    dgc_1 = back(dgc_col[..., 0], 8)
    dgc_2 = back(dgc_row[:, :, 0, :], 8)
    dgc_3 = back(jnp.pad(dgl[:, :, 0, :], ((0, 0), (0, 0), (CHUNK - 1, 0))), 8)
    gdn_cots = [[_whole(_from_chunks(dq_c))], [_whole(_from_chunks(dk_c))], [_whole(_from_chunks(dv_c2))],
                [_whole(dbeta_t)], [_whole(dgc_1), _whole(dgc_2), _whole(dgc_3)]]
    gdn_dt, gdn_dp = _stage_bwd("gdn_pre_bwd", _fn_gdn_pre, gdn_pre_ts, gdn_pre_ps, gdn_cots, tt_h, [True] * 13,
                                [True] * 14 + [False, False])
    d_qkv = [_add_n("d_p_qkv%d" % part, [gdn_dt[4 * part]] + [_shift_up(gdn_dt[4 * part + s], s) for s in (1, 2, 3)], tt_l)
             for part in range(3)]
    dba = gdn_dt[12]

    dp = jnp.concatenate(d_qkv + [dz] + d_rkv[0:4] + [dba, d_rkv[4], dgla, dglb], axis=1)
    dh1 = _matmul("d_h1", dp, win_p, "nt")
    g_win = _win_unpad(_matmul("g_in_proj", h1, dp, "tn"))
    (dx_b,), (dnorm1, dshift1, dscale1) = _stage_bwd("norm_mod1_bwd", _fn_norm_mod, [_whole(x2)], norm1,
                                                     [[_whole(dh1)]], tt_l, [True], [True] * 3)
    grad_x = _add_n("grad_x", [dx_a, dx_b], tt_l)

    dmod = jnp.concatenate([dshift1, dscale1, dgate1, dshift2, dscale2, dgate2], axis=1)
    g_conv_gdn = jnp.concatenate([jnp.concatenate([gdn_dp[4 * part + j] for part in range(3)], axis=1) for j in range(4)], axis=0)
    g_conv_ffn = jnp.concatenate(dcf, axis=0)
    g_mu = jnp.concatenate([dmu_r, dmu_k, dmu_v, dmu_l, dmu_g[:, :160]], axis=1)
    small_parts = {"b_ada": dmod, "norm1_w": dnorm1, "a_log": gdn_dp[12][:, 8:16], "dt_bias": gdn_dp[13][:, 8:16],
                   "mu_rwkv": g_mu, "w0": dw0, "a0": da0, "k_k": dkk, "k_a": dka, "r_k": drk, "lnx_w": dlnxw,
                   "lnx_b": dlnxb, "norm2_w": dnorm2, "norm_f_w": dnormf}
    small_names = [n for n in _SMALL if n != "onorm_gdn"]
    body_rows = _pack([small_parts[n] for n in small_names] + [loss_part[:, 0:1]], 1, f32)
    head_row = body_rows.shape[0]
    small_g = jnp.concatenate([body_rows, jnp.pad(dow512, ((0, 0), (0, PACK_W - WIDTH)))], axis=0)
    small_g = jnp.pad(small_g, ((0, -small_g.shape[0] % 8), (0, 0)))
    small_all_g = _ag8("gather_small_grads", small_g)
    small_sum, head_sum = _sum_devices(small_all_g, head_row)
    small_shapes = [shapes[n][1:] if n != "norm_f_w" else shapes[n] for n in small_names]
    small_grads = dict(zip(small_names, _unpack(small_sum, small_shapes + [(1,)])))
    loss = _unpack(small_sum, small_shapes + [(1,)])[-1].reshape(())
    small_grads["onorm_gdn"] = head_sum[0, 0:HEAD_DIM]

    dmod_all = small_all_g[:, 0:6, :].reshape(8, 6 * PACK_W)
    dmod_cols = lax.dynamic_slice(dmod_all, (0, chip * ncol), (8, ncol))
    g_wada = _matmul("g_w_ada", cond16, jnp.pad(dmod_cols, ((0, 8), (0, 0))), "tn")

    g2_full = dg2p[0:160]
    full = {"w_in": g_win, "conv_gdn": g_conv_gdn, "w_branch_gdn": g_wbg, "w2": dw2p[0:64], "a2": da2p[64:128],
            "g2": g2_full, "w_branch_rwkv": g_wbr, "w_out": g_wout, "w_ffn_in": g_wfi, "conv_ffn": g_conv_ffn,
            "w_ffn_out": g_wfo}
    row_sharded = ("w_out", "w_ffn_out")

    def shard_of(n, j):
        s = shapes[n][1:]
        if n in row_sharded:
            return full[n][j * s[0]:(j + 1) * s[0], :]
        return full[n][:, j * s[1]:(j + 1) * s[1]]

    g_packed = jnp.stack([_pack([shard_of(n, j) for n in _BIG], 32, f32) for j in range(4)])
    mine_h, sib_h = _rs_sibling_swap(g_packed)
    rh = R // 2
    tt_p = _pick(4 * rh, 1024, 8)
    pair = _add_n("grads_pair_sum", [mine_h.reshape(4 * rh, PACK_W), sib_h.reshape(4 * rh, PACK_W)], tt_p)
    own, others = _rs_chip_exchange(pair.reshape(4, rh, PACK_W))
    tt_q = _pick(rh, 1024, 8)
    half = _add_n("grads_chip_sum", [own, others[0], others[1], others[2]], tt_q)
    g_shard = _rs_sibling_join(half)

    pk_big = lambda d: _pack([d[n] for n in _BIG], 32, f32)
    big_delta, big_m, big_v = _adamw("adamw_big", pk_big(W), g_shard, pk_big(Mo), pk_big(Vo))
    pk_small = lambda d: _pack([d[n] for n in _SMALL], 8, f32)
    sg = pk_small(small_grads)
    sm_delta, sm_m, sm_v = _adamw("adamw_small", pk_small(W), sg, pk_small(Mo), pk_small(Vo))
    ada_delta, ada_m, ada_v = _adamw("adamw_w_ada", sq(W["w_ada"]), g_wada, sq(Mo["w_ada"]), sq(Vo["w_ada"]))

    res = {}
    for tag, bbuf, sbuf, abuf in (("grad", g_shard, sg, g_wada), ("delta", big_delta, sm_delta, ada_delta),
                                  ("new_m", big_m, sm_m, ada_m), ("new_v", big_v, sm_v, ada_v)):
        res[tag] = dict(zip(_BIG, _unpack(bbuf, [shapes[n] for n in _BIG])))
        res[tag].update(zip(_SMALL, _unpack(sbuf, [shapes[n] for n in _SMALL])))
        res[tag]["w_ada"] = abuf.reshape(shapes["w_ada"])
    outs = [loss, grad_x.reshape(x.shape)]
    for tag in ("grad", "delta", "new_m", "new_v"):
        outs += [res[tag][n] for n in _ORDER]
    return tuple(outs)
```

```python
import functools
import math

import numpy as np
import jax
import jax.numpy as jnp
from jax import lax
from jax.experimental import pallas as pl
from jax.experimental.pallas import tpu as pltpu

f32 = jnp.float32
bf16 = jnp.bfloat16

LANES = 128
HEADS = 8
HEAD_DIM = 64
WIDTH = HEADS * HEAD_DIM
CHUNK = 64
D_FF = 2816
NORM_EPS = 1e-6
LNX_EPS = 64e-5
PACK_W = 1024
MESH_ID = pl.DeviceIdType.MESH

ADAM_LR, ADAM_B1, ADAM_B2, ADAM_EPS, ADAM_WD, ADAM_STEP = 0.001, 0.9, 0.999, 1e-08, 0.01, 10


def _pick(n, target, mult):
    if n <= target:
        return n
    best = None
    for t in range(mult, target + 1, mult):
        if n % t == 0:
            best = t
    assert best is not None, (n, target, mult)
    return best


def _split_bf16(x, n):
    parts, r = [], x
    for i in range(n):
        p = r.astype(bf16)
        parts.append(p)
        if i + 1 < n:
            r = r - p.astype(f32)
    return parts


def _xdot_r_impl(x, m, n, dims):
    acc = None
    for p in _split_bf16(x, n):
        t = lax.dot_general(p, m, dims, preferred_element_type=f32)
        acc = t if acc is None else acc + t
    return acc


def _make_xdot_r(n):
    nn = (((1,), (0,)), ((), ()))
    nt = (((1,), (1,)), ((), ()))

    @jax.custom_vjp
    def xdot(x, m):
        return _xdot_r_impl(x, m, n, nn)

    def fwd(x, m):
        return _xdot_r_impl(x, m, n, nn), m

    def bwd(m, ct):
        return _xdot_r_impl(ct, m, n, nt), jnp.zeros_like(m)

    xdot.defvjp(fwd, bwd)
    return xdot


_segsum = _make_xdot_r(2)


def _xdot_l_impl(m, x, n, dims):
    acc = None
    for p in _split_bf16(x, n):
        t = lax.dot_general(m, p, dims, preferred_element_type=f32)
        acc = t if acc is None else acc + t
    return acc


@jax.custom_vjp
def _xdot_l(m, x):
    return _xdot_l_impl(m, x, 3, (((1,), (0,)), ((), ())))


def _xdot_l_fwd(m, x):
    return _xdot_l(m, x), m


def _xdot_l_bwd(m, ct):
    return jnp.zeros_like(m), _xdot_l_impl(m, ct, 3, (((0,), (0,)), ((), ())))


_xdot_l.defvjp(_xdot_l_fwd, _xdot_l_bwd)


@jax.custom_vjp
def _bdot(x, w):
    return jnp.dot(x.astype(bf16), w.astype(bf16), preferred_element_type=f32)


def _bdot_fwd(x, w):
    return _bdot(x, w), (x, w)


def _bdot_bwd(res, ct):
    x, w = res
    c = ct.astype(bf16)
    dx = lax.dot_general(c, w.astype(bf16), (((1,), (1,)), ((), ())), preferred_element_type=f32)
    dw = lax.dot_general(x.astype(bf16), c, (((0,), (0,)), ((), ())), preferred_element_type=f32)
    return dx, dw


_bdot.defvjp(_bdot_fwd, _bdot_bwd)


def _silu(x):
    return x * jax.nn.sigmoid(x)


def _softplus(x):
    return jnp.maximum(x, 0.0) + jnp.log(1.0 + jnp.exp(-jnp.abs(x)))


def _rms(x, w, eps):
    return x * lax.rsqrt(jnp.mean(x * x, axis=-1, keepdims=True) + eps) * w


def _seg_matrix(width, seg):
    i = np.arange(width)
    return jnp.asarray((i[:, None] // seg) == (i[None, :] // seg), dtype=bf16)


def _chunk_tri(rows, chunk):
    i = np.arange(rows)
    return jnp.asarray(((i[:, None] // chunk) == (i[None, :] // chunk)) & (i[:, None] >= i[None, :]), dtype=bf16)


def _shift_down(a, s):
    return jnp.pad(a, ((s, 0), (0, 0)))[: a.shape[0]]


def _shift_up(a, s):
    return jnp.pad(a, ((0, s), (0, 0)))[s:]


def _tile_spec(tt, w, cidx):
    return pl.BlockSpec((tt, w), lambda i: (i, cidx))


def _full_spec(shape):
    nd = len(shape)
    return pl.BlockSpec(shape, lambda i: (0,) * nd)


def _stage_fwd(name, fn, tiles, params, outs, tt):
    rows = tiles[0][0].shape[0]
    nt, npar = len(tiles), len(params)

    def body(*refs):
        ts = [r[...].astype(f32) for r in refs[:nt]]
        ps = [r[...] for r in refs[nt:nt + npar]]
        res = fn(ps, ts)
        for r, v in zip(refs[nt + npar:], res):
            r[...] = v.astype(r.dtype)

    return pl.pallas_call(
        body, grid=(rows // tt,),
        in_specs=[_tile_spec(tt, w, ci) for (_, w, ci) in tiles] + [_full_spec(p.shape) for p in params],
        out_specs=[_tile_spec(tt, w, 0) for (w, _) in outs],
        out_shape=[jax.ShapeDtypeStruct((rows, w), dt) for (w, dt) in outs],
        compiler_params=pltpu.CompilerParams(dimension_semantics=("parallel",)),
        name=name,
    )(*[t[0] for t in tiles], *params)


def _stage_bwd(name, fn, tiles, params, cots, tt, tile_grad, param_grad):
    rows = tiles[0][0].shape[0]
    nt, npar = len(tiles), len(params)
    flat_cots = [c for group in cots for c in group]
    groups = [len(g) for g in cots]
    ncot = len(flat_cots)
    dt_w = [w for (_, w, _), g in zip(tiles, tile_grad) if g]
    dp_shapes = [p.shape for p, g in zip(params, param_grad) if g]
    ndt = len(dt_w)

    def body(*refs):
        i = pl.program_id(0)
        t_refs = refs[:nt]
        p_refs = refs[nt:nt + npar]
        c_refs = refs[nt + npar:nt + npar + ncot]
        dt_refs = refs[nt + npar + ncot:nt + npar + ncot + ndt]
        dp_refs = refs[nt + npar + ncot + ndt:]
        ts = [r[...].astype(f32) for r in t_refs]
        ps = [r[...] for r in p_refs]

        def f(dp, dt):
            dp, dt = iter(dp), iter(dt)
            pp = [next(dp) if g else p for p, g in zip(ps, param_grad)]
            tl = [next(dt) if g else t for t, g in zip(ts, tile_grad)]
            return fn(pp, tl)

        _, vjp = jax.vjp(f, [p for p, g in zip(ps, param_grad) if g], [t for t, g in zip(ts, tile_grad) if g])
        cs, j = [], 0
        for n in groups:
            acc = c_refs[j][...].astype(f32)
            for q in range(1, n):
                acc = acc + c_refs[j + q][...].astype(f32)
            cs.append(acc)
            j += n
        gp, gt = vjp(cs)
        for r, v in zip(dt_refs, gt):
            r[...] = v

        @pl.when(i == 0)
        def _():
            for r in dp_refs:
                r[...] = jnp.zeros_like(r)

        for r, v in zip(dp_refs, gp):
            r[...] += v

    res = pl.pallas_call(
        body, grid=(rows // tt,),
        in_specs=[_tile_spec(tt, w, ci) for (_, w, ci) in tiles] + [_full_spec(p.shape) for p in params]
        + [_tile_spec(tt, w, ci) for (_, w, ci) in flat_cots],
        out_specs=[_tile_spec(tt, w, 0) for w in dt_w] + [_full_spec(s) for s in dp_shapes],
        out_shape=[jax.ShapeDtypeStruct((rows, w), f32) for w in dt_w] + [jax.ShapeDtypeStruct(s, f32) for s in dp_shapes],
        compiler_params=pltpu.CompilerParams(dimension_semantics=("arbitrary",)),
        name=name,
    )(*[t[0] for t in tiles], *params, *[c[0] for c in flat_cots])
    return list(res[:ndt]), list(res[ndt:])


def _whole(a):
    return (a, a.shape[1], 0)


def _matmul(name, a, b, mode, out_dtype=f32, tm=1024, tn=1024, tk=1024, shards=1):
    S = shards
    if mode == "nn":
        M, K = a.shape
        w = b.shape[-1]
    elif mode == "nt":
        M = a.shape[0]
        if S > 1:
            _, N, w = b.shape
            K = S * w
        else:
            N, K = b.shape
            w = K
    else:
        K, M = a.shape
        w = b.shape[1] // S
    if mode != "nt":
        N = S * w
    tm = _pick(M, tm, LANES)
    if mode == "nt":
        tn = _pick(N, tn, LANES)
        tk = _pick(w, tk, LANES)
    else:
        tn = _pick(w, tn, LANES)
        tk = _pick(K, tk, LANES if mode == "nn" else 16)
    nk = K // tk
    nb = w // (tk if mode == "nt" else tn)
    if mode == "nn":
        a_spec = pl.BlockSpec((tm, tk), lambda i, j, k: (i, k))
        if S > 1:
            b_spec = pl.BlockSpec((1, tk, tn), lambda i, j, k: (j // nb, k, j % nb))
        else:
            b_spec = pl.BlockSpec((tk, tn), lambda i, j, k: (k, j))
        dims = (((1,), (0,)), ((), ()))
    elif mode == "nt":
        a_spec = pl.BlockSpec((tm, tk), lambda i, j, k: (i, k))
        if S > 1:
            b_spec = pl.BlockSpec((1, tn, tk), lambda i, j, k: (k // nb, j, k % nb))
        else:
            b_spec = pl.BlockSpec((tn, tk), lambda i, j, k: (j, k))
        dims = (((1,), (1,)), ((), ()))
    else:
        a_spec = pl.BlockSpec((tk, tm), lambda i, j, k: (k, i))
        b_spec = pl.BlockSpec((tk, tn), lambda i, j, k: (k, j))
        dims = (((0,), (0,)), ((), ()))
    if mode == "tn" and S > 1:
        o_spec = pl.BlockSpec((1, tm, tn), lambda i, j, k: (j // nb, i, j % nb))
        o_shape = (S, M, w)
    else:
        o_spec = pl.BlockSpec((tm, tn), lambda i, j, k: (i, j))
        o_shape = (M, N)
    b_lead = S > 1 and mode != "tn"
    o_lead = S > 1 and mode == "tn"

    def body(a_ref, b_ref, o_ref, acc_ref):
        k = pl.program_id(2)

        @pl.when(k == 0)
        def _():
            acc_ref[...] = jnp.zeros_like(acc_ref)

        bv = b_ref[0] if b_lead else b_ref[...]
        acc_ref[...] += lax.dot_general(a_ref[...].astype(bf16), bv.astype(bf16), dims, preferred_element_type=f32)

        @pl.when(k == nk - 1)
        def _():
            if o_lead:
                o_ref[0] = acc_ref[...].astype(o_ref.dtype)
            else:
                o_ref[...] = acc_ref[...].astype(o_ref.dtype)

    return pl.pallas_call(
        body, grid=(M // tm, N // tn, nk),
        in_specs=[a_spec, b_spec],
        out_specs=o_spec,
        out_shape=jax.ShapeDtypeStruct(o_shape, out_dtype),
        scratch_shapes=[pltpu.VMEM((tm, tn), f32)],
        compiler_params=pltpu.CompilerParams(dimension_semantics=("parallel", "parallel", "arbitrary")),
        name=name,
    )(a, b)


_SCAN_PRECISION = lax.Precision.HIGHEST


def _bmm(a, b):
    return jnp.einsum("hij,hjk->hik", a, b, preferred_element_type=f32, precision=_SCAN_PRECISION)


def _bmm_nt(a, b):
    return jnp.einsum("hik,hjk->hij", a, b, preferred_element_type=f32, precision=_SCAN_PRECISION)


def _bmm_tn(a, b):
    return jnp.einsum("hki,hkj->hij", a, b, preferred_element_type=f32, precision=_SCAN_PRECISION)


def _masks(n):
    r = lax.broadcasted_iota(jnp.int32, (n, n), 0)
    c = lax.broadcasted_iota(jnp.int32, (n, n), 1)
    return (r >= c)[None], (r > c)[None], (r == c)[None]


def _neumann_inverse(m, eye, n):
    p = eye + m
    mk, k = m, 1
    while 2 * k < n:
        mk = _bmm(mk, mk)
        p = p + _bmm(p, mk)
        k *= 2
    return p


def _gdn_chunk(s, q, k, v, beta, gc, gr, gl):
    n = q.shape[1]
    causal, strict, diag = _masks(n)
    eye = diag.astype(f32)
    decay = jnp.where(causal, jnp.exp(jnp.where(causal, gc - gr, 0.0)), 0.0)
    kb = k * beta
    vb = v * beta
    lower = jnp.where(strict, _bmm_nt(kb, k) * decay, 0.0)
    t_mat = _neumann_inverse(-lower, eye, n)
    egc = jnp.exp(gc)
    u = _bmm(t_mat, vb)
    w = _bmm(t_mat, kb * egc)
    attn = jnp.where(causal, _bmm_nt(q, k) * decay, 0.0)
    v_new = u - _bmm(w, s)
    o = _bmm(q * egc, s) + _bmm(attn, v_new)
    k_dec = k * jnp.exp(gl - gc)
    s_new = s * jnp.exp(gl) + _bmm_tn(k_dec, v_new)
    return s_new, o


def _rwkv_chunk(s, r, lw, k, v, a, b):
    n = r.shape[1]
    causal, strict, diag = _masks(n)
    eye = diag.astype(f32)
    tri = jnp.broadcast_to(causal.astype(f32), (r.shape[0], n, n))
    lc = _bmm(tri, lw)
    ein = jnp.exp(lc)
    eout = jnp.exp(-lc)
    a_t = a * jnp.exp(lc - lw)
    b_t = b * eout
    k_t = k * eout
    r_t = r * ein
    a_ab = jnp.where(strict, _bmm_nt(a_t, b_t), 0.0)
    a_ak = jnp.where(strict, _bmm_nt(a_t, k_t), 0.0)
    u = _bmm(_neumann_inverse(a_ab, eye, n), _bmm_nt(a_t, s) + _bmm(a_ak, v))
    y = (_bmm_nt(r_t, s) + _bmm(jnp.where(causal, _bmm_nt(r_t, b_t), 0.0), u)
         + _bmm(jnp.where(causal, _bmm_nt(r_t, k_t), 0.0), v))
    e_last = jnp.exp(jnp.sum(lw, axis=1, keepdims=True))
    s_new = s * e_last + _bmm_tn(u, b_t * e_last) + _bmm_tn(v, k_t * e_last)
    return s_new, y


def _chunk_spec(shape, n, reverse):
    blk = (1,) + tuple(shape[1:])
    if reverse:
        return pl.BlockSpec(blk, lambda i: (n - 1 - i, 0, 0, 0))
    return pl.BlockSpec(blk, lambda i: (i, 0, 0, 0))


def _scan_fwd(name, fn, ins, out_dim):
    n, h = ins[0].shape[0], ins[0].shape[1]
    c = ins[0].shape[2]
    nin = len(ins)

    def body(*refs):
        in_refs, o_ref, sh_ref, s_scr = refs[:nin], refs[nin], refs[nin + 1], refs[nin + 2]

        @pl.when(pl.program_id(0) == 0)
        def _():
            s_scr[...] = jnp.zeros_like(s_scr)

        s = s_scr[...]
        sh_ref[0] = s
        s_new, o = fn(s, *[r[0] for r in in_refs])
        o_ref[0] = o
        s_scr[...] = s_new

    oshape = (n, h, c, out_dim)
    sshape = (n, h, HEAD_DIM, HEAD_DIM)
    return pl.pallas_call(
        body, grid=(n,),
        in_specs=[_chunk_spec(a.shape, n, False) for a in ins],
        out_specs=[_chunk_spec(oshape, n, False), _chunk_spec(sshape, n, False)],
        out_shape=[jax.ShapeDtypeStruct(oshape, f32), jax.ShapeDtypeStruct(sshape, f32)],
        scratch_shapes=[pltpu.VMEM((h, HEAD_DIM, HEAD_DIM), f32)],
        compiler_params=pltpu.CompilerParams(dimension_semantics=("arbitrary",)),
        name=name,
    )(*ins)


def _scan_bwd(name, fn, ins, s_hist, d_out):
    n = ins[0].shape[0]
    nin = len(ins)

    def body(*refs):
        in_refs = refs[:nin]
        sh_ref, do_ref = refs[nin], refs[nin + 1]
        g_refs = refs[nin + 2:nin + 2 + nin]
        ds_scr = refs[nin + 2 + nin]

        @pl.when(pl.program_id(0) == 0)
        def _():
            ds_scr[...] = jnp.zeros_like(ds_scr)

        _, vjp = jax.vjp(fn, sh_ref[0], *[r[0] for r in in_refs])
        g = vjp((ds_scr[...], do_ref[0]))
        ds_scr[...] = g[0]
        for r, v in zip(g_refs, g[1:]):
            r[0] = v

    return pl.pallas_call(
        body, grid=(n,),
        in_specs=[_chunk_spec(a.shape, n, True) for a in ins]
        + [_chunk_spec(s_hist.shape, n, True), _chunk_spec(d_out.shape, n, True)],
        out_specs=[_chunk_spec(a.shape, n, True) for a in ins],
        out_shape=[jax.ShapeDtypeStruct(a.shape, f32) for a in ins],
        scratch_shapes=[pltpu.VMEM(s_hist.shape[1:], f32)],
        compiler_params=pltpu.CompilerParams(dimension_semantics=("arbitrary",)),
        name=name,
    )(*ins, s_hist, d_out)


def _to_chunks(a, n):
    return a.reshape(n, CHUNK, HEADS, a.shape[1] // HEADS).transpose(0, 2, 1, 3)


def _from_chunks(a):
    n, h, c, d = a.shape
    return a.transpose(0, 2, 1, 3).reshape(n * c, h * d)


def _fn_norm_mod(ps, ts):
    nw, shift, scale = ps
    (x,) = ts
    return [_rms(x, nw, NORM_EPS) * (1.0 + scale) + shift]


def _fn_resid_norm_mod(ps, ts):
    gate, nw, shift, scale = ps
    x, mo = ts
    x1 = x + gate * mo
    return [x1, _rms(x1, nw, NORM_EPS) * (1.0 + scale) + shift]


def _fn_gdn_pre(ps, ts):
    cw = ps[:12]
    alog, dtb, seg, tri = ps[12:]
    ba = ts[12]
    outs = []
    for part in range(3):
        x = ts[4 * part:4 * part + 4]
        w = cw[4 * part:4 * part + 4]
        conv = w[3] * x[0] + w[2] * x[1] + w[1] * x[2] + w[0] * x[3]
        u = _silu(conv)
        if part < 2:
            u = u * lax.rsqrt(_segsum(u * u, seg) + 1e-6)
            if part == 0:
                u = u * (HEAD_DIM ** -0.5)
        outs.append(u)
    beta = jax.nn.sigmoid(ba)
    g = -jnp.exp(alog) * _softplus(ba + dtb)
    gc = _xdot_l(tri, g)
    return outs + [beta, gc]


def _fn_gdn_post(ps, ts):
    ow, seg = ps
    o, z = ts
    ms = _segsum(o * o, seg) * (1.0 / HEAD_DIM)
    return [o * lax.rsqrt(ms + NORM_EPS) * ow * _silu(z)]


def _fn_rwkv_pre(ps, ts):
    mu_r, mu_k, mu_v, mu_l, mu_g, w0, w2p, a0, a2p, g2p, k_k, k_a, seg = ps
    r0, k0, v0, l0, g0, r1, k1, v1, l1, g1 = ts
    xr = r0 + (r1 - r0) * mu_r
    xk = k0 + (k1 - k0) * mu_k
    xv = v0 + (v1 - v0) * mu_v
    xl = l0 + (l1 - l0) * mu_l
    xg = g0 + (g1 - g0) * mu_g
    w = -_softplus(-(w0 + _bdot(jnp.tanh(xl), w2p))) - 0.5
    lw = -jnp.exp(w)
    a = jax.nn.sigmoid(a0 + _bdot(xl, a2p))
    g = _bdot(jax.nn.sigmoid(xg), g2p)
    kk = xk * k_k
    kk = kk * lax.rsqrt(_segsum(kk * kk, seg) + 1e-6)
    k2 = xk * (1.0 + (a - 1.0) * k_a)
    return [xr, lw, k2, xv, -kk, kk * a, g]


def _fn_rwkv_post(ps, ts):
    lw_, lb_, rk, seg = ps
    y, r, k2, v, g = ts
    inv = 1.0 / HEAD_DIM
    yc = y - _segsum(y, seg) * inv
    var = _segsum(yc * yc, seg) * inv
    yn = yc * lax.rsqrt(var + LNX_EPS) * lw_ + lb_
    bonus = _segsum(r * k2 * rk, seg) * v
    return [(yn + bonus) * g]


def _fn_merge(ps, ts):
    gla, glb, ya, yb = ts
    return [jax.nn.sigmoid(gla) * ya + jax.nn.sigmoid(glb) * yb]


def _fn_convglu(ps, ts):
    c0, c1, c2 = ps
    g0, g1, g2, up = ts
    return [_silu(c2 * g0 + c1 * g1 + c0 * g2) * up]


def _fn_add(ps, ts):
    acc = ts[0]
    for t in ts[1:]:
        acc = acc + t
    return [acc]


def _add_n(name, arrs, tt):
    return _stage_fwd(name, _fn_add, [_whole(a) for a in arrs], [], [(arrs[0].shape[1], f32)], tt)[0]


def _final_stage(x1, fo, tgt, gate2, nfw, tt):
    rows, d = x1.shape

    def loss_fn(gate, nw, xa, fa, tg):
        y = _rms(xa + gate * fa, nw, NORM_EPS)
        err = (y - tg) ** 2
        return 0.5 * jnp.sum(jnp.mean(err, axis=-1, keepdims=True), axis=0, keepdims=True)

    def body(x_ref, f_ref, t_ref, g_ref, w_ref, dx_ref, df_ref, dg_ref, dw_ref, l_ref):
        i = pl.program_id(0)
        args = (g_ref[...], w_ref[...], x_ref[...], f_ref[...])
        tg = t_ref[...]
        lv, vjp = jax.vjp(lambda g, w, xa, fa: loss_fn(g, w, xa, fa, tg), *args)
        dg, dw, dx, df = vjp(jnp.ones((1, 1), f32))
        dx_ref[...] = dx
        df_ref[...] = df

        @pl.when(i == 0)
        def _():
            dg_ref[...] = jnp.zeros_like(dg_ref)
            dw_ref[...] = jnp.zeros_like(dw_ref)
            l_ref[...] = jnp.zeros_like(l_ref)

        dg_ref[...] += dg
        dw_ref[...] += dw
        l_ref[...] += jnp.broadcast_to(lv, l_ref.shape)

    row = pl.BlockSpec((tt, d), lambda i: (i, 0))
    vec = pl.BlockSpec((1, d), lambda i: (0, 0))
    return pl.pallas_call(
        body, grid=(rows // tt,),
        in_specs=[row, row, row, vec, vec],
        out_specs=[row, row, vec, vec, pl.BlockSpec((1, LANES), lambda i: (0, 0))],
        out_shape=[jax.ShapeDtypeStruct((rows, d), f32)] * 2 + [jax.ShapeDtypeStruct((1, d), f32)] * 2
        + [jax.ShapeDtypeStruct((1, LANES), f32)],
        compiler_params=pltpu.CompilerParams(dimension_semantics=("arbitrary",)),
        name="loss_head",
    )(x1, fo, tgt, gate2, nfw)


def _ada_fwd(c_all, w_shard, b_cols):
    def body(c_ref, w_ref, b_ref, cond_ref, mod_ref):
        cond = _silu(c_ref[...])
        cond_ref[...] = cond
        mod_ref[...] = jnp.dot(cond.astype(bf16), w_ref[...].astype(bf16), preferred_element_type=f32) + b_ref[...]

    n = w_shard.shape[1]
    return pl.pallas_call(
        body, out_shape=[jax.ShapeDtypeStruct(c_all.shape, f32), jax.ShapeDtypeStruct((c_all.shape[0], n), f32)],
        name="ada_fwd",
    )(c_all, w_shard, b_cols)


def _adamw(name, w, g, m, v):
    rows, width = w.shape
    tt = _pick(rows, 128, 8)
    c1 = 1.0 - ADAM_B1 ** ADAM_STEP
    c2 = 1.0 - ADAM_B2 ** ADAM_STEP

    def body(w_ref, g_ref, m_ref, v_ref, d_ref, mo_ref, vo_ref):
        gg = g_ref[...]
        mn = ADAM_B1 * m_ref[...] + (1.0 - ADAM_B1) * gg
        vn = ADAM_B2 * v_ref[...] + (1.0 - ADAM_B2) * (gg * gg)
        m_hat = mn / c1
        v_hat = vn / c2
        d_ref[...] = -ADAM_LR * (m_hat / (jnp.sqrt(v_hat) + ADAM_EPS) + ADAM_WD * w_ref[...])
        mo_ref[...] = mn
        vo_ref[...] = vn

    spec = pl.BlockSpec((tt, width), lambda i: (i, 0))
    return pl.pallas_call(
        body, grid=(rows // tt,), in_specs=[spec] * 4, out_specs=[spec] * 3,
        out_shape=[jax.ShapeDtypeStruct((rows, width), f32)] * 3,
        compiler_params=pltpu.CompilerParams(dimension_semantics=("parallel",)),
        name=name,
    )(w, g, m, v)


def _place():
    return lax.axis_index("x"), lax.axis_index("y"), lax.axis_index("c")


def _ag8(name, blk):
    m, w = blk.shape
    flips = [(dx, dy, dc) for dx in (0, 1) for dy in (0, 1) for dc in (0, 1)][1:]

    def body(x_ref, out_ref, send_sems, recv_sems, local_sem):
        x, y, c = _place()
        me = 4 * x + 2 * y + c
        mine = pltpu.make_async_copy(x_ref, out_ref.at[me], local_sem)
        mine.start()
        peers = [((1 - x) if dx else x, (1 - y) if dy else y, (1 - c) if dc else c) for dx, dy, dc in flips]
        sends = []
        for k, peer in enumerate(peers):
            cp = pltpu.make_async_remote_copy(src_ref=x_ref, dst_ref=out_ref.at[me], send_sem=send_sems.at[k],
                                              recv_sem=recv_sems.at[k], device_id=peer, device_id_type=MESH_ID)
            cp.start()
            sends.append(cp)
        for k, (px, py, pc) in enumerate(peers):
            pltpu.make_async_remote_copy(src_ref=x_ref, dst_ref=out_ref.at[4 * px + 2 * py + pc],
                                         send_sem=send_sems.at[k], recv_sem=recv_sems.at[k],
                                         device_id=(px, py, pc), device_id_type=MESH_ID).wait_recv()
        for cp in sends:
            cp.wait_send()
        mine.wait()

    return pl.pallas_call(
        body, out_shape=jax.ShapeDtypeStruct((8, m, w), blk.dtype),
        in_specs=[pl.BlockSpec(memory_space=pltpu.VMEM)], out_specs=pl.BlockSpec(memory_space=pltpu.VMEM),
        scratch_shapes=[pltpu.SemaphoreType.DMA((7,)), pltpu.SemaphoreType.DMA((7,)), pltpu.SemaphoreType.DMA],
        name=name,
    )(blk)


def _other_chips(x, y):
    return [(1 - x, y), (x, 1 - y), (1 - x, 1 - y)]


_ANY = pl.BlockSpec(memory_space=pl.ANY)


def _rcopy(src, dst, send_sems, recv_sems, k, dev):
    return pltpu.make_async_remote_copy(src_ref=src, dst_ref=dst, send_sem=send_sems.at[k], recv_sem=recv_sems.at[k],
                                        device_id=dev, device_id_type=MESH_ID)


def _ag4_multi(ws):
    n = len(ws)

    def body(*refs):
        w_refs, out_refs = refs[:n], refs[n:2 * n]
        send_sems, recv_sems, local_sems = refs[2 * n:]
        x, y, c = _place()
        chip = 2 * x + y
        sibling = (x, y, 1 - c)
        chips = _other_chips(x, y)
        halves = [(pl.ds(c * (w.shape[0] // 2), w.shape[0] // 2), pl.ds((1 - c) * (w.shape[0] // 2), w.shape[0] // 2))
                  for w in ws]
        locs, copies = [], []
        for t in range(n):
            mine = pltpu.make_async_copy(w_refs[t], out_refs[t].at[chip], local_sems.at[t])
            mine.start()
            locs.append(mine)
            for k, (px, py) in enumerate(chips):
                cp = _rcopy(w_refs[t].at[halves[t][0]], out_refs[t].at[chip, halves[t][0]], send_sems, recv_sems,
                            6 * t + k, (px, py, c))
                cp.start()
                copies.append(cp)
        for t in range(n):
            for k, (px, py) in enumerate(chips):
                blk = out_refs[t].at[2 * px + py, halves[t][0]]
                _rcopy(blk, blk, send_sems, recv_sems, 6 * t + k, (px, py, c)).wait_recv()
                fw = _rcopy(blk, blk, send_sems, recv_sems, 6 * t + 3 + k, sibling)
                fw.start()
                copies.append(fw)
        for t in range(n):
            for k, (px, py) in enumerate(chips):
                blk = out_refs[t].at[2 * px + py, halves[t][1]]
                _rcopy(blk, blk, send_sems, recv_sems, 6 * t + 3 + k, sibling).wait_recv()
        for cp in copies:
            cp.wait_send()
        for m in locs:
            m.wait()

    return pl.pallas_call(
        body, out_shape=[jax.ShapeDtypeStruct((4,) + w.shape, w.dtype) for w in ws],
        in_specs=[_ANY] * n, out_specs=[_ANY] * n,
        scratch_shapes=[pltpu.SemaphoreType.DMA((6 * n,)), pltpu.SemaphoreType.DMA((6 * n,)),
                        pltpu.SemaphoreType.DMA((n,))],
        name="weights_all_gather",
    )(*ws)


def _rs_sibling_swap(gs):
    n = len(gs)

    def body(*refs):
        g_refs, recv_refs = refs[:n], refs[n:2 * n]
        send_sems, recv_sems = refs[2 * n:]
        x, y, c = _place()
        copies = []
        for t in range(n):
            rh = gs[t].shape[1] // 2
            for s_ in range(4):
                cp = _rcopy(g_refs[t].at[s_, pl.ds((1 - c) * rh, rh)], recv_refs[t].at[s_], send_sems, recv_sems,
                            4 * t + s_, (x, y, 1 - c))
                cp.start()
                copies.append(cp)
        for cp in copies:
            cp.wait_recv()
        for cp in copies:
            cp.wait_send()

    return pl.pallas_call(
        body, out_shape=[jax.ShapeDtypeStruct((4, g.shape[1] // 2, g.shape[2]), g.dtype) for g in gs],
        in_specs=[_ANY] * n, out_specs=[_ANY] * n,
        scratch_shapes=[pltpu.SemaphoreType.DMA((4 * n,)), pltpu.SemaphoreType.DMA((4 * n,))],
        name="grads_sibling_swap",
    )(*gs)


def _rs_chip_exchange(ps):
    n = len(ps)

    def body(*refs):
        p_refs, recv_refs = refs[:n], refs[n:2 * n]
        send_sems, recv_sems = refs[2 * n:]
        x, y, c = _place()
        chips = _other_chips(x, y)
        copies = []
        for t in range(n):
            for k, (px, py) in enumerate(chips):
                cp = _rcopy(p_refs[t].at[2 * px + py], recv_refs[t].at[k], send_sems, recv_sems, 3 * t + k,
                            (px, py, c))
                cp.start()
                copies.append(cp)
        for cp in copies:
            cp.wait_recv()
        for cp in copies:
            cp.wait_send()

    return pl.pallas_call(
        body, out_shape=[jax.ShapeDtypeStruct((3,) + p.shape[1:], p.dtype) for p in ps],
        in_specs=[_ANY] * n, out_specs=[_ANY] * n,
        scratch_shapes=[pltpu.SemaphoreType.DMA((3 * n,)), pltpu.SemaphoreType.DMA((3 * n,))],
        name="grads_chip_exchange",
    )(*ps)


_JOIN_PIECES = 4


def _rs_sibling_join(qs):
    n = len(qs)
    npc = _JOIN_PIECES

    def body(*refs):
        q_refs, out_refs = refs[:n], refs[n:2 * n]
        send_sems, recv_sems, local_sems = refs[2 * n:]
        x, y, c = _place()
        locs, copies = [], []
        for t in range(n):
            rh = qs[t].shape[0]
            loc = pltpu.make_async_copy(q_refs[t], out_refs[t].at[pl.ds(c * rh, rh)], local_sems.at[t])
            loc.start()
            locs.append(loc)
            pr = rh // npc
            for i in range(npc):
                cp = _rcopy(q_refs[t].at[pl.ds(i * pr, pr)], out_refs[t].at[pl.ds(c * rh + i * pr, pr)], send_sems,
                            recv_sems, npc * t + i, (x, y, 1 - c))
                cp.start()
                copies.append(cp)
        for t in range(n):
            rh = qs[t].shape[0]
            pr = rh // npc
            for i in range(npc):
                _rcopy(q_refs[t].at[pl.ds(i * pr, pr)], out_refs[t].at[pl.ds((1 - c) * rh + i * pr, pr)], send_sems,
                       recv_sems, npc * t + i, (x, y, 1 - c)).wait_recv()
        for cp in copies:
            cp.wait_send()
        for m in locs:
            m.wait()

    return pl.pallas_call(
        body, out_shape=[jax.ShapeDtypeStruct((2 * q.shape[0], q.shape[1]), q.dtype) for q in qs],
        in_specs=[_ANY] * n, out_specs=[_ANY] * n,
        scratch_shapes=[pltpu.SemaphoreType.DMA((npc * n,)), pltpu.SemaphoreType.DMA((npc * n,)),
                        pltpu.SemaphoreType.DMA((n,))],
        name="grads_sibling_join",
    )(*qs)


def _add_half(name, g, recv, ci):
    S, r, w = g.shape
    rh = r // 2
    tt = _pick(rh, 256, 8)
    nb = rh // tt

    def body(c_ref, a_ref, b_ref, o_ref):
        o_ref[...] = a_ref[...] + b_ref[...]

    grid_spec = pltpu.PrefetchScalarGridSpec(
        num_scalar_prefetch=1, grid=(S, nb),
        in_specs=[pl.BlockSpec((1, tt, w), lambda s_, i, cr: (s_, cr[0] * nb + i, 0)),
                  pl.BlockSpec((1, tt, w), lambda s_, i, cr: (s_, i, 0))],
        out_specs=pl.BlockSpec((1, tt, w), lambda s_, i, cr: (s_, i, 0)))
    return pl.pallas_call(body, grid_spec=grid_spec, out_shape=jax.ShapeDtypeStruct((S, rh, w), f32),
                          name=name)(ci.reshape(1).astype(jnp.int32), g, recv)


def _sum_chip(name, pair, others, chip):
    _, rh, w = pair.shape
    tt = _pick(rh, 128, 8)

    def body(c_ref, a_ref, b_ref, o_ref):
        o_ref[...] = ((a_ref[0] + b_ref[0]) + b_ref[1]) + b_ref[2]

    grid_spec = pltpu.PrefetchScalarGridSpec(
        num_scalar_prefetch=1, grid=(rh // tt,),
        in_specs=[pl.BlockSpec((1, tt, w), lambda i, cr: (cr[0], i, 0)),
                  pl.BlockSpec((3, tt, w), lambda i, cr: (0, i, 0))],
        out_specs=pl.BlockSpec((tt, w), lambda i, cr: (i, 0)))
    return pl.pallas_call(body, grid_spec=grid_spec, out_shape=jax.ShapeDtypeStruct((rh, w), f32),
                          name=name)(chip.reshape(1).astype(jnp.int32), pair, others)


def _sum_devices(gathered, head_row):
    _, rows, width = gathered.shape

    def body(g_ref, out_ref, head_ref):
        acc = g_ref[0]
        for d in range(1, 8):
            acc = acc + g_ref[d]
        out_ref[...] = acc
        row = acc[head_row:head_row + 1, :]
        hs = row[:, 0:HEAD_DIM]
        for h in range(1, HEADS):
            hs = hs + row[:, h * HEAD_DIM:(h + 1) * HEAD_DIM]
        head_ref[...] = jnp.zeros_like(head_ref)
        head_ref[0:1, 0:HEAD_DIM] = hs

    return pl.pallas_call(
        body, out_shape=[jax.ShapeDtypeStruct((rows, width), f32), jax.ShapeDtypeStruct((8, LANES), f32)],
        name="small_grads_sum",
    )(gathered)


def _pack(arrs, rows_mult, dtype):
    flat = jnp.concatenate([a.reshape(-1).astype(dtype) for a in arrs])
    per = PACK_W * rows_mult
    total = -(-flat.shape[0] // per) * per
    return jnp.pad(flat, (0, total - flat.shape[0])).reshape(total // PACK_W, PACK_W)


def _unpack(buf, shapes):
    flat = buf.reshape(-1)
    out, off = [], 0
    for s in shapes:
        n = int(np.prod(s))
        out.append(flat[off:off + n].reshape(s))
        off += n
    return out


_BIG = ["w_in", "w_branch_gdn", "w_branch_rwkv", "w_out", "w_ffn_in", "w_ffn_out"]
_MID = ["conv_gdn", "conv_ffn", "w2", "a2", "g2"]
_SMALL =["b_ada", "norm1_w", "a_log", "dt_bias", "onorm_gdn", "mu_rwkv", "w0", "a0", "k_k", "k_a", "r_k", "lnx_w",
          "lnx_b", "norm2_w", "norm_f_w"]
_ORDER = ["w_ada", "b_ada", "norm1_w", "w_in", "conv_gdn", "a_log", "dt_bias", "onorm_gdn", "w_branch_gdn", "mu_rwkv",
          "w0", "w2", "a0", "a2", "g2", "k_k", "k_a", "r_k", "lnx_w", "lnx_b", "w_branch_rwkv", "w_out", "norm2_w",
          "w_ffn_in", "conv_ffn", "w_ffn_out", "norm_f_w"]


def _win_pad(w):
    z = lambda n: jnp.zeros((w.shape[0], n), w.dtype)
    return jnp.concatenate([w[:, 0:2048], w[:, 2064:3728], w[:, 2048:2064], z(112), w[:, 3728:3888], z(96),
                            w[:, 3888:5936]], axis=1)


def _win_unpad(g):
    return jnp.concatenate([g[:, 0:2048], g[:, 3712:3728], g[:, 2048:3712], g[:, 3840:4000], g[:, 4096:6144]], axis=1)


def kernel(x, c, w_ada, b_ada, norm1_w, w_in, conv_gdn, a_log, dt_bias, onorm_gdn, w_branch_gdn, mu_rwkv, w0, w2, a0, a2, g2, k_k, k_a, r_k, lnx_w, lnx_b, w_branch_rwkv, w_out, norm2_w, w_ffn_in, conv_ffn, w_ffn_out, norm_f_w, loss_target, m_w_ada, m_b_ada, m_norm1_w, m_w_in, m_conv_gdn, m_a_log, m_dt_bias, m_onorm_gdn, m_w_branch_gdn, m_mu_rwkv, m_w0, m_w2, m_a0, m_a2, m_g2, m_k_k, m_k_a, m_r_k, m_lnx_w, m_lnx_b, m_w_branch_rwkv, m_w_out, m_norm2_w, m_w_ffn_in, m_conv_ffn, m_w_ffn_out, m_norm_f_w, v_w_ada, v_b_ada, v_norm1_w, v_w_in, v_conv_gdn, v_a_log, v_dt_bias, v_onorm_gdn, v_w_branch_gdn, v_mu_rwkv, v_w0, v_w2, v_a0, v_a2, v_g2, v_k_k, v_k_a, v_r_k, v_lnx_w, v_lnx_b, v_w_branch_rwkv, v_w_out, v_norm2_w, v_w_ffn_in, v_conv_ffn, v_w_ffn_out, v_norm_f_w):
    args = dict(locals())
    W = {n: args[n] for n in _ORDER}
    Mo = {n: args["m_" + n] for n in _ORDER}
    Vo = {n: args["v_" + n] for n in _ORDER}
    shapes = {n: W[n].shape for n in _ORDER}
    sq = lambda a: a.reshape(a.shape[-2:]) if a.ndim == 3 else a.reshape(1, -1)
    row = lambda a: a.reshape(1, -1)

    xi, yi, ci = lax.axis_index("x"), lax.axis_index("y"), lax.axis_index("c")
    dev = 4 * xi + 2 * yi + ci
    chip = 2 * xi + yi

    x2 = x[0]
    tgt = loss_target[0]
    T, D = x2.shape
    N = T // CHUNK
    tt_l = _pick(T, 256, CHUNK)
    tt_h = _pick(T, 128, CHUNK)

    mid_shapes = [shapes[n][1:] for n in _MID]
    small_blk = _pack([c] + [W[n] for n in _MID], 8, f32)
    small_all = _ag8("gather_c_mid", small_blk)
    c_all = small_all[:, 0, :]
    per_chip = small_all[0::2].reshape(4, -1)[:, D:]
    mid = [dict(zip(_MID, _unpack(per_chip[j], mid_shapes))) for j in range(4)]
    catm = lambda n: jnp.concatenate([mid[j][n] for j in range(4)], axis=1)
    conv_gdn_f, conv_ffn_f = catm("conv_gdn"), catm("conv_ffn")
    w2f, a2f, g2f = catm("w2"), catm("a2"), catm("g2")

    win_s, wbg_s, wbr_s, wout_s, wfi_s, wfo_s = _ag4_multi([sq(W[n]).astype(bf16) for n in _BIG])
    win_p = _win_pad(jnp.concatenate([win_s[j] for j in range(4)], axis=1))
    wout_f = wout_s.reshape(D, D)
    wfo = wfo_s.reshape(D_FF, D)
    zpad = lambda a, top, bot: jnp.pad(a, ((top, bot), (0, 0)))
    w2p, a2p, g2p = zpad(w2f, 0, 64), zpad(a2f, 64, 0), zpad(g2f, 0, 96)

    ncol = shapes["w_ada"][2]
    b_cols = lax.dynamic_slice(sq(W["b_ada"]), (0, chip * ncol), (1, ncol))
    cond16, mod_cols = _ada_fwd(jnp.pad(c_all, ((0, 8), (0, 0))), sq(W["w_ada"]), b_cols)
    mod_all = _ag8("gather_mod", mod_cols[:8])
    mod_mine = lax.dynamic_slice(mod_all[0::2], (0, dev, 0), (4, 1, ncol)).reshape(1, 4 * ncol)
    shift1, scale1, gate1, shift2, scale2, gate2 = [mod_mine[:, i * D:(i + 1) * D] for i in range(6)]

    seg = _seg_matrix(WIDTH, HEAD_DIM)
    norm1 = [sq(W["norm1_w"]), shift1, scale1]
    h1 = _stage_fwd("norm_mod1", _fn_norm_mod, [_whole(x2)], norm1, [(D, bf16)], tt_l)[0]
    p = _matmul("in_proj", h1, win_p, "nn")
    qkv_s = [p[:, 0:1536]] + [_shift_down(p[:, 0:1536], s) for s in (1, 2, 3)]
    p_prev = _shift_down(p[:, 2048:4096], 1)

    cgq = [row(conv_gdn_f[j, part * WIDTH:(part + 1) * WIDTH]) for part in range(3) for j in range(4)]
    lane_pad = lambda a: jnp.pad(row(a), ((0, 0), (8, LANES - 16)))
    gdn_pre_ps = cgq + [lane_pad(W["a_log"]), lane_pad(W["dt_bias"]), seg, _chunk_tri(tt_h, CHUNK)]
    gdn_pre_ts = [(qkv_s[s], WIDTH, part) for part in range(3) for s in range(4)] + [(p, LANES, 29)]
    q_, k_, v_, beta_t, gc_t = _stage_fwd("gdn_pre", _fn_gdn_pre, gdn_pre_ts, gdn_pre_ps,
                                          [(WIDTH, f32)] * 3 + [(LANES, f32)] * 2, tt_h)
    heads_col = lambda a, lo: a[:, lo:lo + HEADS].reshape(N, CHUNK, HEADS).transpose(0, 2, 1)
    beta_c = heads_col(beta_t, 0)[..., None]
    gch = heads_col(gc_t, 8)
    gdn_ins = [_to_chunks(q_, N), _to_chunks(k_, N), _to_chunks(v_, N), beta_c, gch[..., None], gch[:, :, None, :],
               gch[:, :, CHUNK - 1:CHUNK, None]]
    o_ch, gdn_hist = _scan_fwd("gdn_scan", _gdn_chunk, gdn_ins, HEAD_DIM)
    o_ = _from_chunks(o_ch)
    ow512 = jnp.tile(row(W["onorm_gdn"]), (1, HEADS))
    gdn_post_ts = [_whole(o_), (p, WIDTH, 3)]
    ya = _stage_fwd("gdn_post", _fn_gdn_post, gdn_post_ts, [ow512, seg], [(WIDTH, bf16)], tt_l)[0]

    mu = sq(W["mu_rwkv"])
    rw_ps = [mu[:, 0:512], mu[:, 512:1024], mu[:, 1024:1536], mu[:, 1536:1664], jnp.pad(mu[:, 1664:1824], ((0, 0), (0, 96))),
             sq(W["w0"]), w2p, sq(W["a0"]), a2p, g2p, sq(W["k_k"]), sq(W["k_a"]), seg]
    rw_ts = [(p, WIDTH, 4), (p, WIDTH, 5), (p, WIDTH, 6), (p, LANES, 28), (p, 256, 15),
             (p_prev, WIDTH, 0), (p_prev, WIDTH, 1), (p_prev, WIDTH, 2), (p_prev, LANES, 12), (p_prev, 256, 7)]
    rw_out = _stage_fwd("rwkv_pre", _fn_rwkv_pre, rw_ts, rw_ps, [(WIDTH, f32)] * 7, tt_h)
    r_, lw_, k2_, vv_, na_, b_, g_ = rw_out
    rw_ins = [_to_chunks(a, N) for a in (r_, lw_, k2_, vv_, na_, b_)]
    y_ch, rw_hist = _scan_fwd("rwkv_scan", _rwkv_chunk, rw_ins, HEAD_DIM)
    y_ = _from_chunks(y_ch)
    rwp_ps = [sq(W["lnx_w"]), sq(W["lnx_b"]), row(W["r_k"]), seg]
    rwp_ts = [_whole(y_), _whole(r_), _whole(k2_), _whole(vv_), _whole(g_)]
    yb = _stage_fwd("rwkv_post", _fn_rwkv_post, rwp_ts, rwp_ps, [(WIDTH, bf16)], tt_l)[0]

    big_a = _matmul("branch_gdn", ya, wbg_s, "nn", shards=4)
    big_b = _matmul("branch_rwkv", yb, wbr_s, "nn", shards=4)
    merge_ts = [(p, D, 4), (p, D, 5), _whole(big_a), _whole(big_b)]
    merged = _stage_fwd("merge", _fn_merge, merge_ts, [], [(D, bf16)], tt_l)[0]
    mo = _matmul("out_proj", merged, wout_f, "nn")
    norm2 = [gate1, sq(W["norm2_w"]), shift2, scale2]
    x1, h2 = _stage_fwd("resid_norm_mod2", _fn_resid_norm_mod, [_whole(x2), _whole(mo)], norm2, [(D, f32), (D, bf16)], tt_l)
    f = _matmul("ffn_in", h2, wfi_s, "nn", shards=4, tm=512, tn=1408)
    gate_s = [f[:, 0:D_FF], _shift_down(f[:, 0:D_FF], 1), _shift_down(f[:, 0:D_FF], 2)]
    up = f[:, D_FF:]
    cg_ps = [row(conv_ffn_f[j]) for j in range(3)]
    cg_ts = [_whole(a) for a in gate_s] + [_whole(up)]
    act = _stage_fwd("convglu", _fn_convglu, cg_ts, cg_ps, [(D_FF, bf16)], tt_h)[0]
    fo = _matmul("ffn_out", act, wfo, "nn", tk=1408)

    dx1_a, dfo, dgate2, dnormf, loss_part = _final_stage(x1, fo, tgt, gate2, row(W["norm_f_w"]), tt_l)

    dact = _matmul("d_act", dfo, wfo, "nt", tm=512, tn=1408)
    g_wfo = _matmul("g_ffn_out", act, dfo, "tn", tm=1408)
    (dg0, dg1, dg2, dup), dcf = _stage_bwd("convglu_bwd", _fn_convglu, cg_ts, cg_ps, [[_whole(dact)]], tt_h,
                                           [True] * 4, [True] * 3)
    dgate = _add_n("d_gate", [dg0, _shift_up(dg1, 1), _shift_up(dg2, 2)], tt_l)
    df = jnp.concatenate([dgate, dup], axis=1)
    dh2 = _matmul("d_h2", df, wfi_s, "nt", shards=4, tm=512, tk=1408)
    g_wfi = _matmul("g_ffn_in", h2, df, "tn", shards=4, tn=1408)
    (dx_a, dmo), (dgate1, dnorm2, dshift2, dscale2) = _stage_bwd(
        "resid_norm_mod2_bwd", _fn_resid_norm_mod, [_whole(x2), _whole(mo)], norm2,
        [[_whole(dx1_a)], [_whole(dh2)]], tt_l, [True, True], [True] * 4)
    dmerged = _matmul("d_merged", dmo, wout_f, "nt")
    g_wout = _matmul("g_out_proj", merged, dmo, "tn")
    (dgla, dglb, dbig_a, dbig_b), _ = _stage_bwd("merge_bwd", _fn_merge, merge_ts, [], [[_whole(dmerged)]], tt_l,
                                                 [True] * 4, [])
    dya = _matmul("d_ya", dbig_a, wbg_s, "nt", shards=4)
    g_wbg = _matmul("g_branch_gdn", ya, dbig_a, "tn", shards=4)
    dyb = _matmul("d_yb", dbig_b, wbr_s, "nt", shards=4)
    g_wbr = _matmul("g_branch_rwkv", yb, dbig_b, "tn", shards=4)

    (dy_, dr_p, dk2_p, dv_p, dg_p), (dlnxw, dlnxb, drk) = _stage_bwd(
        "rwkv_post_bwd", _fn_rwkv_post, rwp_ts, rwp_ps, [[_whole(dyb)]], tt_l, [True] * 5, [True, True, True, False])
    d_rw = _scan_bwd("rwkv_scan_bwd", _rwkv_chunk, rw_ins, rw_hist, _to_chunks(dy_, N))
    dr_c, dlw_c, dk2_c, dv_c, dna_c, db_c = [_from_chunks(a) for a in d_rw]
    rw_cots = [[_whole(dr_p), _whole(dr_c)], [_whole(dlw_c)], [_whole(dk2_p), _whole(dk2_c)],
               [_whole(dv_p), _whole(dv_c)], [_whole(dna_c)], [_whole(db_c)], [_whole(dg_p)]]
    rw_dt, rw_dp = _stage_bwd("rwkv_pre_bwd", _fn_rwkv_pre, rw_ts, rw_ps, rw_cots, tt_h, [True] * 10,
                              [True] * 12 + [False])
    dmu_r, dmu_k, dmu_v, dmu_l, dmu_g, dw0, dw2p, da0, da2p, dg2p, dkk, dka = rw_dp
    d_rkv = [_add_n("d_p_rwkv%d" % i, [rw_dt[i], _shift_up(rw_dt[5 + i], 1)], tt_l) for i in range(5)]

    (do_, dz), (dow512,) = _stage_bwd("gdn_post_bwd", _fn_gdn_post, gdn_post_ts, [ow512, seg], [[_whole(dya)]], tt_l,
                                      [True, True], [True, False])
    d_gdn = _scan_bwd("gdn_scan_bwd", _gdn_chunk, gdn_ins, gdn_hist, _to_chunks(do_, N))
    dq_c, dk_c, dv_c2, dbeta_c, dgc_col, dgc_row, dgl = d_gdn
    back = lambda a, lo: jnp.pad(a.transpose(0, 2, 1).reshape(T, HEADS), ((0, 0), (lo, LANES - HEADS - lo)))
    dbeta_t = back(dbeta_c[..., 0], 0)
    dgc_1 = back(dgc_col[..., 0], 8)
    dgc_2 = back(dgc_row[:, :, 0, :], 8)
    dgc_3 = back(jnp.pad(dgl[:, :, 0, :], ((0, 0), (0, 0), (CHUNK - 1, 0))), 8)
    gdn_cots = [[_whole(_from_chunks(dq_c))], [_whole(_from_chunks(dk_c))], [_whole(_from_chunks(dv_c2))],
                [_whole(dbeta_t)], [_whole(dgc_1), _whole(dgc_2), _whole(dgc_3)]]
    gdn_dt, gdn_dp = _stage_bwd("gdn_pre_bwd", _fn_gdn_pre, gdn_pre_ts, gdn_pre_ps, gdn_cots, tt_h, [True] * 13,
                                [True] * 14 + [False, False])
    d_qkv = [_add_n("d_p_qkv%d" % part, [gdn_dt[4 * part]] + [_shift_up(gdn_dt[4 * part + s], s) for s in (1, 2, 3)], tt_l)
             for part in range(3)]
    dba = gdn_dt[12]

    dp = jnp.concatenate(d_qkv + [dz] + d_rkv[0:4] + [dba, d_rkv[4], dgla, dglb], axis=1)
    dh1 = _matmul("d_h1", dp, win_p, "nt")
    g_win = _win_unpad(_matmul("g_in_proj", h1, dp, "tn"))
    (dx_b,), (dnorm1, dshift1, dscale1) = _stage_bwd("norm_mod1_bwd", _fn_norm_mod, [_whole(x2)], norm1,
                                                     [[_whole(dh1)]], tt_l, [True], [True] * 3)
    grad_x = _add_n("grad_x", [dx_a, dx_b], tt_l)

    dmod = jnp.concatenate([dshift1, dscale1, dgate1, dshift2, dscale2, dgate2], axis=1)
    g_conv_gdn = jnp.concatenate([jnp.concatenate([gdn_dp[4 * part + j] for part in range(3)], axis=1) for j in range(4)], axis=0)
    g_conv_ffn = jnp.concatenate(dcf, axis=0)
    g_mu = jnp.concatenate([dmu_r, dmu_k, dmu_v, dmu_l, dmu_g[:, :160]], axis=1)
    small_parts = {"b_ada": dmod, "norm1_w": dnorm1, "a_log": gdn_dp[12][:, 8:16], "dt_bias": gdn_dp[13][:, 8:16],
                   "mu_rwkv": g_mu, "w0": dw0, "a0": da0, "k_k": dkk, "k_a": dka, "r_k": drk, "lnx_w": dlnxw,
                   "lnx_b": dlnxb, "norm2_w": dnorm2, "norm_f_w": dnormf}
    small_names = [n for n in _SMALL if n != "onorm_gdn"]
    mid_full = [g_conv_gdn, g_conv_ffn, dw2p[0:64], da2p[64:128], dg2p[0:160]]
    body_rows = _pack([small_parts[n] for n in small_names] + [loss_part[:, 0:1]] + mid_full, 1, f32)
    head_row = body_rows.shape[0]
    small_g = jnp.concatenate([body_rows, jnp.pad(dow512, ((0, 0), (0, PACK_W - WIDTH)))], axis=0)
    small_g = jnp.pad(small_g, ((0, -small_g.shape[0] % 8), (0, 0)))
    small_all_g = _ag8("gather_small_grads", small_g)
    small_sum, head_sum = _sum_devices(small_all_g, head_row)
    small_shapes = [shapes[n][1:] if n != "norm_f_w" else shapes[n] for n in small_names]
    un = _unpack(small_sum, small_shapes + [(1,)] + [g.shape for g in mid_full])
    small_grads = dict(zip(small_names, un))
    loss = un[len(small_names)].reshape(())
    small_grads["onorm_gdn"] = head_sum[0, 0:HEAD_DIM]
    for n, g in zip(_MID, un[len(small_names) + 1:]):
        wcols = shapes[n][2]
        small_grads[n] = lax.dynamic_slice(g, (0, chip * wcols), (g.shape[0], wcols))

    dmod_all = small_all_g[:, 0:6, :].reshape(8, 6 * PACK_W)
    dmod_cols = lax.dynamic_slice(dmod_all, (0, chip * ncol), (8, ncol))
    g_wada = _matmul("g_w_ada", cond16, jnp.pad(dmod_cols, ((0, 8), (0, 0))), "tn")

    nwin = shapes["w_in"][2]
    g_win_s = g_win.reshape(D, 4, nwin).transpose(1, 0, 2)
    gs = [g_win_s, g_wbg, g_wbr, g_wout.reshape(4, D // 4, D), g_wfi, g_wfo.reshape(4, D_FF // 4, D)]
    recvs = _rs_sibling_swap(gs)
    pairs = [_add_half("grads_pair_sum%d" % t, g, r_, ci) for t, (g, r_) in enumerate(zip(gs, recvs))]
    others = _rs_chip_exchange(pairs)
    halves = [_sum_chip("grads_chip_sum%d" % t, p_, o_, chip) for t, (p_, o_) in enumerate(zip(pairs, others))]
    big_grads = dict(zip(_BIG, _rs_sibling_join(halves)))

    res = {tag: {} for tag in ("grad", "delta", "new_m", "new_v")}

    def put(n, g, d, m_, v_):
        for tag, val in zip(("grad", "delta", "new_m", "new_v"), (g, d, m_, v_)):
            res[tag][n] = val.reshape(shapes[n])

    for n in _BIG:
        put(n, big_grads[n], *_adamw("adamw_" + n, sq(W[n]), big_grads[n], sq(Mo[n]), sq(Vo[n])))
    put("w_ada", g_wada, *_adamw("adamw_w_ada", sq(W["w_ada"]), g_wada, sq(Mo["w_ada"]), sq(Vo["w_ada"])))
    rest = _SMALL + _MID
    pk = lambda d: _pack([d[n] for n in rest], 8, f32)
    sg = pk(small_grads)
    sm = _adamw("adamw_small", pk(W), sg, pk(Mo), pk(Vo))
    for tag, buf in zip(("grad", "delta", "new_m", "new_v"), (sg,) + tuple(sm)):
        res[tag].update(zip(rest, _unpack(buf, [shapes[n] for n in rest])))
    outs = [loss, grad_x.reshape(x.shape)]
    for tag in ("grad", "delta", "new_m", "new_v"):
        outs += [res[tag][n] for n in _ORDER]
    return tuple(outs)
```

```python
import functools
import math

import numpy as np
import jax
import jax.numpy as jnp
from jax import lax
from jax.experimental import pallas as pl
from jax.experimental.pallas import tpu as pltpu

f32 = jnp.float32
bf16 = jnp.bfloat16

LANES = 128
HEADS = 8
HEAD_DIM = 64
WIDTH = HEADS * HEAD_DIM
CHUNK = 64
D_FF = 2816
NORM_EPS = 1e-6
LNX_EPS = 64e-5
PACK_W = 1024
MESH_ID = pl.DeviceIdType.MESH

ADAM_LR, ADAM_B1, ADAM_B2, ADAM_EPS, ADAM_WD, ADAM_STEP = 0.001, 0.9, 0.999, 1e-08, 0.01, 10


def _pick(n, target, mult):
    if n <= target:
        return n
    best = None
    for t in range(mult, target + 1, mult):
        if n % t == 0:
            best = t
    assert best is not None, (n, target, mult)
    return best


def _split_bf16(x, n):
    parts, r = [], x
    for i in range(n):
        p = r.astype(bf16)
        parts.append(p)
        if i + 1 < n:
            r = r - p.astype(f32)
    return parts


def _xdot_r_impl(x, m, n, dims):
    acc = None
    for p in _split_bf16(x, n):
        t = lax.dot_general(p, m, dims, preferred_element_type=f32)
        acc = t if acc is None else acc + t
    return acc


def _make_xdot_r(n):
    nn = (((1,), (0,)), ((), ()))
    nt = (((1,), (1,)), ((), ()))

    @jax.custom_vjp
    def xdot(x, m):
        return _xdot_r_impl(x, m, n, nn)

    def fwd(x, m):
        return _xdot_r_impl(x, m, n, nn), m

    def bwd(m, ct):
        return _xdot_r_impl(ct, m, n, nt), jnp.zeros_like(m)

    xdot.defvjp(fwd, bwd)
    return xdot


_segsum = _make_xdot_r(2)


def _xdot_l_impl(m, x, n, dims):
    acc = None
    for p in _split_bf16(x, n):
        t = lax.dot_general(m, p, dims, preferred_element_type=f32)
        acc = t if acc is None else acc + t
    return acc


@jax.custom_vjp
def _xdot_l(m, x):
    return _xdot_l_impl(m, x, 3, (((1,), (0,)), ((), ())))


def _xdot_l_fwd(m, x):
    return _xdot_l(m, x), m


def _xdot_l_bwd(m, ct):
    return jnp.zeros_like(m), _xdot_l_impl(m, ct, 3, (((0,), (0,)), ((), ())))


_xdot_l.defvjp(_xdot_l_fwd, _xdot_l_bwd)


@jax.custom_vjp
def _bdot(x, w):
    return jnp.dot(x.astype(bf16), w.astype(bf16), preferred_element_type=f32)


def _bdot_fwd(x, w):
    return _bdot(x, w), (x, w)


def _bdot_bwd(res, ct):
    x, w = res
    c = ct.astype(bf16)
    dx = lax.dot_general(c, w.astype(bf16), (((1,), (1,)), ((), ())), preferred_element_type=f32)
    dw = lax.dot_general(x.astype(bf16), c, (((0,), (0,)), ((), ())), preferred_element_type=f32)
    return dx, dw


_bdot.defvjp(_bdot_fwd, _bdot_bwd)


def _silu(x):
    return x * jax.nn.sigmoid(x)


def _softplus(x):
    return jnp.maximum(x, 0.0) + jnp.log(1.0 + jnp.exp(-jnp.abs(x)))


def _rms(x, w, eps):
    return x * lax.rsqrt(jnp.mean(x * x, axis=-1, keepdims=True) + eps) * w


def _seg_matrix(width, seg):
    i = np.arange(width)
    return jnp.asarray((i[:, None] // seg) == (i[None, :] // seg), dtype=bf16)


def _chunk_tri(rows, chunk):
    i = np.arange(rows)
    return jnp.asarray(((i[:, None] // chunk) == (i[None, :] // chunk)) & (i[:, None] >= i[None, :]), dtype=bf16)


def _shift_down(a, s):
    return jnp.pad(a, ((s, 0), (0, 0)))[: a.shape[0]]


def _shift_up(a, s):
    return jnp.pad(a, ((0, s), (0, 0)))[s:]


def _tile_spec(tt, w, cidx):
    return pl.BlockSpec((tt, w), lambda i: (i, cidx))


def _full_spec(shape):
    nd = len(shape)
    return pl.BlockSpec(shape, lambda i: (0,) * nd)


def _stage_fwd(name, fn, tiles, params, outs, tt):
    rows = tiles[0][0].shape[0]
    nt, npar = len(tiles), len(params)

    def body(*refs):
        ts = [r[...].astype(f32) for r in refs[:nt]]
        ps = [r[...] for r in refs[nt:nt + npar]]
        res = fn(ps, ts)
        for r, v in zip(refs[nt + npar:], res):
            r[...] = v.astype(r.dtype)

    return pl.pallas_call(
        body, grid=(rows // tt,),
        in_specs=[_tile_spec(tt, w, ci) for (_, w, ci) in tiles] + [_full_spec(p.shape) for p in params],
        out_specs=[_tile_spec(tt, w, 0) for (w, _) in outs],
        out_shape=[jax.ShapeDtypeStruct((rows, w), dt) for (w, dt) in outs],
        compiler_params=pltpu.CompilerParams(dimension_semantics=("parallel",)),
        name=name,
    )(*[t[0] for t in tiles], *params)


def _stage_bwd(name, fn, tiles, params, cots, tt, tile_grad, param_grad):
    rows = tiles[0][0].shape[0]
    nt, npar = len(tiles), len(params)
    flat_cots = [c for group in cots for c in group]
    groups = [len(g) for g in cots]
    ncot = len(flat_cots)
    dt_w = [w for (_, w, _), g in zip(tiles, tile_grad) if g]
    dp_shapes = [p.shape for p, g in zip(params, param_grad) if g]
    ndt = len(dt_w)

    def body(*refs):
        i = pl.program_id(0)
        t_refs = refs[:nt]
        p_refs = refs[nt:nt + npar]
        c_refs = refs[nt + npar:nt + npar + ncot]
        dt_refs = refs[nt + npar + ncot:nt + npar + ncot + ndt]
        dp_refs = refs[nt + npar + ncot + ndt:]
        ts = [r[...].astype(f32) for r in t_refs]
        ps = [r[...] for r in p_refs]

        def f(dp, dt):
            dp, dt = iter(dp), iter(dt)
            pp = [next(dp) if g else p for p, g in zip(ps, param_grad)]
            tl = [next(dt) if g else t for t, g in zip(ts, tile_grad)]
            return fn(pp, tl)

        _, vjp = jax.vjp(f, [p for p, g in zip(ps, param_grad) if g], [t for t, g in zip(ts, tile_grad) if g])
        cs, j = [], 0
        for n in groups:
            acc = c_refs[j][...].astype(f32)
            for q in range(1, n):
                acc = acc + c_refs[j + q][...].astype(f32)
            cs.append(acc)
            j += n
        gp, gt = vjp(cs)
        for r, v in zip(dt_refs, gt):
            r[...] = v

        @pl.when(i == 0)
        def _():
            for r in dp_refs:
                r[...] = jnp.zeros_like(r)

        for r, v in zip(dp_refs, gp):
            r[...] += v

    res = pl.pallas_call(
        body, grid=(rows // tt,),
        in_specs=[_tile_spec(tt, w, ci) for (_, w, ci) in tiles] + [_full_spec(p.shape) for p in params]
        + [_tile_spec(tt, w, ci) for (_, w, ci) in flat_cots],
        out_specs=[_tile_spec(tt, w, 0) for w in dt_w] + [_full_spec(s) for s in dp_shapes],
        out_shape=[jax.ShapeDtypeStruct((rows, w), f32) for w in dt_w] + [jax.ShapeDtypeStruct(s, f32) for s in dp_shapes],
        compiler_params=pltpu.CompilerParams(dimension_semantics=("arbitrary",)),
        name=name,
    )(*[t[0] for t in tiles], *params, *[c[0] for c in flat_cots])
    return list(res[:ndt]), list(res[ndt:])


def _whole(a):
    return (a, a.shape[1], 0)


def _matmul(name, a, b, mode, out_dtype=f32, tm=1024, tn=1024, tk=1024, shards=1):
    S = shards
    if mode == "nn":
        M, K = a.shape
        w = b.shape[-1]
    elif mode == "nt":
        M = a.shape[0]
        if S > 1:
            _, N, w = b.shape
            K = S * w
        else:
            N, K = b.shape
            w = K
    else:
        K, M = a.shape
        w = b.shape[1] // S
    if mode != "nt":
        N = S * w
    tm = _pick(M, tm, LANES)
    if mode == "nt":
        tn = _pick(N, tn, LANES)
        tk = _pick(w, tk, LANES)
    else:
        tn = _pick(w, tn, LANES)
        tk = _pick(K, tk, LANES if mode == "nn" else 16)
    nk = K // tk
    nb = w // (tk if mode == "nt" else tn)
    if mode == "nn":
        a_spec = pl.BlockSpec((tm, tk), lambda i, j, k: (i, k))
        if S > 1:
            b_spec = pl.BlockSpec((1, tk, tn), lambda i, j, k: (j // nb, k, j % nb))
        else:
            b_spec = pl.BlockSpec((tk, tn), lambda i, j, k: (k, j))
        dims = (((1,), (0,)), ((), ()))
    elif mode == "nt":
        a_spec = pl.BlockSpec((tm, tk), lambda i, j, k: (i, k))
        if S > 1:
            b_spec = pl.BlockSpec((1, tn, tk), lambda i, j, k: (k // nb, j, k % nb))
        else:
            b_spec = pl.BlockSpec((tn, tk), lambda i, j, k: (j, k))
        dims = (((1,), (1,)), ((), ()))
    else:
        a_spec = pl.BlockSpec((tk, tm), lambda i, j, k: (k, i))
        b_spec = pl.BlockSpec((tk, tn), lambda i, j, k: (k, j))
        dims = (((0,), (0,)), ((), ()))
    if mode == "tn" and S > 1:
        o_spec = pl.BlockSpec((1, tm, tn), lambda i, j, k: (j // nb, i, j % nb))
        o_shape = (S, M, w)
    else:
        o_spec = pl.BlockSpec((tm, tn), lambda i, j, k: (i, j))
        o_shape = (M, N)
    b_lead = S > 1 and mode != "tn"
    o_lead = S > 1 and mode == "tn"

    def body(a_ref, b_ref, o_ref, acc_ref):
        k = pl.program_id(2)

        @pl.when(k == 0)
        def _():
            acc_ref[...] = jnp.zeros_like(acc_ref)

        bv = b_ref[0] if b_lead else b_ref[...]
        acc_ref[...] += lax.dot_general(a_ref[...].astype(bf16), bv.astype(bf16), dims, preferred_element_type=f32)

        @pl.when(k == nk - 1)
        def _():
            if o_lead:
                o_ref[0] = acc_ref[...].astype(o_ref.dtype)
            else:
                o_ref[...] = acc_ref[...].astype(o_ref.dtype)

    return pl.pallas_call(
        body, grid=(M // tm, N // tn, nk),
        in_specs=[a_spec, b_spec],
        out_specs=o_spec,
        out_shape=jax.ShapeDtypeStruct(o_shape, out_dtype),
        scratch_shapes=[pltpu.VMEM((tm, tn), f32)],
        compiler_params=pltpu.CompilerParams(dimension_semantics=("parallel", "parallel", "arbitrary")),
        name=name,
    )(a, b)


def _make_bmm(precision):
    if precision is None:
        cast, kw = (lambda v: v.astype(bf16)), {}
    else:
        cast, kw = (lambda v: v), {"precision": precision}

    def nn(a, b):
        return jnp.einsum("hij,hjk->hik", cast(a), cast(b), preferred_element_type=f32, **kw)

    def nt(a, b):
        return jnp.einsum("hik,hjk->hij", cast(a), cast(b), preferred_element_type=f32, **kw)

    def tn(a, b):
        return jnp.einsum("hki,hkj->hij", cast(a), cast(b), preferred_element_type=f32, **kw)

    if precision is not None:
        return nn, nt, tn
    nn_v, nt_v, tn_v = jax.custom_vjp(nn), jax.custom_vjp(nt), jax.custom_vjp(tn)
    keep = lambda f: (lambda a, b: (f(a, b), (a, b)))
    nn_v.defvjp(keep(nn), lambda r, ct: (nt(ct, r[1]), tn(r[0], ct)))
    nt_v.defvjp(keep(nt), lambda r, ct: (nn(ct, r[1]), tn(ct, r[0])))
    tn_v.defvjp(keep(tn), lambda r, ct: (nt(r[1], ct), nn(r[0], ct)))
    return nn_v, nt_v, tn_v


_bmm, _bmm_nt, _bmm_tn = _make_bmm(None)
_bmm_exact = _make_bmm(lax.Precision.HIGHEST)[0]


def _masks(n):
    r = lax.broadcasted_iota(jnp.int32, (n, n), 0)
    c = lax.broadcasted_iota(jnp.int32, (n, n), 1)
    return (r >= c)[None], (r > c)[None], (r == c)[None]


_INV_BLOCK = 8


def _nilpotent_inverse(m, eye):
    p = eye + m
    for _ in range(2):
        m = _bmm(m, m)
        p = p + _bmm(p, m)
    return p


def _neumann_inverse(m, eye, n):
    assert n == _INV_BLOCK * _INV_BLOCK
    r = lax.broadcasted_iota(jnp.int32, (n, n), 0) // _INV_BLOCK
    c = lax.broadcasted_iota(jnp.int32, (n, n), 1) // _INV_BLOCK
    inside = jnp.where((r == c)[None], m, 0.0)
    d_inv = _nilpotent_inverse(inside, eye)
    return _bmm(_nilpotent_inverse(_bmm(d_inv, m - inside), eye), d_inv)


def _gdn_chunk(s, q, k, v, beta, gc, gr, gl):
    n = q.shape[1]
    causal, strict, diag = _masks(n)
    eye = diag.astype(f32)
    decay = jnp.where(causal, jnp.exp(jnp.where(causal, gc - gr, 0.0)), 0.0)
    kb = k * beta
    vb = v * beta
    lower = jnp.where(strict, _bmm_nt(kb, k) * decay, 0.0)
    t_mat = _neumann_inverse(-lower, eye, n)
    egc = jnp.exp(gc)
    u = _bmm(t_mat, vb)
    w = _bmm(t_mat, kb * egc)
    attn = jnp.where(causal, _bmm_nt(q, k) * decay, 0.0)
    v_new = u - _bmm(w, s)
    o = _bmm(q * egc, s) + _bmm(attn, v_new)
    k_dec = k * jnp.exp(gl - gc)
    s_new = s * jnp.exp(gl) + _bmm_tn(k_dec, v_new)
    return s_new, o


def _rwkv_chunk(s, r, lw, k, v, a, b):
    n = r.shape[1]
    causal, strict, diag = _masks(n)
    eye = diag.astype(f32)
    tri = jnp.broadcast_to(causal.astype(f32), (r.shape[0], n, n))
    lc = _bmm_exact(tri, lw)
    ein = jnp.exp(lc)
    eout = jnp.exp(-lc)
    a_t = a * jnp.exp(lc - lw)
    b_t = b * eout
    k_t = k * eout
    r_t = r * ein
    a_ab = jnp.where(strict, _bmm_nt(a_t, b_t), 0.0)
    a_ak = jnp.where(strict, _bmm_nt(a_t, k_t), 0.0)
    u = _bmm(_neumann_inverse(a_ab, eye, n), _bmm_nt(a_t, s) + _bmm(a_ak, v))
    y = (_bmm_nt(r_t, s) + _bmm(jnp.where(causal, _bmm_nt(r_t, b_t), 0.0), u)
         + _bmm(jnp.where(causal, _bmm_nt(r_t, k_t), 0.0), v))
    e_last = jnp.exp(jnp.sum(lw, axis=1, keepdims=True))
    s_new = s * e_last + _bmm_tn(u, b_t * e_last) + _bmm_tn(v, k_t * e_last)
    return s_new, y


def _chunk_spec(shape, n, reverse):
    blk = (1,) + tuple(shape[1:])
    if reverse:
        return pl.BlockSpec(blk, lambda i: (n - 1 - i, 0, 0, 0))
    return pl.BlockSpec(blk, lambda i: (i, 0, 0, 0))


def _scan_fwd(name, fn, ins, out_dim):
    n, h = ins[0].shape[0], ins[0].shape[1]
    c = ins[0].shape[2]
    nin = len(ins)

    def body(*refs):
        in_refs, o_ref, sh_ref, s_scr = refs[:nin], refs[nin], refs[nin + 1], refs[nin + 2]

        @pl.when(pl.program_id(0) == 0)
        def _():
            s_scr[...] = jnp.zeros_like(s_scr)

        s = s_scr[...]
        sh_ref[0] = s
        s_new, o = fn(s, *[r[0] for r in in_refs])
        o_ref[0] = o
        s_scr[...] = s_new

    oshape = (n, h, c, out_dim)
    sshape = (n, h, HEAD_DIM, HEAD_DIM)
    return pl.pallas_call(
        body, grid=(n,),
        in_specs=[_chunk_spec(a.shape, n, False) for a in ins],
        out_specs=[_chunk_spec(oshape, n, False), _chunk_spec(sshape, n, False)],
        out_shape=[jax.ShapeDtypeStruct(oshape, f32), jax.ShapeDtypeStruct(sshape, f32)],
        scratch_shapes=[pltpu.VMEM((h, HEAD_DIM, HEAD_DIM), f32)],
        compiler_params=pltpu.CompilerParams(dimension_semantics=("arbitrary",)),
        name=name,
    )(*ins)


def _scan_bwd(name, fn, ins, s_hist, d_out):
    n = ins[0].shape[0]
    nin = len(ins)

    def body(*refs):
        in_refs = refs[:nin]
        sh_ref, do_ref = refs[nin], refs[nin + 1]
        g_refs = refs[nin + 2:nin + 2 + nin]
        ds_scr = refs[nin + 2 + nin]

        @pl.when(pl.program_id(0) == 0)
        def _():
            ds_scr[...] = jnp.zeros_like(ds_scr)

        _, vjp = jax.vjp(fn, sh_ref[0], *[r[0] for r in in_refs])
        g = vjp((ds_scr[...], do_ref[0]))
        ds_scr[...] = g[0]
        for r, v in zip(g_refs, g[1:]):
            r[0] = v

    return pl.pallas_call(
        body, grid=(n,),
        in_specs=[_chunk_spec(a.shape, n, True) for a in ins]
        + [_chunk_spec(s_hist.shape, n, True), _chunk_spec(d_out.shape, n, True)],
        out_specs=[_chunk_spec(a.shape, n, True) for a in ins],
        out_shape=[jax.ShapeDtypeStruct(a.shape, f32) for a in ins],
        scratch_shapes=[pltpu.VMEM(s_hist.shape[1:], f32)],
        compiler_params=pltpu.CompilerParams(dimension_semantics=("arbitrary",)),
        name=name,
    )(*ins, s_hist, d_out)


def _to_chunks(a, n):
    return a.reshape(n, CHUNK, HEADS, a.shape[1] // HEADS).transpose(0, 2, 1, 3)


def _from_chunks(a):
    n, h, c, d = a.shape
    return a.transpose(0, 2, 1, 3).reshape(n * c, h * d)


def _fn_norm_mod(ps, ts):
    nw, shift, scale = ps
    (x,) = ts
    return [_rms(x, nw, NORM_EPS) * (1.0 + scale) + shift]


def _fn_resid_norm_mod(ps, ts):
    gate, nw, shift, scale = ps
    x, mo = ts
    x1 = x + gate * mo
    return [x1, _rms(x1, nw, NORM_EPS) * (1.0 + scale) + shift]


def _fn_gdn_pre(ps, ts):
    cw = ps[:12]
    alog, dtb, seg, tri = ps[12:]
    ba = ts[12]
    outs = []
    for part in range(3):
        x = ts[4 * part:4 * part + 4]
        w = cw[4 * part:4 * part + 4]
        conv = w[3] * x[0] + w[2] * x[1] + w[1] * x[2] + w[0] * x[3]
        u = _silu(conv)
        if part < 2:
            u = u * lax.rsqrt(_segsum(u * u, seg) + 1e-6)
            if part == 0:
                u = u * (HEAD_DIM ** -0.5)
        outs.append(u)
    beta = jax.nn.sigmoid(ba)
    g = -jnp.exp(alog) * _softplus(ba + dtb)
    gc = _xdot_l(tri, g)
    return outs + [beta, gc]


def _fn_gdn_post(ps, ts):
    ow, seg = ps
    o, z = ts
    ms = _segsum(o * o, seg) * (1.0 / HEAD_DIM)
    return [o * lax.rsqrt(ms + NORM_EPS) * ow * _silu(z)]


def _fn_rwkv_pre(ps, ts):
    mu_r, mu_k, mu_v, mu_l, mu_g, w0, w2p, a0, a2p, g2p, k_k, k_a, seg = ps
    r0, k0, v0, l0, g0, r1, k1, v1, l1, g1 = ts
    xr = r0 + (r1 - r0) * mu_r
    xk = k0 + (k1 - k0) * mu_k
    xv = v0 + (v1 - v0) * mu_v
    xl = l0 + (l1 - l0) * mu_l
    xg = g0 + (g1 - g0) * mu_g
    w = -_softplus(-(w0 + _bdot(jnp.tanh(xl), w2p))) - 0.5
    lw = -jnp.exp(w)
    a = jax.nn.sigmoid(a0 + _bdot(xl, a2p))
    g = _bdot(jax.nn.sigmoid(xg), g2p)
    kk = xk * k_k
    kk = kk * lax.rsqrt(_segsum(kk * kk, seg) + 1e-6)
    k2 = xk * (1.0 + (a - 1.0) * k_a)
    return [xr, lw, k2, xv, -kk, kk * a, g]


def _fn_rwkv_post(ps, ts):
    lw_, lb_, rk, seg = ps
    y, r, k2, v, g = ts
    inv = 1.0 / HEAD_DIM
    yc = y - _segsum(y, seg) * inv
    var = _segsum(yc * yc, seg) * inv
    yn = yc * lax.rsqrt(var + LNX_EPS) * lw_ + lb_
    bonus = _segsum(r * k2 * rk, seg) * v
    return [(yn + bonus) * g]


def _fn_merge(ps, ts):
    gla, glb, ya, yb = ts
    return [jax.nn.sigmoid(gla) * ya + jax.nn.sigmoid(glb) * yb]


def _fn_convglu(ps, ts):
    c0, c1, c2 = ps
    g0, g1, g2, up = ts
    return [_silu(c2 * g0 + c1 * g1 + c0 * g2) * up]


def _fn_add(ps, ts):
    acc = ts[0]
    for t in ts[1:]:
        acc = acc + t
    return [acc]


def _add_n(name, arrs, tt):
    return _stage_fwd(name, _fn_add, [_whole(a) for a in arrs], [], [(arrs[0].shape[1], f32)], tt)[0]


def _final_stage(x1, fo, tgt, gate2, nfw, tt):
    rows, d = x1.shape

    def loss_fn(gate, nw, xa, fa, tg):
        y = _rms(xa + gate * fa, nw, NORM_EPS)
        err = (y - tg) ** 2
        return 0.5 * jnp.sum(jnp.mean(err, axis=-1, keepdims=True), axis=0, keepdims=True)

    def body(x_ref, f_ref, t_ref, g_ref, w_ref, dx_ref, df_ref, dg_ref, dw_ref, l_ref):
        i = pl.program_id(0)
        args = (g_ref[...], w_ref[...], x_ref[...], f_ref[...])
        tg = t_ref[...]
        lv, vjp = jax.vjp(lambda g, w, xa, fa: loss_fn(g, w, xa, fa, tg), *args)
        dg, dw, dx, df = vjp(jnp.ones((1, 1), f32))
        dx_ref[...] = dx
        df_ref[...] = df

        @pl.when(i == 0)
        def _():
            dg_ref[...] = jnp.zeros_like(dg_ref)
            dw_ref[...] = jnp.zeros_like(dw_ref)
            l_ref[...] = jnp.zeros_like(l_ref)

        dg_ref[...] += dg
        dw_ref[...] += dw
        l_ref[...] += jnp.broadcast_to(lv, l_ref.shape)

    row = pl.BlockSpec((tt, d), lambda i: (i, 0))
    vec = pl.BlockSpec((1, d), lambda i: (0, 0))
    return pl.pallas_call(
        body, grid=(rows // tt,),
        in_specs=[row, row, row, vec, vec],
        out_specs=[row, row, vec, vec, pl.BlockSpec((1, LANES), lambda i: (0, 0))],
        out_shape=[jax.ShapeDtypeStruct((rows, d), f32)] * 2 + [jax.ShapeDtypeStruct((1, d), f32)] * 2
        + [jax.ShapeDtypeStruct((1, LANES), f32)],
        compiler_params=pltpu.CompilerParams(dimension_semantics=("arbitrary",)),
        name="loss_head",
    )(x1, fo, tgt, gate2, nfw)


def _ada_fwd(c_all, w_shard, b_cols):
    def body(c_ref, w_ref, b_ref, cond_ref, mod_ref):
        cond = _silu(c_ref[...])
        cond_ref[...] = cond
        mod_ref[...] = jnp.dot(cond.astype(bf16), w_ref[...].astype(bf16), preferred_element_type=f32) + b_ref[...]

    n = w_shard.shape[1]
    return pl.pallas_call(
        body, out_shape=[jax.ShapeDtypeStruct(c_all.shape, f32), jax.ShapeDtypeStruct((c_all.shape[0], n), f32)],
        name="ada_fwd",
    )(c_all, w_shard, b_cols)


def _adamw(name, w, g, m, v):
    rows, width = w.shape
    tt = _pick(rows, 128, 8)
    c1 = 1.0 - ADAM_B1 ** ADAM_STEP
    c2 = 1.0 - ADAM_B2 ** ADAM_STEP

    def body(w_ref, g_ref, m_ref, v_ref, d_ref, mo_ref, vo_ref):
        gg = g_ref[...]
        mn = ADAM_B1 * m_ref[...] + (1.0 - ADAM_B1) * gg
        vn = ADAM_B2 * v_ref[...] + (1.0 - ADAM_B2) * (gg * gg)
        m_hat = mn / c1
        v_hat = vn / c2
        d_ref[...] = -ADAM_LR * (m_hat / (jnp.sqrt(v_hat) + ADAM_EPS) + ADAM_WD * w_ref[...])
        mo_ref[...] = mn
        vo_ref[...] = vn

    spec = pl.BlockSpec((tt, width), lambda i: (i, 0))
    return pl.pallas_call(
        body, grid=(rows // tt,), in_specs=[spec] * 4, out_specs=[spec] * 3,
        out_shape=[jax.ShapeDtypeStruct((rows, width), f32)] * 3,
        compiler_params=pltpu.CompilerParams(dimension_semantics=("parallel",)),
        name=name,
    )(w, g, m, v)


def _place():
    return lax.axis_index("x"), lax.axis_index("y"), lax.axis_index("c")


def _ag8(name, blk):
    m, w = blk.shape
    flips = [(dx, dy, dc) for dx in (0, 1) for dy in (0, 1) for dc in (0, 1)][1:]

    def body(x_ref, out_ref, send_sems, recv_sems, local_sem):
        x, y, c = _place()
        me = 4 * x + 2 * y + c
        mine = pltpu.make_async_copy(x_ref, out_ref.at[me], local_sem)
        mine.start()
        peers = [((1 - x) if dx else x, (1 - y) if dy else y, (1 - c) if dc else c) for dx, dy, dc in flips]
        sends = []
        for k, peer in enumerate(peers):
            cp = pltpu.make_async_remote_copy(src_ref=x_ref, dst_ref=out_ref.at[me], send_sem=send_sems.at[k],
                                              recv_sem=recv_sems.at[k], device_id=peer, device_id_type=MESH_ID)
            cp.start()
            sends.append(cp)
        for k, (px, py, pc) in enumerate(peers):
            pltpu.make_async_remote_copy(src_ref=x_ref, dst_ref=out_ref.at[4 * px + 2 * py + pc],
                                         send_sem=send_sems.at[k], recv_sem=recv_sems.at[k],
                                         device_id=(px, py, pc), device_id_type=MESH_ID).wait_recv()
        for cp in sends:
            cp.wait_send()
        mine.wait()

    return pl.pallas_call(
        body, out_shape=jax.ShapeDtypeStruct((8, m, w), blk.dtype),
        in_specs=[pl.BlockSpec(memory_space=pltpu.VMEM)], out_specs=pl.BlockSpec(memory_space=pltpu.VMEM),
        scratch_shapes=[pltpu.SemaphoreType.DMA((7,)), pltpu.SemaphoreType.DMA((7,)), pltpu.SemaphoreType.DMA],
        name=name,
    )(blk)


def _other_chips(x, y):
    return [(1 - x, y), (x, 1 - y), (1 - x, 1 - y)]


_ANY = pl.BlockSpec(memory_space=pl.ANY)


def _rcopy(src, dst, send_sems, recv_sems, k, dev):
    return pltpu.make_async_remote_copy(src_ref=src, dst_ref=dst, send_sem=send_sems.at[k], recv_sem=recv_sems.at[k],
                                        device_id=dev, device_id_type=MESH_ID)


def _ag4_multi(ws):
    n = len(ws)

    def body(*refs):
        w_refs, out_refs = refs[:n], refs[n:2 * n]
        send_sems, recv_sems = refs[2 * n:]
        x, y, c = _place()
        chip = 2 * x + y
        sibling = (x, y, 1 - c)
        chips = _other_chips(x, y)
        halves = [(pl.ds(c * (w.shape[0] // 2), w.shape[0] // 2), pl.ds((1 - c) * (w.shape[0] // 2), w.shape[0] // 2))
                  for w in ws]
        copies = []
        for t in range(n):
            for k, (px, py) in enumerate(chips):
                cp = _rcopy(w_refs[t].at[halves[t][0]], out_refs[t].at[chip, halves[t][0]], send_sems, recv_sems,
                            7 * t + k, (px, py, c))
                cp.start()
                copies.append(cp)
        for t in range(n):
            own = _rcopy(w_refs[t], out_refs[t].at[chip], send_sems, recv_sems, 7 * t + 6, sibling)
            own.start()
            copies.append(own)
        for t in range(n):
            for k, (px, py) in enumerate(chips):
                blk = out_refs[t].at[2 * px + py, halves[t][0]]
                _rcopy(blk, blk, send_sems, recv_sems, 7 * t + k, (px, py, c)).wait_recv()
                fw = _rcopy(blk, blk, send_sems, recv_sems, 7 * t + 3 + k, sibling)
                fw.start()
                copies.append(fw)
        for t in range(n):
            for k, (px, py) in enumerate(chips):
                blk = out_refs[t].at[2 * px + py, halves[t][1]]
                _rcopy(blk, blk, send_sems, recv_sems, 7 * t + 3 + k, sibling).wait_recv()
            _rcopy(w_refs[t], out_refs[t].at[chip], send_sems, recv_sems, 7 * t + 6, sibling).wait_recv()
        for cp in copies:
            cp.wait_send()

    return pl.pallas_call(
        body, out_shape=[jax.ShapeDtypeStruct((4,) + w.shape, w.dtype) for w in ws],
        in_specs=[_ANY] * n, out_specs=[_ANY] * n,
        scratch_shapes=[pltpu.SemaphoreType.DMA((7 * n,)), pltpu.SemaphoreType.DMA((7 * n,))],
        name="weights_all_gather",
    )(*ws)


def _rs_sibling_swap(gs):
    n = len(gs)

    def body(*refs):
        g_refs, recv_refs = refs[:n], refs[n:2 * n]
        send_sems, recv_sems = refs[2 * n:]
        x, y, c = _place()
        copies = []
        for t in range(n):
            rh = gs[t].shape[1] // 2
            for s_ in range(4):
                cp = _rcopy(g_refs[t].at[s_, pl.ds((1 - c) * rh, rh)], recv_refs[t].at[s_], send_sems, recv_sems,
                            4 * t + s_, (x, y, 1 - c))
                cp.start()
                copies.append(cp)
        for cp in copies:
            cp.wait_recv()
        for cp in copies:
            cp.wait_send()

    return pl.pallas_call(
        body, out_shape=[jax.ShapeDtypeStruct((4, g.shape[1] // 2, g.shape[2]), g.dtype) for g in gs],
        in_specs=[_ANY] * n, out_specs=[_ANY] * n,
        scratch_shapes=[pltpu.SemaphoreType.DMA((4 * n,)), pltpu.SemaphoreType.DMA((4 * n,))],
        name="grads_sibling_swap",
    )(*gs)


def _rs_chip_exchange(ps):
    n = len(ps)

    def body(*refs):
        p_refs, recv_refs = refs[:n], refs[n:2 * n]
        send_sems, recv_sems = refs[2 * n:]
        x, y, c = _place()
        chips = _other_chips(x, y)
        copies = []
        for t in range(n):
            for k, (px, py) in enumerate(chips):
                cp = _rcopy(p_refs[t].at[2 * px + py], recv_refs[t].at[k], send_sems, recv_sems, 3 * t + k,
                            (px, py, c))
                cp.start()
                copies.append(cp)
        for cp in copies:
            cp.wait_recv()
        for cp in copies:
            cp.wait_send()

    return pl.pallas_call(
        body, out_shape=[jax.ShapeDtypeStruct((3,) + p.shape[1:], p.dtype) for p in ps],
        in_specs=[_ANY] * n, out_specs=[_ANY] * n,
        scratch_shapes=[pltpu.SemaphoreType.DMA((3 * n,)), pltpu.SemaphoreType.DMA((3 * n,))],
        name="grads_chip_exchange",
    )(*ps)


_JOIN_PIECES = 4


def _rs_sibling_join(qs):
    n = len(qs)
    npc = _JOIN_PIECES

    def body(*refs):
        q_refs, out_refs = refs[:n], refs[n:2 * n]
        send_sems, recv_sems = refs[2 * n:]
        x, y, c = _place()
        copies = []
        for t in range(n):
            rh = qs[t].shape[0] // 2
            pr = rh // npc
            for i in range(npc):
                rows = pl.ds(c * rh + i * pr, pr)
                cp = _rcopy(q_refs[t].at[rows], out_refs[t].at[rows], send_sems, recv_sems, npc * t + i,
                            (x, y, 1 - c))
                cp.start()
                copies.append(cp)
        for t in range(n):
            rh = qs[t].shape[0] // 2
            pr = rh // npc
            for i in range(npc):
                rows = pl.ds((1 - c) * rh + i * pr, pr)
                _rcopy(q_refs[t].at[rows], out_refs[t].at[rows], send_sems, recv_sems, npc * t + i,
                       (x, y, 1 - c)).wait_recv()
        for cp in copies:
            cp.wait_send()

    return pl.pallas_call(
        body, out_shape=[jax.ShapeDtypeStruct(q.shape, q.dtype) for q in qs],
        in_specs=[_ANY] * n, out_specs=[_ANY] * n, input_output_aliases={t: t for t in range(n)},
        scratch_shapes=[pltpu.SemaphoreType.DMA((npc * n,)), pltpu.SemaphoreType.DMA((npc * n,))],
        name="grads_sibling_join",
    )(*qs)


def _add_half(name, g, recv, ci):
    S, r, w = g.shape
    rh = r // 2
    tt = _pick(rh, 256, 16)
    nb = rh // tt

    def body(c_ref, a_ref, b_ref, o_ref, ob_ref):
        v = a_ref[...] + b_ref[...]
        o_ref[...] = v
        ob_ref[...] = v.astype(bf16)

    blk = pl.BlockSpec((1, tt, w), lambda s_, i, cr: (s_, i, 0))
    grid_spec = pltpu.PrefetchScalarGridSpec(
        num_scalar_prefetch=1, grid=(S, nb),
        in_specs=[pl.BlockSpec((1, tt, w), lambda s_, i, cr: (s_, cr[0] * nb + i, 0)), blk],
        out_specs=[blk, blk])
    return pl.pallas_call(body, grid_spec=grid_spec,
                          out_shape=[jax.ShapeDtypeStruct((S, rh, w), f32), jax.ShapeDtypeStruct((S, rh, w), bf16)],
                          name=name)(ci.reshape(1).astype(jnp.int32), g, recv)


def _sum_chip(name, pair, others, chip, ci):
    _, rh, w = pair.shape
    tt = _pick(rh, 128, 16)
    nb = rh // tt

    def body(chip_ref, core_ref, a_ref, b_ref, o_ref):
        o_ref[...] = ((a_ref[0] + b_ref[0].astype(f32)) + b_ref[1].astype(f32)) + b_ref[2].astype(f32)

    grid_spec = pltpu.PrefetchScalarGridSpec(
        num_scalar_prefetch=2, grid=(nb,),
        in_specs=[pl.BlockSpec((1, tt, w), lambda i, ch, co: (ch[0], i, 0)),
                  pl.BlockSpec((3, tt, w), lambda i, ch, co: (0, i, 0))],
        out_specs=pl.BlockSpec((tt, w), lambda i, ch, co: (co[0] * nb + i, 0)))
    return pl.pallas_call(body, grid_spec=grid_spec, out_shape=jax.ShapeDtypeStruct((2 * rh, w), f32),
                          name=name)(chip.reshape(1).astype(jnp.int32), ci.reshape(1).astype(jnp.int32), pair, others)


def _sum_devices(gathered, head_row):
    _, rows, width = gathered.shape

    def body(g_ref, out_ref, head_ref):
        acc = g_ref[0]
        for d in range(1, 8):
            acc = acc + g_ref[d]
        out_ref[...] = acc
        row = acc[head_row:head_row + 1, :]
        hs = row[:, 0:HEAD_DIM]
        for h in range(1, HEADS):
            hs = hs + row[:, h * HEAD_DIM:(h + 1) * HEAD_DIM]
        head_ref[...] = jnp.zeros_like(head_ref)
        head_ref[0:1, 0:HEAD_DIM] = hs

    return pl.pallas_call(
        body, out_shape=[jax.ShapeDtypeStruct((rows, width), f32), jax.ShapeDtypeStruct((8, LANES), f32)],
        name="small_grads_sum",
    )(gathered)


def _pack(arrs, rows_mult, dtype):
    flat = jnp.concatenate([a.reshape(-1).astype(dtype) for a in arrs])
    per = PACK_W * rows_mult
    total = -(-flat.shape[0] // per) * per
    return jnp.pad(flat, (0, total - flat.shape[0])).reshape(total // PACK_W, PACK_W)


def _unpack(buf, shapes):
    flat = buf.reshape(-1)
    out, off = [], 0
    for s in shapes:
        n = int(np.prod(s))
        out.append(flat[off:off + n].reshape(s))
        off += n
    return out


_BIG = ["w_in", "w_branch_gdn", "w_branch_rwkv", "w_out", "w_ffn_in", "w_ffn_out"]
_MID = ["conv_gdn", "conv_ffn", "w2", "a2", "g2"]
_SMALL =["b_ada", "norm1_w", "a_log", "dt_bias", "onorm_gdn", "mu_rwkv", "w0", "a0", "k_k", "k_a", "r_k", "lnx_w",
          "lnx_b", "norm2_w", "norm_f_w"]
_ORDER = ["w_ada", "b_ada", "norm1_w", "w_in", "conv_gdn", "a_log", "dt_bias", "onorm_gdn", "w_branch_gdn", "mu_rwkv",
          "w0", "w2", "a0", "a2", "g2", "k_k", "k_a", "r_k", "lnx_w", "lnx_b", "w_branch_rwkv", "w_out", "norm2_w",
          "w_ffn_in", "conv_ffn", "w_ffn_out", "norm_f_w"]


def _win_pad(w):
    z = lambda n: jnp.zeros((w.shape[0], n), w.dtype)
    return jnp.concatenate([w[:, 0:2048], w[:, 2064:3728], w[:, 2048:2064], z(112), w[:, 3728:3888], z(96),
                            w[:, 3888:5936]], axis=1)


def _win_unpad(g):
    return jnp.concatenate([g[:, 0:2048], g[:, 3712:3728], g[:, 2048:3712], g[:, 3840:4000], g[:, 4096:6144]], axis=1)


def kernel(x, c, w_ada, b_ada, norm1_w, w_in, conv_gdn, a_log, dt_bias, onorm_gdn, w_branch_gdn, mu_rwkv, w0, w2, a0, a2, g2, k_k, k_a, r_k, lnx_w, lnx_b, w_branch_rwkv, w_out, norm2_w, w_ffn_in, conv_ffn, w_ffn_out, norm_f_w, loss_target, m_w_ada, m_b_ada, m_norm1_w, m_w_in, m_conv_gdn, m_a_log, m_dt_bias, m_onorm_gdn, m_w_branch_gdn, m_mu_rwkv, m_w0, m_w2, m_a0, m_a2, m_g2, m_k_k, m_k_a, m_r_k, m_lnx_w, m_lnx_b, m_w_branch_rwkv, m_w_out, m_norm2_w, m_w_ffn_in, m_conv_ffn, m_w_ffn_out, m_norm_f_w, v_w_ada, v_b_ada, v_norm1_w, v_w_in, v_conv_gdn, v_a_log, v_dt_bias, v_onorm_gdn, v_w_branch_gdn, v_mu_rwkv, v_w0, v_w2, v_a0, v_a2, v_g2, v_k_k, v_k_a, v_r_k, v_lnx_w, v_lnx_b, v_w_branch_rwkv, v_w_out, v_norm2_w, v_w_ffn_in, v_conv_ffn, v_w_ffn_out, v_norm_f_w):
    args = dict(locals())
    W = {n: args[n] for n in _ORDER}
    Mo = {n: args["m_" + n] for n in _ORDER}
    Vo = {n: args["v_" + n] for n in _ORDER}
    shapes = {n: W[n].shape for n in _ORDER}
    sq = lambda a: a.reshape(a.shape[-2:]) if a.ndim == 3 else a.reshape(1, -1)
    row = lambda a: a.reshape(1, -1)

    xi, yi, ci = lax.axis_index("x"), lax.axis_index("y"), lax.axis_index("c")
    dev = 4 * xi + 2 * yi + ci
    chip = 2 * xi + yi

    x2 = x[0]
    tgt = loss_target[0]
    T, D = x2.shape
    N = T // CHUNK
    tt_l = _pick(T, 256, CHUNK)
    tt_h = _pick(T, 128, CHUNK)

    mid_shapes = [shapes[n][1:] for n in _MID]
    small_blk = _pack([c] + [W[n] for n in _MID], 8, f32)
    small_all = _ag8("gather_c_mid", small_blk)
    c_all = small_all[:, 0, :]
    per_chip = small_all[0::2].reshape(4, -1)[:, D:]
    mid = [dict(zip(_MID, _unpack(per_chip[j], mid_shapes))) for j in range(4)]
    catm = lambda n: jnp.concatenate([mid[j][n] for j in range(4)], axis=1)
    conv_gdn_f, conv_ffn_f = catm("conv_gdn"), catm("conv_ffn")
    w2f, a2f, g2f = catm("w2"), catm("a2"), catm("g2")

    win_s, wbg_s, wbr_s, wout_s, wfi_s, wfo_s = _ag4_multi([sq(W[n]).astype(bf16) for n in _BIG])
    win_p = _win_pad(jnp.concatenate([win_s[j] for j in range(4)], axis=1))
    wout_f = wout_s.reshape(D, D)
    wfo = wfo_s.reshape(D_FF, D)
    zpad = lambda a, top, bot: jnp.pad(a, ((top, bot), (0, 0)))
    w2p, a2p, g2p = zpad(w2f, 0, 64), zpad(a2f, 64, 0), zpad(g2f, 0, 96)

    ncol = shapes["w_ada"][2]
    b_cols = lax.dynamic_slice(sq(W["b_ada"]), (0, chip * ncol), (1, ncol))
    cond16, mod_cols = _ada_fwd(jnp.pad(c_all, ((0, 8), (0, 0))), sq(W["w_ada"]), b_cols)
    mod_all = _ag8("gather_mod", mod_cols[:8])
    mod_mine = lax.dynamic_slice(mod_all[0::2], (0, dev, 0), (4, 1, ncol)).reshape(1, 4 * ncol)
    shift1, scale1, gate1, shift2, scale2, gate2 = [mod_mine[:, i * D:(i + 1) * D] for i in range(6)]

    seg = _seg_matrix(WIDTH, HEAD_DIM)
    norm1 = [sq(W["norm1_w"]), shift1, scale1]
    h1 = _stage_fwd("norm_mod1", _fn_norm_mod, [_whole(x2)], norm1, [(D, bf16)], tt_l)[0]
    p = _matmul("in_proj", h1, win_p, "nn")
    qkv_s = [p[:, 0:1536]] + [_shift_down(p[:, 0:1536], s) for s in (1, 2, 3)]
    p_prev = _shift_down(p[:, 2048:4096], 1)

    cgq = [row(conv_gdn_f[j, part * WIDTH:(part + 1) * WIDTH]) for part in range(3) for j in range(4)]
    lane_pad = lambda a: jnp.pad(row(a), ((0, 0), (8, LANES - 16)))
    gdn_pre_ps = cgq + [lane_pad(W["a_log"]), lane_pad(W["dt_bias"]), seg, _chunk_tri(tt_h, CHUNK)]
    gdn_pre_ts = [(qkv_s[s], WIDTH, part) for part in range(3) for s in range(4)] + [(p, LANES, 29)]
    q_, k_, v_, beta_t, gc_t = _stage_fwd("gdn_pre", _fn_gdn_pre, gdn_pre_ts, gdn_pre_ps,
                                          [(WIDTH, f32)] * 3 + [(LANES, f32)] * 2, tt_h)
    heads_col = lambda a, lo: a[:, lo:lo + HEADS].reshape(N, CHUNK, HEADS).transpose(0, 2, 1)
    beta_c = heads_col(beta_t, 0)[..., None]
    gch = heads_col(gc_t, 8)
    gdn_ins = [_to_chunks(q_, N), _to_chunks(k_, N), _to_chunks(v_, N), beta_c, gch[..., None], gch[:, :, None, :],
               gch[:, :, CHUNK - 1:CHUNK, None]]
    o_ch, gdn_hist = _scan_fwd("gdn_scan", _gdn_chunk, gdn_ins, HEAD_DIM)
    o_ = _from_chunks(o_ch)
    ow512 = jnp.tile(row(W["onorm_gdn"]), (1, HEADS))
    gdn_post_ts = [_whole(o_), (p, WIDTH, 3)]
    ya = _stage_fwd("gdn_post", _fn_gdn_post, gdn_post_ts, [ow512, seg], [(WIDTH, bf16)], tt_l)[0]

    mu = sq(W["mu_rwkv"])
    rw_ps = [mu[:, 0:512], mu[:, 512:1024], mu[:, 1024:1536], mu[:, 1536:1664], jnp.pad(mu[:, 1664:1824], ((0, 0), (0, 96))),
             sq(W["w0"]), w2p, sq(W["a0"]), a2p, g2p, sq(W["k_k"]), sq(W["k_a"]), seg]
    rw_ts = [(p, WIDTH, 4), (p, WIDTH, 5), (p, WIDTH, 6), (p, LANES, 28), (p, 256, 15),
             (p_prev, WIDTH, 0), (p_prev, WIDTH, 1), (p_prev, WIDTH, 2), (p_prev, LANES, 12), (p_prev, 256, 7)]
    rw_out = _stage_fwd("rwkv_pre", _fn_rwkv_pre, rw_ts, rw_ps, [(WIDTH, f32)] * 7, tt_h)
    r_, lw_, k2_, vv_, na_, b_, g_ = rw_out
    rw_ins = [_to_chunks(a, N) for a in (r_, lw_, k2_, vv_, na_, b_)]
    y_ch, rw_hist = _scan_fwd("rwkv_scan", _rwkv_chunk, rw_ins, HEAD_DIM)
    y_ = _from_chunks(y_ch)
    rwp_ps = [sq(W["lnx_w"]), sq(W["lnx_b"]), row(W["r_k"]), seg]
    rwp_ts = [_whole(y_), _whole(r_), _whole(k2_), _whole(vv_), _whole(g_)]
    yb = _stage_fwd("rwkv_post", _fn_rwkv_post, rwp_ts, rwp_ps, [(WIDTH, bf16)], tt_l)[0]

    big_a = _matmul("branch_gdn", ya, wbg_s, "nn", shards=4)
    big_b = _matmul("branch_rwkv", yb, wbr_s, "nn", shards=4)
    merge_ts = [(p, D, 4), (p, D, 5), _whole(big_a), _whole(big_b)]
    merged = _stage_fwd("merge", _fn_merge, merge_ts, [], [(D, bf16)], tt_l)[0]
    mo = _matmul("out_proj", merged, wout_f, "nn")
    norm2 = [gate1, sq(W["norm2_w"]), shift2, scale2]
    x1, h2 = _stage_fwd("resid_norm_mod2", _fn_resid_norm_mod, [_whole(x2), _whole(mo)], norm2, [(D, f32), (D, bf16)], tt_l)
    f = _matmul("ffn_in", h2, wfi_s, "nn", shards=4, tm=512, tn=1408)
    gate_s = [f[:, 0:D_FF], _shift_down(f[:, 0:D_FF], 1), _shift_down(f[:, 0:D_FF], 2)]
    up = f[:, D_FF:]
    cg_ps = [row(conv_ffn_f[j]) for j in range(3)]
    cg_ts = [_whole(a) for a in gate_s] + [_whole(up)]
    act = _stage_fwd("convglu", _fn_convglu, cg_ts, cg_ps, [(D_FF, bf16)], tt_h)[0]
    fo = _matmul("ffn_out", act, wfo, "nn", tk=1408)

    dx1_a, dfo, dgate2, dnormf, loss_part = _final_stage(x1, fo, tgt, gate2, row(W["norm_f_w"]), tt_l)

    dact = _matmul("d_act", dfo, wfo, "nt", tm=512, tn=1408)
    g_wfo = _matmul("g_ffn_out", act, dfo, "tn", tm=1408)
    (dg0, dg1, dg2, dup), dcf = _stage_bwd("convglu_bwd", _fn_convglu, cg_ts, cg_ps, [[_whole(dact)]], tt_h,
                                           [True] * 4, [True] * 3)
    dgate = _add_n("d_gate", [dg0, _shift_up(dg1, 1), _shift_up(dg2, 2)], tt_l)
    df = jnp.concatenate([dgate, dup], axis=1)
    dh2 = _matmul("d_h2", df, wfi_s, "nt", shards=4, tm=512, tk=1408)
    g_wfi = _matmul("g_ffn_in", h2, df, "tn", shards=4, tn=1408)
    (dx_a, dmo), (dgate1, dnorm2, dshift2, dscale2) = _stage_bwd(
        "resid_norm_mod2_bwd", _fn_resid_norm_mod, [_whole(x2), _whole(mo)], norm2,
        [[_whole(dx1_a)], [_whole(dh2)]], tt_l, [True, True], [True] * 4)
    dmerged = _matmul("d_merged", dmo, wout_f, "nt")
    g_wout = _matmul("g_out_proj", merged, dmo, "tn")
    (dgla, dglb, dbig_a, dbig_b), _ = _stage_bwd("merge_bwd", _fn_merge, merge_ts, [], [[_whole(dmerged)]], tt_l,
                                                 [True] * 4, [])
    dya = _matmul("d_ya", dbig_a, wbg_s, "nt", shards=4)
    g_wbg = _matmul("g_branch_gdn", ya, dbig_a, "tn", shards=4)
    dyb = _matmul("d_yb", dbig_b, wbr_s, "nt", shards=4)
    g_wbr = _matmul("g_branch_rwkv", yb, dbig_b, "tn", shards=4)

    (dy_, dr_p, dk2_p, dv_p, dg_p), (dlnxw, dlnxb, drk) = _stage_bwd(
        "rwkv_post_bwd", _fn_rwkv_post, rwp_ts, rwp_ps, [[_whole(dyb)]], tt_l, [True] * 5, [True, True, True, False])
    d_rw = _scan_bwd("rwkv_scan_bwd", _rwkv_chunk, rw_ins, rw_hist, _to_chunks(dy_, N))
    dr_c, dlw_c, dk2_c, dv_c, dna_c, db_c = [_from_chunks(a) for a in d_rw]
    rw_cots = [[_whole(dr_p), _whole(dr_c)], [_whole(dlw_c)], [_whole(dk2_p), _whole(dk2_c)],
               [_whole(dv_p), _whole(dv_c)], [_whole(dna_c)], [_whole(db_c)], [_whole(dg_p)]]
    rw_dt, rw_dp = _stage_bwd("rwkv_pre_bwd", _fn_rwkv_pre, rw_ts, rw_ps, rw_cots, tt_h, [True] * 10,
                              [True] * 12 + [False])
    dmu_r, dmu_k, dmu_v, dmu_l, dmu_g, dw0, dw2p, da0, da2p, dg2p, dkk, dka = rw_dp
    d_rkv = [_add_n("d_p_rwkv%d" % i, [rw_dt[i], _shift_up(rw_dt[5 + i], 1)], tt_l) for i in range(5)]

    (do_, dz), (dow512,) = _stage_bwd("gdn_post_bwd", _fn_gdn_post, gdn_post_ts, [ow512, seg], [[_whole(dya)]], tt_l,
                                      [True, True], [True, False])
    d_gdn = _scan_bwd("gdn_scan_bwd", _gdn_chunk, gdn_ins, gdn_hist, _to_chunks(do_, N))
    dq_c, dk_c, dv_c2, dbeta_c, dgc_col, dgc_row, dgl = d_gdn
    back = lambda a, lo: jnp.pad(a.transpose(0, 2, 1).reshape(T, HEADS), ((0, 0), (lo, LANES - HEADS - lo)))
    dbeta_t = back(dbeta_c[..., 0], 0)
    dgc_1 = back(dgc_col[..., 0], 8)
    dgc_2 = back(dgc_row[:, :, 0, :], 8)
    dgc_3 = back(jnp.pad(dgl[:, :, 0, :], ((0, 0), (0, 0), (CHUNK - 1, 0))), 8)
    gdn_cots = [[_whole(_from_chunks(dq_c))], [_whole(_from_chunks(dk_c))], [_whole(_from_chunks(dv_c2))],
                [_whole(dbeta_t)], [_whole(dgc_1), _whole(dgc_2), _whole(dgc_3)]]
    gdn_dt, gdn_dp = _stage_bwd("gdn_pre_bwd", _fn_gdn_pre, gdn_pre_ts, gdn_pre_ps, gdn_cots, tt_h, [True] * 13,
                                [True] * 14 + [False, False])
    d_qkv = [_add_n("d_p_qkv%d" % part, [gdn_dt[4 * part]] + [_shift_up(gdn_dt[4 * part + s], s) for s in (1, 2, 3)], tt_l)
             for part in range(3)]
    dba = gdn_dt[12]

    dp = jnp.concatenate(d_qkv + [dz] + d_rkv[0:4] + [dba, d_rkv[4], dgla, dglb], axis=1)
    dh1 = _matmul("d_h1", dp, win_p, "nt")
    g_win = _win_unpad(_matmul("g_in_proj", h1, dp, "tn"))
    (dx_b,), (dnorm1, dshift1, dscale1) = _stage_bwd("norm_mod1_bwd", _fn_norm_mod, [_whole(x2)], norm1,
                                                     [[_whole(dh1)]], tt_l, [True], [True] * 3)
    grad_x = _add_n("grad_x", [dx_a, dx_b], tt_l)

    dmod = jnp.concatenate([dshift1, dscale1, dgate1, dshift2, dscale2, dgate2], axis=1)
    g_conv_gdn = jnp.concatenate([jnp.concatenate([gdn_dp[4 * part + j] for part in range(3)], axis=1) for j in range(4)], axis=0)
    g_conv_ffn = jnp.concatenate(dcf, axis=0)
    g_mu = jnp.concatenate([dmu_r, dmu_k, dmu_v, dmu_l, dmu_g[:, :160]], axis=1)
    small_parts = {"b_ada": dmod, "norm1_w": dnorm1, "a_log": gdn_dp[12][:, 8:16], "dt_bias": gdn_dp[13][:, 8:16],
                   "mu_rwkv": g_mu, "w0": dw0, "a0": da0, "k_k": dkk, "k_a": dka, "r_k": drk, "lnx_w": dlnxw,
                   "lnx_b": dlnxb, "norm2_w": dnorm2, "norm_f_w": dnormf}
    small_names = [n for n in _SMALL if n != "onorm_gdn"]
    mid_full = [g_conv_gdn, g_conv_ffn, dw2p[0:64], da2p[64:128], dg2p[0:160]]
    body_rows = _pack([small_parts[n] for n in small_names] + [loss_part[:, 0:1]] + mid_full, 1, f32)
    head_row = body_rows.shape[0]
    small_g = jnp.concatenate([body_rows, jnp.pad(dow512, ((0, 0), (0, PACK_W - WIDTH)))], axis=0)
    small_g = jnp.pad(small_g, ((0, -small_g.shape[0] % 8), (0, 0)))
    small_all_g = _ag8("gather_small_grads", small_g)
    small_sum, head_sum = _sum_devices(small_all_g, head_row)
    small_shapes = [shapes[n][1:] if n != "norm_f_w" else shapes[n] for n in small_names]
    un = _unpack(small_sum, small_shapes + [(1,)] + [g.shape for g in mid_full])
    small_grads = dict(zip(small_names, un))
    loss = un[len(small_names)].reshape(())
    small_grads["onorm_gdn"] = head_sum[0, 0:HEAD_DIM]
    for n, g in zip(_MID, un[len(small_names) + 1:]):
        wcols = shapes[n][2]
        small_grads[n] = lax.dynamic_slice(g, (0, chip * wcols), (g.shape[0], wcols))

    dmod_all = small_all_g[:, 0:6, :].reshape(8, 6 * PACK_W)
    dmod_cols = lax.dynamic_slice(dmod_all, (0, chip * ncol), (8, ncol))
    g_wada = _matmul("g_w_ada", cond16, jnp.pad(dmod_cols, ((0, 8), (0, 0))), "tn")

    nwin = shapes["w_in"][2]
    g_win_s = g_win.reshape(D, 4, nwin).transpose(1, 0, 2)
    gs = [g_win_s, g_wbg, g_wbr, g_wout.reshape(4, D // 4, D), g_wfi, g_wfo.reshape(4, D_FF // 4, D)]
    recvs = _rs_sibling_swap(gs)
    pairs = [_add_half("grads_pair_sum%d" % t, g, r_, ci) for t, (g, r_) in enumerate(zip(gs, recvs))]
    others = _rs_chip_exchange([pb for _, pb in pairs])
    halves = [_sum_chip("grads_chip_sum%d" % t, pf, o_, chip, ci) for t, ((pf, _), o_) in enumerate(zip(pairs, others))]
    big_grads = dict(zip(_BIG, _rs_sibling_join(halves)))

    res = {tag: {} for tag in ("grad", "delta", "new_m", "new_v")}

    def put(n, g, d, m_, v_):
        for tag, val in zip(("grad", "delta", "new_m", "new_v"), (g, d, m_, v_)):
            res[tag][n] = val.reshape(shapes[n])

    for n in _BIG:
        put(n, big_grads[n], *_adamw("adamw_" + n, sq(W[n]), big_grads[n], sq(Mo[n]), sq(Vo[n])))
    put("w_ada", g_wada, *_adamw("adamw_w_ada", sq(W["w_ada"]), g_wada, sq(Mo["w_ada"]), sq(Vo["w_ada"])))
    rest = _SMALL + _MID
    pk = lambda d: _pack([d[n] for n in rest], 8, f32)
    sg = pk(small_grads)
    sm = _adamw("adamw_small", pk(W), sg, pk(Mo), pk(Vo))
    for tag, buf in zip(("grad", "delta", "new_m", "new_v"), (sg,) + tuple(sm)):
        res[tag].update(zip(rest, _unpack(buf, [shapes[n] for n in rest])))
    outs = [loss, grad_x.reshape(x.shape)]
    for tag in ("grad", "delta", "new_m", "new_v"):
        outs += [res[tag][n] for n in _ORDER]
    return tuple(outs)
```

```python
import functools
import math

import numpy as np
import jax
import jax.numpy as jnp
from jax import lax
from jax.experimental import pallas as pl
from jax.experimental.pallas import tpu as pltpu

f32 = jnp.float32
bf16 = jnp.bfloat16

LANES = 128
HEADS = 8
HEAD_DIM = 64
WIDTH = HEADS * HEAD_DIM
CHUNK = 64
D_FF = 2816
NORM_EPS = 1e-6
LNX_EPS = 64e-5
PACK_W = 1024
MESH_ID = pl.DeviceIdType.MESH

ADAM_LR, ADAM_B1, ADAM_B2, ADAM_EPS, ADAM_WD, ADAM_STEP = 0.001, 0.9, 0.999, 1e-08, 0.01, 10


def _pick(n, target, mult):
    if n <= target:
        return n
    best = None
    for t in range(mult, target + 1, mult):
        if n % t == 0:
            best = t
    assert best is not None, (n, target, mult)
    return best


def _split_bf16(x, n):
    parts, r = [], x
    for i in range(n):
        p = r.astype(bf16)
        parts.append(p)
        if i + 1 < n:
            r = r - p.astype(f32)
    return parts


def _xdot_r_impl(x, m, n, dims):
    acc = None
    for p in _split_bf16(x, n):
        t = lax.dot_general(p, m, dims, preferred_element_type=f32)
        acc = t if acc is None else acc + t
    return acc


def _make_xdot_r(n):
    nn = (((1,), (0,)), ((), ()))
    nt = (((1,), (1,)), ((), ()))

    @jax.custom_vjp
    def xdot(x, m):
        return _xdot_r_impl(x, m, n, nn)

    def fwd(x, m):
        return _xdot_r_impl(x, m, n, nn), m

    def bwd(m, ct):
        return _xdot_r_impl(ct, m, n, nt), jnp.zeros_like(m)

    xdot.defvjp(fwd, bwd)
    return xdot


_segsum = _make_xdot_r(2)


def _xdot_l_impl(m, x, n, dims):
    acc = None
    for p in _split_bf16(x, n):
        t = lax.dot_general(m, p, dims, preferred_element_type=f32)
        acc = t if acc is None else acc + t
    return acc


@jax.custom_vjp
def _xdot_l(m, x):
    return _xdot_l_impl(m, x, 3, (((1,), (0,)), ((), ())))


def _xdot_l_fwd(m, x):
    return _xdot_l(m, x), m


def _xdot_l_bwd(m, ct):
    return jnp.zeros_like(m), _xdot_l_impl(m, ct, 3, (((0,), (0,)), ((), ())))


_xdot_l.defvjp(_xdot_l_fwd, _xdot_l_bwd)


@jax.custom_vjp
def _bdot(x, w):
    return jnp.dot(x.astype(bf16), w.astype(bf16), preferred_element_type=f32)


def _bdot_fwd(x, w):
    return _bdot(x, w), (x, w)


def _bdot_bwd(res, ct):
    x, w = res
    c = ct.astype(bf16)
    dx = lax.dot_general(c, w.astype(bf16), (((1,), (1,)), ((), ())), preferred_element_type=f32)
    dw = lax.dot_general(x.astype(bf16), c, (((0,), (0,)), ((), ())), preferred_element_type=f32)
    return dx, dw


_bdot.defvjp(_bdot_fwd, _bdot_bwd)


def _silu(x):
    return x * jax.nn.sigmoid(x)


def _softplus(x):
    return jnp.maximum(x, 0.0) + jnp.log(1.0 + jnp.exp(-jnp.abs(x)))


def _rms(x, w, eps):
    return x * lax.rsqrt(jnp.mean(x * x, axis=-1, keepdims=True) + eps) * w


def _seg_matrix(width, seg):
    i = np.arange(width)
    return jnp.asarray((i[:, None] // seg) == (i[None, :] // seg), dtype=bf16)


def _chunk_tri(rows, chunk):
    i = np.arange(rows)
    return jnp.asarray(((i[:, None] // chunk) == (i[None, :] // chunk)) & (i[:, None] >= i[None, :]), dtype=bf16)


HALO = 8


def _full_spec(shape):
    nd = len(shape)
    return pl.BlockSpec(shape, lambda i: (0,) * nd)


def _entry_specs(entries, tt, block_of):
    specs, ops = [], []
    for arr, w, ci, shifts in entries:
        specs.append(pl.BlockSpec((tt, w), lambda i, ci=ci: (block_of(i), ci)))
        ops.append(arr)
        if shifts:
            specs.append(pl.BlockSpec((HALO, w), lambda i, ci=ci: (jnp.maximum(block_of(i) * (tt // HALO) - 1, 0), ci)))
            ops.append(arr)
    return specs, ops


def _load_entries(entries, refs, first):
    tiles, k = [], 0
    for _, w, _, shifts in entries:
        x = refs[k][...].astype(f32)
        k += 1
        if not shifts:
            tiles.append(x)
            continue
        halo = jnp.where(first, 0.0, refs[k][...].astype(f32))
        k += 1
        row = lax.broadcasted_iota(jnp.int32, (HALO, w), 0)
        for s in shifts:
            if s == 0:
                tiles.append(x)
                continue
            r = pltpu.roll(x, s, 0)
            head = jnp.where(row < s, pltpu.roll(halo, s, 0), r[0:HALO])
            tiles.append(jnp.concatenate([head, r[HALO:]], axis=0))
    return tiles


def _unshift_sum(grads, shifts, carry, tt):
    w = grads[0].shape[1]
    row = lax.broadcasted_iota(jnp.int32, (tt, w), 0)
    row8 = lax.broadcasted_iota(jnp.int32, (HALO, w), 0)
    dx, out = None, jnp.zeros((HALO, w), f32)
    for d, s in zip(grads, shifts):
        if s == 0:
            part = d
        else:
            part = jnp.where(row < tt - s, pltpu.roll(d, tt - s, 0), 0.0)
            out = out + jnp.where(row8 >= HALO - s, pltpu.roll(d[0:HALO], HALO - s, 0), 0.0)
        dx = part if dx is None else dx + part
    return jnp.concatenate([dx[:tt - HALO], dx[tt - HALO:] + carry], axis=0), out


def _stage_fwd(name, fn, tiles, params, outs, tt):
    rows = tiles[0][0].shape[0]
    npar = len(params)
    specs, ops = _entry_specs(tiles, tt, lambda i: i)
    nin = len(ops)

    def body(*refs):
        ts = _load_entries(tiles, refs[:nin], pl.program_id(0) == 0)
        ps = [r[...] for r in refs[nin:nin + npar]]
        res = fn(ps, ts)
        for r, v in zip(refs[nin + npar:], res):
            r[...] = v.astype(r.dtype)

    return pl.pallas_call(
        body, grid=(rows // tt,),
        in_specs=specs + [_full_spec(p.shape) for p in params],
        out_specs=[pl.BlockSpec((tt, w), lambda i: (i, 0)) for (w, _) in outs],
        out_shape=[jax.ShapeDtypeStruct((rows, w), dt) for (w, dt) in outs],
        compiler_params=pltpu.CompilerParams(dimension_semantics=("parallel",)),
        name=name,
    )(*ops, *params)


def _stage_bwd(name, fn, tiles, params, cots, tt, tile_grad, param_grad):
    rows = tiles[0][0].shape[0]
    nblk = rows // tt
    npar = len(params)
    block_of = lambda i: nblk - 1 - i
    specs, ops = _entry_specs(tiles, tt, block_of)
    nin = len(ops)
    flat_cots = [c for group in cots for c in group]
    groups = [len(g) for g in cots]
    ncot = len(flat_cots)
    counts = [len(e[3]) if e[3] else 1 for e in tiles]
    dt_entries = [e for e, g in zip(tiles, tile_grad) if g]
    dp_shapes = [p.shape for p, g in zip(params, param_grad) if g]
    ndt = len(dt_entries)
    carry_w = [e[1] for e in dt_entries if e[3]]
    flags = [g for g, n in zip(tile_grad, counts) for _ in range(n)]

    def body(*refs):
        i = pl.program_id(0)
        p_refs = refs[nin:nin + npar]
        c_refs = refs[nin + npar:nin + npar + ncot]
        dt_refs = refs[nin + npar + ncot:nin + npar + ncot + ndt]
        dp_refs = refs[nin + npar + ncot + ndt:nin + npar + ncot + ndt + len(dp_shapes)]
        carry_refs = refs[nin + npar + ncot + ndt + len(dp_shapes):]
        ts = _load_entries(tiles, refs[:nin], block_of(i) == 0)
        ps = [r[...] for r in p_refs]

        def f(dp, dt):
            dp, dt = iter(dp), iter(dt)
            pp = [next(dp) if g else p for p, g in zip(ps, param_grad)]
            tl = [next(dt) if g else t for t, g in zip(ts, flags)]
            return fn(pp, tl)

        _, vjp = jax.vjp(f, [p for p, g in zip(ps, param_grad) if g], [t for t, g in zip(ts, flags) if g])
        cs, j = [], 0
        for n in groups:
            acc = c_refs[j][...].astype(f32)
            for q in range(1, n):
                acc = acc + c_refs[j + q][...].astype(f32)
            cs.append(acc)
            j += n
        gp, gt = vjp(cs)

        @pl.when(i == 0)
        def _():
            for r in dp_refs:
                r[...] = jnp.zeros_like(r)
            for r in carry_refs:
                r[...] = jnp.zeros_like(r)

        gt, k, kc = list(gt), 0, 0
        for r, e, n in zip(dt_refs, dt_entries, [n for n, g in zip(counts, tile_grad) if g]):
            if e[3]:
                dx, out = _unshift_sum(gt[k:k + n], e[3], carry_refs[kc][...], tt)
                r[...] = dx
                carry_refs[kc][...] = out
                kc += 1
            else:
                r[...] = gt[k]
            k += n
        for r, v in zip(dp_refs, gp):
            r[...] += v

    res = pl.pallas_call(
        body, grid=(nblk,),
        in_specs=specs + [_full_spec(p.shape) for p in params]
        + [pl.BlockSpec((tt, w), lambda i, ci=ci: (block_of(i), ci)) for (_, w, ci, *_) in flat_cots],
        out_specs=[pl.BlockSpec((tt, e[1]), lambda i: (block_of(i), 0)) for e in dt_entries]
        + [_full_spec(s) for s in dp_shapes],
        out_shape=[jax.ShapeDtypeStruct((rows, e[1]), f32) for e in dt_entries]
        + [jax.ShapeDtypeStruct(s, f32) for s in dp_shapes],
        scratch_shapes=[pltpu.VMEM((HALO, w), f32) for w in carry_w],
        compiler_params=pltpu.CompilerParams(dimension_semantics=("arbitrary",)),
        name=name,
    )(*ops, *params, *[c[0] for c in flat_cots])
    return list(res[:ndt]), list(res[ndt:])


def _whole(a):
    return (a, a.shape[1], 0, None)


def _matmul(name, a, b, mode, out_dtype=f32, tm=1024, tn=1024, tk=1024, shards=1):
    S = shards
    if mode == "nn":
        M, K = a.shape
        w = b.shape[-1]
    elif mode == "nt":
        M = a.shape[0]
        if S > 1:
            _, N, w = b.shape
            K = S * w
        else:
            N, K = b.shape
            w = K
    else:
        K, M = a.shape
        w = b.shape[1] // S
    if mode != "nt":
        N = S * w
    tm = _pick(M, tm, LANES)
    if mode == "nt":
        tn = _pick(N, tn, LANES)
        tk = _pick(w, tk, LANES)
    else:
        tn = _pick(w, tn, LANES)
        tk = _pick(K, tk, LANES if mode == "nn" else 16)
    nk = K // tk
    nb = w // (tk if mode == "nt" else tn)
    if mode == "nn":
        a_spec = pl.BlockSpec((tm, tk), lambda i, j, k: (i, k))
        if S > 1:
            b_spec = pl.BlockSpec((1, tk, tn), lambda i, j, k: (j // nb, k, j % nb))
        else:
            b_spec = pl.BlockSpec((tk, tn), lambda i, j, k: (k, j))
        dims = (((1,), (0,)), ((), ()))
    elif mode == "nt":
        a_spec = pl.BlockSpec((tm, tk), lambda i, j, k: (i, k))
        if S > 1:
            b_spec = pl.BlockSpec((1, tn, tk), lambda i, j, k: (k // nb, j, k % nb))
        else:
            b_spec = pl.BlockSpec((tn, tk), lambda i, j, k: (j, k))
        dims = (((1,), (1,)), ((), ()))
    else:
        a_spec = pl.BlockSpec((tk, tm), lambda i, j, k: (k, i))
        b_spec = pl.BlockSpec((tk, tn), lambda i, j, k: (k, j))
        dims = (((0,), (0,)), ((), ()))
    if mode == "tn" and S > 1:
        o_spec = pl.BlockSpec((1, tm, tn), lambda i, j, k: (j // nb, i, j % nb))
        o_shape = (S, M, w)
    else:
        o_spec = pl.BlockSpec((tm, tn), lambda i, j, k: (i, j))
        o_shape = (M, N)
    b_lead = S > 1 and mode != "tn"
    o_lead = S > 1 and mode == "tn"

    def body(a_ref, b_ref, o_ref, acc_ref):
        k = pl.program_id(2)

        @pl.when(k == 0)
        def _():
            acc_ref[...] = jnp.zeros_like(acc_ref)

        bv = b_ref[0] if b_lead else b_ref[...]
        acc_ref[...] += lax.dot_general(a_ref[...].astype(bf16), bv.astype(bf16), dims, preferred_element_type=f32)

        @pl.when(k == nk - 1)
        def _():
            if o_lead:
                o_ref[0] = acc_ref[...].astype(o_ref.dtype)
            else:
                o_ref[...] = acc_ref[...].astype(o_ref.dtype)

    return pl.pallas_call(
        body, grid=(M // tm, N // tn, nk),
        in_specs=[a_spec, b_spec],
        out_specs=o_spec,
        out_shape=jax.ShapeDtypeStruct(o_shape, out_dtype),
        scratch_shapes=[pltpu.VMEM((tm, tn), f32)],
        compiler_params=pltpu.CompilerParams(dimension_semantics=("parallel", "parallel", "arbitrary")),
        name=name,
    )(a, b)


def _make_bmm(precision):
    if precision is None:
        cast, kw = (lambda v: v.astype(bf16)), {}
    else:
        cast, kw = (lambda v: v), {"precision": precision}

    def nn(a, b):
        return jnp.einsum("hij,hjk->hik", cast(a), cast(b), preferred_element_type=f32, **kw)

    def nt(a, b):
        return jnp.einsum("hik,hjk->hij", cast(a), cast(b), preferred_element_type=f32, **kw)

    def tn(a, b):
        return jnp.einsum("hki,hkj->hij", cast(a), cast(b), preferred_element_type=f32, **kw)

    if precision is not None:
        return nn, nt, tn
    nn_v, nt_v, tn_v = jax.custom_vjp(nn), jax.custom_vjp(nt), jax.custom_vjp(tn)
    keep = lambda f: (lambda a, b: (f(a, b), (a, b)))
    nn_v.defvjp(keep(nn), lambda r, ct: (nt(ct, r[1]), tn(r[0], ct)))
    nt_v.defvjp(keep(nt), lambda r, ct: (nn(ct, r[1]), tn(ct, r[0])))
    tn_v.defvjp(keep(tn), lambda r, ct: (nt(r[1], ct), nn(r[0], ct)))
    return nn_v, nt_v, tn_v


_bmm, _bmm_nt, _bmm_tn = _make_bmm(None)
_bmm_exact = _make_bmm(lax.Precision.HIGHEST)[0]


def _masks(n):
    r = lax.broadcasted_iota(jnp.int32, (n, n), 0)
    c = lax.broadcasted_iota(jnp.int32, (n, n), 1)
    return (r >= c)[None], (r > c)[None], (r == c)[None]


_INV_BLOCK = 8


def _nilpotent_inverse(m, eye):
    p = eye + m
    for _ in range(2):
        m = _bmm(m, m)
        p = p + _bmm(p, m)
    return p


def _neumann_inverse(m, eye, n):
    assert n == _INV_BLOCK * _INV_BLOCK
    r = lax.broadcasted_iota(jnp.int32, (n, n), 0) // _INV_BLOCK
    c = lax.broadcasted_iota(jnp.int32, (n, n), 1) // _INV_BLOCK
    inside = jnp.where((r == c)[None], m, 0.0)
    d_inv = _nilpotent_inverse(inside, eye)
    return _bmm(_nilpotent_inverse(_bmm(d_inv, m - inside), eye), d_inv)


def _gdn_chunk(s, q, k, v, beta, gc, gr, gl):
    n = q.shape[1]
    causal, strict, diag = _masks(n)
    eye = diag.astype(f32)
    decay = jnp.where(causal, jnp.exp(jnp.where(causal, gc - gr, 0.0)), 0.0)
    kb = k * beta
    vb = v * beta
    lower = jnp.where(strict, _bmm_nt(kb, k) * decay, 0.0)
    t_mat = _neumann_inverse(-lower, eye, n)
    egc = jnp.exp(gc)
    u = _bmm(t_mat, vb)
    w = _bmm(t_mat, kb * egc)
    attn = jnp.where(causal, _bmm_nt(q, k) * decay, 0.0)
    v_new = u - _bmm(w, s)
    o = _bmm(q * egc, s) + _bmm(attn, v_new)
    k_dec = k * jnp.exp(gl - gc)
    s_new = s * jnp.exp(gl) + _bmm_tn(k_dec, v_new)
    return s_new, o


def _rwkv_chunk(s, r, lw, k, v, a, b):
    n = r.shape[1]
    causal, strict, diag = _masks(n)
    eye = diag.astype(f32)
    tri = jnp.broadcast_to(causal.astype(f32), (r.shape[0], n, n))
    lc = _bmm_exact(tri, lw)
    ein = jnp.exp(lc)
    eout = jnp.exp(-lc)
    a_t = a * jnp.exp(lc - lw)
    b_t = b * eout
    k_t = k * eout
    r_t = r * ein
    a_ab = jnp.where(strict, _bmm_nt(a_t, b_t), 0.0)
    a_ak = jnp.where(strict, _bmm_nt(a_t, k_t), 0.0)
    u = _bmm(_neumann_inverse(a_ab, eye, n), _bmm_nt(a_t, s) + _bmm(a_ak, v))
    y = (_bmm_nt(r_t, s) + _bmm(jnp.where(causal, _bmm_nt(r_t, b_t), 0.0), u)
         + _bmm(jnp.where(causal, _bmm_nt(r_t, k_t), 0.0), v))
    e_last = jnp.exp(jnp.sum(lw, axis=1, keepdims=True))
    s_new = s * e_last + _bmm_tn(u, b_t * e_last) + _bmm_tn(v, k_t * e_last)
    return s_new, y


def _heads_in(ref):
    return jnp.stack([ref[:, h * HEAD_DIM:(h + 1) * HEAD_DIM] for h in range(HEADS)], axis=0)


def _heads_out(ref, val):
    for h in range(HEADS):
        ref[:, h * HEAD_DIM:(h + 1) * HEAD_DIM] = val[h]


def _gdn_scalars(bt, gt):
    n = bt.shape[0]
    gtt = gt.T
    hs = range(HEADS)
    return [jnp.stack([bt[:, h:h + 1] for h in hs], axis=0),
            jnp.stack([gt[:, HEADS + h:HEADS + h + 1] for h in hs], axis=0),
            jnp.stack([gtt[HEADS + h:HEADS + h + 1, :] for h in hs], axis=0),
            jnp.stack([gt[n - 1:n, HEADS + h:HEADS + h + 1] for h in hs], axis=0)]


def _gdn_scalars_back(dbeta, dgc, dgr, dgl):
    n = dbeta.shape[1]
    lane = lax.broadcasted_iota(jnp.int32, (n, LANES), 1)
    row = lax.broadcasted_iota(jnp.int32, (n, LANES), 0)
    sub = lax.broadcasted_iota(jnp.int32, (LANES, n), 0)
    db = jnp.zeros((n, LANES), f32)
    dg = jnp.zeros((n, LANES), f32)
    dgt = jnp.zeros((LANES, n), f32)
    for h in range(HEADS):
        db = jnp.where(lane == h, dbeta[h], db)
        dg = jnp.where(lane == HEADS + h, dgc[h] + jnp.where(row == n - 1, dgl[h], 0.0), dg)
        dgt = jnp.where(sub == HEADS + h, dgr[h], dgt)
    return [db, dg + dgt.T]


def _scan_spec(width, n, reverse):
    if reverse:
        return pl.BlockSpec((CHUNK, width), lambda i: (n - 1 - i, 0))
    return pl.BlockSpec((CHUNK, width), lambda i: (i, 0))


def _hist_spec(n, reverse):
    blk = (1, HEADS, HEAD_DIM, HEAD_DIM)
    if reverse:
        return pl.BlockSpec(blk, lambda i: (n - 1 - i, 0, 0, 0))
    return pl.BlockSpec(blk, lambda i: (i, 0, 0, 0))


def _scan_fwd(name, fn, rows_in, scal_in=()):
    t = rows_in[0].shape[0]
    n = t // CHUNK
    nr, ns = len(rows_in), len(scal_in)

    def body(*refs):
        o_ref, sh_ref, s_scr = refs[nr + ns], refs[nr + ns + 1], refs[nr + ns + 2]

        @pl.when(pl.program_id(0) == 0)
        def _():
            s_scr[...] = jnp.zeros_like(s_scr)

        s = s_scr[...]
        sh_ref[0] = s
        ins = [_heads_in(r) for r in refs[:nr]]
        if ns:
            ins += _gdn_scalars(*[r[...] for r in refs[nr:nr + ns]])
        s_new, o = fn(s, *ins)
        _heads_out(o_ref, o)
        s_scr[...] = s_new

    return pl.pallas_call(
        body, grid=(n,),
        in_specs=[_scan_spec(a.shape[1], n, False) for a in (*rows_in, *scal_in)],
        out_specs=[_scan_spec(WIDTH, n, False), _hist_spec(n, False)],
        out_shape=[jax.ShapeDtypeStruct((t, WIDTH), f32), jax.ShapeDtypeStruct((n, HEADS, HEAD_DIM, HEAD_DIM), f32)],
        scratch_shapes=[pltpu.VMEM((HEADS, HEAD_DIM, HEAD_DIM), f32)],
        compiler_params=pltpu.CompilerParams(dimension_semantics=("arbitrary",)),
        name=name,
    )(*rows_in, *scal_in)


def _scan_bwd(name, fn, rows_in, scal_in, s_hist, d_out):
    t = rows_in[0].shape[0]
    n = t // CHUNK
    nr, ns = len(rows_in), len(scal_in)

    def body(*refs):
        sh_ref, do_ref = refs[nr + ns], refs[nr + ns + 1]
        g_refs = refs[nr + ns + 2:nr + ns + 2 + nr + ns]
        ds_scr = refs[nr + ns + 2 + nr + ns]

        @pl.when(pl.program_id(0) == 0)
        def _():
            ds_scr[...] = jnp.zeros_like(ds_scr)

        ins = [_heads_in(r) for r in refs[:nr]]
        if ns:
            ins += _gdn_scalars(*[r[...] for r in refs[nr:nr + ns]])
        _, vjp = jax.vjp(fn, sh_ref[0], *ins)
        g = vjp((ds_scr[...], _heads_in(do_ref)))
        ds_scr[...] = g[0]
        for r, v in zip(g_refs[:nr], g[1:1 + nr]):
            _heads_out(r, v)
        if ns:
            for r, v in zip(g_refs[nr:], _gdn_scalars_back(*g[1 + nr:])):
                r[...] = v

    arrs = (*rows_in, *scal_in)
    return pl.pallas_call(
        body, grid=(n,),
        in_specs=[_scan_spec(a.shape[1], n, True) for a in arrs] + [_hist_spec(n, True), _scan_spec(WIDTH, n, True)],
        out_specs=[_scan_spec(a.shape[1], n, True) for a in arrs],
        out_shape=[jax.ShapeDtypeStruct(a.shape, f32) for a in arrs],
        scratch_shapes=[pltpu.VMEM((HEADS, HEAD_DIM, HEAD_DIM), f32)],
        compiler_params=pltpu.CompilerParams(dimension_semantics=("arbitrary",)),
        name=name,
    )(*arrs, s_hist, d_out)


def _fn_norm_mod(ps, ts):
    nw, shift, scale = ps
    (x,) = ts
    return [_rms(x, nw, NORM_EPS) * (1.0 + scale) + shift]


def _fn_resid_norm_mod(ps, ts):
    gate, nw, shift, scale = ps
    x, mo = ts
    x1 = x + gate * mo
    return [x1, _rms(x1, nw, NORM_EPS) * (1.0 + scale) + shift]


def _fn_gdn_pre(ps, ts):
    cw = ps[:12]
    alog, dtb, seg, tri = ps[12:]
    ba = ts[12]
    outs = []
    for part in range(3):
        x = ts[4 * part:4 * part + 4]
        w = cw[4 * part:4 * part + 4]
        conv = w[3] * x[0] + w[2] * x[1] + w[1] * x[2] + w[0] * x[3]
        u = _silu(conv)
        if part < 2:
            u = u * lax.rsqrt(_segsum(u * u, seg) + 1e-6)
            if part == 0:
                u = u * (HEAD_DIM ** -0.5)
        outs.append(u)
    beta = jax.nn.sigmoid(ba)
    g = -jnp.exp(alog) * _softplus(ba + dtb)
    gc = _xdot_l(tri, g)
    return outs + [beta, gc]


def _fn_gdn_post(ps, ts):
    ow, seg = ps
    o, z = ts
    ms = _segsum(o * o, seg) * (1.0 / HEAD_DIM)
    return [o * lax.rsqrt(ms + NORM_EPS) * ow * _silu(z)]


def _fn_rwkv_pre(ps, ts):
    mu_r, mu_k, mu_v, mu_l, mu_g, w0, w2p, a0, a2p, g2p, k_k, k_a, seg = ps
    r0, r1, k0, k1, v0, v1, l0, l1, g0, g1 = ts
    xr = r0 + (r1 - r0) * mu_r
    xk = k0 + (k1 - k0) * mu_k
    xv = v0 + (v1 - v0) * mu_v
    xl = l0 + (l1 - l0) * mu_l
    xg = g0 + (g1 - g0) * mu_g
    w = -_softplus(-(w0 + _bdot(jnp.tanh(xl), w2p))) - 0.5
    lw = -jnp.exp(w)
    a = jax.nn.sigmoid(a0 + _bdot(xl, a2p))
    g = _bdot(jax.nn.sigmoid(xg), g2p)
    kk = xk * k_k
    kk = kk * lax.rsqrt(_segsum(kk * kk, seg) + 1e-6)
    k2 = xk * (1.0 + (a - 1.0) * k_a)
    return [xr, lw, k2, xv, -kk, kk * a, g]


def _fn_rwkv_post(ps, ts):
    lw_, lb_, rk, seg = ps
    y, r, k2, v, g = ts
    inv = 1.0 / HEAD_DIM
    yc = y - _segsum(y, seg) * inv
    var = _segsum(yc * yc, seg) * inv
    yn = yc * lax.rsqrt(var + LNX_EPS) * lw_ + lb_
    bonus = _segsum(r * k2 * rk, seg) * v
    return [(yn + bonus) * g]


def _fn_merge(ps, ts):
    gla, glb, ya, yb = ts
    return [jax.nn.sigmoid(gla) * ya + jax.nn.sigmoid(glb) * yb]


def _fn_convglu(ps, ts):
    c0, c1, c2 = ps
    g0, g1, g2, up = ts
    return [_silu(c2 * g0 + c1 * g1 + c0 * g2) * up]


def _fn_add(ps, ts):
    acc = ts[0]
    for t in ts[1:]:
        acc = acc + t
    return [acc]


def _add_n(name, arrs, tt):
    return _stage_fwd(name, _fn_add, [_whole(a) for a in arrs], [], [(arrs[0].shape[1], f32)], tt)[0]


def _final_stage(x1, fo, tgt, gate2, nfw, tt):
    rows, d = x1.shape

    def loss_fn(gate, nw, xa, fa, tg):
        y = _rms(xa + gate * fa, nw, NORM_EPS)
        err = (y - tg) ** 2
        return 0.5 * jnp.sum(jnp.mean(err, axis=-1, keepdims=True), axis=0, keepdims=True)

    def body(x_ref, f_ref, t_ref, g_ref, w_ref, dx_ref, df_ref, dg_ref, dw_ref, l_ref):
        i = pl.program_id(0)
        args = (g_ref[...], w_ref[...], x_ref[...], f_ref[...])
        tg = t_ref[...]
        lv, vjp = jax.vjp(lambda g, w, xa, fa: loss_fn(g, w, xa, fa, tg), *args)
        dg, dw, dx, df = vjp(jnp.ones((1, 1), f32))
        dx_ref[...] = dx
        df_ref[...] = df

        @pl.when(i == 0)
        def _():
            dg_ref[...] = jnp.zeros_like(dg_ref)
            dw_ref[...] = jnp.zeros_like(dw_ref)
            l_ref[...] = jnp.zeros_like(l_ref)

        dg_ref[...] += dg
        dw_ref[...] += dw
        l_ref[...] += jnp.broadcast_to(lv, l_ref.shape)

    row = pl.BlockSpec((tt, d), lambda i: (i, 0))
    vec = pl.BlockSpec((1, d), lambda i: (0, 0))
    return pl.pallas_call(
        body, grid=(rows // tt,),
        in_specs=[row, row, row, vec, vec],
        out_specs=[row, row, vec, vec, pl.BlockSpec((1, LANES), lambda i: (0, 0))],
        out_shape=[jax.ShapeDtypeStruct((rows, d), f32)] * 2 + [jax.ShapeDtypeStruct((1, d), f32)] * 2
        + [jax.ShapeDtypeStruct((1, LANES), f32)],
        compiler_params=pltpu.CompilerParams(dimension_semantics=("arbitrary",)),
        name="loss_head",
    )(x1, fo, tgt, gate2, nfw)


def _ada_fwd(c_all, w_shard, b_cols):
    def body(c_ref, w_ref, b_ref, cond_ref, mod_ref):
        cond = _silu(c_ref[...])
        cond_ref[...] = cond
        mod_ref[...] = jnp.dot(cond.astype(bf16), w_ref[...].astype(bf16), preferred_element_type=f32) + b_ref[...]

    n = w_shard.shape[1]
    return pl.pallas_call(
        body, out_shape=[jax.ShapeDtypeStruct(c_all.shape, f32), jax.ShapeDtypeStruct((c_all.shape[0], n), f32)],
        name="ada_fwd",
    )(c_all, w_shard, b_cols)


def _adamw(name, w, g, m, v):
    rows, width = w.shape
    tt = _pick(rows, 128, 8)
    c1 = 1.0 - ADAM_B1 ** ADAM_STEP
    c2 = 1.0 - ADAM_B2 ** ADAM_STEP

    def body(w_ref, g_ref, m_ref, v_ref, d_ref, mo_ref, vo_ref):
        gg = g_ref[...]
        mn = ADAM_B1 * m_ref[...] + (1.0 - ADAM_B1) * gg
        vn = ADAM_B2 * v_ref[...] + (1.0 - ADAM_B2) * (gg * gg)
        m_hat = mn / c1
        v_hat = vn / c2
        d_ref[...] = -ADAM_LR * (m_hat / (jnp.sqrt(v_hat) + ADAM_EPS) + ADAM_WD * w_ref[...])
        mo_ref[...] = mn
        vo_ref[...] = vn

    spec = pl.BlockSpec((tt, width), lambda i: (i, 0))
    return pl.pallas_call(
        body, grid=(rows // tt,), in_specs=[spec] * 4, out_specs=[spec] * 3,
        out_shape=[jax.ShapeDtypeStruct((rows, width), f32)] * 3,
        compiler_params=pltpu.CompilerParams(dimension_semantics=("parallel",)),
        name=name,
    )(w, g, m, v)


def _place():
    return lax.axis_index("x"), lax.axis_index("y"), lax.axis_index("c")


def _ag8(name, blk):
    m, w = blk.shape
    flips = [(dx, dy, dc) for dx in (0, 1) for dy in (0, 1) for dc in (0, 1)][1:]

    def body(x_ref, out_ref, send_sems, recv_sems, local_sem):
        x, y, c = _place()
        me = 4 * x + 2 * y + c
        mine = pltpu.make_async_copy(x_ref, out_ref.at[me], local_sem)
        mine.start()
        peers = [((1 - x) if dx else x, (1 - y) if dy else y, (1 - c) if dc else c) for dx, dy, dc in flips]
        sends = []
        for k, peer in enumerate(peers):
            cp = pltpu.make_async_remote_copy(src_ref=x_ref, dst_ref=out_ref.at[me], send_sem=send_sems.at[k],
                                              recv_sem=recv_sems.at[k], device_id=peer, device_id_type=MESH_ID)
            cp.start()
            sends.append(cp)
        for k, (px, py, pc) in enumerate(peers):
            pltpu.make_async_remote_copy(src_ref=x_ref, dst_ref=out_ref.at[4 * px + 2 * py + pc],
                                         send_sem=send_sems.at[k], recv_sem=recv_sems.at[k],
                                         device_id=(px, py, pc), device_id_type=MESH_ID).wait_recv()
        for cp in sends:
            cp.wait_send()
        mine.wait()

    return pl.pallas_call(
        body, out_shape=jax.ShapeDtypeStruct((8, m, w), blk.dtype),
        in_specs=[pl.BlockSpec(memory_space=pltpu.VMEM)], out_specs=pl.BlockSpec(memory_space=pltpu.VMEM),
        scratch_shapes=[pltpu.SemaphoreType.DMA((7,)), pltpu.SemaphoreType.DMA((7,)), pltpu.SemaphoreType.DMA],
        name=name,
    )(blk)


def _other_chips(x, y):
    return [(1 - x, y), (x, 1 - y), (1 - x, 1 - y)]


_ANY = pl.BlockSpec(memory_space=pl.ANY)


def _rcopy(src, dst, send_sems, recv_sems, k, dev):
    return pltpu.make_async_remote_copy(src_ref=src, dst_ref=dst, send_sem=send_sems.at[k], recv_sem=recv_sems.at[k],
                                        device_id=dev, device_id_type=MESH_ID)


def _ag4_multi(ws):
    n = len(ws)

    def body(*refs):
        w_refs, out_refs = refs[:n], refs[n:2 * n]
        send_sems, recv_sems = refs[2 * n:]
        x, y, c = _place()
        chip = 2 * x + y
        sibling = (x, y, 1 - c)
        chips = _other_chips(x, y)
        halves = [(pl.ds(c * (w.shape[0] // 2), w.shape[0] // 2), pl.ds((1 - c) * (w.shape[0] // 2), w.shape[0] // 2))
                  for w in ws]
        copies = []
        for t in range(n):
            for k, (px, py) in enumerate(chips):
                cp = _rcopy(w_refs[t].at[halves[t][0]], out_refs[t].at[chip, halves[t][0]], send_sems, recv_sems,
                            7 * t + k, (px, py, c))
                cp.start()
                copies.append(cp)
        for t in range(n):
            own = _rcopy(w_refs[t], out_refs[t].at[chip], send_sems, recv_sems, 7 * t + 6, sibling)
            own.start()
            copies.append(own)
        for t in range(n):
            for k, (px, py) in enumerate(chips):
                blk = out_refs[t].at[2 * px + py, halves[t][0]]
                _rcopy(blk, blk, send_sems, recv_sems, 7 * t + k, (px, py, c)).wait_recv()
                fw = _rcopy(blk, blk, send_sems, recv_sems, 7 * t + 3 + k, sibling)
                fw.start()
                copies.append(fw)
        for t in range(n):
            for k, (px, py) in enumerate(chips):
                blk = out_refs[t].at[2 * px + py, halves[t][1]]
                _rcopy(blk, blk, send_sems, recv_sems, 7 * t + 3 + k, sibling).wait_recv()
            _rcopy(w_refs[t], out_refs[t].at[chip], send_sems, recv_sems, 7 * t + 6, sibling).wait_recv()
        for cp in copies:
            cp.wait_send()

    return pl.pallas_call(
        body, out_shape=[jax.ShapeDtypeStruct((4,) + w.shape, w.dtype) for w in ws],
        in_specs=[_ANY] * n, out_specs=[_ANY] * n,
        scratch_shapes=[pltpu.SemaphoreType.DMA((7 * n,)), pltpu.SemaphoreType.DMA((7 * n,))],
        name="weights_all_gather",
    )(*ws)


def _rs_sibling_swap(gs):
    n = len(gs)

    def body(*refs):
        g_refs, recv_refs = refs[:n], refs[n:2 * n]
        send_sems, recv_sems = refs[2 * n:]
        x, y, c = _place()
        copies = []
        for t in range(n):
            rh = gs[t].shape[1] // 2
            for s_ in range(4):
                cp = _rcopy(g_refs[t].at[s_, pl.ds((1 - c) * rh, rh)], recv_refs[t].at[s_], send_sems, recv_sems,
                            4 * t + s_, (x, y, 1 - c))
                cp.start()
                copies.append(cp)
        for cp in copies:
            cp.wait_recv()
        for cp in copies:
            cp.wait_send()

    return pl.pallas_call(
        body, out_shape=[jax.ShapeDtypeStruct((4, g.shape[1] // 2, g.shape[2]), g.dtype) for g in gs],
        in_specs=[_ANY] * n, out_specs=[_ANY] * n,
        scratch_shapes=[pltpu.SemaphoreType.DMA((4 * n,)), pltpu.SemaphoreType.DMA((4 * n,))],
        name="grads_sibling_swap",
    )(*gs)


def _rs_chip_exchange(ps):
    n = len(ps)

    def body(*refs):
        p_refs, recv_refs = refs[:n], refs[n:2 * n]
        send_sems, recv_sems = refs[2 * n:]
        x, y, c = _place()
        chips = _other_chips(x, y)
        copies = []
        for t in range(n):
            for k, (px, py) in enumerate(chips):
                cp = _rcopy(p_refs[t].at[2 * px + py], recv_refs[t].at[k], send_sems, recv_sems, 3 * t + k,
                            (px, py, c))
                cp.start()
                copies.append(cp)
        for cp in copies:
            cp.wait_recv()
        for cp in copies:
            cp.wait_send()

    return pl.pallas_call(
        body, out_shape=[jax.ShapeDtypeStruct((3,) + p.shape[1:], p.dtype) for p in ps],
        in_specs=[_ANY] * n, out_specs=[_ANY] * n,
        scratch_shapes=[pltpu.SemaphoreType.DMA((3 * n,)), pltpu.SemaphoreType.DMA((3 * n,))],
        name="grads_chip_exchange",
    )(*ps)


_JOIN_PIECES = 4


def _rs_sibling_join(qs):
    n = len(qs)
    npc = _JOIN_PIECES

    def body(*refs):
        q_refs, out_refs = refs[:n], refs[n:2 * n]
        send_sems, recv_sems = refs[2 * n:]
        x, y, c = _place()
        copies = []
        for t in range(n):
            rh = qs[t].shape[0] // 2
            pr = rh // npc
            for i in range(npc):
                rows = pl.ds(c * rh + i * pr, pr)
                cp = _rcopy(q_refs[t].at[rows], out_refs[t].at[rows], send_sems, recv_sems, npc * t + i,
                            (x, y, 1 - c))
                cp.start()
                copies.append(cp)
        for t in range(n):
            rh = qs[t].shape[0] // 2
            pr = rh // npc
            for i in range(npc):
                rows = pl.ds((1 - c) * rh + i * pr, pr)
                _rcopy(q_refs[t].at[rows], out_refs[t].at[rows], send_sems, recv_sems, npc * t + i,
                       (x, y, 1 - c)).wait_recv()
        for cp in copies:
            cp.wait_send()

    return pl.pallas_call(
        body, out_shape=[jax.ShapeDtypeStruct(q.shape, q.dtype) for q in qs],
        in_specs=[_ANY] * n, out_specs=[_ANY] * n, input_output_aliases={t: t for t in range(n)},
        scratch_shapes=[pltpu.SemaphoreType.DMA((npc * n,)), pltpu.SemaphoreType.DMA((npc * n,))],
        name="grads_sibling_join",
    )(*qs)


def _add_half(name, g, recv, ci):
    S, r, w = g.shape
    rh = r // 2
    tt = _pick(rh, 256, 16)
    nb = rh // tt

    def body(c_ref, a_ref, b_ref, o_ref, ob_ref):
        v = a_ref[...] + b_ref[...]
        o_ref[...] = v
        ob_ref[...] = v.astype(bf16)

    blk = pl.BlockSpec((1, tt, w), lambda s_, i, cr: (s_, i, 0))
    grid_spec = pltpu.PrefetchScalarGridSpec(
        num_scalar_prefetch=1, grid=(S, nb),
        in_specs=[pl.BlockSpec((1, tt, w), lambda s_, i, cr: (s_, cr[0] * nb + i, 0)), blk],
        out_specs=[blk, blk])
    return pl.pallas_call(body, grid_spec=grid_spec,
                          out_shape=[jax.ShapeDtypeStruct((S, rh, w), f32), jax.ShapeDtypeStruct((S, rh, w), bf16)],
                          name=name)(ci.reshape(1).astype(jnp.int32), g, recv)


def _sum_chip(name, pair, others, chip, ci):
    _, rh, w = pair.shape
    tt = _pick(rh, 128, 16)
    nb = rh // tt

    def body(chip_ref, core_ref, a_ref, b_ref, o_ref):
        o_ref[...] = ((a_ref[0] + b_ref[0].astype(f32)) + b_ref[1].astype(f32)) + b_ref[2].astype(f32)

    grid_spec = pltpu.PrefetchScalarGridSpec(
        num_scalar_prefetch=2, grid=(nb,),
        in_specs=[pl.BlockSpec((1, tt, w), lambda i, ch, co: (ch[0], i, 0)),
                  pl.BlockSpec((3, tt, w), lambda i, ch, co: (0, i, 0))],
        out_specs=pl.BlockSpec((tt, w), lambda i, ch, co: (co[0] * nb + i, 0)))
    return pl.pallas_call(body, grid_spec=grid_spec, out_shape=jax.ShapeDtypeStruct((2 * rh, w), f32),
                          name=name)(chip.reshape(1).astype(jnp.int32), ci.reshape(1).astype(jnp.int32), pair, others)


def _sum_devices(gathered, head_row):
    _, rows, width = gathered.shape

    def body(g_ref, out_ref, head_ref):
        acc = g_ref[0]
        for d in range(1, 8):
            acc = acc + g_ref[d]
        out_ref[...] = acc
        row = acc[head_row:head_row + 1, :]
        hs = row[:, 0:HEAD_DIM]
        for h in range(1, HEADS):
            hs = hs + row[:, h * HEAD_DIM:(h + 1) * HEAD_DIM]
        head_ref[...] = jnp.zeros_like(head_ref)
        head_ref[0:1, 0:HEAD_DIM] = hs

    return pl.pallas_call(
        body, out_shape=[jax.ShapeDtypeStruct((rows, width), f32), jax.ShapeDtypeStruct((8, LANES), f32)],
        name="small_grads_sum",
    )(gathered)


def _pack(arrs, rows_mult, dtype):
    flat = jnp.concatenate([a.reshape(-1).astype(dtype) for a in arrs])
    per = PACK_W * rows_mult
    total = -(-flat.shape[0] // per) * per
    return jnp.pad(flat, (0, total - flat.shape[0])).reshape(total // PACK_W, PACK_W)


def _unpack(buf, shapes):
    flat = buf.reshape(-1)
    out, off = [], 0
    for s in shapes:
        n = int(np.prod(s))
        out.append(flat[off:off + n].reshape(s))
        off += n
    return out


_BIG = ["w_in", "w_branch_gdn", "w_branch_rwkv", "w_out", "w_ffn_in", "w_ffn_out"]
_MID = ["conv_gdn", "conv_ffn", "w2", "a2", "g2"]
_SMALL =["b_ada", "norm1_w", "a_log", "dt_bias", "onorm_gdn", "mu_rwkv", "w0", "a0", "k_k", "k_a", "r_k", "lnx_w",
          "lnx_b", "norm2_w", "norm_f_w"]
_ORDER = ["w_ada", "b_ada", "norm1_w", "w_in", "conv_gdn", "a_log", "dt_bias", "onorm_gdn", "w_branch_gdn", "mu_rwkv",
          "w0", "w2", "a0", "a2", "g2", "k_k", "k_a", "r_k", "lnx_w", "lnx_b", "w_branch_rwkv", "w_out", "norm2_w",
          "w_ffn_in", "conv_ffn", "w_ffn_out", "norm_f_w"]


def _win_pad(w):
    z = lambda n: jnp.zeros((w.shape[0], n), w.dtype)
    return jnp.concatenate([w[:, 0:2048], w[:, 2064:3728], w[:, 2048:2064], z(112), w[:, 3728:3888], z(96),
                            w[:, 3888:5936]], axis=1)


def _win_unpad(g):
    return jnp.concatenate([g[:, 0:2048], g[:, 3712:3728], g[:, 2048:3712], g[:, 3840:4000], g[:, 4096:6144]], axis=1)


def kernel(x, c, w_ada, b_ada, norm1_w, w_in, conv_gdn, a_log, dt_bias, onorm_gdn, w_branch_gdn, mu_rwkv, w0, w2, a0, a2, g2, k_k, k_a, r_k, lnx_w, lnx_b, w_branch_rwkv, w_out, norm2_w, w_ffn_in, conv_ffn, w_ffn_out, norm_f_w, loss_target, m_w_ada, m_b_ada, m_norm1_w, m_w_in, m_conv_gdn, m_a_log, m_dt_bias, m_onorm_gdn, m_w_branch_gdn, m_mu_rwkv, m_w0, m_w2, m_a0, m_a2, m_g2, m_k_k, m_k_a, m_r_k, m_lnx_w, m_lnx_b, m_w_branch_rwkv, m_w_out, m_norm2_w, m_w_ffn_in, m_conv_ffn, m_w_ffn_out, m_norm_f_w, v_w_ada, v_b_ada, v_norm1_w, v_w_in, v_conv_gdn, v_a_log, v_dt_bias, v_onorm_gdn, v_w_branch_gdn, v_mu_rwkv, v_w0, v_w2, v_a0, v_a2, v_g2, v_k_k, v_k_a, v_r_k, v_lnx_w, v_lnx_b, v_w_branch_rwkv, v_w_out, v_norm2_w, v_w_ffn_in, v_conv_ffn, v_w_ffn_out, v_norm_f_w):
    args = dict(locals())
    W = {n: args[n] for n in _ORDER}
    Mo = {n: args["m_" + n] for n in _ORDER}
    Vo = {n: args["v_" + n] for n in _ORDER}
    shapes = {n: W[n].shape for n in _ORDER}
    sq = lambda a: a.reshape(a.shape[-2:]) if a.ndim == 3 else a.reshape(1, -1)
    row = lambda a: a.reshape(1, -1)

    xi, yi, ci = lax.axis_index("x"), lax.axis_index("y"), lax.axis_index("c")
    dev = 4 * xi + 2 * yi + ci
    chip = 2 * xi + yi

    x2 = x[0]
    tgt = loss_target[0]
    T, D = x2.shape
    N = T // CHUNK
    tt_l = _pick(T, 256, CHUNK)
    tt_h = _pick(T, 128, CHUNK)

    mid_shapes = [shapes[n][1:] for n in _MID]
    small_blk = _pack([c] + [W[n] for n in _MID], 8, f32)
    small_all = _ag8("gather_c_mid", small_blk)
    c_all = small_all[:, 0, :]
    per_chip = small_all[0::2].reshape(4, -1)[:, D:]
    mid = [dict(zip(_MID, _unpack(per_chip[j], mid_shapes))) for j in range(4)]
    catm = lambda n: jnp.concatenate([mid[j][n] for j in range(4)], axis=1)
    conv_gdn_f, conv_ffn_f = catm("conv_gdn"), catm("conv_ffn")
    w2f, a2f, g2f = catm("w2"), catm("a2"), catm("g2")

    win_s, wbg_s, wbr_s, wout_s, wfi_s, wfo_s = _ag4_multi([sq(W[n]).astype(bf16) for n in _BIG])
    win_p = _win_pad(jnp.concatenate([win_s[j] for j in range(4)], axis=1))
    wout_f = wout_s.reshape(D, D)
    wfo = wfo_s.reshape(D_FF, D)
    zpad = lambda a, top, bot: jnp.pad(a, ((top, bot), (0, 0)))
    w2p, a2p, g2p = zpad(w2f, 0, 64), zpad(a2f, 64, 0), zpad(g2f, 0, 96)

    ncol = shapes["w_ada"][2]
    b_cols = lax.dynamic_slice(sq(W["b_ada"]), (0, chip * ncol), (1, ncol))
    cond16, mod_cols = _ada_fwd(jnp.pad(c_all, ((0, 8), (0, 0))), sq(W["w_ada"]), b_cols)
    mod_all = _ag8("gather_mod", mod_cols[:8])
    mod_mine = lax.dynamic_slice(mod_all[0::2], (0, dev, 0), (4, 1, ncol)).reshape(1, 4 * ncol)
    shift1, scale1, gate1, shift2, scale2, gate2 = [mod_mine[:, i * D:(i + 1) * D] for i in range(6)]

    seg = _seg_matrix(WIDTH, HEAD_DIM)
    norm1 = [sq(W["norm1_w"]), shift1, scale1]
    h1 = _stage_fwd("norm_mod1", _fn_norm_mod, [_whole(x2)], norm1, [(D, bf16)], tt_l)[0]
    p = _matmul("in_proj", h1, win_p, "nn")

    cgq = [row(conv_gdn_f[j, part * WIDTH:(part + 1) * WIDTH]) for part in range(3) for j in range(4)]
    lane_pad = lambda a: jnp.pad(row(a), ((0, 0), (8, LANES - 16)))
    gdn_pre_ps = cgq + [lane_pad(W["a_log"]), lane_pad(W["dt_bias"]), seg, _chunk_tri(tt_h, CHUNK)]
    gdn_pre_ts = [(p, WIDTH, part, (0, 1, 2, 3)) for part in range(3)] + [(p, LANES, 29, None)]
    q_, k_, v_, beta_t, gc_t = _stage_fwd("gdn_pre", _fn_gdn_pre, gdn_pre_ts, gdn_pre_ps,
                                          [(WIDTH, f32)] * 3 + [(LANES, f32)] * 2, tt_h)
    o_, gdn_hist = _scan_fwd("gdn_scan", _gdn_chunk, [q_, k_, v_], [beta_t, gc_t])
    ow512 = jnp.tile(row(W["onorm_gdn"]), (1, HEADS))
    gdn_post_ts = [_whole(o_), (p, WIDTH, 3, None)]
    ya = _stage_fwd("gdn_post", _fn_gdn_post, gdn_post_ts, [ow512, seg], [(WIDTH, bf16)], tt_l)[0]

    mu = sq(W["mu_rwkv"])
    rw_ps = [mu[:, 0:512], mu[:, 512:1024], mu[:, 1024:1536], mu[:, 1536:1664], jnp.pad(mu[:, 1664:1824], ((0, 0), (0, 96))),
             sq(W["w0"]), w2p, sq(W["a0"]), a2p, g2p, sq(W["k_k"]), sq(W["k_a"]), seg]
    rw_ts = [(p, WIDTH, 4, (0, 1)), (p, WIDTH, 5, (0, 1)), (p, WIDTH, 6, (0, 1)), (p, LANES, 28, (0, 1)),
             (p, 256, 15, (0, 1))]
    rw_out = _stage_fwd("rwkv_pre", _fn_rwkv_pre, rw_ts, rw_ps, [(WIDTH, f32)] * 7, tt_h)
    r_, lw_, k2_, vv_, na_, b_, g_ = rw_out
    rw_ins = [r_, lw_, k2_, vv_, na_, b_]
    y_, rw_hist = _scan_fwd("rwkv_scan", _rwkv_chunk, rw_ins)
    rwp_ps = [sq(W["lnx_w"]), sq(W["lnx_b"]), row(W["r_k"]), seg]
    rwp_ts = [_whole(y_), _whole(r_), _whole(k2_), _whole(vv_), _whole(g_)]
    yb = _stage_fwd("rwkv_post", _fn_rwkv_post, rwp_ts, rwp_ps, [(WIDTH, bf16)], tt_l)[0]

    big_a = _matmul("branch_gdn", ya, wbg_s, "nn", shards=4)
    big_b = _matmul("branch_rwkv", yb, wbr_s, "nn", shards=4)
    merge_ts = [(p, D, 4, None), (p, D, 5, None), _whole(big_a), _whole(big_b)]
    merged = _stage_fwd("merge", _fn_merge, merge_ts, [], [(D, bf16)], tt_l)[0]
    mo = _matmul("out_proj", merged, wout_f, "nn")
    norm2 = [gate1, sq(W["norm2_w"]), shift2, scale2]
    x1, h2 = _stage_fwd("resid_norm_mod2", _fn_resid_norm_mod, [_whole(x2), _whole(mo)], norm2, [(D, f32), (D, bf16)], tt_l)
    f = _matmul("ffn_in", h2, wfi_s, "nn", shards=4, tm=512, tn=1408)
    cg_ps = [row(conv_ffn_f[j]) for j in range(3)]
    cg_ts = [(f, D_FF, 0, (0, 1, 2)), (f, D_FF, 1, None)]
    act = _stage_fwd("convglu", _fn_convglu, cg_ts, cg_ps, [(D_FF, bf16)], tt_h)[0]
    fo = _matmul("ffn_out", act, wfo, "nn", tk=1408)

    dx1_a, dfo, dgate2, dnormf, loss_part = _final_stage(x1, fo, tgt, gate2, row(W["norm_f_w"]), tt_l)

    dact = _matmul("d_act", dfo, wfo, "nt", tm=512, tn=1408)
    g_wfo = _matmul("g_ffn_out", act, dfo, "tn", tm=1408)
    (dgate, dup), dcf = _stage_bwd("convglu_bwd", _fn_convglu, cg_ts, cg_ps, [[_whole(dact)]], tt_h,
                                   [True] * 2, [True] * 3)
    df = jnp.concatenate([dgate, dup], axis=1)
    dh2 = _matmul("d_h2", df, wfi_s, "nt", shards=4, tm=512, tk=1408)
    g_wfi = _matmul("g_ffn_in", h2, df, "tn", shards=4, tn=1408)
    (dx_a, dmo), (dgate1, dnorm2, dshift2, dscale2) = _stage_bwd(
        "resid_norm_mod2_bwd", _fn_resid_norm_mod, [_whole(x2), _whole(mo)], norm2,
        [[_whole(dx1_a)], [_whole(dh2)]], tt_l, [True, True], [True] * 4)
    dmerged = _matmul("d_merged", dmo, wout_f, "nt")
    g_wout = _matmul("g_out_proj", merged, dmo, "tn")
    (dgla, dglb, dbig_a, dbig_b), _ = _stage_bwd("merge_bwd", _fn_merge, merge_ts, [], [[_whole(dmerged)]], tt_l,
                                                 [True] * 4, [])
    dya = _matmul("d_ya", dbig_a, wbg_s, "nt", shards=4)
    g_wbg = _matmul("g_branch_gdn", ya, dbig_a, "tn", shards=4)
    dyb = _matmul("d_yb", dbig_b, wbr_s, "nt", shards=4)
    g_wbr = _matmul("g_branch_rwkv", yb, dbig_b, "tn", shards=4)

    (dy_, dr_p, dk2_p, dv_p, dg_p), (dlnxw, dlnxb, drk) = _stage_bwd(
        "rwkv_post_bwd", _fn_rwkv_post, rwp_ts, rwp_ps, [[_whole(dyb)]], tt_l, [True] * 5, [True, True, True, False])
    dr_c, dlw_c, dk2_c, dv_c, dna_c, db_c = _scan_bwd("rwkv_scan_bwd", _rwkv_chunk, rw_ins, [], rw_hist, dy_)
    rw_cots = [[_whole(dr_p), _whole(dr_c)], [_whole(dlw_c)], [_whole(dk2_p), _whole(dk2_c)],
               [_whole(dv_p), _whole(dv_c)], [_whole(dna_c)], [_whole(db_c)], [_whole(dg_p)]]
    d_rkv, rw_dp = _stage_bwd("rwkv_pre_bwd", _fn_rwkv_pre, rw_ts, rw_ps, rw_cots, tt_h, [True] * 5,
                              [True] * 12 + [False])
    dmu_r, dmu_k, dmu_v, dmu_l, dmu_g, dw0, dw2p, da0, da2p, dg2p, dkk, dka = rw_dp

    (do_, dz), (dow512,) = _stage_bwd("gdn_post_bwd", _fn_gdn_post, gdn_post_ts, [ow512, seg], [[_whole(dya)]], tt_l,
                                      [True, True], [True, False])
    d_gdn = _scan_bwd("gdn_scan_bwd", _gdn_chunk, [q_, k_, v_], [beta_t, gc_t], gdn_hist, do_)
    gdn_cots = [[_whole(a)] for a in d_gdn]
    (*d_qkv, dba), gdn_dp = _stage_bwd("gdn_pre_bwd", _fn_gdn_pre, gdn_pre_ts, gdn_pre_ps, gdn_cots, tt_h, [True] * 4,
                                       [True] * 14 + [False, False])

    dp = jnp.concatenate([*d_qkv, dz, *d_rkv[0:4], dba, d_rkv[4], dgla, dglb], axis=1)
    dh1 = _matmul("d_h1", dp, win_p, "nt")
    g_win = _win_unpad(_matmul("g_in_proj", h1, dp, "tn"))
    (dx_b,), (dnorm1, dshift1, dscale1) = _stage_bwd("norm_mod1_bwd", _fn_norm_mod, [_whole(x2)], norm1,
                                                     [[_whole(dh1)]], tt_l, [True], [True] * 3)
    grad_x = _add_n("grad_x", [dx_a, dx_b], tt_l)

    dmod = jnp.concatenate([dshift1, dscale1, dgate1, dshift2, dscale2, dgate2], axis=1)
    g_conv_gdn = jnp.concatenate([jnp.concatenate([gdn_dp[4 * part + j] for part in range(3)], axis=1) for j in range(4)], axis=0)
    g_conv_ffn = jnp.concatenate(dcf, axis=0)
    g_mu = jnp.concatenate([dmu_r, dmu_k, dmu_v, dmu_l, dmu_g[:, :160]], axis=1)
    small_parts = {"b_ada": dmod, "norm1_w": dnorm1, "a_log": gdn_dp[12][:, 8:16], "dt_bias": gdn_dp[13][:, 8:16],
                   "mu_rwkv": g_mu, "w0": dw0, "a0": da0, "k_k": dkk, "k_a": dka, "r_k": drk, "lnx_w": dlnxw,
                   "lnx_b": dlnxb, "norm2_w": dnorm2, "norm_f_w": dnormf}
    small_names = [n for n in _SMALL if n != "onorm_gdn"]
    mid_full = [g_conv_gdn, g_conv_ffn, dw2p[0:64], da2p[64:128], dg2p[0:160]]
    body_rows = _pack([small_parts[n] for n in small_names] + [loss_part[:, 0:1]] + mid_full, 1, f32)
    head_row = body_rows.shape[0]
    small_g = jnp.concatenate([body_rows, jnp.pad(dow512, ((0, 0), (0, PACK_W - WIDTH)))], axis=0)
    small_g = jnp.pad(small_g, ((0, -small_g.shape[0] % 8), (0, 0)))
    small_all_g = _ag8("gather_small_grads", small_g)
    small_sum, head_sum = _sum_devices(small_all_g, head_row)
    small_shapes = [shapes[n][1:] if n != "norm_f_w" else shapes[n] for n in small_names]
    un = _unpack(small_sum, small_shapes + [(1,)] + [g.shape for g in mid_full])
    small_grads = dict(zip(small_names, un))
    loss = un[len(small_names)].reshape(())
    small_grads["onorm_gdn"] = head_sum[0, 0:HEAD_DIM]
    for n, g in zip(_MID, un[len(small_names) + 1:]):
        wcols = shapes[n][2]
        small_grads[n] = lax.dynamic_slice(g, (0, chip * wcols), (g.shape[0], wcols))

    dmod_all = small_all_g[:, 0:6, :].reshape(8, 6 * PACK_W)
    dmod_cols = lax.dynamic_slice(dmod_all, (0, chip * ncol), (8, ncol))
    g_wada = _matmul("g_w_ada", cond16, jnp.pad(dmod_cols, ((0, 8), (0, 0))), "tn")

    nwin = shapes["w_in"][2]
    g_win_s = g_win.reshape(D, 4, nwin).transpose(1, 0, 2)
    gs = [g_win_s, g_wbg, g_wbr, g_wout.reshape(4, D // 4, D), g_wfi, g_wfo.reshape(4, D_FF // 4, D)]
    recvs = _rs_sibling_swap(gs)
    pairs = [_add_half("grads_pair_sum%d" % t, g, r_, ci) for t, (g, r_) in enumerate(zip(gs, recvs))]
    others = _rs_chip_exchange([pb for _, pb in pairs])
    halves = [_sum_chip("grads_chip_sum%d" % t, pf, o_, chip, ci) for t, ((pf, _), o_) in enumerate(zip(pairs, others))]
    big_grads = dict(zip(_BIG, _rs_sibling_join(halves)))

    res = {tag: {} for tag in ("grad", "delta", "new_m", "new_v")}

    def put(n, g, d, m_, v_):
        for tag, val in zip(("grad", "delta", "new_m", "new_v"), (g, d, m_, v_)):
            res[tag][n] = val.reshape(shapes[n])

    for n in _BIG:
        put(n, big_grads[n], *_adamw("adamw_" + n, sq(W[n]), big_grads[n], sq(Mo[n]), sq(Vo[n])))
    put("w_ada", g_wada, *_adamw("adamw_w_ada", sq(W["w_ada"]), g_wada, sq(Mo["w_ada"]), sq(Vo["w_ada"])))
    rest = _SMALL + _MID
    pk = lambda d: _pack([d[n] for n in rest], 8, f32)
    sg = pk(small_grads)
    sm = _adamw("adamw_small", pk(W), sg, pk(Mo), pk(Vo))
    for tag, buf in zip(("grad", "delta", "new_m", "new_v"), (sg,) + tuple(sm)):
        res[tag].update(zip(rest, _unpack(buf, [shapes[n] for n in rest])))
    outs = [loss, grad_x.reshape(x.shape)]
    for tag in ("grad", "delta", "new_m", "new_v"):
        outs += [res[tag][n] for n in _ORDER]
    return tuple(outs)
```

```python
import functools
import math

import numpy as np
import jax
import jax.numpy as jnp
from jax import lax
from jax.experimental import pallas as pl
from jax.experimental.pallas import tpu as pltpu

f32 = jnp.float32
bf16 = jnp.bfloat16

LANES = 128
HEADS = 8
HEAD_DIM = 64
WIDTH = HEADS * HEAD_DIM
CHUNK = 64
D_FF = 2816
NORM_EPS = 1e-6
LNX_EPS = 64e-5
PACK_W = 1024
MESH_ID = pl.DeviceIdType.MESH

ADAM_LR, ADAM_B1, ADAM_B2, ADAM_EPS, ADAM_WD, ADAM_STEP = 0.001, 0.9, 0.999, 1e-08, 0.01, 10


def _pick(n, target, mult):
    if n <= target:
        return n
    best = None
    for t in range(mult, target + 1, mult):
        if n % t == 0:
            best = t
    assert best is not None, (n, target, mult)
    return best


def _split_bf16(x, n):
    parts, r = [], x
    for i in range(n):
        p = r.astype(bf16)
        parts.append(p)
        if i + 1 < n:
            r = r - p.astype(f32)
    return parts


def _xdot_r_impl(x, m, n, dims):
    acc = None
    for p in _split_bf16(x, n):
        t = lax.dot_general(p, m, dims, preferred_element_type=f32)
        acc = t if acc is None else acc + t
    return acc


def _make_xdot_r(n):
    nn = (((1,), (0,)), ((), ()))
    nt = (((1,), (1,)), ((), ()))

    @jax.custom_vjp
    def xdot(x, m):
        return _xdot_r_impl(x, m, n, nn)

    def fwd(x, m):
        return _xdot_r_impl(x, m, n, nn), m

    def bwd(m, ct):
        return _xdot_r_impl(ct, m, n, nt), jnp.zeros_like(m)

    xdot.defvjp(fwd, bwd)
    return xdot


_segsum = _make_xdot_r(2)


def _xdot_l_impl(m, x, n, dims):
    acc = None
    for p in _split_bf16(x, n):
        t = lax.dot_general(m, p, dims, preferred_element_type=f32)
        acc = t if acc is None else acc + t
    return acc


@jax.custom_vjp
def _xdot_l(m, x):
    return _xdot_l_impl(m, x, 3, (((1,), (0,)), ((), ())))


def _xdot_l_fwd(m, x):
    return _xdot_l(m, x), m


def _xdot_l_bwd(m, ct):
    return jnp.zeros_like(m), _xdot_l_impl(m, ct, 3, (((0,), (0,)), ((), ())))


_xdot_l.defvjp(_xdot_l_fwd, _xdot_l_bwd)


@jax.custom_vjp
def _bdot(x, w):
    return jnp.dot(x.astype(bf16), w.astype(bf16), preferred_element_type=f32)


def _bdot_fwd(x, w):
    return _bdot(x, w), (x, w)


def _bdot_bwd(res, ct):
    x, w = res
    c = ct.astype(bf16)
    dx = lax.dot_general(c, w.astype(bf16), (((1,), (1,)), ((), ())), preferred_element_type=f32)
    dw = lax.dot_general(x.astype(bf16), c, (((0,), (0,)), ((), ())), preferred_element_type=f32)
    return dx, dw


_bdot.defvjp(_bdot_fwd, _bdot_bwd)


def _silu(x):
    return x * jax.nn.sigmoid(x)


def _softplus(x):
    return jnp.maximum(x, 0.0) + jnp.log(1.0 + jnp.exp(-jnp.abs(x)))


def _rms(x, w, eps):
    return x * lax.rsqrt(jnp.mean(x * x, axis=-1, keepdims=True) + eps) * w


def _seg_matrix(width, seg):
    i = np.arange(width)
    return jnp.asarray((i[:, None] // seg) == (i[None, :] // seg), dtype=bf16)


def _chunk_tri(rows, chunk):
    i = np.arange(rows)
    return jnp.asarray(((i[:, None] // chunk) == (i[None, :] // chunk)) & (i[:, None] >= i[None, :]), dtype=bf16)


HALO = 8


def _full_spec(shape):
    nd = len(shape)
    return pl.BlockSpec(shape, lambda i: (0,) * nd)


def _entry_specs(entries, tt, block_of):
    specs, ops = [], []
    for arr, w, ci, shifts in entries:
        specs.append(pl.BlockSpec((tt, w), lambda i, ci=ci: (block_of(i), ci)))
        ops.append(arr)
        if shifts:
            specs.append(pl.BlockSpec((HALO, w), lambda i, ci=ci: (jnp.maximum(block_of(i) * (tt // HALO) - 1, 0), ci)))
            ops.append(arr)
    return specs, ops


def _load_entries(entries, refs, first):
    tiles, k = [], 0
    for _, w, _, shifts in entries:
        x = refs[k][...].astype(f32)
        k += 1
        if not shifts:
            tiles.append(x)
            continue
        halo = jnp.where(first, 0.0, refs[k][...].astype(f32))
        k += 1
        row = lax.broadcasted_iota(jnp.int32, (HALO, w), 0)
        for s in shifts:
            if s == 0:
                tiles.append(x)
                continue
            r = pltpu.roll(x, s, 0)
            head = jnp.where(row < s, pltpu.roll(halo, s, 0), r[0:HALO])
            tiles.append(jnp.concatenate([head, r[HALO:]], axis=0))
    return tiles


def _unshift_sum(grads, shifts, carry, tt):
    w = grads[0].shape[1]
    row = lax.broadcasted_iota(jnp.int32, (tt, w), 0)
    row8 = lax.broadcasted_iota(jnp.int32, (HALO, w), 0)
    dx, out = None, jnp.zeros((HALO, w), f32)
    for d, s in zip(grads, shifts):
        if s == 0:
            part = d
        else:
            part = jnp.where(row < tt - s, pltpu.roll(d, tt - s, 0), 0.0)
            out = out + jnp.where(row8 >= HALO - s, pltpu.roll(d[0:HALO], HALO - s, 0), 0.0)
        dx = part if dx is None else dx + part
    return jnp.concatenate([dx[:tt - HALO], dx[tt - HALO:] + carry], axis=0), out


def _stage_fwd(name, fn, tiles, params, outs, tt):
    rows = tiles[0][0].shape[0]
    npar = len(params)
    specs, ops = _entry_specs(tiles, tt, lambda i: i)
    nin = len(ops)

    def body(*refs):
        ts = _load_entries(tiles, refs[:nin], pl.program_id(0) == 0)
        ps = [r[...] for r in refs[nin:nin + npar]]
        res = fn(ps, ts)
        for r, v in zip(refs[nin + npar:], res):
            r[...] = v.astype(r.dtype)

    return pl.pallas_call(
        body, grid=(rows // tt,),
        in_specs=specs + [_full_spec(p.shape) for p in params],
        out_specs=[pl.BlockSpec((tt, w), lambda i: (i, 0)) for (w, _) in outs],
        out_shape=[jax.ShapeDtypeStruct((rows, w), dt) for (w, dt) in outs],
        compiler_params=pltpu.CompilerParams(dimension_semantics=("parallel",)),
        name=name,
    )(*ops, *params)


def _stage_bwd(name, fn, tiles, params, cots, tt, tile_grad, param_grad):
    rows = tiles[0][0].shape[0]
    nblk = rows // tt
    npar = len(params)
    block_of = lambda i: nblk - 1 - i
    specs, ops = _entry_specs(tiles, tt, block_of)
    nin = len(ops)
    flat_cots = [c for group in cots for c in group]
    groups = [len(g) for g in cots]
    ncot = len(flat_cots)
    counts = [len(e[3]) if e[3] else 1 for e in tiles]
    dt_entries = [e for e, g in zip(tiles, tile_grad) if g]
    dp_shapes = [p.shape for p, g in zip(params, param_grad) if g]
    ndt = len(dt_entries)
    carry_w = [e[1] for e in dt_entries if e[3]]
    flags = [g for g, n in zip(tile_grad, counts) for _ in range(n)]

    def body(*refs):
        i = pl.program_id(0)
        p_refs = refs[nin:nin + npar]
        c_refs = refs[nin + npar:nin + npar + ncot]
        dt_refs = refs[nin + npar + ncot:nin + npar + ncot + ndt]
        dp_refs = refs[nin + npar + ncot + ndt:nin + npar + ncot + ndt + len(dp_shapes)]
        carry_refs = refs[nin + npar + ncot + ndt + len(dp_shapes):]
        ts = _load_entries(tiles, refs[:nin], block_of(i) == 0)
        ps = [r[...] for r in p_refs]

        def f(dp, dt):
            dp, dt = iter(dp), iter(dt)
            pp = [next(dp) if g else p for p, g in zip(ps, param_grad)]
            tl = [next(dt) if g else t for t, g in zip(ts, flags)]
            return fn(pp, tl)

        _, vjp = jax.vjp(f, [p for p, g in zip(ps, param_grad) if g], [t for t, g in zip(ts, flags) if g])
        cs, j = [], 0
        for n in groups:
            acc = c_refs[j][...].astype(f32)
            for q in range(1, n):
                acc = acc + c_refs[j + q][...].astype(f32)
            cs.append(acc)
            j += n
        gp, gt = vjp(cs)

        @pl.when(i == 0)
        def _():
            for r in dp_refs:
                r[...] = jnp.zeros_like(r)
            for r in carry_refs:
                r[...] = jnp.zeros_like(r)

        gt, k, kc = list(gt), 0, 0
        for r, e, n in zip(dt_refs, dt_entries, [n for n, g in zip(counts, tile_grad) if g]):
            if e[3]:
                dx, out = _unshift_sum(gt[k:k + n], e[3], carry_refs[kc][...], tt)
                r[...] = dx
                carry_refs[kc][...] = out
                kc += 1
            else:
                r[...] = gt[k]
            k += n
        for r, v in zip(dp_refs, gp):
            r[...] += v

    res = pl.pallas_call(
        body, grid=(nblk,),
        in_specs=specs + [_full_spec(p.shape) for p in params]
        + [pl.BlockSpec((tt, w), lambda i, ci=ci: (block_of(i), ci)) for (_, w, ci, *_) in flat_cots],
        out_specs=[pl.BlockSpec((tt, e[1]), lambda i: (block_of(i), 0)) for e in dt_entries]
        + [_full_spec(s) for s in dp_shapes],
        out_shape=[jax.ShapeDtypeStruct((rows, e[1]), f32) for e in dt_entries]
        + [jax.ShapeDtypeStruct(s, f32) for s in dp_shapes],
        scratch_shapes=[pltpu.VMEM((HALO, w), f32) for w in carry_w],
        compiler_params=pltpu.CompilerParams(dimension_semantics=("arbitrary",)),
        name=name,
    )(*ops, *params, *[c[0] for c in flat_cots])
    return list(res[:ndt]), list(res[ndt:])


def _whole(a):
    return (a, a.shape[1], 0, None)


def _matmul(name, a, b, mode, out_dtype=f32, tm=1024, tn=1024, tk=1024, shards=1):
    S = shards
    if mode == "nn":
        M, K = a.shape
        w = b.shape[-1]
    elif mode == "nt":
        M = a.shape[0]
        if S > 1:
            _, N, w = b.shape
            K = S * w
        else:
            N, K = b.shape
            w = K
    else:
        K, M = a.shape
        w = b.shape[1] // S
    if mode != "nt":
        N = S * w
    tm = _pick(M, tm, LANES)
    if mode == "nt":
        tn = _pick(N, tn, LANES)
        tk = _pick(w, tk, LANES)
    else:
        tn = _pick(w, tn, LANES)
        tk = _pick(K, tk, LANES if mode == "nn" else 16)
    nk = K // tk
    nb = w // (tk if mode == "nt" else tn)
    if mode == "nn":
        a_spec = pl.BlockSpec((tm, tk), lambda i, j, k: (i, k))
        if S > 1:
            b_spec = pl.BlockSpec((1, tk, tn), lambda i, j, k: (j // nb, k, j % nb))
        else:
            b_spec = pl.BlockSpec((tk, tn), lambda i, j, k: (k, j))
        dims = (((1,), (0,)), ((), ()))
    elif mode == "nt":
        a_spec = pl.BlockSpec((tm, tk), lambda i, j, k: (i, k))
        if S > 1:
            b_spec = pl.BlockSpec((1, tn, tk), lambda i, j, k: (k // nb, j, k % nb))
        else:
            b_spec = pl.BlockSpec((tn, tk), lambda i, j, k: (j, k))
        dims = (((1,), (1,)), ((), ()))
    else:
        a_spec = pl.BlockSpec((tk, tm), lambda i, j, k: (k, i))
        b_spec = pl.BlockSpec((tk, tn), lambda i, j, k: (k, j))
        dims = (((0,), (0,)), ((), ()))
    if mode == "tn" and S > 1:
        o_spec = pl.BlockSpec((1, tm, tn), lambda i, j, k: (j // nb, i, j % nb))
        o_shape = (S, M, w)
    else:
        o_spec = pl.BlockSpec((tm, tn), lambda i, j, k: (i, j))
        o_shape = (M, N)
    b_lead = S > 1 and mode != "tn"
    o_lead = S > 1 and mode == "tn"

    def body(a_ref, b_ref, o_ref, acc_ref):
        k = pl.program_id(2)

        @pl.when(k == 0)
        def _():
            acc_ref[...] = jnp.zeros_like(acc_ref)

        bv = b_ref[0] if b_lead else b_ref[...]
        acc_ref[...] += lax.dot_general(a_ref[...].astype(bf16), bv.astype(bf16), dims, preferred_element_type=f32)

        @pl.when(k == nk - 1)
        def _():
            if o_lead:
                o_ref[0] = acc_ref[...].astype(o_ref.dtype)
            else:
                o_ref[...] = acc_ref[...].astype(o_ref.dtype)

    return pl.pallas_call(
        body, grid=(M // tm, N // tn, nk),
        in_specs=[a_spec, b_spec],
        out_specs=o_spec,
        out_shape=jax.ShapeDtypeStruct(o_shape, out_dtype),
        scratch_shapes=[pltpu.VMEM((tm, tn), f32)],
        compiler_params=pltpu.CompilerParams(dimension_semantics=("parallel", "parallel", "arbitrary")),
        name=name,
    )(a, b)


def _make_bmm(precision):
    if precision is None:
        cast, kw = (lambda v: v.astype(bf16)), {}
    else:
        cast, kw = (lambda v: v), {"precision": precision}

    def nn(a, b):
        return jnp.einsum("hij,hjk->hik", cast(a), cast(b), preferred_element_type=f32, **kw)

    def nt(a, b):
        return jnp.einsum("hik,hjk->hij", cast(a), cast(b), preferred_element_type=f32, **kw)

    def tn(a, b):
        return jnp.einsum("hki,hkj->hij", cast(a), cast(b), preferred_element_type=f32, **kw)

    if precision is not None:
        return nn, nt, tn
    nn_v, nt_v, tn_v = jax.custom_vjp(nn), jax.custom_vjp(nt), jax.custom_vjp(tn)
    keep = lambda f: (lambda a, b: (f(a, b), (a, b)))
    nn_v.defvjp(keep(nn), lambda r, ct: (nt(ct, r[1]), tn(r[0], ct)))
    nt_v.defvjp(keep(nt), lambda r, ct: (nn(ct, r[1]), tn(ct, r[0])))
    tn_v.defvjp(keep(tn), lambda r, ct: (nt(r[1], ct), nn(r[0], ct)))
    return nn_v, nt_v, tn_v


_bmm, _bmm_nt, _bmm_tn = _make_bmm(None)
_bmm_exact = _make_bmm(lax.Precision.HIGHEST)[0]


def _masks(n):
    r = lax.broadcasted_iota(jnp.int32, (n, n), 0)
    c = lax.broadcasted_iota(jnp.int32, (n, n), 1)
    return (r >= c)[None], (r > c)[None], (r == c)[None]


_INV_BLOCK = 8


def _nilpotent_inverse(m, eye):
    p = eye + m
    for _ in range(2):
        m = _bmm(m, m)
        p = p + _bmm(p, m)
    return p


def _neumann_inverse_impl(m):
    n = m.shape[1]
    assert n == _INV_BLOCK * _INV_BLOCK
    r = lax.broadcasted_iota(jnp.int32, (n, n), 0)
    c = lax.broadcasted_iota(jnp.int32, (n, n), 1)
    eye = (r == c).astype(f32)[None]
    inside = jnp.where((r // _INV_BLOCK == c // _INV_BLOCK)[None], m, 0.0)
    d_inv = _nilpotent_inverse(inside, eye)
    return _bmm(_nilpotent_inverse(_bmm(d_inv, m - inside), eye), d_inv)


@jax.custom_vjp
def _neumann_inverse(m):
    return _neumann_inverse_impl(m)


def _neumann_inverse_fwd(m):
    p = _neumann_inverse_impl(m)
    return p, p


def _neumann_inverse_bwd(p, ct):
    return (_bmm_tn(p, _bmm_nt(ct, p)),)


_neumann_inverse.defvjp(_neumann_inverse_fwd, _neumann_inverse_bwd)


@jax.custom_vjp
def _given_inverse(m, p):
    return p


_given_inverse.defvjp(lambda m, p: (p, p), lambda p, ct: (_neumann_inverse_bwd(p, ct)[0], jnp.zeros_like(p)))


def _gdn_chunk(s, q, k, v, beta, gc, gr, gl, p=None):
    n = q.shape[1]
    causal, strict, _ = _masks(n)
    decay = jnp.where(causal, jnp.exp(jnp.where(causal, gc - gr, 0.0)), 0.0)
    kb = k * beta
    vb = v * beta
    lower = jnp.where(strict, _bmm_nt(kb, k) * decay, 0.0)
    t_mat = _neumann_inverse(-lower) if p is None else _given_inverse(-lower, p)
    egc = jnp.exp(gc)
    u = _bmm(t_mat, vb)
    w = _bmm(t_mat, kb * egc)
    attn = jnp.where(causal, _bmm_nt(q, k) * decay, 0.0)
    v_new = u - _bmm(w, s)
    o = _bmm(q * egc, s) + _bmm(attn, v_new)
    k_dec = k * jnp.exp(gl - gc)
    s_new = s * jnp.exp(gl) + _bmm_tn(k_dec, v_new)
    return s_new, o, t_mat


def _rwkv_chunk(s, r, lw, k, v, a, b, p=None):
    n = r.shape[1]
    causal, strict, _ = _masks(n)
    tri = jnp.broadcast_to(causal.astype(f32), (r.shape[0], n, n))
    lc = _bmm_exact(tri, lw)
    ein = jnp.exp(lc)
    eout = jnp.exp(-lc)
    a_t = a * jnp.exp(lc - lw)
    b_t = b * eout
    k_t = k * eout
    r_t = r * ein
    a_ab = jnp.where(strict, _bmm_nt(a_t, b_t), 0.0)
    a_ak = jnp.where(strict, _bmm_nt(a_t, k_t), 0.0)
    inv = _neumann_inverse(a_ab) if p is None else _given_inverse(a_ab, p)
    u = _bmm(inv, _bmm_nt(a_t, s) + _bmm(a_ak, v))
    y = (_bmm_nt(r_t, s) + _bmm(jnp.where(causal, _bmm_nt(r_t, b_t), 0.0), u)
         + _bmm(jnp.where(causal, _bmm_nt(r_t, k_t), 0.0), v))
    e_last = jnp.exp(jnp.sum(lw, axis=1, keepdims=True))
    s_new = s * e_last + _bmm_tn(u, b_t * e_last) + _bmm_tn(v, k_t * e_last)
    return s_new, y, inv


def _heads_in(ref):
    return jnp.stack([ref[:, h * HEAD_DIM:(h + 1) * HEAD_DIM] for h in range(HEADS)], axis=0)


def _heads_out(ref, val):
    for h in range(HEADS):
        ref[:, h * HEAD_DIM:(h + 1) * HEAD_DIM] = val[h]


def _gdn_scalars(bt, gt):
    n = bt.shape[0]
    gtt = gt.T
    hs = range(HEADS)
    return [jnp.stack([bt[:, h:h + 1] for h in hs], axis=0),
            jnp.stack([gt[:, HEADS + h:HEADS + h + 1] for h in hs], axis=0),
            jnp.stack([gtt[HEADS + h:HEADS + h + 1, :] for h in hs], axis=0),
            jnp.stack([gt[n - 1:n, HEADS + h:HEADS + h + 1] for h in hs], axis=0)]


def _gdn_scalars_back(dbeta, dgc, dgr, dgl):
    n = dbeta.shape[1]
    lane = lax.broadcasted_iota(jnp.int32, (n, LANES), 1)
    row = lax.broadcasted_iota(jnp.int32, (n, LANES), 0)
    sub = lax.broadcasted_iota(jnp.int32, (LANES, n), 0)
    db = jnp.zeros((n, LANES), f32)
    dg = jnp.zeros((n, LANES), f32)
    dgt = jnp.zeros((LANES, n), f32)
    for h in range(HEADS):
        db = jnp.where(lane == h, dbeta[h], db)
        dg = jnp.where(lane == HEADS + h, dgc[h] + jnp.where(row == n - 1, dgl[h], 0.0), dg)
        dgt = jnp.where(sub == HEADS + h, dgr[h], dgt)
    return [db, dg + dgt.T]


def _scan_spec(width, n, reverse):
    if reverse:
        return pl.BlockSpec((CHUNK, width), lambda i: (n - 1 - i, 0))
    return pl.BlockSpec((CHUNK, width), lambda i: (i, 0))


def _hist_spec(n, reverse):
    blk = (1, HEADS, HEAD_DIM, HEAD_DIM)
    if reverse:
        return pl.BlockSpec(blk, lambda i: (n - 1 - i, 0, 0, 0))
    return pl.BlockSpec(blk, lambda i: (i, 0, 0, 0))


def _scan_fwd(name, fn, rows_in, scal_in=()):
    t = rows_in[0].shape[0]
    n = t // CHUNK
    nr, ns = len(rows_in), len(scal_in)

    def body(*refs):
        o_ref, sh_ref, ph_ref, s_scr = refs[nr + ns:nr + ns + 4]

        @pl.when(pl.program_id(0) == 0)
        def _():
            s_scr[...] = jnp.zeros_like(s_scr)

        s = s_scr[...]
        sh_ref[0] = s
        ins = [_heads_in(r) for r in refs[:nr]]
        if ns:
            ins += _gdn_scalars(*[r[...] for r in refs[nr:nr + ns]])
        s_new, o, p = fn(s, *ins)
        _heads_out(o_ref, o)
        ph_ref[0] = p
        s_scr[...] = s_new

    return pl.pallas_call(
        body, grid=(n,),
        in_specs=[_scan_spec(a.shape[1], n, False) for a in (*rows_in, *scal_in)],
        out_specs=[_scan_spec(WIDTH, n, False), _hist_spec(n, False), _hist_spec(n, False)],
        out_shape=[jax.ShapeDtypeStruct((t, WIDTH), f32)] + [jax.ShapeDtypeStruct((n, HEADS, HEAD_DIM, HEAD_DIM), f32)] * 2,
        scratch_shapes=[pltpu.VMEM((HEADS, HEAD_DIM, HEAD_DIM), f32)],
        compiler_params=pltpu.CompilerParams(dimension_semantics=("arbitrary",)),
        name=name,
    )(*rows_in, *scal_in)


def _scan_bwd(name, fn, rows_in, scal_in, s_hist, p_hist, d_out):
    t = rows_in[0].shape[0]
    n = t // CHUNK
    nr, ns = len(rows_in), len(scal_in)

    def body(*refs):
        sh_ref, ph_ref, do_ref = refs[nr + ns:nr + ns + 3]
        g_refs = refs[nr + ns + 3:nr + ns + 3 + nr + ns]
        ds_scr = refs[nr + ns + 3 + nr + ns]

        @pl.when(pl.program_id(0) == 0)
        def _():
            ds_scr[...] = jnp.zeros_like(ds_scr)

        ins = [_heads_in(r) for r in refs[:nr]]
        if ns:
            ins += _gdn_scalars(*[r[...] for r in refs[nr:nr + ns]])
        p = ph_ref[0]
        _, vjp = jax.vjp(lambda s, *a: fn(s, *a, p=p)[:2], sh_ref[0], *ins)
        g = vjp((ds_scr[...], _heads_in(do_ref)))
        ds_scr[...] = g[0]
        for r, v in zip(g_refs[:nr], g[1:1 + nr]):
            _heads_out(r, v)
        if ns:
            for r, v in zip(g_refs[nr:], _gdn_scalars_back(*g[1 + nr:])):
                r[...] = v

    arrs = (*rows_in, *scal_in)
    return pl.pallas_call(
        body, grid=(n,),
        in_specs=[_scan_spec(a.shape[1], n, True) for a in arrs]
        + [_hist_spec(n, True), _hist_spec(n, True), _scan_spec(WIDTH, n, True)],
        out_specs=[_scan_spec(a.shape[1], n, True) for a in arrs],
        out_shape=[jax.ShapeDtypeStruct(a.shape, f32) for a in arrs],
        scratch_shapes=[pltpu.VMEM((HEADS, HEAD_DIM, HEAD_DIM), f32)],
        compiler_params=pltpu.CompilerParams(dimension_semantics=("arbitrary",)),
        name=name,
    )(*arrs, s_hist, p_hist, d_out)


def _fn_norm_mod(ps, ts):
    nw, shift, scale = ps
    (x,) = ts
    return [_rms(x, nw, NORM_EPS) * (1.0 + scale) + shift]


def _fn_resid_norm_mod(ps, ts):
    gate, nw, shift, scale = ps
    x, mo = ts
    x1 = x + gate * mo
    return [x1, _rms(x1, nw, NORM_EPS) * (1.0 + scale) + shift]


def _fn_gdn_pre(ps, ts):
    cw = ps[:12]
    alog, dtb, seg, tri = ps[12:]
    ba = ts[12]
    outs = []
    for part in range(3):
        x = ts[4 * part:4 * part + 4]
        w = cw[4 * part:4 * part + 4]
        conv = w[3] * x[0] + w[2] * x[1] + w[1] * x[2] + w[0] * x[3]
        u = _silu(conv)
        if part < 2:
            u = u * lax.rsqrt(_segsum(u * u, seg) + 1e-6)
            if part == 0:
                u = u * (HEAD_DIM ** -0.5)
        outs.append(u)
    beta = jax.nn.sigmoid(ba)
    g = -jnp.exp(alog) * _softplus(ba + dtb)
    gc = _xdot_l(tri, g)
    return outs + [beta, gc]


def _fn_gdn_post(ps, ts):
    ow, seg = ps
    o, z = ts
    ms = _segsum(o * o, seg) * (1.0 / HEAD_DIM)
    return [o * lax.rsqrt(ms + NORM_EPS) * ow * _silu(z)]


def _fn_rwkv_pre(ps, ts):
    mu_r, mu_k, mu_v, mu_l, mu_g, w0, w2p, a0, a2p, g2p, k_k, k_a, seg = ps
    r0, r1, k0, k1, v0, v1, l0, l1, g0, g1 = ts
    xr = r0 + (r1 - r0) * mu_r
    xk = k0 + (k1 - k0) * mu_k
    xv = v0 + (v1 - v0) * mu_v
    xl = l0 + (l1 - l0) * mu_l
    xg = g0 + (g1 - g0) * mu_g
    w = -_softplus(-(w0 + _bdot(jnp.tanh(xl), w2p))) - 0.5
    lw = -jnp.exp(w)
    a = jax.nn.sigmoid(a0 + _bdot(xl, a2p))
    g = _bdot(jax.nn.sigmoid(xg), g2p)
    kk = xk * k_k
    kk = kk * lax.rsqrt(_segsum(kk * kk, seg) + 1e-6)
    k2 = xk * (1.0 + (a - 1.0) * k_a)
    return [xr, lw, k2, xv, -kk, kk * a, g]


def _fn_rwkv_post(ps, ts):
    lw_, lb_, rk, seg = ps
    y, r, k2, v, g = ts
    inv = 1.0 / HEAD_DIM
    yc = y - _segsum(y, seg) * inv
    var = _segsum(yc * yc, seg) * inv
    yn = yc * lax.rsqrt(var + LNX_EPS) * lw_ + lb_
    bonus = _segsum(r * k2 * rk, seg) * v
    return [(yn + bonus) * g]


def _fn_merge(ps, ts):
    gla, glb, ya, yb = ts
    return [jax.nn.sigmoid(gla) * ya + jax.nn.sigmoid(glb) * yb]


def _fn_convglu(ps, ts):
    c0, c1, c2 = ps
    g0, g1, g2, up = ts
    return [_silu(c2 * g0 + c1 * g1 + c0 * g2) * up]


def _fn_add(ps, ts):
    acc = ts[0]
    for t in ts[1:]:
        acc = acc + t
    return [acc]


def _add_n(name, arrs, tt):
    return _stage_fwd(name, _fn_add, [_whole(a) for a in arrs], [], [(arrs[0].shape[1], f32)], tt)[0]


def _final_stage(x1, fo, tgt, gate2, nfw, tt):
    rows, d = x1.shape

    def loss_fn(gate, nw, xa, fa, tg):
        y = _rms(xa + gate * fa, nw, NORM_EPS)
        err = (y - tg) ** 2
        return 0.5 * jnp.sum(jnp.mean(err, axis=-1, keepdims=True), axis=0, keepdims=True)

    def body(x_ref, f_ref, t_ref, g_ref, w_ref, dx_ref, df_ref, dg_ref, dw_ref, l_ref):
        i = pl.program_id(0)
        args = (g_ref[...], w_ref[...], x_ref[...], f_ref[...])
        tg = t_ref[...]
        lv, vjp = jax.vjp(lambda g, w, xa, fa: loss_fn(g, w, xa, fa, tg), *args)
        dg, dw, dx, df = vjp(jnp.ones((1, 1), f32))
        dx_ref[...] = dx
        df_ref[...] = df

        @pl.when(i == 0)
        def _():
            dg_ref[...] = jnp.zeros_like(dg_ref)
            dw_ref[...] = jnp.zeros_like(dw_ref)
            l_ref[...] = jnp.zeros_like(l_ref)

        dg_ref[...] += dg
        dw_ref[...] += dw
        l_ref[...] += jnp.broadcast_to(lv, l_ref.shape)

    row = pl.BlockSpec((tt, d), lambda i: (i, 0))
    vec = pl.BlockSpec((1, d), lambda i: (0, 0))
    return pl.pallas_call(
        body, grid=(rows // tt,),
        in_specs=[row, row, row, vec, vec],
        out_specs=[row, row, vec, vec, pl.BlockSpec((1, LANES), lambda i: (0, 0))],
        out_shape=[jax.ShapeDtypeStruct((rows, d), f32)] * 2 + [jax.ShapeDtypeStruct((1, d), f32)] * 2
        + [jax.ShapeDtypeStruct((1, LANES), f32)],
        compiler_params=pltpu.CompilerParams(dimension_semantics=("arbitrary",)),
        name="loss_head",
    )(x1, fo, tgt, gate2, nfw)


def _ada_fwd(c_all, w_shard, b_cols):
    def body(c_ref, w_ref, b_ref, cond_ref, mod_ref):
        cond = _silu(c_ref[...])
        cond_ref[...] = cond
        mod_ref[...] = jnp.dot(cond.astype(bf16), w_ref[...].astype(bf16), preferred_element_type=f32) + b_ref[...]

    n = w_shard.shape[1]
    return pl.pallas_call(
        body, out_shape=[jax.ShapeDtypeStruct(c_all.shape, f32), jax.ShapeDtypeStruct((c_all.shape[0], n), f32)],
        name="ada_fwd",
    )(c_all, w_shard, b_cols)


def _adamw(name, w, g, m, v):
    rows, width = w.shape
    tt = _pick(rows, 128, 8)
    c1 = 1.0 - ADAM_B1 ** ADAM_STEP
    c2 = 1.0 - ADAM_B2 ** ADAM_STEP

    def body(w_ref, g_ref, m_ref, v_ref, d_ref, mo_ref, vo_ref):
        gg = g_ref[...]
        mn = ADAM_B1 * m_ref[...] + (1.0 - ADAM_B1) * gg
        vn = ADAM_B2 * v_ref[...] + (1.0 - ADAM_B2) * (gg * gg)
        m_hat = mn / c1
        v_hat = vn / c2
        d_ref[...] = -ADAM_LR * (m_hat / (jnp.sqrt(v_hat) + ADAM_EPS) + ADAM_WD * w_ref[...])
        mo_ref[...] = mn
        vo_ref[...] = vn

    spec = pl.BlockSpec((tt, width), lambda i: (i, 0))
    return pl.pallas_call(
        body, grid=(rows // tt,), in_specs=[spec] * 4, out_specs=[spec] * 3,
        out_shape=[jax.ShapeDtypeStruct((rows, width), f32)] * 3,
        compiler_params=pltpu.CompilerParams(dimension_semantics=("parallel",)),
        name=name,
    )(w, g, m, v)


def _place():
    return lax.axis_index("x"), lax.axis_index("y"), lax.axis_index("c")


def _ag8(name, blk):
    m, w = blk.shape

    def body(x_ref, out_ref, send_sems, recv_sems, local_sem):
        x, y, c = _place()
        me, sibling = (x, y, c), (x, y, 1 - c)
        chips = _other_chips(x, y)

        def slot(px, py, pc):
            return out_ref.at[4 * px + 2 * py + pc]

        def copy(k, block, to, src=None):
            return pltpu.make_async_remote_copy(src_ref=slot(*block) if src is None else src, dst_ref=slot(*block),
                                                send_sem=send_sems.at[k], recv_sem=recv_sems.at[k], device_id=to,
                                                device_id_type=MESH_ID)

        mine = pltpu.make_async_copy(x_ref, slot(*me), local_sem)
        mine.start()
        first = [copy(0, me, sibling, src=x_ref)] + [copy(1 + j, me, (*chip, c), src=x_ref) for j, chip in enumerate(chips)]
        for cp in first:
            cp.start()
        passed = [copy(4 + j, (*chip, c), sibling) for j, chip in enumerate(chips)]
        for j, chip in enumerate(chips):
            copy(1 + j, (*chip, c), me).wait_recv()
            passed[j].start()
        copy(0, sibling, me).wait_recv()
        for j, chip in enumerate(chips):
            copy(4 + j, (*chip, 1 - c), me).wait_recv()
        for cp in first + passed:
            cp.wait_send()
        mine.wait()

    return pl.pallas_call(
        body, out_shape=jax.ShapeDtypeStruct((8, m, w), blk.dtype),
        in_specs=[pl.BlockSpec(memory_space=pltpu.VMEM)], out_specs=pl.BlockSpec(memory_space=pltpu.VMEM),
        scratch_shapes=[pltpu.SemaphoreType.DMA((7,)), pltpu.SemaphoreType.DMA((7,)), pltpu.SemaphoreType.DMA],
        name=name,
    )(blk)


def _other_chips(x, y):
    return [(1 - x, y), (x, 1 - y), (1 - x, 1 - y)]


_ANY = pl.BlockSpec(memory_space=pl.ANY)


def _rcopy(src, dst, send_sems, recv_sems, k, dev):
    return pltpu.make_async_remote_copy(src_ref=src, dst_ref=dst, send_sem=send_sems.at[k], recv_sem=recv_sems.at[k],
                                        device_id=dev, device_id_type=MESH_ID)


def _ag4_multi(ws):
    n = len(ws)

    def body(*refs):
        w_refs, out_refs = refs[:n], refs[n:2 * n]
        send_sems, recv_sems = refs[2 * n:]
        x, y, c = _place()
        chip = 2 * x + y
        sibling = (x, y, 1 - c)
        chips = _other_chips(x, y)
        halves = [(pl.ds(c * (w.shape[0] // 2), w.shape[0] // 2), pl.ds((1 - c) * (w.shape[0] // 2), w.shape[0] // 2))
                  for w in ws]
        copies = []
        for t in range(n):
            for k, (px, py) in enumerate(chips):
                cp = _rcopy(w_refs[t].at[halves[t][0]], out_refs[t].at[chip, halves[t][0]], send_sems, recv_sems,
                            7 * t + k, (px, py, c))
                cp.start()
                copies.append(cp)
        for t in range(n):
            own = _rcopy(w_refs[t], out_refs[t].at[chip], send_sems, recv_sems, 7 * t + 6, sibling)
            own.start()
            copies.append(own)
        for t in range(n):
            for k, (px, py) in enumerate(chips):
                blk = out_refs[t].at[2 * px + py, halves[t][0]]
                _rcopy(blk, blk, send_sems, recv_sems, 7 * t + k, (px, py, c)).wait_recv()
                fw = _rcopy(blk, blk, send_sems, recv_sems, 7 * t + 3 + k, sibling)
                fw.start()
                copies.append(fw)
        for t in range(n):
            for k, (px, py) in enumerate(chips):
                blk = out_refs[t].at[2 * px + py, halves[t][1]]
                _rcopy(blk, blk, send_sems, recv_sems, 7 * t + 3 + k, sibling).wait_recv()
            _rcopy(w_refs[t], out_refs[t].at[chip], send_sems, recv_sems, 7 * t + 6, sibling).wait_recv()
        for cp in copies:
            cp.wait_send()

    return pl.pallas_call(
        body, out_shape=[jax.ShapeDtypeStruct((4,) + w.shape, w.dtype) for w in ws],
        in_specs=[_ANY] * n, out_specs=[_ANY] * n,
        scratch_shapes=[pltpu.SemaphoreType.DMA((7 * n,)), pltpu.SemaphoreType.DMA((7 * n,))],
        name="weights_all_gather",
    )(*ws)


def _rs_sibling_swap(gs):
    n = len(gs)

    def body(*refs):
        g_refs, recv_refs = refs[:n], refs[n:2 * n]
        send_sems, recv_sems = refs[2 * n:]
        x, y, c = _place()
        copies = []
        for t in range(n):
            rh = gs[t].shape[1] // 2
            for s_ in range(4):
                cp = _rcopy(g_refs[t].at[s_, pl.ds((1 - c) * rh, rh)], recv_refs[t].at[s_], send_sems, recv_sems,
                            4 * t + s_, (x, y, 1 - c))
                cp.start()
                copies.append(cp)
        for cp in copies:
            cp.wait_recv()
        for cp in copies:
            cp.wait_send()

    return pl.pallas_call(
        body, out_shape=[jax.ShapeDtypeStruct((4, g.shape[1] // 2, g.shape[2]), g.dtype) for g in gs],
        in_specs=[_ANY] * n, out_specs=[_ANY] * n,
        scratch_shapes=[pltpu.SemaphoreType.DMA((4 * n,)), pltpu.SemaphoreType.DMA((4 * n,))],
        name="grads_sibling_swap",
    )(*gs)


def _rs_chip_exchange(ps):
    n = len(ps)

    def body(*refs):
        p_refs, recv_refs = refs[:n], refs[n:2 * n]
        send_sems, recv_sems = refs[2 * n:]
        x, y, c = _place()
        chips = _other_chips(x, y)
        copies = []
        for t in range(n):
            for k, (px, py) in enumerate(chips):
                cp = _rcopy(p_refs[t].at[2 * px + py], recv_refs[t].at[k], send_sems, recv_sems, 3 * t + k,
                            (px, py, c))
                cp.start()
                copies.append(cp)
        for cp in copies:
            cp.wait_recv()
        for cp in copies:
            cp.wait_send()

    return pl.pallas_call(
        body, out_shape=[jax.ShapeDtypeStruct((3,) + p.shape[1:], p.dtype) for p in ps],
        in_specs=[_ANY] * n, out_specs=[_ANY] * n,
        scratch_shapes=[pltpu.SemaphoreType.DMA((3 * n,)), pltpu.SemaphoreType.DMA((3 * n,))],
        name="grads_chip_exchange",
    )(*ps)


_JOIN_PIECES = 4


def _rs_sibling_join(qs):
    n = len(qs)
    npc = _JOIN_PIECES

    def body(*refs):
        q_refs, out_refs = refs[:n], refs[n:2 * n]
        send_sems, recv_sems = refs[2 * n:]
        x, y, c = _place()
        copies = []
        for t in range(n):
            rh = qs[t].shape[0] // 2
            pr = rh // npc
            for i in range(npc):
                rows = pl.ds(c * rh + i * pr, pr)
                cp = _rcopy(q_refs[t].at[rows], out_refs[t].at[rows], send_sems, recv_sems, npc * t + i,
                            (x, y, 1 - c))
                cp.start()
                copies.append(cp)
        for t in range(n):
            rh = qs[t].shape[0] // 2
            pr = rh // npc
            for i in range(npc):
                rows = pl.ds((1 - c) * rh + i * pr, pr)
                _rcopy(q_refs[t].at[rows], out_refs[t].at[rows], send_sems, recv_sems, npc * t + i,
                       (x, y, 1 - c)).wait_recv()
        for cp in copies:
            cp.wait_send()

    return pl.pallas_call(
        body, out_shape=[jax.ShapeDtypeStruct(q.shape, q.dtype) for q in qs],
        in_specs=[_ANY] * n, out_specs=[_ANY] * n, input_output_aliases={t: t for t in range(n)},
        scratch_shapes=[pltpu.SemaphoreType.DMA((npc * n,)), pltpu.SemaphoreType.DMA((npc * n,))],
        name="grads_sibling_join",
    )(*qs)


def _add_half(name, g, recv, ci):
    S, r, w = g.shape
    rh = r // 2
    tt = _pick(rh, 256, 16)
    nb = rh // tt

    def body(c_ref, a_ref, b_ref, o_ref, ob_ref):
        v = a_ref[...] + b_ref[...]
        o_ref[...] = v
        ob_ref[...] = v.astype(bf16)

    blk = pl.BlockSpec((1, tt, w), lambda s_, i, cr: (s_, i, 0))
    grid_spec = pltpu.PrefetchScalarGridSpec(
        num_scalar_prefetch=1, grid=(S, nb),
        in_specs=[pl.BlockSpec((1, tt, w), lambda s_, i, cr: (s_, cr[0] * nb + i, 0)), blk],
        out_specs=[blk, blk])
    return pl.pallas_call(body, grid_spec=grid_spec,
                          out_shape=[jax.ShapeDtypeStruct((S, rh, w), f32), jax.ShapeDtypeStruct((S, rh, w), bf16)],
                          name=name)(ci.reshape(1).astype(jnp.int32), g, recv)


def _sum_chip(name, pair, others, chip, ci):
    _, rh, w = pair.shape
    tt = _pick(rh, 128, 16)
    nb = rh // tt

    def body(chip_ref, core_ref, a_ref, b_ref, o_ref):
        o_ref[...] = ((a_ref[0] + b_ref[0].astype(f32)) + b_ref[1].astype(f32)) + b_ref[2].astype(f32)

    grid_spec = pltpu.PrefetchScalarGridSpec(
        num_scalar_prefetch=2, grid=(nb,),
        in_specs=[pl.BlockSpec((1, tt, w), lambda i, ch, co: (ch[0], i, 0)),
                  pl.BlockSpec((3, tt, w), lambda i, ch, co: (0, i, 0))],
        out_specs=pl.BlockSpec((tt, w), lambda i, ch, co: (co[0] * nb + i, 0)))
    return pl.pallas_call(body, grid_spec=grid_spec, out_shape=jax.ShapeDtypeStruct((2 * rh, w), f32),
                          name=name)(chip.reshape(1).astype(jnp.int32), ci.reshape(1).astype(jnp.int32), pair, others)


def _sum_devices(gathered, head_row):
    _, rows, width = gathered.shape

    def body(g_ref, out_ref, head_ref):
        acc = g_ref[0]
        for d in range(1, 8):
            acc = acc + g_ref[d]
        out_ref[...] = acc
        row = acc[head_row:head_row + 1, :]
        hs = row[:, 0:HEAD_DIM]
        for h in range(1, HEADS):
            hs = hs + row[:, h * HEAD_DIM:(h + 1) * HEAD_DIM]
        head_ref[...] = jnp.zeros_like(head_ref)
        head_ref[0:1, 0:HEAD_DIM] = hs

    return pl.pallas_call(
        body, out_shape=[jax.ShapeDtypeStruct((rows, width), f32), jax.ShapeDtypeStruct((8, LANES), f32)],
        name="small_grads_sum",
    )(gathered)


def _pack(arrs, rows_mult, dtype):
    flat = jnp.concatenate([a.reshape(-1).astype(dtype) for a in arrs])
    per = PACK_W * rows_mult
    total = -(-flat.shape[0] // per) * per
    return jnp.pad(flat, (0, total - flat.shape[0])).reshape(total // PACK_W, PACK_W)


def _unpack(buf, shapes):
    flat = buf.reshape(-1)
    out, off = [], 0
    for s in shapes:
        n = int(np.prod(s))
        out.append(flat[off:off + n].reshape(s))
        off += n
    return out


_BIG = ["w_in", "w_branch_gdn", "w_branch_rwkv", "w_out", "w_ffn_in", "w_ffn_out"]
_MID = ["conv_gdn", "conv_ffn", "w2", "a2", "g2"]
_SMALL =["b_ada", "norm1_w", "a_log", "dt_bias", "onorm_gdn", "mu_rwkv", "w0", "a0", "k_k", "k_a", "r_k", "lnx_w",
          "lnx_b", "norm2_w", "norm_f_w"]
_ORDER = ["w_ada", "b_ada", "norm1_w", "w_in", "conv_gdn", "a_log", "dt_bias", "onorm_gdn", "w_branch_gdn", "mu_rwkv",
          "w0", "w2", "a0", "a2", "g2", "k_k", "k_a", "r_k", "lnx_w", "lnx_b", "w_branch_rwkv", "w_out", "norm2_w",
          "w_ffn_in", "conv_ffn", "w_ffn_out", "norm_f_w"]


def _win_pad(w):
    z = lambda n: jnp.zeros((w.shape[0], n), w.dtype)
    return jnp.concatenate([w[:, 0:2048], w[:, 2064:3728], w[:, 2048:2064], z(112), w[:, 3728:3888], z(96),
                            w[:, 3888:5936]], axis=1)


def _win_unpad(g):
    return jnp.concatenate([g[:, 0:2048], g[:, 3712:3728], g[:, 2048:3712], g[:, 3840:4000], g[:, 4096:6144]], axis=1)


def kernel(x, c, w_ada, b_ada, norm1_w, w_in, conv_gdn, a_log, dt_bias, onorm_gdn, w_branch_gdn, mu_rwkv, w0, w2, a0, a2, g2, k_k, k_a, r_k, lnx_w, lnx_b, w_branch_rwkv, w_out, norm2_w, w_ffn_in, conv_ffn, w_ffn_out, norm_f_w, loss_target, m_w_ada, m_b_ada, m_norm1_w, m_w_in, m_conv_gdn, m_a_log, m_dt_bias, m_onorm_gdn, m_w_branch_gdn, m_mu_rwkv, m_w0, m_w2, m_a0, m_a2, m_g2, m_k_k, m_k_a, m_r_k, m_lnx_w, m_lnx_b, m_w_branch_rwkv, m_w_out, m_norm2_w, m_w_ffn_in, m_conv_ffn, m_w_ffn_out, m_norm_f_w, v_w_ada, v_b_ada, v_norm1_w, v_w_in, v_conv_gdn, v_a_log, v_dt_bias, v_onorm_gdn, v_w_branch_gdn, v_mu_rwkv, v_w0, v_w2, v_a0, v_a2, v_g2, v_k_k, v_k_a, v_r_k, v_lnx_w, v_lnx_b, v_w_branch_rwkv, v_w_out, v_norm2_w, v_w_ffn_in, v_conv_ffn, v_w_ffn_out, v_norm_f_w):
    args = dict(locals())
    W = {n: args[n] for n in _ORDER}
    Mo = {n: args["m_" + n] for n in _ORDER}
    Vo = {n: args["v_" + n] for n in _ORDER}
    shapes = {n: W[n].shape for n in _ORDER}
    sq = lambda a: a.reshape(a.shape[-2:]) if a.ndim == 3 else a.reshape(1, -1)
    row = lambda a: a.reshape(1, -1)

    xi, yi, ci = lax.axis_index("x"), lax.axis_index("y"), lax.axis_index("c")
    dev = 4 * xi + 2 * yi + ci
    chip = 2 * xi + yi

    x2 = x[0]
    tgt = loss_target[0]
    T, D = x2.shape
    N = T // CHUNK
    tt_l = _pick(T, 256, CHUNK)
    tt_h = _pick(T, 128, CHUNK)

    mid_shapes = [shapes[n][1:] for n in _MID]
    small_blk = _pack([c] + [W[n] for n in _MID], 8, f32)
    small_all = _ag8("gather_c_mid", small_blk)
    c_all = small_all[:, 0, :]
    per_chip = small_all[0::2].reshape(4, -1)[:, D:]
    mid = [dict(zip(_MID, _unpack(per_chip[j], mid_shapes))) for j in range(4)]
    catm = lambda n: jnp.concatenate([mid[j][n] for j in range(4)], axis=1)
    conv_gdn_f, conv_ffn_f = catm("conv_gdn"), catm("conv_ffn")
    w2f, a2f, g2f = catm("w2"), catm("a2"), catm("g2")

    win_s, wbg_s, wbr_s, wout_s, wfi_s, wfo_s = _ag4_multi([sq(W[n]).astype(bf16) for n in _BIG])
    win_p = _win_pad(jnp.concatenate([win_s[j] for j in range(4)], axis=1))
    wout_f = wout_s.reshape(D, D)
    wfo = wfo_s.reshape(D_FF, D)
    zpad = lambda a, top, bot: jnp.pad(a, ((top, bot), (0, 0)))
    w2p, a2p, g2p = zpad(w2f, 0, 64), zpad(a2f, 64, 0), zpad(g2f, 0, 96)

    ncol = shapes["w_ada"][2]
    b_cols = lax.dynamic_slice(sq(W["b_ada"]), (0, chip * ncol), (1, ncol))
    cond16, mod_cols = _ada_fwd(jnp.pad(c_all, ((0, 8), (0, 0))), sq(W["w_ada"]), b_cols)
    mod_all = _ag8("gather_mod", mod_cols[:8])
    mod_mine = lax.dynamic_slice(mod_all[0::2], (0, dev, 0), (4, 1, ncol)).reshape(1, 4 * ncol)
    shift1, scale1, gate1, shift2, scale2, gate2 = [mod_mine[:, i * D:(i + 1) * D] for i in range(6)]

    seg = _seg_matrix(WIDTH, HEAD_DIM)
    norm1 = [sq(W["norm1_w"]), shift1, scale1]
    h1 = _stage_fwd("norm_mod1", _fn_norm_mod, [_whole(x2)], norm1, [(D, bf16)], tt_l)[0]
    p = _matmul("in_proj", h1, win_p, "nn")

    cgq = [row(conv_gdn_f[j, part * WIDTH:(part + 1) * WIDTH]) for part in range(3) for j in range(4)]
    lane_pad = lambda a: jnp.pad(row(a), ((0, 0), (8, LANES - 16)))
    gdn_pre_ps = cgq + [lane_pad(W["a_log"]), lane_pad(W["dt_bias"]), seg, _chunk_tri(tt_h, CHUNK)]
    gdn_pre_ts = [(p, WIDTH, part, (0, 1, 2, 3)) for part in range(3)] + [(p, LANES, 29, None)]
    q_, k_, v_, beta_t, gc_t = _stage_fwd("gdn_pre", _fn_gdn_pre, gdn_pre_ts, gdn_pre_ps,
                                          [(WIDTH, f32)] * 3 + [(LANES, f32)] * 2, tt_h)
    o_, gdn_hist, gdn_inv = _scan_fwd("gdn_scan", _gdn_chunk, [q_, k_, v_], [beta_t, gc_t])
    ow512 = jnp.tile(row(W["onorm_gdn"]), (1, HEADS))
    gdn_post_ts = [_whole(o_), (p, WIDTH, 3, None)]
    ya = _stage_fwd("gdn_post", _fn_gdn_post, gdn_post_ts, [ow512, seg], [(WIDTH, bf16)], tt_l)[0]

    mu = sq(W["mu_rwkv"])
    rw_ps = [mu[:, 0:512], mu[:, 512:1024], mu[:, 1024:1536], mu[:, 1536:1664], jnp.pad(mu[:, 1664:1824], ((0, 0), (0, 96))),
             sq(W["w0"]), w2p, sq(W["a0"]), a2p, g2p, sq(W["k_k"]), sq(W["k_a"]), seg]
    rw_ts = [(p, WIDTH, 4, (0, 1)), (p, WIDTH, 5, (0, 1)), (p, WIDTH, 6, (0, 1)), (p, LANES, 28, (0, 1)),
             (p, 256, 15, (0, 1))]
    rw_out = _stage_fwd("rwkv_pre", _fn_rwkv_pre, rw_ts, rw_ps, [(WIDTH, f32)] * 7, tt_h)
    r_, lw_, k2_, vv_, na_, b_, g_ = rw_out
    rw_ins = [r_, lw_, k2_, vv_, na_, b_]
    y_, rw_hist, rw_inv = _scan_fwd("rwkv_scan", _rwkv_chunk, rw_ins)
    rwp_ps = [sq(W["lnx_w"]), sq(W["lnx_b"]), row(W["r_k"]), seg]
    rwp_ts = [_whole(y_), _whole(r_), _whole(k2_), _whole(vv_), _whole(g_)]
    yb = _stage_fwd("rwkv_post", _fn_rwkv_post, rwp_ts, rwp_ps, [(WIDTH, bf16)], tt_l)[0]

    big_a = _matmul("branch_gdn", ya, wbg_s, "nn", shards=4)
    big_b = _matmul("branch_rwkv", yb, wbr_s, "nn", shards=4)
    merge_ts = [(p, D, 4, None), (p, D, 5, None), _whole(big_a), _whole(big_b)]
    merged = _stage_fwd("merge", _fn_merge, merge_ts, [], [(D, bf16)], tt_l)[0]
    mo = _matmul("out_proj", merged, wout_f, "nn")
    norm2 = [gate1, sq(W["norm2_w"]), shift2, scale2]
    x1, h2 = _stage_fwd("resid_norm_mod2", _fn_resid_norm_mod, [_whole(x2), _whole(mo)], norm2, [(D, f32), (D, bf16)], tt_l)
    f = _matmul("ffn_in", h2, wfi_s, "nn", shards=4, tm=512, tn=1408)
    cg_ps = [row(conv_ffn_f[j]) for j in range(3)]
    cg_ts = [(f, D_FF, 0, (0, 1, 2)), (f, D_FF, 1, None)]
    act = _stage_fwd("convglu", _fn_convglu, cg_ts, cg_ps, [(D_FF, bf16)], tt_h)[0]
    fo = _matmul("ffn_out", act, wfo, "nn", tk=1408)

    dx1_a, dfo, dgate2, dnormf, loss_part = _final_stage(x1, fo, tgt, gate2, row(W["norm_f_w"]), tt_l)

    dact = _matmul("d_act", dfo, wfo, "nt", tm=512, tn=1408)
    g_wfo = _matmul("g_ffn_out", act, dfo, "tn", tm=1408)
    (dgate, dup), dcf = _stage_bwd("convglu_bwd", _fn_convglu, cg_ts, cg_ps, [[_whole(dact)]], tt_h,
                                   [True] * 2, [True] * 3)
    df = jnp.concatenate([dgate, dup], axis=1)
    dh2 = _matmul("d_h2", df, wfi_s, "nt", shards=4, tm=512, tk=1408)
    g_wfi = _matmul("g_ffn_in", h2, df, "tn", shards=4, tn=1408)
    (dx_a, dmo), (dgate1, dnorm2, dshift2, dscale2) = _stage_bwd(
        "resid_norm_mod2_bwd", _fn_resid_norm_mod, [_whole(x2), _whole(mo)], norm2,
        [[_whole(dx1_a)], [_whole(dh2)]], tt_l, [True, True], [True] * 4)
    dmerged = _matmul("d_merged", dmo, wout_f, "nt")
    g_wout = _matmul("g_out_proj", merged, dmo, "tn")
    (dgla, dglb, dbig_a, dbig_b), _ = _stage_bwd("merge_bwd", _fn_merge, merge_ts, [], [[_whole(dmerged)]], tt_l,
                                                 [True] * 4, [])
    dya = _matmul("d_ya", dbig_a, wbg_s, "nt", shards=4)
    g_wbg = _matmul("g_branch_gdn", ya, dbig_a, "tn", shards=4)
    dyb = _matmul("d_yb", dbig_b, wbr_s, "nt", shards=4)
    g_wbr = _matmul("g_branch_rwkv", yb, dbig_b, "tn", shards=4)

    (dy_, dr_p, dk2_p, dv_p, dg_p), (dlnxw, dlnxb, drk) = _stage_bwd(
        "rwkv_post_bwd", _fn_rwkv_post, rwp_ts, rwp_ps, [[_whole(dyb)]], tt_l, [True] * 5, [True, True, True, False])
    dr_c, dlw_c, dk2_c, dv_c, dna_c, db_c = _scan_bwd("rwkv_scan_bwd", _rwkv_chunk, rw_ins, [], rw_hist, rw_inv, dy_)
    rw_cots = [[_whole(dr_p), _whole(dr_c)], [_whole(dlw_c)], [_whole(dk2_p), _whole(dk2_c)],
               [_whole(dv_p), _whole(dv_c)], [_whole(dna_c)], [_whole(db_c)], [_whole(dg_p)]]
    d_rkv, rw_dp = _stage_bwd("rwkv_pre_bwd", _fn_rwkv_pre, rw_ts, rw_ps, rw_cots, tt_h, [True] * 5,
                              [True] * 12 + [False])
    dmu_r, dmu_k, dmu_v, dmu_l, dmu_g, dw0, dw2p, da0, da2p, dg2p, dkk, dka = rw_dp

    (do_, dz), (dow512,) = _stage_bwd("gdn_post_bwd", _fn_gdn_post, gdn_post_ts, [ow512, seg], [[_whole(dya)]], tt_l,
                                      [True, True], [True, False])
    d_gdn = _scan_bwd("gdn_scan_bwd", _gdn_chunk, [q_, k_, v_], [beta_t, gc_t], gdn_hist, gdn_inv, do_)
    gdn_cots = [[_whole(a)] for a in d_gdn]
    (*d_qkv, dba), gdn_dp = _stage_bwd("gdn_pre_bwd", _fn_gdn_pre, gdn_pre_ts, gdn_pre_ps, gdn_cots, tt_h, [True] * 4,
                                       [True] * 14 + [False, False])

    dp = jnp.concatenate([*d_qkv, dz, *d_rkv[0:4], dba, d_rkv[4], dgla, dglb], axis=1)
    dh1 = _matmul("d_h1", dp, win_p, "nt")
    g_win = _win_unpad(_matmul("g_in_proj", h1, dp, "tn"))
    (dx_b,), (dnorm1, dshift1, dscale1) = _stage_bwd("norm_mod1_bwd", _fn_norm_mod, [_whole(x2)], norm1,
                                                     [[_whole(dh1)]], tt_l, [True], [True] * 3)
    grad_x = _add_n("grad_x", [dx_a, dx_b], tt_l)

    dmod = jnp.concatenate([dshift1, dscale1, dgate1, dshift2, dscale2, dgate2], axis=1)
    g_conv_gdn = jnp.concatenate([jnp.concatenate([gdn_dp[4 * part + j] for part in range(3)], axis=1) for j in range(4)], axis=0)
    g_conv_ffn = jnp.concatenate(dcf, axis=0)
    g_mu = jnp.concatenate([dmu_r, dmu_k, dmu_v, dmu_l, dmu_g[:, :160]], axis=1)
    small_parts = {"b_ada": dmod, "norm1_w": dnorm1, "a_log": gdn_dp[12][:, 8:16], "dt_bias": gdn_dp[13][:, 8:16],
                   "mu_rwkv": g_mu, "w0": dw0, "a0": da0, "k_k": dkk, "k_a": dka, "r_k": drk, "lnx_w": dlnxw,
                   "lnx_b": dlnxb, "norm2_w": dnorm2, "norm_f_w": dnormf}
    small_names = [n for n in _SMALL if n != "onorm_gdn"]
    mid_full = [g_conv_gdn, g_conv_ffn, dw2p[0:64], da2p[64:128], dg2p[0:160]]
    body_rows = _pack([small_parts[n] for n in small_names] + [loss_part[:, 0:1]] + mid_full, 1, f32)
    head_row = body_rows.shape[0]
    small_g = jnp.concatenate([body_rows, jnp.pad(dow512, ((0, 0), (0, PACK_W - WIDTH)))], axis=0)
    small_g = jnp.pad(small_g, ((0, -small_g.shape[0] % 8), (0, 0)))
    small_all_g = _ag8("gather_small_grads", small_g)
    small_sum, head_sum = _sum_devices(small_all_g, head_row)
    small_shapes = [shapes[n][1:] if n != "norm_f_w" else shapes[n] for n in small_names]
    un = _unpack(small_sum, small_shapes + [(1,)] + [g.shape for g in mid_full])
    small_grads = dict(zip(small_names, un))
    loss = un[len(small_names)].reshape(())
    small_grads["onorm_gdn"] = head_sum[0, 0:HEAD_DIM]
    for n, g in zip(_MID, un[len(small_names) + 1:]):
        wcols = shapes[n][2]
        small_grads[n] = lax.dynamic_slice(g, (0, chip * wcols), (g.shape[0], wcols))

    dmod_all = small_all_g[:, 0:6, :].reshape(8, 6 * PACK_W)
    dmod_cols = lax.dynamic_slice(dmod_all, (0, chip * ncol), (8, ncol))
    g_wada = _matmul("g_w_ada", cond16, jnp.pad(dmod_cols, ((0, 8), (0, 0))), "tn")

    nwin = shapes["w_in"][2]
    g_win_s = g_win.reshape(D, 4, nwin).transpose(1, 0, 2)
    gs = [g_win_s, g_wbg, g_wbr, g_wout.reshape(4, D // 4, D), g_wfi, g_wfo.reshape(4, D_FF // 4, D)]
    recvs = _rs_sibling_swap(gs)
    pairs = [_add_half("grads_pair_sum%d" % t, g, r_, ci) for t, (g, r_) in enumerate(zip(gs, recvs))]
    others = _rs_chip_exchange([pb for _, pb in pairs])
    halves = [_sum_chip("grads_chip_sum%d" % t, pf, o_, chip, ci) for t, ((pf, _), o_) in enumerate(zip(pairs, others))]
    big_grads = dict(zip(_BIG, _rs_sibling_join(halves)))

    res = {tag: {} for tag in ("grad", "delta", "new_m", "new_v")}

    def put(n, g, d, m_, v_):
        for tag, val in zip(("grad", "delta", "new_m", "new_v"), (g, d, m_, v_)):
            res[tag][n] = val.reshape(shapes[n])

    for n in _BIG:
        put(n, big_grads[n], *_adamw("adamw_" + n, sq(W[n]), big_grads[n], sq(Mo[n]), sq(Vo[n])))
    put("w_ada", g_wada, *_adamw("adamw_w_ada", sq(W["w_ada"]), g_wada, sq(Mo["w_ada"]), sq(Vo["w_ada"])))
    rest = _SMALL + _MID
    pk = lambda d: _pack([d[n] for n in rest], 8, f32)
    sg = pk(small_grads)
    sm = _adamw("adamw_small", pk(W), sg, pk(Mo), pk(Vo))
    for tag, buf in zip(("grad", "delta", "new_m", "new_v"), (sg,) + tuple(sm)):
        res[tag].update(zip(rest, _unpack(buf, [shapes[n] for n in rest])))
    outs = [loss, grad_x.reshape(x.shape)]
    for tag in ("grad", "delta", "new_m", "new_v"):
        outs += [res[tag][n] for n in _ORDER]
    return tuple(outs)
```

```python
import functools
import math

import numpy as np
import jax
import jax.numpy as jnp
from jax import lax
from jax.experimental import pallas as pl
from jax.experimental.pallas import tpu as pltpu

f32 = jnp.float32
bf16 = jnp.bfloat16

LANES = 128
HEADS = 8
HEAD_DIM = 64
WIDTH = HEADS * HEAD_DIM
CHUNK = 64
D_FF = 2816
NORM_EPS = 1e-6
LNX_EPS = 64e-5
PACK_W = 1024
MESH_ID = pl.DeviceIdType.MESH

ADAM_LR, ADAM_B1, ADAM_B2, ADAM_EPS, ADAM_WD, ADAM_STEP = 0.001, 0.9, 0.999, 1e-08, 0.01, 10


def _pick(n, target, mult):
    if n <= target:
        return n
    best = None
    for t in range(mult, target + 1, mult):
        if n % t == 0:
            best = t
    assert best is not None, (n, target, mult)
    return best


def _split_bf16(x, n):
    parts, r = [], x
    for i in range(n):
        p = r.astype(bf16)
        parts.append(p)
        if i + 1 < n:
            r = r - p.astype(f32)
    return parts


def _xdot_r_impl(x, m, n, dims):
    acc = None
    for p in _split_bf16(x, n):
        t = lax.dot_general(p, m, dims, preferred_element_type=f32)
        acc = t if acc is None else acc + t
    return acc


def _make_xdot_r(n):
    nn = (((1,), (0,)), ((), ()))
    nt = (((1,), (1,)), ((), ()))

    @jax.custom_vjp
    def xdot(x, m):
        return _xdot_r_impl(x, m, n, nn)

    def fwd(x, m):
        return _xdot_r_impl(x, m, n, nn), m

    def bwd(m, ct):
        return _xdot_r_impl(ct, m, n, nt), jnp.zeros_like(m)

    xdot.defvjp(fwd, bwd)
    return xdot


_segsum = _make_xdot_r(2)


def _xdot_l_impl(m, x, n, dims):
    acc = None
    for p in _split_bf16(x, n):
        t = lax.dot_general(m, p, dims, preferred_element_type=f32)
        acc = t if acc is None else acc + t
    return acc


@jax.custom_vjp
def _xdot_l(m, x):
    return _xdot_l_impl(m, x, 3, (((1,), (0,)), ((), ())))


def _xdot_l_fwd(m, x):
    return _xdot_l(m, x), m


def _xdot_l_bwd(m, ct):
    return jnp.zeros_like(m), _xdot_l_impl(m, ct, 3, (((0,), (0,)), ((), ())))


_xdot_l.defvjp(_xdot_l_fwd, _xdot_l_bwd)


@jax.custom_vjp
def _bdot(x, w):
    return jnp.dot(x.astype(bf16), w.astype(bf16), preferred_element_type=f32)


def _bdot_fwd(x, w):
    return _bdot(x, w), (x, w)


def _bdot_bwd(res, ct):
    x, w = res
    c = ct.astype(bf16)
    dx = lax.dot_general(c, w.astype(bf16), (((1,), (1,)), ((), ())), preferred_element_type=f32)
    dw = lax.dot_general(x.astype(bf16), c, (((0,), (0,)), ((), ())), preferred_element_type=f32)
    return dx, dw


_bdot.defvjp(_bdot_fwd, _bdot_bwd)


def _silu(x):
    return x * jax.nn.sigmoid(x)


def _softplus(x):
    return jnp.maximum(x, 0.0) + jnp.log(1.0 + jnp.exp(-jnp.abs(x)))


def _rms(x, w, eps):
    return x * lax.rsqrt(jnp.mean(x * x, axis=-1, keepdims=True) + eps) * w


def _seg_matrix(width, seg):
    i = np.arange(width)
    return jnp.asarray((i[:, None] // seg) == (i[None, :] // seg), dtype=bf16)


def _chunk_tri(rows, chunk):
    i = np.arange(rows)
    return jnp.asarray(((i[:, None] // chunk) == (i[None, :] // chunk)) & (i[:, None] >= i[None, :]), dtype=bf16)


HALO = 8


def _full_spec(shape):
    nd = len(shape)
    return pl.BlockSpec(shape, lambda i: (0,) * nd)


def _entry_specs(entries, tt, block_of):
    specs, ops = [], []
    for arr, w, ci, shifts in entries:
        specs.append(pl.BlockSpec((tt, w), lambda i, ci=ci: (block_of(i), ci)))
        ops.append(arr)
        if shifts:
            specs.append(pl.BlockSpec((HALO, w), lambda i, ci=ci: (jnp.maximum(block_of(i) * (tt // HALO) - 1, 0), ci)))
            ops.append(arr)
    return specs, ops


def _load_entries(entries, refs, first):
    tiles, k = [], 0
    for _, w, _, shifts in entries:
        x = refs[k][...].astype(f32)
        k += 1
        if not shifts:
            tiles.append(x)
            continue
        halo = jnp.where(first, 0.0, refs[k][...].astype(f32))
        k += 1
        row = lax.broadcasted_iota(jnp.int32, (HALO, w), 0)
        for s in shifts:
            if s == 0:
                tiles.append(x)
                continue
            r = pltpu.roll(x, s, 0)
            head = jnp.where(row < s, pltpu.roll(halo, s, 0), r[0:HALO])
            tiles.append(jnp.concatenate([head, r[HALO:]], axis=0))
    return tiles


def _unshift_sum(grads, shifts, carry, tt):
    w = grads[0].shape[1]
    row = lax.broadcasted_iota(jnp.int32, (tt, w), 0)
    row8 = lax.broadcasted_iota(jnp.int32, (HALO, w), 0)
    dx, out = None, jnp.zeros((HALO, w), f32)
    for d, s in zip(grads, shifts):
        if s == 0:
            part = d
        else:
            part = jnp.where(row < tt - s, pltpu.roll(d, tt - s, 0), 0.0)
            out = out + jnp.where(row8 >= HALO - s, pltpu.roll(d[0:HALO], HALO - s, 0), 0.0)
        dx = part if dx is None else dx + part
    return jnp.concatenate([dx[:tt - HALO], dx[tt - HALO:] + carry], axis=0), out


def _stage_fwd(name, fn, tiles, params, outs, tt):
    rows = tiles[0][0].shape[0]
    npar = len(params)
    specs, ops = _entry_specs(tiles, tt, lambda i: i)
    nin = len(ops)

    def body(*refs):
        ts = _load_entries(tiles, refs[:nin], pl.program_id(0) == 0)
        ps = [r[...] for r in refs[nin:nin + npar]]
        res = fn(ps, ts)
        for r, v in zip(refs[nin + npar:], res):
            r[...] = v.astype(r.dtype)

    return pl.pallas_call(
        body, grid=(rows // tt,),
        in_specs=specs + [_full_spec(p.shape) for p in params],
        out_specs=[pl.BlockSpec((tt, w), lambda i: (i, 0)) for (w, _) in outs],
        out_shape=[jax.ShapeDtypeStruct((rows, w), dt) for (w, dt) in outs],
        compiler_params=pltpu.CompilerParams(dimension_semantics=("parallel",)),
        name=name,
    )(*ops, *params)


def _stage_bwd(name, fn, tiles, params, cots, tt, tile_grad, param_grad, joint=None):
    rows = tiles[0][0].shape[0]
    nblk = rows // tt
    npar = len(params)
    block_of = lambda i: nblk - 1 - i
    specs, ops = _entry_specs(tiles, tt, block_of)
    nin = len(ops)
    flat_cots = [c for group in cots for c in group]
    groups = [len(g) for g in cots]
    ncot = len(flat_cots)
    counts = [len(e[3]) if e[3] else 1 for e in tiles]
    dt_entries = [e for e, g in zip(tiles, tile_grad) if g]
    dp_shapes = [p.shape for p, g in zip(params, param_grad) if g]
    ndt = len(dt_entries)
    carry_w = [e[1] for e in dt_entries if e[3]]
    flags = [g for g, n in zip(tile_grad, counts) for _ in range(n)]
    members, j_width, j_cidx, j_buf = joint if joint else ([], 0, 0, None)
    solo = [k for k in range(ndt) if k not in members]
    nsolo, njoint, nbuf = len(solo), int(bool(members)), int(j_buf is not None)
    j_block = sum(dt_entries[k][1] for k in members)

    def body(*refs):
        i = pl.program_id(0)
        p_refs = refs[nin:nin + npar]
        c_refs = refs[nin + npar:nin + npar + ncot]
        base = nin + npar + ncot + nbuf
        dt_refs = refs[base:base + nsolo]
        joint_refs = refs[base + nsolo:base + nsolo + njoint]
        dp_refs = refs[base + nsolo + njoint:base + nsolo + njoint + len(dp_shapes)]
        carry_refs = refs[base + nsolo + njoint + len(dp_shapes):]
        ts = _load_entries(tiles, refs[:nin], block_of(i) == 0)
        ps = [r[...] for r in p_refs]

        def f(dp, dt):
            dp, dt = iter(dp), iter(dt)
            pp = [next(dp) if g else p for p, g in zip(ps, param_grad)]
            tl = [next(dt) if g else t for t, g in zip(ts, flags)]
            return fn(pp, tl)

        _, vjp = jax.vjp(f, [p for p, g in zip(ps, param_grad) if g], [t for t, g in zip(ts, flags) if g])
        cs, j = [], 0
        for n in groups:
            acc = c_refs[j][...].astype(f32)
            for q in range(1, n):
                acc = acc + c_refs[j + q][...].astype(f32)
            cs.append(acc)
            j += n
        gp, gt = vjp(cs)

        @pl.when(i == 0)
        def _():
            for r in dp_refs:
                r[...] = jnp.zeros_like(r)
            for r in carry_refs:
                r[...] = jnp.zeros_like(r)

        gt, k, kc, dxs = list(gt), 0, 0, []
        for e, n in zip(dt_entries, [n for n, g in zip(counts, tile_grad) if g]):
            if e[3]:
                dx, out = _unshift_sum(gt[k:k + n], e[3], carry_refs[kc][...], tt)
                carry_refs[kc][...] = out
                kc += 1
            else:
                dx = gt[k]
            dxs.append(dx)
            k += n
        for r, k in zip(dt_refs, solo):
            r[...] = dxs[k]
        off = 0
        for k in members:
            w = dt_entries[k][1]
            joint_refs[0][:, off:off + w] = dxs[k]
            off += w
        for r, v in zip(dp_refs, gp):
            r[...] += v

    res = pl.pallas_call(
        body, grid=(nblk,),
        in_specs=specs + [_full_spec(p.shape) for p in params]
        + [pl.BlockSpec((tt, w), lambda i, ci=ci: (block_of(i), ci)) for (_, w, ci, *_) in flat_cots]
        + [pl.BlockSpec(memory_space=pl.ANY)] * nbuf,
        out_specs=[pl.BlockSpec((tt, dt_entries[k][1]), lambda i: (block_of(i), 0)) for k in solo]
        + [pl.BlockSpec((tt, j_block), lambda i: (block_of(i), j_cidx))] * njoint
        + [_full_spec(s) for s in dp_shapes],
        out_shape=[jax.ShapeDtypeStruct((rows, dt_entries[k][1]), f32) for k in solo]
        + [jax.ShapeDtypeStruct((rows, j_width), f32)] * njoint
        + [jax.ShapeDtypeStruct(s, f32) for s in dp_shapes],
        scratch_shapes=[pltpu.VMEM((HALO, w), f32) for w in carry_w],
        input_output_aliases={nin + npar + ncot: nsolo} if nbuf else {},
        compiler_params=pltpu.CompilerParams(dimension_semantics=("arbitrary",)),
        name=name,
    )(*ops, *params, *[c[0] for c in flat_cots], *([j_buf] if nbuf else []))
    res = list(res)
    return res[:nsolo], res[nsolo + njoint:], (res[nsolo] if njoint else None)


def _whole(a):
    return (a, a.shape[1], 0, None)


def _matmul(name, a, b, mode, out_dtype=f32, tm=1024, tn=1024, tk=1024, shards=1):
    S = shards
    if mode == "nn":
        M, K = a.shape
        w = b.shape[-1]
    elif mode == "nt":
        M = a.shape[0]
        if S > 1:
            _, N, w = b.shape
            K = S * w
        else:
            N, K = b.shape
            w = K
    else:
        K, M = a.shape
        w = b.shape[1] // S
    if mode != "nt":
        N = S * w
    tm = _pick(M, tm, LANES)
    if mode == "nt":
        tn = _pick(N, tn, LANES)
        tk = _pick(w, tk, LANES)
    else:
        tn = _pick(w, tn, LANES)
        tk = _pick(K, tk, LANES if mode == "nn" else 16)
    nk = K // tk
    nb = w // (tk if mode == "nt" else tn)
    if mode == "nn":
        a_spec = pl.BlockSpec((tm, tk), lambda i, j, k: (i, k))
        if S > 1:
            b_spec = pl.BlockSpec((1, tk, tn), lambda i, j, k: (j // nb, k, j % nb))
        else:
            b_spec = pl.BlockSpec((tk, tn), lambda i, j, k: (k, j))
        dims = (((1,), (0,)), ((), ()))
    elif mode == "nt":
        a_spec = pl.BlockSpec((tm, tk), lambda i, j, k: (i, k))
        if S > 1:
            b_spec = pl.BlockSpec((1, tn, tk), lambda i, j, k: (k // nb, j, k % nb))
        else:
            b_spec = pl.BlockSpec((tn, tk), lambda i, j, k: (j, k))
        dims = (((1,), (1,)), ((), ()))
    else:
        a_spec = pl.BlockSpec((tk, tm), lambda i, j, k: (k, i))
        b_spec = pl.BlockSpec((tk, tn), lambda i, j, k: (k, j))
        dims = (((0,), (0,)), ((), ()))
    if mode == "tn" and S > 1:
        o_spec = pl.BlockSpec((1, tm, tn), lambda i, j, k: (j // nb, i, j % nb))
        o_shape = (S, M, w)
    else:
        o_spec = pl.BlockSpec((tm, tn), lambda i, j, k: (i, j))
        o_shape = (M, N)
    b_lead = S > 1 and mode != "tn"
    o_lead = S > 1 and mode == "tn"

    def body(a_ref, b_ref, o_ref, acc_ref):
        k = pl.program_id(2)

        @pl.when(k == 0)
        def _():
            acc_ref[...] = jnp.zeros_like(acc_ref)

        bv = b_ref[0] if b_lead else b_ref[...]
        acc_ref[...] += lax.dot_general(a_ref[...].astype(bf16), bv.astype(bf16), dims, preferred_element_type=f32)

        @pl.when(k == nk - 1)
        def _():
            if o_lead:
                o_ref[0] = acc_ref[...].astype(o_ref.dtype)
            else:
                o_ref[...] = acc_ref[...].astype(o_ref.dtype)

    def body_one_step(a_ref, b_ref, o_ref):
        bv = b_ref[0] if b_lead else b_ref[...]
        res = lax.dot_general(a_ref[...].astype(bf16), bv.astype(bf16), dims, preferred_element_type=f32)
        if o_lead:
            o_ref[0] = res.astype(o_ref.dtype)
        else:
            o_ref[...] = res.astype(o_ref.dtype)

    return pl.pallas_call(
        body if nk > 1 else body_one_step, grid=(M // tm, N // tn, nk),
        in_specs=[a_spec, b_spec],
        out_specs=o_spec,
        out_shape=jax.ShapeDtypeStruct(o_shape, out_dtype),
        scratch_shapes=[pltpu.VMEM((tm, tn), f32)] if nk > 1 else [],
        compiler_params=pltpu.CompilerParams(dimension_semantics=("parallel", "parallel", "arbitrary")),
        name=name,
    )(a, b)


def _make_bmm(precision):
    if precision is None:
        cast, kw = (lambda v: v.astype(bf16)), {}
    else:
        cast, kw = (lambda v: v), {"precision": precision}

    def nn(a, b):
        return jnp.einsum("hij,hjk->hik", cast(a), cast(b), preferred_element_type=f32, **kw)

    def nt(a, b):
        return jnp.einsum("hik,hjk->hij", cast(a), cast(b), preferred_element_type=f32, **kw)

    def tn(a, b):
        return jnp.einsum("hki,hkj->hij", cast(a), cast(b), preferred_element_type=f32, **kw)

    if precision is not None:
        return nn, nt, tn
    nn_v, nt_v, tn_v = jax.custom_vjp(nn), jax.custom_vjp(nt), jax.custom_vjp(tn)
    keep = lambda f: (lambda a, b: (f(a, b), (a, b)))
    nn_v.defvjp(keep(nn), lambda r, ct: (nt(ct, r[1]), tn(r[0], ct)))
    nt_v.defvjp(keep(nt), lambda r, ct: (nn(ct, r[1]), tn(ct, r[0])))
    tn_v.defvjp(keep(tn), lambda r, ct: (nt(r[1], ct), nn(r[0], ct)))
    return nn_v, nt_v, tn_v


_bmm, _bmm_nt, _bmm_tn = _make_bmm(None)
_bmm_exact = _make_bmm(lax.Precision.HIGH)[0]


def _masks(n):
    r = lax.broadcasted_iota(jnp.int32, (n, n), 0)
    c = lax.broadcasted_iota(jnp.int32, (n, n), 1)
    return (r >= c)[None], (r > c)[None], (r == c)[None]


_INV_BLOCK = 8


def _nilpotent_inverse(m, eye):
    p = eye + m
    for _ in range(2):
        m = _bmm(m, m)
        p = p + _bmm(p, m)
    return p


def _neumann_inverse_impl(m):
    n = m.shape[1]
    assert n == _INV_BLOCK * _INV_BLOCK
    r = lax.broadcasted_iota(jnp.int32, (n, n), 0)
    c = lax.broadcasted_iota(jnp.int32, (n, n), 1)
    eye = (r == c).astype(f32)[None]
    inside = jnp.where((r // _INV_BLOCK == c // _INV_BLOCK)[None], m, 0.0)
    d_inv = _nilpotent_inverse(inside, eye)
    return _bmm(_nilpotent_inverse(_bmm(d_inv, m - inside), eye), d_inv)


@jax.custom_vjp
def _neumann_inverse(m):
    return _neumann_inverse_impl(m)


def _neumann_inverse_fwd(m):
    p = _neumann_inverse_impl(m)
    return p, p


def _neumann_inverse_bwd(p, ct):
    return (_bmm_tn(p, _bmm_nt(ct, p)),)


_neumann_inverse.defvjp(_neumann_inverse_fwd, _neumann_inverse_bwd)


@jax.custom_vjp
def _given_inverse(m, p):
    return p


_given_inverse.defvjp(lambda m, p: (p, p), lambda p, ct: (_neumann_inverse_bwd(p, ct)[0], jnp.zeros_like(p)))


def _gdn_chunk(s, q, k, v, beta, gc, gr, gl, p=None):
    n = q.shape[1]
    causal, strict, _ = _masks(n)
    decay = jnp.where(causal, jnp.exp(jnp.where(causal, gc - gr, 0.0)), 0.0)
    kb = k * beta
    vb = v * beta
    lower = jnp.where(strict, _bmm_nt(kb, k) * decay, 0.0)
    t_mat = _neumann_inverse(-lower) if p is None else _given_inverse(-lower, p)
    egc = jnp.exp(gc)
    u = _bmm(t_mat, vb)
    w = _bmm(t_mat, kb * egc)
    attn = jnp.where(causal, _bmm_nt(q, k) * decay, 0.0)
    v_new = u - _bmm(w, s)
    o = _bmm(q * egc, s) + _bmm(attn, v_new)
    k_dec = k * jnp.exp(gl - gc)
    s_new = s * jnp.exp(gl) + _bmm_tn(k_dec, v_new)
    return s_new, o, t_mat


def _rwkv_chunk(s, r, lw, k, v, a, b, p=None):
    n = r.shape[1]
    causal, strict, _ = _masks(n)
    tri = jnp.broadcast_to(causal.astype(f32), (r.shape[0], n, n))
    lc = _bmm_exact(tri, lw)
    ein = jnp.exp(lc)
    eout = jnp.exp(-lc)
    a_t = a * jnp.exp(lc - lw)
    b_t = b * eout
    k_t = k * eout
    r_t = r * ein
    a_ab = jnp.where(strict, _bmm_nt(a_t, b_t), 0.0)
    a_ak = jnp.where(strict, _bmm_nt(a_t, k_t), 0.0)
    inv = _neumann_inverse(a_ab) if p is None else _given_inverse(a_ab, p)
    u = _bmm(inv, _bmm_nt(a_t, s) + _bmm(a_ak, v))
    y = (_bmm_nt(r_t, s) + _bmm(jnp.where(causal, _bmm_nt(r_t, b_t), 0.0), u)
         + _bmm(jnp.where(causal, _bmm_nt(r_t, k_t), 0.0), v))
    e_last = jnp.exp(jnp.sum(lw, axis=1, keepdims=True))
    s_new = s * e_last + _bmm_tn(u, b_t * e_last) + _bmm_tn(v, k_t * e_last)
    return s_new, y, inv


def _heads_in(ref):
    return jnp.stack([ref[:, h * HEAD_DIM:(h + 1) * HEAD_DIM] for h in range(HEADS)], axis=0)


def _heads_out(ref, val):
    for h in range(HEADS):
        ref[:, h * HEAD_DIM:(h + 1) * HEAD_DIM] = val[h]


def _gdn_scalars(bt, gt):
    n = bt.shape[0]
    gtt = gt.T
    hs = range(HEADS)
    return [jnp.stack([bt[:, h:h + 1] for h in hs], axis=0),
            jnp.stack([gt[:, HEADS + h:HEADS + h + 1] for h in hs], axis=0),
            jnp.stack([gtt[HEADS + h:HEADS + h + 1, :] for h in hs], axis=0),
            jnp.stack([gt[n - 1:n, HEADS + h:HEADS + h + 1] for h in hs], axis=0)]


def _gdn_scalars_back(dbeta, dgc, dgr, dgl):
    n = dbeta.shape[1]
    lane = lax.broadcasted_iota(jnp.int32, (n, LANES), 1)
    row = lax.broadcasted_iota(jnp.int32, (n, LANES), 0)
    sub = lax.broadcasted_iota(jnp.int32, (LANES, n), 0)
    db = jnp.zeros((n, LANES), f32)
    dg = jnp.zeros((n, LANES), f32)
    dgt = jnp.zeros((LANES, n), f32)
    for h in range(HEADS):
        db = jnp.where(lane == h, dbeta[h], db)
        dg = jnp.where(lane == HEADS + h, dgc[h] + jnp.where(row == n - 1, dgl[h], 0.0), dg)
        dgt = jnp.where(sub == HEADS + h, dgr[h], dgt)
    return [db, dg + dgt.T]


def _scan_spec(width, n, reverse):
    if reverse:
        return pl.BlockSpec((CHUNK, width), lambda i: (n - 1 - i, 0))
    return pl.BlockSpec((CHUNK, width), lambda i: (i, 0))


def _hist_spec(n, reverse):
    blk = (1, HEADS, HEAD_DIM, HEAD_DIM)
    if reverse:
        return pl.BlockSpec(blk, lambda i: (n - 1 - i, 0, 0, 0))
    return pl.BlockSpec(blk, lambda i: (i, 0, 0, 0))


def _scan_fwd(name, fn, rows_in, scal_in=()):
    t = rows_in[0].shape[0]
    n = t // CHUNK
    nr, ns = len(rows_in), len(scal_in)

    def body(*refs):
        o_ref, sh_ref, ph_ref, s_scr = refs[nr + ns:nr + ns + 4]

        @pl.when(pl.program_id(0) == 0)
        def _():
            s_scr[...] = jnp.zeros_like(s_scr)

        s = s_scr[...]
        sh_ref[0] = s
        ins = [_heads_in(r) for r in refs[:nr]]
        if ns:
            ins += _gdn_scalars(*[r[...] for r in refs[nr:nr + ns]])
        s_new, o, p = fn(s, *ins)
        _heads_out(o_ref, o)
        ph_ref[0] = p
        s_scr[...] = s_new

    return pl.pallas_call(
        body, grid=(n,),
        in_specs=[_scan_spec(a.shape[1], n, False) for a in (*rows_in, *scal_in)],
        out_specs=[_scan_spec(WIDTH, n, False), _hist_spec(n, False), _hist_spec(n, False)],
        out_shape=[jax.ShapeDtypeStruct((t, WIDTH), f32)] + [jax.ShapeDtypeStruct((n, HEADS, HEAD_DIM, HEAD_DIM), f32)] * 2,
        scratch_shapes=[pltpu.VMEM((HEADS, HEAD_DIM, HEAD_DIM), f32)],
        compiler_params=pltpu.CompilerParams(dimension_semantics=("arbitrary",)),
        name=name,
    )(*rows_in, *scal_in)


def _scan_bwd(name, fn, rows_in, scal_in, s_hist, p_hist, d_out):
    t = rows_in[0].shape[0]
    n = t // CHUNK
    nr, ns = len(rows_in), len(scal_in)

    def body(*refs):
        sh_ref, ph_ref, do_ref = refs[nr + ns:nr + ns + 3]
        g_refs = refs[nr + ns + 3:nr + ns + 3 + nr + ns]
        ds_scr = refs[nr + ns + 3 + nr + ns]

        @pl.when(pl.program_id(0) == 0)
        def _():
            ds_scr[...] = jnp.zeros_like(ds_scr)

        ins = [_heads_in(r) for r in refs[:nr]]
        if ns:
            ins += _gdn_scalars(*[r[...] for r in refs[nr:nr + ns]])
        p = ph_ref[0]
        _, vjp = jax.vjp(lambda s, *a: fn(s, *a, p=p)[:2], sh_ref[0], *ins)
        g = vjp((ds_scr[...], _heads_in(do_ref)))
        ds_scr[...] = g[0]
        for r, v in zip(g_refs[:nr], g[1:1 + nr]):
            _heads_out(r, v)
        if ns:
            for r, v in zip(g_refs[nr:], _gdn_scalars_back(*g[1 + nr:])):
                r[...] = v

    arrs = (*rows_in, *scal_in)
    return pl.pallas_call(
        body, grid=(n,),
        in_specs=[_scan_spec(a.shape[1], n, True) for a in arrs]
        + [_hist_spec(n, True), _hist_spec(n, True), _scan_spec(WIDTH, n, True)],
        out_specs=[_scan_spec(a.shape[1], n, True) for a in arrs],
        out_shape=[jax.ShapeDtypeStruct(a.shape, f32) for a in arrs],
        scratch_shapes=[pltpu.VMEM((HEADS, HEAD_DIM, HEAD_DIM), f32)],
        compiler_params=pltpu.CompilerParams(dimension_semantics=("arbitrary",)),
        name=name,
    )(*arrs, s_hist, p_hist, d_out)


def _fn_norm_mod(ps, ts):
    nw, shift, scale = ps
    (x,) = ts
    return [_rms(x, nw, NORM_EPS) * (1.0 + scale) + shift]


def _fn_resid_norm_mod(ps, ts):
    gate, nw, shift, scale = ps
    x, mo = ts
    x1 = x + gate * mo
    return [x1, _rms(x1, nw, NORM_EPS) * (1.0 + scale) + shift]


def _fn_gdn_pre(ps, ts):
    cw = ps[:12]
    alog, dtb, seg, tri = ps[12:]
    ba = ts[12]
    outs = []
    for part in range(3):
        x = ts[4 * part:4 * part + 4]
        w = cw[4 * part:4 * part + 4]
        conv = w[3] * x[0] + w[2] * x[1] + w[1] * x[2] + w[0] * x[3]
        u = _silu(conv)
        if part < 2:
            u = u * lax.rsqrt(_segsum(u * u, seg) + 1e-6)
            if part == 0:
                u = u * (HEAD_DIM ** -0.5)
        outs.append(u)
    beta = jax.nn.sigmoid(ba)
    g = -jnp.exp(alog) * _softplus(ba + dtb)
    gc = _xdot_l(tri, g)
    return outs + [beta, gc]


def _fn_gdn_post(ps, ts):
    ow, seg = ps
    o, z = ts
    ms = _segsum(o * o, seg) * (1.0 / HEAD_DIM)
    return [o * lax.rsqrt(ms + NORM_EPS) * ow * _silu(z)]


def _fn_rwkv_pre(ps, ts):
    mu_r, mu_k, mu_v, mu_l, mu_g, w0, w2p, a0, a2p, g2p, k_k, k_a, seg = ps
    r0, r1, k0, k1, v0, v1, l0, l1, g0, g1 = ts
    xr = r0 + (r1 - r0) * mu_r
    xk = k0 + (k1 - k0) * mu_k
    xv = v0 + (v1 - v0) * mu_v
    xl = l0 + (l1 - l0) * mu_l
    xg = g0 + (g1 - g0) * mu_g
    w = -_softplus(-(w0 + _bdot(jnp.tanh(xl), w2p))) - 0.5
    lw = -jnp.exp(w)
    a = jax.nn.sigmoid(a0 + _bdot(xl, a2p))
    g = _bdot(jax.nn.sigmoid(xg), g2p)
    kk = xk * k_k
    kk = kk * lax.rsqrt(_segsum(kk * kk, seg) + 1e-6)
    k2 = xk * (1.0 + (a - 1.0) * k_a)
    return [xr, lw, k2, xv, -kk, kk * a, g]


def _fn_rwkv_post(ps, ts):
    lw_, lb_, rk, seg = ps
    y, r, k2, v, g = ts
    inv = 1.0 / HEAD_DIM
    yc = y - _segsum(y, seg) * inv
    var = _segsum(yc * yc, seg) * inv
    yn = yc * lax.rsqrt(var + LNX_EPS) * lw_ + lb_
    bonus = _segsum(r * k2 * rk, seg) * v
    return [(yn + bonus) * g]


def _fn_merge(ps, ts):
    gla, glb, ya, yb = ts
    return [jax.nn.sigmoid(gla) * ya + jax.nn.sigmoid(glb) * yb]


def _fn_convglu(ps, ts):
    c0, c1, c2 = ps
    g0, g1, g2, up = ts
    return [_silu(c2 * g0 + c1 * g1 + c0 * g2) * up]


def _fn_add(ps, ts):
    acc = ts[0]
    for t in ts[1:]:
        acc = acc + t
    return [acc]


def _add_n(name, arrs, tt):
    return _stage_fwd(name, _fn_add, [_whole(a) for a in arrs], [], [(arrs[0].shape[1], f32)], tt)[0]


def _final_stage(x1, fo, tgt, gate2, nfw, tt):
    rows, d = x1.shape

    def loss_fn(gate, nw, xa, fa, tg):
        y = _rms(xa + gate * fa, nw, NORM_EPS)
        err = (y - tg) ** 2
        return 0.5 * jnp.sum(jnp.mean(err, axis=-1, keepdims=True), axis=0, keepdims=True)

    def body(x_ref, f_ref, t_ref, g_ref, w_ref, dx_ref, df_ref, dg_ref, dw_ref, l_ref):
        i = pl.program_id(0)
        args = (g_ref[...], w_ref[...], x_ref[...], f_ref[...])
        tg = t_ref[...]
        lv, vjp = jax.vjp(lambda g, w, xa, fa: loss_fn(g, w, xa, fa, tg), *args)
        dg, dw, dx, df = vjp(jnp.ones((1, 1), f32))
        dx_ref[...] = dx
        df_ref[...] = df

        @pl.when(i == 0)
        def _():
            dg_ref[...] = jnp.zeros_like(dg_ref)
            dw_ref[...] = jnp.zeros_like(dw_ref)
            l_ref[...] = jnp.zeros_like(l_ref)

        dg_ref[...] += dg
        dw_ref[...] += dw
        l_ref[...] += jnp.broadcast_to(lv, l_ref.shape)

    row = pl.BlockSpec((tt, d), lambda i: (i, 0))
    vec = pl.BlockSpec((1, d), lambda i: (0, 0))
    return pl.pallas_call(
        body, grid=(rows // tt,),
        in_specs=[row, row, row, vec, vec],
        out_specs=[row, row, vec, vec, pl.BlockSpec((1, LANES), lambda i: (0, 0))],
        out_shape=[jax.ShapeDtypeStruct((rows, d), f32)] * 2 + [jax.ShapeDtypeStruct((1, d), f32)] * 2
        + [jax.ShapeDtypeStruct((1, LANES), f32)],
        compiler_params=pltpu.CompilerParams(dimension_semantics=("arbitrary",)),
        name="loss_head",
    )(x1, fo, tgt, gate2, nfw)


def _ada_fwd(c_all, w_shard, b_cols):
    def body(c_ref, w_ref, b_ref, cond_ref, mod_ref):
        cond = _silu(c_ref[...])
        cond_ref[...] = cond
        mod_ref[...] = jnp.dot(cond.astype(bf16), w_ref[...].astype(bf16), preferred_element_type=f32) + b_ref[...]

    n = w_shard.shape[1]
    return pl.pallas_call(
        body, out_shape=[jax.ShapeDtypeStruct(c_all.shape, f32), jax.ShapeDtypeStruct((c_all.shape[0], n), f32)],
        name="ada_fwd",
    )(c_all, w_shard, b_cols)


def _adamw(name, w, g, m, v):
    rows, width = w.shape
    tt = _pick(rows, 128, 8)
    c1 = 1.0 - ADAM_B1 ** ADAM_STEP
    c2 = 1.0 - ADAM_B2 ** ADAM_STEP

    def body(w_ref, g_ref, m_ref, v_ref, d_ref, mo_ref, vo_ref):
        gg = g_ref[...]
        mn = ADAM_B1 * m_ref[...] + (1.0 - ADAM_B1) * gg
        vn = ADAM_B2 * v_ref[...] + (1.0 - ADAM_B2) * (gg * gg)
        m_hat = mn / c1
        v_hat = vn / c2
        d_ref[...] = -ADAM_LR * (m_hat / (jnp.sqrt(v_hat) + ADAM_EPS) + ADAM_WD * w_ref[...])
        mo_ref[...] = mn
        vo_ref[...] = vn

    spec = pl.BlockSpec((tt, width), lambda i: (i, 0))
    return pl.pallas_call(
        body, grid=(rows // tt,), in_specs=[spec] * 4, out_specs=[spec] * 3,
        out_shape=[jax.ShapeDtypeStruct((rows, width), f32)] * 3,
        compiler_params=pltpu.CompilerParams(dimension_semantics=("parallel",)),
        name=name,
    )(w, g, m, v)


def _place():
    return lax.axis_index("x"), lax.axis_index("y"), lax.axis_index("c")


def _ag8(name, blk):
    m, w = blk.shape

    def body(x_ref, out_ref, send_sems, recv_sems, local_sem):
        x, y, c = _place()
        me, sibling = (x, y, c), (x, y, 1 - c)
        chips = _other_chips(x, y)

        def slot(px, py, pc):
            return out_ref.at[4 * px + 2 * py + pc]

        def copy(k, block, to, src=None):
            return pltpu.make_async_remote_copy(src_ref=slot(*block) if src is None else src, dst_ref=slot(*block),
                                                send_sem=send_sems.at[k], recv_sem=recv_sems.at[k], device_id=to,
                                                device_id_type=MESH_ID)

        mine = pltpu.make_async_copy(x_ref, slot(*me), local_sem)
        mine.start()
        first = [copy(0, me, sibling, src=x_ref)] + [copy(1 + j, me, (*chip, c), src=x_ref) for j, chip in enumerate(chips)]
        for cp in first:
            cp.start()
        passed = [copy(4 + j, (*chip, c), sibling) for j, chip in enumerate(chips)]
        for j, chip in enumerate(chips):
            copy(1 + j, (*chip, c), me).wait_recv()
            passed[j].start()
        copy(0, sibling, me).wait_recv()
        for j, chip in enumerate(chips):
            copy(4 + j, (*chip, 1 - c), me).wait_recv()
        for cp in first + passed:
            cp.wait_send()
        mine.wait()

    return pl.pallas_call(
        body, out_shape=jax.ShapeDtypeStruct((8, m, w), blk.dtype),
        in_specs=[pl.BlockSpec(memory_space=pltpu.VMEM)], out_specs=pl.BlockSpec(memory_space=pltpu.VMEM),
        scratch_shapes=[pltpu.SemaphoreType.DMA((7,)), pltpu.SemaphoreType.DMA((7,)), pltpu.SemaphoreType.DMA],
        name=name,
    )(blk)


def _other_chips(x, y):
    return [(1 - x, y), (x, 1 - y), (1 - x, 1 - y)]


_ANY = pl.BlockSpec(memory_space=pl.ANY)


def _rcopy(src, dst, send_sems, recv_sems, k, dev):
    return pltpu.make_async_remote_copy(src_ref=src, dst_ref=dst, send_sem=send_sems.at[k], recv_sem=recv_sems.at[k],
                                        device_id=dev, device_id_type=MESH_ID)


def _ag4_multi(ws):
    n = len(ws)

    def body(*refs):
        w_refs, out_refs = refs[:n], refs[n:2 * n]
        send_sems, recv_sems = refs[2 * n:]
        x, y, c = _place()
        chip = 2 * x + y
        sibling = (x, y, 1 - c)
        chips = _other_chips(x, y)
        halves = [(pl.ds(c * (w.shape[0] // 2), w.shape[0] // 2), pl.ds((1 - c) * (w.shape[0] // 2), w.shape[0] // 2))
                  for w in ws]
        copies = []
        for t in range(n):
            for k, (px, py) in enumerate(chips):
                cp = _rcopy(w_refs[t].at[halves[t][0]], out_refs[t].at[chip, halves[t][0]], send_sems, recv_sems,
                            7 * t + k, (px, py, c))
                cp.start()
                copies.append(cp)
        for t in range(n):
            own = _rcopy(w_refs[t], out_refs[t].at[chip], send_sems, recv_sems, 7 * t + 6, sibling)
            own.start()
            copies.append(own)
        for t in range(n):
            for k, (px, py) in enumerate(chips):
                blk = out_refs[t].at[2 * px + py, halves[t][0]]
                _rcopy(blk, blk, send_sems, recv_sems, 7 * t + k, (px, py, c)).wait_recv()
                fw = _rcopy(blk, blk, send_sems, recv_sems, 7 * t + 3 + k, sibling)
                fw.start()
                copies.append(fw)
        for t in range(n):
            for k, (px, py) in enumerate(chips):
                blk = out_refs[t].at[2 * px + py, halves[t][1]]
                _rcopy(blk, blk, send_sems, recv_sems, 7 * t + 3 + k, sibling).wait_recv()
            _rcopy(w_refs[t], out_refs[t].at[chip], send_sems, recv_sems, 7 * t + 6, sibling).wait_recv()
        for cp in copies:
            cp.wait_send()

    return pl.pallas_call(
        body, out_shape=[jax.ShapeDtypeStruct((4,) + w.shape, w.dtype) for w in ws],
        in_specs=[_ANY] * n, out_specs=[_ANY] * n,
        scratch_shapes=[pltpu.SemaphoreType.DMA((7 * n,)), pltpu.SemaphoreType.DMA((7 * n,))],
        name="weights_all_gather",
    )(*ws)


def _rs_sibling_swap(gs):
    n = len(gs)

    def body(*refs):
        g_refs, recv_refs = refs[:n], refs[n:2 * n]
        send_sems, recv_sems = refs[2 * n:]
        x, y, c = _place()
        copies = []
        for t in range(n):
            rh = gs[t].shape[1] // 2
            for s_ in range(4):
                cp = _rcopy(g_refs[t].at[s_, pl.ds((1 - c) * rh, rh)], recv_refs[t].at[s_], send_sems, recv_sems,
                            4 * t + s_, (x, y, 1 - c))
                cp.start()
                copies.append(cp)
        for cp in copies:
            cp.wait_recv()
        for cp in copies:
            cp.wait_send()

    return pl.pallas_call(
        body, out_shape=[jax.ShapeDtypeStruct((4, g.shape[1] // 2, g.shape[2]), g.dtype) for g in gs],
        in_specs=[_ANY] * n, out_specs=[_ANY] * n,
        scratch_shapes=[pltpu.SemaphoreType.DMA((4 * n,)), pltpu.SemaphoreType.DMA((4 * n,))],
        name="grads_sibling_swap",
    )(*gs)


def _rs_chip_exchange(ps):
    n = len(ps)

    def body(*refs):
        p_refs, recv_refs = refs[:n], refs[n:2 * n]
        send_sems, recv_sems = refs[2 * n:]
        x, y, c = _place()
        chips = _other_chips(x, y)
        copies = []
        for t in range(n):
            for k, (px, py) in enumerate(chips):
                cp = _rcopy(p_refs[t].at[2 * px + py], recv_refs[t].at[k], send_sems, recv_sems, 3 * t + k,
                            (px, py, c))
                cp.start()
                copies.append(cp)
        for cp in copies:
            cp.wait_recv()
        for cp in copies:
            cp.wait_send()

    return pl.pallas_call(
        body, out_shape=[jax.ShapeDtypeStruct((3,) + p.shape[1:], p.dtype) for p in ps],
        in_specs=[_ANY] * n, out_specs=[_ANY] * n,
        scratch_shapes=[pltpu.SemaphoreType.DMA((3 * n,)), pltpu.SemaphoreType.DMA((3 * n,))],
        name="grads_chip_exchange",
    )(*ps)


_JOIN_PIECES = 4


def _rs_sibling_join(qs):
    n = len(qs)
    npc = _JOIN_PIECES

    def body(*refs):
        q_refs, out_refs = refs[:n], refs[n:2 * n]
        send_sems, recv_sems = refs[2 * n:]
        x, y, c = _place()
        copies = []
        for t in range(n):
            rh = qs[t].shape[0] // 2
            pr = rh // npc
            for i in range(npc):
                rows = pl.ds(c * rh + i * pr, pr)
                cp = _rcopy(q_refs[t].at[rows], out_refs[t].at[rows], send_sems, recv_sems, npc * t + i,
                            (x, y, 1 - c))
                cp.start()
                copies.append(cp)
        for t in range(n):
            rh = qs[t].shape[0] // 2
            pr = rh // npc
            for i in range(npc):
                rows = pl.ds((1 - c) * rh + i * pr, pr)
                _rcopy(q_refs[t].at[rows], out_refs[t].at[rows], send_sems, recv_sems, npc * t + i,
                       (x, y, 1 - c)).wait_recv()
        for cp in copies:
            cp.wait_send()

    return pl.pallas_call(
        body, out_shape=[jax.ShapeDtypeStruct(q.shape, q.dtype) for q in qs],
        in_specs=[_ANY] * n, out_specs=[_ANY] * n, input_output_aliases={t: t for t in range(n)},
        scratch_shapes=[pltpu.SemaphoreType.DMA((npc * n,)), pltpu.SemaphoreType.DMA((npc * n,))],
        name="grads_sibling_join",
    )(*qs)


def _add_half(name, g, recv, ci):
    S, r, w = g.shape
    rh = r // 2
    tt = _pick(rh, 256, 16)
    nb = rh // tt

    def body(c_ref, a_ref, b_ref, o_ref, ob_ref):
        v = a_ref[...] + b_ref[...]
        o_ref[...] = v
        ob_ref[...] = v.astype(bf16)

    blk = pl.BlockSpec((1, tt, w), lambda s_, i, cr: (s_, i, 0))
    grid_spec = pltpu.PrefetchScalarGridSpec(
        num_scalar_prefetch=1, grid=(S, nb),
        in_specs=[pl.BlockSpec((1, tt, w), lambda s_, i, cr: (s_, cr[0] * nb + i, 0)), blk],
        out_specs=[blk, blk])
    return pl.pallas_call(body, grid_spec=grid_spec,
                          out_shape=[jax.ShapeDtypeStruct((S, rh, w), f32), jax.ShapeDtypeStruct((S, rh, w), bf16)],
                          name=name)(ci.reshape(1).astype(jnp.int32), g, recv)


def _sum_chip(name, pair, others, chip, ci):
    _, rh, w = pair.shape
    tt = _pick(rh, 128, 16)
    nb = rh // tt

    def body(chip_ref, core_ref, a_ref, b_ref, o_ref):
        o_ref[...] = ((a_ref[0] + b_ref[0].astype(f32)) + b_ref[1].astype(f32)) + b_ref[2].astype(f32)

    grid_spec = pltpu.PrefetchScalarGridSpec(
        num_scalar_prefetch=2, grid=(nb,),
        in_specs=[pl.BlockSpec((1, tt, w), lambda i, ch, co: (ch[0], i, 0)),
                  pl.BlockSpec((3, tt, w), lambda i, ch, co: (0, i, 0))],
        out_specs=pl.BlockSpec((tt, w), lambda i, ch, co: (co[0] * nb + i, 0)))
    return pl.pallas_call(body, grid_spec=grid_spec, out_shape=jax.ShapeDtypeStruct((2 * rh, w), f32),
                          name=name)(chip.reshape(1).astype(jnp.int32), ci.reshape(1).astype(jnp.int32), pair, others)


def _sum_devices(gathered, head_row):
    _, rows, width = gathered.shape

    def body(g_ref, out_ref, head_ref):
        acc = g_ref[0]
        for d in range(1, 8):
            acc = acc + g_ref[d]
        out_ref[...] = acc
        row = acc[head_row:head_row + 1, :]
        hs = row[:, 0:HEAD_DIM]
        for h in range(1, HEADS):
            hs = hs + row[:, h * HEAD_DIM:(h + 1) * HEAD_DIM]
        head_ref[...] = jnp.zeros_like(head_ref)
        head_ref[0:1, 0:HEAD_DIM] = hs

    return pl.pallas_call(
        body, out_shape=[jax.ShapeDtypeStruct((rows, width), f32), jax.ShapeDtypeStruct((8, LANES), f32)],
        name="small_grads_sum",
    )(gathered)


def _pack(arrs, rows_mult, dtype):
    flat = jnp.concatenate([a.reshape(-1).astype(dtype) for a in arrs])
    per = PACK_W * rows_mult
    total = -(-flat.shape[0] // per) * per
    return jnp.pad(flat, (0, total - flat.shape[0])).reshape(total // PACK_W, PACK_W)


def _unpack(buf, shapes):
    flat = buf.reshape(-1)
    out, off = [], 0
    for s in shapes:
        n = int(np.prod(s))
        out.append(flat[off:off + n].reshape(s))
        off += n
    return out


_BIG = ["w_in", "w_branch_gdn", "w_branch_rwkv", "w_out", "w_ffn_in", "w_ffn_out"]
_MID = ["conv_gdn", "conv_ffn", "w2", "a2", "g2"]
_SMALL =["b_ada", "norm1_w", "a_log", "dt_bias", "onorm_gdn", "mu_rwkv", "w0", "a0", "k_k", "k_a", "r_k", "lnx_w",
          "lnx_b", "norm2_w", "norm_f_w"]
_ORDER = ["w_ada", "b_ada", "norm1_w", "w_in", "conv_gdn", "a_log", "dt_bias", "onorm_gdn", "w_branch_gdn", "mu_rwkv",
          "w0", "w2", "a0", "a2", "g2", "k_k", "k_a", "r_k", "lnx_w", "lnx_b", "w_branch_rwkv", "w_out", "norm2_w",
          "w_ffn_in", "conv_ffn", "w_ffn_out", "norm_f_w"]


def _win_pad(w):
    z = lambda n: jnp.zeros((w.shape[0], n), w.dtype)
    return jnp.concatenate([w[:, 0:1536], w[:, 2064:3600], w[:, 1536:2048], w[:, 3600:3728], w[:, 2048:2064], z(112),
                            w[:, 3728:3888], z(96), w[:, 3888:5936]], axis=1)


def _win_unpad(g):
    return jnp.concatenate([g[:, 0:1536], g[:, 3072:3584], g[:, 3712:3728], g[:, 1536:3072], g[:, 3584:3712],
                            g[:, 3840:4000], g[:, 4096:6144]], axis=1)


def kernel(x, c, w_ada, b_ada, norm1_w, w_in, conv_gdn, a_log, dt_bias, onorm_gdn, w_branch_gdn, mu_rwkv, w0, w2, a0, a2, g2, k_k, k_a, r_k, lnx_w, lnx_b, w_branch_rwkv, w_out, norm2_w, w_ffn_in, conv_ffn, w_ffn_out, norm_f_w, loss_target, m_w_ada, m_b_ada, m_norm1_w, m_w_in, m_conv_gdn, m_a_log, m_dt_bias, m_onorm_gdn, m_w_branch_gdn, m_mu_rwkv, m_w0, m_w2, m_a0, m_a2, m_g2, m_k_k, m_k_a, m_r_k, m_lnx_w, m_lnx_b, m_w_branch_rwkv, m_w_out, m_norm2_w, m_w_ffn_in, m_conv_ffn, m_w_ffn_out, m_norm_f_w, v_w_ada, v_b_ada, v_norm1_w, v_w_in, v_conv_gdn, v_a_log, v_dt_bias, v_onorm_gdn, v_w_branch_gdn, v_mu_rwkv, v_w0, v_w2, v_a0, v_a2, v_g2, v_k_k, v_k_a, v_r_k, v_lnx_w, v_lnx_b, v_w_branch_rwkv, v_w_out, v_norm2_w, v_w_ffn_in, v_conv_ffn, v_w_ffn_out, v_norm_f_w):
    args = dict(locals())
    W = {n: args[n] for n in _ORDER}
    Mo = {n: args["m_" + n] for n in _ORDER}
    Vo = {n: args["v_" + n] for n in _ORDER}
    shapes = {n: W[n].shape for n in _ORDER}
    sq = lambda a: a.reshape(a.shape[-2:]) if a.ndim == 3 else a.reshape(1, -1)
    row = lambda a: a.reshape(1, -1)

    xi, yi, ci = lax.axis_index("x"), lax.axis_index("y"), lax.axis_index("c")
    dev = 4 * xi + 2 * yi + ci
    chip = 2 * xi + yi

    x2 = x[0]
    tgt = loss_target[0]
    T, D = x2.shape
    N = T // CHUNK
    tt_l = _pick(T, 256, CHUNK)
    tt_h = _pick(T, 128, CHUNK)

    mid_shapes = [shapes[n][1:] for n in _MID]
    small_blk = _pack([c] + [W[n] for n in _MID], 8, f32)
    small_all = _ag8("gather_c_mid", small_blk)
    c_all = small_all[:, 0, :]
    per_chip = small_all[0::2].reshape(4, -1)[:, D:]
    mid = [dict(zip(_MID, _unpack(per_chip[j], mid_shapes))) for j in range(4)]
    catm = lambda n: jnp.concatenate([mid[j][n] for j in range(4)], axis=1)
    conv_gdn_f, conv_ffn_f = catm("conv_gdn"), catm("conv_ffn")
    w2f, a2f, g2f = catm("w2"), catm("a2"), catm("g2")

    win_s, wbg_s, wbr_s, wout_s, wfi_s, wfo_s = _ag4_multi([sq(W[n]).astype(bf16) for n in _BIG])
    win_p = _win_pad(jnp.concatenate([win_s[j] for j in range(4)], axis=1))
    wout_f = wout_s.reshape(D, D)
    wfo = wfo_s.reshape(D_FF, D)
    zpad = lambda a, top, bot: jnp.pad(a, ((top, bot), (0, 0)))
    w2p, a2p, g2p = zpad(w2f, 0, 64), zpad(a2f, 64, 0), zpad(g2f, 0, 96)

    ncol = shapes["w_ada"][2]
    b_cols = lax.dynamic_slice(sq(W["b_ada"]), (0, chip * ncol), (1, ncol))
    cond16, mod_cols = _ada_fwd(jnp.pad(c_all, ((0, 8), (0, 0))), sq(W["w_ada"]), b_cols)
    mod_all = _ag8("gather_mod", mod_cols[:8])
    mod_mine = lax.dynamic_slice(mod_all[0::2], (0, dev, 0), (4, 1, ncol)).reshape(1, 4 * ncol)
    shift1, scale1, gate1, shift2, scale2, gate2 = [mod_mine[:, i * D:(i + 1) * D] for i in range(6)]

    seg = _seg_matrix(WIDTH, HEAD_DIM)
    norm1 = [sq(W["norm1_w"]), shift1, scale1]
    h1 = _stage_fwd("norm_mod1", _fn_norm_mod, [_whole(x2)], norm1, [(D, bf16)], tt_l)[0]
    p = _matmul("in_proj", h1, win_p, "nn")

    cgq = [row(conv_gdn_f[j, part * WIDTH:(part + 1) * WIDTH]) for part in range(3) for j in range(4)]
    lane_pad = lambda a: jnp.pad(row(a), ((0, 0), (8, LANES - 16)))
    gdn_pre_ps = cgq + [lane_pad(W["a_log"]), lane_pad(W["dt_bias"]), seg, _chunk_tri(tt_h, CHUNK)]
    gdn_pre_ts = [(p, WIDTH, part, (0, 1, 2, 3)) for part in range(3)] + [(p, LANES, 29, None)]
    q_, k_, v_, beta_t, gc_t = _stage_fwd("gdn_pre", _fn_gdn_pre, gdn_pre_ts, gdn_pre_ps,
                                          [(WIDTH, f32)] * 3 + [(LANES, f32)] * 2, tt_h)
    o_, gdn_hist, gdn_inv = _scan_fwd("gdn_scan", _gdn_chunk, [q_, k_, v_], [beta_t, gc_t])
    ow512 = jnp.tile(row(W["onorm_gdn"]), (1, HEADS))
    gdn_post_ts = [_whole(o_), (p, WIDTH, 6, None)]
    ya = _stage_fwd("gdn_post", _fn_gdn_post, gdn_post_ts, [ow512, seg], [(WIDTH, bf16)], tt_l)[0]

    mu = sq(W["mu_rwkv"])
    rw_ps = [mu[:, 0:512], mu[:, 512:1024], mu[:, 1024:1536], mu[:, 1536:1664], jnp.pad(mu[:, 1664:1824], ((0, 0), (0, 96))),
             sq(W["w0"]), w2p, sq(W["a0"]), a2p, g2p, sq(W["k_k"]), sq(W["k_a"]), seg]
    rw_ts = [(p, WIDTH, 3, (0, 1)), (p, WIDTH, 4, (0, 1)), (p, WIDTH, 5, (0, 1)), (p, LANES, 28, (0, 1)),
             (p, 256, 15, (0, 1))]
    rw_out = _stage_fwd("rwkv_pre", _fn_rwkv_pre, rw_ts, rw_ps, [(WIDTH, f32)] * 7, tt_h)
    r_, lw_, k2_, vv_, na_, b_, g_ = rw_out
    rw_ins = [r_, lw_, k2_, vv_, na_, b_]
    y_, rw_hist, rw_inv = _scan_fwd("rwkv_scan", _rwkv_chunk, rw_ins)
    rwp_ps = [sq(W["lnx_w"]), sq(W["lnx_b"]), row(W["r_k"]), seg]
    rwp_ts = [_whole(y_), _whole(r_), _whole(k2_), _whole(vv_), _whole(g_)]
    yb = _stage_fwd("rwkv_post", _fn_rwkv_post, rwp_ts, rwp_ps, [(WIDTH, bf16)], tt_l)[0]

    big_a = _matmul("branch_gdn", ya, wbg_s, "nn", shards=4)
    big_b = _matmul("branch_rwkv", yb, wbr_s, "nn", shards=4)
    merge_ts = [(p, D, 4, None), (p, D, 5, None), _whole(big_a), _whole(big_b)]
    merged = _stage_fwd("merge", _fn_merge, merge_ts, [], [(D, bf16)], tt_l)[0]
    mo = _matmul("out_proj", merged, wout_f, "nn")
    norm2 = [gate1, sq(W["norm2_w"]), shift2, scale2]
    x1, h2 = _stage_fwd("resid_norm_mod2", _fn_resid_norm_mod, [_whole(x2), _whole(mo)], norm2, [(D, f32), (D, bf16)], tt_l)
    f = _matmul("ffn_in", h2, wfi_s, "nn", shards=4, tm=512, tn=1408)
    cg_ps = [row(conv_ffn_f[j]) for j in range(3)]
    cg_ts = [(f, D_FF, 0, (0, 1, 2)), (f, D_FF, 1, None)]
    act = _stage_fwd("convglu", _fn_convglu, cg_ts, cg_ps, [(D_FF, bf16)], tt_h)[0]
    fo = _matmul("ffn_out", act, wfo, "nn", tk=1408)

    dx1_a, dfo, dgate2, dnormf, loss_part = _final_stage(x1, fo, tgt, gate2, row(W["norm_f_w"]), tt_l)

    dact = _matmul("d_act", dfo, wfo, "nt", tm=512, tn=1408)
    g_wfo = _matmul("g_ffn_out", act, dfo, "tn", tm=1408)
    _, dcf, df = _stage_bwd("convglu_bwd", _fn_convglu, cg_ts, cg_ps, [[_whole(dact)]], tt_h, [True] * 2, [True] * 3,
                            joint=([0, 1], 2 * D_FF, 0, None))
    dh2 = _matmul("d_h2", df, wfi_s, "nt", shards=4, tm=512, tk=1408)
    g_wfi = _matmul("g_ffn_in", h2, df, "tn", shards=4, tn=1408)
    (dx_a, dmo), (dgate1, dnorm2, dshift2, dscale2), _ = _stage_bwd(
        "resid_norm_mod2_bwd", _fn_resid_norm_mod, [_whole(x2), _whole(mo)], norm2,
        [[_whole(dx1_a)], [_whole(dh2)]], tt_l, [True, True], [True] * 4)
    dmerged = _matmul("d_merged", dmo, wout_f, "nt")
    g_wout = _matmul("g_out_proj", merged, dmo, "tn")
    (dbig_a, dbig_b), _, dp = _stage_bwd("merge_bwd", _fn_merge, merge_ts, [], [[_whole(dmerged)]], tt_l, [True] * 4, [],
                                         joint=([0, 1], p.shape[1], 2, None))
    dya = _matmul("d_ya", dbig_a, wbg_s, "nt", shards=4)
    g_wbg = _matmul("g_branch_gdn", ya, dbig_a, "tn", shards=4)
    dyb = _matmul("d_yb", dbig_b, wbr_s, "nt", shards=4)
    g_wbr = _matmul("g_branch_rwkv", yb, dbig_b, "tn", shards=4)

    (dy_, dr_p, dk2_p, dv_p, dg_p), (dlnxw, dlnxb, drk), _ = _stage_bwd(
        "rwkv_post_bwd", _fn_rwkv_post, rwp_ts, rwp_ps, [[_whole(dyb)]], tt_l, [True] * 5, [True, True, True, False])
    dr_c, dlw_c, dk2_c, dv_c, dna_c, db_c = _scan_bwd("rwkv_scan_bwd", _rwkv_chunk, rw_ins, [], rw_hist, rw_inv, dy_)
    rw_cots = [[_whole(dr_p), _whole(dr_c)], [_whole(dlw_c)], [_whole(dk2_p), _whole(dk2_c)],
               [_whole(dv_p), _whole(dv_c)], [_whole(dna_c)], [_whole(db_c)], [_whole(dg_p)]]
    (dl_, dg_), rw_dp, dp = _stage_bwd("rwkv_pre_bwd", _fn_rwkv_pre, rw_ts, rw_ps, rw_cots, tt_h, [True] * 5,
                                       [True] * 12 + [False], joint=([0, 1, 2], p.shape[1], 1, dp))
    dmu_r, dmu_k, dmu_v, dmu_l, dmu_g, dw0, dw2p, da0, da2p, dg2p, dkk, dka = rw_dp

    (do_,), (dow512,), dp = _stage_bwd("gdn_post_bwd", _fn_gdn_post, gdn_post_ts, [ow512, seg], [[_whole(dya)]], tt_l,
                                       [True, True], [True, False], joint=([1], p.shape[1], 6, dp))
    d_gdn = _scan_bwd("gdn_scan_bwd", _gdn_chunk, [q_, k_, v_], [beta_t, gc_t], gdn_hist, gdn_inv, do_)
    gdn_cots = [[_whole(a)] for a in d_gdn]
    (dba,), gdn_dp, dp = _stage_bwd("gdn_pre_bwd", _fn_gdn_pre, gdn_pre_ts, gdn_pre_ps, gdn_cots, tt_h, [True] * 4,
                                    [True] * 14 + [False, False], joint=([0, 1, 2], p.shape[1], 0, dp))
    dp = lax.dynamic_update_slice(dp, jnp.concatenate([dl_, dba, dg_], axis=1), (0, 3584))
    dh1 = _matmul("d_h1", dp, win_p, "nt")
    g_win = _win_unpad(_matmul("g_in_proj", h1, dp, "tn"))
    (dx_b,), (dnorm1, dshift1, dscale1), _ = _stage_bwd("norm_mod1_bwd", _fn_norm_mod, [_whole(x2)], norm1,
                                                        [[_whole(dh1)]], tt_l, [True], [True] * 3)
    grad_x = _add_n("grad_x", [dx_a, dx_b], tt_l)

    dmod = jnp.concatenate([dshift1, dscale1, dgate1, dshift2, dscale2, dgate2], axis=1)
    g_conv_gdn = jnp.concatenate([jnp.concatenate([gdn_dp[4 * part + j] for part in range(3)], axis=1) for j in range(4)], axis=0)
    g_conv_ffn = jnp.concatenate(dcf, axis=0)
    g_mu = jnp.concatenate([dmu_r, dmu_k, dmu_v, dmu_l, dmu_g[:, :160]], axis=1)
    small_parts = {"b_ada": dmod, "norm1_w": dnorm1, "a_log": gdn_dp[12][:, 8:16], "dt_bias": gdn_dp[13][:, 8:16],
                   "mu_rwkv": g_mu, "w0": dw0, "a0": da0, "k_k": dkk, "k_a": dka, "r_k": drk, "lnx_w": dlnxw,
                   "lnx_b": dlnxb, "norm2_w": dnorm2, "norm_f_w": dnormf}
    small_names = [n for n in _SMALL if n != "onorm_gdn"]
    mid_full = [g_conv_gdn, g_conv_ffn, dw2p[0:64], da2p[64:128], dg2p[0:160]]
    body_rows = _pack([small_parts[n] for n in small_names] + [loss_part[:, 0:1]] + mid_full, 1, f32)
    head_row = body_rows.shape[0]
    small_g = jnp.concatenate([body_rows, jnp.pad(dow512, ((0, 0), (0, PACK_W - WIDTH)))], axis=0)
    small_g = jnp.pad(small_g, ((0, -small_g.shape[0] % 8), (0, 0)))
    small_all_g = _ag8("gather_small_grads", small_g)
    small_sum, head_sum = _sum_devices(small_all_g, head_row)
    small_shapes = [shapes[n][1:] if n != "norm_f_w" else shapes[n] for n in small_names]
    un = _unpack(small_sum, small_shapes + [(1,)] + [g.shape for g in mid_full])
    small_grads = dict(zip(small_names, un))
    loss = un[len(small_names)].reshape(())
    small_grads["onorm_gdn"] = head_sum[0, 0:HEAD_DIM]
    for n, g in zip(_MID, un[len(small_names) + 1:]):
        wcols = shapes[n][2]
        small_grads[n] = lax.dynamic_slice(g, (0, chip * wcols), (g.shape[0], wcols))

    dmod_all = small_all_g[:, 0:6, :].reshape(8, 6 * PACK_W)
    dmod_cols = lax.dynamic_slice(dmod_all, (0, chip * ncol), (8, ncol))
    g_wada = _matmul("g_w_ada", cond16, jnp.pad(dmod_cols, ((0, 8), (0, 0))), "tn")

    nwin = shapes["w_in"][2]
    g_win_s = g_win.reshape(D, 4, nwin).transpose(1, 0, 2)
    gs = [g_win_s, g_wbg, g_wbr, g_wout.reshape(4, D // 4, D), g_wfi, g_wfo.reshape(4, D_FF // 4, D)]
    recvs = _rs_sibling_swap(gs)
    pairs = [_add_half("grads_pair_sum%d" % t, g, r_, ci) for t, (g, r_) in enumerate(zip(gs, recvs))]
    others = _rs_chip_exchange([pb for _, pb in pairs])
    halves = [_sum_chip("grads_chip_sum%d" % t, pf, o_, chip, ci) for t, ((pf, _), o_) in enumerate(zip(pairs, others))]
    big_grads = dict(zip(_BIG, _rs_sibling_join(halves)))

    res = {tag: {} for tag in ("grad", "delta", "new_m", "new_v")}

    def put(n, g, d, m_, v_):
        for tag, val in zip(("grad", "delta", "new_m", "new_v"), (g, d, m_, v_)):
            res[tag][n] = val.reshape(shapes[n])

    for n in _BIG:
        put(n, big_grads[n], *_adamw("adamw_" + n, sq(W[n]), big_grads[n], sq(Mo[n]), sq(Vo[n])))
    put("w_ada", g_wada, *_adamw("adamw_w_ada", sq(W["w_ada"]), g_wada, sq(Mo["w_ada"]), sq(Vo["w_ada"])))
    rest = _SMALL + _MID
    pk = lambda d: _pack([d[n] for n in rest], 8, f32)
    sg = pk(small_grads)
    sm = _adamw("adamw_small", pk(W), sg, pk(Mo), pk(Vo))
    for tag, buf in zip(("grad", "delta", "new_m", "new_v"), (sg,) + tuple(sm)):
        res[tag].update(zip(rest, _unpack(buf, [shapes[n] for n in rest])))
    outs = [loss, grad_x.reshape(x.shape)]
    for tag in ("grad", "delta", "new_m", "new_v"):
        outs += [res[tag][n] for n in _ORDER]
    return tuple(outs)
```

```python
import functools
import math

import numpy as np
import jax
import jax.numpy as jnp
from jax import lax
from jax.experimental import pallas as pl
from jax.experimental.pallas import tpu as pltpu

f32 = jnp.float32
bf16 = jnp.bfloat16

LANES = 128
HEADS = 8
HEAD_DIM = 64
WIDTH = HEADS * HEAD_DIM
CHUNK = 64
D_FF = 2816
NORM_EPS = 1e-6
LNX_EPS = 64e-5
PACK_W = 1024
MESH_ID = pl.DeviceIdType.MESH

ADAM_LR, ADAM_B1, ADAM_B2, ADAM_EPS, ADAM_WD, ADAM_STEP = 0.001, 0.9, 0.999, 1e-08, 0.01, 10


def _pick(n, target, mult):
    if n <= target:
        return n
    best = None
    for t in range(mult, target + 1, mult):
        if n % t == 0:
            best = t
    assert best is not None, (n, target, mult)
    return best


def _split_bf16(x, n):
    parts, r = [], x
    for i in range(n):
        p = r.astype(bf16)
        parts.append(p)
        if i + 1 < n:
            r = r - p.astype(f32)
    return parts


def _xdot_r_impl(x, m, n, dims):
    acc = None
    for p in _split_bf16(x, n):
        t = lax.dot_general(p, m, dims, preferred_element_type=f32)
        acc = t if acc is None else acc + t
    return acc


def _make_xdot_r(n):
    nn = (((1,), (0,)), ((), ()))
    nt = (((1,), (1,)), ((), ()))

    @jax.custom_vjp
    def xdot(x, m):
        return _xdot_r_impl(x, m, n, nn)

    def fwd(x, m):
        return _xdot_r_impl(x, m, n, nn), m

    def bwd(m, ct):
        return _xdot_r_impl(ct, m, n, nt), jnp.zeros_like(m)

    xdot.defvjp(fwd, bwd)
    return xdot


_segsum = _make_xdot_r(2)


def _xdot_l_impl(m, x, n, dims):
    acc = None
    for p in _split_bf16(x, n):
        t = lax.dot_general(m, p, dims, preferred_element_type=f32)
        acc = t if acc is None else acc + t
    return acc


@jax.custom_vjp
def _xdot_l(m, x):
    return _xdot_l_impl(m, x, 3, (((1,), (0,)), ((), ())))


def _xdot_l_fwd(m, x):
    return _xdot_l(m, x), m


def _xdot_l_bwd(m, ct):
    return jnp.zeros_like(m), _xdot_l_impl(m, ct, 3, (((0,), (0,)), ((), ())))


_xdot_l.defvjp(_xdot_l_fwd, _xdot_l_bwd)


@jax.custom_vjp
def _bdot(x, w):
    return jnp.dot(x.astype(bf16), w.astype(bf16), preferred_element_type=f32)


def _bdot_fwd(x, w):
    return _bdot(x, w), (x, w)


def _bdot_bwd(res, ct):
    x, w = res
    c = ct.astype(bf16)
    dx = lax.dot_general(c, w.astype(bf16), (((1,), (1,)), ((), ())), preferred_element_type=f32)
    dw = lax.dot_general(x.astype(bf16), c, (((0,), (0,)), ((), ())), preferred_element_type=f32)
    return dx, dw


_bdot.defvjp(_bdot_fwd, _bdot_bwd)


def _silu(x):
    return x * jax.nn.sigmoid(x)


def _softplus(x):
    return jnp.maximum(x, 0.0) + jnp.log(1.0 + jnp.exp(-jnp.abs(x)))


def _rms(x, w, eps):
    return x * lax.rsqrt(jnp.mean(x * x, axis=-1, keepdims=True) + eps) * w


def _seg_matrix(width, seg):
    i = np.arange(width)
    return jnp.asarray((i[:, None] // seg) == (i[None, :] // seg), dtype=bf16)


def _chunk_tri(rows, chunk):
    i = np.arange(rows)
    return jnp.asarray(((i[:, None] // chunk) == (i[None, :] // chunk)) & (i[:, None] >= i[None, :]), dtype=bf16)


HALO = 8


def _full_spec(shape):
    nd = len(shape)
    return pl.BlockSpec(shape, lambda i: (0,) * nd)


def _entry_specs(entries, tt, block_of):
    specs, ops = [], []
    for arr, w, ci, shifts in entries:
        specs.append(pl.BlockSpec((tt, w), lambda i, ci=ci: (block_of(i), ci)))
        ops.append(arr)
        if shifts:
            specs.append(pl.BlockSpec((HALO, w), lambda i, ci=ci: (jnp.maximum(block_of(i) * (tt // HALO) - 1, 0), ci)))
            ops.append(arr)
    return specs, ops


def _load_entries(entries, refs, first):
    tiles, k = [], 0
    for _, w, _, shifts in entries:
        x = refs[k][...].astype(f32)
        k += 1
        if not shifts:
            tiles.append(x)
            continue
        halo = jnp.where(first, 0.0, refs[k][...].astype(f32))
        k += 1
        row = lax.broadcasted_iota(jnp.int32, (HALO, w), 0)
        for s in shifts:
            if s == 0:
                tiles.append(x)
                continue
            r = pltpu.roll(x, s, 0)
            head = jnp.where(row < s, pltpu.roll(halo, s, 0), r[0:HALO])
            tiles.append(jnp.concatenate([head, r[HALO:]], axis=0))
    return tiles


def _unshift_sum(grads, shifts, carry, tt):
    w = grads[0].shape[1]
    row = lax.broadcasted_iota(jnp.int32, (tt, w), 0)
    row8 = lax.broadcasted_iota(jnp.int32, (HALO, w), 0)
    dx, out = None, jnp.zeros((HALO, w), f32)
    for d, s in zip(grads, shifts):
        if s == 0:
            part = d
        else:
            part = jnp.where(row < tt - s, pltpu.roll(d, tt - s, 0), 0.0)
            out = out + jnp.where(row8 >= HALO - s, pltpu.roll(d[0:HALO], HALO - s, 0), 0.0)
        dx = part if dx is None else dx + part
    return jnp.concatenate([dx[:tt - HALO], dx[tt - HALO:] + carry], axis=0), out


def _stage_fwd(name, fn, tiles, params, outs, tt):
    rows = tiles[0][0].shape[0]
    npar = len(params)
    specs, ops = _entry_specs(tiles, tt, lambda i: i)
    nin = len(ops)

    def body(*refs):
        ts = _load_entries(tiles, refs[:nin], pl.program_id(0) == 0)
        ps = [r[...] for r in refs[nin:nin + npar]]
        res = fn(ps, ts)
        for r, v in zip(refs[nin + npar:], res):
            r[...] = v.astype(r.dtype)

    return pl.pallas_call(
        body, grid=(rows // tt,),
        in_specs=specs + [_full_spec(p.shape) for p in params],
        out_specs=[pl.BlockSpec((tt, w), lambda i: (i, 0)) for (w, _) in outs],
        out_shape=[jax.ShapeDtypeStruct((rows, w), dt) for (w, dt) in outs],
        compiler_params=pltpu.CompilerParams(dimension_semantics=("parallel",)),
        name=name,
    )(*ops, *params)


def _stage_bwd(name, fn, tiles, params, cots, tt, tile_grad, param_grad, joint=None):
    rows = tiles[0][0].shape[0]
    nblk = rows // tt
    npar = len(params)
    block_of = lambda i: nblk - 1 - i
    specs, ops = _entry_specs(tiles, tt, block_of)
    nin = len(ops)
    flat_cots = [c for group in cots for c in group]
    groups = [len(g) for g in cots]
    ncot = len(flat_cots)
    counts = [len(e[3]) if e[3] else 1 for e in tiles]
    dt_entries = [e for e, g in zip(tiles, tile_grad) if g]
    dp_shapes = [p.shape for p, g in zip(params, param_grad) if g]
    ndt = len(dt_entries)
    carry_w = [e[1] for e in dt_entries if e[3]]
    flags = [g for g, n in zip(tile_grad, counts) for _ in range(n)]
    members, j_width, j_cidx, j_buf = joint if joint else ([], 0, 0, None)
    solo = [k for k in range(ndt) if k not in members]
    nsolo, njoint, nbuf = len(solo), int(bool(members)), int(j_buf is not None)
    j_block = sum(dt_entries[k][1] for k in members)

    def body(*refs):
        i = pl.program_id(0)
        p_refs = refs[nin:nin + npar]
        c_refs = refs[nin + npar:nin + npar + ncot]
        base = nin + npar + ncot + nbuf
        dt_refs = refs[base:base + nsolo]
        joint_refs = refs[base + nsolo:base + nsolo + njoint]
        dp_refs = refs[base + nsolo + njoint:base + nsolo + njoint + len(dp_shapes)]
        carry_refs = refs[base + nsolo + njoint + len(dp_shapes):]
        ts = _load_entries(tiles, refs[:nin], block_of(i) == 0)
        ps = [r[...] for r in p_refs]

        def f(dp, dt):
            dp, dt = iter(dp), iter(dt)
            pp = [next(dp) if g else p for p, g in zip(ps, param_grad)]
            tl = [next(dt) if g else t for t, g in zip(ts, flags)]
            return fn(pp, tl)

        _, vjp = jax.vjp(f, [p for p, g in zip(ps, param_grad) if g], [t for t, g in zip(ts, flags) if g])
        cs, j = [], 0
        for n in groups:
            acc = c_refs[j][...].astype(f32)
            for q in range(1, n):
                acc = acc + c_refs[j + q][...].astype(f32)
            cs.append(acc)
            j += n
        gp, gt = vjp(cs)

        @pl.when(i == 0)
        def _():
            for r in dp_refs:
                r[...] = jnp.zeros_like(r)
            for r in carry_refs:
                r[...] = jnp.zeros_like(r)

        gt, k, kc, dxs = list(gt), 0, 0, []
        for e, n in zip(dt_entries, [n for n, g in zip(counts, tile_grad) if g]):
            if e[3]:
                dx, out = _unshift_sum(gt[k:k + n], e[3], carry_refs[kc][...], tt)
                carry_refs[kc][...] = out
                kc += 1
            else:
                dx = gt[k]
            dxs.append(dx)
            k += n
        for r, k in zip(dt_refs, solo):
            r[...] = dxs[k]
        off = 0
        for k in members:
            w = dt_entries[k][1]
            joint_refs[0][:, off:off + w] = dxs[k]
            off += w
        for r, v in zip(dp_refs, gp):
            r[...] += v

    res = pl.pallas_call(
        body, grid=(nblk,),
        in_specs=specs + [_full_spec(p.shape) for p in params]
        + [pl.BlockSpec((tt, w), lambda i, ci=ci: (block_of(i), ci)) for (_, w, ci, *_) in flat_cots]
        + [pl.BlockSpec(memory_space=pl.ANY)] * nbuf,
        out_specs=[pl.BlockSpec((tt, dt_entries[k][1]), lambda i: (block_of(i), 0)) for k in solo]
        + [pl.BlockSpec((tt, j_block), lambda i: (block_of(i), j_cidx))] * njoint
        + [_full_spec(s) for s in dp_shapes],
        out_shape=[jax.ShapeDtypeStruct((rows, dt_entries[k][1]), f32) for k in solo]
        + [jax.ShapeDtypeStruct((rows, j_width), f32)] * njoint
        + [jax.ShapeDtypeStruct(s, f32) for s in dp_shapes],
        scratch_shapes=[pltpu.VMEM((HALO, w), f32) for w in carry_w],
        input_output_aliases={nin + npar + ncot: nsolo} if nbuf else {},
        compiler_params=pltpu.CompilerParams(dimension_semantics=("arbitrary",)),
        name=name,
    )(*ops, *params, *[c[0] for c in flat_cots], *([j_buf] if nbuf else []))
    res = list(res)
    return res[:nsolo], res[nsolo + njoint:], (res[nsolo] if njoint else None)


def _whole(a):
    return (a, a.shape[1], 0, None)


def _matmul(name, a, b, mode, out_dtype=f32, tm=1024, tn=1024, tk=1024, shards=1):
    S = shards
    if mode == "nn":
        M, K = a.shape
        w = b.shape[-1]
    elif mode == "nt":
        M = a.shape[0]
        if S > 1:
            _, N, w = b.shape
            K = S * w
        else:
            N, K = b.shape
            w = K
    else:
        K, M = a.shape
        w = b.shape[1] // S
    if mode != "nt":
        N = S * w
    tm = _pick(M, tm, LANES)
    if mode == "nt":
        tn = _pick(N, tn, LANES)
        tk = _pick(w, tk, LANES)
    else:
        tn = _pick(w, tn, LANES)
        tk = _pick(K, tk, LANES if mode == "nn" else 16)
    nk = K // tk
    nb = w // (tk if mode == "nt" else tn)
    if mode == "nn":
        a_spec = pl.BlockSpec((tm, tk), lambda i, j, k: (i, k))
        if S > 1:
            b_spec = pl.BlockSpec((1, tk, tn), lambda i, j, k: (j // nb, k, j % nb))
        else:
            b_spec = pl.BlockSpec((tk, tn), lambda i, j, k: (k, j))
        dims = (((1,), (0,)), ((), ()))
    elif mode == "nt":
        a_spec = pl.BlockSpec((tm, tk), lambda i, j, k: (i, k))
        if S > 1:
            b_spec = pl.BlockSpec((1, tn, tk), lambda i, j, k: (k // nb, j, k % nb))
        else:
            b_spec = pl.BlockSpec((tn, tk), lambda i, j, k: (j, k))
        dims = (((1,), (1,)), ((), ()))
    else:
        a_spec = pl.BlockSpec((tk, tm), lambda i, j, k: (k, i))
        b_spec = pl.BlockSpec((tk, tn), lambda i, j, k: (k, j))
        dims = (((0,), (0,)), ((), ()))
    if mode == "tn" and S > 1:
        o_spec = pl.BlockSpec((1, tm, tn), lambda i, j, k: (j // nb, i, j % nb))
        o_shape = (S, M, w)
    else:
        o_spec = pl.BlockSpec((tm, tn), lambda i, j, k: (i, j))
        o_shape = (M, N)
    b_lead = S > 1 and mode != "tn"
    o_lead = S > 1 and mode == "tn"

    def body(a_ref, b_ref, o_ref, acc_ref):
        k = pl.program_id(2)

        @pl.when(k == 0)
        def _():
            acc_ref[...] = jnp.zeros_like(acc_ref)

        bv = b_ref[0] if b_lead else b_ref[...]
        acc_ref[...] += lax.dot_general(a_ref[...].astype(bf16), bv.astype(bf16), dims, preferred_element_type=f32)

        @pl.when(k == nk - 1)
        def _():
            if o_lead:
                o_ref[0] = acc_ref[...].astype(o_ref.dtype)
            else:
                o_ref[...] = acc_ref[...].astype(o_ref.dtype)

    def body_one_step(a_ref, b_ref, o_ref):
        bv = b_ref[0] if b_lead else b_ref[...]
        res = lax.dot_general(a_ref[...].astype(bf16), bv.astype(bf16), dims, preferred_element_type=f32)
        if o_lead:
            o_ref[0] = res.astype(o_ref.dtype)
        else:
            o_ref[...] = res.astype(o_ref.dtype)

    return pl.pallas_call(
        body if nk > 1 else body_one_step, grid=(M // tm, N // tn, nk),
        in_specs=[a_spec, b_spec],
        out_specs=o_spec,
        out_shape=jax.ShapeDtypeStruct(o_shape, out_dtype),
        scratch_shapes=[pltpu.VMEM((tm, tn), f32)] if nk > 1 else [],
        compiler_params=pltpu.CompilerParams(dimension_semantics=("parallel", "parallel", "arbitrary")),
        name=name,
    )(a, b)


def _make_bmm(precision):
    if precision is None:
        cast, kw = (lambda v: v.astype(bf16)), {}
    else:
        cast, kw = (lambda v: v), {"precision": precision}

    def nn(a, b):
        return jnp.einsum("hij,hjk->hik", cast(a), cast(b), preferred_element_type=f32, **kw)

    def nt(a, b):
        return jnp.einsum("hik,hjk->hij", cast(a), cast(b), preferred_element_type=f32, **kw)

    def tn(a, b):
        return jnp.einsum("hki,hkj->hij", cast(a), cast(b), preferred_element_type=f32, **kw)

    if precision is not None:
        return nn, nt, tn
    nn_v, nt_v, tn_v = jax.custom_vjp(nn), jax.custom_vjp(nt), jax.custom_vjp(tn)
    keep = lambda f: (lambda a, b: (f(a, b), (a, b)))
    nn_v.defvjp(keep(nn), lambda r, ct: (nt(ct, r[1]), tn(r[0], ct)))
    nt_v.defvjp(keep(nt), lambda r, ct: (nn(ct, r[1]), tn(ct, r[0])))
    tn_v.defvjp(keep(tn), lambda r, ct: (nt(r[1], ct), nn(r[0], ct)))
    return nn_v, nt_v, tn_v


_bmm, _bmm_nt, _bmm_tn = _make_bmm(None)
_bmm_exact = _make_bmm(lax.Precision.HIGH)[0]


def _masks(n):
    r = lax.broadcasted_iota(jnp.int32, (n, n), 0)
    c = lax.broadcasted_iota(jnp.int32, (n, n), 1)
    return (r >= c)[None], (r > c)[None], (r == c)[None]


_INV_BLOCK = 8


def _nilpotent_inverse(m, eye):
    p = eye + m
    for _ in range(2):
        m = _bmm(m, m)
        p = p + _bmm(p, m)
    return p


def _neumann_inverse_impl(m):
    n = m.shape[1]
    assert n == _INV_BLOCK * _INV_BLOCK
    r = lax.broadcasted_iota(jnp.int32, (n, n), 0)
    c = lax.broadcasted_iota(jnp.int32, (n, n), 1)
    eye = (r == c).astype(f32)[None]
    inside = jnp.where((r // _INV_BLOCK == c // _INV_BLOCK)[None], m, 0.0)
    d_inv = _nilpotent_inverse(inside, eye)
    return _bmm(_nilpotent_inverse(_bmm(d_inv, m - inside), eye), d_inv)


@jax.custom_vjp
def _neumann_inverse(m):
    return _neumann_inverse_impl(m)


def _neumann_inverse_fwd(m):
    p = _neumann_inverse_impl(m)
    return p, p


def _neumann_inverse_bwd(p, ct):
    return (_bmm_tn(p, _bmm_nt(ct, p)),)


_neumann_inverse.defvjp(_neumann_inverse_fwd, _neumann_inverse_bwd)


@jax.custom_vjp
def _given_inverse(m, p):
    return p


_given_inverse.defvjp(lambda m, p: (p, p), lambda p, ct: (_neumann_inverse_bwd(p, ct)[0], jnp.zeros_like(p)))


def _gdn_chunk(s, q, k, v, beta, gc, gr, gl, p=None):
    n = q.shape[1]
    causal, strict, _ = _masks(n)
    decay = jnp.where(causal, jnp.exp(jnp.where(causal, gc - gr, 0.0)), 0.0)
    kb = k * beta
    vb = v * beta
    lower = jnp.where(strict, _bmm_nt(kb, k) * decay, 0.0)
    t_mat = _neumann_inverse(-lower) if p is None else _given_inverse(-lower, p)
    egc = jnp.exp(gc)
    u = _bmm(t_mat, vb)
    w = _bmm(t_mat, kb * egc)
    attn = jnp.where(causal, _bmm_nt(q, k) * decay, 0.0)
    v_new = u - _bmm(w, s)
    o = _bmm(q * egc, s) + _bmm(attn, v_new)
    k_dec = k * jnp.exp(gl - gc)
    s_new = s * jnp.exp(gl) + _bmm_tn(k_dec, v_new)
    return s_new, o, t_mat


def _rwkv_chunk(s, r, lw, k, v, a, b, p=None):
    n = r.shape[1]
    causal, strict, _ = _masks(n)
    tri = jnp.broadcast_to(causal.astype(f32), (r.shape[0], n, n))
    lc = _bmm_exact(tri, lw)
    ein = jnp.exp(lc)
    eout = jnp.exp(-lc)
    a_t = a * jnp.exp(lc - lw)
    b_t = b * eout
    k_t = k * eout
    r_t = r * ein
    a_ab = jnp.where(strict, _bmm_nt(a_t, b_t), 0.0)
    a_ak = jnp.where(strict, _bmm_nt(a_t, k_t), 0.0)
    inv = _neumann_inverse(a_ab) if p is None else _given_inverse(a_ab, p)
    u = _bmm(inv, _bmm_nt(a_t, s) + _bmm(a_ak, v))
    y = (_bmm_nt(r_t, s) + _bmm(jnp.where(causal, _bmm_nt(r_t, b_t), 0.0), u)
         + _bmm(jnp.where(causal, _bmm_nt(r_t, k_t), 0.0), v))
    e_last = jnp.exp(jnp.sum(lw, axis=1, keepdims=True))
    s_new = s * e_last + _bmm_tn(u, b_t * e_last) + _bmm_tn(v, k_t * e_last)
    return s_new, y, inv


def _heads_in(ref):
    return jnp.stack([ref[:, h * HEAD_DIM:(h + 1) * HEAD_DIM] for h in range(HEADS)], axis=0)


def _heads_out(ref, val):
    for h in range(HEADS):
        ref[:, h * HEAD_DIM:(h + 1) * HEAD_DIM] = val[h]


def _gdn_scalars(bt, gt):
    n = bt.shape[0]
    gtt = gt.T
    hs = range(HEADS)
    return [jnp.stack([bt[:, h:h + 1] for h in hs], axis=0),
            jnp.stack([gt[:, HEADS + h:HEADS + h + 1] for h in hs], axis=0),
            jnp.stack([gtt[HEADS + h:HEADS + h + 1, :] for h in hs], axis=0),
            jnp.stack([gt[n - 1:n, HEADS + h:HEADS + h + 1] for h in hs], axis=0)]


def _gdn_scalars_back(dbeta, dgc, dgr, dgl):
    n = dbeta.shape[1]
    lane = lax.broadcasted_iota(jnp.int32, (n, LANES), 1)
    row = lax.broadcasted_iota(jnp.int32, (n, LANES), 0)
    sub = lax.broadcasted_iota(jnp.int32, (LANES, n), 0)
    db = jnp.zeros((n, LANES), f32)
    dg = jnp.zeros((n, LANES), f32)
    dgt = jnp.zeros((LANES, n), f32)
    for h in range(HEADS):
        db = jnp.where(lane == h, dbeta[h], db)
        dg = jnp.where(lane == HEADS + h, dgc[h] + jnp.where(row == n - 1, dgl[h], 0.0), dg)
        dgt = jnp.where(sub == HEADS + h, dgr[h], dgt)
    return [db, dg + dgt.T]


def _scan_spec(width, n, reverse):
    if reverse:
        return pl.BlockSpec((CHUNK, width), lambda i: (n - 1 - i, 0))
    return pl.BlockSpec((CHUNK, width), lambda i: (i, 0))


def _hist_spec(n, reverse):
    blk = (1, HEADS, HEAD_DIM, HEAD_DIM)
    if reverse:
        return pl.BlockSpec(blk, lambda i: (n - 1 - i, 0, 0, 0))
    return pl.BlockSpec(blk, lambda i: (i, 0, 0, 0))


def _job_parts(job):
    if job is None:
        return [], [], []
    sems = [pltpu.SemaphoreType.DMA((job["nsem"],)), pltpu.SemaphoreType.DMA((job["nsem"],))]
    return list(job["ins"]), list(job["out_shapes"]), sems


def _job_steps(job, in_refs, out_refs, sems, step, last):
    if job is None:
        return

    @pl.when(step == 0)
    def _():
        job["start"](in_refs, out_refs, *sems)

    @pl.when(step == last)
    def _():
        job["finish"](in_refs, out_refs, *sems)


def _scan_fwd(name, fn, rows_in, scal_in=(), job=None):
    t = rows_in[0].shape[0]
    n = t // CHUNK
    nr, ns = len(rows_in), len(scal_in)
    j_ins, j_outs, j_sems = _job_parts(job)
    nji, njo = len(j_ins), len(j_outs)

    def body(*refs):
        o_ref, sh_ref, ph_ref = refs[nr + ns + nji:nr + ns + nji + 3]
        s_scr = refs[nr + ns + nji + 3 + njo]
        _job_steps(job, refs[nr + ns:nr + ns + nji], refs[nr + ns + nji + 3:nr + ns + nji + 3 + njo],
                   refs[nr + ns + nji + 3 + njo + 1:], pl.program_id(0), n - 1)

        @pl.when(pl.program_id(0) == 0)
        def _():
            s_scr[...] = jnp.zeros_like(s_scr)

        s = s_scr[...]
        sh_ref[0] = s
        ins = [_heads_in(r) for r in refs[:nr]]
        if ns:
            ins += _gdn_scalars(*[r[...] for r in refs[nr:nr + ns]])
        s_new, o, p = fn(s, *ins)
        _heads_out(o_ref, o)
        ph_ref[0] = p
        s_scr[...] = s_new

    return pl.pallas_call(
        body, grid=(n,),
        in_specs=[_scan_spec(a.shape[1], n, False) for a in (*rows_in, *scal_in)] + [_ANY] * nji,
        out_specs=[_scan_spec(WIDTH, n, False), _hist_spec(n, False), _hist_spec(n, False)] + [_ANY] * njo,
        out_shape=[jax.ShapeDtypeStruct((t, WIDTH), f32)] + [jax.ShapeDtypeStruct((n, HEADS, HEAD_DIM, HEAD_DIM), f32)] * 2
        + j_outs,
        scratch_shapes=[pltpu.VMEM((HEADS, HEAD_DIM, HEAD_DIM), f32)] + j_sems,
        compiler_params=pltpu.CompilerParams(dimension_semantics=("arbitrary",)),
        name=name,
    )(*rows_in, *scal_in, *j_ins)


def _scan_bwd(name, fn, rows_in, scal_in, s_hist, p_hist, d_out, job=None):
    t = rows_in[0].shape[0]
    n = t // CHUNK
    nr, ns = len(rows_in), len(scal_in)
    j_ins, j_outs, j_sems = _job_parts(job)
    nji, njo = len(j_ins), len(j_outs)

    def body(*refs):
        sh_ref, ph_ref, do_ref = refs[nr + ns:nr + ns + 3]
        base = nr + ns + 3 + nji
        g_refs = refs[base:base + nr + ns]
        ds_scr = refs[base + nr + ns + njo]
        _job_steps(job, refs[nr + ns + 3:base], refs[base + nr + ns:base + nr + ns + njo],
                   refs[base + nr + ns + njo + 1:], pl.program_id(0), n - 1)

        @pl.when(pl.program_id(0) == 0)
        def _():
            ds_scr[...] = jnp.zeros_like(ds_scr)

        ins = [_heads_in(r) for r in refs[:nr]]
        if ns:
            ins += _gdn_scalars(*[r[...] for r in refs[nr:nr + ns]])
        p = ph_ref[0]
        _, vjp = jax.vjp(lambda s, *a: fn(s, *a, p=p)[:2], sh_ref[0], *ins)
        g = vjp((ds_scr[...], _heads_in(do_ref)))
        ds_scr[...] = g[0]
        for r, v in zip(g_refs[:nr], g[1:1 + nr]):
            _heads_out(r, v)
        if ns:
            for r, v in zip(g_refs[nr:], _gdn_scalars_back(*g[1 + nr:])):
                r[...] = v

    arrs = (*rows_in, *scal_in)
    return pl.pallas_call(
        body, grid=(n,),
        in_specs=[_scan_spec(a.shape[1], n, True) for a in arrs]
        + [_hist_spec(n, True), _hist_spec(n, True), _scan_spec(WIDTH, n, True)] + [_ANY] * nji,
        out_specs=[_scan_spec(a.shape[1], n, True) for a in arrs] + [_ANY] * njo,
        out_shape=[jax.ShapeDtypeStruct(a.shape, f32) for a in arrs] + j_outs,
        scratch_shapes=[pltpu.VMEM((HEADS, HEAD_DIM, HEAD_DIM), f32)] + j_sems,
        compiler_params=pltpu.CompilerParams(dimension_semantics=("arbitrary",)),
        name=name,
    )(*arrs, s_hist, p_hist, d_out, *j_ins)


def _fn_norm_mod(ps, ts):
    nw, shift, scale = ps
    (x,) = ts
    return [_rms(x, nw, NORM_EPS) * (1.0 + scale) + shift]


def _fn_resid_norm_mod(ps, ts):
    gate, nw, shift, scale = ps
    x, mo = ts
    x1 = x + gate * mo
    return [x1, _rms(x1, nw, NORM_EPS) * (1.0 + scale) + shift]


def _fn_gdn_pre(ps, ts):
    cw = ps[:12]
    alog, dtb, seg, tri = ps[12:]
    ba = ts[12]
    outs = []
    for part in range(3):
        x = ts[4 * part:4 * part + 4]
        w = cw[4 * part:4 * part + 4]
        conv = w[3] * x[0] + w[2] * x[1] + w[1] * x[2] + w[0] * x[3]
        u = _silu(conv)
        if part < 2:
            u = u * lax.rsqrt(_segsum(u * u, seg) + 1e-6)
            if part == 0:
                u = u * (HEAD_DIM ** -0.5)
        outs.append(u)
    beta = jax.nn.sigmoid(ba)
    g = -jnp.exp(alog) * _softplus(ba + dtb)
    gc = _xdot_l(tri, g)
    return outs + [beta, gc]


def _fn_gdn_post(ps, ts):
    ow, seg = ps
    o, z = ts
    ms = _segsum(o * o, seg) * (1.0 / HEAD_DIM)
    return [o * lax.rsqrt(ms + NORM_EPS) * ow * _silu(z)]


def _fn_rwkv_pre(ps, ts):
    mu_r, mu_k, mu_v, mu_l, mu_g, w0, w2p, a0, a2p, g2p, k_k, k_a, seg = ps
    r0, r1, k0, k1, v0, v1, l0, l1, g0, g1 = ts
    xr = r0 + (r1 - r0) * mu_r
    xk = k0 + (k1 - k0) * mu_k
    xv = v0 + (v1 - v0) * mu_v
    xl = l0 + (l1 - l0) * mu_l
    xg = g0 + (g1 - g0) * mu_g
    w = -_softplus(-(w0 + _bdot(jnp.tanh(xl), w2p))) - 0.5
    lw = -jnp.exp(w)
    a = jax.nn.sigmoid(a0 + _bdot(xl, a2p))
    g = _bdot(jax.nn.sigmoid(xg), g2p)
    kk = xk * k_k
    kk = kk * lax.rsqrt(_segsum(kk * kk, seg) + 1e-6)
    k2 = xk * (1.0 + (a - 1.0) * k_a)
    return [xr, lw, k2, xv, -kk, kk * a, g]


def _fn_rwkv_post(ps, ts):
    lw_, lb_, rk, seg = ps
    y, r, k2, v, g = ts
    inv = 1.0 / HEAD_DIM
    yc = y - _segsum(y, seg) * inv
    var = _segsum(yc * yc, seg) * inv
    yn = yc * lax.rsqrt(var + LNX_EPS) * lw_ + lb_
    bonus = _segsum(r * k2 * rk, seg) * v
    return [(yn + bonus) * g]


def _fn_merge(ps, ts):
    gla, glb, ya, yb = ts
    return [jax.nn.sigmoid(gla) * ya + jax.nn.sigmoid(glb) * yb]


def _fn_convglu(ps, ts):
    c0, c1, c2 = ps
    g0, g1, g2, up = ts
    return [_silu(c2 * g0 + c1 * g1 + c0 * g2) * up]


def _fn_add(ps, ts):
    acc = ts[0]
    for t in ts[1:]:
        acc = acc + t
    return [acc]


def _add_n(name, arrs, tt):
    return _stage_fwd(name, _fn_add, [_whole(a) for a in arrs], [], [(arrs[0].shape[1], f32)], tt)[0]


def _final_stage(x1, fo, tgt, gate2, nfw, tt):
    rows, d = x1.shape

    def loss_fn(gate, nw, xa, fa, tg):
        y = _rms(xa + gate * fa, nw, NORM_EPS)
        err = (y - tg) ** 2
        return 0.5 * jnp.sum(jnp.mean(err, axis=-1, keepdims=True), axis=0, keepdims=True)

    def body(x_ref, f_ref, t_ref, g_ref, w_ref, dx_ref, df_ref, dg_ref, dw_ref, l_ref):
        i = pl.program_id(0)
        args = (g_ref[...], w_ref[...], x_ref[...], f_ref[...])
        tg = t_ref[...]
        lv, vjp = jax.vjp(lambda g, w, xa, fa: loss_fn(g, w, xa, fa, tg), *args)
        dg, dw, dx, df = vjp(jnp.ones((1, 1), f32))
        dx_ref[...] = dx
        df_ref[...] = df

        @pl.when(i == 0)
        def _():
            dg_ref[...] = jnp.zeros_like(dg_ref)
            dw_ref[...] = jnp.zeros_like(dw_ref)
            l_ref[...] = jnp.zeros_like(l_ref)

        dg_ref[...] += dg
        dw_ref[...] += dw
        l_ref[...] += jnp.broadcast_to(lv, l_ref.shape)

    row = pl.BlockSpec((tt, d), lambda i: (i, 0))
    vec = pl.BlockSpec((1, d), lambda i: (0, 0))
    return pl.pallas_call(
        body, grid=(rows // tt,),
        in_specs=[row, row, row, vec, vec],
        out_specs=[row, row, vec, vec, pl.BlockSpec((1, LANES), lambda i: (0, 0))],
        out_shape=[jax.ShapeDtypeStruct((rows, d), f32)] * 2 + [jax.ShapeDtypeStruct((1, d), f32)] * 2
        + [jax.ShapeDtypeStruct((1, LANES), f32)],
        compiler_params=pltpu.CompilerParams(dimension_semantics=("arbitrary",)),
        name="loss_head",
    )(x1, fo, tgt, gate2, nfw)


def _ada_fwd(c_all, w_shard, b_cols):
    def body(c_ref, w_ref, b_ref, cond_ref, mod_ref):
        cond = _silu(c_ref[...])
        cond_ref[...] = cond
        mod_ref[...] = jnp.dot(cond.astype(bf16), w_ref[...].astype(bf16), preferred_element_type=f32) + b_ref[...]

    n = w_shard.shape[1]
    return pl.pallas_call(
        body, out_shape=[jax.ShapeDtypeStruct(c_all.shape, f32), jax.ShapeDtypeStruct((c_all.shape[0], n), f32)],
        name="ada_fwd",
    )(c_all, w_shard, b_cols)


def _adamw(name, w, g, m, v):
    rows, width = w.shape
    tt = _pick(rows, 128, 8)
    c1 = 1.0 - ADAM_B1 ** ADAM_STEP
    c2 = 1.0 - ADAM_B2 ** ADAM_STEP

    def body(w_ref, g_ref, m_ref, v_ref, d_ref, mo_ref, vo_ref):
        gg = g_ref[...]
        mn = ADAM_B1 * m_ref[...] + (1.0 - ADAM_B1) * gg
        vn = ADAM_B2 * v_ref[...] + (1.0 - ADAM_B2) * (gg * gg)
        m_hat = mn / c1
        v_hat = vn / c2
        d_ref[...] = -ADAM_LR * (m_hat / (jnp.sqrt(v_hat) + ADAM_EPS) + ADAM_WD * w_ref[...])
        mo_ref[...] = mn
        vo_ref[...] = vn

    spec = pl.BlockSpec((tt, width), lambda i: (i, 0))
    return pl.pallas_call(
        body, grid=(rows // tt,), in_specs=[spec] * 4, out_specs=[spec] * 3,
        out_shape=[jax.ShapeDtypeStruct((rows, width), f32)] * 3,
        compiler_params=pltpu.CompilerParams(dimension_semantics=("parallel",)),
        name=name,
    )(w, g, m, v)


def _place():
    return lax.axis_index("x"), lax.axis_index("y"), lax.axis_index("c")


def _ag8(name, blk):
    m, w = blk.shape

    def body(x_ref, out_ref, send_sems, recv_sems, local_sem):
        x, y, c = _place()
        me, sibling = (x, y, c), (x, y, 1 - c)
        chips = _other_chips(x, y)

        def slot(px, py, pc):
            return out_ref.at[4 * px + 2 * py + pc]

        def copy(k, block, to, src=None):
            return pltpu.make_async_remote_copy(src_ref=slot(*block) if src is None else src, dst_ref=slot(*block),
                                                send_sem=send_sems.at[k], recv_sem=recv_sems.at[k], device_id=to,
                                                device_id_type=MESH_ID)

        mine = pltpu.make_async_copy(x_ref, slot(*me), local_sem)
        mine.start()
        first = [copy(0, me, sibling, src=x_ref)] + [copy(1 + j, me, (*chip, c), src=x_ref) for j, chip in enumerate(chips)]
        for cp in first:
            cp.start()
        passed = [copy(4 + j, (*chip, c), sibling) for j, chip in enumerate(chips)]
        for j, chip in enumerate(chips):
            copy(1 + j, (*chip, c), me).wait_recv()
            passed[j].start()
        copy(0, sibling, me).wait_recv()
        for j, chip in enumerate(chips):
            copy(4 + j, (*chip, 1 - c), me).wait_recv()
        for cp in first + passed:
            cp.wait_send()
        mine.wait()

    return pl.pallas_call(
        body, out_shape=jax.ShapeDtypeStruct((8, m, w), blk.dtype),
        in_specs=[pl.BlockSpec(memory_space=pltpu.VMEM)], out_specs=pl.BlockSpec(memory_space=pltpu.VMEM),
        scratch_shapes=[pltpu.SemaphoreType.DMA((7,)), pltpu.SemaphoreType.DMA((7,)), pltpu.SemaphoreType.DMA],
        name=name,
    )(blk)


def _other_chips(x, y):
    return [(1 - x, y), (x, 1 - y), (1 - x, 1 - y)]


_ANY = pl.BlockSpec(memory_space=pl.ANY)


def _rcopy(src, dst, send_sems, recv_sems, k, dev):
    return pltpu.make_async_remote_copy(src_ref=src, dst_ref=dst, send_sem=send_sems.at[k], recv_sem=recv_sems.at[k],
                                        device_id=dev, device_id_type=MESH_ID)


def _run_job(name, job):
    j_ins, j_outs, j_sems = _job_parts(job)
    n = len(j_ins)

    def body(*refs):
        job["start"](refs[:n], refs[n:n + len(j_outs)], *refs[n + len(j_outs):])
        job["finish"](refs[:n], refs[n:n + len(j_outs)], *refs[n + len(j_outs):])

    return pl.pallas_call(body, out_shape=j_outs, in_specs=[_ANY] * n, out_specs=[_ANY] * len(j_outs),
                          scratch_shapes=j_sems, name=name)(*j_ins)


def _ag4_job(ws):
    n = len(ws)

    def plan(w_refs, out_refs, send_sems, recv_sems):
        x, y, c = _place()
        chip = 2 * x + y
        sibling = (x, y, 1 - c)
        chips = _other_chips(x, y)
        mine = [pl.ds(c * (w.shape[0] // 2), w.shape[0] // 2) for w in ws]
        other = [pl.ds((1 - c) * (w.shape[0] // 2), w.shape[0] // 2) for w in ws]
        rc = lambda src, dst, k, dev: _rcopy(src, dst, send_sems, recv_sems, k, dev)
        first = [rc(w_refs[t].at[mine[t]], out_refs[t].at[chip, mine[t]], 7 * t + k, (px, py, c))
                 for t in range(n) for k, (px, py) in enumerate(chips)]
        own = [rc(w_refs[t], out_refs[t].at[chip], 7 * t + 6, sibling) for t in range(n)]
        landed = [rc(out_refs[t].at[2 * px + py, mine[t]], out_refs[t].at[2 * px + py, mine[t]], 7 * t + k, (px, py, c))
                  for t in range(n) for k, (px, py) in enumerate(chips)]
        forward = [rc(out_refs[t].at[2 * px + py, mine[t]], out_refs[t].at[2 * px + py, mine[t]], 7 * t + 3 + k, sibling)
                   for t in range(n) for k, (px, py) in enumerate(chips)]
        handed = [rc(out_refs[t].at[2 * px + py, other[t]], out_refs[t].at[2 * px + py, other[t]], 7 * t + 3 + k, sibling)
                  for t in range(n) for k, (px, py) in enumerate(chips)]
        return first, own, landed, forward, handed

    def start(*refs):
        first, own, _, _, _ = plan(*refs)
        for cp in first + own:
            cp.start()

    def finish(*refs):
        first, own, landed, forward, handed = plan(*refs)
        for arrived, fw in zip(landed, forward):
            arrived.wait_recv()
            fw.start()
        for cp in handed + own:
            cp.wait_recv()
        for cp in first + own + forward:
            cp.wait_send()

    return dict(ins=ws, out_shapes=[jax.ShapeDtypeStruct((4,) + w.shape, w.dtype) for w in ws], nsem=7 * n,
                start=start, finish=finish)


def _rs_sibling_swap(gs, tag):
    n = len(gs)

    def body(*refs):
        g_refs, recv_refs = refs[:n], refs[n:2 * n]
        send_sems, recv_sems = refs[2 * n:]
        x, y, c = _place()
        copies = []
        for t in range(n):
            rh = gs[t].shape[1] // 2
            for s_ in range(4):
                cp = _rcopy(g_refs[t].at[s_, pl.ds((1 - c) * rh, rh)], recv_refs[t].at[s_], send_sems, recv_sems,
                            4 * t + s_, (x, y, 1 - c))
                cp.start()
                copies.append(cp)
        for cp in copies:
            cp.wait_recv()
        for cp in copies:
            cp.wait_send()

    return pl.pallas_call(
        body, out_shape=[jax.ShapeDtypeStruct((4, g.shape[1] // 2, g.shape[2]), g.dtype) for g in gs],
        in_specs=[_ANY] * n, out_specs=[_ANY] * n,
        scratch_shapes=[pltpu.SemaphoreType.DMA((4 * n,)), pltpu.SemaphoreType.DMA((4 * n,))],
        name="grads_sibling_swap_" + tag,
    )(*gs)


def _chip_exchange_job(ps):
    n = len(ps)

    def plan(p_refs, recv_refs, send_sems, recv_sems):
        x, y, c = _place()
        return [_rcopy(p_refs[t].at[2 * px + py], recv_refs[t].at[k], send_sems, recv_sems, 3 * t + k, (px, py, c))
                for t in range(n) for k, (px, py) in enumerate(_other_chips(x, y))]

    def start(*refs):
        for cp in plan(*refs):
            cp.start()

    def finish(*refs):
        copies = plan(*refs)
        for cp in copies:
            cp.wait_recv()
        for cp in copies:
            cp.wait_send()

    return dict(ins=ps, out_shapes=[jax.ShapeDtypeStruct((3,) + p.shape[1:], p.dtype) for p in ps], nsem=3 * n,
                start=start, finish=finish)


_JOIN_PIECES = 4


def _rs_sibling_join(qs):
    n = len(qs)
    npc = _JOIN_PIECES

    def body(*refs):
        q_refs, out_refs = refs[:n], refs[n:2 * n]
        send_sems, recv_sems = refs[2 * n:]
        x, y, c = _place()
        copies = []
        for t in range(n):
            rh = qs[t].shape[0] // 2
            pr = rh // npc
            for i in range(npc):
                rows = pl.ds(c * rh + i * pr, pr)
                cp = _rcopy(q_refs[t].at[rows], out_refs[t].at[rows], send_sems, recv_sems, npc * t + i,
                            (x, y, 1 - c))
                cp.start()
                copies.append(cp)
        for t in range(n):
            rh = qs[t].shape[0] // 2
            pr = rh // npc
            for i in range(npc):
                rows = pl.ds((1 - c) * rh + i * pr, pr)
                _rcopy(q_refs[t].at[rows], out_refs[t].at[rows], send_sems, recv_sems, npc * t + i,
                       (x, y, 1 - c)).wait_recv()
        for cp in copies:
            cp.wait_send()

    return pl.pallas_call(
        body, out_shape=[jax.ShapeDtypeStruct(q.shape, q.dtype) for q in qs],
        in_specs=[_ANY] * n, out_specs=[_ANY] * n, input_output_aliases={t: t for t in range(n)},
        scratch_shapes=[pltpu.SemaphoreType.DMA((npc * n,)), pltpu.SemaphoreType.DMA((npc * n,))],
        name="grads_sibling_join",
    )(*qs)


def _add_half(name, g, recv, ci):
    S, r, w = g.shape
    rh = r // 2
    tt = _pick(rh, 256, 16)
    nb = rh // tt

    def body(c_ref, a_ref, b_ref, o_ref, ob_ref):
        v = a_ref[...] + b_ref[...]
        o_ref[...] = v
        ob_ref[...] = v.astype(bf16)

    blk = pl.BlockSpec((1, tt, w), lambda s_, i, cr: (s_, i, 0))
    grid_spec = pltpu.PrefetchScalarGridSpec(
        num_scalar_prefetch=1, grid=(S, nb),
        in_specs=[pl.BlockSpec((1, tt, w), lambda s_, i, cr: (s_, cr[0] * nb + i, 0)), blk],
        out_specs=[blk, blk])
    return pl.pallas_call(body, grid_spec=grid_spec,
                          out_shape=[jax.ShapeDtypeStruct((S, rh, w), f32), jax.ShapeDtypeStruct((S, rh, w), bf16)],
                          name=name)(ci.reshape(1).astype(jnp.int32), g, recv)


def _sum_chip(name, pair, others, chip, ci):
    _, rh, w = pair.shape
    tt = _pick(rh, 128, 16)
    nb = rh // tt

    def body(chip_ref, core_ref, a_ref, b_ref, o_ref):
        o_ref[...] = ((a_ref[0] + b_ref[0].astype(f32)) + b_ref[1].astype(f32)) + b_ref[2].astype(f32)

    grid_spec = pltpu.PrefetchScalarGridSpec(
        num_scalar_prefetch=2, grid=(nb,),
        in_specs=[pl.BlockSpec((1, tt, w), lambda i, ch, co: (ch[0], i, 0)),
                  pl.BlockSpec((3, tt, w), lambda i, ch, co: (0, i, 0))],
        out_specs=pl.BlockSpec((tt, w), lambda i, ch, co: (co[0] * nb + i, 0)))
    return pl.pallas_call(body, grid_spec=grid_spec, out_shape=jax.ShapeDtypeStruct((2 * rh, w), f32),
                          name=name)(chip.reshape(1).astype(jnp.int32), ci.reshape(1).astype(jnp.int32), pair, others)


def _sum_devices(gathered, head_row):
    _, rows, width = gathered.shape

    def body(g_ref, out_ref, head_ref):
        acc = g_ref[0]
        for d in range(1, 8):
            acc = acc + g_ref[d]
        out_ref[...] = acc
        row = acc[head_row:head_row + 1, :]
        hs = row[:, 0:HEAD_DIM]
        for h in range(1, HEADS):
            hs = hs + row[:, h * HEAD_DIM:(h + 1) * HEAD_DIM]
        head_ref[...] = jnp.zeros_like(head_ref)
        head_ref[0:1, 0:HEAD_DIM] = hs

    return pl.pallas_call(
        body, out_shape=[jax.ShapeDtypeStruct((rows, width), f32), jax.ShapeDtypeStruct((8, LANES), f32)],
        name="small_grads_sum",
    )(gathered)


def _pack(arrs, rows_mult, dtype):
    flat = jnp.concatenate([a.reshape(-1).astype(dtype) for a in arrs])
    per = PACK_W * rows_mult
    total = -(-flat.shape[0] // per) * per
    return jnp.pad(flat, (0, total - flat.shape[0])).reshape(total // PACK_W, PACK_W)


def _unpack(buf, shapes):
    flat = buf.reshape(-1)
    out, off = [], 0
    for s in shapes:
        n = int(np.prod(s))
        out.append(flat[off:off + n].reshape(s))
        off += n
    return out


_BIG = ["w_in", "w_branch_gdn", "w_branch_rwkv", "w_out", "w_ffn_in", "w_ffn_out"]
_MID = ["conv_gdn", "conv_ffn", "w2", "a2", "g2"]
_SMALL =["b_ada", "norm1_w", "a_log", "dt_bias", "onorm_gdn", "mu_rwkv", "w0", "a0", "k_k", "k_a", "r_k", "lnx_w",
          "lnx_b", "norm2_w", "norm_f_w"]
_ORDER = ["w_ada", "b_ada", "norm1_w", "w_in", "conv_gdn", "a_log", "dt_bias", "onorm_gdn", "w_branch_gdn", "mu_rwkv",
          "w0", "w2", "a0", "a2", "g2", "k_k", "k_a", "r_k", "lnx_w", "lnx_b", "w_branch_rwkv", "w_out", "norm2_w",
          "w_ffn_in", "conv_ffn", "w_ffn_out", "norm_f_w"]


def _win_pad(w):
    z = lambda n: jnp.zeros((w.shape[0], n), w.dtype)
    return jnp.concatenate([w[:, 0:1536], w[:, 2064:3600], w[:, 1536:2048], w[:, 3600:3728], w[:, 2048:2064], z(112),
                            w[:, 3728:3888], z(96), w[:, 3888:5936]], axis=1)


def _win_unpad(g):
    return jnp.concatenate([g[:, 0:1536], g[:, 3072:3584], g[:, 3712:3728], g[:, 1536:3072], g[:, 3584:3712],
                            g[:, 3840:4000], g[:, 4096:6144]], axis=1)


def kernel(x, c, w_ada, b_ada, norm1_w, w_in, conv_gdn, a_log, dt_bias, onorm_gdn, w_branch_gdn, mu_rwkv, w0, w2, a0, a2, g2, k_k, k_a, r_k, lnx_w, lnx_b, w_branch_rwkv, w_out, norm2_w, w_ffn_in, conv_ffn, w_ffn_out, norm_f_w, loss_target, m_w_ada, m_b_ada, m_norm1_w, m_w_in, m_conv_gdn, m_a_log, m_dt_bias, m_onorm_gdn, m_w_branch_gdn, m_mu_rwkv, m_w0, m_w2, m_a0, m_a2, m_g2, m_k_k, m_k_a, m_r_k, m_lnx_w, m_lnx_b, m_w_branch_rwkv, m_w_out, m_norm2_w, m_w_ffn_in, m_conv_ffn, m_w_ffn_out, m_norm_f_w, v_w_ada, v_b_ada, v_norm1_w, v_w_in, v_conv_gdn, v_a_log, v_dt_bias, v_onorm_gdn, v_w_branch_gdn, v_mu_rwkv, v_w0, v_w2, v_a0, v_a2, v_g2, v_k_k, v_k_a, v_r_k, v_lnx_w, v_lnx_b, v_w_branch_rwkv, v_w_out, v_norm2_w, v_w_ffn_in, v_conv_ffn, v_w_ffn_out, v_norm_f_w):
    args = dict(locals())
    W = {n: args[n] for n in _ORDER}
    Mo = {n: args["m_" + n] for n in _ORDER}
    Vo = {n: args["v_" + n] for n in _ORDER}
    shapes = {n: W[n].shape for n in _ORDER}
    sq = lambda a: a.reshape(a.shape[-2:]) if a.ndim == 3 else a.reshape(1, -1)
    row = lambda a: a.reshape(1, -1)

    xi, yi, ci = lax.axis_index("x"), lax.axis_index("y"), lax.axis_index("c")
    dev = 4 * xi + 2 * yi + ci
    chip = 2 * xi + yi

    x2 = x[0]
    tgt = loss_target[0]
    T, D = x2.shape
    N = T // CHUNK
    tt_l = _pick(T, 256, CHUNK)
    tt_h = _pick(T, 128, CHUNK)

    mid_shapes = [shapes[n][1:] for n in _MID]
    small_blk = _pack([c] + [W[n] for n in _MID], 8, f32)
    small_all = _ag8("gather_c_mid", small_blk)
    c_all = small_all[:, 0, :]
    per_chip = small_all[0::2].reshape(4, -1)[:, D:]
    mid = [dict(zip(_MID, _unpack(per_chip[j], mid_shapes))) for j in range(4)]
    catm = lambda n: jnp.concatenate([mid[j][n] for j in range(4)], axis=1)
    conv_gdn_f, conv_ffn_f = catm("conv_gdn"), catm("conv_ffn")
    w2f, a2f, g2f = catm("w2"), catm("a2"), catm("g2")

    (win_s,) = _run_job("w_in_all_gather", _ag4_job([sq(W["w_in"]).astype(bf16)]))
    later_weights = _ag4_job([sq(W[n]).astype(bf16) for n in _BIG[1:]])
    win_p = _win_pad(jnp.concatenate([win_s[j] for j in range(4)], axis=1))
    zpad = lambda a, top, bot: jnp.pad(a, ((top, bot), (0, 0)))
    w2p, a2p, g2p = zpad(w2f, 0, 64), zpad(a2f, 64, 0), zpad(g2f, 0, 96)

    ncol = shapes["w_ada"][2]
    b_cols = lax.dynamic_slice(sq(W["b_ada"]), (0, chip * ncol), (1, ncol))
    cond16, mod_cols = _ada_fwd(jnp.pad(c_all, ((0, 8), (0, 0))), sq(W["w_ada"]), b_cols)
    mod_all = _ag8("gather_mod", mod_cols[:8])
    mod_mine = lax.dynamic_slice(mod_all[0::2], (0, dev, 0), (4, 1, ncol)).reshape(1, 4 * ncol)
    shift1, scale1, gate1, shift2, scale2, gate2 = [mod_mine[:, i * D:(i + 1) * D] for i in range(6)]

    seg = _seg_matrix(WIDTH, HEAD_DIM)
    norm1 = [sq(W["norm1_w"]), shift1, scale1]
    h1 = _stage_fwd("norm_mod1", _fn_norm_mod, [_whole(x2)], norm1, [(D, bf16)], tt_l)[0]
    p = _matmul("in_proj", h1, win_p, "nn")

    cgq = [row(conv_gdn_f[j, part * WIDTH:(part + 1) * WIDTH]) for part in range(3) for j in range(4)]
    lane_pad = lambda a: jnp.pad(row(a), ((0, 0), (8, LANES - 16)))
    gdn_pre_ps = cgq + [lane_pad(W["a_log"]), lane_pad(W["dt_bias"]), seg, _chunk_tri(tt_h, CHUNK)]
    gdn_pre_ts = [(p, WIDTH, part, (0, 1, 2, 3)) for part in range(3)] + [(p, LANES, 29, None)]
    q_, k_, v_, beta_t, gc_t = _stage_fwd("gdn_pre", _fn_gdn_pre, gdn_pre_ts, gdn_pre_ps,
                                          [(WIDTH, f32)] * 3 + [(LANES, f32)] * 2, tt_h)
    o_, gdn_hist, gdn_inv, wbg_s, wbr_s, wout_s, wfi_s, wfo_s = _scan_fwd(
        "gdn_scan", _gdn_chunk, [q_, k_, v_], [beta_t, gc_t], job=later_weights)
    wout_f = wout_s.reshape(D, D)
    wfo = wfo_s.reshape(D_FF, D)
    ow512 = jnp.tile(row(W["onorm_gdn"]), (1, HEADS))
    gdn_post_ts = [_whole(o_), (p, WIDTH, 6, None)]
    ya = _stage_fwd("gdn_post", _fn_gdn_post, gdn_post_ts, [ow512, seg], [(WIDTH, bf16)], tt_l)[0]

    mu = sq(W["mu_rwkv"])
    rw_ps = [mu[:, 0:512], mu[:, 512:1024], mu[:, 1024:1536], mu[:, 1536:1664], jnp.pad(mu[:, 1664:1824], ((0, 0), (0, 96))),
             sq(W["w0"]), w2p, sq(W["a0"]), a2p, g2p, sq(W["k_k"]), sq(W["k_a"]), seg]
    rw_ts = [(p, WIDTH, 3, (0, 1)), (p, WIDTH, 4, (0, 1)), (p, WIDTH, 5, (0, 1)), (p, LANES, 28, (0, 1)),
             (p, 256, 15, (0, 1))]
    rw_out = _stage_fwd("rwkv_pre", _fn_rwkv_pre, rw_ts, rw_ps, [(WIDTH, f32)] * 7, tt_h)
    r_, lw_, k2_, vv_, na_, b_, g_ = rw_out
    rw_ins = [r_, lw_, k2_, vv_, na_, b_]
    y_, rw_hist, rw_inv = _scan_fwd("rwkv_scan", _rwkv_chunk, rw_ins)
    rwp_ps = [sq(W["lnx_w"]), sq(W["lnx_b"]), row(W["r_k"]), seg]
    rwp_ts = [_whole(y_), _whole(r_), _whole(k2_), _whole(vv_), _whole(g_)]
    yb = _stage_fwd("rwkv_post", _fn_rwkv_post, rwp_ts, rwp_ps, [(WIDTH, bf16)], tt_l)[0]

    big_a = _matmul("branch_gdn", ya, wbg_s, "nn", shards=4)
    big_b = _matmul("branch_rwkv", yb, wbr_s, "nn", shards=4)
    merge_ts = [(p, D, 4, None), (p, D, 5, None), _whole(big_a), _whole(big_b)]
    merged = _stage_fwd("merge", _fn_merge, merge_ts, [], [(D, bf16)], tt_l)[0]
    mo = _matmul("out_proj", merged, wout_f, "nn")
    norm2 = [gate1, sq(W["norm2_w"]), shift2, scale2]
    x1, h2 = _stage_fwd("resid_norm_mod2", _fn_resid_norm_mod, [_whole(x2), _whole(mo)], norm2, [(D, f32), (D, bf16)], tt_l)
    f = _matmul("ffn_in", h2, wfi_s, "nn", shards=4, tm=512, tn=1408)
    cg_ps = [row(conv_ffn_f[j]) for j in range(3)]
    cg_ts = [(f, D_FF, 0, (0, 1, 2)), (f, D_FF, 1, None)]
    act = _stage_fwd("convglu", _fn_convglu, cg_ts, cg_ps, [(D_FF, bf16)], tt_h)[0]
    fo = _matmul("ffn_out", act, wfo, "nn", tk=1408)

    dx1_a, dfo, dgate2, dnormf, loss_part = _final_stage(x1, fo, tgt, gate2, row(W["norm_f_w"]), tt_l)

    dact = _matmul("d_act", dfo, wfo, "nt", tm=512, tn=1408)
    g_wfo = _matmul("g_ffn_out", act, dfo, "tn", tm=1408)
    _, dcf, df = _stage_bwd("convglu_bwd", _fn_convglu, cg_ts, cg_ps, [[_whole(dact)]], tt_h, [True] * 2, [True] * 3,
                            joint=([0, 1], 2 * D_FF, 0, None))
    dh2 = _matmul("d_h2", df, wfi_s, "nt", shards=4, tm=512, tk=1408)
    g_wfi = _matmul("g_ffn_in", h2, df, "tn", shards=4, tn=1408)
    (dx_a, dmo), (dgate1, dnorm2, dshift2, dscale2), _ = _stage_bwd(
        "resid_norm_mod2_bwd", _fn_resid_norm_mod, [_whole(x2), _whole(mo)], norm2,
        [[_whole(dx1_a)], [_whole(dh2)]], tt_l, [True, True], [True] * 4)
    dmerged = _matmul("d_merged", dmo, wout_f, "nt")
    g_wout = _matmul("g_out_proj", merged, dmo, "tn")
    (dbig_a, dbig_b), _, dp = _stage_bwd("merge_bwd", _fn_merge, merge_ts, [], [[_whole(dmerged)]], tt_l, [True] * 4, [],
                                         joint=([0, 1], p.shape[1], 2, None))
    dya = _matmul("d_ya", dbig_a, wbg_s, "nt", shards=4)
    g_wbg = _matmul("g_branch_gdn", ya, dbig_a, "tn", shards=4)
    dyb = _matmul("d_yb", dbig_b, wbr_s, "nt", shards=4)
    g_wbr = _matmul("g_branch_rwkv", yb, dbig_b, "tn", shards=4)

    (dy_, dr_p, dk2_p, dv_p, dg_p), (dlnxw, dlnxb, drk), _ = _stage_bwd(
        "rwkv_post_bwd", _fn_rwkv_post, rwp_ts, rwp_ps, [[_whole(dyb)]], tt_l, [True] * 5, [True, True, True, False])
    gs_a = [g_wbg, g_wbr, g_wout.reshape(4, D // 4, D), g_wfi, g_wfo.reshape(4, D_FF // 4, D)]
    pairs_a = [_add_half("grads_pair_sum%d" % (t + 1), g, r_, ci) for t, (g, r_) in enumerate(zip(gs_a, _rs_sibling_swap(gs_a, "a")))]
    dr_c, dlw_c, dk2_c, dv_c, dna_c, db_c, *others_a = _scan_bwd(
        "rwkv_scan_bwd", _rwkv_chunk, rw_ins, [], rw_hist, rw_inv, dy_, job=_chip_exchange_job([pb for _, pb in pairs_a]))
    rw_cots = [[_whole(dr_p), _whole(dr_c)], [_whole(dlw_c)], [_whole(dk2_p), _whole(dk2_c)],
               [_whole(dv_p), _whole(dv_c)], [_whole(dna_c)], [_whole(db_c)], [_whole(dg_p)]]
    (dl_, dg_), rw_dp, dp = _stage_bwd("rwkv_pre_bwd", _fn_rwkv_pre, rw_ts, rw_ps, rw_cots, tt_h, [True] * 5,
                                       [True] * 12 + [False], joint=([0, 1, 2], p.shape[1], 1, dp))
    dmu_r, dmu_k, dmu_v, dmu_l, dmu_g, dw0, dw2p, da0, da2p, dg2p, dkk, dka = rw_dp

    (do_,), (dow512,), dp = _stage_bwd("gdn_post_bwd", _fn_gdn_post, gdn_post_ts, [ow512, seg], [[_whole(dya)]], tt_l,
                                       [True, True], [True, False], joint=([1], p.shape[1], 6, dp))
    d_gdn = _scan_bwd("gdn_scan_bwd", _gdn_chunk, [q_, k_, v_], [beta_t, gc_t], gdn_hist, gdn_inv, do_)
    gdn_cots = [[_whole(a)] for a in d_gdn]
    (dba,), gdn_dp, dp = _stage_bwd("gdn_pre_bwd", _fn_gdn_pre, gdn_pre_ts, gdn_pre_ps, gdn_cots, tt_h, [True] * 4,
                                    [True] * 14 + [False, False], joint=([0, 1, 2], p.shape[1], 0, dp))
    dp = lax.dynamic_update_slice(dp, jnp.concatenate([dl_, dba, dg_], axis=1), (0, 3584))
    dh1 = _matmul("d_h1", dp, win_p, "nt")
    g_win = _win_unpad(_matmul("g_in_proj", h1, dp, "tn"))
    (dx_b,), (dnorm1, dshift1, dscale1), _ = _stage_bwd("norm_mod1_bwd", _fn_norm_mod, [_whole(x2)], norm1,
                                                        [[_whole(dh1)]], tt_l, [True], [True] * 3)
    grad_x = _add_n("grad_x", [dx_a, dx_b], tt_l)

    dmod = jnp.concatenate([dshift1, dscale1, dgate1, dshift2, dscale2, dgate2], axis=1)
    g_conv_gdn = jnp.concatenate([jnp.concatenate([gdn_dp[4 * part + j] for part in range(3)], axis=1) for j in range(4)], axis=0)
    g_conv_ffn = jnp.concatenate(dcf, axis=0)
    g_mu = jnp.concatenate([dmu_r, dmu_k, dmu_v, dmu_l, dmu_g[:, :160]], axis=1)
    small_parts = {"b_ada": dmod, "norm1_w": dnorm1, "a_log": gdn_dp[12][:, 8:16], "dt_bias": gdn_dp[13][:, 8:16],
                   "mu_rwkv": g_mu, "w0": dw0, "a0": da0, "k_k": dkk, "k_a": dka, "r_k": drk, "lnx_w": dlnxw,
                   "lnx_b": dlnxb, "norm2_w": dnorm2, "norm_f_w": dnormf}
    small_names = [n for n in _SMALL if n != "onorm_gdn"]
    mid_full = [g_conv_gdn, g_conv_ffn, dw2p[0:64], da2p[64:128], dg2p[0:160]]
    body_rows = _pack([small_parts[n] for n in small_names] + [loss_part[:, 0:1]] + mid_full, 1, f32)
    head_row = body_rows.shape[0]
    small_g = jnp.concatenate([body_rows, jnp.pad(dow512, ((0, 0), (0, PACK_W - WIDTH)))], axis=0)
    small_g = jnp.pad(small_g, ((0, -small_g.shape[0] % 8), (0, 0)))
    small_all_g = _ag8("gather_small_grads", small_g)
    small_sum, head_sum = _sum_devices(small_all_g, head_row)
    small_shapes = [shapes[n][1:] if n != "norm_f_w" else shapes[n] for n in small_names]
    un = _unpack(small_sum, small_shapes + [(1,)] + [g.shape for g in mid_full])
    small_grads = dict(zip(small_names, un))
    loss = un[len(small_names)].reshape(())
    small_grads["onorm_gdn"] = head_sum[0, 0:HEAD_DIM]
    for n, g in zip(_MID, un[len(small_names) + 1:]):
        wcols = shapes[n][2]
        small_grads[n] = lax.dynamic_slice(g, (0, chip * wcols), (g.shape[0], wcols))

    dmod_all = small_all_g[:, 0:6, :].reshape(8, 6 * PACK_W)
    dmod_cols = lax.dynamic_slice(dmod_all, (0, chip * ncol), (8, ncol))
    g_wada = _matmul("g_w_ada", cond16, jnp.pad(dmod_cols, ((0, 8), (0, 0))), "tn")

    nwin = shapes["w_in"][2]
    g_win_s = g_win.reshape(D, 4, nwin).transpose(1, 0, 2)
    pairs = [_add_half("grads_pair_sum0", g_win_s, _rs_sibling_swap([g_win_s], "b")[0], ci)] + pairs_a
    others = list(_run_job("w_in_grads_chip_exchange", _chip_exchange_job([pairs[0][1]]))) + others_a
    halves = [_sum_chip("grads_chip_sum%d" % t, pf, o_, chip, ci) for t, ((pf, _), o_) in enumerate(zip(pairs, others))]
    big_grads = dict(zip(_BIG, _rs_sibling_join(halves)))

    res = {tag: {} for tag in ("grad", "delta", "new_m", "new_v")}

    def put(n, g, d, m_, v_):
        for tag, val in zip(("grad", "delta", "new_m", "new_v"), (g, d, m_, v_)):
            res[tag][n] = val.reshape(shapes[n])

    for n in _BIG:
        put(n, big_grads[n], *_adamw("adamw_" + n, sq(W[n]), big_grads[n], sq(Mo[n]), sq(Vo[n])))
    put("w_ada", g_wada, *_adamw("adamw_w_ada", sq(W["w_ada"]), g_wada, sq(Mo["w_ada"]), sq(Vo["w_ada"])))
    rest = _SMALL + _MID
    pk = lambda d: _pack([d[n] for n in rest], 8, f32)
    sg = pk(small_grads)
    sm = _adamw("adamw_small", pk(W), sg, pk(Mo), pk(Vo))
    for tag, buf in zip(("grad", "delta", "new_m", "new_v"), (sg,) + tuple(sm)):
        res[tag].update(zip(rest, _unpack(buf, [shapes[n] for n in rest])))
    outs = [loss, grad_x.reshape(x.shape)]
    for tag in ("grad", "delta", "new_m", "new_v"):
        outs += [res[tag][n] for n in _ORDER]
    return tuple(outs)
```

```python
import functools
import math

import numpy as np
import jax
import jax.numpy as jnp
from jax import lax
from jax.experimental import pallas as pl
from jax.experimental.pallas import tpu as pltpu

f32 = jnp.float32
bf16 = jnp.bfloat16

LANES = 128
HEADS = 8
HEAD_DIM = 64
WIDTH = HEADS * HEAD_DIM
CHUNK = 64
D_FF = 2816
NORM_EPS = 1e-6
LNX_EPS = 64e-5
PACK_W = 1024
MESH_ID = pl.DeviceIdType.MESH

ADAM_LR, ADAM_B1, ADAM_B2, ADAM_EPS, ADAM_WD, ADAM_STEP = 0.001, 0.9, 0.999, 1e-08, 0.01, 10


def _pick(n, target, mult):
    if n <= target:
        return n
    best = None
    for t in range(mult, target + 1, mult):
        if n % t == 0:
            best = t
    assert best is not None, (n, target, mult)
    return best


def _split_bf16(x, n):
    parts, r = [], x
    for i in range(n):
        p = r.astype(bf16)
        parts.append(p)
        if i + 1 < n:
            r = r - p.astype(f32)
    return parts


def _xdot_r_impl(x, m, n, dims):
    acc = None
    for p in _split_bf16(x, n):
        t = lax.dot_general(p, m, dims, preferred_element_type=f32)
        acc = t if acc is None else acc + t
    return acc


def _make_xdot_r(n):
    nn = (((1,), (0,)), ((), ()))
    nt = (((1,), (1,)), ((), ()))

    @jax.custom_vjp
    def xdot(x, m):
        return _xdot_r_impl(x, m, n, nn)

    def fwd(x, m):
        return _xdot_r_impl(x, m, n, nn), m

    def bwd(m, ct):
        return _xdot_r_impl(ct, m, n, nt), jnp.zeros_like(m)

    xdot.defvjp(fwd, bwd)
    return xdot


_segsum = _make_xdot_r(2)


def _xdot_l_impl(m, x, n, dims):
    acc = None
    for p in _split_bf16(x, n):
        t = lax.dot_general(m, p, dims, preferred_element_type=f32)
        acc = t if acc is None else acc + t
    return acc


@jax.custom_vjp
def _xdot_l(m, x):
    return _xdot_l_impl(m, x, 3, (((1,), (0,)), ((), ())))


def _xdot_l_fwd(m, x):
    return _xdot_l(m, x), m


def _xdot_l_bwd(m, ct):
    return jnp.zeros_like(m), _xdot_l_impl(m, ct, 3, (((0,), (0,)), ((), ())))


_xdot_l.defvjp(_xdot_l_fwd, _xdot_l_bwd)


@jax.custom_vjp
def _bdot(x, w):
    return jnp.dot(x.astype(bf16), w.astype(bf16), preferred_element_type=f32)


def _bdot_fwd(x, w):
    return _bdot(x, w), (x, w)


def _bdot_bwd(res, ct):
    x, w = res
    c = ct.astype(bf16)
    dx = lax.dot_general(c, w.astype(bf16), (((1,), (1,)), ((), ())), preferred_element_type=f32)
    dw = lax.dot_general(x.astype(bf16), c, (((0,), (0,)), ((), ())), preferred_element_type=f32)
    return dx, dw


_bdot.defvjp(_bdot_fwd, _bdot_bwd)


def _silu(x):
    return x * jax.nn.sigmoid(x)


def _softplus(x):
    return jnp.maximum(x, 0.0) + jnp.log(1.0 + jnp.exp(-jnp.abs(x)))


def _rms(x, w, eps):
    return x * lax.rsqrt(jnp.mean(x * x, axis=-1, keepdims=True) + eps) * w


def _seg_matrix(width, seg):
    i = np.arange(width)
    return jnp.asarray((i[:, None] // seg) == (i[None, :] // seg), dtype=bf16)


def _chunk_tri(rows, chunk):
    i = np.arange(rows)
    return jnp.asarray(((i[:, None] // chunk) == (i[None, :] // chunk)) & (i[:, None] >= i[None, :]), dtype=bf16)


HALO = 8


def _full_spec(shape):
    nd = len(shape)
    return pl.BlockSpec(shape, lambda i: (0,) * nd)


def _entry_specs(entries, tt, block_of):
    specs, ops = [], []
    for arr, w, ci, shifts in entries:
        specs.append(pl.BlockSpec((tt, w), lambda i, ci=ci: (block_of(i), ci)))
        ops.append(arr)
        if shifts:
            specs.append(pl.BlockSpec((HALO, w), lambda i, ci=ci: (jnp.maximum(block_of(i) * (tt // HALO) - 1, 0), ci)))
            ops.append(arr)
    return specs, ops


def _load_entries(entries, refs, first):
    tiles, k = [], 0
    for _, w, _, shifts in entries:
        x = refs[k][...].astype(f32)
        k += 1
        if not shifts:
            tiles.append(x)
            continue
        halo = jnp.where(first, 0.0, refs[k][...].astype(f32))
        k += 1
        row = lax.broadcasted_iota(jnp.int32, (HALO, w), 0)
        for s in shifts:
            if s == 0:
                tiles.append(x)
                continue
            r = pltpu.roll(x, s, 0)
            head = jnp.where(row < s, pltpu.roll(halo, s, 0), r[0:HALO])
            tiles.append(jnp.concatenate([head, r[HALO:]], axis=0))
    return tiles


def _unshift_sum(grads, shifts, carry, tt):
    w = grads[0].shape[1]
    row = lax.broadcasted_iota(jnp.int32, (tt, w), 0)
    row8 = lax.broadcasted_iota(jnp.int32, (HALO, w), 0)
    dx, out = None, jnp.zeros((HALO, w), f32)
    for d, s in zip(grads, shifts):
        if s == 0:
            part = d
        else:
            part = jnp.where(row < tt - s, pltpu.roll(d, tt - s, 0), 0.0)
            out = out + jnp.where(row8 >= HALO - s, pltpu.roll(d[0:HALO], HALO - s, 0), 0.0)
        dx = part if dx is None else dx + part
    return jnp.concatenate([dx[:tt - HALO], dx[tt - HALO:] + carry], axis=0), out


def _stage_fwd(name, fn, tiles, params, outs, tt):
    rows = tiles[0][0].shape[0]
    npar = len(params)
    specs, ops = _entry_specs(tiles, tt, lambda i: i)
    nin = len(ops)

    def body(*refs):
        ts = _load_entries(tiles, refs[:nin], pl.program_id(0) == 0)
        ps = [r[...] for r in refs[nin:nin + npar]]
        res = fn(ps, ts)
        for r, v in zip(refs[nin + npar:], res):
            r[...] = v.astype(r.dtype)

    return pl.pallas_call(
        body, grid=(rows // tt,),
        in_specs=specs + [_full_spec(p.shape) for p in params],
        out_specs=[pl.BlockSpec((tt, w), lambda i: (i, 0)) for (w, _) in outs],
        out_shape=[jax.ShapeDtypeStruct((rows, w), dt) for (w, dt) in outs],
        compiler_params=pltpu.CompilerParams(dimension_semantics=("parallel",)),
        name=name,
    )(*ops, *params)


def _stage_bwd(name, fn, tiles, params, cots, tt, tile_grad, param_grad, joint=None):
    rows = tiles[0][0].shape[0]
    nblk = rows // tt
    npar = len(params)
    block_of = lambda i: nblk - 1 - i
    specs, ops = _entry_specs(tiles, tt, block_of)
    nin = len(ops)
    flat_cots = [c for group in cots for c in group]
    groups = [len(g) for g in cots]
    ncot = len(flat_cots)
    counts = [len(e[3]) if e[3] else 1 for e in tiles]
    dt_entries = [e for e, g in zip(tiles, tile_grad) if g]
    dp_shapes = [p.shape for p, g in zip(params, param_grad) if g]
    ndt = len(dt_entries)
    carry_w = [e[1] for e in dt_entries if e[3]]
    flags = [g for g, n in zip(tile_grad, counts) for _ in range(n)]
    members, j_width, j_cidx, j_buf = joint if joint else ([], 0, 0, None)
    solo = [k for k in range(ndt) if k not in members]
    nsolo, njoint, nbuf = len(solo), int(bool(members)), int(j_buf is not None)
    j_block = sum(dt_entries[k][1] for k in members)

    def body(*refs):
        i = pl.program_id(0)
        p_refs = refs[nin:nin + npar]
        c_refs = refs[nin + npar:nin + npar + ncot]
        base = nin + npar + ncot + nbuf
        dt_refs = refs[base:base + nsolo]
        joint_refs = refs[base + nsolo:base + nsolo + njoint]
        dp_refs = refs[base + nsolo + njoint:base + nsolo + njoint + len(dp_shapes)]
        carry_refs = refs[base + nsolo + njoint + len(dp_shapes):]
        ts = _load_entries(tiles, refs[:nin], block_of(i) == 0)
        ps = [r[...] for r in p_refs]

        def f(dp, dt):
            dp, dt = iter(dp), iter(dt)
            pp = [next(dp) if g else p for p, g in zip(ps, param_grad)]
            tl = [next(dt) if g else t for t, g in zip(ts, flags)]
            return fn(pp, tl)

        _, vjp = jax.vjp(f, [p for p, g in zip(ps, param_grad) if g], [t for t, g in zip(ts, flags) if g])
        cs, j = [], 0
        for n in groups:
            acc = c_refs[j][...].astype(f32)
            for q in range(1, n):
                acc = acc + c_refs[j + q][...].astype(f32)
            cs.append(acc)
            j += n
        gp, gt = vjp(cs)

        @pl.when(i == 0)
        def _():
            for r in dp_refs:
                r[...] = jnp.zeros_like(r)
            for r in carry_refs:
                r[...] = jnp.zeros_like(r)

        gt, k, kc, dxs = list(gt), 0, 0, []
        for e, n in zip(dt_entries, [n for n, g in zip(counts, tile_grad) if g]):
            if e[3]:
                dx, out = _unshift_sum(gt[k:k + n], e[3], carry_refs[kc][...], tt)
                carry_refs[kc][...] = out
                kc += 1
            else:
                dx = gt[k]
            dxs.append(dx)
            k += n
        for r, k in zip(dt_refs, solo):
            r[...] = dxs[k]
        off = 0
        for k in members:
            w = dt_entries[k][1]
            joint_refs[0][:, off:off + w] = dxs[k]
            off += w
        for r, v in zip(dp_refs, gp):
            r[...] += v

    res = pl.pallas_call(
        body, grid=(nblk,),
        in_specs=specs + [_full_spec(p.shape) for p in params]
        + [pl.BlockSpec((tt, w), lambda i, ci=ci: (block_of(i), ci)) for (_, w, ci, *_) in flat_cots]
        + [pl.BlockSpec(memory_space=pl.ANY)] * nbuf,
        out_specs=[pl.BlockSpec((tt, dt_entries[k][1]), lambda i: (block_of(i), 0)) for k in solo]
        + [pl.BlockSpec((tt, j_block), lambda i: (block_of(i), j_cidx))] * njoint
        + [_full_spec(s) for s in dp_shapes],
        out_shape=[jax.ShapeDtypeStruct((rows, dt_entries[k][1]), f32) for k in solo]
        + [jax.ShapeDtypeStruct((rows, j_width), f32)] * njoint
        + [jax.ShapeDtypeStruct(s, f32) for s in dp_shapes],
        scratch_shapes=[pltpu.VMEM((HALO, w), f32) for w in carry_w],
        input_output_aliases={nin + npar + ncot: nsolo} if nbuf else {},
        compiler_params=pltpu.CompilerParams(dimension_semantics=("arbitrary",)),
        name=name,
    )(*ops, *params, *[c[0] for c in flat_cots], *([j_buf] if nbuf else []))
    res = list(res)
    return res[:nsolo], res[nsolo + njoint:], (res[nsolo] if njoint else None)


def _whole(a):
    return (a, a.shape[1], 0, None)


def _matmul(name, a, b, mode, out_dtype=f32, tm=1024, tn=1024, tk=1024, shards=1, job=None):
    S = shards
    if mode == "nn":
        M, K = a.shape
        w = b.shape[-1]
    elif mode == "nt":
        M = a.shape[0]
        if S > 1:
            _, N, w = b.shape
            K = S * w
        else:
            N, K = b.shape
            w = K
    else:
        K, M = a.shape
        w = b.shape[1] // S
    if mode != "nt":
        N = S * w
    tm = _pick(M, tm, LANES)
    if mode == "nt":
        tn = _pick(N, tn, LANES)
        tk = _pick(w, tk, LANES)
    else:
        tn = _pick(w, tn, LANES)
        tk = _pick(K, tk, LANES if mode == "nn" else 16)
    nk = K // tk
    nb = w // (tk if mode == "nt" else tn)
    if mode == "nn":
        a_spec = pl.BlockSpec((tm, tk), lambda i, j, k: (i, k))
        if S > 1:
            b_spec = pl.BlockSpec((1, tk, tn), lambda i, j, k: (j // nb, k, j % nb))
        else:
            b_spec = pl.BlockSpec((tk, tn), lambda i, j, k: (k, j))
        dims = (((1,), (0,)), ((), ()))
    elif mode == "nt":
        a_spec = pl.BlockSpec((tm, tk), lambda i, j, k: (i, k))
        if S > 1:
            b_spec = pl.BlockSpec((1, tn, tk), lambda i, j, k: (k // nb, j, k % nb))
        else:
            b_spec = pl.BlockSpec((tn, tk), lambda i, j, k: (j, k))
        dims = (((1,), (1,)), ((), ()))
    else:
        a_spec = pl.BlockSpec((tk, tm), lambda i, j, k: (k, i))
        b_spec = pl.BlockSpec((tk, tn), lambda i, j, k: (k, j))
        dims = (((0,), (0,)), ((), ()))
    if mode == "tn" and S > 1:
        o_spec = pl.BlockSpec((1, tm, tn), lambda i, j, k: (j // nb, i, j % nb))
        o_shape = (S, M, w)
    else:
        o_spec = pl.BlockSpec((tm, tn), lambda i, j, k: (i, j))
        o_shape = (M, N)
    b_lead = S > 1 and mode != "tn"
    o_lead = S > 1 and mode == "tn"

    j_ins, j_outs, j_sems = _job_parts(job)
    nji, njo = len(j_ins), len(j_outs)
    grid = (M // tm, N // tn, nk)

    def run_job(refs):
        step = (pl.program_id(0) * grid[1] + pl.program_id(1)) * grid[2] + pl.program_id(2)
        _job_steps(job, refs[2:2 + nji], refs[3 + nji:3 + nji + njo], refs[len(refs) - 2:], step,
                   grid[0] * grid[1] * grid[2] - 1)

    def body(*refs):
        a_ref, b_ref, o_ref, acc_ref = refs[0], refs[1], refs[2 + nji], refs[3 + nji + njo]
        run_job(refs)
        k = pl.program_id(2)

        @pl.when(k == 0)
        def _():
            acc_ref[...] = jnp.zeros_like(acc_ref)

        bv = b_ref[0] if b_lead else b_ref[...]
        acc_ref[...] += lax.dot_general(a_ref[...].astype(bf16), bv.astype(bf16), dims, preferred_element_type=f32)

        @pl.when(k == nk - 1)
        def _():
            if o_lead:
                o_ref[0] = acc_ref[...].astype(o_ref.dtype)
            else:
                o_ref[...] = acc_ref[...].astype(o_ref.dtype)

    def body_one_step(*refs):
        a_ref, b_ref, o_ref = refs[0], refs[1], refs[2 + nji]
        run_job(refs)
        bv = b_ref[0] if b_lead else b_ref[...]
        res = lax.dot_general(a_ref[...].astype(bf16), bv.astype(bf16), dims, preferred_element_type=f32)
        if o_lead:
            o_ref[0] = res.astype(o_ref.dtype)
        else:
            o_ref[...] = res.astype(o_ref.dtype)

    res = pl.pallas_call(
        body if nk > 1 else body_one_step, grid=grid,
        in_specs=[a_spec, b_spec] + [_ANY] * nji,
        out_specs=[o_spec] + [_ANY] * njo,
        out_shape=[jax.ShapeDtypeStruct(o_shape, out_dtype)] + j_outs,
        scratch_shapes=([pltpu.VMEM((tm, tn), f32)] if nk > 1 else []) + j_sems,
        compiler_params=pltpu.CompilerParams(
            dimension_semantics=("arbitrary",) * 3 if job else ("parallel", "parallel", "arbitrary")),
        name=name,
    )(a, b, *j_ins)
    return res if job else res[0]


def _make_bmm(precision):
    if precision is None:
        cast, kw = (lambda v: v.astype(bf16)), {}
    else:
        cast, kw = (lambda v: v), {"precision": precision}

    def nn(a, b):
        return jnp.einsum("hij,hjk->hik", cast(a), cast(b), preferred_element_type=f32, **kw)

    def nt(a, b):
        return jnp.einsum("hik,hjk->hij", cast(a), cast(b), preferred_element_type=f32, **kw)

    def tn(a, b):
        return jnp.einsum("hki,hkj->hij", cast(a), cast(b), preferred_element_type=f32, **kw)

    if precision is not None:
        return nn, nt, tn
    nn_v, nt_v, tn_v = jax.custom_vjp(nn), jax.custom_vjp(nt), jax.custom_vjp(tn)
    keep = lambda f: (lambda a, b: (f(a, b), (a, b)))
    nn_v.defvjp(keep(nn), lambda r, ct: (nt(ct, r[1]), tn(r[0], ct)))
    nt_v.defvjp(keep(nt), lambda r, ct: (nn(ct, r[1]), tn(ct, r[0])))
    tn_v.defvjp(keep(tn), lambda r, ct: (nt(r[1], ct), nn(r[0], ct)))
    return nn_v, nt_v, tn_v


_bmm, _bmm_nt, _bmm_tn = _make_bmm(None)
_bmm_exact = _make_bmm(lax.Precision.HIGH)[0]


def _masks(n):
    r = lax.broadcasted_iota(jnp.int32, (n, n), 0)
    c = lax.broadcasted_iota(jnp.int32, (n, n), 1)
    return (r >= c)[None], (r > c)[None], (r == c)[None]


_INV_BLOCK = 8


def _nilpotent_inverse(m, eye):
    p = eye + m
    for _ in range(2):
        m = _bmm(m, m)
        p = p + _bmm(p, m)
    return p


def _neumann_inverse_impl(m):
    n = m.shape[1]
    assert n == _INV_BLOCK * _INV_BLOCK
    r = lax.broadcasted_iota(jnp.int32, (n, n), 0)
    c = lax.broadcasted_iota(jnp.int32, (n, n), 1)
    eye = (r == c).astype(f32)[None]
    inside = jnp.where((r // _INV_BLOCK == c // _INV_BLOCK)[None], m, 0.0)
    d_inv = _nilpotent_inverse(inside, eye)
    return _bmm(_nilpotent_inverse(_bmm(d_inv, m - inside), eye), d_inv)


@jax.custom_vjp
def _neumann_inverse(m):
    return _neumann_inverse_impl(m)


def _neumann_inverse_fwd(m):
    p = _neumann_inverse_impl(m)
    return p, p


def _neumann_inverse_bwd(p, ct):
    return (_bmm_tn(p, _bmm_nt(ct, p)),)


_neumann_inverse.defvjp(_neumann_inverse_fwd, _neumann_inverse_bwd)


@jax.custom_vjp
def _given_inverse(m, p):
    return p


_given_inverse.defvjp(lambda m, p: (p, p), lambda p, ct: (_neumann_inverse_bwd(p, ct)[0], jnp.zeros_like(p)))


def _gdn_chunk(s, q, k, v, beta, gc, gr, gl, p=None):
    n = q.shape[1]
    causal, strict, _ = _masks(n)
    decay = jnp.where(causal, jnp.exp(jnp.where(causal, gc - gr, 0.0)), 0.0)
    kb = k * beta
    vb = v * beta
    lower = jnp.where(strict, _bmm_nt(kb, k) * decay, 0.0)
    t_mat = _neumann_inverse(-lower) if p is None else _given_inverse(-lower, p)
    egc = jnp.exp(gc)
    u = _bmm(t_mat, vb)
    w = _bmm(t_mat, kb * egc)
    attn = jnp.where(causal, _bmm_nt(q, k) * decay, 0.0)
    v_new = u - _bmm(w, s)
    o = _bmm(q * egc, s) + _bmm(attn, v_new)
    k_dec = k * jnp.exp(gl - gc)
    s_new = s * jnp.exp(gl) + _bmm_tn(k_dec, v_new)
    return s_new, o, t_mat


def _rwkv_chunk(s, r, lw, k, v, a, b, p=None):
    n = r.shape[1]
    causal, strict, _ = _masks(n)
    tri = jnp.broadcast_to(causal.astype(f32), (r.shape[0], n, n))
    lc = _bmm_exact(tri, lw)
    ein = jnp.exp(lc)
    eout = jnp.exp(-lc)
    a_t = a * jnp.exp(lc - lw)
    b_t = b * eout
    k_t = k * eout
    r_t = r * ein
    a_ab = jnp.where(strict, _bmm_nt(a_t, b_t), 0.0)
    a_ak = jnp.where(strict, _bmm_nt(a_t, k_t), 0.0)
    inv = _neumann_inverse(a_ab) if p is None else _given_inverse(a_ab, p)
    u = _bmm(inv, _bmm_nt(a_t, s) + _bmm(a_ak, v))
    y = (_bmm_nt(r_t, s) + _bmm(jnp.where(causal, _bmm_nt(r_t, b_t), 0.0), u)
         + _bmm(jnp.where(causal, _bmm_nt(r_t, k_t), 0.0), v))
    e_last = jnp.exp(jnp.sum(lw, axis=1, keepdims=True))
    s_new = s * e_last + _bmm_tn(u, b_t * e_last) + _bmm_tn(v, k_t * e_last)
    return s_new, y, inv


def _heads_in(ref):
    return jnp.stack([ref[:, h * HEAD_DIM:(h + 1) * HEAD_DIM] for h in range(HEADS)], axis=0)


def _heads_out(ref, val):
    for h in range(HEADS):
        ref[:, h * HEAD_DIM:(h + 1) * HEAD_DIM] = val[h]


def _gdn_scalars(bt, gt):
    n = bt.shape[0]
    gtt = gt.T
    hs = range(HEADS)
    return [jnp.stack([bt[:, h:h + 1] for h in hs], axis=0),
            jnp.stack([gt[:, HEADS + h:HEADS + h + 1] for h in hs], axis=0),
            jnp.stack([gtt[HEADS + h:HEADS + h + 1, :] for h in hs], axis=0),
            jnp.stack([gt[n - 1:n, HEADS + h:HEADS + h + 1] for h in hs], axis=0)]


def _gdn_scalars_back(dbeta, dgc, dgr, dgl):
    n = dbeta.shape[1]
    lane = lax.broadcasted_iota(jnp.int32, (n, LANES), 1)
    row = lax.broadcasted_iota(jnp.int32, (n, LANES), 0)
    sub = lax.broadcasted_iota(jnp.int32, (LANES, n), 0)
    db = jnp.zeros((n, LANES), f32)
    dg = jnp.zeros((n, LANES), f32)
    dgt = jnp.zeros((LANES, n), f32)
    for h in range(HEADS):
        db = jnp.where(lane == h, dbeta[h], db)
        dg = jnp.where(lane == HEADS + h, dgc[h] + jnp.where(row == n - 1, dgl[h], 0.0), dg)
        dgt = jnp.where(sub == HEADS + h, dgr[h], dgt)
    return [db, dg + dgt.T]


def _scan_spec(width, n, reverse):
    if reverse:
        return pl.BlockSpec((CHUNK, width), lambda i: (n - 1 - i, 0))
    return pl.BlockSpec((CHUNK, width), lambda i: (i, 0))


def _hist_spec(n, reverse):
    blk = (1, HEADS, HEAD_DIM, HEAD_DIM)
    if reverse:
        return pl.BlockSpec(blk, lambda i: (n - 1 - i, 0, 0, 0))
    return pl.BlockSpec(blk, lambda i: (i, 0, 0, 0))


def _job_parts(job):
    if job is None:
        return [], [], []
    sems = [pltpu.SemaphoreType.DMA((job["nsem"],)), pltpu.SemaphoreType.DMA((job["nsem"],))]
    return list(job["ins"]), list(job["out_shapes"]), sems


def _job_steps(job, in_refs, out_refs, sems, step, last):
    if job is None:
        return

    @pl.when(step == 0)
    def _():
        job["start"](in_refs, out_refs, *sems)

    @pl.when(step == last)
    def _():
        job["finish"](in_refs, out_refs, *sems)


def _scan_fwd(name, fn, rows_in, scal_in=(), job=None):
    t = rows_in[0].shape[0]
    n = t // CHUNK
    nr, ns = len(rows_in), len(scal_in)
    j_ins, j_outs, j_sems = _job_parts(job)
    nji, njo = len(j_ins), len(j_outs)

    def body(*refs):
        o_ref, sh_ref, ph_ref = refs[nr + ns + nji:nr + ns + nji + 3]
        s_scr = refs[nr + ns + nji + 3 + njo]
        _job_steps(job, refs[nr + ns:nr + ns + nji], refs[nr + ns + nji + 3:nr + ns + nji + 3 + njo],
                   refs[nr + ns + nji + 3 + njo + 1:], pl.program_id(0), n - 1)

        @pl.when(pl.program_id(0) == 0)
        def _():
            s_scr[...] = jnp.zeros_like(s_scr)

        s = s_scr[...]
        sh_ref[0] = s
        ins = [_heads_in(r) for r in refs[:nr]]
        if ns:
            ins += _gdn_scalars(*[r[...] for r in refs[nr:nr + ns]])
        s_new, o, p = fn(s, *ins)
        _heads_out(o_ref, o)
        ph_ref[0] = p
        s_scr[...] = s_new

    return pl.pallas_call(
        body, grid=(n,),
        in_specs=[_scan_spec(a.shape[1], n, False) for a in (*rows_in, *scal_in)] + [_ANY] * nji,
        out_specs=[_scan_spec(WIDTH, n, False), _hist_spec(n, False), _hist_spec(n, False)] + [_ANY] * njo,
        out_shape=[jax.ShapeDtypeStruct((t, WIDTH), f32)] + [jax.ShapeDtypeStruct((n, HEADS, HEAD_DIM, HEAD_DIM), f32)] * 2
        + j_outs,
        scratch_shapes=[pltpu.VMEM((HEADS, HEAD_DIM, HEAD_DIM), f32)] + j_sems,
        compiler_params=pltpu.CompilerParams(dimension_semantics=("arbitrary",)),
        name=name,
    )(*rows_in, *scal_in, *j_ins)


def _scan_bwd(name, fn, rows_in, scal_in, s_hist, p_hist, d_out, job=None):
    t = rows_in[0].shape[0]
    n = t // CHUNK
    nr, ns = len(rows_in), len(scal_in)
    j_ins, j_outs, j_sems = _job_parts(job)
    nji, njo = len(j_ins), len(j_outs)

    def body(*refs):
        sh_ref, ph_ref, do_ref = refs[nr + ns:nr + ns + 3]
        base = nr + ns + 3 + nji
        g_refs = refs[base:base + nr + ns]
        ds_scr = refs[base + nr + ns + njo]
        _job_steps(job, refs[nr + ns + 3:base], refs[base + nr + ns:base + nr + ns + njo],
                   refs[base + nr + ns + njo + 1:], pl.program_id(0), n - 1)

        @pl.when(pl.program_id(0) == 0)
        def _():
            ds_scr[...] = jnp.zeros_like(ds_scr)

        ins = [_heads_in(r) for r in refs[:nr]]
        if ns:
            ins += _gdn_scalars(*[r[...] for r in refs[nr:nr + ns]])
        p = ph_ref[0]
        _, vjp = jax.vjp(lambda s, *a: fn(s, *a, p=p)[:2], sh_ref[0], *ins)
        g = vjp((ds_scr[...], _heads_in(do_ref)))
        ds_scr[...] = g[0]
        for r, v in zip(g_refs[:nr], g[1:1 + nr]):
            _heads_out(r, v)
        if ns:
            for r, v in zip(g_refs[nr:], _gdn_scalars_back(*g[1 + nr:])):
                r[...] = v

    arrs = (*rows_in, *scal_in)
    return pl.pallas_call(
        body, grid=(n,),
        in_specs=[_scan_spec(a.shape[1], n, True) for a in arrs]
        + [_hist_spec(n, True), _hist_spec(n, True), _scan_spec(WIDTH, n, True)] + [_ANY] * nji,
        out_specs=[_scan_spec(a.shape[1], n, True) for a in arrs] + [_ANY] * njo,
        out_shape=[jax.ShapeDtypeStruct(a.shape, f32) for a in arrs] + j_outs,
        scratch_shapes=[pltpu.VMEM((HEADS, HEAD_DIM, HEAD_DIM), f32)] + j_sems,
        compiler_params=pltpu.CompilerParams(dimension_semantics=("arbitrary",)),
        name=name,
    )(*arrs, s_hist, p_hist, d_out, *j_ins)


def _fn_norm_mod(ps, ts):
    nw, shift, scale = ps
    (x,) = ts
    return [_rms(x, nw, NORM_EPS) * (1.0 + scale) + shift]


def _fn_resid_norm_mod(ps, ts):
    gate, nw, shift, scale = ps
    x, mo = ts
    x1 = x + gate * mo
    return [x1, _rms(x1, nw, NORM_EPS) * (1.0 + scale) + shift]


def _fn_gdn_pre(ps, ts):
    cw = ps[:12]
    alog, dtb, seg, tri = ps[12:]
    ba = ts[12]
    outs = []
    for part in range(3):
        x = ts[4 * part:4 * part + 4]
        w = cw[4 * part:4 * part + 4]
        conv = w[3] * x[0] + w[2] * x[1] + w[1] * x[2] + w[0] * x[3]
        u = _silu(conv)
        if part < 2:
            u = u * lax.rsqrt(_segsum(u * u, seg) + 1e-6)
            if part == 0:
                u = u * (HEAD_DIM ** -0.5)
        outs.append(u)
    beta = jax.nn.sigmoid(ba)
    g = -jnp.exp(alog) * _softplus(ba + dtb)
    gc = _xdot_l(tri, g)
    return outs + [beta, gc]


def _fn_gdn_post(ps, ts):
    ow, seg = ps
    o, z = ts
    ms = _segsum(o * o, seg) * (1.0 / HEAD_DIM)
    return [o * lax.rsqrt(ms + NORM_EPS) * ow * _silu(z)]


def _fn_rwkv_pre(ps, ts):
    mu_r, mu_k, mu_v, mu_l, mu_g, w0, w2p, a0, a2p, g2p, k_k, k_a, seg = ps
    r0, r1, k0, k1, v0, v1, l0, l1, g0, g1 = ts
    xr = r0 + (r1 - r0) * mu_r
    xk = k0 + (k1 - k0) * mu_k
    xv = v0 + (v1 - v0) * mu_v
    xl = l0 + (l1 - l0) * mu_l
    xg = g0 + (g1 - g0) * mu_g
    w = -_softplus(-(w0 + _bdot(jnp.tanh(xl), w2p))) - 0.5
    lw = -jnp.exp(w)
    a = jax.nn.sigmoid(a0 + _bdot(xl, a2p))
    g = _bdot(jax.nn.sigmoid(xg), g2p)
    kk = xk * k_k
    kk = kk * lax.rsqrt(_segsum(kk * kk, seg) + 1e-6)
    k2 = xk * (1.0 + (a - 1.0) * k_a)
    return [xr, lw, k2, xv, -kk, kk * a, g]


def _fn_rwkv_post(ps, ts):
    lw_, lb_, rk, seg = ps
    y, r, k2, v, g = ts
    inv = 1.0 / HEAD_DIM
    yc = y - _segsum(y, seg) * inv
    var = _segsum(yc * yc, seg) * inv
    yn = yc * lax.rsqrt(var + LNX_EPS) * lw_ + lb_
    bonus = _segsum(r * k2 * rk, seg) * v
    return [(yn + bonus) * g]


def _fn_merge(ps, ts):
    gla, glb, ya, yb = ts
    return [jax.nn.sigmoid(gla) * ya + jax.nn.sigmoid(glb) * yb]


def _fn_convglu(ps, ts):
    c0, c1, c2 = ps
    g0, g1, g2, up = ts
    return [_silu(c2 * g0 + c1 * g1 + c0 * g2) * up]


def _fn_add(ps, ts):
    acc = ts[0]
    for t in ts[1:]:
        acc = acc + t
    return [acc]


def _add_n(name, arrs, tt):
    return _stage_fwd(name, _fn_add, [_whole(a) for a in arrs], [], [(arrs[0].shape[1], f32)], tt)[0]


def _final_stage(x1, fo, tgt, gate2, nfw, tt):
    rows, d = x1.shape

    def loss_fn(gate, nw, xa, fa, tg):
        y = _rms(xa + gate * fa, nw, NORM_EPS)
        err = (y - tg) ** 2
        return 0.5 * jnp.sum(jnp.mean(err, axis=-1, keepdims=True), axis=0, keepdims=True)

    def body(x_ref, f_ref, t_ref, g_ref, w_ref, dx_ref, df_ref, dg_ref, dw_ref, l_ref):
        i = pl.program_id(0)
        args = (g_ref[...], w_ref[...], x_ref[...], f_ref[...])
        tg = t_ref[...]
        lv, vjp = jax.vjp(lambda g, w, xa, fa: loss_fn(g, w, xa, fa, tg), *args)
        dg, dw, dx, df = vjp(jnp.ones((1, 1), f32))
        dx_ref[...] = dx
        df_ref[...] = df

        @pl.when(i == 0)
        def _():
            dg_ref[...] = jnp.zeros_like(dg_ref)
            dw_ref[...] = jnp.zeros_like(dw_ref)
            l_ref[...] = jnp.zeros_like(l_ref)

        dg_ref[...] += dg
        dw_ref[...] += dw
        l_ref[...] += jnp.broadcast_to(lv, l_ref.shape)

    row = pl.BlockSpec((tt, d), lambda i: (i, 0))
    vec = pl.BlockSpec((1, d), lambda i: (0, 0))
    return pl.pallas_call(
        body, grid=(rows // tt,),
        in_specs=[row, row, row, vec, vec],
        out_specs=[row, row, vec, vec, pl.BlockSpec((1, LANES), lambda i: (0, 0))],
        out_shape=[jax.ShapeDtypeStruct((rows, d), f32)] * 2 + [jax.ShapeDtypeStruct((1, d), f32)] * 2
        + [jax.ShapeDtypeStruct((1, LANES), f32)],
        compiler_params=pltpu.CompilerParams(dimension_semantics=("arbitrary",)),
        name="loss_head",
    )(x1, fo, tgt, gate2, nfw)


def _ada_fwd(c_all, w_shard, b_cols):
    def body(c_ref, w_ref, b_ref, cond_ref, mod_ref):
        cond = _silu(c_ref[...])
        cond_ref[...] = cond
        mod_ref[...] = jnp.dot(cond.astype(bf16), w_ref[...].astype(bf16), preferred_element_type=f32) + b_ref[...]

    n = w_shard.shape[1]
    return pl.pallas_call(
        body, out_shape=[jax.ShapeDtypeStruct(c_all.shape, f32), jax.ShapeDtypeStruct((c_all.shape[0], n), f32)],
        name="ada_fwd",
    )(c_all, w_shard, b_cols)


def _adamw(name, w, g, m, v):
    rows, width = w.shape
    tt = _pick(rows, 128, 8)
    c1 = 1.0 - ADAM_B1 ** ADAM_STEP
    c2 = 1.0 - ADAM_B2 ** ADAM_STEP

    def body(w_ref, g_ref, m_ref, v_ref, d_ref, mo_ref, vo_ref):
        gg = g_ref[...]
        mn = ADAM_B1 * m_ref[...] + (1.0 - ADAM_B1) * gg
        vn = ADAM_B2 * v_ref[...] + (1.0 - ADAM_B2) * (gg * gg)
        m_hat = mn / c1
        v_hat = vn / c2
        d_ref[...] = -ADAM_LR * (m_hat / (jnp.sqrt(v_hat) + ADAM_EPS) + ADAM_WD * w_ref[...])
        mo_ref[...] = mn
        vo_ref[...] = vn

    spec = pl.BlockSpec((tt, width), lambda i: (i, 0))
    return pl.pallas_call(
        body, grid=(rows // tt,), in_specs=[spec] * 4, out_specs=[spec] * 3,
        out_shape=[jax.ShapeDtypeStruct((rows, width), f32)] * 3,
        compiler_params=pltpu.CompilerParams(dimension_semantics=("parallel",)),
        name=name,
    )(w, g, m, v)


def _place():
    return lax.axis_index("x"), lax.axis_index("y"), lax.axis_index("c")


def _ag8(name, blk):
    m, w = blk.shape

    def body(x_ref, out_ref, send_sems, recv_sems, local_sem):
        x, y, c = _place()
        me, sibling = (x, y, c), (x, y, 1 - c)
        chips = _other_chips(x, y)

        def slot(px, py, pc):
            return out_ref.at[4 * px + 2 * py + pc]

        def copy(k, block, to, src=None):
            return pltpu.make_async_remote_copy(src_ref=slot(*block) if src is None else src, dst_ref=slot(*block),
                                                send_sem=send_sems.at[k], recv_sem=recv_sems.at[k], device_id=to,
                                                device_id_type=MESH_ID)

        mine = pltpu.make_async_copy(x_ref, slot(*me), local_sem)
        mine.start()
        first = [copy(0, me, sibling, src=x_ref)] + [copy(1 + j, me, (*chip, c), src=x_ref) for j, chip in enumerate(chips)]
        for cp in first:
            cp.start()
        passed = [copy(4 + j, (*chip, c), sibling) for j, chip in enumerate(chips)]
        for j, chip in enumerate(chips):
            copy(1 + j, (*chip, c), me).wait_recv()
            passed[j].start()
        copy(0, sibling, me).wait_recv()
        for j, chip in enumerate(chips):
            copy(4 + j, (*chip, 1 - c), me).wait_recv()
        for cp in first + passed:
            cp.wait_send()
        mine.wait()

    return pl.pallas_call(
        body, out_shape=jax.ShapeDtypeStruct((8, m, w), blk.dtype),
        in_specs=[pl.BlockSpec(memory_space=pltpu.VMEM)], out_specs=pl.BlockSpec(memory_space=pltpu.VMEM),
        scratch_shapes=[pltpu.SemaphoreType.DMA((7,)), pltpu.SemaphoreType.DMA((7,)), pltpu.SemaphoreType.DMA],
        name=name,
    )(blk)


def _other_chips(x, y):
    return [(1 - x, y), (x, 1 - y), (1 - x, 1 - y)]


_ANY = pl.BlockSpec(memory_space=pl.ANY)


def _rcopy(src, dst, send_sems, recv_sems, k, dev):
    return pltpu.make_async_remote_copy(src_ref=src, dst_ref=dst, send_sem=send_sems.at[k], recv_sem=recv_sems.at[k],
                                        device_id=dev, device_id_type=MESH_ID)


def _run_job(name, job):
    j_ins, j_outs, j_sems = _job_parts(job)
    n = len(j_ins)

    def body(*refs):
        job["start"](refs[:n], refs[n:n + len(j_outs)], *refs[n + len(j_outs):])
        job["finish"](refs[:n], refs[n:n + len(j_outs)], *refs[n + len(j_outs):])

    return pl.pallas_call(body, out_shape=j_outs, in_specs=[_ANY] * n, out_specs=[_ANY] * len(j_outs),
                          scratch_shapes=j_sems, name=name)(*j_ins)


def _ag4_job(ws):
    n = len(ws)

    def plan(w_refs, out_refs, send_sems, recv_sems):
        x, y, c = _place()
        chip = 2 * x + y
        sibling = (x, y, 1 - c)
        chips = _other_chips(x, y)
        mine = [pl.ds(c * (w.shape[0] // 2), w.shape[0] // 2) for w in ws]
        other = [pl.ds((1 - c) * (w.shape[0] // 2), w.shape[0] // 2) for w in ws]
        rc = lambda src, dst, k, dev: _rcopy(src, dst, send_sems, recv_sems, k, dev)
        first = [rc(w_refs[t].at[mine[t]], out_refs[t].at[chip, mine[t]], 7 * t + k, (px, py, c))
                 for t in range(n) for k, (px, py) in enumerate(chips)]
        own = [rc(w_refs[t], out_refs[t].at[chip], 7 * t + 6, sibling) for t in range(n)]
        landed = [rc(out_refs[t].at[2 * px + py, mine[t]], out_refs[t].at[2 * px + py, mine[t]], 7 * t + k, (px, py, c))
                  for t in range(n) for k, (px, py) in enumerate(chips)]
        forward = [rc(out_refs[t].at[2 * px + py, mine[t]], out_refs[t].at[2 * px + py, mine[t]], 7 * t + 3 + k, sibling)
                   for t in range(n) for k, (px, py) in enumerate(chips)]
        handed = [rc(out_refs[t].at[2 * px + py, other[t]], out_refs[t].at[2 * px + py, other[t]], 7 * t + 3 + k, sibling)
                  for t in range(n) for k, (px, py) in enumerate(chips)]
        return first, own, landed, forward, handed

    def start(*refs):
        first, own, _, _, _ = plan(*refs)
        for cp in first + own:
            cp.start()

    def finish(*refs):
        first, own, landed, forward, handed = plan(*refs)
        for arrived, fw in zip(landed, forward):
            arrived.wait_recv()
            fw.start()
        for cp in handed + own:
            cp.wait_recv()
        for cp in first + own + forward:
            cp.wait_send()

    return dict(ins=ws, out_shapes=[jax.ShapeDtypeStruct((4,) + w.shape, w.dtype) for w in ws], nsem=7 * n,
                start=start, finish=finish)


def _rs_sibling_swap(gs, tag):
    n = len(gs)

    def body(*refs):
        g_refs, recv_refs = refs[:n], refs[n:2 * n]
        send_sems, recv_sems = refs[2 * n:]
        x, y, c = _place()
        copies = []
        for t in range(n):
            rh = gs[t].shape[1] // 2
            for s_ in range(4):
                cp = _rcopy(g_refs[t].at[s_, pl.ds((1 - c) * rh, rh)], recv_refs[t].at[s_], send_sems, recv_sems,
                            4 * t + s_, (x, y, 1 - c))
                cp.start()
                copies.append(cp)
        for cp in copies:
            cp.wait_recv()
        for cp in copies:
            cp.wait_send()

    return pl.pallas_call(
        body, out_shape=[jax.ShapeDtypeStruct((4, g.shape[1] // 2, g.shape[2]), g.dtype) for g in gs],
        in_specs=[_ANY] * n, out_specs=[_ANY] * n,
        scratch_shapes=[pltpu.SemaphoreType.DMA((4 * n,)), pltpu.SemaphoreType.DMA((4 * n,))],
        name="grads_sibling_swap_" + tag,
    )(*gs)


def _chip_exchange_job(ps):
    n = len(ps)

    def plan(p_refs, recv_refs, send_sems, recv_sems):
        x, y, c = _place()
        return [_rcopy(p_refs[t].at[2 * px + py], recv_refs[t].at[k], send_sems, recv_sems, 3 * t + k, (px, py, c))
                for t in range(n) for k, (px, py) in enumerate(_other_chips(x, y))]

    def start(*refs):
        for cp in plan(*refs):
            cp.start()

    def finish(*refs):
        copies = plan(*refs)
        for cp in copies:
            cp.wait_recv()
        for cp in copies:
            cp.wait_send()

    return dict(ins=ps, out_shapes=[jax.ShapeDtypeStruct((3,) + p.shape[1:], p.dtype) for p in ps], nsem=3 * n,
                start=start, finish=finish)


_JOIN_PIECES = 4


def _rs_sibling_join(qs):
    n = len(qs)
    npc = _JOIN_PIECES

    def body(*refs):
        q_refs, out_refs = refs[:n], refs[n:2 * n]
        send_sems, recv_sems = refs[2 * n:]
        x, y, c = _place()
        copies = []
        for t in range(n):
            rh = qs[t].shape[0] // 2
            pr = rh // npc
            for i in range(npc):
                rows = pl.ds(c * rh + i * pr, pr)
                cp = _rcopy(q_refs[t].at[rows], out_refs[t].at[rows], send_sems, recv_sems, npc * t + i,
                            (x, y, 1 - c))
                cp.start()
                copies.append(cp)
        for t in range(n):
            rh = qs[t].shape[0] // 2
            pr = rh // npc
            for i in range(npc):
                rows = pl.ds((1 - c) * rh + i * pr, pr)
                _rcopy(q_refs[t].at[rows], out_refs[t].at[rows], send_sems, recv_sems, npc * t + i,
                       (x, y, 1 - c)).wait_recv()
        for cp in copies:
            cp.wait_send()

    return pl.pallas_call(
        body, out_shape=[jax.ShapeDtypeStruct(q.shape, q.dtype) for q in qs],
        in_specs=[_ANY] * n, out_specs=[_ANY] * n, input_output_aliases={t: t for t in range(n)},
        scratch_shapes=[pltpu.SemaphoreType.DMA((npc * n,)), pltpu.SemaphoreType.DMA((npc * n,))],
        name="grads_sibling_join",
    )(*qs)


def _add_half(name, g, recv, ci):
    S, r, w = g.shape
    rh = r // 2
    tt = _pick(rh, 256, 16)
    nb = rh // tt

    def body(c_ref, a_ref, b_ref, o_ref, ob_ref):
        v = a_ref[...] + b_ref[...]
        o_ref[...] = v
        ob_ref[...] = v.astype(bf16)

    blk = pl.BlockSpec((1, tt, w), lambda s_, i, cr: (s_, i, 0))
    grid_spec = pltpu.PrefetchScalarGridSpec(
        num_scalar_prefetch=1, grid=(S, nb),
        in_specs=[pl.BlockSpec((1, tt, w), lambda s_, i, cr: (s_, cr[0] * nb + i, 0)), blk],
        out_specs=[blk, blk])
    return pl.pallas_call(body, grid_spec=grid_spec,
                          out_shape=[jax.ShapeDtypeStruct((S, rh, w), f32), jax.ShapeDtypeStruct((S, rh, w), bf16)],
                          name=name)(ci.reshape(1).astype(jnp.int32), g, recv)


def _sum_chip(name, pair, others, chip, ci):
    _, rh, w = pair.shape
    tt = _pick(rh, 128, 16)
    nb = rh // tt

    def body(chip_ref, core_ref, a_ref, b_ref, o_ref):
        o_ref[...] = ((a_ref[0] + b_ref[0].astype(f32)) + b_ref[1].astype(f32)) + b_ref[2].astype(f32)

    grid_spec = pltpu.PrefetchScalarGridSpec(
        num_scalar_prefetch=2, grid=(nb,),
        in_specs=[pl.BlockSpec((1, tt, w), lambda i, ch, co: (ch[0], i, 0)),
                  pl.BlockSpec((3, tt, w), lambda i, ch, co: (0, i, 0))],
        out_specs=pl.BlockSpec((tt, w), lambda i, ch, co: (co[0] * nb + i, 0)))
    return pl.pallas_call(body, grid_spec=grid_spec, out_shape=jax.ShapeDtypeStruct((2 * rh, w), f32),
                          name=name)(chip.reshape(1).astype(jnp.int32), ci.reshape(1).astype(jnp.int32), pair, others)


def _sum_devices(gathered, head_row):
    _, rows, width = gathered.shape

    def body(g_ref, out_ref, head_ref):
        acc = g_ref[0]
        for d in range(1, 8):
            acc = acc + g_ref[d]
        out_ref[...] = acc
        row = acc[head_row:head_row + 1, :]
        hs = row[:, 0:HEAD_DIM]
        for h in range(1, HEADS):
            hs = hs + row[:, h * HEAD_DIM:(h + 1) * HEAD_DIM]
        head_ref[...] = jnp.zeros_like(head_ref)
        head_ref[0:1, 0:HEAD_DIM] = hs

    return pl.pallas_call(
        body, out_shape=[jax.ShapeDtypeStruct((rows, width), f32), jax.ShapeDtypeStruct((8, LANES), f32)],
        name="small_grads_sum",
    )(gathered)


def _pack(arrs, rows_mult, dtype):
    flat = jnp.concatenate([a.reshape(-1).astype(dtype) for a in arrs])
    per = PACK_W * rows_mult
    total = -(-flat.shape[0] // per) * per
    return jnp.pad(flat, (0, total - flat.shape[0])).reshape(total // PACK_W, PACK_W)


def _unpack(buf, shapes):
    flat = buf.reshape(-1)
    out, off = [], 0
    for s in shapes:
        n = int(np.prod(s))
        out.append(flat[off:off + n].reshape(s))
        off += n
    return out


_BIG = ["w_in", "w_branch_gdn", "w_branch_rwkv", "w_out", "w_ffn_in", "w_ffn_out"]
_MID = ["conv_gdn", "conv_ffn", "w2", "a2", "g2"]
_SMALL =["b_ada", "norm1_w", "a_log", "dt_bias", "onorm_gdn", "mu_rwkv", "w0", "a0", "k_k", "k_a", "r_k", "lnx_w",
          "lnx_b", "norm2_w", "norm_f_w"]
_ORDER = ["w_ada", "b_ada", "norm1_w", "w_in", "conv_gdn", "a_log", "dt_bias", "onorm_gdn", "w_branch_gdn", "mu_rwkv",
          "w0", "w2", "a0", "a2", "g2", "k_k", "k_a", "r_k", "lnx_w", "lnx_b", "w_branch_rwkv", "w_out", "norm2_w",
          "w_ffn_in", "conv_ffn", "w_ffn_out", "norm_f_w"]


_WIN_SEGMENTS = [(0, 1536, 0), (2064, 3600, 1536), (1536, 2048, 3072), (3600, 3728, 3584), (2048, 2064, 3712),
                 (3728, 3888, 3840), (3888, 5936, 4096)]
_WIN_PADDED = 6144


def _win_pad(shards):
    n = shards.shape[2]
    parts, at = [], 0
    for lo, hi, dst in _WIN_SEGMENTS:
        if dst > at:
            parts.append(jnp.zeros((shards.shape[1], dst - at), shards.dtype))
        c = lo
        while c < hi:
            j = c // n
            e = min(hi, (j + 1) * n)
            parts.append(shards[j][:, c - j * n:e - j * n])
            c = e
        at = dst + hi - lo
    if at < _WIN_PADDED:
        parts.append(jnp.zeros((shards.shape[1], _WIN_PADDED - at), shards.dtype))
    return jnp.concatenate(parts, axis=1)


def _win_unpad_shards(g, n):
    shards = []
    for j in range(4):
        parts = []
        for lo, hi, dst in sorted(_WIN_SEGMENTS):
            a, b = max(lo, j * n), min(hi, (j + 1) * n)
            if a < b:
                parts.append(g[:, dst + a - lo:dst + b - lo])
        shards.append(jnp.concatenate(parts, axis=1))
    return jnp.stack(shards)


def kernel(x, c, w_ada, b_ada, norm1_w, w_in, conv_gdn, a_log, dt_bias, onorm_gdn, w_branch_gdn, mu_rwkv, w0, w2, a0, a2, g2, k_k, k_a, r_k, lnx_w, lnx_b, w_branch_rwkv, w_out, norm2_w, w_ffn_in, conv_ffn, w_ffn_out, norm_f_w, loss_target, m_w_ada, m_b_ada, m_norm1_w, m_w_in, m_conv_gdn, m_a_log, m_dt_bias, m_onorm_gdn, m_w_branch_gdn, m_mu_rwkv, m_w0, m_w2, m_a0, m_a2, m_g2, m_k_k, m_k_a, m_r_k, m_lnx_w, m_lnx_b, m_w_branch_rwkv, m_w_out, m_norm2_w, m_w_ffn_in, m_conv_ffn, m_w_ffn_out, m_norm_f_w, v_w_ada, v_b_ada, v_norm1_w, v_w_in, v_conv_gdn, v_a_log, v_dt_bias, v_onorm_gdn, v_w_branch_gdn, v_mu_rwkv, v_w0, v_w2, v_a0, v_a2, v_g2, v_k_k, v_k_a, v_r_k, v_lnx_w, v_lnx_b, v_w_branch_rwkv, v_w_out, v_norm2_w, v_w_ffn_in, v_conv_ffn, v_w_ffn_out, v_norm_f_w):
    args = dict(locals())
    W = {n: args[n] for n in _ORDER}
    Mo = {n: args["m_" + n] for n in _ORDER}
    Vo = {n: args["v_" + n] for n in _ORDER}
    shapes = {n: W[n].shape for n in _ORDER}
    sq = lambda a: a.reshape(a.shape[-2:]) if a.ndim == 3 else a.reshape(1, -1)
    row = lambda a: a.reshape(1, -1)

    xi, yi, ci = lax.axis_index("x"), lax.axis_index("y"), lax.axis_index("c")
    dev = 4 * xi + 2 * yi + ci
    chip = 2 * xi + yi

    x2 = x[0]
    tgt = loss_target[0]
    T, D = x2.shape
    N = T // CHUNK
    tt_l = _pick(T, 256, CHUNK)
    tt_h = _pick(T, 128, CHUNK)

    mid_shapes = [shapes[n][1:] for n in _MID]
    small_blk = _pack([c] + [W[n] for n in _MID], 8, f32)
    small_all = _ag8("gather_c_mid", small_blk)
    c_all = small_all[:, 0, :]
    per_chip = small_all[0::2].reshape(4, -1)[:, D:]
    mid = [dict(zip(_MID, _unpack(per_chip[j], mid_shapes))) for j in range(4)]
    catm = lambda n: jnp.concatenate([mid[j][n] for j in range(4)], axis=1)
    conv_gdn_f, conv_ffn_f = catm("conv_gdn"), catm("conv_ffn")
    w2f, a2f, g2f = catm("w2"), catm("a2"), catm("g2")

    (win_s,) = _run_job("w_in_all_gather", _ag4_job([sq(W["w_in"]).astype(bf16)]))
    later_weights = _ag4_job([sq(W[n]).astype(bf16) for n in _BIG[1:]])
    win_p = _win_pad(win_s)
    zpad = lambda a, top, bot: jnp.pad(a, ((top, bot), (0, 0)))
    w2p, a2p, g2p = zpad(w2f, 0, 64), zpad(a2f, 64, 0), zpad(g2f, 0, 96)

    ncol = shapes["w_ada"][2]
    b_cols = lax.dynamic_slice(sq(W["b_ada"]), (0, chip * ncol), (1, ncol))
    cond16, mod_cols = _ada_fwd(jnp.pad(c_all, ((0, 8), (0, 0))), sq(W["w_ada"]), b_cols)
    mod_all = _ag8("gather_mod", mod_cols[:8])
    mod_mine = lax.dynamic_slice(mod_all[0::2], (0, dev, 0), (4, 1, ncol)).reshape(1, 4 * ncol)
    shift1, scale1, gate1, shift2, scale2, gate2 = [mod_mine[:, i * D:(i + 1) * D] for i in range(6)]

    seg = _seg_matrix(WIDTH, HEAD_DIM)
    norm1 = [sq(W["norm1_w"]), shift1, scale1]
    h1 = _stage_fwd("norm_mod1", _fn_norm_mod, [_whole(x2)], norm1, [(D, bf16)], tt_l)[0]
    p = _matmul("in_proj", h1, win_p, "nn")

    cgq = [row(conv_gdn_f[j, part * WIDTH:(part + 1) * WIDTH]) for part in range(3) for j in range(4)]
    lane_pad = lambda a: jnp.pad(row(a), ((0, 0), (8, LANES - 16)))
    gdn_pre_ps = cgq + [lane_pad(W["a_log"]), lane_pad(W["dt_bias"]), seg, _chunk_tri(tt_h, CHUNK)]
    gdn_pre_ts = [(p, WIDTH, part, (0, 1, 2, 3)) for part in range(3)] + [(p, LANES, 29, None)]
    q_, k_, v_, beta_t, gc_t = _stage_fwd("gdn_pre", _fn_gdn_pre, gdn_pre_ts, gdn_pre_ps,
                                          [(WIDTH, f32)] * 3 + [(LANES, f32)] * 2, tt_h)
    o_, gdn_hist, gdn_inv, wbg_s, wbr_s, wout_s, wfi_s, wfo_s = _scan_fwd(
        "gdn_scan", _gdn_chunk, [q_, k_, v_], [beta_t, gc_t], job=later_weights)
    wout_f = wout_s.reshape(D, D)
    wfo = wfo_s.reshape(D_FF, D)
    ow512 = jnp.tile(row(W["onorm_gdn"]), (1, HEADS))
    gdn_post_ts = [_whole(o_), (p, WIDTH, 6, None)]
    ya = _stage_fwd("gdn_post", _fn_gdn_post, gdn_post_ts, [ow512, seg], [(WIDTH, bf16)], tt_l)[0]

    mu = sq(W["mu_rwkv"])
    rw_ps = [mu[:, 0:512], mu[:, 512:1024], mu[:, 1024:1536], mu[:, 1536:1664], jnp.pad(mu[:, 1664:1824], ((0, 0), (0, 96))),
             sq(W["w0"]), w2p, sq(W["a0"]), a2p, g2p, sq(W["k_k"]), sq(W["k_a"]), seg]
    rw_ts = [(p, WIDTH, 3, (0, 1)), (p, WIDTH, 4, (0, 1)), (p, WIDTH, 5, (0, 1)), (p, LANES, 28, (0, 1)),
             (p, 256, 15, (0, 1))]
    rw_out = _stage_fwd("rwkv_pre", _fn_rwkv_pre, rw_ts, rw_ps, [(WIDTH, f32)] * 7, tt_h)
    r_, lw_, k2_, vv_, na_, b_, g_ = rw_out
    rw_ins = [r_, lw_, k2_, vv_, na_, b_]
    y_, rw_hist, rw_inv = _scan_fwd("rwkv_scan", _rwkv_chunk, rw_ins)
    rwp_ps = [sq(W["lnx_w"]), sq(W["lnx_b"]), row(W["r_k"]), seg]
    rwp_ts = [_whole(y_), _whole(r_), _whole(k2_), _whole(vv_), _whole(g_)]
    yb = _stage_fwd("rwkv_post", _fn_rwkv_post, rwp_ts, rwp_ps, [(WIDTH, bf16)], tt_l)[0]

    big_a = _matmul("branch_gdn", ya, wbg_s, "nn", shards=4)
    big_b = _matmul("branch_rwkv", yb, wbr_s, "nn", shards=4)
    merge_ts = [(p, D, 4, None), (p, D, 5, None), _whole(big_a), _whole(big_b)]
    merged = _stage_fwd("merge", _fn_merge, merge_ts, [], [(D, bf16)], tt_l)[0]
    mo = _matmul("out_proj", merged, wout_f, "nn")
    norm2 = [gate1, sq(W["norm2_w"]), shift2, scale2]
    x1, h2 = _stage_fwd("resid_norm_mod2", _fn_resid_norm_mod, [_whole(x2), _whole(mo)], norm2, [(D, f32), (D, bf16)], tt_l)
    f = _matmul("ffn_in", h2, wfi_s, "nn", shards=4, tm=512, tn=1408)
    cg_ps = [row(conv_ffn_f[j]) for j in range(3)]
    cg_ts = [(f, D_FF, 0, (0, 1, 2)), (f, D_FF, 1, None)]
    act = _stage_fwd("convglu", _fn_convglu, cg_ts, cg_ps, [(D_FF, bf16)], tt_h)[0]
    fo = _matmul("ffn_out", act, wfo, "nn", tk=1408)

    dx1_a, dfo, dgate2, dnormf, loss_part = _final_stage(x1, fo, tgt, gate2, row(W["norm_f_w"]), tt_l)

    dact = _matmul("d_act", dfo, wfo, "nt", tm=512, tn=1408)
    g_wfo = _matmul("g_ffn_out", act, dfo, "tn", tm=1408)
    _, dcf, df = _stage_bwd("convglu_bwd", _fn_convglu, cg_ts, cg_ps, [[_whole(dact)]], tt_h, [True] * 2, [True] * 3,
                            joint=([0, 1], 2 * D_FF, 0, None))
    dh2 = _matmul("d_h2", df, wfi_s, "nt", shards=4, tm=512, tk=1408)
    g_wfi = _matmul("g_ffn_in", h2, df, "tn", shards=4, tn=1408)
    (dx_a, dmo), (dgate1, dnorm2, dshift2, dscale2), _ = _stage_bwd(
        "resid_norm_mod2_bwd", _fn_resid_norm_mod, [_whole(x2), _whole(mo)], norm2,
        [[_whole(dx1_a)], [_whole(dh2)]], tt_l, [True, True], [True] * 4)
    dmerged = _matmul("d_merged", dmo, wout_f, "nt")
    g_wout = _matmul("g_out_proj", merged, dmo, "tn")
    (dbig_a, dbig_b), _, dp = _stage_bwd("merge_bwd", _fn_merge, merge_ts, [], [[_whole(dmerged)]], tt_l, [True] * 4, [],
                                         joint=([0, 1], p.shape[1], 2, None))
    dya = _matmul("d_ya", dbig_a, wbg_s, "nt", shards=4)
    g_wbg = _matmul("g_branch_gdn", ya, dbig_a, "tn", shards=4)
    dyb = _matmul("d_yb", dbig_b, wbr_s, "nt", shards=4)
    g_wbr = _matmul("g_branch_rwkv", yb, dbig_b, "tn", shards=4)

    (dy_, dr_p, dk2_p, dv_p, dg_p), (dlnxw, dlnxb, drk), _ = _stage_bwd(
        "rwkv_post_bwd", _fn_rwkv_post, rwp_ts, rwp_ps, [[_whole(dyb)]], tt_l, [True] * 5, [True, True, True, False])
    gs_a = [g_wbg, g_wbr, g_wout.reshape(4, D // 4, D), g_wfi, g_wfo.reshape(4, D_FF // 4, D)]
    pairs_a = [_add_half("grads_pair_sum%d" % (t + 1), g, r_, ci) for t, (g, r_) in enumerate(zip(gs_a, _rs_sibling_swap(gs_a, "a")))]
    dr_c, dlw_c, dk2_c, dv_c, dna_c, db_c, *others_a = _scan_bwd(
        "rwkv_scan_bwd", _rwkv_chunk, rw_ins, [], rw_hist, rw_inv, dy_, job=_chip_exchange_job([pb for _, pb in pairs_a]))
    rw_cots = [[_whole(dr_p), _whole(dr_c)], [_whole(dlw_c)], [_whole(dk2_p), _whole(dk2_c)],
               [_whole(dv_p), _whole(dv_c)], [_whole(dna_c)], [_whole(db_c)], [_whole(dg_p)]]
    (dl_, dg_), rw_dp, dp = _stage_bwd("rwkv_pre_bwd", _fn_rwkv_pre, rw_ts, rw_ps, rw_cots, tt_h, [True] * 5,
                                       [True] * 12 + [False], joint=([0, 1, 2], p.shape[1], 1, dp))
    dmu_r, dmu_k, dmu_v, dmu_l, dmu_g, dw0, dw2p, da0, da2p, dg2p, dkk, dka = rw_dp

    (do_,), (dow512,), dp = _stage_bwd("gdn_post_bwd", _fn_gdn_post, gdn_post_ts, [ow512, seg], [[_whole(dya)]], tt_l,
                                       [True, True], [True, False], joint=([1], p.shape[1], 6, dp))
    d_gdn = _scan_bwd("gdn_scan_bwd", _gdn_chunk, [q_, k_, v_], [beta_t, gc_t], gdn_hist, gdn_inv, do_)
    gdn_cots = [[_whole(a)] for a in d_gdn]
    (dba,), gdn_dp, dp = _stage_bwd("gdn_pre_bwd", _fn_gdn_pre, gdn_pre_ts, gdn_pre_ps, gdn_cots, tt_h, [True] * 4,
                                    [True] * 14 + [False, False], joint=([0, 1, 2], p.shape[1], 0, dp))
    dp = lax.dynamic_update_slice(dp, jnp.concatenate([dl_, dba, dg_], axis=1), (0, 3584))
    g_win_s = _win_unpad_shards(_matmul("g_in_proj", h1, dp, "tn"), shapes["w_in"][2])
    pair_win = _add_half("grads_pair_sum0", g_win_s, _rs_sibling_swap([g_win_s], "b")[0], ci)
    dh1, others_win = _matmul("d_h1", dp, win_p, "nt", job=_chip_exchange_job([pair_win[1]]))
    (dx_b,), (dnorm1, dshift1, dscale1), _ = _stage_bwd("norm_mod1_bwd", _fn_norm_mod, [_whole(x2)], norm1,
                                                        [[_whole(dh1)]], tt_l, [True], [True] * 3)
    grad_x = _add_n("grad_x", [dx_a, dx_b], tt_l)

    dmod = jnp.concatenate([dshift1, dscale1, dgate1, dshift2, dscale2, dgate2], axis=1)
    g_conv_gdn = jnp.concatenate([jnp.concatenate([gdn_dp[4 * part + j] for part in range(3)], axis=1) for j in range(4)], axis=0)
    g_conv_ffn = jnp.concatenate(dcf, axis=0)
    g_mu = jnp.concatenate([dmu_r, dmu_k, dmu_v, dmu_l, dmu_g[:, :160]], axis=1)
    small_parts = {"b_ada": dmod, "norm1_w": dnorm1, "a_log": gdn_dp[12][:, 8:16], "dt_bias": gdn_dp[13][:, 8:16],
                   "mu_rwkv": g_mu, "w0": dw0, "a0": da0, "k_k": dkk, "k_a": dka, "r_k": drk, "lnx_w": dlnxw,
                   "lnx_b": dlnxb, "norm2_w": dnorm2, "norm_f_w": dnormf}
    small_names = [n for n in _SMALL if n != "onorm_gdn"]
    mid_full = [g_conv_gdn, g_conv_ffn, dw2p[0:64], da2p[64:128], dg2p[0:160]]
    body_rows = _pack([small_parts[n] for n in small_names] + [loss_part[:, 0:1]] + mid_full, 1, f32)
    head_row = body_rows.shape[0]
    small_g = jnp.concatenate([body_rows, jnp.pad(dow512, ((0, 0), (0, PACK_W - WIDTH)))], axis=0)
    small_g = jnp.pad(small_g, ((0, -small_g.shape[0] % 8), (0, 0)))
    small_all_g = _ag8("gather_small_grads", small_g)
    small_sum, head_sum = _sum_devices(small_all_g, head_row)
    small_shapes = [shapes[n][1:] if n != "norm_f_w" else shapes[n] for n in small_names]
    un = _unpack(small_sum, small_shapes + [(1,)] + [g.shape for g in mid_full])
    small_grads = dict(zip(small_names, un))
    loss = un[len(small_names)].reshape(())
    small_grads["onorm_gdn"] = head_sum[0, 0:HEAD_DIM]
    for n, g in zip(_MID, un[len(small_names) + 1:]):
        wcols = shapes[n][2]
        small_grads[n] = lax.dynamic_slice(g, (0, chip * wcols), (g.shape[0], wcols))

    dmod_all = small_all_g[:, 0:6, :].reshape(8, 6 * PACK_W)
    dmod_cols = lax.dynamic_slice(dmod_all, (0, chip * ncol), (8, ncol))
    g_wada = _matmul("g_w_ada", cond16, jnp.pad(dmod_cols, ((0, 8), (0, 0))), "tn")

    pairs = [pair_win] + pairs_a
    others = [others_win] + others_a
    halves = [_sum_chip("grads_chip_sum%d" % t, pf, o_, chip, ci) for t, ((pf, _), o_) in enumerate(zip(pairs, others))]
    big_grads = dict(zip(_BIG, _rs_sibling_join(halves)))

    res = {tag: {} for tag in ("grad", "delta", "new_m", "new_v")}

    def put(n, g, d, m_, v_):
        for tag, val in zip(("grad", "delta", "new_m", "new_v"), (g, d, m_, v_)):
            res[tag][n] = val.reshape(shapes[n])

    for n in _BIG:
        put(n, big_grads[n], *_adamw("adamw_" + n, sq(W[n]), big_grads[n], sq(Mo[n]), sq(Vo[n])))
    put("w_ada", g_wada, *_adamw("adamw_w_ada", sq(W["w_ada"]), g_wada, sq(Mo["w_ada"]), sq(Vo["w_ada"])))
    rest = _SMALL + _MID
    pk = lambda d: _pack([d[n] for n in rest], 8, f32)
    sg = pk(small_grads)
    sm = _adamw("adamw_small", pk(W), sg, pk(Mo), pk(Vo))
    for tag, buf in zip(("grad", "delta", "new_m", "new_v"), (sg,) + tuple(sm)):
        res[tag].update(zip(rest, _unpack(buf, [shapes[n] for n in rest])))
    outs = [loss, grad_x.reshape(x.shape)]
    for tag in ("grad", "delta", "new_m", "new_v"):
        outs += [res[tag][n] for n in _ORDER]
    return tuple(outs)
```

```python
import functools
import math

import numpy as np
import jax
import jax.numpy as jnp
from jax import lax
from jax.experimental import pallas as pl
from jax.experimental.pallas import tpu as pltpu

f32 = jnp.float32
bf16 = jnp.bfloat16

LANES = 128
HEADS = 8
HEAD_DIM = 64
WIDTH = HEADS * HEAD_DIM
CHUNK = 64
D_FF = 2816
NORM_EPS = 1e-6
LNX_EPS = 64e-5
PACK_W = 1024
MESH_ID = pl.DeviceIdType.MESH

ADAM_LR, ADAM_B1, ADAM_B2, ADAM_EPS, ADAM_WD, ADAM_STEP = 0.001, 0.9, 0.999, 1e-08, 0.01, 10


def _pick(n, target, mult):
    if n <= target:
        return n
    best = None
    for t in range(mult, target + 1, mult):
        if n % t == 0:
            best = t
    assert best is not None, (n, target, mult)
    return best


def _split_bf16(x, n):
    parts, r = [], x
    for i in range(n):
        p = r.astype(bf16)
        parts.append(p)
        if i + 1 < n:
            r = r - p.astype(f32)
    return parts


def _xdot_r_impl(x, m, n, dims):
    acc = None
    for p in _split_bf16(x, n):
        t = lax.dot_general(p, m, dims, preferred_element_type=f32)
        acc = t if acc is None else acc + t
    return acc


def _make_xdot_r(n):
    nn = (((1,), (0,)), ((), ()))
    nt = (((1,), (1,)), ((), ()))

    @jax.custom_vjp
    def xdot(x, m):
        return _xdot_r_impl(x, m, n, nn)

    def fwd(x, m):
        return _xdot_r_impl(x, m, n, nn), m

    def bwd(m, ct):
        return _xdot_r_impl(ct, m, n, nt), jnp.zeros_like(m)

    xdot.defvjp(fwd, bwd)
    return xdot


_segsum = _make_xdot_r(2)


def _xdot_l_impl(m, x, n, dims):
    acc = None
    for p in _split_bf16(x, n):
        t = lax.dot_general(m, p, dims, preferred_element_type=f32)
        acc = t if acc is None else acc + t
    return acc


@jax.custom_vjp
def _xdot_l(m, x):
    return _xdot_l_impl(m, x, 3, (((1,), (0,)), ((), ())))


def _xdot_l_fwd(m, x):
    return _xdot_l(m, x), m


def _xdot_l_bwd(m, ct):
    return jnp.zeros_like(m), _xdot_l_impl(m, ct, 3, (((0,), (0,)), ((), ())))


_xdot_l.defvjp(_xdot_l_fwd, _xdot_l_bwd)


@jax.custom_vjp
def _bdot(x, w):
    return jnp.dot(x.astype(bf16), w.astype(bf16), preferred_element_type=f32)


def _bdot_fwd(x, w):
    return _bdot(x, w), (x, w)


def _bdot_bwd(res, ct):
    x, w = res
    c = ct.astype(bf16)
    dx = lax.dot_general(c, w.astype(bf16), (((1,), (1,)), ((), ())), preferred_element_type=f32)
    dw = lax.dot_general(x.astype(bf16), c, (((0,), (0,)), ((), ())), preferred_element_type=f32)
    return dx, dw


_bdot.defvjp(_bdot_fwd, _bdot_bwd)


def _silu(x):
    return x * jax.nn.sigmoid(x)


def _softplus(x):
    return jnp.maximum(x, 0.0) + jnp.log(1.0 + jnp.exp(-jnp.abs(x)))


def _rms(x, w, eps):
    return x * lax.rsqrt(jnp.mean(x * x, axis=-1, keepdims=True) + eps) * w


def _seg_matrix(width, seg):
    i = np.arange(width)
    return jnp.asarray((i[:, None] // seg) == (i[None, :] // seg), dtype=bf16)


def _chunk_tri(rows, chunk):
    i = np.arange(rows)
    return jnp.asarray(((i[:, None] // chunk) == (i[None, :] // chunk)) & (i[:, None] >= i[None, :]), dtype=bf16)


HALO = 8


def _full_spec(shape):
    nd = len(shape)
    return pl.BlockSpec(shape, lambda i: (0,) * nd)


def _entry_specs(entries, tt, block_of):
    specs, ops = [], []
    for arr, w, ci, shifts in entries:
        specs.append(pl.BlockSpec((tt, w), lambda i, ci=ci: (block_of(i), ci)))
        ops.append(arr)
        if shifts:
            specs.append(pl.BlockSpec((HALO, w), lambda i, ci=ci: (jnp.maximum(block_of(i) * (tt // HALO) - 1, 0), ci)))
            ops.append(arr)
    return specs, ops


def _load_entries(entries, refs, first):
    tiles, k = [], 0
    for _, w, _, shifts in entries:
        x = refs[k][...].astype(f32)
        k += 1
        if not shifts:
            tiles.append(x)
            continue
        halo = jnp.where(first, 0.0, refs[k][...].astype(f32))
        k += 1
        row = lax.broadcasted_iota(jnp.int32, (HALO, w), 0)
        for s in shifts:
            if s == 0:
                tiles.append(x)
                continue
            r = pltpu.roll(x, s, 0)
            head = jnp.where(row < s, pltpu.roll(halo, s, 0), r[0:HALO])
            tiles.append(jnp.concatenate([head, r[HALO:]], axis=0))
    return tiles


def _unshift_sum(grads, shifts, carry, tt):
    w = grads[0].shape[1]
    row = lax.broadcasted_iota(jnp.int32, (tt, w), 0)
    row8 = lax.broadcasted_iota(jnp.int32, (HALO, w), 0)
    dx, out = None, jnp.zeros((HALO, w), f32)
    for d, s in zip(grads, shifts):
        if s == 0:
            part = d
        else:
            part = jnp.where(row < tt - s, pltpu.roll(d, tt - s, 0), 0.0)
            out = out + jnp.where(row8 >= HALO - s, pltpu.roll(d[0:HALO], HALO - s, 0), 0.0)
        dx = part if dx is None else dx + part
    return jnp.concatenate([dx[:tt - HALO], dx[tt - HALO:] + carry], axis=0), out


def _stage_fwd(name, fn, tiles, params, outs, tt):
    rows = tiles[0][0].shape[0]
    npar = len(params)
    specs, ops = _entry_specs(tiles, tt, lambda i: i)
    nin = len(ops)

    def body(*refs):
        ts = _load_entries(tiles, refs[:nin], pl.program_id(0) == 0)
        ps = [r[...] for r in refs[nin:nin + npar]]
        res = fn(ps, ts)
        for r, v in zip(refs[nin + npar:], res):
            r[...] = v.astype(r.dtype)

    return pl.pallas_call(
        body, grid=(rows // tt,),
        in_specs=specs + [_full_spec(p.shape) for p in params],
        out_specs=[pl.BlockSpec((tt, w), lambda i: (i, 0)) for (w, _) in outs],
        out_shape=[jax.ShapeDtypeStruct((rows, w), dt) for (w, dt) in outs],
        compiler_params=pltpu.CompilerParams(dimension_semantics=("parallel",)),
        name=name,
    )(*ops, *params)


def _stage_bwd(name, fn, tiles, params, cots, tt, tile_grad, param_grad, joint=None):
    rows = tiles[0][0].shape[0]
    nblk = rows // tt
    npar = len(params)
    block_of = lambda i: nblk - 1 - i
    specs, ops = _entry_specs(tiles, tt, block_of)
    nin = len(ops)
    flat_cots = [c for group in cots for c in group]
    groups = [len(g) for g in cots]
    ncot = len(flat_cots)
    counts = [len(e[3]) if e[3] else 1 for e in tiles]
    dt_entries = [e for e, g in zip(tiles, tile_grad) if g]
    dp_shapes = [p.shape for p, g in zip(params, param_grad) if g]
    ndt = len(dt_entries)
    carry_w = [e[1] for e in dt_entries if e[3]]
    flags = [g for g, n in zip(tile_grad, counts) for _ in range(n)]
    members, j_width, j_cidx, j_buf = joint if joint else ([], 0, 0, None)
    solo = [k for k in range(ndt) if k not in members]
    nsolo, njoint, nbuf = len(solo), int(bool(members)), int(j_buf is not None)
    j_block = sum(dt_entries[k][1] for k in members)

    def body(*refs):
        i = pl.program_id(0)
        p_refs = refs[nin:nin + npar]
        c_refs = refs[nin + npar:nin + npar + ncot]
        base = nin + npar + ncot + nbuf
        dt_refs = refs[base:base + nsolo]
        joint_refs = refs[base + nsolo:base + nsolo + njoint]
        dp_refs = refs[base + nsolo + njoint:base + nsolo + njoint + len(dp_shapes)]
        carry_refs = refs[base + nsolo + njoint + len(dp_shapes):]
        ts = _load_entries(tiles, refs[:nin], block_of(i) == 0)
        ps = [r[...] for r in p_refs]

        def f(dp, dt):
            dp, dt = iter(dp), iter(dt)
            pp = [next(dp) if g else p for p, g in zip(ps, param_grad)]
            tl = [next(dt) if g else t for t, g in zip(ts, flags)]
            return fn(pp, tl)

        _, vjp = jax.vjp(f, [p for p, g in zip(ps, param_grad) if g], [t for t, g in zip(ts, flags) if g])
        cs, j = [], 0
        for n in groups:
            acc = c_refs[j][...].astype(f32)
            for q in range(1, n):
                acc = acc + c_refs[j + q][...].astype(f32)
            cs.append(acc)
            j += n
        gp, gt = vjp(cs)

        @pl.when(i == 0)
        def _():
            for r in dp_refs:
                r[...] = jnp.zeros_like(r)
            for r in carry_refs:
                r[...] = jnp.zeros_like(r)

        gt, k, kc, dxs = list(gt), 0, 0, []
        for e, n in zip(dt_entries, [n for n, g in zip(counts, tile_grad) if g]):
            if e[3]:
                dx, out = _unshift_sum(gt[k:k + n], e[3], carry_refs[kc][...], tt)
                carry_refs[kc][...] = out
                kc += 1
            else:
                dx = gt[k]
            dxs.append(dx)
            k += n
        for r, k in zip(dt_refs, solo):
            r[...] = dxs[k]
        off = 0
        for k in members:
            w = dt_entries[k][1]
            joint_refs[0][:, off:off + w] = dxs[k]
            off += w
        for r, v in zip(dp_refs, gp):
            r[...] += v

    res = pl.pallas_call(
        body, grid=(nblk,),
        in_specs=specs + [_full_spec(p.shape) for p in params]
        + [pl.BlockSpec((tt, w), lambda i, ci=ci: (block_of(i), ci)) for (_, w, ci, *_) in flat_cots]
        + [pl.BlockSpec(memory_space=pl.ANY)] * nbuf,
        out_specs=[pl.BlockSpec((tt, dt_entries[k][1]), lambda i: (block_of(i), 0)) for k in solo]
        + [pl.BlockSpec((tt, j_block), lambda i: (block_of(i), j_cidx))] * njoint
        + [_full_spec(s) for s in dp_shapes],
        out_shape=[jax.ShapeDtypeStruct((rows, dt_entries[k][1]), f32) for k in solo]
        + [jax.ShapeDtypeStruct((rows, j_width), f32)] * njoint
        + [jax.ShapeDtypeStruct(s, f32) for s in dp_shapes],
        scratch_shapes=[pltpu.VMEM((HALO, w), f32) for w in carry_w],
        input_output_aliases={nin + npar + ncot: nsolo} if nbuf else {},
        compiler_params=pltpu.CompilerParams(dimension_semantics=("arbitrary",)),
        name=name,
    )(*ops, *params, *[c[0] for c in flat_cots], *([j_buf] if nbuf else []))
    res = list(res)
    return res[:nsolo], res[nsolo + njoint:], (res[nsolo] if njoint else None)


def _whole(a):
    return (a, a.shape[1], 0, None)


def _matmul(name, a, b, mode, out_dtype=f32, tm=1024, tn=1024, tk=1024, shards=1, job=None):
    S = shards
    if mode == "nn":
        M, K = a.shape
        w = b.shape[-1]
    elif mode == "nt":
        M = a.shape[0]
        if S > 1:
            _, N, w = b.shape
            K = S * w
        else:
            N, K = b.shape
            w = K
    else:
        K, M = a.shape
        w = b.shape[1] // S
    if mode != "nt":
        N = S * w
    tm = _pick(M, tm, LANES)
    if mode == "nt":
        tn = _pick(N, tn, LANES)
        tk = _pick(w, tk, LANES)
    else:
        tn = _pick(w, tn, LANES)
        tk = _pick(K, tk, LANES if mode == "nn" else 16)
    nk = K // tk
    nb = w // (tk if mode == "nt" else tn)
    if mode == "nn":
        a_spec = pl.BlockSpec((tm, tk), lambda i, j, k: (i, k))
        if S > 1:
            b_spec = pl.BlockSpec((1, tk, tn), lambda i, j, k: (j // nb, k, j % nb))
        else:
            b_spec = pl.BlockSpec((tk, tn), lambda i, j, k: (k, j))
        dims = (((1,), (0,)), ((), ()))
    elif mode == "nt":
        a_spec = pl.BlockSpec((tm, tk), lambda i, j, k: (i, k))
        if S > 1:
            b_spec = pl.BlockSpec((1, tn, tk), lambda i, j, k: (k // nb, j, k % nb))
        else:
            b_spec = pl.BlockSpec((tn, tk), lambda i, j, k: (j, k))
        dims = (((1,), (1,)), ((), ()))
    else:
        a_spec = pl.BlockSpec((tk, tm), lambda i, j, k: (k, i))
        b_spec = pl.BlockSpec((tk, tn), lambda i, j, k: (k, j))
        dims = (((0,), (0,)), ((), ()))
    if mode == "tn" and S > 1:
        o_spec = pl.BlockSpec((1, tm, tn), lambda i, j, k: (j // nb, i, j % nb))
        o_shape = (S, M, w)
    else:
        o_spec = pl.BlockSpec((tm, tn), lambda i, j, k: (i, j))
        o_shape = (M, N)
    b_lead = S > 1 and mode != "tn"
    o_lead = S > 1 and mode == "tn"

    j_ins, j_outs, j_sems = _job_parts(job)
    nji, njo = len(j_ins), len(j_outs)
    grid = (M // tm, N // tn, nk)

    def run_job(refs):
        step = (pl.program_id(0) * grid[1] + pl.program_id(1)) * grid[2] + pl.program_id(2)
        _job_steps(job, refs[2:2 + nji], refs[3 + nji:3 + nji + njo], refs[len(refs) - 2:], step,
                   grid[0] * grid[1] * grid[2] - 1)

    def body(*refs):
        a_ref, b_ref, o_ref, acc_ref = refs[0], refs[1], refs[2 + nji], refs[3 + nji + njo]
        run_job(refs)
        k = pl.program_id(2)

        @pl.when(k == 0)
        def _():
            acc_ref[...] = jnp.zeros_like(acc_ref)

        bv = b_ref[0] if b_lead else b_ref[...]
        acc_ref[...] += lax.dot_general(a_ref[...].astype(bf16), bv.astype(bf16), dims, preferred_element_type=f32)

        @pl.when(k == nk - 1)
        def _():
            if o_lead:
                o_ref[0] = acc_ref[...].astype(o_ref.dtype)
            else:
                o_ref[...] = acc_ref[...].astype(o_ref.dtype)

    def body_one_step(*refs):
        a_ref, b_ref, o_ref = refs[0], refs[1], refs[2 + nji]
        run_job(refs)
        bv = b_ref[0] if b_lead else b_ref[...]
        res = lax.dot_general(a_ref[...].astype(bf16), bv.astype(bf16), dims, preferred_element_type=f32)
        if o_lead:
            o_ref[0] = res.astype(o_ref.dtype)
        else:
            o_ref[...] = res.astype(o_ref.dtype)

    res = pl.pallas_call(
        body if nk > 1 else body_one_step, grid=grid,
        in_specs=[a_spec, b_spec] + [_ANY] * nji,
        out_specs=[o_spec] + [_ANY] * njo,
        out_shape=[jax.ShapeDtypeStruct(o_shape, out_dtype)] + j_outs,
        scratch_shapes=([pltpu.VMEM((tm, tn), f32)] if nk > 1 else []) + j_sems,
        compiler_params=pltpu.CompilerParams(
            dimension_semantics=("arbitrary",) * 3 if job else ("parallel", "parallel", "arbitrary")),
        name=name,
    )(a, b, *j_ins)
    return res if job else res[0]


def _make_bmm(precision):
    if precision is None:
        cast, kw = (lambda v: v.astype(bf16)), {}
    else:
        cast, kw = (lambda v: v), {"precision": precision}

    def nn(a, b):
        return jnp.einsum("hij,hjk->hik", cast(a), cast(b), preferred_element_type=f32, **kw)

    def nt(a, b):
        return jnp.einsum("hik,hjk->hij", cast(a), cast(b), preferred_element_type=f32, **kw)

    def tn(a, b):
        return jnp.einsum("hki,hkj->hij", cast(a), cast(b), preferred_element_type=f32, **kw)

    if precision is not None:
        return nn, nt, tn
    nn_v, nt_v, tn_v = jax.custom_vjp(nn), jax.custom_vjp(nt), jax.custom_vjp(tn)
    keep = lambda f: (lambda a, b: (f(a, b), (a, b)))
    nn_v.defvjp(keep(nn), lambda r, ct: (nt(ct, r[1]), tn(r[0], ct)))
    nt_v.defvjp(keep(nt), lambda r, ct: (nn(ct, r[1]), tn(ct, r[0])))
    tn_v.defvjp(keep(tn), lambda r, ct: (nt(r[1], ct), nn(r[0], ct)))
    return nn_v, nt_v, tn_v


_bmm, _bmm_nt, _bmm_tn = _make_bmm(None)
_bmm_exact = _make_bmm(lax.Precision.HIGH)[0]


def _masks(n):
    r = lax.broadcasted_iota(jnp.int32, (n, n), 0)
    c = lax.broadcasted_iota(jnp.int32, (n, n), 1)
    return (r >= c)[None], (r > c)[None], (r == c)[None]


_INV_BLOCK = 8


def _nilpotent_inverse(m, eye):
    p = eye + m
    for _ in range(2):
        m = _bmm(m, m)
        p = p + _bmm(p, m)
    return p


def _neumann_inverse_impl(m):
    n = m.shape[1]
    assert n == _INV_BLOCK * _INV_BLOCK
    r = lax.broadcasted_iota(jnp.int32, (n, n), 0)
    c = lax.broadcasted_iota(jnp.int32, (n, n), 1)
    eye = (r == c).astype(f32)[None]
    inside = jnp.where((r // _INV_BLOCK == c // _INV_BLOCK)[None], m, 0.0)
    d_inv = _nilpotent_inverse(inside, eye)
    return _bmm(_nilpotent_inverse(_bmm(d_inv, m - inside), eye), d_inv)


@jax.custom_vjp
def _neumann_inverse(m):
    return _neumann_inverse_impl(m)


def _neumann_inverse_fwd(m):
    p = _neumann_inverse_impl(m)
    return p, p


def _neumann_inverse_bwd(p, ct):
    return (_bmm_tn(p, _bmm_nt(ct, p)),)


_neumann_inverse.defvjp(_neumann_inverse_fwd, _neumann_inverse_bwd)


@jax.custom_vjp
def _given_inverse(m, p):
    return p


_given_inverse.defvjp(lambda m, p: (p, p), lambda p, ct: (_neumann_inverse_bwd(p, ct)[0], jnp.zeros_like(p)))


def _gdn_chunk(s, q, k, v, beta, gc, gr, gl, p=None):
    n = q.shape[1]
    causal, strict, _ = _masks(n)
    decay = jnp.where(causal, jnp.exp(jnp.where(causal, gc - gr, 0.0)), 0.0)
    kb = k * beta
    vb = v * beta
    lower = jnp.where(strict, _bmm_nt(kb, k) * decay, 0.0)
    t_mat = _neumann_inverse(-lower) if p is None else _given_inverse(-lower, p)
    egc = jnp.exp(gc)
    u = _bmm(t_mat, vb)
    w = _bmm(t_mat, kb * egc)
    attn = jnp.where(causal, _bmm_nt(q, k) * decay, 0.0)
    v_new = u - _bmm(w, s)
    o = _bmm(q * egc, s) + _bmm(attn, v_new)
    k_dec = k * jnp.exp(gl - gc)
    s_new = s * jnp.exp(gl) + _bmm_tn(k_dec, v_new)
    return s_new, o, t_mat


def _rwkv_chunk(s, r, lw, k, v, a, b, p=None):
    n = r.shape[1]
    causal, strict, _ = _masks(n)
    tri = jnp.broadcast_to(causal.astype(f32), (r.shape[0], n, n))
    lc = _bmm_exact(tri, lw)
    ein = jnp.exp(lc)
    eout = jnp.exp(-lc)
    a_t = a * jnp.exp(lc - lw)
    b_t = b * eout
    k_t = k * eout
    r_t = r * ein
    a_ab = jnp.where(strict, _bmm_nt(a_t, b_t), 0.0)
    a_ak = jnp.where(strict, _bmm_nt(a_t, k_t), 0.0)
    inv = _neumann_inverse(a_ab) if p is None else _given_inverse(a_ab, p)
    u = _bmm(inv, _bmm_nt(a_t, s) + _bmm(a_ak, v))
    y = (_bmm_nt(r_t, s) + _bmm(jnp.where(causal, _bmm_nt(r_t, b_t), 0.0), u)
         + _bmm(jnp.where(causal, _bmm_nt(r_t, k_t), 0.0), v))
    e_last = jnp.exp(jnp.sum(lw, axis=1, keepdims=True))
    s_new = s * e_last + _bmm_tn(u, b_t * e_last) + _bmm_tn(v, k_t * e_last)
    return s_new, y, inv


def _heads_in(ref, rows):
    return jnp.stack([ref[rows, h * HEAD_DIM:(h + 1) * HEAD_DIM] for h in range(HEADS)], axis=0)


def _heads_out(ref, rows, val):
    for h in range(HEADS):
        ref[rows, h * HEAD_DIM:(h + 1) * HEAD_DIM] = val[h]


def _gdn_scalars(bt, gt):
    n = bt.shape[0]
    gtt = gt.T
    hs = range(HEADS)
    return [jnp.stack([bt[:, h:h + 1] for h in hs], axis=0),
            jnp.stack([gt[:, HEADS + h:HEADS + h + 1] for h in hs], axis=0),
            jnp.stack([gtt[HEADS + h:HEADS + h + 1, :] for h in hs], axis=0),
            jnp.stack([gt[n - 1:n, HEADS + h:HEADS + h + 1] for h in hs], axis=0)]


def _gdn_scalars_back(dbeta, dgc, dgr, dgl):
    n = dbeta.shape[1]
    lane = lax.broadcasted_iota(jnp.int32, (n, LANES), 1)
    row = lax.broadcasted_iota(jnp.int32, (n, LANES), 0)
    sub = lax.broadcasted_iota(jnp.int32, (LANES, n), 0)
    db = jnp.zeros((n, LANES), f32)
    dg = jnp.zeros((n, LANES), f32)
    dgt = jnp.zeros((LANES, n), f32)
    for h in range(HEADS):
        db = jnp.where(lane == h, dbeta[h], db)
        dg = jnp.where(lane == HEADS + h, dgc[h] + jnp.where(row == n - 1, dgl[h], 0.0), dg)
        dgt = jnp.where(sub == HEADS + h, dgr[h], dgt)
    return [db, dg + dgt.T]


SCAN_GROUP = 2


def _scan_steps(t):
    g = SCAN_GROUP if (t // CHUNK) % SCAN_GROUP == 0 else 1
    return g, t // (CHUNK * g)


def _scan_spec(width, g, n, reverse):
    if reverse:
        return pl.BlockSpec((g * CHUNK, width), lambda i: (n - 1 - i, 0))
    return pl.BlockSpec((g * CHUNK, width), lambda i: (i, 0))


def _hist_spec(g, n, reverse):
    blk = (g, HEADS, HEAD_DIM, HEAD_DIM)
    if reverse:
        return pl.BlockSpec(blk, lambda i: (n - 1 - i, 0, 0, 0))
    return pl.BlockSpec(blk, lambda i: (i, 0, 0, 0))


def _job_parts(job):
    if job is None:
        return [], [], []
    sems = [pltpu.SemaphoreType.DMA((job["nsem"],)), pltpu.SemaphoreType.DMA((job["nsem"],))]
    return list(job["ins"]), list(job["out_shapes"]), sems


def _job_steps(job, in_refs, out_refs, sems, step, last):
    if job is None:
        return

    @pl.when(step == 0)
    def _():
        job["start"](in_refs, out_refs, *sems)

    @pl.when(step == last)
    def _():
        job["finish"](in_refs, out_refs, *sems)


def _scan_fwd(name, fn, rows_in, scal_in=(), job=None):
    t = rows_in[0].shape[0]
    grp, n = _scan_steps(t)
    nr, ns = len(rows_in), len(scal_in)
    j_ins, j_outs, j_sems = _job_parts(job)
    nji, njo = len(j_ins), len(j_outs)

    def body(*refs):
        o_ref, sh_ref, ph_ref = refs[nr + ns + nji:nr + ns + nji + 3]
        s_scr = refs[nr + ns + nji + 3 + njo]
        _job_steps(job, refs[nr + ns:nr + ns + nji], refs[nr + ns + nji + 3:nr + ns + nji + 3 + njo],
                   refs[nr + ns + nji + 3 + njo + 1:], pl.program_id(0), n - 1)

        @pl.when(pl.program_id(0) == 0)
        def _():
            s_scr[...] = jnp.zeros_like(s_scr)

        s = s_scr[...]
        for sub in range(grp):
            rows = slice(sub * CHUNK, (sub + 1) * CHUNK)
            sh_ref[sub] = s
            ins = [_heads_in(r, rows) for r in refs[:nr]]
            if ns:
                ins += _gdn_scalars(*[r[rows, :] for r in refs[nr:nr + ns]])
            s, o, p = fn(s, *ins)
            _heads_out(o_ref, rows, o)
            ph_ref[sub] = p
        s_scr[...] = s

    return pl.pallas_call(
        body, grid=(n,),
        in_specs=[_scan_spec(a.shape[1], grp, n, False) for a in (*rows_in, *scal_in)] + [_ANY] * nji,
        out_specs=[_scan_spec(WIDTH, grp, n, False), _hist_spec(grp, n, False), _hist_spec(grp, n, False)] + [_ANY] * njo,
        out_shape=[jax.ShapeDtypeStruct((t, WIDTH), f32)]
        + [jax.ShapeDtypeStruct((t // CHUNK, HEADS, HEAD_DIM, HEAD_DIM), f32)] * 2 + j_outs,
        scratch_shapes=[pltpu.VMEM((HEADS, HEAD_DIM, HEAD_DIM), f32)] + j_sems,
        compiler_params=pltpu.CompilerParams(dimension_semantics=("arbitrary",)),
        name=name,
    )(*rows_in, *scal_in, *j_ins)


def _scan_bwd(name, fn, rows_in, scal_in, s_hist, p_hist, d_out, job=None):
    t = rows_in[0].shape[0]
    grp, n = _scan_steps(t)
    nr, ns = len(rows_in), len(scal_in)
    j_ins, j_outs, j_sems = _job_parts(job)
    nji, njo = len(j_ins), len(j_outs)

    def body(*refs):
        sh_ref, ph_ref, do_ref = refs[nr + ns:nr + ns + 3]
        base = nr + ns + 3 + nji
        g_refs = refs[base:base + nr + ns]
        ds_scr = refs[base + nr + ns + njo]
        _job_steps(job, refs[nr + ns + 3:base], refs[base + nr + ns:base + nr + ns + njo],
                   refs[base + nr + ns + njo + 1:], pl.program_id(0), n - 1)

        @pl.when(pl.program_id(0) == 0)
        def _():
            ds_scr[...] = jnp.zeros_like(ds_scr)

        ds = ds_scr[...]
        for sub in reversed(range(grp)):
            rows = slice(sub * CHUNK, (sub + 1) * CHUNK)
            ins = [_heads_in(r, rows) for r in refs[:nr]]
            if ns:
                ins += _gdn_scalars(*[r[rows, :] for r in refs[nr:nr + ns]])
            p = ph_ref[sub]
            _, vjp = jax.vjp(lambda s, *a, p=p: fn(s, *a, p=p)[:2], sh_ref[sub], *ins)
            g = vjp((ds, _heads_in(do_ref, rows)))
            ds = g[0]
            for r, v in zip(g_refs[:nr], g[1:1 + nr]):
                _heads_out(r, rows, v)
            if ns:
                for r, v in zip(g_refs[nr:], _gdn_scalars_back(*g[1 + nr:])):
                    r[rows, :] = v
        ds_scr[...] = ds

    arrs = (*rows_in, *scal_in)
    return pl.pallas_call(
        body, grid=(n,),
        in_specs=[_scan_spec(a.shape[1], grp, n, True) for a in arrs]
        + [_hist_spec(grp, n, True), _hist_spec(grp, n, True), _scan_spec(WIDTH, grp, n, True)] + [_ANY] * nji,
        out_specs=[_scan_spec(a.shape[1], grp, n, True) for a in arrs] + [_ANY] * njo,
        out_shape=[jax.ShapeDtypeStruct(a.shape, f32) for a in arrs] + j_outs,
        scratch_shapes=[pltpu.VMEM((HEADS, HEAD_DIM, HEAD_DIM), f32)] + j_sems,
        compiler_params=pltpu.CompilerParams(dimension_semantics=("arbitrary",)),
        name=name,
    )(*arrs, s_hist, p_hist, d_out, *j_ins)


def _fn_norm_mod(ps, ts):
    nw, shift, scale = ps
    (x,) = ts
    return [_rms(x, nw, NORM_EPS) * (1.0 + scale) + shift]


def _fn_norm_mod_and_x(ps, ts):
    return _fn_norm_mod(ps, ts) + [ts[0]]


def _fn_resid_norm_mod(ps, ts):
    gate, nw, shift, scale = ps
    x, mo = ts
    x1 = x + gate * mo
    return [x1, _rms(x1, nw, NORM_EPS) * (1.0 + scale) + shift]


def _fn_gdn_pre(ps, ts):
    cw = ps[:12]
    alog, dtb, seg, tri = ps[12:]
    ba = ts[12]
    outs = []
    for part in range(3):
        x = ts[4 * part:4 * part + 4]
        w = cw[4 * part:4 * part + 4]
        conv = w[3] * x[0] + w[2] * x[1] + w[1] * x[2] + w[0] * x[3]
        u = _silu(conv)
        if part < 2:
            u = u * lax.rsqrt(_segsum(u * u, seg) + 1e-6)
            if part == 0:
                u = u * (HEAD_DIM ** -0.5)
        outs.append(u)
    beta = jax.nn.sigmoid(ba)
    g = -jnp.exp(alog) * _softplus(ba + dtb)
    gc = _xdot_l(tri, g)
    return outs + [beta, gc]


def _fn_gdn_post(ps, ts):
    ow, seg = ps
    o, z = ts
    ms = _segsum(o * o, seg) * (1.0 / HEAD_DIM)
    return [o * lax.rsqrt(ms + NORM_EPS) * ow * _silu(z)]


def _fn_rwkv_pre(ps, ts):
    mu_r, mu_k, mu_v, mu_l, mu_g, w0, w2p, a0, a2p, g2p, k_k, k_a, seg = ps
    r0, r1, k0, k1, v0, v1, l0, l1, g0, g1 = ts
    xr = r0 + (r1 - r0) * mu_r
    xk = k0 + (k1 - k0) * mu_k
    xv = v0 + (v1 - v0) * mu_v
    xl = l0 + (l1 - l0) * mu_l
    xg = g0 + (g1 - g0) * mu_g
    w = -_softplus(-(w0 + _bdot(jnp.tanh(xl), w2p))) - 0.5
    lw = -jnp.exp(w)
    a = jax.nn.sigmoid(a0 + _bdot(xl, a2p))
    g = _bdot(jax.nn.sigmoid(xg), g2p)
    kk = xk * k_k
    kk = kk * lax.rsqrt(_segsum(kk * kk, seg) + 1e-6)
    k2 = xk * (1.0 + (a - 1.0) * k_a)
    return [xr, lw, k2, xv, -kk, kk * a, g]


def _fn_rwkv_post(ps, ts):
    lw_, lb_, rk, seg = ps
    y, r, k2, v, g = ts
    inv = 1.0 / HEAD_DIM
    yc = y - _segsum(y, seg) * inv
    var = _segsum(yc * yc, seg) * inv
    yn = yc * lax.rsqrt(var + LNX_EPS) * lw_ + lb_
    bonus = _segsum(r * k2 * rk, seg) * v
    return [(yn + bonus) * g]


def _fn_merge(ps, ts):
    gla, glb, ya, yb = ts
    return [jax.nn.sigmoid(gla) * ya + jax.nn.sigmoid(glb) * yb]


def _fn_convglu(ps, ts):
    c0, c1, c2 = ps
    g0, g1, g2, up = ts
    return [_silu(c2 * g0 + c1 * g1 + c0 * g2) * up]


def _final_stage(x1, fo, tgt, gate2, nfw, tt):
    rows, d = x1.shape

    def loss_fn(gate, nw, xa, fa, tg):
        y = _rms(xa + gate * fa, nw, NORM_EPS)
        err = (y - tg) ** 2
        return 0.5 * jnp.sum(jnp.mean(err, axis=-1, keepdims=True), axis=0, keepdims=True)

    def body(x_ref, f_ref, t_ref, g_ref, w_ref, dx_ref, df_ref, dg_ref, dw_ref, l_ref):
        i = pl.program_id(0)
        args = (g_ref[...], w_ref[...], x_ref[...], f_ref[...])
        tg = t_ref[...]
        lv, vjp = jax.vjp(lambda g, w, xa, fa: loss_fn(g, w, xa, fa, tg), *args)
        dg, dw, dx, df = vjp(jnp.ones((1, 1), f32))
        dx_ref[...] = dx
        df_ref[...] = df

        @pl.when(i == 0)
        def _():
            dg_ref[...] = jnp.zeros_like(dg_ref)
            dw_ref[...] = jnp.zeros_like(dw_ref)
            l_ref[...] = jnp.zeros_like(l_ref)

        dg_ref[...] += dg
        dw_ref[...] += dw
        l_ref[...] += jnp.broadcast_to(lv, l_ref.shape)

    row = pl.BlockSpec((tt, d), lambda i: (i, 0))
    vec = pl.BlockSpec((1, d), lambda i: (0, 0))
    return pl.pallas_call(
        body, grid=(rows // tt,),
        in_specs=[row, row, row, vec, vec],
        out_specs=[row, row, vec, vec, pl.BlockSpec((1, LANES), lambda i: (0, 0))],
        out_shape=[jax.ShapeDtypeStruct((rows, d), f32)] * 2 + [jax.ShapeDtypeStruct((1, d), f32)] * 2
        + [jax.ShapeDtypeStruct((1, LANES), f32)],
        compiler_params=pltpu.CompilerParams(dimension_semantics=("arbitrary",)),
        name="loss_head",
    )(x1, fo, tgt, gate2, nfw)


def _ada_fwd(c_all, w_shard, b_cols):
    def body(c_ref, w_ref, b_ref, cond_ref, mod_ref):
        cond = _silu(c_ref[...])
        cond_ref[...] = cond
        mod_ref[...] = jnp.dot(cond.astype(bf16), w_ref[...].astype(bf16), preferred_element_type=f32) + b_ref[...]

    n = w_shard.shape[1]
    return pl.pallas_call(
        body, out_shape=[jax.ShapeDtypeStruct(c_all.shape, f32), jax.ShapeDtypeStruct((c_all.shape[0], n), f32)],
        name="ada_fwd",
    )(c_all, w_shard, b_cols)


def _adamw(name, w, g, m, v):
    rows, width = w.shape
    tt = _pick(rows, 128, 8)
    c1 = 1.0 - ADAM_B1 ** ADAM_STEP
    c2 = 1.0 - ADAM_B2 ** ADAM_STEP

    def body(w_ref, g_ref, m_ref, v_ref, d_ref, mo_ref, vo_ref):
        gg = g_ref[...]
        mn = ADAM_B1 * m_ref[...] + (1.0 - ADAM_B1) * gg
        vn = ADAM_B2 * v_ref[...] + (1.0 - ADAM_B2) * (gg * gg)
        m_hat = mn / c1
        v_hat = vn / c2
        d_ref[...] = -ADAM_LR * (m_hat / (jnp.sqrt(v_hat) + ADAM_EPS) + ADAM_WD * w_ref[...])
        mo_ref[...] = mn
        vo_ref[...] = vn

    spec = pl.BlockSpec((tt, width), lambda i: (i, 0))
    return pl.pallas_call(
        body, grid=(rows // tt,), in_specs=[spec] * 4, out_specs=[spec] * 3,
        out_shape=[jax.ShapeDtypeStruct((rows, width), f32)] * 3,
        compiler_params=pltpu.CompilerParams(dimension_semantics=("parallel",)),
        name=name,
    )(w, g, m, v)


def _place():
    return lax.axis_index("x"), lax.axis_index("y"), lax.axis_index("c")


def _ag8(name, blk):
    m, w = blk.shape

    def body(x_ref, out_ref, send_sems, recv_sems, local_sem):
        x, y, c = _place()
        me, sibling = (x, y, c), (x, y, 1 - c)
        chips = _other_chips(x, y)

        def slot(px, py, pc):
            return out_ref.at[4 * px + 2 * py + pc]

        def copy(k, block, to, src=None):
            return pltpu.make_async_remote_copy(src_ref=slot(*block) if src is None else src, dst_ref=slot(*block),
                                                send_sem=send_sems.at[k], recv_sem=recv_sems.at[k], device_id=to,
                                                device_id_type=MESH_ID)

        mine = pltpu.make_async_copy(x_ref, slot(*me), local_sem)
        mine.start()
        first = [copy(0, me, sibling, src=x_ref)] + [copy(1 + j, me, (*chip, c), src=x_ref) for j, chip in enumerate(chips)]
        for cp in first:
            cp.start()
        passed = [copy(4 + j, (*chip, c), sibling) for j, chip in enumerate(chips)]
        for j, chip in enumerate(chips):
            copy(1 + j, (*chip, c), me).wait_recv()
            passed[j].start()
        copy(0, sibling, me).wait_recv()
        for j, chip in enumerate(chips):
            copy(4 + j, (*chip, 1 - c), me).wait_recv()
        for cp in first + passed:
            cp.wait_send()
        mine.wait()

    return pl.pallas_call(
        body, out_shape=jax.ShapeDtypeStruct((8, m, w), blk.dtype),
        in_specs=[pl.BlockSpec(memory_space=pltpu.VMEM)], out_specs=pl.BlockSpec(memory_space=pltpu.VMEM),
        scratch_shapes=[pltpu.SemaphoreType.DMA((7,)), pltpu.SemaphoreType.DMA((7,)), pltpu.SemaphoreType.DMA],
        name=name,
    )(blk)


def _other_chips(x, y):
    return [(1 - x, y), (x, 1 - y), (1 - x, 1 - y)]


_ANY = pl.BlockSpec(memory_space=pl.ANY)


def _rcopy(src, dst, send_sems, recv_sems, k, dev):
    return pltpu.make_async_remote_copy(src_ref=src, dst_ref=dst, send_sem=send_sems.at[k], recv_sem=recv_sems.at[k],
                                        device_id=dev, device_id_type=MESH_ID)


def _run_job(name, job):
    j_ins, j_outs, j_sems = _job_parts(job)
    n = len(j_ins)

    def body(*refs):
        job["start"](refs[:n], refs[n:n + len(j_outs)], *refs[n + len(j_outs):])
        job["finish"](refs[:n], refs[n:n + len(j_outs)], *refs[n + len(j_outs):])

    return pl.pallas_call(body, out_shape=j_outs, in_specs=[_ANY] * n, out_specs=[_ANY] * len(j_outs),
                          scratch_shapes=j_sems, name=name)(*j_ins)


def _ag4_job(ws):
    n = len(ws)

    def plan(w_refs, out_refs, send_sems, recv_sems):
        x, y, c = _place()
        chip = 2 * x + y
        sibling = (x, y, 1 - c)
        chips = _other_chips(x, y)
        mine = [pl.ds(c * (w.shape[0] // 2), w.shape[0] // 2) for w in ws]
        other = [pl.ds((1 - c) * (w.shape[0] // 2), w.shape[0] // 2) for w in ws]
        rc = lambda src, dst, k, dev: _rcopy(src, dst, send_sems, recv_sems, k, dev)
        first = [rc(w_refs[t].at[mine[t]], out_refs[t].at[chip, mine[t]], 7 * t + k, (px, py, c))
                 for t in range(n) for k, (px, py) in enumerate(chips)]
        own = [rc(w_refs[t], out_refs[t].at[chip], 7 * t + 6, sibling) for t in range(n)]
        landed = [rc(out_refs[t].at[2 * px + py, mine[t]], out_refs[t].at[2 * px + py, mine[t]], 7 * t + k, (px, py, c))
                  for t in range(n) for k, (px, py) in enumerate(chips)]
        forward = [rc(out_refs[t].at[2 * px + py, mine[t]], out_refs[t].at[2 * px + py, mine[t]], 7 * t + 3 + k, sibling)
                   for t in range(n) for k, (px, py) in enumerate(chips)]
        handed = [rc(out_refs[t].at[2 * px + py, other[t]], out_refs[t].at[2 * px + py, other[t]], 7 * t + 3 + k, sibling)
                  for t in range(n) for k, (px, py) in enumerate(chips)]
        return first, own, landed, forward, handed

    def start(*refs):
        first, own, _, _, _ = plan(*refs)
        for cp in first + own:
            cp.start()

    def finish(*refs):
        first, own, landed, forward, handed = plan(*refs)
        for arrived, fw in zip(landed, forward):
            arrived.wait_recv()
            fw.start()
        for cp in handed + own:
            cp.wait_recv()
        for cp in first + own + forward:
            cp.wait_send()

    return dict(ins=ws, out_shapes=[jax.ShapeDtypeStruct((4,) + w.shape, w.dtype) for w in ws], nsem=7 * n,
                start=start, finish=finish)


def _rs_sibling_swap(gs, tag):
    n = len(gs)

    def body(*refs):
        g_refs, recv_refs = refs[:n], refs[n:2 * n]
        send_sems, recv_sems = refs[2 * n:]
        x, y, c = _place()
        copies = []
        for t in range(n):
            rh = gs[t].shape[1] // 2
            for s_ in range(4):
                cp = _rcopy(g_refs[t].at[s_, pl.ds((1 - c) * rh, rh)], recv_refs[t].at[s_], send_sems, recv_sems,
                            4 * t + s_, (x, y, 1 - c))
                cp.start()
                copies.append(cp)
        for cp in copies:
            cp.wait_recv()
        for cp in copies:
            cp.wait_send()

    return pl.pallas_call(
        body, out_shape=[jax.ShapeDtypeStruct((4, g.shape[1] // 2, g.shape[2]), g.dtype) for g in gs],
        in_specs=[_ANY] * n, out_specs=[_ANY] * n,
        scratch_shapes=[pltpu.SemaphoreType.DMA((4 * n,)), pltpu.SemaphoreType.DMA((4 * n,))],
        name="grads_sibling_swap_" + tag,
    )(*gs)


def _chip_exchange_job(ps):
    n = len(ps)

    def plan(p_refs, recv_refs, send_sems, recv_sems):
        x, y, c = _place()
        return [_rcopy(p_refs[t].at[2 * px + py], recv_refs[t].at[k], send_sems, recv_sems, 3 * t + k, (px, py, c))
                for t in range(n) for k, (px, py) in enumerate(_other_chips(x, y))]

    def start(*refs):
        for cp in plan(*refs):
            cp.start()

    def finish(*refs):
        copies = plan(*refs)
        for cp in copies:
            cp.wait_recv()
        for cp in copies:
            cp.wait_send()

    return dict(ins=ps, out_shapes=[jax.ShapeDtypeStruct((3,) + p.shape[1:], p.dtype) for p in ps], nsem=3 * n,
                start=start, finish=finish)


_JOIN_PIECES = 4


def _rs_sibling_join(qs):
    n = len(qs)
    npc = _JOIN_PIECES

    def body(*refs):
        q_refs, out_refs = refs[:n], refs[n:2 * n]
        send_sems, recv_sems = refs[2 * n:]
        x, y, c = _place()
        copies = []
        for t in range(n):
            rh = qs[t].shape[0] // 2
            pr = rh // npc
            for i in range(npc):
                rows = pl.ds(c * rh + i * pr, pr)
                cp = _rcopy(q_refs[t].at[rows], out_refs[t].at[rows], send_sems, recv_sems, npc * t + i,
                            (x, y, 1 - c))
                cp.start()
                copies.append(cp)
        for t in range(n):
            rh = qs[t].shape[0] // 2
            pr = rh // npc
            for i in range(npc):
                rows = pl.ds((1 - c) * rh + i * pr, pr)
                _rcopy(q_refs[t].at[rows], out_refs[t].at[rows], send_sems, recv_sems, npc * t + i,
                       (x, y, 1 - c)).wait_recv()
        for cp in copies:
            cp.wait_send()

    return pl.pallas_call(
        body, out_shape=[jax.ShapeDtypeStruct(q.shape, q.dtype) for q in qs],
        in_specs=[_ANY] * n, out_specs=[_ANY] * n, input_output_aliases={t: t for t in range(n)},
        scratch_shapes=[pltpu.SemaphoreType.DMA((npc * n,)), pltpu.SemaphoreType.DMA((npc * n,))],
        name="grads_sibling_join",
    )(*qs)


def _add_half(name, g, recv, ci):
    S, r, w = g.shape
    rh = r // 2
    tt = _pick(rh, 256, 16)
    nb = rh // tt

    def body(c_ref, a_ref, b_ref, o_ref, ob_ref):
        v = a_ref[...] + b_ref[...]
        o_ref[...] = v
        ob_ref[...] = v.astype(bf16)

    blk = pl.BlockSpec((1, tt, w), lambda s_, i, cr: (s_, i, 0))
    grid_spec = pltpu.PrefetchScalarGridSpec(
        num_scalar_prefetch=1, grid=(S, nb),
        in_specs=[pl.BlockSpec((1, tt, w), lambda s_, i, cr: (s_, cr[0] * nb + i, 0)), blk],
        out_specs=[blk, blk])
    return pl.pallas_call(body, grid_spec=grid_spec,
                          out_shape=[jax.ShapeDtypeStruct((S, rh, w), f32), jax.ShapeDtypeStruct((S, rh, w), bf16)],
                          name=name)(ci.reshape(1).astype(jnp.int32), g, recv)


def _sum_chip(name, pair, others, chip, ci):
    _, rh, w = pair.shape
    tt = _pick(rh, 128, 16)
    nb = rh // tt

    def body(chip_ref, core_ref, a_ref, b_ref, o_ref):
        o_ref[...] = ((a_ref[0] + b_ref[0].astype(f32)) + b_ref[1].astype(f32)) + b_ref[2].astype(f32)

    grid_spec = pltpu.PrefetchScalarGridSpec(
        num_scalar_prefetch=2, grid=(nb,),
        in_specs=[pl.BlockSpec((1, tt, w), lambda i, ch, co: (ch[0], i, 0)),
                  pl.BlockSpec((3, tt, w), lambda i, ch, co: (0, i, 0))],
        out_specs=pl.BlockSpec((tt, w), lambda i, ch, co: (co[0] * nb + i, 0)))
    return pl.pallas_call(body, grid_spec=grid_spec, out_shape=jax.ShapeDtypeStruct((2 * rh, w), f32),
                          name=name)(chip.reshape(1).astype(jnp.int32), ci.reshape(1).astype(jnp.int32), pair, others)


def _sum_devices(gathered, head_row):
    _, rows, width = gathered.shape

    def body(g_ref, out_ref, head_ref):
        acc = g_ref[0]
        for d in range(1, 8):
            acc = acc + g_ref[d]
        out_ref[...] = acc
        row = acc[head_row:head_row + 1, :]
        hs = row[:, 0:HEAD_DIM]
        for h in range(1, HEADS):
            hs = hs + row[:, h * HEAD_DIM:(h + 1) * HEAD_DIM]
        head_ref[...] = jnp.zeros_like(head_ref)
        head_ref[0:1, 0:HEAD_DIM] = hs

    return pl.pallas_call(
        body, out_shape=[jax.ShapeDtypeStruct((rows, width), f32), jax.ShapeDtypeStruct((8, LANES), f32)],
        name="small_grads_sum",
    )(gathered)


def _pack(arrs, rows_mult, dtype):
    flat = jnp.concatenate([a.reshape(-1).astype(dtype) for a in arrs])
    per = PACK_W * rows_mult
    total = -(-flat.shape[0] // per) * per
    return jnp.pad(flat, (0, total - flat.shape[0])).reshape(total // PACK_W, PACK_W)


def _unpack(buf, shapes):
    flat = buf.reshape(-1)
    out, off = [], 0
    for s in shapes:
        n = int(np.prod(s))
        out.append(flat[off:off + n].reshape(s))
        off += n
    return out


_BIG = ["w_in", "w_branch_gdn", "w_branch_rwkv", "w_out", "w_ffn_in", "w_ffn_out"]
_MID = ["conv_gdn", "conv_ffn", "w2", "a2", "g2"]
_SMALL =["b_ada", "norm1_w", "a_log", "dt_bias", "onorm_gdn", "mu_rwkv", "w0", "a0", "k_k", "k_a", "r_k", "lnx_w",
          "lnx_b", "norm2_w", "norm_f_w"]
_ORDER = ["w_ada", "b_ada", "norm1_w", "w_in", "conv_gdn", "a_log", "dt_bias", "onorm_gdn", "w_branch_gdn", "mu_rwkv",
          "w0", "w2", "a0", "a2", "g2", "k_k", "k_a", "r_k", "lnx_w", "lnx_b", "w_branch_rwkv", "w_out", "norm2_w",
          "w_ffn_in", "conv_ffn", "w_ffn_out", "norm_f_w"]


_WIN_SEGMENTS = [(0, 1536, 0), (2064, 3600, 1536), (1536, 2048, 3072), (3600, 3728, 3584), (2048, 2064, 3712),
                 (3728, 3888, 3840), (3888, 5936, 4096)]
_WIN_PADDED = 6144


def _win_pad(shards):
    n = shards.shape[2]
    parts, at = [], 0
    for lo, hi, dst in _WIN_SEGMENTS:
        if dst > at:
            parts.append(jnp.zeros((shards.shape[1], dst - at), shards.dtype))
        c = lo
        while c < hi:
            j = c // n
            e = min(hi, (j + 1) * n)
            parts.append(shards[j][:, c - j * n:e - j * n])
            c = e
        at = dst + hi - lo
    if at < _WIN_PADDED:
        parts.append(jnp.zeros((shards.shape[1], _WIN_PADDED - at), shards.dtype))
    return jnp.concatenate(parts, axis=1)


def _win_unpad_shards(g, n):
    shards = []
    for j in range(4):
        parts = []
        for lo, hi, dst in sorted(_WIN_SEGMENTS):
            a, b = max(lo, j * n), min(hi, (j + 1) * n)
            if a < b:
                parts.append(g[:, dst + a - lo:dst + b - lo])
        shards.append(jnp.concatenate(parts, axis=1))
    return jnp.stack(shards)


def kernel(x, c, w_ada, b_ada, norm1_w, w_in, conv_gdn, a_log, dt_bias, onorm_gdn, w_branch_gdn, mu_rwkv, w0, w2, a0, a2, g2, k_k, k_a, r_k, lnx_w, lnx_b, w_branch_rwkv, w_out, norm2_w, w_ffn_in, conv_ffn, w_ffn_out, norm_f_w, loss_target, m_w_ada, m_b_ada, m_norm1_w, m_w_in, m_conv_gdn, m_a_log, m_dt_bias, m_onorm_gdn, m_w_branch_gdn, m_mu_rwkv, m_w0, m_w2, m_a0, m_a2, m_g2, m_k_k, m_k_a, m_r_k, m_lnx_w, m_lnx_b, m_w_branch_rwkv, m_w_out, m_norm2_w, m_w_ffn_in, m_conv_ffn, m_w_ffn_out, m_norm_f_w, v_w_ada, v_b_ada, v_norm1_w, v_w_in, v_conv_gdn, v_a_log, v_dt_bias, v_onorm_gdn, v_w_branch_gdn, v_mu_rwkv, v_w0, v_w2, v_a0, v_a2, v_g2, v_k_k, v_k_a, v_r_k, v_lnx_w, v_lnx_b, v_w_branch_rwkv, v_w_out, v_norm2_w, v_w_ffn_in, v_conv_ffn, v_w_ffn_out, v_norm_f_w):
    args = dict(locals())
    W = {n: args[n] for n in _ORDER}
    Mo = {n: args["m_" + n] for n in _ORDER}
    Vo = {n: args["v_" + n] for n in _ORDER}
    shapes = {n: W[n].shape for n in _ORDER}
    sq = lambda a: a.reshape(a.shape[-2:]) if a.ndim == 3 else a.reshape(1, -1)
    row = lambda a: a.reshape(1, -1)

    xi, yi, ci = lax.axis_index("x"), lax.axis_index("y"), lax.axis_index("c")
    dev = 4 * xi + 2 * yi + ci
    chip = 2 * xi + yi

    x2 = x[0]
    tgt = loss_target[0]
    T, D = x2.shape
    N = T // CHUNK
    tt_l = _pick(T, 256, CHUNK)
    tt_h = _pick(T, 128, CHUNK)

    mid_shapes = [shapes[n][1:] for n in _MID]
    small_blk = _pack([c] + [W[n] for n in _MID], 8, f32)
    small_all = _ag8("gather_c_mid", small_blk)
    c_all = small_all[:, 0, :]
    per_chip = small_all[0::2].reshape(4, -1)[:, D:]
    mid = [dict(zip(_MID, _unpack(per_chip[j], mid_shapes))) for j in range(4)]
    catm = lambda n: jnp.concatenate([mid[j][n] for j in range(4)], axis=1)
    conv_gdn_f, conv_ffn_f = catm("conv_gdn"), catm("conv_ffn")
    w2f, a2f, g2f = catm("w2"), catm("a2"), catm("g2")

    (win_s,) = _run_job("w_in_all_gather", _ag4_job([sq(W["w_in"]).astype(bf16)]))
    later_weights = _ag4_job([sq(W[n]).astype(bf16) for n in _BIG[1:]])
    win_p = _win_pad(win_s)
    zpad = lambda a, top, bot: jnp.pad(a, ((top, bot), (0, 0)))
    w2p, a2p, g2p = zpad(w2f, 0, 64), zpad(a2f, 64, 0), zpad(g2f, 0, 96)

    ncol = shapes["w_ada"][2]
    b_cols = lax.dynamic_slice(sq(W["b_ada"]), (0, chip * ncol), (1, ncol))
    cond16, mod_cols = _ada_fwd(jnp.pad(c_all, ((0, 8), (0, 0))), sq(W["w_ada"]), b_cols)
    mod_all = _ag8("gather_mod", mod_cols[:8])
    mod_mine = lax.dynamic_slice(mod_all[0::2], (0, dev, 0), (4, 1, ncol)).reshape(1, 4 * ncol)
    shift1, scale1, gate1, shift2, scale2, gate2 = [mod_mine[:, i * D:(i + 1) * D] for i in range(6)]

    seg = _seg_matrix(WIDTH, HEAD_DIM)
    norm1 = [sq(W["norm1_w"]), shift1, scale1]
    h1 = _stage_fwd("norm_mod1", _fn_norm_mod, [_whole(x2)], norm1, [(D, bf16)], tt_l)[0]
    p = _matmul("in_proj", h1, win_p, "nn")

    cgq = [row(conv_gdn_f[j, part * WIDTH:(part + 1) * WIDTH]) for part in range(3) for j in range(4)]
    lane_pad = lambda a: jnp.pad(row(a), ((0, 0), (8, LANES - 16)))
    gdn_pre_ps = cgq + [lane_pad(W["a_log"]), lane_pad(W["dt_bias"]), seg, _chunk_tri(tt_h, CHUNK)]
    gdn_pre_ts = [(p, WIDTH, part, (0, 1, 2, 3)) for part in range(3)] + [(p, LANES, 29, None)]
    q_, k_, v_, beta_t, gc_t = _stage_fwd("gdn_pre", _fn_gdn_pre, gdn_pre_ts, gdn_pre_ps,
                                          [(WIDTH, f32)] * 3 + [(LANES, f32)] * 2, tt_h)
    o_, gdn_hist, gdn_inv, wbg_s, wbr_s, wout_s, wfi_s, wfo_s = _scan_fwd(
        "gdn_scan", _gdn_chunk, [q_, k_, v_], [beta_t, gc_t], job=later_weights)
    wout_f = wout_s.reshape(D, D)
    wfo = wfo_s.reshape(D_FF, D)
    ow512 = jnp.tile(row(W["onorm_gdn"]), (1, HEADS))
    gdn_post_ts = [_whole(o_), (p, WIDTH, 6, None)]
    ya = _stage_fwd("gdn_post", _fn_gdn_post, gdn_post_ts, [ow512, seg], [(WIDTH, bf16)], tt_l)[0]

    mu = sq(W["mu_rwkv"])
    rw_ps = [mu[:, 0:512], mu[:, 512:1024], mu[:, 1024:1536], mu[:, 1536:1664], jnp.pad(mu[:, 1664:1824], ((0, 0), (0, 96))),
             sq(W["w0"]), w2p, sq(W["a0"]), a2p, g2p, sq(W["k_k"]), sq(W["k_a"]), seg]
    rw_ts = [(p, WIDTH, 3, (0, 1)), (p, WIDTH, 4, (0, 1)), (p, WIDTH, 5, (0, 1)), (p, LANES, 28, (0, 1)),
             (p, 256, 15, (0, 1))]
    rw_out = _stage_fwd("rwkv_pre", _fn_rwkv_pre, rw_ts, rw_ps, [(WIDTH, f32)] * 7, tt_h)
    r_, lw_, k2_, vv_, na_, b_, g_ = rw_out
    rw_ins = [r_, lw_, k2_, vv_, na_, b_]
    y_, rw_hist, rw_inv = _scan_fwd("rwkv_scan", _rwkv_chunk, rw_ins)
    rwp_ps = [sq(W["lnx_w"]), sq(W["lnx_b"]), row(W["r_k"]), seg]
    rwp_ts = [_whole(y_), _whole(r_), _whole(k2_), _whole(vv_), _whole(g_)]
    yb = _stage_fwd("rwkv_post", _fn_rwkv_post, rwp_ts, rwp_ps, [(WIDTH, bf16)], tt_l)[0]

    big_a = _matmul("branch_gdn", ya, wbg_s, "nn", shards=4)
    big_b = _matmul("branch_rwkv", yb, wbr_s, "nn", shards=4)
    merge_ts = [(p, D, 4, None), (p, D, 5, None), _whole(big_a), _whole(big_b)]
    merged = _stage_fwd("merge", _fn_merge, merge_ts, [], [(D, bf16)], tt_l)[0]
    mo = _matmul("out_proj", merged, wout_f, "nn")
    norm2 = [gate1, sq(W["norm2_w"]), shift2, scale2]
    x1, h2 = _stage_fwd("resid_norm_mod2", _fn_resid_norm_mod, [_whole(x2), _whole(mo)], norm2, [(D, f32), (D, bf16)], tt_l)
    f = _matmul("ffn_in", h2, wfi_s, "nn", shards=4, tm=512, tn=1408)
    cg_ps = [row(conv_ffn_f[j]) for j in range(3)]
    cg_ts = [(f, D_FF, 0, (0, 1, 2)), (f, D_FF, 1, None)]
    act = _stage_fwd("convglu", _fn_convglu, cg_ts, cg_ps, [(D_FF, bf16)], tt_h)[0]
    fo = _matmul("ffn_out", act, wfo, "nn", tk=1408)

    dx1_a, dfo, dgate2, dnormf, loss_part = _final_stage(x1, fo, tgt, gate2, row(W["norm_f_w"]), tt_l)

    dact = _matmul("d_act", dfo, wfo, "nt", tm=512, tn=1408)
    g_wfo = _matmul("g_ffn_out", act, dfo, "tn", tm=1408)
    _, dcf, df = _stage_bwd("convglu_bwd", _fn_convglu, cg_ts, cg_ps, [[_whole(dact)]], tt_h, [True] * 2, [True] * 3,
                            joint=([0, 1], 2 * D_FF, 0, None))
    dh2 = _matmul("d_h2", df, wfi_s, "nt", shards=4, tm=512, tk=1408)
    g_wfi = _matmul("g_ffn_in", h2, df, "tn", shards=4, tn=1408)
    (dx_a, dmo), (dgate1, dnorm2, dshift2, dscale2), _ = _stage_bwd(
        "resid_norm_mod2_bwd", _fn_resid_norm_mod, [_whole(x2), _whole(mo)], norm2,
        [[_whole(dx1_a)], [_whole(dh2)]], tt_l, [True, True], [True] * 4)
    dmerged = _matmul("d_merged", dmo, wout_f, "nt")
    g_wout = _matmul("g_out_proj", merged, dmo, "tn")
    (dbig_a, dbig_b), _, dp = _stage_bwd("merge_bwd", _fn_merge, merge_ts, [], [[_whole(dmerged)]], tt_l, [True] * 4, [],
                                         joint=([0, 1], p.shape[1], 2, None))
    dya = _matmul("d_ya", dbig_a, wbg_s, "nt", shards=4)
    g_wbg = _matmul("g_branch_gdn", ya, dbig_a, "tn", shards=4)
    dyb = _matmul("d_yb", dbig_b, wbr_s, "nt", shards=4)
    g_wbr = _matmul("g_branch_rwkv", yb, dbig_b, "tn", shards=4)

    (dy_, dr_p, dk2_p, dv_p, dg_p), (dlnxw, dlnxb, drk), _ = _stage_bwd(
        "rwkv_post_bwd", _fn_rwkv_post, rwp_ts, rwp_ps, [[_whole(dyb)]], tt_l, [True] * 5, [True, True, True, False])
    gs_a = [g_wbg, g_wbr, g_wout.reshape(4, D // 4, D), g_wfi, g_wfo.reshape(4, D_FF // 4, D)]
    pairs_a = [_add_half("grads_pair_sum%d" % (t + 1), g, r_, ci) for t, (g, r_) in enumerate(zip(gs_a, _rs_sibling_swap(gs_a, "a")))]
    dr_c, dlw_c, dk2_c, dv_c, dna_c, db_c, *others_a = _scan_bwd(
        "rwkv_scan_bwd", _rwkv_chunk, rw_ins, [], rw_hist, rw_inv, dy_, job=_chip_exchange_job([pb for _, pb in pairs_a]))
    rw_cots = [[_whole(dr_p), _whole(dr_c)], [_whole(dlw_c)], [_whole(dk2_p), _whole(dk2_c)],
               [_whole(dv_p), _whole(dv_c)], [_whole(dna_c)], [_whole(db_c)], [_whole(dg_p)]]
    (dl_, dg_), rw_dp, dp = _stage_bwd("rwkv_pre_bwd", _fn_rwkv_pre, rw_ts, rw_ps, rw_cots, tt_h, [True] * 5,
                                       [True] * 12 + [False], joint=([0, 1, 2], p.shape[1], 1, dp))
    dmu_r, dmu_k, dmu_v, dmu_l, dmu_g, dw0, dw2p, da0, da2p, dg2p, dkk, dka = rw_dp

    (do_,), (dow512,), dp = _stage_bwd("gdn_post_bwd", _fn_gdn_post, gdn_post_ts, [ow512, seg], [[_whole(dya)]], tt_l,
                                       [True, True], [True, False], joint=([1], p.shape[1], 6, dp))
    d_gdn = _scan_bwd("gdn_scan_bwd", _gdn_chunk, [q_, k_, v_], [beta_t, gc_t], gdn_hist, gdn_inv, do_)
    gdn_cots = [[_whole(a)] for a in d_gdn]
    (dba,), gdn_dp, dp = _stage_bwd("gdn_pre_bwd", _fn_gdn_pre, gdn_pre_ts, gdn_pre_ps, gdn_cots, tt_h, [True] * 4,
                                    [True] * 14 + [False, False], joint=([0, 1, 2], p.shape[1], 0, dp))
    dp = lax.dynamic_update_slice(dp, jnp.concatenate([dl_, dba, dg_], axis=1), (0, 3584))
    g_win_s = _win_unpad_shards(_matmul("g_in_proj", h1, dp, "tn"), shapes["w_in"][2])
    pair_win = _add_half("grads_pair_sum0", g_win_s, _rs_sibling_swap([g_win_s], "b")[0], ci)
    dh1, others_win = _matmul("d_h1", dp, win_p, "nt", job=_chip_exchange_job([pair_win[1]]))
    (grad_x,), (dnorm1, dshift1, dscale1), _ = _stage_bwd("norm_mod1_bwd", _fn_norm_mod_and_x, [_whole(x2)], norm1,
                                                          [[_whole(dh1)], [_whole(dx_a)]], tt_l, [True], [True] * 3)

    dmod = jnp.concatenate([dshift1, dscale1, dgate1, dshift2, dscale2, dgate2], axis=1)
    g_conv_gdn = jnp.concatenate([jnp.concatenate([gdn_dp[4 * part + j] for part in range(3)], axis=1) for j in range(4)], axis=0)
    g_conv_ffn = jnp.concatenate(dcf, axis=0)
    g_mu = jnp.concatenate([dmu_r, dmu_k, dmu_v, dmu_l, dmu_g[:, :160]], axis=1)
    small_parts = {"b_ada": dmod, "norm1_w": dnorm1, "a_log": gdn_dp[12][:, 8:16], "dt_bias": gdn_dp[13][:, 8:16],
                   "mu_rwkv": g_mu, "w0": dw0, "a0": da0, "k_k": dkk, "k_a": dka, "r_k": drk, "lnx_w": dlnxw,
                   "lnx_b": dlnxb, "norm2_w": dnorm2, "norm_f_w": dnormf}
    small_names = [n for n in _SMALL if n != "onorm_gdn"]
    mid_full = [g_conv_gdn, g_conv_ffn, dw2p[0:64], da2p[64:128], dg2p[0:160]]
    body_rows = _pack([small_parts[n] for n in small_names] + [loss_part[:, 0:1]] + mid_full, 1, f32)
    head_row = body_rows.shape[0]
    small_g = jnp.concatenate([body_rows, jnp.pad(dow512, ((0, 0), (0, PACK_W - WIDTH)))], axis=0)
    small_g = jnp.pad(small_g, ((0, -small_g.shape[0] % 8), (0, 0)))
    small_all_g = _ag8("gather_small_grads", small_g)
    small_sum, head_sum = _sum_devices(small_all_g, head_row)
    small_shapes = [shapes[n][1:] if n != "norm_f_w" else shapes[n] for n in small_names]
    un = _unpack(small_sum, small_shapes + [(1,)] + [g.shape for g in mid_full])
    small_grads = dict(zip(small_names, un))
    loss = un[len(small_names)].reshape(())
    small_grads["onorm_gdn"] = head_sum[0, 0:HEAD_DIM]
    for n, g in zip(_MID, un[len(small_names) + 1:]):
        wcols = shapes[n][2]
        small_grads[n] = lax.dynamic_slice(g, (0, chip * wcols), (g.shape[0], wcols))

    dmod_all = small_all_g[:, 0:6, :].reshape(8, 6 * PACK_W)
    dmod_cols = lax.dynamic_slice(dmod_all, (0, chip * ncol), (8, ncol))
    g_wada = _matmul("g_w_ada", cond16, jnp.pad(dmod_cols, ((0, 8), (0, 0))), "tn")

    pairs = [pair_win] + pairs_a
    others = [others_win] + others_a
    halves = [_sum_chip("grads_chip_sum%d" % t, pf, o_, chip, ci) for t, ((pf, _), o_) in enumerate(zip(pairs, others))]
    big_grads = dict(zip(_BIG, _rs_sibling_join(halves)))

    res = {tag: {} for tag in ("grad", "delta", "new_m", "new_v")}

    def put(n, g, d, m_, v_):
        for tag, val in zip(("grad", "delta", "new_m", "new_v"), (g, d, m_, v_)):
            res[tag][n] = val.reshape(shapes[n])

    for n in _BIG:
        put(n, big_grads[n], *_adamw("adamw_" + n, sq(W[n]), big_grads[n], sq(Mo[n]), sq(Vo[n])))
    put("w_ada", g_wada, *_adamw("adamw_w_ada", sq(W["w_ada"]), g_wada, sq(Mo["w_ada"]), sq(Vo["w_ada"])))
    rest = _SMALL + _MID
    pk = lambda d: _pack([d[n] for n in rest], 8, f32)
    sg = pk(small_grads)
    sm = _adamw("adamw_small", pk(W), sg, pk(Mo), pk(Vo))
    for tag, buf in zip(("grad", "delta", "new_m", "new_v"), (sg,) + tuple(sm)):
        res[tag].update(zip(rest, _unpack(buf, [shapes[n] for n in rest])))
    outs = [loss, grad_x.reshape(x.shape)]
    for tag in ("grad", "delta", "new_m", "new_v"):
        outs += [res[tag][n] for n in _ORDER]
    return tuple(outs)
```

```python
import functools
import math

import numpy as np
import jax
import jax.numpy as jnp
from jax import lax
from jax.experimental import pallas as pl
from jax.experimental.pallas import tpu as pltpu

f32 = jnp.float32
bf16 = jnp.bfloat16

LANES = 128
HEADS = 8
HEAD_DIM = 64
WIDTH = HEADS * HEAD_DIM
CHUNK = 64
D_FF = 2816
NORM_EPS = 1e-6
LNX_EPS = 64e-5
PACK_W = 1024
MESH_ID = pl.DeviceIdType.MESH

ADAM_LR, ADAM_B1, ADAM_B2, ADAM_EPS, ADAM_WD, ADAM_STEP = 0.001, 0.9, 0.999, 1e-08, 0.01, 10


def _pick(n, target, mult):
    if n <= target:
        return n
    best = None
    for t in range(mult, target + 1, mult):
        if n % t == 0:
            best = t
    assert best is not None, (n, target, mult)
    return best


def _split_bf16(x, n):
    parts, r = [], x
    for i in range(n):
        p = r.astype(bf16)
        parts.append(p)
        if i + 1 < n:
            r = r - p.astype(f32)
    return parts


def _xdot_r_impl(x, m, n, dims):
    acc = None
    for p in _split_bf16(x, n):
        t = lax.dot_general(p, m, dims, preferred_element_type=f32)
        acc = t if acc is None else acc + t
    return acc


def _make_xdot_r(n):
    nn = (((1,), (0,)), ((), ()))
    nt = (((1,), (1,)), ((), ()))

    @jax.custom_vjp
    def xdot(x, m):
        return _xdot_r_impl(x, m, n, nn)

    def fwd(x, m):
        return _xdot_r_impl(x, m, n, nn), m

    def bwd(m, ct):
        return _xdot_r_impl(ct, m, n, nt), jnp.zeros_like(m)

    xdot.defvjp(fwd, bwd)
    return xdot


_segsum = _make_xdot_r(2)


def _xdot_l_impl(m, x, n, dims):
    acc = None
    for p in _split_bf16(x, n):
        t = lax.dot_general(m, p, dims, preferred_element_type=f32)
        acc = t if acc is None else acc + t
    return acc


@jax.custom_vjp
def _xdot_l(m, x):
    return _xdot_l_impl(m, x, 3, (((1,), (0,)), ((), ())))


def _xdot_l_fwd(m, x):
    return _xdot_l(m, x), m


def _xdot_l_bwd(m, ct):
    return jnp.zeros_like(m), _xdot_l_impl(m, ct, 3, (((0,), (0,)), ((), ())))


_xdot_l.defvjp(_xdot_l_fwd, _xdot_l_bwd)


@jax.custom_vjp
def _bdot(x, w):
    return jnp.dot(x.astype(bf16), w.astype(bf16), preferred_element_type=f32)


def _bdot_fwd(x, w):
    return _bdot(x, w), (x, w)


def _bdot_bwd(res, ct):
    x, w = res
    c = ct.astype(bf16)
    dx = lax.dot_general(c, w.astype(bf16), (((1,), (1,)), ((), ())), preferred_element_type=f32)
    dw = lax.dot_general(x.astype(bf16), c, (((0,), (0,)), ((), ())), preferred_element_type=f32)
    return dx, dw


_bdot.defvjp(_bdot_fwd, _bdot_bwd)


def _silu(x):
    return x * jax.nn.sigmoid(x)


def _softplus(x):
    return jnp.maximum(x, 0.0) + jnp.log(1.0 + jnp.exp(-jnp.abs(x)))


def _rms(x, w, eps):
    return x * lax.rsqrt(jnp.mean(x * x, axis=-1, keepdims=True) + eps) * w


def _seg_matrix(width, seg):
    i = np.arange(width)
    return jnp.asarray((i[:, None] // seg) == (i[None, :] // seg), dtype=bf16)


def _chunk_tri(rows, chunk):
    i = np.arange(rows)
    return jnp.asarray(((i[:, None] // chunk) == (i[None, :] // chunk)) & (i[:, None] >= i[None, :]), dtype=bf16)


HALO = 8


def _full_spec(shape):
    nd = len(shape)
    return pl.BlockSpec(shape, lambda i: (0,) * nd)


def _entry_specs(entries, tt, block_of):
    specs, ops = [], []
    for arr, w, ci, shifts in entries:
        specs.append(pl.BlockSpec((tt, w), lambda i, ci=ci: (block_of(i), ci)))
        ops.append(arr)
        if shifts:
            specs.append(pl.BlockSpec((HALO, w), lambda i, ci=ci: (jnp.maximum(block_of(i) * (tt // HALO) - 1, 0), ci)))
            ops.append(arr)
    return specs, ops


def _load_entries(entries, refs, first):
    tiles, k = [], 0
    for _, w, _, shifts in entries:
        x = refs[k][...].astype(f32)
        k += 1
        if not shifts:
            tiles.append(x)
            continue
        halo = jnp.where(first, 0.0, refs[k][...].astype(f32))
        k += 1
        row = lax.broadcasted_iota(jnp.int32, (HALO, w), 0)
        for s in shifts:
            if s == 0:
                tiles.append(x)
                continue
            r = pltpu.roll(x, s, 0)
            head = jnp.where(row < s, pltpu.roll(halo, s, 0), r[0:HALO])
            tiles.append(jnp.concatenate([head, r[HALO:]], axis=0))
    return tiles


def _unshift_sum(grads, shifts, carry, tt):
    w = grads[0].shape[1]
    row = lax.broadcasted_iota(jnp.int32, (tt, w), 0)
    row8 = lax.broadcasted_iota(jnp.int32, (HALO, w), 0)
    dx, out = None, jnp.zeros((HALO, w), f32)
    for d, s in zip(grads, shifts):
        if s == 0:
            part = d
        else:
            part = jnp.where(row < tt - s, pltpu.roll(d, tt - s, 0), 0.0)
            out = out + jnp.where(row8 >= HALO - s, pltpu.roll(d[0:HALO], HALO - s, 0), 0.0)
        dx = part if dx is None else dx + part
    return jnp.concatenate([dx[:tt - HALO], dx[tt - HALO:] + carry], axis=0), out


def _stage_fwd(name, fn, tiles, params, outs, tt):
    rows = tiles[0][0].shape[0]
    npar = len(params)
    specs, ops = _entry_specs(tiles, tt, lambda i: i)
    nin = len(ops)

    def body(*refs):
        ts = _load_entries(tiles, refs[:nin], pl.program_id(0) == 0)
        ps = [r[...] for r in refs[nin:nin + npar]]
        res = fn(ps, ts)
        for r, v in zip(refs[nin + npar:], res):
            r[...] = v.astype(r.dtype)

    return pl.pallas_call(
        body, grid=(rows // tt,),
        in_specs=specs + [_full_spec(p.shape) for p in params],
        out_specs=[pl.BlockSpec((tt, w), lambda i: (i, 0)) for (w, _) in outs],
        out_shape=[jax.ShapeDtypeStruct((rows, w), dt) for (w, dt) in outs],
        compiler_params=pltpu.CompilerParams(dimension_semantics=("parallel",)),
        name=name,
    )(*ops, *params)


def _stage_bwd(name, fn, tiles, params, cots, tt, tile_grad, param_grad, joint=None, dtypes=None):
    rows = tiles[0][0].shape[0]
    nblk = rows // tt
    npar = len(params)
    block_of = lambda i: nblk - 1 - i
    specs, ops = _entry_specs(tiles, tt, block_of)
    nin = len(ops)
    flat_cots = [c for group in cots for c in group]
    groups = [len(g) for g in cots]
    ncot = len(flat_cots)
    counts = [len(e[3]) if e[3] else 1 for e in tiles]
    dt_entries = [e for e, g in zip(tiles, tile_grad) if g]
    dp_shapes = [p.shape for p, g in zip(params, param_grad) if g]
    ndt = len(dt_entries)
    carry_w = [e[1] for e in dt_entries if e[3]]
    flags = [g for g, n in zip(tile_grad, counts) for _ in range(n)]
    members, j_width, j_cidx, j_buf = joint if joint else ([], 0, 0, None)
    solo = [k for k in range(ndt) if k not in members]
    nsolo, njoint, nbuf = len(solo), int(bool(members)), int(j_buf is not None)
    j_block = sum(dt_entries[k][1] for k in members)
    dtypes = list(dtypes) if dtypes else [f32] * (nsolo + njoint)

    def body(*refs):
        i = pl.program_id(0)
        p_refs = refs[nin:nin + npar]
        c_refs = refs[nin + npar:nin + npar + ncot]
        base = nin + npar + ncot + nbuf
        dt_refs = refs[base:base + nsolo]
        joint_refs = refs[base + nsolo:base + nsolo + njoint]
        dp_refs = refs[base + nsolo + njoint:base + nsolo + njoint + len(dp_shapes)]
        carry_refs = refs[base + nsolo + njoint + len(dp_shapes):]
        ts = _load_entries(tiles, refs[:nin], block_of(i) == 0)
        ps = [r[...] for r in p_refs]

        def f(dp, dt):
            dp, dt = iter(dp), iter(dt)
            pp = [next(dp) if g else p for p, g in zip(ps, param_grad)]
            tl = [next(dt) if g else t for t, g in zip(ts, flags)]
            return fn(pp, tl)

        _, vjp = jax.vjp(f, [p for p, g in zip(ps, param_grad) if g], [t for t, g in zip(ts, flags) if g])
        cs, j = [], 0
        for n in groups:
            acc = c_refs[j][...].astype(f32)
            for q in range(1, n):
                acc = acc + c_refs[j + q][...].astype(f32)
            cs.append(acc)
            j += n
        gp, gt = vjp(cs)

        @pl.when(i == 0)
        def _():
            for r in dp_refs:
                r[...] = jnp.zeros_like(r)
            for r in carry_refs:
                r[...] = jnp.zeros_like(r)

        gt, k, kc, dxs = list(gt), 0, 0, []
        for e, n in zip(dt_entries, [n for n, g in zip(counts, tile_grad) if g]):
            if e[3]:
                dx, out = _unshift_sum(gt[k:k + n], e[3], carry_refs[kc][...], tt)
                carry_refs[kc][...] = out
                kc += 1
            else:
                dx = gt[k]
            dxs.append(dx)
            k += n
        for r, k in zip(dt_refs, solo):
            r[...] = dxs[k].astype(r.dtype)
        off = 0
        for k in members:
            w = dt_entries[k][1]
            joint_refs[0][:, off:off + w] = dxs[k].astype(joint_refs[0].dtype)
            off += w
        for r, v in zip(dp_refs, gp):
            r[...] += v

    res = pl.pallas_call(
        body, grid=(nblk,),
        in_specs=specs + [_full_spec(p.shape) for p in params]
        + [pl.BlockSpec((tt, w), lambda i, ci=ci: (block_of(i), ci)) for (_, w, ci, *_) in flat_cots]
        + [pl.BlockSpec(memory_space=pl.ANY)] * nbuf,
        out_specs=[pl.BlockSpec((tt, dt_entries[k][1]), lambda i: (block_of(i), 0)) for k in solo]
        + [pl.BlockSpec((tt, j_block), lambda i: (block_of(i), j_cidx))] * njoint
        + [_full_spec(s) for s in dp_shapes],
        out_shape=[jax.ShapeDtypeStruct((rows, dt_entries[k][1]), dt) for k, dt in zip(solo, dtypes)]
        + [jax.ShapeDtypeStruct((rows, j_width), dtypes[-1])] * njoint
        + [jax.ShapeDtypeStruct(s, f32) for s in dp_shapes],
        scratch_shapes=[pltpu.VMEM((HALO, w), f32) for w in carry_w],
        input_output_aliases={nin + npar + ncot: nsolo} if nbuf else {},
        compiler_params=pltpu.CompilerParams(dimension_semantics=("arbitrary",)),
        name=name,
    )(*ops, *params, *[c[0] for c in flat_cots], *([j_buf] if nbuf else []))
    res = list(res)
    return res[:nsolo], res[nsolo + njoint:], (res[nsolo] if njoint else None)


def _whole(a):
    return (a, a.shape[1], 0, None)


def _matmul(name, a, b, mode, out_dtype=f32, tm=1024, tn=1024, tk=1024, shards=1, job=None):
    S = shards
    if mode == "nn":
        M, K = a.shape
        w = b.shape[-1]
    elif mode == "nt":
        M = a.shape[0]
        if S > 1:
            _, N, w = b.shape
            K = S * w
        else:
            N, K = b.shape
            w = K
    else:
        K, M = a.shape
        w = b.shape[1] // S
    if mode != "nt":
        N = S * w
    tm = _pick(M, tm, LANES)
    if mode == "nt":
        tn = _pick(N, tn, LANES)
        tk = _pick(w, tk, LANES)
    else:
        tn = _pick(w, tn, LANES)
        tk = _pick(K, tk, LANES if mode == "nn" else 16)
    nk = K // tk
    nb = w // (tk if mode == "nt" else tn)
    if mode == "nn":
        a_spec = pl.BlockSpec((tm, tk), lambda i, j, k: (i, k))
        if S > 1:
            b_spec = pl.BlockSpec((1, tk, tn), lambda i, j, k: (j // nb, k, j % nb))
        else:
            b_spec = pl.BlockSpec((tk, tn), lambda i, j, k: (k, j))
        dims = (((1,), (0,)), ((), ()))
    elif mode == "nt":
        a_spec = pl.BlockSpec((tm, tk), lambda i, j, k: (i, k))
        if S > 1:
            b_spec = pl.BlockSpec((1, tn, tk), lambda i, j, k: (k // nb, j, k % nb))
        else:
            b_spec = pl.BlockSpec((tn, tk), lambda i, j, k: (j, k))
        dims = (((1,), (1,)), ((), ()))
    else:
        a_spec = pl.BlockSpec((tk, tm), lambda i, j, k: (k, i))
        b_spec = pl.BlockSpec((tk, tn), lambda i, j, k: (k, j))
        dims = (((0,), (0,)), ((), ()))
    if mode == "tn" and S > 1:
        o_spec = pl.BlockSpec((1, tm, tn), lambda i, j, k: (j // nb, i, j % nb))
        o_shape = (S, M, w)
    else:
        o_spec = pl.BlockSpec((tm, tn), lambda i, j, k: (i, j))
        o_shape = (M, N)
    b_lead = S > 1 and mode != "tn"
    o_lead = S > 1 and mode == "tn"

    j_ins, j_outs, j_sems = _job_parts(job)
    nji, njo = len(j_ins), len(j_outs)
    grid = (M // tm, N // tn, nk)

    def run_job(refs):
        step = (pl.program_id(0) * grid[1] + pl.program_id(1)) * grid[2] + pl.program_id(2)
        _job_steps(job, refs[2:2 + nji], refs[3 + nji:3 + nji + njo], refs[len(refs) - 2:], step,
                   grid[0] * grid[1] * grid[2] - 1)

    def body(*refs):
        a_ref, b_ref, o_ref, acc_ref = refs[0], refs[1], refs[2 + nji], refs[3 + nji + njo]
        run_job(refs)
        k = pl.program_id(2)

        @pl.when(k == 0)
        def _():
            acc_ref[...] = jnp.zeros_like(acc_ref)

        bv = b_ref[0] if b_lead else b_ref[...]
        acc_ref[...] += lax.dot_general(a_ref[...].astype(bf16), bv.astype(bf16), dims, preferred_element_type=f32)

        @pl.when(k == nk - 1)
        def _():
            if o_lead:
                o_ref[0] = acc_ref[...].astype(o_ref.dtype)
            else:
                o_ref[...] = acc_ref[...].astype(o_ref.dtype)

    def body_one_step(*refs):
        a_ref, b_ref, o_ref = refs[0], refs[1], refs[2 + nji]
        run_job(refs)
        bv = b_ref[0] if b_lead else b_ref[...]
        res = lax.dot_general(a_ref[...].astype(bf16), bv.astype(bf16), dims, preferred_element_type=f32)
        if o_lead:
            o_ref[0] = res.astype(o_ref.dtype)
        else:
            o_ref[...] = res.astype(o_ref.dtype)

    res = pl.pallas_call(
        body if nk > 1 else body_one_step, grid=grid,
        in_specs=[a_spec, b_spec] + [_ANY] * nji,
        out_specs=[o_spec] + [_ANY] * njo,
        out_shape=[jax.ShapeDtypeStruct(o_shape, out_dtype)] + j_outs,
        scratch_shapes=([pltpu.VMEM((tm, tn), f32)] if nk > 1 else []) + j_sems,
        compiler_params=pltpu.CompilerParams(
            dimension_semantics=("arbitrary",) * 3 if job else ("parallel", "parallel", "arbitrary")),
        name=name,
    )(a, b, *j_ins)
    return res if job else res[0]


def _make_bmm(precision):
    if precision is None:
        cast, kw = (lambda v: v.astype(bf16)), {}
    else:
        cast, kw = (lambda v: v), {"precision": precision}

    def nn(a, b):
        return jnp.einsum("hij,hjk->hik", cast(a), cast(b), preferred_element_type=f32, **kw)

    def nt(a, b):
        return jnp.einsum("hik,hjk->hij", cast(a), cast(b), preferred_element_type=f32, **kw)

    def tn(a, b):
        return jnp.einsum("hki,hkj->hij", cast(a), cast(b), preferred_element_type=f32, **kw)

    if precision is not None:
        return nn, nt, tn
    nn_v, nt_v, tn_v = jax.custom_vjp(nn), jax.custom_vjp(nt), jax.custom_vjp(tn)
    keep = lambda f: (lambda a, b: (f(a, b), (a, b)))
    nn_v.defvjp(keep(nn), lambda r, ct: (nt(ct, r[1]), tn(r[0], ct)))
    nt_v.defvjp(keep(nt), lambda r, ct: (nn(ct, r[1]), tn(ct, r[0])))
    tn_v.defvjp(keep(tn), lambda r, ct: (nt(r[1], ct), nn(r[0], ct)))
    return nn_v, nt_v, tn_v


_bmm, _bmm_nt, _bmm_tn = _make_bmm(None)
_bmm_exact = _make_bmm(lax.Precision.HIGH)[0]


def _masks(n):
    r = lax.broadcasted_iota(jnp.int32, (n, n), 0)
    c = lax.broadcasted_iota(jnp.int32, (n, n), 1)
    return (r >= c)[None], (r > c)[None], (r == c)[None]


_INV_BLOCK = 8


def _nilpotent_inverse(m, eye):
    p = eye + m
    for _ in range(2):
        m = _bmm(m, m)
        p = p + _bmm(p, m)
    return p


def _neumann_inverse_impl(m):
    n = m.shape[1]
    assert n == _INV_BLOCK * _INV_BLOCK
    r = lax.broadcasted_iota(jnp.int32, (n, n), 0)
    c = lax.broadcasted_iota(jnp.int32, (n, n), 1)
    eye = (r == c).astype(f32)[None]
    inside = jnp.where((r // _INV_BLOCK == c // _INV_BLOCK)[None], m, 0.0)
    d_inv = _nilpotent_inverse(inside, eye)
    return _bmm(_nilpotent_inverse(_bmm(d_inv, m - inside), eye), d_inv)


@jax.custom_vjp
def _neumann_inverse(m):
    return _neumann_inverse_impl(m)


def _neumann_inverse_fwd(m):
    p = _neumann_inverse_impl(m)
    return p, p


def _neumann_inverse_bwd(p, ct):
    return (_bmm_tn(p, _bmm_nt(ct, p)),)


_neumann_inverse.defvjp(_neumann_inverse_fwd, _neumann_inverse_bwd)


@jax.custom_vjp
def _given_inverse(m, p):
    return p


_given_inverse.defvjp(lambda m, p: (p, p), lambda p, ct: (_neumann_inverse_bwd(p, ct)[0], jnp.zeros_like(p)))


def _gdn_chunk(s, q, k, v, beta, gc, gr, gl, p=None):
    n = q.shape[1]
    causal, strict, _ = _masks(n)
    decay = jnp.where(causal, jnp.exp(jnp.where(causal, gc - gr, 0.0)), 0.0)
    kb = k * beta
    vb = v * beta
    lower = jnp.where(strict, _bmm_nt(kb, k) * decay, 0.0)
    t_mat = _neumann_inverse(-lower) if p is None else _given_inverse(-lower, p)
    egc = jnp.exp(gc)
    u = _bmm(t_mat, vb)
    w = _bmm(t_mat, kb * egc)
    attn = jnp.where(causal, _bmm_nt(q, k) * decay, 0.0)
    v_new = u - _bmm(w, s)
    o = _bmm(q * egc, s) + _bmm(attn, v_new)
    k_dec = k * jnp.exp(gl - gc)
    s_new = s * jnp.exp(gl) + _bmm_tn(k_dec, v_new)
    return s_new, o, t_mat


def _rwkv_chunk(s, r, lw, k, v, a, b, p=None):
    n = r.shape[1]
    causal, strict, _ = _masks(n)
    tri = jnp.broadcast_to(causal.astype(f32), (r.shape[0], n, n))
    lc = _bmm_exact(tri, lw)
    ein = jnp.exp(lc)
    eout = jnp.exp(-lc)
    a_t = a * jnp.exp(lc - lw)
    b_t = b * eout
    k_t = k * eout
    r_t = r * ein
    a_ab = jnp.where(strict, _bmm_nt(a_t, b_t), 0.0)
    a_ak = jnp.where(strict, _bmm_nt(a_t, k_t), 0.0)
    inv = _neumann_inverse(a_ab) if p is None else _given_inverse(a_ab, p)
    u = _bmm(inv, _bmm_nt(a_t, s) + _bmm(a_ak, v))
    y = (_bmm_nt(r_t, s) + _bmm(jnp.where(causal, _bmm_nt(r_t, b_t), 0.0), u)
         + _bmm(jnp.where(causal, _bmm_nt(r_t, k_t), 0.0), v))
    e_last = jnp.exp(jnp.sum(lw, axis=1, keepdims=True))
    s_new = s * e_last + _bmm_tn(u, b_t * e_last) + _bmm_tn(v, k_t * e_last)
    return s_new, y, inv


def _heads_in(ref, rows):
    return jnp.stack([ref[rows, h * HEAD_DIM:(h + 1) * HEAD_DIM] for h in range(HEADS)], axis=0)


def _heads_out(ref, rows, val):
    for h in range(HEADS):
        ref[rows, h * HEAD_DIM:(h + 1) * HEAD_DIM] = val[h]


def _gdn_scalars(bt, gt):
    n = bt.shape[0]
    gtt = gt.T
    hs = range(HEADS)
    return [jnp.stack([bt[:, h:h + 1] for h in hs], axis=0),
            jnp.stack([gt[:, HEADS + h:HEADS + h + 1] for h in hs], axis=0),
            jnp.stack([gtt[HEADS + h:HEADS + h + 1, :] for h in hs], axis=0),
            jnp.stack([gt[n - 1:n, HEADS + h:HEADS + h + 1] for h in hs], axis=0)]


def _gdn_scalars_back(dbeta, dgc, dgr, dgl):
    n = dbeta.shape[1]
    lane = lax.broadcasted_iota(jnp.int32, (n, LANES), 1)
    row = lax.broadcasted_iota(jnp.int32, (n, LANES), 0)
    sub = lax.broadcasted_iota(jnp.int32, (LANES, n), 0)
    db = jnp.zeros((n, LANES), f32)
    dg = jnp.zeros((n, LANES), f32)
    dgt = jnp.zeros((LANES, n), f32)
    for h in range(HEADS):
        db = jnp.where(lane == h, dbeta[h], db)
        dg = jnp.where(lane == HEADS + h, dgc[h] + jnp.where(row == n - 1, dgl[h], 0.0), dg)
        dgt = jnp.where(sub == HEADS + h, dgr[h], dgt)
    return [db, dg + dgt.T]


SCAN_GROUP = 2


def _scan_steps(t):
    g = SCAN_GROUP if (t // CHUNK) % SCAN_GROUP == 0 else 1
    return g, t // (CHUNK * g)


def _scan_spec(width, g, n, reverse):
    if reverse:
        return pl.BlockSpec((g * CHUNK, width), lambda i: (n - 1 - i, 0))
    return pl.BlockSpec((g * CHUNK, width), lambda i: (i, 0))


def _hist_spec(g, n, reverse):
    blk = (g, HEADS, HEAD_DIM, HEAD_DIM)
    if reverse:
        return pl.BlockSpec(blk, lambda i: (n - 1 - i, 0, 0, 0))
    return pl.BlockSpec(blk, lambda i: (i, 0, 0, 0))


def _job_parts(job):
    if job is None:
        return [], [], []
    sems = [pltpu.SemaphoreType.DMA((job["nsem"],)), pltpu.SemaphoreType.DMA((job["nsem"],))]
    return list(job["ins"]), list(job["out_shapes"]), sems


def _job_steps(job, in_refs, out_refs, sems, step, last):
    if job is None:
        return

    @pl.when(step == 0)
    def _():
        job["start"](in_refs, out_refs, *sems)

    @pl.when(step == last)
    def _():
        job["finish"](in_refs, out_refs, *sems)


def _scan_fwd(name, fn, rows_in, scal_in=(), job=None):
    t = rows_in[0].shape[0]
    grp, n = _scan_steps(t)
    nr, ns = len(rows_in), len(scal_in)
    j_ins, j_outs, j_sems = _job_parts(job)
    nji, njo = len(j_ins), len(j_outs)

    def body(*refs):
        o_ref, sh_ref, ph_ref = refs[nr + ns + nji:nr + ns + nji + 3]
        s_scr = refs[nr + ns + nji + 3 + njo]
        _job_steps(job, refs[nr + ns:nr + ns + nji], refs[nr + ns + nji + 3:nr + ns + nji + 3 + njo],
                   refs[nr + ns + nji + 3 + njo + 1:], pl.program_id(0), n - 1)

        @pl.when(pl.program_id(0) == 0)
        def _():
            s_scr[...] = jnp.zeros_like(s_scr)

        s = s_scr[...]
        for sub in range(grp):
            rows = slice(sub * CHUNK, (sub + 1) * CHUNK)
            sh_ref[sub] = s
            ins = [_heads_in(r, rows) for r in refs[:nr]]
            if ns:
                ins += _gdn_scalars(*[r[rows, :] for r in refs[nr:nr + ns]])
            s, o, p = fn(s, *ins)
            _heads_out(o_ref, rows, o)
            ph_ref[sub] = p
        s_scr[...] = s

    return pl.pallas_call(
        body, grid=(n,),
        in_specs=[_scan_spec(a.shape[1], grp, n, False) for a in (*rows_in, *scal_in)] + [_ANY] * nji,
        out_specs=[_scan_spec(WIDTH, grp, n, False), _hist_spec(grp, n, False), _hist_spec(grp, n, False)] + [_ANY] * njo,
        out_shape=[jax.ShapeDtypeStruct((t, WIDTH), f32)]
        + [jax.ShapeDtypeStruct((t // CHUNK, HEADS, HEAD_DIM, HEAD_DIM), f32)] * 2 + j_outs,
        scratch_shapes=[pltpu.VMEM((HEADS, HEAD_DIM, HEAD_DIM), f32)] + j_sems,
        compiler_params=pltpu.CompilerParams(dimension_semantics=("arbitrary",)),
        name=name,
    )(*rows_in, *scal_in, *j_ins)


def _scan_bwd(name, fn, rows_in, scal_in, s_hist, p_hist, d_out, job=None):
    t = rows_in[0].shape[0]
    grp, n = _scan_steps(t)
    nr, ns = len(rows_in), len(scal_in)
    j_ins, j_outs, j_sems = _job_parts(job)
    nji, njo = len(j_ins), len(j_outs)

    def body(*refs):
        sh_ref, ph_ref, do_ref = refs[nr + ns:nr + ns + 3]
        base = nr + ns + 3 + nji
        g_refs = refs[base:base + nr + ns]
        ds_scr = refs[base + nr + ns + njo]
        _job_steps(job, refs[nr + ns + 3:base], refs[base + nr + ns:base + nr + ns + njo],
                   refs[base + nr + ns + njo + 1:], pl.program_id(0), n - 1)

        @pl.when(pl.program_id(0) == 0)
        def _():
            ds_scr[...] = jnp.zeros_like(ds_scr)

        ds = ds_scr[...]
        for sub in reversed(range(grp)):
            rows = slice(sub * CHUNK, (sub + 1) * CHUNK)
            ins = [_heads_in(r, rows) for r in refs[:nr]]
            if ns:
                ins += _gdn_scalars(*[r[rows, :] for r in refs[nr:nr + ns]])
            p = ph_ref[sub]
            _, vjp = jax.vjp(lambda s, *a, p=p: fn(s, *a, p=p)[:2], sh_ref[sub], *ins)
            g = vjp((ds, _heads_in(do_ref, rows)))
            ds = g[0]
            for r, v in zip(g_refs[:nr], g[1:1 + nr]):
                _heads_out(r, rows, v)
            if ns:
                for r, v in zip(g_refs[nr:], _gdn_scalars_back(*g[1 + nr:])):
                    r[rows, :] = v
        ds_scr[...] = ds

    arrs = (*rows_in, *scal_in)
    return pl.pallas_call(
        body, grid=(n,),
        in_specs=[_scan_spec(a.shape[1], grp, n, True) for a in arrs]
        + [_hist_spec(grp, n, True), _hist_spec(grp, n, True), _scan_spec(WIDTH, grp, n, True)] + [_ANY] * nji,
        out_specs=[_scan_spec(a.shape[1], grp, n, True) for a in arrs] + [_ANY] * njo,
        out_shape=[jax.ShapeDtypeStruct(a.shape, f32) for a in arrs] + j_outs,
        scratch_shapes=[pltpu.VMEM((HEADS, HEAD_DIM, HEAD_DIM), f32)] + j_sems,
        compiler_params=pltpu.CompilerParams(dimension_semantics=("arbitrary",)),
        name=name,
    )(*arrs, s_hist, p_hist, d_out, *j_ins)


def _fn_norm_mod(ps, ts):
    nw, shift, scale = ps
    (x,) = ts
    return [_rms(x, nw, NORM_EPS) * (1.0 + scale) + shift]


def _fn_norm_mod_and_x(ps, ts):
    return _fn_norm_mod(ps, ts) + [ts[0]]


def _fn_resid_norm_mod(ps, ts):
    gate, nw, shift, scale = ps
    x, mo = ts
    x1 = x + gate * mo
    return [x1, _rms(x1, nw, NORM_EPS) * (1.0 + scale) + shift]


def _fn_gdn_pre(ps, ts):
    cw = ps[:12]
    alog, dtb, seg, tri = ps[12:]
    ba = ts[12]
    outs = []
    for part in range(3):
        x = ts[4 * part:4 * part + 4]
        w = cw[4 * part:4 * part + 4]
        conv = w[3] * x[0] + w[2] * x[1] + w[1] * x[2] + w[0] * x[3]
        u = _silu(conv)
        if part < 2:
            u = u * lax.rsqrt(_segsum(u * u, seg) + 1e-6)
            if part == 0:
                u = u * (HEAD_DIM ** -0.5)
        outs.append(u)
    beta = jax.nn.sigmoid(ba)
    g = -jnp.exp(alog) * _softplus(ba + dtb)
    gc = _xdot_l(tri, g)
    return outs + [beta, gc]


def _fn_gdn_post(ps, ts):
    ow, seg = ps
    o, z = ts
    ms = _segsum(o * o, seg) * (1.0 / HEAD_DIM)
    return [o * lax.rsqrt(ms + NORM_EPS) * ow * _silu(z)]


def _fn_rwkv_pre(ps, ts):
    mu_r, mu_k, mu_v, mu_l, mu_g, w0, w2p, a0, a2p, g2p, k_k, k_a, seg = ps
    r0, r1, k0, k1, v0, v1, l0, l1, g0, g1 = ts
    xr = r0 + (r1 - r0) * mu_r
    xk = k0 + (k1 - k0) * mu_k
    xv = v0 + (v1 - v0) * mu_v
    xl = l0 + (l1 - l0) * mu_l
    xg = g0 + (g1 - g0) * mu_g
    w = -_softplus(-(w0 + _bdot(jnp.tanh(xl), w2p))) - 0.5
    lw = -jnp.exp(w)
    a = jax.nn.sigmoid(a0 + _bdot(xl, a2p))
    g = _bdot(jax.nn.sigmoid(xg), g2p)
    kk = xk * k_k
    kk = kk * lax.rsqrt(_segsum(kk * kk, seg) + 1e-6)
    k2 = xk * (1.0 + (a - 1.0) * k_a)
    return [xr, lw, k2, xv, -kk, kk * a, g]


def _fn_rwkv_post(ps, ts):
    lw_, lb_, rk, seg = ps
    y, r, k2, v, g = ts
    inv = 1.0 / HEAD_DIM
    yc = y - _segsum(y, seg) * inv
    var = _segsum(yc * yc, seg) * inv
    yn = yc * lax.rsqrt(var + LNX_EPS) * lw_ + lb_
    bonus = _segsum(r * k2 * rk, seg) * v
    return [(yn + bonus) * g]


def _fn_merge(ps, ts):
    gla, glb, ya, yb = ts
    return [jax.nn.sigmoid(gla) * ya + jax.nn.sigmoid(glb) * yb]


def _fn_convglu(ps, ts):
    c0, c1, c2 = ps
    g0, g1, g2, up = ts
    return [_silu(c2 * g0 + c1 * g1 + c0 * g2) * up]


def _final_stage(x1, fo, tgt, gate2, nfw, tt):
    rows, d = x1.shape

    def loss_fn(gate, nw, xa, fa, tg):
        y = _rms(xa + gate * fa, nw, NORM_EPS)
        err = (y - tg) ** 2
        return 0.5 * jnp.sum(jnp.mean(err, axis=-1, keepdims=True), axis=0, keepdims=True)

    def body(x_ref, f_ref, t_ref, g_ref, w_ref, dx_ref, df_ref, dg_ref, dw_ref, l_ref):
        i = pl.program_id(0)
        args = (g_ref[...], w_ref[...], x_ref[...], f_ref[...])
        tg = t_ref[...]
        lv, vjp = jax.vjp(lambda g, w, xa, fa: loss_fn(g, w, xa, fa, tg), *args)
        dg, dw, dx, df = vjp(jnp.ones((1, 1), f32))
        dx_ref[...] = dx
        df_ref[...] = df.astype(df_ref.dtype)

        @pl.when(i == 0)
        def _():
            dg_ref[...] = jnp.zeros_like(dg_ref)
            dw_ref[...] = jnp.zeros_like(dw_ref)
            l_ref[...] = jnp.zeros_like(l_ref)

        dg_ref[...] += dg
        dw_ref[...] += dw
        l_ref[...] += jnp.broadcast_to(lv, l_ref.shape)

    row = pl.BlockSpec((tt, d), lambda i: (i, 0))
    vec = pl.BlockSpec((1, d), lambda i: (0, 0))
    return pl.pallas_call(
        body, grid=(rows // tt,),
        in_specs=[row, row, row, vec, vec],
        out_specs=[row, row, vec, vec, pl.BlockSpec((1, LANES), lambda i: (0, 0))],
        out_shape=[jax.ShapeDtypeStruct((rows, d), f32), jax.ShapeDtypeStruct((rows, d), bf16)]
        + [jax.ShapeDtypeStruct((1, d), f32)] * 2
        + [jax.ShapeDtypeStruct((1, LANES), f32)],
        compiler_params=pltpu.CompilerParams(dimension_semantics=("arbitrary",)),
        name="loss_head",
    )(x1, fo, tgt, gate2, nfw)


def _ada_fwd(c_all, w_shard, b_cols):
    def body(c_ref, w_ref, b_ref, cond_ref, mod_ref):
        cond = _silu(c_ref[...])
        cond_ref[...] = cond
        mod_ref[...] = jnp.dot(cond.astype(bf16), w_ref[...].astype(bf16), preferred_element_type=f32) + b_ref[...]

    n = w_shard.shape[1]
    return pl.pallas_call(
        body, out_shape=[jax.ShapeDtypeStruct(c_all.shape, f32), jax.ShapeDtypeStruct((c_all.shape[0], n), f32)],
        name="ada_fwd",
    )(c_all, w_shard, b_cols)


def _adamw(name, w, g, m, v):
    rows, width = w.shape
    tt = _pick(rows, 128, 8)
    c1 = 1.0 - ADAM_B1 ** ADAM_STEP
    c2 = 1.0 - ADAM_B2 ** ADAM_STEP

    def body(w_ref, g_ref, m_ref, v_ref, d_ref, mo_ref, vo_ref):
        gg = g_ref[...]
        mn = ADAM_B1 * m_ref[...] + (1.0 - ADAM_B1) * gg
        vn = ADAM_B2 * v_ref[...] + (1.0 - ADAM_B2) * (gg * gg)
        m_hat = mn / c1
        v_hat = vn / c2
        d_ref[...] = -ADAM_LR * (m_hat / (jnp.sqrt(v_hat) + ADAM_EPS) + ADAM_WD * w_ref[...])
        mo_ref[...] = mn
        vo_ref[...] = vn

    spec = pl.BlockSpec((tt, width), lambda i: (i, 0))
    return pl.pallas_call(
        body, grid=(rows // tt,), in_specs=[spec] * 4, out_specs=[spec] * 3,
        out_shape=[jax.ShapeDtypeStruct((rows, width), f32)] * 3,
        compiler_params=pltpu.CompilerParams(dimension_semantics=("parallel",)),
        name=name,
    )(w, g, m, v)


def _place():
    return lax.axis_index("x"), lax.axis_index("y"), lax.axis_index("c")


def _ag8(name, blk):
    m, w = blk.shape

    def body(x_ref, out_ref, send_sems, recv_sems, local_sem):
        x, y, c = _place()
        me, sibling = (x, y, c), (x, y, 1 - c)
        chips = _other_chips(x, y)

        def slot(px, py, pc):
            return out_ref.at[4 * px + 2 * py + pc]

        def copy(k, block, to, src=None):
            return pltpu.make_async_remote_copy(src_ref=slot(*block) if src is None else src, dst_ref=slot(*block),
                                                send_sem=send_sems.at[k], recv_sem=recv_sems.at[k], device_id=to,
                                                device_id_type=MESH_ID)

        mine = pltpu.make_async_copy(x_ref, slot(*me), local_sem)
        mine.start()
        first = [copy(0, me, sibling, src=x_ref)] + [copy(1 + j, me, (*chip, c), src=x_ref) for j, chip in enumerate(chips)]
        for cp in first:
            cp.start()
        passed = [copy(4 + j, (*chip, c), sibling) for j, chip in enumerate(chips)]
        for j, chip in enumerate(chips):
            copy(1 + j, (*chip, c), me).wait_recv()
            passed[j].start()
        copy(0, sibling, me).wait_recv()
        for j, chip in enumerate(chips):
            copy(4 + j, (*chip, 1 - c), me).wait_recv()
        for cp in first + passed:
            cp.wait_send()
        mine.wait()

    return pl.pallas_call(
        body, out_shape=jax.ShapeDtypeStruct((8, m, w), blk.dtype),
        in_specs=[pl.BlockSpec(memory_space=pltpu.VMEM)], out_specs=pl.BlockSpec(memory_space=pltpu.VMEM),
        scratch_shapes=[pltpu.SemaphoreType.DMA((7,)), pltpu.SemaphoreType.DMA((7,)), pltpu.SemaphoreType.DMA],
        name=name,
    )(blk)


def _other_chips(x, y):
    return [(1 - x, y), (x, 1 - y), (1 - x, 1 - y)]


_ANY = pl.BlockSpec(memory_space=pl.ANY)


def _rcopy(src, dst, send_sems, recv_sems, k, dev):
    return pltpu.make_async_remote_copy(src_ref=src, dst_ref=dst, send_sem=send_sems.at[k], recv_sem=recv_sems.at[k],
                                        device_id=dev, device_id_type=MESH_ID)


def _run_job(name, job):
    j_ins, j_outs, j_sems = _job_parts(job)
    n = len(j_ins)

    def body(*refs):
        job["start"](refs[:n], refs[n:n + len(j_outs)], *refs[n + len(j_outs):])
        job["finish"](refs[:n], refs[n:n + len(j_outs)], *refs[n + len(j_outs):])

    return pl.pallas_call(body, out_shape=j_outs, in_specs=[_ANY] * n, out_specs=[_ANY] * len(j_outs),
                          scratch_shapes=j_sems, name=name)(*j_ins)


def _ag4_job(ws):
    n = len(ws)

    def plan(w_refs, out_refs, send_sems, recv_sems):
        x, y, c = _place()
        chip = 2 * x + y
        sibling = (x, y, 1 - c)
        chips = _other_chips(x, y)
        mine = [pl.ds(c * (w.shape[0] // 2), w.shape[0] // 2) for w in ws]
        other = [pl.ds((1 - c) * (w.shape[0] // 2), w.shape[0] // 2) for w in ws]
        rc = lambda src, dst, k, dev: _rcopy(src, dst, send_sems, recv_sems, k, dev)
        first = [rc(w_refs[t].at[mine[t]], out_refs[t].at[chip, mine[t]], 7 * t + k, (px, py, c))
                 for t in range(n) for k, (px, py) in enumerate(chips)]
        own = [rc(w_refs[t], out_refs[t].at[chip], 7 * t + 6, sibling) for t in range(n)]
        landed = [rc(out_refs[t].at[2 * px + py, mine[t]], out_refs[t].at[2 * px + py, mine[t]], 7 * t + k, (px, py, c))
                  for t in range(n) for k, (px, py) in enumerate(chips)]
        forward = [rc(out_refs[t].at[2 * px + py, mine[t]], out_refs[t].at[2 * px + py, mine[t]], 7 * t + 3 + k, sibling)
                   for t in range(n) for k, (px, py) in enumerate(chips)]
        handed = [rc(out_refs[t].at[2 * px + py, other[t]], out_refs[t].at[2 * px + py, other[t]], 7 * t + 3 + k, sibling)
                  for t in range(n) for k, (px, py) in enumerate(chips)]
        return first, own, landed, forward, handed

    def start(*refs):
        first, own, _, _, _ = plan(*refs)
        for cp in first + own:
            cp.start()

    def finish(*refs):
        first, own, landed, forward, handed = plan(*refs)
        for arrived, fw in zip(landed, forward):
            arrived.wait_recv()
            fw.start()
        for cp in handed + own:
            cp.wait_recv()
        for cp in first + own + forward:
            cp.wait_send()

    return dict(ins=ws, out_shapes=[jax.ShapeDtypeStruct((4,) + w.shape, w.dtype) for w in ws], nsem=7 * n,
                start=start, finish=finish)


def _rs_sibling_swap(gs, tag):
    n = len(gs)

    def body(*refs):
        g_refs, recv_refs = refs[:n], refs[n:2 * n]
        send_sems, recv_sems = refs[2 * n:]
        x, y, c = _place()
        copies = []
        for t in range(n):
            rh = gs[t].shape[1] // 2
            for s_ in range(4):
                cp = _rcopy(g_refs[t].at[s_, pl.ds((1 - c) * rh, rh)], recv_refs[t].at[s_], send_sems, recv_sems,
                            4 * t + s_, (x, y, 1 - c))
                cp.start()
                copies.append(cp)
        for cp in copies:
            cp.wait_recv()
        for cp in copies:
            cp.wait_send()

    return pl.pallas_call(
        body, out_shape=[jax.ShapeDtypeStruct((4, g.shape[1] // 2, g.shape[2]), g.dtype) for g in gs],
        in_specs=[_ANY] * n, out_specs=[_ANY] * n,
        scratch_shapes=[pltpu.SemaphoreType.DMA((4 * n,)), pltpu.SemaphoreType.DMA((4 * n,))],
        name="grads_sibling_swap_" + tag,
    )(*gs)


def _chip_exchange_job(ps):
    n = len(ps)

    def plan(p_refs, recv_refs, send_sems, recv_sems):
        x, y, c = _place()
        return [_rcopy(p_refs[t].at[2 * px + py], recv_refs[t].at[k], send_sems, recv_sems, 3 * t + k, (px, py, c))
                for t in range(n) for k, (px, py) in enumerate(_other_chips(x, y))]

    def start(*refs):
        for cp in plan(*refs):
            cp.start()

    def finish(*refs):
        copies = plan(*refs)
        for cp in copies:
            cp.wait_recv()
        for cp in copies:
            cp.wait_send()

    return dict(ins=ps, out_shapes=[jax.ShapeDtypeStruct((3,) + p.shape[1:], p.dtype) for p in ps], nsem=3 * n,
                start=start, finish=finish)


_JOIN_PIECES = 4


def _rs_sibling_join(qs):
    n = len(qs)
    npc = _JOIN_PIECES

    def body(*refs):
        q_refs, out_refs = refs[:n], refs[n:2 * n]
        send_sems, recv_sems = refs[2 * n:]
        x, y, c = _place()
        copies = []
        for t in range(n):
            rh = qs[t].shape[0] // 2
            pr = rh // npc
            for i in range(npc):
                rows = pl.ds(c * rh + i * pr, pr)
                cp = _rcopy(q_refs[t].at[rows], out_refs[t].at[rows], send_sems, recv_sems, npc * t + i,
                            (x, y, 1 - c))
                cp.start()
                copies.append(cp)
        for t in range(n):
            rh = qs[t].shape[0] // 2
            pr = rh // npc
            for i in range(npc):
                rows = pl.ds((1 - c) * rh + i * pr, pr)
                _rcopy(q_refs[t].at[rows], out_refs[t].at[rows], send_sems, recv_sems, npc * t + i,
                       (x, y, 1 - c)).wait_recv()
        for cp in copies:
            cp.wait_send()

    return pl.pallas_call(
        body, out_shape=[jax.ShapeDtypeStruct(q.shape, q.dtype) for q in qs],
        in_specs=[_ANY] * n, out_specs=[_ANY] * n, input_output_aliases={t: t for t in range(n)},
        scratch_shapes=[pltpu.SemaphoreType.DMA((npc * n,)), pltpu.SemaphoreType.DMA((npc * n,))],
        name="grads_sibling_join",
    )(*qs)


def _add_half(name, g, recv, ci):
    S, r, w = g.shape
    rh = r // 2
    tt = _pick(rh, 256, 16)
    nb = rh // tt

    def body(c_ref, a_ref, b_ref, o_ref, ob_ref):
        v = a_ref[...] + b_ref[...]
        o_ref[...] = v
        ob_ref[...] = v.astype(bf16)

    blk = pl.BlockSpec((1, tt, w), lambda s_, i, cr: (s_, i, 0))
    grid_spec = pltpu.PrefetchScalarGridSpec(
        num_scalar_prefetch=1, grid=(S, nb),
        in_specs=[pl.BlockSpec((1, tt, w), lambda s_, i, cr: (s_, cr[0] * nb + i, 0)), blk],
        out_specs=[blk, blk])
    return pl.pallas_call(body, grid_spec=grid_spec,
                          out_shape=[jax.ShapeDtypeStruct((S, rh, w), f32), jax.ShapeDtypeStruct((S, rh, w), bf16)],
                          name=name)(ci.reshape(1).astype(jnp.int32), g, recv)


def _sum_chip(name, pair, others, chip, ci):
    _, rh, w = pair.shape
    tt = _pick(rh, 128, 16)
    nb = rh // tt

    def body(chip_ref, core_ref, a_ref, b_ref, o_ref):
        o_ref[...] = ((a_ref[0] + b_ref[0].astype(f32)) + b_ref[1].astype(f32)) + b_ref[2].astype(f32)

    grid_spec = pltpu.PrefetchScalarGridSpec(
        num_scalar_prefetch=2, grid=(nb,),
        in_specs=[pl.BlockSpec((1, tt, w), lambda i, ch, co: (ch[0], i, 0)),
                  pl.BlockSpec((3, tt, w), lambda i, ch, co: (0, i, 0))],
        out_specs=pl.BlockSpec((tt, w), lambda i, ch, co: (co[0] * nb + i, 0)))
    return pl.pallas_call(body, grid_spec=grid_spec, out_shape=jax.ShapeDtypeStruct((2 * rh, w), f32),
                          name=name)(chip.reshape(1).astype(jnp.int32), ci.reshape(1).astype(jnp.int32), pair, others)


def _sum_devices(gathered, head_row):
    _, rows, width = gathered.shape

    def body(g_ref, out_ref, head_ref):
        acc = g_ref[0]
        for d in range(1, 8):
            acc = acc + g_ref[d]
        out_ref[...] = acc
        row = acc[head_row:head_row + 1, :]
        hs = row[:, 0:HEAD_DIM]
        for h in range(1, HEADS):
            hs = hs + row[:, h * HEAD_DIM:(h + 1) * HEAD_DIM]
        head_ref[...] = jnp.zeros_like(head_ref)
        head_ref[0:1, 0:HEAD_DIM] = hs

    return pl.pallas_call(
        body, out_shape=[jax.ShapeDtypeStruct((rows, width), f32), jax.ShapeDtypeStruct((8, LANES), f32)],
        name="small_grads_sum",
    )(gathered)


def _pack(arrs, rows_mult, dtype):
    flat = jnp.concatenate([a.reshape(-1).astype(dtype) for a in arrs])
    per = PACK_W * rows_mult
    total = -(-flat.shape[0] // per) * per
    return jnp.pad(flat, (0, total - flat.shape[0])).reshape(total // PACK_W, PACK_W)


def _unpack(buf, shapes):
    flat = buf.reshape(-1)
    out, off = [], 0
    for s in shapes:
        n = int(np.prod(s))
        out.append(flat[off:off + n].reshape(s))
        off += n
    return out


_BIG = ["w_in", "w_branch_gdn", "w_branch_rwkv", "w_out", "w_ffn_in", "w_ffn_out"]
_MID = ["conv_gdn", "conv_ffn", "w2", "a2", "g2"]
_SMALL =["b_ada", "norm1_w", "a_log", "dt_bias", "onorm_gdn", "mu_rwkv", "w0", "a0", "k_k", "k_a", "r_k", "lnx_w",
          "lnx_b", "norm2_w", "norm_f_w"]
_ORDER = ["w_ada", "b_ada", "norm1_w", "w_in", "conv_gdn", "a_log", "dt_bias", "onorm_gdn", "w_branch_gdn", "mu_rwkv",
          "w0", "w2", "a0", "a2", "g2", "k_k", "k_a", "r_k", "lnx_w", "lnx_b", "w_branch_rwkv", "w_out", "norm2_w",
          "w_ffn_in", "conv_ffn", "w_ffn_out", "norm_f_w"]


_WIN_SEGMENTS = [(0, 1536, 0), (2064, 3600, 1536), (1536, 2048, 3072), (3600, 3728, 3584), (2048, 2064, 3712),
                 (3728, 3888, 3840), (3888, 5936, 4096)]
_WIN_PADDED = 6144


def _win_pad(shards):
    n = shards.shape[2]
    parts, at = [], 0
    for lo, hi, dst in _WIN_SEGMENTS:
        if dst > at:
            parts.append(jnp.zeros((shards.shape[1], dst - at), shards.dtype))
        c = lo
        while c < hi:
            j = c // n
            e = min(hi, (j + 1) * n)
            parts.append(shards[j][:, c - j * n:e - j * n])
            c = e
        at = dst + hi - lo
    if at < _WIN_PADDED:
        parts.append(jnp.zeros((shards.shape[1], _WIN_PADDED - at), shards.dtype))
    return jnp.concatenate(parts, axis=1)


def _win_unpad_shards(g, n):
    shards = []
    for j in range(4):
        parts = []
        for lo, hi, dst in sorted(_WIN_SEGMENTS):
            a, b = max(lo, j * n), min(hi, (j + 1) * n)
            if a < b:
                parts.append(g[:, dst + a - lo:dst + b - lo])
        shards.append(jnp.concatenate(parts, axis=1))
    return jnp.stack(shards)


def kernel(x, c, w_ada, b_ada, norm1_w, w_in, conv_gdn, a_log, dt_bias, onorm_gdn, w_branch_gdn, mu_rwkv, w0, w2, a0, a2, g2, k_k, k_a, r_k, lnx_w, lnx_b, w_branch_rwkv, w_out, norm2_w, w_ffn_in, conv_ffn, w_ffn_out, norm_f_w, loss_target, m_w_ada, m_b_ada, m_norm1_w, m_w_in, m_conv_gdn, m_a_log, m_dt_bias, m_onorm_gdn, m_w_branch_gdn, m_mu_rwkv, m_w0, m_w2, m_a0, m_a2, m_g2, m_k_k, m_k_a, m_r_k, m_lnx_w, m_lnx_b, m_w_branch_rwkv, m_w_out, m_norm2_w, m_w_ffn_in, m_conv_ffn, m_w_ffn_out, m_norm_f_w, v_w_ada, v_b_ada, v_norm1_w, v_w_in, v_conv_gdn, v_a_log, v_dt_bias, v_onorm_gdn, v_w_branch_gdn, v_mu_rwkv, v_w0, v_w2, v_a0, v_a2, v_g2, v_k_k, v_k_a, v_r_k, v_lnx_w, v_lnx_b, v_w_branch_rwkv, v_w_out, v_norm2_w, v_w_ffn_in, v_conv_ffn, v_w_ffn_out, v_norm_f_w):
    args = dict(locals())
    W = {n: args[n] for n in _ORDER}
    Mo = {n: args["m_" + n] for n in _ORDER}
    Vo = {n: args["v_" + n] for n in _ORDER}
    shapes = {n: W[n].shape for n in _ORDER}
    sq = lambda a: a.reshape(a.shape[-2:]) if a.ndim == 3 else a.reshape(1, -1)
    row = lambda a: a.reshape(1, -1)

    xi, yi, ci = lax.axis_index("x"), lax.axis_index("y"), lax.axis_index("c")
    dev = 4 * xi + 2 * yi + ci
    chip = 2 * xi + yi

    x2 = x[0]
    tgt = loss_target[0]
    T, D = x2.shape
    N = T // CHUNK
    tt_l = _pick(T, 256, CHUNK)
    tt_h = _pick(T, 128, CHUNK)

    mid_shapes = [shapes[n][1:] for n in _MID]
    small_blk = _pack([c] + [W[n] for n in _MID], 8, f32)
    small_all = _ag8("gather_c_mid", small_blk)
    c_all = small_all[:, 0, :]
    per_chip = small_all[0::2].reshape(4, -1)[:, D:]
    mid = [dict(zip(_MID, _unpack(per_chip[j], mid_shapes))) for j in range(4)]
    catm = lambda n: jnp.concatenate([mid[j][n] for j in range(4)], axis=1)
    conv_gdn_f, conv_ffn_f = catm("conv_gdn"), catm("conv_ffn")
    w2f, a2f, g2f = catm("w2"), catm("a2"), catm("g2")

    (win_s,) = _run_job("w_in_all_gather", _ag4_job([sq(W["w_in"]).astype(bf16)]))
    later_weights = _ag4_job([sq(W[n]).astype(bf16) for n in _BIG[1:]])
    win_p = _win_pad(win_s)
    zpad = lambda a, top, bot: jnp.pad(a, ((top, bot), (0, 0)))
    w2p, a2p, g2p = zpad(w2f, 0, 64), zpad(a2f, 64, 0), zpad(g2f, 0, 96)

    ncol = shapes["w_ada"][2]
    b_cols = lax.dynamic_slice(sq(W["b_ada"]), (0, chip * ncol), (1, ncol))
    cond16, mod_cols = _ada_fwd(jnp.pad(c_all, ((0, 8), (0, 0))), sq(W["w_ada"]), b_cols)
    mod_all = _ag8("gather_mod", mod_cols[:8])
    mod_mine = lax.dynamic_slice(mod_all[0::2], (0, dev, 0), (4, 1, ncol)).reshape(1, 4 * ncol)
    shift1, scale1, gate1, shift2, scale2, gate2 = [mod_mine[:, i * D:(i + 1) * D] for i in range(6)]

    seg = _seg_matrix(WIDTH, HEAD_DIM)
    norm1 = [sq(W["norm1_w"]), shift1, scale1]
    h1 = _stage_fwd("norm_mod1", _fn_norm_mod, [_whole(x2)], norm1, [(D, bf16)], tt_l)[0]
    p = _matmul("in_proj", h1, win_p, "nn")

    cgq = [row(conv_gdn_f[j, part * WIDTH:(part + 1) * WIDTH]) for part in range(3) for j in range(4)]
    lane_pad = lambda a: jnp.pad(row(a), ((0, 0), (8, LANES - 16)))
    gdn_pre_ps = cgq + [lane_pad(W["a_log"]), lane_pad(W["dt_bias"]), seg, _chunk_tri(tt_h, CHUNK)]
    gdn_pre_ts = [(p, WIDTH, part, (0, 1, 2, 3)) for part in range(3)] + [(p, LANES, 29, None)]
    q_, k_, v_, beta_t, gc_t = _stage_fwd("gdn_pre", _fn_gdn_pre, gdn_pre_ts, gdn_pre_ps,
                                          [(WIDTH, f32)] * 3 + [(LANES, f32)] * 2, tt_h)
    o_, gdn_hist, gdn_inv, wbg_s, wbr_s, wout_s, wfi_s, wfo_s = _scan_fwd(
        "gdn_scan", _gdn_chunk, [q_, k_, v_], [beta_t, gc_t], job=later_weights)
    wout_f = wout_s.reshape(D, D)
    wfo = wfo_s.reshape(D_FF, D)
    ow512 = jnp.tile(row(W["onorm_gdn"]), (1, HEADS))
    gdn_post_ts = [_whole(o_), (p, WIDTH, 6, None)]
    ya = _stage_fwd("gdn_post", _fn_gdn_post, gdn_post_ts, [ow512, seg], [(WIDTH, bf16)], tt_l)[0]

    mu = sq(W["mu_rwkv"])
    rw_ps = [mu[:, 0:512], mu[:, 512:1024], mu[:, 1024:1536], mu[:, 1536:1664], jnp.pad(mu[:, 1664:1824], ((0, 0), (0, 96))),
             sq(W["w0"]), w2p, sq(W["a0"]), a2p, g2p, sq(W["k_k"]), sq(W["k_a"]), seg]
    rw_ts = [(p, WIDTH, 3, (0, 1)), (p, WIDTH, 4, (0, 1)), (p, WIDTH, 5, (0, 1)), (p, LANES, 28, (0, 1)),
             (p, 256, 15, (0, 1))]
    rw_out = _stage_fwd("rwkv_pre", _fn_rwkv_pre, rw_ts, rw_ps, [(WIDTH, f32)] * 7, tt_h)
    r_, lw_, k2_, vv_, na_, b_, g_ = rw_out
    rw_ins = [r_, lw_, k2_, vv_, na_, b_]
    y_, rw_hist, rw_inv = _scan_fwd("rwkv_scan", _rwkv_chunk, rw_ins)
    rwp_ps = [sq(W["lnx_w"]), sq(W["lnx_b"]), row(W["r_k"]), seg]
    rwp_ts = [_whole(y_), _whole(r_), _whole(k2_), _whole(vv_), _whole(g_)]
    yb = _stage_fwd("rwkv_post", _fn_rwkv_post, rwp_ts, rwp_ps, [(WIDTH, bf16)], tt_l)[0]

    big_a = _matmul("branch_gdn", ya, wbg_s, "nn", shards=4)
    big_b = _matmul("branch_rwkv", yb, wbr_s, "nn", shards=4)
    merge_ts = [(p, D, 4, None), (p, D, 5, None), _whole(big_a), _whole(big_b)]
    merged = _stage_fwd("merge", _fn_merge, merge_ts, [], [(D, bf16)], tt_l)[0]
    mo = _matmul("out_proj", merged, wout_f, "nn")
    norm2 = [gate1, sq(W["norm2_w"]), shift2, scale2]
    x1, h2 = _stage_fwd("resid_norm_mod2", _fn_resid_norm_mod, [_whole(x2), _whole(mo)], norm2, [(D, f32), (D, bf16)], tt_l)
    f = _matmul("ffn_in", h2, wfi_s, "nn", shards=4, tn=1408)
    cg_ps = [row(conv_ffn_f[j]) for j in range(3)]
    cg_ts = [(f, D_FF, 0, (0, 1, 2)), (f, D_FF, 1, None)]
    act = _stage_fwd("convglu", _fn_convglu, cg_ts, cg_ps, [(D_FF, bf16)], tt_h)[0]
    fo = _matmul("ffn_out", act, wfo, "nn", tk=1408)

    dx1_a, dfo, dgate2, dnormf, loss_part = _final_stage(x1, fo, tgt, gate2, row(W["norm_f_w"]), tt_l)

    dact = _matmul("d_act", dfo, wfo, "nt", tn=1408)
    g_wfo = _matmul("g_ffn_out", act, dfo, "tn", tm=1408)
    _, dcf, df = _stage_bwd("convglu_bwd", _fn_convglu, cg_ts, cg_ps, [[_whole(dact)]], tt_h, [True] * 2, [True] * 3,
                            joint=([0, 1], 2 * D_FF, 0, None), dtypes=[bf16])
    dh2 = _matmul("d_h2", df, wfi_s, "nt", shards=4, tk=1408)
    g_wfi = _matmul("g_ffn_in", h2, df, "tn", shards=4, tn=1408)
    (dx_a, dmo), (dgate1, dnorm2, dshift2, dscale2), _ = _stage_bwd(
        "resid_norm_mod2_bwd", _fn_resid_norm_mod, [_whole(x2), _whole(mo)], norm2,
        [[_whole(dx1_a)], [_whole(dh2)]], tt_l, [True, True], [True] * 4, dtypes=[f32, bf16])
    dmerged = _matmul("d_merged", dmo, wout_f, "nt")
    g_wout = _matmul("g_out_proj", merged, dmo, "tn")
    (dbig_a, dbig_b), _, dp = _stage_bwd("merge_bwd", _fn_merge, merge_ts, [], [[_whole(dmerged)]], tt_l, [True] * 4, [],
                                         joint=([0, 1], p.shape[1], 2, None), dtypes=[bf16] * 3)
    dya = _matmul("d_ya", dbig_a, wbg_s, "nt", shards=4)
    g_wbg = _matmul("g_branch_gdn", ya, dbig_a, "tn", shards=4)
    dyb = _matmul("d_yb", dbig_b, wbr_s, "nt", shards=4)
    g_wbr = _matmul("g_branch_rwkv", yb, dbig_b, "tn", shards=4)

    (dy_, dr_p, dk2_p, dv_p, dg_p), (dlnxw, dlnxb, drk), _ = _stage_bwd(
        "rwkv_post_bwd", _fn_rwkv_post, rwp_ts, rwp_ps, [[_whole(dyb)]], tt_l, [True] * 5, [True, True, True, False])
    gs_a = [g_wbg, g_wbr, g_wout.reshape(4, D // 4, D), g_wfi, g_wfo.reshape(4, D_FF // 4, D)]
    pairs_a = [_add_half("grads_pair_sum%d" % (t + 1), g, r_, ci) for t, (g, r_) in enumerate(zip(gs_a, _rs_sibling_swap(gs_a, "a")))]
    dr_c, dlw_c, dk2_c, dv_c, dna_c, db_c, *others_a = _scan_bwd(
        "rwkv_scan_bwd", _rwkv_chunk, rw_ins, [], rw_hist, rw_inv, dy_, job=_chip_exchange_job([pb for _, pb in pairs_a]))
    rw_cots = [[_whole(dr_p), _whole(dr_c)], [_whole(dlw_c)], [_whole(dk2_p), _whole(dk2_c)],
               [_whole(dv_p), _whole(dv_c)], [_whole(dna_c)], [_whole(db_c)], [_whole(dg_p)]]
    (dl_, dg_), rw_dp, dp = _stage_bwd("rwkv_pre_bwd", _fn_rwkv_pre, rw_ts, rw_ps, rw_cots, tt_h, [True] * 5,
                                       [True] * 12 + [False], joint=([0, 1, 2], p.shape[1], 1, dp), dtypes=[bf16] * 3)
    dmu_r, dmu_k, dmu_v, dmu_l, dmu_g, dw0, dw2p, da0, da2p, dg2p, dkk, dka = rw_dp

    (do_,), (dow512,), dp = _stage_bwd("gdn_post_bwd", _fn_gdn_post, gdn_post_ts, [ow512, seg], [[_whole(dya)]], tt_l,
                                       [True, True], [True, False], joint=([1], p.shape[1], 6, dp), dtypes=[f32, bf16])
    d_gdn = _scan_bwd("gdn_scan_bwd", _gdn_chunk, [q_, k_, v_], [beta_t, gc_t], gdn_hist, gdn_inv, do_)
    gdn_cots = [[_whole(a)] for a in d_gdn]
    (dba,), gdn_dp, dp = _stage_bwd("gdn_pre_bwd", _fn_gdn_pre, gdn_pre_ts, gdn_pre_ps, gdn_cots, tt_h, [True] * 4,
                                    [True] * 14 + [False, False], joint=([0, 1, 2], p.shape[1], 0, dp), dtypes=[bf16] * 2)
    dp = lax.dynamic_update_slice(dp, jnp.concatenate([dl_, dba, dg_], axis=1), (0, 3584))
    g_win_s = _win_unpad_shards(_matmul("g_in_proj", h1, dp, "tn"), shapes["w_in"][2])
    pair_win = _add_half("grads_pair_sum0", g_win_s, _rs_sibling_swap([g_win_s], "b")[0], ci)
    dh1, others_win = _matmul("d_h1", dp, win_p, "nt", job=_chip_exchange_job([pair_win[1]]))
    (grad_x,), (dnorm1, dshift1, dscale1), _ = _stage_bwd("norm_mod1_bwd", _fn_norm_mod_and_x, [_whole(x2)], norm1,
                                                          [[_whole(dh1)], [_whole(dx_a)]], tt_l, [True], [True] * 3)

    dmod = jnp.concatenate([dshift1, dscale1, dgate1, dshift2, dscale2, dgate2], axis=1)
    g_conv_gdn = jnp.concatenate([jnp.concatenate([gdn_dp[4 * part + j] for part in range(3)], axis=1) for j in range(4)], axis=0)
    g_conv_ffn = jnp.concatenate(dcf, axis=0)
    g_mu = jnp.concatenate([dmu_r, dmu_k, dmu_v, dmu_l, dmu_g[:, :160]], axis=1)
    small_parts = {"b_ada": dmod, "norm1_w": dnorm1, "a_log": gdn_dp[12][:, 8:16], "dt_bias": gdn_dp[13][:, 8:16],
                   "mu_rwkv": g_mu, "w0": dw0, "a0": da0, "k_k": dkk, "k_a": dka, "r_k": drk, "lnx_w": dlnxw,
                   "lnx_b": dlnxb, "norm2_w": dnorm2, "norm_f_w": dnormf}
    small_names = [n for n in _SMALL if n != "onorm_gdn"]
    mid_full = [g_conv_gdn, g_conv_ffn, dw2p[0:64], da2p[64:128], dg2p[0:160]]
    body_rows = _pack([small_parts[n] for n in small_names] + [loss_part[:, 0:1]] + mid_full, 1, f32)
    head_row = body_rows.shape[0]
    small_g = jnp.concatenate([body_rows, jnp.pad(dow512, ((0, 0), (0, PACK_W - WIDTH)))], axis=0)
    small_g = jnp.pad(small_g, ((0, -small_g.shape[0] % 8), (0, 0)))
    small_all_g = _ag8("gather_small_grads", small_g)
    small_sum, head_sum = _sum_devices(small_all_g, head_row)
    small_shapes = [shapes[n][1:] if n != "norm_f_w" else shapes[n] for n in small_names]
    un = _unpack(small_sum, small_shapes + [(1,)] + [g.shape for g in mid_full])
    small_grads = dict(zip(small_names, un))
    loss = un[len(small_names)].reshape(())
    small_grads["onorm_gdn"] = head_sum[0, 0:HEAD_DIM]
    for n, g in zip(_MID, un[len(small_names) + 1:]):
        wcols = shapes[n][2]
        small_grads[n] = lax.dynamic_slice(g, (0, chip * wcols), (g.shape[0], wcols))

    dmod_all = small_all_g[:, 0:6, :].reshape(8, 6 * PACK_W)
    dmod_cols = lax.dynamic_slice(dmod_all, (0, chip * ncol), (8, ncol))
    g_wada = _matmul("g_w_ada", cond16, jnp.pad(dmod_cols, ((0, 8), (0, 0))), "tn")

    pairs = [pair_win] + pairs_a
    others = [others_win] + others_a
    halves = [_sum_chip("grads_chip_sum%d" % t, pf, o_, chip, ci) for t, ((pf, _), o_) in enumerate(zip(pairs, others))]
    big_grads = dict(zip(_BIG, _rs_sibling_join(halves)))

    res = {tag: {} for tag in ("grad", "delta", "new_m", "new_v")}

    def put(n, g, d, m_, v_):
        for tag, val in zip(("grad", "delta", "new_m", "new_v"), (g, d, m_, v_)):
            res[tag][n] = val.reshape(shapes[n])

    for n in _BIG:
        put(n, big_grads[n], *_adamw("adamw_" + n, sq(W[n]), big_grads[n], sq(Mo[n]), sq(Vo[n])))
    put("w_ada", g_wada, *_adamw("adamw_w_ada", sq(W["w_ada"]), g_wada, sq(Mo["w_ada"]), sq(Vo["w_ada"])))
    rest = _SMALL + _MID
    pk = lambda d: _pack([d[n] for n in rest], 8, f32)
    sg = pk(small_grads)
    sm = _adamw("adamw_small", pk(W), sg, pk(Mo), pk(Vo))
    for tag, buf in zip(("grad", "delta", "new_m", "new_v"), (sg,) + tuple(sm)):
        res[tag].update(zip(rest, _unpack(buf, [shapes[n] for n in rest])))
    outs = [loss, grad_x.reshape(x.shape)]
    for tag in ("grad", "delta", "new_m", "new_v"):
        outs += [res[tag][n] for n in _ORDER]
    return tuple(outs)
```

```python
import numpy as np
import jax
import jax.numpy as jnp
from jax import lax
from jax.experimental import pallas as pl
from jax.experimental.pallas import tpu as pltpu

f32 = jnp.float32
bf16 = jnp.bfloat16

LANES = 128
HEADS = 8
HEAD_DIM = 64
WIDTH = HEADS * HEAD_DIM
CHUNK = 64
D_FF = 2816
NORM_EPS = 1e-6
LNX_EPS = 64e-5
PACK_W = 1024
MESH_ID = pl.DeviceIdType.MESH

ADAM_LR, ADAM_B1, ADAM_B2, ADAM_EPS, ADAM_WD, ADAM_STEP = 0.001, 0.9, 0.999, 1e-08, 0.01, 10


def _pick(n, target, mult):
    if n <= target:
        return n
    best = None
    for t in range(mult, target + 1, mult):
        if n % t == 0:
            best = t
    assert best is not None, (n, target, mult)
    return best


def _split_bf16(x, n):
    parts, r = [], x
    for i in range(n):
        p = r.astype(bf16)
        parts.append(p)
        if i + 1 < n:
            r = r - p.astype(f32)
    return parts


def _xdot_r_impl(x, m, n, dims):
    acc = None
    for p in _split_bf16(x, n):
        t = lax.dot_general(p, m, dims, preferred_element_type=f32)
        acc = t if acc is None else acc + t
    return acc


def _make_xdot_r(n):
    nn = (((1,), (0,)), ((), ()))
    nt = (((1,), (1,)), ((), ()))

    @jax.custom_vjp
    def xdot(x, m):
        return _xdot_r_impl(x, m, n, nn)

    def fwd(x, m):
        return _xdot_r_impl(x, m, n, nn), m

    def bwd(m, ct):
        return _xdot_r_impl(ct, m, n, nt), jnp.zeros_like(m)

    xdot.defvjp(fwd, bwd)
    return xdot


_segsum = _make_xdot_r(2)


def _xdot_l_impl(m, x, n, dims):
    acc = None
    for p in _split_bf16(x, n):
        t = lax.dot_general(m, p, dims, preferred_element_type=f32)
        acc = t if acc is None else acc + t
    return acc


@jax.custom_vjp
def _xdot_l(m, x):
    return _xdot_l_impl(m, x, 3, (((1,), (0,)), ((), ())))


def _xdot_l_fwd(m, x):
    return _xdot_l(m, x), m


def _xdot_l_bwd(m, ct):
    return jnp.zeros_like(m), _xdot_l_impl(m, ct, 3, (((0,), (0,)), ((), ())))


_xdot_l.defvjp(_xdot_l_fwd, _xdot_l_bwd)


@jax.custom_vjp
def _bdot(x, w):
    return jnp.dot(x.astype(bf16), w.astype(bf16), preferred_element_type=f32)


def _bdot_fwd(x, w):
    return _bdot(x, w), (x, w)


def _bdot_bwd(res, ct):
    x, w = res
    c = ct.astype(bf16)
    dx = lax.dot_general(c, w.astype(bf16), (((1,), (1,)), ((), ())), preferred_element_type=f32)
    dw = lax.dot_general(x.astype(bf16), c, (((0,), (0,)), ((), ())), preferred_element_type=f32)
    return dx, dw


_bdot.defvjp(_bdot_fwd, _bdot_bwd)


def _silu(x):
    return x * jax.nn.sigmoid(x)


def _softplus(x):
    return jnp.maximum(x, 0.0) + jnp.log(1.0 + jnp.exp(-jnp.abs(x)))


def _rms(x, w, eps):
    return x * lax.rsqrt(jnp.mean(x * x, axis=-1, keepdims=True) + eps) * w


def _seg_matrix(width, seg):
    i = np.arange(width)
    return jnp.asarray((i[:, None] // seg) == (i[None, :] // seg), dtype=bf16)


def _chunk_tri(rows, chunk):
    i = np.arange(rows)
    return jnp.asarray(((i[:, None] // chunk) == (i[None, :] // chunk)) & (i[:, None] >= i[None, :]), dtype=bf16)


HALO = 8


def _full_spec(shape):
    nd = len(shape)
    return pl.BlockSpec(shape, lambda i: (0,) * nd)


def _entry_specs(entries, tt, block_of):
    specs, ops = [], []
    for arr, w, ci, shifts in entries:
        specs.append(pl.BlockSpec((tt, w), lambda i, ci=ci: (block_of(i), ci)))
        ops.append(arr)
        if shifts:
            specs.append(pl.BlockSpec((HALO, w), lambda i, ci=ci: (jnp.maximum(block_of(i) * (tt // HALO) - 1, 0), ci)))
            ops.append(arr)
    return specs, ops


def _load_entries(entries, refs, first):
    tiles, k = [], 0
    for _, w, _, shifts in entries:
        x = refs[k][...].astype(f32)
        k += 1
        if not shifts:
            tiles.append(x)
            continue
        halo = jnp.where(first, 0.0, refs[k][...].astype(f32))
        k += 1
        row = lax.broadcasted_iota(jnp.int32, (HALO, w), 0)
        for s in shifts:
            if s == 0:
                tiles.append(x)
                continue
            r = pltpu.roll(x, s, 0)
            head = jnp.where(row < s, pltpu.roll(halo, s, 0), r[0:HALO])
            tiles.append(jnp.concatenate([head, r[HALO:]], axis=0))
    return tiles


def _unshift_sum(grads, shifts, carry, tt):
    w = grads[0].shape[1]
    row = lax.broadcasted_iota(jnp.int32, (tt, w), 0)
    row8 = lax.broadcasted_iota(jnp.int32, (HALO, w), 0)
    dx, out = None, jnp.zeros((HALO, w), f32)
    for d, s in zip(grads, shifts):
        if s == 0:
            part = d
        else:
            part = jnp.where(row < tt - s, pltpu.roll(d, tt - s, 0), 0.0)
            out = out + jnp.where(row8 >= HALO - s, pltpu.roll(d[0:HALO], HALO - s, 0), 0.0)
        dx = part if dx is None else dx + part
    return jnp.concatenate([dx[:tt - HALO], dx[tt - HALO:] + carry], axis=0), out


def _stage_fwd(name, fn, tiles, params, outs, tt):
    rows = tiles[0][0].shape[0]
    npar = len(params)
    specs, ops = _entry_specs(tiles, tt, lambda i: i)
    nin = len(ops)

    def body(*refs):
        ts = _load_entries(tiles, refs[:nin], pl.program_id(0) == 0)
        ps = [r[...] for r in refs[nin:nin + npar]]
        res = fn(ps, ts)
        for r, v in zip(refs[nin + npar:], res):
            r[...] = v.astype(r.dtype)

    return pl.pallas_call(
        body, grid=(rows // tt,),
        in_specs=specs + [_full_spec(p.shape) for p in params],
        out_specs=[pl.BlockSpec((tt, w), lambda i: (i, 0)) for (w, _) in outs],
        out_shape=[jax.ShapeDtypeStruct((rows, w), dt) for (w, dt) in outs],
        compiler_params=pltpu.CompilerParams(dimension_semantics=("parallel",)),
        name=name,
    )(*ops, *params)


def _stage_bwd(name, fn, tiles, params, cots, tt, tile_grad, param_grad, joint=None, dtypes=None):
    rows = tiles[0][0].shape[0]
    nblk = rows // tt
    npar = len(params)
    block_of = lambda i: nblk - 1 - i
    specs, ops = _entry_specs(tiles, tt, block_of)
    nin = len(ops)
    flat_cots = [c for group in cots for c in group]
    groups = [len(g) for g in cots]
    ncot = len(flat_cots)
    counts = [len(e[3]) if e[3] else 1 for e in tiles]
    dt_entries = [e for e, g in zip(tiles, tile_grad) if g]
    dp_shapes = [p.shape for p, g in zip(params, param_grad) if g]
    ndt = len(dt_entries)
    carry_w = [e[1] for e in dt_entries if e[3]]
    flags = [g for g, n in zip(tile_grad, counts) for _ in range(n)]
    members, j_width, j_cidx, j_buf = joint if joint else ([], 0, 0, None)
    solo = [k for k in range(ndt) if k not in members]
    nsolo, njoint, nbuf = len(solo), int(bool(members)), int(j_buf is not None)
    j_block = sum(dt_entries[k][1] for k in members)
    dtypes = list(dtypes) if dtypes else [f32] * (nsolo + njoint)

    def body(*refs):
        i = pl.program_id(0)
        p_refs = refs[nin:nin + npar]
        c_refs = refs[nin + npar:nin + npar + ncot]
        base = nin + npar + ncot + nbuf
        dt_refs = refs[base:base + nsolo]
        joint_refs = refs[base + nsolo:base + nsolo + njoint]
        dp_refs = refs[base + nsolo + njoint:base + nsolo + njoint + len(dp_shapes)]
        carry_refs = refs[base + nsolo + njoint + len(dp_shapes):]
        ts = _load_entries(tiles, refs[:nin], block_of(i) == 0)
        ps = [r[...] for r in p_refs]

        def f(dp, dt):
            dp, dt = iter(dp), iter(dt)
            pp = [next(dp) if g else p for p, g in zip(ps, param_grad)]
            tl = [next(dt) if g else t for t, g in zip(ts, flags)]
            return fn(pp, tl)

        _, vjp = jax.vjp(f, [p for p, g in zip(ps, param_grad) if g], [t for t, g in zip(ts, flags) if g])
        cs, j = [], 0
        for n in groups:
            acc = c_refs[j][...].astype(f32)
            for q in range(1, n):
                acc = acc + c_refs[j + q][...].astype(f32)
            cs.append(acc)
            j += n
        gp, gt = vjp(cs)

        @pl.when(i == 0)
        def _():
            for r in dp_refs:
                r[...] = jnp.zeros_like(r)
            for r in carry_refs:
                r[...] = jnp.zeros_like(r)

        gt, k, kc, dxs = list(gt), 0, 0, []
        for e, n in zip(dt_entries, [n for n, g in zip(counts, tile_grad) if g]):
            if e[3]:
                dx, out = _unshift_sum(gt[k:k + n], e[3], carry_refs[kc][...], tt)
                carry_refs[kc][...] = out
                kc += 1
            else:
                dx = gt[k]
            dxs.append(dx)
            k += n
        for r, k in zip(dt_refs, solo):
            r[...] = dxs[k].astype(r.dtype)
        off = 0
        for k in members:
            w = dt_entries[k][1]
            joint_refs[0][:, off:off + w] = dxs[k].astype(joint_refs[0].dtype)
            off += w
        for r, v in zip(dp_refs, gp):
            r[...] += v

    res = pl.pallas_call(
        body, grid=(nblk,),
        in_specs=specs + [_full_spec(p.shape) for p in params]
        + [pl.BlockSpec((tt, w), lambda i, ci=ci: (block_of(i), ci)) for (_, w, ci, *_) in flat_cots]
        + [pl.BlockSpec(memory_space=pl.ANY)] * nbuf,
        out_specs=[pl.BlockSpec((tt, dt_entries[k][1]), lambda i: (block_of(i), 0)) for k in solo]
        + [pl.BlockSpec((tt, j_block), lambda i: (block_of(i), j_cidx))] * njoint
        + [_full_spec(s) for s in dp_shapes],
        out_shape=[jax.ShapeDtypeStruct((rows, dt_entries[k][1]), dt) for k, dt in zip(solo, dtypes)]
        + [jax.ShapeDtypeStruct((rows, j_width), dtypes[-1])] * njoint
        + [jax.ShapeDtypeStruct(s, f32) for s in dp_shapes],
        scratch_shapes=[pltpu.VMEM((HALO, w), f32) for w in carry_w],
        input_output_aliases={nin + npar + ncot: nsolo} if nbuf else {},
        compiler_params=pltpu.CompilerParams(dimension_semantics=("arbitrary",)),
        name=name,
    )(*ops, *params, *[c[0] for c in flat_cots], *([j_buf] if nbuf else []))
    res = list(res)
    return res[:nsolo], res[nsolo + njoint:], (res[nsolo] if njoint else None)


def _whole(a):
    return (a, a.shape[1], 0, None)


def _matmul(name, a, b, mode, out_dtype=f32, tm=1024, tn=1024, tk=1024, shards=1, job=None):
    S = shards
    if mode == "nn":
        M, K = a.shape
        w = b.shape[-1]
    elif mode == "nt":
        M = a.shape[0]
        if S > 1:
            _, N, w = b.shape
            K = S * w
        else:
            N, K = b.shape
            w = K
    else:
        K, M = a.shape
        w = b.shape[1] // S
    if mode != "nt":
        N = S * w
    tm = _pick(M, tm, LANES)
    if mode == "nt":
        tn = _pick(N, tn, LANES)
        tk = _pick(w, tk, LANES)
    else:
        tn = _pick(w, tn, LANES)
        tk = _pick(K, tk, LANES if mode == "nn" else 16)
    nk = K // tk
    nb = w // (tk if mode == "nt" else tn)
    if mode == "nn":
        a_spec = pl.BlockSpec((tm, tk), lambda i, j, k: (i, k))
        if S > 1:
            b_spec = pl.BlockSpec((1, tk, tn), lambda i, j, k: (j // nb, k, j % nb))
        else:
            b_spec = pl.BlockSpec((tk, tn), lambda i, j, k: (k, j))
        dims = (((1,), (0,)), ((), ()))
    elif mode == "nt":
        a_spec = pl.BlockSpec((tm, tk), lambda i, j, k: (i, k))
        if S > 1:
            b_spec = pl.BlockSpec((1, tn, tk), lambda i, j, k: (k // nb, j, k % nb))
        else:
            b_spec = pl.BlockSpec((tn, tk), lambda i, j, k: (j, k))
        dims = (((1,), (1,)), ((), ()))
    else:
        a_spec = pl.BlockSpec((tk, tm), lambda i, j, k: (k, i))
        b_spec = pl.BlockSpec((tk, tn), lambda i, j, k: (k, j))
        dims = (((0,), (0,)), ((), ()))
    if mode == "tn" and S > 1:
        o_spec = pl.BlockSpec((1, tm, tn), lambda i, j, k: (j // nb, i, j % nb))
        o_shape = (S, M, w)
    else:
        o_spec = pl.BlockSpec((tm, tn), lambda i, j, k: (i, j))
        o_shape = (M, N)
    b_lead = S > 1 and mode != "tn"
    o_lead = S > 1 and mode == "tn"

    j_ins, j_outs, j_sems = _job_parts(job)
    nji, njo = len(j_ins), len(j_outs)
    grid = (M // tm, N // tn, nk)

    def run_job(refs):
        step = (pl.program_id(0) * grid[1] + pl.program_id(1)) * grid[2] + pl.program_id(2)
        _job_steps(job, refs[2:2 + nji], refs[3 + nji:3 + nji + njo], refs[len(refs) - 2:], step,
                   grid[0] * grid[1] * grid[2] - 1)

    def body(*refs):
        a_ref, b_ref, o_ref, acc_ref = refs[0], refs[1], refs[2 + nji], refs[3 + nji + njo]
        run_job(refs)
        k = pl.program_id(2)

        @pl.when(k == 0)
        def _():
            acc_ref[...] = jnp.zeros_like(acc_ref)

        bv = b_ref[0] if b_lead else b_ref[...]
        acc_ref[...] += lax.dot_general(a_ref[...].astype(bf16), bv.astype(bf16), dims, preferred_element_type=f32)

        @pl.when(k == nk - 1)
        def _():
            if o_lead:
                o_ref[0] = acc_ref[...].astype(o_ref.dtype)
            else:
                o_ref[...] = acc_ref[...].astype(o_ref.dtype)

    def body_one_step(*refs):
        a_ref, b_ref, o_ref = refs[0], refs[1], refs[2 + nji]
        run_job(refs)
        bv = b_ref[0] if b_lead else b_ref[...]
        res = lax.dot_general(a_ref[...].astype(bf16), bv.astype(bf16), dims, preferred_element_type=f32)
        if o_lead:
            o_ref[0] = res.astype(o_ref.dtype)
        else:
            o_ref[...] = res.astype(o_ref.dtype)

    res = pl.pallas_call(
        body if nk > 1 else body_one_step, grid=grid,
        in_specs=[a_spec, b_spec] + [_ANY] * nji,
        out_specs=[o_spec] + [_ANY] * njo,
        out_shape=[jax.ShapeDtypeStruct(o_shape, out_dtype)] + j_outs,
        scratch_shapes=([pltpu.VMEM((tm, tn), f32)] if nk > 1 else []) + j_sems,
        compiler_params=pltpu.CompilerParams(
            dimension_semantics=("arbitrary",) * 3 if job else ("parallel", "parallel", "arbitrary")),
        name=name,
    )(a, b, *j_ins)
    return res if job else res[0]


def _make_bmm(precision):
    if precision is None:
        cast, kw = (lambda v: v.astype(bf16)), {}
    else:
        cast, kw = (lambda v: v), {"precision": precision}

    def nn(a, b):
        return jnp.einsum("hij,hjk->hik", cast(a), cast(b), preferred_element_type=f32, **kw)

    def nt(a, b):
        return jnp.einsum("hik,hjk->hij", cast(a), cast(b), preferred_element_type=f32, **kw)

    def tn(a, b):
        return jnp.einsum("hki,hkj->hij", cast(a), cast(b), preferred_element_type=f32, **kw)

    if precision is not None:
        return nn, nt, tn
    nn_v, nt_v, tn_v = jax.custom_vjp(nn), jax.custom_vjp(nt), jax.custom_vjp(tn)
    keep = lambda f: (lambda a, b: (f(a, b), (a, b)))
    nn_v.defvjp(keep(nn), lambda r, ct: (nt(ct, r[1]), tn(r[0], ct)))
    nt_v.defvjp(keep(nt), lambda r, ct: (nn(ct, r[1]), tn(ct, r[0])))
    tn_v.defvjp(keep(tn), lambda r, ct: (nt(r[1], ct), nn(r[0], ct)))
    return nn_v, nt_v, tn_v


_bmm, _bmm_nt, _bmm_tn = _make_bmm(None)
_bmm_exact = _make_bmm(lax.Precision.HIGH)[0]


def _masks(n):
    r = lax.broadcasted_iota(jnp.int32, (n, n), 0)
    c = lax.broadcasted_iota(jnp.int32, (n, n), 1)
    return (r >= c)[None], (r > c)[None], (r == c)[None]


_INV_BLOCK = 8


def _nilpotent_inverse(m, eye):
    p = eye + m
    for _ in range(2):
        m = _bmm(m, m)
        p = p + _bmm(p, m)
    return p


def _neumann_inverse_impl(m):
    n = m.shape[1]
    assert n == _INV_BLOCK * _INV_BLOCK
    r = lax.broadcasted_iota(jnp.int32, (n, n), 0)
    c = lax.broadcasted_iota(jnp.int32, (n, n), 1)
    eye = (r == c).astype(f32)[None]
    inside = jnp.where((r // _INV_BLOCK == c // _INV_BLOCK)[None], m, 0.0)
    d_inv = _nilpotent_inverse(inside, eye)
    return _bmm(_nilpotent_inverse(_bmm(d_inv, m - inside), eye), d_inv)


@jax.custom_vjp
def _neumann_inverse(m):
    return _neumann_inverse_impl(m)


def _neumann_inverse_fwd(m):
    p = _neumann_inverse_impl(m)
    return p, p


def _neumann_inverse_bwd(p, ct):
    return (_bmm_tn(p, _bmm_nt(ct, p)),)


_neumann_inverse.defvjp(_neumann_inverse_fwd, _neumann_inverse_bwd)


@jax.custom_vjp
def _given_inverse(m, p):
    return p


_given_inverse.defvjp(lambda m, p: (p, p), lambda p, ct: (_neumann_inverse_bwd(p, ct)[0], jnp.zeros_like(p)))


def _gdn_chunk(s, q, k, v, beta, gc, gr, gl, p=None):
    n = q.shape[1]
    causal, strict, _ = _masks(n)
    decay = jnp.where(causal, jnp.exp(jnp.where(causal, gc - gr, 0.0)), 0.0)
    kb = k * beta
    vb = v * beta
    lower = jnp.where(strict, _bmm_nt(kb, k) * decay, 0.0)
    t_mat = _neumann_inverse(-lower) if p is None else _given_inverse(-lower, p)
    egc = jnp.exp(gc)
    u = _bmm(t_mat, vb)
    w = _bmm(t_mat, kb * egc)
    attn = jnp.where(causal, _bmm_nt(q, k) * decay, 0.0)
    v_new = u - _bmm(w, s)
    o = _bmm(q * egc, s) + _bmm(attn, v_new)
    k_dec = k * jnp.exp(gl - gc)
    s_new = s * jnp.exp(gl) + _bmm_tn(k_dec, v_new)
    return s_new, o, t_mat


def _rwkv_chunk(s, r, lw, k, v, a, b, p=None):
    n = r.shape[1]
    causal, strict, _ = _masks(n)
    tri = jnp.broadcast_to(causal.astype(f32), (r.shape[0], n, n))
    lc = _bmm_exact(tri, lw)
    ein = jnp.exp(lc)
    eout = jnp.exp(-lc)
    a_t = a * jnp.exp(lc - lw)
    b_t = b * eout
    k_t = k * eout
    r_t = r * ein
    a_ab = jnp.where(strict, _bmm_nt(a_t, b_t), 0.0)
    a_ak = jnp.where(strict, _bmm_nt(a_t, k_t), 0.0)
    inv = _neumann_inverse(a_ab) if p is None else _given_inverse(a_ab, p)
    u = _bmm(inv, _bmm_nt(a_t, s) + _bmm(a_ak, v))
    y = (_bmm_nt(r_t, s) + _bmm(jnp.where(causal, _bmm_nt(r_t, b_t), 0.0), u)
         + _bmm(jnp.where(causal, _bmm_nt(r_t, k_t), 0.0), v))
    e_last = jnp.exp(jnp.sum(lw, axis=1, keepdims=True))
    s_new = s * e_last + _bmm_tn(u, b_t * e_last) + _bmm_tn(v, k_t * e_last)
    return s_new, y, inv


def _heads_in(ref, rows):
    return jnp.stack([ref[rows, h * HEAD_DIM:(h + 1) * HEAD_DIM] for h in range(HEADS)], axis=0)


def _heads_out(ref, rows, val):
    for h in range(HEADS):
        ref[rows, h * HEAD_DIM:(h + 1) * HEAD_DIM] = val[h]


def _gdn_scalars(bt, gt):
    n = bt.shape[0]
    gtt = gt.T
    hs = range(HEADS)
    return [jnp.stack([bt[:, h:h + 1] for h in hs], axis=0),
            jnp.stack([gt[:, HEADS + h:HEADS + h + 1] for h in hs], axis=0),
            jnp.stack([gtt[HEADS + h:HEADS + h + 1, :] for h in hs], axis=0),
            jnp.stack([gt[n - 1:n, HEADS + h:HEADS + h + 1] for h in hs], axis=0)]


def _gdn_scalars_back(dbeta, dgc, dgr, dgl):
    n = dbeta.shape[1]
    lane = lax.broadcasted_iota(jnp.int32, (n, LANES), 1)
    row = lax.broadcasted_iota(jnp.int32, (n, LANES), 0)
    sub = lax.broadcasted_iota(jnp.int32, (LANES, n), 0)
    db = jnp.zeros((n, LANES), f32)
    dg = jnp.zeros((n, LANES), f32)
    dgt = jnp.zeros((LANES, n), f32)
    for h in range(HEADS):
        db = jnp.where(lane == h, dbeta[h], db)
        dg = jnp.where(lane == HEADS + h, dgc[h] + jnp.where(row == n - 1, dgl[h], 0.0), dg)
        dgt = jnp.where(sub == HEADS + h, dgr[h], dgt)
    return [db, dg + dgt.T]


SCAN_GROUP = 2


def _scan_steps(t):
    g = SCAN_GROUP if (t // CHUNK) % SCAN_GROUP == 0 else 1
    return g, t // (CHUNK * g)


def _scan_spec(width, g, n, reverse):
    if reverse:
        return pl.BlockSpec((g * CHUNK, width), lambda i: (n - 1 - i, 0))
    return pl.BlockSpec((g * CHUNK, width), lambda i: (i, 0))


def _hist_spec(g, n, reverse):
    blk = (g, HEADS, HEAD_DIM, HEAD_DIM)
    if reverse:
        return pl.BlockSpec(blk, lambda i: (n - 1 - i, 0, 0, 0))
    return pl.BlockSpec(blk, lambda i: (i, 0, 0, 0))


def _job_parts(job):
    if job is None:
        return [], [], []
    sems = [pltpu.SemaphoreType.DMA((job["nsem"],)), pltpu.SemaphoreType.DMA((job["nsem"],))]
    return list(job["ins"]), list(job["out_shapes"]), sems


def _job_steps(job, in_refs, out_refs, sems, step, last):
    if job is None:
        return

    @pl.when(step == 0)
    def _():
        job["start"](in_refs, out_refs, *sems)

    @pl.when(step == last)
    def _():
        job["finish"](in_refs, out_refs, *sems)


def _scan_fwd(name, fn, rows_in, scal_in=(), job=None):
    t = rows_in[0].shape[0]
    grp, n = _scan_steps(t)
    nr, ns = len(rows_in), len(scal_in)
    j_ins, j_outs, j_sems = _job_parts(job)
    nji, njo = len(j_ins), len(j_outs)

    def body(*refs):
        o_ref, sh_ref, ph_ref = refs[nr + ns + nji:nr + ns + nji + 3]
        s_scr = refs[nr + ns + nji + 3 + njo]
        _job_steps(job, refs[nr + ns:nr + ns + nji], refs[nr + ns + nji + 3:nr + ns + nji + 3 + njo],
                   refs[nr + ns + nji + 3 + njo + 1:], pl.program_id(0), n - 1)

        @pl.when(pl.program_id(0) == 0)
        def _():
            s_scr[...] = jnp.zeros_like(s_scr)

        s = s_scr[...]
        for sub in range(grp):
            rows = slice(sub * CHUNK, (sub + 1) * CHUNK)
            sh_ref[sub] = s
            ins = [_heads_in(r, rows) for r in refs[:nr]]
            if ns:
                ins += _gdn_scalars(*[r[rows, :] for r in refs[nr:nr + ns]])
            s, o, p = fn(s, *ins)
            _heads_out(o_ref, rows, o)
            ph_ref[sub] = p
        s_scr[...] = s

    return pl.pallas_call(
        body, grid=(n,),
        in_specs=[_scan_spec(a.shape[1], grp, n, False) for a in (*rows_in, *scal_in)] + [_ANY] * nji,
        out_specs=[_scan_spec(WIDTH, grp, n, False), _hist_spec(grp, n, False), _hist_spec(grp, n, False)] + [_ANY] * njo,
        out_shape=[jax.ShapeDtypeStruct((t, WIDTH), f32)]
        + [jax.ShapeDtypeStruct((t // CHUNK, HEADS, HEAD_DIM, HEAD_DIM), f32)] * 2 + j_outs,
        scratch_shapes=[pltpu.VMEM((HEADS, HEAD_DIM, HEAD_DIM), f32)] + j_sems,
        compiler_params=pltpu.CompilerParams(dimension_semantics=("arbitrary",)),
        name=name,
    )(*rows_in, *scal_in, *j_ins)


def _scan_bwd(name, fn, rows_in, scal_in, s_hist, p_hist, d_out, job=None):
    t = rows_in[0].shape[0]
    grp, n = _scan_steps(t)
    nr, ns = len(rows_in), len(scal_in)
    j_ins, j_outs, j_sems = _job_parts(job)
    nji, njo = len(j_ins), len(j_outs)

    def body(*refs):
        sh_ref, ph_ref, do_ref = refs[nr + ns:nr + ns + 3]
        base = nr + ns + 3 + nji
        g_refs = refs[base:base + nr + ns]
        ds_scr = refs[base + nr + ns + njo]
        _job_steps(job, refs[nr + ns + 3:base], refs[base + nr + ns:base + nr + ns + njo],
                   refs[base + nr + ns + njo + 1:], pl.program_id(0), n - 1)

        @pl.when(pl.program_id(0) == 0)
        def _():
            ds_scr[...] = jnp.zeros_like(ds_scr)

        ds = ds_scr[...]
        for sub in reversed(range(grp)):
            rows = slice(sub * CHUNK, (sub + 1) * CHUNK)
            ins = [_heads_in(r, rows) for r in refs[:nr]]
            if ns:
                ins += _gdn_scalars(*[r[rows, :] for r in refs[nr:nr + ns]])
            p = ph_ref[sub]
            _, vjp = jax.vjp(lambda s, *a, p=p: fn(s, *a, p=p)[:2], sh_ref[sub], *ins)
            g = vjp((ds, _heads_in(do_ref, rows)))
            ds = g[0]
            for r, v in zip(g_refs[:nr], g[1:1 + nr]):
                _heads_out(r, rows, v)
            if ns:
                for r, v in zip(g_refs[nr:], _gdn_scalars_back(*g[1 + nr:])):
                    r[rows, :] = v
        ds_scr[...] = ds

    arrs = (*rows_in, *scal_in)
    return pl.pallas_call(
        body, grid=(n,),
        in_specs=[_scan_spec(a.shape[1], grp, n, True) for a in arrs]
        + [_hist_spec(grp, n, True), _hist_spec(grp, n, True), _scan_spec(WIDTH, grp, n, True)] + [_ANY] * nji,
        out_specs=[_scan_spec(a.shape[1], grp, n, True) for a in arrs] + [_ANY] * njo,
        out_shape=[jax.ShapeDtypeStruct(a.shape, f32) for a in arrs] + j_outs,
        scratch_shapes=[pltpu.VMEM((HEADS, HEAD_DIM, HEAD_DIM), f32)] + j_sems,
        compiler_params=pltpu.CompilerParams(dimension_semantics=("arbitrary",)),
        name=name,
    )(*arrs, s_hist, p_hist, d_out, *j_ins)


def _fn_norm_mod(ps, ts):
    nw, shift, scale = ps
    (x,) = ts
    return [_rms(x, nw, NORM_EPS) * (1.0 + scale) + shift]


def _fn_norm_mod_and_x(ps, ts):
    return _fn_norm_mod(ps, ts) + [ts[0]]


def _fn_resid_norm_mod(ps, ts):
    gate, nw, shift, scale = ps
    x, mo = ts
    x1 = x + gate * mo
    return [x1, _rms(x1, nw, NORM_EPS) * (1.0 + scale) + shift]


def _fn_gdn_pre(ps, ts):
    cw = ps[:12]
    alog, dtb, seg, tri = ps[12:]
    ba = ts[12]
    outs = []
    for part in range(3):
        x = ts[4 * part:4 * part + 4]
        w = cw[4 * part:4 * part + 4]
        conv = w[3] * x[0] + w[2] * x[1] + w[1] * x[2] + w[0] * x[3]
        u = _silu(conv)
        if part < 2:
            u = u * lax.rsqrt(_segsum(u * u, seg) + 1e-6)
            if part == 0:
                u = u * (HEAD_DIM ** -0.5)
        outs.append(u)
    beta = jax.nn.sigmoid(ba)
    g = -jnp.exp(alog) * _softplus(ba + dtb)
    gc = _xdot_l(tri, g)
    return outs + [beta, gc]


def _fn_gdn_post(ps, ts):
    ow, seg = ps
    o, z = ts
    ms = _segsum(o * o, seg) * (1.0 / HEAD_DIM)
    return [o * lax.rsqrt(ms + NORM_EPS) * ow * _silu(z)]


def _fn_rwkv_pre(ps, ts):
    mu_r, mu_k, mu_v, mu_l, mu_g, w0, w2p, a0, a2p, g2p, k_k, k_a, seg = ps
    r0, r1, k0, k1, v0, v1, l0, l1, g0, g1 = ts
    xr = r0 + (r1 - r0) * mu_r
    xk = k0 + (k1 - k0) * mu_k
    xv = v0 + (v1 - v0) * mu_v
    xl = l0 + (l1 - l0) * mu_l
    xg = g0 + (g1 - g0) * mu_g
    w = -_softplus(-(w0 + _bdot(jnp.tanh(xl), w2p))) - 0.5
    lw = -jnp.exp(w)
    a = jax.nn.sigmoid(a0 + _bdot(xl, a2p))
    g = _bdot(jax.nn.sigmoid(xg), g2p)
    kk = xk * k_k
    kk = kk * lax.rsqrt(_segsum(kk * kk, seg) + 1e-6)
    k2 = xk * (1.0 + (a - 1.0) * k_a)
    return [xr, lw, k2, xv, -kk, kk * a, g]


def _fn_rwkv_post(ps, ts):
    lw_, lb_, rk, seg = ps
    y, r, k2, v, g = ts
    inv = 1.0 / HEAD_DIM
    yc = y - _segsum(y, seg) * inv
    var = _segsum(yc * yc, seg) * inv
    yn = yc * lax.rsqrt(var + LNX_EPS) * lw_ + lb_
    bonus = _segsum(r * k2 * rk, seg) * v
    return [(yn + bonus) * g]


def _fn_merge(ps, ts):
    gla, glb, ya, yb = ts
    return [jax.nn.sigmoid(gla) * ya + jax.nn.sigmoid(glb) * yb]


def _fn_convglu(ps, ts):
    c0, c1, c2 = ps
    g0, g1, g2, up = ts
    return [_silu(c2 * g0 + c1 * g1 + c0 * g2) * up]


def _final_stage(x1, fo, tgt, gate2, nfw, tt):
    rows, d = x1.shape

    def loss_fn(gate, nw, xa, fa, tg):
        y = _rms(xa + gate * fa, nw, NORM_EPS)
        err = (y - tg) ** 2
        return 0.5 * jnp.sum(jnp.mean(err, axis=-1, keepdims=True), axis=0, keepdims=True)

    def body(x_ref, f_ref, t_ref, g_ref, w_ref, dx_ref, df_ref, dg_ref, dw_ref, l_ref):
        i = pl.program_id(0)
        args = (g_ref[...], w_ref[...], x_ref[...], f_ref[...])
        tg = t_ref[...]
        lv, vjp = jax.vjp(lambda g, w, xa, fa: loss_fn(g, w, xa, fa, tg), *args)
        dg, dw, dx, df = vjp(jnp.ones((1, 1), f32))
        dx_ref[...] = dx
        df_ref[...] = df.astype(df_ref.dtype)

        @pl.when(i == 0)
        def _():
            dg_ref[...] = jnp.zeros_like(dg_ref)
            dw_ref[...] = jnp.zeros_like(dw_ref)
            l_ref[...] = jnp.zeros_like(l_ref)

        dg_ref[...] += dg
        dw_ref[...] += dw
        l_ref[...] += jnp.broadcast_to(lv, l_ref.shape)

    row = pl.BlockSpec((tt, d), lambda i: (i, 0))
    vec = pl.BlockSpec((1, d), lambda i: (0, 0))
    return pl.pallas_call(
        body, grid=(rows // tt,),
        in_specs=[row, row, row, vec, vec],
        out_specs=[row, row, vec, vec, pl.BlockSpec((1, LANES), lambda i: (0, 0))],
        out_shape=[jax.ShapeDtypeStruct((rows, d), f32), jax.ShapeDtypeStruct((rows, d), bf16)]
        + [jax.ShapeDtypeStruct((1, d), f32)] * 2
        + [jax.ShapeDtypeStruct((1, LANES), f32)],
        compiler_params=pltpu.CompilerParams(dimension_semantics=("arbitrary",)),
        name="loss_head",
    )(x1, fo, tgt, gate2, nfw)


def _ada_fwd(c_all, w_shard, b_cols):
    def body(c_ref, w_ref, b_ref, cond_ref, mod_ref):
        cond = _silu(c_ref[...])
        cond_ref[...] = cond
        mod_ref[...] = jnp.dot(cond.astype(bf16), w_ref[...].astype(bf16), preferred_element_type=f32) + b_ref[...]

    n = w_shard.shape[1]
    return pl.pallas_call(
        body, out_shape=[jax.ShapeDtypeStruct(c_all.shape, f32), jax.ShapeDtypeStruct((c_all.shape[0], n), f32)],
        name="ada_fwd",
    )(c_all, w_shard, b_cols)


def _adamw(name, w, g, m, v):
    rows, width = w.shape
    tt = _pick(rows, 256, 8)
    c1 = 1.0 - ADAM_B1 ** ADAM_STEP
    c2 = 1.0 - ADAM_B2 ** ADAM_STEP

    def body(w_ref, g_ref, m_ref, v_ref, d_ref, mo_ref, vo_ref):
        gg = g_ref[...]
        mn = ADAM_B1 * m_ref[...] + (1.0 - ADAM_B1) * gg
        vn = ADAM_B2 * v_ref[...] + (1.0 - ADAM_B2) * (gg * gg)
        m_hat = mn / c1
        v_hat = vn / c2
        d_ref[...] = -ADAM_LR * (m_hat / (jnp.sqrt(v_hat) + ADAM_EPS) + ADAM_WD * w_ref[...])
        mo_ref[...] = mn
        vo_ref[...] = vn

    spec = pl.BlockSpec((tt, width), lambda i: (i, 0))
    return pl.pallas_call(
        body, grid=(rows // tt,), in_specs=[spec] * 4, out_specs=[spec] * 3,
        out_shape=[jax.ShapeDtypeStruct((rows, width), f32)] * 3,
        compiler_params=pltpu.CompilerParams(dimension_semantics=("parallel",)),
        name=name,
    )(w, g, m, v)


def _place():
    return lax.axis_index("x"), lax.axis_index("y"), lax.axis_index("c")


def _ag8(name, blk):
    m, w = blk.shape

    def body(x_ref, out_ref, send_sems, recv_sems, local_sem):
        x, y, c = _place()
        me, sibling = (x, y, c), (x, y, 1 - c)
        chips = _other_chips(x, y)

        def slot(px, py, pc):
            return out_ref.at[4 * px + 2 * py + pc]

        def copy(k, block, to, src=None):
            return pltpu.make_async_remote_copy(src_ref=slot(*block) if src is None else src, dst_ref=slot(*block),
                                                send_sem=send_sems.at[k], recv_sem=recv_sems.at[k], device_id=to,
                                                device_id_type=MESH_ID)

        mine = pltpu.make_async_copy(x_ref, slot(*me), local_sem)
        mine.start()
        first = [copy(0, me, sibling, src=x_ref)] + [copy(1 + j, me, (*chip, c), src=x_ref) for j, chip in enumerate(chips)]
        for cp in first:
            cp.start()
        passed = [copy(4 + j, (*chip, c), sibling) for j, chip in enumerate(chips)]
        for j, chip in enumerate(chips):
            copy(1 + j, (*chip, c), me).wait_recv()
            passed[j].start()
        copy(0, sibling, me).wait_recv()
        for j, chip in enumerate(chips):
            copy(4 + j, (*chip, 1 - c), me).wait_recv()
        for cp in first + passed:
            cp.wait_send()
        mine.wait()

    return pl.pallas_call(
        body, out_shape=jax.ShapeDtypeStruct((8, m, w), blk.dtype),
        in_specs=[pl.BlockSpec(memory_space=pltpu.VMEM)], out_specs=pl.BlockSpec(memory_space=pltpu.VMEM),
        scratch_shapes=[pltpu.SemaphoreType.DMA((7,)), pltpu.SemaphoreType.DMA((7,)), pltpu.SemaphoreType.DMA],
        name=name,
    )(blk)


def _other_chips(x, y):
    return [(1 - x, y), (x, 1 - y), (1 - x, 1 - y)]


_ANY = pl.BlockSpec(memory_space=pl.ANY)


def _rcopy(src, dst, send_sems, recv_sems, k, dev):
    return pltpu.make_async_remote_copy(src_ref=src, dst_ref=dst, send_sem=send_sems.at[k], recv_sem=recv_sems.at[k],
                                        device_id=dev, device_id_type=MESH_ID)


def _run_job(name, job):
    j_ins, j_outs, j_sems = _job_parts(job)
    n = len(j_ins)

    def body(*refs):
        job["start"](refs[:n], refs[n:n + len(j_outs)], *refs[n + len(j_outs):])
        job["finish"](refs[:n], refs[n:n + len(j_outs)], *refs[n + len(j_outs):])

    return pl.pallas_call(body, out_shape=j_outs, in_specs=[_ANY] * n, out_specs=[_ANY] * len(j_outs),
                          scratch_shapes=j_sems, name=name)(*j_ins)


def _ag4_job(ws):
    n = len(ws)

    def plan(w_refs, out_refs, send_sems, recv_sems):
        x, y, c = _place()
        chip = 2 * x + y
        sibling = (x, y, 1 - c)
        chips = _other_chips(x, y)
        mine = [pl.ds(c * (w.shape[0] // 2), w.shape[0] // 2) for w in ws]
        other = [pl.ds((1 - c) * (w.shape[0] // 2), w.shape[0] // 2) for w in ws]
        rc = lambda src, dst, k, dev: _rcopy(src, dst, send_sems, recv_sems, k, dev)
        first = [rc(w_refs[t].at[mine[t]], out_refs[t].at[chip, mine[t]], 7 * t + k, (px, py, c))
                 for t in range(n) for k, (px, py) in enumerate(chips)]
        own = [rc(w_refs[t], out_refs[t].at[chip], 7 * t + 6, sibling) for t in range(n)]
        landed = [rc(out_refs[t].at[2 * px + py, mine[t]], out_refs[t].at[2 * px + py, mine[t]], 7 * t + k, (px, py, c))
                  for t in range(n) for k, (px, py) in enumerate(chips)]
        forward = [rc(out_refs[t].at[2 * px + py, mine[t]], out_refs[t].at[2 * px + py, mine[t]], 7 * t + 3 + k, sibling)
                   for t in range(n) for k, (px, py) in enumerate(chips)]
        handed = [rc(out_refs[t].at[2 * px + py, other[t]], out_refs[t].at[2 * px + py, other[t]], 7 * t + 3 + k, sibling)
                  for t in range(n) for k, (px, py) in enumerate(chips)]
        return first, own, landed, forward, handed

    def start(*refs):
        first, own, _, _, _ = plan(*refs)
        for cp in first + own:
            cp.start()

    def finish(*refs):
        first, own, landed, forward, handed = plan(*refs)
        for arrived, fw in zip(landed, forward):
            arrived.wait_recv()
            fw.start()
        for cp in handed + own:
            cp.wait_recv()
        for cp in first + own + forward:
            cp.wait_send()

    return dict(ins=ws, out_shapes=[jax.ShapeDtypeStruct((4,) + w.shape, w.dtype) for w in ws], nsem=7 * n,
                start=start, finish=finish)


def _rs_sibling_swap(gs, tag):
    n = len(gs)

    def body(*refs):
        g_refs, recv_refs = refs[:n], refs[n:2 * n]
        send_sems, recv_sems = refs[2 * n:]
        x, y, c = _place()
        copies = []
        for t in range(n):
            rh = gs[t].shape[1] // 2
            for s_ in range(4):
                cp = _rcopy(g_refs[t].at[s_, pl.ds((1 - c) * rh, rh)], recv_refs[t].at[s_], send_sems, recv_sems,
                            4 * t + s_, (x, y, 1 - c))
                cp.start()
                copies.append(cp)
        for cp in copies:
            cp.wait_recv()
        for cp in copies:
            cp.wait_send()

    return pl.pallas_call(
        body, out_shape=[jax.ShapeDtypeStruct((4, g.shape[1] // 2, g.shape[2]), g.dtype) for g in gs],
        in_specs=[_ANY] * n, out_specs=[_ANY] * n,
        scratch_shapes=[pltpu.SemaphoreType.DMA((4 * n,)), pltpu.SemaphoreType.DMA((4 * n,))],
        name="grads_sibling_swap_" + tag,
    )(*gs)


def _chip_exchange_job(ps):
    n = len(ps)

    def plan(p_refs, recv_refs, send_sems, recv_sems):
        x, y, c = _place()
        return [_rcopy(p_refs[t].at[2 * px + py], recv_refs[t].at[k], send_sems, recv_sems, 3 * t + k, (px, py, c))
                for t in range(n) for k, (px, py) in enumerate(_other_chips(x, y))]

    def start(*refs):
        for cp in plan(*refs):
            cp.start()

    def finish(*refs):
        copies = plan(*refs)
        for cp in copies:
            cp.wait_recv()
        for cp in copies:
            cp.wait_send()

    return dict(ins=ps, out_shapes=[jax.ShapeDtypeStruct((3,) + p.shape[1:], p.dtype) for p in ps], nsem=3 * n,
                start=start, finish=finish)


_JOIN_PIECES = 4


def _rs_sibling_join(qs):
    n = len(qs)
    npc = _JOIN_PIECES

    def body(*refs):
        q_refs, out_refs = refs[:n], refs[n:2 * n]
        send_sems, recv_sems = refs[2 * n:]
        x, y, c = _place()
        copies = []
        for t in range(n):
            rh = qs[t].shape[0] // 2
            pr = rh // npc
            for i in range(npc):
                rows = pl.ds(c * rh + i * pr, pr)
                cp = _rcopy(q_refs[t].at[rows], out_refs[t].at[rows], send_sems, recv_sems, npc * t + i,
                            (x, y, 1 - c))
                cp.start()
                copies.append(cp)
        for t in range(n):
            rh = qs[t].shape[0] // 2
            pr = rh // npc
            for i in range(npc):
                rows = pl.ds((1 - c) * rh + i * pr, pr)
                _rcopy(q_refs[t].at[rows], out_refs[t].at[rows], send_sems, recv_sems, npc * t + i,
                       (x, y, 1 - c)).wait_recv()
        for cp in copies:
            cp.wait_send()

    return pl.pallas_call(
        body, out_shape=[jax.ShapeDtypeStruct(q.shape, q.dtype) for q in qs],
        in_specs=[_ANY] * n, out_specs=[_ANY] * n, input_output_aliases={t: t for t in range(n)},
        scratch_shapes=[pltpu.SemaphoreType.DMA((npc * n,)), pltpu.SemaphoreType.DMA((npc * n,))],
        name="grads_sibling_join",
    )(*qs)


def _add_half(name, g, recv, ci):
    S, r, w = g.shape
    rh = r // 2
    tt = _pick(rh, 256, 16)
    nb = rh // tt

    def body(c_ref, a_ref, b_ref, o_ref, ob_ref):
        v = a_ref[...] + b_ref[...]
        o_ref[...] = v
        ob_ref[...] = v.astype(bf16)

    blk = pl.BlockSpec((1, tt, w), lambda s_, i, cr: (s_, i, 0))
    grid_spec = pltpu.PrefetchScalarGridSpec(
        num_scalar_prefetch=1, grid=(S, nb),
        in_specs=[pl.BlockSpec((1, tt, w), lambda s_, i, cr: (s_, cr[0] * nb + i, 0)), blk],
        out_specs=[blk, blk])
    return pl.pallas_call(body, grid_spec=grid_spec,
                          out_shape=[jax.ShapeDtypeStruct((S, rh, w), f32), jax.ShapeDtypeStruct((S, rh, w), bf16)],
                          name=name)(ci.reshape(1).astype(jnp.int32), g, recv)


def _sum_chip(name, pair, others, chip, ci):
    _, rh, w = pair.shape
    tt = _pick(rh, 128, 16)
    nb = rh // tt

    def body(chip_ref, core_ref, a_ref, b_ref, o_ref):
        o_ref[...] = ((a_ref[0] + b_ref[0].astype(f32)) + b_ref[1].astype(f32)) + b_ref[2].astype(f32)

    grid_spec = pltpu.PrefetchScalarGridSpec(
        num_scalar_prefetch=2, grid=(nb,),
        in_specs=[pl.BlockSpec((1, tt, w), lambda i, ch, co: (ch[0], i, 0)),
                  pl.BlockSpec((3, tt, w), lambda i, ch, co: (0, i, 0))],
        out_specs=pl.BlockSpec((tt, w), lambda i, ch, co: (co[0] * nb + i, 0)))
    return pl.pallas_call(body, grid_spec=grid_spec, out_shape=jax.ShapeDtypeStruct((2 * rh, w), f32),
                          name=name)(chip.reshape(1).astype(jnp.int32), ci.reshape(1).astype(jnp.int32), pair, others)


def _sum_devices(gathered, head_row):
    _, rows, width = gathered.shape

    def body(g_ref, out_ref, head_ref):
        acc = g_ref[0]
        for d in range(1, 8):
            acc = acc + g_ref[d]
        out_ref[...] = acc
        row = acc[head_row:head_row + 1, :]
        hs = row[:, 0:HEAD_DIM]
        for h in range(1, HEADS):
            hs = hs + row[:, h * HEAD_DIM:(h + 1) * HEAD_DIM]
        head_ref[...] = jnp.zeros_like(head_ref)
        head_ref[0:1, 0:HEAD_DIM] = hs

    return pl.pallas_call(
        body, out_shape=[jax.ShapeDtypeStruct((rows, width), f32), jax.ShapeDtypeStruct((8, LANES), f32)],
        name="small_grads_sum",
    )(gathered)


def _pack(arrs, rows_mult, dtype):
    flat = jnp.concatenate([a.reshape(-1).astype(dtype) for a in arrs])
    per = PACK_W * rows_mult
    total = -(-flat.shape[0] // per) * per
    return jnp.pad(flat, (0, total - flat.shape[0])).reshape(total // PACK_W, PACK_W)


def _unpack(buf, shapes):
    flat = buf.reshape(-1)
    out, off = [], 0
    for s in shapes:
        n = int(np.prod(s))
        out.append(flat[off:off + n].reshape(s))
        off += n
    return out


_BIG = ["w_in", "w_branch_gdn", "w_branch_rwkv", "w_out", "w_ffn_in", "w_ffn_out"]
_MID = ["conv_gdn", "conv_ffn", "w2", "a2", "g2"]
_SMALL =["b_ada", "norm1_w", "a_log", "dt_bias", "onorm_gdn", "mu_rwkv", "w0", "a0", "k_k", "k_a", "r_k", "lnx_w",
          "lnx_b", "norm2_w", "norm_f_w"]
_ORDER = ["w_ada", "b_ada", "norm1_w", "w_in", "conv_gdn", "a_log", "dt_bias", "onorm_gdn", "w_branch_gdn", "mu_rwkv",
          "w0", "w2", "a0", "a2", "g2", "k_k", "k_a", "r_k", "lnx_w", "lnx_b", "w_branch_rwkv", "w_out", "norm2_w",
          "w_ffn_in", "conv_ffn", "w_ffn_out", "norm_f_w"]


_WIN_SEGMENTS = [(0, 1536, 0), (2064, 3600, 1536), (1536, 2048, 3072), (3600, 3728, 3584), (2048, 2064, 3712),
                 (3728, 3888, 3840), (3888, 5936, 4096)]
_WIN_PADDED = 6144
_COL_QKV, _COL_RKV, _COL_Z = (0, 1, 2), (3, 4, 5), 6
_COL_LORA, _COL_BA = 28, 29
_COL_GATE_LORA = 15
_COL_GL = (4, 5)
_JOINT_QKV, _JOINT_RKV, _JOINT_Z, _JOINT_GL = 0, 1, 6, 2
_SMALL_BLOCKS_AT = 3584


def _win_pad(shards):
    n = shards.shape[2]
    parts, at = [], 0
    for lo, hi, dst in _WIN_SEGMENTS:
        if dst > at:
            parts.append(jnp.zeros((shards.shape[1], dst - at), shards.dtype))
        c = lo
        while c < hi:
            j = c // n
            e = min(hi, (j + 1) * n)
            parts.append(shards[j][:, c - j * n:e - j * n])
            c = e
        at = dst + hi - lo
    if at < _WIN_PADDED:
        parts.append(jnp.zeros((shards.shape[1], _WIN_PADDED - at), shards.dtype))
    return jnp.concatenate(parts, axis=1)


def _win_unpad_shards(g, n):
    shards = []
    for j in range(4):
        parts = []
        for lo, hi, dst in sorted(_WIN_SEGMENTS):
            a, b = max(lo, j * n), min(hi, (j + 1) * n)
            if a < b:
                parts.append(g[:, dst + a - lo:dst + b - lo])
        shards.append(jnp.concatenate(parts, axis=1))
    return jnp.stack(shards)


def kernel(x, c, w_ada, b_ada, norm1_w, w_in, conv_gdn, a_log, dt_bias, onorm_gdn, w_branch_gdn, mu_rwkv, w0, w2, a0, a2, g2, k_k, k_a, r_k, lnx_w, lnx_b, w_branch_rwkv, w_out, norm2_w, w_ffn_in, conv_ffn, w_ffn_out, norm_f_w, loss_target, m_w_ada, m_b_ada, m_norm1_w, m_w_in, m_conv_gdn, m_a_log, m_dt_bias, m_onorm_gdn, m_w_branch_gdn, m_mu_rwkv, m_w0, m_w2, m_a0, m_a2, m_g2, m_k_k, m_k_a, m_r_k, m_lnx_w, m_lnx_b, m_w_branch_rwkv, m_w_out, m_norm2_w, m_w_ffn_in, m_conv_ffn, m_w_ffn_out, m_norm_f_w, v_w_ada, v_b_ada, v_norm1_w, v_w_in, v_conv_gdn, v_a_log, v_dt_bias, v_onorm_gdn, v_w_branch_gdn, v_mu_rwkv, v_w0, v_w2, v_a0, v_a2, v_g2, v_k_k, v_k_a, v_r_k, v_lnx_w, v_lnx_b, v_w_branch_rwkv, v_w_out, v_norm2_w, v_w_ffn_in, v_conv_ffn, v_w_ffn_out, v_norm_f_w):
    args = dict(locals())
    W = {n: args[n] for n in _ORDER}
    Mo = {n: args["m_" + n] for n in _ORDER}
    Vo = {n: args["v_" + n] for n in _ORDER}
    shapes = {n: W[n].shape for n in _ORDER}
    sq = lambda a: a.reshape(a.shape[-2:]) if a.ndim == 3 else a.reshape(1, -1)
    row = lambda a: a.reshape(1, -1)

    xi, yi, ci = lax.axis_index("x"), lax.axis_index("y"), lax.axis_index("c")
    dev = 4 * xi + 2 * yi + ci
    chip = 2 * xi + yi

    x2 = x[0]
    tgt = loss_target[0]
    T, D = x2.shape
    N = T // CHUNK
    tt_l = _pick(T, 256, CHUNK)
    tt_h = _pick(T, 128, CHUNK)

    mid_shapes = [shapes[n][1:] for n in _MID]
    small_blk = _pack([c] + [W[n] for n in _MID], 8, f32)
    small_all = _ag8("gather_c_mid", small_blk)
    c_all = small_all[:, 0, :]
    per_chip = small_all[0::2].reshape(4, -1)[:, D:]
    mid = [dict(zip(_MID, _unpack(per_chip[j], mid_shapes))) for j in range(4)]
    catm = lambda n: jnp.concatenate([mid[j][n] for j in range(4)], axis=1)
    conv_gdn_f, conv_ffn_f = catm("conv_gdn"), catm("conv_ffn")
    w2f, a2f, g2f = catm("w2"), catm("a2"), catm("g2")

    (win_s,) = _run_job("w_in_all_gather", _ag4_job([sq(W["w_in"]).astype(bf16)]))
    later_weights = _ag4_job([sq(W[n]).astype(bf16) for n in _BIG[1:]])
    win_p = _win_pad(win_s)
    zpad = lambda a, top, bot: jnp.pad(a, ((top, bot), (0, 0)))
    w2p, a2p, g2p = zpad(w2f, 0, 64), zpad(a2f, 64, 0), zpad(g2f, 0, 96)

    ncol = shapes["w_ada"][2]
    b_cols = lax.dynamic_slice(sq(W["b_ada"]), (0, chip * ncol), (1, ncol))
    cond16, mod_cols = _ada_fwd(jnp.pad(c_all, ((0, 8), (0, 0))), sq(W["w_ada"]), b_cols)
    mod_all = _ag8("gather_mod", mod_cols[:8])
    mod_mine = lax.dynamic_slice(mod_all[0::2], (0, dev, 0), (4, 1, ncol)).reshape(1, 4 * ncol)
    shift1, scale1, gate1, shift2, scale2, gate2 = [mod_mine[:, i * D:(i + 1) * D] for i in range(6)]

    seg = _seg_matrix(WIDTH, HEAD_DIM)
    norm1 = [sq(W["norm1_w"]), shift1, scale1]
    h1 = _stage_fwd("norm_mod1", _fn_norm_mod, [_whole(x2)], norm1, [(D, bf16)], tt_l)[0]
    p = _matmul("in_proj", h1, win_p, "nn")

    cgq = [row(conv_gdn_f[j, part * WIDTH:(part + 1) * WIDTH]) for part in range(3) for j in range(4)]
    lane_pad = lambda a: jnp.pad(row(a), ((0, 0), (8, LANES - 16)))
    gdn_pre_ps = cgq + [lane_pad(W["a_log"]), lane_pad(W["dt_bias"]), seg, _chunk_tri(tt_h, CHUNK)]
    gdn_pre_ts = [(p, WIDTH, ci_, (0, 1, 2, 3)) for ci_ in _COL_QKV] + [(p, LANES, _COL_BA, None)]
    q_, k_, v_, beta_t, gc_t = _stage_fwd("gdn_pre", _fn_gdn_pre, gdn_pre_ts, gdn_pre_ps,
                                          [(WIDTH, f32)] * 3 + [(LANES, f32)] * 2, tt_h)
    o_, gdn_hist, gdn_inv, wbg_s, wbr_s, wout_s, wfi_s, wfo_s = _scan_fwd(
        "gdn_scan", _gdn_chunk, [q_, k_, v_], [beta_t, gc_t], job=later_weights)
    wout_f = wout_s.reshape(D, D)
    wfo = wfo_s.reshape(D_FF, D)
    ow512 = jnp.tile(row(W["onorm_gdn"]), (1, HEADS))
    gdn_post_ts = [_whole(o_), (p, WIDTH, _COL_Z, None)]
    ya = _stage_fwd("gdn_post", _fn_gdn_post, gdn_post_ts, [ow512, seg], [(WIDTH, bf16)], tt_l)[0]

    mu = sq(W["mu_rwkv"])
    rw_ps = [mu[:, 0:512], mu[:, 512:1024], mu[:, 1024:1536], mu[:, 1536:1664], jnp.pad(mu[:, 1664:1824], ((0, 0), (0, 96))),
             sq(W["w0"]), w2p, sq(W["a0"]), a2p, g2p, sq(W["k_k"]), sq(W["k_a"]), seg]
    rw_ts = [(p, WIDTH, ci_, (0, 1)) for ci_ in _COL_RKV] + [(p, LANES, _COL_LORA, (0, 1)),
                                                              (p, 256, _COL_GATE_LORA, (0, 1))]
    rw_out = _stage_fwd("rwkv_pre", _fn_rwkv_pre, rw_ts, rw_ps, [(WIDTH, f32)] * 7, tt_h)
    r_, lw_, k2_, vv_, na_, b_, g_ = rw_out
    rw_ins = [r_, lw_, k2_, vv_, na_, b_]
    y_, rw_hist, rw_inv = _scan_fwd("rwkv_scan", _rwkv_chunk, rw_ins)
    rwp_ps = [sq(W["lnx_w"]), sq(W["lnx_b"]), row(W["r_k"]), seg]
    rwp_ts = [_whole(y_), _whole(r_), _whole(k2_), _whole(vv_), _whole(g_)]
    yb = _stage_fwd("rwkv_post", _fn_rwkv_post, rwp_ts, rwp_ps, [(WIDTH, bf16)], tt_l)[0]

    big_a = _matmul("branch_gdn", ya, wbg_s, "nn", shards=4)
    big_b = _matmul("branch_rwkv", yb, wbr_s, "nn", shards=4)
    merge_ts = [(p, D, _COL_GL[0], None), (p, D, _COL_GL[1], None), _whole(big_a), _whole(big_b)]
    merged = _stage_fwd("merge", _fn_merge, merge_ts, [], [(D, bf16)], tt_l)[0]
    mo = _matmul("out_proj", merged, wout_f, "nn")
    norm2 = [gate1, sq(W["norm2_w"]), shift2, scale2]
    x1, h2 = _stage_fwd("resid_norm_mod2", _fn_resid_norm_mod, [_whole(x2), _whole(mo)], norm2, [(D, f32), (D, bf16)], tt_l)
    f = _matmul("ffn_in", h2, wfi_s, "nn", shards=4, tn=1408)
    cg_ps = [row(conv_ffn_f[j]) for j in range(3)]
    cg_ts = [(f, D_FF, 0, (0, 1, 2)), (f, D_FF, 1, None)]
    act = _stage_fwd("convglu", _fn_convglu, cg_ts, cg_ps, [(D_FF, bf16)], tt_h)[0]
    fo = _matmul("ffn_out", act, wfo, "nn", tk=1408)

    dx1_a, dfo, dgate2, dnormf, loss_part = _final_stage(x1, fo, tgt, gate2, row(W["norm_f_w"]), tt_l)

    dact = _matmul("d_act", dfo, wfo, "nt", tn=1408)
    g_wfo = _matmul("g_ffn_out", act, dfo, "tn", tm=1408)
    _, dcf, df = _stage_bwd("convglu_bwd", _fn_convglu, cg_ts, cg_ps, [[_whole(dact)]], tt_h, [True] * 2, [True] * 3,
                            joint=([0, 1], 2 * D_FF, 0, None), dtypes=[bf16])
    dh2 = _matmul("d_h2", df, wfi_s, "nt", shards=4, tk=1408)
    g_wfi = _matmul("g_ffn_in", h2, df, "tn", shards=4, tn=1408)
    (dx_a, dmo), (dgate1, dnorm2, dshift2, dscale2), _ = _stage_bwd(
        "resid_norm_mod2_bwd", _fn_resid_norm_mod, [_whole(x2), _whole(mo)], norm2,
        [[_whole(dx1_a)], [_whole(dh2)]], tt_l, [True, True], [True] * 4, dtypes=[f32, bf16])
    dmerged = _matmul("d_merged", dmo, wout_f, "nt")
    g_wout = _matmul("g_out_proj", merged, dmo, "tn")
    (dbig_a, dbig_b), _, dp = _stage_bwd("merge_bwd", _fn_merge, merge_ts, [], [[_whole(dmerged)]], tt_l, [True] * 4, [],
                                         joint=([0, 1], p.shape[1], _JOINT_GL, None), dtypes=[bf16] * 3)
    dya = _matmul("d_ya", dbig_a, wbg_s, "nt", shards=4)
    g_wbg = _matmul("g_branch_gdn", ya, dbig_a, "tn", shards=4)
    dyb = _matmul("d_yb", dbig_b, wbr_s, "nt", shards=4)
    g_wbr = _matmul("g_branch_rwkv", yb, dbig_b, "tn", shards=4)

    (dy_, dr_p, dk2_p, dv_p, dg_p), (dlnxw, dlnxb, drk), _ = _stage_bwd(
        "rwkv_post_bwd", _fn_rwkv_post, rwp_ts, rwp_ps, [[_whole(dyb)]], tt_l, [True] * 5, [True, True, True, False])
    gs_a = [g_wbg, g_wbr, g_wout.reshape(4, D // 4, D), g_wfi, g_wfo.reshape(4, D_FF // 4, D)]
    pairs_a = [_add_half("grads_pair_sum%d" % (t + 1), g, r_, ci) for t, (g, r_) in enumerate(zip(gs_a, _rs_sibling_swap(gs_a, "a")))]
    dr_c, dlw_c, dk2_c, dv_c, dna_c, db_c, *others_a = _scan_bwd(
        "rwkv_scan_bwd", _rwkv_chunk, rw_ins, [], rw_hist, rw_inv, dy_, job=_chip_exchange_job([pb for _, pb in pairs_a]))
    rw_cots = [[_whole(dr_p), _whole(dr_c)], [_whole(dlw_c)], [_whole(dk2_p), _whole(dk2_c)],
               [_whole(dv_p), _whole(dv_c)], [_whole(dna_c)], [_whole(db_c)], [_whole(dg_p)]]
    (dl_, dg_), rw_dp, dp = _stage_bwd("rwkv_pre_bwd", _fn_rwkv_pre, rw_ts, rw_ps, rw_cots, tt_h, [True] * 5,
                                       [True] * 12 + [False], joint=([0, 1, 2], p.shape[1], _JOINT_RKV, dp), dtypes=[bf16] * 3)
    dmu_r, dmu_k, dmu_v, dmu_l, dmu_g, dw0, dw2p, da0, da2p, dg2p, dkk, dka = rw_dp

    (do_,), (dow512,), dp = _stage_bwd("gdn_post_bwd", _fn_gdn_post, gdn_post_ts, [ow512, seg], [[_whole(dya)]], tt_l,
                                       [True, True], [True, False], joint=([1], p.shape[1], _JOINT_Z, dp), dtypes=[f32, bf16])
    d_gdn = _scan_bwd("gdn_scan_bwd", _gdn_chunk, [q_, k_, v_], [beta_t, gc_t], gdn_hist, gdn_inv, do_)
    gdn_cots = [[_whole(a)] for a in d_gdn]
    (dba,), gdn_dp, dp = _stage_bwd("gdn_pre_bwd", _fn_gdn_pre, gdn_pre_ts, gdn_pre_ps, gdn_cots, tt_h, [True] * 4,
                                    [True] * 14 + [False, False], joint=([0, 1, 2], p.shape[1], _JOINT_QKV, dp), dtypes=[bf16] * 2)
    dp = lax.dynamic_update_slice(dp, jnp.concatenate([dl_, dba, dg_], axis=1), (0, _SMALL_BLOCKS_AT))
    g_win_s = _win_unpad_shards(_matmul("g_in_proj", h1, dp, "tn"), shapes["w_in"][2])
    pair_win = _add_half("grads_pair_sum0", g_win_s, _rs_sibling_swap([g_win_s], "b")[0], ci)
    dh1, others_win = _matmul("d_h1", dp, win_p, "nt", job=_chip_exchange_job([pair_win[1]]))
    (grad_x,), (dnorm1, dshift1, dscale1), _ = _stage_bwd("norm_mod1_bwd", _fn_norm_mod_and_x, [_whole(x2)], norm1,
                                                          [[_whole(dh1)], [_whole(dx_a)]], tt_l, [True], [True] * 3)

    dmod = jnp.concatenate([dshift1, dscale1, dgate1, dshift2, dscale2, dgate2], axis=1)
    g_conv_gdn = jnp.concatenate([jnp.concatenate([gdn_dp[4 * part + j] for part in range(3)], axis=1) for j in range(4)], axis=0)
    g_conv_ffn = jnp.concatenate(dcf, axis=0)
    g_mu = jnp.concatenate([dmu_r, dmu_k, dmu_v, dmu_l, dmu_g[:, :160]], axis=1)
    small_parts = {"b_ada": dmod, "norm1_w": dnorm1, "a_log": gdn_dp[12][:, 8:16], "dt_bias": gdn_dp[13][:, 8:16],
                   "mu_rwkv": g_mu, "w0": dw0, "a0": da0, "k_k": dkk, "k_a": dka, "r_k": drk, "lnx_w": dlnxw,
                   "lnx_b": dlnxb, "norm2_w": dnorm2, "norm_f_w": dnormf}
    small_names = [n for n in _SMALL if n != "onorm_gdn"]
    mid_full = [g_conv_gdn, g_conv_ffn, dw2p[0:64], da2p[64:128], dg2p[0:160]]
    body_rows = _pack([small_parts[n] for n in small_names] + [loss_part[:, 0:1]] + mid_full, 1, f32)
    head_row = body_rows.shape[0]
    small_g = jnp.concatenate([body_rows, jnp.pad(dow512, ((0, 0), (0, PACK_W - WIDTH)))], axis=0)
    small_g = jnp.pad(small_g, ((0, -small_g.shape[0] % 8), (0, 0)))
    small_all_g = _ag8("gather_small_grads", small_g)
    small_sum, head_sum = _sum_devices(small_all_g, head_row)
    small_shapes = [shapes[n][1:] if n != "norm_f_w" else shapes[n] for n in small_names]
    un = _unpack(small_sum, small_shapes + [(1,)] + [g.shape for g in mid_full])
    small_grads = dict(zip(small_names, un))
    loss = un[len(small_names)].reshape(())
    small_grads["onorm_gdn"] = head_sum[0, 0:HEAD_DIM]
    for n, g in zip(_MID, un[len(small_names) + 1:]):
        wcols = shapes[n][2]
        small_grads[n] = lax.dynamic_slice(g, (0, chip * wcols), (g.shape[0], wcols))

    dmod_all = small_all_g[:, 0:6, :].reshape(8, 6 * PACK_W)
    dmod_cols = lax.dynamic_slice(dmod_all, (0, chip * ncol), (8, ncol))
    g_wada = _matmul("g_w_ada", cond16, jnp.pad(dmod_cols, ((0, 8), (0, 0))), "tn")

    pairs = [pair_win] + pairs_a
    others = [others_win] + others_a
    halves = [_sum_chip("grads_chip_sum%d" % t, pf, o_, chip, ci) for t, ((pf, _), o_) in enumerate(zip(pairs, others))]
    big_grads = dict(zip(_BIG, _rs_sibling_join(halves)))

    res = {tag: {} for tag in ("grad", "delta", "new_m", "new_v")}

    def put(n, g, d, m_, v_):
        for tag, val in zip(("grad", "delta", "new_m", "new_v"), (g, d, m_, v_)):
            res[tag][n] = val.reshape(shapes[n])

    for n in _BIG:
        put(n, big_grads[n], *_adamw("adamw_" + n, sq(W[n]), big_grads[n], sq(Mo[n]), sq(Vo[n])))
    put("w_ada", g_wada, *_adamw("adamw_w_ada", sq(W["w_ada"]), g_wada, sq(Mo["w_ada"]), sq(Vo["w_ada"])))
    rest = _SMALL + _MID
    pk = lambda d: _pack([d[n] for n in rest], 8, f32)
    sg = pk(small_grads)
    sm = _adamw("adamw_small", pk(W), sg, pk(Mo), pk(Vo))
    for tag, buf in zip(("grad", "delta", "new_m", "new_v"), (sg,) + tuple(sm)):
        res[tag].update(zip(rest, _unpack(buf, [shapes[n] for n in rest])))
    outs = [loss, grad_x.reshape(x.shape)]
    for tag in ("grad", "delta", "new_m", "new_v"):
        outs += [res[tag][n] for n in _ORDER]
    return tuple(outs)
```

```python
import numpy as np
import jax
import jax.numpy as jnp
from jax import lax
from jax.experimental import pallas as pl
from jax.experimental.pallas import tpu as pltpu

f32 = jnp.float32
bf16 = jnp.bfloat16

LANES = 128
HEADS = 8
HEAD_DIM = 64
WIDTH = HEADS * HEAD_DIM
CHUNK = 64
D_FF = 2816
NORM_EPS = 1e-6
LNX_EPS = 64e-5
PACK_W = 1024
MESH_ID = pl.DeviceIdType.MESH

ADAM_LR, ADAM_B1, ADAM_B2, ADAM_EPS, ADAM_WD, ADAM_STEP = 0.001, 0.9, 0.999, 1e-08, 0.01, 10


def _pick(n, target, mult):
    if n <= target:
        return n
    best = None
    for t in range(mult, target + 1, mult):
        if n % t == 0:
            best = t
    assert best is not None, (n, target, mult)
    return best


def _split_bf16(x, n):
    parts, r = [], x
    for i in range(n):
        p = r.astype(bf16)
        parts.append(p)
        if i + 1 < n:
            r = r - p.astype(f32)
    return parts


def _xdot_r_impl(x, m, n, dims):
    acc = None
    for p in _split_bf16(x, n):
        t = lax.dot_general(p, m, dims, preferred_element_type=f32)
        acc = t if acc is None else acc + t
    return acc


def _make_xdot_r(n):
    nn = (((1,), (0,)), ((), ()))
    nt = (((1,), (1,)), ((), ()))

    @jax.custom_vjp
    def xdot(x, m):
        return _xdot_r_impl(x, m, n, nn)

    def fwd(x, m):
        return _xdot_r_impl(x, m, n, nn), m

    def bwd(m, ct):
        return _xdot_r_impl(ct, m, n, nt), jnp.zeros_like(m)

    xdot.defvjp(fwd, bwd)
    return xdot


_segsum = _make_xdot_r(2)


def _xdot_l_impl(m, x, n, dims):
    acc = None
    for p in _split_bf16(x, n):
        t = lax.dot_general(m, p, dims, preferred_element_type=f32)
        acc = t if acc is None else acc + t
    return acc


@jax.custom_vjp
def _xdot_l(m, x):
    return _xdot_l_impl(m, x, 3, (((1,), (0,)), ((), ())))


def _xdot_l_fwd(m, x):
    return _xdot_l(m, x), m


def _xdot_l_bwd(m, ct):
    return jnp.zeros_like(m), _xdot_l_impl(m, ct, 3, (((0,), (0,)), ((), ())))


_xdot_l.defvjp(_xdot_l_fwd, _xdot_l_bwd)


@jax.custom_vjp
def _bdot(x, w):
    return jnp.dot(x.astype(bf16), w.astype(bf16), preferred_element_type=f32)


def _bdot_fwd(x, w):
    return _bdot(x, w), (x, w)


def _bdot_bwd(res, ct):
    x, w = res
    c = ct.astype(bf16)
    dx = lax.dot_general(c, w.astype(bf16), (((1,), (1,)), ((), ())), preferred_element_type=f32)
    dw = lax.dot_general(x.astype(bf16), c, (((0,), (0,)), ((), ())), preferred_element_type=f32)
    return dx, dw


_bdot.defvjp(_bdot_fwd, _bdot_bwd)


def _silu(x):
    return x * jax.nn.sigmoid(x)


def _softplus(x):
    return jnp.maximum(x, 0.0) + jnp.log(1.0 + jnp.exp(-jnp.abs(x)))


def _rms(x, w, eps):
    return x * lax.rsqrt(jnp.mean(x * x, axis=-1, keepdims=True) + eps) * w


def _seg_matrix(width, seg):
    i = np.arange(width)
    return jnp.asarray((i[:, None] // seg) == (i[None, :] // seg), dtype=bf16)


def _chunk_tri(rows, chunk):
    i = np.arange(rows)
    return jnp.asarray(((i[:, None] // chunk) == (i[None, :] // chunk)) & (i[:, None] >= i[None, :]), dtype=bf16)


HALO = 8


def _full_spec(shape):
    nd = len(shape)
    return pl.BlockSpec(shape, lambda i: (0,) * nd)


def _entry_specs(entries, tt, block_of):
    specs, ops = [], []
    for arr, w, ci, shifts in entries:
        specs.append(pl.BlockSpec((tt, w), lambda i, ci=ci: (block_of(i), ci)))
        ops.append(arr)
        if shifts:
            specs.append(pl.BlockSpec((HALO, w), lambda i, ci=ci: (jnp.maximum(block_of(i) * (tt // HALO) - 1, 0), ci)))
            ops.append(arr)
    return specs, ops


def _load_entries(entries, refs, first):
    tiles, k = [], 0
    for _, w, _, shifts in entries:
        x = refs[k][...].astype(f32)
        k += 1
        if not shifts:
            tiles.append(x)
            continue
        halo = jnp.where(first, 0.0, refs[k][...].astype(f32))
        k += 1
        row = lax.broadcasted_iota(jnp.int32, (HALO, w), 0)
        for s in shifts:
            if s == 0:
                tiles.append(x)
                continue
            r = pltpu.roll(x, s, 0)
            head = jnp.where(row < s, pltpu.roll(halo, s, 0), r[0:HALO])
            tiles.append(jnp.concatenate([head, r[HALO:]], axis=0))
    return tiles


def _unshift_sum(grads, shifts, carry, tt):
    w = grads[0].shape[1]
    row = lax.broadcasted_iota(jnp.int32, (tt, w), 0)
    row8 = lax.broadcasted_iota(jnp.int32, (HALO, w), 0)
    dx, out = None, jnp.zeros((HALO, w), f32)
    for d, s in zip(grads, shifts):
        if s == 0:
            part = d
        else:
            part = jnp.where(row < tt - s, pltpu.roll(d, tt - s, 0), 0.0)
            out = out + jnp.where(row8 >= HALO - s, pltpu.roll(d[0:HALO], HALO - s, 0), 0.0)
        dx = part if dx is None else dx + part
    return jnp.concatenate([dx[:tt - HALO], dx[tt - HALO:] + carry], axis=0), out


def _stage_fwd(name, fn, tiles, params, outs, tt):
    rows = tiles[0][0].shape[0]
    npar = len(params)
    specs, ops = _entry_specs(tiles, tt, lambda i: i)
    nin = len(ops)

    def body(*refs):
        ts = _load_entries(tiles, refs[:nin], pl.program_id(0) == 0)
        ps = [r[...] for r in refs[nin:nin + npar]]
        res = fn(ps, ts)
        for r, v in zip(refs[nin + npar:], res):
            r[...] = v.astype(r.dtype)

    return pl.pallas_call(
        body, grid=(rows // tt,),
        in_specs=specs + [_full_spec(p.shape) for p in params],
        out_specs=[pl.BlockSpec((tt, w), lambda i: (i, 0)) for (w, _) in outs],
        out_shape=[jax.ShapeDtypeStruct((rows, w), dt) for (w, dt) in outs],
        compiler_params=pltpu.CompilerParams(dimension_semantics=("parallel",)),
        name=name,
    )(*ops, *params)


def _stage_bwd(name, fn, tiles, params, cots, tt, tile_grad, param_grad, joint=None, dtypes=None):
    rows = tiles[0][0].shape[0]
    nblk = rows // tt
    npar = len(params)
    block_of = lambda i: nblk - 1 - i
    specs, ops = _entry_specs(tiles, tt, block_of)
    nin = len(ops)
    flat_cots = [c for group in cots for c in group]
    groups = [len(g) for g in cots]
    ncot = len(flat_cots)
    counts = [len(e[3]) if e[3] else 1 for e in tiles]
    dt_entries = [e for e, g in zip(tiles, tile_grad) if g]
    dp_shapes = [p.shape for p, g in zip(params, param_grad) if g]
    ndt = len(dt_entries)
    carry_w = [e[1] for e in dt_entries if e[3]]
    flags = [g for g, n in zip(tile_grad, counts) for _ in range(n)]
    members, j_width, j_cidx, j_buf = joint if joint else ([], 0, 0, None)
    solo = [k for k in range(ndt) if k not in members]
    nsolo, njoint, nbuf = len(solo), int(bool(members)), int(j_buf is not None)
    j_block = sum(dt_entries[k][1] for k in members)
    dtypes = list(dtypes) if dtypes else [f32] * (nsolo + njoint)

    def body(*refs):
        i = pl.program_id(0)
        p_refs = refs[nin:nin + npar]
        c_refs = refs[nin + npar:nin + npar + ncot]
        base = nin + npar + ncot + nbuf
        dt_refs = refs[base:base + nsolo]
        joint_refs = refs[base + nsolo:base + nsolo + njoint]
        dp_refs = refs[base + nsolo + njoint:base + nsolo + njoint + len(dp_shapes)]
        carry_refs = refs[base + nsolo + njoint + len(dp_shapes):]
        ts = _load_entries(tiles, refs[:nin], block_of(i) == 0)
        ps = [r[...] for r in p_refs]

        def f(dp, dt):
            dp, dt = iter(dp), iter(dt)
            pp = [next(dp) if g else p for p, g in zip(ps, param_grad)]
            tl = [next(dt) if g else t for t, g in zip(ts, flags)]
            return fn(pp, tl)

        _, vjp = jax.vjp(f, [p for p, g in zip(ps, param_grad) if g], [t for t, g in zip(ts, flags) if g])
        cs, j = [], 0
        for n in groups:
            acc = c_refs[j][...].astype(f32)
            for q in range(1, n):
                acc = acc + c_refs[j + q][...].astype(f32)
            cs.append(acc)
            j += n
        gp, gt = vjp(cs)

        @pl.when(i == 0)
        def _():
            for r in dp_refs:
                r[...] = jnp.zeros_like(r)
            for r in carry_refs:
                r[...] = jnp.zeros_like(r)

        gt, k, kc, dxs = list(gt), 0, 0, []
        for e, n in zip(dt_entries, [n for n, g in zip(counts, tile_grad) if g]):
            if e[3]:
                dx, out = _unshift_sum(gt[k:k + n], e[3], carry_refs[kc][...], tt)
                carry_refs[kc][...] = out
                kc += 1
            else:
                dx = gt[k]
            dxs.append(dx)
            k += n
        for r, k in zip(dt_refs, solo):
            r[...] = dxs[k].astype(r.dtype)
        off = 0
        for k in members:
            w = dt_entries[k][1]
            joint_refs[0][:, off:off + w] = dxs[k].astype(joint_refs[0].dtype)
            off += w
        for r, v in zip(dp_refs, gp):
            r[...] += v

    res = pl.pallas_call(
        body, grid=(nblk,),
        in_specs=specs + [_full_spec(p.shape) for p in params]
        + [pl.BlockSpec((tt, w), lambda i, ci=ci: (block_of(i), ci)) for (_, w, ci, *_) in flat_cots]
        + [pl.BlockSpec(memory_space=pl.ANY)] * nbuf,
        out_specs=[pl.BlockSpec((tt, dt_entries[k][1]), lambda i: (block_of(i), 0)) for k in solo]
        + [pl.BlockSpec((tt, j_block), lambda i: (block_of(i), j_cidx))] * njoint
        + [_full_spec(s) for s in dp_shapes],
        out_shape=[jax.ShapeDtypeStruct((rows, dt_entries[k][1]), dt) for k, dt in zip(solo, dtypes)]
        + [jax.ShapeDtypeStruct((rows, j_width), dtypes[-1])] * njoint
        + [jax.ShapeDtypeStruct(s, f32) for s in dp_shapes],
        scratch_shapes=[pltpu.VMEM((HALO, w), f32) for w in carry_w],
        input_output_aliases={nin + npar + ncot: nsolo} if nbuf else {},
        compiler_params=pltpu.CompilerParams(dimension_semantics=("arbitrary",)),
        name=name,
    )(*ops, *params, *[c[0] for c in flat_cots], *([j_buf] if nbuf else []))
    res = list(res)
    return res[:nsolo], res[nsolo + njoint:], (res[nsolo] if njoint else None)


def _whole(a):
    return (a, a.shape[1], 0, None)


def _matmul(name, a, b, mode, out_dtype=f32, tm=1024, tn=1024, tk=1024, shards=1, job=None):
    S = shards
    if mode == "nn":
        M, K = a.shape
        w = b.shape[-1]
    elif mode == "nt":
        M = a.shape[0]
        if S > 1:
            _, N, w = b.shape
            K = S * w
        else:
            N, K = b.shape
            w = K
    else:
        K, M = a.shape
        w = b.shape[1] // S
    if mode != "nt":
        N = S * w
    tm = _pick(M, tm, LANES)
    if mode == "nt":
        tn = _pick(N, tn, LANES)
        tk = _pick(w, tk, LANES)
    else:
        tn = _pick(w, tn, LANES)
        tk = _pick(K, tk, LANES if mode == "nn" else 16)
    nk = K // tk
    nb = w // (tk if mode == "nt" else tn)
    if mode == "nn":
        a_spec = pl.BlockSpec((tm, tk), lambda i, j, k: (i, k))
        if S > 1:
            b_spec = pl.BlockSpec((1, tk, tn), lambda i, j, k: (j // nb, k, j % nb))
        else:
            b_spec = pl.BlockSpec((tk, tn), lambda i, j, k: (k, j))
        dims = (((1,), (0,)), ((), ()))
    elif mode == "nt":
        a_spec = pl.BlockSpec((tm, tk), lambda i, j, k: (i, k))
        if S > 1:
            b_spec = pl.BlockSpec((1, tn, tk), lambda i, j, k: (k // nb, j, k % nb))
        else:
            b_spec = pl.BlockSpec((tn, tk), lambda i, j, k: (j, k))
        dims = (((1,), (1,)), ((), ()))
    else:
        a_spec = pl.BlockSpec((tk, tm), lambda i, j, k: (k, i))
        b_spec = pl.BlockSpec((tk, tn), lambda i, j, k: (k, j))
        dims = (((0,), (0,)), ((), ()))
    if mode == "tn" and S > 1:
        o_spec = pl.BlockSpec((1, tm, tn), lambda i, j, k: (j // nb, i, j % nb))
        o_shape = (S, M, w)
    else:
        o_spec = pl.BlockSpec((tm, tn), lambda i, j, k: (i, j))
        o_shape = (M, N)
    b_lead = S > 1 and mode != "tn"
    o_lead = S > 1 and mode == "tn"

    j_ins, j_outs, j_sems = _job_parts(job)
    nji, njo = len(j_ins), len(j_outs)
    grid = (M // tm, N // tn, nk)

    def run_job(refs):
        step = (pl.program_id(0) * grid[1] + pl.program_id(1)) * grid[2] + pl.program_id(2)
        _job_steps(job, refs[2:2 + nji], refs[3 + nji:3 + nji + njo], refs[len(refs) - 2:], step,
                   grid[0] * grid[1] * grid[2] - 1)

    def body(*refs):
        a_ref, b_ref, o_ref, acc_ref = refs[0], refs[1], refs[2 + nji], refs[3 + nji + njo]
        run_job(refs)
        k = pl.program_id(2)

        @pl.when(k == 0)
        def _():
            acc_ref[...] = jnp.zeros_like(acc_ref)

        bv = b_ref[0] if b_lead else b_ref[...]
        acc_ref[...] += lax.dot_general(a_ref[...].astype(bf16), bv.astype(bf16), dims, preferred_element_type=f32)

        @pl.when(k == nk - 1)
        def _():
            if o_lead:
                o_ref[0] = acc_ref[...].astype(o_ref.dtype)
            else:
                o_ref[...] = acc_ref[...].astype(o_ref.dtype)

    def body_one_step(*refs):
        a_ref, b_ref, o_ref = refs[0], refs[1], refs[2 + nji]
        run_job(refs)
        bv = b_ref[0] if b_lead else b_ref[...]
        res = lax.dot_general(a_ref[...].astype(bf16), bv.astype(bf16), dims, preferred_element_type=f32)
        if o_lead:
            o_ref[0] = res.astype(o_ref.dtype)
        else:
            o_ref[...] = res.astype(o_ref.dtype)

    res = pl.pallas_call(
        body if nk > 1 else body_one_step, grid=grid,
        in_specs=[a_spec, b_spec] + [_ANY] * nji,
        out_specs=[o_spec] + [_ANY] * njo,
        out_shape=[jax.ShapeDtypeStruct(o_shape, out_dtype)] + j_outs,
        scratch_shapes=([pltpu.VMEM((tm, tn), f32)] if nk > 1 else []) + j_sems,
        compiler_params=pltpu.CompilerParams(
            dimension_semantics=("arbitrary",) * 3 if job else ("parallel", "parallel", "arbitrary")),
        name=name,
    )(a, b, *j_ins)
    return res if job else res[0]


def _make_bmm(precision):
    if precision is None:
        cast, kw = (lambda v: v.astype(bf16)), {}
    else:
        cast, kw = (lambda v: v), {"precision": precision}

    def nn(a, b):
        return jnp.einsum("hij,hjk->hik", cast(a), cast(b), preferred_element_type=f32, **kw)

    def nt(a, b):
        return jnp.einsum("hik,hjk->hij", cast(a), cast(b), preferred_element_type=f32, **kw)

    def tn(a, b):
        return jnp.einsum("hki,hkj->hij", cast(a), cast(b), preferred_element_type=f32, **kw)

    if precision is not None:
        return nn, nt, tn
    nn_v, nt_v, tn_v = jax.custom_vjp(nn), jax.custom_vjp(nt), jax.custom_vjp(tn)
    keep = lambda f: (lambda a, b: (f(a, b), (a, b)))
    nn_v.defvjp(keep(nn), lambda r, ct: (nt(ct, r[1]), tn(r[0], ct)))
    nt_v.defvjp(keep(nt), lambda r, ct: (nn(ct, r[1]), tn(ct, r[0])))
    tn_v.defvjp(keep(tn), lambda r, ct: (nt(r[1], ct), nn(r[0], ct)))
    return nn_v, nt_v, tn_v


_bmm, _bmm_nt, _bmm_tn = _make_bmm(None)
_bmm_exact = _make_bmm(lax.Precision.HIGH)[0]


def _masks(n):
    r = lax.broadcasted_iota(jnp.int32, (n, n), 0)
    c = lax.broadcasted_iota(jnp.int32, (n, n), 1)
    return (r >= c)[None], (r > c)[None], (r == c)[None]


_INV_BLOCK = 8


def _nilpotent_inverse(m, eye):
    p = eye + m
    for _ in range(2):
        m = _bmm(m, m)
        p = p + _bmm(p, m)
    return p


def _neumann_inverse_impl(m):
    n = m.shape[1]
    assert n == _INV_BLOCK * _INV_BLOCK
    r = lax.broadcasted_iota(jnp.int32, (n, n), 0)
    c = lax.broadcasted_iota(jnp.int32, (n, n), 1)
    eye = (r == c).astype(f32)[None]
    inside = jnp.where((r // _INV_BLOCK == c // _INV_BLOCK)[None], m, 0.0)
    d_inv = _nilpotent_inverse(inside, eye)
    return _bmm(_nilpotent_inverse(_bmm(d_inv, m - inside), eye), d_inv)


@jax.custom_vjp
def _neumann_inverse(m):
    return _neumann_inverse_impl(m)


def _neumann_inverse_fwd(m):
    p = _neumann_inverse_impl(m)
    return p, p


def _neumann_inverse_bwd(p, ct):
    return (_bmm_tn(p, _bmm_nt(ct, p)),)


_neumann_inverse.defvjp(_neumann_inverse_fwd, _neumann_inverse_bwd)


@jax.custom_vjp
def _given_inverse(m, p):
    return p


_given_inverse.defvjp(lambda m, p: (p, p), lambda p, ct: (_neumann_inverse_bwd(p, ct)[0], jnp.zeros_like(p)))


def _gdn_chunk(s, q, k, v, beta, gc, gr, gl, p=None):
    n = q.shape[1]
    causal, strict, _ = _masks(n)
    decay = jnp.where(causal, jnp.exp(jnp.where(causal, gc - gr, 0.0)), 0.0)
    kb = k * beta
    vb = v * beta
    lower = jnp.where(strict, _bmm_nt(kb, k) * decay, 0.0)
    t_mat = _neumann_inverse(-lower) if p is None else _given_inverse(-lower, p)
    egc = jnp.exp(gc)
    u = _bmm(t_mat, vb)
    w = _bmm(t_mat, kb * egc)
    attn = jnp.where(causal, _bmm_nt(q, k) * decay, 0.0)
    v_new = u - _bmm(w, s)
    o = _bmm(q * egc, s) + _bmm(attn, v_new)
    k_dec = k * jnp.exp(gl - gc)
    s_new = s * jnp.exp(gl) + _bmm_tn(k_dec, v_new)
    return s_new, o, t_mat


def _rwkv_chunk(s, r, lw, k, v, a, b, p=None):
    n = r.shape[1]
    causal, strict, _ = _masks(n)
    tri = jnp.broadcast_to(causal.astype(f32), (r.shape[0], n, n))
    lc = _bmm_exact(tri, lw)
    ein = jnp.exp(lc)
    eout = jnp.exp(-lc)
    a_t = a * jnp.exp(lc - lw)
    b_t = b * eout
    k_t = k * eout
    r_t = r * ein
    a_ab = jnp.where(strict, _bmm_nt(a_t, b_t), 0.0)
    a_ak = jnp.where(strict, _bmm_nt(a_t, k_t), 0.0)
    inv = _neumann_inverse(a_ab) if p is None else _given_inverse(a_ab, p)
    u = _bmm(inv, _bmm_nt(a_t, s) + _bmm(a_ak, v))
    y = (_bmm_nt(r_t, s) + _bmm(jnp.where(causal, _bmm_nt(r_t, b_t), 0.0), u)
         + _bmm(jnp.where(causal, _bmm_nt(r_t, k_t), 0.0), v))
    e_last = jnp.exp(jnp.sum(lw, axis=1, keepdims=True))
    s_new = s * e_last + _bmm_tn(u, b_t * e_last) + _bmm_tn(v, k_t * e_last)
    return s_new, y, inv


def _heads_in(ref, rows):
    return jnp.stack([ref[rows, h * HEAD_DIM:(h + 1) * HEAD_DIM] for h in range(HEADS)], axis=0)


def _heads_out(ref, rows, val):
    for h in range(HEADS):
        ref[rows, h * HEAD_DIM:(h + 1) * HEAD_DIM] = val[h]


def _gdn_scalars(bt, gt):
    n = bt.shape[0]
    gtt = gt.T
    hs = range(HEADS)
    return [jnp.stack([bt[:, h:h + 1] for h in hs], axis=0),
            jnp.stack([gt[:, HEADS + h:HEADS + h + 1] for h in hs], axis=0),
            jnp.stack([gtt[HEADS + h:HEADS + h + 1, :] for h in hs], axis=0),
            jnp.stack([gt[n - 1:n, HEADS + h:HEADS + h + 1] for h in hs], axis=0)]


def _gdn_scalars_back(dbeta, dgc, dgr, dgl):
    n = dbeta.shape[1]
    lane = lax.broadcasted_iota(jnp.int32, (n, LANES), 1)
    row = lax.broadcasted_iota(jnp.int32, (n, LANES), 0)
    sub = lax.broadcasted_iota(jnp.int32, (LANES, n), 0)
    db = jnp.zeros((n, LANES), f32)
    dg = jnp.zeros((n, LANES), f32)
    dgt = jnp.zeros((LANES, n), f32)
    for h in range(HEADS):
        db = jnp.where(lane == h, dbeta[h], db)
        dg = jnp.where(lane == HEADS + h, dgc[h] + jnp.where(row == n - 1, dgl[h], 0.0), dg)
        dgt = jnp.where(sub == HEADS + h, dgr[h], dgt)
    return [db, dg + dgt.T]


SCAN_GROUP = 2


def _scan_steps(t):
    g = SCAN_GROUP if (t // CHUNK) % SCAN_GROUP == 0 else 1
    return g, t // (CHUNK * g)


def _scan_spec(width, g, n, reverse):
    if reverse:
        return pl.BlockSpec((g * CHUNK, width), lambda i: (n - 1 - i, 0))
    return pl.BlockSpec((g * CHUNK, width), lambda i: (i, 0))


def _hist_spec(g, n, reverse):
    blk = (g, HEADS, HEAD_DIM, HEAD_DIM)
    if reverse:
        return pl.BlockSpec(blk, lambda i: (n - 1 - i, 0, 0, 0))
    return pl.BlockSpec(blk, lambda i: (i, 0, 0, 0))


def _job_parts(job):
    if job is None:
        return [], [], []
    sems = [pltpu.SemaphoreType.DMA((job["nsem"],)), pltpu.SemaphoreType.DMA((job["nsem"],))]
    return list(job["ins"]), list(job["out_shapes"]), sems


def _job_steps(job, in_refs, out_refs, sems, step, last):
    if job is None:
        return

    @pl.when(step == 0)
    def _():
        job["start"](in_refs, out_refs, *sems)

    @pl.when(step == last)
    def _():
        job["finish"](in_refs, out_refs, *sems)


def _scan_fwd(name, fn, rows_in, scal_in=(), job=None):
    t = rows_in[0].shape[0]
    grp, n = _scan_steps(t)
    nr, ns = len(rows_in), len(scal_in)
    j_ins, j_outs, j_sems = _job_parts(job)
    nji, njo = len(j_ins), len(j_outs)

    def body(*refs):
        o_ref, sh_ref, ph_ref = refs[nr + ns + nji:nr + ns + nji + 3]
        s_scr = refs[nr + ns + nji + 3 + njo]
        _job_steps(job, refs[nr + ns:nr + ns + nji], refs[nr + ns + nji + 3:nr + ns + nji + 3 + njo],
                   refs[nr + ns + nji + 3 + njo + 1:], pl.program_id(0), n - 1)

        @pl.when(pl.program_id(0) == 0)
        def _():
            s_scr[...] = jnp.zeros_like(s_scr)

        s = s_scr[...]
        for sub in range(grp):
            rows = slice(sub * CHUNK, (sub + 1) * CHUNK)
            sh_ref[sub] = s
            ins = [_heads_in(r, rows) for r in refs[:nr]]
            if ns:
                ins += _gdn_scalars(*[r[rows, :] for r in refs[nr:nr + ns]])
            s, o, p = fn(s, *ins)
            _heads_out(o_ref, rows, o)
            ph_ref[sub] = p
        s_scr[...] = s

    return pl.pallas_call(
        body, grid=(n,),
        in_specs=[_scan_spec(a.shape[1], grp, n, False) for a in (*rows_in, *scal_in)] + [_ANY] * nji,
        out_specs=[_scan_spec(WIDTH, grp, n, False), _hist_spec(grp, n, False), _hist_spec(grp, n, False)] + [_ANY] * njo,
        out_shape=[jax.ShapeDtypeStruct((t, WIDTH), f32)]
        + [jax.ShapeDtypeStruct((t // CHUNK, HEADS, HEAD_DIM, HEAD_DIM), f32)] * 2 + j_outs,
        scratch_shapes=[pltpu.VMEM((HEADS, HEAD_DIM, HEAD_DIM), f32)] + j_sems,
        compiler_params=pltpu.CompilerParams(dimension_semantics=("arbitrary",)),
        name=name,
    )(*rows_in, *scal_in, *j_ins)


def _scan_bwd(name, fn, rows_in, scal_in, s_hist, p_hist, d_out, job=None):
    t = rows_in[0].shape[0]
    grp, n = _scan_steps(t)
    nr, ns = len(rows_in), len(scal_in)
    j_ins, j_outs, j_sems = _job_parts(job)
    nji, njo = len(j_ins), len(j_outs)

    def body(*refs):
        sh_ref, ph_ref, do_ref = refs[nr + ns:nr + ns + 3]
        base = nr + ns + 3 + nji
        g_refs = refs[base:base + nr + ns]
        ds_scr = refs[base + nr + ns + njo]
        _job_steps(job, refs[nr + ns + 3:base], refs[base + nr + ns:base + nr + ns + njo],
                   refs[base + nr + ns + njo + 1:], pl.program_id(0), n - 1)

        @pl.when(pl.program_id(0) == 0)
        def _():
            ds_scr[...] = jnp.zeros_like(ds_scr)

        ds = ds_scr[...]
        for sub in reversed(range(grp)):
            rows = slice(sub * CHUNK, (sub + 1) * CHUNK)
            ins = [_heads_in(r, rows) for r in refs[:nr]]
            if ns:
                ins += _gdn_scalars(*[r[rows, :] for r in refs[nr:nr + ns]])
            p = ph_ref[sub]
            _, vjp = jax.vjp(lambda s, *a, p=p: fn(s, *a, p=p)[:2], sh_ref[sub], *ins)
            g = vjp((ds, _heads_in(do_ref, rows)))
            ds = g[0]
            for r, v in zip(g_refs[:nr], g[1:1 + nr]):
                _heads_out(r, rows, v)
            if ns:
                for r, v in zip(g_refs[nr:], _gdn_scalars_back(*g[1 + nr:])):
                    r[rows, :] = v
        ds_scr[...] = ds

    arrs = (*rows_in, *scal_in)
    return pl.pallas_call(
        body, grid=(n,),
        in_specs=[_scan_spec(a.shape[1], grp, n, True) for a in arrs]
        + [_hist_spec(grp, n, True), _hist_spec(grp, n, True), _scan_spec(WIDTH, grp, n, True)] + [_ANY] * nji,
        out_specs=[_scan_spec(a.shape[1], grp, n, True) for a in arrs] + [_ANY] * njo,
        out_shape=[jax.ShapeDtypeStruct(a.shape, f32) for a in arrs] + j_outs,
        scratch_shapes=[pltpu.VMEM((HEADS, HEAD_DIM, HEAD_DIM), f32)] + j_sems,
        compiler_params=pltpu.CompilerParams(dimension_semantics=("arbitrary",)),
        name=name,
    )(*arrs, s_hist, p_hist, d_out, *j_ins)


def _fn_norm_mod(ps, ts):
    nw, shift, scale = ps
    (x,) = ts
    return [_rms(x, nw, NORM_EPS) * (1.0 + scale) + shift]


def _fn_norm_mod_and_x(ps, ts):
    return _fn_norm_mod(ps, ts) + [ts[0]]


def _fn_resid_norm_mod(ps, ts):
    gate, nw, shift, scale = ps
    x, mo = ts
    x1 = x + gate * mo
    return [x1, _rms(x1, nw, NORM_EPS) * (1.0 + scale) + shift]


def _fn_gdn_pre(ps, ts):
    cw = ps[:12]
    alog, dtb, seg, tri = ps[12:]
    ba = ts[12]
    outs = []
    for part in range(3):
        x = ts[4 * part:4 * part + 4]
        w = cw[4 * part:4 * part + 4]
        conv = w[3] * x[0] + w[2] * x[1] + w[1] * x[2] + w[0] * x[3]
        u = _silu(conv)
        if part < 2:
            u = u * lax.rsqrt(_segsum(u * u, seg) + 1e-6)
            if part == 0:
                u = u * (HEAD_DIM ** -0.5)
        outs.append(u)
    beta = jax.nn.sigmoid(ba)
    g = -jnp.exp(alog) * _softplus(ba + dtb)
    gc = _xdot_l(tri, g)
    return outs + [beta, gc]


def _fn_gdn_post(ps, ts):
    ow, seg = ps
    o, z = ts
    ms = _segsum(o * o, seg) * (1.0 / HEAD_DIM)
    return [o * lax.rsqrt(ms + NORM_EPS) * ow * _silu(z)]


def _fn_rwkv_pre(ps, ts):
    mu_r, mu_k, mu_v, mu_l, mu_g, w0, w2p, a0, a2p, g2p, k_k, k_a, seg = ps
    r0, r1, k0, k1, v0, v1, l0, l1, g0, g1 = ts
    xr = r0 + (r1 - r0) * mu_r
    xk = k0 + (k1 - k0) * mu_k
    xv = v0 + (v1 - v0) * mu_v
    xl = l0 + (l1 - l0) * mu_l
    xg = g0 + (g1 - g0) * mu_g
    w = -_softplus(-(w0 + _bdot(jnp.tanh(xl), w2p))) - 0.5
    lw = -jnp.exp(w)
    a = jax.nn.sigmoid(a0 + _bdot(xl, a2p))
    g = _bdot(jax.nn.sigmoid(xg), g2p)
    kk = xk * k_k
    kk = kk * lax.rsqrt(_segsum(kk * kk, seg) + 1e-6)
    k2 = xk * (1.0 + (a - 1.0) * k_a)
    return [xr, lw, k2, xv, -kk, kk * a, g]


def _fn_rwkv_post(ps, ts):
    lw_, lb_, rk, seg = ps
    y, r, k2, v, g = ts
    inv = 1.0 / HEAD_DIM
    yc = y - _segsum(y, seg) * inv
    var = _segsum(yc * yc, seg) * inv
    yn = yc * lax.rsqrt(var + LNX_EPS) * lw_ + lb_
    bonus = _segsum(r * k2 * rk, seg) * v
    return [(yn + bonus) * g]


def _fn_merge(ps, ts):
    gla, glb, ya, yb = ts
    return [jax.nn.sigmoid(gla) * ya + jax.nn.sigmoid(glb) * yb]


def _fn_convglu(ps, ts):
    c0, c1, c2 = ps
    g0, g1, g2, up = ts
    return [_silu(c2 * g0 + c1 * g1 + c0 * g2) * up]


def _final_stage(x1, fo, tgt, gate2, nfw, tt):
    rows, d = x1.shape

    def loss_fn(gate, nw, xa, fa, tg):
        y = _rms(xa + gate * fa, nw, NORM_EPS)
        err = (y - tg) ** 2
        return 0.5 * jnp.sum(jnp.mean(err, axis=-1, keepdims=True), axis=0, keepdims=True)

    def body(x_ref, f_ref, t_ref, g_ref, w_ref, dx_ref, df_ref, dg_ref, dw_ref, l_ref):
        i = pl.program_id(0)
        args = (g_ref[...], w_ref[...], x_ref[...], f_ref[...])
        tg = t_ref[...]
        lv, vjp = jax.vjp(lambda g, w, xa, fa: loss_fn(g, w, xa, fa, tg), *args)
        dg, dw, dx, df = vjp(jnp.ones((1, 1), f32))
        dx_ref[...] = dx
        df_ref[...] = df.astype(df_ref.dtype)

        @pl.when(i == 0)
        def _():
            dg_ref[...] = jnp.zeros_like(dg_ref)
            dw_ref[...] = jnp.zeros_like(dw_ref)
            l_ref[...] = jnp.zeros_like(l_ref)

        dg_ref[...] += dg
        dw_ref[...] += dw
        l_ref[...] += jnp.broadcast_to(lv, l_ref.shape)

    row = pl.BlockSpec((tt, d), lambda i: (i, 0))
    vec = pl.BlockSpec((1, d), lambda i: (0, 0))
    return pl.pallas_call(
        body, grid=(rows // tt,),
        in_specs=[row, row, row, vec, vec],
        out_specs=[row, row, vec, vec, pl.BlockSpec((1, LANES), lambda i: (0, 0))],
        out_shape=[jax.ShapeDtypeStruct((rows, d), f32), jax.ShapeDtypeStruct((rows, d), bf16)]
        + [jax.ShapeDtypeStruct((1, d), f32)] * 2
        + [jax.ShapeDtypeStruct((1, LANES), f32)],
        compiler_params=pltpu.CompilerParams(dimension_semantics=("arbitrary",)),
        name="loss_head",
    )(x1, fo, tgt, gate2, nfw)


def _ada_fwd(c_all, w_shard, b_cols):
    def body(c_ref, w_ref, b_ref, cond_ref, mod_ref):
        cond = _silu(c_ref[...])
        cond_ref[...] = cond
        mod_ref[...] = jnp.dot(cond.astype(bf16), w_ref[...].astype(bf16), preferred_element_type=f32) + b_ref[...]

    n = w_shard.shape[1]
    return pl.pallas_call(
        body, out_shape=[jax.ShapeDtypeStruct(c_all.shape, f32), jax.ShapeDtypeStruct((c_all.shape[0], n), f32)],
        name="ada_fwd",
    )(c_all, w_shard, b_cols)


def _adamw(name, w, g, m, v):
    rows, width = w.shape
    tt = _pick(rows, 256, 8)
    c1 = 1.0 - ADAM_B1 ** ADAM_STEP
    c2 = 1.0 - ADAM_B2 ** ADAM_STEP

    def body(w_ref, g_ref, m_ref, v_ref, d_ref, mo_ref, vo_ref):
        gg = g_ref[...]
        mn = ADAM_B1 * m_ref[...] + (1.0 - ADAM_B1) * gg
        vn = ADAM_B2 * v_ref[...] + (1.0 - ADAM_B2) * (gg * gg)
        m_hat = mn / c1
        v_hat = vn / c2
        d_ref[...] = -ADAM_LR * (m_hat / (jnp.sqrt(v_hat) + ADAM_EPS) + ADAM_WD * w_ref[...])
        mo_ref[...] = mn
        vo_ref[...] = vn

    spec = pl.BlockSpec((tt, width), lambda i: (i, 0))
    return pl.pallas_call(
        body, grid=(rows // tt,), in_specs=[spec] * 4, out_specs=[spec] * 3,
        out_shape=[jax.ShapeDtypeStruct((rows, width), f32)] * 3,
        compiler_params=pltpu.CompilerParams(dimension_semantics=("parallel",)),
        name=name,
    )(w, g, m, v)


def _place():
    return lax.axis_index("x"), lax.axis_index("y"), lax.axis_index("c")


def _ag8(name, blk):
    m, w = blk.shape

    def body(x_ref, out_ref, send_sems, recv_sems, local_sem):
        x, y, c = _place()
        me, sibling = (x, y, c), (x, y, 1 - c)
        chips = _other_chips(x, y)

        def slot(px, py, pc):
            return out_ref.at[4 * px + 2 * py + pc]

        def copy(k, block, to, src=None):
            return pltpu.make_async_remote_copy(src_ref=slot(*block) if src is None else src, dst_ref=slot(*block),
                                                send_sem=send_sems.at[k], recv_sem=recv_sems.at[k], device_id=to,
                                                device_id_type=MESH_ID)

        mine = pltpu.make_async_copy(x_ref, slot(*me), local_sem)
        mine.start()
        first = [copy(0, me, sibling, src=x_ref)] + [copy(1 + j, me, (*chip, c), src=x_ref) for j, chip in enumerate(chips)]
        for cp in first:
            cp.start()
        passed = [copy(4 + j, (*chip, c), sibling) for j, chip in enumerate(chips)]
        for j, chip in enumerate(chips):
            copy(1 + j, (*chip, c), me).wait_recv()
            passed[j].start()
        copy(0, sibling, me).wait_recv()
        for j, chip in enumerate(chips):
            copy(4 + j, (*chip, 1 - c), me).wait_recv()
        for cp in first + passed:
            cp.wait_send()
        mine.wait()

    return pl.pallas_call(
        body, out_shape=jax.ShapeDtypeStruct((8, m, w), blk.dtype),
        in_specs=[pl.BlockSpec(memory_space=pltpu.VMEM)], out_specs=pl.BlockSpec(memory_space=pltpu.VMEM),
        scratch_shapes=[pltpu.SemaphoreType.DMA((7,)), pltpu.SemaphoreType.DMA((7,)), pltpu.SemaphoreType.DMA],
        name=name,
    )(blk)


def _other_chips(x, y):
    return [(1 - x, y), (x, 1 - y), (1 - x, 1 - y)]


_ANY = pl.BlockSpec(memory_space=pl.ANY)


def _rcopy(src, dst, send_sems, recv_sems, k, dev):
    return pltpu.make_async_remote_copy(src_ref=src, dst_ref=dst, send_sem=send_sems.at[k], recv_sem=recv_sems.at[k],
                                        device_id=dev, device_id_type=MESH_ID)


def _run_job(name, job):
    j_ins, j_outs, j_sems = _job_parts(job)
    n = len(j_ins)

    def body(*refs):
        job["start"](refs[:n], refs[n:n + len(j_outs)], *refs[n + len(j_outs):])
        job["finish"](refs[:n], refs[n:n + len(j_outs)], *refs[n + len(j_outs):])

    return pl.pallas_call(body, out_shape=j_outs, in_specs=[_ANY] * n, out_specs=[_ANY] * len(j_outs),
                          scratch_shapes=j_sems, name=name)(*j_ins)


def _ag4_job(ws):
    n = len(ws)

    def plan(w_refs, out_refs, send_sems, recv_sems):
        x, y, c = _place()
        chip = 2 * x + y
        sibling = (x, y, 1 - c)
        chips = _other_chips(x, y)
        mine = [pl.ds(c * (w.shape[0] // 2), w.shape[0] // 2) for w in ws]
        other = [pl.ds((1 - c) * (w.shape[0] // 2), w.shape[0] // 2) for w in ws]
        rc = lambda src, dst, k, dev: _rcopy(src, dst, send_sems, recv_sems, k, dev)
        first = [rc(w_refs[t].at[mine[t]], out_refs[t].at[chip, mine[t]], 7 * t + k, (px, py, c))
                 for t in range(n) for k, (px, py) in enumerate(chips)]
        own = [rc(w_refs[t], out_refs[t].at[chip], 7 * t + 6, sibling) for t in range(n)]
        landed = [rc(out_refs[t].at[2 * px + py, mine[t]], out_refs[t].at[2 * px + py, mine[t]], 7 * t + k, (px, py, c))
                  for t in range(n) for k, (px, py) in enumerate(chips)]
        forward = [rc(out_refs[t].at[2 * px + py, mine[t]], out_refs[t].at[2 * px + py, mine[t]], 7 * t + 3 + k, sibling)
                   for t in range(n) for k, (px, py) in enumerate(chips)]
        handed = [rc(out_refs[t].at[2 * px + py, other[t]], out_refs[t].at[2 * px + py, other[t]], 7 * t + 3 + k, sibling)
                  for t in range(n) for k, (px, py) in enumerate(chips)]
        return first, own, landed, forward, handed

    def start(*refs):
        first, own, _, _, _ = plan(*refs)
        for cp in first + own:
            cp.start()

    def finish(*refs):
        first, own, landed, forward, handed = plan(*refs)
        for arrived, fw in zip(landed, forward):
            arrived.wait_recv()
            fw.start()
        for cp in handed + own:
            cp.wait_recv()
        for cp in first + own + forward:
            cp.wait_send()

    return dict(ins=ws, out_shapes=[jax.ShapeDtypeStruct((4,) + w.shape, w.dtype) for w in ws], nsem=7 * n,
                start=start, finish=finish)


def _rs_sibling_swap(gs, tag):
    n = len(gs)

    def body(*refs):
        g_refs, recv_refs = refs[:n], refs[n:2 * n]
        send_sems, recv_sems = refs[2 * n:]
        x, y, c = _place()
        copies = []
        for t in range(n):
            rh = gs[t].shape[1] // 2
            for s_ in range(4):
                cp = _rcopy(g_refs[t].at[s_, pl.ds((1 - c) * rh, rh)], recv_refs[t].at[s_], send_sems, recv_sems,
                            4 * t + s_, (x, y, 1 - c))
                cp.start()
                copies.append(cp)
        for cp in copies:
            cp.wait_recv()
        for cp in copies:
            cp.wait_send()

    return pl.pallas_call(
        body, out_shape=[jax.ShapeDtypeStruct((4, g.shape[1] // 2, g.shape[2]), g.dtype) for g in gs],
        in_specs=[_ANY] * n, out_specs=[_ANY] * n,
        scratch_shapes=[pltpu.SemaphoreType.DMA((4 * n,)), pltpu.SemaphoreType.DMA((4 * n,))],
        name="grads_sibling_swap_" + tag,
    )(*gs)


def _chip_exchange_job(ps):
    n = len(ps)

    def plan(p_refs, recv_refs, send_sems, recv_sems):
        x, y, c = _place()
        return [_rcopy(p_refs[t].at[2 * px + py], recv_refs[t].at[k], send_sems, recv_sems, 3 * t + k, (px, py, c))
                for t in range(n) for k, (px, py) in enumerate(_other_chips(x, y))]

    def start(*refs):
        for cp in plan(*refs):
            cp.start()

    def finish(*refs):
        copies = plan(*refs)
        for cp in copies:
            cp.wait_recv()
        for cp in copies:
            cp.wait_send()

    return dict(ins=ps, out_shapes=[jax.ShapeDtypeStruct((3,) + p.shape[1:], p.dtype) for p in ps], nsem=3 * n,
                start=start, finish=finish)


_JOIN_PIECES = 4


def _rs_sibling_join(qs):
    n = len(qs)
    npc = _JOIN_PIECES

    def body(*refs):
        q_refs, out_refs = refs[:n], refs[n:2 * n]
        send_sems, recv_sems = refs[2 * n:]
        x, y, c = _place()
        copies = []
        for t in range(n):
            rh = qs[t].shape[0] // 2
            pr = rh // npc
            for i in range(npc):
                rows = pl.ds(c * rh + i * pr, pr)
                cp = _rcopy(q_refs[t].at[rows], out_refs[t].at[rows], send_sems, recv_sems, npc * t + i,
                            (x, y, 1 - c))
                cp.start()
                copies.append(cp)
        for t in range(n):
            rh = qs[t].shape[0] // 2
            pr = rh // npc
            for i in range(npc):
                rows = pl.ds((1 - c) * rh + i * pr, pr)
                _rcopy(q_refs[t].at[rows], out_refs[t].at[rows], send_sems, recv_sems, npc * t + i,
                       (x, y, 1 - c)).wait_recv()
        for cp in copies:
            cp.wait_send()

    return pl.pallas_call(
        body, out_shape=[jax.ShapeDtypeStruct(q.shape, q.dtype) for q in qs],
        in_specs=[_ANY] * n, out_specs=[_ANY] * n, input_output_aliases={t: t for t in range(n)},
        scratch_shapes=[pltpu.SemaphoreType.DMA((npc * n,)), pltpu.SemaphoreType.DMA((npc * n,))],
        name="grads_sibling_join",
    )(*qs)


def _add_half(name, g, recv, ci):
    S, r, w = g.shape
    rh = r // 2
    tt = _pick(rh, 256, 16)
    nb = rh // tt

    def body(c_ref, a_ref, b_ref, o_ref, ob_ref):
        v = a_ref[...] + b_ref[...]
        o_ref[...] = v
        ob_ref[...] = v.astype(bf16)

    blk = pl.BlockSpec((1, tt, w), lambda s_, i, cr: (s_, i, 0))
    grid_spec = pltpu.PrefetchScalarGridSpec(
        num_scalar_prefetch=1, grid=(S, nb),
        in_specs=[pl.BlockSpec((1, tt, w), lambda s_, i, cr: (s_, cr[0] * nb + i, 0)), blk],
        out_specs=[blk, blk])
    return pl.pallas_call(body, grid_spec=grid_spec,
                          out_shape=[jax.ShapeDtypeStruct((S, rh, w), f32), jax.ShapeDtypeStruct((S, rh, w), bf16)],
                          name=name)(ci.reshape(1).astype(jnp.int32), g, recv)


def _sum_chip(name, pair, others, chip, ci):
    _, rh, w = pair.shape
    tt = _pick(rh, 128, 16)
    nb = rh // tt

    def body(chip_ref, core_ref, a_ref, b_ref, o_ref):
        o_ref[...] = ((a_ref[0] + b_ref[0].astype(f32)) + b_ref[1].astype(f32)) + b_ref[2].astype(f32)

    grid_spec = pltpu.PrefetchScalarGridSpec(
        num_scalar_prefetch=2, grid=(nb,),
        in_specs=[pl.BlockSpec((1, tt, w), lambda i, ch, co: (ch[0], i, 0)),
                  pl.BlockSpec((3, tt, w), lambda i, ch, co: (0, i, 0))],
        out_specs=pl.BlockSpec((tt, w), lambda i, ch, co: (co[0] * nb + i, 0)))
    return pl.pallas_call(body, grid_spec=grid_spec, out_shape=jax.ShapeDtypeStruct((2 * rh, w), f32),
                          name=name)(chip.reshape(1).astype(jnp.int32), ci.reshape(1).astype(jnp.int32), pair, others)


def _sum_devices(gathered, head_row):
    _, rows, width = gathered.shape

    def body(g_ref, out_ref, head_ref):
        acc = g_ref[0]
        for d in range(1, 8):
            acc = acc + g_ref[d]
        out_ref[...] = acc
        row = acc[head_row:head_row + 1, :]
        hs = row[:, 0:HEAD_DIM]
        for h in range(1, HEADS):
            hs = hs + row[:, h * HEAD_DIM:(h + 1) * HEAD_DIM]
        head_ref[...] = jnp.zeros_like(head_ref)
        head_ref[0:1, 0:HEAD_DIM] = hs

    return pl.pallas_call(
        body, out_shape=[jax.ShapeDtypeStruct((rows, width), f32), jax.ShapeDtypeStruct((8, LANES), f32)],
        name="small_grads_sum",
    )(gathered)


def _pack(arrs, rows_mult, dtype):
    flat = jnp.concatenate([a.reshape(-1).astype(dtype) for a in arrs])
    per = PACK_W * rows_mult
    total = -(-flat.shape[0] // per) * per
    return jnp.pad(flat, (0, total - flat.shape[0])).reshape(total // PACK_W, PACK_W)


def _unpack(buf, shapes):
    flat = buf.reshape(-1)
    out, off = [], 0
    for s in shapes:
        n = int(np.prod(s))
        out.append(flat[off:off + n].reshape(s))
        off += n
    return out


_BIG = ["w_in", "w_branch_gdn", "w_branch_rwkv", "w_out", "w_ffn_in", "w_ffn_out"]
_MID = ["conv_gdn", "conv_ffn", "w2", "a2", "g2"]
_SMALL =["b_ada", "norm1_w", "a_log", "dt_bias", "onorm_gdn", "mu_rwkv", "w0", "a0", "k_k", "k_a", "r_k", "lnx_w",
          "lnx_b", "norm2_w", "norm_f_w"]
_ORDER = ["w_ada", "b_ada", "norm1_w", "w_in", "conv_gdn", "a_log", "dt_bias", "onorm_gdn", "w_branch_gdn", "mu_rwkv",
          "w0", "w2", "a0", "a2", "g2", "k_k", "k_a", "r_k", "lnx_w", "lnx_b", "w_branch_rwkv", "w_out", "norm2_w",
          "w_ffn_in", "conv_ffn", "w_ffn_out", "norm_f_w"]


_WIN_SEGMENTS = [(0, 1536, 0), (2064, 3600, 1536), (1536, 2048, 3072), (3600, 3728, 3584), (2048, 2064, 3712),
                 (3728, 3888, 3840), (3888, 5936, 4096)]
_WIN_PADDED = 6144
_COL_QKV, _COL_RKV, _COL_Z = (0, 1, 2), (3, 4, 5), 6
_COL_LORA, _COL_BA = 28, 29
_COL_GATE_LORA = 15
_COL_GL = (4, 5)
_JOINT_QKV, _JOINT_RKV, _JOINT_Z, _JOINT_GL = 0, 1, 6, 2
_SMALL_BLOCKS_AT = 3584


def _win_pad(shards):
    n = shards.shape[2]
    parts, at = [], 0
    for lo, hi, dst in _WIN_SEGMENTS:
        if dst > at:
            parts.append(jnp.zeros((shards.shape[1], dst - at), shards.dtype))
        c = lo
        while c < hi:
            j = c // n
            e = min(hi, (j + 1) * n)
            parts.append(shards[j][:, c - j * n:e - j * n])
            c = e
        at = dst + hi - lo
    if at < _WIN_PADDED:
        parts.append(jnp.zeros((shards.shape[1], _WIN_PADDED - at), shards.dtype))
    return jnp.concatenate(parts, axis=1)


def _win_unpad_shards(g, n):
    shards = []
    for j in range(4):
        parts = []
        for lo, hi, dst in sorted(_WIN_SEGMENTS):
            a, b = max(lo, j * n), min(hi, (j + 1) * n)
            if a < b:
                parts.append(g[:, dst + a - lo:dst + b - lo])
        shards.append(jnp.concatenate(parts, axis=1))
    return jnp.stack(shards)


def kernel(x, c, w_ada, b_ada, norm1_w, w_in, conv_gdn, a_log, dt_bias, onorm_gdn, w_branch_gdn, mu_rwkv, w0, w2, a0, a2, g2, k_k, k_a, r_k, lnx_w, lnx_b, w_branch_rwkv, w_out, norm2_w, w_ffn_in, conv_ffn, w_ffn_out, norm_f_w, loss_target, m_w_ada, m_b_ada, m_norm1_w, m_w_in, m_conv_gdn, m_a_log, m_dt_bias, m_onorm_gdn, m_w_branch_gdn, m_mu_rwkv, m_w0, m_w2, m_a0, m_a2, m_g2, m_k_k, m_k_a, m_r_k, m_lnx_w, m_lnx_b, m_w_branch_rwkv, m_w_out, m_norm2_w, m_w_ffn_in, m_conv_ffn, m_w_ffn_out, m_norm_f_w, v_w_ada, v_b_ada, v_norm1_w, v_w_in, v_conv_gdn, v_a_log, v_dt_bias, v_onorm_gdn, v_w_branch_gdn, v_mu_rwkv, v_w0, v_w2, v_a0, v_a2, v_g2, v_k_k, v_k_a, v_r_k, v_lnx_w, v_lnx_b, v_w_branch_rwkv, v_w_out, v_norm2_w, v_w_ffn_in, v_conv_ffn, v_w_ffn_out, v_norm_f_w):
    args = dict(locals())
    W = {n: args[n] for n in _ORDER}
    Mo = {n: args["m_" + n] for n in _ORDER}
    Vo = {n: args["v_" + n] for n in _ORDER}
    shapes = {n: W[n].shape for n in _ORDER}
    sq = lambda a: a.reshape(a.shape[-2:]) if a.ndim == 3 else a.reshape(1, -1)
    row = lambda a: a.reshape(1, -1)

    xi, yi, ci = lax.axis_index("x"), lax.axis_index("y"), lax.axis_index("c")
    dev = 4 * xi + 2 * yi + ci
    chip = 2 * xi + yi

    x2 = x[0]
    tgt = loss_target[0]
    T, D = x2.shape
    N = T // CHUNK
    tt_l = _pick(T, 512, CHUNK)
    tt_h = _pick(T, 128, CHUNK)

    mid_shapes = [shapes[n][1:] for n in _MID]
    small_blk = _pack([c] + [W[n] for n in _MID], 8, f32)
    small_all = _ag8("gather_c_mid", small_blk)
    c_all = small_all[:, 0, :]
    per_chip = small_all[0::2].reshape(4, -1)[:, D:]
    mid = [dict(zip(_MID, _unpack(per_chip[j], mid_shapes))) for j in range(4)]
    catm = lambda n: jnp.concatenate([mid[j][n] for j in range(4)], axis=1)
    conv_gdn_f, conv_ffn_f = catm("conv_gdn"), catm("conv_ffn")
    w2f, a2f, g2f = catm("w2"), catm("a2"), catm("g2")

    (win_s,) = _run_job("w_in_all_gather", _ag4_job([sq(W["w_in"]).astype(bf16)]))
    later_weights = _ag4_job([sq(W[n]).astype(bf16) for n in _BIG[1:]])
    win_p = _win_pad(win_s)
    zpad = lambda a, top, bot: jnp.pad(a, ((top, bot), (0, 0)))
    w2p, a2p, g2p = zpad(w2f, 0, 64), zpad(a2f, 64, 0), zpad(g2f, 0, 96)

    ncol = shapes["w_ada"][2]
    b_cols = lax.dynamic_slice(sq(W["b_ada"]), (0, chip * ncol), (1, ncol))
    cond16, mod_cols = _ada_fwd(jnp.pad(c_all, ((0, 8), (0, 0))), sq(W["w_ada"]), b_cols)
    mod_all = _ag8("gather_mod", mod_cols[:8])
    mod_mine = lax.dynamic_slice(mod_all[0::2], (0, dev, 0), (4, 1, ncol)).reshape(1, 4 * ncol)
    shift1, scale1, gate1, shift2, scale2, gate2 = [mod_mine[:, i * D:(i + 1) * D] for i in range(6)]

    seg = _seg_matrix(WIDTH, HEAD_DIM)
    norm1 = [sq(W["norm1_w"]), shift1, scale1]
    h1 = _stage_fwd("norm_mod1", _fn_norm_mod, [_whole(x2)], norm1, [(D, bf16)], tt_l)[0]
    p = _matmul("in_proj", h1, win_p, "nn")

    cgq = [row(conv_gdn_f[j, part * WIDTH:(part + 1) * WIDTH]) for part in range(3) for j in range(4)]
    lane_pad = lambda a: jnp.pad(row(a), ((0, 0), (8, LANES - 16)))
    gdn_pre_ps = cgq + [lane_pad(W["a_log"]), lane_pad(W["dt_bias"]), seg, _chunk_tri(tt_h, CHUNK)]
    gdn_pre_ts = [(p, WIDTH, ci_, (0, 1, 2, 3)) for ci_ in _COL_QKV] + [(p, LANES, _COL_BA, None)]
    q_, k_, v_, beta_t, gc_t = _stage_fwd("gdn_pre", _fn_gdn_pre, gdn_pre_ts, gdn_pre_ps,
                                          [(WIDTH, f32)] * 3 + [(LANES, f32)] * 2, tt_h)
    o_, gdn_hist, gdn_inv, wbg_s, wbr_s, wout_s, wfi_s, wfo_s = _scan_fwd(
        "gdn_scan", _gdn_chunk, [q_, k_, v_], [beta_t, gc_t], job=later_weights)
    wout_f = wout_s.reshape(D, D)
    wfo = wfo_s.reshape(D_FF, D)
    ow512 = jnp.tile(row(W["onorm_gdn"]), (1, HEADS))
    gdn_post_ts = [_whole(o_), (p, WIDTH, _COL_Z, None)]
    ya = _stage_fwd("gdn_post", _fn_gdn_post, gdn_post_ts, [ow512, seg], [(WIDTH, bf16)], tt_l)[0]

    mu = sq(W["mu_rwkv"])
    rw_ps = [mu[:, 0:512], mu[:, 512:1024], mu[:, 1024:1536], mu[:, 1536:1664], jnp.pad(mu[:, 1664:1824], ((0, 0), (0, 96))),
             sq(W["w0"]), w2p, sq(W["a0"]), a2p, g2p, sq(W["k_k"]), sq(W["k_a"]), seg]
    rw_ts = [(p, WIDTH, ci_, (0, 1)) for ci_ in _COL_RKV] + [(p, LANES, _COL_LORA, (0, 1)),
                                                              (p, 256, _COL_GATE_LORA, (0, 1))]
    rw_out = _stage_fwd("rwkv_pre", _fn_rwkv_pre, rw_ts, rw_ps, [(WIDTH, f32)] * 7, tt_h)
    r_, lw_, k2_, vv_, na_, b_, g_ = rw_out
    rw_ins = [r_, lw_, k2_, vv_, na_, b_]
    y_, rw_hist, rw_inv = _scan_fwd("rwkv_scan", _rwkv_chunk, rw_ins)
    rwp_ps = [sq(W["lnx_w"]), sq(W["lnx_b"]), row(W["r_k"]), seg]
    rwp_ts = [_whole(y_), _whole(r_), _whole(k2_), _whole(vv_), _whole(g_)]
    yb = _stage_fwd("rwkv_post", _fn_rwkv_post, rwp_ts, rwp_ps, [(WIDTH, bf16)], tt_l)[0]

    big_a = _matmul("branch_gdn", ya, wbg_s, "nn", shards=4)
    big_b = _matmul("branch_rwkv", yb, wbr_s, "nn", shards=4)
    merge_ts = [(p, D, _COL_GL[0], None), (p, D, _COL_GL[1], None), _whole(big_a), _whole(big_b)]
    merged = _stage_fwd("merge", _fn_merge, merge_ts, [], [(D, bf16)], tt_l)[0]
    mo = _matmul("out_proj", merged, wout_f, "nn")
    norm2 = [gate1, sq(W["norm2_w"]), shift2, scale2]
    x1, h2 = _stage_fwd("resid_norm_mod2", _fn_resid_norm_mod, [_whole(x2), _whole(mo)], norm2, [(D, f32), (D, bf16)], tt_l)
    f = _matmul("ffn_in", h2, wfi_s, "nn", shards=4, tn=1408)
    cg_ps = [row(conv_ffn_f[j]) for j in range(3)]
    cg_ts = [(f, D_FF, 0, (0, 1, 2)), (f, D_FF, 1, None)]
    act = _stage_fwd("convglu", _fn_convglu, cg_ts, cg_ps, [(D_FF, bf16)], tt_h)[0]
    fo = _matmul("ffn_out", act, wfo, "nn", tk=1408)

    dx1_a, dfo, dgate2, dnormf, loss_part = _final_stage(x1, fo, tgt, gate2, row(W["norm_f_w"]), tt_l)

    dact = _matmul("d_act", dfo, wfo, "nt", tn=1408)
    g_wfo = _matmul("g_ffn_out", act, dfo, "tn", tm=1408)
    _, dcf, df = _stage_bwd("convglu_bwd", _fn_convglu, cg_ts, cg_ps, [[_whole(dact)]], tt_h, [True] * 2, [True] * 3,
                            joint=([0, 1], 2 * D_FF, 0, None), dtypes=[bf16])
    dh2 = _matmul("d_h2", df, wfi_s, "nt", shards=4, tk=1408)
    g_wfi = _matmul("g_ffn_in", h2, df, "tn", shards=4, tn=1408)
    (dx_a, dmo), (dgate1, dnorm2, dshift2, dscale2), _ = _stage_bwd(
        "resid_norm_mod2_bwd", _fn_resid_norm_mod, [_whole(x2), _whole(mo)], norm2,
        [[_whole(dx1_a)], [_whole(dh2)]], tt_l, [True, True], [True] * 4, dtypes=[f32, bf16])
    dmerged = _matmul("d_merged", dmo, wout_f, "nt")
    g_wout = _matmul("g_out_proj", merged, dmo, "tn")
    (dbig_a, dbig_b), _, dp = _stage_bwd("merge_bwd", _fn_merge, merge_ts, [], [[_whole(dmerged)]], tt_l, [True] * 4, [],
                                         joint=([0, 1], p.shape[1], _JOINT_GL, None), dtypes=[bf16] * 3)
    dya = _matmul("d_ya", dbig_a, wbg_s, "nt", shards=4)
    g_wbg = _matmul("g_branch_gdn", ya, dbig_a, "tn", shards=4)
    dyb = _matmul("d_yb", dbig_b, wbr_s, "nt", shards=4)
    g_wbr = _matmul("g_branch_rwkv", yb, dbig_b, "tn", shards=4)

    (dy_, dr_p, dk2_p, dv_p, dg_p), (dlnxw, dlnxb, drk), _ = _stage_bwd(
        "rwkv_post_bwd", _fn_rwkv_post, rwp_ts, rwp_ps, [[_whole(dyb)]], tt_l, [True] * 5, [True, True, True, False])
    gs_a = [g_wbg, g_wbr, g_wout.reshape(4, D // 4, D), g_wfi, g_wfo.reshape(4, D_FF // 4, D)]
    pairs_a = [_add_half("grads_pair_sum%d" % (t + 1), g, r_, ci) for t, (g, r_) in enumerate(zip(gs_a, _rs_sibling_swap(gs_a, "a")))]
    dr_c, dlw_c, dk2_c, dv_c, dna_c, db_c, *others_a = _scan_bwd(
        "rwkv_scan_bwd", _rwkv_chunk, rw_ins, [], rw_hist, rw_inv, dy_, job=_chip_exchange_job([pb for _, pb in pairs_a]))
    rw_cots = [[_whole(dr_p), _whole(dr_c)], [_whole(dlw_c)], [_whole(dk2_p), _whole(dk2_c)],
               [_whole(dv_p), _whole(dv_c)], [_whole(dna_c)], [_whole(db_c)], [_whole(dg_p)]]
    (dl_, dg_), rw_dp, dp = _stage_bwd("rwkv_pre_bwd", _fn_rwkv_pre, rw_ts, rw_ps, rw_cots, tt_h, [True] * 5,
                                       [True] * 12 + [False], joint=([0, 1, 2], p.shape[1], _JOINT_RKV, dp), dtypes=[bf16] * 3)
    dmu_r, dmu_k, dmu_v, dmu_l, dmu_g, dw0, dw2p, da0, da2p, dg2p, dkk, dka = rw_dp

    (do_,), (dow512,), dp = _stage_bwd("gdn_post_bwd", _fn_gdn_post, gdn_post_ts, [ow512, seg], [[_whole(dya)]], tt_l,
                                       [True, True], [True, False], joint=([1], p.shape[1], _JOINT_Z, dp), dtypes=[f32, bf16])
    d_gdn = _scan_bwd("gdn_scan_bwd", _gdn_chunk, [q_, k_, v_], [beta_t, gc_t], gdn_hist, gdn_inv, do_)
    gdn_cots = [[_whole(a)] for a in d_gdn]
    (dba,), gdn_dp, dp = _stage_bwd("gdn_pre_bwd", _fn_gdn_pre, gdn_pre_ts, gdn_pre_ps, gdn_cots, tt_h, [True] * 4,
                                    [True] * 14 + [False, False], joint=([0, 1, 2], p.shape[1], _JOINT_QKV, dp), dtypes=[bf16] * 2)
    dp = lax.dynamic_update_slice(dp, jnp.concatenate([dl_, dba, dg_], axis=1), (0, _SMALL_BLOCKS_AT))
    g_win_s = _win_unpad_shards(_matmul("g_in_proj", h1, dp, "tn"), shapes["w_in"][2])
    pair_win = _add_half("grads_pair_sum0", g_win_s, _rs_sibling_swap([g_win_s], "b")[0], ci)
    dh1, others_win = _matmul("d_h1", dp, win_p, "nt", job=_chip_exchange_job([pair_win[1]]))
    (grad_x,), (dnorm1, dshift1, dscale1), _ = _stage_bwd("norm_mod1_bwd", _fn_norm_mod_and_x, [_whole(x2)], norm1,
                                                          [[_whole(dh1)], [_whole(dx_a)]], tt_l, [True], [True] * 3)

    dmod = jnp.concatenate([dshift1, dscale1, dgate1, dshift2, dscale2, dgate2], axis=1)
    g_conv_gdn = jnp.concatenate([jnp.concatenate([gdn_dp[4 * part + j] for part in range(3)], axis=1) for j in range(4)], axis=0)
    g_conv_ffn = jnp.concatenate(dcf, axis=0)
    g_mu = jnp.concatenate([dmu_r, dmu_k, dmu_v, dmu_l, dmu_g[:, :160]], axis=1)
    small_parts = {"b_ada": dmod, "norm1_w": dnorm1, "a_log": gdn_dp[12][:, 8:16], "dt_bias": gdn_dp[13][:, 8:16],
                   "mu_rwkv": g_mu, "w0": dw0, "a0": da0, "k_k": dkk, "k_a": dka, "r_k": drk, "lnx_w": dlnxw,
                   "lnx_b": dlnxb, "norm2_w": dnorm2, "norm_f_w": dnormf}
    small_names = [n for n in _SMALL if n != "onorm_gdn"]
    mid_full = [g_conv_gdn, g_conv_ffn, dw2p[0:64], da2p[64:128], dg2p[0:160]]
    body_rows = _pack([small_parts[n] for n in small_names] + [loss_part[:, 0:1]] + mid_full, 1, f32)
    head_row = body_rows.shape[0]
    small_g = jnp.concatenate([body_rows, jnp.pad(dow512, ((0, 0), (0, PACK_W - WIDTH)))], axis=0)
    small_g = jnp.pad(small_g, ((0, -small_g.shape[0] % 8), (0, 0)))
    small_all_g = _ag8("gather_small_grads", small_g)
    small_sum, head_sum = _sum_devices(small_all_g, head_row)
    small_shapes = [shapes[n][1:] if n != "norm_f_w" else shapes[n] for n in small_names]
    un = _unpack(small_sum, small_shapes + [(1,)] + [g.shape for g in mid_full])
    small_grads = dict(zip(small_names, un))
    loss = un[len(small_names)].reshape(())
    small_grads["onorm_gdn"] = head_sum[0, 0:HEAD_DIM]
    for n, g in zip(_MID, un[len(small_names) + 1:]):
        wcols = shapes[n][2]
        small_grads[n] = lax.dynamic_slice(g, (0, chip * wcols), (g.shape[0], wcols))

    dmod_all = small_all_g[:, 0:6, :].reshape(8, 6 * PACK_W)
    dmod_cols = lax.dynamic_slice(dmod_all, (0, chip * ncol), (8, ncol))
    g_wada = _matmul("g_w_ada", cond16, jnp.pad(dmod_cols, ((0, 8), (0, 0))), "tn")

    pairs = [pair_win] + pairs_a
    others = [others_win] + others_a
    halves = [_sum_chip("grads_chip_sum%d" % t, pf, o_, chip, ci) for t, ((pf, _), o_) in enumerate(zip(pairs, others))]
    big_grads = dict(zip(_BIG, _rs_sibling_join(halves)))

    res = {tag: {} for tag in ("grad", "delta", "new_m", "new_v")}

    def put(n, g, d, m_, v_):
        for tag, val in zip(("grad", "delta", "new_m", "new_v"), (g, d, m_, v_)):
            res[tag][n] = val.reshape(shapes[n])

    for n in _BIG:
        put(n, big_grads[n], *_adamw("adamw_" + n, sq(W[n]), big_grads[n], sq(Mo[n]), sq(Vo[n])))
    put("w_ada", g_wada, *_adamw("adamw_w_ada", sq(W["w_ada"]), g_wada, sq(Mo["w_ada"]), sq(Vo["w_ada"])))
    rest = _SMALL + _MID
    pk = lambda d: _pack([d[n] for n in rest], 8, f32)
    sg = pk(small_grads)
    sm = _adamw("adamw_small", pk(W), sg, pk(Mo), pk(Vo))
    for tag, buf in zip(("grad", "delta", "new_m", "new_v"), (sg,) + tuple(sm)):
        res[tag].update(zip(rest, _unpack(buf, [shapes[n] for n in rest])))
    outs = [loss, grad_x.reshape(x.shape)]
    for tag in ("grad", "delta", "new_m", "new_v"):
        outs += [res[tag][n] for n in _ORDER]
    return tuple(outs)
```

```python
import numpy as np
import jax
import jax.numpy as jnp
from jax import lax
from jax.experimental import pallas as pl
from jax.experimental.pallas import tpu as pltpu

f32 = jnp.float32
bf16 = jnp.bfloat16

LANES = 128
HEADS = 8
HEAD_DIM = 64
WIDTH = HEADS * HEAD_DIM
CHUNK = 64
D_FF = 2816
NORM_EPS = 1e-6
LNX_EPS = 64e-5
PACK_W = 1024
MESH_ID = pl.DeviceIdType.MESH

ADAM_LR, ADAM_B1, ADAM_B2, ADAM_EPS, ADAM_WD, ADAM_STEP = 0.001, 0.9, 0.999, 1e-08, 0.01, 10


def _pick(n, target, mult):
    if n <= target:
        return n
    best = None
    for t in range(mult, target + 1, mult):
        if n % t == 0:
            best = t
    assert best is not None, (n, target, mult)
    return best


def _split_bf16(x, n):
    parts, r = [], x
    for i in range(n):
        p = r.astype(bf16)
        parts.append(p)
        if i + 1 < n:
            r = r - p.astype(f32)
    return parts


def _xdot_r_impl(x, m, n, dims):
    acc = None
    for p in _split_bf16(x, n):
        t = lax.dot_general(p, m, dims, preferred_element_type=f32)
        acc = t if acc is None else acc + t
    return acc


def _make_xdot_r(n):
    nn = (((1,), (0,)), ((), ()))
    nt = (((1,), (1,)), ((), ()))

    @jax.custom_vjp
    def xdot(x, m):
        return _xdot_r_impl(x, m, n, nn)

    def fwd(x, m):
        return _xdot_r_impl(x, m, n, nn), m

    def bwd(m, ct):
        return _xdot_r_impl(ct, m, n, nt), jnp.zeros_like(m)

    xdot.defvjp(fwd, bwd)
    return xdot


_segsum = _make_xdot_r(2)


def _xdot_l_impl(m, x, n, dims):
    acc = None
    for p in _split_bf16(x, n):
        t = lax.dot_general(m, p, dims, preferred_element_type=f32)
        acc = t if acc is None else acc + t
    return acc


@jax.custom_vjp
def _xdot_l(m, x):
    return _xdot_l_impl(m, x, 3, (((1,), (0,)), ((), ())))


def _xdot_l_fwd(m, x):
    return _xdot_l(m, x), m


def _xdot_l_bwd(m, ct):
    return jnp.zeros_like(m), _xdot_l_impl(m, ct, 3, (((0,), (0,)), ((), ())))


_xdot_l.defvjp(_xdot_l_fwd, _xdot_l_bwd)


@jax.custom_vjp
def _bdot(x, w):
    return jnp.dot(x.astype(bf16), w.astype(bf16), preferred_element_type=f32)


def _bdot_fwd(x, w):
    return _bdot(x, w), (x, w)


def _bdot_bwd(res, ct):
    x, w = res
    c = ct.astype(bf16)
    dx = lax.dot_general(c, w.astype(bf16), (((1,), (1,)), ((), ())), preferred_element_type=f32)
    dw = lax.dot_general(x.astype(bf16), c, (((0,), (0,)), ((), ())), preferred_element_type=f32)
    return dx, dw


_bdot.defvjp(_bdot_fwd, _bdot_bwd)


def _silu(x):
    return x * jax.nn.sigmoid(x)


def _softplus(x):
    return jnp.maximum(x, 0.0) + jnp.log(1.0 + jnp.exp(-jnp.abs(x)))


def _rms(x, w, eps):
    return x * lax.rsqrt(jnp.mean(x * x, axis=-1, keepdims=True) + eps) * w


def _seg_matrix(width, seg):
    i = np.arange(width)
    return jnp.asarray((i[:, None] // seg) == (i[None, :] // seg), dtype=bf16)


def _chunk_tri(rows, chunk):
    i = np.arange(rows)
    return jnp.asarray(((i[:, None] // chunk) == (i[None, :] // chunk)) & (i[:, None] >= i[None, :]), dtype=bf16)


HALO = 8


def _full_spec(shape):
    nd = len(shape)
    return pl.BlockSpec(shape, lambda i: (0,) * nd)


def _entry_specs(entries, tt, block_of):
    specs, ops = [], []
    for arr, w, ci, shifts in entries:
        specs.append(pl.BlockSpec((tt, w), lambda i, ci=ci: (block_of(i), ci)))
        ops.append(arr)
        if shifts:
            specs.append(pl.BlockSpec((HALO, w), lambda i, ci=ci: (jnp.maximum(block_of(i) * (tt // HALO) - 1, 0), ci)))
            ops.append(arr)
    return specs, ops


def _load_entries(entries, refs, first):
    tiles, k = [], 0
    for _, w, _, shifts in entries:
        x = refs[k][...].astype(f32)
        k += 1
        if not shifts:
            tiles.append(x)
            continue
        halo = jnp.where(first, 0.0, refs[k][...].astype(f32))
        k += 1
        row = lax.broadcasted_iota(jnp.int32, (HALO, w), 0)
        for s in shifts:
            if s == 0:
                tiles.append(x)
                continue
            r = pltpu.roll(x, s, 0)
            head = jnp.where(row < s, pltpu.roll(halo, s, 0), r[0:HALO])
            tiles.append(jnp.concatenate([head, r[HALO:]], axis=0))
    return tiles


def _unshift_sum(grads, shifts, carry, tt):
    w = grads[0].shape[1]
    row = lax.broadcasted_iota(jnp.int32, (tt, w), 0)
    row8 = lax.broadcasted_iota(jnp.int32, (HALO, w), 0)
    dx, out = None, jnp.zeros((HALO, w), f32)
    for d, s in zip(grads, shifts):
        if s == 0:
            part = d
        else:
            part = jnp.where(row < tt - s, pltpu.roll(d, tt - s, 0), 0.0)
            out = out + jnp.where(row8 >= HALO - s, pltpu.roll(d[0:HALO], HALO - s, 0), 0.0)
        dx = part if dx is None else dx + part
    return jnp.concatenate([dx[:tt - HALO], dx[tt - HALO:] + carry], axis=0), out


def _stage_fwd(name, fn, tiles, params, outs, tt):
    rows = tiles[0][0].shape[0]
    npar = len(params)
    specs, ops = _entry_specs(tiles, tt, lambda i: i)
    nin = len(ops)

    def body(*refs):
        ts = _load_entries(tiles, refs[:nin], pl.program_id(0) == 0)
        ps = [r[...] for r in refs[nin:nin + npar]]
        res = fn(ps, ts)
        for r, v in zip(refs[nin + npar:], res):
            r[...] = v.astype(r.dtype)

    return pl.pallas_call(
        body, grid=(rows // tt,),
        in_specs=specs + [_full_spec(p.shape) for p in params],
        out_specs=[pl.BlockSpec((tt, w), lambda i: (i, 0)) for (w, _) in outs],
        out_shape=[jax.ShapeDtypeStruct((rows, w), dt) for (w, dt) in outs],
        compiler_params=pltpu.CompilerParams(dimension_semantics=("parallel",)),
        name=name,
    )(*ops, *params)


def _stage_bwd(name, fn, tiles, params, cots, tt, tile_grad, param_grad, joint=None, dtypes=None):
    rows = tiles[0][0].shape[0]
    nblk = rows // tt
    npar = len(params)
    block_of = lambda i: nblk - 1 - i
    specs, ops = _entry_specs(tiles, tt, block_of)
    nin = len(ops)
    flat_cots = [c for group in cots for c in group]
    groups = [len(g) for g in cots]
    ncot = len(flat_cots)
    counts = [len(e[3]) if e[3] else 1 for e in tiles]
    dt_entries = [e for e, g in zip(tiles, tile_grad) if g]
    dp_shapes = [p.shape for p, g in zip(params, param_grad) if g]
    ndt = len(dt_entries)
    carry_w = [e[1] for e in dt_entries if e[3]]
    flags = [g for g, n in zip(tile_grad, counts) for _ in range(n)]
    members, j_width, j_cidx, j_buf = joint if joint else ([], 0, 0, None)
    solo = [k for k in range(ndt) if k not in members]
    nsolo, njoint, nbuf = len(solo), int(bool(members)), int(j_buf is not None)
    j_block = sum(dt_entries[k][1] for k in members)
    dtypes = list(dtypes) if dtypes else [f32] * (nsolo + njoint)

    def body(*refs):
        i = pl.program_id(0)
        p_refs = refs[nin:nin + npar]
        c_refs = refs[nin + npar:nin + npar + ncot]
        base = nin + npar + ncot + nbuf
        dt_refs = refs[base:base + nsolo]
        joint_refs = refs[base + nsolo:base + nsolo + njoint]
        dp_refs = refs[base + nsolo + njoint:base + nsolo + njoint + len(dp_shapes)]
        carry_refs = refs[base + nsolo + njoint + len(dp_shapes):]
        ts = _load_entries(tiles, refs[:nin], block_of(i) == 0)
        ps = [r[...] for r in p_refs]

        def f(dp, dt):
            dp, dt = iter(dp), iter(dt)
            pp = [next(dp) if g else p for p, g in zip(ps, param_grad)]
            tl = [next(dt) if g else t for t, g in zip(ts, flags)]
            return fn(pp, tl)

        _, vjp = jax.vjp(f, [p for p, g in zip(ps, param_grad) if g], [t for t, g in zip(ts, flags) if g])
        cs, j = [], 0
        for n in groups:
            acc = c_refs[j][...].astype(f32)
            for q in range(1, n):
                acc = acc + c_refs[j + q][...].astype(f32)
            cs.append(acc)
            j += n
        gp, gt = vjp(cs)

        @pl.when(i == 0)
        def _():
            for r in dp_refs:
                r[...] = jnp.zeros_like(r)
            for r in carry_refs:
                r[...] = jnp.zeros_like(r)

        gt, k, kc, dxs = list(gt), 0, 0, []
        for e, n in zip(dt_entries, [n for n, g in zip(counts, tile_grad) if g]):
            if e[3]:
                dx, out = _unshift_sum(gt[k:k + n], e[3], carry_refs[kc][...], tt)
                carry_refs[kc][...] = out
                kc += 1
            else:
                dx = gt[k]
            dxs.append(dx)
            k += n
        for r, k in zip(dt_refs, solo):
            r[...] = dxs[k].astype(r.dtype)
        off = 0
        for k in members:
            w = dt_entries[k][1]
            joint_refs[0][:, off:off + w] = dxs[k].astype(joint_refs[0].dtype)
            off += w
        for r, v in zip(dp_refs, gp):
            r[...] += v

    res = pl.pallas_call(
        body, grid=(nblk,),
        in_specs=specs + [_full_spec(p.shape) for p in params]
        + [pl.BlockSpec((tt, w), lambda i, ci=ci: (block_of(i), ci)) for (_, w, ci, *_) in flat_cots]
        + [pl.BlockSpec(memory_space=pl.ANY)] * nbuf,
        out_specs=[pl.BlockSpec((tt, dt_entries[k][1]), lambda i: (block_of(i), 0)) for k in solo]
        + [pl.BlockSpec((tt, j_block), lambda i: (block_of(i), j_cidx))] * njoint
        + [_full_spec(s) for s in dp_shapes],
        out_shape=[jax.ShapeDtypeStruct((rows, dt_entries[k][1]), dt) for k, dt in zip(solo, dtypes)]
        + [jax.ShapeDtypeStruct((rows, j_width), dtypes[-1])] * njoint
        + [jax.ShapeDtypeStruct(s, f32) for s in dp_shapes],
        scratch_shapes=[pltpu.VMEM((HALO, w), f32) for w in carry_w],
        input_output_aliases={nin + npar + ncot: nsolo} if nbuf else {},
        compiler_params=pltpu.CompilerParams(dimension_semantics=("arbitrary",)),
        name=name,
    )(*ops, *params, *[c[0] for c in flat_cots], *([j_buf] if nbuf else []))
    res = list(res)
    return res[:nsolo], res[nsolo + njoint:], (res[nsolo] if njoint else None)


def _whole(a):
    return (a, a.shape[1], 0, None)


def _matmul(name, a, b, mode, out_dtype=f32, tm=1024, tn=1024, tk=1024, shards=1, job=None):
    S = shards
    if mode == "nn":
        M, K = a.shape
        w = b.shape[-1]
    elif mode == "nt":
        M = a.shape[0]
        if S > 1:
            _, N, w = b.shape
            K = S * w
        else:
            N, K = b.shape
            w = K
    else:
        K, M = a.shape
        w = b.shape[1] // S
    if mode != "nt":
        N = S * w
    tm = _pick(M, tm, LANES)
    if mode == "nt":
        tn = _pick(N, tn, LANES)
        tk = _pick(w, tk, LANES)
    else:
        tn = _pick(w, tn, LANES)
        tk = _pick(K, tk, LANES if mode == "nn" else 16)
    nk = K // tk
    nb = w // (tk if mode == "nt" else tn)
    if mode == "nn":
        a_spec = pl.BlockSpec((tm, tk), lambda i, j, k: (i, k))
        if S > 1:
            b_spec = pl.BlockSpec((1, tk, tn), lambda i, j, k: (j // nb, k, j % nb))
        else:
            b_spec = pl.BlockSpec((tk, tn), lambda i, j, k: (k, j))
        dims = (((1,), (0,)), ((), ()))
    elif mode == "nt":
        a_spec = pl.BlockSpec((tm, tk), lambda i, j, k: (i, k))
        if S > 1:
            b_spec = pl.BlockSpec((1, tn, tk), lambda i, j, k: (k // nb, j, k % nb))
        else:
            b_spec = pl.BlockSpec((tn, tk), lambda i, j, k: (j, k))
        dims = (((1,), (1,)), ((), ()))
    else:
        a_spec = pl.BlockSpec((tk, tm), lambda i, j, k: (k, i))
        b_spec = pl.BlockSpec((tk, tn), lambda i, j, k: (k, j))
        dims = (((0,), (0,)), ((), ()))
    if mode == "tn" and S > 1:
        o_spec = pl.BlockSpec((1, tm, tn), lambda i, j, k: (j // nb, i, j % nb))
        o_shape = (S, M, w)
    else:
        o_spec = pl.BlockSpec((tm, tn), lambda i, j, k: (i, j))
        o_shape = (M, N)
    b_lead = S > 1 and mode != "tn"
    o_lead = S > 1 and mode == "tn"

    j_ins, j_outs, j_sems = _job_parts(job)
    nji, njo = len(j_ins), len(j_outs)
    grid = (M // tm, N // tn, nk)

    def run_job(refs):
        step = (pl.program_id(0) * grid[1] + pl.program_id(1)) * grid[2] + pl.program_id(2)
        _job_steps(job, refs[2:2 + nji], refs[3 + nji:3 + nji + njo], refs[len(refs) - 2:], step,
                   grid[0] * grid[1] * grid[2] - 1)

    def body(*refs):
        a_ref, b_ref, o_ref, acc_ref = refs[0], refs[1], refs[2 + nji], refs[3 + nji + njo]
        run_job(refs)
        k = pl.program_id(2)

        @pl.when(k == 0)
        def _():
            acc_ref[...] = jnp.zeros_like(acc_ref)

        bv = b_ref[0] if b_lead else b_ref[...]
        acc_ref[...] += lax.dot_general(a_ref[...].astype(bf16), bv.astype(bf16), dims, preferred_element_type=f32)

        @pl.when(k == nk - 1)
        def _():
            if o_lead:
                o_ref[0] = acc_ref[...].astype(o_ref.dtype)
            else:
                o_ref[...] = acc_ref[...].astype(o_ref.dtype)

    def body_one_step(*refs):
        a_ref, b_ref, o_ref = refs[0], refs[1], refs[2 + nji]
        run_job(refs)
        bv = b_ref[0] if b_lead else b_ref[...]
        res = lax.dot_general(a_ref[...].astype(bf16), bv.astype(bf16), dims, preferred_element_type=f32)
        if o_lead:
            o_ref[0] = res.astype(o_ref.dtype)
        else:
            o_ref[...] = res.astype(o_ref.dtype)

    res = pl.pallas_call(
        body if nk > 1 else body_one_step, grid=grid,
        in_specs=[a_spec, b_spec] + [_ANY] * nji,
        out_specs=[o_spec] + [_ANY] * njo,
        out_shape=[jax.ShapeDtypeStruct(o_shape, out_dtype)] + j_outs,
        scratch_shapes=([pltpu.VMEM((tm, tn), f32)] if nk > 1 else []) + j_sems,
        compiler_params=pltpu.CompilerParams(
            dimension_semantics=("arbitrary",) * 3 if job else ("parallel", "parallel", "arbitrary")),
        name=name,
    )(a, b, *j_ins)
    return res if job else res[0]


def _make_bmm(precision):
    if precision is None:
        cast, kw = (lambda v: v.astype(bf16)), {}
    else:
        cast, kw = (lambda v: v), {"precision": precision}

    def nn(a, b):
        return jnp.einsum("hij,hjk->hik", cast(a), cast(b), preferred_element_type=f32, **kw)

    def nt(a, b):
        return jnp.einsum("hik,hjk->hij", cast(a), cast(b), preferred_element_type=f32, **kw)

    def tn(a, b):
        return jnp.einsum("hki,hkj->hij", cast(a), cast(b), preferred_element_type=f32, **kw)

    if precision is not None:
        return nn, nt, tn
    nn_v, nt_v, tn_v = jax.custom_vjp(nn), jax.custom_vjp(nt), jax.custom_vjp(tn)
    keep = lambda f: (lambda a, b: (f(a, b), (a, b)))
    nn_v.defvjp(keep(nn), lambda r, ct: (nt(ct, r[1]), tn(r[0], ct)))
    nt_v.defvjp(keep(nt), lambda r, ct: (nn(ct, r[1]), tn(ct, r[0])))
    tn_v.defvjp(keep(tn), lambda r, ct: (nt(r[1], ct), nn(r[0], ct)))
    return nn_v, nt_v, tn_v


_bmm, _bmm_nt, _bmm_tn = _make_bmm(None)
_bmm_exact = _make_bmm(lax.Precision.HIGH)[0]


def _masks(n):
    r = lax.broadcasted_iota(jnp.int32, (n, n), 0)
    c = lax.broadcasted_iota(jnp.int32, (n, n), 1)
    return (r >= c)[None], (r > c)[None], (r == c)[None]


_INV_BLOCK = 8


def _nilpotent_inverse(m, eye):
    p = eye + m
    for _ in range(2):
        m = _bmm(m, m)
        p = p + _bmm(p, m)
    return p


def _neumann_inverse_impl(m):
    n = m.shape[1]
    assert n == _INV_BLOCK * _INV_BLOCK
    r = lax.broadcasted_iota(jnp.int32, (n, n), 0)
    c = lax.broadcasted_iota(jnp.int32, (n, n), 1)
    eye = (r == c).astype(f32)[None]
    inside = jnp.where((r // _INV_BLOCK == c // _INV_BLOCK)[None], m, 0.0)
    d_inv = _nilpotent_inverse(inside, eye)
    return _bmm(_nilpotent_inverse(_bmm(d_inv, m - inside), eye), d_inv)


@jax.custom_vjp
def _neumann_inverse(m):
    return _neumann_inverse_impl(m)


def _neumann_inverse_fwd(m):
    p = _neumann_inverse_impl(m)
    return p, p


def _neumann_inverse_bwd(p, ct):
    return (_bmm_tn(p, _bmm_nt(ct, p)),)


_neumann_inverse.defvjp(_neumann_inverse_fwd, _neumann_inverse_bwd)


@jax.custom_vjp
def _given_inverse(m, p):
    return p


_given_inverse.defvjp(lambda m, p: (p, p), lambda p, ct: (_neumann_inverse_bwd(p, ct)[0], jnp.zeros_like(p)))


def _gdn_chunk(s, q, k, v, beta, gc, gr, gl, p=None):
    n = q.shape[1]
    causal, strict, _ = _masks(n)
    decay = jnp.where(causal, jnp.exp(jnp.where(causal, gc - gr, 0.0)), 0.0)
    kb = k * beta
    vb = v * beta
    lower = jnp.where(strict, _bmm_nt(kb, k) * decay, 0.0)
    t_mat = _neumann_inverse(-lower) if p is None else _given_inverse(-lower, p)
    egc = jnp.exp(gc)
    u = _bmm(t_mat, vb)
    w = _bmm(t_mat, kb * egc)
    attn = jnp.where(causal, _bmm_nt(q, k) * decay, 0.0)
    v_new = u - _bmm(w, s)
    o = _bmm(q * egc, s) + _bmm(attn, v_new)
    k_dec = k * jnp.exp(gl - gc)
    s_new = s * jnp.exp(gl) + _bmm_tn(k_dec, v_new)
    return s_new, o, t_mat


def _rwkv_chunk(s, r, lw, k, v, a, b, p=None):
    n = r.shape[1]
    causal, strict, _ = _masks(n)
    tri = jnp.broadcast_to(causal.astype(f32), (r.shape[0], n, n))
    lc = _bmm_exact(tri, lw)
    ein = jnp.exp(lc)
    eout = jnp.exp(-lc)
    a_t = a * jnp.exp(lc - lw)
    b_t = b * eout
    k_t = k * eout
    r_t = r * ein
    a_ab = jnp.where(strict, _bmm_nt(a_t, b_t), 0.0)
    a_ak = jnp.where(strict, _bmm_nt(a_t, k_t), 0.0)
    inv = _neumann_inverse(a_ab) if p is None else _given_inverse(a_ab, p)
    u = _bmm(inv, _bmm_nt(a_t, s) + _bmm(a_ak, v))
    y = (_bmm_nt(r_t, s) + _bmm(jnp.where(causal, _bmm_nt(r_t, b_t), 0.0), u)
         + _bmm(jnp.where(causal, _bmm_nt(r_t, k_t), 0.0), v))
    e_last = jnp.exp(jnp.sum(lw, axis=1, keepdims=True))
    s_new = s * e_last + _bmm_tn(u, b_t * e_last) + _bmm_tn(v, k_t * e_last)
    return s_new, y, inv


def _heads_in(ref, rows):
    return jnp.stack([ref[rows, h * HEAD_DIM:(h + 1) * HEAD_DIM] for h in range(HEADS)], axis=0)


def _heads_out(ref, rows, val):
    for h in range(HEADS):
        ref[rows, h * HEAD_DIM:(h + 1) * HEAD_DIM] = val[h]


def _gdn_scalars(bt, gt):
    n = bt.shape[0]
    gtt = gt.T
    hs = range(HEADS)
    return [jnp.stack([bt[:, h:h + 1] for h in hs], axis=0),
            jnp.stack([gt[:, HEADS + h:HEADS + h + 1] for h in hs], axis=0),
            jnp.stack([gtt[HEADS + h:HEADS + h + 1, :] for h in hs], axis=0),
            jnp.stack([gt[n - 1:n, HEADS + h:HEADS + h + 1] for h in hs], axis=0)]


def _gdn_scalars_back(dbeta, dgc, dgr, dgl):
    n = dbeta.shape[1]
    lane = lax.broadcasted_iota(jnp.int32, (n, LANES), 1)
    row = lax.broadcasted_iota(jnp.int32, (n, LANES), 0)
    sub = lax.broadcasted_iota(jnp.int32, (LANES, n), 0)
    db = jnp.zeros((n, LANES), f32)
    dg = jnp.zeros((n, LANES), f32)
    dgt = jnp.zeros((LANES, n), f32)
    for h in range(HEADS):
        db = jnp.where(lane == h, dbeta[h], db)
        dg = jnp.where(lane == HEADS + h, dgc[h] + jnp.where(row == n - 1, dgl[h], 0.0), dg)
        dgt = jnp.where(sub == HEADS + h, dgr[h], dgt)
    return [db, dg + dgt.T]


SCAN_GROUP = 2


def _scan_steps(t):
    g = SCAN_GROUP if (t // CHUNK) % SCAN_GROUP == 0 else 1
    return g, t // (CHUNK * g)


def _scan_spec(width, g, n, reverse):
    if reverse:
        return pl.BlockSpec((g * CHUNK, width), lambda i: (n - 1 - i, 0))
    return pl.BlockSpec((g * CHUNK, width), lambda i: (i, 0))


def _hist_spec(g, n, reverse):
    blk = (g, HEADS, HEAD_DIM, HEAD_DIM)
    if reverse:
        return pl.BlockSpec(blk, lambda i: (n - 1 - i, 0, 0, 0))
    return pl.BlockSpec(blk, lambda i: (i, 0, 0, 0))


def _job_parts(job):
    if job is None:
        return [], [], []
    sems = [pltpu.SemaphoreType.DMA((job["nsem"],)), pltpu.SemaphoreType.DMA((job["nsem"],))]
    return list(job["ins"]), list(job["out_shapes"]), sems


def _job_steps(job, in_refs, out_refs, sems, step, last):
    if job is None:
        return

    @pl.when(step == 0)
    def _():
        job["start"](in_refs, out_refs, *sems)

    @pl.when(step == last)
    def _():
        job["finish"](in_refs, out_refs, *sems)


def _scan_fwd(name, fn, rows_in, scal_in=(), job=None):
    t = rows_in[0].shape[0]
    grp, n = _scan_steps(t)
    nr, ns = len(rows_in), len(scal_in)
    j_ins, j_outs, j_sems = _job_parts(job)
    nji, njo = len(j_ins), len(j_outs)

    def body(*refs):
        o_ref, sh_ref, ph_ref = refs[nr + ns + nji:nr + ns + nji + 3]
        s_scr = refs[nr + ns + nji + 3 + njo]
        _job_steps(job, refs[nr + ns:nr + ns + nji], refs[nr + ns + nji + 3:nr + ns + nji + 3 + njo],
                   refs[nr + ns + nji + 3 + njo + 1:], pl.program_id(0), n - 1)

        @pl.when(pl.program_id(0) == 0)
        def _():
            s_scr[...] = jnp.zeros_like(s_scr)

        s = s_scr[...]
        for sub in range(grp):
            rows = slice(sub * CHUNK, (sub + 1) * CHUNK)
            sh_ref[sub] = s
            ins = [_heads_in(r, rows) for r in refs[:nr]]
            if ns:
                ins += _gdn_scalars(*[r[rows, :] for r in refs[nr:nr + ns]])
            s, o, p = fn(s, *ins)
            _heads_out(o_ref, rows, o)
            ph_ref[sub] = p
        s_scr[...] = s

    return pl.pallas_call(
        body, grid=(n,),
        in_specs=[_scan_spec(a.shape[1], grp, n, False) for a in (*rows_in, *scal_in)] + [_ANY] * nji,
        out_specs=[_scan_spec(WIDTH, grp, n, False), _hist_spec(grp, n, False), _hist_spec(grp, n, False)] + [_ANY] * njo,
        out_shape=[jax.ShapeDtypeStruct((t, WIDTH), f32)]
        + [jax.ShapeDtypeStruct((t // CHUNK, HEADS, HEAD_DIM, HEAD_DIM), f32)] * 2 + j_outs,
        scratch_shapes=[pltpu.VMEM((HEADS, HEAD_DIM, HEAD_DIM), f32)] + j_sems,
        compiler_params=pltpu.CompilerParams(dimension_semantics=("arbitrary",)),
        name=name,
    )(*rows_in, *scal_in, *j_ins)


def _scan_bwd(name, fn, rows_in, scal_in, s_hist, p_hist, d_out, job=None):
    t = rows_in[0].shape[0]
    grp, n = _scan_steps(t)
    nr, ns = len(rows_in), len(scal_in)
    j_ins, j_outs, j_sems = _job_parts(job)
    nji, njo = len(j_ins), len(j_outs)

    def body(*refs):
        sh_ref, ph_ref, do_ref = refs[nr + ns:nr + ns + 3]
        base = nr + ns + 3 + nji
        g_refs = refs[base:base + nr + ns]
        ds_scr = refs[base + nr + ns + njo]
        _job_steps(job, refs[nr + ns + 3:base], refs[base + nr + ns:base + nr + ns + njo],
                   refs[base + nr + ns + njo + 1:], pl.program_id(0), n - 1)

        @pl.when(pl.program_id(0) == 0)
        def _():
            ds_scr[...] = jnp.zeros_like(ds_scr)

        ds = ds_scr[...]
        for sub in reversed(range(grp)):
            rows = slice(sub * CHUNK, (sub + 1) * CHUNK)
            ins = [_heads_in(r, rows) for r in refs[:nr]]
            if ns:
                ins += _gdn_scalars(*[r[rows, :] for r in refs[nr:nr + ns]])
            p = ph_ref[sub]
            _, vjp = jax.vjp(lambda s, *a, p=p: fn(s, *a, p=p)[:2], sh_ref[sub], *ins)
            g = vjp((ds, _heads_in(do_ref, rows)))
            ds = g[0]
            for r, v in zip(g_refs[:nr], g[1:1 + nr]):
                _heads_out(r, rows, v)
            if ns:
                for r, v in zip(g_refs[nr:], _gdn_scalars_back(*g[1 + nr:])):
                    r[rows, :] = v
        ds_scr[...] = ds

    arrs = (*rows_in, *scal_in)
    return pl.pallas_call(
        body, grid=(n,),
        in_specs=[_scan_spec(a.shape[1], grp, n, True) for a in arrs]
        + [_hist_spec(grp, n, True), _hist_spec(grp, n, True), _scan_spec(WIDTH, grp, n, True)] + [_ANY] * nji,
        out_specs=[_scan_spec(a.shape[1], grp, n, True) for a in arrs] + [_ANY] * njo,
        out_shape=[jax.ShapeDtypeStruct(a.shape, f32) for a in arrs] + j_outs,
        scratch_shapes=[pltpu.VMEM((HEADS, HEAD_DIM, HEAD_DIM), f32)] + j_sems,
        compiler_params=pltpu.CompilerParams(dimension_semantics=("arbitrary",)),
        name=name,
    )(*arrs, s_hist, p_hist, d_out, *j_ins)


def _fn_norm_mod(ps, ts):
    nw, shift, scale = ps
    (x,) = ts
    return [_rms(x, nw, NORM_EPS) * (1.0 + scale) + shift]


def _fn_norm_mod_and_x(ps, ts):
    return _fn_norm_mod(ps, ts) + [ts[0]]


def _fn_resid_norm_mod(ps, ts):
    gate, nw, shift, scale = ps
    x, mo = ts
    x1 = x + gate * mo
    return [x1, _rms(x1, nw, NORM_EPS) * (1.0 + scale) + shift]


def _fn_gdn_pre(ps, ts):
    cw = ps[:12]
    alog, dtb, seg, tri = ps[12:]
    ba = ts[12]
    outs = []
    for part in range(3):
        x = ts[4 * part:4 * part + 4]
        w = cw[4 * part:4 * part + 4]
        conv = w[3] * x[0] + w[2] * x[1] + w[1] * x[2] + w[0] * x[3]
        u = _silu(conv)
        if part < 2:
            u = u * lax.rsqrt(_segsum(u * u, seg) + 1e-6)
            if part == 0:
                u = u * (HEAD_DIM ** -0.5)
        outs.append(u)
    beta = jax.nn.sigmoid(ba)
    g = -jnp.exp(alog) * _softplus(ba + dtb)
    gc = _xdot_l(tri, g)
    return outs + [beta, gc]


def _fn_gdn_post(ps, ts):
    ow, seg = ps
    o, z = ts
    ms = _segsum(o * o, seg) * (1.0 / HEAD_DIM)
    return [o * lax.rsqrt(ms + NORM_EPS) * ow * _silu(z)]


def _fn_rwkv_pre(ps, ts):
    mu_r, mu_k, mu_v, mu_l, mu_g, w0, w2p, a0, a2p, g2p, k_k, k_a, seg = ps
    r0, r1, k0, k1, v0, v1, l0, l1, g0, g1 = ts
    xr = r0 + (r1 - r0) * mu_r
    xk = k0 + (k1 - k0) * mu_k
    xv = v0 + (v1 - v0) * mu_v
    xl = l0 + (l1 - l0) * mu_l
    xg = g0 + (g1 - g0) * mu_g
    w = -_softplus(-(w0 + _bdot(jnp.tanh(xl), w2p))) - 0.5
    lw = -jnp.exp(w)
    a = jax.nn.sigmoid(a0 + _bdot(xl, a2p))
    g = _bdot(jax.nn.sigmoid(xg), g2p)
    kk = xk * k_k
    kk = kk * lax.rsqrt(_segsum(kk * kk, seg) + 1e-6)
    k2 = xk * (1.0 + (a - 1.0) * k_a)
    return [xr, lw, k2, xv, -kk, kk * a, g]


def _fn_rwkv_post(ps, ts):
    lw_, lb_, rk, seg = ps
    y, r, k2, v, g = ts
    inv = 1.0 / HEAD_DIM
    yc = y - _segsum(y, seg) * inv
    var = _segsum(yc * yc, seg) * inv
    yn = yc * lax.rsqrt(var + LNX_EPS) * lw_ + lb_
    bonus = _segsum(r * k2 * rk, seg) * v
    return [(yn + bonus) * g]


def _fn_merge(ps, ts):
    gla, glb, ya, yb = ts
    return [jax.nn.sigmoid(gla) * ya + jax.nn.sigmoid(glb) * yb]


def _fn_convglu(ps, ts):
    c0, c1, c2 = ps
    g0, g1, g2, up = ts
    return [_silu(c2 * g0 + c1 * g1 + c0 * g2) * up]


def _final_stage(x1, fo, tgt, gate2, nfw, tt):
    rows, d = x1.shape

    def loss_fn(gate, nw, xa, fa, tg):
        y = _rms(xa + gate * fa, nw, NORM_EPS)
        err = (y - tg) ** 2
        return 0.5 * jnp.sum(jnp.mean(err, axis=-1, keepdims=True), axis=0, keepdims=True)

    def body(x_ref, f_ref, t_ref, g_ref, w_ref, dx_ref, df_ref, dg_ref, dw_ref, l_ref):
        i = pl.program_id(0)
        args = (g_ref[...], w_ref[...], x_ref[...], f_ref[...])
        tg = t_ref[...]
        lv, vjp = jax.vjp(lambda g, w, xa, fa: loss_fn(g, w, xa, fa, tg), *args)
        dg, dw, dx, df = vjp(jnp.ones((1, 1), f32))
        dx_ref[...] = dx
        df_ref[...] = df.astype(df_ref.dtype)

        @pl.when(i == 0)
        def _():
            dg_ref[...] = jnp.zeros_like(dg_ref)
            dw_ref[...] = jnp.zeros_like(dw_ref)
            l_ref[...] = jnp.zeros_like(l_ref)

        dg_ref[...] += dg
        dw_ref[...] += dw
        l_ref[...] += jnp.broadcast_to(lv, l_ref.shape)

    row = pl.BlockSpec((tt, d), lambda i: (i, 0))
    vec = pl.BlockSpec((1, d), lambda i: (0, 0))
    return pl.pallas_call(
        body, grid=(rows // tt,),
        in_specs=[row, row, row, vec, vec],
        out_specs=[row, row, vec, vec, pl.BlockSpec((1, LANES), lambda i: (0, 0))],
        out_shape=[jax.ShapeDtypeStruct((rows, d), f32), jax.ShapeDtypeStruct((rows, d), bf16)]
        + [jax.ShapeDtypeStruct((1, d), f32)] * 2
        + [jax.ShapeDtypeStruct((1, LANES), f32)],
        compiler_params=pltpu.CompilerParams(dimension_semantics=("arbitrary",)),
        name="loss_head",
    )(x1, fo, tgt, gate2, nfw)


def _ada_fwd(c_all, w_shard, b_cols):
    def body(c_ref, w_ref, b_ref, cond_ref, mod_ref):
        cond = _silu(c_ref[...])
        cond_ref[...] = cond
        mod_ref[...] = jnp.dot(cond.astype(bf16), w_ref[...].astype(bf16), preferred_element_type=f32) + b_ref[...]

    n = w_shard.shape[1]
    return pl.pallas_call(
        body, out_shape=[jax.ShapeDtypeStruct(c_all.shape, f32), jax.ShapeDtypeStruct((c_all.shape[0], n), f32)],
        name="ada_fwd",
    )(c_all, w_shard, b_cols)


def _adamw(name, w, g, m, v):
    rows, width = w.shape
    tt = _pick(rows, 256, 8)
    c1 = 1.0 - ADAM_B1 ** ADAM_STEP
    c2 = 1.0 - ADAM_B2 ** ADAM_STEP

    def body(w_ref, g_ref, m_ref, v_ref, d_ref, mo_ref, vo_ref):
        gg = g_ref[...]
        mn = ADAM_B1 * m_ref[...] + (1.0 - ADAM_B1) * gg
        vn = ADAM_B2 * v_ref[...] + (1.0 - ADAM_B2) * (gg * gg)
        m_hat = mn / c1
        v_hat = vn / c2
        d_ref[...] = -ADAM_LR * (m_hat / (jnp.sqrt(v_hat) + ADAM_EPS) + ADAM_WD * w_ref[...])
        mo_ref[...] = mn
        vo_ref[...] = vn

    spec = pl.BlockSpec((tt, width), lambda i: (i, 0))
    return pl.pallas_call(
        body, grid=(rows // tt,), in_specs=[spec] * 4, out_specs=[spec] * 3,
        out_shape=[jax.ShapeDtypeStruct((rows, width), f32)] * 3,
        compiler_params=pltpu.CompilerParams(dimension_semantics=("parallel",)),
        name=name,
    )(w, g, m, v)


def _place():
    return lax.axis_index("x"), lax.axis_index("y"), lax.axis_index("c")


def _ag8(name, blk):
    m, w = blk.shape

    def body(x_ref, out_ref, send_sems, recv_sems, local_sem):
        x, y, c = _place()
        me, sibling = (x, y, c), (x, y, 1 - c)
        chips = _other_chips(x, y)

        def slot(px, py, pc):
            return out_ref.at[4 * px + 2 * py + pc]

        def copy(k, block, to, src=None):
            return pltpu.make_async_remote_copy(src_ref=slot(*block) if src is None else src, dst_ref=slot(*block),
                                                send_sem=send_sems.at[k], recv_sem=recv_sems.at[k], device_id=to,
                                                device_id_type=MESH_ID)

        mine = pltpu.make_async_copy(x_ref, slot(*me), local_sem)
        mine.start()
        first = [copy(0, me, sibling, src=x_ref)] + [copy(1 + j, me, (*chip, c), src=x_ref) for j, chip in enumerate(chips)]
        for cp in first:
            cp.start()
        passed = [copy(4 + j, (*chip, c), sibling) for j, chip in enumerate(chips)]
        for j, chip in enumerate(chips):
            copy(1 + j, (*chip, c), me).wait_recv()
            passed[j].start()
        copy(0, sibling, me).wait_recv()
        for j, chip in enumerate(chips):
            copy(4 + j, (*chip, 1 - c), me).wait_recv()
        for cp in first + passed:
            cp.wait_send()
        mine.wait()

    return pl.pallas_call(
        body, out_shape=jax.ShapeDtypeStruct((8, m, w), blk.dtype),
        in_specs=[pl.BlockSpec(memory_space=pltpu.VMEM)], out_specs=pl.BlockSpec(memory_space=pltpu.VMEM),
        scratch_shapes=[pltpu.SemaphoreType.DMA((7,)), pltpu.SemaphoreType.DMA((7,)), pltpu.SemaphoreType.DMA],
        name=name,
    )(blk)


def _other_chips(x, y):
    return [(1 - x, y), (x, 1 - y), (1 - x, 1 - y)]


_ANY = pl.BlockSpec(memory_space=pl.ANY)


def _rcopy(src, dst, send_sems, recv_sems, k, dev):
    return pltpu.make_async_remote_copy(src_ref=src, dst_ref=dst, send_sem=send_sems.at[k], recv_sem=recv_sems.at[k],
                                        device_id=dev, device_id_type=MESH_ID)


def _run_job(name, job):
    j_ins, j_outs, j_sems = _job_parts(job)
    n = len(j_ins)

    def body(*refs):
        job["start"](refs[:n], refs[n:n + len(j_outs)], *refs[n + len(j_outs):])
        job["finish"](refs[:n], refs[n:n + len(j_outs)], *refs[n + len(j_outs):])

    return pl.pallas_call(body, out_shape=j_outs, in_specs=[_ANY] * n, out_specs=[_ANY] * len(j_outs),
                          scratch_shapes=j_sems, name=name)(*j_ins)


def _ag4_job(ws):
    n = len(ws)

    def plan(w_refs, out_refs, send_sems, recv_sems):
        x, y, c = _place()
        chip = 2 * x + y
        sibling = (x, y, 1 - c)
        chips = _other_chips(x, y)
        mine = [pl.ds(c * (w.shape[0] // 2), w.shape[0] // 2) for w in ws]
        other = [pl.ds((1 - c) * (w.shape[0] // 2), w.shape[0] // 2) for w in ws]
        rc = lambda src, dst, k, dev: _rcopy(src, dst, send_sems, recv_sems, k, dev)
        first = [rc(w_refs[t].at[mine[t]], out_refs[t].at[chip, mine[t]], 7 * t + k, (px, py, c))
                 for t in range(n) for k, (px, py) in enumerate(chips)]
        own = [rc(w_refs[t], out_refs[t].at[chip], 7 * t + 6, sibling) for t in range(n)]
        landed = [rc(out_refs[t].at[2 * px + py, mine[t]], out_refs[t].at[2 * px + py, mine[t]], 7 * t + k, (px, py, c))
                  for t in range(n) for k, (px, py) in enumerate(chips)]
        forward = [rc(out_refs[t].at[2 * px + py, mine[t]], out_refs[t].at[2 * px + py, mine[t]], 7 * t + 3 + k, sibling)
                   for t in range(n) for k, (px, py) in enumerate(chips)]
        handed = [rc(out_refs[t].at[2 * px + py, other[t]], out_refs[t].at[2 * px + py, other[t]], 7 * t + 3 + k, sibling)
                  for t in range(n) for k, (px, py) in enumerate(chips)]
        return first, own, landed, forward, handed

    def start(*refs):
        first, own, _, _, _ = plan(*refs)
        for cp in first + own:
            cp.start()

    def finish(*refs):
        first, own, landed, forward, handed = plan(*refs)
        for arrived, fw in zip(landed, forward):
            arrived.wait_recv()
            fw.start()
        for cp in handed + own:
            cp.wait_recv()
        for cp in first + own + forward:
            cp.wait_send()

    return dict(ins=ws, out_shapes=[jax.ShapeDtypeStruct((4,) + w.shape, w.dtype) for w in ws], nsem=7 * n,
                start=start, finish=finish)


def _sibling_swap_job(gs):
    n = len(gs)

    def plan(g_refs, recv_refs, send_sems, recv_sems):
        x, y, c = _place()
        return [_rcopy(g_refs[t].at[s_, pl.ds((1 - c) * (gs[t].shape[1] // 2), gs[t].shape[1] // 2)], recv_refs[t].at[s_],
                       send_sems, recv_sems, 4 * t + s_, (x, y, 1 - c)) for t in range(n) for s_ in range(4)]

    def start(*refs):
        for cp in plan(*refs):
            cp.start()

    def finish(*refs):
        copies = plan(*refs)
        for cp in copies:
            cp.wait_recv()
        for cp in copies:
            cp.wait_send()

    return dict(ins=gs, out_shapes=[jax.ShapeDtypeStruct((4, g.shape[1] // 2, g.shape[2]), g.dtype) for g in gs],
                nsem=4 * n, start=start, finish=finish)


def _chip_exchange_job(ps):
    n = len(ps)

    def plan(p_refs, recv_refs, send_sems, recv_sems):
        x, y, c = _place()
        return [_rcopy(p_refs[t].at[2 * px + py], recv_refs[t].at[k], send_sems, recv_sems, 3 * t + k, (px, py, c))
                for t in range(n) for k, (px, py) in enumerate(_other_chips(x, y))]

    def start(*refs):
        for cp in plan(*refs):
            cp.start()

    def finish(*refs):
        copies = plan(*refs)
        for cp in copies:
            cp.wait_recv()
        for cp in copies:
            cp.wait_send()

    return dict(ins=ps, out_shapes=[jax.ShapeDtypeStruct((3,) + p.shape[1:], p.dtype) for p in ps], nsem=3 * n,
                start=start, finish=finish)


_JOIN_PIECES = 4


def _rs_sibling_join(qs):
    n = len(qs)
    npc = _JOIN_PIECES

    def body(*refs):
        q_refs, out_refs = refs[:n], refs[n:2 * n]
        send_sems, recv_sems = refs[2 * n:]
        x, y, c = _place()
        copies = []
        for t in range(n):
            rh = qs[t].shape[0] // 2
            pr = rh // npc
            for i in range(npc):
                rows = pl.ds(c * rh + i * pr, pr)
                cp = _rcopy(q_refs[t].at[rows], out_refs[t].at[rows], send_sems, recv_sems, npc * t + i,
                            (x, y, 1 - c))
                cp.start()
                copies.append(cp)
        for t in range(n):
            rh = qs[t].shape[0] // 2
            pr = rh // npc
            for i in range(npc):
                rows = pl.ds((1 - c) * rh + i * pr, pr)
                _rcopy(q_refs[t].at[rows], out_refs[t].at[rows], send_sems, recv_sems, npc * t + i,
                       (x, y, 1 - c)).wait_recv()
        for cp in copies:
            cp.wait_send()

    return pl.pallas_call(
        body, out_shape=[jax.ShapeDtypeStruct(q.shape, q.dtype) for q in qs],
        in_specs=[_ANY] * n, out_specs=[_ANY] * n, input_output_aliases={t: t for t in range(n)},
        scratch_shapes=[pltpu.SemaphoreType.DMA((npc * n,)), pltpu.SemaphoreType.DMA((npc * n,))],
        name="grads_sibling_join",
    )(*qs)


def _add_half(name, g, recv, chip, ci):
    S, r, w = g.shape
    rh = r // 2
    tt = _pick(rh, 256, 16)
    nb = rh // tt

    def body(chip_ref, core_ref, a_ref, b_ref, own_ref, ob_ref):
        v = a_ref[...] + b_ref[...]
        ob_ref[...] = v.astype(bf16)

        @pl.when(pl.program_id(1) == chip_ref[0])
        def _():
            own_ref[...] = v[0]

    grid_spec = pltpu.PrefetchScalarGridSpec(
        num_scalar_prefetch=2, grid=(nb, S),
        in_specs=[pl.BlockSpec((1, tt, w), lambda i, s_, ch, co: (s_, co[0] * nb + i, 0)),
                  pl.BlockSpec((1, tt, w), lambda i, s_, ch, co: (s_, i, 0))],
        out_specs=[pl.BlockSpec((tt, w), lambda i, s_, ch, co: (i, 0)),
                   pl.BlockSpec((1, tt, w), lambda i, s_, ch, co: (s_, i, 0))])
    return pl.pallas_call(body, grid_spec=grid_spec,
                          out_shape=[jax.ShapeDtypeStruct((rh, w), f32), jax.ShapeDtypeStruct((S, rh, w), bf16)],
                          compiler_params=pltpu.CompilerParams(dimension_semantics=("arbitrary", "arbitrary")),
                          name=name)(chip.reshape(1).astype(jnp.int32), ci.reshape(1).astype(jnp.int32), g, recv)


def _sum_chip(name, own, others, ci):
    rh, w = own.shape
    tt = _pick(rh, 128, 16)
    nb = rh // tt

    def body(core_ref, a_ref, b_ref, o_ref):
        o_ref[...] = ((a_ref[...] + b_ref[0].astype(f32)) + b_ref[1].astype(f32)) + b_ref[2].astype(f32)

    grid_spec = pltpu.PrefetchScalarGridSpec(
        num_scalar_prefetch=1, grid=(nb,),
        in_specs=[pl.BlockSpec((tt, w), lambda i, co: (i, 0)),
                  pl.BlockSpec((3, tt, w), lambda i, co: (0, i, 0))],
        out_specs=pl.BlockSpec((tt, w), lambda i, co: (co[0] * nb + i, 0)))
    return pl.pallas_call(body, grid_spec=grid_spec, out_shape=jax.ShapeDtypeStruct((2 * rh, w), f32),
                          name=name)(ci.reshape(1).astype(jnp.int32), own, others)


def _sum_devices(gathered, head_row):
    _, rows, width = gathered.shape

    def body(g_ref, out_ref, head_ref):
        acc = g_ref[0]
        for d in range(1, 8):
            acc = acc + g_ref[d]
        out_ref[...] = acc
        row = acc[head_row:head_row + 1, :]
        hs = row[:, 0:HEAD_DIM]
        for h in range(1, HEADS):
            hs = hs + row[:, h * HEAD_DIM:(h + 1) * HEAD_DIM]
        head_ref[...] = jnp.zeros_like(head_ref)
        head_ref[0:1, 0:HEAD_DIM] = hs

    return pl.pallas_call(
        body, out_shape=[jax.ShapeDtypeStruct((rows, width), f32), jax.ShapeDtypeStruct((8, LANES), f32)],
        name="small_grads_sum",
    )(gathered)


def _pack(arrs, rows_mult, dtype):
    flat = jnp.concatenate([a.reshape(-1).astype(dtype) for a in arrs])
    per = PACK_W * rows_mult
    total = -(-flat.shape[0] // per) * per
    return jnp.pad(flat, (0, total - flat.shape[0])).reshape(total // PACK_W, PACK_W)


def _unpack(buf, shapes):
    flat = buf.reshape(-1)
    out, off = [], 0
    for s in shapes:
        n = int(np.prod(s))
        out.append(flat[off:off + n].reshape(s))
        off += n
    return out


_BIG = ["w_in", "w_branch_gdn", "w_branch_rwkv", "w_out", "w_ffn_in", "w_ffn_out"]
_MID = ["conv_gdn", "conv_ffn", "w2", "a2", "g2"]
_SMALL =["b_ada", "norm1_w", "a_log", "dt_bias", "onorm_gdn", "mu_rwkv", "w0", "a0", "k_k", "k_a", "r_k", "lnx_w",
          "lnx_b", "norm2_w", "norm_f_w"]
_ORDER = ["w_ada", "b_ada", "norm1_w", "w_in", "conv_gdn", "a_log", "dt_bias", "onorm_gdn", "w_branch_gdn", "mu_rwkv",
          "w0", "w2", "a0", "a2", "g2", "k_k", "k_a", "r_k", "lnx_w", "lnx_b", "w_branch_rwkv", "w_out", "norm2_w",
          "w_ffn_in", "conv_ffn", "w_ffn_out", "norm_f_w"]


_WIN_SEGMENTS = [(0, 1536, 0), (2064, 3600, 1536), (1536, 2048, 3072), (3600, 3728, 3584), (2048, 2064, 3712),
                 (3728, 3888, 3840), (3888, 5936, 4096)]
_WIN_PADDED = 6144
_COL_QKV, _COL_RKV, _COL_Z = (0, 1, 2), (3, 4, 5), 6
_COL_LORA, _COL_BA = 28, 29
_COL_GATE_LORA = 15
_COL_GL = (4, 5)
_JOINT_QKV, _JOINT_RKV, _JOINT_Z, _JOINT_GL = 0, 1, 6, 2
_SMALL_BLOCKS_AT = 3584


def _win_pad(shards):
    n = shards.shape[2]
    parts, at = [], 0
    for lo, hi, dst in _WIN_SEGMENTS:
        if dst > at:
            parts.append(jnp.zeros((shards.shape[1], dst - at), shards.dtype))
        c = lo
        while c < hi:
            j = c // n
            e = min(hi, (j + 1) * n)
            parts.append(shards[j][:, c - j * n:e - j * n])
            c = e
        at = dst + hi - lo
    if at < _WIN_PADDED:
        parts.append(jnp.zeros((shards.shape[1], _WIN_PADDED - at), shards.dtype))
    return jnp.concatenate(parts, axis=1)


def _win_unpad_shards(g, n):
    shards = []
    for j in range(4):
        parts = []
        for lo, hi, dst in sorted(_WIN_SEGMENTS):
            a, b = max(lo, j * n), min(hi, (j + 1) * n)
            if a < b:
                parts.append(g[:, dst + a - lo:dst + b - lo])
        shards.append(jnp.concatenate(parts, axis=1))
    return jnp.stack(shards)


def kernel(x, c, w_ada, b_ada, norm1_w, w_in, conv_gdn, a_log, dt_bias, onorm_gdn, w_branch_gdn, mu_rwkv, w0, w2, a0, a2, g2, k_k, k_a, r_k, lnx_w, lnx_b, w_branch_rwkv, w_out, norm2_w, w_ffn_in, conv_ffn, w_ffn_out, norm_f_w, loss_target, m_w_ada, m_b_ada, m_norm1_w, m_w_in, m_conv_gdn, m_a_log, m_dt_bias, m_onorm_gdn, m_w_branch_gdn, m_mu_rwkv, m_w0, m_w2, m_a0, m_a2, m_g2, m_k_k, m_k_a, m_r_k, m_lnx_w, m_lnx_b, m_w_branch_rwkv, m_w_out, m_norm2_w, m_w_ffn_in, m_conv_ffn, m_w_ffn_out, m_norm_f_w, v_w_ada, v_b_ada, v_norm1_w, v_w_in, v_conv_gdn, v_a_log, v_dt_bias, v_onorm_gdn, v_w_branch_gdn, v_mu_rwkv, v_w0, v_w2, v_a0, v_a2, v_g2, v_k_k, v_k_a, v_r_k, v_lnx_w, v_lnx_b, v_w_branch_rwkv, v_w_out, v_norm2_w, v_w_ffn_in, v_conv_ffn, v_w_ffn_out, v_norm_f_w):
    args = dict(locals())
    W = {n: args[n] for n in _ORDER}
    Mo = {n: args["m_" + n] for n in _ORDER}
    Vo = {n: args["v_" + n] for n in _ORDER}
    shapes = {n: W[n].shape for n in _ORDER}
    sq = lambda a: a.reshape(a.shape[-2:]) if a.ndim == 3 else a.reshape(1, -1)
    row = lambda a: a.reshape(1, -1)

    xi, yi, ci = lax.axis_index("x"), lax.axis_index("y"), lax.axis_index("c")
    dev = 4 * xi + 2 * yi + ci
    chip = 2 * xi + yi

    x2 = x[0]
    tgt = loss_target[0]
    T, D = x2.shape
    N = T // CHUNK
    tt_l = _pick(T, 512, CHUNK)
    tt_h = _pick(T, 128, CHUNK)

    mid_shapes = [shapes[n][1:] for n in _MID]
    small_blk = _pack([c] + [W[n] for n in _MID], 8, f32)
    small_all = _ag8("gather_c_mid", small_blk)
    c_all = small_all[:, 0, :]
    per_chip = small_all[0::2].reshape(4, -1)[:, D:]
    mid = [dict(zip(_MID, _unpack(per_chip[j], mid_shapes))) for j in range(4)]
    catm = lambda n: jnp.concatenate([mid[j][n] for j in range(4)], axis=1)
    conv_gdn_f, conv_ffn_f = catm("conv_gdn"), catm("conv_ffn")
    w2f, a2f, g2f = catm("w2"), catm("a2"), catm("g2")

    (win_s,) = _run_job("w_in_all_gather", _ag4_job([sq(W["w_in"]).astype(bf16)]))
    later_weights = _ag4_job([sq(W[n]).astype(bf16) for n in _BIG[1:]])
    win_p = _win_pad(win_s)
    zpad = lambda a, top, bot: jnp.pad(a, ((top, bot), (0, 0)))
    w2p, a2p, g2p = zpad(w2f, 0, 64), zpad(a2f, 64, 0), zpad(g2f, 0, 96)

    ncol = shapes["w_ada"][2]
    b_cols = lax.dynamic_slice(sq(W["b_ada"]), (0, chip * ncol), (1, ncol))
    cond16, mod_cols = _ada_fwd(jnp.pad(c_all, ((0, 8), (0, 0))), sq(W["w_ada"]), b_cols)
    mod_all = _ag8("gather_mod", mod_cols[:8])
    mod_mine = lax.dynamic_slice(mod_all[0::2], (0, dev, 0), (4, 1, ncol)).reshape(1, 4 * ncol)
    shift1, scale1, gate1, shift2, scale2, gate2 = [mod_mine[:, i * D:(i + 1) * D] for i in range(6)]

    seg = _seg_matrix(WIDTH, HEAD_DIM)
    norm1 = [sq(W["norm1_w"]), shift1, scale1]
    h1 = _stage_fwd("norm_mod1", _fn_norm_mod, [_whole(x2)], norm1, [(D, bf16)], tt_l)[0]
    p = _matmul("in_proj", h1, win_p, "nn")

    cgq = [row(conv_gdn_f[j, part * WIDTH:(part + 1) * WIDTH]) for part in range(3) for j in range(4)]
    lane_pad = lambda a: jnp.pad(row(a), ((0, 0), (8, LANES - 16)))
    gdn_pre_ps = cgq + [lane_pad(W["a_log"]), lane_pad(W["dt_bias"]), seg, _chunk_tri(tt_h, CHUNK)]
    gdn_pre_ts = [(p, WIDTH, ci_, (0, 1, 2, 3)) for ci_ in _COL_QKV] + [(p, LANES, _COL_BA, None)]
    q_, k_, v_, beta_t, gc_t = _stage_fwd("gdn_pre", _fn_gdn_pre, gdn_pre_ts, gdn_pre_ps,
                                          [(WIDTH, f32)] * 3 + [(LANES, f32)] * 2, tt_h)
    o_, gdn_hist, gdn_inv, wbg_s, wbr_s, wout_s, wfi_s, wfo_s = _scan_fwd(
        "gdn_scan", _gdn_chunk, [q_, k_, v_], [beta_t, gc_t], job=later_weights)
    wout_f = wout_s.reshape(D, D)
    wfo = wfo_s.reshape(D_FF, D)
    ow512 = jnp.tile(row(W["onorm_gdn"]), (1, HEADS))
    gdn_post_ts = [_whole(o_), (p, WIDTH, _COL_Z, None)]
    ya = _stage_fwd("gdn_post", _fn_gdn_post, gdn_post_ts, [ow512, seg], [(WIDTH, bf16)], tt_l)[0]

    mu = sq(W["mu_rwkv"])
    rw_ps = [mu[:, 0:512], mu[:, 512:1024], mu[:, 1024:1536], mu[:, 1536:1664], jnp.pad(mu[:, 1664:1824], ((0, 0), (0, 96))),
             sq(W["w0"]), w2p, sq(W["a0"]), a2p, g2p, sq(W["k_k"]), sq(W["k_a"]), seg]
    rw_ts = [(p, WIDTH, ci_, (0, 1)) for ci_ in _COL_RKV] + [(p, LANES, _COL_LORA, (0, 1)),
                                                              (p, 256, _COL_GATE_LORA, (0, 1))]
    rw_out = _stage_fwd("rwkv_pre", _fn_rwkv_pre, rw_ts, rw_ps, [(WIDTH, f32)] * 7, tt_h)
    r_, lw_, k2_, vv_, na_, b_, g_ = rw_out
    rw_ins = [r_, lw_, k2_, vv_, na_, b_]
    y_, rw_hist, rw_inv = _scan_fwd("rwkv_scan", _rwkv_chunk, rw_ins)
    rwp_ps = [sq(W["lnx_w"]), sq(W["lnx_b"]), row(W["r_k"]), seg]
    rwp_ts = [_whole(y_), _whole(r_), _whole(k2_), _whole(vv_), _whole(g_)]
    yb = _stage_fwd("rwkv_post", _fn_rwkv_post, rwp_ts, rwp_ps, [(WIDTH, bf16)], tt_l)[0]

    big_a = _matmul("branch_gdn", ya, wbg_s, "nn", shards=4)
    big_b = _matmul("branch_rwkv", yb, wbr_s, "nn", shards=4)
    merge_ts = [(p, D, _COL_GL[0], None), (p, D, _COL_GL[1], None), _whole(big_a), _whole(big_b)]
    merged = _stage_fwd("merge", _fn_merge, merge_ts, [], [(D, bf16)], tt_l)[0]
    mo = _matmul("out_proj", merged, wout_f, "nn")
    norm2 = [gate1, sq(W["norm2_w"]), shift2, scale2]
    x1, h2 = _stage_fwd("resid_norm_mod2", _fn_resid_norm_mod, [_whole(x2), _whole(mo)], norm2, [(D, f32), (D, bf16)], tt_l)
    f = _matmul("ffn_in", h2, wfi_s, "nn", shards=4, tn=1408)
    cg_ps = [row(conv_ffn_f[j]) for j in range(3)]
    cg_ts = [(f, D_FF, 0, (0, 1, 2)), (f, D_FF, 1, None)]
    act = _stage_fwd("convglu", _fn_convglu, cg_ts, cg_ps, [(D_FF, bf16)], tt_h)[0]
    fo = _matmul("ffn_out", act, wfo, "nn", tk=1408)

    dx1_a, dfo, dgate2, dnormf, loss_part = _final_stage(x1, fo, tgt, gate2, row(W["norm_f_w"]), tt_l)

    dact = _matmul("d_act", dfo, wfo, "nt", tn=1408)
    g_wfo = _matmul("g_ffn_out", act, dfo, "tn", tm=1408)
    _, dcf, df = _stage_bwd("convglu_bwd", _fn_convglu, cg_ts, cg_ps, [[_whole(dact)]], tt_h, [True] * 2, [True] * 3,
                            joint=([0, 1], 2 * D_FF, 0, None), dtypes=[bf16])
    g_wfi = _matmul("g_ffn_in", h2, df, "tn", shards=4, tn=1408)
    g_wfo_s = g_wfo.reshape(4, D_FF // 4, D)
    dh2, recv_wfi, recv_wfo = _matmul("d_h2", df, wfi_s, "nt", shards=4, tk=1408, job=_sibling_swap_job([g_wfi, g_wfo_s]))
    (dx_a, dmo), (dgate1, dnorm2, dshift2, dscale2), _ = _stage_bwd(
        "resid_norm_mod2_bwd", _fn_resid_norm_mod, [_whole(x2), _whole(mo)], norm2,
        [[_whole(dx1_a)], [_whole(dh2)]], tt_l, [True, True], [True] * 4, dtypes=[f32, bf16])
    dmerged = _matmul("d_merged", dmo, wout_f, "nt")
    g_wout = _matmul("g_out_proj", merged, dmo, "tn")
    (dbig_a, dbig_b), _, dp = _stage_bwd("merge_bwd", _fn_merge, merge_ts, [], [[_whole(dmerged)]], tt_l, [True] * 4, [],
                                         joint=([0, 1], p.shape[1], _JOINT_GL, None), dtypes=[bf16] * 3)
    g_wbg = _matmul("g_branch_gdn", ya, dbig_a, "tn", shards=4)
    g_wbr = _matmul("g_branch_rwkv", yb, dbig_b, "tn", shards=4)
    g_wout_s = g_wout.reshape(4, D // 4, D)
    dyb = _matmul("d_yb", dbig_b, wbr_s, "nt", shards=4)
    dya, *recv_mix = _matmul("d_ya", dbig_a, wbg_s, "nt", shards=4, job=_sibling_swap_job([g_wbg, g_wbr, g_wout_s]))

    (dy_, dr_p, dk2_p, dv_p, dg_p), (dlnxw, dlnxb, drk), _ = _stage_bwd(
        "rwkv_post_bwd", _fn_rwkv_post, rwp_ts, rwp_ps, [[_whole(dyb)]], tt_l, [True] * 5, [True, True, True, False])
    gs_a = [g_wbg, g_wbr, g_wout_s, g_wfi, g_wfo_s]
    pairs_a = [_add_half("grads_pair_sum%d" % (t + 1), g, r_, chip, ci)
               for t, (g, r_) in enumerate(zip(gs_a, recv_mix + [recv_wfi, recv_wfo]))]
    dr_c, dlw_c, dk2_c, dv_c, dna_c, db_c, *others_a = _scan_bwd(
        "rwkv_scan_bwd", _rwkv_chunk, rw_ins, [], rw_hist, rw_inv, dy_, job=_chip_exchange_job([pb for _, pb in pairs_a]))
    rw_cots = [[_whole(dr_p), _whole(dr_c)], [_whole(dlw_c)], [_whole(dk2_p), _whole(dk2_c)],
               [_whole(dv_p), _whole(dv_c)], [_whole(dna_c)], [_whole(db_c)], [_whole(dg_p)]]
    (dl_, dg_), rw_dp, dp = _stage_bwd("rwkv_pre_bwd", _fn_rwkv_pre, rw_ts, rw_ps, rw_cots, tt_h, [True] * 5,
                                       [True] * 12 + [False], joint=([0, 1, 2], p.shape[1], _JOINT_RKV, dp), dtypes=[bf16] * 3)
    dmu_r, dmu_k, dmu_v, dmu_l, dmu_g, dw0, dw2p, da0, da2p, dg2p, dkk, dka = rw_dp

    (do_,), (dow512,), dp = _stage_bwd("gdn_post_bwd", _fn_gdn_post, gdn_post_ts, [ow512, seg], [[_whole(dya)]], tt_l,
                                       [True, True], [True, False], joint=([1], p.shape[1], _JOINT_Z, dp), dtypes=[f32, bf16])
    d_gdn = _scan_bwd("gdn_scan_bwd", _gdn_chunk, [q_, k_, v_], [beta_t, gc_t], gdn_hist, gdn_inv, do_)
    gdn_cots = [[_whole(a)] for a in d_gdn]
    (dba,), gdn_dp, dp = _stage_bwd("gdn_pre_bwd", _fn_gdn_pre, gdn_pre_ts, gdn_pre_ps, gdn_cots, tt_h, [True] * 4,
                                    [True] * 14 + [False, False], joint=([0, 1, 2], p.shape[1], _JOINT_QKV, dp), dtypes=[bf16] * 2)
    dp = lax.dynamic_update_slice(dp, jnp.concatenate([dl_, dba, dg_], axis=1), (0, _SMALL_BLOCKS_AT))
    g_win_s = _win_unpad_shards(_matmul("g_in_proj", h1, dp, "tn"), shapes["w_in"][2])
    pair_win = _add_half("grads_pair_sum0", g_win_s, _run_job("w_in_grads_sibling_swap", _sibling_swap_job([g_win_s]))[0],
                         chip, ci)
    dh1, others_win = _matmul("d_h1", dp, win_p, "nt", job=_chip_exchange_job([pair_win[1]]))
    (grad_x,), (dnorm1, dshift1, dscale1), _ = _stage_bwd("norm_mod1_bwd", _fn_norm_mod_and_x, [_whole(x2)], norm1,
                                                          [[_whole(dh1)], [_whole(dx_a)]], tt_l, [True], [True] * 3)

    dmod = jnp.concatenate([dshift1, dscale1, dgate1, dshift2, dscale2, dgate2], axis=1)
    g_conv_gdn = jnp.concatenate([jnp.concatenate([gdn_dp[4 * part + j] for part in range(3)], axis=1) for j in range(4)], axis=0)
    g_conv_ffn = jnp.concatenate(dcf, axis=0)
    g_mu = jnp.concatenate([dmu_r, dmu_k, dmu_v, dmu_l, dmu_g[:, :160]], axis=1)
    small_parts = {"b_ada": dmod, "norm1_w": dnorm1, "a_log": gdn_dp[12][:, 8:16], "dt_bias": gdn_dp[13][:, 8:16],
                   "mu_rwkv": g_mu, "w0": dw0, "a0": da0, "k_k": dkk, "k_a": dka, "r_k": drk, "lnx_w": dlnxw,
                   "lnx_b": dlnxb, "norm2_w": dnorm2, "norm_f_w": dnormf}
    small_names = [n for n in _SMALL if n != "onorm_gdn"]
    mid_full = [g_conv_gdn, g_conv_ffn, dw2p[0:64], da2p[64:128], dg2p[0:160]]
    body_rows = _pack([small_parts[n] for n in small_names] + [loss_part[:, 0:1]] + mid_full, 1, f32)
    head_row = body_rows.shape[0]
    small_g = jnp.concatenate([body_rows, jnp.pad(dow512, ((0, 0), (0, PACK_W - WIDTH)))], axis=0)
    small_g = jnp.pad(small_g, ((0, -small_g.shape[0] % 8), (0, 0)))
    small_all_g = _ag8("gather_small_grads", small_g)
    small_sum, head_sum = _sum_devices(small_all_g, head_row)
    small_shapes = [shapes[n][1:] if n != "norm_f_w" else shapes[n] for n in small_names]
    un = _unpack(small_sum, small_shapes + [(1,)] + [g.shape for g in mid_full])
    small_grads = dict(zip(small_names, un))
    loss = un[len(small_names)].reshape(())
    small_grads["onorm_gdn"] = head_sum[0, 0:HEAD_DIM]
    for n, g in zip(_MID, un[len(small_names) + 1:]):
        wcols = shapes[n][2]
        small_grads[n] = lax.dynamic_slice(g, (0, chip * wcols), (g.shape[0], wcols))

    dmod_all = small_all_g[:, 0:6, :].reshape(8, 6 * PACK_W)
    dmod_cols = lax.dynamic_slice(dmod_all, (0, chip * ncol), (8, ncol))
    g_wada = _matmul("g_w_ada", cond16, jnp.pad(dmod_cols, ((0, 8), (0, 0))), "tn")

    pairs = [pair_win] + pairs_a
    others = [others_win] + others_a
    halves = [_sum_chip("grads_chip_sum%d" % t, own, o_, ci) for t, ((own, _), o_) in enumerate(zip(pairs, others))]
    big_grads = dict(zip(_BIG, _rs_sibling_join(halves)))

    res = {tag: {} for tag in ("grad", "delta", "new_m", "new_v")}

    def put(n, g, d, m_, v_):
        for tag, val in zip(("grad", "delta", "new_m", "new_v"), (g, d, m_, v_)):
            res[tag][n] = val.reshape(shapes[n])

    for n in _BIG:
        put(n, big_grads[n], *_adamw("adamw_" + n, sq(W[n]), big_grads[n], sq(Mo[n]), sq(Vo[n])))
    put("w_ada", g_wada, *_adamw("adamw_w_ada", sq(W["w_ada"]), g_wada, sq(Mo["w_ada"]), sq(Vo["w_ada"])))
    rest = _SMALL + _MID
    pk = lambda d: _pack([d[n] for n in rest], 8, f32)
    sg = pk(small_grads)
    sm = _adamw("adamw_small", pk(W), sg, pk(Mo), pk(Vo))
    for tag, buf in zip(("grad", "delta", "new_m", "new_v"), (sg,) + tuple(sm)):
        res[tag].update(zip(rest, _unpack(buf, [shapes[n] for n in rest])))
    outs = [loss, grad_x.reshape(x.shape)]
    for tag in ("grad", "delta", "new_m", "new_v"):
        outs += [res[tag][n] for n in _ORDER]
    return tuple(outs)
```

```python
import numpy as np
import jax
import jax.numpy as jnp
from jax import lax
from jax.experimental import pallas as pl
from jax.experimental.pallas import tpu as pltpu

f32 = jnp.float32
bf16 = jnp.bfloat16

LANES = 128
HEADS = 8
HEAD_DIM = 64
WIDTH = HEADS * HEAD_DIM
CHUNK = 64
D_FF = 2816
NORM_EPS = 1e-6
LNX_EPS = 64e-5
PACK_W = 1024
MESH_ID = pl.DeviceIdType.MESH

ADAM_LR, ADAM_B1, ADAM_B2, ADAM_EPS, ADAM_WD, ADAM_STEP = 0.001, 0.9, 0.999, 1e-08, 0.01, 10


def _pick(n, target, mult):
    if n <= target:
        return n
    best = None
    for t in range(mult, target + 1, mult):
        if n % t == 0:
            best = t
    assert best is not None, (n, target, mult)
    return best


def _split_bf16(x, n):
    parts, r = [], x
    for i in range(n):
        p = r.astype(bf16)
        parts.append(p)
        if i + 1 < n:
            r = r - p.astype(f32)
    return parts


def _xdot_r_impl(x, m, n, dims):
    acc = None
    for p in _split_bf16(x, n):
        t = lax.dot_general(p, m, dims, preferred_element_type=f32)
        acc = t if acc is None else acc + t
    return acc


def _make_xdot_r(n):
    nn = (((1,), (0,)), ((), ()))
    nt = (((1,), (1,)), ((), ()))

    @jax.custom_vjp
    def xdot(x, m):
        return _xdot_r_impl(x, m, n, nn)

    def fwd(x, m):
        return _xdot_r_impl(x, m, n, nn), m

    def bwd(m, ct):
        return _xdot_r_impl(ct, m, n, nt), jnp.zeros_like(m)

    xdot.defvjp(fwd, bwd)
    return xdot


_segsum = _make_xdot_r(2)


def _xdot_l_impl(m, x, n, dims):
    acc = None
    for p in _split_bf16(x, n):
        t = lax.dot_general(m, p, dims, preferred_element_type=f32)
        acc = t if acc is None else acc + t
    return acc


@jax.custom_vjp
def _xdot_l(m, x):
    return _xdot_l_impl(m, x, 3, (((1,), (0,)), ((), ())))


def _xdot_l_fwd(m, x):
    return _xdot_l(m, x), m


def _xdot_l_bwd(m, ct):
    return jnp.zeros_like(m), _xdot_l_impl(m, ct, 3, (((0,), (0,)), ((), ())))


_xdot_l.defvjp(_xdot_l_fwd, _xdot_l_bwd)


@jax.custom_vjp
def _bdot(x, w):
    return jnp.dot(x.astype(bf16), w.astype(bf16), preferred_element_type=f32)


def _bdot_fwd(x, w):
    return _bdot(x, w), (x, w)


def _bdot_bwd(res, ct):
    x, w = res
    c = ct.astype(bf16)
    dx = lax.dot_general(c, w.astype(bf16), (((1,), (1,)), ((), ())), preferred_element_type=f32)
    dw = lax.dot_general(x.astype(bf16), c, (((0,), (0,)), ((), ())), preferred_element_type=f32)
    return dx, dw


_bdot.defvjp(_bdot_fwd, _bdot_bwd)


def _silu(x):
    return x * jax.nn.sigmoid(x)


def _softplus(x):
    return jnp.maximum(x, 0.0) + jnp.log(1.0 + jnp.exp(-jnp.abs(x)))


def _rms(x, w, eps):
    return x * lax.rsqrt(jnp.mean(x * x, axis=-1, keepdims=True) + eps) * w


def _seg_matrix(width, seg):
    i = np.arange(width)
    return jnp.asarray((i[:, None] // seg) == (i[None, :] // seg), dtype=bf16)


def _chunk_tri(rows, chunk):
    i = np.arange(rows)
    return jnp.asarray(((i[:, None] // chunk) == (i[None, :] // chunk)) & (i[:, None] >= i[None, :]), dtype=bf16)


HALO = 8


def _full_spec(shape):
    nd = len(shape)
    return pl.BlockSpec(shape, lambda i: (0,) * nd)


def _entry_specs(entries, tt, block_of):
    specs, ops = [], []
    for arr, w, ci, shifts in entries:
        specs.append(pl.BlockSpec((tt, w), lambda i, ci=ci: (block_of(i), ci)))
        ops.append(arr)
        if shifts:
            specs.append(pl.BlockSpec((HALO, w), lambda i, ci=ci: (jnp.maximum(block_of(i) * (tt // HALO) - 1, 0), ci)))
            ops.append(arr)
    return specs, ops


def _load_entries(entries, refs, first):
    tiles, k = [], 0
    for _, w, _, shifts in entries:
        x = refs[k][...].astype(f32)
        k += 1
        if not shifts:
            tiles.append(x)
            continue
        halo = jnp.where(first, 0.0, refs[k][...].astype(f32))
        k += 1
        row = lax.broadcasted_iota(jnp.int32, (HALO, w), 0)
        for s in shifts:
            if s == 0:
                tiles.append(x)
                continue
            r = pltpu.roll(x, s, 0)
            head = jnp.where(row < s, pltpu.roll(halo, s, 0), r[0:HALO])
            tiles.append(jnp.concatenate([head, r[HALO:]], axis=0))
    return tiles


def _unshift_sum(grads, shifts, carry, tt):
    w = grads[0].shape[1]
    row = lax.broadcasted_iota(jnp.int32, (tt, w), 0)
    row8 = lax.broadcasted_iota(jnp.int32, (HALO, w), 0)
    dx, out = None, jnp.zeros((HALO, w), f32)
    for d, s in zip(grads, shifts):
        if s == 0:
            part = d
        else:
            part = jnp.where(row < tt - s, pltpu.roll(d, tt - s, 0), 0.0)
            out = out + jnp.where(row8 >= HALO - s, pltpu.roll(d[0:HALO], HALO - s, 0), 0.0)
        dx = part if dx is None else dx + part
    return jnp.concatenate([dx[:tt - HALO], dx[tt - HALO:] + carry], axis=0), out


def _stage_fwd(name, fn, tiles, params, outs, tt):
    rows = tiles[0][0].shape[0]
    npar = len(params)
    specs, ops = _entry_specs(tiles, tt, lambda i: i)
    nin = len(ops)

    def body(*refs):
        ts = _load_entries(tiles, refs[:nin], pl.program_id(0) == 0)
        ps = [r[...] for r in refs[nin:nin + npar]]
        res = fn(ps, ts)
        for r, v in zip(refs[nin + npar:], res):
            r[...] = v.astype(r.dtype)

    return pl.pallas_call(
        body, grid=(rows // tt,),
        in_specs=specs + [_full_spec(p.shape) for p in params],
        out_specs=[pl.BlockSpec((tt, w), lambda i: (i, 0)) for (w, _) in outs],
        out_shape=[jax.ShapeDtypeStruct((rows, w), dt) for (w, dt) in outs],
        compiler_params=pltpu.CompilerParams(dimension_semantics=("parallel",)),
        name=name,
    )(*ops, *params)


def _stage_bwd(name, fn, tiles, params, cots, tt, tile_grad, param_grad, joint=None, dtypes=None):
    rows = tiles[0][0].shape[0]
    nblk = rows // tt
    npar = len(params)
    block_of = lambda i: nblk - 1 - i
    specs, ops = _entry_specs(tiles, tt, block_of)
    nin = len(ops)
    flat_cots = [c for group in cots for c in group]
    groups = [len(g) for g in cots]
    ncot = len(flat_cots)
    counts = [len(e[3]) if e[3] else 1 for e in tiles]
    dt_entries = [e for e, g in zip(tiles, tile_grad) if g]
    dp_shapes = [p.shape for p, g in zip(params, param_grad) if g]
    ndt = len(dt_entries)
    carry_w = [e[1] for e in dt_entries if e[3]]
    flags = [g for g, n in zip(tile_grad, counts) for _ in range(n)]
    members, j_width, j_cidx, j_buf = joint if joint else ([], 0, 0, None)
    solo = [k for k in range(ndt) if k not in members]
    nsolo, njoint, nbuf = len(solo), int(bool(members)), int(j_buf is not None)
    j_block = sum(dt_entries[k][1] for k in members)
    dtypes = list(dtypes) if dtypes else [f32] * (nsolo + njoint)

    def body(*refs):
        i = pl.program_id(0)
        p_refs = refs[nin:nin + npar]
        c_refs = refs[nin + npar:nin + npar + ncot]
        base = nin + npar + ncot + nbuf
        dt_refs = refs[base:base + nsolo]
        joint_refs = refs[base + nsolo:base + nsolo + njoint]
        dp_refs = refs[base + nsolo + njoint:base + nsolo + njoint + len(dp_shapes)]
        carry_refs = refs[base + nsolo + njoint + len(dp_shapes):]
        ts = _load_entries(tiles, refs[:nin], block_of(i) == 0)
        ps = [r[...] for r in p_refs]

        def f(dp, dt):
            dp, dt = iter(dp), iter(dt)
            pp = [next(dp) if g else p for p, g in zip(ps, param_grad)]
            tl = [next(dt) if g else t for t, g in zip(ts, flags)]
            return fn(pp, tl)

        _, vjp = jax.vjp(f, [p for p, g in zip(ps, param_grad) if g], [t for t, g in zip(ts, flags) if g])
        cs, j = [], 0
        for n in groups:
            acc = c_refs[j][...].astype(f32)
            for q in range(1, n):
                acc = acc + c_refs[j + q][...].astype(f32)
            cs.append(acc)
            j += n
        gp, gt = vjp(cs)

        @pl.when(i == 0)
        def _():
            for r in dp_refs:
                r[...] = jnp.zeros_like(r)
            for r in carry_refs:
                r[...] = jnp.zeros_like(r)

        gt, k, kc, dxs = list(gt), 0, 0, []
        for e, n in zip(dt_entries, [n for n, g in zip(counts, tile_grad) if g]):
            if e[3]:
                dx, out = _unshift_sum(gt[k:k + n], e[3], carry_refs[kc][...], tt)
                carry_refs[kc][...] = out
                kc += 1
            else:
                dx = gt[k]
            dxs.append(dx)
            k += n
        for r, k in zip(dt_refs, solo):
            r[...] = dxs[k].astype(r.dtype)
        off = 0
        for k in members:
            w = dt_entries[k][1]
            joint_refs[0][:, off:off + w] = dxs[k].astype(joint_refs[0].dtype)
            off += w
        for r, v in zip(dp_refs, gp):
            r[...] += v

    res = pl.pallas_call(
        body, grid=(nblk,),
        in_specs=specs + [_full_spec(p.shape) for p in params]
        + [pl.BlockSpec((tt, w), lambda i, ci=ci: (block_of(i), ci)) for (_, w, ci, *_) in flat_cots]
        + [pl.BlockSpec(memory_space=pl.ANY)] * nbuf,
        out_specs=[pl.BlockSpec((tt, dt_entries[k][1]), lambda i: (block_of(i), 0)) for k in solo]
        + [pl.BlockSpec((tt, j_block), lambda i: (block_of(i), j_cidx))] * njoint
        + [_full_spec(s) for s in dp_shapes],
        out_shape=[jax.ShapeDtypeStruct((rows, dt_entries[k][1]), dt) for k, dt in zip(solo, dtypes)]
        + [jax.ShapeDtypeStruct((rows, j_width), dtypes[-1])] * njoint
        + [jax.ShapeDtypeStruct(s, f32) for s in dp_shapes],
        scratch_shapes=[pltpu.VMEM((HALO, w), f32) for w in carry_w],
        input_output_aliases={nin + npar + ncot: nsolo} if nbuf else {},
        compiler_params=pltpu.CompilerParams(dimension_semantics=("arbitrary",)),
        name=name,
    )(*ops, *params, *[c[0] for c in flat_cots], *([j_buf] if nbuf else []))
    res = list(res)
    return res[:nsolo], res[nsolo + njoint:], (res[nsolo] if njoint else None)


def _whole(a):
    return (a, a.shape[1], 0, None)


def _matmul(name, a, b, mode, out_dtype=f32, tm=1024, tn=1024, tk=1024, shards=1, job=None):
    S = shards
    if mode == "nn":
        M, K = a.shape
        w = b.shape[-1]
    elif mode == "nt":
        M = a.shape[0]
        if S > 1:
            _, N, w = b.shape
            K = S * w
        else:
            N, K = b.shape
            w = K
    else:
        K, M = a.shape
        w = b.shape[1] // S
    if mode != "nt":
        N = S * w
    tm = _pick(M, tm, LANES)
    if mode == "nt":
        tn = _pick(N, tn, LANES)
        tk = _pick(w, tk, LANES)
    else:
        tn = _pick(w, tn, LANES)
        tk = _pick(K, tk, LANES if mode == "nn" else 16)
    nk = K // tk
    nb = w // (tk if mode == "nt" else tn)
    if mode == "nn":
        a_spec = pl.BlockSpec((tm, tk), lambda i, j, k: (i, k))
        if S > 1:
            b_spec = pl.BlockSpec((1, tk, tn), lambda i, j, k: (j // nb, k, j % nb))
        else:
            b_spec = pl.BlockSpec((tk, tn), lambda i, j, k: (k, j))
        dims = (((1,), (0,)), ((), ()))
    elif mode == "nt":
        a_spec = pl.BlockSpec((tm, tk), lambda i, j, k: (i, k))
        if S > 1:
            b_spec = pl.BlockSpec((1, tn, tk), lambda i, j, k: (k // nb, j, k % nb))
        else:
            b_spec = pl.BlockSpec((tn, tk), lambda i, j, k: (j, k))
        dims = (((1,), (1,)), ((), ()))
    else:
        a_spec = pl.BlockSpec((tk, tm), lambda i, j, k: (k, i))
        b_spec = pl.BlockSpec((tk, tn), lambda i, j, k: (k, j))
        dims = (((0,), (0,)), ((), ()))
    if mode == "tn" and S > 1:
        o_spec = pl.BlockSpec((1, tm, tn), lambda i, j, k: (j // nb, i, j % nb))
        o_shape = (S, M, w)
    else:
        o_spec = pl.BlockSpec((tm, tn), lambda i, j, k: (i, j))
        o_shape = (M, N)
    b_lead = S > 1 and mode != "tn"
    o_lead = S > 1 and mode == "tn"

    j_ins, j_outs, j_sems = _job_parts(job)
    nji, njo = len(j_ins), len(j_outs)
    grid = (M // tm, N // tn, nk)

    def run_job(refs):
        step = (pl.program_id(0) * grid[1] + pl.program_id(1)) * grid[2] + pl.program_id(2)
        _job_steps(job, refs[2:2 + nji], refs[3 + nji:3 + nji + njo], refs[len(refs) - len(j_sems):], step,
                   grid[0] * grid[1] * grid[2] - 1)

    def body(*refs):
        a_ref, b_ref, o_ref, acc_ref = refs[0], refs[1], refs[2 + nji], refs[3 + nji + njo]
        run_job(refs)
        k = pl.program_id(2)

        @pl.when(k == 0)
        def _():
            acc_ref[...] = jnp.zeros_like(acc_ref)

        bv = b_ref[0] if b_lead else b_ref[...]
        acc_ref[...] += lax.dot_general(a_ref[...].astype(bf16), bv.astype(bf16), dims, preferred_element_type=f32)

        @pl.when(k == nk - 1)
        def _():
            if o_lead:
                o_ref[0] = acc_ref[...].astype(o_ref.dtype)
            else:
                o_ref[...] = acc_ref[...].astype(o_ref.dtype)

    def body_one_step(*refs):
        a_ref, b_ref, o_ref = refs[0], refs[1], refs[2 + nji]
        run_job(refs)
        bv = b_ref[0] if b_lead else b_ref[...]
        res = lax.dot_general(a_ref[...].astype(bf16), bv.astype(bf16), dims, preferred_element_type=f32)
        if o_lead:
            o_ref[0] = res.astype(o_ref.dtype)
        else:
            o_ref[...] = res.astype(o_ref.dtype)

    res = pl.pallas_call(
        body if nk > 1 else body_one_step, grid=grid,
        in_specs=[a_spec, b_spec] + [_ANY] * nji,
        out_specs=[o_spec] + [_ANY] * njo,
        out_shape=[jax.ShapeDtypeStruct(o_shape, out_dtype)] + j_outs,
        scratch_shapes=([pltpu.VMEM((tm, tn), f32)] if nk > 1 else []) + j_sems,
        compiler_params=pltpu.CompilerParams(
            dimension_semantics=("arbitrary",) * 3 if job else ("parallel", "parallel", "arbitrary")),
        name=name,
    )(a, b, *j_ins)
    return res if job else res[0]


def _make_bmm(precision):
    if precision is None:
        cast, kw = (lambda v: v.astype(bf16)), {}
    else:
        cast, kw = (lambda v: v), {"precision": precision}

    def nn(a, b):
        return jnp.einsum("hij,hjk->hik", cast(a), cast(b), preferred_element_type=f32, **kw)

    def nt(a, b):
        return jnp.einsum("hik,hjk->hij", cast(a), cast(b), preferred_element_type=f32, **kw)

    def tn(a, b):
        return jnp.einsum("hki,hkj->hij", cast(a), cast(b), preferred_element_type=f32, **kw)

    if precision is not None:
        return nn, nt, tn
    nn_v, nt_v, tn_v = jax.custom_vjp(nn), jax.custom_vjp(nt), jax.custom_vjp(tn)
    keep = lambda f: (lambda a, b: (f(a, b), (a, b)))
    nn_v.defvjp(keep(nn), lambda r, ct: (nt(ct, r[1]), tn(r[0], ct)))
    nt_v.defvjp(keep(nt), lambda r, ct: (nn(ct, r[1]), tn(ct, r[0])))
    tn_v.defvjp(keep(tn), lambda r, ct: (nt(r[1], ct), nn(r[0], ct)))
    return nn_v, nt_v, tn_v


_bmm, _bmm_nt, _bmm_tn = _make_bmm(None)
_bmm_exact = _make_bmm(lax.Precision.HIGH)[0]


def _masks(n):
    r = lax.broadcasted_iota(jnp.int32, (n, n), 0)
    c = lax.broadcasted_iota(jnp.int32, (n, n), 1)
    return (r >= c)[None], (r > c)[None], (r == c)[None]


_INV_BLOCK = 8


def _nilpotent_inverse(m, eye):
    p = eye + m
    for _ in range(2):
        m = _bmm(m, m)
        p = p + _bmm(p, m)
    return p


def _neumann_inverse_impl(m):
    n = m.shape[1]
    assert n == _INV_BLOCK * _INV_BLOCK
    r = lax.broadcasted_iota(jnp.int32, (n, n), 0)
    c = lax.broadcasted_iota(jnp.int32, (n, n), 1)
    eye = (r == c).astype(f32)[None]
    inside = jnp.where((r // _INV_BLOCK == c // _INV_BLOCK)[None], m, 0.0)
    d_inv = _nilpotent_inverse(inside, eye)
    return _bmm(_nilpotent_inverse(_bmm(d_inv, m - inside), eye), d_inv)


@jax.custom_vjp
def _neumann_inverse(m):
    return _neumann_inverse_impl(m)


def _neumann_inverse_fwd(m):
    p = _neumann_inverse_impl(m)
    return p, p


def _neumann_inverse_bwd(p, ct):
    return (_bmm_tn(p, _bmm_nt(ct, p)),)


_neumann_inverse.defvjp(_neumann_inverse_fwd, _neumann_inverse_bwd)


@jax.custom_vjp
def _given_inverse(m, p):
    return p


_given_inverse.defvjp(lambda m, p: (p, p), lambda p, ct: (_neumann_inverse_bwd(p, ct)[0], jnp.zeros_like(p)))


def _gdn_chunk(s, q, k, v, beta, gc, gr, gl, p=None):
    n = q.shape[1]
    causal, strict, _ = _masks(n)
    decay = jnp.where(causal, jnp.exp(jnp.where(causal, gc - gr, 0.0)), 0.0)
    kb = k * beta
    vb = v * beta
    lower = jnp.where(strict, _bmm_nt(kb, k) * decay, 0.0)
    t_mat = _neumann_inverse(-lower) if p is None else _given_inverse(-lower, p)
    egc = jnp.exp(gc)
    u = _bmm(t_mat, vb)
    w = _bmm(t_mat, kb * egc)
    attn = jnp.where(causal, _bmm_nt(q, k) * decay, 0.0)
    v_new = u - _bmm(w, s)
    o = _bmm(q * egc, s) + _bmm(attn, v_new)
    k_dec = k * jnp.exp(gl - gc)
    s_new = s * jnp.exp(gl) + _bmm_tn(k_dec, v_new)
    return s_new, o, t_mat


def _rwkv_chunk(s, r, lw, k, v, a, b, p=None):
    n = r.shape[1]
    causal, strict, _ = _masks(n)
    tri = jnp.broadcast_to(causal.astype(f32), (r.shape[0], n, n))
    lc = _bmm_exact(tri, lw)
    ein = jnp.exp(lc)
    eout = jnp.exp(-lc)
    a_t = a * jnp.exp(lc - lw)
    b_t = b * eout
    k_t = k * eout
    r_t = r * ein
    a_ab = jnp.where(strict, _bmm_nt(a_t, b_t), 0.0)
    a_ak = jnp.where(strict, _bmm_nt(a_t, k_t), 0.0)
    inv = _neumann_inverse(a_ab) if p is None else _given_inverse(a_ab, p)
    u = _bmm(inv, _bmm_nt(a_t, s) + _bmm(a_ak, v))
    y = (_bmm_nt(r_t, s) + _bmm(jnp.where(causal, _bmm_nt(r_t, b_t), 0.0), u)
         + _bmm(jnp.where(causal, _bmm_nt(r_t, k_t), 0.0), v))
    e_last = jnp.exp(jnp.sum(lw, axis=1, keepdims=True))
    s_new = s * e_last + _bmm_tn(u, b_t * e_last) + _bmm_tn(v, k_t * e_last)
    return s_new, y, inv


def _heads_in(ref, rows):
    return jnp.stack([ref[rows, h * HEAD_DIM:(h + 1) * HEAD_DIM] for h in range(HEADS)], axis=0)


def _heads_out(ref, rows, val):
    for h in range(HEADS):
        ref[rows, h * HEAD_DIM:(h + 1) * HEAD_DIM] = val[h]


def _gdn_scalars(bt, gt):
    n = bt.shape[0]
    gtt = gt.T
    hs = range(HEADS)
    return [jnp.stack([bt[:, h:h + 1] for h in hs], axis=0),
            jnp.stack([gt[:, HEADS + h:HEADS + h + 1] for h in hs], axis=0),
            jnp.stack([gtt[HEADS + h:HEADS + h + 1, :] for h in hs], axis=0),
            jnp.stack([gt[n - 1:n, HEADS + h:HEADS + h + 1] for h in hs], axis=0)]


def _gdn_scalars_back(dbeta, dgc, dgr, dgl):
    n = dbeta.shape[1]
    lane = lax.broadcasted_iota(jnp.int32, (n, LANES), 1)
    row = lax.broadcasted_iota(jnp.int32, (n, LANES), 0)
    sub = lax.broadcasted_iota(jnp.int32, (LANES, n), 0)
    db = jnp.zeros((n, LANES), f32)
    dg = jnp.zeros((n, LANES), f32)
    dgt = jnp.zeros((LANES, n), f32)
    for h in range(HEADS):
        db = jnp.where(lane == h, dbeta[h], db)
        dg = jnp.where(lane == HEADS + h, dgc[h] + jnp.where(row == n - 1, dgl[h], 0.0), dg)
        dgt = jnp.where(sub == HEADS + h, dgr[h], dgt)
    return [db, dg + dgt.T]


SCAN_GROUP = 2


def _scan_steps(t):
    g = SCAN_GROUP if (t // CHUNK) % SCAN_GROUP == 0 else 1
    return g, t // (CHUNK * g)


def _scan_spec(width, g, n, reverse):
    if reverse:
        return pl.BlockSpec((g * CHUNK, width), lambda i: (n - 1 - i, 0))
    return pl.BlockSpec((g * CHUNK, width), lambda i: (i, 0))


def _hist_spec(g, n, reverse):
    blk = (g, HEADS, HEAD_DIM, HEAD_DIM)
    if reverse:
        return pl.BlockSpec(blk, lambda i: (n - 1 - i, 0, 0, 0))
    return pl.BlockSpec(blk, lambda i: (i, 0, 0, 0))


def _job_list(job):
    return [] if job is None else (list(job) if isinstance(job, (list, tuple)) else [job])


def _job_parts(job):
    jobs = _job_list(job)
    sems = [pltpu.SemaphoreType.DMA((j["nsem"],)) for j in jobs for _ in range(2)]
    return [a for j in jobs for a in j["ins"]], [o for j in jobs for o in j["out_shapes"]], sems


def _job_phase(job, phase, in_refs, out_refs, sems):
    ki = ko = 0
    for n, j in enumerate(_job_list(job)):
        ni, no = len(j["ins"]), len(j["out_shapes"])
        j[phase](in_refs[ki:ki + ni], out_refs[ko:ko + no], sems[2 * n], sems[2 * n + 1])
        ki, ko = ki + ni, ko + no


def _job_steps(job, in_refs, out_refs, sems, step, last):
    if job is None:
        return

    @pl.when(step == 0)
    def _():
        _job_phase(job, "start", in_refs, out_refs, sems)

    @pl.when(step == last)
    def _():
        _job_phase(job, "finish", in_refs, out_refs, sems)


def _scan_fwd(name, fn, rows_in, scal_in=(), job=None):
    t = rows_in[0].shape[0]
    grp, n = _scan_steps(t)
    nr, ns = len(rows_in), len(scal_in)
    j_ins, j_outs, j_sems = _job_parts(job)
    nji, njo = len(j_ins), len(j_outs)

    def body(*refs):
        o_ref, sh_ref, ph_ref = refs[nr + ns + nji:nr + ns + nji + 3]
        s_scr = refs[nr + ns + nji + 3 + njo]
        _job_steps(job, refs[nr + ns:nr + ns + nji], refs[nr + ns + nji + 3:nr + ns + nji + 3 + njo],
                   refs[nr + ns + nji + 3 + njo + 1:], pl.program_id(0), n - 1)

        @pl.when(pl.program_id(0) == 0)
        def _():
            s_scr[...] = jnp.zeros_like(s_scr)

        s = s_scr[...]
        for sub in range(grp):
            rows = slice(sub * CHUNK, (sub + 1) * CHUNK)
            sh_ref[sub] = s
            ins = [_heads_in(r, rows) for r in refs[:nr]]
            if ns:
                ins += _gdn_scalars(*[r[rows, :] for r in refs[nr:nr + ns]])
            s, o, p = fn(s, *ins)
            _heads_out(o_ref, rows, o)
            ph_ref[sub] = p
        s_scr[...] = s

    return pl.pallas_call(
        body, grid=(n,),
        in_specs=[_scan_spec(a.shape[1], grp, n, False) for a in (*rows_in, *scal_in)] + [_ANY] * nji,
        out_specs=[_scan_spec(WIDTH, grp, n, False), _hist_spec(grp, n, False), _hist_spec(grp, n, False)] + [_ANY] * njo,
        out_shape=[jax.ShapeDtypeStruct((t, WIDTH), f32)]
        + [jax.ShapeDtypeStruct((t // CHUNK, HEADS, HEAD_DIM, HEAD_DIM), f32)] * 2 + j_outs,
        scratch_shapes=[pltpu.VMEM((HEADS, HEAD_DIM, HEAD_DIM), f32)] + j_sems,
        compiler_params=pltpu.CompilerParams(dimension_semantics=("arbitrary",)),
        name=name,
    )(*rows_in, *scal_in, *j_ins)


def _scan_bwd(name, fn, rows_in, scal_in, s_hist, p_hist, d_out, job=None):
    t = rows_in[0].shape[0]
    grp, n = _scan_steps(t)
    nr, ns = len(rows_in), len(scal_in)
    j_ins, j_outs, j_sems = _job_parts(job)
    nji, njo = len(j_ins), len(j_outs)

    def body(*refs):
        sh_ref, ph_ref, do_ref = refs[nr + ns:nr + ns + 3]
        base = nr + ns + 3 + nji
        g_refs = refs[base:base + nr + ns]
        ds_scr = refs[base + nr + ns + njo]
        _job_steps(job, refs[nr + ns + 3:base], refs[base + nr + ns:base + nr + ns + njo],
                   refs[base + nr + ns + njo + 1:], pl.program_id(0), n - 1)

        @pl.when(pl.program_id(0) == 0)
        def _():
            ds_scr[...] = jnp.zeros_like(ds_scr)

        ds = ds_scr[...]
        for sub in reversed(range(grp)):
            rows = slice(sub * CHUNK, (sub + 1) * CHUNK)
            ins = [_heads_in(r, rows) for r in refs[:nr]]
            if ns:
                ins += _gdn_scalars(*[r[rows, :] for r in refs[nr:nr + ns]])
            p = ph_ref[sub]
            _, vjp = jax.vjp(lambda s, *a, p=p: fn(s, *a, p=p)[:2], sh_ref[sub], *ins)
            g = vjp((ds, _heads_in(do_ref, rows)))
            ds = g[0]
            for r, v in zip(g_refs[:nr], g[1:1 + nr]):
                _heads_out(r, rows, v)
            if ns:
                for r, v in zip(g_refs[nr:], _gdn_scalars_back(*g[1 + nr:])):
                    r[rows, :] = v
        ds_scr[...] = ds

    arrs = (*rows_in, *scal_in)
    return pl.pallas_call(
        body, grid=(n,),
        in_specs=[_scan_spec(a.shape[1], grp, n, True) for a in arrs]
        + [_hist_spec(grp, n, True), _hist_spec(grp, n, True), _scan_spec(WIDTH, grp, n, True)] + [_ANY] * nji,
        out_specs=[_scan_spec(a.shape[1], grp, n, True) for a in arrs] + [_ANY] * njo,
        out_shape=[jax.ShapeDtypeStruct(a.shape, f32) for a in arrs] + j_outs,
        scratch_shapes=[pltpu.VMEM((HEADS, HEAD_DIM, HEAD_DIM), f32)] + j_sems,
        compiler_params=pltpu.CompilerParams(dimension_semantics=("arbitrary",)),
        name=name,
    )(*arrs, s_hist, p_hist, d_out, *j_ins)


def _fn_norm_mod(ps, ts):
    nw, shift, scale = ps
    (x,) = ts
    return [_rms(x, nw, NORM_EPS) * (1.0 + scale) + shift]


def _fn_norm_mod_and_x(ps, ts):
    return _fn_norm_mod(ps, ts) + [ts[0]]


def _fn_resid_norm_mod(ps, ts):
    gate, nw, shift, scale = ps
    x, mo = ts
    x1 = x + gate * mo
    return [x1, _rms(x1, nw, NORM_EPS) * (1.0 + scale) + shift]


def _fn_gdn_pre(ps, ts):
    cw = ps[:12]
    alog, dtb, seg, tri = ps[12:]
    ba = ts[12]
    outs = []
    for part in range(3):
        x = ts[4 * part:4 * part + 4]
        w = cw[4 * part:4 * part + 4]
        conv = w[3] * x[0] + w[2] * x[1] + w[1] * x[2] + w[0] * x[3]
        u = _silu(conv)
        if part < 2:
            u = u * lax.rsqrt(_segsum(u * u, seg) + 1e-6)
            if part == 0:
                u = u * (HEAD_DIM ** -0.5)
        outs.append(u)
    beta = jax.nn.sigmoid(ba)
    g = -jnp.exp(alog) * _softplus(ba + dtb)
    gc = _xdot_l(tri, g)
    return outs + [beta, gc]


def _fn_gdn_post(ps, ts):
    ow, seg = ps
    o, z = ts
    ms = _segsum(o * o, seg) * (1.0 / HEAD_DIM)
    return [o * lax.rsqrt(ms + NORM_EPS) * ow * _silu(z)]


def _fn_rwkv_pre(ps, ts):
    mu_r, mu_k, mu_v, mu_l, mu_g, w0, w2p, a0, a2p, g2p, k_k, k_a, seg = ps
    r0, r1, k0, k1, v0, v1, l0, l1, g0, g1 = ts
    xr = r0 + (r1 - r0) * mu_r
    xk = k0 + (k1 - k0) * mu_k
    xv = v0 + (v1 - v0) * mu_v
    xl = l0 + (l1 - l0) * mu_l
    xg = g0 + (g1 - g0) * mu_g
    w = -_softplus(-(w0 + _bdot(jnp.tanh(xl), w2p))) - 0.5
    lw = -jnp.exp(w)
    a = jax.nn.sigmoid(a0 + _bdot(xl, a2p))
    g = _bdot(jax.nn.sigmoid(xg), g2p)
    kk = xk * k_k
    kk = kk * lax.rsqrt(_segsum(kk * kk, seg) + 1e-6)
    k2 = xk * (1.0 + (a - 1.0) * k_a)
    return [xr, lw, k2, xv, -kk, kk * a, g]


def _fn_rwkv_post(ps, ts):
    lw_, lb_, rk, seg = ps
    y, r, k2, v, g = ts
    inv = 1.0 / HEAD_DIM
    yc = y - _segsum(y, seg) * inv
    var = _segsum(yc * yc, seg) * inv
    yn = yc * lax.rsqrt(var + LNX_EPS) * lw_ + lb_
    bonus = _segsum(r * k2 * rk, seg) * v
    return [(yn + bonus) * g]


def _fn_merge(ps, ts):
    gla, glb, ya, yb = ts
    return [jax.nn.sigmoid(gla) * ya + jax.nn.sigmoid(glb) * yb]


def _fn_convglu(ps, ts):
    c0, c1, c2 = ps
    g0, g1, g2, up = ts
    return [_silu(c2 * g0 + c1 * g1 + c0 * g2) * up]


def _final_stage(x1, fo, tgt, gate2, nfw, tt):
    rows, d = x1.shape

    def loss_fn(gate, nw, xa, fa, tg):
        y = _rms(xa + gate * fa, nw, NORM_EPS)
        err = (y - tg) ** 2
        return 0.5 * jnp.sum(jnp.mean(err, axis=-1, keepdims=True), axis=0, keepdims=True)

    def body(x_ref, f_ref, t_ref, g_ref, w_ref, dx_ref, df_ref, dg_ref, dw_ref, l_ref):
        i = pl.program_id(0)
        args = (g_ref[...], w_ref[...], x_ref[...], f_ref[...])
        tg = t_ref[...]
        lv, vjp = jax.vjp(lambda g, w, xa, fa: loss_fn(g, w, xa, fa, tg), *args)
        dg, dw, dx, df = vjp(jnp.ones((1, 1), f32))
        dx_ref[...] = dx
        df_ref[...] = df.astype(df_ref.dtype)

        @pl.when(i == 0)
        def _():
            dg_ref[...] = jnp.zeros_like(dg_ref)
            dw_ref[...] = jnp.zeros_like(dw_ref)
            l_ref[...] = jnp.zeros_like(l_ref)

        dg_ref[...] += dg
        dw_ref[...] += dw
        l_ref[...] += jnp.broadcast_to(lv, l_ref.shape)

    row = pl.BlockSpec((tt, d), lambda i: (i, 0))
    vec = pl.BlockSpec((1, d), lambda i: (0, 0))
    return pl.pallas_call(
        body, grid=(rows // tt,),
        in_specs=[row, row, row, vec, vec],
        out_specs=[row, row, vec, vec, pl.BlockSpec((1, LANES), lambda i: (0, 0))],
        out_shape=[jax.ShapeDtypeStruct((rows, d), f32), jax.ShapeDtypeStruct((rows, d), bf16)]
        + [jax.ShapeDtypeStruct((1, d), f32)] * 2
        + [jax.ShapeDtypeStruct((1, LANES), f32)],
        compiler_params=pltpu.CompilerParams(dimension_semantics=("arbitrary",)),
        name="loss_head",
    )(x1, fo, tgt, gate2, nfw)


def _ada_fwd(c_all, w_shard, b_cols):
    def body(c_ref, w_ref, b_ref, cond_ref, mod_ref):
        cond = _silu(c_ref[...])
        cond_ref[...] = cond
        mod_ref[...] = jnp.dot(cond.astype(bf16), w_ref[...].astype(bf16), preferred_element_type=f32) + b_ref[...]

    n = w_shard.shape[1]
    return pl.pallas_call(
        body, out_shape=[jax.ShapeDtypeStruct(c_all.shape, f32), jax.ShapeDtypeStruct((c_all.shape[0], n), f32)],
        name="ada_fwd",
    )(c_all, w_shard, b_cols)


def _adamw(name, w, g, m, v):
    rows, width = w.shape
    tt = _pick(rows, 256, 8)
    c1 = 1.0 - ADAM_B1 ** ADAM_STEP
    c2 = 1.0 - ADAM_B2 ** ADAM_STEP

    def body(w_ref, g_ref, m_ref, v_ref, d_ref, mo_ref, vo_ref):
        gg = g_ref[...]
        mn = ADAM_B1 * m_ref[...] + (1.0 - ADAM_B1) * gg
        vn = ADAM_B2 * v_ref[...] + (1.0 - ADAM_B2) * (gg * gg)
        m_hat = mn / c1
        v_hat = vn / c2
        d_ref[...] = -ADAM_LR * (m_hat / (jnp.sqrt(v_hat) + ADAM_EPS) + ADAM_WD * w_ref[...])
        mo_ref[...] = mn
        vo_ref[...] = vn

    spec = pl.BlockSpec((tt, width), lambda i: (i, 0))
    return pl.pallas_call(
        body, grid=(rows // tt,), in_specs=[spec] * 4, out_specs=[spec] * 3,
        out_shape=[jax.ShapeDtypeStruct((rows, width), f32)] * 3,
        compiler_params=pltpu.CompilerParams(dimension_semantics=("parallel",)),
        name=name,
    )(w, g, m, v)


def _place():
    return lax.axis_index("x"), lax.axis_index("y"), lax.axis_index("c")


def _ag8(name, blk):
    m, w = blk.shape

    def body(x_ref, out_ref, send_sems, recv_sems, local_sem):
        x, y, c = _place()
        me, sibling = (x, y, c), (x, y, 1 - c)
        chips = _other_chips(x, y)

        def slot(px, py, pc):
            return out_ref.at[4 * px + 2 * py + pc]

        def copy(k, block, to, src=None):
            return pltpu.make_async_remote_copy(src_ref=slot(*block) if src is None else src, dst_ref=slot(*block),
                                                send_sem=send_sems.at[k], recv_sem=recv_sems.at[k], device_id=to,
                                                device_id_type=MESH_ID)

        mine = pltpu.make_async_copy(x_ref, slot(*me), local_sem)
        mine.start()
        first = [copy(0, me, sibling, src=x_ref)] + [copy(1 + j, me, (*chip, c), src=x_ref) for j, chip in enumerate(chips)]
        for cp in first:
            cp.start()
        passed = [copy(4 + j, (*chip, c), sibling) for j, chip in enumerate(chips)]
        for j, chip in enumerate(chips):
            copy(1 + j, (*chip, c), me).wait_recv()
            passed[j].start()
        copy(0, sibling, me).wait_recv()
        for j, chip in enumerate(chips):
            copy(4 + j, (*chip, 1 - c), me).wait_recv()
        for cp in first + passed:
            cp.wait_send()
        mine.wait()

    return pl.pallas_call(
        body, out_shape=jax.ShapeDtypeStruct((8, m, w), blk.dtype),
        in_specs=[pl.BlockSpec(memory_space=pltpu.VMEM)], out_specs=pl.BlockSpec(memory_space=pltpu.VMEM),
        scratch_shapes=[pltpu.SemaphoreType.DMA((7,)), pltpu.SemaphoreType.DMA((7,)), pltpu.SemaphoreType.DMA],
        name=name,
    )(blk)


def _ag8_job(blk):
    def plan(x_refs, out_refs, send_sems, recv_sems):
        x_ref, out_ref = x_refs[0], out_refs[0]
        x, y, c = _place()
        me, sibling = (x, y, c), (x, y, 1 - c)
        chips = _other_chips(x, y)
        slot = lambda px, py, pc: out_ref.at[4 * px + 2 * py + pc]
        cp = lambda k, block, to, src=None: _rcopy(slot(*block) if src is None else src, slot(*block), send_sems, recv_sems, k, to)
        first = [cp(0, me, sibling, x_ref)] + [cp(1 + j, me, (*chip, c), x_ref) for j, chip in enumerate(chips)]
        landed = [cp(1 + j, (*chip, c), me) for j, chip in enumerate(chips)]
        passed = [cp(4 + j, (*chip, c), sibling) for j, chip in enumerate(chips)]
        handed = [cp(0, sibling, me)] + [cp(4 + j, (*chip, 1 - c), me) for j, chip in enumerate(chips)]
        mine = pltpu.make_async_copy(x_ref, slot(*me), send_sems.at[7])
        return first, landed, passed, handed, mine

    def start(*refs):
        first, _, _, _, mine = plan(*refs)
        mine.start()
        for cp in first:
            cp.start()

    def finish(*refs):
        first, landed, passed, handed, mine = plan(*refs)
        for arrived, fw in zip(landed, passed):
            arrived.wait_recv()
            fw.start()
        for cp in handed:
            cp.wait_recv()
        for cp in first + passed:
            cp.wait_send()
        mine.wait()

    return dict(ins=[blk], out_shapes=[jax.ShapeDtypeStruct((8,) + blk.shape, blk.dtype)], nsem=8, start=start,
                finish=finish)


def _other_chips(x, y):
    return [(1 - x, y), (x, 1 - y), (1 - x, 1 - y)]


_ANY = pl.BlockSpec(memory_space=pl.ANY)


def _rcopy(src, dst, send_sems, recv_sems, k, dev):
    return pltpu.make_async_remote_copy(src_ref=src, dst_ref=dst, send_sem=send_sems.at[k], recv_sem=recv_sems.at[k],
                                        device_id=dev, device_id_type=MESH_ID)


def _run_job(name, job):
    j_ins, j_outs, j_sems = _job_parts(job)
    n = len(j_ins)

    def body(*refs):
        for phase in ("start", "finish"):
            _job_phase(job, phase, refs[:n], refs[n:n + len(j_outs)], refs[n + len(j_outs):])

    return pl.pallas_call(body, out_shape=j_outs, in_specs=[_ANY] * n, out_specs=[_ANY] * len(j_outs),
                          scratch_shapes=j_sems, name=name)(*j_ins)


def _ag4_job(ws):
    n = len(ws)

    def plan(w_refs, out_refs, send_sems, recv_sems):
        x, y, c = _place()
        chip = 2 * x + y
        sibling = (x, y, 1 - c)
        chips = _other_chips(x, y)
        mine = [pl.ds(c * (w.shape[0] // 2), w.shape[0] // 2) for w in ws]
        other = [pl.ds((1 - c) * (w.shape[0] // 2), w.shape[0] // 2) for w in ws]
        rc = lambda src, dst, k, dev: _rcopy(src, dst, send_sems, recv_sems, k, dev)
        first = [rc(w_refs[t].at[mine[t]], out_refs[t].at[chip, mine[t]], 7 * t + k, (px, py, c))
                 for t in range(n) for k, (px, py) in enumerate(chips)]
        own = [rc(w_refs[t], out_refs[t].at[chip], 7 * t + 6, sibling) for t in range(n)]
        landed = [rc(out_refs[t].at[2 * px + py, mine[t]], out_refs[t].at[2 * px + py, mine[t]], 7 * t + k, (px, py, c))
                  for t in range(n) for k, (px, py) in enumerate(chips)]
        forward = [rc(out_refs[t].at[2 * px + py, mine[t]], out_refs[t].at[2 * px + py, mine[t]], 7 * t + 3 + k, sibling)
                   for t in range(n) for k, (px, py) in enumerate(chips)]
        handed = [rc(out_refs[t].at[2 * px + py, other[t]], out_refs[t].at[2 * px + py, other[t]], 7 * t + 3 + k, sibling)
                  for t in range(n) for k, (px, py) in enumerate(chips)]
        return first, own, landed, forward, handed

    def start(*refs):
        first, own, _, _, _ = plan(*refs)
        for cp in first + own:
            cp.start()

    def finish(*refs):
        first, own, landed, forward, handed = plan(*refs)
        for arrived, fw in zip(landed, forward):
            arrived.wait_recv()
            fw.start()
        for cp in handed + own:
            cp.wait_recv()
        for cp in first + own + forward:
            cp.wait_send()

    return dict(ins=ws, out_shapes=[jax.ShapeDtypeStruct((4,) + w.shape, w.dtype) for w in ws], nsem=7 * n,
                start=start, finish=finish)


def _sibling_swap_job(gs):
    n = len(gs)

    def plan(g_refs, recv_refs, send_sems, recv_sems):
        x, y, c = _place()
        return [_rcopy(g_refs[t].at[s_, pl.ds((1 - c) * (gs[t].shape[1] // 2), gs[t].shape[1] // 2)], recv_refs[t].at[s_],
                       send_sems, recv_sems, 4 * t + s_, (x, y, 1 - c)) for t in range(n) for s_ in range(4)]

    def start(*refs):
        for cp in plan(*refs):
            cp.start()

    def finish(*refs):
        copies = plan(*refs)
        for cp in copies:
            cp.wait_recv()
        for cp in copies:
            cp.wait_send()

    return dict(ins=gs, out_shapes=[jax.ShapeDtypeStruct((4, g.shape[1] // 2, g.shape[2]), g.dtype) for g in gs],
                nsem=4 * n, start=start, finish=finish)


def _chip_exchange_job(ps):
    n = len(ps)

    def plan(p_refs, recv_refs, send_sems, recv_sems):
        x, y, c = _place()
        return [_rcopy(p_refs[t].at[2 * px + py], recv_refs[t].at[k], send_sems, recv_sems, 3 * t + k, (px, py, c))
                for t in range(n) for k, (px, py) in enumerate(_other_chips(x, y))]

    def start(*refs):
        for cp in plan(*refs):
            cp.start()

    def finish(*refs):
        copies = plan(*refs)
        for cp in copies:
            cp.wait_recv()
        for cp in copies:
            cp.wait_send()

    return dict(ins=ps, out_shapes=[jax.ShapeDtypeStruct((3,) + p.shape[1:], p.dtype) for p in ps], nsem=3 * n,
                start=start, finish=finish)


_JOIN_PIECES = 4


def _rs_sibling_join(qs):
    n = len(qs)
    npc = _JOIN_PIECES

    def body(*refs):
        q_refs, out_refs = refs[:n], refs[n:2 * n]
        send_sems, recv_sems = refs[2 * n:]
        x, y, c = _place()
        copies = []
        for t in range(n):
            rh = qs[t].shape[0] // 2
            pr = rh // npc
            for i in range(npc):
                rows = pl.ds(c * rh + i * pr, pr)
                cp = _rcopy(q_refs[t].at[rows], out_refs[t].at[rows], send_sems, recv_sems, npc * t + i,
                            (x, y, 1 - c))
                cp.start()
                copies.append(cp)
        for t in range(n):
            rh = qs[t].shape[0] // 2
            pr = rh // npc
            for i in range(npc):
                rows = pl.ds((1 - c) * rh + i * pr, pr)
                _rcopy(q_refs[t].at[rows], out_refs[t].at[rows], send_sems, recv_sems, npc * t + i,
                       (x, y, 1 - c)).wait_recv()
        for cp in copies:
            cp.wait_send()

    return pl.pallas_call(
        body, out_shape=[jax.ShapeDtypeStruct(q.shape, q.dtype) for q in qs],
        in_specs=[_ANY] * n, out_specs=[_ANY] * n, input_output_aliases={t: t for t in range(n)},
        scratch_shapes=[pltpu.SemaphoreType.DMA((npc * n,)), pltpu.SemaphoreType.DMA((npc * n,))],
        name="grads_sibling_join",
    )(*qs)


def _add_half(name, g, recv, chip, ci):
    S, r, w = g.shape
    rh = r // 2
    tt = _pick(rh, 256, 16)
    nb = rh // tt

    def body(chip_ref, core_ref, a_ref, b_ref, own_ref, ob_ref):
        v = a_ref[...] + b_ref[...]
        ob_ref[...] = v.astype(bf16)

        @pl.when(pl.program_id(1) == chip_ref[0])
        def _():
            own_ref[...] = v[0]

    grid_spec = pltpu.PrefetchScalarGridSpec(
        num_scalar_prefetch=2, grid=(nb, S),
        in_specs=[pl.BlockSpec((1, tt, w), lambda i, s_, ch, co: (s_, co[0] * nb + i, 0)),
                  pl.BlockSpec((1, tt, w), lambda i, s_, ch, co: (s_, i, 0))],
        out_specs=[pl.BlockSpec((tt, w), lambda i, s_, ch, co: (i, 0)),
                   pl.BlockSpec((1, tt, w), lambda i, s_, ch, co: (s_, i, 0))])
    return pl.pallas_call(body, grid_spec=grid_spec,
                          out_shape=[jax.ShapeDtypeStruct((rh, w), f32), jax.ShapeDtypeStruct((S, rh, w), bf16)],
                          compiler_params=pltpu.CompilerParams(dimension_semantics=("arbitrary", "arbitrary")),
                          name=name)(chip.reshape(1).astype(jnp.int32), ci.reshape(1).astype(jnp.int32), g, recv)


def _sum_chip(name, own, others, ci):
    rh, w = own.shape
    tt = _pick(rh, 128, 16)
    nb = rh // tt

    def body(core_ref, a_ref, b_ref, o_ref):
        o_ref[...] = ((a_ref[...] + b_ref[0].astype(f32)) + b_ref[1].astype(f32)) + b_ref[2].astype(f32)

    grid_spec = pltpu.PrefetchScalarGridSpec(
        num_scalar_prefetch=1, grid=(nb,),
        in_specs=[pl.BlockSpec((tt, w), lambda i, co: (i, 0)),
                  pl.BlockSpec((3, tt, w), lambda i, co: (0, i, 0))],
        out_specs=pl.BlockSpec((tt, w), lambda i, co: (co[0] * nb + i, 0)))
    return pl.pallas_call(body, grid_spec=grid_spec, out_shape=jax.ShapeDtypeStruct((2 * rh, w), f32),
                          name=name)(ci.reshape(1).astype(jnp.int32), own, others)


def _sum_devices(gathered, late, late_rows, head_row):
    _, rows, width = gathered.shape

    def body(g_ref, l_ref, out_ref, head_ref):
        acc, acc_l = g_ref[0], l_ref[0]
        for d in range(1, 8):
            acc = acc + g_ref[d]
            acc_l = acc_l + l_ref[d]
        out_ref[...] = acc
        for k, r in enumerate(late_rows):
            out_ref[r:r + 1, :] = acc_l[k:k + 1, :]
        row = acc[head_row:head_row + 1, :]
        hs = row[:, 0:HEAD_DIM]
        for h in range(1, HEADS):
            hs = hs + row[:, h * HEAD_DIM:(h + 1) * HEAD_DIM]
        head_ref[...] = jnp.zeros_like(head_ref)
        head_ref[0:1, 0:HEAD_DIM] = hs

    return pl.pallas_call(
        body, out_shape=[jax.ShapeDtypeStruct((rows, width), f32), jax.ShapeDtypeStruct((8, LANES), f32)],
        name="small_grads_sum",
    )(gathered, late)


def _pack(arrs, rows_mult, dtype):
    flat = jnp.concatenate([a.reshape(-1).astype(dtype) for a in arrs])
    per = PACK_W * rows_mult
    total = -(-flat.shape[0] // per) * per
    return jnp.pad(flat, (0, total - flat.shape[0])).reshape(total // PACK_W, PACK_W)


def _unpack(buf, shapes):
    flat = buf.reshape(-1)
    out, off = [], 0
    for s in shapes:
        n = int(np.prod(s))
        out.append(flat[off:off + n].reshape(s))
        off += n
    return out


_BIG = ["w_in", "w_branch_gdn", "w_branch_rwkv", "w_out", "w_ffn_in", "w_ffn_out"]
_MID = ["conv_gdn", "conv_ffn", "w2", "a2", "g2"]
_SMALL =["b_ada", "norm1_w", "a_log", "dt_bias", "onorm_gdn", "mu_rwkv", "w0", "a0", "k_k", "k_a", "r_k", "lnx_w",
          "lnx_b", "norm2_w", "norm_f_w"]
_ORDER = ["w_ada", "b_ada", "norm1_w", "w_in", "conv_gdn", "a_log", "dt_bias", "onorm_gdn", "w_branch_gdn", "mu_rwkv",
          "w0", "w2", "a0", "a2", "g2", "k_k", "k_a", "r_k", "lnx_w", "lnx_b", "w_branch_rwkv", "w_out", "norm2_w",
          "w_ffn_in", "conv_ffn", "w_ffn_out", "norm_f_w"]


_WIN_SEGMENTS = [(0, 1536, 0), (2064, 3600, 1536), (1536, 2048, 3072), (3600, 3728, 3584), (2048, 2064, 3712),
                 (3728, 3888, 3840), (3888, 5936, 4096)]
_WIN_PADDED = 6144
_COL_QKV, _COL_RKV, _COL_Z = (0, 1, 2), (3, 4, 5), 6
_COL_LORA, _COL_BA = 28, 29
_COL_GATE_LORA = 15
_COL_GL = (4, 5)
_JOINT_QKV, _JOINT_RKV, _JOINT_Z, _JOINT_GL = 0, 1, 6, 2
_SMALL_BLOCKS_AT = 3584


def _win_pad(shards):
    n = shards.shape[2]
    parts, at = [], 0
    for lo, hi, dst in _WIN_SEGMENTS:
        if dst > at:
            parts.append(jnp.zeros((shards.shape[1], dst - at), shards.dtype))
        c = lo
        while c < hi:
            j = c // n
            e = min(hi, (j + 1) * n)
            parts.append(shards[j][:, c - j * n:e - j * n])
            c = e
        at = dst + hi - lo
    if at < _WIN_PADDED:
        parts.append(jnp.zeros((shards.shape[1], _WIN_PADDED - at), shards.dtype))
    return jnp.concatenate(parts, axis=1)


def _win_unpad_shards(g, n):
    shards = []
    for j in range(4):
        parts = []
        for lo, hi, dst in sorted(_WIN_SEGMENTS):
            a, b = max(lo, j * n), min(hi, (j + 1) * n)
            if a < b:
                parts.append(g[:, dst + a - lo:dst + b - lo])
        shards.append(jnp.concatenate(parts, axis=1))
    return jnp.stack(shards)


def kernel(x, c, w_ada, b_ada, norm1_w, w_in, conv_gdn, a_log, dt_bias, onorm_gdn, w_branch_gdn, mu_rwkv, w0, w2, a0, a2, g2, k_k, k_a, r_k, lnx_w, lnx_b, w_branch_rwkv, w_out, norm2_w, w_ffn_in, conv_ffn, w_ffn_out, norm_f_w, loss_target, m_w_ada, m_b_ada, m_norm1_w, m_w_in, m_conv_gdn, m_a_log, m_dt_bias, m_onorm_gdn, m_w_branch_gdn, m_mu_rwkv, m_w0, m_w2, m_a0, m_a2, m_g2, m_k_k, m_k_a, m_r_k, m_lnx_w, m_lnx_b, m_w_branch_rwkv, m_w_out, m_norm2_w, m_w_ffn_in, m_conv_ffn, m_w_ffn_out, m_norm_f_w, v_w_ada, v_b_ada, v_norm1_w, v_w_in, v_conv_gdn, v_a_log, v_dt_bias, v_onorm_gdn, v_w_branch_gdn, v_mu_rwkv, v_w0, v_w2, v_a0, v_a2, v_g2, v_k_k, v_k_a, v_r_k, v_lnx_w, v_lnx_b, v_w_branch_rwkv, v_w_out, v_norm2_w, v_w_ffn_in, v_conv_ffn, v_w_ffn_out, v_norm_f_w):
    args = dict(locals())
    W = {n: args[n] for n in _ORDER}
    Mo = {n: args["m_" + n] for n in _ORDER}
    Vo = {n: args["v_" + n] for n in _ORDER}
    shapes = {n: W[n].shape for n in _ORDER}
    sq = lambda a: a.reshape(a.shape[-2:]) if a.ndim == 3 else a.reshape(1, -1)
    row = lambda a: a.reshape(1, -1)

    xi, yi, ci = lax.axis_index("x"), lax.axis_index("y"), lax.axis_index("c")
    dev = 4 * xi + 2 * yi + ci
    chip = 2 * xi + yi

    x2 = x[0]
    tgt = loss_target[0]
    T, D = x2.shape
    N = T // CHUNK
    tt_l = _pick(T, 512, CHUNK)
    tt_h = _pick(T, 128, CHUNK)

    mid_shapes = [shapes[n][1:] for n in _MID]
    small_blk = _pack([c] + [W[n] for n in _MID], 8, f32)
    small_all, win_s = _run_job("gather_c_mid_and_w_in", [_ag8_job(small_blk), _ag4_job([sq(W["w_in"]).astype(bf16)])])
    c_all = small_all[:, 0, :]
    per_chip = small_all[0::2].reshape(4, -1)[:, D:]
    mid = [dict(zip(_MID, _unpack(per_chip[j], mid_shapes))) for j in range(4)]
    catm = lambda n: jnp.concatenate([mid[j][n] for j in range(4)], axis=1)
    conv_gdn_f, conv_ffn_f = catm("conv_gdn"), catm("conv_ffn")
    w2f, a2f, g2f = catm("w2"), catm("a2"), catm("g2")

    later_weights = _ag4_job([sq(W[n]).astype(bf16) for n in _BIG[1:]])
    win_p = _win_pad(win_s)
    zpad = lambda a, top, bot: jnp.pad(a, ((top, bot), (0, 0)))
    w2p, a2p, g2p = zpad(w2f, 0, 64), zpad(a2f, 64, 0), zpad(g2f, 0, 96)

    ncol = shapes["w_ada"][2]
    b_cols = lax.dynamic_slice(sq(W["b_ada"]), (0, chip * ncol), (1, ncol))
    cond16, mod_cols = _ada_fwd(jnp.pad(c_all, ((0, 8), (0, 0))), sq(W["w_ada"]), b_cols)
    mod_all = _ag8("gather_mod", mod_cols[:8])
    mod_mine = lax.dynamic_slice(mod_all[0::2], (0, dev, 0), (4, 1, ncol)).reshape(1, 4 * ncol)
    shift1, scale1, gate1, shift2, scale2, gate2 = [mod_mine[:, i * D:(i + 1) * D] for i in range(6)]

    seg = _seg_matrix(WIDTH, HEAD_DIM)
    norm1 = [sq(W["norm1_w"]), shift1, scale1]
    h1 = _stage_fwd("norm_mod1", _fn_norm_mod, [_whole(x2)], norm1, [(D, bf16)], tt_l)[0]
    p = _matmul("in_proj", h1, win_p, "nn")

    cgq = [row(conv_gdn_f[j, part * WIDTH:(part + 1) * WIDTH]) for part in range(3) for j in range(4)]
    lane_pad = lambda a: jnp.pad(row(a), ((0, 0), (8, LANES - 16)))
    gdn_pre_ps = cgq + [lane_pad(W["a_log"]), lane_pad(W["dt_bias"]), seg, _chunk_tri(tt_h, CHUNK)]
    gdn_pre_ts = [(p, WIDTH, ci_, (0, 1, 2, 3)) for ci_ in _COL_QKV] + [(p, LANES, _COL_BA, None)]
    q_, k_, v_, beta_t, gc_t = _stage_fwd("gdn_pre", _fn_gdn_pre, gdn_pre_ts, gdn_pre_ps,
                                          [(WIDTH, f32)] * 3 + [(LANES, f32)] * 2, tt_h)
    o_, gdn_hist, gdn_inv, wbg_s, wbr_s, wout_s, wfi_s, wfo_s = _scan_fwd(
        "gdn_scan", _gdn_chunk, [q_, k_, v_], [beta_t, gc_t], job=later_weights)
    wout_f = wout_s.reshape(D, D)
    wfo = wfo_s.reshape(D_FF, D)
    ow512 = jnp.tile(row(W["onorm_gdn"]), (1, HEADS))
    gdn_post_ts = [_whole(o_), (p, WIDTH, _COL_Z, None)]
    ya = _stage_fwd("gdn_post", _fn_gdn_post, gdn_post_ts, [ow512, seg], [(WIDTH, bf16)], tt_l)[0]

    mu = sq(W["mu_rwkv"])
    rw_ps = [mu[:, 0:512], mu[:, 512:1024], mu[:, 1024:1536], mu[:, 1536:1664], jnp.pad(mu[:, 1664:1824], ((0, 0), (0, 96))),
             sq(W["w0"]), w2p, sq(W["a0"]), a2p, g2p, sq(W["k_k"]), sq(W["k_a"]), seg]
    rw_ts = [(p, WIDTH, ci_, (0, 1)) for ci_ in _COL_RKV] + [(p, LANES, _COL_LORA, (0, 1)),
                                                              (p, 256, _COL_GATE_LORA, (0, 1))]
    rw_out = _stage_fwd("rwkv_pre", _fn_rwkv_pre, rw_ts, rw_ps, [(WIDTH, f32)] * 7, tt_h)
    r_, lw_, k2_, vv_, na_, b_, g_ = rw_out
    rw_ins = [r_, lw_, k2_, vv_, na_, b_]
    y_, rw_hist, rw_inv = _scan_fwd("rwkv_scan", _rwkv_chunk, rw_ins)
    rwp_ps = [sq(W["lnx_w"]), sq(W["lnx_b"]), row(W["r_k"]), seg]
    rwp_ts = [_whole(y_), _whole(r_), _whole(k2_), _whole(vv_), _whole(g_)]
    yb = _stage_fwd("rwkv_post", _fn_rwkv_post, rwp_ts, rwp_ps, [(WIDTH, bf16)], tt_l)[0]

    big_a = _matmul("branch_gdn", ya, wbg_s, "nn", shards=4)
    big_b = _matmul("branch_rwkv", yb, wbr_s, "nn", shards=4)
    merge_ts = [(p, D, _COL_GL[0], None), (p, D, _COL_GL[1], None), _whole(big_a), _whole(big_b)]
    merged = _stage_fwd("merge", _fn_merge, merge_ts, [], [(D, bf16)], tt_l)[0]
    mo = _matmul("out_proj", merged, wout_f, "nn")
    norm2 = [gate1, sq(W["norm2_w"]), shift2, scale2]
    x1, h2 = _stage_fwd("resid_norm_mod2", _fn_resid_norm_mod, [_whole(x2), _whole(mo)], norm2, [(D, f32), (D, bf16)], tt_l)
    f = _matmul("ffn_in", h2, wfi_s, "nn", shards=4, tn=1408)
    cg_ps = [row(conv_ffn_f[j]) for j in range(3)]
    cg_ts = [(f, D_FF, 0, (0, 1, 2)), (f, D_FF, 1, None)]
    act = _stage_fwd("convglu", _fn_convglu, cg_ts, cg_ps, [(D_FF, bf16)], tt_h)[0]
    fo = _matmul("ffn_out", act, wfo, "nn", tk=1408)

    dx1_a, dfo, dgate2, dnormf, loss_part = _final_stage(x1, fo, tgt, gate2, row(W["norm_f_w"]), tt_l)

    dact = _matmul("d_act", dfo, wfo, "nt", tn=1408)
    g_wfo = _matmul("g_ffn_out", act, dfo, "tn", tm=1408)
    _, dcf, df = _stage_bwd("convglu_bwd", _fn_convglu, cg_ts, cg_ps, [[_whole(dact)]], tt_h, [True] * 2, [True] * 3,
                            joint=([0, 1], 2 * D_FF, 0, None), dtypes=[bf16])
    g_wfi = _matmul("g_ffn_in", h2, df, "tn", shards=4, tn=1408)
    g_wfo_s = g_wfo.reshape(4, D_FF // 4, D)
    dh2, recv_wfi, recv_wfo = _matmul("d_h2", df, wfi_s, "nt", shards=4, tk=1408, job=_sibling_swap_job([g_wfi, g_wfo_s]))
    (dx_a, dmo), (dgate1, dnorm2, dshift2, dscale2), _ = _stage_bwd(
        "resid_norm_mod2_bwd", _fn_resid_norm_mod, [_whole(x2), _whole(mo)], norm2,
        [[_whole(dx1_a)], [_whole(dh2)]], tt_l, [True, True], [True] * 4, dtypes=[f32, bf16])
    dmerged = _matmul("d_merged", dmo, wout_f, "nt")
    g_wout = _matmul("g_out_proj", merged, dmo, "tn")
    (dbig_a, dbig_b), _, dp = _stage_bwd("merge_bwd", _fn_merge, merge_ts, [], [[_whole(dmerged)]], tt_l, [True] * 4, [],
                                         joint=([0, 1], p.shape[1], _JOINT_GL, None), dtypes=[bf16] * 3)
    g_wbg = _matmul("g_branch_gdn", ya, dbig_a, "tn", shards=4)
    g_wbr = _matmul("g_branch_rwkv", yb, dbig_b, "tn", shards=4)
    g_wout_s = g_wout.reshape(4, D // 4, D)
    dyb = _matmul("d_yb", dbig_b, wbr_s, "nt", shards=4)
    dya, *recv_mix = _matmul("d_ya", dbig_a, wbg_s, "nt", shards=4, job=_sibling_swap_job([g_wbg, g_wbr, g_wout_s]))

    (dy_, dr_p, dk2_p, dv_p, dg_p), (dlnxw, dlnxb, drk), _ = _stage_bwd(
        "rwkv_post_bwd", _fn_rwkv_post, rwp_ts, rwp_ps, [[_whole(dyb)]], tt_l, [True] * 5, [True, True, True, False])
    gs_a = [g_wbg, g_wbr, g_wout_s, g_wfi, g_wfo_s]
    pairs_a = [_add_half("grads_pair_sum%d" % (t + 1), g, r_, chip, ci)
               for t, (g, r_) in enumerate(zip(gs_a, recv_mix + [recv_wfi, recv_wfo]))]
    dr_c, dlw_c, dk2_c, dv_c, dna_c, db_c, *others_a = _scan_bwd(
        "rwkv_scan_bwd", _rwkv_chunk, rw_ins, [], rw_hist, rw_inv, dy_, job=_chip_exchange_job([pb for _, pb in pairs_a]))
    rw_cots = [[_whole(dr_p), _whole(dr_c)], [_whole(dlw_c)], [_whole(dk2_p), _whole(dk2_c)],
               [_whole(dv_p), _whole(dv_c)], [_whole(dna_c)], [_whole(db_c)], [_whole(dg_p)]]
    (dl_, dg_), rw_dp, dp = _stage_bwd("rwkv_pre_bwd", _fn_rwkv_pre, rw_ts, rw_ps, rw_cots, tt_h, [True] * 5,
                                       [True] * 12 + [False], joint=([0, 1, 2], p.shape[1], _JOINT_RKV, dp), dtypes=[bf16] * 3)
    dmu_r, dmu_k, dmu_v, dmu_l, dmu_g, dw0, dw2p, da0, da2p, dg2p, dkk, dka = rw_dp

    (do_,), (dow512,), dp = _stage_bwd("gdn_post_bwd", _fn_gdn_post, gdn_post_ts, [ow512, seg], [[_whole(dya)]], tt_l,
                                       [True, True], [True, False], joint=([1], p.shape[1], _JOINT_Z, dp), dtypes=[f32, bf16])
    d_gdn = _scan_bwd("gdn_scan_bwd", _gdn_chunk, [q_, k_, v_], [beta_t, gc_t], gdn_hist, gdn_inv, do_)
    gdn_cots = [[_whole(a)] for a in d_gdn]
    (dba,), gdn_dp, dp = _stage_bwd("gdn_pre_bwd", _fn_gdn_pre, gdn_pre_ts, gdn_pre_ps, gdn_cots, tt_h, [True] * 4,
                                    [True] * 14 + [False, False], joint=([0, 1, 2], p.shape[1], _JOINT_QKV, dp), dtypes=[bf16] * 2)
    dp = lax.dynamic_update_slice(dp, jnp.concatenate([dl_, dba, dg_], axis=1), (0, _SMALL_BLOCKS_AT))
    g_conv_gdn = jnp.concatenate([jnp.concatenate([gdn_dp[4 * part + j] for part in range(3)], axis=1) for j in range(4)], axis=0)
    g_conv_ffn = jnp.concatenate(dcf, axis=0)
    g_mu = jnp.concatenate([dmu_r, dmu_k, dmu_v, dmu_l, dmu_g[:, :160]], axis=1)
    late_zero = jnp.zeros((1, D), f32)
    dmod_early = jnp.concatenate([late_zero, late_zero, dgate1, dshift2, dscale2, dgate2], axis=1)
    small_parts = {"b_ada": dmod_early, "norm1_w": late_zero, "a_log": gdn_dp[12][:, 8:16], "dt_bias": gdn_dp[13][:, 8:16],
                   "mu_rwkv": g_mu, "w0": dw0, "a0": da0, "k_k": dkk, "k_a": dka, "r_k": drk, "lnx_w": dlnxw,
                   "lnx_b": dlnxb, "norm2_w": dnorm2, "norm_f_w": dnormf}
    small_names = [n for n in _SMALL if n != "onorm_gdn"]
    mid_full = [g_conv_gdn, g_conv_ffn, dw2p[0:64], da2p[64:128], dg2p[0:160]]
    body_rows = _pack([small_parts[n] for n in small_names] + [loss_part[:, 0:1]] + mid_full, 1, f32)
    head_row = body_rows.shape[0]
    small_g = jnp.concatenate([body_rows, jnp.pad(dow512, ((0, 0), (0, PACK_W - WIDTH)))], axis=0)
    small_g = jnp.pad(small_g, ((0, -small_g.shape[0] % 8), (0, 0)))

    g_win_pad, small_all_g = _matmul("g_in_proj", h1, dp, "tn", job=_ag8_job(small_g))
    g_win_s = _win_unpad_shards(g_win_pad, shapes["w_in"][2])
    pair_win = _add_half("grads_pair_sum0", g_win_s, _run_job("w_in_grads_sibling_swap", _sibling_swap_job([g_win_s]))[0],
                         chip, ci)
    dh1, others_win = _matmul("d_h1", dp, win_p, "nt", job=_chip_exchange_job([pair_win[1]]))
    (grad_x,), (dnorm1, dshift1, dscale1), _ = _stage_bwd("norm_mod1_bwd", _fn_norm_mod_and_x, [_whole(x2)], norm1,
                                                          [[_whole(dh1)], [_whole(dx_a)]], tt_l, [True], [True] * 3)
    late_all = _ag8("gather_late_grads", jnp.pad(jnp.concatenate([dshift1, dscale1, dnorm1], axis=0), ((0, 5), (0, 0))))
    small_sum, head_sum = _sum_devices(small_all_g, late_all, (0, 1, 6), head_row)
    small_shapes = [shapes[n][1:] if n != "norm_f_w" else shapes[n] for n in small_names]
    un = _unpack(small_sum, small_shapes + [(1,)] + [g.shape for g in mid_full])
    small_grads = dict(zip(small_names, un))
    loss = un[len(small_names)].reshape(())
    small_grads["onorm_gdn"] = head_sum[0, 0:HEAD_DIM]
    for n, g in zip(_MID, un[len(small_names) + 1:]):
        wcols = shapes[n][2]
        small_grads[n] = lax.dynamic_slice(g, (0, chip * wcols), (g.shape[0], wcols))

    dmod_all = jnp.concatenate([late_all[:, 0, :], late_all[:, 1, :], small_all_g[:, 2:6, :].reshape(8, 4 * PACK_W)], axis=1)
    dmod_cols = lax.dynamic_slice(dmod_all, (0, chip * ncol), (8, ncol))
    g_wada = _matmul("g_w_ada", cond16, jnp.pad(dmod_cols, ((0, 8), (0, 0))), "tn")

    pairs = [pair_win] + pairs_a
    others = [others_win] + others_a
    halves = [_sum_chip("grads_chip_sum%d" % t, own, o_, ci) for t, ((own, _), o_) in enumerate(zip(pairs, others))]
    big_grads = dict(zip(_BIG, _rs_sibling_join(halves)))

    res = {tag: {} for tag in ("grad", "delta", "new_m", "new_v")}

    def put(n, g, d, m_, v_):
        for tag, val in zip(("grad", "delta", "new_m", "new_v"), (g, d, m_, v_)):
            res[tag][n] = val.reshape(shapes[n])

    for n in _BIG:
        put(n, big_grads[n], *_adamw("adamw_" + n, sq(W[n]), big_grads[n], sq(Mo[n]), sq(Vo[n])))
    put("w_ada", g_wada, *_adamw("adamw_w_ada", sq(W["w_ada"]), g_wada, sq(Mo["w_ada"]), sq(Vo["w_ada"])))
    rest = _SMALL + _MID
    pk = lambda d: _pack([d[n] for n in rest], 8, f32)
    sg = pk(small_grads)
    sm = _adamw("adamw_small", pk(W), sg, pk(Mo), pk(Vo))
    for tag, buf in zip(("grad", "delta", "new_m", "new_v"), (sg,) + tuple(sm)):
        res[tag].update(zip(rest, _unpack(buf, [shapes[n] for n in rest])))
    outs = [loss, grad_x.reshape(x.shape)]
    for tag in ("grad", "delta", "new_m", "new_v"):
        outs += [res[tag][n] for n in _ORDER]
    return tuple(outs)
```

```python
import numpy as np
import jax
import jax.numpy as jnp
from jax import lax
from jax.experimental import pallas as pl
from jax.experimental.pallas import tpu as pltpu

f32 = jnp.float32
bf16 = jnp.bfloat16

LANES = 128
HEADS = 8
HEAD_DIM = 64
WIDTH = HEADS * HEAD_DIM
CHUNK = 64
D_FF = 2816
NORM_EPS = 1e-6
LNX_EPS = 64e-5
PACK_W = 1024
MESH_ID = pl.DeviceIdType.MESH

ADAM_LR, ADAM_B1, ADAM_B2, ADAM_EPS, ADAM_WD, ADAM_STEP = 0.001, 0.9, 0.999, 1e-08, 0.01, 10


def _pick(n, target, mult):
    if n <= target:
        return n
    best = None
    for t in range(mult, target + 1, mult):
        if n % t == 0:
            best = t
    assert best is not None, (n, target, mult)
    return best


def _split_bf16(x, n):
    parts, r = [], x
    for i in range(n):
        p = r.astype(bf16)
        parts.append(p)
        if i + 1 < n:
            r = r - p.astype(f32)
    return parts


def _xdot_r_impl(x, m, n, dims):
    acc = None
    for p in _split_bf16(x, n):
        t = lax.dot_general(p, m, dims, preferred_element_type=f32)
        acc = t if acc is None else acc + t
    return acc


def _make_xdot_r(n):
    nn = (((1,), (0,)), ((), ()))
    nt = (((1,), (1,)), ((), ()))

    @jax.custom_vjp
    def xdot(x, m):
        return _xdot_r_impl(x, m, n, nn)

    def fwd(x, m):
        return _xdot_r_impl(x, m, n, nn), m

    def bwd(m, ct):
        return _xdot_r_impl(ct, m, n, nt), jnp.zeros_like(m)

    xdot.defvjp(fwd, bwd)
    return xdot


_segsum = _make_xdot_r(2)


def _xdot_l_impl(m, x, n, dims):
    acc = None
    for p in _split_bf16(x, n):
        t = lax.dot_general(m, p, dims, preferred_element_type=f32)
        acc = t if acc is None else acc + t
    return acc


@jax.custom_vjp
def _xdot_l(m, x):
    return _xdot_l_impl(m, x, 3, (((1,), (0,)), ((), ())))


def _xdot_l_fwd(m, x):
    return _xdot_l(m, x), m


def _xdot_l_bwd(m, ct):
    return jnp.zeros_like(m), _xdot_l_impl(m, ct, 3, (((0,), (0,)), ((), ())))


_xdot_l.defvjp(_xdot_l_fwd, _xdot_l_bwd)


@jax.custom_vjp
def _bdot(x, w):
    return jnp.dot(x.astype(bf16), w.astype(bf16), preferred_element_type=f32)


def _bdot_fwd(x, w):
    return _bdot(x, w), (x, w)


def _bdot_bwd(res, ct):
    x, w = res
    c = ct.astype(bf16)
    dx = lax.dot_general(c, w.astype(bf16), (((1,), (1,)), ((), ())), preferred_element_type=f32)
    dw = lax.dot_general(x.astype(bf16), c, (((0,), (0,)), ((), ())), preferred_element_type=f32)
    return dx, dw


_bdot.defvjp(_bdot_fwd, _bdot_bwd)


def _silu(x):
    return x * jax.nn.sigmoid(x)


def _softplus(x):
    return jnp.maximum(x, 0.0) + jnp.log(1.0 + jnp.exp(-jnp.abs(x)))


def _rms(x, w, eps):
    return x * lax.rsqrt(jnp.mean(x * x, axis=-1, keepdims=True) + eps) * w


def _seg_matrix(width, seg):
    i = np.arange(width)
    return jnp.asarray((i[:, None] // seg) == (i[None, :] // seg), dtype=bf16)


def _chunk_tri(rows, chunk):
    i = np.arange(rows)
    return jnp.asarray(((i[:, None] // chunk) == (i[None, :] // chunk)) & (i[:, None] >= i[None, :]), dtype=bf16)


HALO = 8


def _full_spec(shape):
    nd = len(shape)
    return pl.BlockSpec(shape, lambda i: (0,) * nd)


def _entry_specs(entries, tt, block_of):
    specs, ops = [], []
    for arr, w, ci, shifts in entries:
        specs.append(pl.BlockSpec((tt, w), lambda i, ci=ci: (block_of(i), ci)))
        ops.append(arr)
        if shifts:
            specs.append(pl.BlockSpec((HALO, w), lambda i, ci=ci: (jnp.maximum(block_of(i) * (tt // HALO) - 1, 0), ci)))
            ops.append(arr)
    return specs, ops


def _load_entries(entries, refs, first):
    tiles, k = [], 0
    for _, w, _, shifts in entries:
        x = refs[k][...].astype(f32)
        k += 1
        if not shifts:
            tiles.append(x)
            continue
        halo = jnp.where(first, 0.0, refs[k][...].astype(f32))
        k += 1
        row = lax.broadcasted_iota(jnp.int32, (HALO, w), 0)
        for s in shifts:
            if s == 0:
                tiles.append(x)
                continue
            r = pltpu.roll(x, s, 0)
            head = jnp.where(row < s, pltpu.roll(halo, s, 0), r[0:HALO])
            tiles.append(jnp.concatenate([head, r[HALO:]], axis=0))
    return tiles


def _unshift_sum(grads, shifts, carry, tt):
    w = grads[0].shape[1]
    row = lax.broadcasted_iota(jnp.int32, (tt, w), 0)
    row8 = lax.broadcasted_iota(jnp.int32, (HALO, w), 0)
    dx, out = None, jnp.zeros((HALO, w), f32)
    for d, s in zip(grads, shifts):
        if s == 0:
            part = d
        else:
            part = jnp.where(row < tt - s, pltpu.roll(d, tt - s, 0), 0.0)
            out = out + jnp.where(row8 >= HALO - s, pltpu.roll(d[0:HALO], HALO - s, 0), 0.0)
        dx = part if dx is None else dx + part
    return jnp.concatenate([dx[:tt - HALO], dx[tt - HALO:] + carry], axis=0), out


def _stage_fwd(name, fn, tiles, params, outs, tt):
    rows = tiles[0][0].shape[0]
    npar = len(params)
    specs, ops = _entry_specs(tiles, tt, lambda i: i)
    nin = len(ops)

    def body(*refs):
        ts = _load_entries(tiles, refs[:nin], pl.program_id(0) == 0)
        ps = [r[...] for r in refs[nin:nin + npar]]
        res = fn(ps, ts)
        for r, v in zip(refs[nin + npar:], res):
            r[...] = v.astype(r.dtype)

    return pl.pallas_call(
        body, grid=(rows // tt,),
        in_specs=specs + [_full_spec(p.shape) for p in params],
        out_specs=[pl.BlockSpec((tt, w), lambda i: (i, 0)) for (w, _) in outs],
        out_shape=[jax.ShapeDtypeStruct((rows, w), dt) for (w, dt) in outs],
        compiler_params=pltpu.CompilerParams(dimension_semantics=("parallel",)),
        name=name,
    )(*ops, *params)


def _stage_bwd(name, fn, tiles, params, cots, tt, tile_grad, param_grad, joint=None, dtypes=None):
    rows = tiles[0][0].shape[0]
    nblk = rows // tt
    npar = len(params)
    block_of = lambda i: nblk - 1 - i
    specs, ops = _entry_specs(tiles, tt, block_of)
    nin = len(ops)
    flat_cots = [c for group in cots for c in group]
    groups = [len(g) for g in cots]
    ncot = len(flat_cots)
    counts = [len(e[3]) if e[3] else 1 for e in tiles]
    dt_entries = [e for e, g in zip(tiles, tile_grad) if g]
    dp_shapes = [p.shape for p, g in zip(params, param_grad) if g]
    ndt = len(dt_entries)
    carry_w = [e[1] for e in dt_entries if e[3]]
    flags = [g for g, n in zip(tile_grad, counts) for _ in range(n)]
    members, j_width, j_cidx, j_buf = joint if joint else ([], 0, 0, None)
    solo = [k for k in range(ndt) if k not in members]
    nsolo, njoint, nbuf = len(solo), int(bool(members)), int(j_buf is not None)
    j_block = sum(dt_entries[k][1] for k in members)
    dtypes = list(dtypes) if dtypes else [f32] * (nsolo + njoint)

    def body(*refs):
        i = pl.program_id(0)
        p_refs = refs[nin:nin + npar]
        c_refs = refs[nin + npar:nin + npar + ncot]
        base = nin + npar + ncot + nbuf
        dt_refs = refs[base:base + nsolo]
        joint_refs = refs[base + nsolo:base + nsolo + njoint]
        dp_refs = refs[base + nsolo + njoint:base + nsolo + njoint + len(dp_shapes)]
        carry_refs = refs[base + nsolo + njoint + len(dp_shapes):]
        ts = _load_entries(tiles, refs[:nin], block_of(i) == 0)
        ps = [r[...] for r in p_refs]

        def f(dp, dt):
            dp, dt = iter(dp), iter(dt)
            pp = [next(dp) if g else p for p, g in zip(ps, param_grad)]
            tl = [next(dt) if g else t for t, g in zip(ts, flags)]
            return fn(pp, tl)

        _, vjp = jax.vjp(f, [p for p, g in zip(ps, param_grad) if g], [t for t, g in zip(ts, flags) if g])
        cs, j = [], 0
        for n in groups:
            acc = c_refs[j][...].astype(f32)
            for q in range(1, n):
                acc = acc + c_refs[j + q][...].astype(f32)
            cs.append(acc)
            j += n
        gp, gt = vjp(cs)

        @pl.when(i == 0)
        def _():
            for r in dp_refs:
                r[...] = jnp.zeros_like(r)
            for r in carry_refs:
                r[...] = jnp.zeros_like(r)

        gt, k, kc, dxs = list(gt), 0, 0, []
        for e, n in zip(dt_entries, [n for n, g in zip(counts, tile_grad) if g]):
            if e[3]:
                dx, out = _unshift_sum(gt[k:k + n], e[3], carry_refs[kc][...], tt)
                carry_refs[kc][...] = out
                kc += 1
            else:
                dx = gt[k]
            dxs.append(dx)
            k += n
        for r, k in zip(dt_refs, solo):
            r[...] = dxs[k].astype(r.dtype)
        off = 0
        for k in members:
            w = dt_entries[k][1]
            joint_refs[0][:, off:off + w] = dxs[k].astype(joint_refs[0].dtype)
            off += w
        for r, v in zip(dp_refs, gp):
            r[...] += v

    res = pl.pallas_call(
        body, grid=(nblk,),
        in_specs=specs + [_full_spec(p.shape) for p in params]
        + [pl.BlockSpec((tt, w), lambda i, ci=ci: (block_of(i), ci)) for (_, w, ci, *_) in flat_cots]
        + [pl.BlockSpec(memory_space=pl.ANY)] * nbuf,
        out_specs=[pl.BlockSpec((tt, dt_entries[k][1]), lambda i: (block_of(i), 0)) for k in solo]
        + [pl.BlockSpec((tt, j_block), lambda i: (block_of(i), j_cidx))] * njoint
        + [_full_spec(s) for s in dp_shapes],
        out_shape=[jax.ShapeDtypeStruct((rows, dt_entries[k][1]), dt) for k, dt in zip(solo, dtypes)]
        + [jax.ShapeDtypeStruct((rows, j_width), dtypes[-1])] * njoint
        + [jax.ShapeDtypeStruct(s, f32) for s in dp_shapes],
        scratch_shapes=[pltpu.VMEM((HALO, w), f32) for w in carry_w],
        input_output_aliases={nin + npar + ncot: nsolo} if nbuf else {},
        compiler_params=pltpu.CompilerParams(dimension_semantics=("arbitrary",)),
        name=name,
    )(*ops, *params, *[c[0] for c in flat_cots], *([j_buf] if nbuf else []))
    res = list(res)
    return res[:nsolo], res[nsolo + njoint:], (res[nsolo] if njoint else None)


def _whole(a):
    return (a, a.shape[1], 0, None)


def _matmul(name, a, b, mode, out_dtype=f32, tm=1024, tn=1024, tk=1024, shards=1, job=None):
    S = shards
    if mode == "nn":
        M, K = a.shape
        w = b.shape[-1]
    elif mode == "nt":
        M = a.shape[0]
        if S > 1:
            _, N, w = b.shape
            K = S * w
        else:
            N, K = b.shape
            w = K
    else:
        K, M = a.shape
        w = b.shape[1] // S
    if mode != "nt":
        N = S * w
    tm = _pick(M, tm, LANES)
    if mode == "nt":
        tn = _pick(N, tn, LANES)
        tk = _pick(w, tk, LANES)
    else:
        tn = _pick(w, tn, LANES)
        tk = _pick(K, tk, LANES if mode == "nn" else 16)
    nk = K // tk
    nb = w // (tk if mode == "nt" else tn)
    if mode == "nn":
        a_spec = pl.BlockSpec((tm, tk), lambda i, j, k: (i, k))
        if S > 1:
            b_spec = pl.BlockSpec((1, tk, tn), lambda i, j, k: (j // nb, k, j % nb))
        else:
            b_spec = pl.BlockSpec((tk, tn), lambda i, j, k: (k, j))
        dims = (((1,), (0,)), ((), ()))
    elif mode == "nt":
        a_spec = pl.BlockSpec((tm, tk), lambda i, j, k: (i, k))
        if S > 1:
            b_spec = pl.BlockSpec((1, tn, tk), lambda i, j, k: (k // nb, j, k % nb))
        else:
            b_spec = pl.BlockSpec((tn, tk), lambda i, j, k: (j, k))
        dims = (((1,), (1,)), ((), ()))
    else:
        a_spec = pl.BlockSpec((tk, tm), lambda i, j, k: (k, i))
        b_spec = pl.BlockSpec((tk, tn), lambda i, j, k: (k, j))
        dims = (((0,), (0,)), ((), ()))
    if mode == "tn" and S > 1:
        o_spec = pl.BlockSpec((1, tm, tn), lambda i, j, k: (j // nb, i, j % nb))
        o_shape = (S, M, w)
    else:
        o_spec = pl.BlockSpec((tm, tn), lambda i, j, k: (i, j))
        o_shape = (M, N)
    b_lead = S > 1 and mode != "tn"
    o_lead = S > 1 and mode == "tn"

    j_ins, j_outs, j_sems = _job_parts(job)
    nji, njo = len(j_ins), len(j_outs)
    grid = (M // tm, N // tn, nk)

    def run_job(refs):
        step = (pl.program_id(0) * grid[1] + pl.program_id(1)) * grid[2] + pl.program_id(2)
        _job_steps(job, refs[2:2 + nji], refs[3 + nji:3 + nji + njo], refs[len(refs) - len(j_sems):], step,
                   grid[0] * grid[1] * grid[2] - 1)

    def body(*refs):
        a_ref, b_ref, o_ref, acc_ref = refs[0], refs[1], refs[2 + nji], refs[3 + nji + njo]
        run_job(refs)
        k = pl.program_id(2)

        @pl.when(k == 0)
        def _():
            acc_ref[...] = jnp.zeros_like(acc_ref)

        bv = b_ref[0] if b_lead else b_ref[...]
        acc_ref[...] += lax.dot_general(a_ref[...].astype(bf16), bv.astype(bf16), dims, preferred_element_type=f32)

        @pl.when(k == nk - 1)
        def _():
            if o_lead:
                o_ref[0] = acc_ref[...].astype(o_ref.dtype)
            else:
                o_ref[...] = acc_ref[...].astype(o_ref.dtype)

    def body_one_step(*refs):
        a_ref, b_ref, o_ref = refs[0], refs[1], refs[2 + nji]
        run_job(refs)
        bv = b_ref[0] if b_lead else b_ref[...]
        res = lax.dot_general(a_ref[...].astype(bf16), bv.astype(bf16), dims, preferred_element_type=f32)
        if o_lead:
            o_ref[0] = res.astype(o_ref.dtype)
        else:
            o_ref[...] = res.astype(o_ref.dtype)

    res = pl.pallas_call(
        body if nk > 1 else body_one_step, grid=grid,
        in_specs=[a_spec, b_spec] + [_ANY] * nji,
        out_specs=[o_spec] + [_ANY] * njo,
        out_shape=[jax.ShapeDtypeStruct(o_shape, out_dtype)] + j_outs,
        scratch_shapes=([pltpu.VMEM((tm, tn), f32)] if nk > 1 else []) + j_sems,
        compiler_params=pltpu.CompilerParams(
            dimension_semantics=("arbitrary",) * 3 if job else ("parallel", "parallel", "arbitrary")),
        name=name,
    )(a, b, *j_ins)
    return res if job else res[0]


def _make_bmm(precision):
    if precision is None:
        cast, kw = (lambda v: v.astype(bf16)), {}
    else:
        cast, kw = (lambda v: v), {"precision": precision}

    def nn(a, b):
        return jnp.einsum("hij,hjk->hik", cast(a), cast(b), preferred_element_type=f32, **kw)

    def nt(a, b):
        return jnp.einsum("hik,hjk->hij", cast(a), cast(b), preferred_element_type=f32, **kw)

    def tn(a, b):
        return jnp.einsum("hki,hkj->hij", cast(a), cast(b), preferred_element_type=f32, **kw)

    if precision is not None:
        return nn, nt, tn
    nn_v, nt_v, tn_v = jax.custom_vjp(nn), jax.custom_vjp(nt), jax.custom_vjp(tn)
    keep = lambda f: (lambda a, b: (f(a, b), (a, b)))
    nn_v.defvjp(keep(nn), lambda r, ct: (nt(ct, r[1]), tn(r[0], ct)))
    nt_v.defvjp(keep(nt), lambda r, ct: (nn(ct, r[1]), tn(ct, r[0])))
    tn_v.defvjp(keep(tn), lambda r, ct: (nt(r[1], ct), nn(r[0], ct)))
    return nn_v, nt_v, tn_v


_bmm, _bmm_nt, _bmm_tn = _make_bmm(None)
_bmm_exact = _make_bmm(lax.Precision.HIGH)[0]


def _masks(n):
    r = lax.broadcasted_iota(jnp.int32, (n, n), 0)
    c = lax.broadcasted_iota(jnp.int32, (n, n), 1)
    return (r >= c)[None], (r > c)[None], (r == c)[None]


_INV_BLOCK = 8


def _nilpotent_inverse(m, eye):
    p = eye + m
    for _ in range(2):
        m = _bmm(m, m)
        p = p + _bmm(p, m)
    return p


def _neumann_inverse_impl(m):
    n = m.shape[1]
    assert n == _INV_BLOCK * _INV_BLOCK
    r = lax.broadcasted_iota(jnp.int32, (n, n), 0)
    c = lax.broadcasted_iota(jnp.int32, (n, n), 1)
    eye = (r == c).astype(f32)[None]
    inside = jnp.where((r // _INV_BLOCK == c // _INV_BLOCK)[None], m, 0.0)
    d_inv = _nilpotent_inverse(inside, eye)
    return _bmm(_nilpotent_inverse(_bmm(d_inv, m - inside), eye), d_inv)


@jax.custom_vjp
def _neumann_inverse(m):
    return _neumann_inverse_impl(m)


def _neumann_inverse_fwd(m):
    p = _neumann_inverse_impl(m)
    return p, p


def _neumann_inverse_bwd(p, ct):
    return (_bmm_tn(p, _bmm_nt(ct, p)),)


_neumann_inverse.defvjp(_neumann_inverse_fwd, _neumann_inverse_bwd)


@jax.custom_vjp
def _given_inverse(m, p):
    return p


_given_inverse.defvjp(lambda m, p: (p, p), lambda p, ct: (_neumann_inverse_bwd(p, ct)[0], jnp.zeros_like(p)))


def _gdn_chunk(s, q, k, v, beta, gc, gr, gl, p=None):
    n = q.shape[1]
    causal, strict, _ = _masks(n)
    decay = jnp.where(causal, jnp.exp(jnp.where(causal, gc - gr, 0.0)), 0.0)
    kb = k * beta
    vb = v * beta
    lower = jnp.where(strict, _bmm_nt(kb, k) * decay, 0.0)
    t_mat = _neumann_inverse(-lower) if p is None else _given_inverse(-lower, p)
    egc = jnp.exp(gc)
    u = _bmm(t_mat, vb)
    w = _bmm(t_mat, kb * egc)
    attn = jnp.where(causal, _bmm_nt(q, k) * decay, 0.0)
    v_new = u - _bmm(w, s)
    o = _bmm(q * egc, s) + _bmm(attn, v_new)
    k_dec = k * jnp.exp(gl - gc)
    s_new = s * jnp.exp(gl) + _bmm_tn(k_dec, v_new)
    return s_new, o, t_mat


def _rwkv_chunk(s, r, lw, k, v, a, b, p=None):
    n = r.shape[1]
    causal, strict, _ = _masks(n)
    tri = jnp.broadcast_to(causal.astype(f32), (r.shape[0], n, n))
    lc = _bmm_exact(tri, lw)
    ein = jnp.exp(lc)
    eout = jnp.exp(-lc)
    a_t = a * jnp.exp(lc - lw)
    b_t = b * eout
    k_t = k * eout
    r_t = r * ein
    a_ab = jnp.where(strict, _bmm_nt(a_t, b_t), 0.0)
    a_ak = jnp.where(strict, _bmm_nt(a_t, k_t), 0.0)
    inv = _neumann_inverse(a_ab) if p is None else _given_inverse(a_ab, p)
    u = _bmm(inv, _bmm_nt(a_t, s) + _bmm(a_ak, v))
    y = (_bmm_nt(r_t, s) + _bmm(jnp.where(causal, _bmm_nt(r_t, b_t), 0.0), u)
         + _bmm(jnp.where(causal, _bmm_nt(r_t, k_t), 0.0), v))
    e_last = jnp.exp(jnp.sum(lw, axis=1, keepdims=True))
    s_new = s * e_last + _bmm_tn(u, b_t * e_last) + _bmm_tn(v, k_t * e_last)
    return s_new, y, inv


def _heads_in(ref, rows):
    return jnp.stack([ref[rows, h * HEAD_DIM:(h + 1) * HEAD_DIM] for h in range(HEADS)], axis=0)


def _heads_out(ref, rows, val):
    for h in range(HEADS):
        ref[rows, h * HEAD_DIM:(h + 1) * HEAD_DIM] = val[h]


def _gdn_scalars(bt, gt):
    n = bt.shape[0]
    gtt = gt.T
    hs = range(HEADS)
    return [jnp.stack([bt[:, h:h + 1] for h in hs], axis=0),
            jnp.stack([gt[:, HEADS + h:HEADS + h + 1] for h in hs], axis=0),
            jnp.stack([gtt[HEADS + h:HEADS + h + 1, :] for h in hs], axis=0),
            jnp.stack([gt[n - 1:n, HEADS + h:HEADS + h + 1] for h in hs], axis=0)]


def _gdn_scalars_back(dbeta, dgc, dgr, dgl):
    n = dbeta.shape[1]
    lane = lax.broadcasted_iota(jnp.int32, (n, LANES), 1)
    row = lax.broadcasted_iota(jnp.int32, (n, LANES), 0)
    sub = lax.broadcasted_iota(jnp.int32, (LANES, n), 0)
    db = jnp.zeros((n, LANES), f32)
    dg = jnp.zeros((n, LANES), f32)
    dgt = jnp.zeros((LANES, n), f32)
    for h in range(HEADS):
        db = jnp.where(lane == h, dbeta[h], db)
        dg = jnp.where(lane == HEADS + h, dgc[h] + jnp.where(row == n - 1, dgl[h], 0.0), dg)
        dgt = jnp.where(sub == HEADS + h, dgr[h], dgt)
    return [db, dg + dgt.T]


SCAN_GROUP = 2


def _scan_steps(t):
    g = SCAN_GROUP if (t // CHUNK) % SCAN_GROUP == 0 else 1
    return g, t // (CHUNK * g)


def _scan_spec(width, g, n, reverse):
    if reverse:
        return pl.BlockSpec((g * CHUNK, width), lambda i: (n - 1 - i, 0))
    return pl.BlockSpec((g * CHUNK, width), lambda i: (i, 0))


def _hist_spec(g, n, reverse):
    blk = (g, HEADS, HEAD_DIM, HEAD_DIM)
    if reverse:
        return pl.BlockSpec(blk, lambda i: (n - 1 - i, 0, 0, 0))
    return pl.BlockSpec(blk, lambda i: (i, 0, 0, 0))


def _job_list(job):
    return [] if job is None else (list(job) if isinstance(job, (list, tuple)) else [job])


def _job_parts(job):
    jobs = _job_list(job)
    sems = [pltpu.SemaphoreType.DMA((j["nsem"],)) for j in jobs for _ in range(2)]
    return [a for j in jobs for a in j["ins"]], [o for j in jobs for o in j["out_shapes"]], sems


def _job_phase(job, phase, in_refs, out_refs, sems):
    ki = ko = 0
    for n, j in enumerate(_job_list(job)):
        ni, no = len(j["ins"]), len(j["out_shapes"])
        j[phase](in_refs[ki:ki + ni], out_refs[ko:ko + no], sems[2 * n], sems[2 * n + 1])
        ki, ko = ki + ni, ko + no


def _job_steps(job, in_refs, out_refs, sems, step, last):
    if job is None:
        return

    @pl.when(step == 0)
    def _():
        _job_phase(job, "start", in_refs, out_refs, sems)

    @pl.when(step == last)
    def _():
        _job_phase(job, "finish", in_refs, out_refs, sems)


def _scan_fwd(name, fn, rows_in, scal_in=(), job=None):
    t = rows_in[0].shape[0]
    grp, n = _scan_steps(t)
    nr, ns = len(rows_in), len(scal_in)
    j_ins, j_outs, j_sems = _job_parts(job)
    nji, njo = len(j_ins), len(j_outs)

    def body(*refs):
        o_ref, sh_ref, ph_ref = refs[nr + ns + nji:nr + ns + nji + 3]
        s_scr = refs[nr + ns + nji + 3 + njo]
        _job_steps(job, refs[nr + ns:nr + ns + nji], refs[nr + ns + nji + 3:nr + ns + nji + 3 + njo],
                   refs[nr + ns + nji + 3 + njo + 1:], pl.program_id(0), n - 1)

        @pl.when(pl.program_id(0) == 0)
        def _():
            s_scr[...] = jnp.zeros_like(s_scr)

        s = s_scr[...]
        for sub in range(grp):
            rows = slice(sub * CHUNK, (sub + 1) * CHUNK)
            sh_ref[sub] = s
            ins = [_heads_in(r, rows) for r in refs[:nr]]
            if ns:
                ins += _gdn_scalars(*[r[rows, :] for r in refs[nr:nr + ns]])
            s, o, p = fn(s, *ins)
            _heads_out(o_ref, rows, o)
            ph_ref[sub] = p
        s_scr[...] = s

    return pl.pallas_call(
        body, grid=(n,),
        in_specs=[_scan_spec(a.shape[1], grp, n, False) for a in (*rows_in, *scal_in)] + [_ANY] * nji,
        out_specs=[_scan_spec(WIDTH, grp, n, False), _hist_spec(grp, n, False), _hist_spec(grp, n, False)] + [_ANY] * njo,
        out_shape=[jax.ShapeDtypeStruct((t, WIDTH), f32)]
        + [jax.ShapeDtypeStruct((t // CHUNK, HEADS, HEAD_DIM, HEAD_DIM), f32)] * 2 + j_outs,
        scratch_shapes=[pltpu.VMEM((HEADS, HEAD_DIM, HEAD_DIM), f32)] + j_sems,
        compiler_params=pltpu.CompilerParams(dimension_semantics=("arbitrary",)),
        name=name,
    )(*rows_in, *scal_in, *j_ins)


def _scan_bwd(name, fn, rows_in, scal_in, s_hist, p_hist, d_out, job=None):
    t = rows_in[0].shape[0]
    grp, n = _scan_steps(t)
    nr, ns = len(rows_in), len(scal_in)
    j_ins, j_outs, j_sems = _job_parts(job)
    nji, njo = len(j_ins), len(j_outs)

    def body(*refs):
        sh_ref, ph_ref, do_ref = refs[nr + ns:nr + ns + 3]
        base = nr + ns + 3 + nji
        g_refs = refs[base:base + nr + ns]
        ds_scr = refs[base + nr + ns + njo]
        _job_steps(job, refs[nr + ns + 3:base], refs[base + nr + ns:base + nr + ns + njo],
                   refs[base + nr + ns + njo + 1:], pl.program_id(0), n - 1)

        @pl.when(pl.program_id(0) == 0)
        def _():
            ds_scr[...] = jnp.zeros_like(ds_scr)

        ds = ds_scr[...]
        for sub in reversed(range(grp)):
            rows = slice(sub * CHUNK, (sub + 1) * CHUNK)
            ins = [_heads_in(r, rows) for r in refs[:nr]]
            if ns:
                ins += _gdn_scalars(*[r[rows, :] for r in refs[nr:nr + ns]])
            p = ph_ref[sub]
            _, vjp = jax.vjp(lambda s, *a, p=p: fn(s, *a, p=p)[:2], sh_ref[sub], *ins)
            g = vjp((ds, _heads_in(do_ref, rows)))
            ds = g[0]
            for r, v in zip(g_refs[:nr], g[1:1 + nr]):
                _heads_out(r, rows, v)
            if ns:
                for r, v in zip(g_refs[nr:], _gdn_scalars_back(*g[1 + nr:])):
                    r[rows, :] = v
        ds_scr[...] = ds

    arrs = (*rows_in, *scal_in)
    return pl.pallas_call(
        body, grid=(n,),
        in_specs=[_scan_spec(a.shape[1], grp, n, True) for a in arrs]
        + [_hist_spec(grp, n, True), _hist_spec(grp, n, True), _scan_spec(WIDTH, grp, n, True)] + [_ANY] * nji,
        out_specs=[_scan_spec(a.shape[1], grp, n, True) for a in arrs] + [_ANY] * njo,
        out_shape=[jax.ShapeDtypeStruct(a.shape, f32) for a in arrs] + j_outs,
        scratch_shapes=[pltpu.VMEM((HEADS, HEAD_DIM, HEAD_DIM), f32)] + j_sems,
        compiler_params=pltpu.CompilerParams(dimension_semantics=("arbitrary",)),
        name=name,
    )(*arrs, s_hist, p_hist, d_out, *j_ins)


def _fn_norm_mod(ps, ts):
    nw, shift, scale = ps
    (x,) = ts
    return [_rms(x, nw, NORM_EPS) * (1.0 + scale) + shift]


def _fn_norm_mod_and_x(ps, ts):
    return _fn_norm_mod(ps, ts) + [ts[0]]


def _fn_resid_norm_mod(ps, ts):
    gate, nw, shift, scale = ps
    x, mo = ts
    x1 = x + gate * mo
    return [x1, _rms(x1, nw, NORM_EPS) * (1.0 + scale) + shift]


def _fn_gdn_pre(ps, ts):
    cw = ps[:12]
    alog, dtb, seg, tri = ps[12:]
    ba = ts[12]
    outs = []
    for part in range(3):
        x = ts[4 * part:4 * part + 4]
        w = cw[4 * part:4 * part + 4]
        conv = w[3] * x[0] + w[2] * x[1] + w[1] * x[2] + w[0] * x[3]
        u = _silu(conv)
        if part < 2:
            u = u * lax.rsqrt(_segsum(u * u, seg) + 1e-6)
            if part == 0:
                u = u * (HEAD_DIM ** -0.5)
        outs.append(u)
    beta = jax.nn.sigmoid(ba)
    g = -jnp.exp(alog) * _softplus(ba + dtb)
    gc = _xdot_l(tri, g)
    return outs + [beta, gc]


def _fn_gdn_post(ps, ts):
    ow, seg = ps
    o, z = ts
    ms = _segsum(o * o, seg) * (1.0 / HEAD_DIM)
    return [o * lax.rsqrt(ms + NORM_EPS) * ow * _silu(z)]


def _fn_rwkv_pre(ps, ts):
    mu_r, mu_k, mu_v, mu_l, mu_g, w0, w2p, a0, a2p, g2p, k_k, k_a, seg = ps
    r0, r1, k0, k1, v0, v1, l0, l1, g0, g1 = ts
    xr = r0 + (r1 - r0) * mu_r
    xk = k0 + (k1 - k0) * mu_k
    xv = v0 + (v1 - v0) * mu_v
    xl = l0 + (l1 - l0) * mu_l
    xg = g0 + (g1 - g0) * mu_g
    w = -_softplus(-(w0 + _bdot(jnp.tanh(xl), w2p))) - 0.5
    lw = -jnp.exp(w)
    a = jax.nn.sigmoid(a0 + _bdot(xl, a2p))
    g = _bdot(jax.nn.sigmoid(xg), g2p)
    kk = xk * k_k
    kk = kk * lax.rsqrt(_segsum(kk * kk, seg) + 1e-6)
    k2 = xk * (1.0 + (a - 1.0) * k_a)
    return [xr, lw, k2, xv, -kk, kk * a, g]


def _fn_rwkv_post(ps, ts):
    lw_, lb_, rk, seg = ps
    y, r, k2, v, g = ts
    inv = 1.0 / HEAD_DIM
    yc = y - _segsum(y, seg) * inv
    var = _segsum(yc * yc, seg) * inv
    yn = yc * lax.rsqrt(var + LNX_EPS) * lw_ + lb_
    bonus = _segsum(r * k2 * rk, seg) * v
    return [(yn + bonus) * g]


def _fn_merge(ps, ts):
    gla, glb, ya, yb = ts
    return [jax.nn.sigmoid(gla) * ya + jax.nn.sigmoid(glb) * yb]


def _fn_convglu(ps, ts):
    c0, c1, c2 = ps
    g0, g1, g2, up = ts
    return [_silu(c2 * g0 + c1 * g1 + c0 * g2) * up]


def _final_stage(x1, fo, tgt, gate2, nfw, tt):
    rows, d = x1.shape

    def loss_fn(gate, nw, xa, fa, tg):
        y = _rms(xa + gate * fa, nw, NORM_EPS)
        err = (y - tg) ** 2
        return 0.5 * jnp.sum(jnp.mean(err, axis=-1, keepdims=True), axis=0, keepdims=True)

    def body(x_ref, f_ref, t_ref, g_ref, w_ref, dx_ref, df_ref, dg_ref, dw_ref, l_ref):
        i = pl.program_id(0)
        args = (g_ref[...], w_ref[...], x_ref[...], f_ref[...])
        tg = t_ref[...]
        lv, vjp = jax.vjp(lambda g, w, xa, fa: loss_fn(g, w, xa, fa, tg), *args)
        dg, dw, dx, df = vjp(jnp.ones((1, 1), f32))
        dx_ref[...] = dx
        df_ref[...] = df.astype(df_ref.dtype)

        @pl.when(i == 0)
        def _():
            dg_ref[...] = jnp.zeros_like(dg_ref)
            dw_ref[...] = jnp.zeros_like(dw_ref)
            l_ref[...] = jnp.zeros_like(l_ref)

        dg_ref[...] += dg
        dw_ref[...] += dw
        l_ref[...] += jnp.broadcast_to(lv, l_ref.shape)

    row = pl.BlockSpec((tt, d), lambda i: (i, 0))
    vec = pl.BlockSpec((1, d), lambda i: (0, 0))
    return pl.pallas_call(
        body, grid=(rows // tt,),
        in_specs=[row, row, row, vec, vec],
        out_specs=[row, row, vec, vec, pl.BlockSpec((1, LANES), lambda i: (0, 0))],
        out_shape=[jax.ShapeDtypeStruct((rows, d), f32), jax.ShapeDtypeStruct((rows, d), bf16)]
        + [jax.ShapeDtypeStruct((1, d), f32)] * 2
        + [jax.ShapeDtypeStruct((1, LANES), f32)],
        compiler_params=pltpu.CompilerParams(dimension_semantics=("arbitrary",)),
        name="loss_head",
    )(x1, fo, tgt, gate2, nfw)


def _ada_fwd(c_all, w_shard, b_cols):
    def body(c_ref, w_ref, b_ref, cond_ref, mod_ref):
        cond = _silu(c_ref[...])
        cond_ref[...] = cond
        mod_ref[...] = jnp.dot(cond.astype(bf16), w_ref[...].astype(bf16), preferred_element_type=f32) + b_ref[...]

    n = w_shard.shape[1]
    return pl.pallas_call(
        body, out_shape=[jax.ShapeDtypeStruct(c_all.shape, f32), jax.ShapeDtypeStruct((c_all.shape[0], n), f32)],
        name="ada_fwd",
    )(c_all, w_shard, b_cols)


def _adamw(name, w, g, m, v):
    rows, width = w.shape
    tt = _pick(rows, 256, 8)
    c1 = 1.0 - ADAM_B1 ** ADAM_STEP
    c2 = 1.0 - ADAM_B2 ** ADAM_STEP

    def body(w_ref, g_ref, m_ref, v_ref, d_ref, mo_ref, vo_ref):
        gg = g_ref[...]
        mn = ADAM_B1 * m_ref[...] + (1.0 - ADAM_B1) * gg
        vn = ADAM_B2 * v_ref[...] + (1.0 - ADAM_B2) * (gg * gg)
        m_hat = mn / c1
        v_hat = vn / c2
        d_ref[...] = -ADAM_LR * (m_hat / (jnp.sqrt(v_hat) + ADAM_EPS) + ADAM_WD * w_ref[...])
        mo_ref[...] = mn
        vo_ref[...] = vn

    spec = pl.BlockSpec((tt, width), lambda i: (i, 0))
    return pl.pallas_call(
        body, grid=(rows // tt,), in_specs=[spec] * 4, out_specs=[spec] * 3,
        out_shape=[jax.ShapeDtypeStruct((rows, width), f32)] * 3,
        compiler_params=pltpu.CompilerParams(dimension_semantics=("parallel",)),
        name=name,
    )(w, g, m, v)


def _place():
    return lax.axis_index("x"), lax.axis_index("y"), lax.axis_index("c")


def _ag8(name, blk):
    m, w = blk.shape

    def body(x_ref, out_ref, send_sems, recv_sems, local_sem):
        x, y, c = _place()
        me, sibling = (x, y, c), (x, y, 1 - c)
        chips = _other_chips(x, y)

        def slot(px, py, pc):
            return out_ref.at[4 * px + 2 * py + pc]

        def copy(k, block, to, src=None):
            return pltpu.make_async_remote_copy(src_ref=slot(*block) if src is None else src, dst_ref=slot(*block),
                                                send_sem=send_sems.at[k], recv_sem=recv_sems.at[k], device_id=to,
                                                device_id_type=MESH_ID)

        mine = pltpu.make_async_copy(x_ref, slot(*me), local_sem)
        mine.start()
        first = [copy(0, me, sibling, src=x_ref)] + [copy(1 + j, me, (*chip, c), src=x_ref) for j, chip in enumerate(chips)]
        for cp in first:
            cp.start()
        passed = [copy(4 + j, (*chip, c), sibling) for j, chip in enumerate(chips)]
        for j, chip in enumerate(chips):
            copy(1 + j, (*chip, c), me).wait_recv()
            passed[j].start()
        copy(0, sibling, me).wait_recv()
        for j, chip in enumerate(chips):
            copy(4 + j, (*chip, 1 - c), me).wait_recv()
        for cp in first + passed:
            cp.wait_send()
        mine.wait()

    return pl.pallas_call(
        body, out_shape=jax.ShapeDtypeStruct((8, m, w), blk.dtype),
        in_specs=[pl.BlockSpec(memory_space=pltpu.VMEM)], out_specs=pl.BlockSpec(memory_space=pltpu.VMEM),
        scratch_shapes=[pltpu.SemaphoreType.DMA((7,)), pltpu.SemaphoreType.DMA((7,)), pltpu.SemaphoreType.DMA],
        name=name,
    )(blk)


def _ag8_job(blk):
    def plan(x_refs, out_refs, send_sems, recv_sems):
        x_ref, out_ref = x_refs[0], out_refs[0]
        x, y, c = _place()
        me, sibling = (x, y, c), (x, y, 1 - c)
        chips = _other_chips(x, y)
        slot = lambda px, py, pc: out_ref.at[4 * px + 2 * py + pc]
        cp = lambda k, block, to, src=None: _rcopy(slot(*block) if src is None else src, slot(*block), send_sems, recv_sems, k, to)
        first = [cp(0, me, sibling, x_ref)] + [cp(1 + j, me, (*chip, c), x_ref) for j, chip in enumerate(chips)]
        landed = [cp(1 + j, (*chip, c), me) for j, chip in enumerate(chips)]
        passed = [cp(4 + j, (*chip, c), sibling) for j, chip in enumerate(chips)]
        handed = [cp(0, sibling, me)] + [cp(4 + j, (*chip, 1 - c), me) for j, chip in enumerate(chips)]
        mine = pltpu.make_async_copy(x_ref, slot(*me), send_sems.at[7])
        return first, landed, passed, handed, mine

    def start(*refs):
        first, _, _, _, mine = plan(*refs)
        mine.start()
        for cp in first:
            cp.start()

    def finish(*refs):
        first, landed, passed, handed, mine = plan(*refs)
        for arrived, fw in zip(landed, passed):
            arrived.wait_recv()
            fw.start()
        for cp in handed:
            cp.wait_recv()
        for cp in first + passed:
            cp.wait_send()
        mine.wait()

    return dict(ins=[blk], out_shapes=[jax.ShapeDtypeStruct((8,) + blk.shape, blk.dtype)], nsem=8, start=start,
                finish=finish)


def _other_chips(x, y):
    return [(1 - x, y), (x, 1 - y), (1 - x, 1 - y)]


_ANY = pl.BlockSpec(memory_space=pl.ANY)


def _rcopy(src, dst, send_sems, recv_sems, k, dev):
    return pltpu.make_async_remote_copy(src_ref=src, dst_ref=dst, send_sem=send_sems.at[k], recv_sem=recv_sems.at[k],
                                        device_id=dev, device_id_type=MESH_ID)


def _run_job(name, job):
    j_ins, j_outs, j_sems = _job_parts(job)
    n = len(j_ins)

    def body(*refs):
        for phase in ("start", "finish"):
            _job_phase(job, phase, refs[:n], refs[n:n + len(j_outs)], refs[n + len(j_outs):])

    return pl.pallas_call(body, out_shape=j_outs, in_specs=[_ANY] * n, out_specs=[_ANY] * len(j_outs),
                          scratch_shapes=j_sems, name=name)(*j_ins)


def _ag4_job(ws):
    n = len(ws)

    def plan(w_refs, out_refs, send_sems, recv_sems):
        x, y, c = _place()
        chip = 2 * x + y
        sibling = (x, y, 1 - c)
        chips = _other_chips(x, y)
        mine = [pl.ds(c * (w.shape[0] // 2), w.shape[0] // 2) for w in ws]
        other = [pl.ds((1 - c) * (w.shape[0] // 2), w.shape[0] // 2) for w in ws]
        rc = lambda src, dst, k, dev: _rcopy(src, dst, send_sems, recv_sems, k, dev)
        first = [rc(w_refs[t].at[mine[t]], out_refs[t].at[chip, mine[t]], 7 * t + k, (px, py, c))
                 for t in range(n) for k, (px, py) in enumerate(chips)]
        own = [rc(w_refs[t], out_refs[t].at[chip], 7 * t + 6, sibling) for t in range(n)]
        landed = [rc(out_refs[t].at[2 * px + py, mine[t]], out_refs[t].at[2 * px + py, mine[t]], 7 * t + k, (px, py, c))
                  for t in range(n) for k, (px, py) in enumerate(chips)]
        forward = [rc(out_refs[t].at[2 * px + py, mine[t]], out_refs[t].at[2 * px + py, mine[t]], 7 * t + 3 + k, sibling)
                   for t in range(n) for k, (px, py) in enumerate(chips)]
        handed = [rc(out_refs[t].at[2 * px + py, other[t]], out_refs[t].at[2 * px + py, other[t]], 7 * t + 3 + k, sibling)
                  for t in range(n) for k, (px, py) in enumerate(chips)]
        return first, own, landed, forward, handed

    def start(*refs):
        first, own, _, _, _ = plan(*refs)
        for cp in first + own:
            cp.start()

    def finish(*refs):
        first, own, landed, forward, handed = plan(*refs)
        for arrived, fw in zip(landed, forward):
            arrived.wait_recv()
            fw.start()
        for cp in handed + own:
            cp.wait_recv()
        for cp in first + own + forward:
            cp.wait_send()

    return dict(ins=ws, out_shapes=[jax.ShapeDtypeStruct((4,) + w.shape, w.dtype) for w in ws], nsem=7 * n,
                start=start, finish=finish)


def _sibling_swap_job(gs):
    n = len(gs)

    def plan(g_refs, recv_refs, send_sems, recv_sems):
        x, y, c = _place()
        return [_rcopy(g_refs[t].at[s_, pl.ds((1 - c) * (gs[t].shape[1] // 2), gs[t].shape[1] // 2)], recv_refs[t].at[s_],
                       send_sems, recv_sems, 4 * t + s_, (x, y, 1 - c)) for t in range(n) for s_ in range(4)]

    def start(*refs):
        for cp in plan(*refs):
            cp.start()

    def finish(*refs):
        copies = plan(*refs)
        for cp in copies:
            cp.wait_recv()
        for cp in copies:
            cp.wait_send()

    return dict(ins=gs, out_shapes=[jax.ShapeDtypeStruct((4, g.shape[1] // 2, g.shape[2]), g.dtype) for g in gs],
                nsem=4 * n, start=start, finish=finish)


def _chip_exchange_job(ps):
    n = len(ps)

    def plan(p_refs, recv_refs, send_sems, recv_sems):
        x, y, c = _place()
        return [_rcopy(p_refs[t].at[2 * px + py], recv_refs[t].at[k], send_sems, recv_sems, 3 * t + k, (px, py, c))
                for t in range(n) for k, (px, py) in enumerate(_other_chips(x, y))]

    def start(*refs):
        for cp in plan(*refs):
            cp.start()

    def finish(*refs):
        copies = plan(*refs)
        for cp in copies:
            cp.wait_recv()
        for cp in copies:
            cp.wait_send()

    return dict(ins=ps, out_shapes=[jax.ShapeDtypeStruct((3,) + p.shape[1:], p.dtype) for p in ps], nsem=3 * n,
                start=start, finish=finish)


_JOIN_PIECES = 4


def _rs_sibling_join(qs):
    n = len(qs)
    npc = _JOIN_PIECES

    def body(*refs):
        q_refs, out_refs = refs[:n], refs[n:2 * n]
        send_sems, recv_sems = refs[2 * n:]
        x, y, c = _place()
        copies = []
        for t in range(n):
            rh = qs[t].shape[0] // 2
            pr = rh // npc
            for i in range(npc):
                rows = pl.ds(c * rh + i * pr, pr)
                cp = _rcopy(q_refs[t].at[rows], out_refs[t].at[rows], send_sems, recv_sems, npc * t + i,
                            (x, y, 1 - c))
                cp.start()
                copies.append(cp)
        for t in range(n):
            rh = qs[t].shape[0] // 2
            pr = rh // npc
            for i in range(npc):
                rows = pl.ds((1 - c) * rh + i * pr, pr)
                _rcopy(q_refs[t].at[rows], out_refs[t].at[rows], send_sems, recv_sems, npc * t + i,
                       (x, y, 1 - c)).wait_recv()
        for cp in copies:
            cp.wait_send()

    return pl.pallas_call(
        body, out_shape=[jax.ShapeDtypeStruct(q.shape, q.dtype) for q in qs],
        in_specs=[_ANY] * n, out_specs=[_ANY] * n, input_output_aliases={t: t for t in range(n)},
        scratch_shapes=[pltpu.SemaphoreType.DMA((npc * n,)), pltpu.SemaphoreType.DMA((npc * n,))],
        name="grads_sibling_join",
    )(*qs)


def _add_half(name, g, recv, chip, ci):
    S, r, w = g.shape
    rh = r // 2
    tt = _pick(rh, 256, 16)
    nb = rh // tt

    def body(chip_ref, core_ref, a_ref, b_ref, own_ref, ob_ref):
        v = a_ref[...] + b_ref[...]
        ob_ref[...] = v.astype(bf16)

        @pl.when(pl.program_id(1) == chip_ref[0])
        def _():
            own_ref[...] = v[0]

    grid_spec = pltpu.PrefetchScalarGridSpec(
        num_scalar_prefetch=2, grid=(nb, S),
        in_specs=[pl.BlockSpec((1, tt, w), lambda i, s_, ch, co: (s_, co[0] * nb + i, 0)),
                  pl.BlockSpec((1, tt, w), lambda i, s_, ch, co: (s_, i, 0))],
        out_specs=[pl.BlockSpec((tt, w), lambda i, s_, ch, co: (i, 0)),
                   pl.BlockSpec((1, tt, w), lambda i, s_, ch, co: (s_, i, 0))])
    return pl.pallas_call(body, grid_spec=grid_spec,
                          out_shape=[jax.ShapeDtypeStruct((rh, w), f32), jax.ShapeDtypeStruct((S, rh, w), bf16)],
                          compiler_params=pltpu.CompilerParams(dimension_semantics=("arbitrary", "arbitrary")),
                          name=name)(chip.reshape(1).astype(jnp.int32), ci.reshape(1).astype(jnp.int32), g, recv)


def _sum_chip(name, own, others, ci):
    rh, w = own.shape
    tt = _pick(rh, 128, 16)
    nb = rh // tt

    def body(core_ref, a_ref, b_ref, o_ref):
        o_ref[...] = ((a_ref[...] + b_ref[0].astype(f32)) + b_ref[1].astype(f32)) + b_ref[2].astype(f32)

    grid_spec = pltpu.PrefetchScalarGridSpec(
        num_scalar_prefetch=1, grid=(nb,),
        in_specs=[pl.BlockSpec((tt, w), lambda i, co: (i, 0)),
                  pl.BlockSpec((3, tt, w), lambda i, co: (0, i, 0))],
        out_specs=pl.BlockSpec((tt, w), lambda i, co: (co[0] * nb + i, 0)))
    return pl.pallas_call(body, grid_spec=grid_spec, out_shape=jax.ShapeDtypeStruct((2 * rh, w), f32),
                          name=name)(ci.reshape(1).astype(jnp.int32), own, others)


def _sum_devices(gathered, late, late_rows, head_row):
    _, rows, width = gathered.shape

    def body(g_ref, l_ref, out_ref, head_ref):
        acc, acc_l = g_ref[0], l_ref[0]
        for d in range(1, 8):
            acc = acc + g_ref[d]
            acc_l = acc_l + l_ref[d]
        out_ref[...] = acc
        for k, r in enumerate(late_rows):
            out_ref[r:r + 1, :] = acc_l[k:k + 1, :]
        row = acc[head_row:head_row + 1, :]
        hs = row[:, 0:HEAD_DIM]
        for h in range(1, HEADS):
            hs = hs + row[:, h * HEAD_DIM:(h + 1) * HEAD_DIM]
        head_ref[...] = jnp.zeros_like(head_ref)
        head_ref[0:1, 0:HEAD_DIM] = hs

    return pl.pallas_call(
        body, out_shape=[jax.ShapeDtypeStruct((rows, width), f32), jax.ShapeDtypeStruct((8, LANES), f32)],
        name="small_grads_sum",
    )(gathered, late)


def _pack(arrs, rows_mult, dtype):
    flat = jnp.concatenate([a.reshape(-1).astype(dtype) for a in arrs])
    per = PACK_W * rows_mult
    total = -(-flat.shape[0] // per) * per
    return jnp.pad(flat, (0, total - flat.shape[0])).reshape(total // PACK_W, PACK_W)


def _unpack(buf, shapes):
    flat = buf.reshape(-1)
    out, off = [], 0
    for s in shapes:
        n = int(np.prod(s))
        out.append(flat[off:off + n].reshape(s))
        off += n
    return out


_BIG = ["w_in", "w_branch_gdn", "w_branch_rwkv", "w_out", "w_ffn_in", "w_ffn_out"]
_MID = ["conv_gdn", "conv_ffn", "w2", "a2", "g2"]
_SMALL =["b_ada", "norm1_w", "a_log", "dt_bias", "onorm_gdn", "mu_rwkv", "w0", "a0", "k_k", "k_a", "r_k", "lnx_w",
          "lnx_b", "norm2_w", "norm_f_w"]
_ORDER = ["w_ada", "b_ada", "norm1_w", "w_in", "conv_gdn", "a_log", "dt_bias", "onorm_gdn", "w_branch_gdn", "mu_rwkv",
          "w0", "w2", "a0", "a2", "g2", "k_k", "k_a", "r_k", "lnx_w", "lnx_b", "w_branch_rwkv", "w_out", "norm2_w",
          "w_ffn_in", "conv_ffn", "w_ffn_out", "norm_f_w"]


_WIN_SEGMENTS = [(0, 1536, 0), (2064, 3600, 1536), (1536, 2048, 3072), (3600, 3728, 3584), (2048, 2064, 3712),
                 (3728, 3888, 3840), (3888, 5936, 4096)]
_WIN_PADDED = 6144
_COL_QKV, _COL_RKV, _COL_Z = (0, 1, 2), (3, 4, 5), 6
_COL_LORA, _COL_BA = 28, 29
_COL_GATE_LORA = 15
_COL_GL = (4, 5)
_JOINT_QKV, _JOINT_RKV, _JOINT_Z, _JOINT_GL = 0, 1, 6, 2
_SMALL_BLOCKS_AT = 3584


def _win_pad(shards):
    n = shards.shape[2]
    parts, at = [], 0
    for lo, hi, dst in _WIN_SEGMENTS:
        if dst > at:
            parts.append(jnp.zeros((shards.shape[1], dst - at), shards.dtype))
        c = lo
        while c < hi:
            j = c // n
            e = min(hi, (j + 1) * n)
            parts.append(shards[j][:, c - j * n:e - j * n])
            c = e
        at = dst + hi - lo
    if at < _WIN_PADDED:
        parts.append(jnp.zeros((shards.shape[1], _WIN_PADDED - at), shards.dtype))
    return jnp.concatenate(parts, axis=1)


def _win_unpad_shards(g, n):
    shards = []
    for j in range(4):
        parts = []
        for lo, hi, dst in sorted(_WIN_SEGMENTS):
            a, b = max(lo, j * n), min(hi, (j + 1) * n)
            if a < b:
                parts.append(g[:, dst + a - lo:dst + b - lo])
        shards.append(jnp.concatenate(parts, axis=1))
    return jnp.stack(shards)


def kernel(x, c, w_ada, b_ada, norm1_w, w_in, conv_gdn, a_log, dt_bias, onorm_gdn, w_branch_gdn, mu_rwkv, w0, w2, a0, a2, g2, k_k, k_a, r_k, lnx_w, lnx_b, w_branch_rwkv, w_out, norm2_w, w_ffn_in, conv_ffn, w_ffn_out, norm_f_w, loss_target, m_w_ada, m_b_ada, m_norm1_w, m_w_in, m_conv_gdn, m_a_log, m_dt_bias, m_onorm_gdn, m_w_branch_gdn, m_mu_rwkv, m_w0, m_w2, m_a0, m_a2, m_g2, m_k_k, m_k_a, m_r_k, m_lnx_w, m_lnx_b, m_w_branch_rwkv, m_w_out, m_norm2_w, m_w_ffn_in, m_conv_ffn, m_w_ffn_out, m_norm_f_w, v_w_ada, v_b_ada, v_norm1_w, v_w_in, v_conv_gdn, v_a_log, v_dt_bias, v_onorm_gdn, v_w_branch_gdn, v_mu_rwkv, v_w0, v_w2, v_a0, v_a2, v_g2, v_k_k, v_k_a, v_r_k, v_lnx_w, v_lnx_b, v_w_branch_rwkv, v_w_out, v_norm2_w, v_w_ffn_in, v_conv_ffn, v_w_ffn_out, v_norm_f_w):
    args = dict(locals())
    W = {n: args[n] for n in _ORDER}
    Mo = {n: args["m_" + n] for n in _ORDER}
    Vo = {n: args["v_" + n] for n in _ORDER}
    shapes = {n: W[n].shape for n in _ORDER}
    sq = lambda a: a.reshape(a.shape[-2:]) if a.ndim == 3 else a.reshape(1, -1)
    row = lambda a: a.reshape(1, -1)

    xi, yi, ci = lax.axis_index("x"), lax.axis_index("y"), lax.axis_index("c")
    dev = 4 * xi + 2 * yi + ci
    chip = 2 * xi + yi

    x2 = x[0]
    tgt = loss_target[0]
    T, D = x2.shape
    tt_l = _pick(T, 512, CHUNK)
    tt_p = _pick(T, 256, CHUNK)
    tt_h = _pick(T, 128, CHUNK)

    mid_shapes = [shapes[n][1:] for n in _MID]
    small_blk = _pack([c] + [W[n] for n in _MID], 8, f32)
    small_all, win_s = _run_job("gather_c_mid_and_w_in", [_ag8_job(small_blk), _ag4_job([sq(W["w_in"]).astype(bf16)])])
    c_all = small_all[:, 0, :]
    per_chip = small_all[0::2].reshape(4, -1)[:, D:]
    mid = [dict(zip(_MID, _unpack(per_chip[j], mid_shapes))) for j in range(4)]
    catm = lambda n: jnp.concatenate([mid[j][n] for j in range(4)], axis=1)
    conv_gdn_f, conv_ffn_f = catm("conv_gdn"), catm("conv_ffn")
    w2f, a2f, g2f = catm("w2"), catm("a2"), catm("g2")

    later_weights = _ag4_job([sq(W[n]).astype(bf16) for n in _BIG[1:]])
    win_p = _win_pad(win_s)
    zpad = lambda a, top, bot: jnp.pad(a, ((top, bot), (0, 0)))
    w2p, a2p, g2p = zpad(w2f, 0, 64), zpad(a2f, 64, 0), zpad(g2f, 0, 96)

    ncol = shapes["w_ada"][2]
    b_cols = lax.dynamic_slice(sq(W["b_ada"]), (0, chip * ncol), (1, ncol))
    cond16, mod_cols = _ada_fwd(jnp.pad(c_all, ((0, 8), (0, 0))), sq(W["w_ada"]), b_cols)
    mod_all = _ag8("gather_mod", mod_cols[:8])
    mod_mine = lax.dynamic_slice(mod_all[0::2], (0, dev, 0), (4, 1, ncol)).reshape(1, 4 * ncol)
    shift1, scale1, gate1, shift2, scale2, gate2 = [mod_mine[:, i * D:(i + 1) * D] for i in range(6)]

    seg = _seg_matrix(WIDTH, HEAD_DIM)
    norm1 = [sq(W["norm1_w"]), shift1, scale1]
    h1 = _stage_fwd("norm_mod1", _fn_norm_mod, [_whole(x2)], norm1, [(D, bf16)], tt_l)[0]
    p = _matmul("in_proj", h1, win_p, "nn")

    cgq = [row(conv_gdn_f[j, part * WIDTH:(part + 1) * WIDTH]) for part in range(3) for j in range(4)]
    lane_pad = lambda a: jnp.pad(row(a), ((0, 0), (8, LANES - 16)))
    gdn_pre_ps = cgq + [lane_pad(W["a_log"]), lane_pad(W["dt_bias"]), seg, _chunk_tri(tt_p, CHUNK)]
    gdn_pre_ts = [(p, WIDTH, ci_, (0, 1, 2, 3)) for ci_ in _COL_QKV] + [(p, LANES, _COL_BA, None)]
    q_, k_, v_, beta_t, gc_t = _stage_fwd("gdn_pre", _fn_gdn_pre, gdn_pre_ts, gdn_pre_ps,
                                          [(WIDTH, f32)] * 3 + [(LANES, f32)] * 2, tt_p)
    o_, gdn_hist, gdn_inv, wbg_s, wbr_s, wout_s, wfi_s, wfo_s = _scan_fwd(
        "gdn_scan", _gdn_chunk, [q_, k_, v_], [beta_t, gc_t], job=later_weights)
    wout_f = wout_s.reshape(D, D)
    wfo = wfo_s.reshape(D_FF, D)
    ow512 = jnp.tile(row(W["onorm_gdn"]), (1, HEADS))
    gdn_post_ts = [_whole(o_), (p, WIDTH, _COL_Z, None)]
    ya = _stage_fwd("gdn_post", _fn_gdn_post, gdn_post_ts, [ow512, seg], [(WIDTH, bf16)], tt_l)[0]

    mu = sq(W["mu_rwkv"])
    rw_ps = [mu[:, 0:512], mu[:, 512:1024], mu[:, 1024:1536], mu[:, 1536:1664], jnp.pad(mu[:, 1664:1824], ((0, 0), (0, 96))),
             sq(W["w0"]), w2p, sq(W["a0"]), a2p, g2p, sq(W["k_k"]), sq(W["k_a"]), seg]
    rw_ts = [(p, WIDTH, ci_, (0, 1)) for ci_ in _COL_RKV] + [(p, LANES, _COL_LORA, (0, 1)),
                                                              (p, 256, _COL_GATE_LORA, (0, 1))]
    rw_out = _stage_fwd("rwkv_pre", _fn_rwkv_pre, rw_ts, rw_ps, [(WIDTH, f32)] * 7, tt_p)
    r_, lw_, k2_, vv_, na_, b_, g_ = rw_out
    rw_ins = [r_, lw_, k2_, vv_, na_, b_]
    y_, rw_hist, rw_inv = _scan_fwd("rwkv_scan", _rwkv_chunk, rw_ins)
    rwp_ps = [sq(W["lnx_w"]), sq(W["lnx_b"]), row(W["r_k"]), seg]
    rwp_ts = [_whole(y_), _whole(r_), _whole(k2_), _whole(vv_), _whole(g_)]
    yb = _stage_fwd("rwkv_post", _fn_rwkv_post, rwp_ts, rwp_ps, [(WIDTH, bf16)], tt_l)[0]

    big_a = _matmul("branch_gdn", ya, wbg_s, "nn", shards=4)
    big_b = _matmul("branch_rwkv", yb, wbr_s, "nn", shards=4)
    merge_ts = [(p, D, _COL_GL[0], None), (p, D, _COL_GL[1], None), _whole(big_a), _whole(big_b)]
    merged = _stage_fwd("merge", _fn_merge, merge_ts, [], [(D, bf16)], tt_l)[0]
    mo = _matmul("out_proj", merged, wout_f, "nn")
    norm2 = [gate1, sq(W["norm2_w"]), shift2, scale2]
    x1, h2 = _stage_fwd("resid_norm_mod2", _fn_resid_norm_mod, [_whole(x2), _whole(mo)], norm2, [(D, f32), (D, bf16)], tt_l)
    f = _matmul("ffn_in", h2, wfi_s, "nn", shards=4, tn=1408)
    cg_ps = [row(conv_ffn_f[j]) for j in range(3)]
    cg_ts = [(f, D_FF, 0, (0, 1, 2)), (f, D_FF, 1, None)]
    act = _stage_fwd("convglu", _fn_convglu, cg_ts, cg_ps, [(D_FF, bf16)], tt_h)[0]
    fo = _matmul("ffn_out", act, wfo, "nn", tk=1408)

    dx1_a, dfo, dgate2, dnormf, loss_part = _final_stage(x1, fo, tgt, gate2, row(W["norm_f_w"]), tt_l)

    dact = _matmul("d_act", dfo, wfo, "nt", tn=1408)
    g_wfo = _matmul("g_ffn_out", act, dfo, "tn", tm=1408)
    _, dcf, df = _stage_bwd("convglu_bwd", _fn_convglu, cg_ts, cg_ps, [[_whole(dact)]], tt_h, [True] * 2, [True] * 3,
                            joint=([0, 1], 2 * D_FF, 0, None), dtypes=[bf16])
    g_wfi = _matmul("g_ffn_in", h2, df, "tn", shards=4, tn=1408)
    g_wfo_s = g_wfo.reshape(4, D_FF // 4, D)
    dh2, recv_wfi, recv_wfo = _matmul("d_h2", df, wfi_s, "nt", shards=4, tk=1408, job=_sibling_swap_job([g_wfi, g_wfo_s]))
    (dx_a, dmo), (dgate1, dnorm2, dshift2, dscale2), _ = _stage_bwd(
        "resid_norm_mod2_bwd", _fn_resid_norm_mod, [_whole(x2), _whole(mo)], norm2,
        [[_whole(dx1_a)], [_whole(dh2)]], tt_l, [True, True], [True] * 4, dtypes=[f32, bf16])
    dmerged = _matmul("d_merged", dmo, wout_f, "nt")
    g_wout = _matmul("g_out_proj", merged, dmo, "tn")
    (dbig_a, dbig_b), _, dp = _stage_bwd("merge_bwd", _fn_merge, merge_ts, [], [[_whole(dmerged)]], tt_l, [True] * 4, [],
                                         joint=([0, 1], p.shape[1], _JOINT_GL, None), dtypes=[bf16] * 3)
    g_wbg = _matmul("g_branch_gdn", ya, dbig_a, "tn", shards=4)
    g_wbr = _matmul("g_branch_rwkv", yb, dbig_b, "tn", shards=4)
    g_wout_s = g_wout.reshape(4, D // 4, D)
    dyb = _matmul("d_yb", dbig_b, wbr_s, "nt", shards=4)
    dya, *recv_mix = _matmul("d_ya", dbig_a, wbg_s, "nt", shards=4, job=_sibling_swap_job([g_wbg, g_wbr, g_wout_s]))

    (dy_, dr_p, dk2_p, dv_p, dg_p), (dlnxw, dlnxb, drk), _ = _stage_bwd(
        "rwkv_post_bwd", _fn_rwkv_post, rwp_ts, rwp_ps, [[_whole(dyb)]], tt_l, [True] * 5, [True, True, True, False])
    gs_a = [g_wbg, g_wbr, g_wout_s, g_wfi, g_wfo_s]
    pairs_a = [_add_half("grads_pair_sum%d" % (t + 1), g, r_, chip, ci)
               for t, (g, r_) in enumerate(zip(gs_a, recv_mix + [recv_wfi, recv_wfo]))]
    dr_c, dlw_c, dk2_c, dv_c, dna_c, db_c, *others_a = _scan_bwd(
        "rwkv_scan_bwd", _rwkv_chunk, rw_ins, [], rw_hist, rw_inv, dy_, job=_chip_exchange_job([pb for _, pb in pairs_a]))
    rw_cots = [[_whole(dr_p), _whole(dr_c)], [_whole(dlw_c)], [_whole(dk2_p), _whole(dk2_c)],
               [_whole(dv_p), _whole(dv_c)], [_whole(dna_c)], [_whole(db_c)], [_whole(dg_p)]]
    (dl_, dg_), rw_dp, dp = _stage_bwd("rwkv_pre_bwd", _fn_rwkv_pre, rw_ts, rw_ps, rw_cots, tt_p, [True] * 5,
                                       [True] * 12 + [False], joint=([0, 1, 2], p.shape[1], _JOINT_RKV, dp), dtypes=[bf16] * 3)
    dmu_r, dmu_k, dmu_v, dmu_l, dmu_g, dw0, dw2p, da0, da2p, dg2p, dkk, dka = rw_dp

    (do_,), (dow512,), dp = _stage_bwd("gdn_post_bwd", _fn_gdn_post, gdn_post_ts, [ow512, seg], [[_whole(dya)]], tt_l,
                                       [True, True], [True, False], joint=([1], p.shape[1], _JOINT_Z, dp), dtypes=[f32, bf16])
    d_gdn = _scan_bwd("gdn_scan_bwd", _gdn_chunk, [q_, k_, v_], [beta_t, gc_t], gdn_hist, gdn_inv, do_)
    gdn_cots = [[_whole(a)] for a in d_gdn]
    (dba,), gdn_dp, dp = _stage_bwd("gdn_pre_bwd", _fn_gdn_pre, gdn_pre_ts, gdn_pre_ps, gdn_cots, tt_p, [True] * 4,
                                    [True] * 14 + [False, False], joint=([0, 1, 2], p.shape[1], _JOINT_QKV, dp), dtypes=[bf16] * 2)
    dp = lax.dynamic_update_slice(dp, jnp.concatenate([dl_, dba, dg_], axis=1), (0, _SMALL_BLOCKS_AT))
    g_conv_gdn = jnp.concatenate([jnp.concatenate([gdn_dp[4 * part + j] for part in range(3)], axis=1) for j in range(4)], axis=0)
    g_conv_ffn = jnp.concatenate(dcf, axis=0)
    g_mu = jnp.concatenate([dmu_r, dmu_k, dmu_v, dmu_l, dmu_g[:, :160]], axis=1)
    late_zero = jnp.zeros((1, D), f32)
    dmod_early = jnp.concatenate([late_zero, late_zero, dgate1, dshift2, dscale2, dgate2], axis=1)
    small_parts = {"b_ada": dmod_early, "norm1_w": late_zero, "a_log": gdn_dp[12][:, 8:16], "dt_bias": gdn_dp[13][:, 8:16],
                   "mu_rwkv": g_mu, "w0": dw0, "a0": da0, "k_k": dkk, "k_a": dka, "r_k": drk, "lnx_w": dlnxw,
                   "lnx_b": dlnxb, "norm2_w": dnorm2, "norm_f_w": dnormf}
    small_names = [n for n in _SMALL if n != "onorm_gdn"]
    mid_full = [g_conv_gdn, g_conv_ffn, dw2p[0:64], da2p[64:128], dg2p[0:160]]
    body_rows = _pack([small_parts[n] for n in small_names] + [loss_part[:, 0:1]] + mid_full, 1, f32)
    head_row = body_rows.shape[0]
    small_g = jnp.concatenate([body_rows, jnp.pad(dow512, ((0, 0), (0, PACK_W - WIDTH)))], axis=0)
    small_g = jnp.pad(small_g, ((0, -small_g.shape[0] % 8), (0, 0)))

    g_win_pad, small_all_g = _matmul("g_in_proj", h1, dp, "tn", job=_ag8_job(small_g))
    g_win_s = _win_unpad_shards(g_win_pad, shapes["w_in"][2])
    pair_win = _add_half("grads_pair_sum0", g_win_s, _run_job("w_in_grads_sibling_swap", _sibling_swap_job([g_win_s]))[0],
                         chip, ci)
    dh1, others_win = _matmul("d_h1", dp, win_p, "nt", job=_chip_exchange_job([pair_win[1]]))
    (grad_x,), (dnorm1, dshift1, dscale1), _ = _stage_bwd("norm_mod1_bwd", _fn_norm_mod_and_x, [_whole(x2)], norm1,
                                                          [[_whole(dh1)], [_whole(dx_a)]], tt_l, [True], [True] * 3)
    late_all = _ag8("gather_late_grads", jnp.pad(jnp.concatenate([dshift1, dscale1, dnorm1], axis=0), ((0, 5), (0, 0))))
    small_sum, head_sum = _sum_devices(small_all_g, late_all, (0, 1, 6), head_row)
    small_shapes = [shapes[n][1:] if n != "norm_f_w" else shapes[n] for n in small_names]
    un = _unpack(small_sum, small_shapes + [(1,)] + [g.shape for g in mid_full])
    small_grads = dict(zip(small_names, un))
    loss = un[len(small_names)].reshape(())
    small_grads["onorm_gdn"] = head_sum[0, 0:HEAD_DIM]
    for n, g in zip(_MID, un[len(small_names) + 1:]):
        wcols = shapes[n][2]
        small_grads[n] = lax.dynamic_slice(g, (0, chip * wcols), (g.shape[0], wcols))

    dmod_all = jnp.concatenate([late_all[:, 0, :], late_all[:, 1, :], small_all_g[:, 2:6, :].reshape(8, 4 * PACK_W)], axis=1)
    dmod_cols = lax.dynamic_slice(dmod_all, (0, chip * ncol), (8, ncol))
    g_wada = _matmul("g_w_ada", cond16, jnp.pad(dmod_cols, ((0, 8), (0, 0))), "tn")

    pairs = [pair_win] + pairs_a
    others = [others_win] + others_a
    halves = [_sum_chip("grads_chip_sum%d" % t, own, o_, ci) for t, ((own, _), o_) in enumerate(zip(pairs, others))]
    big_grads = dict(zip(_BIG, _rs_sibling_join(halves)))

    res = {tag: {} for tag in ("grad", "delta", "new_m", "new_v")}

    def put(n, g, d, m_, v_):
        for tag, val in zip(("grad", "delta", "new_m", "new_v"), (g, d, m_, v_)):
            res[tag][n] = val.reshape(shapes[n])

    for n in _BIG:
        put(n, big_grads[n], *_adamw("adamw_" + n, sq(W[n]), big_grads[n], sq(Mo[n]), sq(Vo[n])))
    put("w_ada", g_wada, *_adamw("adamw_w_ada", sq(W["w_ada"]), g_wada, sq(Mo["w_ada"]), sq(Vo["w_ada"])))
    rest = _SMALL + _MID
    pk = lambda d: _pack([d[n] for n in rest], 8, f32)
    sg = pk(small_grads)
    sm = _adamw("adamw_small", pk(W), sg, pk(Mo), pk(Vo))
    for tag, buf in zip(("grad", "delta", "new_m", "new_v"), (sg,) + tuple(sm)):
        res[tag].update(zip(rest, _unpack(buf, [shapes[n] for n in rest])))
    outs = [loss, grad_x.reshape(x.shape)]
    for tag in ("grad", "delta", "new_m", "new_v"):
        outs += [res[tag][n] for n in _ORDER]
    return tuple(outs)
```

```python
import numpy as np
import jax
import jax.numpy as jnp
from jax import lax
from jax.experimental import pallas as pl
from jax.experimental.pallas import tpu as pltpu

f32 = jnp.float32
bf16 = jnp.bfloat16

LANES = 128
HEADS = 8
HEAD_DIM = 64
WIDTH = HEADS * HEAD_DIM
CHUNK = 64
D_FF = 2816
NORM_EPS = 1e-6
LNX_EPS = 64e-5
PACK_W = 1024
MESH_ID = pl.DeviceIdType.MESH

ADAM_LR, ADAM_B1, ADAM_B2, ADAM_EPS, ADAM_WD, ADAM_STEP = 0.001, 0.9, 0.999, 1e-08, 0.01, 10


def _pick(n, target, mult):
    if n <= target:
        return n
    best = None
    for t in range(mult, target + 1, mult):
        if n % t == 0:
            best = t
    assert best is not None, (n, target, mult)
    return best


def _split_bf16(x, n):
    parts, r = [], x
    for i in range(n):
        p = r.astype(bf16)
        parts.append(p)
        if i + 1 < n:
            r = r - p.astype(f32)
    return parts


def _xdot_r_impl(x, m, n, dims):
    acc = None
    for p in _split_bf16(x, n):
        t = lax.dot_general(p, m, dims, preferred_element_type=f32)
        acc = t if acc is None else acc + t
    return acc


def _make_xdot_r(n):
    nn = (((1,), (0,)), ((), ()))
    nt = (((1,), (1,)), ((), ()))

    @jax.custom_vjp
    def xdot(x, m):
        return _xdot_r_impl(x, m, n, nn)

    def fwd(x, m):
        return _xdot_r_impl(x, m, n, nn), m

    def bwd(m, ct):
        return _xdot_r_impl(ct, m, n, nt), jnp.zeros_like(m)

    xdot.defvjp(fwd, bwd)
    return xdot


_segsum = _make_xdot_r(2)


def _xdot_l_impl(m, x, n, dims):
    acc = None
    for p in _split_bf16(x, n):
        t = lax.dot_general(m, p, dims, preferred_element_type=f32)
        acc = t if acc is None else acc + t
    return acc


@jax.custom_vjp
def _xdot_l(m, x):
    return _xdot_l_impl(m, x, 3, (((1,), (0,)), ((), ())))


def _xdot_l_fwd(m, x):
    return _xdot_l(m, x), m


def _xdot_l_bwd(m, ct):
    return jnp.zeros_like(m), _xdot_l_impl(m, ct, 3, (((0,), (0,)), ((), ())))


_xdot_l.defvjp(_xdot_l_fwd, _xdot_l_bwd)


@jax.custom_vjp
def _bdot(x, w):
    return jnp.dot(x.astype(bf16), w.astype(bf16), preferred_element_type=f32)


def _bdot_fwd(x, w):
    return _bdot(x, w), (x, w)


def _bdot_bwd(res, ct):
    x, w = res
    c = ct.astype(bf16)
    dx = lax.dot_general(c, w.astype(bf16), (((1,), (1,)), ((), ())), preferred_element_type=f32)
    dw = lax.dot_general(x.astype(bf16), c, (((0,), (0,)), ((), ())), preferred_element_type=f32)
    return dx, dw


_bdot.defvjp(_bdot_fwd, _bdot_bwd)


def _silu(x):
    return x * jax.nn.sigmoid(x)


def _softplus(x):
    return jnp.maximum(x, 0.0) + jnp.log(1.0 + jnp.exp(-jnp.abs(x)))


def _rms(x, w, eps):
    return x * lax.rsqrt(jnp.mean(x * x, axis=-1, keepdims=True) + eps) * w


def _seg_matrix(width, seg):
    i = np.arange(width)
    return jnp.asarray((i[:, None] // seg) == (i[None, :] // seg), dtype=bf16)


def _chunk_tri(rows, chunk):
    i = np.arange(rows)
    return jnp.asarray(((i[:, None] // chunk) == (i[None, :] // chunk)) & (i[:, None] >= i[None, :]), dtype=bf16)


HALO = 8


def _full_spec(shape):
    nd = len(shape)
    return pl.BlockSpec(shape, lambda i: (0,) * nd)


def _entry_specs(entries, tt, block_of):
    specs, ops = [], []
    for arr, w, ci, shifts in entries:
        specs.append(pl.BlockSpec((tt, w), lambda i, ci=ci: (block_of(i), ci)))
        ops.append(arr)
        if shifts:
            specs.append(pl.BlockSpec((HALO, w), lambda i, ci=ci: (jnp.maximum(block_of(i) * (tt // HALO) - 1, 0), ci)))
            ops.append(arr)
    return specs, ops


def _load_entries(entries, refs, first):
    tiles, k = [], 0
    for _, w, _, shifts in entries:
        x = refs[k][...].astype(f32)
        k += 1
        if not shifts:
            tiles.append(x)
            continue
        halo = jnp.where(first, 0.0, refs[k][...].astype(f32))
        k += 1
        row = lax.broadcasted_iota(jnp.int32, (HALO, w), 0)
        for s in shifts:
            if s == 0:
                tiles.append(x)
                continue
            r = pltpu.roll(x, s, 0)
            head = jnp.where(row < s, pltpu.roll(halo, s, 0), r[0:HALO])
            tiles.append(jnp.concatenate([head, r[HALO:]], axis=0))
    return tiles


def _unshift_sum(grads, shifts, carry, tt):
    w = grads[0].shape[1]
    row = lax.broadcasted_iota(jnp.int32, (tt, w), 0)
    row8 = lax.broadcasted_iota(jnp.int32, (HALO, w), 0)
    dx, out = None, jnp.zeros((HALO, w), f32)
    for d, s in zip(grads, shifts):
        if s == 0:
            part = d
        else:
            part = jnp.where(row < tt - s, pltpu.roll(d, tt - s, 0), 0.0)
            out = out + jnp.where(row8 >= HALO - s, pltpu.roll(d[0:HALO], HALO - s, 0), 0.0)
        dx = part if dx is None else dx + part
    return jnp.concatenate([dx[:tt - HALO], dx[tt - HALO:] + carry], axis=0), out


def _stage_fwd(name, fn, tiles, params, outs, tt):
    rows = tiles[0][0].shape[0]
    npar = len(params)
    specs, ops = _entry_specs(tiles, tt, lambda i: i)
    nin = len(ops)

    def body(*refs):
        ts = _load_entries(tiles, refs[:nin], pl.program_id(0) == 0)
        ps = [r[...] for r in refs[nin:nin + npar]]
        res = fn(ps, ts)
        for r, v in zip(refs[nin + npar:], res):
            r[...] = v.astype(r.dtype)

    return pl.pallas_call(
        body, grid=(rows // tt,),
        in_specs=specs + [_full_spec(p.shape) for p in params],
        out_specs=[pl.BlockSpec((tt, w), lambda i: (i, 0)) for (w, _) in outs],
        out_shape=[jax.ShapeDtypeStruct((rows, w), dt) for (w, dt) in outs],
        compiler_params=pltpu.CompilerParams(dimension_semantics=("parallel",)),
        name=name,
    )(*ops, *params)


def _stage_bwd(name, fn, tiles, params, cots, tt, tile_grad, param_grad, joint=None, dtypes=None):
    rows = tiles[0][0].shape[0]
    nblk = rows // tt
    npar = len(params)
    block_of = lambda i: nblk - 1 - i
    specs, ops = _entry_specs(tiles, tt, block_of)
    nin = len(ops)
    flat_cots = [c for group in cots for c in group]
    groups = [len(g) for g in cots]
    ncot = len(flat_cots)
    counts = [len(e[3]) if e[3] else 1 for e in tiles]
    dt_entries = [e for e, g in zip(tiles, tile_grad) if g]
    dp_shapes = [p.shape for p, g in zip(params, param_grad) if g]
    ndt = len(dt_entries)
    carry_w = [e[1] for e in dt_entries if e[3]]
    flags = [g for g, n in zip(tile_grad, counts) for _ in range(n)]
    members, j_width, j_cidx, j_buf = joint if joint else ([], 0, 0, None)
    solo = [k for k in range(ndt) if k not in members]
    nsolo, njoint, nbuf = len(solo), int(bool(members)), int(j_buf is not None)
    j_block = sum(dt_entries[k][1] for k in members)
    dtypes = list(dtypes) if dtypes else [f32] * (nsolo + njoint)

    def body(*refs):
        i = pl.program_id(0)
        p_refs = refs[nin:nin + npar]
        c_refs = refs[nin + npar:nin + npar + ncot]
        base = nin + npar + ncot + nbuf
        dt_refs = refs[base:base + nsolo]
        joint_refs = refs[base + nsolo:base + nsolo + njoint]
        dp_refs = refs[base + nsolo + njoint:base + nsolo + njoint + len(dp_shapes)]
        carry_refs = refs[base + nsolo + njoint + len(dp_shapes):]
        ts = _load_entries(tiles, refs[:nin], block_of(i) == 0)
        ps = [r[...] for r in p_refs]

        def f(dp, dt):
            dp, dt = iter(dp), iter(dt)
            pp = [next(dp) if g else p for p, g in zip(ps, param_grad)]
            tl = [next(dt) if g else t for t, g in zip(ts, flags)]
            return fn(pp, tl)

        _, vjp = jax.vjp(f, [p for p, g in zip(ps, param_grad) if g], [t for t, g in zip(ts, flags) if g])
        cs, j = [], 0
        for n in groups:
            acc = c_refs[j][...].astype(f32)
            for q in range(1, n):
                acc = acc + c_refs[j + q][...].astype(f32)
            cs.append(acc)
            j += n
        gp, gt = vjp(cs)

        @pl.when(i == 0)
        def _():
            for r in dp_refs:
                r[...] = jnp.zeros_like(r)
            for r in carry_refs:
                r[...] = jnp.zeros_like(r)

        gt, k, kc, dxs = list(gt), 0, 0, []
        for e, n in zip(dt_entries, [n for n, g in zip(counts, tile_grad) if g]):
            if e[3]:
                dx, out = _unshift_sum(gt[k:k + n], e[3], carry_refs[kc][...], tt)
                carry_refs[kc][...] = out
                kc += 1
            else:
                dx = gt[k]
            dxs.append(dx)
            k += n
        for r, k in zip(dt_refs, solo):
            r[...] = dxs[k].astype(r.dtype)
        off = 0
        for k in members:
            w = dt_entries[k][1]
            joint_refs[0][:, off:off + w] = dxs[k].astype(joint_refs[0].dtype)
            off += w
        for r, v in zip(dp_refs, gp):
            r[...] += v

    res = pl.pallas_call(
        body, grid=(nblk,),
        in_specs=specs + [_full_spec(p.shape) for p in params]
        + [pl.BlockSpec((tt, w), lambda i, ci=ci: (block_of(i), ci)) for (_, w, ci, *_) in flat_cots]
        + [pl.BlockSpec(memory_space=pl.ANY)] * nbuf,
        out_specs=[pl.BlockSpec((tt, dt_entries[k][1]), lambda i: (block_of(i), 0)) for k in solo]
        + [pl.BlockSpec((tt, j_block), lambda i: (block_of(i), j_cidx))] * njoint
        + [_full_spec(s) for s in dp_shapes],
        out_shape=[jax.ShapeDtypeStruct((rows, dt_entries[k][1]), dt) for k, dt in zip(solo, dtypes)]
        + [jax.ShapeDtypeStruct((rows, j_width), dtypes[-1])] * njoint
        + [jax.ShapeDtypeStruct(s, f32) for s in dp_shapes],
        scratch_shapes=[pltpu.VMEM((HALO, w), f32) for w in carry_w],
        input_output_aliases={nin + npar + ncot: nsolo} if nbuf else {},
        compiler_params=pltpu.CompilerParams(dimension_semantics=("arbitrary",)),
        name=name,
    )(*ops, *params, *[c[0] for c in flat_cots], *([j_buf] if nbuf else []))
    res = list(res)
    return res[:nsolo], res[nsolo + njoint:], (res[nsolo] if njoint else None)


def _whole(a):
    return (a, a.shape[1], 0, None)


def _matmul(name, a, b, mode, out_dtype=f32, tm=1024, tn=1024, tk=1024, shards=1, job=None):
    S = shards
    if mode == "nn":
        M, K = a.shape
        w = b.shape[-1]
    elif mode == "nt":
        M = a.shape[0]
        if S > 1:
            _, N, w = b.shape
            K = S * w
        else:
            N, K = b.shape
            w = K
    else:
        K, M = a.shape
        w = b.shape[1] // S
    if mode != "nt":
        N = S * w
    tm = _pick(M, tm, LANES)
    if mode == "nt":
        tn = _pick(N, tn, LANES)
        tk = _pick(w, tk, LANES)
    else:
        tn = _pick(w, tn, LANES)
        tk = _pick(K, tk, LANES if mode == "nn" else 16)
    nk = K // tk
    nb = w // (tk if mode == "nt" else tn)
    if mode == "nn":
        a_spec = pl.BlockSpec((tm, tk), lambda i, j, k: (i, k))
        if S > 1:
            b_spec = pl.BlockSpec((1, tk, tn), lambda i, j, k: (j // nb, k, j % nb))
        else:
            b_spec = pl.BlockSpec((tk, tn), lambda i, j, k: (k, j))
        dims = (((1,), (0,)), ((), ()))
    elif mode == "nt":
        a_spec = pl.BlockSpec((tm, tk), lambda i, j, k: (i, k))
        if S > 1:
            b_spec = pl.BlockSpec((1, tn, tk), lambda i, j, k: (k // nb, j, k % nb))
        else:
            b_spec = pl.BlockSpec((tn, tk), lambda i, j, k: (j, k))
        dims = (((1,), (1,)), ((), ()))
    else:
        a_spec = pl.BlockSpec((tk, tm), lambda i, j, k: (k, i))
        b_spec = pl.BlockSpec((tk, tn), lambda i, j, k: (k, j))
        dims = (((0,), (0,)), ((), ()))
    if mode == "tn" and S > 1:
        o_spec = pl.BlockSpec((1, tm, tn), lambda i, j, k: (j // nb, i, j % nb))
        o_shape = (S, M, w)
    else:
        o_spec = pl.BlockSpec((tm, tn), lambda i, j, k: (i, j))
        o_shape = (M, N)
    b_lead = S > 1 and mode != "tn"
    o_lead = S > 1 and mode == "tn"

    j_ins, j_outs, j_sems = _job_parts(job)
    nji, njo = len(j_ins), len(j_outs)
    grid = (M // tm, N // tn, nk)

    def run_job(refs):
        step = (pl.program_id(0) * grid[1] + pl.program_id(1)) * grid[2] + pl.program_id(2)
        _job_steps(job, refs[2:2 + nji], refs[3 + nji:3 + nji + njo], refs[len(refs) - len(j_sems):], step,
                   grid[0] * grid[1] * grid[2] - 1)

    def body(*refs):
        a_ref, b_ref, o_ref, acc_ref = refs[0], refs[1], refs[2 + nji], refs[3 + nji + njo]
        run_job(refs)
        k = pl.program_id(2)

        @pl.when(k == 0)
        def _():
            acc_ref[...] = jnp.zeros_like(acc_ref)

        bv = b_ref[0] if b_lead else b_ref[...]
        acc_ref[...] += lax.dot_general(a_ref[...].astype(bf16), bv.astype(bf16), dims, preferred_element_type=f32)

        @pl.when(k == nk - 1)
        def _():
            if o_lead:
                o_ref[0] = acc_ref[...].astype(o_ref.dtype)
            else:
                o_ref[...] = acc_ref[...].astype(o_ref.dtype)

    def body_one_step(*refs):
        a_ref, b_ref, o_ref = refs[0], refs[1], refs[2 + nji]
        run_job(refs)
        bv = b_ref[0] if b_lead else b_ref[...]
        res = lax.dot_general(a_ref[...].astype(bf16), bv.astype(bf16), dims, preferred_element_type=f32)
        if o_lead:
            o_ref[0] = res.astype(o_ref.dtype)
        else:
            o_ref[...] = res.astype(o_ref.dtype)

    res = pl.pallas_call(
        body if nk > 1 else body_one_step, grid=grid,
        in_specs=[a_spec, b_spec] + [_ANY] * nji,
        out_specs=[o_spec] + [_ANY] * njo,
        out_shape=[jax.ShapeDtypeStruct(o_shape, out_dtype)] + j_outs,
        scratch_shapes=([pltpu.VMEM((tm, tn), f32)] if nk > 1 else []) + j_sems,
        compiler_params=pltpu.CompilerParams(
            dimension_semantics=("arbitrary",) * 3 if job else ("parallel", "parallel", "arbitrary")),
        name=name,
    )(a, b, *j_ins)
    return res if job else res[0]


def _make_bmm(precision):
    if precision is None:
        cast, kw = (lambda v: v.astype(bf16)), {}
    else:
        cast, kw = (lambda v: v), {"precision": precision}

    def nn(a, b):
        return jnp.einsum("hij,hjk->hik", cast(a), cast(b), preferred_element_type=f32, **kw)

    def nt(a, b):
        return jnp.einsum("hik,hjk->hij", cast(a), cast(b), preferred_element_type=f32, **kw)

    def tn(a, b):
        return jnp.einsum("hki,hkj->hij", cast(a), cast(b), preferred_element_type=f32, **kw)

    if precision is not None:
        return nn, nt, tn
    nn_v, nt_v, tn_v = jax.custom_vjp(nn), jax.custom_vjp(nt), jax.custom_vjp(tn)
    keep = lambda f: (lambda a, b: (f(a, b), (a, b)))
    nn_v.defvjp(keep(nn), lambda r, ct: (nt(ct, r[1]), tn(r[0], ct)))
    nt_v.defvjp(keep(nt), lambda r, ct: (nn(ct, r[1]), tn(ct, r[0])))
    tn_v.defvjp(keep(tn), lambda r, ct: (nt(r[1], ct), nn(r[0], ct)))
    return nn_v, nt_v, tn_v


_bmm, _bmm_nt, _bmm_tn = _make_bmm(None)
_bmm_exact = _make_bmm(lax.Precision.HIGH)[0]


def _masks(n):
    r = lax.broadcasted_iota(jnp.int32, (n, n), 0)
    c = lax.broadcasted_iota(jnp.int32, (n, n), 1)
    return (r >= c)[None], (r > c)[None], (r == c)[None]


_INV_BLOCK = 8


def _nilpotent_inverse(m, eye):
    p = eye + m
    for _ in range(2):
        m = _bmm(m, m)
        p = p + _bmm(p, m)
    return p


def _neumann_inverse_impl(m):
    n = m.shape[1]
    assert n == _INV_BLOCK * _INV_BLOCK
    r = lax.broadcasted_iota(jnp.int32, (n, n), 0)
    c = lax.broadcasted_iota(jnp.int32, (n, n), 1)
    eye = (r == c).astype(f32)[None]
    inside = jnp.where((r // _INV_BLOCK == c // _INV_BLOCK)[None], m, 0.0)
    d_inv = _nilpotent_inverse(inside, eye)
    return _bmm(_nilpotent_inverse(_bmm(d_inv, m - inside), eye), d_inv)


@jax.custom_vjp
def _neumann_inverse(m):
    return _neumann_inverse_impl(m)


def _neumann_inverse_fwd(m):
    p = _neumann_inverse_impl(m)
    return p, p


def _neumann_inverse_bwd(p, ct):
    return (_bmm_tn(p, _bmm_nt(ct, p)),)


_neumann_inverse.defvjp(_neumann_inverse_fwd, _neumann_inverse_bwd)


@jax.custom_vjp
def _given_inverse(m, p):
    return p


_given_inverse.defvjp(lambda m, p: (p, p), lambda p, ct: (_neumann_inverse_bwd(p, ct)[0], jnp.zeros_like(p)))


def _gdn_chunk(s, q, k, v, beta, gc, gr, gl, p=None):
    n = q.shape[1]
    causal, strict, _ = _masks(n)
    decay = jnp.where(causal, jnp.exp(jnp.where(causal, gc - gr, 0.0)), 0.0)
    kb = k * beta
    vb = v * beta
    lower = jnp.where(strict, _bmm_nt(kb, k) * decay, 0.0)
    t_mat = _neumann_inverse(-lower) if p is None else _given_inverse(-lower, p)
    egc = jnp.exp(gc)
    u = _bmm(t_mat, vb)
    w = _bmm(t_mat, kb * egc)
    attn = jnp.where(causal, _bmm_nt(q, k) * decay, 0.0)
    v_new = u - _bmm(w, s)
    o = _bmm(q * egc, s) + _bmm(attn, v_new)
    k_dec = k * jnp.exp(gl - gc)
    s_new = s * jnp.exp(gl) + _bmm_tn(k_dec, v_new)
    return s_new, o, t_mat


def _rwkv_chunk(s, r, lw, k, v, a, b, p=None):
    n = r.shape[1]
    causal, strict, _ = _masks(n)
    tri = jnp.broadcast_to(causal.astype(f32), (r.shape[0], n, n))
    lc = _bmm_exact(tri, lw)
    ein = jnp.exp(lc)
    eout = jnp.exp(-lc)
    a_t = a * jnp.exp(lc - lw)
    b_t = b * eout
    k_t = k * eout
    r_t = r * ein
    a_ab = jnp.where(strict, _bmm_nt(a_t, b_t), 0.0)
    a_ak = jnp.where(strict, _bmm_nt(a_t, k_t), 0.0)
    inv = _neumann_inverse(a_ab) if p is None else _given_inverse(a_ab, p)
    u = _bmm(inv, _bmm_nt(a_t, s) + _bmm(a_ak, v))
    y = (_bmm_nt(r_t, s) + _bmm(jnp.where(causal, _bmm_nt(r_t, b_t), 0.0), u)
         + _bmm(jnp.where(causal, _bmm_nt(r_t, k_t), 0.0), v))
    e_last = jnp.exp(jnp.sum(lw, axis=1, keepdims=True))
    s_new = s * e_last + _bmm_tn(u, b_t * e_last) + _bmm_tn(v, k_t * e_last)
    return s_new, y, inv


def _heads_in(ref, rows):
    return jnp.stack([ref[rows, h * HEAD_DIM:(h + 1) * HEAD_DIM] for h in range(HEADS)], axis=0)


def _heads_out(ref, rows, val):
    for h in range(HEADS):
        ref[rows, h * HEAD_DIM:(h + 1) * HEAD_DIM] = val[h]


def _gdn_scalars(bt, gt):
    n = bt.shape[0]
    gtt = gt.T
    hs = range(HEADS)
    return [jnp.stack([bt[:, h:h + 1] for h in hs], axis=0),
            jnp.stack([gt[:, HEADS + h:HEADS + h + 1] for h in hs], axis=0),
            jnp.stack([gtt[HEADS + h:HEADS + h + 1, :] for h in hs], axis=0),
            jnp.stack([gt[n - 1:n, HEADS + h:HEADS + h + 1] for h in hs], axis=0)]


def _gdn_scalars_back(dbeta, dgc, dgr, dgl):
    n = dbeta.shape[1]
    lane = lax.broadcasted_iota(jnp.int32, (n, LANES), 1)
    row = lax.broadcasted_iota(jnp.int32, (n, LANES), 0)
    sub = lax.broadcasted_iota(jnp.int32, (LANES, n), 0)
    db = jnp.zeros((n, LANES), f32)
    dg = jnp.zeros((n, LANES), f32)
    dgt = jnp.zeros((LANES, n), f32)
    for h in range(HEADS):
        db = jnp.where(lane == h, dbeta[h], db)
        dg = jnp.where(lane == HEADS + h, dgc[h] + jnp.where(row == n - 1, dgl[h], 0.0), dg)
        dgt = jnp.where(sub == HEADS + h, dgr[h], dgt)
    return [db, dg + dgt.T]


SCAN_GROUP = 2


def _scan_steps(t):
    g = SCAN_GROUP if (t // CHUNK) % SCAN_GROUP == 0 else 1
    return g, t // (CHUNK * g)


def _scan_spec(width, g, n, reverse):
    if reverse:
        return pl.BlockSpec((g * CHUNK, width), lambda i: (n - 1 - i, 0))
    return pl.BlockSpec((g * CHUNK, width), lambda i: (i, 0))


def _hist_spec(g, n, reverse):
    blk = (g, HEADS, HEAD_DIM, HEAD_DIM)
    if reverse:
        return pl.BlockSpec(blk, lambda i: (n - 1 - i, 0, 0, 0))
    return pl.BlockSpec(blk, lambda i: (i, 0, 0, 0))


def _job_list(job):
    return [] if job is None else (list(job) if isinstance(job, (list, tuple)) else [job])


def _job_parts(job):
    jobs = _job_list(job)
    sems = [pltpu.SemaphoreType.DMA((j["nsem"],)) for j in jobs for _ in range(2)]
    return [a for j in jobs for a in j["ins"]], [o for j in jobs for o in j["out_shapes"]], sems


def _job_phase(job, phase, in_refs, out_refs, sems):
    ki = ko = 0
    for n, j in enumerate(_job_list(job)):
        ni, no = len(j["ins"]), len(j["out_shapes"])
        j[phase](in_refs[ki:ki + ni], out_refs[ko:ko + no], sems[2 * n], sems[2 * n + 1])
        ki, ko = ki + ni, ko + no


def _job_steps(job, in_refs, out_refs, sems, step, last):
    if job is None:
        return

    @pl.when(step == 0)
    def _():
        _job_phase(job, "start", in_refs, out_refs, sems)

    @pl.when(step == last)
    def _():
        _job_phase(job, "finish", in_refs, out_refs, sems)


def _scan_fwd(name, fn, rows_in, scal_in=(), job=None):
    t = rows_in[0].shape[0]
    grp, n = _scan_steps(t)
    nr, ns = len(rows_in), len(scal_in)
    j_ins, j_outs, j_sems = _job_parts(job)
    nji, njo = len(j_ins), len(j_outs)

    def body(*refs):
        o_ref, sh_ref, ph_ref = refs[nr + ns + nji:nr + ns + nji + 3]
        s_scr = refs[nr + ns + nji + 3 + njo]
        _job_steps(job, refs[nr + ns:nr + ns + nji], refs[nr + ns + nji + 3:nr + ns + nji + 3 + njo],
                   refs[nr + ns + nji + 3 + njo + 1:], pl.program_id(0), n - 1)

        @pl.when(pl.program_id(0) == 0)
        def _():
            s_scr[...] = jnp.zeros_like(s_scr)

        s = s_scr[...]
        for sub in range(grp):
            rows = slice(sub * CHUNK, (sub + 1) * CHUNK)
            sh_ref[sub] = s
            ins = [_heads_in(r, rows) for r in refs[:nr]]
            if ns:
                ins += _gdn_scalars(*[r[rows, :] for r in refs[nr:nr + ns]])
            s, o, p = fn(s, *ins)
            _heads_out(o_ref, rows, o)
            ph_ref[sub] = p
        s_scr[...] = s

    return pl.pallas_call(
        body, grid=(n,),
        in_specs=[_scan_spec(a.shape[1], grp, n, False) for a in (*rows_in, *scal_in)] + [_ANY] * nji,
        out_specs=[_scan_spec(WIDTH, grp, n, False), _hist_spec(grp, n, False), _hist_spec(grp, n, False)] + [_ANY] * njo,
        out_shape=[jax.ShapeDtypeStruct((t, WIDTH), f32)]
        + [jax.ShapeDtypeStruct((t // CHUNK, HEADS, HEAD_DIM, HEAD_DIM), f32)] * 2 + j_outs,
        scratch_shapes=[pltpu.VMEM((HEADS, HEAD_DIM, HEAD_DIM), f32)] + j_sems,
        compiler_params=pltpu.CompilerParams(dimension_semantics=("arbitrary",)),
        name=name,
    )(*rows_in, *scal_in, *j_ins)


def _scan_bwd(name, fn, rows_in, scal_in, s_hist, p_hist, d_out, job=None):
    t = rows_in[0].shape[0]
    grp, n = _scan_steps(t)
    nr, ns = len(rows_in), len(scal_in)
    j_ins, j_outs, j_sems = _job_parts(job)
    nji, njo = len(j_ins), len(j_outs)

    def body(*refs):
        sh_ref, ph_ref, do_ref = refs[nr + ns:nr + ns + 3]
        base = nr + ns + 3 + nji
        g_refs = refs[base:base + nr + ns]
        ds_scr = refs[base + nr + ns + njo]
        _job_steps(job, refs[nr + ns + 3:base], refs[base + nr + ns:base + nr + ns + njo],
                   refs[base + nr + ns + njo + 1:], pl.program_id(0), n - 1)

        @pl.when(pl.program_id(0) == 0)
        def _():
            ds_scr[...] = jnp.zeros_like(ds_scr)

        ds = ds_scr[...]
        for sub in reversed(range(grp)):
            rows = slice(sub * CHUNK, (sub + 1) * CHUNK)
            ins = [_heads_in(r, rows) for r in refs[:nr]]
            if ns:
                ins += _gdn_scalars(*[r[rows, :] for r in refs[nr:nr + ns]])
            p = ph_ref[sub]
            _, vjp = jax.vjp(lambda s, *a, p=p: fn(s, *a, p=p)[:2], sh_ref[sub], *ins)
            g = vjp((ds, _heads_in(do_ref, rows)))
            ds = g[0]
            for r, v in zip(g_refs[:nr], g[1:1 + nr]):
                _heads_out(r, rows, v)
            if ns:
                for r, v in zip(g_refs[nr:], _gdn_scalars_back(*g[1 + nr:])):
                    r[rows, :] = v
        ds_scr[...] = ds

    arrs = (*rows_in, *scal_in)
    return pl.pallas_call(
        body, grid=(n,),
        in_specs=[_scan_spec(a.shape[1], grp, n, True) for a in arrs]
        + [_hist_spec(grp, n, True), _hist_spec(grp, n, True), _scan_spec(WIDTH, grp, n, True)] + [_ANY] * nji,
        out_specs=[_scan_spec(a.shape[1], grp, n, True) for a in arrs] + [_ANY] * njo,
        out_shape=[jax.ShapeDtypeStruct(a.shape, f32) for a in arrs] + j_outs,
        scratch_shapes=[pltpu.VMEM((HEADS, HEAD_DIM, HEAD_DIM), f32)] + j_sems,
        compiler_params=pltpu.CompilerParams(dimension_semantics=("arbitrary",)),
        name=name,
    )(*arrs, s_hist, p_hist, d_out, *j_ins)


def _fn_norm_mod(ps, ts):
    nw, shift, scale = ps
    (x,) = ts
    return [_rms(x, nw, NORM_EPS) * (1.0 + scale) + shift]


def _fn_norm_mod_and_x(ps, ts):
    return _fn_norm_mod(ps, ts) + [ts[0]]


def _fn_resid_norm_mod(ps, ts):
    gate, nw, shift, scale = ps
    x, mo = ts
    x1 = x + gate * mo
    return [x1, _rms(x1, nw, NORM_EPS) * (1.0 + scale) + shift]


def _fn_gdn_pre(ps, ts):
    cw = ps[:12]
    alog, dtb, seg, tri = ps[12:]
    ba = ts[12]
    outs = []
    for part in range(3):
        x = ts[4 * part:4 * part + 4]
        w = cw[4 * part:4 * part + 4]
        conv = w[3] * x[0] + w[2] * x[1] + w[1] * x[2] + w[0] * x[3]
        u = _silu(conv)
        if part < 2:
            u = u * lax.rsqrt(_segsum(u * u, seg) + 1e-6)
            if part == 0:
                u = u * (HEAD_DIM ** -0.5)
        outs.append(u)
    beta = jax.nn.sigmoid(ba)
    g = -jnp.exp(alog) * _softplus(ba + dtb)
    gc = _xdot_l(tri, g)
    return outs + [beta, gc]


def _fn_gdn_post(ps, ts):
    ow, seg = ps
    o, z = ts
    ms = _segsum(o * o, seg) * (1.0 / HEAD_DIM)
    return [o * lax.rsqrt(ms + NORM_EPS) * ow * _silu(z)]


def _fn_rwkv_pre(ps, ts):
    mu_r, mu_k, mu_v, mu_l, mu_g, w0, w2p, a0, a2p, g2p, k_k, k_a, seg = ps
    r0, r1, k0, k1, v0, v1, l0, l1, g0, g1 = ts
    xr = r0 + (r1 - r0) * mu_r
    xk = k0 + (k1 - k0) * mu_k
    xv = v0 + (v1 - v0) * mu_v
    xl = l0 + (l1 - l0) * mu_l
    xg = g0 + (g1 - g0) * mu_g
    w = -_softplus(-(w0 + _bdot(jnp.tanh(xl), w2p))) - 0.5
    lw = -jnp.exp(w)
    a = jax.nn.sigmoid(a0 + _bdot(xl, a2p))
    g = _bdot(jax.nn.sigmoid(xg), g2p)
    kk = xk * k_k
    kk = kk * lax.rsqrt(_segsum(kk * kk, seg) + 1e-6)
    k2 = xk * (1.0 + (a - 1.0) * k_a)
    return [xr, lw, k2, xv, -kk, kk * a, g]


def _fn_rwkv_post(ps, ts):
    lw_, lb_, rk, seg = ps
    y, r, k2, v, g = ts
    inv = 1.0 / HEAD_DIM
    yc = y - _segsum(y, seg) * inv
    var = _segsum(yc * yc, seg) * inv
    yn = yc * lax.rsqrt(var + LNX_EPS) * lw_ + lb_
    bonus = _segsum(r * k2 * rk, seg) * v
    return [(yn + bonus) * g]


def _fn_merge(ps, ts):
    gla, glb, ya, yb = ts
    return [jax.nn.sigmoid(gla) * ya + jax.nn.sigmoid(glb) * yb]


def _fn_convglu(ps, ts):
    c0, c1, c2 = ps
    g0, g1, g2, up = ts
    return [_silu(c2 * g0 + c1 * g1 + c0 * g2) * up]


def _final_stage(x1, fo, tgt, gate2, nfw, tt):
    rows, d = x1.shape

    def loss_fn(gate, nw, xa, fa, tg):
        y = _rms(xa + gate * fa, nw, NORM_EPS)
        err = (y - tg) ** 2
        return 0.5 * jnp.sum(jnp.mean(err, axis=-1, keepdims=True), axis=0, keepdims=True)

    def body(x_ref, f_ref, t_ref, g_ref, w_ref, dx_ref, df_ref, dg_ref, dw_ref, l_ref):
        i = pl.program_id(0)
        args = (g_ref[...], w_ref[...], x_ref[...], f_ref[...])
        tg = t_ref[...]
        lv, vjp = jax.vjp(lambda g, w, xa, fa: loss_fn(g, w, xa, fa, tg), *args)
        dg, dw, dx, df = vjp(jnp.ones((1, 1), f32))
        dx_ref[...] = dx
        df_ref[...] = df.astype(df_ref.dtype)

        @pl.when(i == 0)
        def _():
            dg_ref[...] = jnp.zeros_like(dg_ref)
            dw_ref[...] = jnp.zeros_like(dw_ref)
            l_ref[...] = jnp.zeros_like(l_ref)

        dg_ref[...] += dg
        dw_ref[...] += dw
        l_ref[...] += jnp.broadcast_to(lv, l_ref.shape)

    row = pl.BlockSpec((tt, d), lambda i: (i, 0))
    vec = pl.BlockSpec((1, d), lambda i: (0, 0))
    return pl.pallas_call(
        body, grid=(rows // tt,),
        in_specs=[row, row, row, vec, vec],
        out_specs=[row, row, vec, vec, pl.BlockSpec((1, LANES), lambda i: (0, 0))],
        out_shape=[jax.ShapeDtypeStruct((rows, d), f32), jax.ShapeDtypeStruct((rows, d), bf16)]
        + [jax.ShapeDtypeStruct((1, d), f32)] * 2
        + [jax.ShapeDtypeStruct((1, LANES), f32)],
        compiler_params=pltpu.CompilerParams(dimension_semantics=("arbitrary",)),
        name="loss_head",
    )(x1, fo, tgt, gate2, nfw)


def _ada_fwd(c_all, w_shard, b_cols):
    def body(c_ref, w_ref, b_ref, cond_ref, mod_ref):
        cond = _silu(c_ref[...])
        cond_ref[...] = cond
        mod_ref[...] = jnp.dot(cond.astype(bf16), w_ref[...].astype(bf16), preferred_element_type=f32) + b_ref[...]

    n = w_shard.shape[1]
    return pl.pallas_call(
        body, out_shape=[jax.ShapeDtypeStruct(c_all.shape, f32), jax.ShapeDtypeStruct((c_all.shape[0], n), f32)],
        name="ada_fwd",
    )(c_all, w_shard, b_cols)


def _adamw(name, w, g, m, v):
    rows, width = w.shape
    tt = _pick(rows, 256, 8)
    c1 = 1.0 - ADAM_B1 ** ADAM_STEP
    c2 = 1.0 - ADAM_B2 ** ADAM_STEP

    def body(w_ref, g_ref, m_ref, v_ref, d_ref, mo_ref, vo_ref):
        gg = g_ref[...]
        mn = ADAM_B1 * m_ref[...] + (1.0 - ADAM_B1) * gg
        vn = ADAM_B2 * v_ref[...] + (1.0 - ADAM_B2) * (gg * gg)
        m_hat = mn / c1
        v_hat = vn / c2
        d_ref[...] = -ADAM_LR * (m_hat / (jnp.sqrt(v_hat) + ADAM_EPS) + ADAM_WD * w_ref[...])
        mo_ref[...] = mn
        vo_ref[...] = vn

    spec = pl.BlockSpec((tt, width), lambda i: (i, 0))
    return pl.pallas_call(
        body, grid=(rows // tt,), in_specs=[spec] * 4, out_specs=[spec] * 3,
        out_shape=[jax.ShapeDtypeStruct((rows, width), f32)] * 3,
        compiler_params=pltpu.CompilerParams(dimension_semantics=("parallel",)),
        name=name,
    )(w, g, m, v)


def _place():
    return lax.axis_index("x"), lax.axis_index("y"), lax.axis_index("c")


def _ag8(name, blk):
    m, w = blk.shape

    def body(x_ref, out_ref, send_sems, recv_sems, local_sem):
        x, y, c = _place()
        me, sibling = (x, y, c), (x, y, 1 - c)
        chips = _other_chips(x, y)

        def slot(px, py, pc):
            return out_ref.at[4 * px + 2 * py + pc]

        def copy(k, block, to, src=None):
            return pltpu.make_async_remote_copy(src_ref=slot(*block) if src is None else src, dst_ref=slot(*block),
                                                send_sem=send_sems.at[k], recv_sem=recv_sems.at[k], device_id=to,
                                                device_id_type=MESH_ID)

        mine = pltpu.make_async_copy(x_ref, slot(*me), local_sem)
        mine.start()
        first = [copy(0, me, sibling, src=x_ref)] + [copy(1 + j, me, (*chip, c), src=x_ref) for j, chip in enumerate(chips)]
        for cp in first:
            cp.start()
        passed = [copy(4 + j, (*chip, c), sibling) for j, chip in enumerate(chips)]
        for j, chip in enumerate(chips):
            copy(1 + j, (*chip, c), me).wait_recv()
            passed[j].start()
        copy(0, sibling, me).wait_recv()
        for j, chip in enumerate(chips):
            copy(4 + j, (*chip, 1 - c), me).wait_recv()
        for cp in first + passed:
            cp.wait_send()
        mine.wait()

    return pl.pallas_call(
        body, out_shape=jax.ShapeDtypeStruct((8, m, w), blk.dtype),
        in_specs=[pl.BlockSpec(memory_space=pltpu.VMEM)], out_specs=pl.BlockSpec(memory_space=pltpu.VMEM),
        scratch_shapes=[pltpu.SemaphoreType.DMA((7,)), pltpu.SemaphoreType.DMA((7,)), pltpu.SemaphoreType.DMA],
        name=name,
    )(blk)


def _ag8_job(blk):
    def plan(x_refs, out_refs, send_sems, recv_sems):
        x_ref, out_ref = x_refs[0], out_refs[0]
        x, y, c = _place()
        me, sibling = (x, y, c), (x, y, 1 - c)
        chips = _other_chips(x, y)
        slot = lambda px, py, pc: out_ref.at[4 * px + 2 * py + pc]
        cp = lambda k, block, to, src=None: _rcopy(slot(*block) if src is None else src, slot(*block), send_sems, recv_sems, k, to)
        first = [cp(0, me, sibling, x_ref)] + [cp(1 + j, me, (*chip, c), x_ref) for j, chip in enumerate(chips)]
        landed = [cp(1 + j, (*chip, c), me) for j, chip in enumerate(chips)]
        passed = [cp(4 + j, (*chip, c), sibling) for j, chip in enumerate(chips)]
        handed = [cp(0, sibling, me)] + [cp(4 + j, (*chip, 1 - c), me) for j, chip in enumerate(chips)]
        mine = pltpu.make_async_copy(x_ref, slot(*me), send_sems.at[7])
        return first, landed, passed, handed, mine

    def start(*refs):
        first, _, _, _, mine = plan(*refs)
        mine.start()
        for cp in first:
            cp.start()

    def finish(*refs):
        first, landed, passed, handed, mine = plan(*refs)
        for arrived, fw in zip(landed, passed):
            arrived.wait_recv()
            fw.start()
        for cp in handed:
            cp.wait_recv()
        for cp in first + passed:
            cp.wait_send()
        mine.wait()

    return dict(ins=[blk], out_shapes=[jax.ShapeDtypeStruct((8,) + blk.shape, blk.dtype)], nsem=8, start=start,
                finish=finish)


def _other_chips(x, y):
    return [(1 - x, y), (x, 1 - y), (1 - x, 1 - y)]


_ANY = pl.BlockSpec(memory_space=pl.ANY)


def _rcopy(src, dst, send_sems, recv_sems, k, dev):
    return pltpu.make_async_remote_copy(src_ref=src, dst_ref=dst, send_sem=send_sems.at[k], recv_sem=recv_sems.at[k],
                                        device_id=dev, device_id_type=MESH_ID)


def _run_job(name, job):
    j_ins, j_outs, j_sems = _job_parts(job)
    n = len(j_ins)

    def body(*refs):
        for phase in ("start", "finish"):
            _job_phase(job, phase, refs[:n], refs[n:n + len(j_outs)], refs[n + len(j_outs):])

    return pl.pallas_call(body, out_shape=j_outs, in_specs=[_ANY] * n, out_specs=[_ANY] * len(j_outs),
                          scratch_shapes=j_sems, name=name)(*j_ins)


def _ag4_job(ws):
    n = len(ws)

    def plan(w_refs, out_refs, send_sems, recv_sems):
        x, y, c = _place()
        chip = 2 * x + y
        sibling = (x, y, 1 - c)
        chips = _other_chips(x, y)
        mine = [pl.ds(c * (w.shape[0] // 2), w.shape[0] // 2) for w in ws]
        other = [pl.ds((1 - c) * (w.shape[0] // 2), w.shape[0] // 2) for w in ws]
        rc = lambda src, dst, k, dev: _rcopy(src, dst, send_sems, recv_sems, k, dev)
        first = [rc(w_refs[t].at[mine[t]], out_refs[t].at[chip, mine[t]], 7 * t + k, (px, py, c))
                 for t in range(n) for k, (px, py) in enumerate(chips)]
        own = [rc(w_refs[t], out_refs[t].at[chip], 7 * t + 6, sibling) for t in range(n)]
        landed = [rc(out_refs[t].at[2 * px + py, mine[t]], out_refs[t].at[2 * px + py, mine[t]], 7 * t + k, (px, py, c))
                  for t in range(n) for k, (px, py) in enumerate(chips)]
        forward = [rc(out_refs[t].at[2 * px + py, mine[t]], out_refs[t].at[2 * px + py, mine[t]], 7 * t + 3 + k, sibling)
                   for t in range(n) for k, (px, py) in enumerate(chips)]
        handed = [rc(out_refs[t].at[2 * px + py, other[t]], out_refs[t].at[2 * px + py, other[t]], 7 * t + 3 + k, sibling)
                  for t in range(n) for k, (px, py) in enumerate(chips)]
        return first, own, landed, forward, handed

    def start(*refs):
        first, own, _, _, _ = plan(*refs)
        for cp in first + own:
            cp.start()

    def finish(*refs):
        first, own, landed, forward, handed = plan(*refs)
        for arrived, fw in zip(landed, forward):
            arrived.wait_recv()
            fw.start()
        for cp in handed + own:
            cp.wait_recv()
        for cp in first + own + forward:
            cp.wait_send()

    return dict(ins=ws, out_shapes=[jax.ShapeDtypeStruct((4,) + w.shape, w.dtype) for w in ws], nsem=7 * n,
                start=start, finish=finish)


def _sibling_swap_job(gs):
    n = len(gs)

    def plan(g_refs, recv_refs, send_sems, recv_sems):
        x, y, c = _place()
        return [_rcopy(g_refs[t].at[s_, pl.ds((1 - c) * (gs[t].shape[1] // 2), gs[t].shape[1] // 2)], recv_refs[t].at[s_],
                       send_sems, recv_sems, 4 * t + s_, (x, y, 1 - c)) for t in range(n) for s_ in range(4)]

    def start(*refs):
        for cp in plan(*refs):
            cp.start()

    def finish(*refs):
        copies = plan(*refs)
        for cp in copies:
            cp.wait_recv()
        for cp in copies:
            cp.wait_send()

    return dict(ins=gs, out_shapes=[jax.ShapeDtypeStruct((4, g.shape[1] // 2, g.shape[2]), g.dtype) for g in gs],
                nsem=4 * n, start=start, finish=finish)


def _chip_exchange_job(ps):
    n = len(ps)

    def plan(p_refs, recv_refs, send_sems, recv_sems):
        x, y, c = _place()
        return [_rcopy(p_refs[t].at[2 * px + py], recv_refs[t].at[k], send_sems, recv_sems, 3 * t + k, (px, py, c))
                for t in range(n) for k, (px, py) in enumerate(_other_chips(x, y))]

    def start(*refs):
        for cp in plan(*refs):
            cp.start()

    def finish(*refs):
        copies = plan(*refs)
        for cp in copies:
            cp.wait_recv()
        for cp in copies:
            cp.wait_send()

    return dict(ins=ps, out_shapes=[jax.ShapeDtypeStruct((3,) + p.shape[1:], p.dtype) for p in ps], nsem=3 * n,
                start=start, finish=finish)


_JOIN_PIECES = 4


def _rs_sibling_join(qs):
    n = len(qs)
    npc = _JOIN_PIECES

    def body(*refs):
        q_refs, out_refs = refs[:n], refs[n:2 * n]
        send_sems, recv_sems = refs[2 * n:]
        x, y, c = _place()
        copies = []
        for t in range(n):
            rh = qs[t].shape[0] // 2
            pr = rh // npc
            for i in range(npc):
                rows = pl.ds(c * rh + i * pr, pr)
                cp = _rcopy(q_refs[t].at[rows], out_refs[t].at[rows], send_sems, recv_sems, npc * t + i,
                            (x, y, 1 - c))
                cp.start()
                copies.append(cp)
        for t in range(n):
            rh = qs[t].shape[0] // 2
            pr = rh // npc
            for i in range(npc):
                rows = pl.ds((1 - c) * rh + i * pr, pr)
                _rcopy(q_refs[t].at[rows], out_refs[t].at[rows], send_sems, recv_sems, npc * t + i,
                       (x, y, 1 - c)).wait_recv()
        for cp in copies:
            cp.wait_send()

    return pl.pallas_call(
        body, out_shape=[jax.ShapeDtypeStruct(q.shape, q.dtype) for q in qs],
        in_specs=[_ANY] * n, out_specs=[_ANY] * n, input_output_aliases={t: t for t in range(n)},
        scratch_shapes=[pltpu.SemaphoreType.DMA((npc * n,)), pltpu.SemaphoreType.DMA((npc * n,))],
        name="grads_sibling_join",
    )(*qs)


def _add_half(name, g, recv, chip, ci):
    S, r, w = g.shape
    rh = r // 2
    tt = _pick(rh, 256, 16)
    nb = rh // tt

    def body(chip_ref, core_ref, a_ref, b_ref, own_ref, ob_ref):
        v = a_ref[...] + b_ref[...]
        ob_ref[...] = v.astype(bf16)

        @pl.when(pl.program_id(1) == chip_ref[0])
        def _():
            own_ref[...] = v[0]

    grid_spec = pltpu.PrefetchScalarGridSpec(
        num_scalar_prefetch=2, grid=(nb, S),
        in_specs=[pl.BlockSpec((1, tt, w), lambda i, s_, ch, co: (s_, co[0] * nb + i, 0)),
                  pl.BlockSpec((1, tt, w), lambda i, s_, ch, co: (s_, i, 0))],
        out_specs=[pl.BlockSpec((tt, w), lambda i, s_, ch, co: (i, 0)),
                   pl.BlockSpec((1, tt, w), lambda i, s_, ch, co: (s_, i, 0))])
    return pl.pallas_call(body, grid_spec=grid_spec,
                          out_shape=[jax.ShapeDtypeStruct((rh, w), f32), jax.ShapeDtypeStruct((S, rh, w), bf16)],
                          compiler_params=pltpu.CompilerParams(dimension_semantics=("arbitrary", "arbitrary")),
                          name=name)(chip.reshape(1).astype(jnp.int32), ci.reshape(1).astype(jnp.int32), g, recv)


def _sum_chip(name, own, others, ci):
    rh, w = own.shape
    tt = _pick(rh, 128, 16)
    nb = rh // tt

    def body(core_ref, a_ref, b_ref, o_ref):
        o_ref[...] = ((a_ref[...] + b_ref[0].astype(f32)) + b_ref[1].astype(f32)) + b_ref[2].astype(f32)

    grid_spec = pltpu.PrefetchScalarGridSpec(
        num_scalar_prefetch=1, grid=(nb,),
        in_specs=[pl.BlockSpec((tt, w), lambda i, co: (i, 0)),
                  pl.BlockSpec((3, tt, w), lambda i, co: (0, i, 0))],
        out_specs=pl.BlockSpec((tt, w), lambda i, co: (co[0] * nb + i, 0)))
    return pl.pallas_call(body, grid_spec=grid_spec, out_shape=jax.ShapeDtypeStruct((2 * rh, w), f32),
                          name=name)(ci.reshape(1).astype(jnp.int32), own, others)


def _sum_devices(gathered, late, late_rows, head_row):
    _, rows, width = gathered.shape

    def body(g_ref, l_ref, out_ref, head_ref):
        acc, acc_l = g_ref[0], l_ref[0]
        for d in range(1, 8):
            acc = acc + g_ref[d]
            acc_l = acc_l + l_ref[d]
        out_ref[...] = acc
        for k, r in enumerate(late_rows):
            out_ref[r:r + 1, :] = acc_l[k:k + 1, :]
        row = acc[head_row:head_row + 1, :]
        hs = row[:, 0:HEAD_DIM]
        for h in range(1, HEADS):
            hs = hs + row[:, h * HEAD_DIM:(h + 1) * HEAD_DIM]
        head_ref[...] = jnp.zeros_like(head_ref)
        head_ref[0:1, 0:HEAD_DIM] = hs

    return pl.pallas_call(
        body, out_shape=[jax.ShapeDtypeStruct((rows, width), f32), jax.ShapeDtypeStruct((8, LANES), f32)],
        name="small_grads_sum",
    )(gathered, late)


def _pack(arrs, rows_mult, dtype):
    flat = jnp.concatenate([a.reshape(-1).astype(dtype) for a in arrs])
    per = PACK_W * rows_mult
    total = -(-flat.shape[0] // per) * per
    return jnp.pad(flat, (0, total - flat.shape[0])).reshape(total // PACK_W, PACK_W)


def _unpack(buf, shapes):
    flat = buf.reshape(-1)
    out, off = [], 0
    for s in shapes:
        n = int(np.prod(s))
        out.append(flat[off:off + n].reshape(s))
        off += n
    return out


_BIG = ["w_in", "w_branch_gdn", "w_branch_rwkv", "w_out", "w_ffn_in", "w_ffn_out"]
_MID = ["conv_gdn", "conv_ffn", "w2", "a2", "g2"]
_SMALL =["b_ada", "norm1_w", "a_log", "dt_bias", "onorm_gdn", "mu_rwkv", "w0", "a0", "k_k", "k_a", "r_k", "lnx_w",
          "lnx_b", "norm2_w", "norm_f_w"]
_ORDER = ["w_ada", "b_ada", "norm1_w", "w_in", "conv_gdn", "a_log", "dt_bias", "onorm_gdn", "w_branch_gdn", "mu_rwkv",
          "w0", "w2", "a0", "a2", "g2", "k_k", "k_a", "r_k", "lnx_w", "lnx_b", "w_branch_rwkv", "w_out", "norm2_w",
          "w_ffn_in", "conv_ffn", "w_ffn_out", "norm_f_w"]


_WIN_SEGMENTS = [(0, 1536, 0), (2064, 3600, 1536), (1536, 2048, 3072), (3600, 3728, 3584), (2048, 2064, 3712),
                 (3728, 3888, 3840), (3888, 5936, 4096)]
_WIN_PADDED = 6144
_COL_QKV, _COL_RKV, _COL_Z = (0, 1, 2), (3, 4, 5), 6
_COL_LORA, _COL_BA = 28, 29
_COL_GATE_LORA = 15
_COL_GL = (4, 5)
_JOINT_QKV, _JOINT_RKV, _JOINT_Z, _JOINT_GL = 0, 1, 6, 2
_SMALL_BLOCKS_AT = 3584


def _win_pad(shards):
    n = shards.shape[2]
    parts, at = [], 0
    for lo, hi, dst in _WIN_SEGMENTS:
        if dst > at:
            parts.append(jnp.zeros((shards.shape[1], dst - at), shards.dtype))
        c = lo
        while c < hi:
            j = c // n
            e = min(hi, (j + 1) * n)
            parts.append(shards[j][:, c - j * n:e - j * n])
            c = e
        at = dst + hi - lo
    if at < _WIN_PADDED:
        parts.append(jnp.zeros((shards.shape[1], _WIN_PADDED - at), shards.dtype))
    return jnp.concatenate(parts, axis=1)


def _win_unpad_shards(g, n):
    shards = []
    for j in range(4):
        parts = []
        for lo, hi, dst in sorted(_WIN_SEGMENTS):
            a, b = max(lo, j * n), min(hi, (j + 1) * n)
            if a < b:
                parts.append(g[:, dst + a - lo:dst + b - lo])
        shards.append(jnp.concatenate(parts, axis=1))
    return jnp.stack(shards)


def kernel(x, c, w_ada, b_ada, norm1_w, w_in, conv_gdn, a_log, dt_bias, onorm_gdn, w_branch_gdn, mu_rwkv, w0, w2, a0, a2, g2, k_k, k_a, r_k, lnx_w, lnx_b, w_branch_rwkv, w_out, norm2_w, w_ffn_in, conv_ffn, w_ffn_out, norm_f_w, loss_target, m_w_ada, m_b_ada, m_norm1_w, m_w_in, m_conv_gdn, m_a_log, m_dt_bias, m_onorm_gdn, m_w_branch_gdn, m_mu_rwkv, m_w0, m_w2, m_a0, m_a2, m_g2, m_k_k, m_k_a, m_r_k, m_lnx_w, m_lnx_b, m_w_branch_rwkv, m_w_out, m_norm2_w, m_w_ffn_in, m_conv_ffn, m_w_ffn_out, m_norm_f_w, v_w_ada, v_b_ada, v_norm1_w, v_w_in, v_conv_gdn, v_a_log, v_dt_bias, v_onorm_gdn, v_w_branch_gdn, v_mu_rwkv, v_w0, v_w2, v_a0, v_a2, v_g2, v_k_k, v_k_a, v_r_k, v_lnx_w, v_lnx_b, v_w_branch_rwkv, v_w_out, v_norm2_w, v_w_ffn_in, v_conv_ffn, v_w_ffn_out, v_norm_f_w):
    args = dict(locals())
    W = {n: args[n] for n in _ORDER}
    Mo = {n: args["m_" + n] for n in _ORDER}
    Vo = {n: args["v_" + n] for n in _ORDER}
    shapes = {n: W[n].shape for n in _ORDER}
    sq = lambda a: a.reshape(a.shape[-2:]) if a.ndim == 3 else a.reshape(1, -1)
    row = lambda a: a.reshape(1, -1)

    xi, yi, ci = lax.axis_index("x"), lax.axis_index("y"), lax.axis_index("c")
    dev = 4 * xi + 2 * yi + ci
    chip = 2 * xi + yi

    x2 = x[0]
    tgt = loss_target[0]
    T, D = x2.shape
    tt_l = _pick(T, 512, CHUNK)
    tt_p = _pick(T, 256, CHUNK)
    tt_h = _pick(T, 128, CHUNK)

    mid_shapes = [shapes[n][1:] for n in _MID]
    small_blk = _pack([c] + [W[n] for n in _MID], 8, f32)
    small_all, win_s = _run_job("gather_c_mid_and_w_in", [_ag8_job(small_blk), _ag4_job([sq(W["w_in"]).astype(bf16)])])
    c_all = small_all[:, 0, :]
    per_chip = small_all[0::2].reshape(4, -1)[:, D:]
    mid = [dict(zip(_MID, _unpack(per_chip[j], mid_shapes))) for j in range(4)]
    catm = lambda n: jnp.concatenate([mid[j][n] for j in range(4)], axis=1)
    conv_gdn_f, conv_ffn_f = catm("conv_gdn"), catm("conv_ffn")
    w2f, a2f, g2f = catm("w2"), catm("a2"), catm("g2")

    later_weights = _ag4_job([sq(W[n]).astype(bf16) for n in _BIG[1:]])
    win_p = _win_pad(win_s)
    zpad = lambda a, top, bot: jnp.pad(a, ((top, bot), (0, 0)))
    w2p, a2p, g2p = zpad(w2f, 0, 64), zpad(a2f, 64, 0), zpad(g2f, 0, 96)

    ncol = shapes["w_ada"][2]
    b_cols = lax.dynamic_slice(sq(W["b_ada"]), (0, chip * ncol), (1, ncol))
    cond16, mod_cols = _ada_fwd(jnp.pad(c_all, ((0, 8), (0, 0))), sq(W["w_ada"]), b_cols)
    mod_all = _ag8("gather_mod", mod_cols[:8])
    mod_mine = lax.dynamic_slice(mod_all[0::2], (0, dev, 0), (4, 1, ncol)).reshape(1, 4 * ncol)
    shift1, scale1, gate1, shift2, scale2, gate2 = [mod_mine[:, i * D:(i + 1) * D] for i in range(6)]

    seg = _seg_matrix(WIDTH, HEAD_DIM)
    norm1 = [sq(W["norm1_w"]), shift1, scale1]
    h1 = _stage_fwd("norm_mod1", _fn_norm_mod, [_whole(x2)], norm1, [(D, bf16)], tt_l)[0]
    p = _matmul("in_proj", h1, win_p, "nn")

    cgq = [row(conv_gdn_f[j, part * WIDTH:(part + 1) * WIDTH]) for part in range(3) for j in range(4)]
    lane_pad = lambda a: jnp.pad(row(a), ((0, 0), (8, LANES - 16)))
    gdn_pre_ps = cgq + [lane_pad(W["a_log"]), lane_pad(W["dt_bias"]), seg, _chunk_tri(tt_p, CHUNK)]
    gdn_pre_ts = [(p, WIDTH, ci_, (0, 1, 2, 3)) for ci_ in _COL_QKV] + [(p, LANES, _COL_BA, None)]
    q_, k_, v_, beta_t, gc_t = _stage_fwd("gdn_pre", _fn_gdn_pre, gdn_pre_ts, gdn_pre_ps,
                                          [(WIDTH, f32)] * 3 + [(LANES, f32)] * 2, tt_p)
    o_, gdn_hist, gdn_inv, wbg_s, wbr_s, wout_s, wfi_s, wfo_s = _scan_fwd(
        "gdn_scan", _gdn_chunk, [q_, k_, v_], [beta_t, gc_t], job=later_weights)
    wout_f = wout_s.reshape(D, D)
    wfo = wfo_s.reshape(D_FF, D)
    ow512 = jnp.tile(row(W["onorm_gdn"]), (1, HEADS))
    gdn_post_ts = [_whole(o_), (p, WIDTH, _COL_Z, None)]
    ya = _stage_fwd("gdn_post", _fn_gdn_post, gdn_post_ts, [ow512, seg], [(WIDTH, bf16)], tt_l)[0]

    mu = sq(W["mu_rwkv"])
    rw_ps = [mu[:, 0:512], mu[:, 512:1024], mu[:, 1024:1536], mu[:, 1536:1664], jnp.pad(mu[:, 1664:1824], ((0, 0), (0, 96))),
             sq(W["w0"]), w2p, sq(W["a0"]), a2p, g2p, sq(W["k_k"]), sq(W["k_a"]), seg]
    rw_ts = [(p, WIDTH, ci_, (0, 1)) for ci_ in _COL_RKV] + [(p, LANES, _COL_LORA, (0, 1)),
                                                              (p, 256, _COL_GATE_LORA, (0, 1))]
    rw_out = _stage_fwd("rwkv_pre", _fn_rwkv_pre, rw_ts, rw_ps, [(WIDTH, f32)] * 7, tt_p)
    r_, lw_, k2_, vv_, na_, b_, g_ = rw_out
    rw_ins = [r_, lw_, k2_, vv_, na_, b_]
    y_, rw_hist, rw_inv = _scan_fwd("rwkv_scan", _rwkv_chunk, rw_ins)
    rwp_ps = [sq(W["lnx_w"]), sq(W["lnx_b"]), row(W["r_k"]), seg]
    rwp_ts = [_whole(y_), _whole(r_), _whole(k2_), _whole(vv_), _whole(g_)]
    yb = _stage_fwd("rwkv_post", _fn_rwkv_post, rwp_ts, rwp_ps, [(WIDTH, bf16)], tt_l)[0]

    big_a = _matmul("branch_gdn", ya, wbg_s, "nn", shards=4)
    big_b = _matmul("branch_rwkv", yb, wbr_s, "nn", shards=4)
    merge_ts = [(p, D, _COL_GL[0], None), (p, D, _COL_GL[1], None), _whole(big_a), _whole(big_b)]
    merged = _stage_fwd("merge", _fn_merge, merge_ts, [], [(D, bf16)], tt_l)[0]
    mo = _matmul("out_proj", merged, wout_f, "nn")
    norm2 = [gate1, sq(W["norm2_w"]), shift2, scale2]
    x1, h2 = _stage_fwd("resid_norm_mod2", _fn_resid_norm_mod, [_whole(x2), _whole(mo)], norm2, [(D, f32), (D, bf16)], tt_l)
    f = _matmul("ffn_in", h2, wfi_s, "nn", shards=4, tn=1408)
    cg_ps = [row(conv_ffn_f[j]) for j in range(3)]
    cg_ts = [(f, D_FF, 0, (0, 1, 2)), (f, D_FF, 1, None)]
    act = _stage_fwd("convglu", _fn_convglu, cg_ts, cg_ps, [(D_FF, bf16)], tt_h)[0]
    fo = _matmul("ffn_out", act, wfo, "nn", tk=D_FF)

    dx1_a, dfo, dgate2, dnormf, loss_part = _final_stage(x1, fo, tgt, gate2, row(W["norm_f_w"]), tt_l)

    dact = _matmul("d_act", dfo, wfo, "nt", tn=1408)
    g_wfo = _matmul("g_ffn_out", act, dfo, "tn", tm=1408, tk=2048)
    _, dcf, df = _stage_bwd("convglu_bwd", _fn_convglu, cg_ts, cg_ps, [[_whole(dact)]], tt_h, [True] * 2, [True] * 3,
                            joint=([0, 1], 2 * D_FF, 0, None), dtypes=[bf16])
    g_wfi = _matmul("g_ffn_in", h2, df, "tn", shards=4, tn=1408, tk=2048)
    g_wfo_s = g_wfo.reshape(4, D_FF // 4, D)
    dh2, recv_wfi, recv_wfo = _matmul("d_h2", df, wfi_s, "nt", shards=4, tk=1408, job=_sibling_swap_job([g_wfi, g_wfo_s]))
    (dx_a, dmo), (dgate1, dnorm2, dshift2, dscale2), _ = _stage_bwd(
        "resid_norm_mod2_bwd", _fn_resid_norm_mod, [_whole(x2), _whole(mo)], norm2,
        [[_whole(dx1_a)], [_whole(dh2)]], tt_l, [True, True], [True] * 4, dtypes=[f32, bf16])
    dmerged = _matmul("d_merged", dmo, wout_f, "nt")
    g_wout = _matmul("g_out_proj", merged, dmo, "tn", tk=T)
    (dbig_a, dbig_b), _, dp = _stage_bwd("merge_bwd", _fn_merge, merge_ts, [], [[_whole(dmerged)]], tt_l, [True] * 4, [],
                                         joint=([0, 1], p.shape[1], _JOINT_GL, None), dtypes=[bf16] * 3)
    g_wbg = _matmul("g_branch_gdn", ya, dbig_a, "tn", shards=4, tk=T)
    g_wbr = _matmul("g_branch_rwkv", yb, dbig_b, "tn", shards=4, tk=T)
    g_wout_s = g_wout.reshape(4, D // 4, D)
    dyb = _matmul("d_yb", dbig_b, wbr_s, "nt", shards=4)
    dya, *recv_mix = _matmul("d_ya", dbig_a, wbg_s, "nt", shards=4, job=_sibling_swap_job([g_wbg, g_wbr, g_wout_s]))

    (dy_, dr_p, dk2_p, dv_p, dg_p), (dlnxw, dlnxb, drk), _ = _stage_bwd(
        "rwkv_post_bwd", _fn_rwkv_post, rwp_ts, rwp_ps, [[_whole(dyb)]], tt_l, [True] * 5, [True, True, True, False])
    gs_a = [g_wbg, g_wbr, g_wout_s, g_wfi, g_wfo_s]
    pairs_a = [_add_half("grads_pair_sum%d" % (t + 1), g, r_, chip, ci)
               for t, (g, r_) in enumerate(zip(gs_a, recv_mix + [recv_wfi, recv_wfo]))]
    dr_c, dlw_c, dk2_c, dv_c, dna_c, db_c, *others_a = _scan_bwd(
        "rwkv_scan_bwd", _rwkv_chunk, rw_ins, [], rw_hist, rw_inv, dy_, job=_chip_exchange_job([pb for _, pb in pairs_a]))
    rw_cots = [[_whole(dr_p), _whole(dr_c)], [_whole(dlw_c)], [_whole(dk2_p), _whole(dk2_c)],
               [_whole(dv_p), _whole(dv_c)], [_whole(dna_c)], [_whole(db_c)], [_whole(dg_p)]]
    (dl_, dg_), rw_dp, dp = _stage_bwd("rwkv_pre_bwd", _fn_rwkv_pre, rw_ts, rw_ps, rw_cots, tt_p, [True] * 5,
                                       [True] * 12 + [False], joint=([0, 1, 2], p.shape[1], _JOINT_RKV, dp), dtypes=[bf16] * 3)
    dmu_r, dmu_k, dmu_v, dmu_l, dmu_g, dw0, dw2p, da0, da2p, dg2p, dkk, dka = rw_dp

    (do_,), (dow512,), dp = _stage_bwd("gdn_post_bwd", _fn_gdn_post, gdn_post_ts, [ow512, seg], [[_whole(dya)]], tt_l,
                                       [True, True], [True, False], joint=([1], p.shape[1], _JOINT_Z, dp), dtypes=[f32, bf16])
    d_gdn = _scan_bwd("gdn_scan_bwd", _gdn_chunk, [q_, k_, v_], [beta_t, gc_t], gdn_hist, gdn_inv, do_)
    gdn_cots = [[_whole(a)] for a in d_gdn]
    (dba,), gdn_dp, dp = _stage_bwd("gdn_pre_bwd", _fn_gdn_pre, gdn_pre_ts, gdn_pre_ps, gdn_cots, tt_p, [True] * 4,
                                    [True] * 14 + [False, False], joint=([0, 1, 2], p.shape[1], _JOINT_QKV, dp), dtypes=[bf16] * 2)
    dp = lax.dynamic_update_slice(dp, jnp.concatenate([dl_, dba, dg_], axis=1), (0, _SMALL_BLOCKS_AT))
    g_conv_gdn = jnp.concatenate([jnp.concatenate([gdn_dp[4 * part + j] for part in range(3)], axis=1) for j in range(4)], axis=0)
    g_conv_ffn = jnp.concatenate(dcf, axis=0)
    g_mu = jnp.concatenate([dmu_r, dmu_k, dmu_v, dmu_l, dmu_g[:, :160]], axis=1)
    late_zero = jnp.zeros((1, D), f32)
    dmod_early = jnp.concatenate([late_zero, late_zero, dgate1, dshift2, dscale2, dgate2], axis=1)
    small_parts = {"b_ada": dmod_early, "norm1_w": late_zero, "a_log": gdn_dp[12][:, 8:16], "dt_bias": gdn_dp[13][:, 8:16],
                   "mu_rwkv": g_mu, "w0": dw0, "a0": da0, "k_k": dkk, "k_a": dka, "r_k": drk, "lnx_w": dlnxw,
                   "lnx_b": dlnxb, "norm2_w": dnorm2, "norm_f_w": dnormf}
    small_names = [n for n in _SMALL if n != "onorm_gdn"]
    mid_full = [g_conv_gdn, g_conv_ffn, dw2p[0:64], da2p[64:128], dg2p[0:160]]
    body_rows = _pack([small_parts[n] for n in small_names] + [loss_part[:, 0:1]] + mid_full, 1, f32)
    head_row = body_rows.shape[0]
    small_g = jnp.concatenate([body_rows, jnp.pad(dow512, ((0, 0), (0, PACK_W - WIDTH)))], axis=0)
    small_g = jnp.pad(small_g, ((0, -small_g.shape[0] % 8), (0, 0)))

    g_win_pad, small_all_g = _matmul("g_in_proj", h1, dp, "tn", tk=T, job=_ag8_job(small_g))
    g_win_s = _win_unpad_shards(g_win_pad, shapes["w_in"][2])
    pair_win = _add_half("grads_pair_sum0", g_win_s, _run_job("w_in_grads_sibling_swap", _sibling_swap_job([g_win_s]))[0],
                         chip, ci)
    dh1, others_win = _matmul("d_h1", dp, win_p, "nt", tk=3072, job=_chip_exchange_job([pair_win[1]]))
    (grad_x,), (dnorm1, dshift1, dscale1), _ = _stage_bwd("norm_mod1_bwd", _fn_norm_mod_and_x, [_whole(x2)], norm1,
                                                          [[_whole(dh1)], [_whole(dx_a)]], tt_l, [True], [True] * 3)
    late_all = _ag8("gather_late_grads", jnp.pad(jnp.concatenate([dshift1, dscale1, dnorm1], axis=0), ((0, 5), (0, 0))))
    small_sum, head_sum = _sum_devices(small_all_g, late_all, (0, 1, 6), head_row)
    small_shapes = [shapes[n][1:] if n != "norm_f_w" else shapes[n] for n in small_names]
    un = _unpack(small_sum, small_shapes + [(1,)] + [g.shape for g in mid_full])
    small_grads = dict(zip(small_names, un))
    loss = un[len(small_names)].reshape(())
    small_grads["onorm_gdn"] = head_sum[0, 0:HEAD_DIM]
    for n, g in zip(_MID, un[len(small_names) + 1:]):
        wcols = shapes[n][2]
        small_grads[n] = lax.dynamic_slice(g, (0, chip * wcols), (g.shape[0], wcols))

    dmod_all = jnp.concatenate([late_all[:, 0, :], late_all[:, 1, :], small_all_g[:, 2:6, :].reshape(8, 4 * PACK_W)], axis=1)
    dmod_cols = lax.dynamic_slice(dmod_all, (0, chip * ncol), (8, ncol))
    g_wada = _matmul("g_w_ada", cond16, jnp.pad(dmod_cols, ((0, 8), (0, 0))), "tn")

    pairs = [pair_win] + pairs_a
    others = [others_win] + others_a
    halves = [_sum_chip("grads_chip_sum%d" % t, own, o_, ci) for t, ((own, _), o_) in enumerate(zip(pairs, others))]
    big_grads = dict(zip(_BIG, _rs_sibling_join(halves)))

    res = {tag: {} for tag in ("grad", "delta", "new_m", "new_v")}

    def put(n, g, d, m_, v_):
        for tag, val in zip(("grad", "delta", "new_m", "new_v"), (g, d, m_, v_)):
            res[tag][n] = val.reshape(shapes[n])

    for n in _BIG:
        put(n, big_grads[n], *_adamw("adamw_" + n, sq(W[n]), big_grads[n], sq(Mo[n]), sq(Vo[n])))
    put("w_ada", g_wada, *_adamw("adamw_w_ada", sq(W["w_ada"]), g_wada, sq(Mo["w_ada"]), sq(Vo["w_ada"])))
    rest = _SMALL + _MID
    pk = lambda d: _pack([d[n] for n in rest], 8, f32)
    sg = pk(small_grads)
    sm = _adamw("adamw_small", pk(W), sg, pk(Mo), pk(Vo))
    for tag, buf in zip(("grad", "delta", "new_m", "new_v"), (sg,) + tuple(sm)):
        res[tag].update(zip(rest, _unpack(buf, [shapes[n] for n in rest])))
    outs = [loss, grad_x.reshape(x.shape)]
    for tag in ("grad", "delta", "new_m", "new_v"):
        outs += [res[tag][n] for n in _ORDER]
    return tuple(outs)
```

```python
import numpy as np
import jax
import jax.numpy as jnp
from jax import lax
from jax.experimental import pallas as pl
from jax.experimental.pallas import tpu as pltpu

f32 = jnp.float32
bf16 = jnp.bfloat16

LANES = 128
HEADS = 8
HEAD_DIM = 64
WIDTH = HEADS * HEAD_DIM
CHUNK = 64
D_FF = 2816
NORM_EPS = 1e-6
LNX_EPS = 64e-5
PACK_W = 1024
MESH_ID = pl.DeviceIdType.MESH

ADAM_LR, ADAM_B1, ADAM_B2, ADAM_EPS, ADAM_WD, ADAM_STEP = 0.001, 0.9, 0.999, 1e-08, 0.01, 10


def _pick(n, target, mult):
    if n <= target:
        return n
    best = None
    for t in range(mult, target + 1, mult):
        if n % t == 0:
            best = t
    assert best is not None, (n, target, mult)
    return best


def _split_bf16(x, n):
    parts, r = [], x
    for i in range(n):
        p = r.astype(bf16)
        parts.append(p)
        if i + 1 < n:
            r = r - p.astype(f32)
    return parts


def _xdot_r_impl(x, m, n, dims):
    acc = None
    for p in _split_bf16(x, n):
        t = lax.dot_general(p, m, dims, preferred_element_type=f32)
        acc = t if acc is None else acc + t
    return acc


def _make_xdot_r(n):
    nn = (((1,), (0,)), ((), ()))
    nt = (((1,), (1,)), ((), ()))

    @jax.custom_vjp
    def xdot(x, m):
        return _xdot_r_impl(x, m, n, nn)

    def fwd(x, m):
        return _xdot_r_impl(x, m, n, nn), m

    def bwd(m, ct):
        return _xdot_r_impl(ct, m, n, nt), jnp.zeros_like(m)

    xdot.defvjp(fwd, bwd)
    return xdot


_segsum = _make_xdot_r(2)


def _xdot_l_impl(m, x, n, dims):
    acc = None
    for p in _split_bf16(x, n):
        t = lax.dot_general(m, p, dims, preferred_element_type=f32)
        acc = t if acc is None else acc + t
    return acc


@jax.custom_vjp
def _xdot_l(m, x):
    return _xdot_l_impl(m, x, 3, (((1,), (0,)), ((), ())))


def _xdot_l_fwd(m, x):
    return _xdot_l(m, x), m


def _xdot_l_bwd(m, ct):
    return jnp.zeros_like(m), _xdot_l_impl(m, ct, 3, (((0,), (0,)), ((), ())))


_xdot_l.defvjp(_xdot_l_fwd, _xdot_l_bwd)


@jax.custom_vjp
def _bdot(x, w):
    return jnp.dot(x.astype(bf16), w.astype(bf16), preferred_element_type=f32)


def _bdot_fwd(x, w):
    return _bdot(x, w), (x, w)


def _bdot_bwd(res, ct):
    x, w = res
    c = ct.astype(bf16)
    dx = lax.dot_general(c, w.astype(bf16), (((1,), (1,)), ((), ())), preferred_element_type=f32)
    dw = lax.dot_general(x.astype(bf16), c, (((0,), (0,)), ((), ())), preferred_element_type=f32)
    return dx, dw


_bdot.defvjp(_bdot_fwd, _bdot_bwd)


def _silu(x):
    return x * jax.nn.sigmoid(x)


def _softplus(x):
    return jnp.maximum(x, 0.0) + jnp.log(1.0 + jnp.exp(-jnp.abs(x)))


def _rms(x, w, eps):
    return x * lax.rsqrt(jnp.mean(x * x, axis=-1, keepdims=True) + eps) * w


def _seg_matrix(width, seg):
    i = np.arange(width)
    return jnp.asarray((i[:, None] // seg) == (i[None, :] // seg), dtype=bf16)


def _chunk_tri(rows, chunk):
    i = np.arange(rows)
    return jnp.asarray(((i[:, None] // chunk) == (i[None, :] // chunk)) & (i[:, None] >= i[None, :]), dtype=bf16)


HALO = 8


def _full_spec(shape):
    nd = len(shape)
    return pl.BlockSpec(shape, lambda i: (0,) * nd)


def _entry_specs(entries, tt, block_of):
    specs, ops = [], []
    for arr, w, ci, shifts in entries:
        specs.append(pl.BlockSpec((tt, w), lambda i, ci=ci: (block_of(i), ci)))
        ops.append(arr)
        if shifts:
            specs.append(pl.BlockSpec((HALO, w), lambda i, ci=ci: (jnp.maximum(block_of(i) * (tt // HALO) - 1, 0), ci)))
            ops.append(arr)
    return specs, ops


def _load_entries(entries, refs, first):
    tiles, k = [], 0
    for _, w, _, shifts in entries:
        x = refs[k][...].astype(f32)
        k += 1
        if not shifts:
            tiles.append(x)
            continue
        halo = jnp.where(first, 0.0, refs[k][...].astype(f32))
        k += 1
        row = lax.broadcasted_iota(jnp.int32, (HALO, w), 0)
        for s in shifts:
            if s == 0:
                tiles.append(x)
                continue
            r = pltpu.roll(x, s, 0)
            head = jnp.where(row < s, pltpu.roll(halo, s, 0), r[0:HALO])
            tiles.append(jnp.concatenate([head, r[HALO:]], axis=0))
    return tiles


def _unshift_sum(grads, shifts, carry, tt):
    w = grads[0].shape[1]
    row = lax.broadcasted_iota(jnp.int32, (tt, w), 0)
    row8 = lax.broadcasted_iota(jnp.int32, (HALO, w), 0)
    dx, out = None, jnp.zeros((HALO, w), f32)
    for d, s in zip(grads, shifts):
        if s == 0:
            part = d
        else:
            part = jnp.where(row < tt - s, pltpu.roll(d, tt - s, 0), 0.0)
            out = out + jnp.where(row8 >= HALO - s, pltpu.roll(d[0:HALO], HALO - s, 0), 0.0)
        dx = part if dx is None else dx + part
    return jnp.concatenate([dx[:tt - HALO], dx[tt - HALO:] + carry], axis=0), out


def _stage_fwd(name, fn, tiles, params, outs, tt):
    rows = tiles[0][0].shape[0]
    npar = len(params)
    specs, ops = _entry_specs(tiles, tt, lambda i: i)
    nin = len(ops)

    def body(*refs):
        ts = _load_entries(tiles, refs[:nin], pl.program_id(0) == 0)
        ps = [r[...] for r in refs[nin:nin + npar]]
        res = fn(ps, ts)
        for r, v in zip(refs[nin + npar:], res):
            r[...] = v.astype(r.dtype)

    return pl.pallas_call(
        body, grid=(rows // tt,),
        in_specs=specs + [_full_spec(p.shape) for p in params],
        out_specs=[pl.BlockSpec((tt, w), lambda i: (i, 0)) for (w, _) in outs],
        out_shape=[jax.ShapeDtypeStruct((rows, w), dt) for (w, dt) in outs],
        compiler_params=pltpu.CompilerParams(dimension_semantics=("parallel",)),
        name=name,
    )(*ops, *params)


def _stage_bwd(name, fn, tiles, params, cots, tt, tile_grad, param_grad, joint=None, dtypes=None):
    rows = tiles[0][0].shape[0]
    nblk = rows // tt
    npar = len(params)
    block_of = lambda i: nblk - 1 - i
    specs, ops = _entry_specs(tiles, tt, block_of)
    nin = len(ops)
    flat_cots = [c for group in cots for c in group]
    groups = [len(g) for g in cots]
    ncot = len(flat_cots)
    counts = [len(e[3]) if e[3] else 1 for e in tiles]
    dt_entries = [e for e, g in zip(tiles, tile_grad) if g]
    dp_shapes = [p.shape for p, g in zip(params, param_grad) if g]
    ndt = len(dt_entries)
    carry_w = [e[1] for e in dt_entries if e[3]]
    flags = [g for g, n in zip(tile_grad, counts) for _ in range(n)]
    members, j_width, j_cidx, j_buf = joint if joint else ([], 0, 0, None)
    solo = [k for k in range(ndt) if k not in members]
    nsolo, njoint, nbuf = len(solo), int(bool(members)), int(j_buf is not None)
    j_block = sum(dt_entries[k][1] for k in members)
    dtypes = list(dtypes) if dtypes else [f32] * (nsolo + njoint)

    def body(*refs):
        i = pl.program_id(0)
        p_refs = refs[nin:nin + npar]
        c_refs = refs[nin + npar:nin + npar + ncot]
        base = nin + npar + ncot + nbuf
        dt_refs = refs[base:base + nsolo]
        joint_refs = refs[base + nsolo:base + nsolo + njoint]
        dp_refs = refs[base + nsolo + njoint:base + nsolo + njoint + len(dp_shapes)]
        carry_refs = refs[base + nsolo + njoint + len(dp_shapes):]
        ts = _load_entries(tiles, refs[:nin], block_of(i) == 0)
        ps = [r[...] for r in p_refs]

        def f(dp, dt):
            dp, dt = iter(dp), iter(dt)
            pp = [next(dp) if g else p for p, g in zip(ps, param_grad)]
            tl = [next(dt) if g else t for t, g in zip(ts, flags)]
            return fn(pp, tl)

        _, vjp = jax.vjp(f, [p for p, g in zip(ps, param_grad) if g], [t for t, g in zip(ts, flags) if g])
        cs, j = [], 0
        for n in groups:
            acc = c_refs[j][...].astype(f32)
            for q in range(1, n):
                acc = acc + c_refs[j + q][...].astype(f32)
            cs.append(acc)
            j += n
        gp, gt = vjp(cs)

        @pl.when(i == 0)
        def _():
            for r in dp_refs:
                r[...] = jnp.zeros_like(r)
            for r in carry_refs:
                r[...] = jnp.zeros_like(r)

        gt, k, kc, dxs = list(gt), 0, 0, []
        for e, n in zip(dt_entries, [n for n, g in zip(counts, tile_grad) if g]):
            if e[3]:
                dx, out = _unshift_sum(gt[k:k + n], e[3], carry_refs[kc][...], tt)
                carry_refs[kc][...] = out
                kc += 1
            else:
                dx = gt[k]
            dxs.append(dx)
            k += n
        for r, k in zip(dt_refs, solo):
            r[...] = dxs[k].astype(r.dtype)
        off = 0
        for k in members:
            w = dt_entries[k][1]
            joint_refs[0][:, off:off + w] = dxs[k].astype(joint_refs[0].dtype)
            off += w
        for r, v in zip(dp_refs, gp):
            r[...] += v

    res = pl.pallas_call(
        body, grid=(nblk,),
        in_specs=specs + [_full_spec(p.shape) for p in params]
        + [pl.BlockSpec((tt, w), lambda i, ci=ci: (block_of(i), ci)) for (_, w, ci, *_) in flat_cots]
        + [pl.BlockSpec(memory_space=pl.ANY)] * nbuf,
        out_specs=[pl.BlockSpec((tt, dt_entries[k][1]), lambda i: (block_of(i), 0)) for k in solo]
        + [pl.BlockSpec((tt, j_block), lambda i: (block_of(i), j_cidx))] * njoint
        + [_full_spec(s) for s in dp_shapes],
        out_shape=[jax.ShapeDtypeStruct((rows, dt_entries[k][1]), dt) for k, dt in zip(solo, dtypes)]
        + [jax.ShapeDtypeStruct((rows, j_width), dtypes[-1])] * njoint
        + [jax.ShapeDtypeStruct(s, f32) for s in dp_shapes],
        scratch_shapes=[pltpu.VMEM((HALO, w), f32) for w in carry_w],
        input_output_aliases={nin + npar + ncot: nsolo} if nbuf else {},
        compiler_params=pltpu.CompilerParams(dimension_semantics=("arbitrary",)),
        name=name,
    )(*ops, *params, *[c[0] for c in flat_cots], *([j_buf] if nbuf else []))
    res = list(res)
    return res[:nsolo], res[nsolo + njoint:], (res[nsolo] if njoint else None)


def _whole(a):
    return (a, a.shape[1], 0, None)


def _matmul(name, a, b, mode, out_dtype=f32, tm=1024, tn=1024, tk=1024, shards=1, job=None):
    S = shards
    if mode == "nn":
        M, K = a.shape
        w = b.shape[-1]
    elif mode == "nt":
        M = a.shape[0]
        if S > 1:
            _, N, w = b.shape
            K = S * w
        else:
            N, K = b.shape
            w = K
    else:
        K, M = a.shape
        w = b.shape[1] // S
    if mode != "nt":
        N = S * w
    tm = _pick(M, tm, LANES)
    if mode == "nt":
        tn = _pick(N, tn, LANES)
        tk = _pick(w, tk, LANES)
    else:
        tn = _pick(w, tn, LANES)
        tk = _pick(K, tk, LANES if mode == "nn" else 16)
    nk = K // tk
    nb = w // (tk if mode == "nt" else tn)
    if mode == "nn":
        a_spec = pl.BlockSpec((tm, tk), lambda i, j, k: (i, k))
        if S > 1:
            b_spec = pl.BlockSpec((1, tk, tn), lambda i, j, k: (j // nb, k, j % nb))
        else:
            b_spec = pl.BlockSpec((tk, tn), lambda i, j, k: (k, j))
        dims = (((1,), (0,)), ((), ()))
    elif mode == "nt":
        a_spec = pl.BlockSpec((tm, tk), lambda i, j, k: (i, k))
        if S > 1:
            b_spec = pl.BlockSpec((1, tn, tk), lambda i, j, k: (k // nb, j, k % nb))
        else:
            b_spec = pl.BlockSpec((tn, tk), lambda i, j, k: (j, k))
        dims = (((1,), (1,)), ((), ()))
    else:
        a_spec = pl.BlockSpec((tk, tm), lambda i, j, k: (k, i))
        b_spec = pl.BlockSpec((tk, tn), lambda i, j, k: (k, j))
        dims = (((0,), (0,)), ((), ()))
    if mode == "tn" and S > 1:
        o_spec = pl.BlockSpec((1, tm, tn), lambda i, j, k: (j // nb, i, j % nb))
        o_shape = (S, M, w)
    else:
        o_spec = pl.BlockSpec((tm, tn), lambda i, j, k: (i, j))
        o_shape = (M, N)
    b_lead = S > 1 and mode != "tn"
    o_lead = S > 1 and mode == "tn"

    j_ins, j_outs, j_sems = _job_parts(job)
    nji, njo = len(j_ins), len(j_outs)
    grid = (M // tm, N // tn, nk)

    def run_job(refs):
        step = (pl.program_id(0) * grid[1] + pl.program_id(1)) * grid[2] + pl.program_id(2)
        _job_steps(job, refs[2:2 + nji], refs[3 + nji:3 + nji + njo], refs[len(refs) - len(j_sems):], step,
                   grid[0] * grid[1] * grid[2] - 1)

    def body(*refs):
        a_ref, b_ref, o_ref, acc_ref = refs[0], refs[1], refs[2 + nji], refs[3 + nji + njo]
        run_job(refs)
        k = pl.program_id(2)

        @pl.when(k == 0)
        def _():
            acc_ref[...] = jnp.zeros_like(acc_ref)

        bv = b_ref[0] if b_lead else b_ref[...]
        acc_ref[...] += lax.dot_general(a_ref[...].astype(bf16), bv.astype(bf16), dims, preferred_element_type=f32)

        @pl.when(k == nk - 1)
        def _():
            if o_lead:
                o_ref[0] = acc_ref[...].astype(o_ref.dtype)
            else:
                o_ref[...] = acc_ref[...].astype(o_ref.dtype)

    def body_one_step(*refs):
        a_ref, b_ref, o_ref = refs[0], refs[1], refs[2 + nji]
        run_job(refs)
        bv = b_ref[0] if b_lead else b_ref[...]
        res = lax.dot_general(a_ref[...].astype(bf16), bv.astype(bf16), dims, preferred_element_type=f32)
        if o_lead:
            o_ref[0] = res.astype(o_ref.dtype)
        else:
            o_ref[...] = res.astype(o_ref.dtype)

    res = pl.pallas_call(
        body if nk > 1 else body_one_step, grid=grid,
        in_specs=[a_spec, b_spec] + [_ANY] * nji,
        out_specs=[o_spec] + [_ANY] * njo,
        out_shape=[jax.ShapeDtypeStruct(o_shape, out_dtype)] + j_outs,
        scratch_shapes=([pltpu.VMEM((tm, tn), f32)] if nk > 1 else []) + j_sems,
        compiler_params=pltpu.CompilerParams(
            dimension_semantics=("arbitrary",) * 3 if job else ("parallel", "parallel", "arbitrary")),
        name=name,
    )(a, b, *j_ins)
    return res if job else res[0]


def _make_bmm(precision):
    if precision is None:
        cast, kw = (lambda v: v.astype(bf16)), {}
    else:
        cast, kw = (lambda v: v), {"precision": precision}

    def nn(a, b):
        return jnp.einsum("hij,hjk->hik", cast(a), cast(b), preferred_element_type=f32, **kw)

    def nt(a, b):
        return jnp.einsum("hik,hjk->hij", cast(a), cast(b), preferred_element_type=f32, **kw)

    def tn(a, b):
        return jnp.einsum("hki,hkj->hij", cast(a), cast(b), preferred_element_type=f32, **kw)

    if precision is not None:
        return nn, nt, tn
    nn_v, nt_v, tn_v = jax.custom_vjp(nn), jax.custom_vjp(nt), jax.custom_vjp(tn)
    keep = lambda f: (lambda a, b: (f(a, b), (a, b)))
    nn_v.defvjp(keep(nn), lambda r, ct: (nt(ct, r[1]), tn(r[0], ct)))
    nt_v.defvjp(keep(nt), lambda r, ct: (nn(ct, r[1]), tn(ct, r[0])))
    tn_v.defvjp(keep(tn), lambda r, ct: (nt(r[1], ct), nn(r[0], ct)))
    return nn_v, nt_v, tn_v


_bmm, _bmm_nt, _bmm_tn = _make_bmm(None)
_bmm_exact = _make_bmm(lax.Precision.HIGH)[0]


def _masks(n):
    r = lax.broadcasted_iota(jnp.int32, (n, n), 0)
    c = lax.broadcasted_iota(jnp.int32, (n, n), 1)
    return (r >= c)[None], (r > c)[None], (r == c)[None]


_INV_BLOCK = 8


def _nilpotent_inverse(m, eye):
    p = eye + m
    for _ in range(2):
        m = _bmm(m, m)
        p = p + _bmm(p, m)
    return p


def _neumann_inverse_impl(m):
    n = m.shape[1]
    assert n == _INV_BLOCK * _INV_BLOCK
    r = lax.broadcasted_iota(jnp.int32, (n, n), 0)
    c = lax.broadcasted_iota(jnp.int32, (n, n), 1)
    eye = (r == c).astype(f32)[None]
    inside = jnp.where((r // _INV_BLOCK == c // _INV_BLOCK)[None], m, 0.0)
    d_inv = _nilpotent_inverse(inside, eye)
    return _bmm(_nilpotent_inverse(_bmm(d_inv, m - inside), eye), d_inv)


@jax.custom_vjp
def _neumann_inverse(m):
    return _neumann_inverse_impl(m)


def _neumann_inverse_fwd(m):
    p = _neumann_inverse_impl(m)
    return p, p


def _neumann_inverse_bwd(p, ct):
    return (_bmm_tn(p, _bmm_nt(ct, p)),)


_neumann_inverse.defvjp(_neumann_inverse_fwd, _neumann_inverse_bwd)


@jax.custom_vjp
def _given_inverse(m, p):
    return p


_given_inverse.defvjp(lambda m, p: (p, p), lambda p, ct: (_neumann_inverse_bwd(p, ct)[0], jnp.zeros_like(p)))


def _gdn_chunk(s, q, k, v, beta, gc, gr, gl, p=None):
    n = q.shape[1]
    causal, strict, _ = _masks(n)
    decay = jnp.where(causal, jnp.exp(jnp.where(causal, gc - gr, 0.0)), 0.0)
    kb = k * beta
    vb = v * beta
    lower = jnp.where(strict, _bmm_nt(kb, k) * decay, 0.0)
    t_mat = _neumann_inverse(-lower) if p is None else _given_inverse(-lower, p)
    egc = jnp.exp(gc)
    u = _bmm(t_mat, vb)
    w = _bmm(t_mat, kb * egc)
    attn = jnp.where(causal, _bmm_nt(q, k) * decay, 0.0)
    v_new = u - _bmm(w, s)
    o = _bmm(q * egc, s) + _bmm(attn, v_new)
    k_dec = k * jnp.exp(gl - gc)
    s_new = s * jnp.exp(gl) + _bmm_tn(k_dec, v_new)
    return s_new, o, t_mat


def _rwkv_chunk(s, r, lw, k, v, a, b, p=None):
    n = r.shape[1]
    causal, strict, _ = _masks(n)
    tri = jnp.broadcast_to(causal.astype(f32), (r.shape[0], n, n))
    lc = _bmm_exact(tri, lw)
    ein = jnp.exp(lc)
    eout = jnp.exp(-lc)
    a_t = a * jnp.exp(lc - lw)
    b_t = b * eout
    k_t = k * eout
    r_t = r * ein
    a_ab = jnp.where(strict, _bmm_nt(a_t, b_t), 0.0)
    a_ak = jnp.where(strict, _bmm_nt(a_t, k_t), 0.0)
    inv = _neumann_inverse(a_ab) if p is None else _given_inverse(a_ab, p)
    u = _bmm(inv, _bmm_nt(a_t, s) + _bmm(a_ak, v))
    y = (_bmm_nt(r_t, s) + _bmm(jnp.where(causal, _bmm_nt(r_t, b_t), 0.0), u)
         + _bmm(jnp.where(causal, _bmm_nt(r_t, k_t), 0.0), v))
    e_last = jnp.exp(jnp.sum(lw, axis=1, keepdims=True))
    s_new = s * e_last + _bmm_tn(u, b_t * e_last) + _bmm_tn(v, k_t * e_last)
    return s_new, y, inv


def _heads_in(ref, rows):
    return jnp.stack([ref[rows, h * HEAD_DIM:(h + 1) * HEAD_DIM] for h in range(HEADS)], axis=0)


def _heads_out(ref, rows, val):
    for h in range(HEADS):
        ref[rows, h * HEAD_DIM:(h + 1) * HEAD_DIM] = val[h]


def _gdn_scalars(bt, gt):
    n = bt.shape[0]
    gtt = gt.T
    hs = range(HEADS)
    return [jnp.stack([bt[:, h:h + 1] for h in hs], axis=0),
            jnp.stack([gt[:, HEADS + h:HEADS + h + 1] for h in hs], axis=0),
            jnp.stack([gtt[HEADS + h:HEADS + h + 1, :] for h in hs], axis=0),
            jnp.stack([gt[n - 1:n, HEADS + h:HEADS + h + 1] for h in hs], axis=0)]


def _gdn_scalars_back(dbeta, dgc, dgr, dgl):
    n = dbeta.shape[1]
    lane = lax.broadcasted_iota(jnp.int32, (n, LANES), 1)
    row = lax.broadcasted_iota(jnp.int32, (n, LANES), 0)
    sub = lax.broadcasted_iota(jnp.int32, (LANES, n), 0)
    db = jnp.zeros((n, LANES), f32)
    dg = jnp.zeros((n, LANES), f32)
    dgt = jnp.zeros((LANES, n), f32)
    for h in range(HEADS):
        db = jnp.where(lane == h, dbeta[h], db)
        dg = jnp.where(lane == HEADS + h, dgc[h] + jnp.where(row == n - 1, dgl[h], 0.0), dg)
        dgt = jnp.where(sub == HEADS + h, dgr[h], dgt)
    return [db, dg + dgt.T]


SCAN_GROUP = 2


def _scan_steps(t):
    g = SCAN_GROUP if (t // CHUNK) % SCAN_GROUP == 0 else 1
    return g, t // (CHUNK * g)


def _scan_spec(width, g, n, reverse):
    if reverse:
        return pl.BlockSpec((g * CHUNK, width), lambda i: (n - 1 - i, 0))
    return pl.BlockSpec((g * CHUNK, width), lambda i: (i, 0))


def _hist_spec(g, n, reverse):
    blk = (g, HEADS, HEAD_DIM, HEAD_DIM)
    if reverse:
        return pl.BlockSpec(blk, lambda i: (n - 1 - i, 0, 0, 0))
    return pl.BlockSpec(blk, lambda i: (i, 0, 0, 0))


def _job_list(job):
    return [] if job is None else (list(job) if isinstance(job, (list, tuple)) else [job])


def _job_parts(job):
    jobs = _job_list(job)
    sems = [pltpu.SemaphoreType.DMA((j["nsem"],)) for j in jobs for _ in range(2)]
    return [a for j in jobs for a in j["ins"]], [o for j in jobs for o in j["out_shapes"]], sems


def _job_phase(job, phase, in_refs, out_refs, sems):
    ki = ko = 0
    for n, j in enumerate(_job_list(job)):
        ni, no = len(j["ins"]), len(j["out_shapes"])
        j[phase](in_refs[ki:ki + ni], out_refs[ko:ko + no], sems[2 * n], sems[2 * n + 1])
        ki, ko = ki + ni, ko + no


def _job_steps(job, in_refs, out_refs, sems, step, last):
    if job is None:
        return

    @pl.when(step == 0)
    def _():
        _job_phase(job, "start", in_refs, out_refs, sems)

    @pl.when(step == last)
    def _():
        _job_phase(job, "finish", in_refs, out_refs, sems)


def _scan_fwd(name, fn, rows_in, scal_in=(), job=None):
    t = rows_in[0].shape[0]
    grp, n = _scan_steps(t)
    nr, ns = len(rows_in), len(scal_in)
    j_ins, j_outs, j_sems = _job_parts(job)
    nji, njo = len(j_ins), len(j_outs)

    def body(*refs):
        o_ref, sh_ref, ph_ref = refs[nr + ns + nji:nr + ns + nji + 3]
        s_scr = refs[nr + ns + nji + 3 + njo]
        _job_steps(job, refs[nr + ns:nr + ns + nji], refs[nr + ns + nji + 3:nr + ns + nji + 3 + njo],
                   refs[nr + ns + nji + 3 + njo + 1:], pl.program_id(0), n - 1)

        @pl.when(pl.program_id(0) == 0)
        def _():
            s_scr[...] = jnp.zeros_like(s_scr)

        s = s_scr[...]
        for sub in range(grp):
            rows = slice(sub * CHUNK, (sub + 1) * CHUNK)
            sh_ref[sub] = s
            ins = [_heads_in(r, rows) for r in refs[:nr]]
            if ns:
                ins += _gdn_scalars(*[r[rows, :] for r in refs[nr:nr + ns]])
            s, o, p = fn(s, *ins)
            _heads_out(o_ref, rows, o)
            ph_ref[sub] = p
        s_scr[...] = s

    return pl.pallas_call(
        body, grid=(n,),
        in_specs=[_scan_spec(a.shape[1], grp, n, False) for a in (*rows_in, *scal_in)] + [_ANY] * nji,
        out_specs=[_scan_spec(WIDTH, grp, n, False), _hist_spec(grp, n, False), _hist_spec(grp, n, False)] + [_ANY] * njo,
        out_shape=[jax.ShapeDtypeStruct((t, WIDTH), f32)]
        + [jax.ShapeDtypeStruct((t // CHUNK, HEADS, HEAD_DIM, HEAD_DIM), f32)] * 2 + j_outs,
        scratch_shapes=[pltpu.VMEM((HEADS, HEAD_DIM, HEAD_DIM), f32)] + j_sems,
        compiler_params=pltpu.CompilerParams(dimension_semantics=("arbitrary",)),
        name=name,
    )(*rows_in, *scal_in, *j_ins)


def _scan_bwd(name, fn, rows_in, scal_in, s_hist, p_hist, d_out, job=None):
    t = rows_in[0].shape[0]
    grp, n = _scan_steps(t)
    nr, ns = len(rows_in), len(scal_in)
    j_ins, j_outs, j_sems = _job_parts(job)
    nji, njo = len(j_ins), len(j_outs)

    def body(*refs):
        sh_ref, ph_ref, do_ref = refs[nr + ns:nr + ns + 3]
        base = nr + ns + 3 + nji
        g_refs = refs[base:base + nr + ns]
        ds_scr = refs[base + nr + ns + njo]
        _job_steps(job, refs[nr + ns + 3:base], refs[base + nr + ns:base + nr + ns + njo],
                   refs[base + nr + ns + njo + 1:], pl.program_id(0), n - 1)

        @pl.when(pl.program_id(0) == 0)
        def _():
            ds_scr[...] = jnp.zeros_like(ds_scr)

        ds = ds_scr[...]
        for sub in reversed(range(grp)):
            rows = slice(sub * CHUNK, (sub + 1) * CHUNK)
            ins = [_heads_in(r, rows) for r in refs[:nr]]
            if ns:
                ins += _gdn_scalars(*[r[rows, :] for r in refs[nr:nr + ns]])
            p = ph_ref[sub]
            _, vjp = jax.vjp(lambda s, *a, p=p: fn(s, *a, p=p)[:2], sh_ref[sub], *ins)
            g = vjp((ds, _heads_in(do_ref, rows)))
            ds = g[0]
            for r, v in zip(g_refs[:nr], g[1:1 + nr]):
                _heads_out(r, rows, v)
            if ns:
                for r, v in zip(g_refs[nr:], _gdn_scalars_back(*g[1 + nr:])):
                    r[rows, :] = v
        ds_scr[...] = ds

    arrs = (*rows_in, *scal_in)
    return pl.pallas_call(
        body, grid=(n,),
        in_specs=[_scan_spec(a.shape[1], grp, n, True) for a in arrs]
        + [_hist_spec(grp, n, True), _hist_spec(grp, n, True), _scan_spec(WIDTH, grp, n, True)] + [_ANY] * nji,
        out_specs=[_scan_spec(a.shape[1], grp, n, True) for a in arrs] + [_ANY] * njo,
        out_shape=[jax.ShapeDtypeStruct(a.shape, f32) for a in arrs] + j_outs,
        scratch_shapes=[pltpu.VMEM((HEADS, HEAD_DIM, HEAD_DIM), f32)] + j_sems,
        compiler_params=pltpu.CompilerParams(dimension_semantics=("arbitrary",)),
        name=name,
    )(*arrs, s_hist, p_hist, d_out, *j_ins)


def _fn_norm_mod(ps, ts):
    nw, shift, scale = ps
    (x,) = ts
    return [_rms(x, nw, NORM_EPS) * (1.0 + scale) + shift]


def _fn_norm_mod_and_x(ps, ts):
    return _fn_norm_mod(ps, ts) + [ts[0]]


def _fn_resid_norm_mod(ps, ts):
    gate, nw, shift, scale = ps
    x, mo = ts
    x1 = x + gate * mo
    return [x1, _rms(x1, nw, NORM_EPS) * (1.0 + scale) + shift]


def _fn_gdn_pre(ps, ts):
    cw = ps[:12]
    alog, dtb, seg, tri = ps[12:]
    ba = ts[12]
    outs = []
    for part in range(3):
        x = ts[4 * part:4 * part + 4]
        w = cw[4 * part:4 * part + 4]
        conv = w[3] * x[0] + w[2] * x[1] + w[1] * x[2] + w[0] * x[3]
        u = _silu(conv)
        if part < 2:
            u = u * lax.rsqrt(_segsum(u * u, seg) + 1e-6)
            if part == 0:
                u = u * (HEAD_DIM ** -0.5)
        outs.append(u)
    beta = jax.nn.sigmoid(ba)
    g = -jnp.exp(alog) * _softplus(ba + dtb)
    gc = _xdot_l(tri, g)
    return outs + [beta, gc]


def _fn_gdn_post(ps, ts):
    ow, seg = ps
    o, z = ts
    ms = _segsum(o * o, seg) * (1.0 / HEAD_DIM)
    return [o * lax.rsqrt(ms + NORM_EPS) * ow * _silu(z)]


def _fn_rwkv_pre(ps, ts):
    mu_r, mu_k, mu_v, mu_l, mu_g, w0, w2p, a0, a2p, g2p, k_k, k_a, seg = ps
    r0, r1, k0, k1, v0, v1, l0, l1, g0, g1 = ts
    xr = r0 + (r1 - r0) * mu_r
    xk = k0 + (k1 - k0) * mu_k
    xv = v0 + (v1 - v0) * mu_v
    xl = l0 + (l1 - l0) * mu_l
    xg = g0 + (g1 - g0) * mu_g
    w = -_softplus(-(w0 + _bdot(jnp.tanh(xl), w2p))) - 0.5
    lw = -jnp.exp(w)
    a = jax.nn.sigmoid(a0 + _bdot(xl, a2p))
    g = _bdot(jax.nn.sigmoid(xg), g2p)
    kk = xk * k_k
    kk = kk * lax.rsqrt(_segsum(kk * kk, seg) + 1e-6)
    k2 = xk * (1.0 + (a - 1.0) * k_a)
    return [xr, lw, k2, xv, -kk, kk * a, g]


def _fn_rwkv_post(ps, ts):
    lw_, lb_, rk, seg = ps
    y, r, k2, v, g = ts
    inv = 1.0 / HEAD_DIM
    yc = y - _segsum(y, seg) * inv
    var = _segsum(yc * yc, seg) * inv
    yn = yc * lax.rsqrt(var + LNX_EPS) * lw_ + lb_
    bonus = _segsum(r * k2 * rk, seg) * v
    return [(yn + bonus) * g]


def _fn_merge(ps, ts):
    gla, glb, ya, yb = ts
    return [jax.nn.sigmoid(gla) * ya + jax.nn.sigmoid(glb) * yb]


def _fn_convglu(ps, ts):
    c0, c1, c2 = ps
    g0, g1, g2, up = ts
    return [_silu(c2 * g0 + c1 * g1 + c0 * g2) * up]


def _final_stage(x1, fo, tgt, gate2, nfw, tt):
    rows, d = x1.shape

    def loss_fn(gate, nw, xa, fa, tg):
        y = _rms(xa + gate * fa, nw, NORM_EPS)
        err = (y - tg) ** 2
        return 0.5 * jnp.sum(jnp.mean(err, axis=-1, keepdims=True), axis=0, keepdims=True)

    def body(x_ref, f_ref, t_ref, g_ref, w_ref, dx_ref, df_ref, dg_ref, dw_ref, l_ref):
        i = pl.program_id(0)
        args = (g_ref[...], w_ref[...], x_ref[...], f_ref[...])
        tg = t_ref[...]
        lv, vjp = jax.vjp(lambda g, w, xa, fa: loss_fn(g, w, xa, fa, tg), *args)
        dg, dw, dx, df = vjp(jnp.ones((1, 1), f32))
        dx_ref[...] = dx
        df_ref[...] = df.astype(df_ref.dtype)

        @pl.when(i == 0)
        def _():
            dg_ref[...] = jnp.zeros_like(dg_ref)
            dw_ref[...] = jnp.zeros_like(dw_ref)
            l_ref[...] = jnp.zeros_like(l_ref)

        dg_ref[...] += dg
        dw_ref[...] += dw
        l_ref[...] += jnp.broadcast_to(lv, l_ref.shape)

    row = pl.BlockSpec((tt, d), lambda i: (i, 0))
    vec = pl.BlockSpec((1, d), lambda i: (0, 0))
    return pl.pallas_call(
        body, grid=(rows // tt,),
        in_specs=[row, row, row, vec, vec],
        out_specs=[row, row, vec, vec, pl.BlockSpec((1, LANES), lambda i: (0, 0))],
        out_shape=[jax.ShapeDtypeStruct((rows, d), f32), jax.ShapeDtypeStruct((rows, d), bf16)]
        + [jax.ShapeDtypeStruct((1, d), f32)] * 2
        + [jax.ShapeDtypeStruct((1, LANES), f32)],
        compiler_params=pltpu.CompilerParams(dimension_semantics=("arbitrary",)),
        name="loss_head",
    )(x1, fo, tgt, gate2, nfw)


def _ada_fwd(c_all, w_shard, b_cols):
    def body(c_ref, w_ref, b_ref, cond_ref, mod_ref):
        cond = _silu(c_ref[...])
        cond_ref[...] = cond
        mod_ref[...] = jnp.dot(cond.astype(bf16), w_ref[...].astype(bf16), preferred_element_type=f32) + b_ref[...]

    n = w_shard.shape[1]
    return pl.pallas_call(
        body, out_shape=[jax.ShapeDtypeStruct(c_all.shape, f32), jax.ShapeDtypeStruct((c_all.shape[0], n), f32)],
        name="ada_fwd",
    )(c_all, w_shard, b_cols)


def _adamw(name, w, g, m, v):
    rows, width = w.shape
    tt = _pick(rows, 256, 8)
    c1 = 1.0 - ADAM_B1 ** ADAM_STEP
    c2 = 1.0 - ADAM_B2 ** ADAM_STEP

    def body(w_ref, g_ref, m_ref, v_ref, d_ref, mo_ref, vo_ref):
        gg = g_ref[...]
        mn = ADAM_B1 * m_ref[...] + (1.0 - ADAM_B1) * gg
        vn = ADAM_B2 * v_ref[...] + (1.0 - ADAM_B2) * (gg * gg)
        m_hat = mn / c1
        v_hat = vn / c2
        d_ref[...] = -ADAM_LR * (m_hat / (jnp.sqrt(v_hat) + ADAM_EPS) + ADAM_WD * w_ref[...])
        mo_ref[...] = mn
        vo_ref[...] = vn

    spec = pl.BlockSpec((tt, width), lambda i: (i, 0))
    return pl.pallas_call(
        body, grid=(rows // tt,), in_specs=[spec] * 4, out_specs=[spec] * 3,
        out_shape=[jax.ShapeDtypeStruct((rows, width), f32)] * 3,
        compiler_params=pltpu.CompilerParams(dimension_semantics=("parallel",)),
        name=name,
    )(w, g, m, v)


def _place():
    return lax.axis_index("x"), lax.axis_index("y"), lax.axis_index("c")


def _ag8(name, blk):
    m, w = blk.shape

    def body(x_ref, out_ref, send_sems, recv_sems, local_sem):
        x, y, c = _place()
        me, sibling = (x, y, c), (x, y, 1 - c)
        chips = _other_chips(x, y)

        def slot(px, py, pc):
            return out_ref.at[4 * px + 2 * py + pc]

        def copy(k, block, to, src=None):
            return pltpu.make_async_remote_copy(src_ref=slot(*block) if src is None else src, dst_ref=slot(*block),
                                                send_sem=send_sems.at[k], recv_sem=recv_sems.at[k], device_id=to,
                                                device_id_type=MESH_ID)

        mine = pltpu.make_async_copy(x_ref, slot(*me), local_sem)
        mine.start()
        first = [copy(0, me, sibling, src=x_ref)] + [copy(1 + j, me, (*chip, c), src=x_ref) for j, chip in enumerate(chips)]
        for cp in first:
            cp.start()
        passed = [copy(4 + j, (*chip, c), sibling) for j, chip in enumerate(chips)]
        for j, chip in enumerate(chips):
            copy(1 + j, (*chip, c), me).wait_recv()
            passed[j].start()
        copy(0, sibling, me).wait_recv()
        for j, chip in enumerate(chips):
            copy(4 + j, (*chip, 1 - c), me).wait_recv()
        for cp in first + passed:
            cp.wait_send()
        mine.wait()

    return pl.pallas_call(
        body, out_shape=jax.ShapeDtypeStruct((8, m, w), blk.dtype),
        in_specs=[pl.BlockSpec(memory_space=pltpu.VMEM)], out_specs=pl.BlockSpec(memory_space=pltpu.VMEM),
        scratch_shapes=[pltpu.SemaphoreType.DMA((7,)), pltpu.SemaphoreType.DMA((7,)), pltpu.SemaphoreType.DMA],
        name=name,
    )(blk)


def _ag8_job(blk):
    def plan(x_refs, out_refs, send_sems, recv_sems):
        x_ref, out_ref = x_refs[0], out_refs[0]
        x, y, c = _place()
        me, sibling = (x, y, c), (x, y, 1 - c)
        chips = _other_chips(x, y)
        slot = lambda px, py, pc: out_ref.at[4 * px + 2 * py + pc]
        cp = lambda k, block, to, src=None: _rcopy(slot(*block) if src is None else src, slot(*block), send_sems, recv_sems, k, to)
        first = [cp(0, me, sibling, x_ref)] + [cp(1 + j, me, (*chip, c), x_ref) for j, chip in enumerate(chips)]
        landed = [cp(1 + j, (*chip, c), me) for j, chip in enumerate(chips)]
        passed = [cp(4 + j, (*chip, c), sibling) for j, chip in enumerate(chips)]
        handed = [cp(0, sibling, me)] + [cp(4 + j, (*chip, 1 - c), me) for j, chip in enumerate(chips)]
        mine = pltpu.make_async_copy(x_ref, slot(*me), send_sems.at[7])
        return first, landed, passed, handed, mine

    def start(*refs):
        first, _, _, _, mine = plan(*refs)
        mine.start()
        for cp in first:
            cp.start()

    def finish(*refs):
        first, landed, passed, handed, mine = plan(*refs)
        for arrived, fw in zip(landed, passed):
            arrived.wait_recv()
            fw.start()
        for cp in handed:
            cp.wait_recv()
        for cp in first + passed:
            cp.wait_send()
        mine.wait()

    return dict(ins=[blk], out_shapes=[jax.ShapeDtypeStruct((8,) + blk.shape, blk.dtype)], nsem=8, start=start,
                finish=finish)


def _other_chips(x, y):
    return [(1 - x, y), (x, 1 - y), (1 - x, 1 - y)]


_ANY = pl.BlockSpec(memory_space=pl.ANY)


def _rcopy(src, dst, send_sems, recv_sems, k, dev):
    return pltpu.make_async_remote_copy(src_ref=src, dst_ref=dst, send_sem=send_sems.at[k], recv_sem=recv_sems.at[k],
                                        device_id=dev, device_id_type=MESH_ID)


def _run_job(name, job):
    j_ins, j_outs, j_sems = _job_parts(job)
    n = len(j_ins)

    def body(*refs):
        for phase in ("start", "finish"):
            _job_phase(job, phase, refs[:n], refs[n:n + len(j_outs)], refs[n + len(j_outs):])

    return pl.pallas_call(body, out_shape=j_outs, in_specs=[_ANY] * n, out_specs=[_ANY] * len(j_outs),
                          scratch_shapes=j_sems, name=name)(*j_ins)


def _ag4_job(ws):
    n = len(ws)

    def plan(w_refs, out_refs, send_sems, recv_sems):
        x, y, c = _place()
        chip = 2 * x + y
        sibling = (x, y, 1 - c)
        chips = _other_chips(x, y)
        mine = [pl.ds(c * (w.shape[0] // 2), w.shape[0] // 2) for w in ws]
        other = [pl.ds((1 - c) * (w.shape[0] // 2), w.shape[0] // 2) for w in ws]
        rc = lambda src, dst, k, dev: _rcopy(src, dst, send_sems, recv_sems, k, dev)
        first = [rc(w_refs[t].at[mine[t]], out_refs[t].at[chip, mine[t]], 7 * t + k, (px, py, c))
                 for t in range(n) for k, (px, py) in enumerate(chips)]
        own = [rc(w_refs[t], out_refs[t].at[chip], 7 * t + 6, sibling) for t in range(n)]
        landed = [rc(out_refs[t].at[2 * px + py, mine[t]], out_refs[t].at[2 * px + py, mine[t]], 7 * t + k, (px, py, c))
                  for t in range(n) for k, (px, py) in enumerate(chips)]
        forward = [rc(out_refs[t].at[2 * px + py, mine[t]], out_refs[t].at[2 * px + py, mine[t]], 7 * t + 3 + k, sibling)
                   for t in range(n) for k, (px, py) in enumerate(chips)]
        handed = [rc(out_refs[t].at[2 * px + py, other[t]], out_refs[t].at[2 * px + py, other[t]], 7 * t + 3 + k, sibling)
                  for t in range(n) for k, (px, py) in enumerate(chips)]
        return first, own, landed, forward, handed

    def start(*refs):
        first, own, _, _, _ = plan(*refs)
        for cp in first + own:
            cp.start()

    def finish(*refs):
        first, own, landed, forward, handed = plan(*refs)
        for arrived, fw in zip(landed, forward):
            arrived.wait_recv()
            fw.start()
        for cp in handed + own:
            cp.wait_recv()
        for cp in first + own + forward:
            cp.wait_send()

    return dict(ins=ws, out_shapes=[jax.ShapeDtypeStruct((4,) + w.shape, w.dtype) for w in ws], nsem=7 * n,
                start=start, finish=finish)


def _sibling_swap_job(gs):
    n = len(gs)

    def plan(g_refs, recv_refs, send_sems, recv_sems):
        x, y, c = _place()
        return [_rcopy(g_refs[t].at[s_, pl.ds((1 - c) * (gs[t].shape[1] // 2), gs[t].shape[1] // 2)], recv_refs[t].at[s_],
                       send_sems, recv_sems, 4 * t + s_, (x, y, 1 - c)) for t in range(n) for s_ in range(4)]

    def start(*refs):
        for cp in plan(*refs):
            cp.start()

    def finish(*refs):
        copies = plan(*refs)
        for cp in copies:
            cp.wait_recv()
        for cp in copies:
            cp.wait_send()

    return dict(ins=gs, out_shapes=[jax.ShapeDtypeStruct((4, g.shape[1] // 2, g.shape[2]), g.dtype) for g in gs],
                nsem=4 * n, start=start, finish=finish)


def _chip_exchange_job(ps):
    n = len(ps)

    def plan(p_refs, recv_refs, send_sems, recv_sems):
        x, y, c = _place()
        return [_rcopy(p_refs[t].at[2 * px + py], recv_refs[t].at[k], send_sems, recv_sems, 3 * t + k, (px, py, c))
                for t in range(n) for k, (px, py) in enumerate(_other_chips(x, y))]

    def start(*refs):
        for cp in plan(*refs):
            cp.start()

    def finish(*refs):
        copies = plan(*refs)
        for cp in copies:
            cp.wait_recv()
        for cp in copies:
            cp.wait_send()

    return dict(ins=ps, out_shapes=[jax.ShapeDtypeStruct((3,) + p.shape[1:], p.dtype) for p in ps], nsem=3 * n,
                start=start, finish=finish)


_JOIN_PIECES = 4


def _rs_sibling_join(qs):
    n = len(qs)
    npc = _JOIN_PIECES

    def body(*refs):
        q_refs, out_refs = refs[:n], refs[n:2 * n]
        send_sems, recv_sems = refs[2 * n:]
        x, y, c = _place()
        copies = []
        for t in range(n):
            rh = qs[t].shape[0] // 2
            pr = rh // npc
            for i in range(npc):
                rows = pl.ds(c * rh + i * pr, pr)
                cp = _rcopy(q_refs[t].at[rows], out_refs[t].at[rows], send_sems, recv_sems, npc * t + i,
                            (x, y, 1 - c))
                cp.start()
                copies.append(cp)
        for t in range(n):
            rh = qs[t].shape[0] // 2
            pr = rh // npc
            for i in range(npc):
                rows = pl.ds((1 - c) * rh + i * pr, pr)
                _rcopy(q_refs[t].at[rows], out_refs[t].at[rows], send_sems, recv_sems, npc * t + i,
                       (x, y, 1 - c)).wait_recv()
        for cp in copies:
            cp.wait_send()

    return pl.pallas_call(
        body, out_shape=[jax.ShapeDtypeStruct(q.shape, q.dtype) for q in qs],
        in_specs=[_ANY] * n, out_specs=[_ANY] * n, input_output_aliases={t: t for t in range(n)},
        scratch_shapes=[pltpu.SemaphoreType.DMA((npc * n,)), pltpu.SemaphoreType.DMA((npc * n,))],
        name="grads_sibling_join",
    )(*qs)


def _add_half(name, g, recv, chip, ci):
    S, r, w = g.shape
    rh = r // 2
    tt = _pick(rh, 256, 16)
    nb = rh // tt

    def body(chip_ref, core_ref, a_ref, b_ref, own_ref, ob_ref):
        v = a_ref[...] + b_ref[...]
        ob_ref[...] = v.astype(bf16)

        @pl.when(pl.program_id(1) == chip_ref[0])
        def _():
            own_ref[...] = v[0]

    grid_spec = pltpu.PrefetchScalarGridSpec(
        num_scalar_prefetch=2, grid=(nb, S),
        in_specs=[pl.BlockSpec((1, tt, w), lambda i, s_, ch, co: (s_, co[0] * nb + i, 0)),
                  pl.BlockSpec((1, tt, w), lambda i, s_, ch, co: (s_, i, 0))],
        out_specs=[pl.BlockSpec((tt, w), lambda i, s_, ch, co: (i, 0)),
                   pl.BlockSpec((1, tt, w), lambda i, s_, ch, co: (s_, i, 0))])
    return pl.pallas_call(body, grid_spec=grid_spec,
                          out_shape=[jax.ShapeDtypeStruct((rh, w), f32), jax.ShapeDtypeStruct((S, rh, w), bf16)],
                          compiler_params=pltpu.CompilerParams(dimension_semantics=("arbitrary", "arbitrary")),
                          name=name)(chip.reshape(1).astype(jnp.int32), ci.reshape(1).astype(jnp.int32), g, recv)


def _sum_chip(name, own, others, ci):
    rh, w = own.shape
    tt = _pick(rh, 128, 16)
    nb = rh // tt

    def body(core_ref, a_ref, b_ref, o_ref):
        o_ref[...] = ((a_ref[...] + b_ref[0].astype(f32)) + b_ref[1].astype(f32)) + b_ref[2].astype(f32)

    grid_spec = pltpu.PrefetchScalarGridSpec(
        num_scalar_prefetch=1, grid=(nb,),
        in_specs=[pl.BlockSpec((tt, w), lambda i, co: (i, 0)),
                  pl.BlockSpec((3, tt, w), lambda i, co: (0, i, 0))],
        out_specs=pl.BlockSpec((tt, w), lambda i, co: (co[0] * nb + i, 0)))
    return pl.pallas_call(body, grid_spec=grid_spec, out_shape=jax.ShapeDtypeStruct((2 * rh, w), f32),
                          name=name)(ci.reshape(1).astype(jnp.int32), own, others)


def _sum_devices(gathered, late, late_rows, head_row):
    _, rows, width = gathered.shape

    def body(g_ref, l_ref, out_ref, head_ref):
        acc, acc_l = g_ref[0], l_ref[0]
        for d in range(1, 8):
            acc = acc + g_ref[d]
            acc_l = acc_l + l_ref[d]
        out_ref[...] = acc
        for k, r in enumerate(late_rows):
            out_ref[r:r + 1, :] = acc_l[k:k + 1, :]
        row = acc[head_row:head_row + 1, :]
        hs = row[:, 0:HEAD_DIM]
        for h in range(1, HEADS):
            hs = hs + row[:, h * HEAD_DIM:(h + 1) * HEAD_DIM]
        head_ref[...] = jnp.zeros_like(head_ref)
        head_ref[0:1, 0:HEAD_DIM] = hs

    return pl.pallas_call(
        body, out_shape=[jax.ShapeDtypeStruct((rows, width), f32), jax.ShapeDtypeStruct((8, LANES), f32)],
        name="small_grads_sum",
    )(gathered, late)


def _pack(arrs, rows_mult, dtype):
    flat = jnp.concatenate([a.reshape(-1).astype(dtype) for a in arrs])
    per = PACK_W * rows_mult
    total = -(-flat.shape[0] // per) * per
    return jnp.pad(flat, (0, total - flat.shape[0])).reshape(total // PACK_W, PACK_W)


def _unpack(buf, shapes):
    flat = buf.reshape(-1)
    out, off = [], 0
    for s in shapes:
        n = int(np.prod(s))
        out.append(flat[off:off + n].reshape(s))
        off += n
    return out


_BIG = ["w_in", "w_branch_gdn", "w_branch_rwkv", "w_out", "w_ffn_in", "w_ffn_out"]
_MID = ["conv_gdn", "conv_ffn", "w2", "a2", "g2"]
_SMALL =["b_ada", "norm1_w", "a_log", "dt_bias", "onorm_gdn", "mu_rwkv", "w0", "a0", "k_k", "k_a", "r_k", "lnx_w",
          "lnx_b", "norm2_w", "norm_f_w"]
_ORDER = ["w_ada", "b_ada", "norm1_w", "w_in", "conv_gdn", "a_log", "dt_bias", "onorm_gdn", "w_branch_gdn", "mu_rwkv",
          "w0", "w2", "a0", "a2", "g2", "k_k", "k_a", "r_k", "lnx_w", "lnx_b", "w_branch_rwkv", "w_out", "norm2_w",
          "w_ffn_in", "conv_ffn", "w_ffn_out", "norm_f_w"]


_WIN_SEGMENTS = [(0, 1536, 0), (2064, 3600, 1536), (1536, 2048, 3072), (3600, 3728, 3584), (2048, 2064, 3712),
                 (3728, 3888, 3840), (3888, 5936, 4096)]
_WIN_PADDED = 6144
_COL_QKV, _COL_RKV, _COL_Z = (0, 1, 2), (3, 4, 5), 6
_COL_LORA, _COL_BA = 28, 29
_COL_GATE_LORA = 15
_COL_GL = (4, 5)
_JOINT_QKV, _JOINT_RKV, _JOINT_Z, _JOINT_GL = 0, 1, 6, 2
_SMALL_BLOCKS_AT = 3584


def _win_pad(shards):
    n = shards.shape[2]
    parts, at = [], 0
    for lo, hi, dst in _WIN_SEGMENTS:
        if dst > at:
            parts.append(jnp.zeros((shards.shape[1], dst - at), shards.dtype))
        c = lo
        while c < hi:
            j = c // n
            e = min(hi, (j + 1) * n)
            parts.append(shards[j][:, c - j * n:e - j * n])
            c = e
        at = dst + hi - lo
    if at < _WIN_PADDED:
        parts.append(jnp.zeros((shards.shape[1], _WIN_PADDED - at), shards.dtype))
    return jnp.concatenate(parts, axis=1)


def _win_unpad_shards(g, n):
    shards = []
    for j in range(4):
        parts = []
        for lo, hi, dst in sorted(_WIN_SEGMENTS):
            a, b = max(lo, j * n), min(hi, (j + 1) * n)
            if a < b:
                parts.append(g[:, dst + a - lo:dst + b - lo])
        shards.append(jnp.concatenate(parts, axis=1))
    return jnp.stack(shards)


def kernel(x, c, w_ada, b_ada, norm1_w, w_in, conv_gdn, a_log, dt_bias, onorm_gdn, w_branch_gdn, mu_rwkv, w0, w2, a0, a2, g2, k_k, k_a, r_k, lnx_w, lnx_b, w_branch_rwkv, w_out, norm2_w, w_ffn_in, conv_ffn, w_ffn_out, norm_f_w, loss_target, m_w_ada, m_b_ada, m_norm1_w, m_w_in, m_conv_gdn, m_a_log, m_dt_bias, m_onorm_gdn, m_w_branch_gdn, m_mu_rwkv, m_w0, m_w2, m_a0, m_a2, m_g2, m_k_k, m_k_a, m_r_k, m_lnx_w, m_lnx_b, m_w_branch_rwkv, m_w_out, m_norm2_w, m_w_ffn_in, m_conv_ffn, m_w_ffn_out, m_norm_f_w, v_w_ada, v_b_ada, v_norm1_w, v_w_in, v_conv_gdn, v_a_log, v_dt_bias, v_onorm_gdn, v_w_branch_gdn, v_mu_rwkv, v_w0, v_w2, v_a0, v_a2, v_g2, v_k_k, v_k_a, v_r_k, v_lnx_w, v_lnx_b, v_w_branch_rwkv, v_w_out, v_norm2_w, v_w_ffn_in, v_conv_ffn, v_w_ffn_out, v_norm_f_w):
    args = dict(locals())
    W = {n: args[n] for n in _ORDER}
    Mo = {n: args["m_" + n] for n in _ORDER}
    Vo = {n: args["v_" + n] for n in _ORDER}
    shapes = {n: W[n].shape for n in _ORDER}
    sq = lambda a: a.reshape(a.shape[-2:]) if a.ndim == 3 else a.reshape(1, -1)
    row = lambda a: a.reshape(1, -1)

    xi, yi, ci = lax.axis_index("x"), lax.axis_index("y"), lax.axis_index("c")
    dev = 4 * xi + 2 * yi + ci
    chip = 2 * xi + yi

    x2 = x[0]
    tgt = loss_target[0]
    T, D = x2.shape
    tt_l = _pick(T, 512, CHUNK)
    tt_p = _pick(T, 256, CHUNK)
    tt_h = _pick(T, 128, CHUNK)

    small_blk = _pack([c] + [W[n] for n in _MID], 8, f32)
    small_all, win_s = _run_job("gather_c_mid_and_w_in", [_ag8_job(small_blk), _ag4_job([sq(W["w_in"]).astype(bf16)])])
    c_all = small_all[:, 0, :]
    per_chip = small_all[0::2].reshape(4, -1)[:, D:]
    mid_w, at = {}, 0
    for n in _MID:
        nr, nw = shapes[n][1:]
        mid_w[n] = per_chip[:, at:at + nr * nw].reshape(4, nr, nw).transpose(1, 0, 2).reshape(nr, 4 * nw)
        at += nr * nw
    conv_gdn_f, conv_ffn_f = mid_w["conv_gdn"], mid_w["conv_ffn"]
    w2f, a2f, g2f = mid_w["w2"], mid_w["a2"], mid_w["g2"]

    later_weights = _ag4_job([sq(W[n]).astype(bf16) for n in _BIG[1:]])
    win_p = _win_pad(win_s)
    zpad = lambda a, top, bot: jnp.pad(a, ((top, bot), (0, 0)))
    w2p, a2p, g2p = zpad(w2f, 0, 64), zpad(a2f, 64, 0), zpad(g2f, 0, 96)

    ncol = shapes["w_ada"][2]
    b_cols = lax.dynamic_slice(sq(W["b_ada"]), (0, chip * ncol), (1, ncol))
    cond16, mod_cols = _ada_fwd(jnp.pad(c_all, ((0, 8), (0, 0))), sq(W["w_ada"]), b_cols)
    mod_all = _ag8("gather_mod", mod_cols[:8])
    mod_mine = lax.dynamic_slice(mod_all[0::2], (0, dev, 0), (4, 1, ncol)).reshape(1, 4 * ncol)
    shift1, scale1, gate1, shift2, scale2, gate2 = [mod_mine[:, i * D:(i + 1) * D] for i in range(6)]

    seg = _seg_matrix(WIDTH, HEAD_DIM)
    norm1 = [sq(W["norm1_w"]), shift1, scale1]
    h1 = _stage_fwd("norm_mod1", _fn_norm_mod, [_whole(x2)], norm1, [(D, bf16)], tt_l)[0]
    p = _matmul("in_proj", h1, win_p, "nn")

    cgq = [row(conv_gdn_f[j, part * WIDTH:(part + 1) * WIDTH]) for part in range(3) for j in range(4)]
    lane_pad = lambda a: jnp.pad(row(a), ((0, 0), (8, LANES - 16)))
    gdn_pre_ps = cgq + [lane_pad(W["a_log"]), lane_pad(W["dt_bias"]), seg, _chunk_tri(tt_p, CHUNK)]
    gdn_pre_ts = [(p, WIDTH, ci_, (0, 1, 2, 3)) for ci_ in _COL_QKV] + [(p, LANES, _COL_BA, None)]
    q_, k_, v_, beta_t, gc_t = _stage_fwd("gdn_pre", _fn_gdn_pre, gdn_pre_ts, gdn_pre_ps,
                                          [(WIDTH, f32)] * 3 + [(LANES, f32)] * 2, tt_p)
    o_, gdn_hist, gdn_inv, wbg_s, wbr_s, wout_s, wfi_s, wfo_s = _scan_fwd(
        "gdn_scan", _gdn_chunk, [q_, k_, v_], [beta_t, gc_t], job=later_weights)
    wout_f = wout_s.reshape(D, D)
    wfo = wfo_s.reshape(D_FF, D)
    ow512 = jnp.tile(row(W["onorm_gdn"]), (1, HEADS))
    gdn_post_ts = [_whole(o_), (p, WIDTH, _COL_Z, None)]
    ya = _stage_fwd("gdn_post", _fn_gdn_post, gdn_post_ts, [ow512, seg], [(WIDTH, bf16)], tt_l)[0]

    mu = sq(W["mu_rwkv"])
    rw_ps = [mu[:, 0:512], mu[:, 512:1024], mu[:, 1024:1536], mu[:, 1536:1664], jnp.pad(mu[:, 1664:1824], ((0, 0), (0, 96))),
             sq(W["w0"]), w2p, sq(W["a0"]), a2p, g2p, sq(W["k_k"]), sq(W["k_a"]), seg]
    rw_ts = [(p, WIDTH, ci_, (0, 1)) for ci_ in _COL_RKV] + [(p, LANES, _COL_LORA, (0, 1)),
                                                              (p, 256, _COL_GATE_LORA, (0, 1))]
    rw_out = _stage_fwd("rwkv_pre", _fn_rwkv_pre, rw_ts, rw_ps, [(WIDTH, f32)] * 7, tt_p)
    r_, lw_, k2_, vv_, na_, b_, g_ = rw_out
    rw_ins = [r_, lw_, k2_, vv_, na_, b_]
    y_, rw_hist, rw_inv = _scan_fwd("rwkv_scan", _rwkv_chunk, rw_ins)
    rwp_ps = [sq(W["lnx_w"]), sq(W["lnx_b"]), row(W["r_k"]), seg]
    rwp_ts = [_whole(y_), _whole(r_), _whole(k2_), _whole(vv_), _whole(g_)]
    yb = _stage_fwd("rwkv_post", _fn_rwkv_post, rwp_ts, rwp_ps, [(WIDTH, bf16)], tt_l)[0]

    big_a = _matmul("branch_gdn", ya, wbg_s, "nn", shards=4)
    big_b = _matmul("branch_rwkv", yb, wbr_s, "nn", shards=4)
    merge_ts = [(p, D, _COL_GL[0], None), (p, D, _COL_GL[1], None), _whole(big_a), _whole(big_b)]
    merged = _stage_fwd("merge", _fn_merge, merge_ts, [], [(D, bf16)], tt_l)[0]
    mo = _matmul("out_proj", merged, wout_f, "nn")
    norm2 = [gate1, sq(W["norm2_w"]), shift2, scale2]
    x1, h2 = _stage_fwd("resid_norm_mod2", _fn_resid_norm_mod, [_whole(x2), _whole(mo)], norm2, [(D, f32), (D, bf16)], tt_l)
    f = _matmul("ffn_in", h2, wfi_s, "nn", shards=4, tn=1408)
    cg_ps = [row(conv_ffn_f[j]) for j in range(3)]
    cg_ts = [(f, D_FF, 0, (0, 1, 2)), (f, D_FF, 1, None)]
    act = _stage_fwd("convglu", _fn_convglu, cg_ts, cg_ps, [(D_FF, bf16)], tt_h)[0]
    fo = _matmul("ffn_out", act, wfo, "nn", tk=D_FF)

    dx1_a, dfo, dgate2, dnormf, loss_part = _final_stage(x1, fo, tgt, gate2, row(W["norm_f_w"]), tt_l)

    dact = _matmul("d_act", dfo, wfo, "nt", tn=D_FF)
    g_wfo = _matmul("g_ffn_out", act, dfo, "tn", tm=1408, tk=2048)
    _, dcf, df = _stage_bwd("convglu_bwd", _fn_convglu, cg_ts, cg_ps, [[_whole(dact)]], tt_h, [True] * 2, [True] * 3,
                            joint=([0, 1], 2 * D_FF, 0, None), dtypes=[bf16])
    g_wfi = _matmul("g_ffn_in", h2, df, "tn", shards=4, tn=1408, tk=2048)
    g_wfo_s = g_wfo.reshape(4, D_FF // 4, D)
    dh2, recv_wfi, recv_wfo = _matmul("d_h2", df, wfi_s, "nt", shards=4, tk=1408, job=_sibling_swap_job([g_wfi, g_wfo_s]))
    (dx_a, dmo), (dgate1, dnorm2, dshift2, dscale2), _ = _stage_bwd(
        "resid_norm_mod2_bwd", _fn_resid_norm_mod, [_whole(x2), _whole(mo)], norm2,
        [[_whole(dx1_a)], [_whole(dh2)]], tt_l, [True, True], [True] * 4, dtypes=[f32, bf16])
    dmerged = _matmul("d_merged", dmo, wout_f, "nt")
    g_wout = _matmul("g_out_proj", merged, dmo, "tn", tk=T)
    (dbig_a, dbig_b), _, dp = _stage_bwd("merge_bwd", _fn_merge, merge_ts, [], [[_whole(dmerged)]], tt_l, [True] * 4, [],
                                         joint=([0, 1], p.shape[1], _JOINT_GL, None), dtypes=[bf16] * 3)
    g_wbg = _matmul("g_branch_gdn", ya, dbig_a, "tn", shards=4, tk=T)
    g_wbr = _matmul("g_branch_rwkv", yb, dbig_b, "tn", shards=4, tk=T)
    g_wout_s = g_wout.reshape(4, D // 4, D)
    dyb = _matmul("d_yb", dbig_b, wbr_s, "nt", shards=4)
    dya, *recv_mix = _matmul("d_ya", dbig_a, wbg_s, "nt", shards=4, job=_sibling_swap_job([g_wbg, g_wbr, g_wout_s]))

    (dy_, dr_p, dk2_p, dv_p, dg_p), (dlnxw, dlnxb, drk), _ = _stage_bwd(
        "rwkv_post_bwd", _fn_rwkv_post, rwp_ts, rwp_ps, [[_whole(dyb)]], tt_l, [True] * 5, [True, True, True, False])
    gs_a = [g_wbg, g_wbr, g_wout_s, g_wfi, g_wfo_s]
    pairs_a = [_add_half("grads_pair_sum%d" % (t + 1), g, r_, chip, ci)
               for t, (g, r_) in enumerate(zip(gs_a, recv_mix + [recv_wfi, recv_wfo]))]
    dr_c, dlw_c, dk2_c, dv_c, dna_c, db_c, *others_a = _scan_bwd(
        "rwkv_scan_bwd", _rwkv_chunk, rw_ins, [], rw_hist, rw_inv, dy_, job=_chip_exchange_job([pb for _, pb in pairs_a]))
    rw_cots = [[_whole(dr_p), _whole(dr_c)], [_whole(dlw_c)], [_whole(dk2_p), _whole(dk2_c)],
               [_whole(dv_p), _whole(dv_c)], [_whole(dna_c)], [_whole(db_c)], [_whole(dg_p)]]
    (dl_, dg_), rw_dp, dp = _stage_bwd("rwkv_pre_bwd", _fn_rwkv_pre, rw_ts, rw_ps, rw_cots, tt_p, [True] * 5,
                                       [True] * 12 + [False], joint=([0, 1, 2], p.shape[1], _JOINT_RKV, dp), dtypes=[bf16] * 3)
    dmu_r, dmu_k, dmu_v, dmu_l, dmu_g, dw0, dw2p, da0, da2p, dg2p, dkk, dka = rw_dp

    (do_,), (dow512,), dp = _stage_bwd("gdn_post_bwd", _fn_gdn_post, gdn_post_ts, [ow512, seg], [[_whole(dya)]], tt_l,
                                       [True, True], [True, False], joint=([1], p.shape[1], _JOINT_Z, dp), dtypes=[f32, bf16])
    d_gdn = _scan_bwd("gdn_scan_bwd", _gdn_chunk, [q_, k_, v_], [beta_t, gc_t], gdn_hist, gdn_inv, do_)
    gdn_cots = [[_whole(a)] for a in d_gdn]
    (dba,), gdn_dp, dp = _stage_bwd("gdn_pre_bwd", _fn_gdn_pre, gdn_pre_ts, gdn_pre_ps, gdn_cots, tt_p, [True] * 4,
                                    [True] * 14 + [False, False], joint=([0, 1, 2], p.shape[1], _JOINT_QKV, dp), dtypes=[bf16] * 2)
    dp = lax.dynamic_update_slice(dp, jnp.concatenate([dl_, dba, dg_], axis=1), (0, _SMALL_BLOCKS_AT))
    g_conv_gdn = jnp.concatenate([jnp.concatenate([gdn_dp[4 * part + j] for part in range(3)], axis=1) for j in range(4)], axis=0)
    g_conv_ffn = jnp.concatenate(dcf, axis=0)
    g_mu = jnp.concatenate([dmu_r, dmu_k, dmu_v, dmu_l, dmu_g[:, :160]], axis=1)
    late_zero = jnp.zeros((1, D), f32)
    dmod_early = jnp.concatenate([late_zero, late_zero, dgate1, dshift2, dscale2, dgate2], axis=1)
    small_parts = {"b_ada": dmod_early, "norm1_w": late_zero, "a_log": gdn_dp[12][:, 8:16], "dt_bias": gdn_dp[13][:, 8:16],
                   "mu_rwkv": g_mu, "w0": dw0, "a0": da0, "k_k": dkk, "k_a": dka, "r_k": drk, "lnx_w": dlnxw,
                   "lnx_b": dlnxb, "norm2_w": dnorm2, "norm_f_w": dnormf}
    small_names = [n for n in _SMALL if n != "onorm_gdn"]
    mid_full = [g_conv_gdn, g_conv_ffn, dw2p[0:64], da2p[64:128], dg2p[0:160]]
    body_rows = _pack([small_parts[n] for n in small_names] + [loss_part[:, 0:1]] + mid_full, 1, f32)
    head_row = body_rows.shape[0]
    small_g = jnp.concatenate([body_rows, jnp.pad(dow512, ((0, 0), (0, PACK_W - WIDTH)))], axis=0)
    small_g = jnp.pad(small_g, ((0, -small_g.shape[0] % 8), (0, 0)))

    g_win_pad, small_all_g = _matmul("g_in_proj", h1, dp, "tn", tk=T, job=_ag8_job(small_g))
    g_win_s = _win_unpad_shards(g_win_pad, shapes["w_in"][2])
    pair_win = _add_half("grads_pair_sum0", g_win_s, _run_job("w_in_grads_sibling_swap", _sibling_swap_job([g_win_s]))[0],
                         chip, ci)
    dh1, others_win = _matmul("d_h1", dp, win_p, "nt", tk=3072, job=_chip_exchange_job([pair_win[1]]))
    (grad_x,), (dnorm1, dshift1, dscale1), _ = _stage_bwd("norm_mod1_bwd", _fn_norm_mod_and_x, [_whole(x2)], norm1,
                                                          [[_whole(dh1)], [_whole(dx_a)]], tt_l, [True], [True] * 3)
    late_all = _ag8("gather_late_grads", jnp.pad(jnp.concatenate([dshift1, dscale1, dnorm1], axis=0), ((0, 5), (0, 0))))
    small_sum, head_sum = _sum_devices(small_all_g, late_all, (0, 1, 6), head_row)
    small_shapes = [shapes[n][1:] if n != "norm_f_w" else shapes[n] for n in small_names]
    un = _unpack(small_sum, small_shapes + [(1,)] + [g.shape for g in mid_full])
    small_grads = dict(zip(small_names, un))
    loss = un[len(small_names)].reshape(())
    small_grads["onorm_gdn"] = head_sum[0, 0:HEAD_DIM]
    for n, g in zip(_MID, un[len(small_names) + 1:]):
        wcols = shapes[n][2]
        small_grads[n] = lax.dynamic_slice(g, (0, chip * wcols), (g.shape[0], wcols))

    dmod_all = jnp.concatenate([late_all[:, 0, :], late_all[:, 1, :], small_all_g[:, 2:6, :].reshape(8, 4 * PACK_W)], axis=1)
    dmod_cols = lax.dynamic_slice(dmod_all, (0, chip * ncol), (8, ncol))
    g_wada = _matmul("g_w_ada", cond16, jnp.pad(dmod_cols, ((0, 8), (0, 0))), "tn")

    pairs = [pair_win] + pairs_a
    others = [others_win] + others_a
    halves = [_sum_chip("grads_chip_sum%d" % t, own, o_, ci) for t, ((own, _), o_) in enumerate(zip(pairs, others))]
    big_grads = dict(zip(_BIG, _rs_sibling_join(halves)))

    res = {tag: {} for tag in ("grad", "delta", "new_m", "new_v")}

    def put(n, g, d, m_, v_):
        for tag, val in zip(("grad", "delta", "new_m", "new_v"), (g, d, m_, v_)):
            res[tag][n] = val.reshape(shapes[n])

    for n in _BIG:
        put(n, big_grads[n], *_adamw("adamw_" + n, sq(W[n]), big_grads[n], sq(Mo[n]), sq(Vo[n])))
    put("w_ada", g_wada, *_adamw("adamw_w_ada", sq(W["w_ada"]), g_wada, sq(Mo["w_ada"]), sq(Vo["w_ada"])))
    rest = _SMALL + _MID
    pk = lambda d: _pack([d[n] for n in rest], 8, f32)
    sg = pk(small_grads)
    sm = _adamw("adamw_small", pk(W), sg, pk(Mo), pk(Vo))
    for tag, buf in zip(("grad", "delta", "new_m", "new_v"), (sg,) + tuple(sm)):
        res[tag].update(zip(rest, _unpack(buf, [shapes[n] for n in rest])))
    outs = [loss, grad_x.reshape(x.shape)]
    for tag in ("grad", "delta", "new_m", "new_v"):
        outs += [res[tag][n] for n in _ORDER]
    return tuple(outs)
```

```python
import numpy as np
import jax
import jax.numpy as jnp
from jax import lax
from jax.experimental import pallas as pl
from jax.experimental.pallas import tpu as pltpu

f32 = jnp.float32
bf16 = jnp.bfloat16

LANES = 128
HEADS = 8
HEAD_DIM = 64
WIDTH = HEADS * HEAD_DIM
CHUNK = 64
D_FF = 2816
NORM_EPS = 1e-6
LNX_EPS = 64e-5
PACK_W = 1024
MESH_ID = pl.DeviceIdType.MESH

ADAM_LR, ADAM_B1, ADAM_B2, ADAM_EPS, ADAM_WD, ADAM_STEP = 0.001, 0.9, 0.999, 1e-08, 0.01, 10


def _pick(n, target, mult):
    if n <= target:
        return n
    best = None
    for t in range(mult, target + 1, mult):
        if n % t == 0:
            best = t
    assert best is not None, (n, target, mult)
    return best


def _split_bf16(x, n):
    parts, r = [], x
    for i in range(n):
        p = r.astype(bf16)
        parts.append(p)
        if i + 1 < n:
            r = r - p.astype(f32)
    return parts


def _xdot_r_impl(x, m, n, dims):
    acc = None
    for p in _split_bf16(x, n):
        t = lax.dot_general(p, m, dims, preferred_element_type=f32)
        acc = t if acc is None else acc + t
    return acc


def _make_xdot_r(n):
    nn = (((1,), (0,)), ((), ()))
    nt = (((1,), (1,)), ((), ()))

    @jax.custom_vjp
    def xdot(x, m):
        return _xdot_r_impl(x, m, n, nn)

    def fwd(x, m):
        return _xdot_r_impl(x, m, n, nn), m

    def bwd(m, ct):
        return _xdot_r_impl(ct, m, n, nt), jnp.zeros_like(m)

    xdot.defvjp(fwd, bwd)
    return xdot


_segsum = _make_xdot_r(2)


def _xdot_l_impl(m, x, n, dims):
    acc = None
    for p in _split_bf16(x, n):
        t = lax.dot_general(m, p, dims, preferred_element_type=f32)
        acc = t if acc is None else acc + t
    return acc


@jax.custom_vjp
def _xdot_l(m, x):
    return _xdot_l_impl(m, x, 3, (((1,), (0,)), ((), ())))


def _xdot_l_fwd(m, x):
    return _xdot_l(m, x), m


def _xdot_l_bwd(m, ct):
    return jnp.zeros_like(m), _xdot_l_impl(m, ct, 3, (((0,), (0,)), ((), ())))


_xdot_l.defvjp(_xdot_l_fwd, _xdot_l_bwd)


@jax.custom_vjp
def _bdot(x, w):
    return jnp.dot(x.astype(bf16), w.astype(bf16), preferred_element_type=f32)


def _bdot_fwd(x, w):
    return _bdot(x, w), (x, w)


def _bdot_bwd(res, ct):
    x, w = res
    c = ct.astype(bf16)
    dx = lax.dot_general(c, w.astype(bf16), (((1,), (1,)), ((), ())), preferred_element_type=f32)
    dw = lax.dot_general(x.astype(bf16), c, (((0,), (0,)), ((), ())), preferred_element_type=f32)
    return dx, dw


_bdot.defvjp(_bdot_fwd, _bdot_bwd)


def _silu(x):
    return x * jax.nn.sigmoid(x)


def _softplus(x):
    return jnp.maximum(x, 0.0) + jnp.log(1.0 + jnp.exp(-jnp.abs(x)))


def _rms(x, w, eps):
    return x * lax.rsqrt(jnp.mean(x * x, axis=-1, keepdims=True) + eps) * w


def _seg_matrix(width, seg):
    i = np.arange(width)
    return jnp.asarray((i[:, None] // seg) == (i[None, :] // seg), dtype=bf16)


def _chunk_tri(rows, chunk):
    i = np.arange(rows)
    return jnp.asarray(((i[:, None] // chunk) == (i[None, :] // chunk)) & (i[:, None] >= i[None, :]), dtype=bf16)


HALO = 8


def _full_spec(shape):
    nd = len(shape)
    return pl.BlockSpec(shape, lambda i: (0,) * nd)


def _entry_specs(entries, tt, block_of):
    specs, ops = [], []
    for arr, w, ci, shifts in entries:
        specs.append(pl.BlockSpec((tt, w), lambda i, ci=ci: (block_of(i), ci)))
        ops.append(arr)
        if shifts:
            specs.append(pl.BlockSpec((HALO, w), lambda i, ci=ci: (jnp.maximum(block_of(i) * (tt // HALO) - 1, 0), ci)))
            ops.append(arr)
    return specs, ops


def _load_entries(entries, refs, first):
    tiles, k = [], 0
    for _, w, _, shifts in entries:
        x = refs[k][...].astype(f32)
        k += 1
        if not shifts:
            tiles.append(x)
            continue
        halo = jnp.where(first, 0.0, refs[k][...].astype(f32))
        k += 1
        row = lax.broadcasted_iota(jnp.int32, (HALO, w), 0)
        for s in shifts:
            if s == 0:
                tiles.append(x)
                continue
            r = pltpu.roll(x, s, 0)
            head = jnp.where(row < s, pltpu.roll(halo, s, 0), r[0:HALO])
            tiles.append(jnp.concatenate([head, r[HALO:]], axis=0))
    return tiles


def _unshift_sum(grads, shifts, carry, tt):
    w = grads[0].shape[1]
    row = lax.broadcasted_iota(jnp.int32, (tt, w), 0)
    row8 = lax.broadcasted_iota(jnp.int32, (HALO, w), 0)
    dx, out = None, jnp.zeros((HALO, w), f32)
    for d, s in zip(grads, shifts):
        if s == 0:
            part = d
        else:
            part = jnp.where(row < tt - s, pltpu.roll(d, tt - s, 0), 0.0)
            out = out + jnp.where(row8 >= HALO - s, pltpu.roll(d[0:HALO], HALO - s, 0), 0.0)
        dx = part if dx is None else dx + part
    return jnp.concatenate([dx[:tt - HALO], dx[tt - HALO:] + carry], axis=0), out


def _stage_fwd(name, fn, tiles, params, outs, tt):
    rows = tiles[0][0].shape[0]
    npar = len(params)
    specs, ops = _entry_specs(tiles, tt, lambda i: i)
    nin = len(ops)

    def body(*refs):
        ts = _load_entries(tiles, refs[:nin], pl.program_id(0) == 0)
        ps = [r[...] for r in refs[nin:nin + npar]]
        res = fn(ps, ts)
        for r, v in zip(refs[nin + npar:], res):
            r[...] = v.astype(r.dtype)

    return pl.pallas_call(
        body, grid=(rows // tt,),
        in_specs=specs + [_full_spec(p.shape) for p in params],
        out_specs=[pl.BlockSpec((tt, w), lambda i: (i, 0)) for (w, _) in outs],
        out_shape=[jax.ShapeDtypeStruct((rows, w), dt) for (w, dt) in outs],
        compiler_params=pltpu.CompilerParams(dimension_semantics=("parallel",)),
        name=name,
    )(*ops, *params)


def _stage_bwd(name, fn, tiles, params, cots, tt, tile_grad, param_grad, joint=None, dtypes=None):
    rows = tiles[0][0].shape[0]
    nblk = rows // tt
    npar = len(params)
    block_of = lambda i: nblk - 1 - i
    specs, ops = _entry_specs(tiles, tt, block_of)
    nin = len(ops)
    flat_cots = [c for group in cots for c in group]
    groups = [len(g) for g in cots]
    ncot = len(flat_cots)
    counts = [len(e[3]) if e[3] else 1 for e in tiles]
    dt_entries = [e for e, g in zip(tiles, tile_grad) if g]
    dp_shapes = [p.shape for p, g in zip(params, param_grad) if g]
    ndt = len(dt_entries)
    carry_w = [e[1] for e in dt_entries if e[3]]
    flags = [g for g, n in zip(tile_grad, counts) for _ in range(n)]
    members, j_width, j_cidx, j_buf = joint if joint else ([], 0, 0, None)
    solo = [k for k in range(ndt) if k not in members]
    nsolo, njoint, nbuf = len(solo), int(bool(members)), int(j_buf is not None)
    j_block = sum(dt_entries[k][1] for k in members)
    dtypes = list(dtypes) if dtypes else [f32] * (nsolo + njoint)

    def body(*refs):
        i = pl.program_id(0)
        p_refs = refs[nin:nin + npar]
        c_refs = refs[nin + npar:nin + npar + ncot]
        base = nin + npar + ncot + nbuf
        dt_refs = refs[base:base + nsolo]
        joint_refs = refs[base + nsolo:base + nsolo + njoint]
        dp_refs = refs[base + nsolo + njoint:base + nsolo + njoint + len(dp_shapes)]
        carry_refs = refs[base + nsolo + njoint + len(dp_shapes):]
        ts = _load_entries(tiles, refs[:nin], block_of(i) == 0)
        ps = [r[...] for r in p_refs]

        def f(dp, dt):
            dp, dt = iter(dp), iter(dt)
            pp = [next(dp) if g else p for p, g in zip(ps, param_grad)]
            tl = [next(dt) if g else t for t, g in zip(ts, flags)]
            return fn(pp, tl)

        _, vjp = jax.vjp(f, [p for p, g in zip(ps, param_grad) if g], [t for t, g in zip(ts, flags) if g])
        cs, j = [], 0
        for n in groups:
            acc = c_refs[j][...].astype(f32)
            for q in range(1, n):
                acc = acc + c_refs[j + q][...].astype(f32)
            cs.append(acc)
            j += n
        gp, gt = vjp(cs)

        @pl.when(i == 0)
        def _():
            for r in dp_refs:
                r[...] = jnp.zeros_like(r)
            for r in carry_refs:
                r[...] = jnp.zeros_like(r)

        gt, k, kc, dxs = list(gt), 0, 0, []
        for e, n in zip(dt_entries, [n for n, g in zip(counts, tile_grad) if g]):
            if e[3]:
                dx, out = _unshift_sum(gt[k:k + n], e[3], carry_refs[kc][...], tt)
                carry_refs[kc][...] = out
                kc += 1
            else:
                dx = gt[k]
            dxs.append(dx)
            k += n
        for r, k in zip(dt_refs, solo):
            r[...] = dxs[k].astype(r.dtype)
        off = 0
        for k in members:
            w = dt_entries[k][1]
            joint_refs[0][:, off:off + w] = dxs[k].astype(joint_refs[0].dtype)
            off += w
        for r, v in zip(dp_refs, gp):
            r[...] += v

    res = pl.pallas_call(
        body, grid=(nblk,),
        in_specs=specs + [_full_spec(p.shape) for p in params]
        + [pl.BlockSpec((tt, w), lambda i, ci=ci: (block_of(i), ci)) for (_, w, ci, *_) in flat_cots]
        + [pl.BlockSpec(memory_space=pl.ANY)] * nbuf,
        out_specs=[pl.BlockSpec((tt, dt_entries[k][1]), lambda i: (block_of(i), 0)) for k in solo]
        + [pl.BlockSpec((tt, j_block), lambda i: (block_of(i), j_cidx))] * njoint
        + [_full_spec(s) for s in dp_shapes],
        out_shape=[jax.ShapeDtypeStruct((rows, dt_entries[k][1]), dt) for k, dt in zip(solo, dtypes)]
        + [jax.ShapeDtypeStruct((rows, j_width), dtypes[-1])] * njoint
        + [jax.ShapeDtypeStruct(s, f32) for s in dp_shapes],
        scratch_shapes=[pltpu.VMEM((HALO, w), f32) for w in carry_w],
        input_output_aliases={nin + npar + ncot: nsolo} if nbuf else {},
        compiler_params=pltpu.CompilerParams(dimension_semantics=("arbitrary",)),
        name=name,
    )(*ops, *params, *[c[0] for c in flat_cots], *([j_buf] if nbuf else []))
    res = list(res)
    return res[:nsolo], res[nsolo + njoint:], (res[nsolo] if njoint else None)


def _whole(a):
    return (a, a.shape[1], 0, None)


def _matmul(name, a, b, mode, out_dtype=f32, tm=1024, tn=1024, tk=1024, shards=1, job=None):
    S = shards
    if mode == "nn":
        M, K = a.shape
        w = b.shape[-1]
    elif mode == "nt":
        M = a.shape[0]
        if S > 1:
            _, N, w = b.shape
            K = S * w
        else:
            N, K = b.shape
            w = K
    else:
        K, M = a.shape
        w = b.shape[1] // S
    if mode != "nt":
        N = S * w
    tm = _pick(M, tm, LANES)
    if mode == "nt":
        tn = _pick(N, tn, LANES)
        tk = _pick(w, tk, LANES)
    else:
        tn = _pick(w, tn, LANES)
        tk = _pick(K, tk, LANES if mode == "nn" else 16)
    nk = K // tk
    nb = w // (tk if mode == "nt" else tn)
    if mode == "nn":
        a_spec = pl.BlockSpec((tm, tk), lambda i, j, k: (i, k))
        if S > 1:
            b_spec = pl.BlockSpec((1, tk, tn), lambda i, j, k: (j // nb, k, j % nb))
        else:
            b_spec = pl.BlockSpec((tk, tn), lambda i, j, k: (k, j))
        dims = (((1,), (0,)), ((), ()))
    elif mode == "nt":
        a_spec = pl.BlockSpec((tm, tk), lambda i, j, k: (i, k))
        if S > 1:
            b_spec = pl.BlockSpec((1, tn, tk), lambda i, j, k: (k // nb, j, k % nb))
        else:
            b_spec = pl.BlockSpec((tn, tk), lambda i, j, k: (j, k))
        dims = (((1,), (1,)), ((), ()))
    else:
        a_spec = pl.BlockSpec((tk, tm), lambda i, j, k: (k, i))
        b_spec = pl.BlockSpec((tk, tn), lambda i, j, k: (k, j))
        dims = (((0,), (0,)), ((), ()))
    if mode == "tn" and S > 1:
        o_spec = pl.BlockSpec((1, tm, tn), lambda i, j, k: (j // nb, i, j % nb))
        o_shape = (S, M, w)
    else:
        o_spec = pl.BlockSpec((tm, tn), lambda i, j, k: (i, j))
        o_shape = (M, N)
    b_lead = S > 1 and mode != "tn"
    o_lead = S > 1 and mode == "tn"

    j_ins, j_outs, j_sems = _job_parts(job)
    nji, njo = len(j_ins), len(j_outs)
    grid = (M // tm, N // tn, nk)

    def run_job(refs):
        step = (pl.program_id(0) * grid[1] + pl.program_id(1)) * grid[2] + pl.program_id(2)
        _job_steps(job, refs[2:2 + nji], refs[3 + nji:3 + nji + njo], refs[len(refs) - len(j_sems):], step,
                   grid[0] * grid[1] * grid[2] - 1)

    def body(*refs):
        a_ref, b_ref, o_ref, acc_ref = refs[0], refs[1], refs[2 + nji], refs[3 + nji + njo]
        run_job(refs)
        k = pl.program_id(2)

        @pl.when(k == 0)
        def _():
            acc_ref[...] = jnp.zeros_like(acc_ref)

        bv = b_ref[0] if b_lead else b_ref[...]
        acc_ref[...] += lax.dot_general(a_ref[...].astype(bf16), bv.astype(bf16), dims, preferred_element_type=f32)

        @pl.when(k == nk - 1)
        def _():
            if o_lead:
                o_ref[0] = acc_ref[...].astype(o_ref.dtype)
            else:
                o_ref[...] = acc_ref[...].astype(o_ref.dtype)

    def body_one_step(*refs):
        a_ref, b_ref, o_ref = refs[0], refs[1], refs[2 + nji]
        run_job(refs)
        bv = b_ref[0] if b_lead else b_ref[...]
        res = lax.dot_general(a_ref[...].astype(bf16), bv.astype(bf16), dims, preferred_element_type=f32)
        if o_lead:
            o_ref[0] = res.astype(o_ref.dtype)
        else:
            o_ref[...] = res.astype(o_ref.dtype)

    res = pl.pallas_call(
        body if nk > 1 else body_one_step, grid=grid,
        in_specs=[a_spec, b_spec] + [_ANY] * nji,
        out_specs=[o_spec] + [_ANY] * njo,
        out_shape=[jax.ShapeDtypeStruct(o_shape, out_dtype)] + j_outs,
        scratch_shapes=([pltpu.VMEM((tm, tn), f32)] if nk > 1 else []) + j_sems,
        compiler_params=pltpu.CompilerParams(
            dimension_semantics=("arbitrary",) * 3 if job else ("parallel", "parallel", "arbitrary")),
        name=name,
    )(a, b, *j_ins)
    return res if job else res[0]


def _make_bmm(precision):
    if precision is None:
        cast, kw = (lambda v: v.astype(bf16)), {}
    else:
        cast, kw = (lambda v: v), {"precision": precision}

    def nn(a, b):
        return jnp.einsum("hij,hjk->hik", cast(a), cast(b), preferred_element_type=f32, **kw)

    def nt(a, b):
        return jnp.einsum("hik,hjk->hij", cast(a), cast(b), preferred_element_type=f32, **kw)

    def tn(a, b):
        return jnp.einsum("hki,hkj->hij", cast(a), cast(b), preferred_element_type=f32, **kw)

    if precision is not None:
        return nn, nt, tn
    nn_v, nt_v, tn_v = jax.custom_vjp(nn), jax.custom_vjp(nt), jax.custom_vjp(tn)
    keep = lambda f: (lambda a, b: (f(a, b), (a, b)))
    nn_v.defvjp(keep(nn), lambda r, ct: (nt(ct, r[1]), tn(r[0], ct)))
    nt_v.defvjp(keep(nt), lambda r, ct: (nn(ct, r[1]), tn(ct, r[0])))
    tn_v.defvjp(keep(tn), lambda r, ct: (nt(r[1], ct), nn(r[0], ct)))
    return nn_v, nt_v, tn_v


_bmm, _bmm_nt, _bmm_tn = _make_bmm(None)
_bmm_exact = _make_bmm(lax.Precision.HIGH)[0]


def _masks(n):
    r = lax.broadcasted_iota(jnp.int32, (n, n), 0)
    c = lax.broadcasted_iota(jnp.int32, (n, n), 1)
    return (r >= c)[None], (r > c)[None], (r == c)[None]


_INV_BLOCK = 8


def _nilpotent_inverse(m, eye):
    p = eye + m
    for _ in range(2):
        m = _bmm(m, m)
        p = p + _bmm(p, m)
    return p


def _neumann_inverse_impl(m):
    n = m.shape[1]
    assert n == _INV_BLOCK * _INV_BLOCK
    r = lax.broadcasted_iota(jnp.int32, (n, n), 0)
    c = lax.broadcasted_iota(jnp.int32, (n, n), 1)
    eye = (r == c).astype(f32)[None]
    inside = jnp.where((r // _INV_BLOCK == c // _INV_BLOCK)[None], m, 0.0)
    d_inv = _nilpotent_inverse(inside, eye)
    return _bmm(_nilpotent_inverse(_bmm(d_inv, m - inside), eye), d_inv)


@jax.custom_vjp
def _neumann_inverse(m):
    return _neumann_inverse_impl(m)


def _neumann_inverse_fwd(m):
    p = _neumann_inverse_impl(m)
    return p, p


def _neumann_inverse_bwd(p, ct):
    return (_bmm_tn(p, _bmm_nt(ct, p)),)


_neumann_inverse.defvjp(_neumann_inverse_fwd, _neumann_inverse_bwd)


@jax.custom_vjp
def _given_inverse(m, p):
    return p


_given_inverse.defvjp(lambda m, p: (p, p), lambda p, ct: (_neumann_inverse_bwd(p, ct)[0], jnp.zeros_like(p)))


def _gdn_chunk(s, q, k, v, beta, gc, gr, gl, p=None):
    n = q.shape[1]
    causal, strict, _ = _masks(n)
    decay = jnp.where(causal, jnp.exp(jnp.where(causal, gc - gr, 0.0)), 0.0)
    kb = k * beta
    vb = v * beta
    lower = jnp.where(strict, _bmm_nt(kb, k) * decay, 0.0)
    t_mat = _neumann_inverse(-lower) if p is None else _given_inverse(-lower, p)
    egc = jnp.exp(gc)
    u = _bmm(t_mat, vb)
    w = _bmm(t_mat, kb * egc)
    attn = jnp.where(causal, _bmm_nt(q, k) * decay, 0.0)
    v_new = u - _bmm(w, s)
    o = _bmm(q * egc, s) + _bmm(attn, v_new)
    k_dec = k * jnp.exp(gl - gc)
    s_new = s * jnp.exp(gl) + _bmm_tn(k_dec, v_new)
    return s_new, o, t_mat


def _rwkv_chunk(s, r, lw, k, v, a, b, p=None):
    n = r.shape[1]
    causal, strict, _ = _masks(n)
    tri = jnp.broadcast_to(causal.astype(f32), (r.shape[0], n, n))
    lc = _bmm_exact(tri, lw)
    ein = jnp.exp(lc)
    eout = jnp.exp(-lc)
    a_t = a * jnp.exp(lc - lw)
    b_t = b * eout
    k_t = k * eout
    r_t = r * ein
    a_ab = jnp.where(strict, _bmm_nt(a_t, b_t), 0.0)
    a_ak = jnp.where(strict, _bmm_nt(a_t, k_t), 0.0)
    inv = _neumann_inverse(a_ab) if p is None else _given_inverse(a_ab, p)
    u = _bmm(inv, _bmm_nt(a_t, s) + _bmm(a_ak, v))
    y = (_bmm_nt(r_t, s) + _bmm(jnp.where(causal, _bmm_nt(r_t, b_t), 0.0), u)
         + _bmm(jnp.where(causal, _bmm_nt(r_t, k_t), 0.0), v))
    e_last = jnp.exp(jnp.sum(lw, axis=1, keepdims=True))
    s_new = s * e_last + _bmm_tn(u, b_t * e_last) + _bmm_tn(v, k_t * e_last)
    return s_new, y, inv


def _heads_in(ref, rows):
    return jnp.stack([ref[rows, h * HEAD_DIM:(h + 1) * HEAD_DIM] for h in range(HEADS)], axis=0)


def _heads_out(ref, rows, val):
    for h in range(HEADS):
        ref[rows, h * HEAD_DIM:(h + 1) * HEAD_DIM] = val[h]


def _gdn_scalars(bt, gt):
    n = bt.shape[0]
    gtt = gt.T
    hs = range(HEADS)
    return [jnp.stack([bt[:, h:h + 1] for h in hs], axis=0),
            jnp.stack([gt[:, HEADS + h:HEADS + h + 1] for h in hs], axis=0),
            jnp.stack([gtt[HEADS + h:HEADS + h + 1, :] for h in hs], axis=0),
            jnp.stack([gt[n - 1:n, HEADS + h:HEADS + h + 1] for h in hs], axis=0)]


def _gdn_scalars_back(dbeta, dgc, dgr, dgl):
    n = dbeta.shape[1]
    lane = lax.broadcasted_iota(jnp.int32, (n, LANES), 1)
    row = lax.broadcasted_iota(jnp.int32, (n, LANES), 0)
    sub = lax.broadcasted_iota(jnp.int32, (LANES, n), 0)
    db = jnp.zeros((n, LANES), f32)
    dg = jnp.zeros((n, LANES), f32)
    dgt = jnp.zeros((LANES, n), f32)
    for h in range(HEADS):
        db = jnp.where(lane == h, dbeta[h], db)
        dg = jnp.where(lane == HEADS + h, dgc[h] + jnp.where(row == n - 1, dgl[h], 0.0), dg)
        dgt = jnp.where(sub == HEADS + h, dgr[h], dgt)
    return [db, dg + dgt.T]


SCAN_GROUP = 2


def _scan_steps(t):
    g = SCAN_GROUP if (t // CHUNK) % SCAN_GROUP == 0 else 1
    return g, t // (CHUNK * g)


def _scan_spec(width, g, n, reverse):
    if reverse:
        return pl.BlockSpec((g * CHUNK, width), lambda i: (n - 1 - i, 0))
    return pl.BlockSpec((g * CHUNK, width), lambda i: (i, 0))


def _hist_spec(g, n, reverse):
    blk = (g, HEADS, HEAD_DIM, HEAD_DIM)
    if reverse:
        return pl.BlockSpec(blk, lambda i: (n - 1 - i, 0, 0, 0))
    return pl.BlockSpec(blk, lambda i: (i, 0, 0, 0))


def _job_list(job):
    return [] if job is None else (list(job) if isinstance(job, (list, tuple)) else [job])


def _job_parts(job):
    jobs = _job_list(job)
    sems = [pltpu.SemaphoreType.DMA((j["nsem"],)) for j in jobs for _ in range(2)]
    return [a for j in jobs for a in j["ins"]], [o for j in jobs for o in j["out_shapes"]], sems


_JOB_PHASES = ("start", "forward", "finish")


def _job_phase(job, phase, in_refs, out_refs, sems):
    ki = ko = 0
    for n, j in enumerate(_job_list(job)):
        ni, no = len(j["ins"]), len(j["out_shapes"])
        if phase in j:
            j[phase](in_refs[ki:ki + ni], out_refs[ko:ko + no], sems[2 * n], sems[2 * n + 1])
        ki, ko = ki + ni, ko + no


def _job_steps(job, in_refs, out_refs, sems, step, last):
    if job is None:
        return
    for phase, at in zip(_JOB_PHASES, (0, max(last - 2, 0), last)):
        @pl.when(step == at)
        def _(phase=phase):
            _job_phase(job, phase, in_refs, out_refs, sems)


def _scan_fwd(name, fn, rows_in, scal_in=(), job=None):
    t = rows_in[0].shape[0]
    grp, n = _scan_steps(t)
    nr, ns = len(rows_in), len(scal_in)
    j_ins, j_outs, j_sems = _job_parts(job)
    nji, njo = len(j_ins), len(j_outs)

    def body(*refs):
        o_ref, sh_ref, ph_ref = refs[nr + ns + nji:nr + ns + nji + 3]
        s_scr = refs[nr + ns + nji + 3 + njo]
        _job_steps(job, refs[nr + ns:nr + ns + nji], refs[nr + ns + nji + 3:nr + ns + nji + 3 + njo],
                   refs[nr + ns + nji + 3 + njo + 1:], pl.program_id(0), n - 1)

        @pl.when(pl.program_id(0) == 0)
        def _():
            s_scr[...] = jnp.zeros_like(s_scr)

        s = s_scr[...]
        for sub in range(grp):
            rows = slice(sub * CHUNK, (sub + 1) * CHUNK)
            sh_ref[sub] = s
            ins = [_heads_in(r, rows) for r in refs[:nr]]
            if ns:
                ins += _gdn_scalars(*[r[rows, :] for r in refs[nr:nr + ns]])
            s, o, p = fn(s, *ins)
            _heads_out(o_ref, rows, o)
            ph_ref[sub] = p
        s_scr[...] = s

    return pl.pallas_call(
        body, grid=(n,),
        in_specs=[_scan_spec(a.shape[1], grp, n, False) for a in (*rows_in, *scal_in)] + [_ANY] * nji,
        out_specs=[_scan_spec(WIDTH, grp, n, False), _hist_spec(grp, n, False), _hist_spec(grp, n, False)] + [_ANY] * njo,
        out_shape=[jax.ShapeDtypeStruct((t, WIDTH), f32)]
        + [jax.ShapeDtypeStruct((t // CHUNK, HEADS, HEAD_DIM, HEAD_DIM), f32)] * 2 + j_outs,
        scratch_shapes=[pltpu.VMEM((HEADS, HEAD_DIM, HEAD_DIM), f32)] + j_sems,
        compiler_params=pltpu.CompilerParams(dimension_semantics=("arbitrary",)),
        name=name,
    )(*rows_in, *scal_in, *j_ins)


def _scan_bwd(name, fn, rows_in, scal_in, s_hist, p_hist, d_out, job=None):
    t = rows_in[0].shape[0]
    grp, n = _scan_steps(t)
    nr, ns = len(rows_in), len(scal_in)
    j_ins, j_outs, j_sems = _job_parts(job)
    nji, njo = len(j_ins), len(j_outs)

    def body(*refs):
        sh_ref, ph_ref, do_ref = refs[nr + ns:nr + ns + 3]
        base = nr + ns + 3 + nji
        g_refs = refs[base:base + nr + ns]
        ds_scr = refs[base + nr + ns + njo]
        _job_steps(job, refs[nr + ns + 3:base], refs[base + nr + ns:base + nr + ns + njo],
                   refs[base + nr + ns + njo + 1:], pl.program_id(0), n - 1)

        @pl.when(pl.program_id(0) == 0)
        def _():
            ds_scr[...] = jnp.zeros_like(ds_scr)

        ds = ds_scr[...]
        for sub in reversed(range(grp)):
            rows = slice(sub * CHUNK, (sub + 1) * CHUNK)
            ins = [_heads_in(r, rows) for r in refs[:nr]]
            if ns:
                ins += _gdn_scalars(*[r[rows, :] for r in refs[nr:nr + ns]])
            p = ph_ref[sub]
            _, vjp = jax.vjp(lambda s, *a, p=p: fn(s, *a, p=p)[:2], sh_ref[sub], *ins)
            g = vjp((ds, _heads_in(do_ref, rows)))
            ds = g[0]
            for r, v in zip(g_refs[:nr], g[1:1 + nr]):
                _heads_out(r, rows, v)
            if ns:
                for r, v in zip(g_refs[nr:], _gdn_scalars_back(*g[1 + nr:])):
                    r[rows, :] = v
        ds_scr[...] = ds

    arrs = (*rows_in, *scal_in)
    return pl.pallas_call(
        body, grid=(n,),
        in_specs=[_scan_spec(a.shape[1], grp, n, True) for a in arrs]
        + [_hist_spec(grp, n, True), _hist_spec(grp, n, True), _scan_spec(WIDTH, grp, n, True)] + [_ANY] * nji,
        out_specs=[_scan_spec(a.shape[1], grp, n, True) for a in arrs] + [_ANY] * njo,
        out_shape=[jax.ShapeDtypeStruct(a.shape, f32) for a in arrs] + j_outs,
        scratch_shapes=[pltpu.VMEM((HEADS, HEAD_DIM, HEAD_DIM), f32)] + j_sems,
        compiler_params=pltpu.CompilerParams(dimension_semantics=("arbitrary",)),
        name=name,
    )(*arrs, s_hist, p_hist, d_out, *j_ins)


def _fn_norm_mod(ps, ts):
    nw, shift, scale = ps
    (x,) = ts
    return [_rms(x, nw, NORM_EPS) * (1.0 + scale) + shift]


def _fn_norm_mod_and_x(ps, ts):
    return _fn_norm_mod(ps, ts) + [ts[0]]


def _fn_resid_norm_mod(ps, ts):
    gate, nw, shift, scale = ps
    x, mo = ts
    x1 = x + gate * mo
    return [x1, _rms(x1, nw, NORM_EPS) * (1.0 + scale) + shift]


def _fn_gdn_pre(ps, ts):
    cw = ps[:12]
    alog, dtb, seg, tri = ps[12:]
    ba = ts[12]
    outs = []
    for part in range(3):
        x = ts[4 * part:4 * part + 4]
        w = cw[4 * part:4 * part + 4]
        conv = w[3] * x[0] + w[2] * x[1] + w[1] * x[2] + w[0] * x[3]
        u = _silu(conv)
        if part < 2:
            u = u * lax.rsqrt(_segsum(u * u, seg) + 1e-6)
            if part == 0:
                u = u * (HEAD_DIM ** -0.5)
        outs.append(u)
    beta = jax.nn.sigmoid(ba)
    g = -jnp.exp(alog) * _softplus(ba + dtb)
    gc = _xdot_l(tri, g)
    return outs + [beta, gc]


def _fn_gdn_post(ps, ts):
    ow, seg = ps
    o, z = ts
    ms = _segsum(o * o, seg) * (1.0 / HEAD_DIM)
    return [o * lax.rsqrt(ms + NORM_EPS) * ow * _silu(z)]


def _fn_rwkv_pre(ps, ts):
    mu_r, mu_k, mu_v, mu_l, mu_g, w0, w2p, a0, a2p, g2p, k_k, k_a, seg = ps
    r0, r1, k0, k1, v0, v1, l0, l1, g0, g1 = ts
    xr = r0 + (r1 - r0) * mu_r
    xk = k0 + (k1 - k0) * mu_k
    xv = v0 + (v1 - v0) * mu_v
    xl = l0 + (l1 - l0) * mu_l
    xg = g0 + (g1 - g0) * mu_g
    w = -_softplus(-(w0 + _bdot(jnp.tanh(xl), w2p))) - 0.5
    lw = -jnp.exp(w)
    a = jax.nn.sigmoid(a0 + _bdot(xl, a2p))
    g = _bdot(jax.nn.sigmoid(xg), g2p)
    kk = xk * k_k
    kk = kk * lax.rsqrt(_segsum(kk * kk, seg) + 1e-6)
    k2 = xk * (1.0 + (a - 1.0) * k_a)
    return [xr, lw, k2, xv, -kk, kk * a, g]


def _fn_rwkv_post(ps, ts):
    lw_, lb_, rk, seg = ps
    y, r, k2, v, g = ts
    inv = 1.0 / HEAD_DIM
    yc = y - _segsum(y, seg) * inv
    var = _segsum(yc * yc, seg) * inv
    yn = yc * lax.rsqrt(var + LNX_EPS) * lw_ + lb_
    bonus = _segsum(r * k2 * rk, seg) * v
    return [(yn + bonus) * g]


def _fn_merge(ps, ts):
    gla, glb, ya, yb = ts
    return [jax.nn.sigmoid(gla) * ya + jax.nn.sigmoid(glb) * yb]


def _fn_convglu(ps, ts):
    c0, c1, c2 = ps
    g0, g1, g2, up = ts
    return [_silu(c2 * g0 + c1 * g1 + c0 * g2) * up]


def _final_stage(x1, fo, tgt, gate2, nfw, tt):
    rows, d = x1.shape

    def loss_fn(gate, nw, xa, fa, tg):
        y = _rms(xa + gate * fa, nw, NORM_EPS)
        err = (y - tg) ** 2
        return 0.5 * jnp.sum(jnp.mean(err, axis=-1, keepdims=True), axis=0, keepdims=True)

    def body(x_ref, f_ref, t_ref, g_ref, w_ref, dx_ref, df_ref, dg_ref, dw_ref, l_ref):
        i = pl.program_id(0)
        args = (g_ref[...], w_ref[...], x_ref[...], f_ref[...])
        tg = t_ref[...]
        lv, vjp = jax.vjp(lambda g, w, xa, fa: loss_fn(g, w, xa, fa, tg), *args)
        dg, dw, dx, df = vjp(jnp.ones((1, 1), f32))
        dx_ref[...] = dx
        df_ref[...] = df.astype(df_ref.dtype)

        @pl.when(i == 0)
        def _():
            dg_ref[...] = jnp.zeros_like(dg_ref)
            dw_ref[...] = jnp.zeros_like(dw_ref)
            l_ref[...] = jnp.zeros_like(l_ref)

        dg_ref[...] += dg
        dw_ref[...] += dw
        l_ref[...] += jnp.broadcast_to(lv, l_ref.shape)

    row = pl.BlockSpec((tt, d), lambda i: (i, 0))
    vec = pl.BlockSpec((1, d), lambda i: (0, 0))
    return pl.pallas_call(
        body, grid=(rows // tt,),
        in_specs=[row, row, row, vec, vec],
        out_specs=[row, row, vec, vec, pl.BlockSpec((1, LANES), lambda i: (0, 0))],
        out_shape=[jax.ShapeDtypeStruct((rows, d), f32), jax.ShapeDtypeStruct((rows, d), bf16)]
        + [jax.ShapeDtypeStruct((1, d), f32)] * 2
        + [jax.ShapeDtypeStruct((1, LANES), f32)],
        compiler_params=pltpu.CompilerParams(dimension_semantics=("arbitrary",)),
        name="loss_head",
    )(x1, fo, tgt, gate2, nfw)


def _ada_fwd(c_all, w_shard, b_cols):
    def body(c_ref, w_ref, b_ref, cond_ref, mod_ref):
        cond = _silu(c_ref[...])
        cond_ref[...] = cond
        mod_ref[...] = jnp.dot(cond.astype(bf16), w_ref[...].astype(bf16), preferred_element_type=f32) + b_ref[...]

    n = w_shard.shape[1]
    return pl.pallas_call(
        body, out_shape=[jax.ShapeDtypeStruct(c_all.shape, f32), jax.ShapeDtypeStruct((c_all.shape[0], n), f32)],
        name="ada_fwd",
    )(c_all, w_shard, b_cols)


def _adamw(name, w, g, m, v):
    rows, width = w.shape
    tt = _pick(rows, 256, 8)
    c1 = 1.0 - ADAM_B1 ** ADAM_STEP
    c2 = 1.0 - ADAM_B2 ** ADAM_STEP

    def body(w_ref, g_ref, m_ref, v_ref, d_ref, mo_ref, vo_ref):
        gg = g_ref[...]
        mn = ADAM_B1 * m_ref[...] + (1.0 - ADAM_B1) * gg
        vn = ADAM_B2 * v_ref[...] + (1.0 - ADAM_B2) * (gg * gg)
        m_hat = mn / c1
        v_hat = vn / c2
        d_ref[...] = -ADAM_LR * (m_hat / (jnp.sqrt(v_hat) + ADAM_EPS) + ADAM_WD * w_ref[...])
        mo_ref[...] = mn
        vo_ref[...] = vn

    spec = pl.BlockSpec((tt, width), lambda i: (i, 0))
    return pl.pallas_call(
        body, grid=(rows // tt,), in_specs=[spec] * 4, out_specs=[spec] * 3,
        out_shape=[jax.ShapeDtypeStruct((rows, width), f32)] * 3,
        compiler_params=pltpu.CompilerParams(dimension_semantics=("parallel",)),
        name=name,
    )(w, g, m, v)


def _place():
    return lax.axis_index("x"), lax.axis_index("y"), lax.axis_index("c")


def _ag8(name, blk):
    m, w = blk.shape

    def body(x_ref, out_ref, send_sems, recv_sems, local_sem):
        x, y, c = _place()
        me, sibling = (x, y, c), (x, y, 1 - c)
        chips = _other_chips(x, y)

        def slot(px, py, pc):
            return out_ref.at[4 * px + 2 * py + pc]

        def copy(k, block, to, src=None):
            return pltpu.make_async_remote_copy(src_ref=slot(*block) if src is None else src, dst_ref=slot(*block),
                                                send_sem=send_sems.at[k], recv_sem=recv_sems.at[k], device_id=to,
                                                device_id_type=MESH_ID)

        mine = pltpu.make_async_copy(x_ref, slot(*me), local_sem)
        mine.start()
        first = [copy(0, me, sibling, src=x_ref)] + [copy(1 + j, me, (*chip, c), src=x_ref) for j, chip in enumerate(chips)]
        for cp in first:
            cp.start()
        passed = [copy(4 + j, (*chip, c), sibling) for j, chip in enumerate(chips)]
        for j, chip in enumerate(chips):
            copy(1 + j, (*chip, c), me).wait_recv()
            passed[j].start()
        copy(0, sibling, me).wait_recv()
        for j, chip in enumerate(chips):
            copy(4 + j, (*chip, 1 - c), me).wait_recv()
        for cp in first + passed:
            cp.wait_send()
        mine.wait()

    return pl.pallas_call(
        body, out_shape=jax.ShapeDtypeStruct((8, m, w), blk.dtype),
        in_specs=[pl.BlockSpec(memory_space=pltpu.VMEM)], out_specs=pl.BlockSpec(memory_space=pltpu.VMEM),
        scratch_shapes=[pltpu.SemaphoreType.DMA((7,)), pltpu.SemaphoreType.DMA((7,)), pltpu.SemaphoreType.DMA],
        name=name,
    )(blk)


def _ag8_job(blk):
    def plan(x_refs, out_refs, send_sems, recv_sems):
        x_ref, out_ref = x_refs[0], out_refs[0]
        x, y, c = _place()
        me, sibling = (x, y, c), (x, y, 1 - c)
        chips = _other_chips(x, y)
        slot = lambda px, py, pc: out_ref.at[4 * px + 2 * py + pc]
        cp = lambda k, block, to, src=None: _rcopy(slot(*block) if src is None else src, slot(*block), send_sems, recv_sems, k, to)
        first = [cp(0, me, sibling, x_ref)] + [cp(1 + j, me, (*chip, c), x_ref) for j, chip in enumerate(chips)]
        landed = [cp(1 + j, (*chip, c), me) for j, chip in enumerate(chips)]
        passed = [cp(4 + j, (*chip, c), sibling) for j, chip in enumerate(chips)]
        handed = [cp(0, sibling, me)] + [cp(4 + j, (*chip, 1 - c), me) for j, chip in enumerate(chips)]
        mine = pltpu.make_async_copy(x_ref, slot(*me), send_sems.at[7])
        return first, landed, passed, handed, mine

    def start(*refs):
        first, _, _, _, mine = plan(*refs)
        mine.start()
        for cp in first:
            cp.start()

    def forward(*refs):
        _, landed, passed, _, _ = plan(*refs)
        for arrived, fw in zip(landed, passed):
            arrived.wait_recv()
            fw.start()

    def finish(*refs):
        first, _, passed, handed, mine = plan(*refs)
        for cp in handed:
            cp.wait_recv()
        for cp in first + passed:
            cp.wait_send()
        mine.wait()

    return dict(ins=[blk], out_shapes=[jax.ShapeDtypeStruct((8,) + blk.shape, blk.dtype)], nsem=8, start=start,
                forward=forward, finish=finish)


def _other_chips(x, y):
    return [(1 - x, y), (x, 1 - y), (1 - x, 1 - y)]


_ANY = pl.BlockSpec(memory_space=pl.ANY)


def _rcopy(src, dst, send_sems, recv_sems, k, dev):
    return pltpu.make_async_remote_copy(src_ref=src, dst_ref=dst, send_sem=send_sems.at[k], recv_sem=recv_sems.at[k],
                                        device_id=dev, device_id_type=MESH_ID)


def _run_job(name, job):
    j_ins, j_outs, j_sems = _job_parts(job)
    n = len(j_ins)

    def body(*refs):
        for phase in _JOB_PHASES:
            _job_phase(job, phase, refs[:n], refs[n:n + len(j_outs)], refs[n + len(j_outs):])

    return pl.pallas_call(body, out_shape=j_outs, in_specs=[_ANY] * n, out_specs=[_ANY] * len(j_outs),
                          scratch_shapes=j_sems, name=name)(*j_ins)


def _ag4_job(ws):
    n = len(ws)

    def plan(w_refs, out_refs, send_sems, recv_sems):
        x, y, c = _place()
        chip = 2 * x + y
        sibling = (x, y, 1 - c)
        chips = _other_chips(x, y)
        mine = [pl.ds(c * (w.shape[0] // 2), w.shape[0] // 2) for w in ws]
        other = [pl.ds((1 - c) * (w.shape[0] // 2), w.shape[0] // 2) for w in ws]
        rc = lambda src, dst, k, dev: _rcopy(src, dst, send_sems, recv_sems, k, dev)
        first = [rc(w_refs[t].at[mine[t]], out_refs[t].at[chip, mine[t]], 7 * t + k, (px, py, c))
                 for t in range(n) for k, (px, py) in enumerate(chips)]
        own = [rc(w_refs[t], out_refs[t].at[chip], 7 * t + 6, sibling) for t in range(n)]
        landed = [rc(out_refs[t].at[2 * px + py, mine[t]], out_refs[t].at[2 * px + py, mine[t]], 7 * t + k, (px, py, c))
                  for t in range(n) for k, (px, py) in enumerate(chips)]
        forward = [rc(out_refs[t].at[2 * px + py, mine[t]], out_refs[t].at[2 * px + py, mine[t]], 7 * t + 3 + k, sibling)
                   for t in range(n) for k, (px, py) in enumerate(chips)]
        handed = [rc(out_refs[t].at[2 * px + py, other[t]], out_refs[t].at[2 * px + py, other[t]], 7 * t + 3 + k, sibling)
                  for t in range(n) for k, (px, py) in enumerate(chips)]
        return first, own, landed, forward, handed

    def start(*refs):
        first, own, _, _, _ = plan(*refs)
        for cp in first + own:
            cp.start()

    def forward(*refs):
        _, _, landed, passed, _ = plan(*refs)
        for arrived, fw in zip(landed, passed):
            arrived.wait_recv()
            fw.start()

    def finish(*refs):
        first, own, _, passed, handed = plan(*refs)
        for cp in handed + own:
            cp.wait_recv()
        for cp in first + own + passed:
            cp.wait_send()

    return dict(ins=ws, out_shapes=[jax.ShapeDtypeStruct((4,) + w.shape, w.dtype) for w in ws], nsem=7 * n,
                start=start, forward=forward, finish=finish)


def _sibling_swap_job(gs):
    n = len(gs)

    def plan(g_refs, recv_refs, send_sems, recv_sems):
        x, y, c = _place()
        return [_rcopy(g_refs[t].at[s_, pl.ds((1 - c) * (gs[t].shape[1] // 2), gs[t].shape[1] // 2)], recv_refs[t].at[s_],
                       send_sems, recv_sems, 4 * t + s_, (x, y, 1 - c)) for t in range(n) for s_ in range(4)]

    def start(*refs):
        for cp in plan(*refs):
            cp.start()

    def finish(*refs):
        copies = plan(*refs)
        for cp in copies:
            cp.wait_recv()
        for cp in copies:
            cp.wait_send()

    return dict(ins=gs, out_shapes=[jax.ShapeDtypeStruct((4, g.shape[1] // 2, g.shape[2]), g.dtype) for g in gs],
                nsem=4 * n, start=start, finish=finish)


def _chip_exchange_job(ps):
    n = len(ps)

    def plan(p_refs, recv_refs, send_sems, recv_sems):
        x, y, c = _place()
        return [_rcopy(p_refs[t].at[2 * px + py], recv_refs[t].at[k], send_sems, recv_sems, 3 * t + k, (px, py, c))
                for t in range(n) for k, (px, py) in enumerate(_other_chips(x, y))]

    def start(*refs):
        for cp in plan(*refs):
            cp.start()

    def finish(*refs):
        copies = plan(*refs)
        for cp in copies:
            cp.wait_recv()
        for cp in copies:
            cp.wait_send()

    return dict(ins=ps, out_shapes=[jax.ShapeDtypeStruct((3,) + p.shape[1:], p.dtype) for p in ps], nsem=3 * n,
                start=start, finish=finish)


_JOIN_PIECES = 4


def _rs_sibling_join(qs):
    n = len(qs)
    npc = _JOIN_PIECES

    def body(*refs):
        q_refs, out_refs = refs[:n], refs[n:2 * n]
        send_sems, recv_sems = refs[2 * n:]
        x, y, c = _place()
        copies = []
        for t in range(n):
            rh = qs[t].shape[0] // 2
            pr = rh // npc
            for i in range(npc):
                rows = pl.ds(c * rh + i * pr, pr)
                cp = _rcopy(q_refs[t].at[rows], out_refs[t].at[rows], send_sems, recv_sems, npc * t + i,
                            (x, y, 1 - c))
                cp.start()
                copies.append(cp)
        for t in range(n):
            rh = qs[t].shape[0] // 2
            pr = rh // npc
            for i in range(npc):
                rows = pl.ds((1 - c) * rh + i * pr, pr)
                _rcopy(q_refs[t].at[rows], out_refs[t].at[rows], send_sems, recv_sems, npc * t + i,
                       (x, y, 1 - c)).wait_recv()
        for cp in copies:
            cp.wait_send()

    return pl.pallas_call(
        body, out_shape=[jax.ShapeDtypeStruct(q.shape, q.dtype) for q in qs],
        in_specs=[_ANY] * n, out_specs=[_ANY] * n, input_output_aliases={t: t for t in range(n)},
        scratch_shapes=[pltpu.SemaphoreType.DMA((npc * n,)), pltpu.SemaphoreType.DMA((npc * n,))],
        name="grads_sibling_join",
    )(*qs)


def _add_half(name, g, recv, chip, ci):
    S, r, w = g.shape
    rh = r // 2
    tt = _pick(rh, 256, 16)
    nb = rh // tt

    def body(chip_ref, core_ref, a_ref, b_ref, own_ref, ob_ref):
        v = a_ref[...] + b_ref[...]
        ob_ref[...] = v.astype(bf16)

        @pl.when(pl.program_id(1) == chip_ref[0])
        def _():
            own_ref[...] = v[0]

    grid_spec = pltpu.PrefetchScalarGridSpec(
        num_scalar_prefetch=2, grid=(nb, S),
        in_specs=[pl.BlockSpec((1, tt, w), lambda i, s_, ch, co: (s_, co[0] * nb + i, 0)),
                  pl.BlockSpec((1, tt, w), lambda i, s_, ch, co: (s_, i, 0))],
        out_specs=[pl.BlockSpec((tt, w), lambda i, s_, ch, co: (i, 0)),
                   pl.BlockSpec((1, tt, w), lambda i, s_, ch, co: (s_, i, 0))])
    return pl.pallas_call(body, grid_spec=grid_spec,
                          out_shape=[jax.ShapeDtypeStruct((rh, w), f32), jax.ShapeDtypeStruct((S, rh, w), bf16)],
                          compiler_params=pltpu.CompilerParams(dimension_semantics=("arbitrary", "arbitrary")),
                          name=name)(chip.reshape(1).astype(jnp.int32), ci.reshape(1).astype(jnp.int32), g, recv)


def _sum_chip(name, own, others, ci):
    rh, w = own.shape
    tt = _pick(rh, 128, 16)
    nb = rh // tt

    def body(core_ref, a_ref, b_ref, o_ref):
        o_ref[...] = ((a_ref[...] + b_ref[0].astype(f32)) + b_ref[1].astype(f32)) + b_ref[2].astype(f32)

    grid_spec = pltpu.PrefetchScalarGridSpec(
        num_scalar_prefetch=1, grid=(nb,),
        in_specs=[pl.BlockSpec((tt, w), lambda i, co: (i, 0)),
                  pl.BlockSpec((3, tt, w), lambda i, co: (0, i, 0))],
        out_specs=pl.BlockSpec((tt, w), lambda i, co: (co[0] * nb + i, 0)))
    return pl.pallas_call(body, grid_spec=grid_spec, out_shape=jax.ShapeDtypeStruct((2 * rh, w), f32),
                          name=name)(ci.reshape(1).astype(jnp.int32), own, others)


def _sum_devices(gathered, late, late_rows, head_row):
    _, rows, width = gathered.shape

    def body(g_ref, l_ref, out_ref, head_ref):
        acc, acc_l = g_ref[0], l_ref[0]
        for d in range(1, 8):
            acc = acc + g_ref[d]
            acc_l = acc_l + l_ref[d]
        out_ref[...] = acc
        for k, r in enumerate(late_rows):
            out_ref[r:r + 1, :] = acc_l[k:k + 1, :]
        row = acc[head_row:head_row + 1, :]
        hs = row[:, 0:HEAD_DIM]
        for h in range(1, HEADS):
            hs = hs + row[:, h * HEAD_DIM:(h + 1) * HEAD_DIM]
        head_ref[...] = jnp.zeros_like(head_ref)
        head_ref[0:1, 0:HEAD_DIM] = hs

    return pl.pallas_call(
        body, out_shape=[jax.ShapeDtypeStruct((rows, width), f32), jax.ShapeDtypeStruct((8, LANES), f32)],
        name="small_grads_sum",
    )(gathered, late)


def _pack(arrs, rows_mult, dtype):
    flat = jnp.concatenate([a.reshape(-1).astype(dtype) for a in arrs])
    per = PACK_W * rows_mult
    total = -(-flat.shape[0] // per) * per
    return jnp.pad(flat, (0, total - flat.shape[0])).reshape(total // PACK_W, PACK_W)


def _unpack(buf, shapes):
    flat = buf.reshape(-1)
    out, off = [], 0
    for s in shapes:
        n = int(np.prod(s))
        out.append(flat[off:off + n].reshape(s))
        off += n
    return out


_BIG = ["w_in", "w_branch_gdn", "w_branch_rwkv", "w_out", "w_ffn_in", "w_ffn_out"]
_MID = ["conv_gdn", "conv_ffn", "w2", "a2", "g2"]
_SMALL =["b_ada", "norm1_w", "a_log", "dt_bias", "onorm_gdn", "mu_rwkv", "w0", "a0", "k_k", "k_a", "r_k", "lnx_w",
          "lnx_b", "norm2_w", "norm_f_w"]
_ORDER = ["w_ada", "b_ada", "norm1_w", "w_in", "conv_gdn", "a_log", "dt_bias", "onorm_gdn", "w_branch_gdn", "mu_rwkv",
          "w0", "w2", "a0", "a2", "g2", "k_k", "k_a", "r_k", "lnx_w", "lnx_b", "w_branch_rwkv", "w_out", "norm2_w",
          "w_ffn_in", "conv_ffn", "w_ffn_out", "norm_f_w"]


_WIN_SEGMENTS = [(0, 1536, 0), (2064, 3600, 1536), (1536, 2048, 3072), (3600, 3728, 3584), (2048, 2064, 3712),
                 (3728, 3888, 3840), (3888, 5936, 4096)]
_WIN_PADDED = 6144
_COL_QKV, _COL_RKV, _COL_Z = (0, 1, 2), (3, 4, 5), 6
_COL_LORA, _COL_BA = 28, 29
_COL_GATE_LORA = 15
_COL_GL = (4, 5)
_JOINT_QKV, _JOINT_RKV, _JOINT_Z, _JOINT_GL = 0, 1, 6, 2
_SMALL_BLOCKS_AT = 3584


def _win_pad(shards):
    n = shards.shape[2]
    parts, at = [], 0
    for lo, hi, dst in _WIN_SEGMENTS:
        if dst > at:
            parts.append(jnp.zeros((shards.shape[1], dst - at), shards.dtype))
        c = lo
        while c < hi:
            j = c // n
            e = min(hi, (j + 1) * n)
            parts.append(shards[j][:, c - j * n:e - j * n])
            c = e
        at = dst + hi - lo
    if at < _WIN_PADDED:
        parts.append(jnp.zeros((shards.shape[1], _WIN_PADDED - at), shards.dtype))
    return jnp.concatenate(parts, axis=1)


def _win_unpad_shards(g, n):
    shards = []
    for j in range(4):
        parts = []
        for lo, hi, dst in sorted(_WIN_SEGMENTS):
            a, b = max(lo, j * n), min(hi, (j + 1) * n)
            if a < b:
                parts.append(g[:, dst + a - lo:dst + b - lo])
        shards.append(jnp.concatenate(parts, axis=1))
    return jnp.stack(shards)


def kernel(x, c, w_ada, b_ada, norm1_w, w_in, conv_gdn, a_log, dt_bias, onorm_gdn, w_branch_gdn, mu_rwkv, w0, w2, a0, a2, g2, k_k, k_a, r_k, lnx_w, lnx_b, w_branch_rwkv, w_out, norm2_w, w_ffn_in, conv_ffn, w_ffn_out, norm_f_w, loss_target, m_w_ada, m_b_ada, m_norm1_w, m_w_in, m_conv_gdn, m_a_log, m_dt_bias, m_onorm_gdn, m_w_branch_gdn, m_mu_rwkv, m_w0, m_w2, m_a0, m_a2, m_g2, m_k_k, m_k_a, m_r_k, m_lnx_w, m_lnx_b, m_w_branch_rwkv, m_w_out, m_norm2_w, m_w_ffn_in, m_conv_ffn, m_w_ffn_out, m_norm_f_w, v_w_ada, v_b_ada, v_norm1_w, v_w_in, v_conv_gdn, v_a_log, v_dt_bias, v_onorm_gdn, v_w_branch_gdn, v_mu_rwkv, v_w0, v_w2, v_a0, v_a2, v_g2, v_k_k, v_k_a, v_r_k, v_lnx_w, v_lnx_b, v_w_branch_rwkv, v_w_out, v_norm2_w, v_w_ffn_in, v_conv_ffn, v_w_ffn_out, v_norm_f_w):
    args = dict(locals())
    W = {n: args[n] for n in _ORDER}
    Mo = {n: args["m_" + n] for n in _ORDER}
    Vo = {n: args["v_" + n] for n in _ORDER}
    shapes = {n: W[n].shape for n in _ORDER}
    sq = lambda a: a.reshape(a.shape[-2:]) if a.ndim == 3 else a.reshape(1, -1)
    row = lambda a: a.reshape(1, -1)

    xi, yi, ci = lax.axis_index("x"), lax.axis_index("y"), lax.axis_index("c")
    dev = 4 * xi + 2 * yi + ci
    chip = 2 * xi + yi

    x2 = x[0]
    tgt = loss_target[0]
    T, D = x2.shape
    tt_l = _pick(T, 512, CHUNK)
    tt_p = _pick(T, 256, CHUNK)
    tt_h = _pick(T, 128, CHUNK)

    small_blk = _pack([c] + [W[n] for n in _MID], 8, f32)
    small_all, win_s = _run_job("gather_c_mid_and_w_in", [_ag8_job(small_blk), _ag4_job([sq(W["w_in"]).astype(bf16)])])
    c_all = small_all[:, 0, :]
    per_chip = small_all[0::2].reshape(4, -1)[:, D:]
    mid_w, at = {}, 0
    for n in _MID:
        nr, nw = shapes[n][1:]
        mid_w[n] = per_chip[:, at:at + nr * nw].reshape(4, nr, nw).transpose(1, 0, 2).reshape(nr, 4 * nw)
        at += nr * nw
    conv_gdn_f, conv_ffn_f = mid_w["conv_gdn"], mid_w["conv_ffn"]
    w2f, a2f, g2f = mid_w["w2"], mid_w["a2"], mid_w["g2"]

    later_weights = _ag4_job([sq(W[n]).astype(bf16) for n in _BIG[1:]])
    win_p = _win_pad(win_s)
    zpad = lambda a, top, bot: jnp.pad(a, ((top, bot), (0, 0)))
    w2p, a2p, g2p = zpad(w2f, 0, 64), zpad(a2f, 64, 0), zpad(g2f, 0, 96)

    ncol = shapes["w_ada"][2]
    b_cols = lax.dynamic_slice(sq(W["b_ada"]), (0, chip * ncol), (1, ncol))
    cond16, mod_cols = _ada_fwd(jnp.pad(c_all, ((0, 8), (0, 0))), sq(W["w_ada"]), b_cols)
    mod_all = _ag8("gather_mod", mod_cols[:8])
    mod_mine = lax.dynamic_slice(mod_all[0::2], (0, dev, 0), (4, 1, ncol)).reshape(1, 4 * ncol)
    shift1, scale1, gate1, shift2, scale2, gate2 = [mod_mine[:, i * D:(i + 1) * D] for i in range(6)]

    seg = _seg_matrix(WIDTH, HEAD_DIM)
    norm1 = [sq(W["norm1_w"]), shift1, scale1]
    h1 = _stage_fwd("norm_mod1", _fn_norm_mod, [_whole(x2)], norm1, [(D, bf16)], tt_l)[0]
    p = _matmul("in_proj", h1, win_p, "nn")

    cgq = [row(conv_gdn_f[j, part * WIDTH:(part + 1) * WIDTH]) for part in range(3) for j in range(4)]
    lane_pad = lambda a: jnp.pad(row(a), ((0, 0), (8, LANES - 16)))
    gdn_pre_ps = cgq + [lane_pad(W["a_log"]), lane_pad(W["dt_bias"]), seg, _chunk_tri(tt_p, CHUNK)]
    gdn_pre_ts = [(p, WIDTH, ci_, (0, 1, 2, 3)) for ci_ in _COL_QKV] + [(p, LANES, _COL_BA, None)]
    q_, k_, v_, beta_t, gc_t = _stage_fwd("gdn_pre", _fn_gdn_pre, gdn_pre_ts, gdn_pre_ps,
                                          [(WIDTH, f32)] * 3 + [(LANES, f32)] * 2, tt_p)
    o_, gdn_hist, gdn_inv, wbg_s, wbr_s, wout_s, wfi_s, wfo_s = _scan_fwd(
        "gdn_scan", _gdn_chunk, [q_, k_, v_], [beta_t, gc_t], job=later_weights)
    wout_f = wout_s.reshape(D, D)
    wfo = wfo_s.reshape(D_FF, D)
    ow512 = jnp.tile(row(W["onorm_gdn"]), (1, HEADS))
    gdn_post_ts = [_whole(o_), (p, WIDTH, _COL_Z, None)]
    ya = _stage_fwd("gdn_post", _fn_gdn_post, gdn_post_ts, [ow512, seg], [(WIDTH, bf16)], tt_l)[0]

    mu = sq(W["mu_rwkv"])
    rw_ps = [mu[:, 0:512], mu[:, 512:1024], mu[:, 1024:1536], mu[:, 1536:1664], jnp.pad(mu[:, 1664:1824], ((0, 0), (0, 96))),
             sq(W["w0"]), w2p, sq(W["a0"]), a2p, g2p, sq(W["k_k"]), sq(W["k_a"]), seg]
    rw_ts = [(p, WIDTH, ci_, (0, 1)) for ci_ in _COL_RKV] + [(p, LANES, _COL_LORA, (0, 1)),
                                                              (p, 256, _COL_GATE_LORA, (0, 1))]
    rw_out = _stage_fwd("rwkv_pre", _fn_rwkv_pre, rw_ts, rw_ps, [(WIDTH, f32)] * 7, tt_p)
    r_, lw_, k2_, vv_, na_, b_, g_ = rw_out
    rw_ins = [r_, lw_, k2_, vv_, na_, b_]
    y_, rw_hist, rw_inv = _scan_fwd("rwkv_scan", _rwkv_chunk, rw_ins)
    rwp_ps = [sq(W["lnx_w"]), sq(W["lnx_b"]), row(W["r_k"]), seg]
    rwp_ts = [_whole(y_), _whole(r_), _whole(k2_), _whole(vv_), _whole(g_)]
    yb = _stage_fwd("rwkv_post", _fn_rwkv_post, rwp_ts, rwp_ps, [(WIDTH, bf16)], tt_l)[0]

    big_a = _matmul("branch_gdn", ya, wbg_s, "nn", shards=4)
    big_b = _matmul("branch_rwkv", yb, wbr_s, "nn", shards=4)
    merge_ts = [(p, D, _COL_GL[0], None), (p, D, _COL_GL[1], None), _whole(big_a), _whole(big_b)]
    merged = _stage_fwd("merge", _fn_merge, merge_ts, [], [(D, bf16)], tt_l)[0]
    mo = _matmul("out_proj", merged, wout_f, "nn")
    norm2 = [gate1, sq(W["norm2_w"]), shift2, scale2]
    x1, h2 = _stage_fwd("resid_norm_mod2", _fn_resid_norm_mod, [_whole(x2), _whole(mo)], norm2, [(D, f32), (D, bf16)], tt_l)
    f = _matmul("ffn_in", h2, wfi_s, "nn", shards=4, tn=1408)
    cg_ps = [row(conv_ffn_f[j]) for j in range(3)]
    cg_ts = [(f, D_FF, 0, (0, 1, 2)), (f, D_FF, 1, None)]
    act = _stage_fwd("convglu", _fn_convglu, cg_ts, cg_ps, [(D_FF, bf16)], tt_h)[0]
    fo = _matmul("ffn_out", act, wfo, "nn", tk=D_FF)

    dx1_a, dfo, dgate2, dnormf, loss_part = _final_stage(x1, fo, tgt, gate2, row(W["norm_f_w"]), tt_l)

    dact = _matmul("d_act", dfo, wfo, "nt", tn=D_FF)
    g_wfo = _matmul("g_ffn_out", act, dfo, "tn", tm=1408, tk=2048)
    _, dcf, df = _stage_bwd("convglu_bwd", _fn_convglu, cg_ts, cg_ps, [[_whole(dact)]], tt_h, [True] * 2, [True] * 3,
                            joint=([0, 1], 2 * D_FF, 0, None), dtypes=[bf16])
    g_wfi = _matmul("g_ffn_in", h2, df, "tn", shards=4, tn=1408, tk=2048)
    g_wfo_s = g_wfo.reshape(4, D_FF // 4, D)
    dh2, recv_wfi, recv_wfo = _matmul("d_h2", df, wfi_s, "nt", shards=4, tk=1408, job=_sibling_swap_job([g_wfi, g_wfo_s]))
    (dx_a, dmo), (dgate1, dnorm2, dshift2, dscale2), _ = _stage_bwd(
        "resid_norm_mod2_bwd", _fn_resid_norm_mod, [_whole(x2), _whole(mo)], norm2,
        [[_whole(dx1_a)], [_whole(dh2)]], tt_l, [True, True], [True] * 4, dtypes=[f32, bf16])
    dmerged = _matmul("d_merged", dmo, wout_f, "nt")
    g_wout = _matmul("g_out_proj", merged, dmo, "tn", tk=T)
    (dbig_a, dbig_b), _, dp = _stage_bwd("merge_bwd", _fn_merge, merge_ts, [], [[_whole(dmerged)]], tt_l, [True] * 4, [],
                                         joint=([0, 1], p.shape[1], _JOINT_GL, None), dtypes=[bf16] * 3)
    g_wbg = _matmul("g_branch_gdn", ya, dbig_a, "tn", shards=4, tk=T)
    g_wbr = _matmul("g_branch_rwkv", yb, dbig_b, "tn", shards=4, tk=T)
    g_wout_s = g_wout.reshape(4, D // 4, D)
    dyb = _matmul("d_yb", dbig_b, wbr_s, "nt", shards=4)
    dya, *recv_mix = _matmul("d_ya", dbig_a, wbg_s, "nt", shards=4, job=_sibling_swap_job([g_wbg, g_wbr, g_wout_s]))

    (dy_, dr_p, dk2_p, dv_p, dg_p), (dlnxw, dlnxb, drk), _ = _stage_bwd(
        "rwkv_post_bwd", _fn_rwkv_post, rwp_ts, rwp_ps, [[_whole(dyb)]], tt_l, [True] * 5, [True, True, True, False])
    gs_a = [g_wbg, g_wbr, g_wout_s, g_wfi, g_wfo_s]
    pairs_a = [_add_half("grads_pair_sum%d" % (t + 1), g, r_, chip, ci)
               for t, (g, r_) in enumerate(zip(gs_a, recv_mix + [recv_wfi, recv_wfo]))]
    dr_c, dlw_c, dk2_c, dv_c, dna_c, db_c, *others_a = _scan_bwd(
        "rwkv_scan_bwd", _rwkv_chunk, rw_ins, [], rw_hist, rw_inv, dy_, job=_chip_exchange_job([pb for _, pb in pairs_a]))
    rw_cots = [[_whole(dr_p), _whole(dr_c)], [_whole(dlw_c)], [_whole(dk2_p), _whole(dk2_c)],
               [_whole(dv_p), _whole(dv_c)], [_whole(dna_c)], [_whole(db_c)], [_whole(dg_p)]]
    (dl_, dg_), rw_dp, dp = _stage_bwd("rwkv_pre_bwd", _fn_rwkv_pre, rw_ts, rw_ps, rw_cots, tt_p, [True] * 5,
                                       [True] * 12 + [False], joint=([0, 1, 2], p.shape[1], _JOINT_RKV, dp), dtypes=[bf16] * 3)
    dmu_r, dmu_k, dmu_v, dmu_l, dmu_g, dw0, dw2p, da0, da2p, dg2p, dkk, dka = rw_dp

    (do_,), (dow512,), dp = _stage_bwd("gdn_post_bwd", _fn_gdn_post, gdn_post_ts, [ow512, seg], [[_whole(dya)]], tt_l,
                                       [True, True], [True, False], joint=([1], p.shape[1], _JOINT_Z, dp), dtypes=[f32, bf16])
    d_gdn = _scan_bwd("gdn_scan_bwd", _gdn_chunk, [q_, k_, v_], [beta_t, gc_t], gdn_hist, gdn_inv, do_)
    gdn_cots = [[_whole(a)] for a in d_gdn]
    (dba,), gdn_dp, dp = _stage_bwd("gdn_pre_bwd", _fn_gdn_pre, gdn_pre_ts, gdn_pre_ps, gdn_cots, tt_p, [True] * 4,
                                    [True] * 14 + [False, False], joint=([0, 1, 2], p.shape[1], _JOINT_QKV, dp), dtypes=[bf16] * 2)
    dp = lax.dynamic_update_slice(dp, jnp.concatenate([dl_, dba, dg_], axis=1), (0, _SMALL_BLOCKS_AT))
    g_conv_gdn = jnp.concatenate([jnp.concatenate([gdn_dp[4 * part + j] for part in range(3)], axis=1) for j in range(4)], axis=0)
    g_conv_ffn = jnp.concatenate(dcf, axis=0)
    g_mu = jnp.concatenate([dmu_r, dmu_k, dmu_v, dmu_l, dmu_g[:, :160]], axis=1)
    late_zero = jnp.zeros((1, D), f32)
    dmod_early = jnp.concatenate([late_zero, late_zero, dgate1, dshift2, dscale2, dgate2], axis=1)
    small_parts = {"b_ada": dmod_early, "norm1_w": late_zero, "a_log": gdn_dp[12][:, 8:16], "dt_bias": gdn_dp[13][:, 8:16],
                   "mu_rwkv": g_mu, "w0": dw0, "a0": da0, "k_k": dkk, "k_a": dka, "r_k": drk, "lnx_w": dlnxw,
                   "lnx_b": dlnxb, "norm2_w": dnorm2, "norm_f_w": dnormf}
    small_names = [n for n in _SMALL if n != "onorm_gdn"]
    mid_full = [g_conv_gdn, g_conv_ffn, dw2p[0:64], da2p[64:128], dg2p[0:160]]
    body_rows = _pack([small_parts[n] for n in small_names] + [loss_part[:, 0:1]] + mid_full, 1, f32)
    head_row = body_rows.shape[0]
    small_g = jnp.concatenate([body_rows, jnp.pad(dow512, ((0, 0), (0, PACK_W - WIDTH)))], axis=0)
    small_g = jnp.pad(small_g, ((0, -small_g.shape[0] % 8), (0, 0)))

    g_win_pad, small_all_g = _matmul("g_in_proj", h1, dp, "tn", tk=T, job=_ag8_job(small_g))
    g_win_s = _win_unpad_shards(g_win_pad, shapes["w_in"][2])
    pair_win = _add_half("grads_pair_sum0", g_win_s, _run_job("w_in_grads_sibling_swap", _sibling_swap_job([g_win_s]))[0],
                         chip, ci)
    dh1, others_win = _matmul("d_h1", dp, win_p, "nt", tk=3072, job=_chip_exchange_job([pair_win[1]]))
    (grad_x,), (dnorm1, dshift1, dscale1), _ = _stage_bwd("norm_mod1_bwd", _fn_norm_mod_and_x, [_whole(x2)], norm1,
                                                          [[_whole(dh1)], [_whole(dx_a)]], tt_l, [True], [True] * 3)
    late_all = _ag8("gather_late_grads", jnp.pad(jnp.concatenate([dshift1, dscale1, dnorm1], axis=0), ((0, 5), (0, 0))))
    small_sum, head_sum = _sum_devices(small_all_g, late_all, (0, 1, 6), head_row)
    small_shapes = [shapes[n][1:] if n != "norm_f_w" else shapes[n] for n in small_names]
    un = _unpack(small_sum, small_shapes + [(1,)] + [g.shape for g in mid_full])
    small_grads = dict(zip(small_names, un))
    loss = un[len(small_names)].reshape(())
    small_grads["onorm_gdn"] = head_sum[0, 0:HEAD_DIM]
    for n, g in zip(_MID, un[len(small_names) + 1:]):
        wcols = shapes[n][2]
        small_grads[n] = lax.dynamic_slice(g, (0, chip * wcols), (g.shape[0], wcols))

    dmod_all = jnp.concatenate([late_all[:, 0, :], late_all[:, 1, :], small_all_g[:, 2:6, :].reshape(8, 4 * PACK_W)], axis=1)
    dmod_cols = lax.dynamic_slice(dmod_all, (0, chip * ncol), (8, ncol))
    g_wada = _matmul("g_w_ada", cond16, jnp.pad(dmod_cols, ((0, 8), (0, 0))), "tn")

    pairs = [pair_win] + pairs_a
    others = [others_win] + others_a
    halves = [_sum_chip("grads_chip_sum%d" % t, own, o_, ci) for t, ((own, _), o_) in enumerate(zip(pairs, others))]
    big_grads = dict(zip(_BIG, _rs_sibling_join(halves)))

    res = {tag: {} for tag in ("grad", "delta", "new_m", "new_v")}

    def put(n, g, d, m_, v_):
        for tag, val in zip(("grad", "delta", "new_m", "new_v"), (g, d, m_, v_)):
            res[tag][n] = val.reshape(shapes[n])

    for n in _BIG:
        put(n, big_grads[n], *_adamw("adamw_" + n, sq(W[n]), big_grads[n], sq(Mo[n]), sq(Vo[n])))
    put("w_ada", g_wada, *_adamw("adamw_w_ada", sq(W["w_ada"]), g_wada, sq(Mo["w_ada"]), sq(Vo["w_ada"])))
    rest = _SMALL + _MID
    pk = lambda d: _pack([d[n] for n in rest], 8, f32)
    sg = pk(small_grads)
    sm = _adamw("adamw_small", pk(W), sg, pk(Mo), pk(Vo))
    for tag, buf in zip(("grad", "delta", "new_m", "new_v"), (sg,) + tuple(sm)):
        res[tag].update(zip(rest, _unpack(buf, [shapes[n] for n in rest])))
    outs = [loss, grad_x.reshape(x.shape)]
    for tag in ("grad", "delta", "new_m", "new_v"):
        outs += [res[tag][n] for n in _ORDER]
    return tuple(outs)
```

```python
import numpy as np
import jax
import jax.numpy as jnp
from jax import lax
from jax.experimental import pallas as pl
from jax.experimental.pallas import tpu as pltpu

f32 = jnp.float32
bf16 = jnp.bfloat16

LANES = 128
HEADS = 8
HEAD_DIM = 64
WIDTH = HEADS * HEAD_DIM
CHUNK = 64
D_FF = 2816
NORM_EPS = 1e-6
LNX_EPS = 64e-5
PACK_W = 1024
MESH_ID = pl.DeviceIdType.MESH

ADAM_LR, ADAM_B1, ADAM_B2, ADAM_EPS, ADAM_WD, ADAM_STEP = 0.001, 0.9, 0.999, 1e-08, 0.01, 10


def _pick(n, target, mult):
    if n <= target:
        return n
    best = None
    for t in range(mult, target + 1, mult):
        if n % t == 0:
            best = t
    assert best is not None, (n, target, mult)
    return best


def _split_bf16(x, n):
    parts, r = [], x
    for i in range(n):
        p = r.astype(bf16)
        parts.append(p)
        if i + 1 < n:
            r = r - p.astype(f32)
    return parts


def _xdot_r_impl(x, m, n, dims):
    acc = None
    for p in _split_bf16(x, n):
        t = lax.dot_general(p, m, dims, preferred_element_type=f32)
        acc = t if acc is None else acc + t
    return acc


def _make_xdot_r(n):
    nn = (((1,), (0,)), ((), ()))
    nt = (((1,), (1,)), ((), ()))

    @jax.custom_vjp
    def xdot(x, m):
        return _xdot_r_impl(x, m, n, nn)

    def fwd(x, m):
        return _xdot_r_impl(x, m, n, nn), m

    def bwd(m, ct):
        return _xdot_r_impl(ct, m, n, nt), jnp.zeros_like(m)

    xdot.defvjp(fwd, bwd)
    return xdot


_segsum = _make_xdot_r(2)


def _xdot_l_impl(m, x, n, dims):
    acc = None
    for p in _split_bf16(x, n):
        t = lax.dot_general(m, p, dims, preferred_element_type=f32)
        acc = t if acc is None else acc + t
    return acc


@jax.custom_vjp
def _xdot_l(m, x):
    return _xdot_l_impl(m, x, 3, (((1,), (0,)), ((), ())))


def _xdot_l_fwd(m, x):
    return _xdot_l(m, x), m


def _xdot_l_bwd(m, ct):
    return jnp.zeros_like(m), _xdot_l_impl(m, ct, 3, (((0,), (0,)), ((), ())))


_xdot_l.defvjp(_xdot_l_fwd, _xdot_l_bwd)


@jax.custom_vjp
def _bdot(x, w):
    return jnp.dot(x.astype(bf16), w.astype(bf16), preferred_element_type=f32)


def _bdot_fwd(x, w):
    return _bdot(x, w), (x, w)


def _bdot_bwd(res, ct):
    x, w = res
    c = ct.astype(bf16)
    dx = lax.dot_general(c, w.astype(bf16), (((1,), (1,)), ((), ())), preferred_element_type=f32)
    dw = lax.dot_general(x.astype(bf16), c, (((0,), (0,)), ((), ())), preferred_element_type=f32)
    return dx, dw


_bdot.defvjp(_bdot_fwd, _bdot_bwd)


def _silu(x):
    return x * jax.nn.sigmoid(x)


def _softplus(x):
    return jnp.maximum(x, 0.0) + jnp.log(1.0 + jnp.exp(-jnp.abs(x)))


def _rms(x, w, eps):
    return x * lax.rsqrt(jnp.mean(x * x, axis=-1, keepdims=True) + eps) * w


def _seg_matrix(width, seg):
    i = np.arange(width)
    return jnp.asarray((i[:, None] // seg) == (i[None, :] // seg), dtype=bf16)


def _chunk_tri(rows, chunk):
    i = np.arange(rows)
    return jnp.asarray(((i[:, None] // chunk) == (i[None, :] // chunk)) & (i[:, None] >= i[None, :]), dtype=bf16)


HALO = 8


def _full_spec(shape):
    nd = len(shape)
    return pl.BlockSpec(shape, lambda i: (0,) * nd)


def _entry_specs(entries, tt, block_of):
    specs, ops = [], []
    for arr, w, ci, shifts in entries:
        specs.append(pl.BlockSpec((tt, w), lambda i, ci=ci: (block_of(i), ci)))
        ops.append(arr)
        if shifts:
            specs.append(pl.BlockSpec((HALO, w), lambda i, ci=ci: (jnp.maximum(block_of(i) * (tt // HALO) - 1, 0), ci)))
            ops.append(arr)
    return specs, ops


def _load_entries(entries, refs, first):
    tiles, k = [], 0
    for _, w, _, shifts in entries:
        x = refs[k][...].astype(f32)
        k += 1
        if not shifts:
            tiles.append(x)
            continue
        halo = jnp.where(first, 0.0, refs[k][...].astype(f32))
        k += 1
        row = lax.broadcasted_iota(jnp.int32, (HALO, w), 0)
        for s in shifts:
            if s == 0:
                tiles.append(x)
                continue
            r = pltpu.roll(x, s, 0)
            head = jnp.where(row < s, pltpu.roll(halo, s, 0), r[0:HALO])
            tiles.append(jnp.concatenate([head, r[HALO:]], axis=0))
    return tiles


def _unshift_sum(grads, shifts, carry, tt):
    w = grads[0].shape[1]
    row = lax.broadcasted_iota(jnp.int32, (tt, w), 0)
    row8 = lax.broadcasted_iota(jnp.int32, (HALO, w), 0)
    dx, out = None, jnp.zeros((HALO, w), f32)
    for d, s in zip(grads, shifts):
        if s == 0:
            part = d
        else:
            part = jnp.where(row < tt - s, pltpu.roll(d, tt - s, 0), 0.0)
            out = out + jnp.where(row8 >= HALO - s, pltpu.roll(d[0:HALO], HALO - s, 0), 0.0)
        dx = part if dx is None else dx + part
    return jnp.concatenate([dx[:tt - HALO], dx[tt - HALO:] + carry], axis=0), out


def _stage_fwd(name, fn, tiles, params, outs, tt):
    rows = tiles[0][0].shape[0]
    npar = len(params)
    specs, ops = _entry_specs(tiles, tt, lambda i: i)
    nin = len(ops)

    def body(*refs):
        ts = _load_entries(tiles, refs[:nin], pl.program_id(0) == 0)
        ps = [r[...] for r in refs[nin:nin + npar]]
        res = fn(ps, ts)
        for r, v in zip(refs[nin + npar:], res):
            r[...] = v.astype(r.dtype)

    return pl.pallas_call(
        body, grid=(rows // tt,),
        in_specs=specs + [_full_spec(p.shape) for p in params],
        out_specs=[pl.BlockSpec((tt, w), lambda i: (i, 0)) for (w, _) in outs],
        out_shape=[jax.ShapeDtypeStruct((rows, w), dt) for (w, dt) in outs],
        compiler_params=pltpu.CompilerParams(dimension_semantics=("parallel",)),
        name=name,
    )(*ops, *params)


def _stage_bwd(name, fn, tiles, params, cots, tt, tile_grad, param_grad, joint=None, dtypes=None):
    rows = tiles[0][0].shape[0]
    nblk = rows // tt
    npar = len(params)
    block_of = lambda i: nblk - 1 - i
    specs, ops = _entry_specs(tiles, tt, block_of)
    nin = len(ops)
    flat_cots = [c for group in cots for c in group]
    groups = [len(g) for g in cots]
    ncot = len(flat_cots)
    counts = [len(e[3]) if e[3] else 1 for e in tiles]
    dt_entries = [e for e, g in zip(tiles, tile_grad) if g]
    dp_shapes = [p.shape for p, g in zip(params, param_grad) if g]
    ndt = len(dt_entries)
    carry_w = [e[1] for e in dt_entries if e[3]]
    flags = [g for g, n in zip(tile_grad, counts) for _ in range(n)]
    members, j_width, j_cidx, j_buf = joint if joint else ([], 0, 0, None)
    solo = [k for k in range(ndt) if k not in members]
    nsolo, njoint, nbuf = len(solo), int(bool(members)), int(j_buf is not None)
    j_block = sum(dt_entries[k][1] for k in members)
    dtypes = list(dtypes) if dtypes else [f32] * (nsolo + njoint)

    def body(*refs):
        i = pl.program_id(0)
        p_refs = refs[nin:nin + npar]
        c_refs = refs[nin + npar:nin + npar + ncot]
        base = nin + npar + ncot + nbuf
        dt_refs = refs[base:base + nsolo]
        joint_refs = refs[base + nsolo:base + nsolo + njoint]
        dp_refs = refs[base + nsolo + njoint:base + nsolo + njoint + len(dp_shapes)]
        carry_refs = refs[base + nsolo + njoint + len(dp_shapes):]
        ts = _load_entries(tiles, refs[:nin], block_of(i) == 0)
        ps = [r[...] for r in p_refs]

        def f(dp, dt):
            dp, dt = iter(dp), iter(dt)
            pp = [next(dp) if g else p for p, g in zip(ps, param_grad)]
            tl = [next(dt) if g else t for t, g in zip(ts, flags)]
            return fn(pp, tl)

        _, vjp = jax.vjp(f, [p for p, g in zip(ps, param_grad) if g], [t for t, g in zip(ts, flags) if g])
        cs, j = [], 0
        for n in groups:
            acc = c_refs[j][...].astype(f32)
            for q in range(1, n):
                acc = acc + c_refs[j + q][...].astype(f32)
            cs.append(acc)
            j += n
        gp, gt = vjp(cs)

        @pl.when(i == 0)
        def _():
            for r in dp_refs:
                r[...] = jnp.zeros_like(r)
            for r in carry_refs:
                r[...] = jnp.zeros_like(r)

        gt, k, kc, dxs = list(gt), 0, 0, []
        for e, n in zip(dt_entries, [n for n, g in zip(counts, tile_grad) if g]):
            if e[3]:
                dx, out = _unshift_sum(gt[k:k + n], e[3], carry_refs[kc][...], tt)
                carry_refs[kc][...] = out
                kc += 1
            else:
                dx = gt[k]
            dxs.append(dx)
            k += n
        for r, k in zip(dt_refs, solo):
            r[...] = dxs[k].astype(r.dtype)
        off = 0
        for k in members:
            w = dt_entries[k][1]
            joint_refs[0][:, off:off + w] = dxs[k].astype(joint_refs[0].dtype)
            off += w
        for r, v in zip(dp_refs, gp):
            r[...] += v

    res = pl.pallas_call(
        body, grid=(nblk,),
        in_specs=specs + [_full_spec(p.shape) for p in params]
        + [pl.BlockSpec((tt, w), lambda i, ci=ci: (block_of(i), ci)) for (_, w, ci, *_) in flat_cots]
        + [pl.BlockSpec(memory_space=pl.ANY)] * nbuf,
        out_specs=[pl.BlockSpec((tt, dt_entries[k][1]), lambda i: (block_of(i), 0)) for k in solo]
        + [pl.BlockSpec((tt, j_block), lambda i: (block_of(i), j_cidx))] * njoint
        + [_full_spec(s) for s in dp_shapes],
        out_shape=[jax.ShapeDtypeStruct((rows, dt_entries[k][1]), dt) for k, dt in zip(solo, dtypes)]
        + [jax.ShapeDtypeStruct((rows, j_width), dtypes[-1])] * njoint
        + [jax.ShapeDtypeStruct(s, f32) for s in dp_shapes],
        scratch_shapes=[pltpu.VMEM((HALO, w), f32) for w in carry_w],
        input_output_aliases={nin + npar + ncot: nsolo} if nbuf else {},
        compiler_params=pltpu.CompilerParams(dimension_semantics=("arbitrary",)),
        name=name,
    )(*ops, *params, *[c[0] for c in flat_cots], *([j_buf] if nbuf else []))
    res = list(res)
    return res[:nsolo], res[nsolo + njoint:], (res[nsolo] if njoint else None)


def _whole(a):
    return (a, a.shape[1], 0, None)


def _matmul(name, a, b, mode, out_dtype=f32, tm=1024, tn=1024, tk=1024, shards=1, job=None):
    S = shards
    if mode == "nn":
        M, K = a.shape
        w = b.shape[-1]
    elif mode == "nt":
        M = a.shape[0]
        if S > 1:
            _, N, w = b.shape
            K = S * w
        else:
            N, K = b.shape
            w = K
    else:
        K, M = a.shape
        w = b.shape[1] // S
    if mode != "nt":
        N = S * w
    tm = _pick(M, tm, LANES)
    if mode == "nt":
        tn = _pick(N, tn, LANES)
        tk = _pick(w, tk, LANES)
    else:
        tn = _pick(w, tn, LANES)
        tk = _pick(K, tk, LANES if mode == "nn" else 16)
    nk = K // tk
    nb = w // (tk if mode == "nt" else tn)
    if mode == "nn":
        a_spec = pl.BlockSpec((tm, tk), lambda i, j, k: (i, k))
        if S > 1:
            b_spec = pl.BlockSpec((1, tk, tn), lambda i, j, k: (j // nb, k, j % nb))
        else:
            b_spec = pl.BlockSpec((tk, tn), lambda i, j, k: (k, j))
        dims = (((1,), (0,)), ((), ()))
    elif mode == "nt":
        a_spec = pl.BlockSpec((tm, tk), lambda i, j, k: (i, k))
        if S > 1:
            b_spec = pl.BlockSpec((1, tn, tk), lambda i, j, k: (k // nb, j, k % nb))
        else:
            b_spec = pl.BlockSpec((tn, tk), lambda i, j, k: (j, k))
        dims = (((1,), (1,)), ((), ()))
    else:
        a_spec = pl.BlockSpec((tk, tm), lambda i, j, k: (k, i))
        b_spec = pl.BlockSpec((tk, tn), lambda i, j, k: (k, j))
        dims = (((0,), (0,)), ((), ()))
    if mode == "tn" and S > 1:
        o_spec = pl.BlockSpec((1, tm, tn), lambda i, j, k: (j // nb, i, j % nb))
        o_shape = (S, M, w)
    else:
        o_spec = pl.BlockSpec((tm, tn), lambda i, j, k: (i, j))
        o_shape = (M, N)
    b_lead = S > 1 and mode != "tn"
    o_lead = S > 1 and mode == "tn"

    j_ins, j_outs, j_sems = _job_parts(job)
    nji, njo = len(j_ins), len(j_outs)
    grid = (M // tm, N // tn, nk)

    def run_job(refs):
        step = (pl.program_id(0) * grid[1] + pl.program_id(1)) * grid[2] + pl.program_id(2)
        _job_steps(job, refs[2:2 + nji], refs[3 + nji:3 + nji + njo], refs[len(refs) - len(j_sems):], step,
                   grid[0] * grid[1] * grid[2] - 1)

    def body(*refs):
        a_ref, b_ref, o_ref, acc_ref = refs[0], refs[1], refs[2 + nji], refs[3 + nji + njo]
        run_job(refs)
        k = pl.program_id(2)

        @pl.when(k == 0)
        def _():
            acc_ref[...] = jnp.zeros_like(acc_ref)

        bv = b_ref[0] if b_lead else b_ref[...]
        acc_ref[...] += lax.dot_general(a_ref[...].astype(bf16), bv.astype(bf16), dims, preferred_element_type=f32)

        @pl.when(k == nk - 1)
        def _():
            if o_lead:
                o_ref[0] = acc_ref[...].astype(o_ref.dtype)
            else:
                o_ref[...] = acc_ref[...].astype(o_ref.dtype)

    def body_one_step(*refs):
        a_ref, b_ref, o_ref = refs[0], refs[1], refs[2 + nji]
        run_job(refs)
        bv = b_ref[0] if b_lead else b_ref[...]
        res = lax.dot_general(a_ref[...].astype(bf16), bv.astype(bf16), dims, preferred_element_type=f32)
        if o_lead:
            o_ref[0] = res.astype(o_ref.dtype)
        else:
            o_ref[...] = res.astype(o_ref.dtype)

    res = pl.pallas_call(
        body if nk > 1 else body_one_step, grid=grid,
        in_specs=[a_spec, b_spec] + [_ANY] * nji,
        out_specs=[o_spec] + [_ANY] * njo,
        out_shape=[jax.ShapeDtypeStruct(o_shape, out_dtype)] + j_outs,
        scratch_shapes=([pltpu.VMEM((tm, tn), f32)] if nk > 1 else []) + j_sems,
        compiler_params=pltpu.CompilerParams(
            dimension_semantics=("arbitrary",) * 3 if job else ("parallel", "parallel", "arbitrary")),
        name=name,
    )(a, b, *j_ins)
    return res if job else res[0]


def _make_bmm(precision):
    if precision is None:
        cast, kw = (lambda v: v.astype(bf16)), {}
    else:
        cast, kw = (lambda v: v), {"precision": precision}

    def nn(a, b):
        return jnp.einsum("hij,hjk->hik", cast(a), cast(b), preferred_element_type=f32, **kw)

    def nt(a, b):
        return jnp.einsum("hik,hjk->hij", cast(a), cast(b), preferred_element_type=f32, **kw)

    def tn(a, b):
        return jnp.einsum("hki,hkj->hij", cast(a), cast(b), preferred_element_type=f32, **kw)

    if precision is not None:
        return nn, nt, tn
    nn_v, nt_v, tn_v = jax.custom_vjp(nn), jax.custom_vjp(nt), jax.custom_vjp(tn)
    keep = lambda f: (lambda a, b: (f(a, b), (a, b)))
    nn_v.defvjp(keep(nn), lambda r, ct: (nt(ct, r[1]), tn(r[0], ct)))
    nt_v.defvjp(keep(nt), lambda r, ct: (nn(ct, r[1]), tn(ct, r[0])))
    tn_v.defvjp(keep(tn), lambda r, ct: (nt(r[1], ct), nn(r[0], ct)))
    return nn_v, nt_v, tn_v


_bmm, _bmm_nt, _bmm_tn = _make_bmm(None)
_bmm_exact = _make_bmm(lax.Precision.HIGH)[0]


def _masks(n):
    r = lax.broadcasted_iota(jnp.int32, (n, n), 0)
    c = lax.broadcasted_iota(jnp.int32, (n, n), 1)
    return (r >= c)[None], (r > c)[None], (r == c)[None]


_INV_BLOCK = 8


def _nilpotent_inverse(m, eye):
    p = eye + m
    for _ in range(2):
        m = _bmm(m, m)
        p = p + _bmm(p, m)
    return p


def _neumann_inverse_impl(m):
    n = m.shape[1]
    assert n == _INV_BLOCK * _INV_BLOCK
    r = lax.broadcasted_iota(jnp.int32, (n, n), 0)
    c = lax.broadcasted_iota(jnp.int32, (n, n), 1)
    eye = (r == c).astype(f32)[None]
    inside = jnp.where((r // _INV_BLOCK == c // _INV_BLOCK)[None], m, 0.0)
    d_inv = _nilpotent_inverse(inside, eye)
    return _bmm(_nilpotent_inverse(_bmm(d_inv, m - inside), eye), d_inv)


@jax.custom_vjp
def _neumann_inverse(m):
    return _neumann_inverse_impl(m)


def _neumann_inverse_fwd(m):
    p = _neumann_inverse_impl(m)
    return p, p


def _neumann_inverse_bwd(p, ct):
    return (_bmm_tn(p, _bmm_nt(ct, p)),)


_neumann_inverse.defvjp(_neumann_inverse_fwd, _neumann_inverse_bwd)


@jax.custom_vjp
def _given_inverse(m, p):
    return p


_given_inverse.defvjp(lambda m, p: (p, p), lambda p, ct: (_neumann_inverse_bwd(p, ct)[0], jnp.zeros_like(p)))


def _gdn_chunk(s, q, k, v, beta, gc, gr, gl, p=None):
    n = q.shape[1]
    causal, strict, _ = _masks(n)
    decay = jnp.where(causal, jnp.exp(jnp.where(causal, gc - gr, 0.0)), 0.0)
    kb = k * beta
    vb = v * beta
    lower = jnp.where(strict, _bmm_nt(kb, k) * decay, 0.0)
    t_mat = _neumann_inverse(-lower) if p is None else _given_inverse(-lower, p)
    egc = jnp.exp(gc)
    u = _bmm(t_mat, vb)
    w = _bmm(t_mat, kb * egc)
    attn = jnp.where(causal, _bmm_nt(q, k) * decay, 0.0)
    v_new = u - _bmm(w, s)
    o = _bmm(q * egc, s) + _bmm(attn, v_new)
    k_dec = k * jnp.exp(gl - gc)
    s_new = s * jnp.exp(gl) + _bmm_tn(k_dec, v_new)
    return s_new, o, t_mat


def _rwkv_chunk(s, r, lw, k, v, a, b, p=None):
    n = r.shape[1]
    causal, strict, _ = _masks(n)
    tri = jnp.broadcast_to(causal.astype(f32), (r.shape[0], n, n))
    lc = _bmm_exact(tri, lw)
    ein = jnp.exp(lc)
    eout = jnp.exp(-lc)
    a_t = a * jnp.exp(lc - lw)
    b_t = b * eout
    k_t = k * eout
    r_t = r * ein
    a_ab = jnp.where(strict, _bmm_nt(a_t, b_t), 0.0)
    a_ak = jnp.where(strict, _bmm_nt(a_t, k_t), 0.0)
    inv = _neumann_inverse(a_ab) if p is None else _given_inverse(a_ab, p)
    u = _bmm(inv, _bmm_nt(a_t, s) + _bmm(a_ak, v))
    y = (_bmm_nt(r_t, s) + _bmm(jnp.where(causal, _bmm_nt(r_t, b_t), 0.0), u)
         + _bmm(jnp.where(causal, _bmm_nt(r_t, k_t), 0.0), v))
    e_last = jnp.exp(jnp.sum(lw, axis=1, keepdims=True))
    s_new = s * e_last + _bmm_tn(u, b_t * e_last) + _bmm_tn(v, k_t * e_last)
    return s_new, y, inv


def _heads_in(ref, rows):
    return jnp.stack([ref[rows, h * HEAD_DIM:(h + 1) * HEAD_DIM] for h in range(HEADS)], axis=0)


def _heads_out(ref, rows, val):
    for h in range(HEADS):
        ref[rows, h * HEAD_DIM:(h + 1) * HEAD_DIM] = val[h]


def _gdn_scalars(bt, gt):
    n = bt.shape[0]
    gtt = gt.T
    hs = range(HEADS)
    return [jnp.stack([bt[:, h:h + 1] for h in hs], axis=0),
            jnp.stack([gt[:, HEADS + h:HEADS + h + 1] for h in hs], axis=0),
            jnp.stack([gtt[HEADS + h:HEADS + h + 1, :] for h in hs], axis=0),
            jnp.stack([gt[n - 1:n, HEADS + h:HEADS + h + 1] for h in hs], axis=0)]


def _gdn_scalars_back(dbeta, dgc, dgr, dgl):
    n = dbeta.shape[1]
    lane = lax.broadcasted_iota(jnp.int32, (n, LANES), 1)
    row = lax.broadcasted_iota(jnp.int32, (n, LANES), 0)
    sub = lax.broadcasted_iota(jnp.int32, (LANES, n), 0)
    db = jnp.zeros((n, LANES), f32)
    dg = jnp.zeros((n, LANES), f32)
    dgt = jnp.zeros((LANES, n), f32)
    for h in range(HEADS):
        db = jnp.where(lane == h, dbeta[h], db)
        dg = jnp.where(lane == HEADS + h, dgc[h] + jnp.where(row == n - 1, dgl[h], 0.0), dg)
        dgt = jnp.where(sub == HEADS + h, dgr[h], dgt)
    return [db, dg + dgt.T]


SCAN_GROUP = 2


def _scan_steps(t):
    g = SCAN_GROUP if (t // CHUNK) % SCAN_GROUP == 0 else 1
    return g, t // (CHUNK * g)


def _scan_spec(width, g, n, reverse):
    if reverse:
        return pl.BlockSpec((g * CHUNK, width), lambda i: (n - 1 - i, 0))
    return pl.BlockSpec((g * CHUNK, width), lambda i: (i, 0))


def _hist_spec(g, n, reverse):
    blk = (g, HEADS, HEAD_DIM, HEAD_DIM)
    if reverse:
        return pl.BlockSpec(blk, lambda i: (n - 1 - i, 0, 0, 0))
    return pl.BlockSpec(blk, lambda i: (i, 0, 0, 0))


def _job_list(job):
    return [] if job is None else (list(job) if isinstance(job, (list, tuple)) else [job])


def _job_parts(job):
    jobs = _job_list(job)
    sems = [pltpu.SemaphoreType.DMA((j["nsem"],)) for j in jobs for _ in range(2)]
    return [a for j in jobs for a in j["ins"]], [o for j in jobs for o in j["out_shapes"]], sems


_JOB_PHASES = ("start", "forward", "finish")


def _job_phase(job, phase, in_refs, out_refs, sems):
    ki = ko = 0
    for n, j in enumerate(_job_list(job)):
        ni, no = len(j["ins"]), len(j["out_shapes"])
        if phase in j:
            j[phase](in_refs[ki:ki + ni], out_refs[ko:ko + no], sems[2 * n], sems[2 * n + 1])
        ki, ko = ki + ni, ko + no


def _job_steps(job, in_refs, out_refs, sems, step, last):
    if job is None:
        return
    for phase, at in zip(_JOB_PHASES, (0, max(last - 2, 0), last)):
        @pl.when(step == at)
        def _(phase=phase):
            _job_phase(job, phase, in_refs, out_refs, sems)


def _scan_fwd(name, fn, rows_in, scal_in=(), job=None):
    t = rows_in[0].shape[0]
    grp, n = _scan_steps(t)
    nr, ns = len(rows_in), len(scal_in)
    j_ins, j_outs, j_sems = _job_parts(job)
    nji, njo = len(j_ins), len(j_outs)

    def body(*refs):
        o_ref, sh_ref, ph_ref = refs[nr + ns + nji:nr + ns + nji + 3]
        s_scr = refs[nr + ns + nji + 3 + njo]
        _job_steps(job, refs[nr + ns:nr + ns + nji], refs[nr + ns + nji + 3:nr + ns + nji + 3 + njo],
                   refs[nr + ns + nji + 3 + njo + 1:], pl.program_id(0), n - 1)

        @pl.when(pl.program_id(0) == 0)
        def _():
            s_scr[...] = jnp.zeros_like(s_scr)

        s = s_scr[...]
        for sub in range(grp):
            rows = slice(sub * CHUNK, (sub + 1) * CHUNK)
            sh_ref[sub] = s
            ins = [_heads_in(r, rows) for r in refs[:nr]]
            if ns:
                ins += _gdn_scalars(*[r[rows, :] for r in refs[nr:nr + ns]])
            s, o, p = fn(s, *ins)
            _heads_out(o_ref, rows, o)
            ph_ref[sub] = p
        s_scr[...] = s

    return pl.pallas_call(
        body, grid=(n,),
        in_specs=[_scan_spec(a.shape[1], grp, n, False) for a in (*rows_in, *scal_in)] + [_ANY] * nji,
        out_specs=[_scan_spec(WIDTH, grp, n, False), _hist_spec(grp, n, False), _hist_spec(grp, n, False)] + [_ANY] * njo,
        out_shape=[jax.ShapeDtypeStruct((t, WIDTH), f32)]
        + [jax.ShapeDtypeStruct((t // CHUNK, HEADS, HEAD_DIM, HEAD_DIM), f32)] * 2 + j_outs,
        scratch_shapes=[pltpu.VMEM((HEADS, HEAD_DIM, HEAD_DIM), f32)] + j_sems,
        compiler_params=pltpu.CompilerParams(dimension_semantics=("arbitrary",)),
        name=name,
    )(*rows_in, *scal_in, *j_ins)


def _scan_bwd(name, fn, rows_in, scal_in, s_hist, p_hist, d_out, job=None):
    t = rows_in[0].shape[0]
    grp, n = _scan_steps(t)
    nr, ns = len(rows_in), len(scal_in)
    j_ins, j_outs, j_sems = _job_parts(job)
    nji, njo = len(j_ins), len(j_outs)

    def body(*refs):
        sh_ref, ph_ref, do_ref = refs[nr + ns:nr + ns + 3]
        base = nr + ns + 3 + nji
        g_refs = refs[base:base + nr + ns]
        ds_scr = refs[base + nr + ns + njo]
        _job_steps(job, refs[nr + ns + 3:base], refs[base + nr + ns:base + nr + ns + njo],
                   refs[base + nr + ns + njo + 1:], pl.program_id(0), n - 1)

        @pl.when(pl.program_id(0) == 0)
        def _():
            ds_scr[...] = jnp.zeros_like(ds_scr)

        ds = ds_scr[...]
        for sub in reversed(range(grp)):
            rows = slice(sub * CHUNK, (sub + 1) * CHUNK)
            ins = [_heads_in(r, rows) for r in refs[:nr]]
            if ns:
                ins += _gdn_scalars(*[r[rows, :] for r in refs[nr:nr + ns]])
            p = ph_ref[sub]
            _, vjp = jax.vjp(lambda s, *a, p=p: fn(s, *a, p=p)[:2], sh_ref[sub], *ins)
            g = vjp((ds, _heads_in(do_ref, rows)))
            ds = g[0]
            for r, v in zip(g_refs[:nr], g[1:1 + nr]):
                _heads_out(r, rows, v)
            if ns:
                for r, v in zip(g_refs[nr:], _gdn_scalars_back(*g[1 + nr:])):
                    r[rows, :] = v
        ds_scr[...] = ds

    arrs = (*rows_in, *scal_in)
    return pl.pallas_call(
        body, grid=(n,),
        in_specs=[_scan_spec(a.shape[1], grp, n, True) for a in arrs]
        + [_hist_spec(grp, n, True), _hist_spec(grp, n, True), _scan_spec(WIDTH, grp, n, True)] + [_ANY] * nji,
        out_specs=[_scan_spec(a.shape[1], grp, n, True) for a in arrs] + [_ANY] * njo,
        out_shape=[jax.ShapeDtypeStruct(a.shape, f32) for a in arrs] + j_outs,
        scratch_shapes=[pltpu.VMEM((HEADS, HEAD_DIM, HEAD_DIM), f32)] + j_sems,
        compiler_params=pltpu.CompilerParams(dimension_semantics=("arbitrary",)),
        name=name,
    )(*arrs, s_hist, p_hist, d_out, *j_ins)


def _fn_norm_mod(ps, ts):
    nw, shift, scale = ps
    (x,) = ts
    return [_rms(x, nw, NORM_EPS) * (1.0 + scale) + shift]


def _fn_norm_mod_and_x(ps, ts):
    return _fn_norm_mod(ps, ts) + [ts[0]]


def _fn_resid_norm_mod(ps, ts):
    gate, nw, shift, scale = ps
    x, mo = ts
    x1 = x + gate * mo
    return [x1, _rms(x1, nw, NORM_EPS) * (1.0 + scale) + shift]


def _fn_gdn_pre(ps, ts):
    cw = ps[:12]
    alog, dtb, seg, tri = ps[12:]
    ba = ts[12]
    outs = []
    for part in range(3):
        x = ts[4 * part:4 * part + 4]
        w = cw[4 * part:4 * part + 4]
        conv = w[3] * x[0] + w[2] * x[1] + w[1] * x[2] + w[0] * x[3]
        u = _silu(conv)
        if part < 2:
            u = u * lax.rsqrt(_segsum(u * u, seg) + 1e-6)
            if part == 0:
                u = u * (HEAD_DIM ** -0.5)
        outs.append(u)
    beta = jax.nn.sigmoid(ba)
    g = -jnp.exp(alog) * _softplus(ba + dtb)
    gc = _xdot_l(tri, g)
    return outs + [beta, gc]


def _fn_gdn_post(ps, ts):
    ow, seg = ps
    o, z = ts
    ms = _segsum(o * o, seg) * (1.0 / HEAD_DIM)
    return [o * lax.rsqrt(ms + NORM_EPS) * ow * _silu(z)]


def _fn_rwkv_pre(ps, ts):
    mu_r, mu_k, mu_v, mu_l, mu_g, w0, w2p, a0, a2p, g2p, k_k, k_a, seg = ps
    r0, r1, k0, k1, v0, v1, l0, l1, g0, g1 = ts
    xr = r0 + (r1 - r0) * mu_r
    xk = k0 + (k1 - k0) * mu_k
    xv = v0 + (v1 - v0) * mu_v
    xl = l0 + (l1 - l0) * mu_l
    xg = g0 + (g1 - g0) * mu_g
    w = -_softplus(-(w0 + _bdot(jnp.tanh(xl), w2p))) - 0.5
    lw = -jnp.exp(w)
    a = jax.nn.sigmoid(a0 + _bdot(xl, a2p))
    g = _bdot(jax.nn.sigmoid(xg), g2p)
    kk = xk * k_k
    kk = kk * lax.rsqrt(_segsum(kk * kk, seg) + 1e-6)
    k2 = xk * (1.0 + (a - 1.0) * k_a)
    return [xr, lw, k2, xv, -kk, kk * a, g]


def _fn_rwkv_post(ps, ts):
    lw_, lb_, rk, seg = ps
    y, r, k2, v, g = ts
    inv = 1.0 / HEAD_DIM
    yc = y - _segsum(y, seg) * inv
    var = _segsum(yc * yc, seg) * inv
    yn = yc * lax.rsqrt(var + LNX_EPS) * lw_ + lb_
    bonus = _segsum(r * k2 * rk, seg) * v
    return [(yn + bonus) * g]


def _fn_merge(ps, ts):
    gla, glb, ya, yb = ts
    return [jax.nn.sigmoid(gla) * ya + jax.nn.sigmoid(glb) * yb]


def _fn_convglu(ps, ts):
    c0, c1, c2 = ps
    g0, g1, g2, up = ts
    return [_silu(c2 * g0 + c1 * g1 + c0 * g2) * up]


def _final_stage(x1, fo, tgt, gate2, nfw, tt):
    rows, d = x1.shape

    def loss_fn(gate, nw, xa, fa, tg):
        y = _rms(xa + gate * fa, nw, NORM_EPS)
        err = (y - tg) ** 2
        return 0.5 * jnp.sum(jnp.mean(err, axis=-1, keepdims=True), axis=0, keepdims=True)

    def body(x_ref, f_ref, t_ref, g_ref, w_ref, dx_ref, df_ref, dg_ref, dw_ref, l_ref):
        i = pl.program_id(0)
        args = (g_ref[...], w_ref[...], x_ref[...], f_ref[...])
        tg = t_ref[...]
        lv, vjp = jax.vjp(lambda g, w, xa, fa: loss_fn(g, w, xa, fa, tg), *args)
        dg, dw, dx, df = vjp(jnp.ones((1, 1), f32))
        dx_ref[...] = dx
        df_ref[...] = df.astype(df_ref.dtype)

        @pl.when(i == 0)
        def _():
            dg_ref[...] = jnp.zeros_like(dg_ref)
            dw_ref[...] = jnp.zeros_like(dw_ref)
            l_ref[...] = jnp.zeros_like(l_ref)

        dg_ref[...] += dg
        dw_ref[...] += dw
        l_ref[...] += jnp.broadcast_to(lv, l_ref.shape)

    row = pl.BlockSpec((tt, d), lambda i: (i, 0))
    vec = pl.BlockSpec((1, d), lambda i: (0, 0))
    return pl.pallas_call(
        body, grid=(rows // tt,),
        in_specs=[row, row, row, vec, vec],
        out_specs=[row, row, vec, vec, pl.BlockSpec((1, LANES), lambda i: (0, 0))],
        out_shape=[jax.ShapeDtypeStruct((rows, d), f32), jax.ShapeDtypeStruct((rows, d), bf16)]
        + [jax.ShapeDtypeStruct((1, d), f32)] * 2
        + [jax.ShapeDtypeStruct((1, LANES), f32)],
        compiler_params=pltpu.CompilerParams(dimension_semantics=("arbitrary",)),
        name="loss_head",
    )(x1, fo, tgt, gate2, nfw)


def _ada_fwd(c_all, w_shard, b_cols):
    def body(c_ref, w_ref, b_ref, cond_ref, mod_ref):
        cond = _silu(c_ref[...])
        cond_ref[...] = cond
        mod_ref[...] = jnp.dot(cond.astype(bf16), w_ref[...].astype(bf16), preferred_element_type=f32) + b_ref[...]

    n = w_shard.shape[1]
    return pl.pallas_call(
        body, out_shape=[jax.ShapeDtypeStruct(c_all.shape, f32), jax.ShapeDtypeStruct((c_all.shape[0], n), f32)],
        name="ada_fwd",
    )(c_all, w_shard, b_cols)


def _adamw(name, w, g, m, v):
    rows, width = w.shape
    tt = _pick(rows, 256, 8)
    c1 = 1.0 - ADAM_B1 ** ADAM_STEP
    c2 = 1.0 - ADAM_B2 ** ADAM_STEP

    def body(w_ref, g_ref, m_ref, v_ref, d_ref, mo_ref, vo_ref):
        gg = g_ref[...]
        mn = ADAM_B1 * m_ref[...] + (1.0 - ADAM_B1) * gg
        vn = ADAM_B2 * v_ref[...] + (1.0 - ADAM_B2) * (gg * gg)
        m_hat = mn / c1
        v_hat = vn / c2
        d_ref[...] = -ADAM_LR * (m_hat / (jnp.sqrt(v_hat) + ADAM_EPS) + ADAM_WD * w_ref[...])
        mo_ref[...] = mn
        vo_ref[...] = vn

    spec = pl.BlockSpec((tt, width), lambda i: (i, 0))
    return pl.pallas_call(
        body, grid=(rows // tt,), in_specs=[spec] * 4, out_specs=[spec] * 3,
        out_shape=[jax.ShapeDtypeStruct((rows, width), f32)] * 3,
        compiler_params=pltpu.CompilerParams(dimension_semantics=("parallel",)),
        name=name,
    )(w, g, m, v)


def _place():
    return lax.axis_index("x"), lax.axis_index("y"), lax.axis_index("c")


def _ag8(name, blk):
    m, w = blk.shape

    def body(x_ref, out_ref, send_sems, recv_sems, local_sem):
        x, y, c = _place()
        me, sibling = (x, y, c), (x, y, 1 - c)
        chips = _other_chips(x, y)

        def slot(px, py, pc):
            return out_ref.at[4 * px + 2 * py + pc]

        def copy(k, block, to, src=None):
            return pltpu.make_async_remote_copy(src_ref=slot(*block) if src is None else src, dst_ref=slot(*block),
                                                send_sem=send_sems.at[k], recv_sem=recv_sems.at[k], device_id=to,
                                                device_id_type=MESH_ID)

        mine = pltpu.make_async_copy(x_ref, slot(*me), local_sem)
        mine.start()
        first = [copy(0, me, sibling, src=x_ref)] + [copy(1 + j, me, (*chip, c), src=x_ref) for j, chip in enumerate(chips)]
        for cp in first:
            cp.start()
        passed = [copy(4 + j, (*chip, c), sibling) for j, chip in enumerate(chips)]
        for j, chip in enumerate(chips):
            copy(1 + j, (*chip, c), me).wait_recv()
            passed[j].start()
        copy(0, sibling, me).wait_recv()
        for j, chip in enumerate(chips):
            copy(4 + j, (*chip, 1 - c), me).wait_recv()
        for cp in first + passed:
            cp.wait_send()
        mine.wait()

    return pl.pallas_call(
        body, out_shape=jax.ShapeDtypeStruct((8, m, w), blk.dtype),
        in_specs=[pl.BlockSpec(memory_space=pltpu.VMEM)], out_specs=pl.BlockSpec(memory_space=pltpu.VMEM),
        scratch_shapes=[pltpu.SemaphoreType.DMA((7,)), pltpu.SemaphoreType.DMA((7,)), pltpu.SemaphoreType.DMA],
        name=name,
    )(blk)


def _ag8_job(blk):
    def plan(x_refs, out_refs, send_sems, recv_sems):
        x_ref, out_ref = x_refs[0], out_refs[0]
        x, y, c = _place()
        me, sibling = (x, y, c), (x, y, 1 - c)
        chips = _other_chips(x, y)
        slot = lambda px, py, pc: out_ref.at[4 * px + 2 * py + pc]
        cp = lambda k, block, to, src=None: _rcopy(slot(*block) if src is None else src, slot(*block), send_sems, recv_sems, k, to)
        first = [cp(0, me, sibling, x_ref)] + [cp(1 + j, me, (*chip, c), x_ref) for j, chip in enumerate(chips)]
        landed = [cp(1 + j, (*chip, c), me) for j, chip in enumerate(chips)]
        passed = [cp(4 + j, (*chip, c), sibling) for j, chip in enumerate(chips)]
        handed = [cp(0, sibling, me)] + [cp(4 + j, (*chip, 1 - c), me) for j, chip in enumerate(chips)]
        mine = pltpu.make_async_copy(x_ref, slot(*me), send_sems.at[7])
        return first, landed, passed, handed, mine

    def start(*refs):
        first, _, _, _, mine = plan(*refs)
        mine.start()
        for cp in first:
            cp.start()

    def forward(*refs):
        _, landed, passed, _, _ = plan(*refs)
        for arrived, fw in zip(landed, passed):
            arrived.wait_recv()
            fw.start()

    def finish(*refs):
        first, _, passed, handed, mine = plan(*refs)
        for cp in handed:
            cp.wait_recv()
        for cp in first + passed:
            cp.wait_send()
        mine.wait()

    return dict(ins=[blk], out_shapes=[jax.ShapeDtypeStruct((8,) + blk.shape, blk.dtype)], nsem=8, start=start,
                forward=forward, finish=finish)


def _other_chips(x, y):
    return [(1 - x, y), (x, 1 - y), (1 - x, 1 - y)]


_ANY = pl.BlockSpec(memory_space=pl.ANY)


def _rcopy(src, dst, send_sems, recv_sems, k, dev):
    return pltpu.make_async_remote_copy(src_ref=src, dst_ref=dst, send_sem=send_sems.at[k], recv_sem=recv_sems.at[k],
                                        device_id=dev, device_id_type=MESH_ID)


def _run_job(name, job):
    j_ins, j_outs, j_sems = _job_parts(job)
    n = len(j_ins)

    def body(*refs):
        for phase in _JOB_PHASES:
            _job_phase(job, phase, refs[:n], refs[n:n + len(j_outs)], refs[n + len(j_outs):])

    return pl.pallas_call(body, out_shape=j_outs, in_specs=[_ANY] * n, out_specs=[_ANY] * len(j_outs),
                          scratch_shapes=j_sems, name=name)(*j_ins)


def _ag4_job(ws):
    n = len(ws)

    def plan(w_refs, out_refs, send_sems, recv_sems):
        x, y, c = _place()
        chip = 2 * x + y
        sibling = (x, y, 1 - c)
        chips = _other_chips(x, y)
        mine = [pl.ds(c * (w.shape[0] // 2), w.shape[0] // 2) for w in ws]
        other = [pl.ds((1 - c) * (w.shape[0] // 2), w.shape[0] // 2) for w in ws]
        rc = lambda src, dst, k, dev: _rcopy(src, dst, send_sems, recv_sems, k, dev)
        first = [rc(w_refs[t].at[mine[t]], out_refs[t].at[chip, mine[t]], 7 * t + k, (px, py, c))
                 for t in range(n) for k, (px, py) in enumerate(chips)]
        own = [rc(w_refs[t], out_refs[t].at[chip], 7 * t + 6, sibling) for t in range(n)]
        landed = [rc(out_refs[t].at[2 * px + py, mine[t]], out_refs[t].at[2 * px + py, mine[t]], 7 * t + k, (px, py, c))
                  for t in range(n) for k, (px, py) in enumerate(chips)]
        forward = [rc(out_refs[t].at[2 * px + py, mine[t]], out_refs[t].at[2 * px + py, mine[t]], 7 * t + 3 + k, sibling)
                   for t in range(n) for k, (px, py) in enumerate(chips)]
        handed = [rc(out_refs[t].at[2 * px + py, other[t]], out_refs[t].at[2 * px + py, other[t]], 7 * t + 3 + k, sibling)
                  for t in range(n) for k, (px, py) in enumerate(chips)]
        return first, own, landed, forward, handed

    def start(*refs):
        first, own, _, _, _ = plan(*refs)
        for cp in first + own:
            cp.start()

    def forward(*refs):
        _, _, landed, passed, _ = plan(*refs)
        for arrived, fw in zip(landed, passed):
            arrived.wait_recv()
            fw.start()

    def finish(*refs):
        first, own, _, passed, handed = plan(*refs)
        for cp in handed + own:
            cp.wait_recv()
        for cp in first + own + passed:
            cp.wait_send()

    return dict(ins=ws, out_shapes=[jax.ShapeDtypeStruct((4,) + w.shape, w.dtype) for w in ws], nsem=7 * n,
                start=start, forward=forward, finish=finish)


def _sibling_swap_job(gs):
    n = len(gs)

    def plan(g_refs, recv_refs, send_sems, recv_sems):
        x, y, c = _place()
        return [_rcopy(g_refs[t].at[s_, pl.ds((1 - c) * (gs[t].shape[1] // 2), gs[t].shape[1] // 2)], recv_refs[t].at[s_],
                       send_sems, recv_sems, 4 * t + s_, (x, y, 1 - c)) for t in range(n) for s_ in range(4)]

    def start(*refs):
        for cp in plan(*refs):
            cp.start()

    def finish(*refs):
        copies = plan(*refs)
        for cp in copies:
            cp.wait_recv()
        for cp in copies:
            cp.wait_send()

    return dict(ins=gs, out_shapes=[jax.ShapeDtypeStruct((4, g.shape[1] // 2, g.shape[2]), g.dtype) for g in gs],
                nsem=4 * n, start=start, finish=finish)


def _chip_exchange_job(ps):
    n = len(ps)

    def plan(p_refs, recv_refs, send_sems, recv_sems):
        x, y, c = _place()
        return [_rcopy(p_refs[t].at[2 * px + py], recv_refs[t].at[k], send_sems, recv_sems, 3 * t + k, (px, py, c))
                for t in range(n) for k, (px, py) in enumerate(_other_chips(x, y))]

    def start(*refs):
        for cp in plan(*refs):
            cp.start()

    def finish(*refs):
        copies = plan(*refs)
        for cp in copies:
            cp.wait_recv()
        for cp in copies:
            cp.wait_send()

    return dict(ins=ps, out_shapes=[jax.ShapeDtypeStruct((3,) + p.shape[1:], p.dtype) for p in ps], nsem=3 * n,
                start=start, finish=finish)


_JOIN_PIECES = 4


def _rs_sibling_join(qs):
    n = len(qs)
    npc = _JOIN_PIECES

    def body(*refs):
        q_refs, out_refs = refs[:n], refs[n:2 * n]
        send_sems, recv_sems = refs[2 * n:]
        x, y, c = _place()
        copies = []
        for t in range(n):
            rh = qs[t].shape[0] // 2
            pr = rh // npc
            for i in range(npc):
                rows = pl.ds(c * rh + i * pr, pr)
                cp = _rcopy(q_refs[t].at[rows], out_refs[t].at[rows], send_sems, recv_sems, npc * t + i,
                            (x, y, 1 - c))
                cp.start()
                copies.append(cp)
        for t in range(n):
            rh = qs[t].shape[0] // 2
            pr = rh // npc
            for i in range(npc):
                rows = pl.ds((1 - c) * rh + i * pr, pr)
                _rcopy(q_refs[t].at[rows], out_refs[t].at[rows], send_sems, recv_sems, npc * t + i,
                       (x, y, 1 - c)).wait_recv()
        for cp in copies:
            cp.wait_send()

    return pl.pallas_call(
        body, out_shape=[jax.ShapeDtypeStruct(q.shape, q.dtype) for q in qs],
        in_specs=[_ANY] * n, out_specs=[_ANY] * n, input_output_aliases={t: t for t in range(n)},
        scratch_shapes=[pltpu.SemaphoreType.DMA((npc * n,)), pltpu.SemaphoreType.DMA((npc * n,))],
        name="grads_sibling_join",
    )(*qs)


def _add_half(name, g, recv, chip, ci):
    S, r, w = g.shape
    rh = r // 2
    tt = _pick(rh, 256, 16)
    nb = rh // tt

    def body(chip_ref, core_ref, a_ref, b_ref, own_ref, ob_ref):
        v = a_ref[...] + b_ref[...]
        ob_ref[...] = v.astype(bf16)

        @pl.when(pl.program_id(1) == chip_ref[0])
        def _():
            own_ref[...] = v[0]

    grid_spec = pltpu.PrefetchScalarGridSpec(
        num_scalar_prefetch=2, grid=(nb, S),
        in_specs=[pl.BlockSpec((1, tt, w), lambda i, s_, ch, co: (s_, co[0] * nb + i, 0)),
                  pl.BlockSpec((1, tt, w), lambda i, s_, ch, co: (s_, i, 0))],
        out_specs=[pl.BlockSpec((tt, w), lambda i, s_, ch, co: (i, 0)),
                   pl.BlockSpec((1, tt, w), lambda i, s_, ch, co: (s_, i, 0))])
    return pl.pallas_call(body, grid_spec=grid_spec,
                          out_shape=[jax.ShapeDtypeStruct((rh, w), f32), jax.ShapeDtypeStruct((S, rh, w), bf16)],
                          compiler_params=pltpu.CompilerParams(dimension_semantics=("arbitrary", "arbitrary")),
                          name=name)(chip.reshape(1).astype(jnp.int32), ci.reshape(1).astype(jnp.int32), g, recv)


def _sum_chip(name, own, others, ci):
    rh, w = own.shape
    tt = _pick(rh, 256, 16)
    nb = rh // tt

    def body(core_ref, a_ref, b_ref, o_ref):
        o_ref[...] = ((a_ref[...] + b_ref[0].astype(f32)) + b_ref[1].astype(f32)) + b_ref[2].astype(f32)

    grid_spec = pltpu.PrefetchScalarGridSpec(
        num_scalar_prefetch=1, grid=(nb,),
        in_specs=[pl.BlockSpec((tt, w), lambda i, co: (i, 0)),
                  pl.BlockSpec((3, tt, w), lambda i, co: (0, i, 0))],
        out_specs=pl.BlockSpec((tt, w), lambda i, co: (co[0] * nb + i, 0)))
    return pl.pallas_call(body, grid_spec=grid_spec, out_shape=jax.ShapeDtypeStruct((2 * rh, w), f32),
                          name=name)(ci.reshape(1).astype(jnp.int32), own, others)


def _sum_devices(gathered, late, late_rows, head_row):
    _, rows, width = gathered.shape

    def body(g_ref, l_ref, out_ref, head_ref):
        acc, acc_l = g_ref[0], l_ref[0]
        for d in range(1, 8):
            acc = acc + g_ref[d]
            acc_l = acc_l + l_ref[d]
        out_ref[...] = acc
        for k, r in enumerate(late_rows):
            out_ref[r:r + 1, :] = acc_l[k:k + 1, :]
        row = acc[head_row:head_row + 1, :]
        hs = row[:, 0:HEAD_DIM]
        for h in range(1, HEADS):
            hs = hs + row[:, h * HEAD_DIM:(h + 1) * HEAD_DIM]
        head_ref[...] = jnp.zeros_like(head_ref)
        head_ref[0:1, 0:HEAD_DIM] = hs

    return pl.pallas_call(
        body, out_shape=[jax.ShapeDtypeStruct((rows, width), f32), jax.ShapeDtypeStruct((8, LANES), f32)],
        name="small_grads_sum",
    )(gathered, late)


def _pack(arrs, rows_mult, dtype):
    flat = jnp.concatenate([a.reshape(-1).astype(dtype) for a in arrs])
    per = PACK_W * rows_mult
    total = -(-flat.shape[0] // per) * per
    return jnp.pad(flat, (0, total - flat.shape[0])).reshape(total // PACK_W, PACK_W)


def _unpack(buf, shapes):
    flat = buf.reshape(-1)
    out, off = [], 0
    for s in shapes:
        n = int(np.prod(s))
        out.append(flat[off:off + n].reshape(s))
        off += n
    return out


_BIG = ["w_in", "w_branch_gdn", "w_branch_rwkv", "w_out", "w_ffn_in", "w_ffn_out"]
_MID = ["conv_gdn", "conv_ffn", "w2", "a2", "g2"]
_SMALL =["b_ada", "norm1_w", "a_log", "dt_bias", "onorm_gdn", "mu_rwkv", "w0", "a0", "k_k", "k_a", "r_k", "lnx_w",
          "lnx_b", "norm2_w", "norm_f_w"]
_ORDER = ["w_ada", "b_ada", "norm1_w", "w_in", "conv_gdn", "a_log", "dt_bias", "onorm_gdn", "w_branch_gdn", "mu_rwkv",
          "w0", "w2", "a0", "a2", "g2", "k_k", "k_a", "r_k", "lnx_w", "lnx_b", "w_branch_rwkv", "w_out", "norm2_w",
          "w_ffn_in", "conv_ffn", "w_ffn_out", "norm_f_w"]


_WIN_SEGMENTS = [(0, 1536, 0), (2064, 3600, 1536), (1536, 2048, 3072), (3600, 3728, 3584), (2048, 2064, 3712),
                 (3728, 3888, 3840), (3888, 5936, 4096)]
_WIN_PADDED = 6144
_COL_QKV, _COL_RKV, _COL_Z = (0, 1, 2), (3, 4, 5), 6
_COL_LORA, _COL_BA = 28, 29
_COL_GATE_LORA = 15
_COL_GL = (4, 5)
_JOINT_QKV, _JOINT_RKV, _JOINT_Z, _JOINT_GL = 0, 1, 6, 2
_SMALL_BLOCKS_AT = 3584


def _win_pad(shards):
    n = shards.shape[2]
    parts, at = [], 0
    for lo, hi, dst in _WIN_SEGMENTS:
        if dst > at:
            parts.append(jnp.zeros((shards.shape[1], dst - at), shards.dtype))
        c = lo
        while c < hi:
            j = c // n
            e = min(hi, (j + 1) * n)
            parts.append(shards[j][:, c - j * n:e - j * n])
            c = e
        at = dst + hi - lo
    if at < _WIN_PADDED:
        parts.append(jnp.zeros((shards.shape[1], _WIN_PADDED - at), shards.dtype))
    return jnp.concatenate(parts, axis=1)


def _win_unpad_shards(g, n):
    shards = []
    for j in range(4):
        parts = []
        for lo, hi, dst in sorted(_WIN_SEGMENTS):
            a, b = max(lo, j * n), min(hi, (j + 1) * n)
            if a < b:
                parts.append(g[:, dst + a - lo:dst + b - lo])
        shards.append(jnp.concatenate(parts, axis=1))
    return jnp.stack(shards)


def kernel(x, c, w_ada, b_ada, norm1_w, w_in, conv_gdn, a_log, dt_bias, onorm_gdn, w_branch_gdn, mu_rwkv, w0, w2, a0, a2, g2, k_k, k_a, r_k, lnx_w, lnx_b, w_branch_rwkv, w_out, norm2_w, w_ffn_in, conv_ffn, w_ffn_out, norm_f_w, loss_target, m_w_ada, m_b_ada, m_norm1_w, m_w_in, m_conv_gdn, m_a_log, m_dt_bias, m_onorm_gdn, m_w_branch_gdn, m_mu_rwkv, m_w0, m_w2, m_a0, m_a2, m_g2, m_k_k, m_k_a, m_r_k, m_lnx_w, m_lnx_b, m_w_branch_rwkv, m_w_out, m_norm2_w, m_w_ffn_in, m_conv_ffn, m_w_ffn_out, m_norm_f_w, v_w_ada, v_b_ada, v_norm1_w, v_w_in, v_conv_gdn, v_a_log, v_dt_bias, v_onorm_gdn, v_w_branch_gdn, v_mu_rwkv, v_w0, v_w2, v_a0, v_a2, v_g2, v_k_k, v_k_a, v_r_k, v_lnx_w, v_lnx_b, v_w_branch_rwkv, v_w_out, v_norm2_w, v_w_ffn_in, v_conv_ffn, v_w_ffn_out, v_norm_f_w):
    args = dict(locals())
    W = {n: args[n] for n in _ORDER}
    Mo = {n: args["m_" + n] for n in _ORDER}
    Vo = {n: args["v_" + n] for n in _ORDER}
    shapes = {n: W[n].shape for n in _ORDER}
    sq = lambda a: a.reshape(a.shape[-2:]) if a.ndim == 3 else a.reshape(1, -1)
    row = lambda a: a.reshape(1, -1)

    xi, yi, ci = lax.axis_index("x"), lax.axis_index("y"), lax.axis_index("c")
    dev = 4 * xi + 2 * yi + ci
    chip = 2 * xi + yi

    x2 = x[0]
    tgt = loss_target[0]
    T, D = x2.shape
    tt_l = _pick(T, 512, CHUNK)
    tt_p = _pick(T, 256, CHUNK)
    tt_h = _pick(T, 128, CHUNK)

    small_blk = _pack([c] + [W[n] for n in _MID], 8, f32)
    small_all, win_s = _run_job("gather_c_mid_and_w_in", [_ag8_job(small_blk), _ag4_job([sq(W["w_in"]).astype(bf16)])])
    c_all = small_all[:, 0, :]
    per_chip = small_all[0::2].reshape(4, -1)[:, D:]
    mid_w, at = {}, 0
    for n in _MID:
        nr, nw = shapes[n][1:]
        mid_w[n] = per_chip[:, at:at + nr * nw].reshape(4, nr, nw).transpose(1, 0, 2).reshape(nr, 4 * nw)
        at += nr * nw
    conv_gdn_f, conv_ffn_f = mid_w["conv_gdn"], mid_w["conv_ffn"]
    w2f, a2f, g2f = mid_w["w2"], mid_w["a2"], mid_w["g2"]

    later_weights = _ag4_job([sq(W[n]).astype(bf16) for n in _BIG[1:]])
    win_p = _win_pad(win_s)
    zpad = lambda a, top, bot: jnp.pad(a, ((top, bot), (0, 0)))
    w2p, a2p, g2p = zpad(w2f, 0, 64), zpad(a2f, 64, 0), zpad(g2f, 0, 96)

    ncol = shapes["w_ada"][2]
    b_cols = lax.dynamic_slice(sq(W["b_ada"]), (0, chip * ncol), (1, ncol))
    cond16, mod_cols = _ada_fwd(jnp.pad(c_all, ((0, 8), (0, 0))), sq(W["w_ada"]), b_cols)
    mod_all = _ag8("gather_mod", mod_cols[:8])
    mod_mine = lax.dynamic_slice(mod_all[0::2], (0, dev, 0), (4, 1, ncol)).reshape(1, 4 * ncol)
    shift1, scale1, gate1, shift2, scale2, gate2 = [mod_mine[:, i * D:(i + 1) * D] for i in range(6)]

    seg = _seg_matrix(WIDTH, HEAD_DIM)
    norm1 = [sq(W["norm1_w"]), shift1, scale1]
    h1 = _stage_fwd("norm_mod1", _fn_norm_mod, [_whole(x2)], norm1, [(D, bf16)], tt_l)[0]
    p = _matmul("in_proj", h1, win_p, "nn", tm=2048)

    cgq = [row(conv_gdn_f[j, part * WIDTH:(part + 1) * WIDTH]) for part in range(3) for j in range(4)]
    lane_pad = lambda a: jnp.pad(row(a), ((0, 0), (8, LANES - 16)))
    gdn_pre_ps = cgq + [lane_pad(W["a_log"]), lane_pad(W["dt_bias"]), seg, _chunk_tri(tt_p, CHUNK)]
    gdn_pre_ts = [(p, WIDTH, ci_, (0, 1, 2, 3)) for ci_ in _COL_QKV] + [(p, LANES, _COL_BA, None)]
    q_, k_, v_, beta_t, gc_t = _stage_fwd("gdn_pre", _fn_gdn_pre, gdn_pre_ts, gdn_pre_ps,
                                          [(WIDTH, f32)] * 3 + [(LANES, f32)] * 2, tt_p)
    o_, gdn_hist, gdn_inv, wbg_s, wbr_s, wout_s, wfi_s, wfo_s = _scan_fwd(
        "gdn_scan", _gdn_chunk, [q_, k_, v_], [beta_t, gc_t], job=later_weights)
    wout_f = wout_s.reshape(D, D)
    wfo = wfo_s.reshape(D_FF, D)
    ow512 = jnp.tile(row(W["onorm_gdn"]), (1, HEADS))
    gdn_post_ts = [_whole(o_), (p, WIDTH, _COL_Z, None)]
    ya = _stage_fwd("gdn_post", _fn_gdn_post, gdn_post_ts, [ow512, seg], [(WIDTH, bf16)], tt_l)[0]

    mu = sq(W["mu_rwkv"])
    rw_ps = [mu[:, 0:512], mu[:, 512:1024], mu[:, 1024:1536], mu[:, 1536:1664], jnp.pad(mu[:, 1664:1824], ((0, 0), (0, 96))),
             sq(W["w0"]), w2p, sq(W["a0"]), a2p, g2p, sq(W["k_k"]), sq(W["k_a"]), seg]
    rw_ts = [(p, WIDTH, ci_, (0, 1)) for ci_ in _COL_RKV] + [(p, LANES, _COL_LORA, (0, 1)),
                                                              (p, 256, _COL_GATE_LORA, (0, 1))]
    rw_out = _stage_fwd("rwkv_pre", _fn_rwkv_pre, rw_ts, rw_ps, [(WIDTH, f32)] * 7, tt_p)
    r_, lw_, k2_, vv_, na_, b_, g_ = rw_out
    rw_ins = [r_, lw_, k2_, vv_, na_, b_]
    y_, rw_hist, rw_inv = _scan_fwd("rwkv_scan", _rwkv_chunk, rw_ins)
    rwp_ps = [sq(W["lnx_w"]), sq(W["lnx_b"]), row(W["r_k"]), seg]
    rwp_ts = [_whole(y_), _whole(r_), _whole(k2_), _whole(vv_), _whole(g_)]
    yb = _stage_fwd("rwkv_post", _fn_rwkv_post, rwp_ts, rwp_ps, [(WIDTH, bf16)], tt_l)[0]

    big_a = _matmul("branch_gdn", ya, wbg_s, "nn", shards=4)
    big_b = _matmul("branch_rwkv", yb, wbr_s, "nn", shards=4)
    merge_ts = [(p, D, _COL_GL[0], None), (p, D, _COL_GL[1], None), _whole(big_a), _whole(big_b)]
    merged = _stage_fwd("merge", _fn_merge, merge_ts, [], [(D, bf16)], tt_l)[0]
    mo = _matmul("out_proj", merged, wout_f, "nn")
    norm2 = [gate1, sq(W["norm2_w"]), shift2, scale2]
    x1, h2 = _stage_fwd("resid_norm_mod2", _fn_resid_norm_mod, [_whole(x2), _whole(mo)], norm2, [(D, f32), (D, bf16)], tt_l)
    f = _matmul("ffn_in", h2, wfi_s, "nn", shards=4, tm=2048, tn=1408)
    cg_ps = [row(conv_ffn_f[j]) for j in range(3)]
    cg_ts = [(f, D_FF, 0, (0, 1, 2)), (f, D_FF, 1, None)]
    act = _stage_fwd("convglu", _fn_convglu, cg_ts, cg_ps, [(D_FF, bf16)], tt_h)[0]
    fo = _matmul("ffn_out", act, wfo, "nn", tk=D_FF)

    dx1_a, dfo, dgate2, dnormf, loss_part = _final_stage(x1, fo, tgt, gate2, row(W["norm_f_w"]), tt_l)

    dact = _matmul("d_act", dfo, wfo, "nt", tn=D_FF)
    g_wfo = _matmul("g_ffn_out", act, dfo, "tn", tm=1408, tk=2048)
    _, dcf, df = _stage_bwd("convglu_bwd", _fn_convglu, cg_ts, cg_ps, [[_whole(dact)]], tt_h, [True] * 2, [True] * 3,
                            joint=([0, 1], 2 * D_FF, 0, None), dtypes=[bf16])
    g_wfi = _matmul("g_ffn_in", h2, df, "tn", shards=4, tn=1408, tk=2048)
    g_wfo_s = g_wfo.reshape(4, D_FF // 4, D)
    dh2, recv_wfi, recv_wfo = _matmul("d_h2", df, wfi_s, "nt", shards=4, tm=2048, tk=1408,
                                      job=_sibling_swap_job([g_wfi, g_wfo_s]))
    (dx_a, dmo), (dgate1, dnorm2, dshift2, dscale2), _ = _stage_bwd(
        "resid_norm_mod2_bwd", _fn_resid_norm_mod, [_whole(x2), _whole(mo)], norm2,
        [[_whole(dx1_a)], [_whole(dh2)]], tt_l, [True, True], [True] * 4, dtypes=[f32, bf16])
    dmerged = _matmul("d_merged", dmo, wout_f, "nt")
    g_wout = _matmul("g_out_proj", merged, dmo, "tn", tk=T)
    (dbig_a, dbig_b), _, dp = _stage_bwd("merge_bwd", _fn_merge, merge_ts, [], [[_whole(dmerged)]], tt_l, [True] * 4, [],
                                         joint=([0, 1], p.shape[1], _JOINT_GL, None), dtypes=[bf16] * 3)
    g_wbg = _matmul("g_branch_gdn", ya, dbig_a, "tn", shards=4, tk=T)
    g_wbr = _matmul("g_branch_rwkv", yb, dbig_b, "tn", shards=4, tk=T)
    g_wout_s = g_wout.reshape(4, D // 4, D)
    dyb = _matmul("d_yb", dbig_b, wbr_s, "nt", shards=4)
    dya, *recv_mix = _matmul("d_ya", dbig_a, wbg_s, "nt", shards=4, job=_sibling_swap_job([g_wbg, g_wbr, g_wout_s]))

    (dy_, dr_p, dk2_p, dv_p, dg_p), (dlnxw, dlnxb, drk), _ = _stage_bwd(
        "rwkv_post_bwd", _fn_rwkv_post, rwp_ts, rwp_ps, [[_whole(dyb)]], tt_l, [True] * 5, [True, True, True, False])
    gs_a = [g_wbg, g_wbr, g_wout_s, g_wfi, g_wfo_s]
    pairs_a = [_add_half("grads_pair_sum%d" % (t + 1), g, r_, chip, ci)
               for t, (g, r_) in enumerate(zip(gs_a, recv_mix + [recv_wfi, recv_wfo]))]
    dr_c, dlw_c, dk2_c, dv_c, dna_c, db_c, *others_a = _scan_bwd(
        "rwkv_scan_bwd", _rwkv_chunk, rw_ins, [], rw_hist, rw_inv, dy_, job=_chip_exchange_job([pb for _, pb in pairs_a]))
    rw_cots = [[_whole(dr_p), _whole(dr_c)], [_whole(dlw_c)], [_whole(dk2_p), _whole(dk2_c)],
               [_whole(dv_p), _whole(dv_c)], [_whole(dna_c)], [_whole(db_c)], [_whole(dg_p)]]
    (dl_, dg_), rw_dp, dp = _stage_bwd("rwkv_pre_bwd", _fn_rwkv_pre, rw_ts, rw_ps, rw_cots, tt_p, [True] * 5,
                                       [True] * 12 + [False], joint=([0, 1, 2], p.shape[1], _JOINT_RKV, dp), dtypes=[bf16] * 3)
    dmu_r, dmu_k, dmu_v, dmu_l, dmu_g, dw0, dw2p, da0, da2p, dg2p, dkk, dka = rw_dp

    (do_,), (dow512,), dp = _stage_bwd("gdn_post_bwd", _fn_gdn_post, gdn_post_ts, [ow512, seg], [[_whole(dya)]], tt_l,
                                       [True, True], [True, False], joint=([1], p.shape[1], _JOINT_Z, dp), dtypes=[f32, bf16])
    d_gdn = _scan_bwd("gdn_scan_bwd", _gdn_chunk, [q_, k_, v_], [beta_t, gc_t], gdn_hist, gdn_inv, do_)
    gdn_cots = [[_whole(a)] for a in d_gdn]
    (dba,), gdn_dp, dp = _stage_bwd("gdn_pre_bwd", _fn_gdn_pre, gdn_pre_ts, gdn_pre_ps, gdn_cots, tt_p, [True] * 4,
                                    [True] * 14 + [False, False], joint=([0, 1, 2], p.shape[1], _JOINT_QKV, dp), dtypes=[bf16] * 2)
    dp = lax.dynamic_update_slice(dp, jnp.concatenate([dl_, dba, dg_], axis=1), (0, _SMALL_BLOCKS_AT))
    g_conv_gdn = jnp.concatenate([jnp.concatenate([gdn_dp[4 * part + j] for part in range(3)], axis=1) for j in range(4)], axis=0)
    g_conv_ffn = jnp.concatenate(dcf, axis=0)
    g_mu = jnp.concatenate([dmu_r, dmu_k, dmu_v, dmu_l, dmu_g[:, :160]], axis=1)
    late_zero = jnp.zeros((1, D), f32)
    dmod_early = jnp.concatenate([late_zero, late_zero, dgate1, dshift2, dscale2, dgate2], axis=1)
    small_parts = {"b_ada": dmod_early, "norm1_w": late_zero, "a_log": gdn_dp[12][:, 8:16], "dt_bias": gdn_dp[13][:, 8:16],
                   "mu_rwkv": g_mu, "w0": dw0, "a0": da0, "k_k": dkk, "k_a": dka, "r_k": drk, "lnx_w": dlnxw,
                   "lnx_b": dlnxb, "norm2_w": dnorm2, "norm_f_w": dnormf}
    small_names = [n for n in _SMALL if n != "onorm_gdn"]
    mid_full = [g_conv_gdn, g_conv_ffn, dw2p[0:64], da2p[64:128], dg2p[0:160]]
    body_rows = _pack([small_parts[n] for n in small_names] + [loss_part[:, 0:1]] + mid_full, 1, f32)
    head_row = body_rows.shape[0]
    small_g = jnp.concatenate([body_rows, jnp.pad(dow512, ((0, 0), (0, PACK_W - WIDTH)))], axis=0)
    small_g = jnp.pad(small_g, ((0, -small_g.shape[0] % 8), (0, 0)))

    g_win_pad, small_all_g = _matmul("g_in_proj", h1, dp, "tn", tk=T, job=_ag8_job(small_g))
    g_win_s = _win_unpad_shards(g_win_pad, shapes["w_in"][2])
    pair_win = _add_half("grads_pair_sum0", g_win_s, _run_job("w_in_grads_sibling_swap", _sibling_swap_job([g_win_s]))[0],
                         chip, ci)
    dh1, others_win = _matmul("d_h1", dp, win_p, "nt", tk=3072, job=_chip_exchange_job([pair_win[1]]))
    (grad_x,), (dnorm1, dshift1, dscale1), _ = _stage_bwd("norm_mod1_bwd", _fn_norm_mod_and_x, [_whole(x2)], norm1,
                                                          [[_whole(dh1)], [_whole(dx_a)]], tt_l, [True], [True] * 3)
    late_all = _ag8("gather_late_grads", jnp.pad(jnp.concatenate([dshift1, dscale1, dnorm1], axis=0), ((0, 5), (0, 0))))
    small_sum, head_sum = _sum_devices(small_all_g, late_all, (0, 1, 6), head_row)
    small_shapes = [shapes[n][1:] if n != "norm_f_w" else shapes[n] for n in small_names]
    un = _unpack(small_sum, small_shapes + [(1,)] + [g.shape for g in mid_full])
    small_grads = dict(zip(small_names, un))
    loss = un[len(small_names)].reshape(())
    small_grads["onorm_gdn"] = head_sum[0, 0:HEAD_DIM]
    for n, g in zip(_MID, un[len(small_names) + 1:]):
        wcols = shapes[n][2]
        small_grads[n] = lax.dynamic_slice(g, (0, chip * wcols), (g.shape[0], wcols))

    dmod_all = jnp.concatenate([late_all[:, 0, :], late_all[:, 1, :], small_all_g[:, 2:6, :].reshape(8, 4 * PACK_W)], axis=1)
    dmod_cols = lax.dynamic_slice(dmod_all, (0, chip * ncol), (8, ncol))
    g_wada = _matmul("g_w_ada", cond16, jnp.pad(dmod_cols, ((0, 8), (0, 0))), "tn")

    pairs = [pair_win] + pairs_a
    others = [others_win] + others_a
    halves = [_sum_chip("grads_chip_sum%d" % t, own, o_, ci) for t, ((own, _), o_) in enumerate(zip(pairs, others))]
    big_grads = dict(zip(_BIG, _rs_sibling_join(halves)))

    res = {tag: {} for tag in ("grad", "delta", "new_m", "new_v")}

    def put(n, g, d, m_, v_):
        for tag, val in zip(("grad", "delta", "new_m", "new_v"), (g, d, m_, v_)):
            res[tag][n] = val.reshape(shapes[n])

    for n in _BIG:
        put(n, big_grads[n], *_adamw("adamw_" + n, sq(W[n]), big_grads[n], sq(Mo[n]), sq(Vo[n])))
    put("w_ada", g_wada, *_adamw("adamw_w_ada", sq(W["w_ada"]), g_wada, sq(Mo["w_ada"]), sq(Vo["w_ada"])))
    rest = _SMALL + _MID
    pk = lambda d: _pack([d[n] for n in rest], 8, f32)
    sg = pk(small_grads)
    sm = _adamw("adamw_small", pk(W), sg, pk(Mo), pk(Vo))
    for tag, buf in zip(("grad", "delta", "new_m", "new_v"), (sg,) + tuple(sm)):
        res[tag].update(zip(rest, _unpack(buf, [shapes[n] for n in rest])))
    outs = [loss, grad_x.reshape(x.shape)]
    for tag in ("grad", "delta", "new_m", "new_v"):
        outs += [res[tag][n] for n in _ORDER]
    return tuple(outs)
```

```python
import numpy as np
import jax
import jax.numpy as jnp
from jax import lax
from jax.experimental import pallas as pl
from jax.experimental.pallas import tpu as pltpu

f32 = jnp.float32
bf16 = jnp.bfloat16

LANES = 128
HEADS = 8
HEAD_DIM = 64
WIDTH = HEADS * HEAD_DIM
CHUNK = 64
D_FF = 2816
NORM_EPS = 1e-6
LNX_EPS = 64e-5
PACK_W = 1024
MESH_ID = pl.DeviceIdType.MESH

ADAM_LR, ADAM_B1, ADAM_B2, ADAM_EPS, ADAM_WD, ADAM_STEP = 0.001, 0.9, 0.999, 1e-08, 0.01, 10


def _pick(n, target, mult):
    if n <= target:
        return n
    best = None
    for t in range(mult, target + 1, mult):
        if n % t == 0:
            best = t
    assert best is not None, (n, target, mult)
    return best


def _split_bf16(x, n):
    parts, r = [], x
    for i in range(n):
        p = r.astype(bf16)
        parts.append(p)
        if i + 1 < n:
            r = r - p.astype(f32)
    return parts


def _xdot_r_impl(x, m, n, dims):
    acc = None
    for p in _split_bf16(x, n):
        t = lax.dot_general(p, m, dims, preferred_element_type=f32)
        acc = t if acc is None else acc + t
    return acc


def _make_xdot_r(n):
    nn = (((1,), (0,)), ((), ()))
    nt = (((1,), (1,)), ((), ()))

    @jax.custom_vjp
    def xdot(x, m):
        return _xdot_r_impl(x, m, n, nn)

    def fwd(x, m):
        return _xdot_r_impl(x, m, n, nn), m

    def bwd(m, ct):
        return _xdot_r_impl(ct, m, n, nt), jnp.zeros_like(m)

    xdot.defvjp(fwd, bwd)
    return xdot


_segsum = _make_xdot_r(2)


def _xdot_l_impl(m, x, n, dims):
    acc = None
    for p in _split_bf16(x, n):
        t = lax.dot_general(m, p, dims, preferred_element_type=f32)
        acc = t if acc is None else acc + t
    return acc


@jax.custom_vjp
def _xdot_l(m, x):
    return _xdot_l_impl(m, x, 3, (((1,), (0,)), ((), ())))


def _xdot_l_fwd(m, x):
    return _xdot_l(m, x), m


def _xdot_l_bwd(m, ct):
    return jnp.zeros_like(m), _xdot_l_impl(m, ct, 3, (((0,), (0,)), ((), ())))


_xdot_l.defvjp(_xdot_l_fwd, _xdot_l_bwd)


@jax.custom_vjp
def _bdot(x, w):
    return jnp.dot(x.astype(bf16), w.astype(bf16), preferred_element_type=f32)


def _bdot_fwd(x, w):
    return _bdot(x, w), (x, w)


def _bdot_bwd(res, ct):
    x, w = res
    c = ct.astype(bf16)
    dx = lax.dot_general(c, w.astype(bf16), (((1,), (1,)), ((), ())), preferred_element_type=f32)
    dw = lax.dot_general(x.astype(bf16), c, (((0,), (0,)), ((), ())), preferred_element_type=f32)
    return dx, dw


_bdot.defvjp(_bdot_fwd, _bdot_bwd)


def _silu(x):
    return x * jax.nn.sigmoid(x)


def _softplus(x):
    return jnp.maximum(x, 0.0) + jnp.log(1.0 + jnp.exp(-jnp.abs(x)))


def _rms(x, w, eps):
    return x * lax.rsqrt(jnp.mean(x * x, axis=-1, keepdims=True) + eps) * w


def _seg_matrix(width, seg):
    i = np.arange(width)
    return jnp.asarray((i[:, None] // seg) == (i[None, :] // seg), dtype=bf16)


def _chunk_tri(rows, chunk):
    i = np.arange(rows)
    return jnp.asarray(((i[:, None] // chunk) == (i[None, :] // chunk)) & (i[:, None] >= i[None, :]), dtype=bf16)


HALO = 8


def _full_spec(shape):
    nd = len(shape)
    return pl.BlockSpec(shape, lambda i: (0,) * nd)


def _entry_specs(entries, tt, block_of):
    specs, ops = [], []
    for arr, w, ci, shifts in entries:
        specs.append(pl.BlockSpec((tt, w), lambda i, ci=ci: (block_of(i), ci)))
        ops.append(arr)
        if shifts:
            specs.append(pl.BlockSpec((HALO, w), lambda i, ci=ci: (jnp.maximum(block_of(i) * (tt // HALO) - 1, 0), ci)))
            ops.append(arr)
    return specs, ops


def _load_entries(entries, refs, first):
    tiles, k = [], 0
    for _, w, _, shifts in entries:
        x = refs[k][...].astype(f32)
        k += 1
        if not shifts:
            tiles.append(x)
            continue
        halo = jnp.where(first, 0.0, refs[k][...].astype(f32))
        k += 1
        row = lax.broadcasted_iota(jnp.int32, (HALO, w), 0)
        for s in shifts:
            if s == 0:
                tiles.append(x)
                continue
            r = pltpu.roll(x, s, 0)
            head = jnp.where(row < s, pltpu.roll(halo, s, 0), r[0:HALO])
            tiles.append(jnp.concatenate([head, r[HALO:]], axis=0))
    return tiles


def _unshift_sum(grads, shifts, carry, tt):
    w = grads[0].shape[1]
    row = lax.broadcasted_iota(jnp.int32, (tt, w), 0)
    row8 = lax.broadcasted_iota(jnp.int32, (HALO, w), 0)
    dx, out = None, jnp.zeros((HALO, w), f32)
    for d, s in zip(grads, shifts):
        if s == 0:
            part = d
        else:
            part = jnp.where(row < tt - s, pltpu.roll(d, tt - s, 0), 0.0)
            out = out + jnp.where(row8 >= HALO - s, pltpu.roll(d[0:HALO], HALO - s, 0), 0.0)
        dx = part if dx is None else dx + part
    return jnp.concatenate([dx[:tt - HALO], dx[tt - HALO:] + carry], axis=0), out


def _stage_fwd(name, fn, tiles, params, outs, tt):
    rows = tiles[0][0].shape[0]
    npar = len(params)
    specs, ops = _entry_specs(tiles, tt, lambda i: i)
    nin = len(ops)

    def body(*refs):
        ts = _load_entries(tiles, refs[:nin], pl.program_id(0) == 0)
        ps = [r[...] for r in refs[nin:nin + npar]]
        res = fn(ps, ts)
        for r, v in zip(refs[nin + npar:], res):
            r[...] = v.astype(r.dtype)

    return pl.pallas_call(
        body, grid=(rows // tt,),
        in_specs=specs + [_full_spec(p.shape) for p in params],
        out_specs=[pl.BlockSpec((tt, w), lambda i: (i, 0)) for (w, _) in outs],
        out_shape=[jax.ShapeDtypeStruct((rows, w), dt) for (w, dt) in outs],
        compiler_params=pltpu.CompilerParams(dimension_semantics=("parallel",)),
        name=name,
    )(*ops, *params)


def _stage_bwd(name, fn, tiles, params, cots, tt, tile_grad, param_grad, joint=None, dtypes=None):
    rows = tiles[0][0].shape[0]
    nblk = rows // tt
    npar = len(params)
    block_of = lambda i: nblk - 1 - i
    specs, ops = _entry_specs(tiles, tt, block_of)
    nin = len(ops)
    flat_cots = [c for group in cots for c in group]
    groups = [len(g) for g in cots]
    ncot = len(flat_cots)
    counts = [len(e[3]) if e[3] else 1 for e in tiles]
    dt_entries = [e for e, g in zip(tiles, tile_grad) if g]
    dp_shapes = [p.shape for p, g in zip(params, param_grad) if g]
    ndt = len(dt_entries)
    carry_w = [e[1] for e in dt_entries if e[3]]
    flags = [g for g, n in zip(tile_grad, counts) for _ in range(n)]
    members, j_width, j_cidx, j_buf = joint if joint else ([], 0, 0, None)
    solo = [k for k in range(ndt) if k not in members]
    nsolo, njoint, nbuf = len(solo), int(bool(members)), int(j_buf is not None)
    j_block = sum(dt_entries[k][1] for k in members)
    dtypes = list(dtypes) if dtypes else [f32] * (nsolo + njoint)

    def body(*refs):
        i = pl.program_id(0)
        p_refs = refs[nin:nin + npar]
        c_refs = refs[nin + npar:nin + npar + ncot]
        base = nin + npar + ncot + nbuf
        dt_refs = refs[base:base + nsolo]
        joint_refs = refs[base + nsolo:base + nsolo + njoint]
        dp_refs = refs[base + nsolo + njoint:base + nsolo + njoint + len(dp_shapes)]
        carry_refs = refs[base + nsolo + njoint + len(dp_shapes):]
        ts = _load_entries(tiles, refs[:nin], block_of(i) == 0)
        ps = [r[...] for r in p_refs]

        def f(dp, dt):
            dp, dt = iter(dp), iter(dt)
            pp = [next(dp) if g else p for p, g in zip(ps, param_grad)]
            tl = [next(dt) if g else t for t, g in zip(ts, flags)]
            return fn(pp, tl)

        _, vjp = jax.vjp(f, [p for p, g in zip(ps, param_grad) if g], [t for t, g in zip(ts, flags) if g])
        cs, j = [], 0
        for n in groups:
            acc = c_refs[j][...].astype(f32)
            for q in range(1, n):
                acc = acc + c_refs[j + q][...].astype(f32)
            cs.append(acc)
            j += n
        gp, gt = vjp(cs)

        @pl.when(i == 0)
        def _():
            for r in dp_refs:
                r[...] = jnp.zeros_like(r)
            for r in carry_refs:
                r[...] = jnp.zeros_like(r)

        gt, k, kc, dxs = list(gt), 0, 0, []
        for e, n in zip(dt_entries, [n for n, g in zip(counts, tile_grad) if g]):
            if e[3]:
                dx, out = _unshift_sum(gt[k:k + n], e[3], carry_refs[kc][...], tt)
                carry_refs[kc][...] = out
                kc += 1
            else:
                dx = gt[k]
            dxs.append(dx)
            k += n
        for r, k in zip(dt_refs, solo):
            r[...] = dxs[k].astype(r.dtype)
        off = 0
        for k in members:
            w = dt_entries[k][1]
            joint_refs[0][:, off:off + w] = dxs[k].astype(joint_refs[0].dtype)
            off += w
        for r, v in zip(dp_refs, gp):
            r[...] += v

    res = pl.pallas_call(
        body, grid=(nblk,),
        in_specs=specs + [_full_spec(p.shape) for p in params]
        + [pl.BlockSpec((tt, w), lambda i, ci=ci: (block_of(i), ci)) for (_, w, ci, *_) in flat_cots]
        + [pl.BlockSpec(memory_space=pl.ANY)] * nbuf,
        out_specs=[pl.BlockSpec((tt, dt_entries[k][1]), lambda i: (block_of(i), 0)) for k in solo]
        + [pl.BlockSpec((tt, j_block), lambda i: (block_of(i), j_cidx))] * njoint
        + [_full_spec(s) for s in dp_shapes],
        out_shape=[jax.ShapeDtypeStruct((rows, dt_entries[k][1]), dt) for k, dt in zip(solo, dtypes)]
        + [jax.ShapeDtypeStruct((rows, j_width), dtypes[-1])] * njoint
        + [jax.ShapeDtypeStruct(s, f32) for s in dp_shapes],
        scratch_shapes=[pltpu.VMEM((HALO, w), f32) for w in carry_w],
        input_output_aliases={nin + npar + ncot: nsolo} if nbuf else {},
        compiler_params=pltpu.CompilerParams(dimension_semantics=("arbitrary",)),
        name=name,
    )(*ops, *params, *[c[0] for c in flat_cots], *([j_buf] if nbuf else []))
    res = list(res)
    return res[:nsolo], res[nsolo + njoint:], (res[nsolo] if njoint else None)


def _whole(a):
    return (a, a.shape[1], 0, None)


def _matmul(name, a, b, mode, out_dtype=f32, tm=1024, tn=1024, tk=1024, shards=1, job=None):
    S = shards
    if mode == "nn":
        M, K = a.shape
        w = b.shape[-1]
    elif mode == "nt":
        M = a.shape[0]
        if S > 1:
            _, N, w = b.shape
            K = S * w
        else:
            N, K = b.shape
            w = K
    else:
        K, M = a.shape
        w = b.shape[1] // S
    if mode != "nt":
        N = S * w
    tm = _pick(M, tm, LANES)
    if mode == "nt":
        tn = _pick(N, tn, LANES)
        tk = _pick(w, tk, LANES)
    else:
        tn = _pick(w, tn, LANES)
        tk = _pick(K, tk, LANES if mode == "nn" else 16)
    nk = K // tk
    nb = w // (tk if mode == "nt" else tn)
    if mode == "nn":
        a_spec = pl.BlockSpec((tm, tk), lambda i, j, k: (i, k))
        if S > 1:
            b_spec = pl.BlockSpec((1, tk, tn), lambda i, j, k: (j // nb, k, j % nb))
        else:
            b_spec = pl.BlockSpec((tk, tn), lambda i, j, k: (k, j))
        dims = (((1,), (0,)), ((), ()))
    elif mode == "nt":
        a_spec = pl.BlockSpec((tm, tk), lambda i, j, k: (i, k))
        if S > 1:
            b_spec = pl.BlockSpec((1, tn, tk), lambda i, j, k: (k // nb, j, k % nb))
        else:
            b_spec = pl.BlockSpec((tn, tk), lambda i, j, k: (j, k))
        dims = (((1,), (1,)), ((), ()))
    else:
        a_spec = pl.BlockSpec((tk, tm), lambda i, j, k: (k, i))
        b_spec = pl.BlockSpec((tk, tn), lambda i, j, k: (k, j))
        dims = (((0,), (0,)), ((), ()))
    if mode == "tn" and S > 1:
        o_spec = pl.BlockSpec((1, tm, tn), lambda i, j, k: (j // nb, i, j % nb))
        o_shape = (S, M, w)
    else:
        o_spec = pl.BlockSpec((tm, tn), lambda i, j, k: (i, j))
        o_shape = (M, N)
    b_lead = S > 1 and mode != "tn"
    o_lead = S > 1 and mode == "tn"

    j_ins, j_outs, j_sems = _job_parts(job)
    nji, njo = len(j_ins), len(j_outs)
    grid = (M // tm, N // tn, nk)

    def run_job(refs):
        step = (pl.program_id(0) * grid[1] + pl.program_id(1)) * grid[2] + pl.program_id(2)
        _job_steps(job, refs[2:2 + nji], refs[3 + nji:3 + nji + njo], refs[len(refs) - len(j_sems):], step,
                   grid[0] * grid[1] * grid[2] - 1)

    def body(*refs):
        a_ref, b_ref, o_ref, acc_ref = refs[0], refs[1], refs[2 + nji], refs[3 + nji + njo]
        run_job(refs)
        k = pl.program_id(2)

        @pl.when(k == 0)
        def _():
            acc_ref[...] = jnp.zeros_like(acc_ref)

        bv = b_ref[0] if b_lead else b_ref[...]
        acc_ref[...] += lax.dot_general(a_ref[...].astype(bf16), bv.astype(bf16), dims, preferred_element_type=f32)

        @pl.when(k == nk - 1)
        def _():
            if o_lead:
                o_ref[0] = acc_ref[...].astype(o_ref.dtype)
            else:
                o_ref[...] = acc_ref[...].astype(o_ref.dtype)

    def body_one_step(*refs):
        a_ref, b_ref, o_ref = refs[0], refs[1], refs[2 + nji]
        run_job(refs)
        bv = b_ref[0] if b_lead else b_ref[...]
        res = lax.dot_general(a_ref[...].astype(bf16), bv.astype(bf16), dims, preferred_element_type=f32)
        if o_lead:
            o_ref[0] = res.astype(o_ref.dtype)
        else:
            o_ref[...] = res.astype(o_ref.dtype)

    res = pl.pallas_call(
        body if nk > 1 else body_one_step, grid=grid,
        in_specs=[a_spec, b_spec] + [_ANY] * nji,
        out_specs=[o_spec] + [_ANY] * njo,
        out_shape=[jax.ShapeDtypeStruct(o_shape, out_dtype)] + j_outs,
        scratch_shapes=([pltpu.VMEM((tm, tn), f32)] if nk > 1 else []) + j_sems,
        compiler_params=pltpu.CompilerParams(
            dimension_semantics=("arbitrary",) * 3 if job else ("parallel", "parallel", "arbitrary")),
        name=name,
    )(a, b, *j_ins)
    return res if job else res[0]


def _make_bmm(precision):
    if precision is None:
        cast, kw = (lambda v: v.astype(bf16)), {}
    else:
        cast, kw = (lambda v: v), {"precision": precision}

    def nn(a, b):
        return jnp.einsum("hij,hjk->hik", cast(a), cast(b), preferred_element_type=f32, **kw)

    def nt(a, b):
        return jnp.einsum("hik,hjk->hij", cast(a), cast(b), preferred_element_type=f32, **kw)

    def tn(a, b):
        return jnp.einsum("hki,hkj->hij", cast(a), cast(b), preferred_element_type=f32, **kw)

    if precision is not None:
        return nn, nt, tn
    nn_v, nt_v, tn_v = jax.custom_vjp(nn), jax.custom_vjp(nt), jax.custom_vjp(tn)
    keep = lambda f: (lambda a, b: (f(a, b), (a, b)))
    nn_v.defvjp(keep(nn), lambda r, ct: (nt(ct, r[1]), tn(r[0], ct)))
    nt_v.defvjp(keep(nt), lambda r, ct: (nn(ct, r[1]), tn(ct, r[0])))
    tn_v.defvjp(keep(tn), lambda r, ct: (nt(r[1], ct), nn(r[0], ct)))
    return nn_v, nt_v, tn_v


_bmm, _bmm_nt, _bmm_tn = _make_bmm(None)
_bmm_exact = _make_bmm(lax.Precision.HIGH)[0]


def _masks(n):
    r = lax.broadcasted_iota(jnp.int32, (n, n), 0)
    c = lax.broadcasted_iota(jnp.int32, (n, n), 1)
    return (r >= c)[None], (r > c)[None], (r == c)[None]


_INV_BLOCK = 8


def _nilpotent_inverse(m, eye):
    p = eye + m
    for _ in range(2):
        m = _bmm(m, m)
        p = p + _bmm(p, m)
    return p


def _neumann_inverse_impl(m):
    n = m.shape[1]
    assert n == _INV_BLOCK * _INV_BLOCK
    r = lax.broadcasted_iota(jnp.int32, (n, n), 0)
    c = lax.broadcasted_iota(jnp.int32, (n, n), 1)
    eye = (r == c).astype(f32)[None]
    inside = jnp.where((r // _INV_BLOCK == c // _INV_BLOCK)[None], m, 0.0)
    d_inv = _nilpotent_inverse(inside, eye)
    return _bmm(_nilpotent_inverse(_bmm(d_inv, m - inside), eye), d_inv)


@jax.custom_vjp
def _neumann_inverse(m):
    return _neumann_inverse_impl(m)


def _neumann_inverse_fwd(m):
    p = _neumann_inverse_impl(m)
    return p, p


def _neumann_inverse_bwd(p, ct):
    return (_bmm_tn(p, _bmm_nt(ct, p)),)


_neumann_inverse.defvjp(_neumann_inverse_fwd, _neumann_inverse_bwd)


@jax.custom_vjp
def _given_inverse(m, p):
    return p


_given_inverse.defvjp(lambda m, p: (p, p), lambda p, ct: (_neumann_inverse_bwd(p, ct)[0], jnp.zeros_like(p)))


def _gdn_chunk(s, q, k, v, beta, gc, gr, gl, p=None):
    n = q.shape[1]
    causal, strict, _ = _masks(n)
    decay = jnp.where(causal, jnp.exp(jnp.where(causal, gc - gr, 0.0)), 0.0)
    kb = k * beta
    vb = v * beta
    lower = jnp.where(strict, _bmm_nt(kb, k) * decay, 0.0)
    t_mat = _neumann_inverse(-lower) if p is None else _given_inverse(-lower, p)
    egc = jnp.exp(gc)
    u = _bmm(t_mat, vb)
    w = _bmm(t_mat, kb * egc)
    attn = jnp.where(causal, _bmm_nt(q, k) * decay, 0.0)
    v_new = u - _bmm(w, s)
    o = _bmm(q * egc, s) + _bmm(attn, v_new)
    k_dec = k * jnp.exp(gl - gc)
    s_new = s * jnp.exp(gl) + _bmm_tn(k_dec, v_new)
    return s_new, o, t_mat


def _rwkv_chunk(s, r, lw, k, v, a, b, p=None):
    n = r.shape[1]
    causal, strict, _ = _masks(n)
    tri = jnp.broadcast_to(causal.astype(f32), (r.shape[0], n, n))
    lc = _bmm_exact(tri, lw)
    ein = jnp.exp(lc)
    eout = jnp.exp(-lc)
    a_t = a * jnp.exp(lc - lw)
    b_t = b * eout
    k_t = k * eout
    r_t = r * ein
    a_ab = jnp.where(strict, _bmm_nt(a_t, b_t), 0.0)
    a_ak = jnp.where(strict, _bmm_nt(a_t, k_t), 0.0)
    inv = _neumann_inverse(a_ab) if p is None else _given_inverse(a_ab, p)
    u = _bmm(inv, _bmm_nt(a_t, s) + _bmm(a_ak, v))
    y = (_bmm_nt(r_t, s) + _bmm(jnp.where(causal, _bmm_nt(r_t, b_t), 0.0), u)
         + _bmm(jnp.where(causal, _bmm_nt(r_t, k_t), 0.0), v))
    e_last = jnp.exp(jnp.sum(lw, axis=1, keepdims=True))
    s_new = s * e_last + _bmm_tn(u, b_t * e_last) + _bmm_tn(v, k_t * e_last)
    return s_new, y, inv


def _heads_in(ref, rows):
    return jnp.stack([ref[rows, h * HEAD_DIM:(h + 1) * HEAD_DIM] for h in range(HEADS)], axis=0)


def _heads_out(ref, rows, val):
    for h in range(HEADS):
        ref[rows, h * HEAD_DIM:(h + 1) * HEAD_DIM] = val[h]


def _gdn_scalars(bt, gt):
    n = bt.shape[0]
    gtt = gt.T
    hs = range(HEADS)
    return [jnp.stack([bt[:, h:h + 1] for h in hs], axis=0),
            jnp.stack([gt[:, HEADS + h:HEADS + h + 1] for h in hs], axis=0),
            jnp.stack([gtt[HEADS + h:HEADS + h + 1, :] for h in hs], axis=0),
            jnp.stack([gt[n - 1:n, HEADS + h:HEADS + h + 1] for h in hs], axis=0)]


def _gdn_scalars_back(dbeta, dgc, dgr, dgl):
    n = dbeta.shape[1]
    lane = lax.broadcasted_iota(jnp.int32, (n, LANES), 1)
    row = lax.broadcasted_iota(jnp.int32, (n, LANES), 0)
    sub = lax.broadcasted_iota(jnp.int32, (LANES, n), 0)
    db = jnp.zeros((n, LANES), f32)
    dg = jnp.zeros((n, LANES), f32)
    dgt = jnp.zeros((LANES, n), f32)
    for h in range(HEADS):
        db = jnp.where(lane == h, dbeta[h], db)
        dg = jnp.where(lane == HEADS + h, dgc[h] + jnp.where(row == n - 1, dgl[h], 0.0), dg)
        dgt = jnp.where(sub == HEADS + h, dgr[h], dgt)
    return [db, dg + dgt.T]


SCAN_GROUP = 4


def _scan_steps(t):
    g = SCAN_GROUP if (t // CHUNK) % SCAN_GROUP == 0 else 1
    return g, t // (CHUNK * g)


def _scan_spec(width, g, n, reverse):
    if reverse:
        return pl.BlockSpec((g * CHUNK, width), lambda i: (n - 1 - i, 0))
    return pl.BlockSpec((g * CHUNK, width), lambda i: (i, 0))


def _hist_spec(g, n, reverse):
    blk = (g, HEADS, HEAD_DIM, HEAD_DIM)
    if reverse:
        return pl.BlockSpec(blk, lambda i: (n - 1 - i, 0, 0, 0))
    return pl.BlockSpec(blk, lambda i: (i, 0, 0, 0))


def _job_list(job):
    return [] if job is None else (list(job) if isinstance(job, (list, tuple)) else [job])


def _job_parts(job):
    jobs = _job_list(job)
    sems = [pltpu.SemaphoreType.DMA((j["nsem"],)) for j in jobs for _ in range(2)]
    return [a for j in jobs for a in j["ins"]], [o for j in jobs for o in j["out_shapes"]], sems


_JOB_PHASES = ("start", "forward", "finish")


def _job_phase(job, phase, in_refs, out_refs, sems):
    ki = ko = 0
    for n, j in enumerate(_job_list(job)):
        ni, no = len(j["ins"]), len(j["out_shapes"])
        if phase in j:
            j[phase](in_refs[ki:ki + ni], out_refs[ko:ko + no], sems[2 * n], sems[2 * n + 1])
        ki, ko = ki + ni, ko + no


def _job_steps(job, in_refs, out_refs, sems, step, last):
    if job is None:
        return
    for phase, at in zip(_JOB_PHASES, (0, max(last - 2, 0), last)):
        @pl.when(step == at)
        def _(phase=phase):
            _job_phase(job, phase, in_refs, out_refs, sems)


def _scan_fwd(name, fn, rows_in, scal_in=(), job=None):
    t = rows_in[0].shape[0]
    grp, n = _scan_steps(t)
    nr, ns = len(rows_in), len(scal_in)
    j_ins, j_outs, j_sems = _job_parts(job)
    nji, njo = len(j_ins), len(j_outs)

    def body(*refs):
        o_ref, sh_ref, ph_ref = refs[nr + ns + nji:nr + ns + nji + 3]
        s_scr = refs[nr + ns + nji + 3 + njo]
        _job_steps(job, refs[nr + ns:nr + ns + nji], refs[nr + ns + nji + 3:nr + ns + nji + 3 + njo],
                   refs[nr + ns + nji + 3 + njo + 1:], pl.program_id(0), n - 1)

        @pl.when(pl.program_id(0) == 0)
        def _():
            s_scr[...] = jnp.zeros_like(s_scr)

        s = s_scr[...]
        for sub in range(grp):
            rows = slice(sub * CHUNK, (sub + 1) * CHUNK)
            sh_ref[sub] = s
            ins = [_heads_in(r, rows) for r in refs[:nr]]
            if ns:
                ins += _gdn_scalars(*[r[rows, :] for r in refs[nr:nr + ns]])
            s, o, p = fn(s, *ins)
            _heads_out(o_ref, rows, o)
            ph_ref[sub] = p
        s_scr[...] = s

    return pl.pallas_call(
        body, grid=(n,),
        in_specs=[_scan_spec(a.shape[1], grp, n, False) for a in (*rows_in, *scal_in)] + [_ANY] * nji,
        out_specs=[_scan_spec(WIDTH, grp, n, False), _hist_spec(grp, n, False), _hist_spec(grp, n, False)] + [_ANY] * njo,
        out_shape=[jax.ShapeDtypeStruct((t, WIDTH), f32)]
        + [jax.ShapeDtypeStruct((t // CHUNK, HEADS, HEAD_DIM, HEAD_DIM), f32)] * 2 + j_outs,
        scratch_shapes=[pltpu.VMEM((HEADS, HEAD_DIM, HEAD_DIM), f32)] + j_sems,
        compiler_params=pltpu.CompilerParams(dimension_semantics=("arbitrary",)),
        name=name,
    )(*rows_in, *scal_in, *j_ins)


def _scan_bwd(name, fn, rows_in, scal_in, s_hist, p_hist, d_out, job=None):
    t = rows_in[0].shape[0]
    grp, n = _scan_steps(t)
    nr, ns = len(rows_in), len(scal_in)
    j_ins, j_outs, j_sems = _job_parts(job)
    nji, njo = len(j_ins), len(j_outs)

    def body(*refs):
        sh_ref, ph_ref, do_ref = refs[nr + ns:nr + ns + 3]
        base = nr + ns + 3 + nji
        g_refs = refs[base:base + nr + ns]
        ds_scr = refs[base + nr + ns + njo]
        _job_steps(job, refs[nr + ns + 3:base], refs[base + nr + ns:base + nr + ns + njo],
                   refs[base + nr + ns + njo + 1:], pl.program_id(0), n - 1)

        @pl.when(pl.program_id(0) == 0)
        def _():
            ds_scr[...] = jnp.zeros_like(ds_scr)

        ds = ds_scr[...]
        for sub in reversed(range(grp)):
            rows = slice(sub * CHUNK, (sub + 1) * CHUNK)
            ins = [_heads_in(r, rows) for r in refs[:nr]]
            if ns:
                ins += _gdn_scalars(*[r[rows, :] for r in refs[nr:nr + ns]])
            p = ph_ref[sub]
            _, vjp = jax.vjp(lambda s, *a, p=p: fn(s, *a, p=p)[:2], sh_ref[sub], *ins)
            g = vjp((ds, _heads_in(do_ref, rows)))
            ds = g[0]
            for r, v in zip(g_refs[:nr], g[1:1 + nr]):
                _heads_out(r, rows, v)
            if ns:
                for r, v in zip(g_refs[nr:], _gdn_scalars_back(*g[1 + nr:])):
                    r[rows, :] = v
        ds_scr[...] = ds

    arrs = (*rows_in, *scal_in)
    return pl.pallas_call(
        body, grid=(n,),
        in_specs=[_scan_spec(a.shape[1], grp, n, True) for a in arrs]
        + [_hist_spec(grp, n, True), _hist_spec(grp, n, True), _scan_spec(WIDTH, grp, n, True)] + [_ANY] * nji,
        out_specs=[_scan_spec(a.shape[1], grp, n, True) for a in arrs] + [_ANY] * njo,
        out_shape=[jax.ShapeDtypeStruct(a.shape, f32) for a in arrs] + j_outs,
        scratch_shapes=[pltpu.VMEM((HEADS, HEAD_DIM, HEAD_DIM), f32)] + j_sems,
        compiler_params=pltpu.CompilerParams(dimension_semantics=("arbitrary",)),
        name=name,
    )(*arrs, s_hist, p_hist, d_out, *j_ins)


def _fn_norm_mod(ps, ts):
    nw, shift, scale = ps
    (x,) = ts
    return [_rms(x, nw, NORM_EPS) * (1.0 + scale) + shift]


def _fn_norm_mod_and_x(ps, ts):
    return _fn_norm_mod(ps, ts) + [ts[0]]


def _fn_resid_norm_mod(ps, ts):
    gate, nw, shift, scale = ps
    x, mo = ts
    x1 = x + gate * mo
    return [x1, _rms(x1, nw, NORM_EPS) * (1.0 + scale) + shift]


def _fn_gdn_pre(ps, ts):
    cw = ps[:12]
    alog, dtb, seg, tri = ps[12:]
    ba = ts[12]
    outs = []
    for part in range(3):
        x = ts[4 * part:4 * part + 4]
        w = cw[4 * part:4 * part + 4]
        conv = w[3] * x[0] + w[2] * x[1] + w[1] * x[2] + w[0] * x[3]
        u = _silu(conv)
        if part < 2:
            u = u * lax.rsqrt(_segsum(u * u, seg) + 1e-6)
            if part == 0:
                u = u * (HEAD_DIM ** -0.5)
        outs.append(u)
    beta = jax.nn.sigmoid(ba)
    g = -jnp.exp(alog) * _softplus(ba + dtb)
    gc = _xdot_l(tri, g)
    return outs + [beta, gc]


def _fn_gdn_post(ps, ts):
    ow, seg = ps
    o, z = ts
    ms = _segsum(o * o, seg) * (1.0 / HEAD_DIM)
    return [o * lax.rsqrt(ms + NORM_EPS) * ow * _silu(z)]


def _fn_rwkv_pre(ps, ts):
    mu_r, mu_k, mu_v, mu_l, mu_g, w0, w2p, a0, a2p, g2p, k_k, k_a, seg = ps
    r0, r1, k0, k1, v0, v1, l0, l1, g0, g1 = ts
    xr = r0 + (r1 - r0) * mu_r
    xk = k0 + (k1 - k0) * mu_k
    xv = v0 + (v1 - v0) * mu_v
    xl = l0 + (l1 - l0) * mu_l
    xg = g0 + (g1 - g0) * mu_g
    w = -_softplus(-(w0 + _bdot(jnp.tanh(xl), w2p))) - 0.5
    lw = -jnp.exp(w)
    a = jax.nn.sigmoid(a0 + _bdot(xl, a2p))
    g = _bdot(jax.nn.sigmoid(xg), g2p)
    kk = xk * k_k
    kk = kk * lax.rsqrt(_segsum(kk * kk, seg) + 1e-6)
    k2 = xk * (1.0 + (a - 1.0) * k_a)
    return [xr, lw, k2, xv, -kk, kk * a, g]


def _fn_rwkv_post(ps, ts):
    lw_, lb_, rk, seg = ps
    y, r, k2, v, g = ts
    inv = 1.0 / HEAD_DIM
    yc = y - _segsum(y, seg) * inv
    var = _segsum(yc * yc, seg) * inv
    yn = yc * lax.rsqrt(var + LNX_EPS) * lw_ + lb_
    bonus = _segsum(r * k2 * rk, seg) * v
    return [(yn + bonus) * g]


def _fn_merge(ps, ts):
    gla, glb, ya, yb = ts
    return [jax.nn.sigmoid(gla) * ya + jax.nn.sigmoid(glb) * yb]


def _fn_convglu(ps, ts):
    c0, c1, c2 = ps
    g0, g1, g2, up = ts
    return [_silu(c2 * g0 + c1 * g1 + c0 * g2) * up]


def _final_stage(x1, fo, tgt, gate2, nfw, tt):
    rows, d = x1.shape

    def loss_fn(gate, nw, xa, fa, tg):
        y = _rms(xa + gate * fa, nw, NORM_EPS)
        err = (y - tg) ** 2
        return 0.5 * jnp.sum(jnp.mean(err, axis=-1, keepdims=True), axis=0, keepdims=True)

    def body(x_ref, f_ref, t_ref, g_ref, w_ref, dx_ref, df_ref, dg_ref, dw_ref, l_ref):
        i = pl.program_id(0)
        args = (g_ref[...], w_ref[...], x_ref[...], f_ref[...])
        tg = t_ref[...]
        lv, vjp = jax.vjp(lambda g, w, xa, fa: loss_fn(g, w, xa, fa, tg), *args)
        dg, dw, dx, df = vjp(jnp.ones((1, 1), f32))
        dx_ref[...] = dx
        df_ref[...] = df.astype(df_ref.dtype)

        @pl.when(i == 0)
        def _():
            dg_ref[...] = jnp.zeros_like(dg_ref)
            dw_ref[...] = jnp.zeros_like(dw_ref)
            l_ref[...] = jnp.zeros_like(l_ref)

        dg_ref[...] += dg
        dw_ref[...] += dw
        l_ref[...] += jnp.broadcast_to(lv, l_ref.shape)

    row = pl.BlockSpec((tt, d), lambda i: (i, 0))
    vec = pl.BlockSpec((1, d), lambda i: (0, 0))
    return pl.pallas_call(
        body, grid=(rows // tt,),
        in_specs=[row, row, row, vec, vec],
        out_specs=[row, row, vec, vec, pl.BlockSpec((1, LANES), lambda i: (0, 0))],
        out_shape=[jax.ShapeDtypeStruct((rows, d), f32), jax.ShapeDtypeStruct((rows, d), bf16)]
        + [jax.ShapeDtypeStruct((1, d), f32)] * 2
        + [jax.ShapeDtypeStruct((1, LANES), f32)],
        compiler_params=pltpu.CompilerParams(dimension_semantics=("arbitrary",)),
        name="loss_head",
    )(x1, fo, tgt, gate2, nfw)


def _ada_fwd(c_all, w_shard, b_cols):
    def body(c_ref, w_ref, b_ref, cond_ref, mod_ref):
        cond = _silu(c_ref[...])
        cond_ref[...] = cond
        mod_ref[...] = jnp.dot(cond.astype(bf16), w_ref[...].astype(bf16), preferred_element_type=f32) + b_ref[...]

    n = w_shard.shape[1]
    return pl.pallas_call(
        body, out_shape=[jax.ShapeDtypeStruct(c_all.shape, f32), jax.ShapeDtypeStruct((c_all.shape[0], n), f32)],
        name="ada_fwd",
    )(c_all, w_shard, b_cols)


def _adamw(name, w, g, m, v):
    rows, width = w.shape
    tt = _pick(rows, 256, 8)
    c1 = 1.0 - ADAM_B1 ** ADAM_STEP
    c2 = 1.0 - ADAM_B2 ** ADAM_STEP

    def body(w_ref, g_ref, m_ref, v_ref, d_ref, mo_ref, vo_ref):
        gg = g_ref[...]
        mn = ADAM_B1 * m_ref[...] + (1.0 - ADAM_B1) * gg
        vn = ADAM_B2 * v_ref[...] + (1.0 - ADAM_B2) * (gg * gg)
        m_hat = mn / c1
        v_hat = vn / c2
        d_ref[...] = -ADAM_LR * (m_hat / (jnp.sqrt(v_hat) + ADAM_EPS) + ADAM_WD * w_ref[...])
        mo_ref[...] = mn
        vo_ref[...] = vn

    spec = pl.BlockSpec((tt, width), lambda i: (i, 0))
    return pl.pallas_call(
        body, grid=(rows // tt,), in_specs=[spec] * 4, out_specs=[spec] * 3,
        out_shape=[jax.ShapeDtypeStruct((rows, width), f32)] * 3,
        compiler_params=pltpu.CompilerParams(dimension_semantics=("parallel",)),
        name=name,
    )(w, g, m, v)


def _place():
    return lax.axis_index("x"), lax.axis_index("y"), lax.axis_index("c")


def _ag8(name, blk):
    m, w = blk.shape

    def body(x_ref, out_ref, send_sems, recv_sems, local_sem):
        x, y, c = _place()
        me, sibling = (x, y, c), (x, y, 1 - c)
        chips = _other_chips(x, y)

        def slot(px, py, pc):
            return out_ref.at[4 * px + 2 * py + pc]

        def copy(k, block, to, src=None):
            return pltpu.make_async_remote_copy(src_ref=slot(*block) if src is None else src, dst_ref=slot(*block),
                                                send_sem=send_sems.at[k], recv_sem=recv_sems.at[k], device_id=to,
                                                device_id_type=MESH_ID)

        mine = pltpu.make_async_copy(x_ref, slot(*me), local_sem)
        mine.start()
        first = [copy(0, me, sibling, src=x_ref)] + [copy(1 + j, me, (*chip, c), src=x_ref) for j, chip in enumerate(chips)]
        for cp in first:
            cp.start()
        passed = [copy(4 + j, (*chip, c), sibling) for j, chip in enumerate(chips)]
        for j, chip in enumerate(chips):
            copy(1 + j, (*chip, c), me).wait_recv()
            passed[j].start()
        copy(0, sibling, me).wait_recv()
        for j, chip in enumerate(chips):
            copy(4 + j, (*chip, 1 - c), me).wait_recv()
        for cp in first + passed:
            cp.wait_send()
        mine.wait()

    return pl.pallas_call(
        body, out_shape=jax.ShapeDtypeStruct((8, m, w), blk.dtype),
        in_specs=[pl.BlockSpec(memory_space=pltpu.VMEM)], out_specs=pl.BlockSpec(memory_space=pltpu.VMEM),
        scratch_shapes=[pltpu.SemaphoreType.DMA((7,)), pltpu.SemaphoreType.DMA((7,)), pltpu.SemaphoreType.DMA],
        name=name,
    )(blk)


def _ag8_job(blk):
    def plan(x_refs, out_refs, send_sems, recv_sems):
        x_ref, out_ref = x_refs[0], out_refs[0]
        x, y, c = _place()
        me, sibling = (x, y, c), (x, y, 1 - c)
        chips = _other_chips(x, y)
        slot = lambda px, py, pc: out_ref.at[4 * px + 2 * py + pc]
        cp = lambda k, block, to, src=None: _rcopy(slot(*block) if src is None else src, slot(*block), send_sems, recv_sems, k, to)
        first = [cp(0, me, sibling, x_ref)] + [cp(1 + j, me, (*chip, c), x_ref) for j, chip in enumerate(chips)]
        landed = [cp(1 + j, (*chip, c), me) for j, chip in enumerate(chips)]
        passed = [cp(4 + j, (*chip, c), sibling) for j, chip in enumerate(chips)]
        handed = [cp(0, sibling, me)] + [cp(4 + j, (*chip, 1 - c), me) for j, chip in enumerate(chips)]
        mine = pltpu.make_async_copy(x_ref, slot(*me), send_sems.at[7])
        return first, landed, passed, handed, mine

    def start(*refs):
        first, _, _, _, mine = plan(*refs)
        mine.start()
        for cp in first:
            cp.start()

    def forward(*refs):
        _, landed, passed, _, _ = plan(*refs)
        for arrived, fw in zip(landed, passed):
            arrived.wait_recv()
            fw.start()

    def finish(*refs):
        first, _, passed, handed, mine = plan(*refs)
        for cp in handed:
            cp.wait_recv()
        for cp in first + passed:
            cp.wait_send()
        mine.wait()

    return dict(ins=[blk], out_shapes=[jax.ShapeDtypeStruct((8,) + blk.shape, blk.dtype)], nsem=8, start=start,
                forward=forward, finish=finish)


def _other_chips(x, y):
    return [(1 - x, y), (x, 1 - y), (1 - x, 1 - y)]


_ANY = pl.BlockSpec(memory_space=pl.ANY)


def _rcopy(src, dst, send_sems, recv_sems, k, dev):
    return pltpu.make_async_remote_copy(src_ref=src, dst_ref=dst, send_sem=send_sems.at[k], recv_sem=recv_sems.at[k],
                                        device_id=dev, device_id_type=MESH_ID)


def _run_job(name, job):
    j_ins, j_outs, j_sems = _job_parts(job)
    n = len(j_ins)

    def body(*refs):
        for phase in _JOB_PHASES:
            _job_phase(job, phase, refs[:n], refs[n:n + len(j_outs)], refs[n + len(j_outs):])

    return pl.pallas_call(body, out_shape=j_outs, in_specs=[_ANY] * n, out_specs=[_ANY] * len(j_outs),
                          scratch_shapes=j_sems, name=name)(*j_ins)


def _ag4_job(ws):
    n = len(ws)

    def plan(w_refs, out_refs, send_sems, recv_sems):
        x, y, c = _place()
        chip = 2 * x + y
        sibling = (x, y, 1 - c)
        chips = _other_chips(x, y)
        mine = [pl.ds(c * (w.shape[0] // 2), w.shape[0] // 2) for w in ws]
        other = [pl.ds((1 - c) * (w.shape[0] // 2), w.shape[0] // 2) for w in ws]
        rc = lambda src, dst, k, dev: _rcopy(src, dst, send_sems, recv_sems, k, dev)
        first = [rc(w_refs[t].at[mine[t]], out_refs[t].at[chip, mine[t]], 7 * t + k, (px, py, c))
                 for t in range(n) for k, (px, py) in enumerate(chips)]
        own = [rc(w_refs[t], out_refs[t].at[chip], 7 * t + 6, sibling) for t in range(n)]
        landed = [rc(out_refs[t].at[2 * px + py, mine[t]], out_refs[t].at[2 * px + py, mine[t]], 7 * t + k, (px, py, c))
                  for t in range(n) for k, (px, py) in enumerate(chips)]
        forward = [rc(out_refs[t].at[2 * px + py, mine[t]], out_refs[t].at[2 * px + py, mine[t]], 7 * t + 3 + k, sibling)
                   for t in range(n) for k, (px, py) in enumerate(chips)]
        handed = [rc(out_refs[t].at[2 * px + py, other[t]], out_refs[t].at[2 * px + py, other[t]], 7 * t + 3 + k, sibling)
                  for t in range(n) for k, (px, py) in enumerate(chips)]
        return first, own, landed, forward, handed

    def start(*refs):
        first, own, _, _, _ = plan(*refs)
        for cp in first + own:
            cp.start()

    def forward(*refs):
        _, _, landed, passed, _ = plan(*refs)
        for arrived, fw in zip(landed, passed):
            arrived.wait_recv()
            fw.start()

    def finish(*refs):
        first, own, _, passed, handed = plan(*refs)
        for cp in handed + own:
            cp.wait_recv()
        for cp in first + own + passed:
            cp.wait_send()

    return dict(ins=ws, out_shapes=[jax.ShapeDtypeStruct((4,) + w.shape, w.dtype) for w in ws], nsem=7 * n,
                start=start, forward=forward, finish=finish)


def _sibling_swap_job(gs):
    n = len(gs)

    def plan(g_refs, recv_refs, send_sems, recv_sems):
        x, y, c = _place()
        return [_rcopy(g_refs[t].at[s_, pl.ds((1 - c) * (gs[t].shape[1] // 2), gs[t].shape[1] // 2)], recv_refs[t].at[s_],
                       send_sems, recv_sems, 4 * t + s_, (x, y, 1 - c)) for t in range(n) for s_ in range(4)]

    def start(*refs):
        for cp in plan(*refs):
            cp.start()

    def finish(*refs):
        copies = plan(*refs)
        for cp in copies:
            cp.wait_recv()
        for cp in copies:
            cp.wait_send()

    return dict(ins=gs, out_shapes=[jax.ShapeDtypeStruct((4, g.shape[1] // 2, g.shape[2]), g.dtype) for g in gs],
                nsem=4 * n, start=start, finish=finish)


def _chip_exchange_job(ps):
    n = len(ps)

    def plan(p_refs, recv_refs, send_sems, recv_sems):
        x, y, c = _place()
        return [_rcopy(p_refs[t].at[2 * px + py], recv_refs[t].at[k], send_sems, recv_sems, 3 * t + k, (px, py, c))
                for t in range(n) for k, (px, py) in enumerate(_other_chips(x, y))]

    def start(*refs):
        for cp in plan(*refs):
            cp.start()

    def finish(*refs):
        copies = plan(*refs)
        for cp in copies:
            cp.wait_recv()
        for cp in copies:
            cp.wait_send()

    return dict(ins=ps, out_shapes=[jax.ShapeDtypeStruct((3,) + p.shape[1:], p.dtype) for p in ps], nsem=3 * n,
                start=start, finish=finish)


_JOIN_PIECES = 4


def _rs_sibling_join(qs):
    n = len(qs)
    npc = _JOIN_PIECES

    def body(*refs):
        q_refs, out_refs = refs[:n], refs[n:2 * n]
        send_sems, recv_sems = refs[2 * n:]
        x, y, c = _place()
        copies = []
        for t in range(n):
            rh = qs[t].shape[0] // 2
            pr = rh // npc
            for i in range(npc):
                rows = pl.ds(c * rh + i * pr, pr)
                cp = _rcopy(q_refs[t].at[rows], out_refs[t].at[rows], send_sems, recv_sems, npc * t + i,
                            (x, y, 1 - c))
                cp.start()
                copies.append(cp)
        for t in range(n):
            rh = qs[t].shape[0] // 2
            pr = rh // npc
            for i in range(npc):
                rows = pl.ds((1 - c) * rh + i * pr, pr)
                _rcopy(q_refs[t].at[rows], out_refs[t].at[rows], send_sems, recv_sems, npc * t + i,
                       (x, y, 1 - c)).wait_recv()
        for cp in copies:
            cp.wait_send()

    return pl.pallas_call(
        body, out_shape=[jax.ShapeDtypeStruct(q.shape, q.dtype) for q in qs],
        in_specs=[_ANY] * n, out_specs=[_ANY] * n, input_output_aliases={t: t for t in range(n)},
        scratch_shapes=[pltpu.SemaphoreType.DMA((npc * n,)), pltpu.SemaphoreType.DMA((npc * n,))],
        name="grads_sibling_join",
    )(*qs)


def _add_half(name, g, recv, chip, ci):
    S, r, w = g.shape
    rh = r // 2
    tt = _pick(rh, 256, 16)
    nb = rh // tt

    def body(chip_ref, core_ref, a_ref, b_ref, own_ref, ob_ref):
        v = a_ref[...] + b_ref[...]
        ob_ref[...] = v.astype(bf16)

        @pl.when(pl.program_id(1) == chip_ref[0])
        def _():
            own_ref[...] = v[0]

    grid_spec = pltpu.PrefetchScalarGridSpec(
        num_scalar_prefetch=2, grid=(nb, S),
        in_specs=[pl.BlockSpec((1, tt, w), lambda i, s_, ch, co: (s_, co[0] * nb + i, 0)),
                  pl.BlockSpec((1, tt, w), lambda i, s_, ch, co: (s_, i, 0))],
        out_specs=[pl.BlockSpec((tt, w), lambda i, s_, ch, co: (i, 0)),
                   pl.BlockSpec((1, tt, w), lambda i, s_, ch, co: (s_, i, 0))])
    return pl.pallas_call(body, grid_spec=grid_spec,
                          out_shape=[jax.ShapeDtypeStruct((rh, w), f32), jax.ShapeDtypeStruct((S, rh, w), bf16)],
                          compiler_params=pltpu.CompilerParams(dimension_semantics=("arbitrary", "arbitrary")),
                          name=name)(chip.reshape(1).astype(jnp.int32), ci.reshape(1).astype(jnp.int32), g, recv)


def _sum_chip(name, own, others, ci):
    rh, w = own.shape
    tt = _pick(rh, 256, 16)
    nb = rh // tt

    def body(core_ref, a_ref, b_ref, o_ref):
        o_ref[...] = ((a_ref[...] + b_ref[0].astype(f32)) + b_ref[1].astype(f32)) + b_ref[2].astype(f32)

    grid_spec = pltpu.PrefetchScalarGridSpec(
        num_scalar_prefetch=1, grid=(nb,),
        in_specs=[pl.BlockSpec((tt, w), lambda i, co: (i, 0)),
                  pl.BlockSpec((3, tt, w), lambda i, co: (0, i, 0))],
        out_specs=pl.BlockSpec((tt, w), lambda i, co: (co[0] * nb + i, 0)))
    return pl.pallas_call(body, grid_spec=grid_spec, out_shape=jax.ShapeDtypeStruct((2 * rh, w), f32),
                          name=name)(ci.reshape(1).astype(jnp.int32), own, others)


def _sum_devices(gathered, late, late_rows, head_row):
    _, rows, width = gathered.shape

    def body(g_ref, l_ref, out_ref, head_ref):
        acc, acc_l = g_ref[0], l_ref[0]
        for d in range(1, 8):
            acc = acc + g_ref[d]
            acc_l = acc_l + l_ref[d]
        out_ref[...] = acc
        for k, r in enumerate(late_rows):
            out_ref[r:r + 1, :] = acc_l[k:k + 1, :]
        row = acc[head_row:head_row + 1, :]
        hs = row[:, 0:HEAD_DIM]
        for h in range(1, HEADS):
            hs = hs + row[:, h * HEAD_DIM:(h + 1) * HEAD_DIM]
        head_ref[...] = jnp.zeros_like(head_ref)
        head_ref[0:1, 0:HEAD_DIM] = hs

    return pl.pallas_call(
        body, out_shape=[jax.ShapeDtypeStruct((rows, width), f32), jax.ShapeDtypeStruct((8, LANES), f32)],
        name="small_grads_sum",
    )(gathered, late)


def _pack(arrs, rows_mult, dtype):
    flat = jnp.concatenate([a.reshape(-1).astype(dtype) for a in arrs])
    per = PACK_W * rows_mult
    total = -(-flat.shape[0] // per) * per
    return jnp.pad(flat, (0, total - flat.shape[0])).reshape(total // PACK_W, PACK_W)


def _unpack(buf, shapes):
    flat = buf.reshape(-1)
    out, off = [], 0
    for s in shapes:
        n = int(np.prod(s))
        out.append(flat[off:off + n].reshape(s))
        off += n
    return out


_BIG = ["w_in", "w_branch_gdn", "w_branch_rwkv", "w_out", "w_ffn_in", "w_ffn_out"]
_MID = ["conv_gdn", "conv_ffn", "w2", "a2", "g2"]
_SMALL =["b_ada", "norm1_w", "a_log", "dt_bias", "onorm_gdn", "mu_rwkv", "w0", "a0", "k_k", "k_a", "r_k", "lnx_w",
          "lnx_b", "norm2_w", "norm_f_w"]
_ORDER = ["w_ada", "b_ada", "norm1_w", "w_in", "conv_gdn", "a_log", "dt_bias", "onorm_gdn", "w_branch_gdn", "mu_rwkv",
          "w0", "w2", "a0", "a2", "g2", "k_k", "k_a", "r_k", "lnx_w", "lnx_b", "w_branch_rwkv", "w_out", "norm2_w",
          "w_ffn_in", "conv_ffn", "w_ffn_out", "norm_f_w"]


_WIN_SEGMENTS = [(0, 1536, 0), (2064, 3600, 1536), (1536, 2048, 3072), (3600, 3728, 3584), (2048, 2064, 3712),
                 (3728, 3888, 3840), (3888, 5936, 4096)]
_WIN_PADDED = 6144
_COL_QKV, _COL_RKV, _COL_Z = (0, 1, 2), (3, 4, 5), 6
_COL_LORA, _COL_BA = 28, 29
_COL_GATE_LORA = 15
_COL_GL = (4, 5)
_JOINT_QKV, _JOINT_RKV, _JOINT_Z, _JOINT_GL = 0, 1, 6, 2
_SMALL_BLOCKS_AT = 3584


def _win_pad(shards):
    n = shards.shape[2]
    parts, at = [], 0
    for lo, hi, dst in _WIN_SEGMENTS:
        if dst > at:
            parts.append(jnp.zeros((shards.shape[1], dst - at), shards.dtype))
        c = lo
        while c < hi:
            j = c // n
            e = min(hi, (j + 1) * n)
            parts.append(shards[j][:, c - j * n:e - j * n])
            c = e
        at = dst + hi - lo
    if at < _WIN_PADDED:
        parts.append(jnp.zeros((shards.shape[1], _WIN_PADDED - at), shards.dtype))
    return jnp.concatenate(parts, axis=1)


def _win_unpad_shards(g, n):
    shards = []
    for j in range(4):
        parts = []
        for lo, hi, dst in sorted(_WIN_SEGMENTS):
            a, b = max(lo, j * n), min(hi, (j + 1) * n)
            if a < b:
                parts.append(g[:, dst + a - lo:dst + b - lo])
        shards.append(jnp.concatenate(parts, axis=1))
    return jnp.stack(shards)


def kernel(x, c, w_ada, b_ada, norm1_w, w_in, conv_gdn, a_log, dt_bias, onorm_gdn, w_branch_gdn, mu_rwkv, w0, w2, a0, a2, g2, k_k, k_a, r_k, lnx_w, lnx_b, w_branch_rwkv, w_out, norm2_w, w_ffn_in, conv_ffn, w_ffn_out, norm_f_w, loss_target, m_w_ada, m_b_ada, m_norm1_w, m_w_in, m_conv_gdn, m_a_log, m_dt_bias, m_onorm_gdn, m_w_branch_gdn, m_mu_rwkv, m_w0, m_w2, m_a0, m_a2, m_g2, m_k_k, m_k_a, m_r_k, m_lnx_w, m_lnx_b, m_w_branch_rwkv, m_w_out, m_norm2_w, m_w_ffn_in, m_conv_ffn, m_w_ffn_out, m_norm_f_w, v_w_ada, v_b_ada, v_norm1_w, v_w_in, v_conv_gdn, v_a_log, v_dt_bias, v_onorm_gdn, v_w_branch_gdn, v_mu_rwkv, v_w0, v_w2, v_a0, v_a2, v_g2, v_k_k, v_k_a, v_r_k, v_lnx_w, v_lnx_b, v_w_branch_rwkv, v_w_out, v_norm2_w, v_w_ffn_in, v_conv_ffn, v_w_ffn_out, v_norm_f_w):
    args = dict(locals())
    W = {n: args[n] for n in _ORDER}
    Mo = {n: args["m_" + n] for n in _ORDER}
    Vo = {n: args["v_" + n] for n in _ORDER}
    shapes = {n: W[n].shape for n in _ORDER}
    sq = lambda a: a.reshape(a.shape[-2:]) if a.ndim == 3 else a.reshape(1, -1)
    row = lambda a: a.reshape(1, -1)

    xi, yi, ci = lax.axis_index("x"), lax.axis_index("y"), lax.axis_index("c")
    dev = 4 * xi + 2 * yi + ci
    chip = 2 * xi + yi

    x2 = x[0]
    tgt = loss_target[0]
    T, D = x2.shape
    tt_l = _pick(T, 512, CHUNK)
    tt_p = _pick(T, 256, CHUNK)
    tt_h = _pick(T, 128, CHUNK)

    small_blk = _pack([c] + [W[n] for n in _MID], 8, f32)
    small_all, win_s = _run_job("gather_c_mid_and_w_in", [_ag8_job(small_blk), _ag4_job([sq(W["w_in"]).astype(bf16)])])
    c_all = small_all[:, 0, :]
    per_chip = small_all[0::2].reshape(4, -1)[:, D:]
    mid_w, at = {}, 0
    for n in _MID:
        nr, nw = shapes[n][1:]
        mid_w[n] = per_chip[:, at:at + nr * nw].reshape(4, nr, nw).transpose(1, 0, 2).reshape(nr, 4 * nw)
        at += nr * nw
    conv_gdn_f, conv_ffn_f = mid_w["conv_gdn"], mid_w["conv_ffn"]
    w2f, a2f, g2f = mid_w["w2"], mid_w["a2"], mid_w["g2"]

    later_weights = _ag4_job([sq(W[n]).astype(bf16) for n in _BIG[1:]])
    win_p = _win_pad(win_s)
    zpad = lambda a, top, bot: jnp.pad(a, ((top, bot), (0, 0)))
    w2p, a2p, g2p = zpad(w2f, 0, 64), zpad(a2f, 64, 0), zpad(g2f, 0, 96)

    ncol = shapes["w_ada"][2]
    b_cols = lax.dynamic_slice(sq(W["b_ada"]), (0, chip * ncol), (1, ncol))
    cond16, mod_cols = _ada_fwd(jnp.pad(c_all, ((0, 8), (0, 0))), sq(W["w_ada"]), b_cols)
    mod_all = _ag8("gather_mod", mod_cols[:8])
    mod_mine = lax.dynamic_slice(mod_all[0::2], (0, dev, 0), (4, 1, ncol)).reshape(1, 4 * ncol)
    shift1, scale1, gate1, shift2, scale2, gate2 = [mod_mine[:, i * D:(i + 1) * D] for i in range(6)]

    seg = _seg_matrix(WIDTH, HEAD_DIM)
    norm1 = [sq(W["norm1_w"]), shift1, scale1]
    h1 = _stage_fwd("norm_mod1", _fn_norm_mod, [_whole(x2)], norm1, [(D, bf16)], tt_l)[0]
    p = _matmul("in_proj", h1, win_p, "nn", tm=2048)

    cgq = [row(conv_gdn_f[j, part * WIDTH:(part + 1) * WIDTH]) for part in range(3) for j in range(4)]
    lane_pad = lambda a: jnp.pad(row(a), ((0, 0), (8, LANES - 16)))
    gdn_pre_ps = cgq + [lane_pad(W["a_log"]), lane_pad(W["dt_bias"]), seg, _chunk_tri(tt_p, CHUNK)]
    gdn_pre_ts = [(p, WIDTH, ci_, (0, 1, 2, 3)) for ci_ in _COL_QKV] + [(p, LANES, _COL_BA, None)]
    q_, k_, v_, beta_t, gc_t = _stage_fwd("gdn_pre", _fn_gdn_pre, gdn_pre_ts, gdn_pre_ps,
                                          [(WIDTH, f32)] * 3 + [(LANES, f32)] * 2, tt_p)
    o_, gdn_hist, gdn_inv, wbg_s, wbr_s, wout_s, wfi_s, wfo_s = _scan_fwd(
        "gdn_scan", _gdn_chunk, [q_, k_, v_], [beta_t, gc_t], job=later_weights)
    wout_f = wout_s.reshape(D, D)
    wfo = wfo_s.reshape(D_FF, D)
    ow512 = jnp.tile(row(W["onorm_gdn"]), (1, HEADS))
    gdn_post_ts = [_whole(o_), (p, WIDTH, _COL_Z, None)]
    ya = _stage_fwd("gdn_post", _fn_gdn_post, gdn_post_ts, [ow512, seg], [(WIDTH, bf16)], tt_l)[0]

    mu = sq(W["mu_rwkv"])
    rw_ps = [mu[:, 0:512], mu[:, 512:1024], mu[:, 1024:1536], mu[:, 1536:1664], jnp.pad(mu[:, 1664:1824], ((0, 0), (0, 96))),
             sq(W["w0"]), w2p, sq(W["a0"]), a2p, g2p, sq(W["k_k"]), sq(W["k_a"]), seg]
    rw_ts = [(p, WIDTH, ci_, (0, 1)) for ci_ in _COL_RKV] + [(p, LANES, _COL_LORA, (0, 1)),
                                                              (p, 256, _COL_GATE_LORA, (0, 1))]
    rw_out = _stage_fwd("rwkv_pre", _fn_rwkv_pre, rw_ts, rw_ps, [(WIDTH, f32)] * 7, tt_p)
    r_, lw_, k2_, vv_, na_, b_, g_ = rw_out
    rw_ins = [r_, lw_, k2_, vv_, na_, b_]
    y_, rw_hist, rw_inv = _scan_fwd("rwkv_scan", _rwkv_chunk, rw_ins)
    rwp_ps = [sq(W["lnx_w"]), sq(W["lnx_b"]), row(W["r_k"]), seg]
    rwp_ts = [_whole(y_), _whole(r_), _whole(k2_), _whole(vv_), _whole(g_)]
    yb = _stage_fwd("rwkv_post", _fn_rwkv_post, rwp_ts, rwp_ps, [(WIDTH, bf16)], tt_l)[0]

    big_a = _matmul("branch_gdn", ya, wbg_s, "nn", shards=4)
    big_b = _matmul("branch_rwkv", yb, wbr_s, "nn", shards=4)
    merge_ts = [(p, D, _COL_GL[0], None), (p, D, _COL_GL[1], None), _whole(big_a), _whole(big_b)]
    merged = _stage_fwd("merge", _fn_merge, merge_ts, [], [(D, bf16)], tt_l)[0]
    mo = _matmul("out_proj", merged, wout_f, "nn")
    norm2 = [gate1, sq(W["norm2_w"]), shift2, scale2]
    x1, h2 = _stage_fwd("resid_norm_mod2", _fn_resid_norm_mod, [_whole(x2), _whole(mo)], norm2, [(D, f32), (D, bf16)], tt_l)
    f = _matmul("ffn_in", h2, wfi_s, "nn", shards=4, tm=2048, tn=1408)
    cg_ps = [row(conv_ffn_f[j]) for j in range(3)]
    cg_ts = [(f, D_FF, 0, (0, 1, 2)), (f, D_FF, 1, None)]
    act = _stage_fwd("convglu", _fn_convglu, cg_ts, cg_ps, [(D_FF, bf16)], tt_h)[0]
    fo = _matmul("ffn_out", act, wfo, "nn", tk=D_FF)

    dx1_a, dfo, dgate2, dnormf, loss_part = _final_stage(x1, fo, tgt, gate2, row(W["norm_f_w"]), tt_l)

    dact = _matmul("d_act", dfo, wfo, "nt", tn=D_FF)
    g_wfo = _matmul("g_ffn_out", act, dfo, "tn", tm=1408, tk=2048)
    _, dcf, df = _stage_bwd("convglu_bwd", _fn_convglu, cg_ts, cg_ps, [[_whole(dact)]], tt_h, [True] * 2, [True] * 3,
                            joint=([0, 1], 2 * D_FF, 0, None), dtypes=[bf16])
    g_wfi = _matmul("g_ffn_in", h2, df, "tn", shards=4, tn=1408, tk=2048)
    g_wfo_s = g_wfo.reshape(4, D_FF // 4, D)
    dh2, recv_wfi, recv_wfo = _matmul("d_h2", df, wfi_s, "nt", shards=4, tm=2048, tk=1408,
                                      job=_sibling_swap_job([g_wfi, g_wfo_s]))
    (dx_a, dmo), (dgate1, dnorm2, dshift2, dscale2), _ = _stage_bwd(
        "resid_norm_mod2_bwd", _fn_resid_norm_mod, [_whole(x2), _whole(mo)], norm2,
        [[_whole(dx1_a)], [_whole(dh2)]], tt_l, [True, True], [True] * 4, dtypes=[f32, bf16])
    dmerged = _matmul("d_merged", dmo, wout_f, "nt")
    g_wout = _matmul("g_out_proj", merged, dmo, "tn", tk=T)
    (dbig_a, dbig_b), _, dp = _stage_bwd("merge_bwd", _fn_merge, merge_ts, [], [[_whole(dmerged)]], tt_l, [True] * 4, [],
                                         joint=([0, 1], p.shape[1], _JOINT_GL, None), dtypes=[bf16] * 3)
    g_wbg = _matmul("g_branch_gdn", ya, dbig_a, "tn", shards=4, tk=T)
    g_wbr = _matmul("g_branch_rwkv", yb, dbig_b, "tn", shards=4, tk=T)
    g_wout_s = g_wout.reshape(4, D // 4, D)
    dyb = _matmul("d_yb", dbig_b, wbr_s, "nt", shards=4)
    dya, *recv_mix = _matmul("d_ya", dbig_a, wbg_s, "nt", shards=4, job=_sibling_swap_job([g_wbg, g_wbr, g_wout_s]))

    (dy_, dr_p, dk2_p, dv_p, dg_p), (dlnxw, dlnxb, drk), _ = _stage_bwd(
        "rwkv_post_bwd", _fn_rwkv_post, rwp_ts, rwp_ps, [[_whole(dyb)]], tt_l, [True] * 5, [True, True, True, False])
    gs_a = [g_wbg, g_wbr, g_wout_s, g_wfi, g_wfo_s]
    pairs_a = [_add_half("grads_pair_sum%d" % (t + 1), g, r_, chip, ci)
               for t, (g, r_) in enumerate(zip(gs_a, recv_mix + [recv_wfi, recv_wfo]))]
    dr_c, dlw_c, dk2_c, dv_c, dna_c, db_c, *others_a = _scan_bwd(
        "rwkv_scan_bwd", _rwkv_chunk, rw_ins, [], rw_hist, rw_inv, dy_, job=_chip_exchange_job([pb for _, pb in pairs_a]))
    rw_cots = [[_whole(dr_p), _whole(dr_c)], [_whole(dlw_c)], [_whole(dk2_p), _whole(dk2_c)],
               [_whole(dv_p), _whole(dv_c)], [_whole(dna_c)], [_whole(db_c)], [_whole(dg_p)]]
    (dl_, dg_), rw_dp, dp = _stage_bwd("rwkv_pre_bwd", _fn_rwkv_pre, rw_ts, rw_ps, rw_cots, tt_p, [True] * 5,
                                       [True] * 12 + [False], joint=([0, 1, 2], p.shape[1], _JOINT_RKV, dp), dtypes=[bf16] * 3)
    dmu_r, dmu_k, dmu_v, dmu_l, dmu_g, dw0, dw2p, da0, da2p, dg2p, dkk, dka = rw_dp

    (do_,), (dow512,), dp = _stage_bwd("gdn_post_bwd", _fn_gdn_post, gdn_post_ts, [ow512, seg], [[_whole(dya)]], tt_l,
                                       [True, True], [True, False], joint=([1], p.shape[1], _JOINT_Z, dp), dtypes=[f32, bf16])
    d_gdn = _scan_bwd("gdn_scan_bwd", _gdn_chunk, [q_, k_, v_], [beta_t, gc_t], gdn_hist, gdn_inv, do_)
    gdn_cots = [[_whole(a)] for a in d_gdn]
    (dba,), gdn_dp, dp = _stage_bwd("gdn_pre_bwd", _fn_gdn_pre, gdn_pre_ts, gdn_pre_ps, gdn_cots, tt_p, [True] * 4,
                                    [True] * 14 + [False, False], joint=([0, 1, 2], p.shape[1], _JOINT_QKV, dp), dtypes=[bf16] * 2)
    dp = lax.dynamic_update_slice(dp, jnp.concatenate([dl_, dba, dg_], axis=1), (0, _SMALL_BLOCKS_AT))
    g_conv_gdn = jnp.concatenate([jnp.concatenate([gdn_dp[4 * part + j] for part in range(3)], axis=1) for j in range(4)], axis=0)
    g_conv_ffn = jnp.concatenate(dcf, axis=0)
    g_mu = jnp.concatenate([dmu_r, dmu_k, dmu_v, dmu_l, dmu_g[:, :160]], axis=1)
    late_zero = jnp.zeros((1, D), f32)
    dmod_early = jnp.concatenate([late_zero, late_zero, dgate1, dshift2, dscale2, dgate2], axis=1)
    small_parts = {"b_ada": dmod_early, "norm1_w": late_zero, "a_log": gdn_dp[12][:, 8:16], "dt_bias": gdn_dp[13][:, 8:16],
                   "mu_rwkv": g_mu, "w0": dw0, "a0": da0, "k_k": dkk, "k_a": dka, "r_k": drk, "lnx_w": dlnxw,
                   "lnx_b": dlnxb, "norm2_w": dnorm2, "norm_f_w": dnormf}
    small_names = [n for n in _SMALL if n != "onorm_gdn"]
    mid_full = [g_conv_gdn, g_conv_ffn, dw2p[0:64], da2p[64:128], dg2p[0:160]]
    body_rows = _pack([small_parts[n] for n in small_names] + [loss_part[:, 0:1]] + mid_full, 1, f32)
    head_row = body_rows.shape[0]
    small_g = jnp.concatenate([body_rows, jnp.pad(dow512, ((0, 0), (0, PACK_W - WIDTH)))], axis=0)
    small_g = jnp.pad(small_g, ((0, -small_g.shape[0] % 8), (0, 0)))

    g_win_pad, small_all_g = _matmul("g_in_proj", h1, dp, "tn", tk=T, job=_ag8_job(small_g))
    g_win_s = _win_unpad_shards(g_win_pad, shapes["w_in"][2])
    pair_win = _add_half("grads_pair_sum0", g_win_s, _run_job("w_in_grads_sibling_swap", _sibling_swap_job([g_win_s]))[0],
                         chip, ci)
    dh1, others_win = _matmul("d_h1", dp, win_p, "nt", tk=3072, job=_chip_exchange_job([pair_win[1]]))
    (grad_x,), (dnorm1, dshift1, dscale1), _ = _stage_bwd("norm_mod1_bwd", _fn_norm_mod_and_x, [_whole(x2)], norm1,
                                                          [[_whole(dh1)], [_whole(dx_a)]], tt_l, [True], [True] * 3)
    late_all = _ag8("gather_late_grads", jnp.pad(jnp.concatenate([dshift1, dscale1, dnorm1], axis=0), ((0, 5), (0, 0))))
    small_sum, head_sum = _sum_devices(small_all_g, late_all, (0, 1, 6), head_row)
    small_shapes = [shapes[n][1:] if n != "norm_f_w" else shapes[n] for n in small_names]
    un = _unpack(small_sum, small_shapes + [(1,)] + [g.shape for g in mid_full])
    small_grads = dict(zip(small_names, un))
    loss = un[len(small_names)].reshape(())
    small_grads["onorm_gdn"] = head_sum[0, 0:HEAD_DIM]
    for n, g in zip(_MID, un[len(small_names) + 1:]):
        wcols = shapes[n][2]
        small_grads[n] = lax.dynamic_slice(g, (0, chip * wcols), (g.shape[0], wcols))

    dmod_all = jnp.concatenate([late_all[:, 0, :], late_all[:, 1, :], small_all_g[:, 2:6, :].reshape(8, 4 * PACK_W)], axis=1)
    dmod_cols = lax.dynamic_slice(dmod_all, (0, chip * ncol), (8, ncol))
    g_wada = _matmul("g_w_ada", cond16, jnp.pad(dmod_cols, ((0, 8), (0, 0))), "tn")

    pairs = [pair_win] + pairs_a
    others = [others_win] + others_a
    halves = [_sum_chip("grads_chip_sum%d" % t, own, o_, ci) for t, ((own, _), o_) in enumerate(zip(pairs, others))]
    big_grads = dict(zip(_BIG, _rs_sibling_join(halves)))

    res = {tag: {} for tag in ("grad", "delta", "new_m", "new_v")}

    def put(n, g, d, m_, v_):
        for tag, val in zip(("grad", "delta", "new_m", "new_v"), (g, d, m_, v_)):
            res[tag][n] = val.reshape(shapes[n])

    for n in _BIG:
        put(n, big_grads[n], *_adamw("adamw_" + n, sq(W[n]), big_grads[n], sq(Mo[n]), sq(Vo[n])))
    put("w_ada", g_wada, *_adamw("adamw_w_ada", sq(W["w_ada"]), g_wada, sq(Mo["w_ada"]), sq(Vo["w_ada"])))
    rest = _SMALL + _MID
    pk = lambda d: _pack([d[n] for n in rest], 8, f32)
    sg = pk(small_grads)
    sm = _adamw("adamw_small", pk(W), sg, pk(Mo), pk(Vo))
    for tag, buf in zip(("grad", "delta", "new_m", "new_v"), (sg,) + tuple(sm)):
        res[tag].update(zip(rest, _unpack(buf, [shapes[n] for n in rest])))
    outs = [loss, grad_x.reshape(x.shape)]
    for tag in ("grad", "delta", "new_m", "new_v"):
        outs += [res[tag][n] for n in _ORDER]
    return tuple(outs)
```

```python
import numpy as np
import jax
import jax.numpy as jnp
from jax import lax
from jax.experimental import pallas as pl
from jax.experimental.pallas import tpu as pltpu

f32 = jnp.float32
bf16 = jnp.bfloat16

LANES = 128
HEADS = 8
HEAD_DIM = 64
WIDTH = HEADS * HEAD_DIM
CHUNK = 64
D_FF = 2816
NORM_EPS = 1e-6
LNX_EPS = 64e-5
PACK_W = 1024
MESH_ID = pl.DeviceIdType.MESH

ADAM_LR, ADAM_B1, ADAM_B2, ADAM_EPS, ADAM_WD, ADAM_STEP = 0.001, 0.9, 0.999, 1e-08, 0.01, 10


def _pick(n, target, mult):
    if n <= target:
        return n
    best = None
    for t in range(mult, target + 1, mult):
        if n % t == 0:
            best = t
    assert best is not None, (n, target, mult)
    return best


def _split_bf16(x, n):
    parts, r = [], x
    for i in range(n):
        p = r.astype(bf16)
        parts.append(p)
        if i + 1 < n:
            r = r - p.astype(f32)
    return parts


def _xdot_r_impl(x, m, n, dims):
    acc = None
    for p in _split_bf16(x, n):
        t = lax.dot_general(p, m, dims, preferred_element_type=f32)
        acc = t if acc is None else acc + t
    return acc


def _make_xdot_r(n):
    nn = (((1,), (0,)), ((), ()))
    nt = (((1,), (1,)), ((), ()))

    @jax.custom_vjp
    def xdot(x, m):
        return _xdot_r_impl(x, m, n, nn)

    def fwd(x, m):
        return _xdot_r_impl(x, m, n, nn), m

    def bwd(m, ct):
        return _xdot_r_impl(ct, m, n, nt), jnp.zeros_like(m)

    xdot.defvjp(fwd, bwd)
    return xdot


_segsum = _make_xdot_r(2)


def _xdot_l_impl(m, x, n, dims):
    acc = None
    for p in _split_bf16(x, n):
        t = lax.dot_general(m, p, dims, preferred_element_type=f32)
        acc = t if acc is None else acc + t
    return acc


@jax.custom_vjp
def _xdot_l(m, x):
    return _xdot_l_impl(m, x, 3, (((1,), (0,)), ((), ())))


def _xdot_l_fwd(m, x):
    return _xdot_l(m, x), m


def _xdot_l_bwd(m, ct):
    return jnp.zeros_like(m), _xdot_l_impl(m, ct, 3, (((0,), (0,)), ((), ())))


_xdot_l.defvjp(_xdot_l_fwd, _xdot_l_bwd)


@jax.custom_vjp
def _bdot(x, w):
    return jnp.dot(x.astype(bf16), w.astype(bf16), preferred_element_type=f32)


def _bdot_fwd(x, w):
    return _bdot(x, w), (x, w)


def _bdot_bwd(res, ct):
    x, w = res
    c = ct.astype(bf16)
    dx = lax.dot_general(c, w.astype(bf16), (((1,), (1,)), ((), ())), preferred_element_type=f32)
    dw = lax.dot_general(x.astype(bf16), c, (((0,), (0,)), ((), ())), preferred_element_type=f32)
    return dx, dw


_bdot.defvjp(_bdot_fwd, _bdot_bwd)


def _silu(x):
    return x * jax.nn.sigmoid(x)


def _softplus(x):
    return jnp.maximum(x, 0.0) + jnp.log(1.0 + jnp.exp(-jnp.abs(x)))


def _rms(x, w, eps):
    return x * lax.rsqrt(jnp.mean(x * x, axis=-1, keepdims=True) + eps) * w


def _seg_matrix(width, seg):
    i = np.arange(width)
    return jnp.asarray((i[:, None] // seg) == (i[None, :] // seg), dtype=bf16)


def _chunk_tri(rows, chunk):
    i = np.arange(rows)
    return jnp.asarray(((i[:, None] // chunk) == (i[None, :] // chunk)) & (i[:, None] >= i[None, :]), dtype=bf16)


HALO = 8


def _full_spec(shape):
    nd = len(shape)
    return pl.BlockSpec(shape, lambda i: (0,) * nd)


def _entry_specs(entries, tt, block_of):
    specs, ops = [], []
    for arr, w, ci, shifts in entries:
        specs.append(pl.BlockSpec((tt, w), lambda i, ci=ci: (block_of(i), ci)))
        ops.append(arr)
        if shifts:
            specs.append(pl.BlockSpec((HALO, w), lambda i, ci=ci: (jnp.maximum(block_of(i) * (tt // HALO) - 1, 0), ci)))
            ops.append(arr)
    return specs, ops


def _load_entries(entries, refs, first):
    tiles, k = [], 0
    for _, w, _, shifts in entries:
        x = refs[k][...].astype(f32)
        k += 1
        if not shifts:
            tiles.append(x)
            continue
        halo = jnp.where(first, 0.0, refs[k][...].astype(f32))
        k += 1
        row = lax.broadcasted_iota(jnp.int32, (HALO, w), 0)
        for s in shifts:
            if s == 0:
                tiles.append(x)
                continue
            r = pltpu.roll(x, s, 0)
            head = jnp.where(row < s, pltpu.roll(halo, s, 0), r[0:HALO])
            tiles.append(jnp.concatenate([head, r[HALO:]], axis=0))
    return tiles


def _unshift_sum(grads, shifts, carry, tt):
    w = grads[0].shape[1]
    row = lax.broadcasted_iota(jnp.int32, (tt, w), 0)
    row8 = lax.broadcasted_iota(jnp.int32, (HALO, w), 0)
    dx, out = None, jnp.zeros((HALO, w), f32)
    for d, s in zip(grads, shifts):
        if s == 0:
            part = d
        else:
            part = jnp.where(row < tt - s, pltpu.roll(d, tt - s, 0), 0.0)
            out = out + jnp.where(row8 >= HALO - s, pltpu.roll(d[0:HALO], HALO - s, 0), 0.0)
        dx = part if dx is None else dx + part
    return jnp.concatenate([dx[:tt - HALO], dx[tt - HALO:] + carry], axis=0), out


def _stage_fwd(name, fn, tiles, params, outs, tt):
    rows = tiles[0][0].shape[0]
    npar = len(params)
    specs, ops = _entry_specs(tiles, tt, lambda i: i)
    nin = len(ops)

    def body(*refs):
        ts = _load_entries(tiles, refs[:nin], pl.program_id(0) == 0)
        ps = [r[...] for r in refs[nin:nin + npar]]
        res = fn(ps, ts)
        for r, v in zip(refs[nin + npar:], res):
            r[...] = v.astype(r.dtype)

    return pl.pallas_call(
        body, grid=(rows // tt,),
        in_specs=specs + [_full_spec(p.shape) for p in params],
        out_specs=[pl.BlockSpec((tt, w), lambda i: (i, 0)) for (w, _) in outs],
        out_shape=[jax.ShapeDtypeStruct((rows, w), dt) for (w, dt) in outs],
        compiler_params=pltpu.CompilerParams(dimension_semantics=("parallel",)),
        name=name,
    )(*ops, *params)


def _stage_bwd(name, fn, tiles, params, cots, tt, tile_grad, param_grad, joint=None, dtypes=None):
    rows = tiles[0][0].shape[0]
    nblk = rows // tt
    npar = len(params)
    block_of = lambda i: nblk - 1 - i
    specs, ops = _entry_specs(tiles, tt, block_of)
    nin = len(ops)
    flat_cots = [c for group in cots for c in group]
    groups = [len(g) for g in cots]
    ncot = len(flat_cots)
    counts = [len(e[3]) if e[3] else 1 for e in tiles]
    dt_entries = [e for e, g in zip(tiles, tile_grad) if g]
    dp_shapes = [p.shape for p, g in zip(params, param_grad) if g]
    ndt = len(dt_entries)
    carry_w = [e[1] for e in dt_entries if e[3]]
    flags = [g for g, n in zip(tile_grad, counts) for _ in range(n)]
    members, j_width, j_cidx, j_buf = joint if joint else ([], 0, 0, None)
    solo = [k for k in range(ndt) if k not in members]
    nsolo, njoint, nbuf = len(solo), int(bool(members)), int(j_buf is not None)
    j_block = sum(dt_entries[k][1] for k in members)
    dtypes = list(dtypes) if dtypes else [f32] * (nsolo + njoint)

    def body(*refs):
        i = pl.program_id(0)
        p_refs = refs[nin:nin + npar]
        c_refs = refs[nin + npar:nin + npar + ncot]
        base = nin + npar + ncot + nbuf
        dt_refs = refs[base:base + nsolo]
        joint_refs = refs[base + nsolo:base + nsolo + njoint]
        dp_refs = refs[base + nsolo + njoint:base + nsolo + njoint + len(dp_shapes)]
        carry_refs = refs[base + nsolo + njoint + len(dp_shapes):]
        ts = _load_entries(tiles, refs[:nin], block_of(i) == 0)
        ps = [r[...] for r in p_refs]

        def f(dp, dt):
            dp, dt = iter(dp), iter(dt)
            pp = [next(dp) if g else p for p, g in zip(ps, param_grad)]
            tl = [next(dt) if g else t for t, g in zip(ts, flags)]
            return fn(pp, tl)

        _, vjp = jax.vjp(f, [p for p, g in zip(ps, param_grad) if g], [t for t, g in zip(ts, flags) if g])
        cs, j = [], 0
        for n in groups:
            acc = c_refs[j][...].astype(f32)
            for q in range(1, n):
                acc = acc + c_refs[j + q][...].astype(f32)
            cs.append(acc)
            j += n
        gp, gt = vjp(cs)

        @pl.when(i == 0)
        def _():
            for r in dp_refs:
                r[...] = jnp.zeros_like(r)
            for r in carry_refs:
                r[...] = jnp.zeros_like(r)

        gt, k, kc, dxs = list(gt), 0, 0, []
        for e, n in zip(dt_entries, [n for n, g in zip(counts, tile_grad) if g]):
            if e[3]:
                dx, out = _unshift_sum(gt[k:k + n], e[3], carry_refs[kc][...], tt)
                carry_refs[kc][...] = out
                kc += 1
            else:
                dx = gt[k]
            dxs.append(dx)
            k += n
        for r, k in zip(dt_refs, solo):
            r[...] = dxs[k].astype(r.dtype)
        off = 0
        for k in members:
            w = dt_entries[k][1]
            joint_refs[0][:, off:off + w] = dxs[k].astype(joint_refs[0].dtype)
            off += w
        for r, v in zip(dp_refs, gp):
            r[...] += v

    res = pl.pallas_call(
        body, grid=(nblk,),
        in_specs=specs + [_full_spec(p.shape) for p in params]
        + [pl.BlockSpec((tt, w), lambda i, ci=ci: (block_of(i), ci)) for (_, w, ci, *_) in flat_cots]
        + [pl.BlockSpec(memory_space=pl.ANY)] * nbuf,
        out_specs=[pl.BlockSpec((tt, dt_entries[k][1]), lambda i: (block_of(i), 0)) for k in solo]
        + [pl.BlockSpec((tt, j_block), lambda i: (block_of(i), j_cidx))] * njoint
        + [_full_spec(s) for s in dp_shapes],
        out_shape=[jax.ShapeDtypeStruct((rows, dt_entries[k][1]), dt) for k, dt in zip(solo, dtypes)]
        + [jax.ShapeDtypeStruct((rows, j_width), dtypes[-1])] * njoint
        + [jax.ShapeDtypeStruct(s, f32) for s in dp_shapes],
        scratch_shapes=[pltpu.VMEM((HALO, w), f32) for w in carry_w],
        input_output_aliases={nin + npar + ncot: nsolo} if nbuf else {},
        compiler_params=pltpu.CompilerParams(dimension_semantics=("arbitrary",)),
        name=name,
    )(*ops, *params, *[c[0] for c in flat_cots], *([j_buf] if nbuf else []))
    res = list(res)
    return res[:nsolo], res[nsolo + njoint:], (res[nsolo] if njoint else None)


def _whole(a):
    return (a, a.shape[1], 0, None)


def _matmul(name, a, b, mode, out_dtype=f32, tm=1024, tn=1024, tk=1024, shards=1, job=None):
    S = shards
    if mode == "nn":
        M, K = a.shape
        w = b.shape[-1]
    elif mode == "nt":
        M = a.shape[0]
        if S > 1:
            _, N, w = b.shape
            K = S * w
        else:
            N, K = b.shape
            w = K
    else:
        K, M = a.shape
        w = b.shape[1] // S
    if mode != "nt":
        N = S * w
    tm = _pick(M, tm, LANES)
    if mode == "nt":
        tn = _pick(N, tn, LANES)
        tk = _pick(w, tk, LANES)
    else:
        tn = _pick(w, tn, LANES)
        tk = _pick(K, tk, LANES if mode == "nn" else 16)
    nk = K // tk
    nb = w // (tk if mode == "nt" else tn)
    if mode == "nn":
        a_spec = pl.BlockSpec((tm, tk), lambda i, j, k: (i, k))
        if S > 1:
            b_spec = pl.BlockSpec((1, tk, tn), lambda i, j, k: (j // nb, k, j % nb))
        else:
            b_spec = pl.BlockSpec((tk, tn), lambda i, j, k: (k, j))
        dims = (((1,), (0,)), ((), ()))
    elif mode == "nt":
        a_spec = pl.BlockSpec((tm, tk), lambda i, j, k: (i, k))
        if S > 1:
            b_spec = pl.BlockSpec((1, tn, tk), lambda i, j, k: (k // nb, j, k % nb))
        else:
            b_spec = pl.BlockSpec((tn, tk), lambda i, j, k: (j, k))
        dims = (((1,), (1,)), ((), ()))
    else:
        a_spec = pl.BlockSpec((tk, tm), lambda i, j, k: (k, i))
        b_spec = pl.BlockSpec((tk, tn), lambda i, j, k: (k, j))
        dims = (((0,), (0,)), ((), ()))
    if mode == "tn" and S > 1:
        o_spec = pl.BlockSpec((1, tm, tn), lambda i, j, k: (j // nb, i, j % nb))
        o_shape = (S, M, w)
    else:
        o_spec = pl.BlockSpec((tm, tn), lambda i, j, k: (i, j))
        o_shape = (M, N)
    b_lead = S > 1 and mode != "tn"
    o_lead = S > 1 and mode == "tn"

    j_ins, j_outs, j_sems = _job_parts(job)
    nji, njo = len(j_ins), len(j_outs)
    grid = (M // tm, N // tn, nk)

    def run_job(refs):
        step = (pl.program_id(0) * grid[1] + pl.program_id(1)) * grid[2] + pl.program_id(2)
        _job_steps(job, refs[2:2 + nji], refs[3 + nji:3 + nji + njo], refs[len(refs) - len(j_sems):], step,
                   grid[0] * grid[1] * grid[2] - 1)

    def body(*refs):
        a_ref, b_ref, o_ref, acc_ref = refs[0], refs[1], refs[2 + nji], refs[3 + nji + njo]
        run_job(refs)
        k = pl.program_id(2)

        @pl.when(k == 0)
        def _():
            acc_ref[...] = jnp.zeros_like(acc_ref)

        bv = b_ref[0] if b_lead else b_ref[...]
        acc_ref[...] += lax.dot_general(a_ref[...].astype(bf16), bv.astype(bf16), dims, preferred_element_type=f32)

        @pl.when(k == nk - 1)
        def _():
            if o_lead:
                o_ref[0] = acc_ref[...].astype(o_ref.dtype)
            else:
                o_ref[...] = acc_ref[...].astype(o_ref.dtype)

    def body_one_step(*refs):
        a_ref, b_ref, o_ref = refs[0], refs[1], refs[2 + nji]
        run_job(refs)
        bv = b_ref[0] if b_lead else b_ref[...]
        res = lax.dot_general(a_ref[...].astype(bf16), bv.astype(bf16), dims, preferred_element_type=f32)
        if o_lead:
            o_ref[0] = res.astype(o_ref.dtype)
        else:
            o_ref[...] = res.astype(o_ref.dtype)

    res = pl.pallas_call(
        body if nk > 1 else body_one_step, grid=grid,
        in_specs=[a_spec, b_spec] + [_ANY] * nji,
        out_specs=[o_spec] + [_ANY] * njo,
        out_shape=[jax.ShapeDtypeStruct(o_shape, out_dtype)] + j_outs,
        scratch_shapes=([pltpu.VMEM((tm, tn), f32)] if nk > 1 else []) + j_sems,
        compiler_params=pltpu.CompilerParams(
            dimension_semantics=("arbitrary",) * 3 if job else ("parallel", "parallel", "arbitrary")),
        name=name,
    )(a, b, *j_ins)
    return res if job else res[0]


def _make_bmm(precision):
    if precision is None:
        cast, kw = (lambda v: v.astype(bf16)), {}
    else:
        cast, kw = (lambda v: v), {"precision": precision}

    def nn(a, b):
        return jnp.einsum("hij,hjk->hik", cast(a), cast(b), preferred_element_type=f32, **kw)

    def nt(a, b):
        return jnp.einsum("hik,hjk->hij", cast(a), cast(b), preferred_element_type=f32, **kw)

    def tn(a, b):
        return jnp.einsum("hki,hkj->hij", cast(a), cast(b), preferred_element_type=f32, **kw)

    if precision is not None:
        return nn, nt, tn
    nn_v, nt_v, tn_v = jax.custom_vjp(nn), jax.custom_vjp(nt), jax.custom_vjp(tn)
    keep = lambda f: (lambda a, b: (f(a, b), (a, b)))
    nn_v.defvjp(keep(nn), lambda r, ct: (nt(ct, r[1]), tn(r[0], ct)))
    nt_v.defvjp(keep(nt), lambda r, ct: (nn(ct, r[1]), tn(ct, r[0])))
    tn_v.defvjp(keep(tn), lambda r, ct: (nt(r[1], ct), nn(r[0], ct)))
    return nn_v, nt_v, tn_v


_bmm, _bmm_nt, _bmm_tn = _make_bmm(None)
_bmm_exact = _make_bmm(lax.Precision.HIGH)[0]


def _masks(n):
    r = lax.broadcasted_iota(jnp.int32, (n, n), 0)
    c = lax.broadcasted_iota(jnp.int32, (n, n), 1)
    return (r >= c)[None], (r > c)[None], (r == c)[None]


_INV_BLOCK = 8


def _nilpotent_inverse(m, eye):
    p = eye + m
    for _ in range(2):
        m = _bmm(m, m)
        p = p + _bmm(p, m)
    return p


def _neumann_inverse_impl(m):
    n = m.shape[1]
    assert n == _INV_BLOCK * _INV_BLOCK
    r = lax.broadcasted_iota(jnp.int32, (n, n), 0)
    c = lax.broadcasted_iota(jnp.int32, (n, n), 1)
    eye = (r == c).astype(f32)[None]
    inside = jnp.where((r // _INV_BLOCK == c // _INV_BLOCK)[None], m, 0.0)
    d_inv = _nilpotent_inverse(inside, eye)
    return _bmm(_nilpotent_inverse(_bmm(d_inv, m - inside), eye), d_inv)


@jax.custom_vjp
def _neumann_inverse(m):
    return _neumann_inverse_impl(m)


def _neumann_inverse_fwd(m):
    p = _neumann_inverse_impl(m)
    return p, p


def _neumann_inverse_bwd(p, ct):
    return (_bmm_tn(p, _bmm_nt(ct, p)),)


_neumann_inverse.defvjp(_neumann_inverse_fwd, _neumann_inverse_bwd)


@jax.custom_vjp
def _given_inverse(m, p):
    return p


_given_inverse.defvjp(lambda m, p: (p, p), lambda p, ct: (_neumann_inverse_bwd(p, ct)[0], jnp.zeros_like(p)))


def _gdn_chunk(s, q, k, v, beta, gc, gr, gl, p=None):
    n = q.shape[1]
    causal, strict, _ = _masks(n)
    decay = jnp.where(causal, jnp.exp(jnp.where(causal, gc - gr, 0.0)), 0.0)
    kb = k * beta
    vb = v * beta
    lower = jnp.where(strict, _bmm_nt(kb, k) * decay, 0.0)
    t_mat = _neumann_inverse(-lower) if p is None else _given_inverse(-lower, p)
    egc = jnp.exp(gc)
    u = _bmm(t_mat, vb)
    w = _bmm(t_mat, kb * egc)
    attn = jnp.where(causal, _bmm_nt(q, k) * decay, 0.0)
    v_new = u - _bmm(w, s)
    o = _bmm(q * egc, s) + _bmm(attn, v_new)
    k_dec = k * jnp.exp(gl - gc)
    s_new = s * jnp.exp(gl) + _bmm_tn(k_dec, v_new)
    return s_new, o, t_mat


def _rwkv_chunk(s, r, lw, k, v, a, b, p=None):
    n = r.shape[1]
    causal, strict, _ = _masks(n)
    tri = jnp.broadcast_to(causal.astype(f32), (r.shape[0], n, n))
    lc = _bmm_exact(tri, lw)
    ein = jnp.exp(lc)
    eout = jnp.exp(-lc)
    a_t = a * jnp.exp(lc - lw)
    b_t = b * eout
    k_t = k * eout
    r_t = r * ein
    a_ab = jnp.where(strict, _bmm_nt(a_t, b_t), 0.0)
    a_ak = jnp.where(strict, _bmm_nt(a_t, k_t), 0.0)
    inv = _neumann_inverse(a_ab) if p is None else _given_inverse(a_ab, p)
    u = _bmm(inv, _bmm_nt(a_t, s) + _bmm(a_ak, v))
    y = (_bmm_nt(r_t, s) + _bmm(jnp.where(causal, _bmm_nt(r_t, b_t), 0.0), u)
         + _bmm(jnp.where(causal, _bmm_nt(r_t, k_t), 0.0), v))
    e_last = jnp.exp(jnp.sum(lw, axis=1, keepdims=True))
    s_new = s * e_last + _bmm_tn(u, b_t * e_last) + _bmm_tn(v, k_t * e_last)
    return s_new, y, inv


def _heads_in(ref, rows):
    return jnp.stack([ref[rows, h * HEAD_DIM:(h + 1) * HEAD_DIM] for h in range(HEADS)], axis=0)


def _heads_out(ref, rows, val):
    for h in range(HEADS):
        ref[rows, h * HEAD_DIM:(h + 1) * HEAD_DIM] = val[h]


def _gdn_scalars(bt, gt):
    n = bt.shape[0]
    gtt = gt.T
    hs = range(HEADS)
    return [jnp.stack([bt[:, h:h + 1] for h in hs], axis=0),
            jnp.stack([gt[:, HEADS + h:HEADS + h + 1] for h in hs], axis=0),
            jnp.stack([gtt[HEADS + h:HEADS + h + 1, :] for h in hs], axis=0),
            jnp.stack([gt[n - 1:n, HEADS + h:HEADS + h + 1] for h in hs], axis=0)]


def _gdn_scalars_back(dbeta, dgc, dgr, dgl):
    n = dbeta.shape[1]
    lane = lax.broadcasted_iota(jnp.int32, (n, LANES), 1)
    row = lax.broadcasted_iota(jnp.int32, (n, LANES), 0)
    sub = lax.broadcasted_iota(jnp.int32, (LANES, n), 0)
    db = jnp.zeros((n, LANES), f32)
    dg = jnp.zeros((n, LANES), f32)
    dgt = jnp.zeros((LANES, n), f32)
    for h in range(HEADS):
        db = jnp.where(lane == h, dbeta[h], db)
        dg = jnp.where(lane == HEADS + h, dgc[h] + jnp.where(row == n - 1, dgl[h], 0.0), dg)
        dgt = jnp.where(sub == HEADS + h, dgr[h], dgt)
    return [db, dg + dgt.T]


SCAN_GROUP = 2


def _scan_steps(t):
    g = SCAN_GROUP if (t // CHUNK) % SCAN_GROUP == 0 else 1
    return g, t // (CHUNK * g)


def _scan_spec(width, g, n, reverse):
    if reverse:
        return pl.BlockSpec((g * CHUNK, width), lambda i: (n - 1 - i, 0))
    return pl.BlockSpec((g * CHUNK, width), lambda i: (i, 0))


def _hist_spec(g, n, reverse):
    blk = (g, HEADS, HEAD_DIM, HEAD_DIM)
    if reverse:
        return pl.BlockSpec(blk, lambda i: (n - 1 - i, 0, 0, 0))
    return pl.BlockSpec(blk, lambda i: (i, 0, 0, 0))


def _job_list(job):
    return [] if job is None else (list(job) if isinstance(job, (list, tuple)) else [job])


def _job_parts(job):
    jobs = _job_list(job)
    sems = [pltpu.SemaphoreType.DMA((j["nsem"],)) for j in jobs for _ in range(2)]
    return [a for j in jobs for a in j["ins"]], [o for j in jobs for o in j["out_shapes"]], sems


_JOB_PHASES = ("start", "forward", "finish")


def _job_phase(job, phase, in_refs, out_refs, sems):
    ki = ko = 0
    for n, j in enumerate(_job_list(job)):
        ni, no = len(j["ins"]), len(j["out_shapes"])
        if phase in j:
            j[phase](in_refs[ki:ki + ni], out_refs[ko:ko + no], sems[2 * n], sems[2 * n + 1])
        ki, ko = ki + ni, ko + no


def _job_steps(job, in_refs, out_refs, sems, step, last):
    if job is None:
        return
    for phase, at in zip(_JOB_PHASES, (0, max(last - 2, 0), last)):
        @pl.when(step == at)
        def _(phase=phase):
            _job_phase(job, phase, in_refs, out_refs, sems)


def _scan_fwd(name, fn, rows_in, scal_in=(), job=None):
    t = rows_in[0].shape[0]
    grp, n = _scan_steps(t)
    nr, ns = len(rows_in), len(scal_in)
    j_ins, j_outs, j_sems = _job_parts(job)
    nji, njo = len(j_ins), len(j_outs)

    def body(*refs):
        o_ref, sh_ref, ph_ref = refs[nr + ns + nji:nr + ns + nji + 3]
        s_scr = refs[nr + ns + nji + 3 + njo]
        _job_steps(job, refs[nr + ns:nr + ns + nji], refs[nr + ns + nji + 3:nr + ns + nji + 3 + njo],
                   refs[nr + ns + nji + 3 + njo + 1:], pl.program_id(0), n - 1)

        @pl.when(pl.program_id(0) == 0)
        def _():
            s_scr[...] = jnp.zeros_like(s_scr)

        s = s_scr[...]
        for sub in range(grp):
            rows = slice(sub * CHUNK, (sub + 1) * CHUNK)
            sh_ref[sub] = s
            ins = [_heads_in(r, rows) for r in refs[:nr]]
            if ns:
                ins += _gdn_scalars(*[r[rows, :] for r in refs[nr:nr + ns]])
            s, o, p = fn(s, *ins)
            _heads_out(o_ref, rows, o)
            ph_ref[sub] = p
        s_scr[...] = s

    return pl.pallas_call(
        body, grid=(n,),
        in_specs=[_scan_spec(a.shape[1], grp, n, False) for a in (*rows_in, *scal_in)] + [_ANY] * nji,
        out_specs=[_scan_spec(WIDTH, grp, n, False), _hist_spec(grp, n, False), _hist_spec(grp, n, False)] + [_ANY] * njo,
        out_shape=[jax.ShapeDtypeStruct((t, WIDTH), f32)]
        + [jax.ShapeDtypeStruct((t // CHUNK, HEADS, HEAD_DIM, HEAD_DIM), f32)] * 2 + j_outs,
        scratch_shapes=[pltpu.VMEM((HEADS, HEAD_DIM, HEAD_DIM), f32)] + j_sems,
        compiler_params=pltpu.CompilerParams(dimension_semantics=("arbitrary",)),
        name=name,
    )(*rows_in, *scal_in, *j_ins)


def _scan_bwd(name, fn, rows_in, scal_in, s_hist, p_hist, d_out, job=None):
    t = rows_in[0].shape[0]
    grp, n = _scan_steps(t)
    nr, ns = len(rows_in), len(scal_in)
    j_ins, j_outs, j_sems = _job_parts(job)
    nji, njo = len(j_ins), len(j_outs)

    def body(*refs):
        sh_ref, ph_ref, do_ref = refs[nr + ns:nr + ns + 3]
        base = nr + ns + 3 + nji
        g_refs = refs[base:base + nr + ns]
        ds_scr = refs[base + nr + ns + njo]
        _job_steps(job, refs[nr + ns + 3:base], refs[base + nr + ns:base + nr + ns + njo],
                   refs[base + nr + ns + njo + 1:], pl.program_id(0), n - 1)

        @pl.when(pl.program_id(0) == 0)
        def _():
            ds_scr[...] = jnp.zeros_like(ds_scr)

        ds = ds_scr[...]
        for sub in reversed(range(grp)):
            rows = slice(sub * CHUNK, (sub + 1) * CHUNK)
            ins = [_heads_in(r, rows) for r in refs[:nr]]
            if ns:
                ins += _gdn_scalars(*[r[rows, :] for r in refs[nr:nr + ns]])
            p = ph_ref[sub]
            _, vjp = jax.vjp(lambda s, *a, p=p: fn(s, *a, p=p)[:2], sh_ref[sub], *ins)
            g = vjp((ds, _heads_in(do_ref, rows)))
            ds = g[0]
            for r, v in zip(g_refs[:nr], g[1:1 + nr]):
                _heads_out(r, rows, v)
            if ns:
                for r, v in zip(g_refs[nr:], _gdn_scalars_back(*g[1 + nr:])):
                    r[rows, :] = v
        ds_scr[...] = ds

    arrs = (*rows_in, *scal_in)
    return pl.pallas_call(
        body, grid=(n,),
        in_specs=[_scan_spec(a.shape[1], grp, n, True) for a in arrs]
        + [_hist_spec(grp, n, True), _hist_spec(grp, n, True), _scan_spec(WIDTH, grp, n, True)] + [_ANY] * nji,
        out_specs=[_scan_spec(a.shape[1], grp, n, True) for a in arrs] + [_ANY] * njo,
        out_shape=[jax.ShapeDtypeStruct(a.shape, f32) for a in arrs] + j_outs,
        scratch_shapes=[pltpu.VMEM((HEADS, HEAD_DIM, HEAD_DIM), f32)] + j_sems,
        compiler_params=pltpu.CompilerParams(dimension_semantics=("arbitrary",)),
        name=name,
    )(*arrs, s_hist, p_hist, d_out, *j_ins)


def _fn_norm_mod(ps, ts):
    nw, shift, scale = ps
    (x,) = ts
    return [_rms(x, nw, NORM_EPS) * (1.0 + scale) + shift]


def _fn_norm_mod_and_x(ps, ts):
    return _fn_norm_mod(ps, ts) + [ts[0]]


def _fn_resid_norm_mod(ps, ts):
    gate, nw, shift, scale = ps
    x, mo = ts
    x1 = x + gate * mo
    return [x1, _rms(x1, nw, NORM_EPS) * (1.0 + scale) + shift]


def _fn_gdn_pre(ps, ts):
    cw = ps[:12]
    alog, dtb, seg, tri = ps[12:]
    ba = ts[12]
    outs = []
    for part in range(3):
        x = ts[4 * part:4 * part + 4]
        w = cw[4 * part:4 * part + 4]
        conv = w[3] * x[0] + w[2] * x[1] + w[1] * x[2] + w[0] * x[3]
        u = _silu(conv)
        if part < 2:
            u = u * lax.rsqrt(_segsum(u * u, seg) + 1e-6)
            if part == 0:
                u = u * (HEAD_DIM ** -0.5)
        outs.append(u)
    beta = jax.nn.sigmoid(ba)
    g = -jnp.exp(alog) * _softplus(ba + dtb)
    gc = _xdot_l(tri, g)
    return outs + [beta, gc]


def _fn_gdn_post(ps, ts):
    ow, seg = ps
    o, z = ts
    ms = _segsum(o * o, seg) * (1.0 / HEAD_DIM)
    return [o * lax.rsqrt(ms + NORM_EPS) * ow * _silu(z)]


def _fn_rwkv_pre(ps, ts):
    mu_r, mu_k, mu_v, mu_l, mu_g, w0, w2p, a0, a2p, g2p, k_k, k_a, seg = ps
    r0, r1, k0, k1, v0, v1, l0, l1, g0, g1 = ts
    xr = r0 + (r1 - r0) * mu_r
    xk = k0 + (k1 - k0) * mu_k
    xv = v0 + (v1 - v0) * mu_v
    xl = l0 + (l1 - l0) * mu_l
    xg = g0 + (g1 - g0) * mu_g
    w = -_softplus(-(w0 + _bdot(jnp.tanh(xl), w2p))) - 0.5
    lw = -jnp.exp(w)
    a = jax.nn.sigmoid(a0 + _bdot(xl, a2p))
    g = _bdot(jax.nn.sigmoid(xg), g2p)
    kk = xk * k_k
    kk = kk * lax.rsqrt(_segsum(kk * kk, seg) + 1e-6)
    k2 = xk * (1.0 + (a - 1.0) * k_a)
    return [xr, lw, k2, xv, -kk, kk * a, g]


def _fn_rwkv_post(ps, ts):
    lw_, lb_, rk, seg = ps
    y, r, k2, v, g = ts
    inv = 1.0 / HEAD_DIM
    yc = y - _segsum(y, seg) * inv
    var = _segsum(yc * yc, seg) * inv
    yn = yc * lax.rsqrt(var + LNX_EPS) * lw_ + lb_
    bonus = _segsum(r * k2 * rk, seg) * v
    return [(yn + bonus) * g]


def _fn_merge(ps, ts):
    gla, glb, ya, yb = ts
    return [jax.nn.sigmoid(gla) * ya + jax.nn.sigmoid(glb) * yb]


def _fn_convglu(ps, ts):
    c0, c1, c2 = ps
    g0, g1, g2, up = ts
    return [_silu(c2 * g0 + c1 * g1 + c0 * g2) * up]


def _final_stage(x1, fo, tgt, gate2, nfw, tt):
    rows, d = x1.shape

    def loss_fn(gate, nw, xa, fa, tg):
        y = _rms(xa + gate * fa, nw, NORM_EPS)
        err = (y - tg) ** 2
        return 0.5 * jnp.sum(jnp.mean(err, axis=-1, keepdims=True), axis=0, keepdims=True)

    def body(x_ref, f_ref, t_ref, g_ref, w_ref, dx_ref, df_ref, dg_ref, dw_ref, l_ref):
        i = pl.program_id(0)
        args = (g_ref[...], w_ref[...], x_ref[...], f_ref[...])
        tg = t_ref[...]
        lv, vjp = jax.vjp(lambda g, w, xa, fa: loss_fn(g, w, xa, fa, tg), *args)
        dg, dw, dx, df = vjp(jnp.ones((1, 1), f32))
        dx_ref[...] = dx
        df_ref[...] = df.astype(df_ref.dtype)

        @pl.when(i == 0)
        def _():
            dg_ref[...] = jnp.zeros_like(dg_ref)
            dw_ref[...] = jnp.zeros_like(dw_ref)
            l_ref[...] = jnp.zeros_like(l_ref)

        dg_ref[...] += dg
        dw_ref[...] += dw
        l_ref[...] += jnp.broadcast_to(lv, l_ref.shape)

    row = pl.BlockSpec((tt, d), lambda i: (i, 0))
    vec = pl.BlockSpec((1, d), lambda i: (0, 0))
    return pl.pallas_call(
        body, grid=(rows // tt,),
        in_specs=[row, row, row, vec, vec],
        out_specs=[row, row, vec, vec, pl.BlockSpec((1, LANES), lambda i: (0, 0))],
        out_shape=[jax.ShapeDtypeStruct((rows, d), f32), jax.ShapeDtypeStruct((rows, d), bf16)]
        + [jax.ShapeDtypeStruct((1, d), f32)] * 2
        + [jax.ShapeDtypeStruct((1, LANES), f32)],
        compiler_params=pltpu.CompilerParams(dimension_semantics=("arbitrary",)),
        name="loss_head",
    )(x1, fo, tgt, gate2, nfw)


def _ada_fwd(c_all, w_shard, b_cols):
    def body(c_ref, w_ref, b_ref, cond_ref, mod_ref):
        cond = _silu(c_ref[...])
        cond_ref[...] = cond
        mod_ref[...] = jnp.dot(cond.astype(bf16), w_ref[...].astype(bf16), preferred_element_type=f32) + b_ref[...]

    n = w_shard.shape[1]
    return pl.pallas_call(
        body, out_shape=[jax.ShapeDtypeStruct(c_all.shape, f32), jax.ShapeDtypeStruct((c_all.shape[0], n), f32)],
        name="ada_fwd",
    )(c_all, w_shard, b_cols)


def _adamw(name, w, g, m, v):
    rows, width = w.shape
    tt = _pick(rows, 256, 8)
    c1 = 1.0 - ADAM_B1 ** ADAM_STEP
    c2 = 1.0 - ADAM_B2 ** ADAM_STEP

    def body(w_ref, g_ref, m_ref, v_ref, d_ref, mo_ref, vo_ref):
        gg = g_ref[...]
        mn = ADAM_B1 * m_ref[...] + (1.0 - ADAM_B1) * gg
        vn = ADAM_B2 * v_ref[...] + (1.0 - ADAM_B2) * (gg * gg)
        m_hat = mn / c1
        v_hat = vn / c2
        d_ref[...] = -ADAM_LR * (m_hat / (jnp.sqrt(v_hat) + ADAM_EPS) + ADAM_WD * w_ref[...])
        mo_ref[...] = mn
        vo_ref[...] = vn

    spec = pl.BlockSpec((tt, width), lambda i: (i, 0))
    return pl.pallas_call(
        body, grid=(rows // tt,), in_specs=[spec] * 4, out_specs=[spec] * 3,
        out_shape=[jax.ShapeDtypeStruct((rows, width), f32)] * 3,
        compiler_params=pltpu.CompilerParams(dimension_semantics=("parallel",)),
        name=name,
    )(w, g, m, v)


def _place():
    return lax.axis_index("x"), lax.axis_index("y"), lax.axis_index("c")


def _ag8(name, blk):
    m, w = blk.shape

    def body(x_ref, out_ref, send_sems, recv_sems, local_sem):
        x, y, c = _place()
        me, sibling = (x, y, c), (x, y, 1 - c)
        chips = _other_chips(x, y)

        def slot(px, py, pc):
            return out_ref.at[4 * px + 2 * py + pc]

        def copy(k, block, to, src=None):
            return pltpu.make_async_remote_copy(src_ref=slot(*block) if src is None else src, dst_ref=slot(*block),
                                                send_sem=send_sems.at[k], recv_sem=recv_sems.at[k], device_id=to,
                                                device_id_type=MESH_ID)

        mine = pltpu.make_async_copy(x_ref, slot(*me), local_sem)
        mine.start()
        first = [copy(0, me, sibling, src=x_ref)] + [copy(1 + j, me, (*chip, c), src=x_ref) for j, chip in enumerate(chips)]
        for cp in first:
            cp.start()
        passed = [copy(4 + j, (*chip, c), sibling) for j, chip in enumerate(chips)]
        for j, chip in enumerate(chips):
            copy(1 + j, (*chip, c), me).wait_recv()
            passed[j].start()
        copy(0, sibling, me).wait_recv()
        for j, chip in enumerate(chips):
            copy(4 + j, (*chip, 1 - c), me).wait_recv()
        for cp in first + passed:
            cp.wait_send()
        mine.wait()

    return pl.pallas_call(
        body, out_shape=jax.ShapeDtypeStruct((8, m, w), blk.dtype),
        in_specs=[pl.BlockSpec(memory_space=pltpu.VMEM)], out_specs=pl.BlockSpec(memory_space=pltpu.VMEM),
        scratch_shapes=[pltpu.SemaphoreType.DMA((7,)), pltpu.SemaphoreType.DMA((7,)), pltpu.SemaphoreType.DMA],
        name=name,
    )(blk)


def _ag8_job(blk):
    def plan(x_refs, out_refs, send_sems, recv_sems):
        x_ref, out_ref = x_refs[0], out_refs[0]
        x, y, c = _place()
        me, sibling = (x, y, c), (x, y, 1 - c)
        chips = _other_chips(x, y)
        slot = lambda px, py, pc: out_ref.at[4 * px + 2 * py + pc]
        cp = lambda k, block, to, src=None: _rcopy(slot(*block) if src is None else src, slot(*block), send_sems, recv_sems, k, to)
        first = [cp(0, me, sibling, x_ref)] + [cp(1 + j, me, (*chip, c), x_ref) for j, chip in enumerate(chips)]
        landed = [cp(1 + j, (*chip, c), me) for j, chip in enumerate(chips)]
        passed = [cp(4 + j, (*chip, c), sibling) for j, chip in enumerate(chips)]
        handed = [cp(0, sibling, me)] + [cp(4 + j, (*chip, 1 - c), me) for j, chip in enumerate(chips)]
        mine = pltpu.make_async_copy(x_ref, slot(*me), send_sems.at[7])
        return first, landed, passed, handed, mine

    def start(*refs):
        first, _, _, _, mine = plan(*refs)
        mine.start()
        for cp in first:
            cp.start()

    def forward(*refs):
        _, landed, passed, _, _ = plan(*refs)
        for arrived, fw in zip(landed, passed):
            arrived.wait_recv()
            fw.start()

    def finish(*refs):
        first, _, passed, handed, mine = plan(*refs)
        for cp in handed:
            cp.wait_recv()
        for cp in first + passed:
            cp.wait_send()
        mine.wait()

    return dict(ins=[blk], out_shapes=[jax.ShapeDtypeStruct((8,) + blk.shape, blk.dtype)], nsem=8, start=start,
                forward=forward, finish=finish)


def _other_chips(x, y):
    return [(1 - x, y), (x, 1 - y), (1 - x, 1 - y)]


_ANY = pl.BlockSpec(memory_space=pl.ANY)


def _rcopy(src, dst, send_sems, recv_sems, k, dev):
    return pltpu.make_async_remote_copy(src_ref=src, dst_ref=dst, send_sem=send_sems.at[k], recv_sem=recv_sems.at[k],
                                        device_id=dev, device_id_type=MESH_ID)


def _run_job(name, job):
    j_ins, j_outs, j_sems = _job_parts(job)
    n = len(j_ins)

    def body(*refs):
        for phase in _JOB_PHASES:
            _job_phase(job, phase, refs[:n], refs[n:n + len(j_outs)], refs[n + len(j_outs):])

    return pl.pallas_call(body, out_shape=j_outs, in_specs=[_ANY] * n, out_specs=[_ANY] * len(j_outs),
                          scratch_shapes=j_sems, name=name)(*j_ins)


def _ag4_job(ws):
    n = len(ws)

    def plan(w_refs, out_refs, send_sems, recv_sems):
        x, y, c = _place()
        chip = 2 * x + y
        sibling = (x, y, 1 - c)
        chips = _other_chips(x, y)
        mine = [pl.ds(c * (w.shape[0] // 2), w.shape[0] // 2) for w in ws]
        other = [pl.ds((1 - c) * (w.shape[0] // 2), w.shape[0] // 2) for w in ws]
        rc = lambda src, dst, k, dev: _rcopy(src, dst, send_sems, recv_sems, k, dev)
        first = [rc(w_refs[t].at[mine[t]], out_refs[t].at[chip, mine[t]], 7 * t + k, (px, py, c))
                 for t in range(n) for k, (px, py) in enumerate(chips)]
        own = [rc(w_refs[t], out_refs[t].at[chip], 7 * t + 6, sibling) for t in range(n)]
        landed = [rc(out_refs[t].at[2 * px + py, mine[t]], out_refs[t].at[2 * px + py, mine[t]], 7 * t + k, (px, py, c))
                  for t in range(n) for k, (px, py) in enumerate(chips)]
        forward = [rc(out_refs[t].at[2 * px + py, mine[t]], out_refs[t].at[2 * px + py, mine[t]], 7 * t + 3 + k, sibling)
                   for t in range(n) for k, (px, py) in enumerate(chips)]
        handed = [rc(out_refs[t].at[2 * px + py, other[t]], out_refs[t].at[2 * px + py, other[t]], 7 * t + 3 + k, sibling)
                  for t in range(n) for k, (px, py) in enumerate(chips)]
        return first, own, landed, forward, handed

    def start(*refs):
        first, own, _, _, _ = plan(*refs)
        for cp in first + own:
            cp.start()

    def forward(*refs):
        _, _, landed, passed, _ = plan(*refs)
        for arrived, fw in zip(landed, passed):
            arrived.wait_recv()
            fw.start()

    def finish(*refs):
        first, own, _, passed, handed = plan(*refs)
        for cp in handed + own:
            cp.wait_recv()
        for cp in first + own + passed:
            cp.wait_send()

    return dict(ins=ws, out_shapes=[jax.ShapeDtypeStruct((4,) + w.shape, w.dtype) for w in ws], nsem=7 * n,
                start=start, forward=forward, finish=finish)


def _sibling_swap_job(gs):
    n = len(gs)

    def plan(g_refs, recv_refs, send_sems, recv_sems):
        x, y, c = _place()
        return [_rcopy(g_refs[t].at[s_, pl.ds((1 - c) * (gs[t].shape[1] // 2), gs[t].shape[1] // 2)], recv_refs[t].at[s_],
                       send_sems, recv_sems, 4 * t + s_, (x, y, 1 - c)) for t in range(n) for s_ in range(4)]

    def start(*refs):
        for cp in plan(*refs):
            cp.start()

    def finish(*refs):
        copies = plan(*refs)
        for cp in copies:
            cp.wait_recv()
        for cp in copies:
            cp.wait_send()

    return dict(ins=gs, out_shapes=[jax.ShapeDtypeStruct((4, g.shape[1] // 2, g.shape[2]), g.dtype) for g in gs],
                nsem=4 * n, start=start, finish=finish)


def _chip_exchange_job(ps):
    n = len(ps)

    def plan(p_refs, recv_refs, send_sems, recv_sems):
        x, y, c = _place()
        return [_rcopy(p_refs[t].at[2 * px + py], recv_refs[t].at[k], send_sems, recv_sems, 3 * t + k, (px, py, c))
                for t in range(n) for k, (px, py) in enumerate(_other_chips(x, y))]

    def start(*refs):
        for cp in plan(*refs):
            cp.start()

    def finish(*refs):
        copies = plan(*refs)
        for cp in copies:
            cp.wait_recv()
        for cp in copies:
            cp.wait_send()

    return dict(ins=ps, out_shapes=[jax.ShapeDtypeStruct((3,) + p.shape[1:], p.dtype) for p in ps], nsem=3 * n,
                start=start, finish=finish)


_JOIN_PIECES = 4


def _rs_sibling_join(qs):
    n = len(qs)
    npc = _JOIN_PIECES

    def body(*refs):
        q_refs, out_refs = refs[:n], refs[n:2 * n]
        send_sems, recv_sems = refs[2 * n:]
        x, y, c = _place()
        copies = []
        for t in range(n):
            rh = qs[t].shape[0] // 2
            pr = rh // npc
            for i in range(npc):
                rows = pl.ds(c * rh + i * pr, pr)
                cp = _rcopy(q_refs[t].at[rows], out_refs[t].at[rows], send_sems, recv_sems, npc * t + i,
                            (x, y, 1 - c))
                cp.start()
                copies.append(cp)
        for t in range(n):
            rh = qs[t].shape[0] // 2
            pr = rh // npc
            for i in range(npc):
                rows = pl.ds((1 - c) * rh + i * pr, pr)
                _rcopy(q_refs[t].at[rows], out_refs[t].at[rows], send_sems, recv_sems, npc * t + i,
                       (x, y, 1 - c)).wait_recv()
        for cp in copies:
            cp.wait_send()

    return pl.pallas_call(
        body, out_shape=[jax.ShapeDtypeStruct(q.shape, q.dtype) for q in qs],
        in_specs=[_ANY] * n, out_specs=[_ANY] * n, input_output_aliases={t: t for t in range(n)},
        scratch_shapes=[pltpu.SemaphoreType.DMA((npc * n,)), pltpu.SemaphoreType.DMA((npc * n,))],
        name="grads_sibling_join",
    )(*qs)


def _add_half(name, g, recv, chip, ci):
    S, r, w = g.shape
    rh = r // 2
    tt = _pick(rh, 256, 16)
    nb = rh // tt

    def body(chip_ref, core_ref, a_ref, b_ref, own_ref, ob_ref):
        v = a_ref[...] + b_ref[...]
        ob_ref[...] = v.astype(bf16)

        @pl.when(pl.program_id(1) == chip_ref[0])
        def _():
            own_ref[...] = v[0]

    grid_spec = pltpu.PrefetchScalarGridSpec(
        num_scalar_prefetch=2, grid=(nb, S),
        in_specs=[pl.BlockSpec((1, tt, w), lambda i, s_, ch, co: (s_, co[0] * nb + i, 0)),
                  pl.BlockSpec((1, tt, w), lambda i, s_, ch, co: (s_, i, 0))],
        out_specs=[pl.BlockSpec((tt, w), lambda i, s_, ch, co: (i, 0)),
                   pl.BlockSpec((1, tt, w), lambda i, s_, ch, co: (s_, i, 0))])
    return pl.pallas_call(body, grid_spec=grid_spec,
                          out_shape=[jax.ShapeDtypeStruct((rh, w), f32), jax.ShapeDtypeStruct((S, rh, w), bf16)],
                          compiler_params=pltpu.CompilerParams(dimension_semantics=("arbitrary", "arbitrary")),
                          name=name)(chip.reshape(1).astype(jnp.int32), ci.reshape(1).astype(jnp.int32), g, recv)


def _sum_chip(name, own, others, ci):
    rh, w = own.shape
    tt = _pick(rh, 256, 16)
    nb = rh // tt

    def body(core_ref, a_ref, b_ref, o_ref):
        o_ref[...] = ((a_ref[...] + b_ref[0].astype(f32)) + b_ref[1].astype(f32)) + b_ref[2].astype(f32)

    grid_spec = pltpu.PrefetchScalarGridSpec(
        num_scalar_prefetch=1, grid=(nb,),
        in_specs=[pl.BlockSpec((tt, w), lambda i, co: (i, 0)),
                  pl.BlockSpec((3, tt, w), lambda i, co: (0, i, 0))],
        out_specs=pl.BlockSpec((tt, w), lambda i, co: (co[0] * nb + i, 0)))
    return pl.pallas_call(body, grid_spec=grid_spec, out_shape=jax.ShapeDtypeStruct((2 * rh, w), f32),
                          name=name)(ci.reshape(1).astype(jnp.int32), own, others)


def _sum_devices(gathered, late, late_rows, head_row):
    _, rows, width = gathered.shape

    def body(g_ref, l_ref, out_ref, head_ref):
        acc, acc_l = g_ref[0], l_ref[0]
        for d in range(1, 8):
            acc = acc + g_ref[d]
            acc_l = acc_l + l_ref[d]
        out_ref[...] = acc
        for k, r in enumerate(late_rows):
            out_ref[r:r + 1, :] = acc_l[k:k + 1, :]
        row = acc[head_row:head_row + 1, :]
        hs = row[:, 0:HEAD_DIM]
        for h in range(1, HEADS):
            hs = hs + row[:, h * HEAD_DIM:(h + 1) * HEAD_DIM]
        head_ref[...] = jnp.zeros_like(head_ref)
        head_ref[0:1, 0:HEAD_DIM] = hs

    return pl.pallas_call(
        body, out_shape=[jax.ShapeDtypeStruct((rows, width), f32), jax.ShapeDtypeStruct((8, LANES), f32)],
        name="small_grads_sum",
    )(gathered, late)


def _pack(arrs, rows_mult, dtype):
    flat = jnp.concatenate([a.reshape(-1).astype(dtype) for a in arrs])
    per = PACK_W * rows_mult
    total = -(-flat.shape[0] // per) * per
    return jnp.pad(flat, (0, total - flat.shape[0])).reshape(total // PACK_W, PACK_W)


def _unpack(buf, shapes):
    flat = buf.reshape(-1)
    out, off = [], 0
    for s in shapes:
        n = int(np.prod(s))
        out.append(flat[off:off + n].reshape(s))
        off += n
    return out


_BIG = ["w_in", "w_branch_gdn", "w_branch_rwkv", "w_out", "w_ffn_in", "w_ffn_out"]
_MID = ["conv_gdn", "conv_ffn", "w2", "a2", "g2"]
_SMALL =["b_ada", "norm1_w", "a_log", "dt_bias", "onorm_gdn", "mu_rwkv", "w0", "a0", "k_k", "k_a", "r_k", "lnx_w",
          "lnx_b", "norm2_w", "norm_f_w"]
_ORDER = ["w_ada", "b_ada", "norm1_w", "w_in", "conv_gdn", "a_log", "dt_bias", "onorm_gdn", "w_branch_gdn", "mu_rwkv",
          "w0", "w2", "a0", "a2", "g2", "k_k", "k_a", "r_k", "lnx_w", "lnx_b", "w_branch_rwkv", "w_out", "norm2_w",
          "w_ffn_in", "conv_ffn", "w_ffn_out", "norm_f_w"]


_WIN_SEGMENTS = [(0, 1536, 0), (2064, 3600, 1536), (1536, 2048, 3072), (3600, 3728, 3584), (2048, 2064, 3712),
                 (3728, 3888, 3840), (3888, 5936, 4096)]
_WIN_PADDED = 6144
_COL_QKV, _COL_RKV, _COL_Z = (0, 1, 2), (3, 4, 5), 6
_COL_LORA, _COL_BA = 28, 29
_COL_GATE_LORA = 15
_COL_GL = (4, 5)
_JOINT_QKV, _JOINT_RKV, _JOINT_Z, _JOINT_GL = 0, 1, 6, 2
_SMALL_BLOCKS_AT = 3584


def _win_pad(shards):
    n = shards.shape[2]
    parts, at = [], 0
    for lo, hi, dst in _WIN_SEGMENTS:
        if dst > at:
            parts.append(jnp.zeros((shards.shape[1], dst - at), shards.dtype))
        c = lo
        while c < hi:
            j = c // n
            e = min(hi, (j + 1) * n)
            parts.append(shards[j][:, c - j * n:e - j * n])
            c = e
        at = dst + hi - lo
    if at < _WIN_PADDED:
        parts.append(jnp.zeros((shards.shape[1], _WIN_PADDED - at), shards.dtype))
    return jnp.concatenate(parts, axis=1)


def _win_unpad_shards(g, n):
    shards = []
    for j in range(4):
        parts = []
        for lo, hi, dst in sorted(_WIN_SEGMENTS):
            a, b = max(lo, j * n), min(hi, (j + 1) * n)
            if a < b:
                parts.append(g[:, dst + a - lo:dst + b - lo])
        shards.append(jnp.concatenate(parts, axis=1))
    return jnp.stack(shards)


def kernel(x, c, w_ada, b_ada, norm1_w, w_in, conv_gdn, a_log, dt_bias, onorm_gdn, w_branch_gdn, mu_rwkv, w0, w2, a0, a2, g2, k_k, k_a, r_k, lnx_w, lnx_b, w_branch_rwkv, w_out, norm2_w, w_ffn_in, conv_ffn, w_ffn_out, norm_f_w, loss_target, m_w_ada, m_b_ada, m_norm1_w, m_w_in, m_conv_gdn, m_a_log, m_dt_bias, m_onorm_gdn, m_w_branch_gdn, m_mu_rwkv, m_w0, m_w2, m_a0, m_a2, m_g2, m_k_k, m_k_a, m_r_k, m_lnx_w, m_lnx_b, m_w_branch_rwkv, m_w_out, m_norm2_w, m_w_ffn_in, m_conv_ffn, m_w_ffn_out, m_norm_f_w, v_w_ada, v_b_ada, v_norm1_w, v_w_in, v_conv_gdn, v_a_log, v_dt_bias, v_onorm_gdn, v_w_branch_gdn, v_mu_rwkv, v_w0, v_w2, v_a0, v_a2, v_g2, v_k_k, v_k_a, v_r_k, v_lnx_w, v_lnx_b, v_w_branch_rwkv, v_w_out, v_norm2_w, v_w_ffn_in, v_conv_ffn, v_w_ffn_out, v_norm_f_w):
    args = dict(locals())
    W = {n: args[n] for n in _ORDER}
    Mo = {n: args["m_" + n] for n in _ORDER}
    Vo = {n: args["v_" + n] for n in _ORDER}
    shapes = {n: W[n].shape for n in _ORDER}
    sq = lambda a: a.reshape(a.shape[-2:]) if a.ndim == 3 else a.reshape(1, -1)
    row = lambda a: a.reshape(1, -1)

    xi, yi, ci = lax.axis_index("x"), lax.axis_index("y"), lax.axis_index("c")
    dev = 4 * xi + 2 * yi + ci
    chip = 2 * xi + yi

    x2 = x[0]
    tgt = loss_target[0]
    T, D = x2.shape
    tt_l = _pick(T, 512, CHUNK)
    tt_p = _pick(T, 256, CHUNK)
    tt_h = _pick(T, 128, CHUNK)

    small_blk = _pack([c] + [W[n] for n in _MID], 8, f32)
    small_all, win_s = _run_job("gather_c_mid_and_w_in", [_ag8_job(small_blk), _ag4_job([sq(W["w_in"]).astype(bf16)])])
    c_all = small_all[:, 0, :]
    per_chip = small_all[0::2].reshape(4, -1)[:, D:]
    mid_w, at = {}, 0
    for n in _MID:
        nr, nw = shapes[n][1:]
        mid_w[n] = per_chip[:, at:at + nr * nw].reshape(4, nr, nw).transpose(1, 0, 2).reshape(nr, 4 * nw)
        at += nr * nw
    conv_gdn_f, conv_ffn_f = mid_w["conv_gdn"], mid_w["conv_ffn"]
    w2f, a2f, g2f = mid_w["w2"], mid_w["a2"], mid_w["g2"]

    later_weights = _ag4_job([sq(W[n]).astype(bf16) for n in _BIG[1:]])
    win_p = _win_pad(win_s)
    zpad = lambda a, top, bot: jnp.pad(a, ((top, bot), (0, 0)))
    w2p, a2p, g2p = zpad(w2f, 0, 64), zpad(a2f, 64, 0), zpad(g2f, 0, 96)

    ncol = shapes["w_ada"][2]
    b_cols = lax.dynamic_slice(sq(W["b_ada"]), (0, chip * ncol), (1, ncol))
    cond16, mod_cols = _ada_fwd(jnp.pad(c_all, ((0, 8), (0, 0))), sq(W["w_ada"]), b_cols)
    mod_all = _ag8("gather_mod", mod_cols[:8])
    mod_mine = lax.dynamic_slice(mod_all[0::2], (0, dev, 0), (4, 1, ncol)).reshape(1, 4 * ncol)
    shift1, scale1, gate1, shift2, scale2, gate2 = [mod_mine[:, i * D:(i + 1) * D] for i in range(6)]

    seg = _seg_matrix(WIDTH, HEAD_DIM)
    norm1 = [sq(W["norm1_w"]), shift1, scale1]
    h1 = _stage_fwd("norm_mod1", _fn_norm_mod, [_whole(x2)], norm1, [(D, bf16)], tt_l)[0]
    p = _matmul("in_proj", h1, win_p, "nn", tm=2048)

    cgq = [row(conv_gdn_f[j, part * WIDTH:(part + 1) * WIDTH]) for part in range(3) for j in range(4)]
    lane_pad = lambda a: jnp.pad(row(a), ((0, 0), (8, LANES - 16)))
    gdn_pre_ps = cgq + [lane_pad(W["a_log"]), lane_pad(W["dt_bias"]), seg, _chunk_tri(tt_p, CHUNK)]
    gdn_pre_ts = [(p, WIDTH, ci_, (0, 1, 2, 3)) for ci_ in _COL_QKV] + [(p, LANES, _COL_BA, None)]
    q_, k_, v_, beta_t, gc_t = _stage_fwd("gdn_pre", _fn_gdn_pre, gdn_pre_ts, gdn_pre_ps,
                                          [(WIDTH, f32)] * 3 + [(LANES, f32)] * 2, tt_p)
    o_, gdn_hist, gdn_inv = _scan_fwd("gdn_scan", _gdn_chunk, [q_, k_, v_], [beta_t, gc_t])
    ow512 = jnp.tile(row(W["onorm_gdn"]), (1, HEADS))
    gdn_post_ts = [_whole(o_), (p, WIDTH, _COL_Z, None)]
    ya = _stage_fwd("gdn_post", _fn_gdn_post, gdn_post_ts, [ow512, seg], [(WIDTH, bf16)], tt_l)[0]

    mu = sq(W["mu_rwkv"])
    rw_ps = [mu[:, 0:512], mu[:, 512:1024], mu[:, 1024:1536], mu[:, 1536:1664], jnp.pad(mu[:, 1664:1824], ((0, 0), (0, 96))),
             sq(W["w0"]), w2p, sq(W["a0"]), a2p, g2p, sq(W["k_k"]), sq(W["k_a"]), seg]
    rw_ts = [(p, WIDTH, ci_, (0, 1)) for ci_ in _COL_RKV] + [(p, LANES, _COL_LORA, (0, 1)),
                                                              (p, 256, _COL_GATE_LORA, (0, 1))]
    rw_out = _stage_fwd("rwkv_pre", _fn_rwkv_pre, rw_ts, rw_ps, [(WIDTH, f32)] * 7, tt_p)
    r_, lw_, k2_, vv_, na_, b_, g_ = rw_out
    rw_ins = [r_, lw_, k2_, vv_, na_, b_]
    y_, rw_hist, rw_inv, wbg_s, wbr_s, wout_s, wfi_s, wfo_s = _scan_fwd("rwkv_scan", _rwkv_chunk, rw_ins, job=later_weights)
    wout_f = wout_s.reshape(D, D)
    wfo = wfo_s.reshape(D_FF, D)
    rwp_ps = [sq(W["lnx_w"]), sq(W["lnx_b"]), row(W["r_k"]), seg]
    rwp_ts = [_whole(y_), _whole(r_), _whole(k2_), _whole(vv_), _whole(g_)]
    yb = _stage_fwd("rwkv_post", _fn_rwkv_post, rwp_ts, rwp_ps, [(WIDTH, bf16)], tt_l)[0]

    big_a = _matmul("branch_gdn", ya, wbg_s, "nn", shards=4)
    big_b = _matmul("branch_rwkv", yb, wbr_s, "nn", shards=4)
    merge_ts = [(p, D, _COL_GL[0], None), (p, D, _COL_GL[1], None), _whole(big_a), _whole(big_b)]
    merged = _stage_fwd("merge", _fn_merge, merge_ts, [], [(D, bf16)], tt_l)[0]
    mo = _matmul("out_proj", merged, wout_f, "nn")
    norm2 = [gate1, sq(W["norm2_w"]), shift2, scale2]
    x1, h2 = _stage_fwd("resid_norm_mod2", _fn_resid_norm_mod, [_whole(x2), _whole(mo)], norm2, [(D, f32), (D, bf16)], tt_l)
    f = _matmul("ffn_in", h2, wfi_s, "nn", shards=4, tm=2048, tn=1408)
    cg_ps = [row(conv_ffn_f[j]) for j in range(3)]
    cg_ts = [(f, D_FF, 0, (0, 1, 2)), (f, D_FF, 1, None)]
    act = _stage_fwd("convglu", _fn_convglu, cg_ts, cg_ps, [(D_FF, bf16)], tt_h)[0]
    fo = _matmul("ffn_out", act, wfo, "nn", tk=D_FF)

    dx1_a, dfo, dgate2, dnormf, loss_part = _final_stage(x1, fo, tgt, gate2, row(W["norm_f_w"]), tt_l)

    dact = _matmul("d_act", dfo, wfo, "nt", tn=D_FF)
    g_wfo = _matmul("g_ffn_out", act, dfo, "tn", tm=1408, tk=2048)
    _, dcf, df = _stage_bwd("convglu_bwd", _fn_convglu, cg_ts, cg_ps, [[_whole(dact)]], tt_h, [True] * 2, [True] * 3,
                            joint=([0, 1], 2 * D_FF, 0, None), dtypes=[bf16])
    g_wfi = _matmul("g_ffn_in", h2, df, "tn", shards=4, tn=1408, tk=2048)
    g_wfo_s = g_wfo.reshape(4, D_FF // 4, D)
    dh2, recv_wfi, recv_wfo = _matmul("d_h2", df, wfi_s, "nt", shards=4, tm=2048, tk=1408,
                                      job=_sibling_swap_job([g_wfi, g_wfo_s]))
    (dx_a, dmo), (dgate1, dnorm2, dshift2, dscale2), _ = _stage_bwd(
        "resid_norm_mod2_bwd", _fn_resid_norm_mod, [_whole(x2), _whole(mo)], norm2,
        [[_whole(dx1_a)], [_whole(dh2)]], tt_l, [True, True], [True] * 4, dtypes=[f32, bf16])
    dmerged = _matmul("d_merged", dmo, wout_f, "nt")
    g_wout = _matmul("g_out_proj", merged, dmo, "tn", tk=T)
    (dbig_a, dbig_b), _, dp = _stage_bwd("merge_bwd", _fn_merge, merge_ts, [], [[_whole(dmerged)]], tt_l, [True] * 4, [],
                                         joint=([0, 1], p.shape[1], _JOINT_GL, None), dtypes=[bf16] * 3)
    g_wbg = _matmul("g_branch_gdn", ya, dbig_a, "tn", shards=4, tk=T)
    g_wbr = _matmul("g_branch_rwkv", yb, dbig_b, "tn", shards=4, tk=T)
    g_wout_s = g_wout.reshape(4, D // 4, D)
    dyb = _matmul("d_yb", dbig_b, wbr_s, "nt", shards=4)
    dya, *recv_mix = _matmul("d_ya", dbig_a, wbg_s, "nt", shards=4, job=_sibling_swap_job([g_wbg, g_wbr, g_wout_s]))

    (dy_, dr_p, dk2_p, dv_p, dg_p), (dlnxw, dlnxb, drk), _ = _stage_bwd(
        "rwkv_post_bwd", _fn_rwkv_post, rwp_ts, rwp_ps, [[_whole(dyb)]], tt_l, [True] * 5, [True, True, True, False])
    gs_a = [g_wbg, g_wbr, g_wout_s, g_wfi, g_wfo_s]
    pairs_a = [_add_half("grads_pair_sum%d" % (t + 1), g, r_, chip, ci)
               for t, (g, r_) in enumerate(zip(gs_a, recv_mix + [recv_wfi, recv_wfo]))]
    dr_c, dlw_c, dk2_c, dv_c, dna_c, db_c, *others_a = _scan_bwd(
        "rwkv_scan_bwd", _rwkv_chunk, rw_ins, [], rw_hist, rw_inv, dy_, job=_chip_exchange_job([pb for _, pb in pairs_a]))
    rw_cots = [[_whole(dr_p), _whole(dr_c)], [_whole(dlw_c)], [_whole(dk2_p), _whole(dk2_c)],
               [_whole(dv_p), _whole(dv_c)], [_whole(dna_c)], [_whole(db_c)], [_whole(dg_p)]]
    (dl_, dg_), rw_dp, dp = _stage_bwd("rwkv_pre_bwd", _fn_rwkv_pre, rw_ts, rw_ps, rw_cots, tt_p, [True] * 5,
                                       [True] * 12 + [False], joint=([0, 1, 2], p.shape[1], _JOINT_RKV, dp), dtypes=[bf16] * 3)
    dmu_r, dmu_k, dmu_v, dmu_l, dmu_g, dw0, dw2p, da0, da2p, dg2p, dkk, dka = rw_dp

    (do_,), (dow512,), dp = _stage_bwd("gdn_post_bwd", _fn_gdn_post, gdn_post_ts, [ow512, seg], [[_whole(dya)]], tt_l,
                                       [True, True], [True, False], joint=([1], p.shape[1], _JOINT_Z, dp), dtypes=[f32, bf16])
    d_gdn = _scan_bwd("gdn_scan_bwd", _gdn_chunk, [q_, k_, v_], [beta_t, gc_t], gdn_hist, gdn_inv, do_)
    gdn_cots = [[_whole(a)] for a in d_gdn]
    (dba,), gdn_dp, dp = _stage_bwd("gdn_pre_bwd", _fn_gdn_pre, gdn_pre_ts, gdn_pre_ps, gdn_cots, tt_p, [True] * 4,
                                    [True] * 14 + [False, False], joint=([0, 1, 2], p.shape[1], _JOINT_QKV, dp), dtypes=[bf16] * 2)
    dp = lax.dynamic_update_slice(dp, jnp.concatenate([dl_, dba, dg_], axis=1), (0, _SMALL_BLOCKS_AT))
    g_conv_gdn = jnp.concatenate([jnp.concatenate([gdn_dp[4 * part + j] for part in range(3)], axis=1) for j in range(4)], axis=0)
    g_conv_ffn = jnp.concatenate(dcf, axis=0)
    g_mu = jnp.concatenate([dmu_r, dmu_k, dmu_v, dmu_l, dmu_g[:, :160]], axis=1)
    late_zero = jnp.zeros((1, D), f32)
    dmod_early = jnp.concatenate([late_zero, late_zero, dgate1, dshift2, dscale2, dgate2], axis=1)
    small_parts = {"b_ada": dmod_early, "norm1_w": late_zero, "a_log": gdn_dp[12][:, 8:16], "dt_bias": gdn_dp[13][:, 8:16],
                   "mu_rwkv": g_mu, "w0": dw0, "a0": da0, "k_k": dkk, "k_a": dka, "r_k": drk, "lnx_w": dlnxw,
                   "lnx_b": dlnxb, "norm2_w": dnorm2, "norm_f_w": dnormf}
    small_names = [n for n in _SMALL if n != "onorm_gdn"]
    mid_full = [g_conv_gdn, g_conv_ffn, dw2p[0:64], da2p[64:128], dg2p[0:160]]
    body_rows = _pack([small_parts[n] for n in small_names] + [loss_part[:, 0:1]] + mid_full, 1, f32)
    head_row = body_rows.shape[0]
    small_g = jnp.concatenate([body_rows, jnp.pad(dow512, ((0, 0), (0, PACK_W - WIDTH)))], axis=0)
    small_g = jnp.pad(small_g, ((0, -small_g.shape[0] % 8), (0, 0)))

    g_win_pad, small_all_g = _matmul("g_in_proj", h1, dp, "tn", tk=T, job=_ag8_job(small_g))
    g_win_s = _win_unpad_shards(g_win_pad, shapes["w_in"][2])
    pair_win = _add_half("grads_pair_sum0", g_win_s, _run_job("w_in_grads_sibling_swap", _sibling_swap_job([g_win_s]))[0],
                         chip, ci)
    dh1, others_win = _matmul("d_h1", dp, win_p, "nt", tk=3072, job=_chip_exchange_job([pair_win[1]]))
    (grad_x,), (dnorm1, dshift1, dscale1), _ = _stage_bwd("norm_mod1_bwd", _fn_norm_mod_and_x, [_whole(x2)], norm1,
                                                          [[_whole(dh1)], [_whole(dx_a)]], tt_l, [True], [True] * 3)
    late_all = _ag8("gather_late_grads", jnp.pad(jnp.concatenate([dshift1, dscale1, dnorm1], axis=0), ((0, 5), (0, 0))))
    small_sum, head_sum = _sum_devices(small_all_g, late_all, (0, 1, 6), head_row)
    small_shapes = [shapes[n][1:] if n != "norm_f_w" else shapes[n] for n in small_names]
    un = _unpack(small_sum, small_shapes + [(1,)] + [g.shape for g in mid_full])
    small_grads = dict(zip(small_names, un))
    loss = un[len(small_names)].reshape(())
    small_grads["onorm_gdn"] = head_sum[0, 0:HEAD_DIM]
    for n, g in zip(_MID, un[len(small_names) + 1:]):
        wcols = shapes[n][2]
        small_grads[n] = lax.dynamic_slice(g, (0, chip * wcols), (g.shape[0], wcols))

    dmod_all = jnp.concatenate([late_all[:, 0, :], late_all[:, 1, :], small_all_g[:, 2:6, :].reshape(8, 4 * PACK_W)], axis=1)
    dmod_cols = lax.dynamic_slice(dmod_all, (0, chip * ncol), (8, ncol))
    g_wada = _matmul("g_w_ada", cond16, jnp.pad(dmod_cols, ((0, 8), (0, 0))), "tn")

    pairs = [pair_win] + pairs_a
    others = [others_win] + others_a
    halves = [_sum_chip("grads_chip_sum%d" % t, own, o_, ci) for t, ((own, _), o_) in enumerate(zip(pairs, others))]
    big_grads = dict(zip(_BIG, _rs_sibling_join(halves)))

    res = {tag: {} for tag in ("grad", "delta", "new_m", "new_v")}

    def put(n, g, d, m_, v_):
        for tag, val in zip(("grad", "delta", "new_m", "new_v"), (g, d, m_, v_)):
            res[tag][n] = val.reshape(shapes[n])

    for n in _BIG:
        put(n, big_grads[n], *_adamw("adamw_" + n, sq(W[n]), big_grads[n], sq(Mo[n]), sq(Vo[n])))
    put("w_ada", g_wada, *_adamw("adamw_w_ada", sq(W["w_ada"]), g_wada, sq(Mo["w_ada"]), sq(Vo["w_ada"])))
    rest = _SMALL + _MID
    pk = lambda d: _pack([d[n] for n in rest], 8, f32)
    sg = pk(small_grads)
    sm = _adamw("adamw_small", pk(W), sg, pk(Mo), pk(Vo))
    for tag, buf in zip(("grad", "delta", "new_m", "new_v"), (sg,) + tuple(sm)):
        res[tag].update(zip(rest, _unpack(buf, [shapes[n] for n in rest])))
    outs = [loss, grad_x.reshape(x.shape)]
    for tag in ("grad", "delta", "new_m", "new_v"):
        outs += [res[tag][n] for n in _ORDER]
    return tuple(outs)
```

```python
import numpy as np
import jax
import jax.numpy as jnp
from jax import lax
from jax.experimental import pallas as pl
from jax.experimental.pallas import tpu as pltpu

f32 = jnp.float32
bf16 = jnp.bfloat16

LANES = 128
HEADS = 8
HEAD_DIM = 64
WIDTH = HEADS * HEAD_DIM
CHUNK = 64
D_FF = 2816
NORM_EPS = 1e-6
LNX_EPS = 64e-5
PACK_W = 1024
MESH_ID = pl.DeviceIdType.MESH

ADAM_LR, ADAM_B1, ADAM_B2, ADAM_EPS, ADAM_WD, ADAM_STEP = 0.001, 0.9, 0.999, 1e-08, 0.01, 10


def _pick(n, target, mult):
    if n <= target:
        return n
    best = None
    for t in range(mult, target + 1, mult):
        if n % t == 0:
            best = t
    assert best is not None, (n, target, mult)
    return best


def _split_bf16(x, n):
    parts, r = [], x
    for i in range(n):
        p = r.astype(bf16)
        parts.append(p)
        if i + 1 < n:
            r = r - p.astype(f32)
    return parts


def _xdot_r_impl(x, m, n, dims):
    acc = None
    for p in _split_bf16(x, n):
        t = lax.dot_general(p, m, dims, preferred_element_type=f32)
        acc = t if acc is None else acc + t
    return acc


def _make_xdot_r(n):
    nn = (((1,), (0,)), ((), ()))
    nt = (((1,), (1,)), ((), ()))

    @jax.custom_vjp
    def xdot(x, m):
        return _xdot_r_impl(x, m, n, nn)

    def fwd(x, m):
        return _xdot_r_impl(x, m, n, nn), m

    def bwd(m, ct):
        return _xdot_r_impl(ct, m, n, nt), jnp.zeros_like(m)

    xdot.defvjp(fwd, bwd)
    return xdot


_segsum = _make_xdot_r(2)


def _xdot_l_impl(m, x, n, dims):
    acc = None
    for p in _split_bf16(x, n):
        t = lax.dot_general(m, p, dims, preferred_element_type=f32)
        acc = t if acc is None else acc + t
    return acc


@jax.custom_vjp
def _xdot_l(m, x):
    return _xdot_l_impl(m, x, 3, (((1,), (0,)), ((), ())))


def _xdot_l_fwd(m, x):
    return _xdot_l(m, x), m


def _xdot_l_bwd(m, ct):
    return jnp.zeros_like(m), _xdot_l_impl(m, ct, 3, (((0,), (0,)), ((), ())))


_xdot_l.defvjp(_xdot_l_fwd, _xdot_l_bwd)


@jax.custom_vjp
def _bdot(x, w):
    return jnp.dot(x.astype(bf16), w.astype(bf16), preferred_element_type=f32)


def _bdot_fwd(x, w):
    return _bdot(x, w), (x, w)


def _bdot_bwd(res, ct):
    x, w = res
    c = ct.astype(bf16)
    dx = lax.dot_general(c, w.astype(bf16), (((1,), (1,)), ((), ())), preferred_element_type=f32)
    dw = lax.dot_general(x.astype(bf16), c, (((0,), (0,)), ((), ())), preferred_element_type=f32)
    return dx, dw


_bdot.defvjp(_bdot_fwd, _bdot_bwd)


def _silu(x):
    return x * jax.nn.sigmoid(x)


def _softplus(x):
    return jnp.maximum(x, 0.0) + jnp.log(1.0 + jnp.exp(-jnp.abs(x)))


def _rms(x, w, eps):
    return x * lax.rsqrt(jnp.mean(x * x, axis=-1, keepdims=True) + eps) * w


def _seg_matrix(width, seg):
    i = np.arange(width)
    return jnp.asarray((i[:, None] // seg) == (i[None, :] // seg), dtype=bf16)


def _chunk_tri(rows, chunk):
    i = np.arange(rows)
    return jnp.asarray(((i[:, None] // chunk) == (i[None, :] // chunk)) & (i[:, None] >= i[None, :]), dtype=bf16)


HALO = 8


def _full_spec(shape):
    nd = len(shape)
    return pl.BlockSpec(shape, lambda i: (0,) * nd)


def _entry_specs(entries, tt, block_of):
    specs, ops = [], []
    for arr, w, ci, shifts in entries:
        specs.append(pl.BlockSpec((tt, w), lambda i, ci=ci: (block_of(i), ci)))
        ops.append(arr)
        if shifts:
            specs.append(pl.BlockSpec((HALO, w), lambda i, ci=ci: (jnp.maximum(block_of(i) * (tt // HALO) - 1, 0), ci)))
            ops.append(arr)
    return specs, ops


def _load_entries(entries, refs, first):
    tiles, k = [], 0
    for _, w, _, shifts in entries:
        x = refs[k][...].astype(f32)
        k += 1
        if not shifts:
            tiles.append(x)
            continue
        halo = jnp.where(first, 0.0, refs[k][...].astype(f32))
        k += 1
        row = lax.broadcasted_iota(jnp.int32, (HALO, w), 0)
        for s in shifts:
            if s == 0:
                tiles.append(x)
                continue
            r = pltpu.roll(x, s, 0)
            head = jnp.where(row < s, pltpu.roll(halo, s, 0), r[0:HALO])
            tiles.append(jnp.concatenate([head, r[HALO:]], axis=0))
    return tiles


def _unshift_sum(grads, shifts, carry, tt):
    w = grads[0].shape[1]
    row = lax.broadcasted_iota(jnp.int32, (tt, w), 0)
    row8 = lax.broadcasted_iota(jnp.int32, (HALO, w), 0)
    dx, out = None, jnp.zeros((HALO, w), f32)
    for d, s in zip(grads, shifts):
        if s == 0:
            part = d
        else:
            part = jnp.where(row < tt - s, pltpu.roll(d, tt - s, 0), 0.0)
            out = out + jnp.where(row8 >= HALO - s, pltpu.roll(d[0:HALO], HALO - s, 0), 0.0)
        dx = part if dx is None else dx + part
    return jnp.concatenate([dx[:tt - HALO], dx[tt - HALO:] + carry], axis=0), out


def _stage_fwd(name, fn, tiles, params, outs, tt):
    rows = tiles[0][0].shape[0]
    npar = len(params)
    specs, ops = _entry_specs(tiles, tt, lambda i: i)
    nin = len(ops)

    def body(*refs):
        ts = _load_entries(tiles, refs[:nin], pl.program_id(0) == 0)
        ps = [r[...] for r in refs[nin:nin + npar]]
        res = fn(ps, ts)
        for r, v in zip(refs[nin + npar:], res):
            r[...] = v.astype(r.dtype)

    return pl.pallas_call(
        body, grid=(rows // tt,),
        in_specs=specs + [_full_spec(p.shape) for p in params],
        out_specs=[pl.BlockSpec((tt, w), lambda i: (i, 0)) for (w, _) in outs],
        out_shape=[jax.ShapeDtypeStruct((rows, w), dt) for (w, dt) in outs],
        compiler_params=pltpu.CompilerParams(dimension_semantics=("parallel",)),
        name=name,
    )(*ops, *params)


def _stage_bwd(name, fn, tiles, params, cots, tt, tile_grad, param_grad, joint=None, dtypes=None):
    rows = tiles[0][0].shape[0]
    nblk = rows // tt
    npar = len(params)
    block_of = lambda i: nblk - 1 - i
    specs, ops = _entry_specs(tiles, tt, block_of)
    nin = len(ops)
    flat_cots = [c for group in cots for c in group]
    groups = [len(g) for g in cots]
    ncot = len(flat_cots)
    counts = [len(e[3]) if e[3] else 1 for e in tiles]
    dt_entries = [e for e, g in zip(tiles, tile_grad) if g]
    dp_shapes = [p.shape for p, g in zip(params, param_grad) if g]
    ndt = len(dt_entries)
    carry_w = [e[1] for e in dt_entries if e[3]]
    flags = [g for g, n in zip(tile_grad, counts) for _ in range(n)]
    members, j_width, j_cidx, j_buf = joint if joint else ([], 0, 0, None)
    solo = [k for k in range(ndt) if k not in members]
    nsolo, njoint, nbuf = len(solo), int(bool(members)), int(j_buf is not None)
    j_block = sum(dt_entries[k][1] for k in members)
    dtypes = list(dtypes) if dtypes else [f32] * (nsolo + njoint)

    def body(*refs):
        i = pl.program_id(0)
        p_refs = refs[nin:nin + npar]
        c_refs = refs[nin + npar:nin + npar + ncot]
        base = nin + npar + ncot + nbuf
        dt_refs = refs[base:base + nsolo]
        joint_refs = refs[base + nsolo:base + nsolo + njoint]
        dp_refs = refs[base + nsolo + njoint:base + nsolo + njoint + len(dp_shapes)]
        carry_refs = refs[base + nsolo + njoint + len(dp_shapes):]
        ts = _load_entries(tiles, refs[:nin], block_of(i) == 0)
        ps = [r[...] for r in p_refs]

        def f(dp, dt):
            dp, dt = iter(dp), iter(dt)
            pp = [next(dp) if g else p for p, g in zip(ps, param_grad)]
            tl = [next(dt) if g else t for t, g in zip(ts, flags)]
            return fn(pp, tl)

        _, vjp = jax.vjp(f, [p for p, g in zip(ps, param_grad) if g], [t for t, g in zip(ts, flags) if g])
        cs, j = [], 0
        for n in groups:
            acc = c_refs[j][...].astype(f32)
            for q in range(1, n):
                acc = acc + c_refs[j + q][...].astype(f32)
            cs.append(acc)
            j += n
        gp, gt = vjp(cs)

        @pl.when(i == 0)
        def _():
            for r in dp_refs:
                r[...] = jnp.zeros_like(r)
            for r in carry_refs:
                r[...] = jnp.zeros_like(r)

        gt, k, kc, dxs = list(gt), 0, 0, []
        for e, n in zip(dt_entries, [n for n, g in zip(counts, tile_grad) if g]):
            if e[3]:
                dx, out = _unshift_sum(gt[k:k + n], e[3], carry_refs[kc][...], tt)
                carry_refs[kc][...] = out
                kc += 1
            else:
                dx = gt[k]
            dxs.append(dx)
            k += n
        for r, k in zip(dt_refs, solo):
            r[...] = dxs[k].astype(r.dtype)
        off = 0
        for k in members:
            w = dt_entries[k][1]
            joint_refs[0][:, off:off + w] = dxs[k].astype(joint_refs[0].dtype)
            off += w
        for r, v in zip(dp_refs, gp):
            r[...] += v

    res = pl.pallas_call(
        body, grid=(nblk,),
        in_specs=specs + [_full_spec(p.shape) for p in params]
        + [pl.BlockSpec((tt, w), lambda i, ci=ci: (block_of(i), ci)) for (_, w, ci, *_) in flat_cots]
        + [pl.BlockSpec(memory_space=pl.ANY)] * nbuf,
        out_specs=[pl.BlockSpec((tt, dt_entries[k][1]), lambda i: (block_of(i), 0)) for k in solo]
        + [pl.BlockSpec((tt, j_block), lambda i: (block_of(i), j_cidx))] * njoint
        + [_full_spec(s) for s in dp_shapes],
        out_shape=[jax.ShapeDtypeStruct((rows, dt_entries[k][1]), dt) for k, dt in zip(solo, dtypes)]
        + [jax.ShapeDtypeStruct((rows, j_width), dtypes[-1])] * njoint
        + [jax.ShapeDtypeStruct(s, f32) for s in dp_shapes],
        scratch_shapes=[pltpu.VMEM((HALO, w), f32) for w in carry_w],
        input_output_aliases={nin + npar + ncot: nsolo} if nbuf else {},
        compiler_params=pltpu.CompilerParams(dimension_semantics=("arbitrary",)),
        name=name,
    )(*ops, *params, *[c[0] for c in flat_cots], *([j_buf] if nbuf else []))
    res = list(res)
    return res[:nsolo], res[nsolo + njoint:], (res[nsolo] if njoint else None)


def _whole(a):
    return (a, a.shape[1], 0, None)


def _matmul(name, a, b, mode, out_dtype=f32, tm=1024, tn=1024, tk=1024, shards=1, job=None):
    S = shards
    if mode == "nn":
        M, K = a.shape
        w = b.shape[-1]
    elif mode == "nt":
        M = a.shape[0]
        if S > 1:
            _, N, w = b.shape
            K = S * w
        else:
            N, K = b.shape
            w = K
    else:
        K, M = a.shape
        w = b.shape[1] // S
    if mode != "nt":
        N = S * w
    tm = _pick(M, tm, LANES)
    if mode == "nt":
        tn = _pick(N, tn, LANES)
        tk = _pick(w, tk, LANES)
    else:
        tn = _pick(w, tn, LANES)
        tk = _pick(K, tk, LANES if mode == "nn" else 16)
    nk = K // tk
    nb = w // (tk if mode == "nt" else tn)
    if mode == "nn":
        a_spec = pl.BlockSpec((tm, tk), lambda i, j, k: (i, k))
        if S > 1:
            b_spec = pl.BlockSpec((1, tk, tn), lambda i, j, k: (j // nb, k, j % nb))
        else:
            b_spec = pl.BlockSpec((tk, tn), lambda i, j, k: (k, j))
        dims = (((1,), (0,)), ((), ()))
    elif mode == "nt":
        a_spec = pl.BlockSpec((tm, tk), lambda i, j, k: (i, k))
        if S > 1:
            b_spec = pl.BlockSpec((1, tn, tk), lambda i, j, k: (k // nb, j, k % nb))
        else:
            b_spec = pl.BlockSpec((tn, tk), lambda i, j, k: (j, k))
        dims = (((1,), (1,)), ((), ()))
    else:
        a_spec = pl.BlockSpec((tk, tm), lambda i, j, k: (k, i))
        b_spec = pl.BlockSpec((tk, tn), lambda i, j, k: (k, j))
        dims = (((0,), (0,)), ((), ()))
    if mode == "tn" and S > 1:
        o_spec = pl.BlockSpec((1, tm, tn), lambda i, j, k: (j // nb, i, j % nb))
        o_shape = (S, M, w)
    else:
        o_spec = pl.BlockSpec((tm, tn), lambda i, j, k: (i, j))
        o_shape = (M, N)
    b_lead = S > 1 and mode != "tn"
    o_lead = S > 1 and mode == "tn"

    j_ins, j_outs, j_sems = _job_parts(job)
    nji, njo = len(j_ins), len(j_outs)
    grid = (M // tm, N // tn, nk)

    def run_job(refs):
        step = (pl.program_id(0) * grid[1] + pl.program_id(1)) * grid[2] + pl.program_id(2)
        _job_steps(job, refs[2:2 + nji], refs[3 + nji:3 + nji + njo], refs[len(refs) - len(j_sems):], step,
                   grid[0] * grid[1] * grid[2] - 1)

    def body(*refs):
        a_ref, b_ref, o_ref, acc_ref = refs[0], refs[1], refs[2 + nji], refs[3 + nji + njo]
        run_job(refs)
        k = pl.program_id(2)

        @pl.when(k == 0)
        def _():
            acc_ref[...] = jnp.zeros_like(acc_ref)

        bv = b_ref[0] if b_lead else b_ref[...]
        acc_ref[...] += lax.dot_general(a_ref[...].astype(bf16), bv.astype(bf16), dims, preferred_element_type=f32)

        @pl.when(k == nk - 1)
        def _():
            if o_lead:
                o_ref[0] = acc_ref[...].astype(o_ref.dtype)
            else:
                o_ref[...] = acc_ref[...].astype(o_ref.dtype)

    def body_one_step(*refs):
        a_ref, b_ref, o_ref = refs[0], refs[1], refs[2 + nji]
        run_job(refs)
        bv = b_ref[0] if b_lead else b_ref[...]
        res = lax.dot_general(a_ref[...].astype(bf16), bv.astype(bf16), dims, preferred_element_type=f32)
        if o_lead:
            o_ref[0] = res.astype(o_ref.dtype)
        else:
            o_ref[...] = res.astype(o_ref.dtype)

    res = pl.pallas_call(
        body if nk > 1 else body_one_step, grid=grid,
        in_specs=[a_spec, b_spec] + [_ANY] * nji,
        out_specs=[o_spec] + [_ANY] * njo,
        out_shape=[jax.ShapeDtypeStruct(o_shape, out_dtype)] + j_outs,
        scratch_shapes=([pltpu.VMEM((tm, tn), f32)] if nk > 1 else []) + j_sems,
        compiler_params=pltpu.CompilerParams(
            dimension_semantics=("arbitrary",) * 3 if job else ("parallel", "parallel", "arbitrary")),
        name=name,
    )(a, b, *j_ins)
    return res if job else res[0]


def _make_bmm(precision):
    if precision is None:
        cast, kw = (lambda v: v.astype(bf16)), {}
    else:
        cast, kw = (lambda v: v), {"precision": precision}

    def nn(a, b):
        return jnp.einsum("hij,hjk->hik", cast(a), cast(b), preferred_element_type=f32, **kw)

    def nt(a, b):
        return jnp.einsum("hik,hjk->hij", cast(a), cast(b), preferred_element_type=f32, **kw)

    def tn(a, b):
        return jnp.einsum("hki,hkj->hij", cast(a), cast(b), preferred_element_type=f32, **kw)

    if precision is not None:
        return nn, nt, tn
    nn_v, nt_v, tn_v = jax.custom_vjp(nn), jax.custom_vjp(nt), jax.custom_vjp(tn)
    keep = lambda f: (lambda a, b: (f(a, b), (a, b)))
    nn_v.defvjp(keep(nn), lambda r, ct: (nt(ct, r[1]), tn(r[0], ct)))
    nt_v.defvjp(keep(nt), lambda r, ct: (nn(ct, r[1]), tn(ct, r[0])))
    tn_v.defvjp(keep(tn), lambda r, ct: (nt(r[1], ct), nn(r[0], ct)))
    return nn_v, nt_v, tn_v


_bmm, _bmm_nt, _bmm_tn = _make_bmm(None)
_bmm_exact = _make_bmm(lax.Precision.HIGH)[0]


def _masks(n):
    r = lax.broadcasted_iota(jnp.int32, (n, n), 0)
    c = lax.broadcasted_iota(jnp.int32, (n, n), 1)
    return (r >= c)[None], (r > c)[None], (r == c)[None]


_INV_BLOCK = 8


def _nilpotent_inverse(m, eye):
    p = eye + m
    for _ in range(2):
        m = _bmm(m, m)
        p = p + _bmm(p, m)
    return p


def _neumann_inverse_impl(m):
    n = m.shape[1]
    assert n == _INV_BLOCK * _INV_BLOCK
    r = lax.broadcasted_iota(jnp.int32, (n, n), 0)
    c = lax.broadcasted_iota(jnp.int32, (n, n), 1)
    eye = (r == c).astype(f32)[None]
    inside = jnp.where((r // _INV_BLOCK == c // _INV_BLOCK)[None], m, 0.0)
    d_inv = _nilpotent_inverse(inside, eye)
    return _bmm(_nilpotent_inverse(_bmm(d_inv, m - inside), eye), d_inv)


@jax.custom_vjp
def _neumann_inverse(m):
    return _neumann_inverse_impl(m)


def _neumann_inverse_fwd(m):
    p = _neumann_inverse_impl(m)
    return p, p


def _neumann_inverse_bwd(p, ct):
    return (_bmm_tn(p, _bmm_nt(ct, p)),)


_neumann_inverse.defvjp(_neumann_inverse_fwd, _neumann_inverse_bwd)


@jax.custom_vjp
def _given_inverse(m, p):
    return p


_given_inverse.defvjp(lambda m, p: (p, p), lambda p, ct: (_neumann_inverse_bwd(p, ct)[0], jnp.zeros_like(p)))


def _gdn_chunk(s, q, k, v, beta, gc, gr, gl, p=None):
    n = q.shape[1]
    causal, strict, _ = _masks(n)
    decay = jnp.where(causal, jnp.exp(jnp.where(causal, gc - gr, 0.0)), 0.0)
    kb = k * beta
    vb = v * beta
    lower = jnp.where(strict, _bmm_nt(kb, k) * decay, 0.0)
    t_mat = _neumann_inverse(-lower) if p is None else _given_inverse(-lower, p)
    egc = jnp.exp(gc)
    u = _bmm(t_mat, vb)
    w = _bmm(t_mat, kb * egc)
    attn = jnp.where(causal, _bmm_nt(q, k) * decay, 0.0)
    v_new = u - _bmm(w, s)
    o = _bmm(q * egc, s) + _bmm(attn, v_new)
    k_dec = k * jnp.exp(gl - gc)
    s_new = s * jnp.exp(gl) + _bmm_tn(k_dec, v_new)
    return s_new, o, t_mat


def _rwkv_chunk(s, r, lw, k, v, a, b, p=None):
    n = r.shape[1]
    causal, strict, _ = _masks(n)
    tri = jnp.broadcast_to(causal.astype(f32), (r.shape[0], n, n))
    lc = _bmm_exact(tri, lw)
    ein = jnp.exp(lc)
    eout = jnp.exp(-lc)
    a_t = a * jnp.exp(lc - lw)
    b_t = b * eout
    k_t = k * eout
    r_t = r * ein
    a_ab = jnp.where(strict, _bmm_nt(a_t, b_t), 0.0)
    a_ak = jnp.where(strict, _bmm_nt(a_t, k_t), 0.0)
    inv = _neumann_inverse(a_ab) if p is None else _given_inverse(a_ab, p)
    u = _bmm(inv, _bmm_nt(a_t, s) + _bmm(a_ak, v))
    y = (_bmm_nt(r_t, s) + _bmm(jnp.where(causal, _bmm_nt(r_t, b_t), 0.0), u)
         + _bmm(jnp.where(causal, _bmm_nt(r_t, k_t), 0.0), v))
    e_last = jnp.exp(jnp.sum(lw, axis=1, keepdims=True))
    s_new = s * e_last + _bmm_tn(u, b_t * e_last) + _bmm_tn(v, k_t * e_last)
    return s_new, y, inv


def _heads_in(ref, rows):
    return jnp.stack([ref[rows, h * HEAD_DIM:(h + 1) * HEAD_DIM] for h in range(HEADS)], axis=0)


def _heads_out(ref, rows, val):
    for h in range(HEADS):
        ref[rows, h * HEAD_DIM:(h + 1) * HEAD_DIM] = val[h]


def _gdn_scalars(bt, gt):
    n = bt.shape[0]
    gtt = gt.T
    hs = range(HEADS)
    return [jnp.stack([bt[:, h:h + 1] for h in hs], axis=0),
            jnp.stack([gt[:, HEADS + h:HEADS + h + 1] for h in hs], axis=0),
            jnp.stack([gtt[HEADS + h:HEADS + h + 1, :] for h in hs], axis=0),
            jnp.stack([gt[n - 1:n, HEADS + h:HEADS + h + 1] for h in hs], axis=0)]


def _gdn_scalars_back(dbeta, dgc, dgr, dgl):
    n = dbeta.shape[1]
    lane = lax.broadcasted_iota(jnp.int32, (n, LANES), 1)
    row = lax.broadcasted_iota(jnp.int32, (n, LANES), 0)
    sub = lax.broadcasted_iota(jnp.int32, (LANES, n), 0)
    db = jnp.zeros((n, LANES), f32)
    dg = jnp.zeros((n, LANES), f32)
    dgt = jnp.zeros((LANES, n), f32)
    for h in range(HEADS):
        db = jnp.where(lane == h, dbeta[h], db)
        dg = jnp.where(lane == HEADS + h, dgc[h] + jnp.where(row == n - 1, dgl[h], 0.0), dg)
        dgt = jnp.where(sub == HEADS + h, dgr[h], dgt)
    return [db, dg + dgt.T]


SCAN_GROUP = 2


def _scan_steps(t):
    g = SCAN_GROUP if (t // CHUNK) % SCAN_GROUP == 0 else 1
    return g, t // (CHUNK * g)


def _scan_spec(width, g, n, reverse):
    if reverse:
        return pl.BlockSpec((g * CHUNK, width), lambda i: (n - 1 - i, 0))
    return pl.BlockSpec((g * CHUNK, width), lambda i: (i, 0))


def _hist_spec(g, n, reverse):
    blk = (g, HEADS, HEAD_DIM, HEAD_DIM)
    if reverse:
        return pl.BlockSpec(blk, lambda i: (n - 1 - i, 0, 0, 0))
    return pl.BlockSpec(blk, lambda i: (i, 0, 0, 0))


def _job_list(job):
    return [] if job is None else (list(job) if isinstance(job, (list, tuple)) else [job])


def _job_parts(job):
    jobs = _job_list(job)
    sems = [pltpu.SemaphoreType.DMA((j["nsem"],)) for j in jobs for _ in range(2)]
    return [a for j in jobs for a in j["ins"]], [o for j in jobs for o in j["out_shapes"]], sems


_JOB_PHASES = ("start", "forward", "finish")


def _job_phase(job, phase, in_refs, out_refs, sems):
    ki = ko = 0
    for n, j in enumerate(_job_list(job)):
        ni, no = len(j["ins"]), len(j["out_shapes"])
        if phase in j:
            j[phase](in_refs[ki:ki + ni], out_refs[ko:ko + no], sems[2 * n], sems[2 * n + 1])
        ki, ko = ki + ni, ko + no


def _job_steps(job, in_refs, out_refs, sems, step, last):
    if job is None:
        return
    for phase, at in zip(_JOB_PHASES, (0, max(last - 2, 0), last)):
        @pl.when(step == at)
        def _(phase=phase):
            _job_phase(job, phase, in_refs, out_refs, sems)


def _scan_fwd(name, fn, rows_in, scal_in=(), job=None):
    t = rows_in[0].shape[0]
    grp, n = _scan_steps(t)
    nr, ns = len(rows_in), len(scal_in)
    j_ins, j_outs, j_sems = _job_parts(job)
    nji, njo = len(j_ins), len(j_outs)

    def body(*refs):
        o_ref, sh_ref, ph_ref = refs[nr + ns + nji:nr + ns + nji + 3]
        s_scr = refs[nr + ns + nji + 3 + njo]
        _job_steps(job, refs[nr + ns:nr + ns + nji], refs[nr + ns + nji + 3:nr + ns + nji + 3 + njo],
                   refs[nr + ns + nji + 3 + njo + 1:], pl.program_id(0), n - 1)

        @pl.when(pl.program_id(0) == 0)
        def _():
            s_scr[...] = jnp.zeros_like(s_scr)

        s = s_scr[...]
        for sub in range(grp):
            rows = slice(sub * CHUNK, (sub + 1) * CHUNK)
            sh_ref[sub] = s
            ins = [_heads_in(r, rows) for r in refs[:nr]]
            if ns:
                ins += _gdn_scalars(*[r[rows, :] for r in refs[nr:nr + ns]])
            s, o, p = fn(s, *ins)
            _heads_out(o_ref, rows, o)
            ph_ref[sub] = p
        s_scr[...] = s

    return pl.pallas_call(
        body, grid=(n,),
        in_specs=[_scan_spec(a.shape[1], grp, n, False) for a in (*rows_in, *scal_in)] + [_ANY] * nji,
        out_specs=[_scan_spec(WIDTH, grp, n, False), _hist_spec(grp, n, False), _hist_spec(grp, n, False)] + [_ANY] * njo,
        out_shape=[jax.ShapeDtypeStruct((t, WIDTH), f32)]
        + [jax.ShapeDtypeStruct((t // CHUNK, HEADS, HEAD_DIM, HEAD_DIM), f32)] * 2 + j_outs,
        scratch_shapes=[pltpu.VMEM((HEADS, HEAD_DIM, HEAD_DIM), f32)] + j_sems,
        compiler_params=pltpu.CompilerParams(dimension_semantics=("arbitrary",)),
        name=name,
    )(*rows_in, *scal_in, *j_ins)


def _scan_bwd(name, fn, rows_in, scal_in, s_hist, p_hist, d_out, job=None):
    t = rows_in[0].shape[0]
    grp, n = _scan_steps(t)
    nr, ns = len(rows_in), len(scal_in)
    j_ins, j_outs, j_sems = _job_parts(job)
    nji, njo = len(j_ins), len(j_outs)

    def body(*refs):
        sh_ref, ph_ref, do_ref = refs[nr + ns:nr + ns + 3]
        base = nr + ns + 3 + nji
        g_refs = refs[base:base + nr + ns]
        ds_scr = refs[base + nr + ns + njo]
        _job_steps(job, refs[nr + ns + 3:base], refs[base + nr + ns:base + nr + ns + njo],
                   refs[base + nr + ns + njo + 1:], pl.program_id(0), n - 1)

        @pl.when(pl.program_id(0) == 0)
        def _():
            ds_scr[...] = jnp.zeros_like(ds_scr)

        ds = ds_scr[...]
        for sub in reversed(range(grp)):
            rows = slice(sub * CHUNK, (sub + 1) * CHUNK)
            ins = [_heads_in(r, rows) for r in refs[:nr]]
            if ns:
                ins += _gdn_scalars(*[r[rows, :] for r in refs[nr:nr + ns]])
            p = ph_ref[sub]
            _, vjp = jax.vjp(lambda s, *a, p=p: fn(s, *a, p=p)[:2], sh_ref[sub], *ins)
            g = vjp((ds, _heads_in(do_ref, rows)))
            ds = g[0]
            for r, v in zip(g_refs[:nr], g[1:1 + nr]):
                _heads_out(r, rows, v)
            if ns:
                for r, v in zip(g_refs[nr:], _gdn_scalars_back(*g[1 + nr:])):
                    r[rows, :] = v
        ds_scr[...] = ds

    arrs = (*rows_in, *scal_in)
    return pl.pallas_call(
        body, grid=(n,),
        in_specs=[_scan_spec(a.shape[1], grp, n, True) for a in arrs]
        + [_hist_spec(grp, n, True), _hist_spec(grp, n, True), _scan_spec(WIDTH, grp, n, True)] + [_ANY] * nji,
        out_specs=[_scan_spec(a.shape[1], grp, n, True) for a in arrs] + [_ANY] * njo,
        out_shape=[jax.ShapeDtypeStruct(a.shape, f32) for a in arrs] + j_outs,
        scratch_shapes=[pltpu.VMEM((HEADS, HEAD_DIM, HEAD_DIM), f32)] + j_sems,
        compiler_params=pltpu.CompilerParams(dimension_semantics=("arbitrary",)),
        name=name,
    )(*arrs, s_hist, p_hist, d_out, *j_ins)


def _fn_norm_mod(ps, ts):
    nw, shift, scale = ps
    (x,) = ts
    return [_rms(x, nw, NORM_EPS) * (1.0 + scale) + shift]


def _fn_norm_mod_and_x(ps, ts):
    return _fn_norm_mod(ps, ts) + [ts[0]]


def _fn_resid_norm_mod(ps, ts):
    gate, nw, shift, scale = ps
    x, mo = ts
    x1 = x + gate * mo
    return [x1, _rms(x1, nw, NORM_EPS) * (1.0 + scale) + shift]


def _fn_gdn_pre(ps, ts):
    cw = ps[:12]
    alog, dtb, seg, tri = ps[12:]
    ba = ts[12]
    outs = []
    for part in range(3):
        x = ts[4 * part:4 * part + 4]
        w = cw[4 * part:4 * part + 4]
        conv = w[3] * x[0] + w[2] * x[1] + w[1] * x[2] + w[0] * x[3]
        u = _silu(conv)
        if part < 2:
            u = u * lax.rsqrt(_segsum(u * u, seg) + 1e-6)
            if part == 0:
                u = u * (HEAD_DIM ** -0.5)
        outs.append(u)
    beta = jax.nn.sigmoid(ba)
    g = -jnp.exp(alog) * _softplus(ba + dtb)
    gc = _xdot_l(tri, g)
    return outs + [beta, gc]


def _fn_gdn_post(ps, ts):
    ow, seg = ps
    o, z = ts
    ms = _segsum(o * o, seg) * (1.0 / HEAD_DIM)
    return [o * lax.rsqrt(ms + NORM_EPS) * ow * _silu(z)]


def _fn_rwkv_pre(ps, ts):
    mu_r, mu_k, mu_v, mu_l, mu_g, w0, w2p, a0, a2p, g2p, k_k, k_a, seg = ps
    r0, r1, k0, k1, v0, v1, l0, l1, g0, g1 = ts
    xr = r0 + (r1 - r0) * mu_r
    xk = k0 + (k1 - k0) * mu_k
    xv = v0 + (v1 - v0) * mu_v
    xl = l0 + (l1 - l0) * mu_l
    xg = g0 + (g1 - g0) * mu_g
    w = -_softplus(-(w0 + _bdot(jnp.tanh(xl), w2p))) - 0.5
    lw = -jnp.exp(w)
    a = jax.nn.sigmoid(a0 + _bdot(xl, a2p))
    g = _bdot(jax.nn.sigmoid(xg), g2p)
    kk = xk * k_k
    kk = kk * lax.rsqrt(_segsum(kk * kk, seg) + 1e-6)
    k2 = xk * (1.0 + (a - 1.0) * k_a)
    return [xr, lw, k2, xv, -kk, kk * a, g]


def _fn_rwkv_post(ps, ts):
    lw_, lb_, rk, seg = ps
    y, r, k2, v, g = ts
    inv = 1.0 / HEAD_DIM
    yc = y - _segsum(y, seg) * inv
    var = _segsum(yc * yc, seg) * inv
    yn = yc * lax.rsqrt(var + LNX_EPS) * lw_ + lb_
    bonus = _segsum(r * k2 * rk, seg) * v
    return [(yn + bonus) * g]


def _fn_merge(ps, ts):
    gla, glb, ya, yb = ts
    return [jax.nn.sigmoid(gla) * ya + jax.nn.sigmoid(glb) * yb]


def _fn_convglu(ps, ts):
    c0, c1, c2 = ps
    g0, g1, g2, up = ts
    return [_silu(c2 * g0 + c1 * g1 + c0 * g2) * up]


def _final_stage(x1, fo, tgt, gate2, nfw, tt):
    rows, d = x1.shape

    def loss_fn(gate, nw, xa, fa, tg):
        y = _rms(xa + gate * fa, nw, NORM_EPS)
        err = (y - tg) ** 2
        return 0.5 * jnp.sum(jnp.mean(err, axis=-1, keepdims=True), axis=0, keepdims=True)

    def body(x_ref, f_ref, t_ref, g_ref, w_ref, dx_ref, df_ref, dg_ref, dw_ref, l_ref):
        i = pl.program_id(0)
        args = (g_ref[...], w_ref[...], x_ref[...], f_ref[...])
        tg = t_ref[...]
        lv, vjp = jax.vjp(lambda g, w, xa, fa: loss_fn(g, w, xa, fa, tg), *args)
        dg, dw, dx, df = vjp(jnp.ones((1, 1), f32))
        dx_ref[...] = dx
        df_ref[...] = df.astype(df_ref.dtype)

        @pl.when(i == 0)
        def _():
            dg_ref[...] = jnp.zeros_like(dg_ref)
            dw_ref[...] = jnp.zeros_like(dw_ref)
            l_ref[...] = jnp.zeros_like(l_ref)

        dg_ref[...] += dg
        dw_ref[...] += dw
        l_ref[...] += jnp.broadcast_to(lv, l_ref.shape)

    row = pl.BlockSpec((tt, d), lambda i: (i, 0))
    vec = pl.BlockSpec((1, d), lambda i: (0, 0))
    return pl.pallas_call(
        body, grid=(rows // tt,),
        in_specs=[row, row, row, vec, vec],
        out_specs=[row, row, vec, vec, pl.BlockSpec((1, LANES), lambda i: (0, 0))],
        out_shape=[jax.ShapeDtypeStruct((rows, d), f32), jax.ShapeDtypeStruct((rows, d), bf16)]
        + [jax.ShapeDtypeStruct((1, d), f32)] * 2
        + [jax.ShapeDtypeStruct((1, LANES), f32)],
        compiler_params=pltpu.CompilerParams(dimension_semantics=("arbitrary",)),
        name="loss_head",
    )(x1, fo, tgt, gate2, nfw)


def _ada_fwd(c_all, w_shard, b_cols):
    def body(c_ref, w_ref, b_ref, cond_ref, mod_ref):
        cond = _silu(c_ref[...])
        cond_ref[...] = cond
        mod_ref[...] = jnp.dot(cond.astype(bf16), w_ref[...].astype(bf16), preferred_element_type=f32) + b_ref[...]

    n = w_shard.shape[1]
    return pl.pallas_call(
        body, out_shape=[jax.ShapeDtypeStruct(c_all.shape, f32), jax.ShapeDtypeStruct((c_all.shape[0], n), f32)],
        name="ada_fwd",
    )(c_all, w_shard, b_cols)


def _adamw(name, w, g, m, v):
    rows, width = w.shape
    tt = _pick(rows, 256, 8)
    c1 = 1.0 - ADAM_B1 ** ADAM_STEP
    c2 = 1.0 - ADAM_B2 ** ADAM_STEP

    def body(w_ref, g_ref, m_ref, v_ref, d_ref, mo_ref, vo_ref):
        gg = g_ref[...]
        mn = ADAM_B1 * m_ref[...] + (1.0 - ADAM_B1) * gg
        vn = ADAM_B2 * v_ref[...] + (1.0 - ADAM_B2) * (gg * gg)
        m_hat = mn / c1
        v_hat = vn / c2
        d_ref[...] = -ADAM_LR * (m_hat / (jnp.sqrt(v_hat) + ADAM_EPS) + ADAM_WD * w_ref[...])
        mo_ref[...] = mn
        vo_ref[...] = vn

    spec = pl.BlockSpec((tt, width), lambda i: (i, 0))
    return pl.pallas_call(
        body, grid=(rows // tt,), in_specs=[spec] * 4, out_specs=[spec] * 3,
        out_shape=[jax.ShapeDtypeStruct((rows, width), f32)] * 3,
        compiler_params=pltpu.CompilerParams(dimension_semantics=("parallel",)),
        name=name,
    )(w, g, m, v)


def _place():
    return lax.axis_index("x"), lax.axis_index("y"), lax.axis_index("c")


def _ag8(name, blk):
    m, w = blk.shape

    def body(x_ref, out_ref, send_sems, recv_sems, local_sem):
        x, y, c = _place()
        me, sibling = (x, y, c), (x, y, 1 - c)
        chips = _other_chips(x, y)

        def slot(px, py, pc):
            return out_ref.at[4 * px + 2 * py + pc]

        def copy(k, block, to, src=None):
            return pltpu.make_async_remote_copy(src_ref=slot(*block) if src is None else src, dst_ref=slot(*block),
                                                send_sem=send_sems.at[k], recv_sem=recv_sems.at[k], device_id=to,
                                                device_id_type=MESH_ID)

        mine = pltpu.make_async_copy(x_ref, slot(*me), local_sem)
        mine.start()
        first = [copy(0, me, sibling, src=x_ref)] + [copy(1 + j, me, (*chip, c), src=x_ref) for j, chip in enumerate(chips)]
        for cp in first:
            cp.start()
        passed = [copy(4 + j, (*chip, c), sibling) for j, chip in enumerate(chips)]
        for j, chip in enumerate(chips):
            copy(1 + j, (*chip, c), me).wait_recv()
            passed[j].start()
        copy(0, sibling, me).wait_recv()
        for j, chip in enumerate(chips):
            copy(4 + j, (*chip, 1 - c), me).wait_recv()
        for cp in first + passed:
            cp.wait_send()
        mine.wait()

    return pl.pallas_call(
        body, out_shape=jax.ShapeDtypeStruct((8, m, w), blk.dtype),
        in_specs=[pl.BlockSpec(memory_space=pltpu.VMEM)], out_specs=pl.BlockSpec(memory_space=pltpu.VMEM),
        scratch_shapes=[pltpu.SemaphoreType.DMA((7,)), pltpu.SemaphoreType.DMA((7,)), pltpu.SemaphoreType.DMA],
        name=name,
    )(blk)


def _ag8_job(blk):
    def plan(x_refs, out_refs, send_sems, recv_sems):
        x_ref, out_ref = x_refs[0], out_refs[0]
        x, y, c = _place()
        me, sibling = (x, y, c), (x, y, 1 - c)
        chips = _other_chips(x, y)
        slot = lambda px, py, pc: out_ref.at[4 * px + 2 * py + pc]
        cp = lambda k, block, to, src=None: _rcopy(slot(*block) if src is None else src, slot(*block), send_sems, recv_sems, k, to)
        first = [cp(0, me, sibling, x_ref)] + [cp(1 + j, me, (*chip, c), x_ref) for j, chip in enumerate(chips)]
        landed = [cp(1 + j, (*chip, c), me) for j, chip in enumerate(chips)]
        passed = [cp(4 + j, (*chip, c), sibling) for j, chip in enumerate(chips)]
        handed = [cp(0, sibling, me)] + [cp(4 + j, (*chip, 1 - c), me) for j, chip in enumerate(chips)]
        mine = pltpu.make_async_copy(x_ref, slot(*me), send_sems.at[7])
        return first, landed, passed, handed, mine

    def start(*refs):
        first, _, _, _, mine = plan(*refs)
        mine.start()
        for cp in first:
            cp.start()

    def forward(*refs):
        _, landed, passed, _, _ = plan(*refs)
        for arrived, fw in zip(landed, passed):
            arrived.wait_recv()
            fw.start()

    def finish(*refs):
        first, _, passed, handed, mine = plan(*refs)
        for cp in handed:
            cp.wait_recv()
        for cp in first + passed:
            cp.wait_send()
        mine.wait()

    return dict(ins=[blk], out_shapes=[jax.ShapeDtypeStruct((8,) + blk.shape, blk.dtype)], nsem=8, start=start,
                forward=forward, finish=finish)


def _other_chips(x, y):
    return [(1 - x, y), (x, 1 - y), (1 - x, 1 - y)]


_ANY = pl.BlockSpec(memory_space=pl.ANY)


def _rcopy(src, dst, send_sems, recv_sems, k, dev):
    return pltpu.make_async_remote_copy(src_ref=src, dst_ref=dst, send_sem=send_sems.at[k], recv_sem=recv_sems.at[k],
                                        device_id=dev, device_id_type=MESH_ID)


def _run_job(name, job):
    j_ins, j_outs, j_sems = _job_parts(job)
    n = len(j_ins)

    def body(*refs):
        for phase in _JOB_PHASES:
            _job_phase(job, phase, refs[:n], refs[n:n + len(j_outs)], refs[n + len(j_outs):])

    return pl.pallas_call(body, out_shape=j_outs, in_specs=[_ANY] * n, out_specs=[_ANY] * len(j_outs),
                          scratch_shapes=j_sems, name=name)(*j_ins)


def _ag4_relay_job(ws):
    n = len(ws)

    def plan(w_refs, out_refs, send_sems, recv_sems):
        x, y, c = _place()
        chip, xn, yn, dg = 2 * x + y, 2 * (1 - x) + y, 2 * x + (1 - y), 2 * (1 - x) + (1 - y)
        sibling, to_x, to_y = (x, y, 1 - c), (1 - x, y, c), (x, 1 - y, c)
        rc = lambda src, dst, k, dev: _rcopy(src, dst, send_sems, recv_sems, k, dev)
        p = dict(direct=[], own=[], landed=[], relay=[], relayed=[], hand=[], handed=[])
        for t, w in enumerate(ws):
            rh = w.shape[0] // 2
            rq = rh // 2
            mine, other = pl.ds(c * rh, rh), pl.ds((1 - c) * rh, rh)
            qa, qb = pl.ds(c * rh, rq), pl.ds(c * rh + rq, rq)
            o = out_refs[t]
            p["direct"] += [rc(w_refs[t].at[mine], o.at[chip, mine], 8 * t, to_x),
                            rc(w_refs[t].at[mine], o.at[chip, mine], 8 * t + 1, to_y)]
            p["own"].append(rc(w_refs[t], o.at[chip], 8 * t + 7, sibling))
            p["landed"] += [rc(o.at[xn, mine], o.at[xn, mine], 8 * t, to_x), rc(o.at[yn, mine], o.at[yn, mine], 8 * t + 1, to_y)]
            p["relay"] += [rc(o.at[xn, qa], o.at[xn, qa], 8 * t + 2, to_y), rc(o.at[yn, qb], o.at[yn, qb], 8 * t + 3, to_x)]
            p["relayed"] += [rc(o.at[dg, qa], o.at[dg, qa], 8 * t + 2, to_y), rc(o.at[dg, qb], o.at[dg, qb], 8 * t + 3, to_x)]
            p["hand"] += [rc(o.at[j, mine], o.at[j, mine], 8 * t + 4 + k, sibling) for k, j in enumerate((xn, yn, dg))]
            p["handed"] += [rc(o.at[j, other], o.at[j, other], 8 * t + 4 + k, sibling) for k, j in enumerate((xn, yn, dg))]
        return p

    def start(*refs):
        p = plan(*refs)
        for cp in p["direct"] + p["own"]:
            cp.start()

    def forward(*refs):
        p = plan(*refs)
        for t in range(n):
            for k in range(2):
                p["landed"][2 * t + k].wait_recv()
                p["relay"][2 * t + k].start()
                p["hand"][3 * t + k].start()
        for t in range(n):
            for k in range(2):
                p["relayed"][2 * t + k].wait_recv()
            p["hand"][3 * t + 2].start()

    def finish(*refs):
        p = plan(*refs)
        for cp in p["handed"] + p["own"]:
            cp.wait_recv()
        for cp in p["direct"] + p["own"] + p["relay"] + p["hand"]:
            cp.wait_send()

    return dict(ins=ws, out_shapes=[jax.ShapeDtypeStruct((4,) + w.shape, w.dtype) for w in ws], nsem=8 * n,
                start=start, forward=forward, finish=finish)


def _ag4_job(ws):
    n = len(ws)

    def plan(w_refs, out_refs, send_sems, recv_sems):
        x, y, c = _place()
        chip = 2 * x + y
        sibling = (x, y, 1 - c)
        chips = _other_chips(x, y)
        mine = [pl.ds(c * (w.shape[0] // 2), w.shape[0] // 2) for w in ws]
        other = [pl.ds((1 - c) * (w.shape[0] // 2), w.shape[0] // 2) for w in ws]
        rc = lambda src, dst, k, dev: _rcopy(src, dst, send_sems, recv_sems, k, dev)
        first = [rc(w_refs[t].at[mine[t]], out_refs[t].at[chip, mine[t]], 7 * t + k, (px, py, c))
                 for t in range(n) for k, (px, py) in enumerate(chips)]
        own = [rc(w_refs[t], out_refs[t].at[chip], 7 * t + 6, sibling) for t in range(n)]
        landed = [rc(out_refs[t].at[2 * px + py, mine[t]], out_refs[t].at[2 * px + py, mine[t]], 7 * t + k, (px, py, c))
                  for t in range(n) for k, (px, py) in enumerate(chips)]
        forward = [rc(out_refs[t].at[2 * px + py, mine[t]], out_refs[t].at[2 * px + py, mine[t]], 7 * t + 3 + k, sibling)
                   for t in range(n) for k, (px, py) in enumerate(chips)]
        handed = [rc(out_refs[t].at[2 * px + py, other[t]], out_refs[t].at[2 * px + py, other[t]], 7 * t + 3 + k, sibling)
                  for t in range(n) for k, (px, py) in enumerate(chips)]
        return first, own, landed, forward, handed

    def start(*refs):
        first, own, _, _, _ = plan(*refs)
        for cp in first + own:
            cp.start()

    def forward(*refs):
        _, _, landed, passed, _ = plan(*refs)
        for arrived, fw in zip(landed, passed):
            arrived.wait_recv()
            fw.start()

    def finish(*refs):
        first, own, _, passed, handed = plan(*refs)
        for cp in handed + own:
            cp.wait_recv()
        for cp in first + own + passed:
            cp.wait_send()

    return dict(ins=ws, out_shapes=[jax.ShapeDtypeStruct((4,) + w.shape, w.dtype) for w in ws], nsem=7 * n,
                start=start, forward=forward, finish=finish)


def _sibling_swap_job(gs):
    n = len(gs)

    def plan(g_refs, recv_refs, send_sems, recv_sems):
        x, y, c = _place()
        return [_rcopy(g_refs[t].at[s_, pl.ds((1 - c) * (gs[t].shape[1] // 2), gs[t].shape[1] // 2)], recv_refs[t].at[s_],
                       send_sems, recv_sems, 4 * t + s_, (x, y, 1 - c)) for t in range(n) for s_ in range(4)]

    def start(*refs):
        for cp in plan(*refs):
            cp.start()

    def finish(*refs):
        copies = plan(*refs)
        for cp in copies:
            cp.wait_recv()
        for cp in copies:
            cp.wait_send()

    return dict(ins=gs, out_shapes=[jax.ShapeDtypeStruct((4, g.shape[1] // 2, g.shape[2]), g.dtype) for g in gs],
                nsem=4 * n, start=start, finish=finish)


def _chip_exchange_job(ps):
    n = len(ps)

    def plan(p_refs, recv_refs, send_sems, recv_sems):
        x, y, c = _place()
        return [_rcopy(p_refs[t].at[2 * px + py], recv_refs[t].at[k], send_sems, recv_sems, 3 * t + k, (px, py, c))
                for t in range(n) for k, (px, py) in enumerate(_other_chips(x, y))]

    def start(*refs):
        for cp in plan(*refs):
            cp.start()

    def finish(*refs):
        copies = plan(*refs)
        for cp in copies:
            cp.wait_recv()
        for cp in copies:
            cp.wait_send()

    return dict(ins=ps, out_shapes=[jax.ShapeDtypeStruct((3,) + p.shape[1:], p.dtype) for p in ps], nsem=3 * n,
                start=start, finish=finish)


_JOIN_PIECES = 4


def _rs_sibling_join(qs):
    n = len(qs)
    npc = _JOIN_PIECES

    def body(*refs):
        q_refs, out_refs = refs[:n], refs[n:2 * n]
        send_sems, recv_sems = refs[2 * n:]
        x, y, c = _place()
        copies = []
        for t in range(n):
            rh = qs[t].shape[0] // 2
            pr = rh // npc
            for i in range(npc):
                rows = pl.ds(c * rh + i * pr, pr)
                cp = _rcopy(q_refs[t].at[rows], out_refs[t].at[rows], send_sems, recv_sems, npc * t + i,
                            (x, y, 1 - c))
                cp.start()
                copies.append(cp)
        for t in range(n):
            rh = qs[t].shape[0] // 2
            pr = rh // npc
            for i in range(npc):
                rows = pl.ds((1 - c) * rh + i * pr, pr)
                _rcopy(q_refs[t].at[rows], out_refs[t].at[rows], send_sems, recv_sems, npc * t + i,
                       (x, y, 1 - c)).wait_recv()
        for cp in copies:
            cp.wait_send()

    return pl.pallas_call(
        body, out_shape=[jax.ShapeDtypeStruct(q.shape, q.dtype) for q in qs],
        in_specs=[_ANY] * n, out_specs=[_ANY] * n, input_output_aliases={t: t for t in range(n)},
        scratch_shapes=[pltpu.SemaphoreType.DMA((npc * n,)), pltpu.SemaphoreType.DMA((npc * n,))],
        name="grads_sibling_join",
    )(*qs)


def _add_half(name, g, recv, chip, ci):
    S, r, w = g.shape
    rh = r // 2
    tt = _pick(rh, 256, 16)
    nb = rh // tt

    def body(chip_ref, core_ref, a_ref, b_ref, own_ref, ob_ref):
        v = a_ref[...] + b_ref[...]
        ob_ref[...] = v.astype(bf16)

        @pl.when(pl.program_id(1) == chip_ref[0])
        def _():
            own_ref[...] = v[0]

    grid_spec = pltpu.PrefetchScalarGridSpec(
        num_scalar_prefetch=2, grid=(nb, S),
        in_specs=[pl.BlockSpec((1, tt, w), lambda i, s_, ch, co: (s_, co[0] * nb + i, 0)),
                  pl.BlockSpec((1, tt, w), lambda i, s_, ch, co: (s_, i, 0))],
        out_specs=[pl.BlockSpec((tt, w), lambda i, s_, ch, co: (i, 0)),
                   pl.BlockSpec((1, tt, w), lambda i, s_, ch, co: (s_, i, 0))])
    return pl.pallas_call(body, grid_spec=grid_spec,
                          out_shape=[jax.ShapeDtypeStruct((rh, w), f32), jax.ShapeDtypeStruct((S, rh, w), bf16)],
                          compiler_params=pltpu.CompilerParams(dimension_semantics=("arbitrary", "arbitrary")),
                          name=name)(chip.reshape(1).astype(jnp.int32), ci.reshape(1).astype(jnp.int32), g, recv)


def _sum_chip(name, own, others, ci):
    rh, w = own.shape
    tt = _pick(rh, 256, 16)
    nb = rh // tt

    def body(core_ref, a_ref, b_ref, o_ref):
        o_ref[...] = ((a_ref[...] + b_ref[0].astype(f32)) + b_ref[1].astype(f32)) + b_ref[2].astype(f32)

    grid_spec = pltpu.PrefetchScalarGridSpec(
        num_scalar_prefetch=1, grid=(nb,),
        in_specs=[pl.BlockSpec((tt, w), lambda i, co: (i, 0)),
                  pl.BlockSpec((3, tt, w), lambda i, co: (0, i, 0))],
        out_specs=pl.BlockSpec((tt, w), lambda i, co: (co[0] * nb + i, 0)))
    return pl.pallas_call(body, grid_spec=grid_spec, out_shape=jax.ShapeDtypeStruct((2 * rh, w), f32),
                          name=name)(ci.reshape(1).astype(jnp.int32), own, others)


def _sum_devices(gathered, late, late_rows, head_row):
    _, rows, width = gathered.shape

    def body(g_ref, l_ref, out_ref, head_ref):
        acc, acc_l = g_ref[0], l_ref[0]
        for d in range(1, 8):
            acc = acc + g_ref[d]
            acc_l = acc_l + l_ref[d]
        out_ref[...] = acc
        for k, r in enumerate(late_rows):
            out_ref[r:r + 1, :] = acc_l[k:k + 1, :]
        row = acc[head_row:head_row + 1, :]
        hs = row[:, 0:HEAD_DIM]
        for h in range(1, HEADS):
            hs = hs + row[:, h * HEAD_DIM:(h + 1) * HEAD_DIM]
        head_ref[...] = jnp.zeros_like(head_ref)
        head_ref[0:1, 0:HEAD_DIM] = hs

    return pl.pallas_call(
        body, out_shape=[jax.ShapeDtypeStruct((rows, width), f32), jax.ShapeDtypeStruct((8, LANES), f32)],
        name="small_grads_sum",
    )(gathered, late)


def _pack(arrs, rows_mult, dtype):
    flat = jnp.concatenate([a.reshape(-1).astype(dtype) for a in arrs])
    per = PACK_W * rows_mult
    total = -(-flat.shape[0] // per) * per
    return jnp.pad(flat, (0, total - flat.shape[0])).reshape(total // PACK_W, PACK_W)


def _unpack(buf, shapes):
    flat = buf.reshape(-1)
    out, off = [], 0
    for s in shapes:
        n = int(np.prod(s))
        out.append(flat[off:off + n].reshape(s))
        off += n
    return out


_BIG = ["w_in", "w_branch_gdn", "w_branch_rwkv", "w_out", "w_ffn_in", "w_ffn_out"]
_MID = ["conv_gdn", "conv_ffn", "w2", "a2", "g2"]
_SMALL =["b_ada", "norm1_w", "a_log", "dt_bias", "onorm_gdn", "mu_rwkv", "w0", "a0", "k_k", "k_a", "r_k", "lnx_w",
          "lnx_b", "norm2_w", "norm_f_w"]
_ORDER = ["w_ada", "b_ada", "norm1_w", "w_in", "conv_gdn", "a_log", "dt_bias", "onorm_gdn", "w_branch_gdn", "mu_rwkv",
          "w0", "w2", "a0", "a2", "g2", "k_k", "k_a", "r_k", "lnx_w", "lnx_b", "w_branch_rwkv", "w_out", "norm2_w",
          "w_ffn_in", "conv_ffn", "w_ffn_out", "norm_f_w"]


_WIN_SEGMENTS = [(0, 1536, 0), (2064, 3600, 1536), (1536, 2048, 3072), (3600, 3728, 3584), (2048, 2064, 3712),
                 (3728, 3888, 3840), (3888, 5936, 4096)]
_WIN_PADDED = 6144
_COL_QKV, _COL_RKV, _COL_Z = (0, 1, 2), (3, 4, 5), 6
_COL_LORA, _COL_BA = 28, 29
_COL_GATE_LORA = 15
_COL_GL = (4, 5)
_JOINT_QKV, _JOINT_RKV, _JOINT_Z, _JOINT_GL = 0, 1, 6, 2
_SMALL_BLOCKS_AT = 3584


def _win_pad(shards):
    n = shards.shape[2]
    parts, at = [], 0
    for lo, hi, dst in _WIN_SEGMENTS:
        if dst > at:
            parts.append(jnp.zeros((shards.shape[1], dst - at), shards.dtype))
        c = lo
        while c < hi:
            j = c // n
            e = min(hi, (j + 1) * n)
            parts.append(shards[j][:, c - j * n:e - j * n])
            c = e
        at = dst + hi - lo
    if at < _WIN_PADDED:
        parts.append(jnp.zeros((shards.shape[1], _WIN_PADDED - at), shards.dtype))
    return jnp.concatenate(parts, axis=1)


def _win_unpad_shards(g, n):
    shards = []
    for j in range(4):
        parts = []
        for lo, hi, dst in sorted(_WIN_SEGMENTS):
            a, b = max(lo, j * n), min(hi, (j + 1) * n)
            if a < b:
                parts.append(g[:, dst + a - lo:dst + b - lo])
        shards.append(jnp.concatenate(parts, axis=1))
    return jnp.stack(shards)


def kernel(x, c, w_ada, b_ada, norm1_w, w_in, conv_gdn, a_log, dt_bias, onorm_gdn, w_branch_gdn, mu_rwkv, w0, w2, a0, a2, g2, k_k, k_a, r_k, lnx_w, lnx_b, w_branch_rwkv, w_out, norm2_w, w_ffn_in, conv_ffn, w_ffn_out, norm_f_w, loss_target, m_w_ada, m_b_ada, m_norm1_w, m_w_in, m_conv_gdn, m_a_log, m_dt_bias, m_onorm_gdn, m_w_branch_gdn, m_mu_rwkv, m_w0, m_w2, m_a0, m_a2, m_g2, m_k_k, m_k_a, m_r_k, m_lnx_w, m_lnx_b, m_w_branch_rwkv, m_w_out, m_norm2_w, m_w_ffn_in, m_conv_ffn, m_w_ffn_out, m_norm_f_w, v_w_ada, v_b_ada, v_norm1_w, v_w_in, v_conv_gdn, v_a_log, v_dt_bias, v_onorm_gdn, v_w_branch_gdn, v_mu_rwkv, v_w0, v_w2, v_a0, v_a2, v_g2, v_k_k, v_k_a, v_r_k, v_lnx_w, v_lnx_b, v_w_branch_rwkv, v_w_out, v_norm2_w, v_w_ffn_in, v_conv_ffn, v_w_ffn_out, v_norm_f_w):
    args = dict(locals())
    W = {n: args[n] for n in _ORDER}
    Mo = {n: args["m_" + n] for n in _ORDER}
    Vo = {n: args["v_" + n] for n in _ORDER}
    shapes = {n: W[n].shape for n in _ORDER}
    sq = lambda a: a.reshape(a.shape[-2:]) if a.ndim == 3 else a.reshape(1, -1)
    row = lambda a: a.reshape(1, -1)

    xi, yi, ci = lax.axis_index("x"), lax.axis_index("y"), lax.axis_index("c")
    dev = 4 * xi + 2 * yi + ci
    chip = 2 * xi + yi

    x2 = x[0]
    tgt = loss_target[0]
    T, D = x2.shape
    tt_l = _pick(T, 512, CHUNK)
    tt_p = _pick(T, 256, CHUNK)
    tt_h = _pick(T, 128, CHUNK)

    small_blk = _pack([c] + [W[n] for n in _MID], 8, f32)
    small_all, win_s = _run_job("gather_c_mid_and_w_in", [_ag8_job(small_blk), _ag4_relay_job([sq(W["w_in"]).astype(bf16)])])
    c_all = small_all[:, 0, :]
    per_chip = small_all[0::2].reshape(4, -1)[:, D:]
    mid_w, at = {}, 0
    for n in _MID:
        nr, nw = shapes[n][1:]
        mid_w[n] = per_chip[:, at:at + nr * nw].reshape(4, nr, nw).transpose(1, 0, 2).reshape(nr, 4 * nw)
        at += nr * nw
    conv_gdn_f, conv_ffn_f = mid_w["conv_gdn"], mid_w["conv_ffn"]
    w2f, a2f, g2f = mid_w["w2"], mid_w["a2"], mid_w["g2"]

    later_weights = _ag4_job([sq(W[n]).astype(bf16) for n in _BIG[1:]])
    win_p = _win_pad(win_s)
    zpad = lambda a, top, bot: jnp.pad(a, ((top, bot), (0, 0)))
    w2p, a2p, g2p = zpad(w2f, 0, 64), zpad(a2f, 64, 0), zpad(g2f, 0, 96)

    ncol = shapes["w_ada"][2]
    b_cols = lax.dynamic_slice(sq(W["b_ada"]), (0, chip * ncol), (1, ncol))
    cond16, mod_cols = _ada_fwd(jnp.pad(c_all, ((0, 8), (0, 0))), sq(W["w_ada"]), b_cols)
    mod_all = _ag8("gather_mod", mod_cols[:8])
    mod_mine = lax.dynamic_slice(mod_all[0::2], (0, dev, 0), (4, 1, ncol)).reshape(1, 4 * ncol)
    shift1, scale1, gate1, shift2, scale2, gate2 = [mod_mine[:, i * D:(i + 1) * D] for i in range(6)]

    seg = _seg_matrix(WIDTH, HEAD_DIM)
    norm1 = [sq(W["norm1_w"]), shift1, scale1]
    h1 = _stage_fwd("norm_mod1", _fn_norm_mod, [_whole(x2)], norm1, [(D, bf16)], tt_l)[0]
    p = _matmul("in_proj", h1, win_p, "nn", tm=2048)

    cgq = [row(conv_gdn_f[j, part * WIDTH:(part + 1) * WIDTH]) for part in range(3) for j in range(4)]
    lane_pad = lambda a: jnp.pad(row(a), ((0, 0), (8, LANES - 16)))
    gdn_pre_ps = cgq + [lane_pad(W["a_log"]), lane_pad(W["dt_bias"]), seg, _chunk_tri(tt_p, CHUNK)]
    gdn_pre_ts = [(p, WIDTH, ci_, (0, 1, 2, 3)) for ci_ in _COL_QKV] + [(p, LANES, _COL_BA, None)]
    q_, k_, v_, beta_t, gc_t = _stage_fwd("gdn_pre", _fn_gdn_pre, gdn_pre_ts, gdn_pre_ps,
                                          [(WIDTH, f32)] * 3 + [(LANES, f32)] * 2, tt_p)
    o_, gdn_hist, gdn_inv = _scan_fwd("gdn_scan", _gdn_chunk, [q_, k_, v_], [beta_t, gc_t])
    ow512 = jnp.tile(row(W["onorm_gdn"]), (1, HEADS))
    gdn_post_ts = [_whole(o_), (p, WIDTH, _COL_Z, None)]
    ya = _stage_fwd("gdn_post", _fn_gdn_post, gdn_post_ts, [ow512, seg], [(WIDTH, bf16)], tt_l)[0]

    mu = sq(W["mu_rwkv"])
    rw_ps = [mu[:, 0:512], mu[:, 512:1024], mu[:, 1024:1536], mu[:, 1536:1664], jnp.pad(mu[:, 1664:1824], ((0, 0), (0, 96))),
             sq(W["w0"]), w2p, sq(W["a0"]), a2p, g2p, sq(W["k_k"]), sq(W["k_a"]), seg]
    rw_ts = [(p, WIDTH, ci_, (0, 1)) for ci_ in _COL_RKV] + [(p, LANES, _COL_LORA, (0, 1)),
                                                              (p, 256, _COL_GATE_LORA, (0, 1))]
    rw_out = _stage_fwd("rwkv_pre", _fn_rwkv_pre, rw_ts, rw_ps, [(WIDTH, f32)] * 7, tt_p)
    r_, lw_, k2_, vv_, na_, b_, g_ = rw_out
    rw_ins = [r_, lw_, k2_, vv_, na_, b_]
    y_, rw_hist, rw_inv, wbg_s, wbr_s, wout_s, wfi_s, wfo_s = _scan_fwd("rwkv_scan", _rwkv_chunk, rw_ins, job=later_weights)
    wout_f = wout_s.reshape(D, D)
    wfo = wfo_s.reshape(D_FF, D)
    rwp_ps = [sq(W["lnx_w"]), sq(W["lnx_b"]), row(W["r_k"]), seg]
    rwp_ts = [_whole(y_), _whole(r_), _whole(k2_), _whole(vv_), _whole(g_)]
    yb = _stage_fwd("rwkv_post", _fn_rwkv_post, rwp_ts, rwp_ps, [(WIDTH, bf16)], tt_l)[0]

    big_a = _matmul("branch_gdn", ya, wbg_s, "nn", shards=4)
    big_b = _matmul("branch_rwkv", yb, wbr_s, "nn", shards=4)
    merge_ts = [(p, D, _COL_GL[0], None), (p, D, _COL_GL[1], None), _whole(big_a), _whole(big_b)]
    merged = _stage_fwd("merge", _fn_merge, merge_ts, [], [(D, bf16)], tt_l)[0]
    mo = _matmul("out_proj", merged, wout_f, "nn")
    norm2 = [gate1, sq(W["norm2_w"]), shift2, scale2]
    x1, h2 = _stage_fwd("resid_norm_mod2", _fn_resid_norm_mod, [_whole(x2), _whole(mo)], norm2, [(D, f32), (D, bf16)], tt_l)
    f = _matmul("ffn_in", h2, wfi_s, "nn", shards=4, tm=2048, tn=1408)
    cg_ps = [row(conv_ffn_f[j]) for j in range(3)]
    cg_ts = [(f, D_FF, 0, (0, 1, 2)), (f, D_FF, 1, None)]
    act = _stage_fwd("convglu", _fn_convglu, cg_ts, cg_ps, [(D_FF, bf16)], tt_h)[0]
    fo = _matmul("ffn_out", act, wfo, "nn", tk=D_FF)

    dx1_a, dfo, dgate2, dnormf, loss_part = _final_stage(x1, fo, tgt, gate2, row(W["norm_f_w"]), tt_l)

    dact = _matmul("d_act", dfo, wfo, "nt", tn=D_FF)
    g_wfo = _matmul("g_ffn_out", act, dfo, "tn", tm=1408, tk=2048)
    _, dcf, df = _stage_bwd("convglu_bwd", _fn_convglu, cg_ts, cg_ps, [[_whole(dact)]], tt_h, [True] * 2, [True] * 3,
                            joint=([0, 1], 2 * D_FF, 0, None), dtypes=[bf16])
    g_wfi = _matmul("g_ffn_in", h2, df, "tn", shards=4, tn=1408, tk=2048)
    g_wfo_s = g_wfo.reshape(4, D_FF // 4, D)
    dh2, recv_wfi, recv_wfo = _matmul("d_h2", df, wfi_s, "nt", shards=4, tm=2048, tk=1408,
                                      job=_sibling_swap_job([g_wfi, g_wfo_s]))
    (dx_a, dmo), (dgate1, dnorm2, dshift2, dscale2), _ = _stage_bwd(
        "resid_norm_mod2_bwd", _fn_resid_norm_mod, [_whole(x2), _whole(mo)], norm2,
        [[_whole(dx1_a)], [_whole(dh2)]], tt_l, [True, True], [True] * 4, dtypes=[f32, bf16])
    dmerged = _matmul("d_merged", dmo, wout_f, "nt")
    g_wout = _matmul("g_out_proj", merged, dmo, "tn", tk=T)
    (dbig_a, dbig_b), _, dp = _stage_bwd("merge_bwd", _fn_merge, merge_ts, [], [[_whole(dmerged)]], tt_l, [True] * 4, [],
                                         joint=([0, 1], p.shape[1], _JOINT_GL, None), dtypes=[bf16] * 3)
    g_wbg = _matmul("g_branch_gdn", ya, dbig_a, "tn", shards=4, tk=T)
    g_wbr = _matmul("g_branch_rwkv", yb, dbig_b, "tn", shards=4, tk=T)
    g_wout_s = g_wout.reshape(4, D // 4, D)
    dyb = _matmul("d_yb", dbig_b, wbr_s, "nt", shards=4)
    dya, *recv_mix = _matmul("d_ya", dbig_a, wbg_s, "nt", shards=4, job=_sibling_swap_job([g_wbg, g_wbr, g_wout_s]))

    (dy_, dr_p, dk2_p, dv_p, dg_p), (dlnxw, dlnxb, drk), _ = _stage_bwd(
        "rwkv_post_bwd", _fn_rwkv_post, rwp_ts, rwp_ps, [[_whole(dyb)]], tt_l, [True] * 5, [True, True, True, False])
    gs_a = [g_wbg, g_wbr, g_wout_s, g_wfi, g_wfo_s]
    pairs_a = [_add_half("grads_pair_sum%d" % (t + 1), g, r_, chip, ci)
               for t, (g, r_) in enumerate(zip(gs_a, recv_mix + [recv_wfi, recv_wfo]))]
    dr_c, dlw_c, dk2_c, dv_c, dna_c, db_c, *others_a = _scan_bwd(
        "rwkv_scan_bwd", _rwkv_chunk, rw_ins, [], rw_hist, rw_inv, dy_, job=_chip_exchange_job([pb for _, pb in pairs_a]))
    rw_cots = [[_whole(dr_p), _whole(dr_c)], [_whole(dlw_c)], [_whole(dk2_p), _whole(dk2_c)],
               [_whole(dv_p), _whole(dv_c)], [_whole(dna_c)], [_whole(db_c)], [_whole(dg_p)]]
    (dl_, dg_), rw_dp, dp = _stage_bwd("rwkv_pre_bwd", _fn_rwkv_pre, rw_ts, rw_ps, rw_cots, tt_p, [True] * 5,
                                       [True] * 12 + [False], joint=([0, 1, 2], p.shape[1], _JOINT_RKV, dp), dtypes=[bf16] * 3)
    dmu_r, dmu_k, dmu_v, dmu_l, dmu_g, dw0, dw2p, da0, da2p, dg2p, dkk, dka = rw_dp

    (do_,), (dow512,), dp = _stage_bwd("gdn_post_bwd", _fn_gdn_post, gdn_post_ts, [ow512, seg], [[_whole(dya)]], tt_l,
                                       [True, True], [True, False], joint=([1], p.shape[1], _JOINT_Z, dp), dtypes=[f32, bf16])
    d_gdn = _scan_bwd("gdn_scan_bwd", _gdn_chunk, [q_, k_, v_], [beta_t, gc_t], gdn_hist, gdn_inv, do_)
    gdn_cots = [[_whole(a)] for a in d_gdn]
    (dba,), gdn_dp, dp = _stage_bwd("gdn_pre_bwd", _fn_gdn_pre, gdn_pre_ts, gdn_pre_ps, gdn_cots, tt_p, [True] * 4,
                                    [True] * 14 + [False, False], joint=([0, 1, 2], p.shape[1], _JOINT_QKV, dp), dtypes=[bf16] * 2)
    dp = lax.dynamic_update_slice(dp, jnp.concatenate([dl_, dba, dg_], axis=1), (0, _SMALL_BLOCKS_AT))
    g_conv_gdn = jnp.concatenate([jnp.concatenate([gdn_dp[4 * part + j] for part in range(3)], axis=1) for j in range(4)], axis=0)
    g_conv_ffn = jnp.concatenate(dcf, axis=0)
    g_mu = jnp.concatenate([dmu_r, dmu_k, dmu_v, dmu_l, dmu_g[:, :160]], axis=1)
    late_zero = jnp.zeros((1, D), f32)
    dmod_early = jnp.concatenate([late_zero, late_zero, dgate1, dshift2, dscale2, dgate2], axis=1)
    small_parts = {"b_ada": dmod_early, "norm1_w": late_zero, "a_log": gdn_dp[12][:, 8:16], "dt_bias": gdn_dp[13][:, 8:16],
                   "mu_rwkv": g_mu, "w0": dw0, "a0": da0, "k_k": dkk, "k_a": dka, "r_k": drk, "lnx_w": dlnxw,
                   "lnx_b": dlnxb, "norm2_w": dnorm2, "norm_f_w": dnormf}
    small_names = [n for n in _SMALL if n != "onorm_gdn"]
    mid_full = [g_conv_gdn, g_conv_ffn, dw2p[0:64], da2p[64:128], dg2p[0:160]]
    body_rows = _pack([small_parts[n] for n in small_names] + [loss_part[:, 0:1]] + mid_full, 1, f32)
    head_row = body_rows.shape[0]
    small_g = jnp.concatenate([body_rows, jnp.pad(dow512, ((0, 0), (0, PACK_W - WIDTH)))], axis=0)
    small_g = jnp.pad(small_g, ((0, -small_g.shape[0] % 8), (0, 0)))

    g_win_pad, small_all_g = _matmul("g_in_proj", h1, dp, "tn", tk=T, job=_ag8_job(small_g))
    g_win_s = _win_unpad_shards(g_win_pad, shapes["w_in"][2])
    pair_win = _add_half("grads_pair_sum0", g_win_s, _run_job("w_in_grads_sibling_swap", _sibling_swap_job([g_win_s]))[0],
                         chip, ci)
    dh1, others_win = _matmul("d_h1", dp, win_p, "nt", tk=3072, job=_chip_exchange_job([pair_win[1]]))
    (grad_x,), (dnorm1, dshift1, dscale1), _ = _stage_bwd("norm_mod1_bwd", _fn_norm_mod_and_x, [_whole(x2)], norm1,
                                                          [[_whole(dh1)], [_whole(dx_a)]], tt_l, [True], [True] * 3)
    late_all = _ag8("gather_late_grads", jnp.pad(jnp.concatenate([dshift1, dscale1, dnorm1], axis=0), ((0, 5), (0, 0))))
    small_sum, head_sum = _sum_devices(small_all_g, late_all, (0, 1, 6), head_row)
    small_shapes = [shapes[n][1:] if n != "norm_f_w" else shapes[n] for n in small_names]
    un = _unpack(small_sum, small_shapes + [(1,)] + [g.shape for g in mid_full])
    small_grads = dict(zip(small_names, un))
    loss = un[len(small_names)].reshape(())
    small_grads["onorm_gdn"] = head_sum[0, 0:HEAD_DIM]
    for n, g in zip(_MID, un[len(small_names) + 1:]):
        wcols = shapes[n][2]
        small_grads[n] = lax.dynamic_slice(g, (0, chip * wcols), (g.shape[0], wcols))

    dmod_all = jnp.concatenate([late_all[:, 0, :], late_all[:, 1, :], small_all_g[:, 2:6, :].reshape(8, 4 * PACK_W)], axis=1)
    dmod_cols = lax.dynamic_slice(dmod_all, (0, chip * ncol), (8, ncol))
    g_wada = _matmul("g_w_ada", cond16, jnp.pad(dmod_cols, ((0, 8), (0, 0))), "tn")

    pairs = [pair_win] + pairs_a
    others = [others_win] + others_a
    halves = [_sum_chip("grads_chip_sum%d" % t, own, o_, ci) for t, ((own, _), o_) in enumerate(zip(pairs, others))]
    big_grads = dict(zip(_BIG, _rs_sibling_join(halves)))

    res = {tag: {} for tag in ("grad", "delta", "new_m", "new_v")}

    def put(n, g, d, m_, v_):
        for tag, val in zip(("grad", "delta", "new_m", "new_v"), (g, d, m_, v_)):
            res[tag][n] = val.reshape(shapes[n])

    for n in _BIG:
        put(n, big_grads[n], *_adamw("adamw_" + n, sq(W[n]), big_grads[n], sq(Mo[n]), sq(Vo[n])))
    put("w_ada", g_wada, *_adamw("adamw_w_ada", sq(W["w_ada"]), g_wada, sq(Mo["w_ada"]), sq(Vo["w_ada"])))
    rest = _SMALL + _MID
    pk = lambda d: _pack([d[n] for n in rest], 8, f32)
    sg = pk(small_grads)
    sm = _adamw("adamw_small", pk(W), sg, pk(Mo), pk(Vo))
    for tag, buf in zip(("grad", "delta", "new_m", "new_v"), (sg,) + tuple(sm)):
        res[tag].update(zip(rest, _unpack(buf, [shapes[n] for n in rest])))
    outs = [loss, grad_x.reshape(x.shape)]
    for tag in ("grad", "delta", "new_m", "new_v"):
        outs += [res[tag][n] for n in _ORDER]
    return tuple(outs)
```

```python
import numpy as np
import jax
import jax.numpy as jnp
from jax import lax
from jax.experimental import pallas as pl
from jax.experimental.pallas import tpu as pltpu

f32 = jnp.float32
bf16 = jnp.bfloat16

LANES = 128
HEADS = 8
HEAD_DIM = 64
WIDTH = HEADS * HEAD_DIM
CHUNK = 64
D_FF = 2816
NORM_EPS = 1e-6
LNX_EPS = 64e-5
PACK_W = 1024
MESH_ID = pl.DeviceIdType.MESH

ADAM_LR, ADAM_B1, ADAM_B2, ADAM_EPS, ADAM_WD, ADAM_STEP = 0.001, 0.9, 0.999, 1e-08, 0.01, 10


def _pick(n, target, mult):
    if n <= target:
        return n
    best = None
    for t in range(mult, target + 1, mult):
        if n % t == 0:
            best = t
    assert best is not None, (n, target, mult)
    return best


def _split_bf16(x, n):
    parts, r = [], x
    for i in range(n):
        p = r.astype(bf16)
        parts.append(p)
        if i + 1 < n:
            r = r - p.astype(f32)
    return parts


def _xdot_r_impl(x, m, n, dims):
    acc = None
    for p in _split_bf16(x, n):
        t = lax.dot_general(p, m, dims, preferred_element_type=f32)
        acc = t if acc is None else acc + t
    return acc


def _make_xdot_r(n):
    nn = (((1,), (0,)), ((), ()))
    nt = (((1,), (1,)), ((), ()))

    @jax.custom_vjp
    def xdot(x, m):
        return _xdot_r_impl(x, m, n, nn)

    def fwd(x, m):
        return _xdot_r_impl(x, m, n, nn), m

    def bwd(m, ct):
        return _xdot_r_impl(ct, m, n, nt), jnp.zeros_like(m)

    xdot.defvjp(fwd, bwd)
    return xdot


_segsum = _make_xdot_r(2)


def _xdot_l_impl(m, x, n, dims):
    acc = None
    for p in _split_bf16(x, n):
        t = lax.dot_general(m, p, dims, preferred_element_type=f32)
        acc = t if acc is None else acc + t
    return acc


@jax.custom_vjp
def _xdot_l(m, x):
    return _xdot_l_impl(m, x, 3, (((1,), (0,)), ((), ())))


def _xdot_l_fwd(m, x):
    return _xdot_l(m, x), m


def _xdot_l_bwd(m, ct):
    return jnp.zeros_like(m), _xdot_l_impl(m, ct, 3, (((0,), (0,)), ((), ())))


_xdot_l.defvjp(_xdot_l_fwd, _xdot_l_bwd)


@jax.custom_vjp
def _bdot(x, w):
    return jnp.dot(x.astype(bf16), w.astype(bf16), preferred_element_type=f32)


def _bdot_fwd(x, w):
    return _bdot(x, w), (x, w)


def _bdot_bwd(res, ct):
    x, w = res
    c = ct.astype(bf16)
    dx = lax.dot_general(c, w.astype(bf16), (((1,), (1,)), ((), ())), preferred_element_type=f32)
    dw = lax.dot_general(x.astype(bf16), c, (((0,), (0,)), ((), ())), preferred_element_type=f32)
    return dx, dw


_bdot.defvjp(_bdot_fwd, _bdot_bwd)


def _silu(x):
    return x * jax.nn.sigmoid(x)


def _softplus(x):
    return jnp.maximum(x, 0.0) + jnp.log(1.0 + jnp.exp(-jnp.abs(x)))


def _rms(x, w, eps):
    return x * lax.rsqrt(jnp.mean(x * x, axis=-1, keepdims=True) + eps) * w


def _seg_matrix(width, seg):
    i = np.arange(width)
    return jnp.asarray((i[:, None] // seg) == (i[None, :] // seg), dtype=bf16)


def _chunk_tri(rows, chunk):
    i = np.arange(rows)
    return jnp.asarray(((i[:, None] // chunk) == (i[None, :] // chunk)) & (i[:, None] >= i[None, :]), dtype=bf16)


HALO = 8


def _full_spec(shape):
    nd = len(shape)
    return pl.BlockSpec(shape, lambda i: (0,) * nd)


def _entry_specs(entries, tt, block_of):
    specs, ops = [], []
    for arr, w, ci, shifts in entries:
        specs.append(pl.BlockSpec((tt, w), lambda i, ci=ci: (block_of(i), ci)))
        ops.append(arr)
        if shifts:
            specs.append(pl.BlockSpec((HALO, w), lambda i, ci=ci: (jnp.maximum(block_of(i) * (tt // HALO) - 1, 0), ci)))
            ops.append(arr)
    return specs, ops


def _load_entries(entries, refs, first):
    tiles, k = [], 0
    for _, w, _, shifts in entries:
        x = refs[k][...].astype(f32)
        k += 1
        if not shifts:
            tiles.append(x)
            continue
        halo = jnp.where(first, 0.0, refs[k][...].astype(f32))
        k += 1
        row = lax.broadcasted_iota(jnp.int32, (HALO, w), 0)
        for s in shifts:
            if s == 0:
                tiles.append(x)
                continue
            r = pltpu.roll(x, s, 0)
            head = jnp.where(row < s, pltpu.roll(halo, s, 0), r[0:HALO])
            tiles.append(jnp.concatenate([head, r[HALO:]], axis=0))
    return tiles


def _unshift_sum(grads, shifts, carry, tt):
    w = grads[0].shape[1]
    row = lax.broadcasted_iota(jnp.int32, (tt, w), 0)
    row8 = lax.broadcasted_iota(jnp.int32, (HALO, w), 0)
    dx, out = None, jnp.zeros((HALO, w), f32)
    for d, s in zip(grads, shifts):
        if s == 0:
            part = d
        else:
            part = jnp.where(row < tt - s, pltpu.roll(d, tt - s, 0), 0.0)
            out = out + jnp.where(row8 >= HALO - s, pltpu.roll(d[0:HALO], HALO - s, 0), 0.0)
        dx = part if dx is None else dx + part
    return jnp.concatenate([dx[:tt - HALO], dx[tt - HALO:] + carry], axis=0), out


def _stage_fwd(name, fn, tiles, params, outs, tt):
    rows = tiles[0][0].shape[0]
    npar = len(params)
    specs, ops = _entry_specs(tiles, tt, lambda i: i)
    nin = len(ops)

    def body(*refs):
        ts = _load_entries(tiles, refs[:nin], pl.program_id(0) == 0)
        ps = [r[...] for r in refs[nin:nin + npar]]
        res = fn(ps, ts)
        for r, v in zip(refs[nin + npar:], res):
            r[...] = v.astype(r.dtype)

    return pl.pallas_call(
        body, grid=(rows // tt,),
        in_specs=specs + [_full_spec(p.shape) for p in params],
        out_specs=[pl.BlockSpec((tt, w), lambda i: (i, 0)) for (w, _) in outs],
        out_shape=[jax.ShapeDtypeStruct((rows, w), dt) for (w, dt) in outs],
        compiler_params=pltpu.CompilerParams(dimension_semantics=("parallel",)),
        name=name,
    )(*ops, *params)


def _stage_bwd(name, fn, tiles, params, cots, tt, tile_grad, param_grad, joint=None, dtypes=None):
    rows = tiles[0][0].shape[0]
    nblk = rows // tt
    npar = len(params)
    block_of = lambda i: nblk - 1 - i
    specs, ops = _entry_specs(tiles, tt, block_of)
    nin = len(ops)
    flat_cots = [c for group in cots for c in group]
    groups = [len(g) for g in cots]
    ncot = len(flat_cots)
    counts = [len(e[3]) if e[3] else 1 for e in tiles]
    dt_entries = [e for e, g in zip(tiles, tile_grad) if g]
    dp_shapes = [p.shape for p, g in zip(params, param_grad) if g]
    ndt = len(dt_entries)
    carry_w = [e[1] for e in dt_entries if e[3]]
    flags = [g for g, n in zip(tile_grad, counts) for _ in range(n)]
    members, j_width, j_cidx, j_buf = joint if joint else ([], 0, 0, None)
    solo = [k for k in range(ndt) if k not in members]
    nsolo, njoint, nbuf = len(solo), int(bool(members)), int(j_buf is not None)
    j_block = sum(dt_entries[k][1] for k in members)
    dtypes = list(dtypes) if dtypes else [f32] * (nsolo + njoint)

    def body(*refs):
        i = pl.program_id(0)
        p_refs = refs[nin:nin + npar]
        c_refs = refs[nin + npar:nin + npar + ncot]
        base = nin + npar + ncot + nbuf
        dt_refs = refs[base:base + nsolo]
        joint_refs = refs[base + nsolo:base + nsolo + njoint]
        dp_refs = refs[base + nsolo + njoint:base + nsolo + njoint + len(dp_shapes)]
        carry_refs = refs[base + nsolo + njoint + len(dp_shapes):]
        ts = _load_entries(tiles, refs[:nin], block_of(i) == 0)
        ps = [r[...] for r in p_refs]

        def f(dp, dt):
            dp, dt = iter(dp), iter(dt)
            pp = [next(dp) if g else p for p, g in zip(ps, param_grad)]
            tl = [next(dt) if g else t for t, g in zip(ts, flags)]
            return fn(pp, tl)

        _, vjp = jax.vjp(f, [p for p, g in zip(ps, param_grad) if g], [t for t, g in zip(ts, flags) if g])
        cs, j = [], 0
        for n in groups:
            acc = c_refs[j][...].astype(f32)
            for q in range(1, n):
                acc = acc + c_refs[j + q][...].astype(f32)
            cs.append(acc)
            j += n
        gp, gt = vjp(cs)

        @pl.when(i == 0)
        def _():
            for r in dp_refs:
                r[...] = jnp.zeros_like(r)
            for r in carry_refs:
                r[...] = jnp.zeros_like(r)

        gt, k, kc, dxs = list(gt), 0, 0, []
        for e, n in zip(dt_entries, [n for n, g in zip(counts, tile_grad) if g]):
            if e[3]:
                dx, out = _unshift_sum(gt[k:k + n], e[3], carry_refs[kc][...], tt)
                carry_refs[kc][...] = out
                kc += 1
            else:
                dx = gt[k]
            dxs.append(dx)
            k += n
        for r, k in zip(dt_refs, solo):
            r[...] = dxs[k].astype(r.dtype)
        off = 0
        for k in members:
            w = dt_entries[k][1]
            joint_refs[0][:, off:off + w] = dxs[k].astype(joint_refs[0].dtype)
            off += w
        for r, v in zip(dp_refs, gp):
            r[...] += v

    res = pl.pallas_call(
        body, grid=(nblk,),
        in_specs=specs + [_full_spec(p.shape) for p in params]
        + [pl.BlockSpec((tt, w), lambda i, ci=ci: (block_of(i), ci)) for (_, w, ci, *_) in flat_cots]
        + [pl.BlockSpec(memory_space=pl.ANY)] * nbuf,
        out_specs=[pl.BlockSpec((tt, dt_entries[k][1]), lambda i: (block_of(i), 0)) for k in solo]
        + [pl.BlockSpec((tt, j_block), lambda i: (block_of(i), j_cidx))] * njoint
        + [_full_spec(s) for s in dp_shapes],
        out_shape=[jax.ShapeDtypeStruct((rows, dt_entries[k][1]), dt) for k, dt in zip(solo, dtypes)]
        + [jax.ShapeDtypeStruct((rows, j_width), dtypes[-1])] * njoint
        + [jax.ShapeDtypeStruct(s, f32) for s in dp_shapes],
        scratch_shapes=[pltpu.VMEM((HALO, w), f32) for w in carry_w],
        input_output_aliases={nin + npar + ncot: nsolo} if nbuf else {},
        compiler_params=pltpu.CompilerParams(dimension_semantics=("arbitrary",)),
        name=name,
    )(*ops, *params, *[c[0] for c in flat_cots], *([j_buf] if nbuf else []))
    res = list(res)
    return res[:nsolo], res[nsolo + njoint:], (res[nsolo] if njoint else None)


def _whole(a):
    return (a, a.shape[1], 0, None)


def _matmul(name, a, b, mode, out_dtype=f32, tm=1024, tn=1024, tk=1024, shards=1, job=None):
    S = shards
    if mode == "nn":
        M, K = a.shape
        w = b.shape[-1]
    elif mode == "nt":
        M = a.shape[0]
        if S > 1:
            _, N, w = b.shape
            K = S * w
        else:
            N, K = b.shape
            w = K
    else:
        K, M = a.shape
        w = b.shape[1] // S
    if mode != "nt":
        N = S * w
    tm = _pick(M, tm, LANES)
    if mode == "nt":
        tn = _pick(N, tn, LANES)
        tk = _pick(w, tk, LANES)
    else:
        tn = _pick(w, tn, LANES)
        tk = _pick(K, tk, LANES if mode == "nn" else 16)
    nk = K // tk
    nb = w // (tk if mode == "nt" else tn)
    if mode == "nn":
        a_spec = pl.BlockSpec((tm, tk), lambda i, j, k: (i, k))
        if S > 1:
            b_spec = pl.BlockSpec((1, tk, tn), lambda i, j, k: (j // nb, k, j % nb))
        else:
            b_spec = pl.BlockSpec((tk, tn), lambda i, j, k: (k, j))
        dims = (((1,), (0,)), ((), ()))
    elif mode == "nt":
        a_spec = pl.BlockSpec((tm, tk), lambda i, j, k: (i, k))
        if S > 1:
            b_spec = pl.BlockSpec((1, tn, tk), lambda i, j, k: (k // nb, j, k % nb))
        else:
            b_spec = pl.BlockSpec((tn, tk), lambda i, j, k: (j, k))
        dims = (((1,), (1,)), ((), ()))
    else:
        a_spec = pl.BlockSpec((tk, tm), lambda i, j, k: (k, i))
        b_spec = pl.BlockSpec((tk, tn), lambda i, j, k: (k, j))
        dims = (((0,), (0,)), ((), ()))
    if mode == "tn" and S > 1:
        o_spec = pl.BlockSpec((1, tm, tn), lambda i, j, k: (j // nb, i, j % nb))
        o_shape = (S, M, w)
    else:
        o_spec = pl.BlockSpec((tm, tn), lambda i, j, k: (i, j))
        o_shape = (M, N)
    b_lead = S > 1 and mode != "tn"
    o_lead = S > 1 and mode == "tn"

    j_ins, j_outs, j_sems = _job_parts(job)
    nji, njo = len(j_ins), len(j_outs)
    grid = (M // tm, N // tn, nk)

    def run_job(refs):
        step = (pl.program_id(0) * grid[1] + pl.program_id(1)) * grid[2] + pl.program_id(2)
        _job_steps(job, refs[2:2 + nji], refs[3 + nji:3 + nji + njo], refs[len(refs) - len(j_sems):], step,
                   grid[0] * grid[1] * grid[2] - 1)

    def body(*refs):
        a_ref, b_ref, o_ref, acc_ref = refs[0], refs[1], refs[2 + nji], refs[3 + nji + njo]
        run_job(refs)
        k = pl.program_id(2)

        @pl.when(k == 0)
        def _():
            acc_ref[...] = jnp.zeros_like(acc_ref)

        bv = b_ref[0] if b_lead else b_ref[...]
        acc_ref[...] += lax.dot_general(a_ref[...].astype(bf16), bv.astype(bf16), dims, preferred_element_type=f32)

        @pl.when(k == nk - 1)
        def _():
            if o_lead:
                o_ref[0] = acc_ref[...].astype(o_ref.dtype)
            else:
                o_ref[...] = acc_ref[...].astype(o_ref.dtype)

    def body_one_step(*refs):
        a_ref, b_ref, o_ref = refs[0], refs[1], refs[2 + nji]
        run_job(refs)
        bv = b_ref[0] if b_lead else b_ref[...]
        res = lax.dot_general(a_ref[...].astype(bf16), bv.astype(bf16), dims, preferred_element_type=f32)
        if o_lead:
            o_ref[0] = res.astype(o_ref.dtype)
        else:
            o_ref[...] = res.astype(o_ref.dtype)

    res = pl.pallas_call(
        body if nk > 1 else body_one_step, grid=grid,
        in_specs=[a_spec, b_spec] + [_ANY] * nji,
        out_specs=[o_spec] + [_ANY] * njo,
        out_shape=[jax.ShapeDtypeStruct(o_shape, out_dtype)] + j_outs,
        scratch_shapes=([pltpu.VMEM((tm, tn), f32)] if nk > 1 else []) + j_sems,
        compiler_params=pltpu.CompilerParams(
            dimension_semantics=("arbitrary",) * 3 if job else ("parallel", "parallel", "arbitrary")),
        name=name,
    )(a, b, *j_ins)
    return res if job else res[0]


def _make_bmm(precision):
    if precision is None:
        cast, kw = (lambda v: v.astype(bf16)), {}
    else:
        cast, kw = (lambda v: v), {"precision": precision}

    def nn(a, b):
        return jnp.einsum("hij,hjk->hik", cast(a), cast(b), preferred_element_type=f32, **kw)

    def nt(a, b):
        return jnp.einsum("hik,hjk->hij", cast(a), cast(b), preferred_element_type=f32, **kw)

    def tn(a, b):
        return jnp.einsum("hki,hkj->hij", cast(a), cast(b), preferred_element_type=f32, **kw)

    if precision is not None:
        return nn, nt, tn
    nn_v, nt_v, tn_v = jax.custom_vjp(nn), jax.custom_vjp(nt), jax.custom_vjp(tn)
    keep = lambda f: (lambda a, b: (f(a, b), (a, b)))
    nn_v.defvjp(keep(nn), lambda r, ct: (nt(ct, r[1]), tn(r[0], ct)))
    nt_v.defvjp(keep(nt), lambda r, ct: (nn(ct, r[1]), tn(ct, r[0])))
    tn_v.defvjp(keep(tn), lambda r, ct: (nt(r[1], ct), nn(r[0], ct)))
    return nn_v, nt_v, tn_v


_bmm, _bmm_nt, _bmm_tn = _make_bmm(None)
_bmm_exact = _make_bmm(lax.Precision.HIGH)[0]


def _masks(n):
    r = lax.broadcasted_iota(jnp.int32, (n, n), 0)
    c = lax.broadcasted_iota(jnp.int32, (n, n), 1)
    return (r >= c)[None], (r > c)[None], (r == c)[None]


_INV_BLOCK = 8


def _nilpotent_inverse(m, eye):
    p = eye + m
    for _ in range(2):
        m = _bmm(m, m)
        p = p + _bmm(p, m)
    return p


def _neumann_inverse_impl(m):
    n = m.shape[1]
    assert n == _INV_BLOCK * _INV_BLOCK
    r = lax.broadcasted_iota(jnp.int32, (n, n), 0)
    c = lax.broadcasted_iota(jnp.int32, (n, n), 1)
    eye = (r == c).astype(f32)[None]
    inside = jnp.where((r // _INV_BLOCK == c // _INV_BLOCK)[None], m, 0.0)
    d_inv = _nilpotent_inverse(inside, eye)
    return _bmm(_nilpotent_inverse(_bmm(d_inv, m - inside), eye), d_inv)


@jax.custom_vjp
def _neumann_inverse(m):
    return _neumann_inverse_impl(m)


def _neumann_inverse_fwd(m):
    p = _neumann_inverse_impl(m)
    return p, p


def _neumann_inverse_bwd(p, ct):
    return (_bmm_tn(p, _bmm_nt(ct, p)),)


_neumann_inverse.defvjp(_neumann_inverse_fwd, _neumann_inverse_bwd)


@jax.custom_vjp
def _given_inverse(m, p):
    return p


_given_inverse.defvjp(lambda m, p: (p, p), lambda p, ct: (_neumann_inverse_bwd(p, ct)[0], jnp.zeros_like(p)))


def _gdn_chunk(s, q, k, v, beta, gc, gr, gl, p=None):
    n = q.shape[1]
    causal, strict, _ = _masks(n)
    decay = jnp.where(causal, jnp.exp(jnp.where(causal, gc - gr, 0.0)), 0.0)
    kb = k * beta
    vb = v * beta
    lower = jnp.where(strict, _bmm_nt(kb, k) * decay, 0.0)
    t_mat = _neumann_inverse(-lower) if p is None else _given_inverse(-lower, p)
    egc = jnp.exp(gc)
    u = _bmm(t_mat, vb)
    w = _bmm(t_mat, kb * egc)
    attn = jnp.where(causal, _bmm_nt(q, k) * decay, 0.0)
    v_new = u - _bmm(w, s)
    o = _bmm(q * egc, s) + _bmm(attn, v_new)
    k_dec = k * jnp.exp(gl - gc)
    s_new = s * jnp.exp(gl) + _bmm_tn(k_dec, v_new)
    return s_new, o, t_mat


def _rwkv_chunk(s, r, lw, k, v, a, b, p=None):
    n = r.shape[1]
    causal, strict, _ = _masks(n)
    tri = jnp.broadcast_to(causal.astype(f32), (r.shape[0], n, n))
    lc = _bmm_exact(tri, lw)
    ein = jnp.exp(lc)
    eout = jnp.exp(-lc)
    a_t = a * jnp.exp(lc - lw)
    b_t = b * eout
    k_t = k * eout
    r_t = r * ein
    a_ab = jnp.where(strict, _bmm_nt(a_t, b_t), 0.0)
    a_ak = jnp.where(strict, _bmm_nt(a_t, k_t), 0.0)
    inv = _neumann_inverse(a_ab) if p is None else _given_inverse(a_ab, p)
    u = _bmm(inv, _bmm_nt(a_t, s) + _bmm(a_ak, v))
    y = (_bmm_nt(r_t, s) + _bmm(jnp.where(causal, _bmm_nt(r_t, b_t), 0.0), u)
         + _bmm(jnp.where(causal, _bmm_nt(r_t, k_t), 0.0), v))
    e_last = jnp.exp(jnp.sum(lw, axis=1, keepdims=True))
    s_new = s * e_last + _bmm_tn(u, b_t * e_last) + _bmm_tn(v, k_t * e_last)
    return s_new, y, inv


def _heads_in(ref, rows):
    return jnp.stack([ref[rows, h * HEAD_DIM:(h + 1) * HEAD_DIM] for h in range(HEADS)], axis=0)


def _heads_out(ref, rows, val):
    for h in range(HEADS):
        ref[rows, h * HEAD_DIM:(h + 1) * HEAD_DIM] = val[h]


def _gdn_scalars(bt, gt):
    n = bt.shape[0]
    gtt = gt.T
    hs = range(HEADS)
    return [jnp.stack([bt[:, h:h + 1] for h in hs], axis=0),
            jnp.stack([gt[:, HEADS + h:HEADS + h + 1] for h in hs], axis=0),
            jnp.stack([gtt[HEADS + h:HEADS + h + 1, :] for h in hs], axis=0),
            jnp.stack([gt[n - 1:n, HEADS + h:HEADS + h + 1] for h in hs], axis=0)]


def _gdn_scalars_back(dbeta, dgc, dgr, dgl):
    n = dbeta.shape[1]
    lane = lax.broadcasted_iota(jnp.int32, (n, LANES), 1)
    row = lax.broadcasted_iota(jnp.int32, (n, LANES), 0)
    sub = lax.broadcasted_iota(jnp.int32, (LANES, n), 0)
    db = jnp.zeros((n, LANES), f32)
    dg = jnp.zeros((n, LANES), f32)
    dgt = jnp.zeros((LANES, n), f32)
    for h in range(HEADS):
        db = jnp.where(lane == h, dbeta[h], db)
        dg = jnp.where(lane == HEADS + h, dgc[h] + jnp.where(row == n - 1, dgl[h], 0.0), dg)
        dgt = jnp.where(sub == HEADS + h, dgr[h], dgt)
    return [db, dg + dgt.T]


SCAN_GROUP = 2


def _scan_steps(t):
    g = SCAN_GROUP if (t // CHUNK) % SCAN_GROUP == 0 else 1
    return g, t // (CHUNK * g)


def _scan_spec(width, g, n, reverse):
    if reverse:
        return pl.BlockSpec((g * CHUNK, width), lambda i: (n - 1 - i, 0))
    return pl.BlockSpec((g * CHUNK, width), lambda i: (i, 0))


def _hist_spec(g, n, reverse):
    blk = (g, HEADS, HEAD_DIM, HEAD_DIM)
    if reverse:
        return pl.BlockSpec(blk, lambda i: (n - 1 - i, 0, 0, 0))
    return pl.BlockSpec(blk, lambda i: (i, 0, 0, 0))


def _job_list(job):
    return [] if job is None else (list(job) if isinstance(job, (list, tuple)) else [job])


def _job_parts(job):
    jobs = _job_list(job)
    sems = [pltpu.SemaphoreType.DMA((j["nsem"],)) for j in jobs for _ in range(2)]
    return [a for j in jobs for a in j["ins"]], [o for j in jobs for o in j["out_shapes"]], sems


_JOB_PHASES = ("start", "forward", "finish")


def _job_phase(job, phase, in_refs, out_refs, sems):
    ki = ko = 0
    for n, j in enumerate(_job_list(job)):
        ni, no = len(j["ins"]), len(j["out_shapes"])
        if phase in j:
            j[phase](in_refs[ki:ki + ni], out_refs[ko:ko + no], sems[2 * n], sems[2 * n + 1])
        ki, ko = ki + ni, ko + no


def _job_steps(job, in_refs, out_refs, sems, step, last):
    if job is None:
        return
    for phase, at in zip(_JOB_PHASES, (0, max(last - 2, 0), last)):
        @pl.when(step == at)
        def _(phase=phase):
            _job_phase(job, phase, in_refs, out_refs, sems)


def _scan_fwd(name, fn, rows_in, scal_in=(), job=None):
    t = rows_in[0].shape[0]
    grp, n = _scan_steps(t)
    nr, ns = len(rows_in), len(scal_in)
    j_ins, j_outs, j_sems = _job_parts(job)
    nji, njo = len(j_ins), len(j_outs)

    def body(*refs):
        o_ref, sh_ref, ph_ref = refs[nr + ns + nji:nr + ns + nji + 3]
        s_scr = refs[nr + ns + nji + 3 + njo]
        _job_steps(job, refs[nr + ns:nr + ns + nji], refs[nr + ns + nji + 3:nr + ns + nji + 3 + njo],
                   refs[nr + ns + nji + 3 + njo + 1:], pl.program_id(0), n - 1)

        @pl.when(pl.program_id(0) == 0)
        def _():
            s_scr[...] = jnp.zeros_like(s_scr)

        s = s_scr[...]
        for sub in range(grp):
            rows = slice(sub * CHUNK, (sub + 1) * CHUNK)
            sh_ref[sub] = s
            ins = [_heads_in(r, rows) for r in refs[:nr]]
            if ns:
                ins += _gdn_scalars(*[r[rows, :] for r in refs[nr:nr + ns]])
            s, o, p = fn(s, *ins)
            _heads_out(o_ref, rows, o)
            ph_ref[sub] = p
        s_scr[...] = s

    return pl.pallas_call(
        body, grid=(n,),
        in_specs=[_scan_spec(a.shape[1], grp, n, False) for a in (*rows_in, *scal_in)] + [_ANY] * nji,
        out_specs=[_scan_spec(WIDTH, grp, n, False), _hist_spec(grp, n, False), _hist_spec(grp, n, False)] + [_ANY] * njo,
        out_shape=[jax.ShapeDtypeStruct((t, WIDTH), f32)]
        + [jax.ShapeDtypeStruct((t // CHUNK, HEADS, HEAD_DIM, HEAD_DIM), f32)] * 2 + j_outs,
        scratch_shapes=[pltpu.VMEM((HEADS, HEAD_DIM, HEAD_DIM), f32)] + j_sems,
        compiler_params=pltpu.CompilerParams(dimension_semantics=("arbitrary",)),
        name=name,
    )(*rows_in, *scal_in, *j_ins)


def _scan_bwd(name, fn, rows_in, scal_in, s_hist, p_hist, d_out, job=None):
    t = rows_in[0].shape[0]
    grp, n = _scan_steps(t)
    nr, ns = len(rows_in), len(scal_in)
    j_ins, j_outs, j_sems = _job_parts(job)
    nji, njo = len(j_ins), len(j_outs)

    def body(*refs):
        sh_ref, ph_ref, do_ref = refs[nr + ns:nr + ns + 3]
        base = nr + ns + 3 + nji
        g_refs = refs[base:base + nr + ns]
        ds_scr = refs[base + nr + ns + njo]
        _job_steps(job, refs[nr + ns + 3:base], refs[base + nr + ns:base + nr + ns + njo],
                   refs[base + nr + ns + njo + 1:], pl.program_id(0), n - 1)

        @pl.when(pl.program_id(0) == 0)
        def _():
            ds_scr[...] = jnp.zeros_like(ds_scr)

        ds = ds_scr[...]
        for sub in reversed(range(grp)):
            rows = slice(sub * CHUNK, (sub + 1) * CHUNK)
            ins = [_heads_in(r, rows) for r in refs[:nr]]
            if ns:
                ins += _gdn_scalars(*[r[rows, :] for r in refs[nr:nr + ns]])
            p = ph_ref[sub]
            _, vjp = jax.vjp(lambda s, *a, p=p: fn(s, *a, p=p)[:2], sh_ref[sub], *ins)
            g = vjp((ds, _heads_in(do_ref, rows)))
            ds = g[0]
            for r, v in zip(g_refs[:nr], g[1:1 + nr]):
                _heads_out(r, rows, v)
            if ns:
                for r, v in zip(g_refs[nr:], _gdn_scalars_back(*g[1 + nr:])):
                    r[rows, :] = v
        ds_scr[...] = ds

    arrs = (*rows_in, *scal_in)
    return pl.pallas_call(
        body, grid=(n,),
        in_specs=[_scan_spec(a.shape[1], grp, n, True) for a in arrs]
        + [_hist_spec(grp, n, True), _hist_spec(grp, n, True), _scan_spec(WIDTH, grp, n, True)] + [_ANY] * nji,
        out_specs=[_scan_spec(a.shape[1], grp, n, True) for a in arrs] + [_ANY] * njo,
        out_shape=[jax.ShapeDtypeStruct(a.shape, f32) for a in arrs] + j_outs,
        scratch_shapes=[pltpu.VMEM((HEADS, HEAD_DIM, HEAD_DIM), f32)] + j_sems,
        compiler_params=pltpu.CompilerParams(dimension_semantics=("arbitrary",)),
        name=name,
    )(*arrs, s_hist, p_hist, d_out, *j_ins)


def _fn_norm_mod(ps, ts):
    nw, shift, scale = ps
    (x,) = ts
    return [_rms(x, nw, NORM_EPS) * (1.0 + scale) + shift]


def _fn_norm_mod_and_x(ps, ts):
    return _fn_norm_mod(ps, ts) + [ts[0]]


def _fn_resid_norm_mod(ps, ts):
    gate, nw, shift, scale = ps
    x, mo = ts
    x1 = x + gate * mo
    return [x1, _rms(x1, nw, NORM_EPS) * (1.0 + scale) + shift]


def _fn_gdn_pre(ps, ts):
    cw = ps[:12]
    alog, dtb, seg, tri = ps[12:]
    ba = ts[12]
    outs = []
    for part in range(3):
        x = ts[4 * part:4 * part + 4]
        w = cw[4 * part:4 * part + 4]
        conv = w[3] * x[0] + w[2] * x[1] + w[1] * x[2] + w[0] * x[3]
        u = _silu(conv)
        if part < 2:
            u = u * lax.rsqrt(_segsum(u * u, seg) + 1e-6)
            if part == 0:
                u = u * (HEAD_DIM ** -0.5)
        outs.append(u)
    beta = jax.nn.sigmoid(ba)
    g = -jnp.exp(alog) * _softplus(ba + dtb)
    gc = _xdot_l(tri, g)
    return outs + [beta, gc]


def _fn_gdn_post(ps, ts):
    ow, seg = ps
    o, z = ts
    ms = _segsum(o * o, seg) * (1.0 / HEAD_DIM)
    return [o * lax.rsqrt(ms + NORM_EPS) * ow * _silu(z)]


def _fn_rwkv_pre(ps, ts):
    mu_r, mu_k, mu_v, mu_l, mu_g, w0, w2p, a0, a2p, g2p, k_k, k_a, seg = ps
    r0, r1, k0, k1, v0, v1, l0, l1, g0, g1 = ts
    xr = r0 + (r1 - r0) * mu_r
    xk = k0 + (k1 - k0) * mu_k
    xv = v0 + (v1 - v0) * mu_v
    xl = l0 + (l1 - l0) * mu_l
    xg = g0 + (g1 - g0) * mu_g
    w = -_softplus(-(w0 + _bdot(jnp.tanh(xl), w2p))) - 0.5
    lw = -jnp.exp(w)
    a = jax.nn.sigmoid(a0 + _bdot(xl, a2p))
    g = _bdot(jax.nn.sigmoid(xg), g2p)
    kk = xk * k_k
    kk = kk * lax.rsqrt(_segsum(kk * kk, seg) + 1e-6)
    k2 = xk * (1.0 + (a - 1.0) * k_a)
    return [xr, lw, k2, xv, -kk, kk * a, g]


def _fn_rwkv_post(ps, ts):
    lw_, lb_, rk, seg = ps
    y, r, k2, v, g = ts
    inv = 1.0 / HEAD_DIM
    yc = y - _segsum(y, seg) * inv
    var = _segsum(yc * yc, seg) * inv
    yn = yc * lax.rsqrt(var + LNX_EPS) * lw_ + lb_
    bonus = _segsum(r * k2 * rk, seg) * v
    return [(yn + bonus) * g]


def _fn_merge(ps, ts):
    gla, glb, ya, yb = ts
    return [jax.nn.sigmoid(gla) * ya + jax.nn.sigmoid(glb) * yb]


def _fn_convglu(ps, ts):
    c0, c1, c2 = ps
    g0, g1, g2, up = ts
    return [_silu(c2 * g0 + c1 * g1 + c0 * g2) * up]


def _final_stage(x1, fo, tgt, gate2, nfw, tt):
    rows, d = x1.shape

    def loss_fn(gate, nw, xa, fa, tg):
        y = _rms(xa + gate * fa, nw, NORM_EPS)
        err = (y - tg) ** 2
        return 0.5 * jnp.sum(jnp.mean(err, axis=-1, keepdims=True), axis=0, keepdims=True)

    def body(x_ref, f_ref, t_ref, g_ref, w_ref, dx_ref, df_ref, dg_ref, dw_ref, l_ref):
        i = pl.program_id(0)
        args = (g_ref[...], w_ref[...], x_ref[...], f_ref[...])
        tg = t_ref[...]
        lv, vjp = jax.vjp(lambda g, w, xa, fa: loss_fn(g, w, xa, fa, tg), *args)
        dg, dw, dx, df = vjp(jnp.ones((1, 1), f32))
        dx_ref[...] = dx
        df_ref[...] = df.astype(df_ref.dtype)

        @pl.when(i == 0)
        def _():
            dg_ref[...] = jnp.zeros_like(dg_ref)
            dw_ref[...] = jnp.zeros_like(dw_ref)
            l_ref[...] = jnp.zeros_like(l_ref)

        dg_ref[...] += dg
        dw_ref[...] += dw
        l_ref[...] += jnp.broadcast_to(lv, l_ref.shape)

    row = pl.BlockSpec((tt, d), lambda i: (i, 0))
    vec = pl.BlockSpec((1, d), lambda i: (0, 0))
    return pl.pallas_call(
        body, grid=(rows // tt,),
        in_specs=[row, row, row, vec, vec],
        out_specs=[row, row, vec, vec, pl.BlockSpec((1, LANES), lambda i: (0, 0))],
        out_shape=[jax.ShapeDtypeStruct((rows, d), f32), jax.ShapeDtypeStruct((rows, d), bf16)]
        + [jax.ShapeDtypeStruct((1, d), f32)] * 2
        + [jax.ShapeDtypeStruct((1, LANES), f32)],
        compiler_params=pltpu.CompilerParams(dimension_semantics=("arbitrary",)),
        name="loss_head",
    )(x1, fo, tgt, gate2, nfw)


def _ada_fwd(c_all, w_shard, b_cols):
    def body(c_ref, w_ref, b_ref, cond_ref, mod_ref):
        cond = _silu(c_ref[...])
        cond_ref[...] = cond
        mod_ref[...] = jnp.dot(cond.astype(bf16), w_ref[...].astype(bf16), preferred_element_type=f32) + b_ref[...]

    n = w_shard.shape[1]
    return pl.pallas_call(
        body, out_shape=[jax.ShapeDtypeStruct(c_all.shape, f32), jax.ShapeDtypeStruct((c_all.shape[0], n), f32)],
        name="ada_fwd",
    )(c_all, w_shard, b_cols)


def _adamw(name, w, g, m, v):
    rows, width = w.shape
    tt = _pick(rows, 256, 8)
    c1 = 1.0 - ADAM_B1 ** ADAM_STEP
    c2 = 1.0 - ADAM_B2 ** ADAM_STEP

    def body(w_ref, g_ref, m_ref, v_ref, d_ref, mo_ref, vo_ref):
        gg = g_ref[...]
        mn = ADAM_B1 * m_ref[...] + (1.0 - ADAM_B1) * gg
        vn = ADAM_B2 * v_ref[...] + (1.0 - ADAM_B2) * (gg * gg)
        m_hat = mn / c1
        v_hat = vn / c2
        d_ref[...] = -ADAM_LR * (m_hat / (jnp.sqrt(v_hat) + ADAM_EPS) + ADAM_WD * w_ref[...])
        mo_ref[...] = mn
        vo_ref[...] = vn

    spec = pl.BlockSpec((tt, width), lambda i: (i, 0))
    return pl.pallas_call(
        body, grid=(rows // tt,), in_specs=[spec] * 4, out_specs=[spec] * 3,
        out_shape=[jax.ShapeDtypeStruct((rows, width), f32)] * 3,
        compiler_params=pltpu.CompilerParams(dimension_semantics=("parallel",)),
        name=name,
    )(w, g, m, v)


def _place():
    return lax.axis_index("x"), lax.axis_index("y"), lax.axis_index("c")


def _ag8(name, blk):
    m, w = blk.shape

    def body(x_ref, out_ref, send_sems, recv_sems, local_sem):
        x, y, c = _place()
        me, sibling = (x, y, c), (x, y, 1 - c)
        chips = _other_chips(x, y)

        def slot(px, py, pc):
            return out_ref.at[4 * px + 2 * py + pc]

        def copy(k, block, to, src=None):
            return pltpu.make_async_remote_copy(src_ref=slot(*block) if src is None else src, dst_ref=slot(*block),
                                                send_sem=send_sems.at[k], recv_sem=recv_sems.at[k], device_id=to,
                                                device_id_type=MESH_ID)

        mine = pltpu.make_async_copy(x_ref, slot(*me), local_sem)
        mine.start()
        first = [copy(0, me, sibling, src=x_ref)] + [copy(1 + j, me, (*chip, c), src=x_ref) for j, chip in enumerate(chips)]
        for cp in first:
            cp.start()
        passed = [copy(4 + j, (*chip, c), sibling) for j, chip in enumerate(chips)]
        for j, chip in enumerate(chips):
            copy(1 + j, (*chip, c), me).wait_recv()
            passed[j].start()
        copy(0, sibling, me).wait_recv()
        for j, chip in enumerate(chips):
            copy(4 + j, (*chip, 1 - c), me).wait_recv()
        for cp in first + passed:
            cp.wait_send()
        mine.wait()

    return pl.pallas_call(
        body, out_shape=jax.ShapeDtypeStruct((8, m, w), blk.dtype),
        in_specs=[pl.BlockSpec(memory_space=pltpu.VMEM)], out_specs=pl.BlockSpec(memory_space=pltpu.VMEM),
        scratch_shapes=[pltpu.SemaphoreType.DMA((7,)), pltpu.SemaphoreType.DMA((7,)), pltpu.SemaphoreType.DMA],
        name=name,
    )(blk)


def _ag8_job(blk):
    def plan(x_refs, out_refs, send_sems, recv_sems):
        x_ref, out_ref = x_refs[0], out_refs[0]
        x, y, c = _place()
        me, sibling = (x, y, c), (x, y, 1 - c)
        chips = _other_chips(x, y)
        slot = lambda px, py, pc: out_ref.at[4 * px + 2 * py + pc]
        cp = lambda k, block, to, src=None: _rcopy(slot(*block) if src is None else src, slot(*block), send_sems, recv_sems, k, to)
        first = [cp(0, me, sibling, x_ref)] + [cp(1 + j, me, (*chip, c), x_ref) for j, chip in enumerate(chips)]
        landed = [cp(1 + j, (*chip, c), me) for j, chip in enumerate(chips)]
        passed = [cp(4 + j, (*chip, c), sibling) for j, chip in enumerate(chips)]
        handed = [cp(0, sibling, me)] + [cp(4 + j, (*chip, 1 - c), me) for j, chip in enumerate(chips)]
        mine = pltpu.make_async_copy(x_ref, slot(*me), send_sems.at[7])
        return first, landed, passed, handed, mine

    def start(*refs):
        first, _, _, _, mine = plan(*refs)
        mine.start()
        for cp in first:
            cp.start()

    def forward(*refs):
        _, landed, passed, _, _ = plan(*refs)
        for arrived, fw in zip(landed, passed):
            arrived.wait_recv()
            fw.start()

    def finish(*refs):
        first, _, passed, handed, mine = plan(*refs)
        for cp in handed:
            cp.wait_recv()
        for cp in first + passed:
            cp.wait_send()
        mine.wait()

    return dict(ins=[blk], out_shapes=[jax.ShapeDtypeStruct((8,) + blk.shape, blk.dtype)], nsem=8, start=start,
                forward=forward, finish=finish)


def _other_chips(x, y):
    return [(1 - x, y), (x, 1 - y), (1 - x, 1 - y)]


_ANY = pl.BlockSpec(memory_space=pl.ANY)


def _rcopy(src, dst, send_sems, recv_sems, k, dev):
    return pltpu.make_async_remote_copy(src_ref=src, dst_ref=dst, send_sem=send_sems.at[k], recv_sem=recv_sems.at[k],
                                        device_id=dev, device_id_type=MESH_ID)


def _run_job(name, job):
    j_ins, j_outs, j_sems = _job_parts(job)
    n = len(j_ins)

    def body(*refs):
        for phase in _JOB_PHASES:
            _job_phase(job, phase, refs[:n], refs[n:n + len(j_outs)], refs[n + len(j_outs):])

    return pl.pallas_call(body, out_shape=j_outs, in_specs=[_ANY] * n, out_specs=[_ANY] * len(j_outs),
                          scratch_shapes=j_sems, name=name)(*j_ins)


def _ag4_relay_job(ws):
    n = len(ws)

    def plan(w_refs, out_refs, send_sems, recv_sems):
        x, y, c = _place()
        chip, xn, yn, dg = 2 * x + y, 2 * (1 - x) + y, 2 * x + (1 - y), 2 * (1 - x) + (1 - y)
        sibling, to_x, to_y = (x, y, 1 - c), (1 - x, y, c), (x, 1 - y, c)
        rc = lambda src, dst, k, dev: _rcopy(src, dst, send_sems, recv_sems, k, dev)
        p = dict(direct=[], own=[], landed=[], relay=[], relayed=[], hand=[], handed=[])
        for t, w in enumerate(ws):
            rh = w.shape[0] // 2
            rq = rh // 2
            mine, other = pl.ds(c * rh, rh), pl.ds((1 - c) * rh, rh)
            qa, qb = pl.ds(c * rh, rq), pl.ds(c * rh + rq, rq)
            o, k0 = out_refs[t], 10 * t
            sends = [(qa, k0, to_x, xn), (qb, k0 + 2, to_y, yn), (qb, k0 + 1, to_x, xn), (qa, k0 + 3, to_y, yn)]
            p["direct"] += [rc(w_refs[t].at[rows], o.at[chip, rows], k, dev) for rows, k, dev, _ in sends]
            p["landed"] += [rc(o.at[j, rows], o.at[j, rows], k, dev) for rows, k, dev, j in sends]
            p["own"].append(rc(w_refs[t], o.at[chip], k0 + 9, sibling))
            p["relay"] += [rc(o.at[xn, qa], o.at[xn, qa], k0 + 4, to_y), rc(o.at[yn, qb], o.at[yn, qb], k0 + 5, to_x)]
            p["relayed"] += [rc(o.at[dg, qa], o.at[dg, qa], k0 + 4, to_y), rc(o.at[dg, qb], o.at[dg, qb], k0 + 5, to_x)]
            p["hand"] += [rc(o.at[j, mine], o.at[j, mine], k0 + 6 + k, sibling) for k, j in enumerate((xn, yn, dg))]
            p["handed"] += [rc(o.at[j, other], o.at[j, other], k0 + 6 + k, sibling) for k, j in enumerate((xn, yn, dg))]
        return p

    def start(*refs):
        p = plan(*refs)
        for cp in p["direct"] + p["own"]:
            cp.start()

    def forward(*refs):
        p = plan(*refs)
        for t in range(n):
            for k in range(2):
                p["landed"][4 * t + k].wait_recv()
                p["relay"][2 * t + k].start()
        for t in range(n):
            for k in range(2):
                p["landed"][4 * t + 2 + k].wait_recv()
                p["hand"][3 * t + k].start()
        for t in range(n):
            for k in range(2):
                p["relayed"][2 * t + k].wait_recv()
            p["hand"][3 * t + 2].start()

    def finish(*refs):
        p = plan(*refs)
        for cp in p["handed"] + p["own"]:
            cp.wait_recv()
        for cp in p["direct"] + p["own"] + p["relay"] + p["hand"]:
            cp.wait_send()

    return dict(ins=ws, out_shapes=[jax.ShapeDtypeStruct((4,) + w.shape, w.dtype) for w in ws], nsem=10 * n,
                start=start, forward=forward, finish=finish)


def _ag4_job(ws):
    n = len(ws)

    def plan(w_refs, out_refs, send_sems, recv_sems):
        x, y, c = _place()
        chip = 2 * x + y
        sibling = (x, y, 1 - c)
        chips = _other_chips(x, y)
        mine = [pl.ds(c * (w.shape[0] // 2), w.shape[0] // 2) for w in ws]
        other = [pl.ds((1 - c) * (w.shape[0] // 2), w.shape[0] // 2) for w in ws]
        rc = lambda src, dst, k, dev: _rcopy(src, dst, send_sems, recv_sems, k, dev)
        first = [rc(w_refs[t].at[mine[t]], out_refs[t].at[chip, mine[t]], 7 * t + k, (px, py, c))
                 for t in range(n) for k, (px, py) in enumerate(chips)]
        own = [rc(w_refs[t], out_refs[t].at[chip], 7 * t + 6, sibling) for t in range(n)]
        landed = [rc(out_refs[t].at[2 * px + py, mine[t]], out_refs[t].at[2 * px + py, mine[t]], 7 * t + k, (px, py, c))
                  for t in range(n) for k, (px, py) in enumerate(chips)]
        forward = [rc(out_refs[t].at[2 * px + py, mine[t]], out_refs[t].at[2 * px + py, mine[t]], 7 * t + 3 + k, sibling)
                   for t in range(n) for k, (px, py) in enumerate(chips)]
        handed = [rc(out_refs[t].at[2 * px + py, other[t]], out_refs[t].at[2 * px + py, other[t]], 7 * t + 3 + k, sibling)
                  for t in range(n) for k, (px, py) in enumerate(chips)]
        return first, own, landed, forward, handed

    def start(*refs):
        first, own, _, _, _ = plan(*refs)
        for cp in first + own:
            cp.start()

    def forward(*refs):
        _, _, landed, passed, _ = plan(*refs)
        for arrived, fw in zip(landed, passed):
            arrived.wait_recv()
            fw.start()

    def finish(*refs):
        first, own, _, passed, handed = plan(*refs)
        for cp in handed + own:
            cp.wait_recv()
        for cp in first + own + passed:
            cp.wait_send()

    return dict(ins=ws, out_shapes=[jax.ShapeDtypeStruct((4,) + w.shape, w.dtype) for w in ws], nsem=7 * n,
                start=start, forward=forward, finish=finish)


def _sibling_swap_job(gs):
    n = len(gs)

    def plan(g_refs, recv_refs, send_sems, recv_sems):
        x, y, c = _place()
        return [_rcopy(g_refs[t].at[s_, pl.ds((1 - c) * (gs[t].shape[1] // 2), gs[t].shape[1] // 2)], recv_refs[t].at[s_],
                       send_sems, recv_sems, 4 * t + s_, (x, y, 1 - c)) for t in range(n) for s_ in range(4)]

    def start(*refs):
        for cp in plan(*refs):
            cp.start()

    def finish(*refs):
        copies = plan(*refs)
        for cp in copies:
            cp.wait_recv()
        for cp in copies:
            cp.wait_send()

    return dict(ins=gs, out_shapes=[jax.ShapeDtypeStruct((4, g.shape[1] // 2, g.shape[2]), g.dtype) for g in gs],
                nsem=4 * n, start=start, finish=finish)


def _chip_exchange_job(ps):
    n = len(ps)

    def plan(p_refs, recv_refs, send_sems, recv_sems):
        x, y, c = _place()
        return [_rcopy(p_refs[t].at[2 * px + py], recv_refs[t].at[k], send_sems, recv_sems, 3 * t + k, (px, py, c))
                for t in range(n) for k, (px, py) in enumerate(_other_chips(x, y))]

    def start(*refs):
        for cp in plan(*refs):
            cp.start()

    def finish(*refs):
        copies = plan(*refs)
        for cp in copies:
            cp.wait_recv()
        for cp in copies:
            cp.wait_send()

    return dict(ins=ps, out_shapes=[jax.ShapeDtypeStruct((3,) + p.shape[1:], p.dtype) for p in ps], nsem=3 * n,
                start=start, finish=finish)


_JOIN_PIECES = 4


def _rs_sibling_join(qs):
    n = len(qs)
    npc = _JOIN_PIECES

    def body(*refs):
        q_refs, out_refs = refs[:n], refs[n:2 * n]
        send_sems, recv_sems = refs[2 * n:]
        x, y, c = _place()
        copies = []
        for t in range(n):
            rh = qs[t].shape[0] // 2
            pr = rh // npc
            for i in range(npc):
                rows = pl.ds(c * rh + i * pr, pr)
                cp = _rcopy(q_refs[t].at[rows], out_refs[t].at[rows], send_sems, recv_sems, npc * t + i,
                            (x, y, 1 - c))
                cp.start()
                copies.append(cp)
        for t in range(n):
            rh = qs[t].shape[0] // 2
            pr = rh // npc
            for i in range(npc):
                rows = pl.ds((1 - c) * rh + i * pr, pr)
                _rcopy(q_refs[t].at[rows], out_refs[t].at[rows], send_sems, recv_sems, npc * t + i,
                       (x, y, 1 - c)).wait_recv()
        for cp in copies:
            cp.wait_send()

    return pl.pallas_call(
        body, out_shape=[jax.ShapeDtypeStruct(q.shape, q.dtype) for q in qs],
        in_specs=[_ANY] * n, out_specs=[_ANY] * n, input_output_aliases={t: t for t in range(n)},
        scratch_shapes=[pltpu.SemaphoreType.DMA((npc * n,)), pltpu.SemaphoreType.DMA((npc * n,))],
        name="grads_sibling_join",
    )(*qs)


def _add_half(name, g, recv, chip, ci):
    S, r, w = g.shape
    rh = r // 2
    tt = _pick(rh, 256, 16)
    nb = rh // tt

    def body(chip_ref, core_ref, a_ref, b_ref, own_ref, ob_ref):
        v = a_ref[...] + b_ref[...]
        ob_ref[...] = v.astype(bf16)

        @pl.when(pl.program_id(1) == chip_ref[0])
        def _():
            own_ref[...] = v[0]

    grid_spec = pltpu.PrefetchScalarGridSpec(
        num_scalar_prefetch=2, grid=(nb, S),
        in_specs=[pl.BlockSpec((1, tt, w), lambda i, s_, ch, co: (s_, co[0] * nb + i, 0)),
                  pl.BlockSpec((1, tt, w), lambda i, s_, ch, co: (s_, i, 0))],
        out_specs=[pl.BlockSpec((tt, w), lambda i, s_, ch, co: (i, 0)),
                   pl.BlockSpec((1, tt, w), lambda i, s_, ch, co: (s_, i, 0))])
    return pl.pallas_call(body, grid_spec=grid_spec,
                          out_shape=[jax.ShapeDtypeStruct((rh, w), f32), jax.ShapeDtypeStruct((S, rh, w), bf16)],
                          compiler_params=pltpu.CompilerParams(dimension_semantics=("arbitrary", "arbitrary")),
                          name=name)(chip.reshape(1).astype(jnp.int32), ci.reshape(1).astype(jnp.int32), g, recv)


def _sum_chip(name, own, others, ci):
    rh, w = own.shape
    tt = _pick(rh, 256, 16)
    nb = rh // tt

    def body(core_ref, a_ref, b_ref, o_ref):
        o_ref[...] = ((a_ref[...] + b_ref[0].astype(f32)) + b_ref[1].astype(f32)) + b_ref[2].astype(f32)

    grid_spec = pltpu.PrefetchScalarGridSpec(
        num_scalar_prefetch=1, grid=(nb,),
        in_specs=[pl.BlockSpec((tt, w), lambda i, co: (i, 0)),
                  pl.BlockSpec((3, tt, w), lambda i, co: (0, i, 0))],
        out_specs=pl.BlockSpec((tt, w), lambda i, co: (co[0] * nb + i, 0)))
    return pl.pallas_call(body, grid_spec=grid_spec, out_shape=jax.ShapeDtypeStruct((2 * rh, w), f32),
                          name=name)(ci.reshape(1).astype(jnp.int32), own, others)


def _sum_devices(gathered, late, late_rows, head_row):
    _, rows, width = gathered.shape

    def body(g_ref, l_ref, out_ref, head_ref):
        acc, acc_l = g_ref[0], l_ref[0]
        for d in range(1, 8):
            acc = acc + g_ref[d]
            acc_l = acc_l + l_ref[d]
        out_ref[...] = acc
        for k, r in enumerate(late_rows):
            out_ref[r:r + 1, :] = acc_l[k:k + 1, :]
        row = acc[head_row:head_row + 1, :]
        hs = row[:, 0:HEAD_DIM]
        for h in range(1, HEADS):
            hs = hs + row[:, h * HEAD_DIM:(h + 1) * HEAD_DIM]
        head_ref[...] = jnp.zeros_like(head_ref)
        head_ref[0:1, 0:HEAD_DIM] = hs

    return pl.pallas_call(
        body, out_shape=[jax.ShapeDtypeStruct((rows, width), f32), jax.ShapeDtypeStruct((8, LANES), f32)],
        name="small_grads_sum",
    )(gathered, late)


def _pack(arrs, rows_mult, dtype):
    flat = jnp.concatenate([a.reshape(-1).astype(dtype) for a in arrs])
    per = PACK_W * rows_mult
    total = -(-flat.shape[0] // per) * per
    return jnp.pad(flat, (0, total - flat.shape[0])).reshape(total // PACK_W, PACK_W)


def _unpack(buf, shapes):
    flat = buf.reshape(-1)
    out, off = [], 0
    for s in shapes:
        n = int(np.prod(s))
        out.append(flat[off:off + n].reshape(s))
        off += n
    return out


_BIG = ["w_in", "w_branch_gdn", "w_branch_rwkv", "w_out", "w_ffn_in", "w_ffn_out"]
_MID = ["conv_gdn", "conv_ffn", "w2", "a2", "g2"]
_SMALL =["b_ada", "norm1_w", "a_log", "dt_bias", "onorm_gdn", "mu_rwkv", "w0", "a0", "k_k", "k_a", "r_k", "lnx_w",
          "lnx_b", "norm2_w", "norm_f_w"]
_ORDER = ["w_ada", "b_ada", "norm1_w", "w_in", "conv_gdn", "a_log", "dt_bias", "onorm_gdn", "w_branch_gdn", "mu_rwkv",
          "w0", "w2", "a0", "a2", "g2", "k_k", "k_a", "r_k", "lnx_w", "lnx_b", "w_branch_rwkv", "w_out", "norm2_w",
          "w_ffn_in", "conv_ffn", "w_ffn_out", "norm_f_w"]


_WIN_SEGMENTS = [(0, 1536, 0), (2064, 3600, 1536), (1536, 2048, 3072), (3600, 3728, 3584), (2048, 2064, 3712),
                 (3728, 3888, 3840), (3888, 5936, 4096)]
_WIN_PADDED = 6144
_COL_QKV, _COL_RKV, _COL_Z = (0, 1, 2), (3, 4, 5), 6
_COL_LORA, _COL_BA = 28, 29
_COL_GATE_LORA = 15
_COL_GL = (4, 5)
_JOINT_QKV, _JOINT_RKV, _JOINT_Z, _JOINT_GL = 0, 1, 6, 2
_SMALL_BLOCKS_AT = 3584


def _win_pad(shards):
    n = shards.shape[2]
    parts, at = [], 0
    for lo, hi, dst in _WIN_SEGMENTS:
        if dst > at:
            parts.append(jnp.zeros((shards.shape[1], dst - at), shards.dtype))
        c = lo
        while c < hi:
            j = c // n
            e = min(hi, (j + 1) * n)
            parts.append(shards[j][:, c - j * n:e - j * n])
            c = e
        at = dst + hi - lo
    if at < _WIN_PADDED:
        parts.append(jnp.zeros((shards.shape[1], _WIN_PADDED - at), shards.dtype))
    return jnp.concatenate(parts, axis=1)


def _win_unpad_shards(g, n):
    shards = []
    for j in range(4):
        parts = []
        for lo, hi, dst in sorted(_WIN_SEGMENTS):
            a, b = max(lo, j * n), min(hi, (j + 1) * n)
            if a < b:
                parts.append(g[:, dst + a - lo:dst + b - lo])
        shards.append(jnp.concatenate(parts, axis=1))
    return jnp.stack(shards)


def kernel(x, c, w_ada, b_ada, norm1_w, w_in, conv_gdn, a_log, dt_bias, onorm_gdn, w_branch_gdn, mu_rwkv, w0, w2, a0, a2, g2, k_k, k_a, r_k, lnx_w, lnx_b, w_branch_rwkv, w_out, norm2_w, w_ffn_in, conv_ffn, w_ffn_out, norm_f_w, loss_target, m_w_ada, m_b_ada, m_norm1_w, m_w_in, m_conv_gdn, m_a_log, m_dt_bias, m_onorm_gdn, m_w_branch_gdn, m_mu_rwkv, m_w0, m_w2, m_a0, m_a2, m_g2, m_k_k, m_k_a, m_r_k, m_lnx_w, m_lnx_b, m_w_branch_rwkv, m_w_out, m_norm2_w, m_w_ffn_in, m_conv_ffn, m_w_ffn_out, m_norm_f_w, v_w_ada, v_b_ada, v_norm1_w, v_w_in, v_conv_gdn, v_a_log, v_dt_bias, v_onorm_gdn, v_w_branch_gdn, v_mu_rwkv, v_w0, v_w2, v_a0, v_a2, v_g2, v_k_k, v_k_a, v_r_k, v_lnx_w, v_lnx_b, v_w_branch_rwkv, v_w_out, v_norm2_w, v_w_ffn_in, v_conv_ffn, v_w_ffn_out, v_norm_f_w):
    args = dict(locals())
    W = {n: args[n] for n in _ORDER}
    Mo = {n: args["m_" + n] for n in _ORDER}
    Vo = {n: args["v_" + n] for n in _ORDER}
    shapes = {n: W[n].shape for n in _ORDER}
    sq = lambda a: a.reshape(a.shape[-2:]) if a.ndim == 3 else a.reshape(1, -1)
    row = lambda a: a.reshape(1, -1)

    xi, yi, ci = lax.axis_index("x"), lax.axis_index("y"), lax.axis_index("c")
    dev = 4 * xi + 2 * yi + ci
    chip = 2 * xi + yi

    x2 = x[0]
    tgt = loss_target[0]
    T, D = x2.shape
    tt_l = _pick(T, 512, CHUNK)
    tt_p = _pick(T, 256, CHUNK)
    tt_h = _pick(T, 128, CHUNK)

    small_blk = _pack([c] + [W[n] for n in _MID], 8, f32)
    small_all, win_s = _run_job("gather_c_mid_and_w_in", [_ag8_job(small_blk), _ag4_relay_job([sq(W["w_in"]).astype(bf16)])])
    c_all = small_all[:, 0, :]
    per_chip = small_all[0::2].reshape(4, -1)[:, D:]
    mid_w, at = {}, 0
    for n in _MID:
        nr, nw = shapes[n][1:]
        mid_w[n] = per_chip[:, at:at + nr * nw].reshape(4, nr, nw).transpose(1, 0, 2).reshape(nr, 4 * nw)
        at += nr * nw
    conv_gdn_f, conv_ffn_f = mid_w["conv_gdn"], mid_w["conv_ffn"]
    w2f, a2f, g2f = mid_w["w2"], mid_w["a2"], mid_w["g2"]

    later_weights = _ag4_job([sq(W[n]).astype(bf16) for n in _BIG[1:]])
    win_p = _win_pad(win_s)
    zpad = lambda a, top, bot: jnp.pad(a, ((top, bot), (0, 0)))
    w2p, a2p, g2p = zpad(w2f, 0, 64), zpad(a2f, 64, 0), zpad(g2f, 0, 96)

    ncol = shapes["w_ada"][2]
    b_cols = lax.dynamic_slice(sq(W["b_ada"]), (0, chip * ncol), (1, ncol))
    cond16, mod_cols = _ada_fwd(jnp.pad(c_all, ((0, 8), (0, 0))), sq(W["w_ada"]), b_cols)
    mod_all = _ag8("gather_mod", mod_cols[:8])
    mod_mine = lax.dynamic_slice(mod_all[0::2], (0, dev, 0), (4, 1, ncol)).reshape(1, 4 * ncol)
    shift1, scale1, gate1, shift2, scale2, gate2 = [mod_mine[:, i * D:(i + 1) * D] for i in range(6)]

    seg = _seg_matrix(WIDTH, HEAD_DIM)
    norm1 = [sq(W["norm1_w"]), shift1, scale1]
    h1 = _stage_fwd("norm_mod1", _fn_norm_mod, [_whole(x2)], norm1, [(D, bf16)], tt_l)[0]
    p = _matmul("in_proj", h1, win_p, "nn", tm=2048)

    cgq = [row(conv_gdn_f[j, part * WIDTH:(part + 1) * WIDTH]) for part in range(3) for j in range(4)]
    lane_pad = lambda a: jnp.pad(row(a), ((0, 0), (8, LANES - 16)))
    gdn_pre_ps = cgq + [lane_pad(W["a_log"]), lane_pad(W["dt_bias"]), seg, _chunk_tri(tt_p, CHUNK)]
    gdn_pre_ts = [(p, WIDTH, ci_, (0, 1, 2, 3)) for ci_ in _COL_QKV] + [(p, LANES, _COL_BA, None)]
    q_, k_, v_, beta_t, gc_t = _stage_fwd("gdn_pre", _fn_gdn_pre, gdn_pre_ts, gdn_pre_ps,
                                          [(WIDTH, f32)] * 3 + [(LANES, f32)] * 2, tt_p)
    o_, gdn_hist, gdn_inv = _scan_fwd("gdn_scan", _gdn_chunk, [q_, k_, v_], [beta_t, gc_t])
    ow512 = jnp.tile(row(W["onorm_gdn"]), (1, HEADS))
    gdn_post_ts = [_whole(o_), (p, WIDTH, _COL_Z, None)]
    ya = _stage_fwd("gdn_post", _fn_gdn_post, gdn_post_ts, [ow512, seg], [(WIDTH, bf16)], tt_l)[0]

    mu = sq(W["mu_rwkv"])
    rw_ps = [mu[:, 0:512], mu[:, 512:1024], mu[:, 1024:1536], mu[:, 1536:1664], jnp.pad(mu[:, 1664:1824], ((0, 0), (0, 96))),
             sq(W["w0"]), w2p, sq(W["a0"]), a2p, g2p, sq(W["k_k"]), sq(W["k_a"]), seg]
    rw_ts = [(p, WIDTH, ci_, (0, 1)) for ci_ in _COL_RKV] + [(p, LANES, _COL_LORA, (0, 1)),
                                                              (p, 256, _COL_GATE_LORA, (0, 1))]
    rw_out = _stage_fwd("rwkv_pre", _fn_rwkv_pre, rw_ts, rw_ps, [(WIDTH, f32)] * 7, tt_p)
    r_, lw_, k2_, vv_, na_, b_, g_ = rw_out
    rw_ins = [r_, lw_, k2_, vv_, na_, b_]
    y_, rw_hist, rw_inv, wbg_s, wbr_s, wout_s, wfi_s, wfo_s = _scan_fwd("rwkv_scan", _rwkv_chunk, rw_ins, job=later_weights)
    wout_f = wout_s.reshape(D, D)
    wfo = wfo_s.reshape(D_FF, D)
    rwp_ps = [sq(W["lnx_w"]), sq(W["lnx_b"]), row(W["r_k"]), seg]
    rwp_ts = [_whole(y_), _whole(r_), _whole(k2_), _whole(vv_), _whole(g_)]
    yb = _stage_fwd("rwkv_post", _fn_rwkv_post, rwp_ts, rwp_ps, [(WIDTH, bf16)], tt_l)[0]

    big_a = _matmul("branch_gdn", ya, wbg_s, "nn", shards=4)
    big_b = _matmul("branch_rwkv", yb, wbr_s, "nn", shards=4)
    merge_ts = [(p, D, _COL_GL[0], None), (p, D, _COL_GL[1], None), _whole(big_a), _whole(big_b)]
    merged = _stage_fwd("merge", _fn_merge, merge_ts, [], [(D, bf16)], tt_l)[0]
    mo = _matmul("out_proj", merged, wout_f, "nn")
    norm2 = [gate1, sq(W["norm2_w"]), shift2, scale2]
    x1, h2 = _stage_fwd("resid_norm_mod2", _fn_resid_norm_mod, [_whole(x2), _whole(mo)], norm2, [(D, f32), (D, bf16)], tt_l)
    f = _matmul("ffn_in", h2, wfi_s, "nn", shards=4, tm=2048, tn=1408)
    cg_ps = [row(conv_ffn_f[j]) for j in range(3)]
    cg_ts = [(f, D_FF, 0, (0, 1, 2)), (f, D_FF, 1, None)]
    act = _stage_fwd("convglu", _fn_convglu, cg_ts, cg_ps, [(D_FF, bf16)], tt_h)[0]
    fo = _matmul("ffn_out", act, wfo, "nn", tk=D_FF)

    dx1_a, dfo, dgate2, dnormf, loss_part = _final_stage(x1, fo, tgt, gate2, row(W["norm_f_w"]), tt_l)

    dact = _matmul("d_act", dfo, wfo, "nt", tn=D_FF)
    g_wfo = _matmul("g_ffn_out", act, dfo, "tn", tm=1408, tk=2048)
    _, dcf, df = _stage_bwd("convglu_bwd", _fn_convglu, cg_ts, cg_ps, [[_whole(dact)]], tt_h, [True] * 2, [True] * 3,
                            joint=([0, 1], 2 * D_FF, 0, None), dtypes=[bf16])
    g_wfi = _matmul("g_ffn_in", h2, df, "tn", shards=4, tn=1408, tk=2048)
    g_wfo_s = g_wfo.reshape(4, D_FF // 4, D)
    dh2, recv_wfi, recv_wfo = _matmul("d_h2", df, wfi_s, "nt", shards=4, tm=2048, tk=1408,
                                      job=_sibling_swap_job([g_wfi, g_wfo_s]))
    (dx_a, dmo), (dgate1, dnorm2, dshift2, dscale2), _ = _stage_bwd(
        "resid_norm_mod2_bwd", _fn_resid_norm_mod, [_whole(x2), _whole(mo)], norm2,
        [[_whole(dx1_a)], [_whole(dh2)]], tt_l, [True, True], [True] * 4, dtypes=[f32, bf16])
    dmerged = _matmul("d_merged", dmo, wout_f, "nt")
    g_wout = _matmul("g_out_proj", merged, dmo, "tn", tk=T)
    (dbig_a, dbig_b), _, dp = _stage_bwd("merge_bwd", _fn_merge, merge_ts, [], [[_whole(dmerged)]], tt_l, [True] * 4, [],
                                         joint=([0, 1], p.shape[1], _JOINT_GL, None), dtypes=[bf16] * 3)
    g_wbg = _matmul("g_branch_gdn", ya, dbig_a, "tn", shards=4, tk=T)
    g_wbr = _matmul("g_branch_rwkv", yb, dbig_b, "tn", shards=4, tk=T)
    g_wout_s = g_wout.reshape(4, D // 4, D)
    dyb = _matmul("d_yb", dbig_b, wbr_s, "nt", shards=4)
    dya, *recv_mix = _matmul("d_ya", dbig_a, wbg_s, "nt", shards=4, job=_sibling_swap_job([g_wbg, g_wbr, g_wout_s]))

    (dy_, dr_p, dk2_p, dv_p, dg_p), (dlnxw, dlnxb, drk), _ = _stage_bwd(
        "rwkv_post_bwd", _fn_rwkv_post, rwp_ts, rwp_ps, [[_whole(dyb)]], tt_l, [True] * 5, [True, True, True, False])
    gs_a = [g_wbg, g_wbr, g_wout_s, g_wfi, g_wfo_s]
    pairs_a = [_add_half("grads_pair_sum%d" % (t + 1), g, r_, chip, ci)
               for t, (g, r_) in enumerate(zip(gs_a, recv_mix + [recv_wfi, recv_wfo]))]
    dr_c, dlw_c, dk2_c, dv_c, dna_c, db_c, *others_a = _scan_bwd(
        "rwkv_scan_bwd", _rwkv_chunk, rw_ins, [], rw_hist, rw_inv, dy_, job=_chip_exchange_job([pb for _, pb in pairs_a]))
    rw_cots = [[_whole(dr_p), _whole(dr_c)], [_whole(dlw_c)], [_whole(dk2_p), _whole(dk2_c)],
               [_whole(dv_p), _whole(dv_c)], [_whole(dna_c)], [_whole(db_c)], [_whole(dg_p)]]
    (dl_, dg_), rw_dp, dp = _stage_bwd("rwkv_pre_bwd", _fn_rwkv_pre, rw_ts, rw_ps, rw_cots, tt_p, [True] * 5,
                                       [True] * 12 + [False], joint=([0, 1, 2], p.shape[1], _JOINT_RKV, dp), dtypes=[bf16] * 3)
    dmu_r, dmu_k, dmu_v, dmu_l, dmu_g, dw0, dw2p, da0, da2p, dg2p, dkk, dka = rw_dp

    (do_,), (dow512,), dp = _stage_bwd("gdn_post_bwd", _fn_gdn_post, gdn_post_ts, [ow512, seg], [[_whole(dya)]], tt_l,
                                       [True, True], [True, False], joint=([1], p.shape[1], _JOINT_Z, dp), dtypes=[f32, bf16])
    d_gdn = _scan_bwd("gdn_scan_bwd", _gdn_chunk, [q_, k_, v_], [beta_t, gc_t], gdn_hist, gdn_inv, do_)
    gdn_cots = [[_whole(a)] for a in d_gdn]
    (dba,), gdn_dp, dp = _stage_bwd("gdn_pre_bwd", _fn_gdn_pre, gdn_pre_ts, gdn_pre_ps, gdn_cots, tt_p, [True] * 4,
                                    [True] * 14 + [False, False], joint=([0, 1, 2], p.shape[1], _JOINT_QKV, dp), dtypes=[bf16] * 2)
    dp = lax.dynamic_update_slice(dp, jnp.concatenate([dl_, dba, dg_], axis=1), (0, _SMALL_BLOCKS_AT))
    g_conv_gdn = jnp.concatenate([jnp.concatenate([gdn_dp[4 * part + j] for part in range(3)], axis=1) for j in range(4)], axis=0)
    g_conv_ffn = jnp.concatenate(dcf, axis=0)
    g_mu = jnp.concatenate([dmu_r, dmu_k, dmu_v, dmu_l, dmu_g[:, :160]], axis=1)
    late_zero = jnp.zeros((1, D), f32)
    dmod_early = jnp.concatenate([late_zero, late_zero, dgate1, dshift2, dscale2, dgate2], axis=1)
    small_parts = {"b_ada": dmod_early, "norm1_w": late_zero, "a_log": gdn_dp[12][:, 8:16], "dt_bias": gdn_dp[13][:, 8:16],
                   "mu_rwkv": g_mu, "w0": dw0, "a0": da0, "k_k": dkk, "k_a": dka, "r_k": drk, "lnx_w": dlnxw,
                   "lnx_b": dlnxb, "norm2_w": dnorm2, "norm_f_w": dnormf}
    small_names = [n for n in _SMALL if n != "onorm_gdn"]
    mid_full = [g_conv_gdn, g_conv_ffn, dw2p[0:64], da2p[64:128], dg2p[0:160]]
    body_rows = _pack([small_parts[n] for n in small_names] + [loss_part[:, 0:1]] + mid_full, 1, f32)
    head_row = body_rows.shape[0]
    small_g = jnp.concatenate([body_rows, jnp.pad(dow512, ((0, 0), (0, PACK_W - WIDTH)))], axis=0)
    small_g = jnp.pad(small_g, ((0, -small_g.shape[0] % 8), (0, 0)))

    g_win_pad, small_all_g = _matmul("g_in_proj", h1, dp, "tn", tk=T, job=_ag8_job(small_g))
    g_win_s = _win_unpad_shards(g_win_pad, shapes["w_in"][2])
    pair_win = _add_half("grads_pair_sum0", g_win_s, _run_job("w_in_grads_sibling_swap", _sibling_swap_job([g_win_s]))[0],
                         chip, ci)
    dh1, others_win = _matmul("d_h1", dp, win_p, "nt", tk=3072, job=_chip_exchange_job([pair_win[1]]))
    (grad_x,), (dnorm1, dshift1, dscale1), _ = _stage_bwd("norm_mod1_bwd", _fn_norm_mod_and_x, [_whole(x2)], norm1,
                                                          [[_whole(dh1)], [_whole(dx_a)]], tt_l, [True], [True] * 3)
    late_all = _ag8("gather_late_grads", jnp.pad(jnp.concatenate([dshift1, dscale1, dnorm1], axis=0), ((0, 5), (0, 0))))
    small_sum, head_sum = _sum_devices(small_all_g, late_all, (0, 1, 6), head_row)
    small_shapes = [shapes[n][1:] if n != "norm_f_w" else shapes[n] for n in small_names]
    un = _unpack(small_sum, small_shapes + [(1,)] + [g.shape for g in mid_full])
    small_grads = dict(zip(small_names, un))
    loss = un[len(small_names)].reshape(())
    small_grads["onorm_gdn"] = head_sum[0, 0:HEAD_DIM]
    for n, g in zip(_MID, un[len(small_names) + 1:]):
        wcols = shapes[n][2]
        small_grads[n] = lax.dynamic_slice(g, (0, chip * wcols), (g.shape[0], wcols))

    dmod_all = jnp.concatenate([late_all[:, 0, :], late_all[:, 1, :], small_all_g[:, 2:6, :].reshape(8, 4 * PACK_W)], axis=1)
    dmod_cols = lax.dynamic_slice(dmod_all, (0, chip * ncol), (8, ncol))
    g_wada = _matmul("g_w_ada", cond16, jnp.pad(dmod_cols, ((0, 8), (0, 0))), "tn")

    pairs = [pair_win] + pairs_a
    others = [others_win] + others_a
    halves = [_sum_chip("grads_chip_sum%d" % t, own, o_, ci) for t, ((own, _), o_) in enumerate(zip(pairs, others))]
    big_grads = dict(zip(_BIG, _rs_sibling_join(halves)))

    res = {tag: {} for tag in ("grad", "delta", "new_m", "new_v")}

    def put(n, g, d, m_, v_):
        for tag, val in zip(("grad", "delta", "new_m", "new_v"), (g, d, m_, v_)):
            res[tag][n] = val.reshape(shapes[n])

    for n in _BIG:
        put(n, big_grads[n], *_adamw("adamw_" + n, sq(W[n]), big_grads[n], sq(Mo[n]), sq(Vo[n])))
    put("w_ada", g_wada, *_adamw("adamw_w_ada", sq(W["w_ada"]), g_wada, sq(Mo["w_ada"]), sq(Vo["w_ada"])))
    rest = _SMALL + _MID
    pk = lambda d: _pack([d[n] for n in rest], 8, f32)
    sg = pk(small_grads)
    sm = _adamw("adamw_small", pk(W), sg, pk(Mo), pk(Vo))
    for tag, buf in zip(("grad", "delta", "new_m", "new_v"), (sg,) + tuple(sm)):
        res[tag].update(zip(rest, _unpack(buf, [shapes[n] for n in rest])))
    outs = [loss, grad_x.reshape(x.shape)]
    for tag in ("grad", "delta", "new_m", "new_v"):
        outs += [res[tag][n] for n in _ORDER]
    return tuple(outs)
```
